```python
import jax, jax.numpy as jnp
from jax import lax
import numpy as np

D_MODEL = 1024
BATCH = 16
SEQ = 2048
DEPTH = 1

D_MIX = 2 * D_MODEL
A_HEADS = 16
A_HEAD_DIM = 64
A_WIDTH = A_HEADS * A_HEAD_DIM
A_ROT_DIM = A_HEAD_DIM // 4
DILATED_PATTERNS = ((128, 1), (512, 4), (2048, 16))
MLA_HEADS = 8
MLA_Q_RANK = 256
MLA_KV_RANK = 128
MLA_NOPE_DIM = 64
MLA_ROPE_DIM = 32
MLA_V_DIM = 64
MLA_WIDTH = MLA_HEADS * MLA_V_DIM
N_MEM = 256
MEM_HEADS = 4
MEM_HEAD_DIM = 128
MEM_WIDTH = MEM_HEADS * MEM_HEAD_DIM

ROPE_THETA = 500000.0
Q_BLOCK = 128
NORM_EPS = 1e-5
NEG_INF = -1e30
DEEPNORM_ALPHA = (2 * DEPTH) ** 0.25
DEEPNORM_BETA = (8 * DEPTH) ** -0.25

IN_SPLITS = (A_WIDTH, A_WIDTH, A_WIDTH, A_WIDTH,
             MLA_Q_RANK, MLA_KV_RANK, MLA_ROPE_DIM, MLA_WIDTH,
             MEM_WIDTH, MEM_WIDTH)
D_IN = sum(IN_SPLITS)

kernel_name = "hymba_dilated_mla_memory_deepnorm"


def _layer_norm(x, g, b):
    xf = x.astype(jnp.float32)
    mu = jnp.mean(xf, axis=-1, keepdims=True)
    var = jnp.mean(jnp.square(xf - mu), axis=-1, keepdims=True)
    return ((xf - mu) * lax.rsqrt(var + NORM_EPS) * g.astype(jnp.float32) + b.astype(jnp.float32)).astype(x.dtype)


def _rms_norm(x, g, out_dtype):
    xf = x.astype(jnp.float32)
    ms = jnp.mean(jnp.square(xf), axis=-1, keepdims=True)
    return (xf * lax.rsqrt(ms + NORM_EPS) * g.astype(jnp.float32)).astype(out_dtype)


def _rope(x, pos):
    r = x.shape[-1]
    inv_freq = ROPE_THETA ** (-(jnp.arange(0, r, 2, dtype=jnp.float32) / r))
    ang = pos.astype(jnp.float32)[..., None] * inv_freq
    cos, sin = jnp.cos(ang)[:, :, None, :], jnp.sin(ang)[:, :, None, :]
    xf = x.astype(jnp.float32)
    x1, x2 = xf[..., : r // 2], xf[..., r // 2:]
    return jnp.concatenate([x1 * cos - x2 * sin, x2 * cos + x1 * sin], axis=-1).astype(x.dtype)


def _partial_rope(x, pos):
    return jnp.concatenate([_rope(x[..., :A_ROT_DIM], pos), x[..., A_ROT_DIM:]], axis=-1)


def _window_attn(q, k, v, n_side):
    n, length, h, e = q.shape
    blk = n_side
    nb = -(-length // blk)
    pad = nb * blk - length
    qb = jnp.pad(q, ((0, 0), (0, pad), (0, 0), (0, 0))).reshape(n, nb, blk, h, e).astype(jnp.float32)

    def bands(t):
        tb = jnp.pad(t, ((0, 0), (blk, pad + blk), (0, 0), (0, 0))).reshape(n, nb + 2, blk, h, t.shape[-1])
        return jnp.concatenate([tb[:, :-2], tb[:, 1:-1], tb[:, 2:]], axis=2).astype(jnp.float32)

    kb, vb = bands(k), bands(v)
    qpos = jnp.arange(nb)[:, None] * blk + jnp.arange(blk)[None, :]
    kpos = (jnp.arange(nb)[:, None] - 1) * blk + jnp.arange(3 * blk)[None, :]
    off = kpos[:, None, :] - qpos[:, :, None]
    valid = (jnp.abs(off) <= n_side) & (kpos[:, None, :] >= 0) & (kpos[:, None, :] < length)
    s = jnp.einsum('nbqhe,nbkhe->nbhqk', qb, kb) * (e ** -0.5)
    s = jnp.where(valid[None, :, None], s, NEG_INF)
    m = jnp.max(s, axis=-1, keepdims=True)
    p = jnp.exp(s - m)
    den = jnp.sum(p, axis=-1, keepdims=True)
    o = jnp.einsum('nbhqk,nbkhe->nbqhe', p / den, vb).reshape(n, nb * blk, h, vb.shape[-1])[:, :length]
    lse = (m + jnp.log(den))[..., 0]
    lse = lse.transpose(0, 1, 3, 2).reshape(n, nb * blk, h)[:, :length]
    return o, lse


def _dilated_attention(q, k, v):
    b, s, h, e = q.shape
    outs, lses = [], []
    for window, dil in DILATED_PATTERNS:
        n_side = window // (2 * dil)
        length = s // dil

        def to_sub(t):
            return t.reshape(b, length, dil, h, t.shape[-1]).transpose(0, 2, 1, 3, 4).reshape(b * dil, length, h, t.shape[-1])

        o, lse = _window_attn(to_sub(q), to_sub(k), to_sub(v), n_side)
        outs.append(o.reshape(b, dil, length, h, e).transpose(0, 2, 1, 3, 4).reshape(b, s, h, e))
        lses.append(lse.reshape(b, dil, length, h).transpose(0, 2, 1, 3).reshape(b, s, h))
    w = jax.nn.softmax(jnp.stack(lses, axis=0), axis=0)
    return jnp.einsum('gbsh,gbshe->bshe', w, jnp.stack(outs, axis=0))


def _mla_attention(q_nope, q_rope, k_nope, k_rope, v):
    b, s, h, _ = q_nope.shape
    scale = (MLA_NOPE_DIM + MLA_ROPE_DIM) ** -0.5
    nq = s // Q_BLOCK
    kn, kr, vf = k_nope.astype(jnp.float32), k_rope.astype(jnp.float32), v.astype(jnp.float32)

    def blocks(t):
        return t.reshape((b, nq, Q_BLOCK) + t.shape[2:]).swapaxes(0, 1)

    def one_block(args):
        qn, qr = args
        sc = (jnp.einsum('bqhe,bkhe->bhqk', qn.astype(jnp.float32), kn)
              + jnp.einsum('bqhr,bkr->bhqk', qr.astype(jnp.float32), kr)) * scale
        p = jax.nn.softmax(sc, axis=-1)
        return jnp.einsum('bhqk,bkhe->bqhe', p, vf)

    o = lax.map(one_block, (blocks(q_nope), blocks(q_rope)))
    return o.swapaxes(0, 1).reshape(b, s, h, v.shape[-1])


def _memory_attention(q, k, v):
    sc = jnp.einsum('bshe,bmhe->bhsm', q.astype(jnp.float32), k.astype(jnp.float32)) * (q.shape[-1] ** -0.5)
    p = jax.nn.softmax(sc, axis=-1)
    return jnp.einsum('bhsm,bmhe->bshe', p, v.astype(jnp.float32))


def _hybrid_layer(h, pos, mem, w_in, g_cq, g_ckv, w_uq, w_ukv, w_mem_kv,
                  g_out_a, g_out_b, g_out_m, w_out, g_post, b_post):
    b, s, _ = h.shape
    dt = h.dtype
    idx = [int(i) for i in np.cumsum(IN_SPLITS)[:-1]]
    proj = h @ w_in
    a_q, a_k, a_v, a_g, c_q, c_kv, b_kr, b_g, m_q, m_g = jnp.split(proj, idx, axis=-1)

    hd = (b, s, A_HEADS, A_HEAD_DIM)
    y_a = _dilated_attention(_partial_rope(a_q.reshape(hd), pos),
                             _partial_rope(a_k.reshape(hd), pos),
                             a_v.reshape(hd)).reshape(b, s, A_WIDTH)

    q = (_rms_norm(c_q, g_cq, dt) @ w_uq).reshape(b, s, MLA_HEADS, MLA_NOPE_DIM + MLA_ROPE_DIM)
    q_nope, q_rope = q[..., :MLA_NOPE_DIM], _rope(q[..., MLA_NOPE_DIM:], pos)
    kv = (_rms_norm(c_kv, g_ckv, dt) @ w_ukv).reshape(b, s, MLA_HEADS, MLA_NOPE_DIM + MLA_V_DIM)
    k_nope, v = kv[..., :MLA_NOPE_DIM], kv[..., MLA_NOPE_DIM:]
    k_rope = _rope(b_kr[:, :, None, :], pos)[:, :, 0]
    y_b = _mla_attention(q_nope, q_rope, k_nope, k_rope, v).reshape(b, s, MLA_WIDTH)

    mkv = mem @ w_mem_kv
    mk = mkv[..., :MEM_WIDTH].reshape(b, -1, MEM_HEADS, MEM_HEAD_DIM)
    mv = mkv[..., MEM_WIDTH:].reshape(b, -1, MEM_HEADS, MEM_HEAD_DIM)
    y_m = _memory_attention(m_q.reshape(b, s, MEM_HEADS, MEM_HEAD_DIM), mk, mv).reshape(b, s, MEM_WIDTH)

    y = jnp.concatenate([_rms_norm(y_a, g_out_a, dt) * jax.nn.silu(a_g),
                         _rms_norm(y_b, g_out_b, dt) * jax.nn.silu(b_g),
                         _rms_norm(y_m, g_out_m, dt) * jax.nn.silu(m_g)], axis=-1)
    sub = y @ w_out
    return _layer_norm(DEEPNORM_ALPHA * h + sub, g_post, b_post)


def _fwd_setup_inputs(seed: int = 0) -> dict:
    key = jax.random.key(seed)
    ks = jax.random.split(key, 20)
    f32 = jnp.float32

    def nrm(k, shape, fan_in, scale=1.0):
        return jax.random.normal(k, shape, f32) * (fan_in ** -0.5) * scale

    def gain(k, shape):
        return 1.0 + 0.02 * jax.random.normal(k, shape, f32)

    x = jax.random.normal(ks[0], (BATCH, SEQ, D_MODEL), f32)
    mem = jax.random.normal(ks[1], (BATCH, N_MEM, D_MODEL), f32)
    offsets = jax.random.randint(ks[2], (BATCH, 1), 0, 4096, dtype=jnp.int32)
    positions = offsets + jnp.arange(SEQ, dtype=jnp.int32)[None, :]
    return {
        "x": x,
        "mem": mem,
        "positions": positions,
        "g_emb": gain(ks[3], (D_MODEL,)),
        "b_emb": 0.02 * jax.random.normal(ks[4], (D_MODEL,), f32),
        "w_in": nrm(ks[5], (DEPTH, D_MODEL, D_IN), D_MODEL),
        "g_cq": gain(ks[6], (DEPTH, MLA_Q_RANK)),
        "g_ckv": gain(ks[7], (DEPTH, MLA_KV_RANK)),
        "w_uq": nrm(ks[8], (DEPTH, MLA_Q_RANK, MLA_HEADS * (MLA_NOPE_DIM + MLA_ROPE_DIM)), MLA_Q_RANK),
        "w_ukv": nrm(ks[9], (DEPTH, MLA_KV_RANK, MLA_HEADS * (MLA_NOPE_DIM + MLA_V_DIM)), MLA_KV_RANK),
        "w_mem_kv": nrm(ks[10], (DEPTH, D_MODEL, 2 * MEM_WIDTH), D_MODEL),
        "g_out_a": gain(ks[11], (DEPTH, A_WIDTH)),
        "g_out_b": gain(ks[12], (DEPTH, MLA_WIDTH)),
        "g_out_m": gain(ks[13], (DEPTH, MEM_WIDTH)),
        "w_out": nrm(ks[14], (DEPTH, D_MIX, D_MODEL), D_MIX, DEEPNORM_BETA),
        "g_post": gain(ks[15], (DEPTH, D_MODEL)),
        "b_post": 0.02 * jax.random.normal(ks[16], (DEPTH, D_MODEL), f32),
    }


def _fwd_reference(x, mem, positions, g_emb, b_emb, w_in, g_cq, g_ckv, w_uq, w_ukv, w_mem_kv,
              g_out_a, g_out_b, g_out_m, w_out, g_post, b_post):
    h = _layer_norm(x, g_emb, b_emb)
    for l in range(DEPTH):
        h = _hybrid_layer(h, positions, mem, w_in[l], g_cq[l], g_ckv[l], w_uq[l], w_ukv[l], w_mem_kv[l],
                          g_out_a[l], g_out_b[l], g_out_m[l], w_out[l], g_post[l], b_post[l])
    return h


import jax as _jax
import jax.numpy as _jnp

TWIN_FORMAT = 'train_step'
FWD_PARAMS = ['x', 'mem', 'positions', 'g_emb', 'b_emb', 'w_in', 'g_cq', 'g_ckv', 'w_uq', 'w_ukv', 'w_mem_kv', 'g_out_a', 'g_out_b', 'g_out_m', 'w_out', 'g_post', 'b_post']
TWIN_WEIGHTS = ['g_emb', 'b_emb', 'w_in', 'g_cq', 'g_ckv', 'w_uq', 'w_ukv', 'w_mem_kv', 'g_out_a', 'g_out_b', 'g_out_m', 'w_out', 'g_post', 'b_post']
TWIN_DIFF_INPUT = 'x'
TWIN_INPUTS = ['x', 'mem', 'positions', 'g_emb', 'b_emb', 'w_in', 'g_cq', 'g_ckv', 'w_uq', 'w_ukv', 'w_mem_kv', 'g_out_a', 'g_out_b', 'g_out_m', 'w_out', 'g_post', 'b_post', 'loss_target', 'm_g_emb', 'm_b_emb', 'm_w_in', 'm_g_cq', 'm_g_ckv', 'm_w_uq', 'm_w_ukv', 'm_w_mem_kv', 'm_g_out_a', 'm_g_out_b', 'm_g_out_m', 'm_w_out', 'm_g_post', 'm_b_post', 'v_g_emb', 'v_b_emb', 'v_w_in', 'v_g_cq', 'v_g_ckv', 'v_w_uq', 'v_w_ukv', 'v_w_mem_kv', 'v_g_out_a', 'v_g_out_b', 'v_g_out_m', 'v_w_out', 'v_g_post', 'v_b_post']
TWIN_OUTPUTS = ['loss', 'grad_x', 'grad_g_emb', 'grad_b_emb', 'grad_w_in', 'grad_g_cq', 'grad_g_ckv', 'grad_w_uq', 'grad_w_ukv', 'grad_w_mem_kv', 'grad_g_out_a', 'grad_g_out_b', 'grad_g_out_m', 'grad_w_out', 'grad_g_post', 'grad_b_post', 'delta_g_emb', 'delta_b_emb', 'delta_w_in', 'delta_g_cq', 'delta_g_ckv', 'delta_w_uq', 'delta_w_ukv', 'delta_w_mem_kv', 'delta_g_out_a', 'delta_g_out_b', 'delta_g_out_m', 'delta_w_out', 'delta_g_post', 'delta_b_post', 'new_m_g_emb', 'new_m_b_emb', 'new_m_w_in', 'new_m_g_cq', 'new_m_g_ckv', 'new_m_w_uq', 'new_m_w_ukv', 'new_m_w_mem_kv', 'new_m_g_out_a', 'new_m_g_out_b', 'new_m_g_out_m', 'new_m_w_out', 'new_m_g_post', 'new_m_b_post', 'new_v_g_emb', 'new_v_b_emb', 'new_v_w_in', 'new_v_g_cq', 'new_v_g_ckv', 'new_v_w_uq', 'new_v_w_ukv', 'new_v_w_mem_kv', 'new_v_g_out_a', 'new_v_g_out_b', 'new_v_g_out_m', 'new_v_w_out', 'new_v_g_post', 'new_v_b_post']
TWIN_LEAF_KINDS = {'loss': 'loss', 'grad_x': 'grad_x', 'grad_g_emb': 'grad_w', 'grad_b_emb': 'grad_w', 'grad_w_in': 'grad_w', 'grad_g_cq': 'grad_w', 'grad_g_ckv': 'grad_w', 'grad_w_uq': 'grad_w', 'grad_w_ukv': 'grad_w', 'grad_w_mem_kv': 'grad_w', 'grad_g_out_a': 'grad_w', 'grad_g_out_b': 'grad_w', 'grad_g_out_m': 'grad_w', 'grad_w_out': 'grad_w', 'grad_g_post': 'grad_w', 'grad_b_post': 'grad_w', 'delta_g_emb': 'delta_w', 'delta_b_emb': 'delta_w', 'delta_w_in': 'delta_w', 'delta_g_cq': 'delta_w', 'delta_g_ckv': 'delta_w', 'delta_w_uq': 'delta_w', 'delta_w_ukv': 'delta_w', 'delta_w_mem_kv': 'delta_w', 'delta_g_out_a': 'delta_w', 'delta_g_out_b': 'delta_w', 'delta_g_out_m': 'delta_w', 'delta_w_out': 'delta_w', 'delta_g_post': 'delta_w', 'delta_b_post': 'delta_w', 'new_m_g_emb': 'new_m', 'new_m_b_emb': 'new_m', 'new_m_w_in': 'new_m', 'new_m_g_cq': 'new_m', 'new_m_g_ckv': 'new_m', 'new_m_w_uq': 'new_m', 'new_m_w_ukv': 'new_m', 'new_m_w_mem_kv': 'new_m', 'new_m_g_out_a': 'new_m', 'new_m_g_out_b': 'new_m', 'new_m_g_out_m': 'new_m', 'new_m_w_out': 'new_m', 'new_m_g_post': 'new_m', 'new_m_b_post': 'new_m', 'new_v_g_emb': 'new_v', 'new_v_b_emb': 'new_v', 'new_v_w_in': 'new_v', 'new_v_g_cq': 'new_v', 'new_v_g_ckv': 'new_v', 'new_v_w_uq': 'new_v', 'new_v_w_ukv': 'new_v', 'new_v_w_mem_kv': 'new_v', 'new_v_g_out_a': 'new_v', 'new_v_g_out_b': 'new_v', 'new_v_g_out_m': 'new_v', 'new_v_w_out': 'new_v', 'new_v_g_post': 'new_v', 'new_v_b_post': 'new_v'}


def _forward(args):
    return _fwd_reference(*[args[k] for k in FWD_PARAMS])


def _output_shape():
    out = _jax.eval_shape(lambda: _forward(_fwd_setup_inputs(0)))
    return out.shape, out.dtype

N_MICROBATCH = 1
ADAM_LR = 0.001
ADAM_B1 = 0.9
ADAM_B2 = 0.999
ADAM_EPS = 1e-08
ADAM_WD = 0.01
ADAM_STEP = 10
PER_EXAMPLE_BATCH_AXIS = {'x': 0, 'mem': 0, 'positions': 0, 'loss_target': 0}
SHARED_INPUTS = []
_WEIGHT_DTYPES = {'g_emb': _jnp.float32, 'b_emb': _jnp.float32, 'w_in': _jnp.float32, 'g_cq': _jnp.float32, 'g_ckv': _jnp.float32, 'w_uq': _jnp.float32, 'w_ukv': _jnp.float32, 'w_mem_kv': _jnp.float32, 'g_out_a': _jnp.float32, 'g_out_b': _jnp.float32, 'g_out_m': _jnp.float32, 'w_out': _jnp.float32, 'g_post': _jnp.float32, 'b_post': _jnp.float32}
MOMENT_SCALE = {'g_emb': 9.528774e-01, 'b_emb': 8.660039e-01, 'w_in': 3.772511e-02, 'g_cq': 4.796140e-02, 'g_ckv': 1.485723e-01, 'w_uq': 2.851589e-02, 'w_ukv': 3.545336e-02, 'w_mem_kv': 3.602894e-02, 'g_out_a': 3.683415e-02, 'g_out_b': 4.022919e-02, 'g_out_m': 3.601730e-02, 'w_out': 8.622117e-02, 'g_post': 3.200162e+01, 'b_post': 8.726190e-01}


def _to_microbatches(a, axis):
    t = _jnp.moveaxis(a, axis, 0)
    t = t.reshape((N_MICROBATCH, t.shape[0] // N_MICROBATCH) + t.shape[1:])
    return _jnp.moveaxis(t, 1, axis + 1)


def setup_inputs(seed: int = 0) -> dict:
    inp = _fwd_setup_inputs(seed)
    key = _jax.random.fold_in(_jax.random.key(seed), 7919)
    shape, _ = _output_shape()
    out = dict(inp)
    out["loss_target"] = _jax.random.normal(_jax.random.fold_in(key, 0), shape, _jnp.float32)
    for i, name in enumerate(TWIN_WEIGHTS):
        w = inp[name].astype(_jnp.float32)
        if MOMENT_SCALE is None:
            s = _jnp.sqrt(_jnp.mean(_jnp.square(w)) + 1e-30)
        else:
            s = MOMENT_SCALE[name]
        km, kv = _jax.random.split(_jax.random.fold_in(key, i + 1))
        out[name] = w
        out["m_" + name] = s * _jax.random.normal(km, w.shape, _jnp.float32)
        out["v_" + name] = (s * s) * _jax.random.uniform(kv, w.shape, _jnp.float32, 0.5, 1.5)
    if N_MICROBATCH > 1:
        for name, axis in PER_EXAMPLE_BATCH_AXIS.items():
            out[name] = _to_microbatches(out[name], axis)
    return {'x': out['x'], 'mem': out['mem'], 'positions': out['positions'], 'g_emb': out['g_emb'], 'b_emb': out['b_emb'], 'w_in': out['w_in'], 'g_cq': out['g_cq'], 'g_ckv': out['g_ckv'], 'w_uq': out['w_uq'], 'w_ukv': out['w_ukv'], 'w_mem_kv': out['w_mem_kv'], 'g_out_a': out['g_out_a'], 'g_out_b': out['g_out_b'], 'g_out_m': out['g_out_m'], 'w_out': out['w_out'], 'g_post': out['g_post'], 'b_post': out['b_post'], 'loss_target': out['loss_target'], 'm_g_emb': out['m_g_emb'], 'm_b_emb': out['m_b_emb'], 'm_w_in': out['m_w_in'], 'm_g_cq': out['m_g_cq'], 'm_g_ckv': out['m_g_ckv'], 'm_w_uq': out['m_w_uq'], 'm_w_ukv': out['m_w_ukv'], 'm_w_mem_kv': out['m_w_mem_kv'], 'm_g_out_a': out['m_g_out_a'], 'm_g_out_b': out['m_g_out_b'], 'm_g_out_m': out['m_g_out_m'], 'm_w_out': out['m_w_out'], 'm_g_post': out['m_g_post'], 'm_b_post': out['m_b_post'], 'v_g_emb': out['v_g_emb'], 'v_b_emb': out['v_b_emb'], 'v_w_in': out['v_w_in'], 'v_g_cq': out['v_g_cq'], 'v_g_ckv': out['v_g_ckv'], 'v_w_uq': out['v_w_uq'], 'v_w_ukv': out['v_w_ukv'], 'v_w_mem_kv': out['v_w_mem_kv'], 'v_g_out_a': out['v_g_out_a'], 'v_g_out_b': out['v_g_out_b'], 'v_g_out_m': out['v_g_out_m'], 'v_w_out': out['v_w_out'], 'v_g_post': out['v_g_post'], 'v_b_post': out['v_b_post']}


def _loss(weights, diff, rest, loss_target):
    with _jax.named_scope("forward"):
        args = {**rest, TWIN_DIFF_INPUT: diff, **{k: w.astype(_WEIGHT_DTYPES[k]) for k, w in weights.items()}}
        y = _forward(args)
    with _jax.named_scope("loss_head"):
        err = _jnp.square(y.astype(_jnp.float32) - loss_target)
        return 0.5 * _jnp.sum(_jnp.mean(err, axis=-1)) if err.ndim else 0.5 * err


def _adamw(w, g, m, v):
    m = ADAM_B1 * m + (1.0 - ADAM_B1) * g
    v = ADAM_B2 * v + (1.0 - ADAM_B2) * _jnp.square(g)
    m_hat = m / (1.0 - ADAM_B1 ** ADAM_STEP)
    v_hat = v / (1.0 - ADAM_B2 ** ADAM_STEP)
    delta = -ADAM_LR * (m_hat / (_jnp.sqrt(v_hat) + ADAM_EPS) + ADAM_WD * w)
    return delta, m, v


def reference(x, mem, positions, g_emb, b_emb, w_in, g_cq, g_ckv, w_uq, w_ukv, w_mem_kv, g_out_a, g_out_b, g_out_m, w_out, g_post, b_post, loss_target, m_g_emb, m_b_emb, m_w_in, m_g_cq, m_g_ckv, m_w_uq, m_w_ukv, m_w_mem_kv, m_g_out_a, m_g_out_b, m_g_out_m, m_w_out, m_g_post, m_b_post, v_g_emb, v_b_emb, v_w_in, v_g_cq, v_g_ckv, v_w_uq, v_w_ukv, v_w_mem_kv, v_g_out_a, v_g_out_b, v_g_out_m, v_w_out, v_g_post, v_b_post):
    given = dict(x=x, mem=mem, positions=positions, g_emb=g_emb, b_emb=b_emb, w_in=w_in, g_cq=g_cq, g_ckv=g_ckv, w_uq=w_uq, w_ukv=w_ukv, w_mem_kv=w_mem_kv, g_out_a=g_out_a, g_out_b=g_out_b, g_out_m=g_out_m, w_out=w_out, g_post=g_post, b_post=b_post, loss_target=loss_target, m_g_emb=m_g_emb, m_b_emb=m_b_emb, m_w_in=m_w_in, m_g_cq=m_g_cq, m_g_ckv=m_g_ckv, m_w_uq=m_w_uq, m_w_ukv=m_w_ukv, m_w_mem_kv=m_w_mem_kv, m_g_out_a=m_g_out_a, m_g_out_b=m_g_out_b, m_g_out_m=m_g_out_m, m_w_out=m_w_out, m_g_post=m_g_post, m_b_post=m_b_post, v_g_emb=v_g_emb, v_b_emb=v_b_emb, v_w_in=v_w_in, v_g_cq=v_g_cq, v_g_ckv=v_g_ckv, v_w_uq=v_w_uq, v_w_ukv=v_w_ukv, v_w_mem_kv=v_w_mem_kv, v_g_out_a=v_g_out_a, v_g_out_b=v_g_out_b, v_g_out_m=v_g_out_m, v_w_out=v_w_out, v_g_post=v_g_post, v_b_post=v_b_post)
    weights = {n: given[n] for n in TWIN_WEIGHTS}
    shared = {n: given[n] for n in SHARED_INPUTS}
    per_example = {n: given[n] for n in ['x', 'mem', 'positions']}
    grad_fn = _jax.value_and_grad(_loss, argnums=(0, 1))

    def one_microbatch(ex, loss_target):
        ex = dict(ex)
        diff = ex.pop(TWIN_DIFF_INPUT)
        return grad_fn(weights, diff, {**shared, **ex}, loss_target)

    if N_MICROBATCH == 1:
        loss, (grad_w, grad_x) = one_microbatch(per_example, given["loss_target"])
    else:
        def body(carry, xs):
            loss_sum, grad_sum = carry
            l_k, (gw_k, gx_k) = one_microbatch(xs[0], xs[1])
            with _jax.named_scope("update"):
                return (loss_sum + l_k, _jax.tree.map(_jnp.add, grad_sum, gw_k)), gx_k

        init = (_jnp.zeros((), _jnp.float32), _jax.tree.map(_jnp.zeros_like, weights))
        (loss, grad_w), grad_x = _jax.lax.scan(body, init, (per_example, given["loss_target"]))
    with _jax.named_scope("update"):
        delta_w, new_m, new_v = {}, {}, {}
        for n in TWIN_WEIGHTS:
            delta_w[n], new_m[n], new_v[n] = _adamw(weights[n], grad_w[n], given["m_" + n], given["v_" + n])
    return (loss, grad_x, *[grad_w[n] for n in TWIN_WEIGHTS], *[delta_w[n] for n in TWIN_WEIGHTS],
            *[new_m[n] for n in TWIN_WEIGHTS], *[new_v[n] for n in TWIN_WEIGHTS])
```

```python
import functools
import math

import jax
import jax.numpy as jnp
from jax import lax
from jax.experimental import pallas as pl
from jax.experimental.pallas import tpu as pltpu

F32 = jnp.float32
BF16 = jnp.bfloat16
MESH = pl.DeviceIdType.MESH

D_MODEL = 1024
A_WIDTH = 1024
MLA_HEADS = 8
MLA_Q_RANK = 256
MLA_KV_RANK = 128
MLA_QK_DIM = 96
MEM_WIDTH = 512
N_MEM = 256
D_IN = 6048
ROPE_THETA = 500000.0
NORM_EPS = 1e-5
NEG_INF = -1e30
DEEPNORM_ALPHA = 2.0 ** 0.25
DILATED = ((64, 1), (256, 4), (1024, 16))

ADAM_LR = 0.001
ADAM_B1 = 0.9
ADAM_B2 = 0.999
ADAM_EPS = 1e-08
ADAM_WD = 0.01
ADAM_STEP = 10

LANES = 128
VMEM_LIMIT = 56 * 1024 * 1024

PROJ_W = 6144
COL_CQ = 4096
COL_BG = 4608
COL_MQ = 5120
COL_MG = 5632

ROWS_IN, ROWS_UQ, ROWS_UKV, ROWS_MEM, ROWS_OUT = 1512, 48, 32, 256, 512
ROWS_USED = ROWS_IN + ROWS_UQ + ROWS_UKV + ROWS_MEM + ROWS_OUT
ROWS_PACK = 2368
SMALL_ROWS = 8


def _params(sem=None, vmem=VMEM_LIMIT):
    return pltpu.CompilerParams(dimension_semantics=sem, vmem_limit_bytes=vmem)


def _dot(a, b):
    return jnp.dot(a, b, preferred_element_type=F32)


def _dot_nt(a, b):
    return lax.dot_general(a, b, (((1,), (1,)), ((), ())), preferred_element_type=F32)


def _dot_tn(a, b):
    return lax.dot_general(a, b, (((0,), (0,)), ((), ())), preferred_element_type=F32)


def _ln_hat(x):
    mu = jnp.mean(x, axis=-1, keepdims=True)
    xc = x - mu
    var = jnp.mean(xc * xc, axis=-1, keepdims=True)
    rstd = lax.rsqrt(var + NORM_EPS)
    return xc * rstd, rstd


def _rms_hat(x, width):
    ms = jnp.sum(x * x, axis=-1, keepdims=True) * (1.0 / width)
    r = lax.rsqrt(ms + NORM_EPS)
    return x * r, r


def _rms_bwd(u, xh, r, width):
    return r * (u - xh * (jnp.sum(u * xh, axis=-1, keepdims=True) * (1.0 / width)))


def _colsum(v):
    return jnp.sum(v, axis=0, keepdims=True)


def _rope_tables(pos, consts):
    ang = pos * consts[0:1, :]
    c = jnp.cos(ang)
    s = jnp.sin(ang)
    return c, s * consts[2:3, :], -s * consts[1:2, :]


def _rope(x, tables, half, inverse=False):
    c, s_up, s_dn = tables
    if inverse:
        s_up, s_dn = -s_up, -s_dn
    return x * c + pltpu.roll(x, half, 1) * s_up + pltpu.roll(x, LANES - half, 1) * s_dn


def _ln_fwd(x, g, b, tm=512):
    t, d = x.shape

    def body(x_ref, g_ref, b_ref, h_ref):
        xh, _ = _ln_hat(x_ref[...])
        h_ref[...] = (xh * g_ref[...] + b_ref[...]).astype(BF16)

    row = pl.BlockSpec((1, d), lambda i: (0, 0))
    return pl.pallas_call(
        body, name="ln_fwd", grid=(t // tm,),
        out_shape=jax.ShapeDtypeStruct((t, d), BF16),
        in_specs=[pl.BlockSpec((tm, d), lambda i: (i, 0)), row, row],
        out_specs=pl.BlockSpec((tm, d), lambda i: (i, 0)),
        compiler_params=_params(("parallel",)),
    )(x, g, b)


def _mm(a, b, out_dtype, tm, tn, tk, name):
    m, k = a.shape
    n = b.shape[1]
    nk = k // tk

    def body(a_ref, b_ref, o_ref, acc_ref):
        part = _dot(a_ref[...].astype(BF16), b_ref[...].astype(BF16))
        if nk == 1:
            o_ref[...] = part.astype(out_dtype)
        else:
            kk = pl.program_id(2)

            @pl.when(kk == 0)
            def _():
                acc_ref[...] = part

            @pl.when(kk > 0)
            def _():
                acc_ref[...] += part

            @pl.when(kk == nk - 1)
            def _():
                o_ref[...] = acc_ref[...].astype(out_dtype)

    return pl.pallas_call(
        body, name=name, grid=(n // tn, m // tm, nk),
        out_shape=jax.ShapeDtypeStruct((m, n), out_dtype),
        in_specs=[pl.BlockSpec((tm, tk), lambda j, i, kk: (i, kk)),
                  pl.BlockSpec((tk, tn), lambda j, i, kk: (kk, j))],
        out_specs=pl.BlockSpec((tm, tn), lambda j, i, kk: (i, j)),
        scratch_shapes=[pltpu.VMEM((tm, tn), F32)],
        compiler_params=_params(("parallel", "parallel", "arbitrary")),
    )(a, b)


def _prep(proj, pos, w_uq, w_ukv, g_cq, g_ckv, rope_a, rope_b, tm=256):
    t = proj.shape[0]

    def body(aq_ref, ak_ref, av_ref, bs_ref, mq_ref, pos_ref, wuq_ref, wukv_ref, gcq_ref, gckv_ref,
             ra_ref, rb_ref, qa_ref, ka_ref, va_ref, qb_ref, kb_ref, vb_ref, qm_ref, cqn_ref, ckvn_ref):
        pos_c = pos_ref[...]
        ta = _rope_tables(pos_c, ra_ref[...])
        tb = _rope_tables(pos_c, rb_ref[...])
        for j in range(A_WIDTH // LANES):
            sl = slice(j * LANES, (j + 1) * LANES)
            qa_ref[:, sl] = _rope(aq_ref[:, sl], ta, 8).astype(BF16)
            ka_ref[:, sl] = _rope(ak_ref[:, sl], ta, 8).astype(BF16)
        va_ref[...] = av_ref[...].astype(BF16)
        qm_ref[...] = mq_ref[...].astype(BF16)

        cq_hat, _ = _rms_hat(bs_ref[:, 0:MLA_Q_RANK], MLA_Q_RANK)
        cqn = (cq_hat * gcq_ref[...]).astype(BF16)
        cqn_ref[...] = cqn
        ckv_hat, _ = _rms_hat(bs_ref[:, MLA_Q_RANK:MLA_Q_RANK + MLA_KV_RANK], MLA_KV_RANK)
        ckvn = (ckv_hat * gckv_ref[...]).astype(BF16)
        ckvn_ref[...] = ckvn
        qfull = _dot(cqn, wuq_ref[...])
        kv = _dot(ckvn, wukv_ref[...])
        kr = _rope(bs_ref[:, 384:512], tb, 16)
        lane = lax.broadcasted_iota(jnp.int32, (1, LANES), 1)
        low = lane < 64
        for h in range(MLA_HEADS):
            sl = slice(h * LANES, (h + 1) * LANES)
            qb_ref[:, sl] = _rope(qfull[:, sl], tb, 16).astype(BF16)
            kb_ref[:, sl] = jnp.where(low, kv[:, sl], kr).astype(BF16)
            vb_ref[:, sl] = jnp.where(low, 0.0, kv[:, sl]).astype(BF16)

    def col(width, idx):
        return pl.BlockSpec((tm, width), lambda i: (i, idx))

    def full(shape):
        return pl.BlockSpec(shape, lambda i: (0, 0))

    wide = jax.ShapeDtypeStruct((t, 1024), BF16)
    return pl.pallas_call(
        body, name="prep", grid=(t // tm,),
        out_shape=(wide, wide, wide, wide, wide, wide,
                   jax.ShapeDtypeStruct((t, MEM_WIDTH), BF16),
                   jax.ShapeDtypeStruct((t, MLA_Q_RANK), BF16),
                   jax.ShapeDtypeStruct((t, MLA_KV_RANK), BF16)),
        in_specs=[col(1024, 0), col(1024, 1), col(1024, 2), col(512, COL_CQ // 512), col(512, COL_MQ // 512),
                  pl.BlockSpec((tm, 1), lambda i: (i, 0)),
                  full((MLA_Q_RANK, 1024)), full((MLA_KV_RANK, 1024)),
                  full((1, MLA_Q_RANK)), full((1, MLA_KV_RANK)), full((8, LANES)), full((8, LANES))],
        out_specs=(col(1024, 0),) * 6 + (col(MEM_WIDTH, 0), col(MLA_Q_RANK, 0), col(MLA_KV_RANK, 0)),
        compiler_params=_params(("parallel",)),
    )(proj, proj, proj, proj, proj, pos, w_uq, w_ukv, g_cq, g_ckv, rope_a, rope_b)


def _attn_fwd(q, k, v, bias, *, nb, s, sk, groups, hp, scale, qoff, koff, voff, bq, name):
    nq = s // bq
    hw = LANES // hp

    def body(*refs):
        if bias is None:
            q_ref, k_ref, v_ref, o_ref, lse_ref = refs
        else:
            q_ref, k_ref, v_ref, bias_ref, o_ref, lse_ref = refs
        qf = q_ref[...]
        kk = k_ref[...]
        vv = v_ref[...]
        lane = lax.broadcasted_iota(jnp.int32, (1, LANES), 1)
        o_all = None
        lse_all = None
        for h in range(hp):
            mask = (lane >= h * hw) & (lane < (h + 1) * hw)
            qh = jnp.where(mask, qf, jnp.zeros_like(qf)) if hp > 1 else qf
            vh = jnp.where(mask, vv, jnp.zeros_like(vv)) if hp > 1 else vv
            sc = _dot_nt(qh, kk) * scale
            if bias is not None:
                sc = sc + bias_ref[...]
            m = jnp.max(sc, axis=1, keepdims=True)
            p = jnp.exp(sc - m)
            l = jnp.sum(p, axis=1, keepdims=True)
            o = _dot(p.astype(BF16), vh) / l
            lse = jnp.broadcast_to(m + jnp.log(l), (bq, LANES))
            o_all = o if h == 0 else o_all + o
            lse_all = lse if h == 0 else jnp.where(mask, lse, lse_all)
        o_ref[...] = o_all
        lse_ref[...] = lse_all

    in_specs = [pl.BlockSpec((bq, LANES), lambda b, i, g: (b * nq + i, qoff + g)),
                pl.BlockSpec((sk, LANES), lambda b, i, g: (b, koff + g)),
                pl.BlockSpec((sk, LANES), lambda b, i, g: (b, voff + g))]
    args = [q, k, v]
    if bias is not None:
        in_specs.append(pl.BlockSpec((None, bq, sk), lambda b, i, g: (i, 0, 0)))
        args.append(bias)
    out = jax.ShapeDtypeStruct((nb * s, groups * LANES), F32)
    ospec = pl.BlockSpec((bq, LANES), lambda b, i, g: (b * nq + i, g))
    return pl.pallas_call(
        body, name=name, grid=(nb, nq, groups),
        out_shape=(out, out), in_specs=in_specs, out_specs=(ospec, ospec),
        compiler_params=_params(("parallel", "parallel", "parallel")),
    )(*args)


def _attn_bwd(q, k, v, o, do, lse, bias, *, nb, s, sk, groups, hp, scale, qoff, koff, voff, bq, name):
    nq = s // bq
    hw = LANES // hp

    def body(*refs):
        if bias is None:
            q_ref, k_ref, v_ref, o_ref, do_ref, lse_ref, dq_ref, dk_ref, dv_ref = refs
        else:
            q_ref, k_ref, v_ref, o_ref, do_ref, lse_ref, bias_ref, dq_ref, dk_ref, dv_ref = refs
        i = pl.program_id(2)

        @pl.when(i == 0)
        def _():
            dk_ref[...] = jnp.zeros_like(dk_ref)
            dv_ref[...] = jnp.zeros_like(dv_ref)

        qf = q_ref[...]
        kk = k_ref[...]
        vv = v_ref[...]
        dof = do_ref[...]
        prod = dof.astype(F32) * o_ref[...]
        lse = lse_ref[...]
        lane = lax.broadcasted_iota(jnp.int32, (1, LANES), 1)
        dq_all = None
        for h in range(hp):
            mask = (lane >= h * hw) & (lane < (h + 1) * hw)
            if hp > 1:
                qh = jnp.where(mask, qf, jnp.zeros_like(qf))
                doh = jnp.where(mask, dof, jnp.zeros_like(dof))
                delta = jnp.sum(jnp.where(mask, prod, 0.0), axis=1, keepdims=True)
            else:
                qh, doh = qf, dof
                delta = jnp.sum(prod, axis=1, keepdims=True)
            lse_h = lse[:, h * hw:h * hw + 1]
            sc = _dot_nt(qh, kk) * scale
            if bias is not None:
                sc = sc + bias_ref[...]
            p = jnp.exp(sc - lse_h)
            dp = _dot_nt(doh, vv)
            ds = (p * (dp - delta) * scale).astype(BF16)
            dq = _dot(ds, kk)
            dq_all = jnp.where(mask, dq, 0.0 if h == 0 else dq_all) if hp > 1 else dq
            dk_ref[...] += _dot_tn(ds, qh)
            dv_ref[...] += _dot_tn(p.astype(BF16), doh)
        dq_ref[...] = dq_all

    in_specs = [pl.BlockSpec((bq, LANES), lambda b, g, i: (b * nq + i, qoff + g)),
                pl.BlockSpec((sk, LANES), lambda b, g, i: (b, koff + g)),
                pl.BlockSpec((sk, LANES), lambda b, g, i: (b, voff + g)),
                pl.BlockSpec((bq, LANES), lambda b, g, i: (b * nq + i, g)),
                pl.BlockSpec((bq, LANES), lambda b, g, i: (b * nq + i, g)),
                pl.BlockSpec((bq, LANES), lambda b, g, i: (b * nq + i, g))]
    args = [q, k, v, o, do, lse]
    if bias is not None:
        in_specs.append(pl.BlockSpec((None, bq, sk), lambda b, g, i: (i, 0, 0)))
        args.append(bias)
    dq_shape = jax.ShapeDtypeStruct((nb * s, groups * LANES), F32)
    dkv_shape = jax.ShapeDtypeStruct((nb * sk, groups * LANES), F32)
    kv_spec = pl.BlockSpec((sk, LANES), lambda b, g, i: (b, g))
    return pl.pallas_call(
        body, name=name, grid=(nb, groups, nq),
        out_shape=(dq_shape, dkv_shape, dkv_shape), in_specs=in_specs,
        out_specs=(pl.BlockSpec((bq, LANES), lambda b, g, i: (b * nq + i, g)), kv_spec, kv_spec),
        compiler_params=_params(("parallel", "parallel", "arbitrary")),
    )(*args)


def _post(x, ya, ybp, ym, proj, target, w_out, w_out_t, g_emb, b_emb, g_a, g_b, g_m, g_post, b_post, tm=256):
    t = x.shape[0]

    def body(x_ref, ya_ref, yb_ref, ym_ref, ga_ref, gb_ref, gm_ref, tg_ref, wo_ref, wot_ref,
             ge_ref, be_ref, goa_ref, gob_ref, gom_ref, gp_ref, bp_ref,
             y_ref, dz_ref, doa_ref, dob_ref, dom_ref, dga_ref, dgb_ref, dgm_ref,
             loss_ref, dgp_ref, dbp_ref, dgoa_ref, dgob_ref, dgom_ref):
        i = pl.program_id(0)

        @pl.when(i == 0)
        def _():
            for r in (loss_ref, dgp_ref, dbp_ref, dgoa_ref, dgob_ref, dgom_ref):
                r[...] = jnp.zeros_like(r)

        lane = lax.broadcasted_iota(jnp.int32, (1, LANES), 1)
        low = lane < 64
        xh0, _ = _ln_hat(x_ref[...])
        h = xh0 * ge_ref[...] + be_ref[...]

        ybp_v = yb_ref[...]
        yb = jnp.concatenate(
            [jnp.where(low, pltpu.roll(ybp_v[:, 2 * j * LANES:(2 * j + 1) * LANES], 64, 1),
                       ybp_v[:, (2 * j + 1) * LANES:(2 * j + 2) * LANES]) for j in range(4)], axis=1)

        def gated(raw, gate, gain, width):
            xh, r = _rms_hat(raw, width)
            n = xh * gain
            sg = 1.0 / (1.0 + jnp.exp(-gate))
            return xh, r, n, sg, n * (gate * sg)

        gate_a, gate_b, gate_m = ga_ref[...], gb_ref[...], gm_ref[...]
        xh_a, r_a, n_a, sg_a, y_a = gated(ya_ref[...], gate_a, goa_ref[...], A_WIDTH)
        xh_b, r_b, n_b, sg_b, y_b = gated(yb, gate_b, gob_ref[...], 512)
        xh_m, r_m, n_m, sg_m, y_m = gated(ym_ref[...], gate_m, gom_ref[...], 512)
        y = jnp.concatenate([y_a, y_b, y_m], axis=1).astype(BF16)
        y_ref[...] = y
        z = DEEPNORM_ALPHA * h + _dot(y, wo_ref[...])
        zh, rstd = _ln_hat(z)
        err = zh * gp_ref[...] + bp_ref[...] - tg_ref[...]
        rows = jnp.sum(err * err, axis=1, keepdims=True)
        loss_ref[...] += jnp.broadcast_to(jnp.sum(rows, axis=0, keepdims=True) * (0.5 / D_MODEL), (1, LANES))
        dout = err * (1.0 / D_MODEL)
        dgp_ref[...] += _colsum(dout * zh)
        dbp_ref[...] += _colsum(dout)
        dzh = dout * gp_ref[...]
        dz = rstd * (dzh - jnp.mean(dzh, axis=-1, keepdims=True) - zh * jnp.mean(dzh * zh, axis=-1, keepdims=True))
        dz_ref[...] = dz
        dy = _dot(dz.astype(BF16), wot_ref[...])

        def gated_bwd(dyg, xh, r, n, sg, gate, gain, width, dgain_ref):
            dn = dyg * (gate * sg)
            dgate = dyg * n * (sg * (1.0 + gate * (1.0 - sg)))
            dgain_ref[...] += _colsum(dn * xh)
            return _rms_bwd(dn * gain, xh, r, width), dgate

        dya, dgate_a = gated_bwd(dy[:, 0:1024], xh_a, r_a, n_a, sg_a, gate_a, goa_ref[...], A_WIDTH, dgoa_ref)
        dyb, dgate_b = gated_bwd(dy[:, 1024:1536], xh_b, r_b, n_b, sg_b, gate_b, gob_ref[...], 512, dgob_ref)
        dym, dgate_m = gated_bwd(dy[:, 1536:2048], xh_m, r_m, n_m, sg_m, gate_m, gom_ref[...], 512, dgom_ref)
        doa_ref[...] = dya.astype(BF16)
        dom_ref[...] = dym.astype(BF16)
        dga_ref[...] = dgate_a.astype(BF16)
        dgb_ref[...] = dgate_b.astype(BF16)
        dgm_ref[...] = dgate_m.astype(BF16)
        for j in range(4):
            blk = dyb[:, j * LANES:(j + 1) * LANES]
            dob_ref[:, 2 * j * LANES:(2 * j + 1) * LANES] = jnp.where(low, 0.0, pltpu.roll(blk, 64, 1)).astype(BF16)
            dob_ref[:, (2 * j + 1) * LANES:(2 * j + 2) * LANES] = jnp.where(low, 0.0, blk).astype(BF16)

    def col(width, idx):
        return pl.BlockSpec((tm, width), lambda i: (i, idx))

    def full(shape):
        return pl.BlockSpec(shape, lambda i: (0, 0))

    def acc(width):
        return jax.ShapeDtypeStruct((1, width), F32)

    return pl.pallas_call(
        body, name="post", grid=(t // tm,),
        out_shape=(jax.ShapeDtypeStruct((t, 2048), BF16), jax.ShapeDtypeStruct((t, 1024), F32),
                   jax.ShapeDtypeStruct((t, 1024), BF16), jax.ShapeDtypeStruct((t, 1024), BF16),
                   jax.ShapeDtypeStruct((t, 512), BF16),
                   jax.ShapeDtypeStruct((t, 1024), BF16), jax.ShapeDtypeStruct((t, 512), BF16),
                   jax.ShapeDtypeStruct((t, 512), BF16),
                   acc(LANES), acc(1024), acc(1024), acc(1024), acc(512), acc(512)),
        in_specs=[col(1024, 0), col(1024, 0), col(1024, 0), col(512, 0),
                  col(1024, 3), col(512, COL_BG // 512), col(512, COL_MG // 512), col(1024, 0),
                  full((2048, 1024)), full((1024, 2048)),
                  full((1, 1024)), full((1, 1024)), full((1, 1024)), full((1, 512)), full((1, 512)),
                  full((1, 1024)), full((1, 1024))],
        out_specs=(col(2048, 0), col(1024, 0), col(1024, 0), col(1024, 0), col(512, 0),
                   col(1024, 0), col(512, 0), col(512, 0),
                   full((1, LANES)), full((1, 1024)), full((1, 1024)), full((1, 1024)), full((1, 512)),
                   full((1, 512))),
        compiler_params=_params(("arbitrary",)),
    )(x, ya, ybp, ym, proj, proj, proj, target, w_out, w_out_t, g_emb, b_emb, g_a, g_b, g_m, g_post, b_post)


def _prep_bwd(dqa, dka, dva, dqb, dkb, dvb, dqm, dga, dgb, dgm, proj, pos, w_uq_t, w_ukv_t, g_cq, g_ckv,
              rope_a, rope_b, tm=256):
    t = proj.shape[0]

    def body(dqa_ref, dka_ref, dva_ref, dqb_ref, dkb_ref, dvb_ref, dqm_ref, dga_ref, dgb_ref, dgm_ref,
             bs_ref, pos_ref, wuqt_ref, wukvt_ref, gcq_ref, gckv_ref, ra_ref, rb_ref,
             dproj_ref, dqf_ref, dkv_ref, dgcq_ref, dgckv_ref):
        i = pl.program_id(0)

        @pl.when(i == 0)
        def _():
            dgcq_ref[...] = jnp.zeros_like(dgcq_ref)
            dgckv_ref[...] = jnp.zeros_like(dgckv_ref)

        pos_c = pos_ref[...]
        ta = _rope_tables(pos_c, ra_ref[...])
        tb = _rope_tables(pos_c, rb_ref[...])
        for j in range(A_WIDTH // LANES):
            sl = slice(j * LANES, (j + 1) * LANES)
            dproj_ref[:, j * LANES:(j + 1) * LANES] = _rope(dqa_ref[:, sl], ta, 8, inverse=True).astype(BF16)
            dproj_ref[:, 1024 + j * LANES:1024 + (j + 1) * LANES] = (
                _rope(dka_ref[:, sl], ta, 8, inverse=True).astype(BF16))
        dproj_ref[:, 2048:3072] = dva_ref[...].astype(BF16)
        dproj_ref[:, 3072:4096] = dga_ref[...]

        lane = lax.broadcasted_iota(jnp.int32, (1, LANES), 1)
        low = lane < 64
        rope_lanes = (lane >= 64) & (lane < 96)
        dkr = jnp.zeros((tm, LANES), F32)
        for h in range(MLA_HEADS):
            sl = slice(h * LANES, (h + 1) * LANES)
            dqf_ref[:, sl] = _rope(dqb_ref[:, sl], tb, 16, inverse=True).astype(BF16)
            dk_h = dkb_ref[:, sl]
            dkv_ref[:, sl] = jnp.where(low, dk_h, dvb_ref[:, sl]).astype(BF16)
            dkr = dkr + jnp.where(rope_lanes, dk_h, 0.0)
        dkr = _rope(dkr, tb, 16, inverse=True)

        cq_hat, r_q = _rms_hat(bs_ref[:, 0:MLA_Q_RANK], MLA_Q_RANK)
        dcqn = _dot(dqf_ref[...], wuqt_ref[...])
        dgcq_ref[...] += _colsum(dcqn * cq_hat)
        dproj_ref[:, COL_CQ:COL_CQ + 256] = _rms_bwd(dcqn * gcq_ref[...], cq_hat, r_q, MLA_Q_RANK).astype(BF16)
        ckv_hat, r_kv = _rms_hat(bs_ref[:, MLA_Q_RANK:MLA_Q_RANK + MLA_KV_RANK], MLA_KV_RANK)
        dckvn = _dot(dkv_ref[...], wukvt_ref[...])
        dgckv_ref[...] += _colsum(dckvn * ckv_hat)
        dproj_ref[:, COL_CQ + 256:COL_CQ + 384] = (
            _rms_bwd(dckvn * gckv_ref[...], ckv_hat, r_kv, MLA_KV_RANK).astype(BF16))
        dproj_ref[:, COL_CQ + 384:COL_CQ + 512] = dkr.astype(BF16)
        dproj_ref[:, COL_BG:COL_BG + 512] = dgb_ref[...]
        dproj_ref[:, COL_MQ:COL_MQ + 512] = dqm_ref[...].astype(BF16)
        dproj_ref[:, COL_MG:COL_MG + 512] = dgm_ref[...]

    def col(width, idx):
        return pl.BlockSpec((tm, width), lambda i: (i, idx))

    def full(shape):
        return pl.BlockSpec(shape, lambda i: (0, 0))

    return pl.pallas_call(
        body, name="prep_bwd", grid=(t // tm,),
        out_shape=(jax.ShapeDtypeStruct((t, PROJ_W), BF16), jax.ShapeDtypeStruct((t, 1024), BF16),
                   jax.ShapeDtypeStruct((t, 1024), BF16),
                   jax.ShapeDtypeStruct((1, MLA_Q_RANK), F32), jax.ShapeDtypeStruct((1, MLA_KV_RANK), F32)),
        in_specs=[col(1024, 0)] * 6 + [col(512, 0), col(1024, 0), col(512, 0), col(512, 0),
                  col(512, COL_CQ // 512), pl.BlockSpec((tm, 1), lambda i: (i, 0)),
                  full((1024, MLA_Q_RANK)), full((1024, MLA_KV_RANK)),
                  full((1, MLA_Q_RANK)), full((1, MLA_KV_RANK)), full((8, LANES)), full((8, LANES))],
        out_specs=(col(PROJ_W, 0), col(1024, 0), col(1024, 0), full((1, MLA_Q_RANK)), full((1, MLA_KV_RANK))),
        compiler_params=_params(("arbitrary",)),
    )(dqa, dka, dva, dqb, dkb, dvb, dqm, dga, dgb, dgm, proj, pos, w_uq_t, w_ukv_t, g_cq, g_ckv, rope_a, rope_b)


def _ln_bwd(x, dh, dz, g, tm=512):
    t, d = x.shape

    def body(x_ref, dh_ref, dz_ref, g_ref, dx_ref, dg_ref, db_ref):
        i = pl.program_id(0)

        @pl.when(i == 0)
        def _():
            dg_ref[...] = jnp.zeros_like(dg_ref)
            db_ref[...] = jnp.zeros_like(db_ref)

        xh, rstd = _ln_hat(x_ref[...])
        dht = dh_ref[...] + DEEPNORM_ALPHA * dz_ref[...]
        dg_ref[...] += _colsum(dht * xh)
        db_ref[...] += _colsum(dht)
        dxh = dht * g_ref[...]
        dx_ref[...] = rstd * (dxh - jnp.mean(dxh, axis=-1, keepdims=True)
                              - xh * jnp.mean(dxh * xh, axis=-1, keepdims=True))

    tile = pl.BlockSpec((tm, d), lambda i: (i, 0))
    row = pl.BlockSpec((1, d), lambda i: (0, 0))
    return pl.pallas_call(
        body, name="ln_bwd", grid=(t // tm,),
        out_shape=(jax.ShapeDtypeStruct((t, d), F32), jax.ShapeDtypeStruct((1, d), F32),
                   jax.ShapeDtypeStruct((1, d), F32)),
        in_specs=[tile, tile, tile, row], out_specs=(tile, row, row),
        compiler_params=_params(("arbitrary",)),
    )(x, dh, dz, g)


def _rs_sum(own, recv, tr=592):
    r = own.shape[0]

    def body(own_ref, recv_ref, out_ref):
        acc = own_ref[...]
        for k in range(3):
            acc = acc + recv_ref[k].astype(F32)
        out_ref[...] = acc

    return pl.pallas_call(
        body, name="rs_sum", grid=(r // tr,),
        out_shape=jax.ShapeDtypeStruct((r, 1024), F32),
        in_specs=[pl.BlockSpec((tr, 1024), lambda i: (i, 0)), pl.BlockSpec((3, tr, 1024), lambda i: (0, i, 0))],
        out_specs=pl.BlockSpec((tr, 1024), lambda i: (i, 0)),
        compiler_params=_params(("parallel",)),
    )(own, recv)


def _adamw(ga, gb, w, m, v, tr, name):
    r = w.shape[0]

    def body(ga_ref, gb_ref, w_ref, m_ref, v_ref, g_ref, d_ref, nm_ref, nv_ref):
        g = ga_ref[...] + gb_ref[...]
        m_new = ADAM_B1 * m_ref[...] + (1.0 - ADAM_B1) * g
        v_new = ADAM_B2 * v_ref[...] + (1.0 - ADAM_B2) * (g * g)
        m_hat = m_new / (1.0 - ADAM_B1 ** ADAM_STEP)
        v_hat = v_new / (1.0 - ADAM_B2 ** ADAM_STEP)
        g_ref[...] = g
        d_ref[...] = -ADAM_LR * (m_hat / (jnp.sqrt(v_hat) + ADAM_EPS) + ADAM_WD * w_ref[...])
        nm_ref[...] = m_new
        nv_ref[...] = v_new

    tile = pl.BlockSpec((tr, 1024), lambda i: (i, 0))
    shape = jax.ShapeDtypeStruct((r, 1024), F32)
    return pl.pallas_call(
        body, name=name, grid=(r // tr,),
        out_shape=(shape,) * 4, in_specs=[tile] * 5, out_specs=(tile,) * 4,
        compiler_params=_params(("parallel",)),
    )(ga, gb, w, m, v)


def _chip_peers():
    x, y, c = lax.axis_index("x"), lax.axis_index("y"), lax.axis_index("c")
    return x, y, c, [(1 - x, y), (x, 1 - y), (1 - x, 1 - y)]


def _gather_weights(packed):
    r = packed.shape[0]

    def body(src_ref, out_ref, send_sems, recv_sems, local_sem):
        x, y, c, peers = _chip_peers()
        me = 2 * x + y
        mine = pltpu.make_async_copy(src_ref, out_ref.at[me], local_sem)
        mine.start()

        def copy(k, px, py, slot):
            return pltpu.make_async_remote_copy(
                src_ref=src_ref, dst_ref=out_ref.at[slot], send_sem=send_sems.at[k], recv_sem=recv_sems.at[k],
                device_id=(px, py, c), device_id_type=MESH)

        sends = [copy(k, px, py, me) for k, (px, py) in enumerate(peers)]
        for cp in sends:
            cp.start()
        for k, (px, py) in enumerate(peers):
            copy(k, px, py, 2 * px + py).wait_recv()
        for cp in sends:
            cp.wait_send()
        mine.wait()

    return pl.pallas_call(
        body, name="gather_weights",
        out_shape=jax.ShapeDtypeStruct((4, r, 1024), packed.dtype),
        in_specs=[pl.BlockSpec(memory_space=pl.ANY)], out_specs=pl.BlockSpec(memory_space=pl.ANY),
        scratch_shapes=[pltpu.SemaphoreType.DMA((3,)), pltpu.SemaphoreType.DMA((3,)), pltpu.SemaphoreType.DMA(())],
    )(packed)


def _scatter_grads(gbf):
    r = gbf.shape[1]

    def body(src_ref, out_ref, send_sems, recv_sems):
        x, y, c, peers = _chip_peers()

        def copy(k, px, py):
            return pltpu.make_async_remote_copy(
                src_ref=src_ref.at[2 * px + py], dst_ref=out_ref.at[k], send_sem=send_sems.at[k],
                recv_sem=recv_sems.at[k], device_id=(px, py, c), device_id_type=MESH)

        copies = [copy(k, px, py) for k, (px, py) in enumerate(peers)]
        for cp in copies:
            cp.start()
        for cp in copies:
            cp.wait_recv()
        for cp in copies:
            cp.wait_send()

    return pl.pallas_call(
        body, name="scatter_grads",
        out_shape=jax.ShapeDtypeStruct((3, r, 1024), gbf.dtype),
        in_specs=[pl.BlockSpec(memory_space=pl.ANY)], out_specs=pl.BlockSpec(memory_space=pl.ANY),
        scratch_shapes=[pltpu.SemaphoreType.DMA((3,)), pltpu.SemaphoreType.DMA((3,))],
    )(gbf)


def _swap_cores(part):
    def body(src_ref, out_ref, send_sem, recv_sem):
        x, y, c = lax.axis_index("x"), lax.axis_index("y"), lax.axis_index("c")
        cp = pltpu.make_async_remote_copy(
            src_ref=src_ref, dst_ref=out_ref, send_sem=send_sem, recv_sem=recv_sem,
            device_id=(x, y, 1 - c), device_id_type=MESH)
        cp.start()
        cp.wait_recv()
        cp.wait_send()

    return pl.pallas_call(
        body, name="swap_cores",
        out_shape=jax.ShapeDtypeStruct(part.shape, part.dtype),
        in_specs=[pl.BlockSpec(memory_space=pl.ANY)], out_specs=pl.BlockSpec(memory_space=pl.ANY),
        scratch_shapes=[pltpu.SemaphoreType.DMA(()), pltpu.SemaphoreType.DMA(())],
    )(part)


def _allreduce_small(vec):
    def body(vec_ref, out_ref, all_ref, send_sems, recv_sems):
        x, y, c = lax.axis_index("x"), lax.axis_index("y"), lax.axis_index("c")
        me = 4 * x + 2 * y + c
        all_ref[me] = vec_ref[...]
        copies = []
        for k in range(1, 8):
            peer = (x ^ (k >> 2), y ^ ((k >> 1) & 1), c ^ (k & 1))
            copies.append(pltpu.make_async_remote_copy(
                src_ref=vec_ref, dst_ref=all_ref.at[me], send_sem=send_sems.at[k - 1], recv_sem=recv_sems.at[k - 1],
                device_id=peer, device_id_type=MESH))
        for cp in copies:
            cp.start()
        for k in range(1, 8):
            px, py, pc = x ^ (k >> 2), y ^ ((k >> 1) & 1), c ^ (k & 1)
            pltpu.make_async_remote_copy(
                src_ref=vec_ref, dst_ref=all_ref.at[4 * px + 2 * py + pc], send_sem=send_sems.at[k - 1],
                recv_sem=recv_sems.at[k - 1], device_id=(px, py, pc), device_id_type=MESH).wait_recv()
        for cp in copies:
            cp.wait_send()
        total = all_ref[0]
        for d in range(1, 8):
            total = total + all_ref[d]
        out_ref[...] = total

    return pl.pallas_call(
        body, name="allreduce_small",
        out_shape=jax.ShapeDtypeStruct(vec.shape, vec.dtype),
        in_specs=[pl.BlockSpec(memory_space=pltpu.VMEM)], out_specs=pl.BlockSpec(memory_space=pltpu.VMEM),
        scratch_shapes=[pltpu.VMEM((8,) + vec.shape, vec.dtype), pltpu.SemaphoreType.DMA((7,)),
                        pltpu.SemaphoreType.DMA((7,))],
    )(vec)


def _pack_rows(w_in, w_uq, w_ukv, w_mem, w_out):
    rows = jnp.concatenate([w_in.reshape(-1, 1024), w_uq.reshape(-1, 1024), w_ukv.reshape(-1, 1024),
                            w_mem.reshape(-1, 1024), w_out.reshape(-1, 1024)], axis=0)
    return jnp.pad(rows, ((0, ROWS_PACK - ROWS_USED), (0, 0)))


def _unpack_rows(p):
    o = 0
    out = []
    for rows, shape in ((ROWS_IN, (1, 1024, 1512)), (ROWS_UQ, (1, 256, 192)), (ROWS_UKV, (1, 128, 256)),
                        (ROWS_MEM, (1, 256, 1024)), (ROWS_OUT, (1, 512, 1024))):
        out.append(p[o:o + rows].reshape(shape))
        o += rows
    return out


def _full_weights(gathered):
    def cols(lo, rows, shape):
        return jnp.concatenate([gathered[j, lo:lo + rows].reshape(shape) for j in range(4)], axis=1)

    def rows_of(lo, rows):
        return jnp.concatenate([gathered[j, lo:lo + rows] for j in range(4)], axis=0)

    w_in = cols(0, ROWS_IN, (1024, 1512))
    w_uq = cols(ROWS_IN, ROWS_UQ, (256, 192))
    w_ukv = cols(ROWS_IN + ROWS_UQ, ROWS_UKV, (128, 256))
    w_mem = rows_of(ROWS_IN + ROWS_UQ + ROWS_UKV, ROWS_MEM)
    w_out = rows_of(ROWS_IN + ROWS_UQ + ROWS_UKV + ROWS_MEM, ROWS_OUT)
    z = functools.partial(jnp.zeros, dtype=w_in.dtype)
    w_in_arr = jnp.concatenate([w_in[:, :4480], z((1024, 64)), w_in[:, 4480:4512], z((1024, 32)), w_in[:, 4512:]],
                               axis=1)
    w_uq_pad = jnp.pad(w_uq.reshape(256, MLA_HEADS, MLA_QK_DIM), ((0, 0), (0, 0), (0, 32))).reshape(256, 1024)
    return w_in_arr, w_uq_pad, w_ukv, w_mem, w_out


def _pack_grads(dw_in_arr, dw_uq_pad, dw_ukv, dw_mem, dw_out):
    dw_in = jnp.concatenate([dw_in_arr[:, :4480], dw_in_arr[:, 4544:4576], dw_in_arr[:, 4608:]], axis=1)
    dw_uq = dw_uq_pad.reshape(256, MLA_HEADS, LANES)[:, :, :MLA_QK_DIM].reshape(256, 768)

    def by_cols(a, width):
        r = a.shape[0]
        return a.reshape(r, 4, width).transpose(1, 0, 2).reshape(4, -1, 1024)

    parts = [by_cols(dw_in, 1512), by_cols(dw_uq, 192), by_cols(dw_ukv, 256),
             dw_mem.reshape(4, ROWS_MEM, 1024), dw_out.reshape(4, ROWS_OUT, 1024)]
    return jnp.pad(jnp.concatenate(parts, axis=1), ((0, 0), (0, ROWS_PACK - ROWS_USED), (0, 0)))


def _rope_consts(rot, first, period):
    half = rot // 2
    inv_freq = ROPE_THETA ** (-(jnp.arange(0, rot, 2, dtype=F32) / rot))
    lane = jnp.arange(LANES) % period - first
    in_rot = (lane >= 0) & (lane < rot)
    freq = jnp.where(in_rot, inv_freq[jnp.clip(lane, 0, rot - 1) % half], 0.0)
    lo = (in_rot & (lane < half)).astype(F32)
    hi = (in_rot & (lane >= half)).astype(F32)
    return jnp.concatenate([freq[None], lo[None], hi[None], jnp.zeros((5, LANES), F32)], axis=0)


def _dilated_bias(s, bq):
    delta = jnp.arange(s)[None, :] - jnp.arange(s)[:, None]
    count = jnp.zeros((s, s), F32)
    for reach, dil in DILATED:
        count = count + ((jnp.abs(delta) <= reach) & (delta % dil == 0)).astype(F32)
    bias = jnp.where(count > 0, jnp.log(jnp.maximum(count, 1.0)), NEG_INF)
    return bias.reshape(s // bq, bq, s)


def _local_step(x, mem, positions, target, weights, gains):
    w_in_arr, w_uq_pad, w_ukv, w_mem, w_out = weights
    g_emb, b_emb, g_cq, g_ckv, g_out_a, g_out_b, g_out_m, g_post, b_post = gains
    nb, s, d = x.shape
    t = nb * s
    x2 = x.reshape(t, d)
    mem2 = mem.reshape(nb * N_MEM, d)
    tgt2 = target.reshape(t, d)
    pos = positions.reshape(t, 1).astype(F32)
    rope_a = _rope_consts(16, 0, 64)
    rope_b = _rope_consts(32, 64, 128)
    bq_a = 256
    bias = _dilated_bias(s, bq_a)

    h = _ln_fwd(x2, g_emb, b_emb)
    proj = _mm(h, w_in_arr, F32, 1024, 1024, 1024, "in_proj")
    qa, ka, va, qb, kb, vb, qm, cqn, ckvn = _prep(proj, pos, w_uq_pad, w_ukv, g_cq, g_ckv, rope_a, rope_b)
    mkv = _mm(mem2, w_mem, BF16, nb * N_MEM, 1024, 1024, "mem_kv")

    cfg_a = dict(nb=nb, s=s, sk=s, groups=8, hp=2, scale=0.125, qoff=0, koff=0, voff=0, bq=bq_a)
    cfg_b = dict(nb=nb, s=s, sk=s, groups=8, hp=1, scale=MLA_QK_DIM ** -0.5, qoff=0, koff=0, voff=0, bq=256)
    cfg_m = dict(nb=nb, s=s, sk=N_MEM, groups=4, hp=1, scale=128 ** -0.5, qoff=0, koff=0, voff=4, bq=512)
    ya, lse_a = _attn_fwd(qa, ka, va, bias, name="attn_a_fwd", **cfg_a)
    yb, lse_b = _attn_fwd(qb, kb, vb, None, name="attn_b_fwd", **cfg_b)
    ym, lse_m = _attn_fwd(qm, mkv, mkv, None, name="attn_m_fwd", **cfg_m)

    (y, dz, doa, dob, dom, dga, dgb, dgm, loss, dg_post, db_post, dg_a, dg_b, dg_m) = _post(
        x2, ya, yb, ym, proj, tgt2, w_out, w_out.T, g_emb, b_emb, g_out_a, g_out_b, g_out_m, g_post, b_post)

    dqa, dka, dva = _attn_bwd(qa, ka, va, ya, doa, lse_a, bias, name="attn_a_bwd", **cfg_a)
    dqb, dkb, dvb = _attn_bwd(qb, kb, vb, yb, dob, lse_b, None, name="attn_b_bwd", **cfg_b)
    dqm, dmk, dmv = _attn_bwd(qm, mkv, mkv, ym, dom, lse_m, None, name="attn_m_bwd", **cfg_m)
    dmkv = jnp.concatenate([dmk, dmv], axis=1)

    dproj, dqf, dkv, dg_cq, dg_ckv = _prep_bwd(
        dqa, dka, dva, dqb, dkb, dvb, dqm, dga, dgb, dgm, proj, pos, w_uq_pad.T, w_ukv.T, g_cq, g_ckv,
        rope_a, rope_b)

    dw_in_arr = _mm(h.T, dproj, F32, 1024, 1024, 1024, "dw_in")
    dh = _mm(dproj, w_in_arr.T, F32, 1024, 1024, 1024, "dh")
    dw_out = _mm(y.T, dz, F32, 1024, 1024, 1024, "dw_out")
    dw_uq_pad = _mm(cqn.T, dqf, F32, 256, 1024, 1024, "dw_uq")
    dw_ukv = _mm(ckvn.T, dkv, F32, 128, 1024, 1024, "dw_ukv")
    dw_mem = _mm(mem2.T, dmkv, F32, 1024, 1024, nb * N_MEM, "dw_mem")
    grad_x, dg_emb, db_emb = _ln_bwd(x2, dh, dz, g_emb)

    row2 = jnp.concatenate([dg_cq, dg_ckv, loss, jnp.zeros((1, 512), F32)], axis=1)
    small = jnp.concatenate([dg_emb, db_emb, row2, dg_a, jnp.concatenate([dg_b, dg_m], axis=1), dg_post, db_post,
                             jnp.zeros((1, 1024), F32)], axis=0)
    return grad_x.reshape(nb, s, d), (dw_in_arr, dw_uq_pad, dw_ukv, dw_mem, dw_out), small


def _pack_small(g_emb, b_emb, g_cq, g_ckv, g_out_a, g_out_b, g_out_m, g_post, b_post):
    row2 = jnp.concatenate([g_cq.reshape(1, -1), g_ckv.reshape(1, -1), jnp.zeros((1, 640), F32)], axis=1)
    return jnp.concatenate([g_emb.reshape(1, -1), b_emb.reshape(1, -1), row2, g_out_a.reshape(1, -1),
                            jnp.concatenate([g_out_b.reshape(1, -1), g_out_m.reshape(1, -1)], axis=1),
                            g_post.reshape(1, -1), b_post.reshape(1, -1), jnp.zeros((1, 1024), F32)], axis=0)


def _unpack_small(p):
    return [p[0], p[1], p[2:3, 0:256], p[2:3, 256:384], p[3:4], p[4:5, 0:512], p[4:5, 512:1024], p[5:6], p[6:7]]


def kernel(x, mem, positions, g_emb, b_emb, w_in, g_cq, g_ckv, w_uq, w_ukv, w_mem_kv, g_out_a, g_out_b, g_out_m, w_out, g_post, b_post, loss_target, m_g_emb, m_b_emb, m_w_in, m_g_cq, m_g_ckv, m_w_uq, m_w_ukv, m_w_mem_kv, m_g_out_a, m_g_out_b, m_g_out_m, m_w_out, m_g_post, m_b_post, v_g_emb, v_b_emb, v_w_in, v_g_cq, v_g_ckv, v_w_uq, v_w_ukv, v_w_mem_kv, v_g_out_a, v_g_out_b, v_g_out_m, v_w_out, v_g_post, v_b_post):
    w_pack = _pack_rows(w_in, w_uq, w_ukv, w_mem_kv, w_out)
    gathered = _gather_weights(w_pack.astype(BF16))
    weights = _full_weights(gathered)
    gains = (g_emb.reshape(1, -1), b_emb.reshape(1, -1), g_cq, g_ckv, g_out_a, g_out_b, g_out_m, g_post, b_post)
    grad_x, dws, small = _local_step(x, mem, positions, loss_target, weights, gains)

    gpack = _pack_grads(*dws)
    recv = _scatter_grads(gpack.astype(BF16))
    me = 2 * lax.axis_index("x") + lax.axis_index("y")
    own = lax.dynamic_index_in_dim(gpack, me, axis=0, keepdims=False)
    part = _rs_sum(own, recv)
    other = _swap_cores(part)
    g_big, d_big, m_big, v_big = _adamw(
        part, other, w_pack, _pack_rows(m_w_in, m_w_uq, m_w_ukv, m_w_mem_kv, m_w_out),
        _pack_rows(v_w_in, v_w_uq, v_w_ukv, v_w_mem_kv, v_w_out), 592, "adamw_big")

    small_sum = _allreduce_small(small)
    g_sm, d_sm, m_sm, v_sm = _adamw(
        small_sum, jnp.zeros_like(small_sum),
        _pack_small(g_emb, b_emb, g_cq, g_ckv, g_out_a, g_out_b, g_out_m, g_post, b_post),
        _pack_small(m_g_emb, m_b_emb, m_g_cq, m_g_ckv, m_g_out_a, m_g_out_b, m_g_out_m, m_g_post, m_b_post),
        _pack_small(v_g_emb, v_b_emb, v_g_cq, v_g_ckv, v_g_out_a, v_g_out_b, v_g_out_m, v_g_post, v_b_post),
        SMALL_ROWS, "adamw_small")
    loss = small_sum[2, 384]

    def ordered(big, sm):
        b_in, b_uq, b_ukv, b_mem, b_out = _unpack_rows(big)
        s_gemb, s_bemb, s_gcq, s_gckv, s_ga, s_gb, s_gm, s_gpost, s_bpost = _unpack_small(sm)
        return [s_gemb, s_bemb, b_in, s_gcq, s_gckv, b_uq, b_ukv, b_mem, s_ga, s_gb, s_gm, b_out, s_gpost, s_bpost]

    return (loss, grad_x, *ordered(g_big, g_sm), *ordered(d_big, d_sm), *ordered(m_big, m_sm),
            *ordered(v_big, v_sm))
```

```python
import functools
import math

import jax
import jax.numpy as jnp
from jax import lax
from jax.experimental import pallas as pl
from jax.experimental.pallas import tpu as pltpu

F32 = jnp.float32
BF16 = jnp.bfloat16
MESH = pl.DeviceIdType.MESH
ANY = pl.BlockSpec(memory_space=pl.ANY)

D_MODEL = 1024
A_WIDTH = 1024
MLA_HEADS = 8
MLA_Q_RANK = 256
MLA_KV_RANK = 128
MLA_QK_DIM = 96
MEM_WIDTH = 512
N_MEM = 256
ROPE_THETA = 500000.0
NORM_EPS = 1e-5
NEG_INF = -1e30
DEEPNORM_ALPHA = 2.0 ** 0.25
DILATED = ((64, 1), (256, 4), (1024, 16))

ADAM_LR = 0.001
ADAM_B1 = 0.9
ADAM_B2 = 0.999
ADAM_EPS = 1e-08
ADAM_WD = 0.01
ADAM_STEP = 10

LANES = 128
VMEM_LIMIT = 56 * 1024 * 1024
LOG2E = math.log2(math.e)
LN2 = math.log(2.0)

PROJ_W = 6144
COL_CQ = 4096
COL_BG = 4608
COL_MQ = 5120
COL_MG = 5632

SHARD_COLS = 1512
ROWS_UQ, ROWS_UKV, ROWS_MEM, ROWS_OUT = 48, 32, 256, 512
ROWS_USED = ROWS_UQ + ROWS_UKV + ROWS_MEM + ROWS_OUT
ROWS_REST = 864
HALF_IN = 512
HALF_REST = ROWS_REST // 2
SMALL_ROWS = 8


def _params(sem=None, vmem=VMEM_LIMIT):
    return pltpu.CompilerParams(dimension_semantics=sem, vmem_limit_bytes=vmem)


def _dot(a, b):
    return jnp.dot(a, b, preferred_element_type=F32)


def _dot_nt(a, b):
    return lax.dot_general(a, b, (((1,), (1,)), ((), ())), preferred_element_type=F32)


def _dot_tn(a, b):
    return lax.dot_general(a, b, (((0,), (0,)), ((), ())), preferred_element_type=F32)


def _ln_hat(x):
    mu = jnp.mean(x, axis=-1, keepdims=True)
    xc = x - mu
    var = jnp.mean(xc * xc, axis=-1, keepdims=True)
    rstd = lax.rsqrt(var + NORM_EPS)
    return xc * rstd, rstd


def _ln_bwd_rows(dxh, xh, rstd):
    return rstd * (dxh - jnp.mean(dxh, axis=-1, keepdims=True) - xh * jnp.mean(dxh * xh, axis=-1, keepdims=True))


def _rms_hat(x, width):
    ms = jnp.sum(x * x, axis=-1, keepdims=True) * (1.0 / width)
    r = lax.rsqrt(ms + NORM_EPS)
    return x * r, r


def _rms_bwd(u, xh, r, width):
    return r * (u - xh * (jnp.sum(u * xh, axis=-1, keepdims=True) * (1.0 / width)))


def _colsum(v):
    return jnp.sum(v, axis=0, keepdims=True)


def _rope_tables(pos, consts):
    ang = pos * consts[0:1, :]
    c = jnp.cos(ang)
    s = jnp.sin(ang)
    return c, s * consts[2:3, :], -s * consts[1:2, :]


def _rope(x, tables, half, inverse=False):
    c, s_up, s_dn = tables
    if inverse:
        s_up, s_dn = -s_up, -s_dn
    return x * c + pltpu.roll(x, half, 1) * s_up + pltpu.roll(x, LANES - half, 1) * s_dn


def _ln_fwd(x, g, b, tm=512):
    t, d = x.shape

    def body(x_ref, g_ref, b_ref, h_ref):
        xh, _ = _ln_hat(x_ref[...])
        h_ref[...] = (xh * g_ref[...] + b_ref[...]).astype(BF16)

    row = pl.BlockSpec((1, d), lambda i: (0, 0))
    return pl.pallas_call(
        body, name="ln_fwd", grid=(t // tm,),
        out_shape=jax.ShapeDtypeStruct((t, d), BF16),
        in_specs=[pl.BlockSpec((tm, d), lambda i: (i, 0)), row, row],
        out_specs=pl.BlockSpec((tm, d), lambda i: (i, 0)),
        compiler_params=_params(("parallel",)),
    )(x, g, b)


def _mm(a, b, out_dtype, tm, tn, tk, name, mode="nn"):
    if mode == "tn":
        k, m = a.shape
    else:
        m, k = a.shape
    n = b.shape[0] if mode == "nt" else b.shape[1]
    nk = k // tk

    def body(a_ref, b_ref, o_ref, acc_ref):
        av = a_ref[...].astype(BF16)
        bv = b_ref[...].astype(BF16)
        part = _dot_tn(av, bv) if mode == "tn" else _dot_nt(av, bv) if mode == "nt" else _dot(av, bv)
        if nk == 1:
            o_ref[...] = part.astype(out_dtype)
        else:
            kk = pl.program_id(2)

            @pl.when(kk == 0)
            def _():
                acc_ref[...] = part

            @pl.when(kk > 0)
            def _():
                acc_ref[...] += part

            @pl.when(kk == nk - 1)
            def _():
                o_ref[...] = acc_ref[...].astype(out_dtype)

    a_spec = (pl.BlockSpec((tk, tm), lambda j, i, kk: (kk, i)) if mode == "tn"
              else pl.BlockSpec((tm, tk), lambda j, i, kk: (i, kk)))
    b_spec = (pl.BlockSpec((tn, tk), lambda j, i, kk: (j, kk)) if mode == "nt"
              else pl.BlockSpec((tk, tn), lambda j, i, kk: (kk, j)))
    return pl.pallas_call(
        body, name=name, grid=(n // tn, m // tm, nk),
        out_shape=jax.ShapeDtypeStruct((m, n), out_dtype),
        in_specs=[a_spec, b_spec],
        out_specs=pl.BlockSpec((tm, tn), lambda j, i, kk: (i, j)),
        scratch_shapes=[pltpu.VMEM((tm, tn), F32)],
        compiler_params=_params(("parallel", "parallel", "arbitrary")),
    )(a, b)


def _prep(proj, pos, w_uq, w_ukv, g_cq, g_ckv, rope_a, rope_b, scales, tm=256):
    t = proj.shape[0]
    sc_a, sc_b, sc_m = (s * LOG2E for s in scales)

    def body(aq_ref, ak_ref, av_ref, bs_ref, mq_ref, pos_ref, wuq_ref, wukv_ref, gcq_ref, gckv_ref,
             ra_ref, rb_ref, qa_ref, ka_ref, va_ref, qb_ref, kb_ref, vb_ref, qm_ref, cqn_ref, ckvn_ref):
        pos_c = pos_ref[...]
        ta = _rope_tables(pos_c, ra_ref[...])
        tb = _rope_tables(pos_c, rb_ref[...])
        for j in range(A_WIDTH // LANES):
            sl = slice(j * LANES, (j + 1) * LANES)
            qa_ref[:, sl] = (_rope(aq_ref[:, sl], ta, 8) * sc_a).astype(BF16)
            ka_ref[:, sl] = _rope(ak_ref[:, sl], ta, 8).astype(BF16)
        va_ref[...] = av_ref[...].astype(BF16)
        qm_ref[...] = (mq_ref[...] * sc_m).astype(BF16)

        cq_hat, _ = _rms_hat(bs_ref[:, 0:MLA_Q_RANK], MLA_Q_RANK)
        cqn = (cq_hat * gcq_ref[...]).astype(BF16)
        cqn_ref[...] = cqn
        ckv_hat, _ = _rms_hat(bs_ref[:, MLA_Q_RANK:MLA_Q_RANK + MLA_KV_RANK], MLA_KV_RANK)
        ckvn = (ckv_hat * gckv_ref[...]).astype(BF16)
        ckvn_ref[...] = ckvn
        qfull = _dot(cqn, wuq_ref[...])
        kv = _dot(ckvn, wukv_ref[...])
        kr = _rope(bs_ref[:, 384:512], tb, 16)
        lane = lax.broadcasted_iota(jnp.int32, (1, LANES), 1)
        low = lane < 64
        for h in range(MLA_HEADS):
            sl = slice(h * LANES, (h + 1) * LANES)
            qb_ref[:, sl] = (_rope(qfull[:, sl], tb, 16) * sc_b).astype(BF16)
            kb_ref[:, sl] = jnp.where(low, kv[:, sl], kr).astype(BF16)
            vb_ref[:, sl] = jnp.where(low, 0.0, kv[:, sl]).astype(BF16)

    def col(width, idx):
        return pl.BlockSpec((tm, width), lambda i: (i, idx))

    def full(shape):
        return pl.BlockSpec(shape, lambda i: (0, 0))

    wide = jax.ShapeDtypeStruct((t, 1024), BF16)
    return pl.pallas_call(
        body, name="prep", grid=(t // tm,),
        out_shape=(wide, wide, wide, wide, wide, wide,
                   jax.ShapeDtypeStruct((t, MEM_WIDTH), BF16),
                   jax.ShapeDtypeStruct((t, MLA_Q_RANK), BF16),
                   jax.ShapeDtypeStruct((t, MLA_KV_RANK), BF16)),
        in_specs=[col(1024, 0), col(1024, 1), col(1024, 2), col(512, COL_CQ // 512), col(512, COL_MQ // 512),
                  pl.BlockSpec((tm, 1), lambda i: (i, 0)),
                  full((MLA_Q_RANK, 1024)), full((MLA_KV_RANK, 1024)),
                  full((1, MLA_Q_RANK)), full((1, MLA_KV_RANK)), full((8, LANES)), full((8, LANES))],
        out_specs=(col(1024, 0),) * 6 + (col(MEM_WIDTH, 0), col(MLA_Q_RANK, 0), col(MLA_KV_RANK, 0)),
        compiler_params=_params(("parallel",)),
    )(proj, proj, proj, proj, proj, pos, w_uq, w_ukv, g_cq, g_ckv, rope_a, rope_b)


def _bias_spec(bias, bq, sk, nq, order):
    shape = (pl.Element(bq), pl.Element(sk))
    if order == "big":
        return pl.BlockSpec(shape, lambda b, i, g: (0, (nq - 1 - i) * bq))
    return pl.BlockSpec(shape, lambda b, g, i: (0, (nq - 1 - i) * bq))


def _attn_fwd(q, k, v, bias, *, nb, s, sk, groups, gpb, hp, qoff, koff, voff, bq, name):
    nq = s // bq
    hw = LANES // hp
    width = gpb * LANES

    def body(*refs):
        if bias is None:
            q_ref, k_ref, v_ref, o_ref, lse_ref = refs
        else:
            q_ref, k_ref, v_ref, bias_ref, o_ref, lse_ref = refs
        lane = lax.broadcasted_iota(jnp.int32, (1, LANES), 1)
        for gi in range(gpb):
            sl = slice(gi * LANES, (gi + 1) * LANES)
            qf = q_ref[:, sl]
            kk = k_ref[:, sl]
            vv = v_ref[:, sl]
            o_all = None
            lse_all = None
            for h in range(hp):
                mask = (lane >= h * hw) & (lane < (h + 1) * hw)
                qh = jnp.where(mask, qf, jnp.zeros_like(qf)) if hp > 1 else qf
                vh = jnp.where(mask, vv, jnp.zeros_like(vv)) if hp > 1 else vv
                sc = _dot_nt(qh, kk)
                if bias is not None:
                    sc = sc + bias_ref[...]
                m = jnp.max(sc, axis=1, keepdims=True)
                p = jnp.exp2(sc - m)
                l = jnp.sum(p, axis=1, keepdims=True)
                o = _dot(p.astype(BF16), vh) / l
                lse = jnp.broadcast_to(m + jnp.log(l) * LOG2E, (bq, LANES))
                o_all = o if h == 0 else o_all + o
                lse_all = lse if h == 0 else jnp.where(mask, lse, lse_all)
            o_ref[:, sl] = o_all
            lse_ref[:, sl] = lse_all

    in_specs = [pl.BlockSpec((bq, width), lambda b, i, g: (b * nq + i, qoff + g)),
                pl.BlockSpec((sk, width), lambda b, i, g: (b, koff + g)),
                pl.BlockSpec((sk, width), lambda b, i, g: (b, voff + g))]
    args = [q, k, v]
    if bias is not None:
        in_specs.append(_bias_spec(bias, bq, sk, nq, "big"))
        args.append(bias)
    out = jax.ShapeDtypeStruct((nb * s, groups * LANES), F32)
    ospec = pl.BlockSpec((bq, width), lambda b, i, g: (b * nq + i, g))
    return pl.pallas_call(
        body, name=name, grid=(nb, nq, groups // gpb),
        out_shape=(out, out), in_specs=in_specs, out_specs=(ospec, ospec),
        compiler_params=_params(("parallel", "parallel", "parallel")),
    )(*args)


def _attn_bwd(q, k, v, o, do, lse, bias, *, nb, s, sk, groups, gpb, hp, scale, qoff, koff, voff, bq, name):
    nq = s // bq
    hw = LANES // hp
    width = gpb * LANES

    def body(*refs):
        if bias is None:
            q_ref, k_ref, v_ref, o_ref, do_ref, lse_ref, dq_ref, dk_ref, dv_ref = refs
        else:
            q_ref, k_ref, v_ref, o_ref, do_ref, lse_ref, bias_ref, dq_ref, dk_ref, dv_ref = refs
        i = pl.program_id(2)

        @pl.when(i == 0)
        def _():
            dk_ref[...] = jnp.zeros_like(dk_ref)
            dv_ref[...] = jnp.zeros_like(dv_ref)

        lane = lax.broadcasted_iota(jnp.int32, (1, LANES), 1)
        for gi in range(gpb):
            sl = slice(gi * LANES, (gi + 1) * LANES)
            qf = q_ref[:, sl]
            kk = k_ref[:, sl]
            vv = v_ref[:, sl]
            dof = do_ref[:, sl]
            prod = dof.astype(F32) * o_ref[:, sl]
            lse = lse_ref[:, sl]
            dq_all = None
            for h in range(hp):
                mask = (lane >= h * hw) & (lane < (h + 1) * hw)
                if hp > 1:
                    qh = jnp.where(mask, qf, jnp.zeros_like(qf))
                    doh = jnp.where(mask, dof, jnp.zeros_like(dof))
                    delta = jnp.sum(jnp.where(mask, prod, 0.0), axis=1, keepdims=True)
                else:
                    qh, doh = qf, dof
                    delta = jnp.sum(prod, axis=1, keepdims=True)
                sc = _dot_nt(qh, kk)
                if bias is not None:
                    sc = sc + bias_ref[...]
                p = jnp.exp2(sc - lse[:, h * hw:h * hw + 1])
                dp = _dot_nt(doh, vv)
                ds = (p * (dp - delta)).astype(BF16)
                dq = _dot(ds, kk) * scale
                dq_all = jnp.where(mask, dq, 0.0 if h == 0 else dq_all) if hp > 1 else dq
                dk_ref[:, sl] += _dot_tn(ds, qh)
                dv_ref[:, sl] += _dot_tn(p.astype(BF16), doh)
            dq_ref[:, sl] = dq_all

        @pl.when(i == nq - 1)
        def _():
            dk_ref[...] = dk_ref[...] * LN2

    in_specs = [pl.BlockSpec((bq, width), lambda b, g, i: (b * nq + i, qoff + g)),
                pl.BlockSpec((sk, width), lambda b, g, i: (b, koff + g)),
                pl.BlockSpec((sk, width), lambda b, g, i: (b, voff + g)),
                pl.BlockSpec((bq, width), lambda b, g, i: (b * nq + i, g)),
                pl.BlockSpec((bq, width), lambda b, g, i: (b * nq + i, g)),
                pl.BlockSpec((bq, width), lambda b, g, i: (b * nq + i, g))]
    args = [q, k, v, o, do, lse]
    if bias is not None:
        in_specs.append(_bias_spec(bias, bq, sk, nq, "small"))
        args.append(bias)
    dq_shape = jax.ShapeDtypeStruct((nb * s, groups * LANES), F32)
    dkv_shape = jax.ShapeDtypeStruct((nb * sk, groups * LANES), F32)
    kv_spec = pl.BlockSpec((sk, width), lambda b, g, i: (b, g))
    return pl.pallas_call(
        body, name=name, grid=(nb, groups // gpb, nq),
        out_shape=(dq_shape, dkv_shape, dkv_shape), in_specs=in_specs,
        out_specs=(pl.BlockSpec((bq, width), lambda b, g, i: (b * nq + i, g)), kv_spec, kv_spec),
        compiler_params=_params(("parallel", "parallel", "arbitrary")),
    )(*args)


def _post(x, ya, ybp, ym, proj, target, w_out, g_emb, b_emb, g_a, g_b, g_m, g_post, b_post, tm=256):
    t = x.shape[0]

    def body(x_ref, ya_ref, yb_ref, ym_ref, ga_ref, gb_ref, gm_ref, tg_ref, wo_ref,
             ge_ref, be_ref, goa_ref, gob_ref, gom_ref, gp_ref, bp_ref,
             y_ref, dz_ref, doa_ref, dob_ref, dom_ref, dga_ref, dgb_ref, dgm_ref,
             loss_ref, dgp_ref, dbp_ref, dgoa_ref, dgob_ref, dgom_ref):
        i = pl.program_id(0)

        @pl.when(i == 0)
        def _():
            for r in (loss_ref, dgp_ref, dbp_ref, dgoa_ref, dgob_ref, dgom_ref):
                r[...] = jnp.zeros_like(r)

        lane = lax.broadcasted_iota(jnp.int32, (1, LANES), 1)
        low = lane < 64
        xh0, _ = _ln_hat(x_ref[...])
        h = xh0 * ge_ref[...] + be_ref[...]

        ybp_v = yb_ref[...]
        yb = jnp.concatenate(
            [jnp.where(low, pltpu.roll(ybp_v[:, 2 * j * LANES:(2 * j + 1) * LANES], 64, 1),
                       ybp_v[:, (2 * j + 1) * LANES:(2 * j + 2) * LANES]) for j in range(4)], axis=1)

        def gated(raw, gate, gain, width):
            xh, r = _rms_hat(raw, width)
            n = xh * gain
            sg = 1.0 / (1.0 + jnp.exp(-gate))
            return xh, r, n, sg, n * (gate * sg)

        gate_a, gate_b, gate_m = ga_ref[...], gb_ref[...], gm_ref[...]
        xh_a, r_a, n_a, sg_a, y_a = gated(ya_ref[...], gate_a, goa_ref[...], A_WIDTH)
        xh_b, r_b, n_b, sg_b, y_b = gated(yb, gate_b, gob_ref[...], 512)
        xh_m, r_m, n_m, sg_m, y_m = gated(ym_ref[...], gate_m, gom_ref[...], 512)
        y = jnp.concatenate([y_a, y_b, y_m], axis=1).astype(BF16)
        y_ref[...] = y
        z = DEEPNORM_ALPHA * h + _dot(y, wo_ref[...])
        zh, rstd = _ln_hat(z)
        err = zh * gp_ref[...] + bp_ref[...] - tg_ref[...]
        rows = jnp.sum(err * err, axis=1, keepdims=True)
        loss_ref[...] += jnp.broadcast_to(jnp.sum(rows, axis=0, keepdims=True) * (0.5 / D_MODEL), (1, LANES))
        dout = err * (1.0 / D_MODEL)
        dgp_ref[...] += _colsum(dout * zh)
        dbp_ref[...] += _colsum(dout)
        dz = _ln_bwd_rows(dout * gp_ref[...], zh, rstd)
        dz_ref[...] = dz
        dy = _dot_nt(dz.astype(BF16), wo_ref[...])

        def gated_bwd(dyg, xh, r, n, sg, gate, gain, width, dgain_ref):
            dn = dyg * (gate * sg)
            dgate = dyg * n * (sg * (1.0 + gate * (1.0 - sg)))
            dgain_ref[...] += _colsum(dn * xh)
            return _rms_bwd(dn * gain, xh, r, width), dgate

        dya, dgate_a = gated_bwd(dy[:, 0:1024], xh_a, r_a, n_a, sg_a, gate_a, goa_ref[...], A_WIDTH, dgoa_ref)
        dyb, dgate_b = gated_bwd(dy[:, 1024:1536], xh_b, r_b, n_b, sg_b, gate_b, gob_ref[...], 512, dgob_ref)
        dym, dgate_m = gated_bwd(dy[:, 1536:2048], xh_m, r_m, n_m, sg_m, gate_m, gom_ref[...], 512, dgom_ref)
        doa_ref[...] = dya.astype(BF16)
        dom_ref[...] = dym.astype(BF16)
        dga_ref[...] = dgate_a.astype(BF16)
        dgb_ref[...] = dgate_b.astype(BF16)
        dgm_ref[...] = dgate_m.astype(BF16)
        for j in range(4):
            blk = dyb[:, j * LANES:(j + 1) * LANES]
            dob_ref[:, 2 * j * LANES:(2 * j + 1) * LANES] = jnp.where(low, 0.0, pltpu.roll(blk, 64, 1)).astype(BF16)
            dob_ref[:, (2 * j + 1) * LANES:(2 * j + 2) * LANES] = jnp.where(low, 0.0, blk).astype(BF16)

    def col(width, idx):
        return pl.BlockSpec((tm, width), lambda i: (i, idx))

    def full(shape):
        return pl.BlockSpec(shape, lambda i: (0, 0))

    def acc(width):
        return jax.ShapeDtypeStruct((1, width), F32)

    return pl.pallas_call(
        body, name="post", grid=(t // tm,),
        out_shape=(jax.ShapeDtypeStruct((t, 2048), BF16), jax.ShapeDtypeStruct((t, 1024), F32),
                   jax.ShapeDtypeStruct((t, 1024), BF16), jax.ShapeDtypeStruct((t, 1024), BF16),
                   jax.ShapeDtypeStruct((t, 512), BF16),
                   jax.ShapeDtypeStruct((t, 1024), BF16), jax.ShapeDtypeStruct((t, 512), BF16),
                   jax.ShapeDtypeStruct((t, 512), BF16),
                   acc(LANES), acc(1024), acc(1024), acc(1024), acc(512), acc(512)),
        in_specs=[col(1024, 0), col(1024, 0), col(1024, 0), col(512, 0),
                  col(1024, 3), col(512, COL_BG // 512), col(512, COL_MG // 512), col(1024, 0),
                  full((2048, 1024)),
                  full((1, 1024)), full((1, 1024)), full((1, 1024)), full((1, 512)), full((1, 512)),
                  full((1, 1024)), full((1, 1024))],
        out_specs=(col(2048, 0), col(1024, 0), col(1024, 0), col(1024, 0), col(512, 0),
                   col(1024, 0), col(512, 0), col(512, 0),
                   full((1, LANES)), full((1, 1024)), full((1, 1024)), full((1, 1024)), full((1, 512)),
                   full((1, 512))),
        compiler_params=_params(("arbitrary",)),
    )(x, ya, ybp, ym, proj, proj, proj, target, w_out, g_emb, b_emb, g_a, g_b, g_m, g_post, b_post)


def _prep_bwd(dqa, dka, dva, dqb, dkb, dvb, dqm, dga, dgb, dgm, proj, pos, w_uq, w_ukv, g_cq, g_ckv,
              rope_a, rope_b, tm=256):
    t = proj.shape[0]

    def body(dqa_ref, dka_ref, dva_ref, dqb_ref, dkb_ref, dvb_ref, dqm_ref, dga_ref, dgb_ref, dgm_ref,
             bs_ref, pos_ref, wuq_ref, wukv_ref, gcq_ref, gckv_ref, ra_ref, rb_ref,
             dproj_ref, dqf_ref, dkv_ref, dgcq_ref, dgckv_ref):
        i = pl.program_id(0)

        @pl.when(i == 0)
        def _():
            dgcq_ref[...] = jnp.zeros_like(dgcq_ref)
            dgckv_ref[...] = jnp.zeros_like(dgckv_ref)

        pos_c = pos_ref[...]
        ta = _rope_tables(pos_c, ra_ref[...])
        tb = _rope_tables(pos_c, rb_ref[...])
        for j in range(A_WIDTH // LANES):
            sl = slice(j * LANES, (j + 1) * LANES)
            dproj_ref[:, j * LANES:(j + 1) * LANES] = _rope(dqa_ref[:, sl], ta, 8, inverse=True).astype(BF16)
            dproj_ref[:, 1024 + j * LANES:1024 + (j + 1) * LANES] = (
                _rope(dka_ref[:, sl], ta, 8, inverse=True).astype(BF16))
        dproj_ref[:, 2048:3072] = dva_ref[...].astype(BF16)
        dproj_ref[:, 3072:4096] = dga_ref[...]

        lane = lax.broadcasted_iota(jnp.int32, (1, LANES), 1)
        low = lane < 64
        rope_lanes = (lane >= 64) & (lane < 96)
        dkr = jnp.zeros((tm, LANES), F32)
        for h in range(MLA_HEADS):
            sl = slice(h * LANES, (h + 1) * LANES)
            dqf_ref[:, sl] = _rope(dqb_ref[:, sl], tb, 16, inverse=True).astype(BF16)
            dk_h = dkb_ref[:, sl]
            dkv_ref[:, sl] = jnp.where(low, dk_h, dvb_ref[:, sl]).astype(BF16)
            dkr = dkr + jnp.where(rope_lanes, dk_h, 0.0)
        dkr = _rope(dkr, tb, 16, inverse=True)

        cq_hat, r_q = _rms_hat(bs_ref[:, 0:MLA_Q_RANK], MLA_Q_RANK)
        dcqn = _dot_nt(dqf_ref[...], wuq_ref[...])
        dgcq_ref[...] += _colsum(dcqn * cq_hat)
        dproj_ref[:, COL_CQ:COL_CQ + 256] = _rms_bwd(dcqn * gcq_ref[...], cq_hat, r_q, MLA_Q_RANK).astype(BF16)
        ckv_hat, r_kv = _rms_hat(bs_ref[:, MLA_Q_RANK:MLA_Q_RANK + MLA_KV_RANK], MLA_KV_RANK)
        dckvn = _dot_nt(dkv_ref[...], wukv_ref[...])
        dgckv_ref[...] += _colsum(dckvn * ckv_hat)
        dproj_ref[:, COL_CQ + 256:COL_CQ + 384] = (
            _rms_bwd(dckvn * gckv_ref[...], ckv_hat, r_kv, MLA_KV_RANK).astype(BF16))
        dproj_ref[:, COL_CQ + 384:COL_CQ + 512] = dkr.astype(BF16)
        dproj_ref[:, COL_BG:COL_BG + 512] = dgb_ref[...]
        dproj_ref[:, COL_MQ:COL_MQ + 512] = dqm_ref[...].astype(BF16)
        dproj_ref[:, COL_MG:COL_MG + 512] = dgm_ref[...]

    def col(width, idx):
        return pl.BlockSpec((tm, width), lambda i: (i, idx))

    def full(shape):
        return pl.BlockSpec(shape, lambda i: (0, 0))

    return pl.pallas_call(
        body, name="prep_bwd", grid=(t // tm,),
        out_shape=(jax.ShapeDtypeStruct((t, PROJ_W), BF16), jax.ShapeDtypeStruct((t, 1024), BF16),
                   jax.ShapeDtypeStruct((t, 1024), BF16),
                   jax.ShapeDtypeStruct((1, MLA_Q_RANK), F32), jax.ShapeDtypeStruct((1, MLA_KV_RANK), F32)),
        in_specs=[col(1024, 0)] * 6 + [col(512, 0), col(1024, 0), col(512, 0), col(512, 0),
                  col(512, COL_CQ // 512), pl.BlockSpec((tm, 1), lambda i: (i, 0)),
                  full((MLA_Q_RANK, 1024)), full((MLA_KV_RANK, 1024)),
                  full((1, MLA_Q_RANK)), full((1, MLA_KV_RANK)), full((8, LANES)), full((8, LANES))],
        out_specs=(col(PROJ_W, 0), col(1024, 0), col(1024, 0), full((1, MLA_Q_RANK)), full((1, MLA_KV_RANK))),
        compiler_params=_params(("arbitrary",)),
    )(dqa, dka, dva, dqb, dkb, dvb, dqm, dga, dgb, dgm, proj, pos, w_uq, w_ukv, g_cq, g_ckv, rope_a, rope_b)


def _adamw(g, w, m, v, tr, name):
    r, cols = w.shape

    def body(g_ref, w_ref, m_ref, v_ref, d_ref, nm_ref, nv_ref):
        gv = g_ref[...]
        m_new = ADAM_B1 * m_ref[...] + (1.0 - ADAM_B1) * gv
        v_new = ADAM_B2 * v_ref[...] + (1.0 - ADAM_B2) * (gv * gv)
        m_hat = m_new / (1.0 - ADAM_B1 ** ADAM_STEP)
        v_hat = v_new / (1.0 - ADAM_B2 ** ADAM_STEP)
        d_ref[...] = -ADAM_LR * (m_hat / (jnp.sqrt(v_hat) + ADAM_EPS) + ADAM_WD * w_ref[...])
        nm_ref[...] = m_new
        nv_ref[...] = v_new

    tile = pl.BlockSpec((tr, cols), lambda i: (i, 0))
    shape = jax.ShapeDtypeStruct((r, cols), F32)
    return pl.pallas_call(
        body, name=name, grid=(r // tr,),
        out_shape=(shape,) * 3, in_specs=[tile] * 4, out_specs=(tile,) * 3,
        compiler_params=_params(("parallel",)),
    )(g, w, m, v)


def _core_sum(g, recv, core, rows, tr, name):
    cols = g.shape[2]
    nblk = rows // tr

    def body(c_ref, g_ref, r_ref, sf_ref, sb_ref):
        tot = g_ref[...] + r_ref[...]
        sf_ref[...] = tot
        sb_ref[...] = tot.astype(BF16)

    half = pl.BlockSpec((None, tr, cols), lambda j, i, c_ref: (j, i, 0))
    return pl.pallas_call(
        body, name=name,
        grid_spec=pltpu.PrefetchScalarGridSpec(
            num_scalar_prefetch=1, grid=(4, nblk),
            in_specs=[pl.BlockSpec((None, tr, cols), lambda j, i, c_ref: (j, c_ref[0] * nblk + i, 0)), half],
            out_specs=(half, half)),
        out_shape=(jax.ShapeDtypeStruct((4, rows, cols), F32), jax.ShapeDtypeStruct((4, rows, cols), BF16)),
        compiler_params=_params(("parallel", "parallel")),
    )(core, g, recv)


def _chip_sum(sf, recv, chip, rows, tr, name):
    cols = sf.shape[2]

    def body(me_ref, sf_ref, r_ref, out_ref):
        acc = sf_ref[...]
        for k in range(3):
            acc = acc + r_ref[k].astype(F32)
        out_ref[...] = acc

    return pl.pallas_call(
        body, name=name,
        grid_spec=pltpu.PrefetchScalarGridSpec(
            num_scalar_prefetch=1, grid=(rows // tr,),
            in_specs=[pl.BlockSpec((None, tr, cols), lambda i, me_ref: (me_ref[0], i, 0)),
                      pl.BlockSpec((3, tr, cols), lambda i, me_ref: (0, i, 0))],
            out_specs=pl.BlockSpec((tr, cols), lambda i, me_ref: (i, 0))),
        out_shape=jax.ShapeDtypeStruct((rows, cols), F32),
        compiler_params=_params(("parallel",)),
    )(chip, sf, recv)


def _position():
    return lax.axis_index("x"), lax.axis_index("y"), lax.axis_index("c")


def _gather_weights(w_in_b, rest_b):
    halves = (HALF_IN, HALF_REST)

    def body(in_ref, rest_ref, oin_ref, orest_ref, send_sems, recv_sems, local_sems):
        x, y, c = _position()
        me = 2 * x + y
        srcs = (in_ref, rest_ref)
        dsts = (oin_ref, orest_ref)
        local = [pltpu.make_async_copy(srcs[a], dsts[a].at[me], local_sems.at[a]) for a in range(2)]
        for cp in local:
            cp.start()

        def piece(a, chip, half):
            return dsts[a].at[chip, pl.ds(half * halves[a], halves[a]), :]

        def from_chip(a, k):
            return pltpu.make_async_remote_copy(
                src_ref=srcs[a].at[pl.ds(c * halves[a], halves[a]), :], dst_ref=piece(a, me, c),
                send_sem=send_sems.at[3 * a + k - 1], recv_sem=recv_sems.at[3 * a + k - 1],
                device_id=(x ^ (k >> 1), y ^ (k & 1), c), device_id_type=MESH)

        def arrived(a, k):
            return pltpu.make_async_remote_copy(
                src_ref=piece(a, me ^ k, c), dst_ref=piece(a, me ^ k, c),
                send_sem=send_sems.at[3 * a + k - 1], recv_sem=recv_sems.at[3 * a + k - 1],
                device_id=(x ^ (k >> 1), y ^ (k & 1), c), device_id_type=MESH)

        def to_sibling(a, k, half):
            return pltpu.make_async_remote_copy(
                src_ref=piece(a, me ^ k, half), dst_ref=piece(a, me ^ k, half),
                send_sem=send_sems.at[6 + 3 * a + k - 1], recv_sem=recv_sems.at[6 + 3 * a + k - 1],
                device_id=(x, y, 1 - c), device_id_type=MESH)

        pairs = [(a, k) for a in range(2) for k in (1, 2, 3)]
        sends = [from_chip(a, k) for a, k in pairs]
        for cp in sends:
            cp.start()
        passed = []
        for a, k in pairs:
            arrived(a, k).wait_recv()
            cp = to_sibling(a, k, c)
            cp.start()
            passed.append(cp)
        for a, k in pairs:
            to_sibling(a, k, 1 - c).wait_recv()
        for cp in sends + passed:
            cp.wait_send()
        for cp in local:
            cp.wait()

    return pl.pallas_call(
        body, name="gather_weights",
        out_shape=(jax.ShapeDtypeStruct((4,) + w_in_b.shape, BF16), jax.ShapeDtypeStruct((4,) + rest_b.shape, BF16)),
        in_specs=[ANY, ANY], out_specs=(ANY, ANY),
        scratch_shapes=[pltpu.SemaphoreType.DMA((12,)), pltpu.SemaphoreType.DMA((12,)), pltpu.SemaphoreType.DMA((2,))],
    )(w_in_b, rest_b)


def _send_other_half(g_in, g_rest):
    halves = (HALF_IN, HALF_REST)

    def body(gin_ref, grest_ref, rin_ref, rrest_ref, send_sems, recv_sems):
        x, y, c = _position()
        srcs = (gin_ref, grest_ref)
        dsts = (rin_ref, rrest_ref)
        copies = [pltpu.make_async_remote_copy(
            src_ref=srcs[a].at[:, pl.ds((1 - c) * halves[a], halves[a]), :], dst_ref=dsts[a],
            send_sem=send_sems.at[a], recv_sem=recv_sems.at[a], device_id=(x, y, 1 - c), device_id_type=MESH)
            for a in range(2)]
        for cp in copies:
            cp.start()
        for cp in copies:
            cp.wait_recv()
        for cp in copies:
            cp.wait_send()

    return pl.pallas_call(
        body, name="send_other_half",
        out_shape=(jax.ShapeDtypeStruct((4, HALF_IN, SHARD_COLS), F32),
                   jax.ShapeDtypeStruct((4, HALF_REST, 1024), F32)),
        in_specs=[ANY, ANY], out_specs=(ANY, ANY),
        scratch_shapes=[pltpu.SemaphoreType.DMA((2,)), pltpu.SemaphoreType.DMA((2,))],
    )(g_in, g_rest)


def _dh_scatter(dproj, w_in_arr, x, dz, g, sb_in, sb_rest, tm=1024, tk=1024):
    t, d = x.shape
    nk = dproj.shape[1] // tk
    ni = t // tm

    def body(dp_ref, w_ref, x_ref, dz_ref, g_ref, sbin_ref, sbrest_ref,
             dx_ref, dg_ref, db_ref, rin_ref, rrest_ref, acc_ref, send_sems, recv_sems):
        i = pl.program_id(0)
        kk = pl.program_id(1)
        px, py, pc = _position()
        me = 2 * px + py
        srcs = (sbin_ref, sbrest_ref)
        dsts = (rin_ref, rrest_ref)

        def copy(a, k):
            return pltpu.make_async_remote_copy(
                src_ref=srcs[a].at[me ^ k], dst_ref=dsts[a].at[k - 1],
                send_sem=send_sems.at[3 * a + k - 1], recv_sem=recv_sems.at[3 * a + k - 1],
                device_id=(px ^ (k >> 1), py ^ (k & 1), pc), device_id_type=MESH)

        pairs = [(a, k) for a in range(2) for k in (1, 2, 3)]

        @pl.when((i == 0) & (kk == 0))
        def _():
            dg_ref[...] = jnp.zeros_like(dg_ref)
            db_ref[...] = jnp.zeros_like(db_ref)
            for a, k in pairs:
                copy(a, k).start()

        part = _dot_nt(dp_ref[...], w_ref[...])

        @pl.when(kk == 0)
        def _():
            acc_ref[...] = part

        @pl.when(kk > 0)
        def _():
            acc_ref[...] += part

        @pl.when(kk == nk - 1)
        def _():
            xh, rstd = _ln_hat(x_ref[...])
            dht = acc_ref[...] + DEEPNORM_ALPHA * dz_ref[...]
            dg_ref[...] += _colsum(dht * xh)
            db_ref[...] += _colsum(dht)
            dx_ref[...] = _ln_bwd_rows(dht * g_ref[...], xh, rstd)

        @pl.when((i == ni - 1) & (kk == nk - 1))
        def _():
            for a, k in pairs:
                copy(a, k).wait_recv()
            for a, k in pairs:
                copy(a, k).wait_send()

    tile = pl.BlockSpec((tm, d), lambda i, kk: (i, 0))
    row = pl.BlockSpec((1, d), lambda i, kk: (0, 0))
    return pl.pallas_call(
        body, name="dh_scatter", grid=(ni, nk),
        out_shape=(jax.ShapeDtypeStruct((t, d), F32), jax.ShapeDtypeStruct((1, d), F32),
                   jax.ShapeDtypeStruct((1, d), F32),
                   jax.ShapeDtypeStruct((3, HALF_IN, SHARD_COLS), BF16),
                   jax.ShapeDtypeStruct((3, HALF_REST, 1024), BF16)),
        in_specs=[pl.BlockSpec((tm, tk), lambda i, kk: (i, kk)), pl.BlockSpec((d, tk), lambda i, kk: (0, kk)),
                  tile, tile, row, ANY, ANY],
        out_specs=(tile, row, row, ANY, ANY),
        scratch_shapes=[pltpu.VMEM((tm, d), F32), pltpu.SemaphoreType.DMA((6,)), pltpu.SemaphoreType.DMA((6,))],
        compiler_params=_params(("arbitrary", "arbitrary")),
    )(dproj, w_in_arr, x, dz, g, sb_in, sb_rest)


def _join_halves(gh_in, gh_rest):
    halves = (HALF_IN, HALF_REST)

    def body(hin_ref, hrest_ref, oin_ref, orest_ref, send_sems, recv_sems, local_sems):
        x, y, c = _position()
        srcs = (hin_ref, hrest_ref)
        dsts = (oin_ref, orest_ref)

        def rows(a, half):
            return dsts[a].at[pl.ds(half * halves[a], halves[a]), :]

        local = [pltpu.make_async_copy(srcs[a], rows(a, c), local_sems.at[a]) for a in range(2)]
        remote = [pltpu.make_async_remote_copy(
            src_ref=srcs[a], dst_ref=rows(a, c), send_sem=send_sems.at[a], recv_sem=recv_sems.at[a],
            device_id=(x, y, 1 - c), device_id_type=MESH) for a in range(2)]
        for cp in local + remote:
            cp.start()
        for a in range(2):
            pltpu.make_async_remote_copy(
                src_ref=srcs[a], dst_ref=rows(a, 1 - c), send_sem=send_sems.at[a], recv_sem=recv_sems.at[a],
                device_id=(x, y, 1 - c), device_id_type=MESH).wait_recv()
        for cp in remote:
            cp.wait_send()
        for cp in local:
            cp.wait()

    return pl.pallas_call(
        body, name="join_halves",
        out_shape=(jax.ShapeDtypeStruct((2 * HALF_IN, SHARD_COLS), F32),
                   jax.ShapeDtypeStruct((ROWS_REST, 1024), F32)),
        in_specs=[ANY, ANY], out_specs=(ANY, ANY),
        scratch_shapes=[pltpu.SemaphoreType.DMA((2,)), pltpu.SemaphoreType.DMA((2,)), pltpu.SemaphoreType.DMA((2,))],
    )(gh_in, gh_rest)


def _allreduce_small(vec):
    def body(vec_ref, out_ref, all_ref, send_sems, recv_sems):
        x, y, c = _position()
        me = 4 * x + 2 * y + c
        all_ref[me] = vec_ref[...]

        def copy(k, slot):
            return pltpu.make_async_remote_copy(
                src_ref=vec_ref, dst_ref=all_ref.at[slot], send_sem=send_sems.at[k - 1], recv_sem=recv_sems.at[k - 1],
                device_id=(x ^ (k >> 2), y ^ ((k >> 1) & 1), c ^ (k & 1)), device_id_type=MESH)

        copies = [copy(k, me) for k in range(1, 8)]
        for cp in copies:
            cp.start()
        for k in range(1, 8):
            copy(k, me ^ k).wait_recv()
        for cp in copies:
            cp.wait_send()
        total = all_ref[0]
        for d in range(1, 8):
            total = total + all_ref[d]
        out_ref[...] = total

    return pl.pallas_call(
        body, name="allreduce_small",
        out_shape=jax.ShapeDtypeStruct(vec.shape, vec.dtype),
        in_specs=[pl.BlockSpec(memory_space=pltpu.VMEM)], out_specs=pl.BlockSpec(memory_space=pltpu.VMEM),
        scratch_shapes=[pltpu.VMEM((8,) + vec.shape, vec.dtype), pltpu.SemaphoreType.DMA((7,)),
                        pltpu.SemaphoreType.DMA((7,))],
    )(vec)


def _pack_rest(w_uq, w_ukv, w_mem, w_out):
    rows = jnp.concatenate([w_uq.reshape(-1, 1024), w_ukv.reshape(-1, 1024), w_mem.reshape(-1, 1024),
                            w_out.reshape(-1, 1024)], axis=0)
    return jnp.pad(rows, ((0, ROWS_REST - ROWS_USED), (0, 0)))


def _unpack_rest(p):
    o = 0
    out = []
    for rows, shape in ((ROWS_UQ, (1, 256, 192)), (ROWS_UKV, (1, 128, 256)), (ROWS_MEM, (1, 256, 1024)),
                        (ROWS_OUT, (1, 512, 1024))):
        out.append(p[o:o + rows].reshape(shape))
        o += rows
    return out


def _full_weights(g_in, g_rest):
    def cols(lo, rows, shape):
        return jnp.concatenate([g_rest[j, lo:lo + rows].reshape(shape) for j in range(4)], axis=1)

    def rows_of(lo, rows):
        return g_rest[:, lo:lo + rows].reshape(4 * rows, 1024)

    z = functools.partial(jnp.zeros, dtype=g_in.dtype)
    pieces = [g_in[0], g_in[1], g_in[2][:, :4480 - 2 * SHARD_COLS], z((1024, 64)),
              g_in[2][:, 4480 - 2 * SHARD_COLS:4512 - 2 * SHARD_COLS], z((1024, 32)),
              g_in[2][:, 4512 - 2 * SHARD_COLS:], g_in[3]]
    w_in_arr = jnp.concatenate(pieces, axis=1)
    w_uq = cols(0, ROWS_UQ, (256, 192))
    w_ukv = cols(ROWS_UQ, ROWS_UKV, (128, 256))
    w_mem = rows_of(ROWS_UQ + ROWS_UKV, ROWS_MEM)
    w_out = rows_of(ROWS_UQ + ROWS_UKV + ROWS_MEM, ROWS_OUT)
    w_uq_pad = jnp.pad(w_uq.reshape(256, MLA_HEADS, MLA_QK_DIM), ((0, 0), (0, 0), (0, 32))).reshape(256, 1024)
    return w_in_arr, w_uq_pad, w_ukv, w_mem, w_out


def _split_grads(dw_in_arr, dw_uq_pad, dw_ukv, dw_mem, dw_out):
    dw_in = jnp.concatenate([dw_in_arr[:, :4480], dw_in_arr[:, 4544:4576], dw_in_arr[:, 4608:]], axis=1)
    g_in = dw_in.reshape(1024, 4, SHARD_COLS).transpose(1, 0, 2)
    dw_uq = dw_uq_pad.reshape(256, MLA_HEADS, LANES)[:, :, :MLA_QK_DIM].reshape(256, 768)

    def by_cols(a, width):
        r = a.shape[0]
        return a.reshape(r, 4, width).transpose(1, 0, 2).reshape(4, -1, 1024)

    parts = [by_cols(dw_uq, 192), by_cols(dw_ukv, 256), dw_mem.reshape(4, ROWS_MEM, 1024),
             dw_out.reshape(4, ROWS_OUT, 1024)]
    g_rest = jnp.pad(jnp.concatenate(parts, axis=1), ((0, 0), (0, ROWS_REST - ROWS_USED), (0, 0)))
    return g_in, g_rest


def _rope_consts(rot, first, period):
    half = rot // 2
    inv_freq = ROPE_THETA ** (-(jnp.arange(0, rot, 2, dtype=F32) / rot))
    lane = jnp.arange(LANES) % period - first
    in_rot = (lane >= 0) & (lane < rot)
    freq = jnp.where(in_rot, inv_freq[jnp.clip(lane, 0, rot - 1) % half], 0.0)
    lo = (in_rot & (lane < half)).astype(F32)
    hi = (in_rot & (lane >= half)).astype(F32)
    return jnp.concatenate([freq[None], lo[None], hi[None], jnp.zeros((5, LANES), F32)], axis=0)


def _dilated_bias(s, bq):
    delta = jnp.arange(2 * s - bq)[None, :] - (s - bq) - jnp.arange(bq)[:, None]
    count = jnp.zeros(delta.shape, F32)
    for reach, dil in DILATED:
        count = count + ((jnp.abs(delta) <= reach) & (delta % dil == 0)).astype(F32)
    return jnp.where(count > 0, jnp.log(jnp.maximum(count, 1.0)) * LOG2E, NEG_INF)


def _forward_backward(x, mem, positions, target, weights, gains):
    w_in_arr, w_uq_pad, w_ukv, w_mem, w_out = weights
    g_emb, b_emb, g_cq, g_ckv, g_out_a, g_out_b, g_out_m, g_post, b_post = gains
    nb, s, d = x.shape
    t = nb * s
    x2 = x.reshape(t, d)
    mem2 = mem.reshape(nb * N_MEM, d)
    tgt2 = target.reshape(t, d)
    pos = positions.reshape(t, 1).astype(F32)
    rope_a = _rope_consts(16, 0, 64)
    rope_b = _rope_consts(32, 64, 128)
    bq_a = 256
    bias = _dilated_bias(s, bq_a)
    scales = (0.125, MLA_QK_DIM ** -0.5, 128 ** -0.5)

    h = _ln_fwd(x2, g_emb, b_emb)
    proj = _mm(h, w_in_arr, F32, 1024, 1024, 1024, "in_proj")
    qa, ka, va, qb, kb, vb, qm, cqn, ckvn = _prep(proj, pos, w_uq_pad, w_ukv, g_cq, g_ckv, rope_a, rope_b, scales)
    mkv = _mm(mem2, w_mem, BF16, nb * N_MEM, 1024, 1024, "mem_kv")

    cfg_a = dict(nb=nb, s=s, sk=s, groups=8, gpb=1, hp=2, qoff=0, koff=0, voff=0, bq=bq_a)
    cfg_b = dict(nb=nb, s=s, sk=s, groups=8, gpb=2, hp=1, qoff=0, koff=0, voff=0, bq=256)
    cfg_m = dict(nb=nb, s=s, sk=N_MEM, groups=4, gpb=1, hp=1, qoff=0, koff=0, voff=4, bq=512)
    ya, lse_a = _attn_fwd(qa, ka, va, bias, name="attn_a_fwd", **cfg_a)
    yb, lse_b = _attn_fwd(qb, kb, vb, None, name="attn_b_fwd", **cfg_b)
    ym, lse_m = _attn_fwd(qm, mkv, mkv, None, name="attn_m_fwd", **cfg_m)

    (y, dz, doa, dob, dom, dga, dgb, dgm, loss, dg_post, db_post, dg_a, dg_b, dg_m) = _post(
        x2, ya, yb, ym, proj, tgt2, w_out, g_emb, b_emb, g_out_a, g_out_b, g_out_m, g_post, b_post)

    dqa, dka, dva = _attn_bwd(qa, ka, va, ya, doa, lse_a, bias, name="attn_a_bwd", scale=scales[0], **cfg_a)
    dqb, dkb, dvb = _attn_bwd(qb, kb, vb, yb, dob, lse_b, None, name="attn_b_bwd", scale=scales[1], **cfg_b)
    dqm, dmk, dmv = _attn_bwd(qm, mkv, mkv, ym, dom, lse_m, None, name="attn_m_bwd", scale=scales[2], **cfg_m)
    dmkv = jnp.concatenate([dmk, dmv], axis=1)

    dproj, dqf, dkv, dg_cq, dg_ckv = _prep_bwd(
        dqa, dka, dva, dqb, dkb, dvb, dqm, dga, dgb, dgm, proj, pos, w_uq_pad, w_ukv, g_cq, g_ckv, rope_a, rope_b)

    dw_in_arr = _mm(h, dproj, F32, 1024, 1024, 1024, "dw_in", mode="tn")
    dw_out = _mm(y, dz, F32, 1024, 1024, 1024, "dw_out", mode="tn")
    dw_uq_pad = _mm(cqn, dqf, F32, 256, 1024, 1024, "dw_uq", mode="tn")
    dw_ukv = _mm(ckvn, dkv, F32, 128, 1024, 1024, "dw_ukv", mode="tn")
    dw_mem = _mm(mem2, dmkv, F32, 1024, 1024, nb * N_MEM, "dw_mem", mode="tn")
    small_rows = (dg_cq, dg_ckv, loss, dg_a, dg_b, dg_m, dg_post, db_post)
    return (dw_in_arr, dw_uq_pad, dw_ukv, dw_mem, dw_out), (dproj, x2, dz), small_rows


def _small_block(dg_emb, db_emb, small_rows):
    dg_cq, dg_ckv, loss, dg_a, dg_b, dg_m, dg_post, db_post = small_rows
    row2 = jnp.concatenate([dg_cq, dg_ckv, loss, jnp.zeros((1, 512), F32)], axis=1)
    return jnp.concatenate([dg_emb, db_emb, row2, dg_a, jnp.concatenate([dg_b, dg_m], axis=1), dg_post, db_post,
                            jnp.zeros((1, 1024), F32)], axis=0)


def _pack_small(g_emb, b_emb, g_cq, g_ckv, g_out_a, g_out_b, g_out_m, g_post, b_post):
    row2 = jnp.concatenate([g_cq.reshape(1, -1), g_ckv.reshape(1, -1), jnp.zeros((1, 640), F32)], axis=1)
    return jnp.concatenate([g_emb.reshape(1, -1), b_emb.reshape(1, -1), row2, g_out_a.reshape(1, -1),
                            jnp.concatenate([g_out_b.reshape(1, -1), g_out_m.reshape(1, -1)], axis=1),
                            g_post.reshape(1, -1), b_post.reshape(1, -1), jnp.zeros((1, 1024), F32)], axis=0)


def _unpack_small(p):
    return [p[0], p[1], p[2:3, 0:256], p[2:3, 256:384], p[3:4], p[4:5, 0:512], p[4:5, 512:1024], p[5:6], p[6:7]]


def kernel(x, mem, positions, g_emb, b_emb, w_in, g_cq, g_ckv, w_uq, w_ukv, w_mem_kv, g_out_a, g_out_b, g_out_m, w_out, g_post, b_post, loss_target, m_g_emb, m_b_emb, m_w_in, m_g_cq, m_g_ckv, m_w_uq, m_w_ukv, m_w_mem_kv, m_g_out_a, m_g_out_b, m_g_out_m, m_w_out, m_g_post, m_b_post, v_g_emb, v_b_emb, v_w_in, v_g_cq, v_g_ckv, v_w_uq, v_w_ukv, v_w_mem_kv, v_g_out_a, v_g_out_b, v_g_out_m, v_w_out, v_g_post, v_b_post):
    w_rest = _pack_rest(w_uq, w_ukv, w_mem_kv, w_out)
    gathered_in, gathered_rest = _gather_weights(w_in[0].astype(BF16), w_rest.astype(BF16))
    weights = _full_weights(gathered_in, gathered_rest)
    gains = (g_emb.reshape(1, -1), b_emb.reshape(1, -1), g_cq, g_ckv, g_out_a, g_out_b, g_out_m, g_post, b_post)
    dws, (dproj, x2, dz), small_rows = _forward_backward(x, mem, positions, loss_target, weights, gains)

    core = lax.axis_index("c").astype(jnp.int32).reshape(1)
    chip = (2 * lax.axis_index("x") + lax.axis_index("y")).astype(jnp.int32).reshape(1)
    g_in, g_rest = _split_grads(*dws)
    r_in, r_rest = _send_other_half(g_in, g_rest)
    sf_in, sb_in = _core_sum(g_in, r_in, core, HALF_IN, 256, "core_sum_in")
    sf_rest, sb_rest = _core_sum(g_rest, r_rest, core, HALF_REST, HALF_REST, "core_sum_rest")
    grad_x, dg_emb, db_emb, rb_in, rb_rest = _dh_scatter(dproj, weights[0], x2, dz, gains[0], sb_in, sb_rest)
    gh_in = _chip_sum(sf_in, rb_in, chip, HALF_IN, 256, "chip_sum_in")
    gh_rest = _chip_sum(sf_rest, rb_rest, chip, HALF_REST, HALF_REST, "chip_sum_rest")
    grad_in, grad_rest = _join_halves(gh_in, gh_rest)

    d_in, m_in, v_in = _adamw(grad_in, w_in[0], m_w_in[0], v_w_in[0], 256, "adamw_in")
    d_rest, m_rest, v_rest = _adamw(
        grad_rest, w_rest, _pack_rest(m_w_uq, m_w_ukv, m_w_mem_kv, m_w_out),
        _pack_rest(v_w_uq, v_w_ukv, v_w_mem_kv, v_w_out), HALF_REST, "adamw_rest")
    small_sum = _allreduce_small(_small_block(dg_emb, db_emb, small_rows))
    d_sm, m_sm, v_sm = _adamw(
        small_sum,
        _pack_small(g_emb, b_emb, g_cq, g_ckv, g_out_a, g_out_b, g_out_m, g_post, b_post),
        _pack_small(m_g_emb, m_b_emb, m_g_cq, m_g_ckv, m_g_out_a, m_g_out_b, m_g_out_m, m_g_post, m_b_post),
        _pack_small(v_g_emb, v_b_emb, v_g_cq, v_g_ckv, v_g_out_a, v_g_out_b, v_g_out_m, v_g_post, v_b_post),
        SMALL_ROWS, "adamw_small")
    loss = small_sum[2, 384]

    def ordered(big_in, rest, sm):
        b_uq, b_ukv, b_mem, b_out = _unpack_rest(rest)
        s_gemb, s_bemb, s_gcq, s_gckv, s_ga, s_gb, s_gm, s_gpost, s_bpost = _unpack_small(sm)
        return [s_gemb, s_bemb, big_in[None], s_gcq, s_gckv, b_uq, b_ukv, b_mem, s_ga, s_gb, s_gm, b_out,
                s_gpost, s_bpost]

    return (loss, grad_x.reshape(x.shape), *ordered(grad_in, grad_rest, small_sum), *ordered(d_in, d_rest, d_sm),
            *ordered(m_in, m_rest, m_sm), *ordered(v_in, v_rest, v_sm))
```

```python
import functools
import math

import jax
import jax.numpy as jnp
from jax import lax
from jax.experimental import pallas as pl
from jax.experimental.pallas import tpu as pltpu

F32 = jnp.float32
BF16 = jnp.bfloat16
MESH = pl.DeviceIdType.MESH
ANY = pl.BlockSpec(memory_space=pl.ANY)

D_MODEL = 1024
A_WIDTH = 1024
MLA_HEADS = 8
MLA_Q_RANK = 256
MLA_KV_RANK = 128
MLA_QK_DIM = 96
MEM_WIDTH = 512
N_MEM = 256
ROPE_THETA = 500000.0
NORM_EPS = 1e-5
NEG_INF = -1e30
DEEPNORM_ALPHA = 2.0 ** 0.25
DILATED = ((64, 1), (256, 4), (1024, 16))

ADAM_LR = 0.001
ADAM_B1 = 0.9
ADAM_B2 = 0.999
ADAM_EPS = 1e-08
ADAM_WD = 0.01
ADAM_STEP = 10

LANES = 128
VMEM_LIMIT = 56 * 1024 * 1024
LOG2E = math.log2(math.e)
LN2 = math.log(2.0)

PROJ_W = 6144
COL_CQ = 4096
COL_BG = 4608
COL_MQ = 5120
COL_MG = 5632

SHARD_ROWS = 1512
ROWS_IN = 1536
ROWS_UQ, ROWS_UKV, ROWS_MEM, ROWS_OUT = 48, 32, 256, 512
ROWS_USED = ROWS_UQ + ROWS_UKV + ROWS_MEM + ROWS_OUT
ROWS_REST = 864
HALF_IN = ROWS_IN // 2
HALF_REST = ROWS_REST // 2
SMALL_ROWS = 8


def _params(sem=None, vmem=VMEM_LIMIT):
    return pltpu.CompilerParams(dimension_semantics=sem, vmem_limit_bytes=vmem)


def _dot(a, b):
    return jnp.dot(a, b, preferred_element_type=F32)


def _dot_nt(a, b):
    return lax.dot_general(a, b, (((1,), (1,)), ((), ())), preferred_element_type=F32)


def _dot_tn(a, b):
    return lax.dot_general(a, b, (((0,), (0,)), ((), ())), preferred_element_type=F32)


def _ln_hat(x):
    mu = jnp.mean(x, axis=-1, keepdims=True)
    xc = x - mu
    var = jnp.mean(xc * xc, axis=-1, keepdims=True)
    rstd = lax.rsqrt(var + NORM_EPS)
    return xc * rstd, rstd


def _ln_bwd_rows(dxh, xh, rstd):
    return rstd * (dxh - jnp.mean(dxh, axis=-1, keepdims=True) - xh * jnp.mean(dxh * xh, axis=-1, keepdims=True))


def _rms_hat(x, width):
    ms = jnp.sum(x * x, axis=-1, keepdims=True) * (1.0 / width)
    r = lax.rsqrt(ms + NORM_EPS)
    return x * r, r


def _rms_bwd(u, xh, r, width):
    return r * (u - xh * (jnp.sum(u * xh, axis=-1, keepdims=True) * (1.0 / width)))


def _colsum(v):
    return jnp.sum(v, axis=0, keepdims=True)


def _rope_tables(pos, consts):
    ang = pos * consts[0:1, :]
    c = jnp.cos(ang)
    s = jnp.sin(ang)
    return c, s * consts[2:3, :], -s * consts[1:2, :]


def _rope(x, tables, half, inverse=False):
    c, s_up, s_dn = tables
    if inverse:
        s_up, s_dn = -s_up, -s_dn
    return x * c + pltpu.roll(x, half, 1) * s_up + pltpu.roll(x, LANES - half, 1) * s_dn


def _ln_fwd(x, g, b, tm=512):
    t, d = x.shape

    def body(x_ref, g_ref, b_ref, h_ref):
        xh, _ = _ln_hat(x_ref[...])
        h_ref[...] = (xh * g_ref[...] + b_ref[...]).astype(BF16)

    row = pl.BlockSpec((1, d), lambda i: (0, 0))
    return pl.pallas_call(
        body, name="ln_fwd", grid=(t // tm,),
        out_shape=jax.ShapeDtypeStruct((t, d), BF16),
        in_specs=[pl.BlockSpec((tm, d), lambda i: (i, 0)), row, row],
        out_specs=pl.BlockSpec((tm, d), lambda i: (i, 0)),
        compiler_params=_params(("parallel",)),
    )(x, g, b)


def _mm(a, b, out_dtype, tm, tn, tk, name, mode="nn"):
    if mode == "tn":
        k, m = a.shape
    else:
        m, k = a.shape
    n = b.shape[0] if mode == "nt" else b.shape[1]
    nk = k // tk

    def body(a_ref, b_ref, o_ref, acc_ref):
        av = a_ref[...].astype(BF16)
        bv = b_ref[...].astype(BF16)
        part = _dot_tn(av, bv) if mode == "tn" else _dot_nt(av, bv) if mode == "nt" else _dot(av, bv)
        if nk == 1:
            o_ref[...] = part.astype(out_dtype)
        else:
            kk = pl.program_id(2)

            @pl.when(kk == 0)
            def _():
                acc_ref[...] = part

            @pl.when(kk > 0)
            def _():
                acc_ref[...] += part

            @pl.when(kk == nk - 1)
            def _():
                o_ref[...] = acc_ref[...].astype(out_dtype)

    a_spec = (pl.BlockSpec((tk, tm), lambda j, i, kk: (kk, i)) if mode == "tn"
              else pl.BlockSpec((tm, tk), lambda j, i, kk: (i, kk)))
    b_spec = (pl.BlockSpec((tn, tk), lambda j, i, kk: (j, kk)) if mode == "nt"
              else pl.BlockSpec((tk, tn), lambda j, i, kk: (kk, j)))
    return pl.pallas_call(
        body, name=name, grid=(n // tn, m // tm, nk),
        out_shape=jax.ShapeDtypeStruct((m, n), out_dtype),
        in_specs=[a_spec, b_spec],
        out_specs=pl.BlockSpec((tm, tn), lambda j, i, kk: (i, j)),
        scratch_shapes=[pltpu.VMEM((tm, tn), F32)],
        compiler_params=_params(("parallel", "parallel", "arbitrary")),
    )(a, b)


def _prep(proj, pos, w_uq, w_ukv, g_cq, g_ckv, rope_a, rope_b, scales, tm=256):
    t = proj.shape[0]
    sc_a, sc_b, sc_m = (s * LOG2E for s in scales)

    def body(aq_ref, ak_ref, av_ref, bs_ref, mq_ref, pos_ref, wuq_ref, wukv_ref, gcq_ref, gckv_ref,
             ra_ref, rb_ref, qa_ref, ka_ref, va_ref, qb_ref, kb_ref, vb_ref, qm_ref, cqn_ref, ckvn_ref):
        pos_c = pos_ref[...]
        ta = _rope_tables(pos_c, ra_ref[...])
        tb = _rope_tables(pos_c, rb_ref[...])
        for j in range(A_WIDTH // LANES):
            sl = slice(j * LANES, (j + 1) * LANES)
            qa_ref[:, sl] = (_rope(aq_ref[:, sl], ta, 8) * sc_a).astype(BF16)
            ka_ref[:, sl] = _rope(ak_ref[:, sl], ta, 8).astype(BF16)
        va_ref[...] = av_ref[...].astype(BF16)
        qm_ref[...] = (mq_ref[...] * sc_m).astype(BF16)

        cq_hat, _ = _rms_hat(bs_ref[:, 0:MLA_Q_RANK], MLA_Q_RANK)
        cqn = (cq_hat * gcq_ref[...]).astype(BF16)
        cqn_ref[...] = cqn
        ckv_hat, _ = _rms_hat(bs_ref[:, MLA_Q_RANK:MLA_Q_RANK + MLA_KV_RANK], MLA_KV_RANK)
        ckvn = (ckv_hat * gckv_ref[...]).astype(BF16)
        ckvn_ref[...] = ckvn
        qfull = _dot_nt(cqn, wuq_ref[...])
        kv = _dot(ckvn, wukv_ref[...])
        kr = _rope(bs_ref[:, 384:512], tb, 16)
        lane = lax.broadcasted_iota(jnp.int32, (1, LANES), 1)
        low = lane < 64
        for h in range(MLA_HEADS):
            sl = slice(h * LANES, (h + 1) * LANES)
            qb_ref[:, sl] = (_rope(qfull[:, sl], tb, 16) * sc_b).astype(BF16)
            kb_ref[:, sl] = jnp.where(low, kv[:, sl], kr).astype(BF16)
            vb_ref[:, sl] = jnp.where(low, 0.0, kv[:, sl]).astype(BF16)

    def col(width, idx):
        return pl.BlockSpec((tm, width), lambda i: (i, idx))

    def full(shape):
        return pl.BlockSpec(shape, lambda i: (0, 0))

    wide = jax.ShapeDtypeStruct((t, 1024), BF16)
    return pl.pallas_call(
        body, name="prep", grid=(t // tm,),
        out_shape=(wide, wide, wide, wide, wide, wide,
                   jax.ShapeDtypeStruct((t, MEM_WIDTH), BF16),
                   jax.ShapeDtypeStruct((t, MLA_Q_RANK), BF16),
                   jax.ShapeDtypeStruct((t, MLA_KV_RANK), BF16)),
        in_specs=[col(1024, 0), col(1024, 1), col(1024, 2), col(512, COL_CQ // 512), col(512, COL_MQ // 512),
                  pl.BlockSpec((tm, 1), lambda i: (i, 0)),
                  full((1024, MLA_Q_RANK)), full((MLA_KV_RANK, 1024)),
                  full((1, MLA_Q_RANK)), full((1, MLA_KV_RANK)), full((8, LANES)), full((8, LANES))],
        out_specs=(col(1024, 0),) * 6 + (col(MEM_WIDTH, 0), col(MLA_Q_RANK, 0), col(MLA_KV_RANK, 0)),
        compiler_params=_params(("parallel",)),
    )(proj, proj, proj, proj, proj, pos, w_uq, w_ukv, g_cq, g_ckv, rope_a, rope_b)


def _bias_spec(bias, bq, sk, nq, order):
    shape = (pl.Element(bq), pl.Element(sk))
    if order == "big":
        return pl.BlockSpec(shape, lambda b, i, g: (0, (nq - 1 - i) * bq))
    return pl.BlockSpec(shape, lambda b, g, i: (0, (nq - 1 - i) * bq))


def _attn_fwd(q, k, v, bias, *, nb, s, sk, groups, gpb, hp, qoff, koff, voff, bq, name):
    nq = s // bq
    hw = LANES // hp
    width = gpb * LANES

    def body(*refs):
        if bias is None:
            q_ref, k_ref, v_ref, o_ref, lse_ref = refs
        else:
            q_ref, k_ref, v_ref, bias_ref, o_ref, lse_ref = refs
        lane = lax.broadcasted_iota(jnp.int32, (1, LANES), 1)
        for gi in range(gpb):
            sl = slice(gi * LANES, (gi + 1) * LANES)
            qf = q_ref[:, sl]
            kk = k_ref[:, sl]
            vv = v_ref[:, sl]
            o_all = None
            lse_all = None
            for h in range(hp):
                mask = (lane >= h * hw) & (lane < (h + 1) * hw)
                qh = jnp.where(mask, qf, jnp.zeros_like(qf)) if hp > 1 else qf
                vh = jnp.where(mask, vv, jnp.zeros_like(vv)) if hp > 1 else vv
                sc = _dot_nt(qh, kk)
                if bias is not None:
                    sc = sc + bias_ref[...]
                m = jnp.max(sc, axis=1, keepdims=True)
                p = jnp.exp2(sc - m)
                l = jnp.sum(p, axis=1, keepdims=True)
                o = _dot(p.astype(BF16), vh) / l
                lse = jnp.broadcast_to(m + jnp.log(l) * LOG2E, (bq, LANES))
                o_all = o if h == 0 else o_all + o
                lse_all = lse if h == 0 else jnp.where(mask, lse, lse_all)
            o_ref[:, sl] = o_all
            lse_ref[:, sl] = lse_all

    in_specs = [pl.BlockSpec((bq, width), lambda b, i, g: (b * nq + i, qoff + g)),
                pl.BlockSpec((sk, width), lambda b, i, g: (b, koff + g)),
                pl.BlockSpec((sk, width), lambda b, i, g: (b, voff + g))]
    args = [q, k, v]
    if bias is not None:
        in_specs.append(_bias_spec(bias, bq, sk, nq, "big"))
        args.append(bias)
    out = jax.ShapeDtypeStruct((nb * s, groups * LANES), F32)
    ospec = pl.BlockSpec((bq, width), lambda b, i, g: (b * nq + i, g))
    return pl.pallas_call(
        body, name=name, grid=(nb, nq, groups // gpb),
        out_shape=(out, out), in_specs=in_specs, out_specs=(ospec, ospec),
        compiler_params=_params(("parallel", "parallel", "parallel")),
    )(*args)


def _attn_bwd(q, k, v, o, do, lse, bias, *, nb, s, sk, groups, gpb, hp, scale, qoff, koff, voff, bq, name):
    nq = s // bq
    hw = LANES // hp
    width = gpb * LANES

    def body(*refs):
        if bias is None:
            q_ref, k_ref, v_ref, o_ref, do_ref, lse_ref, dq_ref, dk_ref, dv_ref = refs
        else:
            q_ref, k_ref, v_ref, o_ref, do_ref, lse_ref, bias_ref, dq_ref, dk_ref, dv_ref = refs
        i = pl.program_id(2)

        @pl.when(i == 0)
        def _():
            dk_ref[...] = jnp.zeros_like(dk_ref)
            dv_ref[...] = jnp.zeros_like(dv_ref)

        lane = lax.broadcasted_iota(jnp.int32, (1, LANES), 1)
        for gi in range(gpb):
            sl = slice(gi * LANES, (gi + 1) * LANES)
            qf = q_ref[:, sl]
            kk = k_ref[:, sl]
            vv = v_ref[:, sl]
            dof = do_ref[:, sl]
            prod = dof.astype(F32) * o_ref[:, sl]
            lse = lse_ref[:, sl]
            dq_all = None
            for h in range(hp):
                mask = (lane >= h * hw) & (lane < (h + 1) * hw)
                if hp > 1:
                    qh = jnp.where(mask, qf, jnp.zeros_like(qf))
                    doh = jnp.where(mask, dof, jnp.zeros_like(dof))
                    delta = jnp.sum(jnp.where(mask, prod, 0.0), axis=1, keepdims=True)
                else:
                    qh, doh = qf, dof
                    delta = jnp.sum(prod, axis=1, keepdims=True)
                sc = _dot_nt(qh, kk)
                if bias is not None:
                    sc = sc + bias_ref[...]
                p = jnp.exp2(sc - lse[:, h * hw:h * hw + 1])
                dp = _dot_nt(doh, vv)
                ds = (p * (dp - delta)).astype(BF16)
                dq = _dot(ds, kk) * scale
                dq_all = jnp.where(mask, dq, 0.0 if h == 0 else dq_all) if hp > 1 else dq
                dk_ref[:, sl] += _dot_tn(ds, qh)
                dv_ref[:, sl] += _dot_tn(p.astype(BF16), doh)
            dq_ref[:, sl] = dq_all

        @pl.when(i == nq - 1)
        def _():
            dk_ref[...] = dk_ref[...] * LN2

    in_specs = [pl.BlockSpec((bq, width), lambda b, g, i: (b * nq + i, qoff + g)),
                pl.BlockSpec((sk, width), lambda b, g, i: (b, koff + g)),
                pl.BlockSpec((sk, width), lambda b, g, i: (b, voff + g)),
                pl.BlockSpec((bq, width), lambda b, g, i: (b * nq + i, g)),
                pl.BlockSpec((bq, width), lambda b, g, i: (b * nq + i, g)),
                pl.BlockSpec((bq, width), lambda b, g, i: (b * nq + i, g))]
    args = [q, k, v, o, do, lse]
    if bias is not None:
        in_specs.append(_bias_spec(bias, bq, sk, nq, "small"))
        args.append(bias)
    dq_shape = jax.ShapeDtypeStruct((nb * s, groups * LANES), F32)
    dkv_shape = jax.ShapeDtypeStruct((nb * sk, groups * LANES), F32)
    kv_spec = pl.BlockSpec((sk, width), lambda b, g, i: (b, g))
    return pl.pallas_call(
        body, name=name, grid=(nb, groups // gpb, nq),
        out_shape=(dq_shape, dkv_shape, dkv_shape), in_specs=in_specs,
        out_specs=(pl.BlockSpec((bq, width), lambda b, g, i: (b * nq + i, g)), kv_spec, kv_spec),
        compiler_params=_params(("parallel", "parallel", "arbitrary")),
    )(*args)


def _post(x, ya, ybp, ym, proj, target, w_out, g_emb, b_emb, g_a, g_b, g_m, g_post, b_post, tm=256):
    t = x.shape[0]

    def body(x_ref, ya_ref, yb_ref, ym_ref, ga_ref, gb_ref, gm_ref, tg_ref, wo_ref,
             ge_ref, be_ref, goa_ref, gob_ref, gom_ref, gp_ref, bp_ref,
             y_ref, dz_ref, doa_ref, dob_ref, dom_ref, dga_ref, dgb_ref, dgm_ref,
             loss_ref, dgp_ref, dbp_ref, dgoa_ref, dgob_ref, dgom_ref):
        i = pl.program_id(0)

        @pl.when(i == 0)
        def _():
            for r in (loss_ref, dgp_ref, dbp_ref, dgoa_ref, dgob_ref, dgom_ref):
                r[...] = jnp.zeros_like(r)

        lane = lax.broadcasted_iota(jnp.int32, (1, LANES), 1)
        low = lane < 64
        xh0, _ = _ln_hat(x_ref[...])
        h = xh0 * ge_ref[...] + be_ref[...]

        ybp_v = yb_ref[...]
        yb = jnp.concatenate(
            [jnp.where(low, pltpu.roll(ybp_v[:, 2 * j * LANES:(2 * j + 1) * LANES], 64, 1),
                       ybp_v[:, (2 * j + 1) * LANES:(2 * j + 2) * LANES]) for j in range(4)], axis=1)

        def gated(raw, gate, gain, width):
            xh, r = _rms_hat(raw, width)
            n = xh * gain
            sg = 1.0 / (1.0 + jnp.exp(-gate))
            return xh, r, n, sg, n * (gate * sg)

        gate_a, gate_b, gate_m = ga_ref[...], gb_ref[...], gm_ref[...]
        xh_a, r_a, n_a, sg_a, y_a = gated(ya_ref[...], gate_a, goa_ref[...], A_WIDTH)
        xh_b, r_b, n_b, sg_b, y_b = gated(yb, gate_b, gob_ref[...], 512)
        xh_m, r_m, n_m, sg_m, y_m = gated(ym_ref[...], gate_m, gom_ref[...], 512)
        y = jnp.concatenate([y_a, y_b, y_m], axis=1).astype(BF16)
        y_ref[...] = y
        z = DEEPNORM_ALPHA * h + _dot(y, wo_ref[...])
        zh, rstd = _ln_hat(z)
        err = zh * gp_ref[...] + bp_ref[...] - tg_ref[...]
        rows = jnp.sum(err * err, axis=1, keepdims=True)
        loss_ref[...] += jnp.broadcast_to(jnp.sum(rows, axis=0, keepdims=True) * (0.5 / D_MODEL), (1, LANES))
        dout = err * (1.0 / D_MODEL)
        dgp_ref[...] += _colsum(dout * zh)
        dbp_ref[...] += _colsum(dout)
        dz = _ln_bwd_rows(dout * gp_ref[...], zh, rstd)
        dz_ref[...] = dz
        dy = _dot_nt(dz.astype(BF16), wo_ref[...])

        def gated_bwd(dyg, xh, r, n, sg, gate, gain, width, dgain_ref):
            dn = dyg * (gate * sg)
            dgate = dyg * n * (sg * (1.0 + gate * (1.0 - sg)))
            dgain_ref[...] += _colsum(dn * xh)
            return _rms_bwd(dn * gain, xh, r, width), dgate

        dya, dgate_a = gated_bwd(dy[:, 0:1024], xh_a, r_a, n_a, sg_a, gate_a, goa_ref[...], A_WIDTH, dgoa_ref)
        dyb, dgate_b = gated_bwd(dy[:, 1024:1536], xh_b, r_b, n_b, sg_b, gate_b, gob_ref[...], 512, dgob_ref)
        dym, dgate_m = gated_bwd(dy[:, 1536:2048], xh_m, r_m, n_m, sg_m, gate_m, gom_ref[...], 512, dgom_ref)
        doa_ref[...] = dya.astype(BF16)
        dom_ref[...] = dym.astype(BF16)
        dga_ref[...] = dgate_a.astype(BF16)
        dgb_ref[...] = dgate_b.astype(BF16)
        dgm_ref[...] = dgate_m.astype(BF16)
        for j in range(4):
            blk = dyb[:, j * LANES:(j + 1) * LANES]
            dob_ref[:, 2 * j * LANES:(2 * j + 1) * LANES] = jnp.where(low, 0.0, pltpu.roll(blk, 64, 1)).astype(BF16)
            dob_ref[:, (2 * j + 1) * LANES:(2 * j + 2) * LANES] = jnp.where(low, 0.0, blk).astype(BF16)

    def col(width, idx):
        return pl.BlockSpec((tm, width), lambda i: (i, idx))

    def full(shape):
        return pl.BlockSpec(shape, lambda i: (0, 0))

    def acc(width):
        return jax.ShapeDtypeStruct((1, width), F32)

    return pl.pallas_call(
        body, name="post", grid=(t // tm,),
        out_shape=(jax.ShapeDtypeStruct((t, 2048), BF16), jax.ShapeDtypeStruct((t, 1024), F32),
                   jax.ShapeDtypeStruct((t, 1024), BF16), jax.ShapeDtypeStruct((t, 1024), BF16),
                   jax.ShapeDtypeStruct((t, 512), BF16),
                   jax.ShapeDtypeStruct((t, 1024), BF16), jax.ShapeDtypeStruct((t, 512), BF16),
                   jax.ShapeDtypeStruct((t, 512), BF16),
                   acc(LANES), acc(1024), acc(1024), acc(1024), acc(512), acc(512)),
        in_specs=[col(1024, 0), col(1024, 0), col(1024, 0), col(512, 0),
                  col(1024, 3), col(512, COL_BG // 512), col(512, COL_MG // 512), col(1024, 0),
                  full((2048, 1024)),
                  full((1, 1024)), full((1, 1024)), full((1, 1024)), full((1, 512)), full((1, 512)),
                  full((1, 1024)), full((1, 1024))],
        out_specs=(col(2048, 0), col(1024, 0), col(1024, 0), col(1024, 0), col(512, 0),
                   col(1024, 0), col(512, 0), col(512, 0),
                   full((1, LANES)), full((1, 1024)), full((1, 1024)), full((1, 1024)), full((1, 512)),
                   full((1, 512))),
        compiler_params=_params(("arbitrary",)),
    )(x, ya, ybp, ym, proj, proj, proj, target, w_out, g_emb, b_emb, g_a, g_b, g_m, g_post, b_post)


def _prep_bwd(dqa, dka, dva, dqb, dkb, dvb, dqm, dga, dgb, dgm, proj, pos, w_uq, w_ukv, g_cq, g_ckv,
              rope_a, rope_b, tm=256):
    t = proj.shape[0]

    def body(dqa_ref, dka_ref, dva_ref, dqb_ref, dkb_ref, dvb_ref, dqm_ref, dga_ref, dgb_ref, dgm_ref,
             bs_ref, pos_ref, wuq_ref, wukv_ref, gcq_ref, gckv_ref, ra_ref, rb_ref,
             dproj_ref, dqf_ref, dkv_ref, dgcq_ref, dgckv_ref):
        i = pl.program_id(0)

        @pl.when(i == 0)
        def _():
            dgcq_ref[...] = jnp.zeros_like(dgcq_ref)
            dgckv_ref[...] = jnp.zeros_like(dgckv_ref)

        pos_c = pos_ref[...]
        ta = _rope_tables(pos_c, ra_ref[...])
        tb = _rope_tables(pos_c, rb_ref[...])
        for j in range(A_WIDTH // LANES):
            sl = slice(j * LANES, (j + 1) * LANES)
            dproj_ref[:, j * LANES:(j + 1) * LANES] = _rope(dqa_ref[:, sl], ta, 8, inverse=True).astype(BF16)
            dproj_ref[:, 1024 + j * LANES:1024 + (j + 1) * LANES] = (
                _rope(dka_ref[:, sl], ta, 8, inverse=True).astype(BF16))
        dproj_ref[:, 2048:3072] = dva_ref[...].astype(BF16)
        dproj_ref[:, 3072:4096] = dga_ref[...]

        lane = lax.broadcasted_iota(jnp.int32, (1, LANES), 1)
        low = lane < 64
        rope_lanes = (lane >= 64) & (lane < 96)
        dkr = jnp.zeros((tm, LANES), F32)
        for h in range(MLA_HEADS):
            sl = slice(h * LANES, (h + 1) * LANES)
            dqf_ref[:, sl] = _rope(dqb_ref[:, sl], tb, 16, inverse=True).astype(BF16)
            dk_h = dkb_ref[:, sl]
            dkv_ref[:, sl] = jnp.where(low, dk_h, dvb_ref[:, sl]).astype(BF16)
            dkr = dkr + jnp.where(rope_lanes, dk_h, 0.0)
        dkr = _rope(dkr, tb, 16, inverse=True)

        cq_hat, r_q = _rms_hat(bs_ref[:, 0:MLA_Q_RANK], MLA_Q_RANK)
        dcqn = _dot(dqf_ref[...], wuq_ref[...])
        dgcq_ref[...] += _colsum(dcqn * cq_hat)
        dproj_ref[:, COL_CQ:COL_CQ + 256] = _rms_bwd(dcqn * gcq_ref[...], cq_hat, r_q, MLA_Q_RANK).astype(BF16)
        ckv_hat, r_kv = _rms_hat(bs_ref[:, MLA_Q_RANK:MLA_Q_RANK + MLA_KV_RANK], MLA_KV_RANK)
        dckvn = _dot_nt(dkv_ref[...], wukv_ref[...])
        dgckv_ref[...] += _colsum(dckvn * ckv_hat)
        dproj_ref[:, COL_CQ + 256:COL_CQ + 384] = (
            _rms_bwd(dckvn * gckv_ref[...], ckv_hat, r_kv, MLA_KV_RANK).astype(BF16))
        dproj_ref[:, COL_CQ + 384:COL_CQ + 512] = dkr.astype(BF16)
        dproj_ref[:, COL_BG:COL_BG + 512] = dgb_ref[...]
        dproj_ref[:, COL_MQ:COL_MQ + 512] = dqm_ref[...].astype(BF16)
        dproj_ref[:, COL_MG:COL_MG + 512] = dgm_ref[...]

    def col(width, idx):
        return pl.BlockSpec((tm, width), lambda i: (i, idx))

    def full(shape):
        return pl.BlockSpec(shape, lambda i: (0, 0))

    return pl.pallas_call(
        body, name="prep_bwd", grid=(t // tm,),
        out_shape=(jax.ShapeDtypeStruct((t, PROJ_W), BF16), jax.ShapeDtypeStruct((t, 1024), BF16),
                   jax.ShapeDtypeStruct((t, 1024), BF16),
                   jax.ShapeDtypeStruct((1, MLA_Q_RANK), F32), jax.ShapeDtypeStruct((1, MLA_KV_RANK), F32)),
        in_specs=[col(1024, 0)] * 6 + [col(512, 0), col(1024, 0), col(512, 0), col(512, 0),
                  col(512, COL_CQ // 512), pl.BlockSpec((tm, 1), lambda i: (i, 0)),
                  full((1024, MLA_Q_RANK)), full((MLA_KV_RANK, 1024)),
                  full((1, MLA_Q_RANK)), full((1, MLA_KV_RANK)), full((8, LANES)), full((8, LANES))],
        out_specs=(col(PROJ_W, 0), col(1024, 0), col(1024, 0), full((1, MLA_Q_RANK)), full((1, MLA_KV_RANK))),
        compiler_params=_params(("arbitrary",)),
    )(dqa, dka, dva, dqb, dkb, dvb, dqm, dga, dgb, dgm, proj, pos, w_uq, w_ukv, g_cq, g_ckv, rope_a, rope_b)


def _adamw(g, w, m, v, tr, name):
    r, cols = w.shape

    def body(g_ref, w_ref, m_ref, v_ref, d_ref, nm_ref, nv_ref):
        gv = g_ref[...]
        m_new = ADAM_B1 * m_ref[...] + (1.0 - ADAM_B1) * gv
        v_new = ADAM_B2 * v_ref[...] + (1.0 - ADAM_B2) * (gv * gv)
        m_hat = m_new / (1.0 - ADAM_B1 ** ADAM_STEP)
        v_hat = v_new / (1.0 - ADAM_B2 ** ADAM_STEP)
        d_ref[...] = -ADAM_LR * (m_hat / (jnp.sqrt(v_hat) + ADAM_EPS) + ADAM_WD * w_ref[...])
        nm_ref[...] = m_new
        nv_ref[...] = v_new

    tile = pl.BlockSpec((tr, cols), lambda i: (i, 0))
    shape = jax.ShapeDtypeStruct((r, cols), F32)
    return pl.pallas_call(
        body, name=name, grid=(r // tr,),
        out_shape=(shape,) * 3, in_specs=[tile] * 4, out_specs=(tile,) * 3,
        compiler_params=_params(("parallel",)),
    )(g, w, m, v)


def _core_sum(g, recv, core, rows, tr, name):
    cols = g.shape[2]
    nblk = rows // tr

    def body(c_ref, g_ref, r_ref, sf_ref, sb_ref):
        tot = g_ref[...] + r_ref[...]
        sf_ref[...] = tot
        sb_ref[...] = tot.astype(BF16)

    half = pl.BlockSpec((None, tr, cols), lambda j, i, c_ref: (j, i, 0))
    return pl.pallas_call(
        body, name=name,
        grid_spec=pltpu.PrefetchScalarGridSpec(
            num_scalar_prefetch=1, grid=(4, nblk),
            in_specs=[pl.BlockSpec((None, tr, cols), lambda j, i, c_ref: (j, c_ref[0] * nblk + i, 0)), half],
            out_specs=(half, half)),
        out_shape=(jax.ShapeDtypeStruct((4, rows, cols), F32), jax.ShapeDtypeStruct((4, rows, cols), BF16)),
        compiler_params=_params(("parallel", "parallel")),
    )(core, g, recv)


def _chip_sum(sf, recv, chip, rows, tr, name):
    cols = sf.shape[2]

    def body(me_ref, sf_ref, r_ref, out_ref):
        acc = sf_ref[...]
        for k in range(3):
            acc = acc + r_ref[k].astype(F32)
        out_ref[...] = acc

    return pl.pallas_call(
        body, name=name,
        grid_spec=pltpu.PrefetchScalarGridSpec(
            num_scalar_prefetch=1, grid=(rows // tr,),
            in_specs=[pl.BlockSpec((None, tr, cols), lambda i, me_ref: (me_ref[0], i, 0)),
                      pl.BlockSpec((3, tr, cols), lambda i, me_ref: (0, i, 0))],
            out_specs=pl.BlockSpec((tr, cols), lambda i, me_ref: (i, 0))),
        out_shape=jax.ShapeDtypeStruct((rows, cols), F32),
        compiler_params=_params(("parallel",)),
    )(chip, sf, recv)


def _position():
    return lax.axis_index("x"), lax.axis_index("y"), lax.axis_index("c")


def _gather_weights(w_in_b, rest_b):
    def body(in_ref, rest_ref, oin_ref, orest_ref, send_sems, recv_sems, local_sems):
        x, y, c = _position()
        me = 2 * x + y
        srcs = (in_ref, rest_ref)
        dsts = (oin_ref, orest_ref)
        local = [pltpu.make_async_copy(srcs[a], dsts[a].at[me], local_sems.at[a]) for a in range(2)]
        for cp in local:
            cp.start()

        def piece(a, chip, half):
            return dsts[a].at[chip, half]

        def from_chip(a, k):
            return pltpu.make_async_remote_copy(
                src_ref=srcs[a].at[c], dst_ref=piece(a, me, c),
                send_sem=send_sems.at[3 * a + k - 1], recv_sem=recv_sems.at[3 * a + k - 1],
                device_id=(x ^ (k >> 1), y ^ (k & 1), c), device_id_type=MESH)

        def arrived(a, k):
            return pltpu.make_async_remote_copy(
                src_ref=piece(a, me ^ k, c), dst_ref=piece(a, me ^ k, c),
                send_sem=send_sems.at[3 * a + k - 1], recv_sem=recv_sems.at[3 * a + k - 1],
                device_id=(x ^ (k >> 1), y ^ (k & 1), c), device_id_type=MESH)

        def to_sibling(a, k, half):
            return pltpu.make_async_remote_copy(
                src_ref=piece(a, me ^ k, half), dst_ref=piece(a, me ^ k, half),
                send_sem=send_sems.at[6 + 3 * a + k - 1], recv_sem=recv_sems.at[6 + 3 * a + k - 1],
                device_id=(x, y, 1 - c), device_id_type=MESH)

        pairs = [(a, k) for a in range(2) for k in (1, 2, 3)]
        sends = [from_chip(a, k) for a, k in pairs]
        for cp in sends:
            cp.start()
        passed = []
        for a, k in pairs:
            arrived(a, k).wait_recv()
            cp = to_sibling(a, k, c)
            cp.start()
            passed.append(cp)
        for a, k in pairs:
            to_sibling(a, k, 1 - c).wait_recv()
        for cp in sends + passed:
            cp.wait_send()
        for cp in local:
            cp.wait()

    return pl.pallas_call(
        body, name="gather_weights",
        out_shape=(jax.ShapeDtypeStruct((4,) + w_in_b.shape, BF16), jax.ShapeDtypeStruct((4,) + rest_b.shape, BF16)),
        in_specs=[ANY, ANY], out_specs=(ANY, ANY),
        scratch_shapes=[pltpu.SemaphoreType.DMA((12,)), pltpu.SemaphoreType.DMA((12,)), pltpu.SemaphoreType.DMA((2,))],
    )(w_in_b, rest_b)


def _send_other_half(g_in, g_rest):
    def body(gin_ref, grest_ref, rin_ref, rrest_ref, send_sems, recv_sems):
        x, y, c = _position()
        srcs = (gin_ref, grest_ref)
        dsts = (rin_ref, rrest_ref)
        copies = [pltpu.make_async_remote_copy(
            src_ref=srcs[a].at[:, 1 - c], dst_ref=dsts[a],
            send_sem=send_sems.at[a], recv_sem=recv_sems.at[a], device_id=(x, y, 1 - c), device_id_type=MESH)
            for a in range(2)]
        for cp in copies:
            cp.start()
        for cp in copies:
            cp.wait_recv()
        for cp in copies:
            cp.wait_send()

    return pl.pallas_call(
        body, name="send_other_half",
        out_shape=(jax.ShapeDtypeStruct((4, HALF_IN, 1024), F32),
                   jax.ShapeDtypeStruct((4, HALF_REST, 1024), F32)),
        in_specs=[ANY, ANY], out_specs=(ANY, ANY),
        scratch_shapes=[pltpu.SemaphoreType.DMA((2,)), pltpu.SemaphoreType.DMA((2,))],
    )(g_in, g_rest)


def _dh_scatter(dproj, w_in_arr_t, x, dz, g, sb_in, sb_rest, tm=1024, tk=1024):
    t, d = x.shape
    nk = dproj.shape[1] // tk
    ni = t // tm

    def body(dp_ref, w_ref, x_ref, dz_ref, g_ref, sbin_ref, sbrest_ref,
             dx_ref, dg_ref, db_ref, rin_ref, rrest_ref, acc_ref, send_sems, recv_sems):
        i = pl.program_id(0)
        kk = pl.program_id(1)
        px, py, pc = _position()
        me = 2 * px + py
        srcs = (sbin_ref, sbrest_ref)
        dsts = (rin_ref, rrest_ref)

        def copy(a, k):
            return pltpu.make_async_remote_copy(
                src_ref=srcs[a].at[me ^ k], dst_ref=dsts[a].at[k - 1],
                send_sem=send_sems.at[3 * a + k - 1], recv_sem=recv_sems.at[3 * a + k - 1],
                device_id=(px ^ (k >> 1), py ^ (k & 1), pc), device_id_type=MESH)

        pairs = [(a, k) for a in range(2) for k in (1, 2, 3)]

        @pl.when((i == 0) & (kk == 0))
        def _():
            dg_ref[...] = jnp.zeros_like(dg_ref)
            db_ref[...] = jnp.zeros_like(db_ref)
            for a, k in pairs:
                copy(a, k).start()

        part = _dot(dp_ref[...], w_ref[...])

        @pl.when(kk == 0)
        def _():
            acc_ref[...] = part

        @pl.when(kk > 0)
        def _():
            acc_ref[...] += part

        @pl.when(kk == nk - 1)
        def _():
            xh, rstd = _ln_hat(x_ref[...])
            dht = acc_ref[...] + DEEPNORM_ALPHA * dz_ref[...]
            dg_ref[...] += _colsum(dht * xh)
            db_ref[...] += _colsum(dht)
            dx_ref[...] = _ln_bwd_rows(dht * g_ref[...], xh, rstd)

        @pl.when((i == ni - 1) & (kk == nk - 1))
        def _():
            for a, k in pairs:
                copy(a, k).wait_recv()
            for a, k in pairs:
                copy(a, k).wait_send()

    tile = pl.BlockSpec((tm, d), lambda i, kk: (i, 0))
    row = pl.BlockSpec((1, d), lambda i, kk: (0, 0))
    return pl.pallas_call(
        body, name="dh_scatter", grid=(ni, nk),
        out_shape=(jax.ShapeDtypeStruct((t, d), F32), jax.ShapeDtypeStruct((1, d), F32),
                   jax.ShapeDtypeStruct((1, d), F32),
                   jax.ShapeDtypeStruct((3, HALF_IN, 1024), BF16),
                   jax.ShapeDtypeStruct((3, HALF_REST, 1024), BF16)),
        in_specs=[pl.BlockSpec((tm, tk), lambda i, kk: (i, kk)), pl.BlockSpec((tk, d), lambda i, kk: (kk, 0)),
                  tile, tile, row, ANY, ANY],
        out_specs=(tile, row, row, ANY, ANY),
        scratch_shapes=[pltpu.VMEM((tm, d), F32), pltpu.SemaphoreType.DMA((6,)), pltpu.SemaphoreType.DMA((6,))],
        compiler_params=_params(("arbitrary", "arbitrary")),
    )(dproj, w_in_arr_t, x, dz, g, sb_in, sb_rest)


def _join_halves(gh_in, gh_rest):
    def body(hin_ref, hrest_ref, oin_ref, orest_ref, send_sems, recv_sems, local_sems):
        x, y, c = _position()
        srcs = (hin_ref, hrest_ref)
        dsts = (oin_ref, orest_ref)

        def rows(a, half):
            return dsts[a].at[half]

        local = [pltpu.make_async_copy(srcs[a], rows(a, c), local_sems.at[a]) for a in range(2)]
        remote = [pltpu.make_async_remote_copy(
            src_ref=srcs[a], dst_ref=rows(a, c), send_sem=send_sems.at[a], recv_sem=recv_sems.at[a],
            device_id=(x, y, 1 - c), device_id_type=MESH) for a in range(2)]
        for cp in local + remote:
            cp.start()
        for a in range(2):
            pltpu.make_async_remote_copy(
                src_ref=srcs[a], dst_ref=rows(a, 1 - c), send_sem=send_sems.at[a], recv_sem=recv_sems.at[a],
                device_id=(x, y, 1 - c), device_id_type=MESH).wait_recv()
        for cp in remote:
            cp.wait_send()
        for cp in local:
            cp.wait()

    return pl.pallas_call(
        body, name="join_halves",
        out_shape=(jax.ShapeDtypeStruct((2, HALF_IN, 1024), F32),
                   jax.ShapeDtypeStruct((2, HALF_REST, 1024), F32)),
        in_specs=[ANY, ANY], out_specs=(ANY, ANY),
        scratch_shapes=[pltpu.SemaphoreType.DMA((2,)), pltpu.SemaphoreType.DMA((2,)), pltpu.SemaphoreType.DMA((2,))],
    )(gh_in, gh_rest)


def _allreduce_small(vec):
    def body(vec_ref, out_ref, all_ref, send_sems, recv_sems):
        x, y, c = _position()
        me = 4 * x + 2 * y + c
        all_ref[me] = vec_ref[...]

        def copy(k, slot):
            return pltpu.make_async_remote_copy(
                src_ref=vec_ref, dst_ref=all_ref.at[slot], send_sem=send_sems.at[k - 1], recv_sem=recv_sems.at[k - 1],
                device_id=(x ^ (k >> 2), y ^ ((k >> 1) & 1), c ^ (k & 1)), device_id_type=MESH)

        copies = [copy(k, me) for k in range(1, 8)]
        for cp in copies:
            cp.start()
        for k in range(1, 8):
            copy(k, me ^ k).wait_recv()
        for cp in copies:
            cp.wait_send()
        total = all_ref[0]
        for d in range(1, 8):
            total = total + all_ref[d]
        out_ref[...] = total

    return pl.pallas_call(
        body, name="allreduce_small",
        out_shape=jax.ShapeDtypeStruct(vec.shape, vec.dtype),
        in_specs=[pl.BlockSpec(memory_space=pltpu.VMEM)], out_specs=pl.BlockSpec(memory_space=pltpu.VMEM),
        scratch_shapes=[pltpu.VMEM((8,) + vec.shape, vec.dtype), pltpu.SemaphoreType.DMA((7,)),
                        pltpu.SemaphoreType.DMA((7,))],
    )(vec)


def _pack_rest(w_uq, w_ukv, w_mem, w_out):
    rows = jnp.concatenate([w_uq[0].T.reshape(-1, 1024), w_ukv.reshape(-1, 1024), w_mem.reshape(-1, 1024),
                            w_out.reshape(-1, 1024)], axis=0)
    return jnp.pad(rows, ((0, ROWS_REST - ROWS_USED), (0, 0)))


def _unpack_rest(p):
    uq = p[0:ROWS_UQ].reshape(192, 256).T[None]
    o = ROWS_UQ
    out = [uq]
    for rows, shape in ((ROWS_UKV, (1, 128, 256)), (ROWS_MEM, (1, 256, 1024)), (ROWS_OUT, (1, 512, 1024))):
        out.append(p[o:o + rows].reshape(shape))
        o += rows
    return out


def _full_weights(g_in, g_rest):
    z = functools.partial(jnp.zeros, dtype=g_in.dtype)
    cut = 4480 - 2 * SHARD_ROWS
    w_in_arr_t = jnp.concatenate(
        [g_in[0, :SHARD_ROWS], g_in[1, :SHARD_ROWS], g_in[2, :cut], z((64, 1024)), g_in[2, cut:cut + 32],
         z((32, 1024)), g_in[2, cut + 32:SHARD_ROWS], g_in[3, :SHARD_ROWS]], axis=0)
    w_uq_t = g_rest[:, 0:ROWS_UQ].reshape(768, 256)
    w_uq_pad_t = jnp.pad(w_uq_t.reshape(MLA_HEADS, MLA_QK_DIM, 256), ((0, 0), (0, 32), (0, 0))).reshape(1024, 256)
    w_ukv = jnp.concatenate([g_rest[j, ROWS_UQ:ROWS_UQ + ROWS_UKV].reshape(128, 256) for j in range(4)], axis=1)
    lo = ROWS_UQ + ROWS_UKV
    w_mem = g_rest[:, lo:lo + ROWS_MEM].reshape(4 * ROWS_MEM, 1024)
    w_out = g_rest[:, lo + ROWS_MEM:lo + ROWS_MEM + ROWS_OUT].reshape(4 * ROWS_OUT, 1024)
    return w_in_arr_t, w_uq_pad_t, w_ukv, w_mem, w_out


def _split_grads(dw_in_arr_t, dw_uq_pad_t, dw_ukv, dw_mem, dw_out):
    dw_in_t = jnp.concatenate([dw_in_arr_t[:4480], dw_in_arr_t[4544:4576], dw_in_arr_t[4608:]], axis=0)
    g_in = jnp.pad(dw_in_t.reshape(4, SHARD_ROWS, 1024), ((0, 0), (0, ROWS_IN - SHARD_ROWS), (0, 0)))
    dw_uq_t = dw_uq_pad_t.reshape(MLA_HEADS, LANES, 256)[:, :MLA_QK_DIM].reshape(4, ROWS_UQ, 1024)
    parts = [dw_uq_t, dw_ukv.reshape(128, 4, 256).transpose(1, 0, 2).reshape(4, ROWS_UKV, 1024),
             dw_mem.reshape(4, ROWS_MEM, 1024), dw_out.reshape(4, ROWS_OUT, 1024)]
    g_rest = jnp.pad(jnp.concatenate(parts, axis=1), ((0, 0), (0, ROWS_REST - ROWS_USED), (0, 0)))
    return g_in, g_rest


def _rope_consts(rot, first, period):
    half = rot // 2
    inv_freq = ROPE_THETA ** (-(jnp.arange(0, rot, 2, dtype=F32) / rot))
    lane = jnp.arange(LANES) % period - first
    in_rot = (lane >= 0) & (lane < rot)
    freq = jnp.where(in_rot, inv_freq[jnp.clip(lane, 0, rot - 1) % half], 0.0)
    lo = (in_rot & (lane < half)).astype(F32)
    hi = (in_rot & (lane >= half)).astype(F32)
    return jnp.concatenate([freq[None], lo[None], hi[None], jnp.zeros((5, LANES), F32)], axis=0)


def _dilated_bias(s, bq):
    delta = jnp.arange(2 * s - bq)[None, :] - (s - bq) - jnp.arange(bq)[:, None]
    count = jnp.zeros(delta.shape, F32)
    for reach, dil in DILATED:
        count = count + ((jnp.abs(delta) <= reach) & (delta % dil == 0)).astype(F32)
    return jnp.where(count > 0, jnp.log(jnp.maximum(count, 1.0)) * LOG2E, NEG_INF)


def _forward_backward(x, mem, positions, target, weights, gains):
    w_in_arr_t, w_uq_pad_t, w_ukv, w_mem, w_out = weights
    g_emb, b_emb, g_cq, g_ckv, g_out_a, g_out_b, g_out_m, g_post, b_post = gains
    nb, s, d = x.shape
    t = nb * s
    x2 = x.reshape(t, d)
    mem2 = mem.reshape(nb * N_MEM, d)
    tgt2 = target.reshape(t, d)
    pos = positions.reshape(t, 1).astype(F32)
    rope_a = _rope_consts(16, 0, 64)
    rope_b = _rope_consts(32, 64, 128)
    bq_a = 256
    bias = _dilated_bias(s, bq_a)
    scales = (0.125, MLA_QK_DIM ** -0.5, 128 ** -0.5)

    h = _ln_fwd(x2, g_emb, b_emb)
    proj = _mm(h, w_in_arr_t, F32, 1024, 1024, 1024, "in_proj", mode="nt")
    qa, ka, va, qb, kb, vb, qm, cqn, ckvn = _prep(proj, pos, w_uq_pad_t, w_ukv, g_cq, g_ckv, rope_a, rope_b, scales)
    mkv = _mm(mem2, w_mem, BF16, nb * N_MEM, 1024, 1024, "mem_kv")

    cfg_a = dict(nb=nb, s=s, sk=s, groups=8, gpb=1, hp=2, qoff=0, koff=0, voff=0, bq=bq_a)
    cfg_b = dict(nb=nb, s=s, sk=s, groups=8, gpb=2, hp=1, qoff=0, koff=0, voff=0, bq=256)
    cfg_m = dict(nb=nb, s=s, sk=N_MEM, groups=4, gpb=1, hp=1, qoff=0, koff=0, voff=4, bq=512)
    ya, lse_a = _attn_fwd(qa, ka, va, bias, name="attn_a_fwd", **cfg_a)
    yb, lse_b = _attn_fwd(qb, kb, vb, None, name="attn_b_fwd", **cfg_b)
    ym, lse_m = _attn_fwd(qm, mkv, mkv, None, name="attn_m_fwd", **cfg_m)

    (y, dz, doa, dob, dom, dga, dgb, dgm, loss, dg_post, db_post, dg_a, dg_b, dg_m) = _post(
        x2, ya, yb, ym, proj, tgt2, w_out, g_emb, b_emb, g_out_a, g_out_b, g_out_m, g_post, b_post)

    dqa, dka, dva = _attn_bwd(qa, ka, va, ya, doa, lse_a, bias, name="attn_a_bwd", scale=scales[0], **cfg_a)
    dqb, dkb, dvb = _attn_bwd(qb, kb, vb, yb, dob, lse_b, None, name="attn_b_bwd", scale=scales[1], **cfg_b)
    dqm, dmk, dmv = _attn_bwd(qm, mkv, mkv, ym, dom, lse_m, None, name="attn_m_bwd", scale=scales[2], **cfg_m)
    dmkv = jnp.concatenate([dmk, dmv], axis=1)

    dproj, dqf, dkv, dg_cq, dg_ckv = _prep_bwd(
        dqa, dka, dva, dqb, dkb, dvb, dqm, dga, dgb, dgm, proj, pos, w_uq_pad_t, w_ukv, g_cq, g_ckv, rope_a, rope_b)

    dw_in_arr = _mm(dproj, h, F32, 1024, 1024, 1024, "dw_in", mode="tn")
    dw_out = _mm(y, dz, F32, 1024, 1024, 1024, "dw_out", mode="tn")
    dw_uq_pad = _mm(dqf, cqn, F32, 1024, 256, 1024, "dw_uq", mode="tn")
    dw_ukv = _mm(ckvn, dkv, F32, 128, 1024, 1024, "dw_ukv", mode="tn")
    dw_mem = _mm(mem2, dmkv, F32, 1024, 1024, nb * N_MEM, "dw_mem", mode="tn")
    small_rows = (dg_cq, dg_ckv, loss, dg_a, dg_b, dg_m, dg_post, db_post)
    return (dw_in_arr, dw_uq_pad, dw_ukv, dw_mem, dw_out), (dproj, x2, dz), small_rows


def _small_block(dg_emb, db_emb, small_rows):
    dg_cq, dg_ckv, loss, dg_a, dg_b, dg_m, dg_post, db_post = small_rows
    row2 = jnp.concatenate([dg_cq, dg_ckv, loss, jnp.zeros((1, 512), F32)], axis=1)
    return jnp.concatenate([dg_emb, db_emb, row2, dg_a, jnp.concatenate([dg_b, dg_m], axis=1), dg_post, db_post,
                            jnp.zeros((1, 1024), F32)], axis=0)


def _pack_small(g_emb, b_emb, g_cq, g_ckv, g_out_a, g_out_b, g_out_m, g_post, b_post):
    row2 = jnp.concatenate([g_cq.reshape(1, -1), g_ckv.reshape(1, -1), jnp.zeros((1, 640), F32)], axis=1)
    return jnp.concatenate([g_emb.reshape(1, -1), b_emb.reshape(1, -1), row2, g_out_a.reshape(1, -1),
                            jnp.concatenate([g_out_b.reshape(1, -1), g_out_m.reshape(1, -1)], axis=1),
                            g_post.reshape(1, -1), b_post.reshape(1, -1), jnp.zeros((1, 1024), F32)], axis=0)


def _unpack_small(p):
    return [p[0], p[1], p[2:3, 0:256], p[2:3, 256:384], p[3:4], p[4:5, 0:512], p[4:5, 512:1024], p[5:6], p[6:7]]


def kernel(x, mem, positions, g_emb, b_emb, w_in, g_cq, g_ckv, w_uq, w_ukv, w_mem_kv, g_out_a, g_out_b, g_out_m, w_out, g_post, b_post, loss_target, m_g_emb, m_b_emb, m_w_in, m_g_cq, m_g_ckv, m_w_uq, m_w_ukv, m_w_mem_kv, m_g_out_a, m_g_out_b, m_g_out_m, m_w_out, m_g_post, m_b_post, v_g_emb, v_b_emb, v_w_in, v_g_cq, v_g_ckv, v_w_uq, v_w_ukv, v_w_mem_kv, v_g_out_a, v_g_out_b, v_g_out_m, v_w_out, v_g_post, v_b_post):
    w_rest = _pack_rest(w_uq, w_ukv, w_mem_kv, w_out)
    w_in_t = w_in[0].T
    w_in_b = jnp.pad(w_in_t.astype(BF16), ((0, ROWS_IN - SHARD_ROWS), (0, 0)))
    gathered_in, gathered_rest = _gather_weights(w_in_b.reshape(2, HALF_IN, 1024),
                                                 w_rest.astype(BF16).reshape(2, HALF_REST, 1024))
    weights = _full_weights(gathered_in.reshape(4, ROWS_IN, 1024), gathered_rest.reshape(4, ROWS_REST, 1024))
    gains = (g_emb.reshape(1, -1), b_emb.reshape(1, -1), g_cq, g_ckv, g_out_a, g_out_b, g_out_m, g_post, b_post)
    dws, (dproj, x2, dz), small_rows = _forward_backward(x, mem, positions, loss_target, weights, gains)

    core = lax.axis_index("c").astype(jnp.int32).reshape(1)
    chip = (2 * lax.axis_index("x") + lax.axis_index("y")).astype(jnp.int32).reshape(1)
    g_in, g_rest = _split_grads(*dws)
    r_in, r_rest = _send_other_half(g_in.reshape(4, 2, HALF_IN, 1024), g_rest.reshape(4, 2, HALF_REST, 1024))
    sf_in, sb_in = _core_sum(g_in, r_in, core, HALF_IN, HALF_IN // 2, "core_sum_in")
    sf_rest, sb_rest = _core_sum(g_rest, r_rest, core, HALF_REST, HALF_REST, "core_sum_rest")
    grad_x, dg_emb, db_emb, rb_in, rb_rest = _dh_scatter(dproj, weights[0], x2, dz, gains[0], sb_in, sb_rest)
    gh_in = _chip_sum(sf_in, rb_in, chip, HALF_IN, HALF_IN // 2, "chip_sum_in")
    gh_rest = _chip_sum(sf_rest, rb_rest, chip, HALF_REST, HALF_REST, "chip_sum_rest")
    grad_in, grad_rest = _join_halves(gh_in, gh_rest)
    grad_in = grad_in.reshape(ROWS_IN, 1024)
    grad_rest = grad_rest.reshape(ROWS_REST, 1024)

    d_in, m_in, v_in = _adamw(grad_in, w_in_t, m_w_in[0].T, v_w_in[0].T, SHARD_ROWS // 3, "adamw_in")
    d_rest, m_rest, v_rest = _adamw(
        grad_rest, w_rest, _pack_rest(m_w_uq, m_w_ukv, m_w_mem_kv, m_w_out),
        _pack_rest(v_w_uq, v_w_ukv, v_w_mem_kv, v_w_out), HALF_REST, "adamw_rest")
    small_sum = _allreduce_small(_small_block(dg_emb, db_emb, small_rows))
    d_sm, m_sm, v_sm = _adamw(
        small_sum,
        _pack_small(g_emb, b_emb, g_cq, g_ckv, g_out_a, g_out_b, g_out_m, g_post, b_post),
        _pack_small(m_g_emb, m_b_emb, m_g_cq, m_g_ckv, m_g_out_a, m_g_out_b, m_g_out_m, m_g_post, m_b_post),
        _pack_small(v_g_emb, v_b_emb, v_g_cq, v_g_ckv, v_g_out_a, v_g_out_b, v_g_out_m, v_g_post, v_b_post),
        SMALL_ROWS, "adamw_small")
    loss = small_sum[2, 384]

    def ordered(big_in, rest, sm):
        b_uq, b_ukv, b_mem, b_out = _unpack_rest(rest)
        s_gemb, s_bemb, s_gcq, s_gckv, s_ga, s_gb, s_gm, s_gpost, s_bpost = _unpack_small(sm)
        return [s_gemb, s_bemb, big_in[:SHARD_ROWS].T[None], s_gcq, s_gckv, b_uq, b_ukv, b_mem, s_ga, s_gb, s_gm,
                b_out, s_gpost, s_bpost]

    return (loss, grad_x.reshape(x.shape), *ordered(grad_in, grad_rest, small_sum), *ordered(d_in, d_rest, d_sm),
            *ordered(m_in, m_rest, m_sm), *ordered(v_in, v_rest, v_sm))
```

```python
import functools
import math

import jax
import jax.numpy as jnp
from jax import lax
from jax.experimental import pallas as pl
from jax.experimental.pallas import tpu as pltpu

F32 = jnp.float32
BF16 = jnp.bfloat16
MESH = pl.DeviceIdType.MESH
ANY = pl.BlockSpec(memory_space=pl.ANY)
IN_VMEM = pl.BlockSpec(memory_space=pltpu.VMEM)

D_MODEL = 1024
A_WIDTH = 1024
MLA_HEADS = 8
MLA_Q_RANK = 256
MLA_KV_RANK = 128
MLA_QK_DIM = 96
MEM_WIDTH = 512
N_MEM = 256
ROPE_THETA = 500000.0
NORM_EPS = 1e-5
NEG_INF = -1e30
DEEPNORM_ALPHA = 2.0 ** 0.25
DILATED = ((64, 1), (256, 4), (1024, 16))

ADAM_LR = 0.001
ADAM_B1 = 0.9
ADAM_B2 = 0.999
ADAM_EPS = 1e-08
ADAM_WD = 0.01
ADAM_STEP = 10

LANES = 128
VMEM_LIMIT = 56 * 1024 * 1024
LOG2E = math.log2(math.e)
LN2 = math.log(2.0)

PROJ_W = 6144
COL_CQ = 4096
COL_BG = 4608
COL_MQ = 5120
COL_MG = 5632

SHARD_ROWS = 1512
ROWS_IN = 1536
ROWS_UQ, ROWS_UKV, ROWS_MEM, ROWS_OUT = 48, 32, 256, 512
ROWS_USED = ROWS_UQ + ROWS_UKV + ROWS_MEM + ROWS_OUT
ROWS_REST = 864
HALF_IN = ROWS_IN // 2
HALF_REST = ROWS_REST // 2
SMALL_ROWS = 8


def _params(sem=None, vmem=VMEM_LIMIT):
    return pltpu.CompilerParams(dimension_semantics=sem, vmem_limit_bytes=vmem)


def _dot(a, b):
    return jnp.dot(a, b, preferred_element_type=F32)


def _dot_nt(a, b):
    return lax.dot_general(a, b, (((1,), (1,)), ((), ())), preferred_element_type=F32)


def _dot_tn(a, b):
    return lax.dot_general(a, b, (((0,), (0,)), ((), ())), preferred_element_type=F32)


def _ln_hat(x):
    mu = jnp.mean(x, axis=-1, keepdims=True)
    xc = x - mu
    var = jnp.mean(xc * xc, axis=-1, keepdims=True)
    rstd = lax.rsqrt(var + NORM_EPS)
    return xc * rstd, rstd


def _ln_bwd_rows(dxh, xh, rstd):
    return rstd * (dxh - jnp.mean(dxh, axis=-1, keepdims=True) - xh * jnp.mean(dxh * xh, axis=-1, keepdims=True))


def _rms_hat(x, width):
    ms = jnp.sum(x * x, axis=-1, keepdims=True) * (1.0 / width)
    r = lax.rsqrt(ms + NORM_EPS)
    return x * r, r


def _rms_bwd(u, xh, r, width):
    return r * (u - xh * (jnp.sum(u * xh, axis=-1, keepdims=True) * (1.0 / width)))


def _colsum(v):
    return jnp.sum(v, axis=0, keepdims=True)


def _rope_tables(pos, consts):
    ang = pos * consts[0:1, :]
    c = jnp.cos(ang)
    s = jnp.sin(ang)
    return c, s * consts[2:3, :], -s * consts[1:2, :]


def _rope(x, tables, half, inverse=False):
    c, s_up, s_dn = tables
    if inverse:
        s_up, s_dn = -s_up, -s_dn
    return x * c + pltpu.roll(x, half, 1) * s_up + pltpu.roll(x, LANES - half, 1) * s_dn


def _ln_fwd(x, g, b, tm=512):
    t, d = x.shape

    def body(x_ref, g_ref, b_ref, h_ref):
        xh, _ = _ln_hat(x_ref[...])
        h_ref[...] = (xh * g_ref[...] + b_ref[...]).astype(BF16)

    row = pl.BlockSpec((1, d), lambda i: (0, 0))
    return pl.pallas_call(
        body, name="ln_fwd", grid=(t // tm,),
        out_shape=jax.ShapeDtypeStruct((t, d), BF16),
        in_specs=[pl.BlockSpec((tm, d), lambda i: (i, 0)), row, row],
        out_specs=pl.BlockSpec((tm, d), lambda i: (i, 0)),
        compiler_params=_params(("parallel",)),
    )(x, g, b)


def _mm(a, b, out_dtype, tm, tn, tk, name, mode="nn"):
    if mode == "tn":
        k, m = a.shape
    else:
        m, k = a.shape
    n = b.shape[0] if mode == "nt" else b.shape[1]
    nk = k // tk

    def body(a_ref, b_ref, o_ref, acc_ref):
        av = a_ref[...].astype(BF16)
        bv = b_ref[...].astype(BF16)
        part = _dot_tn(av, bv) if mode == "tn" else _dot_nt(av, bv) if mode == "nt" else _dot(av, bv)
        if nk == 1:
            o_ref[...] = part.astype(out_dtype)
        else:
            kk = pl.program_id(2)

            @pl.when(kk == 0)
            def _():
                acc_ref[...] = part

            @pl.when(kk > 0)
            def _():
                acc_ref[...] += part

            @pl.when(kk == nk - 1)
            def _():
                o_ref[...] = acc_ref[...].astype(out_dtype)

    a_spec = (pl.BlockSpec((tk, tm), lambda j, i, kk: (kk, i)) if mode == "tn"
              else pl.BlockSpec((tm, tk), lambda j, i, kk: (i, kk)))
    b_spec = (pl.BlockSpec((tn, tk), lambda j, i, kk: (j, kk)) if mode == "nt"
              else pl.BlockSpec((tk, tn), lambda j, i, kk: (kk, j)))
    return pl.pallas_call(
        body, name=name, grid=(n // tn, m // tm, nk),
        out_shape=jax.ShapeDtypeStruct((m, n), out_dtype),
        in_specs=[a_spec, b_spec],
        out_specs=pl.BlockSpec((tm, tn), lambda j, i, kk: (i, j)),
        scratch_shapes=[pltpu.VMEM((tm, tn), F32)],
        compiler_params=_params(("parallel", "parallel", "arbitrary")),
    )(a, b)


def _prep(proj, pos, w_uq, w_ukv, g_cq, g_ckv, rope_a, rope_b, scales, tm=256):
    t = proj.shape[0]
    sc_a, sc_b, sc_m = (s * LOG2E for s in scales)

    def body(aq_ref, ak_ref, av_ref, bs_ref, mq_ref, pos_ref, wuq_ref, wukv_ref, gcq_ref, gckv_ref,
             ra_ref, rb_ref, qa_ref, ka_ref, va_ref, qb_ref, kb_ref, vb_ref, qm_ref, cqn_ref, ckvn_ref):
        pos_c = pos_ref[...]
        ta = _rope_tables(pos_c, ra_ref[...])
        tb = _rope_tables(pos_c, rb_ref[...])
        for j in range(A_WIDTH // LANES):
            sl = slice(j * LANES, (j + 1) * LANES)
            qa_ref[:, sl] = (_rope(aq_ref[:, sl], ta, 8) * sc_a).astype(BF16)
            ka_ref[:, sl] = _rope(ak_ref[:, sl], ta, 8).astype(BF16)
        va_ref[...] = av_ref[...].astype(BF16)
        qm_ref[...] = (mq_ref[...] * sc_m).astype(BF16)

        cq_hat, _ = _rms_hat(bs_ref[:, 0:MLA_Q_RANK], MLA_Q_RANK)
        cqn = (cq_hat * gcq_ref[...]).astype(BF16)
        cqn_ref[...] = cqn
        ckv_hat, _ = _rms_hat(bs_ref[:, MLA_Q_RANK:MLA_Q_RANK + MLA_KV_RANK], MLA_KV_RANK)
        ckvn = (ckv_hat * gckv_ref[...]).astype(BF16)
        ckvn_ref[...] = ckvn
        qfull = _dot_nt(cqn, wuq_ref[...])
        kv = _dot(ckvn, wukv_ref[...])
        kr = _rope(bs_ref[:, 384:512], tb, 16)
        lane = lax.broadcasted_iota(jnp.int32, (1, LANES), 1)
        low = lane < 64
        for h in range(MLA_HEADS):
            sl = slice(h * LANES, (h + 1) * LANES)
            qb_ref[:, sl] = (_rope(qfull[:, sl], tb, 16) * sc_b).astype(BF16)
            kb_ref[:, sl] = jnp.where(low, kv[:, sl], kr).astype(BF16)
            vb_ref[:, sl] = jnp.where(low, 0.0, kv[:, sl]).astype(BF16)

    def col(width, idx):
        return pl.BlockSpec((tm, width), lambda i: (i, idx))

    def full(shape):
        return pl.BlockSpec(shape, lambda i: (0, 0))

    wide = jax.ShapeDtypeStruct((t, 1024), BF16)
    return pl.pallas_call(
        body, name="prep", grid=(t // tm,),
        out_shape=(wide, wide, wide, wide, wide, wide,
                   jax.ShapeDtypeStruct((t, MEM_WIDTH), BF16),
                   jax.ShapeDtypeStruct((t, MLA_Q_RANK), BF16),
                   jax.ShapeDtypeStruct((t, MLA_KV_RANK), BF16)),
        in_specs=[col(1024, 0), col(1024, 1), col(1024, 2), col(512, COL_CQ // 512), col(512, COL_MQ // 512),
                  pl.BlockSpec((tm, 1), lambda i: (i, 0)),
                  full((1024, MLA_Q_RANK)), full((MLA_KV_RANK, 1024)),
                  full((1, MLA_Q_RANK)), full((1, MLA_KV_RANK)), full((8, LANES)), full((8, LANES))],
        out_specs=(col(1024, 0),) * 6 + (col(MEM_WIDTH, 0), col(MLA_Q_RANK, 0), col(MLA_KV_RANK, 0)),
        compiler_params=_params(("parallel",)),
    )(proj, proj, proj, proj, proj, pos, w_uq, w_ukv, g_cq, g_ckv, rope_a, rope_b)


def _bias_spec(bias, bq, sk, nq, order):
    shape = (pl.Element(bq), pl.Element(sk))
    if order == "big":
        return pl.BlockSpec(shape, lambda b, i, g: (0, (nq - 1 - i) * bq))
    return pl.BlockSpec(shape, lambda b, g, i: (0, (nq - 1 - i) * bq))


def _attn_fwd(q, k, v, bias, *, nb, s, sk, groups, gpb, hp, qoff, koff, voff, bq, name):
    nq = s // bq
    hw = LANES // hp
    width = gpb * LANES

    def body(*refs):
        if bias is None:
            q_ref, k_ref, v_ref, o_ref, lse_ref = refs
        else:
            q_ref, k_ref, v_ref, bias_ref, o_ref, lse_ref = refs
        lane = lax.broadcasted_iota(jnp.int32, (1, LANES), 1)
        for gi in range(gpb):
            sl = slice(gi * LANES, (gi + 1) * LANES)
            qf = q_ref[:, sl]
            kk = k_ref[:, sl]
            vv = v_ref[:, sl]
            o_all = None
            lse_all = None
            for h in range(hp):
                mask = (lane >= h * hw) & (lane < (h + 1) * hw)
                qh = jnp.where(mask, qf, jnp.zeros_like(qf)) if hp > 1 else qf
                vh = jnp.where(mask, vv, jnp.zeros_like(vv)) if hp > 1 else vv
                sc = _dot_nt(qh, kk)
                if bias is not None:
                    sc = sc + bias_ref[...]
                m = jnp.max(sc, axis=1, keepdims=True)
                p = jnp.exp2(sc - m)
                l = jnp.sum(p, axis=1, keepdims=True)
                o = _dot(p.astype(BF16), vh) / l
                lse = jnp.broadcast_to(m + jnp.log(l) * LOG2E, (bq, LANES))
                o_all = o if h == 0 else o_all + o
                lse_all = lse if h == 0 else jnp.where(mask, lse, lse_all)
            o_ref[:, sl] = o_all
            lse_ref[:, sl] = lse_all

    in_specs = [pl.BlockSpec((bq, width), lambda b, i, g: (b * nq + i, qoff + g)),
                pl.BlockSpec((sk, width), lambda b, i, g: (b, koff + g)),
                pl.BlockSpec((sk, width), lambda b, i, g: (b, voff + g))]
    args = [q, k, v]
    if bias is not None:
        in_specs.append(_bias_spec(bias, bq, sk, nq, "big"))
        args.append(bias)
    out = jax.ShapeDtypeStruct((nb * s, groups * LANES), F32)
    ospec = pl.BlockSpec((bq, width), lambda b, i, g: (b * nq + i, g))
    return pl.pallas_call(
        body, name=name, grid=(nb, nq, groups // gpb),
        out_shape=(out, out), in_specs=in_specs, out_specs=(ospec, ospec),
        compiler_params=_params(("parallel", "parallel", "parallel")),
    )(*args)


def _attn_bwd(q, k, v, o, do, lse, bias, *, nb, s, sk, groups, gpb, hp, scale, qoff, koff, voff, bq, name):
    nq = s // bq
    hw = LANES // hp
    width = gpb * LANES

    def body(*refs):
        if bias is None:
            q_ref, k_ref, v_ref, o_ref, do_ref, lse_ref, dq_ref, dk_ref, dv_ref = refs
        else:
            q_ref, k_ref, v_ref, o_ref, do_ref, lse_ref, bias_ref, dq_ref, dk_ref, dv_ref = refs
        i = pl.program_id(2)

        @pl.when(i == 0)
        def _():
            dk_ref[...] = jnp.zeros_like(dk_ref)
            dv_ref[...] = jnp.zeros_like(dv_ref)

        lane = lax.broadcasted_iota(jnp.int32, (1, LANES), 1)
        for gi in range(gpb):
            sl = slice(gi * LANES, (gi + 1) * LANES)
            qf = q_ref[:, sl]
            kk = k_ref[:, sl]
            vv = v_ref[:, sl]
            dof = do_ref[:, sl]
            prod = dof.astype(F32) * o_ref[:, sl]
            lse = lse_ref[:, sl]
            dq_all = None
            for h in range(hp):
                mask = (lane >= h * hw) & (lane < (h + 1) * hw)
                if hp > 1:
                    qh = jnp.where(mask, qf, jnp.zeros_like(qf))
                    doh = jnp.where(mask, dof, jnp.zeros_like(dof))
                    delta = jnp.sum(jnp.where(mask, prod, 0.0), axis=1, keepdims=True)
                else:
                    qh, doh = qf, dof
                    delta = jnp.sum(prod, axis=1, keepdims=True)
                sc = _dot_nt(qh, kk)
                if bias is not None:
                    sc = sc + bias_ref[...]
                p = jnp.exp2(sc - lse[:, h * hw:h * hw + 1])
                dp = _dot_nt(doh, vv)
                ds = (p * (dp - delta)).astype(BF16)
                dq = _dot(ds, kk) * scale
                dq_all = jnp.where(mask, dq, 0.0 if h == 0 else dq_all) if hp > 1 else dq
                dk_ref[:, sl] += _dot_tn(ds, qh)
                dv_ref[:, sl] += _dot_tn(p.astype(BF16), doh)
            dq_ref[:, sl] = dq_all

        @pl.when(i == nq - 1)
        def _():
            dk_ref[...] = dk_ref[...] * LN2

    in_specs = [pl.BlockSpec((bq, width), lambda b, g, i: (b * nq + i, qoff + g)),
                pl.BlockSpec((sk, width), lambda b, g, i: (b, koff + g)),
                pl.BlockSpec((sk, width), lambda b, g, i: (b, voff + g)),
                pl.BlockSpec((bq, width), lambda b, g, i: (b * nq + i, g)),
                pl.BlockSpec((bq, width), lambda b, g, i: (b * nq + i, g)),
                pl.BlockSpec((bq, width), lambda b, g, i: (b * nq + i, g))]
    args = [q, k, v, o, do, lse]
    if bias is not None:
        in_specs.append(_bias_spec(bias, bq, sk, nq, "small"))
        args.append(bias)
    dq_shape = jax.ShapeDtypeStruct((nb * s, groups * LANES), F32)
    dkv_shape = jax.ShapeDtypeStruct((nb * sk, groups * LANES), F32)
    kv_spec = pl.BlockSpec((sk, width), lambda b, g, i: (b, g))
    return pl.pallas_call(
        body, name=name, grid=(nb, groups // gpb, nq),
        out_shape=(dq_shape, dkv_shape, dkv_shape), in_specs=in_specs,
        out_specs=(pl.BlockSpec((bq, width), lambda b, g, i: (b * nq + i, g)), kv_spec, kv_spec),
        compiler_params=_params(("parallel", "parallel", "arbitrary")),
    )(*args)


def _post(x, ya, ybp, ym, proj, target, w_out, g_emb, b_emb, g_a, g_b, g_m, g_post, b_post, tm=256):
    t = x.shape[0]

    def body(x_ref, ya_ref, yb_ref, ym_ref, ga_ref, gb_ref, gm_ref, tg_ref, wo_ref,
             ge_ref, be_ref, goa_ref, gob_ref, gom_ref, gp_ref, bp_ref,
             y_ref, dz_ref, doa_ref, dob_ref, dom_ref, dga_ref, dgb_ref, dgm_ref,
             loss_ref, dgp_ref, dbp_ref, dgoa_ref, dgob_ref, dgom_ref):
        i = pl.program_id(0)

        @pl.when(i == 0)
        def _():
            for r in (loss_ref, dgp_ref, dbp_ref, dgoa_ref, dgob_ref, dgom_ref):
                r[...] = jnp.zeros_like(r)

        lane = lax.broadcasted_iota(jnp.int32, (1, LANES), 1)
        low = lane < 64
        xh0, _ = _ln_hat(x_ref[...])
        h = xh0 * ge_ref[...] + be_ref[...]

        ybp_v = yb_ref[...]
        yb = jnp.concatenate(
            [jnp.where(low, pltpu.roll(ybp_v[:, 2 * j * LANES:(2 * j + 1) * LANES], 64, 1),
                       ybp_v[:, (2 * j + 1) * LANES:(2 * j + 2) * LANES]) for j in range(4)], axis=1)

        def gated(raw, gate, gain, width):
            xh, r = _rms_hat(raw, width)
            n = xh * gain
            sg = 1.0 / (1.0 + jnp.exp(-gate))
            return xh, r, n, sg, n * (gate * sg)

        gate_a, gate_b, gate_m = ga_ref[...], gb_ref[...], gm_ref[...]
        xh_a, r_a, n_a, sg_a, y_a = gated(ya_ref[...], gate_a, goa_ref[...], A_WIDTH)
        xh_b, r_b, n_b, sg_b, y_b = gated(yb, gate_b, gob_ref[...], 512)
        xh_m, r_m, n_m, sg_m, y_m = gated(ym_ref[...], gate_m, gom_ref[...], 512)
        y = jnp.concatenate([y_a, y_b, y_m], axis=1).astype(BF16)
        y_ref[...] = y
        z = DEEPNORM_ALPHA * h + _dot(y, wo_ref[...])
        zh, rstd = _ln_hat(z)
        err = zh * gp_ref[...] + bp_ref[...] - tg_ref[...]
        rows = jnp.sum(err * err, axis=1, keepdims=True)
        loss_ref[...] += jnp.broadcast_to(jnp.sum(rows, axis=0, keepdims=True) * (0.5 / D_MODEL), (1, LANES))
        dout = err * (1.0 / D_MODEL)
        dgp_ref[...] += _colsum(dout * zh)
        dbp_ref[...] += _colsum(dout)
        dz = _ln_bwd_rows(dout * gp_ref[...], zh, rstd)
        dz_ref[...] = dz
        dy = _dot_nt(dz.astype(BF16), wo_ref[...])

        def gated_bwd(dyg, xh, r, n, sg, gate, gain, width, dgain_ref):
            dn = dyg * (gate * sg)
            dgate = dyg * n * (sg * (1.0 + gate * (1.0 - sg)))
            dgain_ref[...] += _colsum(dn * xh)
            return _rms_bwd(dn * gain, xh, r, width), dgate

        dya, dgate_a = gated_bwd(dy[:, 0:1024], xh_a, r_a, n_a, sg_a, gate_a, goa_ref[...], A_WIDTH, dgoa_ref)
        dyb, dgate_b = gated_bwd(dy[:, 1024:1536], xh_b, r_b, n_b, sg_b, gate_b, gob_ref[...], 512, dgob_ref)
        dym, dgate_m = gated_bwd(dy[:, 1536:2048], xh_m, r_m, n_m, sg_m, gate_m, gom_ref[...], 512, dgom_ref)
        doa_ref[...] = dya.astype(BF16)
        dom_ref[...] = dym.astype(BF16)
        dga_ref[...] = dgate_a.astype(BF16)
        dgb_ref[...] = dgate_b.astype(BF16)
        dgm_ref[...] = dgate_m.astype(BF16)
        for j in range(4):
            blk = dyb[:, j * LANES:(j + 1) * LANES]
            dob_ref[:, 2 * j * LANES:(2 * j + 1) * LANES] = jnp.where(low, 0.0, pltpu.roll(blk, 64, 1)).astype(BF16)
            dob_ref[:, (2 * j + 1) * LANES:(2 * j + 2) * LANES] = jnp.where(low, 0.0, blk).astype(BF16)

    def col(width, idx):
        return pl.BlockSpec((tm, width), lambda i: (i, idx))

    def full(shape):
        return pl.BlockSpec(shape, lambda i: (0, 0))

    def acc(width):
        return jax.ShapeDtypeStruct((1, width), F32)

    return pl.pallas_call(
        body, name="post", grid=(t // tm,),
        out_shape=(jax.ShapeDtypeStruct((t, 2048), BF16), jax.ShapeDtypeStruct((t, 1024), F32),
                   jax.ShapeDtypeStruct((t, 1024), BF16), jax.ShapeDtypeStruct((t, 1024), BF16),
                   jax.ShapeDtypeStruct((t, 512), BF16),
                   jax.ShapeDtypeStruct((t, 1024), BF16), jax.ShapeDtypeStruct((t, 512), BF16),
                   jax.ShapeDtypeStruct((t, 512), BF16),
                   acc(LANES), acc(1024), acc(1024), acc(1024), acc(512), acc(512)),
        in_specs=[col(1024, 0), col(1024, 0), col(1024, 0), col(512, 0),
                  col(1024, 3), col(512, COL_BG // 512), col(512, COL_MG // 512), col(1024, 0),
                  full((2048, 1024)),
                  full((1, 1024)), full((1, 1024)), full((1, 1024)), full((1, 512)), full((1, 512)),
                  full((1, 1024)), full((1, 1024))],
        out_specs=(col(2048, 0), col(1024, 0), col(1024, 0), col(1024, 0), col(512, 0),
                   col(1024, 0), col(512, 0), col(512, 0),
                   full((1, LANES)), full((1, 1024)), full((1, 1024)), full((1, 1024)), full((1, 512)),
                   full((1, 512))),
        compiler_params=_params(("arbitrary",)),
    )(x, ya, ybp, ym, proj, proj, proj, target, w_out, g_emb, b_emb, g_a, g_b, g_m, g_post, b_post)


def _prep_bwd(dqa, dka, dva, dqb, dkb, dvb, dqm, dga, dgb, dgm, proj, pos, w_uq, w_ukv, g_cq, g_ckv,
              rope_a, rope_b, tm=256):
    t = proj.shape[0]

    def body(dqa_ref, dka_ref, dva_ref, dqb_ref, dkb_ref, dvb_ref, dqm_ref, dga_ref, dgb_ref, dgm_ref,
             bs_ref, pos_ref, wuq_ref, wukv_ref, gcq_ref, gckv_ref, ra_ref, rb_ref,
             dproj_ref, dqf_ref, dkv_ref, dgcq_ref, dgckv_ref):
        i = pl.program_id(0)

        @pl.when(i == 0)
        def _():
            dgcq_ref[...] = jnp.zeros_like(dgcq_ref)
            dgckv_ref[...] = jnp.zeros_like(dgckv_ref)

        pos_c = pos_ref[...]
        ta = _rope_tables(pos_c, ra_ref[...])
        tb = _rope_tables(pos_c, rb_ref[...])
        for j in range(A_WIDTH // LANES):
            sl = slice(j * LANES, (j + 1) * LANES)
            dproj_ref[:, j * LANES:(j + 1) * LANES] = _rope(dqa_ref[:, sl], ta, 8, inverse=True).astype(BF16)
            dproj_ref[:, 1024 + j * LANES:1024 + (j + 1) * LANES] = (
                _rope(dka_ref[:, sl], ta, 8, inverse=True).astype(BF16))
        dproj_ref[:, 2048:3072] = dva_ref[...].astype(BF16)
        dproj_ref[:, 3072:4096] = dga_ref[...]

        lane = lax.broadcasted_iota(jnp.int32, (1, LANES), 1)
        low = lane < 64
        rope_lanes = (lane >= 64) & (lane < 96)
        dkr = jnp.zeros((tm, LANES), F32)
        for h in range(MLA_HEADS):
            sl = slice(h * LANES, (h + 1) * LANES)
            dqf_ref[:, sl] = _rope(dqb_ref[:, sl], tb, 16, inverse=True).astype(BF16)
            dk_h = dkb_ref[:, sl]
            dkv_ref[:, sl] = jnp.where(low, dk_h, dvb_ref[:, sl]).astype(BF16)
            dkr = dkr + jnp.where(rope_lanes, dk_h, 0.0)
        dkr = _rope(dkr, tb, 16, inverse=True)

        cq_hat, r_q = _rms_hat(bs_ref[:, 0:MLA_Q_RANK], MLA_Q_RANK)
        dcqn = _dot(dqf_ref[...], wuq_ref[...])
        dgcq_ref[...] += _colsum(dcqn * cq_hat)
        dproj_ref[:, COL_CQ:COL_CQ + 256] = _rms_bwd(dcqn * gcq_ref[...], cq_hat, r_q, MLA_Q_RANK).astype(BF16)
        ckv_hat, r_kv = _rms_hat(bs_ref[:, MLA_Q_RANK:MLA_Q_RANK + MLA_KV_RANK], MLA_KV_RANK)
        dckvn = _dot_nt(dkv_ref[...], wukv_ref[...])
        dgckv_ref[...] += _colsum(dckvn * ckv_hat)
        dproj_ref[:, COL_CQ + 256:COL_CQ + 384] = (
            _rms_bwd(dckvn * gckv_ref[...], ckv_hat, r_kv, MLA_KV_RANK).astype(BF16))
        dproj_ref[:, COL_CQ + 384:COL_CQ + 512] = dkr.astype(BF16)
        dproj_ref[:, COL_BG:COL_BG + 512] = dgb_ref[...]
        dproj_ref[:, COL_MQ:COL_MQ + 512] = dqm_ref[...].astype(BF16)
        dproj_ref[:, COL_MG:COL_MG + 512] = dgm_ref[...]

    def col(width, idx):
        return pl.BlockSpec((tm, width), lambda i: (i, idx))

    def full(shape):
        return pl.BlockSpec(shape, lambda i: (0, 0))

    return pl.pallas_call(
        body, name="prep_bwd", grid=(t // tm,),
        out_shape=(jax.ShapeDtypeStruct((t, PROJ_W), BF16), jax.ShapeDtypeStruct((t, 1024), BF16),
                   jax.ShapeDtypeStruct((t, 1024), BF16),
                   jax.ShapeDtypeStruct((1, MLA_Q_RANK), F32), jax.ShapeDtypeStruct((1, MLA_KV_RANK), F32)),
        in_specs=[col(1024, 0)] * 6 + [col(512, 0), col(1024, 0), col(512, 0), col(512, 0),
                  col(512, COL_CQ // 512), pl.BlockSpec((tm, 1), lambda i: (i, 0)),
                  full((1024, MLA_Q_RANK)), full((MLA_KV_RANK, 1024)),
                  full((1, MLA_Q_RANK)), full((1, MLA_KV_RANK)), full((8, LANES)), full((8, LANES))],
        out_specs=(col(PROJ_W, 0), col(1024, 0), col(1024, 0), full((1, MLA_Q_RANK)), full((1, MLA_KV_RANK))),
        compiler_params=_params(("arbitrary",)),
    )(dqa, dka, dva, dqb, dkb, dvb, dqm, dga, dgb, dgm, proj, pos, w_uq, w_ukv, g_cq, g_ckv, rope_a, rope_b)


def _adamw(g, w, m, v, tr, name):
    r, cols = w.shape

    def body(g_ref, w_ref, m_ref, v_ref, d_ref, nm_ref, nv_ref):
        gv = g_ref[...]
        m_new = ADAM_B1 * m_ref[...] + (1.0 - ADAM_B1) * gv
        v_new = ADAM_B2 * v_ref[...] + (1.0 - ADAM_B2) * (gv * gv)
        m_hat = m_new / (1.0 - ADAM_B1 ** ADAM_STEP)
        v_hat = v_new / (1.0 - ADAM_B2 ** ADAM_STEP)
        d_ref[...] = -ADAM_LR * (m_hat / (jnp.sqrt(v_hat) + ADAM_EPS) + ADAM_WD * w_ref[...])
        nm_ref[...] = m_new
        nv_ref[...] = v_new

    tile = pl.BlockSpec((tr, cols), lambda i: (i, 0))
    shape = jax.ShapeDtypeStruct((r, cols), F32)
    return pl.pallas_call(
        body, name=name, grid=(r // tr,),
        out_shape=(shape,) * 3, in_specs=[tile] * 4, out_specs=(tile,) * 3,
        compiler_params=_params(("parallel",)),
    )(g, w, m, v)


def _core_sum(g, recv, core, rows, tr, name):
    cols = g.shape[2]
    nblk = rows // tr

    def body(c_ref, g_ref, r_ref, sf_ref, sb_ref):
        tot = g_ref[...] + r_ref[...]
        sf_ref[...] = tot
        sb_ref[...] = tot.astype(BF16)

    half = pl.BlockSpec((None, tr, cols), lambda j, i, c_ref: (j, i, 0))
    return pl.pallas_call(
        body, name=name,
        grid_spec=pltpu.PrefetchScalarGridSpec(
            num_scalar_prefetch=1, grid=(4, nblk),
            in_specs=[pl.BlockSpec((None, tr, cols), lambda j, i, c_ref: (j, c_ref[0] * nblk + i, 0)), half],
            out_specs=(half, half)),
        out_shape=(jax.ShapeDtypeStruct((4, rows, cols), F32), jax.ShapeDtypeStruct((4, rows, cols), BF16)),
        compiler_params=_params(("parallel", "parallel")),
    )(core, g, recv)


def _chip_sum(sf, recv, chip, rows, tr, name):
    cols = sf.shape[2]

    def body(me_ref, sf_ref, r_ref, out_ref):
        acc = sf_ref[...]
        for k in range(3):
            acc = acc + r_ref[k].astype(F32)
        out_ref[...] = acc

    return pl.pallas_call(
        body, name=name,
        grid_spec=pltpu.PrefetchScalarGridSpec(
            num_scalar_prefetch=1, grid=(rows // tr,),
            in_specs=[pl.BlockSpec((None, tr, cols), lambda i, me_ref: (me_ref[0], i, 0)),
                      pl.BlockSpec((3, tr, cols), lambda i, me_ref: (0, i, 0))],
            out_specs=pl.BlockSpec((tr, cols), lambda i, me_ref: (i, 0))),
        out_shape=jax.ShapeDtypeStruct((rows, cols), F32),
        compiler_params=_params(("parallel",)),
    )(chip, sf, recv)


def _position():
    return lax.axis_index("x"), lax.axis_index("y"), lax.axis_index("c")


def _gather_weights(w_in_b, rest_b):
    def body(in_ref, rest_ref, oin_ref, orest_ref, send_sems, recv_sems, local_sems):
        x, y, c = _position()
        me = 2 * x + y
        srcs = (in_ref, rest_ref)
        dsts = (oin_ref, orest_ref)
        local = [pltpu.make_async_copy(srcs[a], dsts[a].at[me], local_sems.at[a]) for a in range(2)]
        for cp in local:
            cp.start()

        def piece(a, chip, half):
            return dsts[a].at[chip, half]

        def from_chip(a, k):
            return pltpu.make_async_remote_copy(
                src_ref=srcs[a].at[c], dst_ref=piece(a, me, c),
                send_sem=send_sems.at[3 * a + k - 1], recv_sem=recv_sems.at[3 * a + k - 1],
                device_id=(x ^ (k >> 1), y ^ (k & 1), c), device_id_type=MESH)

        def arrived(a, k):
            return pltpu.make_async_remote_copy(
                src_ref=piece(a, me ^ k, c), dst_ref=piece(a, me ^ k, c),
                send_sem=send_sems.at[3 * a + k - 1], recv_sem=recv_sems.at[3 * a + k - 1],
                device_id=(x ^ (k >> 1), y ^ (k & 1), c), device_id_type=MESH)

        def to_sibling(a, k, half):
            return pltpu.make_async_remote_copy(
                src_ref=piece(a, me ^ k, half), dst_ref=piece(a, me ^ k, half),
                send_sem=send_sems.at[6 + 3 * a + k - 1], recv_sem=recv_sems.at[6 + 3 * a + k - 1],
                device_id=(x, y, 1 - c), device_id_type=MESH)

        pairs = [(a, k) for a in range(2) for k in (1, 2, 3)]
        sends = [from_chip(a, k) for a, k in pairs]
        for cp in sends:
            cp.start()
        passed = []
        for a, k in pairs:
            arrived(a, k).wait_recv()
            cp = to_sibling(a, k, c)
            cp.start()
            passed.append(cp)
        for a, k in pairs:
            to_sibling(a, k, 1 - c).wait_recv()
        for cp in sends + passed:
            cp.wait_send()
        for cp in local:
            cp.wait()

    return pl.pallas_call(
        body, name="gather_weights",
        out_shape=(jax.ShapeDtypeStruct((4,) + w_in_b.shape, BF16), jax.ShapeDtypeStruct((4,) + rest_b.shape, BF16)),
        in_specs=[IN_VMEM, IN_VMEM], out_specs=(ANY, ANY),
        scratch_shapes=[pltpu.SemaphoreType.DMA((12,)), pltpu.SemaphoreType.DMA((12,)), pltpu.SemaphoreType.DMA((2,))],
    )(w_in_b, rest_b)


def _send_other_half(g_in, g_rest):
    def body(gin_ref, grest_ref, rin_ref, rrest_ref, send_sems, recv_sems):
        x, y, c = _position()
        srcs = (gin_ref, grest_ref)
        dsts = (rin_ref, rrest_ref)
        copies = [pltpu.make_async_remote_copy(
            src_ref=srcs[a].at[:, 1 - c], dst_ref=dsts[a],
            send_sem=send_sems.at[a], recv_sem=recv_sems.at[a], device_id=(x, y, 1 - c), device_id_type=MESH)
            for a in range(2)]
        for cp in copies:
            cp.start()
        for cp in copies:
            cp.wait_recv()
        for cp in copies:
            cp.wait_send()

    return pl.pallas_call(
        body, name="send_other_half",
        out_shape=(jax.ShapeDtypeStruct((4, HALF_IN, 1024), F32),
                   jax.ShapeDtypeStruct((4, HALF_REST, 1024), F32)),
        in_specs=[ANY, ANY], out_specs=(ANY, ANY),
        scratch_shapes=[pltpu.SemaphoreType.DMA((2,)), pltpu.SemaphoreType.DMA((2,))],
    )(g_in, g_rest)


def _dh_scatter(dproj, w_in_arr_t, x, dz, g, sb_in, sb_rest, tm=1024, tk=1024):
    t, d = x.shape
    nk = dproj.shape[1] // tk
    ni = t // tm

    def body(dp_ref, w_ref, x_ref, dz_ref, g_ref, sbin_ref, sbrest_ref,
             dx_ref, dg_ref, db_ref, rin_ref, rrest_ref, acc_ref, send_sems, recv_sems):
        i = pl.program_id(0)
        kk = pl.program_id(1)
        px, py, pc = _position()
        me = 2 * px + py
        srcs = (sbin_ref, sbrest_ref)
        dsts = (rin_ref, rrest_ref)

        def copy(a, k):
            return pltpu.make_async_remote_copy(
                src_ref=srcs[a].at[me ^ k], dst_ref=dsts[a].at[k - 1],
                send_sem=send_sems.at[3 * a + k - 1], recv_sem=recv_sems.at[3 * a + k - 1],
                device_id=(px ^ (k >> 1), py ^ (k & 1), pc), device_id_type=MESH)

        pairs = [(a, k) for a in range(2) for k in (1, 2, 3)]

        @pl.when((i == 0) & (kk == 0))
        def _():
            dg_ref[...] = jnp.zeros_like(dg_ref)
            db_ref[...] = jnp.zeros_like(db_ref)
            for a, k in pairs:
                copy(a, k).start()

        part = _dot(dp_ref[...], w_ref[...])

        @pl.when(kk == 0)
        def _():
            acc_ref[...] = part

        @pl.when(kk > 0)
        def _():
            acc_ref[...] += part

        @pl.when(kk == nk - 1)
        def _():
            xh, rstd = _ln_hat(x_ref[...])
            dht = acc_ref[...] + DEEPNORM_ALPHA * dz_ref[...]
            dg_ref[...] += _colsum(dht * xh)
            db_ref[...] += _colsum(dht)
            dx_ref[...] = _ln_bwd_rows(dht * g_ref[...], xh, rstd)

        @pl.when((i == ni - 1) & (kk == nk - 1))
        def _():
            for a, k in pairs:
                copy(a, k).wait_recv()
            for a, k in pairs:
                copy(a, k).wait_send()

    tile = pl.BlockSpec((tm, d), lambda i, kk: (i, 0))
    row = pl.BlockSpec((1, d), lambda i, kk: (0, 0))
    return pl.pallas_call(
        body, name="dh_scatter", grid=(ni, nk),
        out_shape=(jax.ShapeDtypeStruct((t, d), F32), jax.ShapeDtypeStruct((1, d), F32),
                   jax.ShapeDtypeStruct((1, d), F32),
                   jax.ShapeDtypeStruct((3, HALF_IN, 1024), BF16),
                   jax.ShapeDtypeStruct((3, HALF_REST, 1024), BF16)),
        in_specs=[pl.BlockSpec((tm, tk), lambda i, kk: (i, kk)), pl.BlockSpec((tk, d), lambda i, kk: (kk, 0)),
                  tile, tile, row, ANY, ANY],
        out_specs=(tile, row, row, ANY, ANY),
        scratch_shapes=[pltpu.VMEM((tm, d), F32), pltpu.SemaphoreType.DMA((6,)), pltpu.SemaphoreType.DMA((6,))],
        compiler_params=_params(("arbitrary", "arbitrary")),
    )(dproj, w_in_arr_t, x, dz, g, sb_in, sb_rest)


def _join_halves(gh_in, gh_rest):
    def body(hin_ref, hrest_ref, oin_ref, orest_ref, send_sems, recv_sems, local_sems):
        x, y, c = _position()
        srcs = (hin_ref, hrest_ref)
        dsts = (oin_ref, orest_ref)

        def rows(a, half):
            return dsts[a].at[half]

        local = [pltpu.make_async_copy(srcs[a], rows(a, c), local_sems.at[a]) for a in range(2)]
        remote = [pltpu.make_async_remote_copy(
            src_ref=srcs[a], dst_ref=rows(a, c), send_sem=send_sems.at[a], recv_sem=recv_sems.at[a],
            device_id=(x, y, 1 - c), device_id_type=MESH) for a in range(2)]
        for cp in local + remote:
            cp.start()
        for a in range(2):
            pltpu.make_async_remote_copy(
                src_ref=srcs[a], dst_ref=rows(a, 1 - c), send_sem=send_sems.at[a], recv_sem=recv_sems.at[a],
                device_id=(x, y, 1 - c), device_id_type=MESH).wait_recv()
        for cp in remote:
            cp.wait_send()
        for cp in local:
            cp.wait()

    return pl.pallas_call(
        body, name="join_halves",
        out_shape=(jax.ShapeDtypeStruct((2, HALF_IN, 1024), F32),
                   jax.ShapeDtypeStruct((2, HALF_REST, 1024), F32)),
        in_specs=[IN_VMEM, IN_VMEM], out_specs=(ANY, ANY),
        scratch_shapes=[pltpu.SemaphoreType.DMA((2,)), pltpu.SemaphoreType.DMA((2,)), pltpu.SemaphoreType.DMA((2,))],
    )(gh_in, gh_rest)


def _allreduce_small(vec):
    def body(vec_ref, out_ref, all_ref, send_sems, recv_sems):
        x, y, c = _position()
        me = 4 * x + 2 * y + c
        all_ref[me] = vec_ref[...]

        def copy(k, slot):
            return pltpu.make_async_remote_copy(
                src_ref=vec_ref, dst_ref=all_ref.at[slot], send_sem=send_sems.at[k - 1], recv_sem=recv_sems.at[k - 1],
                device_id=(x ^ (k >> 2), y ^ ((k >> 1) & 1), c ^ (k & 1)), device_id_type=MESH)

        copies = [copy(k, me) for k in range(1, 8)]
        for cp in copies:
            cp.start()
        for k in range(1, 8):
            copy(k, me ^ k).wait_recv()
        for cp in copies:
            cp.wait_send()
        total = all_ref[0]
        for d in range(1, 8):
            total = total + all_ref[d]
        out_ref[...] = total

    return pl.pallas_call(
        body, name="allreduce_small",
        out_shape=jax.ShapeDtypeStruct(vec.shape, vec.dtype),
        in_specs=[pl.BlockSpec(memory_space=pltpu.VMEM)], out_specs=pl.BlockSpec(memory_space=pltpu.VMEM),
        scratch_shapes=[pltpu.VMEM((8,) + vec.shape, vec.dtype), pltpu.SemaphoreType.DMA((7,)),
                        pltpu.SemaphoreType.DMA((7,))],
    )(vec)


def _pack_rest(w_uq, w_ukv, w_mem, w_out):
    rows = jnp.concatenate([w_uq[0].T.reshape(-1, 1024), w_ukv.reshape(-1, 1024), w_mem.reshape(-1, 1024),
                            w_out.reshape(-1, 1024)], axis=0)
    return jnp.pad(rows, ((0, ROWS_REST - ROWS_USED), (0, 0)))


def _unpack_rest(p):
    uq = p[0:ROWS_UQ].reshape(192, 256).T[None]
    o = ROWS_UQ
    out = [uq]
    for rows, shape in ((ROWS_UKV, (1, 128, 256)), (ROWS_MEM, (1, 256, 1024)), (ROWS_OUT, (1, 512, 1024))):
        out.append(p[o:o + rows].reshape(shape))
        o += rows
    return out


def _full_weights(g_in, g_rest):
    z = functools.partial(jnp.zeros, dtype=g_in.dtype)
    cut = 4480 - 2 * SHARD_ROWS
    w_in_arr_t = jnp.concatenate(
        [g_in[0, :SHARD_ROWS], g_in[1, :SHARD_ROWS], g_in[2, :cut], z((64, 1024)), g_in[2, cut:cut + 32],
         z((32, 1024)), g_in[2, cut + 32:SHARD_ROWS], g_in[3, :SHARD_ROWS]], axis=0)
    w_uq_t = g_rest[:, 0:ROWS_UQ].reshape(768, 256)
    w_uq_pad_t = jnp.pad(w_uq_t.reshape(MLA_HEADS, MLA_QK_DIM, 256), ((0, 0), (0, 32), (0, 0))).reshape(1024, 256)
    w_ukv = jnp.concatenate([g_rest[j, ROWS_UQ:ROWS_UQ + ROWS_UKV].reshape(128, 256) for j in range(4)], axis=1)
    lo = ROWS_UQ + ROWS_UKV
    w_mem = g_rest[:, lo:lo + ROWS_MEM].reshape(4 * ROWS_MEM, 1024)
    w_out = g_rest[:, lo + ROWS_MEM:lo + ROWS_MEM + ROWS_OUT].reshape(4 * ROWS_OUT, 1024)
    return w_in_arr_t, w_uq_pad_t, w_ukv, w_mem, w_out


def _split_grads(dw_in_arr_t, dw_uq_pad_t, dw_ukv, dw_mem, dw_out):
    dw_in_t = jnp.concatenate([dw_in_arr_t[:4480], dw_in_arr_t[4544:4576], dw_in_arr_t[4608:]], axis=0)
    g_in = jnp.pad(dw_in_t.reshape(4, SHARD_ROWS, 1024), ((0, 0), (0, ROWS_IN - SHARD_ROWS), (0, 0)))
    dw_uq_t = dw_uq_pad_t.reshape(MLA_HEADS, LANES, 256)[:, :MLA_QK_DIM].reshape(4, ROWS_UQ, 1024)
    parts = [dw_uq_t, dw_ukv.reshape(128, 4, 256).transpose(1, 0, 2).reshape(4, ROWS_UKV, 1024),
             dw_mem.reshape(4, ROWS_MEM, 1024), dw_out.reshape(4, ROWS_OUT, 1024)]
    g_rest = jnp.pad(jnp.concatenate(parts, axis=1), ((0, 0), (0, ROWS_REST - ROWS_USED), (0, 0)))
    return g_in, g_rest


def _rope_consts(rot, first, period):
    half = rot // 2
    inv_freq = ROPE_THETA ** (-(jnp.arange(0, rot, 2, dtype=F32) / rot))
    lane = jnp.arange(LANES) % period - first
    in_rot = (lane >= 0) & (lane < rot)
    freq = jnp.where(in_rot, inv_freq[jnp.clip(lane, 0, rot - 1) % half], 0.0)
    lo = (in_rot & (lane < half)).astype(F32)
    hi = (in_rot & (lane >= half)).astype(F32)
    return jnp.concatenate([freq[None], lo[None], hi[None], jnp.zeros((5, LANES), F32)], axis=0)


def _dilated_bias(s, bq):
    delta = jnp.arange(2 * s - bq)[None, :] - (s - bq) - jnp.arange(bq)[:, None]
    count = jnp.zeros(delta.shape, F32)
    for reach, dil in DILATED:
        count = count + ((jnp.abs(delta) <= reach) & (delta % dil == 0)).astype(F32)
    return jnp.where(count > 0, jnp.log(jnp.maximum(count, 1.0)) * LOG2E, NEG_INF)


def _forward_backward(x, mem, positions, target, weights, gains):
    w_in_arr_t, w_uq_pad_t, w_ukv, w_mem, w_out = weights
    g_emb, b_emb, g_cq, g_ckv, g_out_a, g_out_b, g_out_m, g_post, b_post = gains
    nb, s, d = x.shape
    t = nb * s
    x2 = x.reshape(t, d)
    mem2 = mem.reshape(nb * N_MEM, d)
    tgt2 = target.reshape(t, d)
    pos = positions.reshape(t, 1).astype(F32)
    rope_a = _rope_consts(16, 0, 64)
    rope_b = _rope_consts(32, 64, 128)
    bq_a = 256
    bias = _dilated_bias(s, bq_a)
    scales = (0.125, MLA_QK_DIM ** -0.5, 128 ** -0.5)

    h = _ln_fwd(x2, g_emb, b_emb)
    proj = _mm(h, w_in_arr_t, F32, 1024, 1024, 1024, "in_proj", mode="nt")
    qa, ka, va, qb, kb, vb, qm, cqn, ckvn = _prep(proj, pos, w_uq_pad_t, w_ukv, g_cq, g_ckv, rope_a, rope_b, scales)
    mkv = _mm(mem2, w_mem, BF16, nb * N_MEM, 1024, 1024, "mem_kv")

    cfg_a = dict(nb=nb, s=s, sk=s, groups=8, gpb=1, hp=2, qoff=0, koff=0, voff=0, bq=bq_a)
    cfg_b = dict(nb=nb, s=s, sk=s, groups=8, gpb=2, hp=1, qoff=0, koff=0, voff=0, bq=256)
    cfg_m = dict(nb=nb, s=s, sk=N_MEM, groups=4, gpb=1, hp=1, qoff=0, koff=0, voff=4, bq=512)
    ya, lse_a = _attn_fwd(qa, ka, va, bias, name="attn_a_fwd", **cfg_a)
    yb, lse_b = _attn_fwd(qb, kb, vb, None, name="attn_b_fwd", **cfg_b)
    ym, lse_m = _attn_fwd(qm, mkv, mkv, None, name="attn_m_fwd", **cfg_m)

    (y, dz, doa, dob, dom, dga, dgb, dgm, loss, dg_post, db_post, dg_a, dg_b, dg_m) = _post(
        x2, ya, yb, ym, proj, tgt2, w_out, g_emb, b_emb, g_out_a, g_out_b, g_out_m, g_post, b_post)

    dqa, dka, dva = _attn_bwd(qa, ka, va, ya, doa, lse_a, bias, name="attn_a_bwd", scale=scales[0], **cfg_a)
    dqb, dkb, dvb = _attn_bwd(qb, kb, vb, yb, dob, lse_b, None, name="attn_b_bwd", scale=scales[1], **cfg_b)
    dqm, dmk, dmv = _attn_bwd(qm, mkv, mkv, ym, dom, lse_m, None, name="attn_m_bwd", scale=scales[2], **cfg_m)
    dmkv = jnp.concatenate([dmk, dmv], axis=1)

    dproj, dqf, dkv, dg_cq, dg_ckv = _prep_bwd(
        dqa, dka, dva, dqb, dkb, dvb, dqm, dga, dgb, dgm, proj, pos, w_uq_pad_t, w_ukv, g_cq, g_ckv, rope_a, rope_b)

    dw_in_arr = _mm(dproj, h, F32, 1024, 1024, 1024, "dw_in", mode="tn")
    dw_out = _mm(y, dz, F32, 1024, 1024, 1024, "dw_out", mode="tn")
    dw_uq_pad = _mm(dqf, cqn, F32, 1024, 256, 1024, "dw_uq", mode="tn")
    dw_ukv = _mm(ckvn, dkv, F32, 128, 1024, 1024, "dw_ukv", mode="tn")
    dw_mem = _mm(mem2, dmkv, F32, 1024, 1024, nb * N_MEM, "dw_mem", mode="tn")
    small_rows = (dg_cq, dg_ckv, loss, dg_a, dg_b, dg_m, dg_post, db_post)
    return (dw_in_arr, dw_uq_pad, dw_ukv, dw_mem, dw_out), (dproj, x2, dz), small_rows


def _small_block(dg_emb, db_emb, small_rows):
    dg_cq, dg_ckv, loss, dg_a, dg_b, dg_m, dg_post, db_post = small_rows
    row2 = jnp.concatenate([dg_cq, dg_ckv, loss, jnp.zeros((1, 512), F32)], axis=1)
    return jnp.concatenate([dg_emb, db_emb, row2, dg_a, jnp.concatenate([dg_b, dg_m], axis=1), dg_post, db_post,
                            jnp.zeros((1, 1024), F32)], axis=0)


def _pack_small(g_emb, b_emb, g_cq, g_ckv, g_out_a, g_out_b, g_out_m, g_post, b_post):
    row2 = jnp.concatenate([g_cq.reshape(1, -1), g_ckv.reshape(1, -1), jnp.zeros((1, 640), F32)], axis=1)
    return jnp.concatenate([g_emb.reshape(1, -1), b_emb.reshape(1, -1), row2, g_out_a.reshape(1, -1),
                            jnp.concatenate([g_out_b.reshape(1, -1), g_out_m.reshape(1, -1)], axis=1),
                            g_post.reshape(1, -1), b_post.reshape(1, -1), jnp.zeros((1, 1024), F32)], axis=0)


def _unpack_small(p):
    return [p[0], p[1], p[2:3, 0:256], p[2:3, 256:384], p[3:4], p[4:5, 0:512], p[4:5, 512:1024], p[5:6], p[6:7]]


def kernel(x, mem, positions, g_emb, b_emb, w_in, g_cq, g_ckv, w_uq, w_ukv, w_mem_kv, g_out_a, g_out_b, g_out_m, w_out, g_post, b_post, loss_target, m_g_emb, m_b_emb, m_w_in, m_g_cq, m_g_ckv, m_w_uq, m_w_ukv, m_w_mem_kv, m_g_out_a, m_g_out_b, m_g_out_m, m_w_out, m_g_post, m_b_post, v_g_emb, v_b_emb, v_w_in, v_g_cq, v_g_ckv, v_w_uq, v_w_ukv, v_w_mem_kv, v_g_out_a, v_g_out_b, v_g_out_m, v_w_out, v_g_post, v_b_post):
    w_rest = _pack_rest(w_uq, w_ukv, w_mem_kv, w_out)
    w_in_t = w_in[0].T
    w_in_b = jnp.pad(w_in_t.astype(BF16), ((0, ROWS_IN - SHARD_ROWS), (0, 0)))
    gathered_in, gathered_rest = _gather_weights(w_in_b.reshape(2, HALF_IN, 1024),
                                                 w_rest.astype(BF16).reshape(2, HALF_REST, 1024))
    weights = _full_weights(gathered_in.reshape(4, ROWS_IN, 1024), gathered_rest.reshape(4, ROWS_REST, 1024))
    gains = (g_emb.reshape(1, -1), b_emb.reshape(1, -1), g_cq, g_ckv, g_out_a, g_out_b, g_out_m, g_post, b_post)
    dws, (dproj, x2, dz), small_rows = _forward_backward(x, mem, positions, loss_target, weights, gains)

    core = lax.axis_index("c").astype(jnp.int32).reshape(1)
    chip = (2 * lax.axis_index("x") + lax.axis_index("y")).astype(jnp.int32).reshape(1)
    g_in, g_rest = _split_grads(*dws)
    r_in, r_rest = _send_other_half(g_in.reshape(4, 2, HALF_IN, 1024), g_rest.reshape(4, 2, HALF_REST, 1024))
    sf_in, sb_in = _core_sum(g_in, r_in, core, HALF_IN, HALF_IN // 2, "core_sum_in")
    sf_rest, sb_rest = _core_sum(g_rest, r_rest, core, HALF_REST, HALF_REST, "core_sum_rest")
    grad_x, dg_emb, db_emb, rb_in, rb_rest = _dh_scatter(dproj, weights[0], x2, dz, gains[0], sb_in, sb_rest)
    gh_in = _chip_sum(sf_in, rb_in, chip, HALF_IN, HALF_IN // 2, "chip_sum_in")
    gh_rest = _chip_sum(sf_rest, rb_rest, chip, HALF_REST, HALF_REST, "chip_sum_rest")
    grad_in, grad_rest = _join_halves(gh_in, gh_rest)
    grad_in = grad_in.reshape(ROWS_IN, 1024)
    grad_rest = grad_rest.reshape(ROWS_REST, 1024)

    d_in, m_in, v_in = _adamw(grad_in, w_in_t, m_w_in[0].T, v_w_in[0].T, SHARD_ROWS // 3, "adamw_in")
    d_rest, m_rest, v_rest = _adamw(
        grad_rest, w_rest, _pack_rest(m_w_uq, m_w_ukv, m_w_mem_kv, m_w_out),
        _pack_rest(v_w_uq, v_w_ukv, v_w_mem_kv, v_w_out), HALF_REST, "adamw_rest")
    small_sum = _allreduce_small(_small_block(dg_emb, db_emb, small_rows))
    d_sm, m_sm, v_sm = _adamw(
        small_sum,
        _pack_small(g_emb, b_emb, g_cq, g_ckv, g_out_a, g_out_b, g_out_m, g_post, b_post),
        _pack_small(m_g_emb, m_b_emb, m_g_cq, m_g_ckv, m_g_out_a, m_g_out_b, m_g_out_m, m_g_post, m_b_post),
        _pack_small(v_g_emb, v_b_emb, v_g_cq, v_g_ckv, v_g_out_a, v_g_out_b, v_g_out_m, v_g_post, v_b_post),
        SMALL_ROWS, "adamw_small")
    loss = small_sum[2, 384]

    def ordered(big_in, rest, sm):
        b_uq, b_ukv, b_mem, b_out = _unpack_rest(rest)
        s_gemb, s_bemb, s_gcq, s_gckv, s_ga, s_gb, s_gm, s_gpost, s_bpost = _unpack_small(sm)
        return [s_gemb, s_bemb, big_in[:SHARD_ROWS].T[None], s_gcq, s_gckv, b_uq, b_ukv, b_mem, s_ga, s_gb, s_gm,
                b_out, s_gpost, s_bpost]

    return (loss, grad_x.reshape(x.shape), *ordered(grad_in, grad_rest, small_sum), *ordered(d_in, d_rest, d_sm),
            *ordered(m_in, m_rest, m_sm), *ordered(v_in, v_rest, v_sm))
```

```python
import functools
import math

import jax
import jax.numpy as jnp
from jax import lax
from jax.experimental import pallas as pl
from jax.experimental.pallas import tpu as pltpu

F32 = jnp.float32
BF16 = jnp.bfloat16
MESH = pl.DeviceIdType.MESH
ANY = pl.BlockSpec(memory_space=pl.ANY)
IN_VMEM = pl.BlockSpec(memory_space=pltpu.VMEM)

D_MODEL = 1024
A_WIDTH = 1024
MLA_HEADS = 8
MLA_Q_RANK = 256
MLA_KV_RANK = 128
MLA_QK_DIM = 96
MEM_WIDTH = 512
N_MEM = 256
ROPE_THETA = 500000.0
NORM_EPS = 1e-5
NEG_INF = -1e30
DEEPNORM_ALPHA = 2.0 ** 0.25
DILATED = ((64, 1), (256, 4), (1024, 16))

ADAM_LR = 0.001
ADAM_B1 = 0.9
ADAM_B2 = 0.999
ADAM_EPS = 1e-08
ADAM_WD = 0.01
ADAM_STEP = 10

LANES = 128
VMEM_LIMIT = 56 * 1024 * 1024
LOG2E = math.log2(math.e)
LN2 = math.log(2.0)

PROJ_W = 6144
COL_CQ = 4096
COL_BG = 4608
COL_MQ = 5120
COL_MG = 5632

SHARD_ROWS = 1512
ROWS_IN = 1536
ROWS_UQ, ROWS_UKV, ROWS_MEM, ROWS_OUT = 48, 32, 256, 512
ROWS_USED = ROWS_UQ + ROWS_UKV + ROWS_MEM + ROWS_OUT
ROWS_REST = 864
HALF_IN = ROWS_IN // 2
HALF_REST = ROWS_REST // 2
SMALL_ROWS = 8


def _params(sem=None, vmem=VMEM_LIMIT):
    return pltpu.CompilerParams(dimension_semantics=sem, vmem_limit_bytes=vmem)


def _dot(a, b):
    return jnp.dot(a, b, preferred_element_type=F32)


def _dot_nt(a, b):
    return lax.dot_general(a, b, (((1,), (1,)), ((), ())), preferred_element_type=F32)


def _dot_tn(a, b):
    return lax.dot_general(a, b, (((0,), (0,)), ((), ())), preferred_element_type=F32)


def _ln_hat(x):
    mu = jnp.mean(x, axis=-1, keepdims=True)
    xc = x - mu
    var = jnp.mean(xc * xc, axis=-1, keepdims=True)
    rstd = lax.rsqrt(var + NORM_EPS)
    return xc * rstd, rstd


def _ln_bwd_rows(dxh, xh, rstd):
    return rstd * (dxh - jnp.mean(dxh, axis=-1, keepdims=True) - xh * jnp.mean(dxh * xh, axis=-1, keepdims=True))


def _rms_hat(x, width):
    ms = jnp.sum(x * x, axis=-1, keepdims=True) * (1.0 / width)
    r = lax.rsqrt(ms + NORM_EPS)
    return x * r, r


def _rms_bwd(u, xh, r, width):
    return r * (u - xh * (jnp.sum(u * xh, axis=-1, keepdims=True) * (1.0 / width)))


def _colsum(v):
    return jnp.sum(v, axis=0, keepdims=True)


def _rope_tables(pos, consts):
    ang = pos * consts[0:1, :]
    c = jnp.cos(ang)
    s = jnp.sin(ang)
    return c, s * consts[2:3, :], -s * consts[1:2, :]


def _rope(x, tables, half, inverse=False):
    c, s_up, s_dn = tables
    if inverse:
        s_up, s_dn = -s_up, -s_dn
    return x * c + pltpu.roll(x, half, 1) * s_up + pltpu.roll(x, LANES - half, 1) * s_dn


def _ln_fwd(x, g, b, tm=512):
    t, d = x.shape

    def body(x_ref, g_ref, b_ref, h_ref):
        xh, _ = _ln_hat(x_ref[...])
        h_ref[...] = (xh * g_ref[...] + b_ref[...]).astype(BF16)

    row = pl.BlockSpec((1, d), lambda i: (0, 0))
    return pl.pallas_call(
        body, name="ln_fwd", grid=(t // tm,),
        out_shape=jax.ShapeDtypeStruct((t, d), BF16),
        in_specs=[pl.BlockSpec((tm, d), lambda i: (i, 0)), row, row],
        out_specs=pl.BlockSpec((tm, d), lambda i: (i, 0)),
        compiler_params=_params(("parallel",)),
    )(x, g, b)


def _mm(a, b, out_dtype, tm, tn, tk, name, mode="nn"):
    if mode == "tn":
        k, m = a.shape
    else:
        m, k = a.shape
    n = b.shape[0] if mode == "nt" else b.shape[1]
    nk = k // tk

    def body(a_ref, b_ref, o_ref, acc_ref):
        av = a_ref[...].astype(BF16)
        bv = b_ref[...].astype(BF16)
        part = _dot_tn(av, bv) if mode == "tn" else _dot_nt(av, bv) if mode == "nt" else _dot(av, bv)
        if nk == 1:
            o_ref[...] = part.astype(out_dtype)
        else:
            kk = pl.program_id(2)

            @pl.when(kk == 0)
            def _():
                acc_ref[...] = part

            @pl.when(kk > 0)
            def _():
                acc_ref[...] += part

            @pl.when(kk == nk - 1)
            def _():
                o_ref[...] = acc_ref[...].astype(out_dtype)

    a_spec = (pl.BlockSpec((tk, tm), lambda j, i, kk: (kk, i)) if mode == "tn"
              else pl.BlockSpec((tm, tk), lambda j, i, kk: (i, kk)))
    b_spec = (pl.BlockSpec((tn, tk), lambda j, i, kk: (j, kk)) if mode == "nt"
              else pl.BlockSpec((tk, tn), lambda j, i, kk: (kk, j)))
    return pl.pallas_call(
        body, name=name, grid=(n // tn, m // tm, nk),
        out_shape=jax.ShapeDtypeStruct((m, n), out_dtype),
        in_specs=[a_spec, b_spec],
        out_specs=pl.BlockSpec((tm, tn), lambda j, i, kk: (i, j)),
        scratch_shapes=[pltpu.VMEM((tm, tn), F32)],
        compiler_params=_params(("parallel", "parallel", "arbitrary")),
    )(a, b)


def _prep(proj, pos, w_uq, w_ukv, g_cq, g_ckv, rope_a, rope_b, scales, tm=256):
    t = proj.shape[0]
    sc_a, sc_b, sc_m = (s * LOG2E for s in scales)

    def body(aq_ref, ak_ref, av_ref, bs_ref, mq_ref, pos_ref, wuq_ref, wukv_ref, gcq_ref, gckv_ref,
             ra_ref, rb_ref, qa_ref, ka_ref, va_ref, qb_ref, kb_ref, vb_ref, qm_ref, cqn_ref, ckvn_ref):
        pos_c = pos_ref[...]
        ta = _rope_tables(pos_c, ra_ref[...])
        tb = _rope_tables(pos_c, rb_ref[...])
        for j in range(A_WIDTH // LANES):
            sl = slice(j * LANES, (j + 1) * LANES)
            qa_ref[:, sl] = (_rope(aq_ref[:, sl], ta, 8) * sc_a).astype(BF16)
            ka_ref[:, sl] = _rope(ak_ref[:, sl], ta, 8).astype(BF16)
        va_ref[...] = av_ref[...].astype(BF16)
        qm_ref[...] = (mq_ref[...] * sc_m).astype(BF16)

        cq_hat, _ = _rms_hat(bs_ref[:, 0:MLA_Q_RANK], MLA_Q_RANK)
        cqn = (cq_hat * gcq_ref[...]).astype(BF16)
        cqn_ref[...] = cqn
        ckv_hat, _ = _rms_hat(bs_ref[:, MLA_Q_RANK:MLA_Q_RANK + MLA_KV_RANK], MLA_KV_RANK)
        ckvn = (ckv_hat * gckv_ref[...]).astype(BF16)
        ckvn_ref[...] = ckvn
        qfull = _dot_nt(cqn, wuq_ref[...])
        kv = _dot(ckvn, wukv_ref[...])
        kr = _rope(bs_ref[:, 384:512], tb, 16)
        lane = lax.broadcasted_iota(jnp.int32, (1, LANES), 1)
        low = lane < 64
        for h in range(MLA_HEADS):
            sl = slice(h * LANES, (h + 1) * LANES)
            qb_ref[:, sl] = (_rope(qfull[:, sl], tb, 16) * sc_b).astype(BF16)
            kb_ref[:, sl] = jnp.where(low, kv[:, sl], kr).astype(BF16)
            vb_ref[:, sl] = jnp.where(low, 0.0, kv[:, sl]).astype(BF16)

    def col(width, idx):
        return pl.BlockSpec((tm, width), lambda i: (i, idx))

    def full(shape):
        return pl.BlockSpec(shape, lambda i: (0, 0))

    wide = jax.ShapeDtypeStruct((t, 1024), BF16)
    return pl.pallas_call(
        body, name="prep", grid=(t // tm,),
        out_shape=(wide, wide, wide, wide, wide, wide,
                   jax.ShapeDtypeStruct((t, MEM_WIDTH), BF16),
                   jax.ShapeDtypeStruct((t, MLA_Q_RANK), BF16),
                   jax.ShapeDtypeStruct((t, MLA_KV_RANK), BF16)),
        in_specs=[col(1024, 0), col(1024, 1), col(1024, 2), col(512, COL_CQ // 512), col(512, COL_MQ // 512),
                  pl.BlockSpec((tm, 1), lambda i: (i, 0)),
                  full((1024, MLA_Q_RANK)), full((MLA_KV_RANK, 1024)),
                  full((1, MLA_Q_RANK)), full((1, MLA_KV_RANK)), full((8, LANES)), full((8, LANES))],
        out_specs=(col(1024, 0),) * 6 + (col(MEM_WIDTH, 0), col(MLA_Q_RANK, 0), col(MLA_KV_RANK, 0)),
        compiler_params=_params(("parallel",)),
    )(proj, proj, proj, proj, proj, pos, w_uq, w_ukv, g_cq, g_ckv, rope_a, rope_b)


def _bias_spec(bias, bq, sk, nq, order):
    shape = (pl.Element(bq), pl.Element(sk))
    if order == "big":
        return pl.BlockSpec(shape, lambda b, i, g: (0, (nq - 1 - i) * bq))
    return pl.BlockSpec(shape, lambda b, g, i: (0, (nq - 1 - i) * bq))


def _attn_fwd(q, k, v, bias, *, nb, s, sk, groups, gpb, hp, qoff, koff, voff, bq, name):
    nq = s // bq
    hw = LANES // hp
    width = gpb * LANES

    def body(*refs):
        if bias is None:
            q_ref, k_ref, v_ref, o_ref, lse_ref = refs
        else:
            q_ref, k_ref, v_ref, bias_ref, o_ref, lse_ref = refs
        lane = lax.broadcasted_iota(jnp.int32, (1, LANES), 1)
        for gi in range(gpb):
            sl = slice(gi * LANES, (gi + 1) * LANES)
            qf = q_ref[:, sl]
            kk = k_ref[:, sl]
            vv = v_ref[:, sl]
            o_all = None
            lse_all = None
            for h in range(hp):
                mask = (lane >= h * hw) & (lane < (h + 1) * hw)
                qh = jnp.where(mask, qf, jnp.zeros_like(qf)) if hp > 1 else qf
                vh = jnp.where(mask, vv, jnp.zeros_like(vv)) if hp > 1 else vv
                sc = _dot_nt(qh, kk)
                if bias is not None:
                    sc = sc + bias_ref[...]
                m = jnp.max(sc, axis=1, keepdims=True)
                p = jnp.exp2(sc - m)
                l = jnp.sum(p, axis=1, keepdims=True)
                o = _dot(p.astype(BF16), vh) / l
                lse = jnp.broadcast_to(m + jnp.log(l) * LOG2E, (bq, LANES))
                o_all = o if h == 0 else o_all + o
                lse_all = lse if h == 0 else jnp.where(mask, lse, lse_all)
            o_ref[:, sl] = o_all
            lse_ref[:, sl] = lse_all

    in_specs = [pl.BlockSpec((bq, width), lambda b, i, g: (b * nq + i, qoff + g)),
                pl.BlockSpec((sk, width), lambda b, i, g: (b, koff + g)),
                pl.BlockSpec((sk, width), lambda b, i, g: (b, voff + g))]
    args = [q, k, v]
    if bias is not None:
        in_specs.append(_bias_spec(bias, bq, sk, nq, "big"))
        args.append(bias)
    out = jax.ShapeDtypeStruct((nb * s, groups * LANES), F32)
    ospec = pl.BlockSpec((bq, width), lambda b, i, g: (b * nq + i, g))
    return pl.pallas_call(
        body, name=name, grid=(nb, nq, groups // gpb),
        out_shape=(out, out), in_specs=in_specs, out_specs=(ospec, ospec),
        compiler_params=_params(("parallel", "parallel", "parallel")),
    )(*args)


def _attn_bwd(q, k, v, o, do, lse, bias, *, nb, s, sk, groups, gpb, hp, scale, qoff, koff, voff, bq, name):
    nq = s // bq
    hw = LANES // hp
    width = gpb * LANES

    def body(*refs):
        if bias is None:
            q_ref, k_ref, v_ref, o_ref, do_ref, lse_ref, dq_ref, dk_ref, dv_ref = refs
        else:
            q_ref, k_ref, v_ref, o_ref, do_ref, lse_ref, bias_ref, dq_ref, dk_ref, dv_ref = refs
        i = pl.program_id(2)

        @pl.when(i == 0)
        def _():
            dk_ref[...] = jnp.zeros_like(dk_ref)
            dv_ref[...] = jnp.zeros_like(dv_ref)

        lane = lax.broadcasted_iota(jnp.int32, (1, LANES), 1)
        for gi in range(gpb):
            sl = slice(gi * LANES, (gi + 1) * LANES)
            qf = q_ref[:, sl]
            kk = k_ref[:, sl]
            vv = v_ref[:, sl]
            dof = do_ref[:, sl]
            prod = dof.astype(F32) * o_ref[:, sl]
            lse = lse_ref[:, sl]
            dq_all = None
            for h in range(hp):
                mask = (lane >= h * hw) & (lane < (h + 1) * hw)
                if hp > 1:
                    qh = jnp.where(mask, qf, jnp.zeros_like(qf))
                    doh = jnp.where(mask, dof, jnp.zeros_like(dof))
                    delta = jnp.sum(jnp.where(mask, prod, 0.0), axis=1, keepdims=True)
                else:
                    qh, doh = qf, dof
                    delta = jnp.sum(prod, axis=1, keepdims=True)
                sc = _dot_nt(qh, kk)
                if bias is not None:
                    sc = sc + bias_ref[...]
                p = jnp.exp2(sc - lse[:, h * hw:h * hw + 1])
                dp = _dot_nt(doh, vv)
                ds = (p * (dp - delta)).astype(BF16)
                dq = _dot(ds, kk) * scale
                dq_all = jnp.where(mask, dq, 0.0 if h == 0 else dq_all) if hp > 1 else dq
                dk_ref[:, sl] += _dot_tn(ds, qh)
                dv_ref[:, sl] += _dot_tn(p.astype(BF16), doh)
            dq_ref[:, sl] = dq_all

        @pl.when(i == nq - 1)
        def _():
            dk_ref[...] = dk_ref[...] * LN2

    in_specs = [pl.BlockSpec((bq, width), lambda b, g, i: (b * nq + i, qoff + g)),
                pl.BlockSpec((sk, width), lambda b, g, i: (b, koff + g)),
                pl.BlockSpec((sk, width), lambda b, g, i: (b, voff + g)),
                pl.BlockSpec((bq, width), lambda b, g, i: (b * nq + i, g)),
                pl.BlockSpec((bq, width), lambda b, g, i: (b * nq + i, g)),
                pl.BlockSpec((bq, width), lambda b, g, i: (b * nq + i, g))]
    args = [q, k, v, o, do, lse]
    if bias is not None:
        in_specs.append(_bias_spec(bias, bq, sk, nq, "small"))
        args.append(bias)
    dq_shape = jax.ShapeDtypeStruct((nb * s, groups * LANES), F32)
    dkv_shape = jax.ShapeDtypeStruct((nb * sk, groups * LANES), F32)
    kv_spec = pl.BlockSpec((sk, width), lambda b, g, i: (b, g))
    return pl.pallas_call(
        body, name=name, grid=(nb, groups // gpb, nq),
        out_shape=(dq_shape, dkv_shape, dkv_shape), in_specs=in_specs,
        out_specs=(pl.BlockSpec((bq, width), lambda b, g, i: (b * nq + i, g)), kv_spec, kv_spec),
        compiler_params=_params(("parallel", "parallel", "arbitrary")),
    )(*args)


BAND_Q = 128
BAND_WIN = 256


def _band_start(i, s):
    return min(max(i * BAND_Q - 64, 0), s - BAND_WIN)


def _to_pattern_order(src_ref, dst_ref, stage_ref, s, d):
    length = s // d
    stage_ref[...] = src_ref[...].astype(F32)
    for r in range(d):
        dst_ref[r * length:(r + 1) * length, :] = stage_ref[pl.ds(r, length, stride=d), :].astype(dst_ref.dtype)


def _dilated_fwd(q, k, v, bias, *, nb, s, name):
    nblk = s // BAND_Q
    npat = len(DILATED)

    def body(q_ref, k_ref, v_ref, bias_ref, o_ref, lse_ref, stage_ref, qp_ref, kp_ref, vp_ref, op_ref, lp_ref,
             on_ref, ln_ref):
        lane = lax.broadcasted_iota(jnp.int32, (1, LANES), 1)
        first = lane < 64
        for p, (_, d) in enumerate(DILATED):
            if d == 1:
                qs, ks, vs = q_ref, k_ref, v_ref
            else:
                for src, dst in ((q_ref, qp_ref), (k_ref, kp_ref), (v_ref, vp_ref)):
                    _to_pattern_order(src, dst, stage_ref, s, d)
                qs, ks, vs = qp_ref, kp_ref, vp_ref
            for i in range(nblk):
                u0 = i * BAND_Q
                st = _band_start(i, s)
                qi = qs[u0:u0 + BAND_Q, :]
                kw = ks[st:st + BAND_WIN, :]
                vw = vs[st:st + BAND_WIN, :]
                zero = jnp.zeros_like(qi)
                q2 = jnp.concatenate([jnp.where(first, qi, zero), jnp.where(first, zero, qi)], axis=0)
                sc = _dot_nt(q2, kw)
                b = bias_ref[p, i]
                halves = []
                for h in range(2):
                    sh = sc[h * BAND_Q:(h + 1) * BAND_Q] + b
                    m = jnp.max(sh, axis=1, keepdims=True)
                    pr = jnp.exp2(sh - m)
                    l = jnp.sum(pr, axis=1, keepdims=True)
                    halves.append((pr.astype(BF16), l, m + jnp.log(l) * LOG2E))
                o2 = _dot(jnp.concatenate([halves[0][0], halves[1][0]], axis=0), vw)
                o_blk = jnp.where(first, o2[:BAND_Q] / halves[0][1], o2[BAND_Q:] / halves[1][1])
                lse_blk = jnp.where(first, jnp.broadcast_to(halves[0][2], (BAND_Q, LANES)),
                                    jnp.broadcast_to(halves[1][2], (BAND_Q, LANES)))
                op_ref[p, u0:u0 + BAND_Q, :] = o_blk
                lp_ref[p, u0:u0 + BAND_Q, :] = lse_blk
            if d > 1:
                length = s // d
                for r in range(d):
                    on_ref.at[p - 1][pl.ds(r, length, stride=d), :] = op_ref[p, r * length:(r + 1) * length, :]
                    ln_ref.at[p - 1][pl.ds(r, length, stride=d), :] = lp_ref[p, r * length:(r + 1) * length, :]
        lses = [lp_ref[0]] + [ln_ref[p] for p in range(npat - 1)]
        outs = [op_ref[0]] + [on_ref[p] for p in range(npat - 1)]
        m = functools.reduce(jnp.maximum, lses)
        ws = [jnp.exp2(l - m) for l in lses]
        den = functools.reduce(lambda a, c: a + c, ws)
        o_ref[...] = functools.reduce(lambda a, c: a + c, [w * o for w, o in zip(ws, outs)]) / den
        lse_ref[...] = m + jnp.log(den) * LOG2E

    blk = pl.BlockSpec((s, LANES), lambda b, g: (b, g))
    out = jax.ShapeDtypeStruct((nb * s, A_WIDTH), F32)
    return pl.pallas_call(
        body, name=name, grid=(nb, A_WIDTH // LANES),
        out_shape=(out, out),
        in_specs=[blk, blk, blk, pl.BlockSpec(bias.shape, lambda b, g: (0, 0, 0, 0))],
        out_specs=(blk, blk),
        scratch_shapes=[pltpu.VMEM((s, LANES), F32), pltpu.VMEM((s, LANES), BF16), pltpu.VMEM((s, LANES), BF16),
                        pltpu.VMEM((s, LANES), BF16), pltpu.VMEM((npat, s, LANES), F32),
                        pltpu.VMEM((npat, s, LANES), F32), pltpu.VMEM((npat - 1, s, LANES), F32),
                        pltpu.VMEM((npat - 1, s, LANES), F32)],
        compiler_params=_params(("parallel", "parallel")),
    )(q, k, v, bias)


def _dilated_bwd(q, k, v, o, do, lse, bias, *, nb, s, scale, name):
    nblk = s // BAND_Q
    npat = len(DILATED)

    def body(q_ref, k_ref, v_ref, o_ref, do_ref, lse_ref, bias_ref, dq_ref, dk_ref, dv_ref,
             stage_ref, dl_ref, qp_ref, kp_ref, vp_ref, dop_ref, lsp_ref, dlp_ref, dqp_ref, dkp_ref, dvp_ref):
        lane = lax.broadcasted_iota(jnp.int32, (1, LANES), 1)
        first = lane < 64
        prod = do_ref[...].astype(F32) * o_ref[...]
        d0 = jnp.sum(jnp.where(first, prod, 0.0), axis=1, keepdims=True)
        d1 = jnp.sum(jnp.where(first, 0.0, prod), axis=1, keepdims=True)
        dl_ref[...] = jnp.where(first, jnp.broadcast_to(d0, (s, LANES)), jnp.broadcast_to(d1, (s, LANES)))
        for p, (_, d) in enumerate(DILATED):
            length = s // d
            if d == 1:
                qs, ks, vs, dos, lss, dls = q_ref, k_ref, v_ref, do_ref, lse_ref, dl_ref
                dqs, dks, dvs = dq_ref, dk_ref, dv_ref
            else:
                for src, dst in ((q_ref, qp_ref), (k_ref, kp_ref), (v_ref, vp_ref), (do_ref, dop_ref),
                                 (lse_ref, lsp_ref), (dl_ref, dlp_ref)):
                    _to_pattern_order(src, dst, stage_ref, s, d)
                qs, ks, vs, dos, lss, dls = qp_ref, kp_ref, vp_ref, dop_ref, lsp_ref, dlp_ref
                dqs, dks, dvs = dqp_ref, dkp_ref, dvp_ref
            dks[...] = jnp.zeros((s, LANES), F32)
            dvs[...] = jnp.zeros((s, LANES), F32)
            for i in range(nblk):
                u0 = i * BAND_Q
                st = _band_start(i, s)
                qi = qs[u0:u0 + BAND_Q, :]
                doi = dos[u0:u0 + BAND_Q, :]
                kw = ks[st:st + BAND_WIN, :]
                vw = vs[st:st + BAND_WIN, :]
                zero = jnp.zeros_like(qi)
                q2 = jnp.concatenate([jnp.where(first, qi, zero), jnp.where(first, zero, qi)], axis=0)
                do2 = jnp.concatenate([jnp.where(first, doi, zero), jnp.where(first, zero, doi)], axis=0)
                sc = _dot_nt(q2, kw)
                dp = _dot_nt(do2, vw)
                b = bias_ref[p, i]
                lse_i = lss[u0:u0 + BAND_Q, :]
                dl_i = dls[u0:u0 + BAND_Q, :]
                ps, dss = [], []
                for h in range(2):
                    rows = slice(h * BAND_Q, (h + 1) * BAND_Q)
                    pr = jnp.exp2(sc[rows] + b - lse_i[:, 64 * h:64 * h + 1])
                    ps.append(pr.astype(BF16))
                    dss.append((pr * (dp[rows] - dl_i[:, 64 * h:64 * h + 1])).astype(BF16))
                p2 = jnp.concatenate(ps, axis=0)
                ds2 = jnp.concatenate(dss, axis=0)
                dq2 = _dot(ds2, kw)
                dqs[u0:u0 + BAND_Q, :] = jnp.where(first, dq2[:BAND_Q], dq2[BAND_Q:]) * scale
                dks[st:st + BAND_WIN, :] += _dot_tn(ds2, q2)
                dvs[st:st + BAND_WIN, :] += _dot_tn(p2, do2)
            if d > 1:
                for dst, src in ((dq_ref, dqp_ref), (dk_ref, dkp_ref), (dv_ref, dvp_ref)):
                    for r in range(d):
                        dst[pl.ds(r, length, stride=d), :] += src[r * length:(r + 1) * length, :]
        dk_ref[...] = dk_ref[...] * LN2

    blk = pl.BlockSpec((s, LANES), lambda b, g: (b, g))
    out = jax.ShapeDtypeStruct((nb * s, A_WIDTH), F32)
    f32_buf = pltpu.VMEM((s, LANES), F32)
    bf_buf = pltpu.VMEM((s, LANES), BF16)
    return pl.pallas_call(
        body, name=name, grid=(nb, A_WIDTH // LANES),
        out_shape=(out, out, out),
        in_specs=[blk] * 6 + [pl.BlockSpec(bias.shape, lambda b, g: (0, 0, 0, 0))],
        out_specs=(blk, blk, blk),
        scratch_shapes=[f32_buf, f32_buf, bf_buf, bf_buf, bf_buf, bf_buf, f32_buf, f32_buf, f32_buf, f32_buf, f32_buf],
        compiler_params=_params(("parallel", "parallel")),
    )(q, k, v, o, do, lse, bias)


def _post(x, ya, ybp, ym, proj, target, w_out, g_emb, b_emb, g_a, g_b, g_m, g_post, b_post, tm=256):
    t = x.shape[0]

    def body(x_ref, ya_ref, yb_ref, ym_ref, ga_ref, gb_ref, gm_ref, tg_ref, wo_ref,
             ge_ref, be_ref, goa_ref, gob_ref, gom_ref, gp_ref, bp_ref,
             y_ref, dz_ref, doa_ref, dob_ref, dom_ref, dga_ref, dgb_ref, dgm_ref,
             loss_ref, dgp_ref, dbp_ref, dgoa_ref, dgob_ref, dgom_ref):
        i = pl.program_id(0)

        @pl.when(i == 0)
        def _():
            for r in (loss_ref, dgp_ref, dbp_ref, dgoa_ref, dgob_ref, dgom_ref):
                r[...] = jnp.zeros_like(r)

        lane = lax.broadcasted_iota(jnp.int32, (1, LANES), 1)
        low = lane < 64
        xh0, _ = _ln_hat(x_ref[...])
        h = xh0 * ge_ref[...] + be_ref[...]

        ybp_v = yb_ref[...]
        yb = jnp.concatenate(
            [jnp.where(low, pltpu.roll(ybp_v[:, 2 * j * LANES:(2 * j + 1) * LANES], 64, 1),
                       ybp_v[:, (2 * j + 1) * LANES:(2 * j + 2) * LANES]) for j in range(4)], axis=1)

        def gated(raw, gate, gain, width):
            xh, r = _rms_hat(raw, width)
            n = xh * gain
            sg = 1.0 / (1.0 + jnp.exp(-gate))
            return xh, r, n, sg, n * (gate * sg)

        gate_a, gate_b, gate_m = ga_ref[...], gb_ref[...], gm_ref[...]
        xh_a, r_a, n_a, sg_a, y_a = gated(ya_ref[...], gate_a, goa_ref[...], A_WIDTH)
        xh_b, r_b, n_b, sg_b, y_b = gated(yb, gate_b, gob_ref[...], 512)
        xh_m, r_m, n_m, sg_m, y_m = gated(ym_ref[...], gate_m, gom_ref[...], 512)
        y = jnp.concatenate([y_a, y_b, y_m], axis=1).astype(BF16)
        y_ref[...] = y
        z = DEEPNORM_ALPHA * h + _dot(y, wo_ref[...])
        zh, rstd = _ln_hat(z)
        err = zh * gp_ref[...] + bp_ref[...] - tg_ref[...]
        rows = jnp.sum(err * err, axis=1, keepdims=True)
        loss_ref[...] += jnp.broadcast_to(jnp.sum(rows, axis=0, keepdims=True) * (0.5 / D_MODEL), (1, LANES))
        dout = err * (1.0 / D_MODEL)
        dgp_ref[...] += _colsum(dout * zh)
        dbp_ref[...] += _colsum(dout)
        dz = _ln_bwd_rows(dout * gp_ref[...], zh, rstd)
        dz_ref[...] = dz
        dy = _dot_nt(dz.astype(BF16), wo_ref[...])

        def gated_bwd(dyg, xh, r, n, sg, gate, gain, width, dgain_ref):
            dn = dyg * (gate * sg)
            dgate = dyg * n * (sg * (1.0 + gate * (1.0 - sg)))
            dgain_ref[...] += _colsum(dn * xh)
            return _rms_bwd(dn * gain, xh, r, width), dgate

        dya, dgate_a = gated_bwd(dy[:, 0:1024], xh_a, r_a, n_a, sg_a, gate_a, goa_ref[...], A_WIDTH, dgoa_ref)
        dyb, dgate_b = gated_bwd(dy[:, 1024:1536], xh_b, r_b, n_b, sg_b, gate_b, gob_ref[...], 512, dgob_ref)
        dym, dgate_m = gated_bwd(dy[:, 1536:2048], xh_m, r_m, n_m, sg_m, gate_m, gom_ref[...], 512, dgom_ref)
        doa_ref[...] = dya.astype(BF16)
        dom_ref[...] = dym.astype(BF16)
        dga_ref[...] = dgate_a.astype(BF16)
        dgb_ref[...] = dgate_b.astype(BF16)
        dgm_ref[...] = dgate_m.astype(BF16)
        for j in range(4):
            blk = dyb[:, j * LANES:(j + 1) * LANES]
            dob_ref[:, 2 * j * LANES:(2 * j + 1) * LANES] = jnp.where(low, 0.0, pltpu.roll(blk, 64, 1)).astype(BF16)
            dob_ref[:, (2 * j + 1) * LANES:(2 * j + 2) * LANES] = jnp.where(low, 0.0, blk).astype(BF16)

    def col(width, idx):
        return pl.BlockSpec((tm, width), lambda i: (i, idx))

    def full(shape):
        return pl.BlockSpec(shape, lambda i: (0, 0))

    def acc(width):
        return jax.ShapeDtypeStruct((1, width), F32)

    return pl.pallas_call(
        body, name="post", grid=(t // tm,),
        out_shape=(jax.ShapeDtypeStruct((t, 2048), BF16), jax.ShapeDtypeStruct((t, 1024), F32),
                   jax.ShapeDtypeStruct((t, 1024), BF16), jax.ShapeDtypeStruct((t, 1024), BF16),
                   jax.ShapeDtypeStruct((t, 512), BF16),
                   jax.ShapeDtypeStruct((t, 1024), BF16), jax.ShapeDtypeStruct((t, 512), BF16),
                   jax.ShapeDtypeStruct((t, 512), BF16),
                   acc(LANES), acc(1024), acc(1024), acc(1024), acc(512), acc(512)),
        in_specs=[col(1024, 0), col(1024, 0), col(1024, 0), col(512, 0),
                  col(1024, 3), col(512, COL_BG // 512), col(512, COL_MG // 512), col(1024, 0),
                  full((2048, 1024)),
                  full((1, 1024)), full((1, 1024)), full((1, 1024)), full((1, 512)), full((1, 512)),
                  full((1, 1024)), full((1, 1024))],
        out_specs=(col(2048, 0), col(1024, 0), col(1024, 0), col(1024, 0), col(512, 0),
                   col(1024, 0), col(512, 0), col(512, 0),
                   full((1, LANES)), full((1, 1024)), full((1, 1024)), full((1, 1024)), full((1, 512)),
                   full((1, 512))),
        compiler_params=_params(("arbitrary",)),
    )(x, ya, ybp, ym, proj, proj, proj, target, w_out, g_emb, b_emb, g_a, g_b, g_m, g_post, b_post)


def _prep_bwd(dqa, dka, dva, dqb, dkb, dvb, dqm, dga, dgb, dgm, proj, pos, w_uq, w_ukv, g_cq, g_ckv,
              rope_a, rope_b, tm=256):
    t = proj.shape[0]

    def body(dqa_ref, dka_ref, dva_ref, dqb_ref, dkb_ref, dvb_ref, dqm_ref, dga_ref, dgb_ref, dgm_ref,
             bs_ref, pos_ref, wuq_ref, wukv_ref, gcq_ref, gckv_ref, ra_ref, rb_ref,
             dproj_ref, dqf_ref, dkv_ref, dgcq_ref, dgckv_ref):
        i = pl.program_id(0)

        @pl.when(i == 0)
        def _():
            dgcq_ref[...] = jnp.zeros_like(dgcq_ref)
            dgckv_ref[...] = jnp.zeros_like(dgckv_ref)

        pos_c = pos_ref[...]
        ta = _rope_tables(pos_c, ra_ref[...])
        tb = _rope_tables(pos_c, rb_ref[...])
        for j in range(A_WIDTH // LANES):
            sl = slice(j * LANES, (j + 1) * LANES)
            dproj_ref[:, j * LANES:(j + 1) * LANES] = _rope(dqa_ref[:, sl], ta, 8, inverse=True).astype(BF16)
            dproj_ref[:, 1024 + j * LANES:1024 + (j + 1) * LANES] = (
                _rope(dka_ref[:, sl], ta, 8, inverse=True).astype(BF16))
        dproj_ref[:, 2048:3072] = dva_ref[...].astype(BF16)
        dproj_ref[:, 3072:4096] = dga_ref[...]

        lane = lax.broadcasted_iota(jnp.int32, (1, LANES), 1)
        low = lane < 64
        rope_lanes = (lane >= 64) & (lane < 96)
        dkr = jnp.zeros((tm, LANES), F32)
        for h in range(MLA_HEADS):
            sl = slice(h * LANES, (h + 1) * LANES)
            dqf_ref[:, sl] = _rope(dqb_ref[:, sl], tb, 16, inverse=True).astype(BF16)
            dk_h = dkb_ref[:, sl]
            dkv_ref[:, sl] = jnp.where(low, dk_h, dvb_ref[:, sl]).astype(BF16)
            dkr = dkr + jnp.where(rope_lanes, dk_h, 0.0)
        dkr = _rope(dkr, tb, 16, inverse=True)

        cq_hat, r_q = _rms_hat(bs_ref[:, 0:MLA_Q_RANK], MLA_Q_RANK)
        dcqn = _dot(dqf_ref[...], wuq_ref[...])
        dgcq_ref[...] += _colsum(dcqn * cq_hat)
        dproj_ref[:, COL_CQ:COL_CQ + 256] = _rms_bwd(dcqn * gcq_ref[...], cq_hat, r_q, MLA_Q_RANK).astype(BF16)
        ckv_hat, r_kv = _rms_hat(bs_ref[:, MLA_Q_RANK:MLA_Q_RANK + MLA_KV_RANK], MLA_KV_RANK)
        dckvn = _dot_nt(dkv_ref[...], wukv_ref[...])
        dgckv_ref[...] += _colsum(dckvn * ckv_hat)
        dproj_ref[:, COL_CQ + 256:COL_CQ + 384] = (
            _rms_bwd(dckvn * gckv_ref[...], ckv_hat, r_kv, MLA_KV_RANK).astype(BF16))
        dproj_ref[:, COL_CQ + 384:COL_CQ + 512] = dkr.astype(BF16)
        dproj_ref[:, COL_BG:COL_BG + 512] = dgb_ref[...]
        dproj_ref[:, COL_MQ:COL_MQ + 512] = dqm_ref[...].astype(BF16)
        dproj_ref[:, COL_MG:COL_MG + 512] = dgm_ref[...]

    def col(width, idx):
        return pl.BlockSpec((tm, width), lambda i: (i, idx))

    def full(shape):
        return pl.BlockSpec(shape, lambda i: (0, 0))

    return pl.pallas_call(
        body, name="prep_bwd", grid=(t // tm,),
        out_shape=(jax.ShapeDtypeStruct((t, PROJ_W), BF16), jax.ShapeDtypeStruct((t, 1024), BF16),
                   jax.ShapeDtypeStruct((t, 1024), BF16),
                   jax.ShapeDtypeStruct((1, MLA_Q_RANK), F32), jax.ShapeDtypeStruct((1, MLA_KV_RANK), F32)),
        in_specs=[col(1024, 0)] * 6 + [col(512, 0), col(1024, 0), col(512, 0), col(512, 0),
                  col(512, COL_CQ // 512), pl.BlockSpec((tm, 1), lambda i: (i, 0)),
                  full((1024, MLA_Q_RANK)), full((MLA_KV_RANK, 1024)),
                  full((1, MLA_Q_RANK)), full((1, MLA_KV_RANK)), full((8, LANES)), full((8, LANES))],
        out_specs=(col(PROJ_W, 0), col(1024, 0), col(1024, 0), full((1, MLA_Q_RANK)), full((1, MLA_KV_RANK))),
        compiler_params=_params(("arbitrary",)),
    )(dqa, dka, dva, dqb, dkb, dvb, dqm, dga, dgb, dgm, proj, pos, w_uq, w_ukv, g_cq, g_ckv, rope_a, rope_b)


def _adamw(g, w, m, v, tr, name):
    r, cols = w.shape

    def body(g_ref, w_ref, m_ref, v_ref, d_ref, nm_ref, nv_ref):
        gv = g_ref[...]
        m_new = ADAM_B1 * m_ref[...] + (1.0 - ADAM_B1) * gv
        v_new = ADAM_B2 * v_ref[...] + (1.0 - ADAM_B2) * (gv * gv)
        m_hat = m_new / (1.0 - ADAM_B1 ** ADAM_STEP)
        v_hat = v_new / (1.0 - ADAM_B2 ** ADAM_STEP)
        d_ref[...] = -ADAM_LR * (m_hat / (jnp.sqrt(v_hat) + ADAM_EPS) + ADAM_WD * w_ref[...])
        nm_ref[...] = m_new
        nv_ref[...] = v_new

    tile = pl.BlockSpec((tr, cols), lambda i: (i, 0))
    shape = jax.ShapeDtypeStruct((r, cols), F32)
    return pl.pallas_call(
        body, name=name, grid=(r // tr,),
        out_shape=(shape,) * 3, in_specs=[tile] * 4, out_specs=(tile,) * 3,
        compiler_params=_params(("parallel",)),
    )(g, w, m, v)


def _core_sum(g, recv, core, rows, tr, name):
    cols = g.shape[2]
    nblk = rows // tr

    def body(c_ref, g_ref, r_ref, sf_ref, sb_ref):
        tot = g_ref[...] + r_ref[...]
        sf_ref[...] = tot
        sb_ref[...] = tot.astype(BF16)

    half = pl.BlockSpec((None, tr, cols), lambda j, i, c_ref: (j, i, 0))
    return pl.pallas_call(
        body, name=name,
        grid_spec=pltpu.PrefetchScalarGridSpec(
            num_scalar_prefetch=1, grid=(4, nblk),
            in_specs=[pl.BlockSpec((None, tr, cols), lambda j, i, c_ref: (j, c_ref[0] * nblk + i, 0)), half],
            out_specs=(half, half)),
        out_shape=(jax.ShapeDtypeStruct((4, rows, cols), F32), jax.ShapeDtypeStruct((4, rows, cols), BF16)),
        compiler_params=_params(("parallel", "parallel")),
    )(core, g, recv)


def _chip_sum(sf, recv, chip, rows, tr, name):
    cols = sf.shape[2]

    def body(me_ref, sf_ref, r_ref, out_ref):
        acc = sf_ref[...]
        for k in range(3):
            acc = acc + r_ref[k].astype(F32)
        out_ref[...] = acc

    return pl.pallas_call(
        body, name=name,
        grid_spec=pltpu.PrefetchScalarGridSpec(
            num_scalar_prefetch=1, grid=(rows // tr,),
            in_specs=[pl.BlockSpec((None, tr, cols), lambda i, me_ref: (me_ref[0], i, 0)),
                      pl.BlockSpec((3, tr, cols), lambda i, me_ref: (0, i, 0))],
            out_specs=pl.BlockSpec((tr, cols), lambda i, me_ref: (i, 0))),
        out_shape=jax.ShapeDtypeStruct((rows, cols), F32),
        compiler_params=_params(("parallel",)),
    )(chip, sf, recv)


def _position():
    return lax.axis_index("x"), lax.axis_index("y"), lax.axis_index("c")


def _gather_weights(w_in_b, rest_b):
    def body(in_ref, rest_ref, oin_ref, orest_ref, send_sems, recv_sems, local_sems):
        x, y, c = _position()
        me = 2 * x + y
        srcs = (in_ref, rest_ref)
        dsts = (oin_ref, orest_ref)
        local = [pltpu.make_async_copy(srcs[a], dsts[a].at[me], local_sems.at[a]) for a in range(2)]
        for cp in local:
            cp.start()

        def piece(a, chip, half):
            return dsts[a].at[chip, half]

        def from_chip(a, k):
            return pltpu.make_async_remote_copy(
                src_ref=srcs[a].at[c], dst_ref=piece(a, me, c),
                send_sem=send_sems.at[3 * a + k - 1], recv_sem=recv_sems.at[3 * a + k - 1],
                device_id=(x ^ (k >> 1), y ^ (k & 1), c), device_id_type=MESH)

        def arrived(a, k):
            return pltpu.make_async_remote_copy(
                src_ref=piece(a, me ^ k, c), dst_ref=piece(a, me ^ k, c),
                send_sem=send_sems.at[3 * a + k - 1], recv_sem=recv_sems.at[3 * a + k - 1],
                device_id=(x ^ (k >> 1), y ^ (k & 1), c), device_id_type=MESH)

        def to_sibling(a, k, half):
            return pltpu.make_async_remote_copy(
                src_ref=piece(a, me ^ k, half), dst_ref=piece(a, me ^ k, half),
                send_sem=send_sems.at[6 + 3 * a + k - 1], recv_sem=recv_sems.at[6 + 3 * a + k - 1],
                device_id=(x, y, 1 - c), device_id_type=MESH)

        pairs = [(a, k) for a in range(2) for k in (1, 2, 3)]
        sends = [from_chip(a, k) for a, k in pairs]
        for cp in sends:
            cp.start()
        passed = []
        for a, k in pairs:
            arrived(a, k).wait_recv()
            cp = to_sibling(a, k, c)
            cp.start()
            passed.append(cp)
        for a, k in pairs:
            to_sibling(a, k, 1 - c).wait_recv()
        for cp in sends + passed:
            cp.wait_send()
        for cp in local:
            cp.wait()

    return pl.pallas_call(
        body, name="gather_weights",
        out_shape=(jax.ShapeDtypeStruct((4,) + w_in_b.shape, BF16), jax.ShapeDtypeStruct((4,) + rest_b.shape, BF16)),
        in_specs=[IN_VMEM, IN_VMEM], out_specs=(ANY, ANY),
        scratch_shapes=[pltpu.SemaphoreType.DMA((12,)), pltpu.SemaphoreType.DMA((12,)), pltpu.SemaphoreType.DMA((2,))],
    )(w_in_b, rest_b)


def _send_other_half(g_in, g_rest):
    def body(gin_ref, grest_ref, rin_ref, rrest_ref, send_sems, recv_sems):
        x, y, c = _position()
        srcs = (gin_ref, grest_ref)
        dsts = (rin_ref, rrest_ref)
        copies = [pltpu.make_async_remote_copy(
            src_ref=srcs[a].at[:, 1 - c], dst_ref=dsts[a],
            send_sem=send_sems.at[a], recv_sem=recv_sems.at[a], device_id=(x, y, 1 - c), device_id_type=MESH)
            for a in range(2)]
        for cp in copies:
            cp.start()
        for cp in copies:
            cp.wait_recv()
        for cp in copies:
            cp.wait_send()

    return pl.pallas_call(
        body, name="send_other_half",
        out_shape=(jax.ShapeDtypeStruct((4, HALF_IN, 1024), F32),
                   jax.ShapeDtypeStruct((4, HALF_REST, 1024), F32)),
        in_specs=[ANY, ANY], out_specs=(ANY, ANY),
        scratch_shapes=[pltpu.SemaphoreType.DMA((2,)), pltpu.SemaphoreType.DMA((2,))],
    )(g_in, g_rest)


def _dh_scatter(dproj, w_in_arr_t, x, dz, g, sb_in, sb_rest, tm=1024, tk=1024):
    t, d = x.shape
    nk = dproj.shape[1] // tk
    ni = t // tm

    def body(dp_ref, w_ref, x_ref, dz_ref, g_ref, sbin_ref, sbrest_ref,
             dx_ref, dg_ref, db_ref, rin_ref, rrest_ref, acc_ref, send_sems, recv_sems):
        i = pl.program_id(0)
        kk = pl.program_id(1)
        px, py, pc = _position()
        me = 2 * px + py
        srcs = (sbin_ref, sbrest_ref)
        dsts = (rin_ref, rrest_ref)

        def copy(a, k):
            return pltpu.make_async_remote_copy(
                src_ref=srcs[a].at[me ^ k], dst_ref=dsts[a].at[k - 1],
                send_sem=send_sems.at[3 * a + k - 1], recv_sem=recv_sems.at[3 * a + k - 1],
                device_id=(px ^ (k >> 1), py ^ (k & 1), pc), device_id_type=MESH)

        pairs = [(a, k) for a in range(2) for k in (1, 2, 3)]

        @pl.when((i == 0) & (kk == 0))
        def _():
            dg_ref[...] = jnp.zeros_like(dg_ref)
            db_ref[...] = jnp.zeros_like(db_ref)
            for a, k in pairs:
                copy(a, k).start()

        part = _dot(dp_ref[...], w_ref[...])

        @pl.when(kk == 0)
        def _():
            acc_ref[...] = part

        @pl.when(kk > 0)
        def _():
            acc_ref[...] += part

        @pl.when(kk == nk - 1)
        def _():
            xh, rstd = _ln_hat(x_ref[...])
            dht = acc_ref[...] + DEEPNORM_ALPHA * dz_ref[...]
            dg_ref[...] += _colsum(dht * xh)
            db_ref[...] += _colsum(dht)
            dx_ref[...] = _ln_bwd_rows(dht * g_ref[...], xh, rstd)

        @pl.when((i == ni - 1) & (kk == nk - 1))
        def _():
            for a, k in pairs:
                copy(a, k).wait_recv()
            for a, k in pairs:
                copy(a, k).wait_send()

    tile = pl.BlockSpec((tm, d), lambda i, kk: (i, 0))
    row = pl.BlockSpec((1, d), lambda i, kk: (0, 0))
    return pl.pallas_call(
        body, name="dh_scatter", grid=(ni, nk),
        out_shape=(jax.ShapeDtypeStruct((t, d), F32), jax.ShapeDtypeStruct((1, d), F32),
                   jax.ShapeDtypeStruct((1, d), F32),
                   jax.ShapeDtypeStruct((3, HALF_IN, 1024), BF16),
                   jax.ShapeDtypeStruct((3, HALF_REST, 1024), BF16)),
        in_specs=[pl.BlockSpec((tm, tk), lambda i, kk: (i, kk)), pl.BlockSpec((tk, d), lambda i, kk: (kk, 0)),
                  tile, tile, row, ANY, ANY],
        out_specs=(tile, row, row, ANY, ANY),
        scratch_shapes=[pltpu.VMEM((tm, d), F32), pltpu.SemaphoreType.DMA((6,)), pltpu.SemaphoreType.DMA((6,))],
        compiler_params=_params(("arbitrary", "arbitrary")),
    )(dproj, w_in_arr_t, x, dz, g, sb_in, sb_rest)


def _join_halves(gh_in, gh_rest):
    def body(hin_ref, hrest_ref, oin_ref, orest_ref, send_sems, recv_sems, local_sems):
        x, y, c = _position()
        srcs = (hin_ref, hrest_ref)
        dsts = (oin_ref, orest_ref)

        def rows(a, half):
            return dsts[a].at[half]

        local = [pltpu.make_async_copy(srcs[a], rows(a, c), local_sems.at[a]) for a in range(2)]
        remote = [pltpu.make_async_remote_copy(
            src_ref=srcs[a], dst_ref=rows(a, c), send_sem=send_sems.at[a], recv_sem=recv_sems.at[a],
            device_id=(x, y, 1 - c), device_id_type=MESH) for a in range(2)]
        for cp in local + remote:
            cp.start()
        for a in range(2):
            pltpu.make_async_remote_copy(
                src_ref=srcs[a], dst_ref=rows(a, 1 - c), send_sem=send_sems.at[a], recv_sem=recv_sems.at[a],
                device_id=(x, y, 1 - c), device_id_type=MESH).wait_recv()
        for cp in remote:
            cp.wait_send()
        for cp in local:
            cp.wait()

    return pl.pallas_call(
        body, name="join_halves",
        out_shape=(jax.ShapeDtypeStruct((2, HALF_IN, 1024), F32),
                   jax.ShapeDtypeStruct((2, HALF_REST, 1024), F32)),
        in_specs=[IN_VMEM, IN_VMEM], out_specs=(ANY, ANY),
        scratch_shapes=[pltpu.SemaphoreType.DMA((2,)), pltpu.SemaphoreType.DMA((2,)), pltpu.SemaphoreType.DMA((2,))],
    )(gh_in, gh_rest)


def _allreduce_small(vec):
    def body(vec_ref, out_ref, all_ref, send_sems, recv_sems):
        x, y, c = _position()
        me = 4 * x + 2 * y + c
        all_ref[me] = vec_ref[...]

        def copy(k, slot):
            return pltpu.make_async_remote_copy(
                src_ref=vec_ref, dst_ref=all_ref.at[slot], send_sem=send_sems.at[k - 1], recv_sem=recv_sems.at[k - 1],
                device_id=(x ^ (k >> 2), y ^ ((k >> 1) & 1), c ^ (k & 1)), device_id_type=MESH)

        copies = [copy(k, me) for k in range(1, 8)]
        for cp in copies:
            cp.start()
        for k in range(1, 8):
            copy(k, me ^ k).wait_recv()
        for cp in copies:
            cp.wait_send()
        total = all_ref[0]
        for d in range(1, 8):
            total = total + all_ref[d]
        out_ref[...] = total

    return pl.pallas_call(
        body, name="allreduce_small",
        out_shape=jax.ShapeDtypeStruct(vec.shape, vec.dtype),
        in_specs=[pl.BlockSpec(memory_space=pltpu.VMEM)], out_specs=pl.BlockSpec(memory_space=pltpu.VMEM),
        scratch_shapes=[pltpu.VMEM((8,) + vec.shape, vec.dtype), pltpu.SemaphoreType.DMA((7,)),
                        pltpu.SemaphoreType.DMA((7,))],
    )(vec)


def _pack_rest(w_uq, w_ukv, w_mem, w_out):
    rows = jnp.concatenate([w_uq[0].T.reshape(-1, 1024), w_ukv.reshape(-1, 1024), w_mem.reshape(-1, 1024),
                            w_out.reshape(-1, 1024)], axis=0)
    return jnp.pad(rows, ((0, ROWS_REST - ROWS_USED), (0, 0)))


def _unpack_rest(p):
    uq = p[0:ROWS_UQ].reshape(192, 256).T[None]
    o = ROWS_UQ
    out = [uq]
    for rows, shape in ((ROWS_UKV, (1, 128, 256)), (ROWS_MEM, (1, 256, 1024)), (ROWS_OUT, (1, 512, 1024))):
        out.append(p[o:o + rows].reshape(shape))
        o += rows
    return out


def _full_weights(g_in, g_rest):
    z = functools.partial(jnp.zeros, dtype=g_in.dtype)
    cut = 4480 - 2 * SHARD_ROWS
    w_in_arr_t = jnp.concatenate(
        [g_in[0, :SHARD_ROWS], g_in[1, :SHARD_ROWS], g_in[2, :cut], z((64, 1024)), g_in[2, cut:cut + 32],
         z((32, 1024)), g_in[2, cut + 32:SHARD_ROWS], g_in[3, :SHARD_ROWS]], axis=0)
    w_uq_t = g_rest[:, 0:ROWS_UQ].reshape(768, 256)
    w_uq_pad_t = jnp.pad(w_uq_t.reshape(MLA_HEADS, MLA_QK_DIM, 256), ((0, 0), (0, 32), (0, 0))).reshape(1024, 256)
    w_ukv = jnp.concatenate([g_rest[j, ROWS_UQ:ROWS_UQ + ROWS_UKV].reshape(128, 256) for j in range(4)], axis=1)
    lo = ROWS_UQ + ROWS_UKV
    w_mem = g_rest[:, lo:lo + ROWS_MEM].reshape(4 * ROWS_MEM, 1024)
    w_out = g_rest[:, lo + ROWS_MEM:lo + ROWS_MEM + ROWS_OUT].reshape(4 * ROWS_OUT, 1024)
    return w_in_arr_t, w_uq_pad_t, w_ukv, w_mem, w_out


def _split_grads(dw_in_arr_t, dw_uq_pad_t, dw_ukv, dw_mem, dw_out):
    dw_in_t = jnp.concatenate([dw_in_arr_t[:4480], dw_in_arr_t[4544:4576], dw_in_arr_t[4608:]], axis=0)
    g_in = jnp.pad(dw_in_t.reshape(4, SHARD_ROWS, 1024), ((0, 0), (0, ROWS_IN - SHARD_ROWS), (0, 0)))
    dw_uq_t = dw_uq_pad_t.reshape(MLA_HEADS, LANES, 256)[:, :MLA_QK_DIM].reshape(4, ROWS_UQ, 1024)
    parts = [dw_uq_t, dw_ukv.reshape(128, 4, 256).transpose(1, 0, 2).reshape(4, ROWS_UKV, 1024),
             dw_mem.reshape(4, ROWS_MEM, 1024), dw_out.reshape(4, ROWS_OUT, 1024)]
    g_rest = jnp.pad(jnp.concatenate(parts, axis=1), ((0, 0), (0, ROWS_REST - ROWS_USED), (0, 0)))
    return g_in, g_rest


def _rope_consts(rot, first, period):
    half = rot // 2
    inv_freq = ROPE_THETA ** (-(jnp.arange(0, rot, 2, dtype=F32) / rot))
    lane = jnp.arange(LANES) % period - first
    in_rot = (lane >= 0) & (lane < rot)
    freq = jnp.where(in_rot, inv_freq[jnp.clip(lane, 0, rot - 1) % half], 0.0)
    lo = (in_rot & (lane < half)).astype(F32)
    hi = (in_rot & (lane >= half)).astype(F32)
    return jnp.concatenate([freq[None], lo[None], hi[None], jnp.zeros((5, LANES), F32)], axis=0)


def _band_bias(s):
    nblk = s // BAND_Q
    starts = jnp.array([_band_start(i, s) for i in range(nblk)], jnp.int32)
    uq = (jnp.arange(nblk)[:, None] * BAND_Q + jnp.arange(BAND_Q)[None, :])[:, :, None]
    uk = (starts[:, None] + jnp.arange(BAND_WIN)[None, :])[:, None, :]
    tiles = []
    for _, d in DILATED:
        length = s // d
        ok = (uq // length == uk // length) & (jnp.abs(uq - uk) <= 64)
        tiles.append(jnp.where(ok, 0.0, NEG_INF).astype(F32))
    return jnp.stack(tiles, axis=0)


def _forward_backward(x, mem, positions, target, weights, gains):
    w_in_arr_t, w_uq_pad_t, w_ukv, w_mem, w_out = weights
    g_emb, b_emb, g_cq, g_ckv, g_out_a, g_out_b, g_out_m, g_post, b_post = gains
    nb, s, d = x.shape
    t = nb * s
    x2 = x.reshape(t, d)
    mem2 = mem.reshape(nb * N_MEM, d)
    tgt2 = target.reshape(t, d)
    pos = positions.reshape(t, 1).astype(F32)
    rope_a = _rope_consts(16, 0, 64)
    rope_b = _rope_consts(32, 64, 128)
    bias = _band_bias(s)
    scales = (0.125, MLA_QK_DIM ** -0.5, 128 ** -0.5)

    h = _ln_fwd(x2, g_emb, b_emb)
    proj = _mm(h, w_in_arr_t, F32, 1024, 1024, 1024, "in_proj", mode="nt")
    qa, ka, va, qb, kb, vb, qm, cqn, ckvn = _prep(proj, pos, w_uq_pad_t, w_ukv, g_cq, g_ckv, rope_a, rope_b, scales)
    mkv = _mm(mem2, w_mem, BF16, nb * N_MEM, 1024, 1024, "mem_kv")

    cfg_b = dict(nb=nb, s=s, sk=s, groups=8, gpb=2, hp=1, qoff=0, koff=0, voff=0, bq=256)
    cfg_m = dict(nb=nb, s=s, sk=N_MEM, groups=4, gpb=1, hp=1, qoff=0, koff=0, voff=4, bq=512)
    ya, lse_a = _dilated_fwd(qa, ka, va, bias, nb=nb, s=s, name="attn_a_fwd")
    yb, lse_b = _attn_fwd(qb, kb, vb, None, name="attn_b_fwd", **cfg_b)
    ym, lse_m = _attn_fwd(qm, mkv, mkv, None, name="attn_m_fwd", **cfg_m)

    (y, dz, doa, dob, dom, dga, dgb, dgm, loss, dg_post, db_post, dg_a, dg_b, dg_m) = _post(
        x2, ya, yb, ym, proj, tgt2, w_out, g_emb, b_emb, g_out_a, g_out_b, g_out_m, g_post, b_post)

    dqa, dka, dva = _dilated_bwd(qa, ka, va, ya, doa, lse_a, bias, nb=nb, s=s, scale=scales[0], name="attn_a_bwd")
    dqb, dkb, dvb = _attn_bwd(qb, kb, vb, yb, dob, lse_b, None, name="attn_b_bwd", scale=scales[1], **cfg_b)
    dqm, dmk, dmv = _attn_bwd(qm, mkv, mkv, ym, dom, lse_m, None, name="attn_m_bwd", scale=scales[2], **cfg_m)
    dmkv = jnp.concatenate([dmk, dmv], axis=1)

    dproj, dqf, dkv, dg_cq, dg_ckv = _prep_bwd(
        dqa, dka, dva, dqb, dkb, dvb, dqm, dga, dgb, dgm, proj, pos, w_uq_pad_t, w_ukv, g_cq, g_ckv, rope_a, rope_b)

    dw_in_arr = _mm(dproj, h, F32, 1024, 1024, 1024, "dw_in", mode="tn")
    dw_out = _mm(y, dz, F32, 1024, 1024, 1024, "dw_out", mode="tn")
    dw_uq_pad = _mm(dqf, cqn, F32, 1024, 256, 1024, "dw_uq", mode="tn")
    dw_ukv = _mm(ckvn, dkv, F32, 128, 1024, 1024, "dw_ukv", mode="tn")
    dw_mem = _mm(mem2, dmkv, F32, 1024, 1024, nb * N_MEM, "dw_mem", mode="tn")
    small_rows = (dg_cq, dg_ckv, loss, dg_a, dg_b, dg_m, dg_post, db_post)
    return (dw_in_arr, dw_uq_pad, dw_ukv, dw_mem, dw_out), (dproj, x2, dz), small_rows


def _small_block(dg_emb, db_emb, small_rows):
    dg_cq, dg_ckv, loss, dg_a, dg_b, dg_m, dg_post, db_post = small_rows
    row2 = jnp.concatenate([dg_cq, dg_ckv, loss, jnp.zeros((1, 512), F32)], axis=1)
    return jnp.concatenate([dg_emb, db_emb, row2, dg_a, jnp.concatenate([dg_b, dg_m], axis=1), dg_post, db_post,
                            jnp.zeros((1, 1024), F32)], axis=0)


def _pack_small(g_emb, b_emb, g_cq, g_ckv, g_out_a, g_out_b, g_out_m, g_post, b_post):
    row2 = jnp.concatenate([g_cq.reshape(1, -1), g_ckv.reshape(1, -1), jnp.zeros((1, 640), F32)], axis=1)
    return jnp.concatenate([g_emb.reshape(1, -1), b_emb.reshape(1, -1), row2, g_out_a.reshape(1, -1),
                            jnp.concatenate([g_out_b.reshape(1, -1), g_out_m.reshape(1, -1)], axis=1),
                            g_post.reshape(1, -1), b_post.reshape(1, -1), jnp.zeros((1, 1024), F32)], axis=0)


def _unpack_small(p):
    return [p[0], p[1], p[2:3, 0:256], p[2:3, 256:384], p[3:4], p[4:5, 0:512], p[4:5, 512:1024], p[5:6], p[6:7]]


def kernel(x, mem, positions, g_emb, b_emb, w_in, g_cq, g_ckv, w_uq, w_ukv, w_mem_kv, g_out_a, g_out_b, g_out_m, w_out, g_post, b_post, loss_target, m_g_emb, m_b_emb, m_w_in, m_g_cq, m_g_ckv, m_w_uq, m_w_ukv, m_w_mem_kv, m_g_out_a, m_g_out_b, m_g_out_m, m_w_out, m_g_post, m_b_post, v_g_emb, v_b_emb, v_w_in, v_g_cq, v_g_ckv, v_w_uq, v_w_ukv, v_w_mem_kv, v_g_out_a, v_g_out_b, v_g_out_m, v_w_out, v_g_post, v_b_post):
    w_rest = _pack_rest(w_uq, w_ukv, w_mem_kv, w_out)
    w_in_t = w_in[0].T
    w_in_b = jnp.pad(w_in_t.astype(BF16), ((0, ROWS_IN - SHARD_ROWS), (0, 0)))
    gathered_in, gathered_rest = _gather_weights(w_in_b.reshape(2, HALF_IN, 1024),
                                                 w_rest.astype(BF16).reshape(2, HALF_REST, 1024))
    weights = _full_weights(gathered_in.reshape(4, ROWS_IN, 1024), gathered_rest.reshape(4, ROWS_REST, 1024))
    gains = (g_emb.reshape(1, -1), b_emb.reshape(1, -1), g_cq, g_ckv, g_out_a, g_out_b, g_out_m, g_post, b_post)
    dws, (dproj, x2, dz), small_rows = _forward_backward(x, mem, positions, loss_target, weights, gains)

    core = lax.axis_index("c").astype(jnp.int32).reshape(1)
    chip = (2 * lax.axis_index("x") + lax.axis_index("y")).astype(jnp.int32).reshape(1)
    g_in, g_rest = _split_grads(*dws)
    r_in, r_rest = _send_other_half(g_in.reshape(4, 2, HALF_IN, 1024), g_rest.reshape(4, 2, HALF_REST, 1024))
    sf_in, sb_in = _core_sum(g_in, r_in, core, HALF_IN, HALF_IN // 2, "core_sum_in")
    sf_rest, sb_rest = _core_sum(g_rest, r_rest, core, HALF_REST, HALF_REST, "core_sum_rest")
    grad_x, dg_emb, db_emb, rb_in, rb_rest = _dh_scatter(dproj, weights[0], x2, dz, gains[0], sb_in, sb_rest)
    gh_in = _chip_sum(sf_in, rb_in, chip, HALF_IN, HALF_IN // 2, "chip_sum_in")
    gh_rest = _chip_sum(sf_rest, rb_rest, chip, HALF_REST, HALF_REST, "chip_sum_rest")
    grad_in, grad_rest = _join_halves(gh_in, gh_rest)
    grad_in = grad_in.reshape(ROWS_IN, 1024)
    grad_rest = grad_rest.reshape(ROWS_REST, 1024)

    d_in, m_in, v_in = _adamw(grad_in, w_in_t, m_w_in[0].T, v_w_in[0].T, SHARD_ROWS // 3, "adamw_in")
    d_rest, m_rest, v_rest = _adamw(
        grad_rest, w_rest, _pack_rest(m_w_uq, m_w_ukv, m_w_mem_kv, m_w_out),
        _pack_rest(v_w_uq, v_w_ukv, v_w_mem_kv, v_w_out), HALF_REST, "adamw_rest")
    small_sum = _allreduce_small(_small_block(dg_emb, db_emb, small_rows))
    d_sm, m_sm, v_sm = _adamw(
        small_sum,
        _pack_small(g_emb, b_emb, g_cq, g_ckv, g_out_a, g_out_b, g_out_m, g_post, b_post),
        _pack_small(m_g_emb, m_b_emb, m_g_cq, m_g_ckv, m_g_out_a, m_g_out_b, m_g_out_m, m_g_post, m_b_post),
        _pack_small(v_g_emb, v_b_emb, v_g_cq, v_g_ckv, v_g_out_a, v_g_out_b, v_g_out_m, v_g_post, v_b_post),
        SMALL_ROWS, "adamw_small")
    loss = small_sum[2, 384]

    def ordered(big_in, rest, sm):
        b_uq, b_ukv, b_mem, b_out = _unpack_rest(rest)
        s_gemb, s_bemb, s_gcq, s_gckv, s_ga, s_gb, s_gm, s_gpost, s_bpost = _unpack_small(sm)
        return [s_gemb, s_bemb, big_in[:SHARD_ROWS].T[None], s_gcq, s_gckv, b_uq, b_ukv, b_mem, s_ga, s_gb, s_gm,
                b_out, s_gpost, s_bpost]

    return (loss, grad_x.reshape(x.shape), *ordered(grad_in, grad_rest, small_sum), *ordered(d_in, d_rest, d_sm),
            *ordered(m_in, m_rest, m_sm), *ordered(v_in, v_rest, v_sm))
```

```python
import functools
import math

import jax
import jax.numpy as jnp
import numpy as np
from jax import lax
from jax.experimental import pallas as pl
from jax.experimental.pallas import tpu as pltpu

F32 = jnp.float32
BF16 = jnp.bfloat16
MESH = pl.DeviceIdType.MESH
ANY = pl.BlockSpec(memory_space=pl.ANY)
IN_VMEM = pl.BlockSpec(memory_space=pltpu.VMEM)

D_MODEL = 1024
A_WIDTH = 1024
MLA_HEADS = 8
MLA_Q_RANK = 256
MLA_KV_RANK = 128
MLA_QK_DIM = 96
MEM_WIDTH = 512
N_MEM = 256
ROPE_THETA = 500000.0
NORM_EPS = 1e-5
NEG_INF = -1e30
DEEPNORM_ALPHA = 2.0 ** 0.25
DILATED = ((64, 1), (256, 4), (1024, 16))

ADAM_LR = 0.001
ADAM_B1 = 0.9
ADAM_B2 = 0.999
ADAM_EPS = 1e-08
ADAM_WD = 0.01
ADAM_STEP = 10

LANES = 128
VMEM_LIMIT = 56 * 1024 * 1024
LOG2E = math.log2(math.e)
LN2 = math.log(2.0)

PROJ_W = 6144
COL_CQ = 4096
COL_BG = 4608
COL_MQ = 5120
COL_MG = 5632

SHARD_ROWS = 1512
ROWS_IN = 1536
ROWS_UQ, ROWS_UKV, ROWS_MEM, ROWS_OUT = 48, 32, 256, 512
ROWS_USED = ROWS_UQ + ROWS_UKV + ROWS_MEM + ROWS_OUT
ROWS_REST = 864
HALF_IN = ROWS_IN // 2
HALF_REST = ROWS_REST // 2
SMALL_ROWS = 8


def _params(sem=None, vmem=VMEM_LIMIT):
    return pltpu.CompilerParams(dimension_semantics=sem, vmem_limit_bytes=vmem)


def _dot(a, b):
    return jnp.dot(a, b, preferred_element_type=F32)


def _dot_nt(a, b):
    return lax.dot_general(a, b, (((1,), (1,)), ((), ())), preferred_element_type=F32)


def _dot_tn(a, b):
    return lax.dot_general(a, b, (((0,), (0,)), ((), ())), preferred_element_type=F32)


def _ln_hat(x):
    mu = jnp.mean(x, axis=-1, keepdims=True)
    xc = x - mu
    var = jnp.mean(xc * xc, axis=-1, keepdims=True)
    rstd = lax.rsqrt(var + NORM_EPS)
    return xc * rstd, rstd


def _ln_bwd_rows(dxh, xh, rstd):
    return rstd * (dxh - jnp.mean(dxh, axis=-1, keepdims=True) - xh * jnp.mean(dxh * xh, axis=-1, keepdims=True))


def _rms_hat(x, width):
    ms = jnp.sum(x * x, axis=-1, keepdims=True) * (1.0 / width)
    r = lax.rsqrt(ms + NORM_EPS)
    return x * r, r


def _rms_bwd(u, xh, r, width):
    return r * (u - xh * (jnp.sum(u * xh, axis=-1, keepdims=True) * (1.0 / width)))


def _colsum(v):
    return jnp.sum(v, axis=0, keepdims=True)


def _rope_tables(pos, consts):
    ang = pos * consts[0:1, :]
    c = jnp.cos(ang)
    s = jnp.sin(ang)
    return c, s * consts[2:3, :], -s * consts[1:2, :]


def _rope(x, tables, half, inverse=False):
    c, s_up, s_dn = tables
    if inverse:
        s_up, s_dn = -s_up, -s_dn
    return x * c + pltpu.roll(x, half, 1) * s_up + pltpu.roll(x, LANES - half, 1) * s_dn


def _ln_fwd(x, g, b, tm=512):
    t, d = x.shape

    def body(x_ref, g_ref, b_ref, h_ref):
        xh, _ = _ln_hat(x_ref[...])
        h_ref[...] = (xh * g_ref[...] + b_ref[...]).astype(BF16)

    row = pl.BlockSpec((1, d), lambda i: (0, 0))
    return pl.pallas_call(
        body, name="ln_fwd", grid=(t // tm,),
        out_shape=jax.ShapeDtypeStruct((t, d), BF16),
        in_specs=[pl.BlockSpec((tm, d), lambda i: (i, 0)), row, row],
        out_specs=pl.BlockSpec((tm, d), lambda i: (i, 0)),
        compiler_params=_params(("parallel",)),
    )(x, g, b)


class _Ride:
    def __init__(self, args, out_shapes, ncopies, copies):
        self.args, self.out_shapes, self.ncopies, self.copies = list(args), list(out_shapes), ncopies, copies

    def scratch(self):
        return [pltpu.SemaphoreType.DMA((self.ncopies,)), pltpu.SemaphoreType.DMA((self.ncopies,))]

    def run(self, first, last, in_refs, out_refs, send_sems, recv_sems):
        @pl.when(first)
        def _():
            for cp in self.copies(in_refs, out_refs, send_sems, recv_sems):
                cp.start()

        @pl.when(last)
        def _():
            cps = self.copies(in_refs, out_refs, send_sems, recv_sems)
            for cp in cps:
                cp.wait_recv()
            for cp in cps:
                cp.wait_send()


def _mm(a, b, out_dtype, tm, tn, tk, name, mode="nn", ride=None):
    if mode == "tn":
        k, m = a.shape
    else:
        m, k = a.shape
    n = b.shape[0] if mode == "nt" else b.shape[1]
    nk = k // tk
    nj, ni = n // tn, m // tm
    n_in = len(ride.args) if ride else 0
    n_out = len(ride.out_shapes) if ride else 0

    def body(a_ref, b_ref, *rest):
        o_ref = rest[n_in]
        acc_ref = rest[n_in + 1 + n_out]
        if ride:
            j, i, kk = pl.program_id(0), pl.program_id(1), pl.program_id(2)
            ride.run((j == 0) & (i == 0) & (kk == 0), (j == nj - 1) & (i == ni - 1) & (kk == nk - 1),
                     rest[:n_in], rest[n_in + 1:n_in + 1 + n_out], rest[-2], rest[-1])
        av = a_ref[...].astype(BF16)
        bv = b_ref[...].astype(BF16)
        part = _dot_tn(av, bv) if mode == "tn" else _dot_nt(av, bv) if mode == "nt" else _dot(av, bv)
        if nk == 1:
            o_ref[...] = part.astype(out_dtype)
        else:
            kk = pl.program_id(2)

            @pl.when(kk == 0)
            def _():
                acc_ref[...] = part

            @pl.when(kk > 0)
            def _():
                acc_ref[...] += part

            @pl.when(kk == nk - 1)
            def _():
                o_ref[...] = acc_ref[...].astype(out_dtype)

    a_spec = (pl.BlockSpec((tk, tm), lambda j, i, kk: (kk, i)) if mode == "tn"
              else pl.BlockSpec((tm, tk), lambda j, i, kk: (i, kk)))
    b_spec = (pl.BlockSpec((tn, tk), lambda j, i, kk: (j, kk)) if mode == "nt"
              else pl.BlockSpec((tk, tn), lambda j, i, kk: (kk, j)))
    o_spec = pl.BlockSpec((tm, tn), lambda j, i, kk: (i, j))
    o_shape = jax.ShapeDtypeStruct((m, n), out_dtype)
    if not ride:
        return pl.pallas_call(
            body, name=name, grid=(nj, ni, nk), out_shape=o_shape, in_specs=[a_spec, b_spec], out_specs=o_spec,
            scratch_shapes=[pltpu.VMEM((tm, tn), F32)],
            compiler_params=_params(("parallel", "parallel", "arbitrary")),
        )(a, b)
    return pl.pallas_call(
        body, name=name, grid=(nj, ni, nk),
        out_shape=(o_shape, *ride.out_shapes),
        in_specs=[a_spec, b_spec] + [ANY] * n_in,
        out_specs=(o_spec,) + (ANY,) * n_out,
        scratch_shapes=[pltpu.VMEM((tm, tn), F32)] + ride.scratch(),
        compiler_params=_params(("arbitrary", "arbitrary", "arbitrary")),
    )(a, b, *ride.args)


def _prep(proj, pos, w_uq, w_ukv, g_cq, g_ckv, rope_a, rope_b, scales, tm=256):
    t = proj.shape[0]
    sc_a, sc_b, sc_m = (s * LOG2E for s in scales)

    def body(aq_ref, ak_ref, av_ref, bs_ref, mq_ref, pos_ref, wuq_ref, wukv_ref, gcq_ref, gckv_ref,
             ra_ref, rb_ref, qa_ref, ka_ref, va_ref, qb_ref, kb_ref, vb_ref, qm_ref, cqn_ref, ckvn_ref):
        pos_c = pos_ref[...]
        ta = _rope_tables(pos_c, ra_ref[...])
        tb = _rope_tables(pos_c, rb_ref[...])
        for j in range(A_WIDTH // LANES):
            sl = slice(j * LANES, (j + 1) * LANES)
            qa_ref[:, sl] = (_rope(aq_ref[:, sl], ta, 8) * sc_a).astype(BF16)
            ka_ref[:, sl] = _rope(ak_ref[:, sl], ta, 8).astype(BF16)
        va_ref[...] = av_ref[...].astype(BF16)
        qm_ref[...] = (mq_ref[...] * sc_m).astype(BF16)

        cq_hat, _ = _rms_hat(bs_ref[:, 0:MLA_Q_RANK], MLA_Q_RANK)
        cqn = (cq_hat * gcq_ref[...]).astype(BF16)
        cqn_ref[...] = cqn
        ckv_hat, _ = _rms_hat(bs_ref[:, MLA_Q_RANK:MLA_Q_RANK + MLA_KV_RANK], MLA_KV_RANK)
        ckvn = (ckv_hat * gckv_ref[...]).astype(BF16)
        ckvn_ref[...] = ckvn
        qfull = _dot_nt(cqn, wuq_ref[...])
        kv = _dot(ckvn, wukv_ref[...])
        kr = _rope(bs_ref[:, 384:512], tb, 16)
        lane = lax.broadcasted_iota(jnp.int32, (1, LANES), 1)
        low = lane < 64
        for h in range(MLA_HEADS):
            sl = slice(h * LANES, (h + 1) * LANES)
            qb_ref[:, sl] = (_rope(qfull[:, sl], tb, 16) * sc_b).astype(BF16)
            kb_ref[:, sl] = jnp.where(low, kv[:, sl], kr).astype(BF16)
            vb_ref[:, sl] = jnp.where(low, 0.0, kv[:, sl]).astype(BF16)

    def col(width, idx):
        return pl.BlockSpec((tm, width), lambda i: (i, idx))

    def full(shape):
        return pl.BlockSpec(shape, lambda i: (0, 0))

    wide = jax.ShapeDtypeStruct((t, 1024), BF16)
    return pl.pallas_call(
        body, name="prep", grid=(t // tm,),
        out_shape=(wide, wide, wide, wide, wide, wide,
                   jax.ShapeDtypeStruct((t, MEM_WIDTH), BF16),
                   jax.ShapeDtypeStruct((t, MLA_Q_RANK), BF16),
                   jax.ShapeDtypeStruct((t, MLA_KV_RANK), BF16)),
        in_specs=[col(1024, 0), col(1024, 1), col(1024, 2), col(512, COL_CQ // 512), col(512, COL_MQ // 512),
                  pl.BlockSpec((tm, 1), lambda i: (i, 0)),
                  full((1024, MLA_Q_RANK)), full((MLA_KV_RANK, 1024)),
                  full((1, MLA_Q_RANK)), full((1, MLA_KV_RANK)), full((8, LANES)), full((8, LANES))],
        out_specs=(col(1024, 0),) * 6 + (col(MEM_WIDTH, 0), col(MLA_Q_RANK, 0), col(MLA_KV_RANK, 0)),
        compiler_params=_params(("parallel",)),
    )(proj, proj, proj, proj, proj, pos, w_uq, w_ukv, g_cq, g_ckv, rope_a, rope_b)


def _attn_fwd(q, k, v, *, nb, s, sk, heads, hpb, voff, bq, name):
    nq = s // bq
    width = hpb * LANES
    vblk = voff // hpb

    def body(q_ref, k_ref, v_ref, o_ref, lse_ref):
        for h in range(hpb):
            sl = slice(h * LANES, (h + 1) * LANES)
            sc = _dot_nt(q_ref[:, sl], k_ref[:, sl])
            m = jnp.max(sc, axis=1, keepdims=True)
            p = jnp.exp2(sc - m)
            l = jnp.sum(p, axis=1, keepdims=True)
            o_ref[:, sl] = _dot(p.astype(BF16), v_ref[:, sl]) / l
            lse_ref[:, sl] = jnp.broadcast_to(m + jnp.log(l) * LOG2E, (bq, LANES))

    out = jax.ShapeDtypeStruct((nb * s, heads * LANES), F32)
    ospec = pl.BlockSpec((bq, width), lambda b, i, g: (b * nq + i, g))
    return pl.pallas_call(
        body, name=name, grid=(nb, nq, heads // hpb),
        out_shape=(out, out),
        in_specs=[ospec, pl.BlockSpec((sk, width), lambda b, i, g: (b, g)),
                  pl.BlockSpec((sk, width), lambda b, i, g: (b, vblk + g))],
        out_specs=(ospec, ospec),
        compiler_params=_params(("parallel", "parallel", "parallel")),
    )(q, k, v)


def _attn_bwd(q, k, v, o, do, lse, *, nb, s, sk, heads, hpb, voff, scale, bq, name):
    nq = s // bq
    width = hpb * LANES
    vblk = voff // hpb

    def body(q_ref, k_ref, v_ref, o_ref, do_ref, lse_ref, dq_ref, dk_ref, dv_ref):
        i = pl.program_id(2)

        @pl.when(i == 0)
        def _():
            dk_ref[...] = jnp.zeros_like(dk_ref)
            dv_ref[...] = jnp.zeros_like(dv_ref)

        for h in range(hpb):
            sl = slice(h * LANES, (h + 1) * LANES)
            qh = q_ref[:, sl]
            kk = k_ref[:, sl]
            doh = do_ref[:, sl]
            delta = jnp.sum(doh.astype(F32) * o_ref[:, sl], axis=1, keepdims=True)
            p = jnp.exp2(_dot_nt(qh, kk) - lse_ref[:, h * LANES:h * LANES + 1])
            ds = (p * (_dot_nt(doh, v_ref[:, sl]) - delta)).astype(BF16)
            dq_ref[:, sl] = _dot(ds, kk) * scale
            dk_ref[:, sl] += _dot_tn(ds, qh)
            dv_ref[:, sl] += _dot_tn(p.astype(BF16), doh)

        @pl.when(i == nq - 1)
        def _():
            dk_ref[...] = dk_ref[...] * LN2

    qspec = pl.BlockSpec((bq, width), lambda b, g, i: (b * nq + i, g))
    kv_spec = pl.BlockSpec((sk, width), lambda b, g, i: (b, g))
    dq_shape = jax.ShapeDtypeStruct((nb * s, heads * LANES), F32)
    dkv_shape = jax.ShapeDtypeStruct((nb * sk, heads * LANES), F32)
    return pl.pallas_call(
        body, name=name, grid=(nb, heads // hpb, nq),
        out_shape=(dq_shape, dkv_shape, dkv_shape),
        in_specs=[qspec, kv_spec, pl.BlockSpec((sk, width), lambda b, g, i: (b, vblk + g)), qspec, qspec, qspec],
        out_specs=(qspec, kv_spec, kv_spec),
        compiler_params=_params(("parallel", "parallel", "arbitrary")),
    )(q, k, v, o, do, lse)


BAND_Q = 128
BAND_WIN = 256


def _band_start(i, s):
    return min(max(i * BAND_Q - 64, 0), s - BAND_WIN)


def _to_pattern_order(src_ref, dst_ref, stage_ref, s, d):
    length = s // d
    stage_ref[...] = src_ref[...].astype(F32)
    for r in range(d):
        dst_ref[r * length:(r + 1) * length, :] = stage_ref[pl.ds(r, length, stride=d), :].astype(dst_ref.dtype)


def _dilated_fwd(q, k, v, bias, bias_index, *, nb, s, name):
    nblk = s // BAND_Q
    npat = len(DILATED)

    def body(q_ref, k_ref, v_ref, bias_ref, o_ref, lse_ref, stage_ref, qp_ref, kp_ref, vp_ref, op_ref, lp_ref,
             on_ref, ln_ref):
        lane = lax.broadcasted_iota(jnp.int32, (1, LANES), 1)
        first = lane < 64
        for p, (_, d) in enumerate(DILATED):
            if d == 1:
                qs, ks, vs = q_ref, k_ref, v_ref
            else:
                for src, dst in ((q_ref, qp_ref), (k_ref, kp_ref), (v_ref, vp_ref)):
                    _to_pattern_order(src, dst, stage_ref, s, d)
                qs, ks, vs = qp_ref, kp_ref, vp_ref
            for i in range(nblk):
                u0 = i * BAND_Q
                st = _band_start(i, s)
                qi = qs[u0:u0 + BAND_Q, :]
                kw = ks[st:st + BAND_WIN, :]
                vw = vs[st:st + BAND_WIN, :]
                zero = jnp.zeros_like(qi)
                q2 = jnp.concatenate([jnp.where(first, qi, zero), jnp.where(first, zero, qi)], axis=0)
                sc = _dot_nt(q2, kw)
                b = bias_ref[bias_index[p][i]]
                halves = []
                for h in range(2):
                    sh = sc[h * BAND_Q:(h + 1) * BAND_Q] + b
                    m = jnp.max(sh, axis=1, keepdims=True)
                    pr = jnp.exp2(sh - m)
                    l = jnp.sum(pr, axis=1, keepdims=True)
                    halves.append((pr.astype(BF16), l, m + jnp.log(l) * LOG2E))
                o2 = _dot(jnp.concatenate([halves[0][0], halves[1][0]], axis=0), vw)
                o_blk = jnp.where(first, o2[:BAND_Q] / halves[0][1], o2[BAND_Q:] / halves[1][1])
                lse_blk = jnp.where(first, jnp.broadcast_to(halves[0][2], (BAND_Q, LANES)),
                                    jnp.broadcast_to(halves[1][2], (BAND_Q, LANES)))
                op_ref[p, u0:u0 + BAND_Q, :] = o_blk
                lp_ref[p, u0:u0 + BAND_Q, :] = lse_blk
            if d > 1:
                length = s // d
                for r in range(d):
                    on_ref.at[p - 1][pl.ds(r, length, stride=d), :] = op_ref[p, r * length:(r + 1) * length, :]
                    ln_ref.at[p - 1][pl.ds(r, length, stride=d), :] = lp_ref[p, r * length:(r + 1) * length, :]
        lses = [lp_ref[0]] + [ln_ref[p] for p in range(npat - 1)]
        outs = [op_ref[0]] + [on_ref[p] for p in range(npat - 1)]
        m = functools.reduce(jnp.maximum, lses)
        ws = [jnp.exp2(l - m) for l in lses]
        den = functools.reduce(lambda a, c: a + c, ws)
        o_ref[...] = functools.reduce(lambda a, c: a + c, [w * o for w, o in zip(ws, outs)]) / den
        lse_ref[...] = m + jnp.log(den) * LOG2E

    blk = pl.BlockSpec((s, LANES), lambda b, g: (b, g))
    out = jax.ShapeDtypeStruct((nb * s, A_WIDTH), F32)
    return pl.pallas_call(
        body, name=name, grid=(nb, A_WIDTH // LANES),
        out_shape=(out, out),
        in_specs=[blk, blk, blk, pl.BlockSpec(bias.shape, lambda b, g: (0, 0, 0))],
        out_specs=(blk, blk),
        scratch_shapes=[pltpu.VMEM((s, LANES), F32), pltpu.VMEM((s, LANES), BF16), pltpu.VMEM((s, LANES), BF16),
                        pltpu.VMEM((s, LANES), BF16), pltpu.VMEM((npat, s, LANES), F32),
                        pltpu.VMEM((npat, s, LANES), F32), pltpu.VMEM((npat - 1, s, LANES), F32),
                        pltpu.VMEM((npat - 1, s, LANES), F32)],
        compiler_params=_params(("parallel", "parallel")),
    )(q, k, v, bias)


def _dilated_bwd(q, k, v, o, do, lse, bias, bias_index, *, nb, s, scale, name):
    nblk = s // BAND_Q
    npat = len(DILATED)

    def body(q_ref, k_ref, v_ref, o_ref, do_ref, lse_ref, bias_ref, dq_ref, dk_ref, dv_ref,
             stage_ref, dl_ref, qp_ref, kp_ref, vp_ref, dop_ref, lsp_ref, dlp_ref, dqp_ref, dkp_ref, dvp_ref):
        lane = lax.broadcasted_iota(jnp.int32, (1, LANES), 1)
        first = lane < 64
        prod = do_ref[...].astype(F32) * o_ref[...]
        d0 = jnp.sum(jnp.where(first, prod, 0.0), axis=1, keepdims=True)
        d1 = jnp.sum(jnp.where(first, 0.0, prod), axis=1, keepdims=True)
        dl_ref[...] = jnp.where(first, jnp.broadcast_to(d0, (s, LANES)), jnp.broadcast_to(d1, (s, LANES)))
        for p, (_, d) in enumerate(DILATED):
            length = s // d
            if d == 1:
                qs, ks, vs, dos, lss, dls = q_ref, k_ref, v_ref, do_ref, lse_ref, dl_ref
                dqs, dks, dvs = dq_ref, dk_ref, dv_ref
            else:
                for src, dst in ((q_ref, qp_ref), (k_ref, kp_ref), (v_ref, vp_ref), (do_ref, dop_ref),
                                 (lse_ref, lsp_ref), (dl_ref, dlp_ref)):
                    _to_pattern_order(src, dst, stage_ref, s, d)
                qs, ks, vs, dos, lss, dls = qp_ref, kp_ref, vp_ref, dop_ref, lsp_ref, dlp_ref
                dqs, dks, dvs = dqp_ref, dkp_ref, dvp_ref
            dks[...] = jnp.zeros((s, LANES), F32)
            dvs[...] = jnp.zeros((s, LANES), F32)
            for i in range(nblk):
                u0 = i * BAND_Q
                st = _band_start(i, s)
                qi = qs[u0:u0 + BAND_Q, :]
                doi = dos[u0:u0 + BAND_Q, :]
                kw = ks[st:st + BAND_WIN, :]
                vw = vs[st:st + BAND_WIN, :]
                zero = jnp.zeros_like(qi)
                q2 = jnp.concatenate([jnp.where(first, qi, zero), jnp.where(first, zero, qi)], axis=0)
                do2 = jnp.concatenate([jnp.where(first, doi, zero), jnp.where(first, zero, doi)], axis=0)
                sc = _dot_nt(q2, kw)
                dp = _dot_nt(do2, vw)
                b = bias_ref[bias_index[p][i]]
                lse_i = lss[u0:u0 + BAND_Q, :]
                dl_i = dls[u0:u0 + BAND_Q, :]
                ps, dss = [], []
                for h in range(2):
                    rows = slice(h * BAND_Q, (h + 1) * BAND_Q)
                    pr = jnp.exp2(sc[rows] + b - lse_i[:, 64 * h:64 * h + 1])
                    ps.append(pr.astype(BF16))
                    dss.append((pr * (dp[rows] - dl_i[:, 64 * h:64 * h + 1])).astype(BF16))
                p2 = jnp.concatenate(ps, axis=0)
                ds2 = jnp.concatenate(dss, axis=0)
                dq2 = _dot(ds2, kw)
                dqs[u0:u0 + BAND_Q, :] = jnp.where(first, dq2[:BAND_Q], dq2[BAND_Q:]) * scale
                dks[st:st + BAND_WIN, :] += _dot_tn(ds2, q2)
                dvs[st:st + BAND_WIN, :] += _dot_tn(p2, do2)
            if d > 1:
                for dst, src in ((dq_ref, dqp_ref), (dk_ref, dkp_ref), (dv_ref, dvp_ref)):
                    for r in range(d):
                        dst[pl.ds(r, length, stride=d), :] += src[r * length:(r + 1) * length, :]
        dk_ref[...] = dk_ref[...] * LN2

    blk = pl.BlockSpec((s, LANES), lambda b, g: (b, g))
    out = jax.ShapeDtypeStruct((nb * s, A_WIDTH), F32)
    f32_buf = pltpu.VMEM((s, LANES), F32)
    bf_buf = pltpu.VMEM((s, LANES), BF16)
    return pl.pallas_call(
        body, name=name, grid=(nb, A_WIDTH // LANES),
        out_shape=(out, out, out),
        in_specs=[blk] * 6 + [pl.BlockSpec(bias.shape, lambda b, g: (0, 0, 0))],
        out_specs=(blk, blk, blk),
        scratch_shapes=[f32_buf, f32_buf, bf_buf, bf_buf, bf_buf, bf_buf, f32_buf, f32_buf, f32_buf, f32_buf, f32_buf],
        compiler_params=_params(("parallel", "parallel")),
    )(q, k, v, o, do, lse, bias)


def _post(x, ya, ybp, ym, proj, target, w_out, g_emb, b_emb, g_a, g_b, g_m, g_post, b_post, tm=256):
    t = x.shape[0]

    def body(x_ref, ya_ref, yb_ref, ym_ref, ga_ref, gb_ref, gm_ref, tg_ref, wo_ref,
             ge_ref, be_ref, goa_ref, gob_ref, gom_ref, gp_ref, bp_ref,
             y_ref, dz_ref, doa_ref, dob_ref, dom_ref, dga_ref, dgb_ref, dgm_ref,
             loss_ref, dgp_ref, dbp_ref, dgoa_ref, dgob_ref, dgom_ref):
        i = pl.program_id(0)

        @pl.when(i == 0)
        def _():
            for r in (loss_ref, dgp_ref, dbp_ref, dgoa_ref, dgob_ref, dgom_ref):
                r[...] = jnp.zeros_like(r)

        lane = lax.broadcasted_iota(jnp.int32, (1, LANES), 1)
        low = lane < 64
        xh0, _ = _ln_hat(x_ref[...])
        h = xh0 * ge_ref[...] + be_ref[...]

        ybp_v = yb_ref[...]
        yb = jnp.concatenate(
            [jnp.where(low, pltpu.roll(ybp_v[:, 2 * j * LANES:(2 * j + 1) * LANES], 64, 1),
                       ybp_v[:, (2 * j + 1) * LANES:(2 * j + 2) * LANES]) for j in range(4)], axis=1)

        def gated(raw, gate, gain, width):
            xh, r = _rms_hat(raw, width)
            n = xh * gain
            sg = 1.0 / (1.0 + jnp.exp(-gate))
            return xh, r, n, sg, n * (gate * sg)

        gate_a, gate_b, gate_m = ga_ref[...], gb_ref[...], gm_ref[...]
        xh_a, r_a, n_a, sg_a, y_a = gated(ya_ref[...], gate_a, goa_ref[...], A_WIDTH)
        xh_b, r_b, n_b, sg_b, y_b = gated(yb, gate_b, gob_ref[...], 512)
        xh_m, r_m, n_m, sg_m, y_m = gated(ym_ref[...], gate_m, gom_ref[...], 512)
        y = jnp.concatenate([y_a, y_b, y_m], axis=1).astype(BF16)
        y_ref[...] = y
        z = DEEPNORM_ALPHA * h + _dot(y, wo_ref[...])
        zh, rstd = _ln_hat(z)
        err = zh * gp_ref[...] + bp_ref[...] - tg_ref[...]
        rows = jnp.sum(err * err, axis=1, keepdims=True)
        loss_ref[...] += jnp.broadcast_to(jnp.sum(rows, axis=0, keepdims=True) * (0.5 / D_MODEL), (1, LANES))
        dout = err * (1.0 / D_MODEL)
        dgp_ref[...] += _colsum(dout * zh)
        dbp_ref[...] += _colsum(dout)
        dz = _ln_bwd_rows(dout * gp_ref[...], zh, rstd)
        dz_ref[...] = dz
        dy = _dot_nt(dz.astype(BF16), wo_ref[...])

        def gated_bwd(dyg, xh, r, n, sg, gate, gain, width, dgain_ref):
            dn = dyg * (gate * sg)
            dgate = dyg * n * (sg * (1.0 + gate * (1.0 - sg)))
            dgain_ref[...] += _colsum(dn * xh)
            return _rms_bwd(dn * gain, xh, r, width), dgate

        dya, dgate_a = gated_bwd(dy[:, 0:1024], xh_a, r_a, n_a, sg_a, gate_a, goa_ref[...], A_WIDTH, dgoa_ref)
        dyb, dgate_b = gated_bwd(dy[:, 1024:1536], xh_b, r_b, n_b, sg_b, gate_b, gob_ref[...], 512, dgob_ref)
        dym, dgate_m = gated_bwd(dy[:, 1536:2048], xh_m, r_m, n_m, sg_m, gate_m, gom_ref[...], 512, dgom_ref)
        doa_ref[...] = dya.astype(BF16)
        dom_ref[...] = dym.astype(BF16)
        dga_ref[...] = dgate_a.astype(BF16)
        dgb_ref[...] = dgate_b.astype(BF16)
        dgm_ref[...] = dgate_m.astype(BF16)
        for j in range(4):
            blk = dyb[:, j * LANES:(j + 1) * LANES]
            dob_ref[:, 2 * j * LANES:(2 * j + 1) * LANES] = jnp.where(low, 0.0, pltpu.roll(blk, 64, 1)).astype(BF16)
            dob_ref[:, (2 * j + 1) * LANES:(2 * j + 2) * LANES] = jnp.where(low, 0.0, blk).astype(BF16)

    def col(width, idx):
        return pl.BlockSpec((tm, width), lambda i: (i, idx))

    def full(shape):
        return pl.BlockSpec(shape, lambda i: (0, 0))

    def acc(width):
        return jax.ShapeDtypeStruct((1, width), F32)

    return pl.pallas_call(
        body, name="post", grid=(t // tm,),
        out_shape=(jax.ShapeDtypeStruct((t, 2048), BF16), jax.ShapeDtypeStruct((t, 1024), F32),
                   jax.ShapeDtypeStruct((t, 1024), BF16), jax.ShapeDtypeStruct((t, 1024), BF16),
                   jax.ShapeDtypeStruct((t, 512), BF16),
                   jax.ShapeDtypeStruct((t, 1024), BF16), jax.ShapeDtypeStruct((t, 512), BF16),
                   jax.ShapeDtypeStruct((t, 512), BF16),
                   acc(LANES), acc(1024), acc(1024), acc(1024), acc(512), acc(512)),
        in_specs=[col(1024, 0), col(1024, 0), col(1024, 0), col(512, 0),
                  col(1024, 3), col(512, COL_BG // 512), col(512, COL_MG // 512), col(1024, 0),
                  full((2048, 1024)),
                  full((1, 1024)), full((1, 1024)), full((1, 1024)), full((1, 512)), full((1, 512)),
                  full((1, 1024)), full((1, 1024))],
        out_specs=(col(2048, 0), col(1024, 0), col(1024, 0), col(1024, 0), col(512, 0),
                   col(1024, 0), col(512, 0), col(512, 0),
                   full((1, LANES)), full((1, 1024)), full((1, 1024)), full((1, 1024)), full((1, 512)),
                   full((1, 512))),
        compiler_params=_params(("arbitrary",)),
    )(x, ya, ybp, ym, proj, proj, proj, target, w_out, g_emb, b_emb, g_a, g_b, g_m, g_post, b_post)


def _prep_bwd(dqa, dka, dva, dqb, dkb, dvb, dqm, dga, dgb, dgm, proj, pos, w_uq, w_ukv, g_cq, g_ckv,
              rope_a, rope_b, tm=256):
    t = proj.shape[0]

    def body(dqa_ref, dka_ref, dva_ref, dqb_ref, dkb_ref, dvb_ref, dqm_ref, dga_ref, dgb_ref, dgm_ref,
             bs_ref, pos_ref, wuq_ref, wukv_ref, gcq_ref, gckv_ref, ra_ref, rb_ref,
             dproj_ref, dqf_ref, dkv_ref, dgcq_ref, dgckv_ref):
        i = pl.program_id(0)

        @pl.when(i == 0)
        def _():
            dgcq_ref[...] = jnp.zeros_like(dgcq_ref)
            dgckv_ref[...] = jnp.zeros_like(dgckv_ref)

        pos_c = pos_ref[...]
        ta = _rope_tables(pos_c, ra_ref[...])
        tb = _rope_tables(pos_c, rb_ref[...])
        for j in range(A_WIDTH // LANES):
            sl = slice(j * LANES, (j + 1) * LANES)
            dproj_ref[:, j * LANES:(j + 1) * LANES] = _rope(dqa_ref[:, sl], ta, 8, inverse=True).astype(BF16)
            dproj_ref[:, 1024 + j * LANES:1024 + (j + 1) * LANES] = (
                _rope(dka_ref[:, sl], ta, 8, inverse=True).astype(BF16))
        dproj_ref[:, 2048:3072] = dva_ref[...].astype(BF16)
        dproj_ref[:, 3072:4096] = dga_ref[...]

        lane = lax.broadcasted_iota(jnp.int32, (1, LANES), 1)
        low = lane < 64
        rope_lanes = (lane >= 64) & (lane < 96)
        dkr = jnp.zeros((tm, LANES), F32)
        for h in range(MLA_HEADS):
            sl = slice(h * LANES, (h + 1) * LANES)
            dqf_ref[:, sl] = _rope(dqb_ref[:, sl], tb, 16, inverse=True).astype(BF16)
            dk_h = dkb_ref[:, sl]
            dkv_ref[:, sl] = jnp.where(low, dk_h, dvb_ref[:, sl]).astype(BF16)
            dkr = dkr + jnp.where(rope_lanes, dk_h, 0.0)
        dkr = _rope(dkr, tb, 16, inverse=True)

        cq_hat, r_q = _rms_hat(bs_ref[:, 0:MLA_Q_RANK], MLA_Q_RANK)
        dcqn = _dot(dqf_ref[...], wuq_ref[...])
        dgcq_ref[...] += _colsum(dcqn * cq_hat)
        dproj_ref[:, COL_CQ:COL_CQ + 256] = _rms_bwd(dcqn * gcq_ref[...], cq_hat, r_q, MLA_Q_RANK).astype(BF16)
        ckv_hat, r_kv = _rms_hat(bs_ref[:, MLA_Q_RANK:MLA_Q_RANK + MLA_KV_RANK], MLA_KV_RANK)
        dckvn = _dot_nt(dkv_ref[...], wukv_ref[...])
        dgckv_ref[...] += _colsum(dckvn * ckv_hat)
        dproj_ref[:, COL_CQ + 256:COL_CQ + 384] = (
            _rms_bwd(dckvn * gckv_ref[...], ckv_hat, r_kv, MLA_KV_RANK).astype(BF16))
        dproj_ref[:, COL_CQ + 384:COL_CQ + 512] = dkr.astype(BF16)
        dproj_ref[:, COL_BG:COL_BG + 512] = dgb_ref[...]
        dproj_ref[:, COL_MQ:COL_MQ + 512] = dqm_ref[...].astype(BF16)
        dproj_ref[:, COL_MG:COL_MG + 512] = dgm_ref[...]

    def col(width, idx):
        return pl.BlockSpec((tm, width), lambda i: (i, idx))

    def full(shape):
        return pl.BlockSpec(shape, lambda i: (0, 0))

    return pl.pallas_call(
        body, name="prep_bwd", grid=(t // tm,),
        out_shape=(jax.ShapeDtypeStruct((t, PROJ_W), BF16), jax.ShapeDtypeStruct((t, 1024), BF16),
                   jax.ShapeDtypeStruct((t, 1024), BF16),
                   jax.ShapeDtypeStruct((1, MLA_Q_RANK), F32), jax.ShapeDtypeStruct((1, MLA_KV_RANK), F32)),
        in_specs=[col(1024, 0)] * 6 + [col(512, 0), col(1024, 0), col(512, 0), col(512, 0),
                  col(512, COL_CQ // 512), pl.BlockSpec((tm, 1), lambda i: (i, 0)),
                  full((1024, MLA_Q_RANK)), full((MLA_KV_RANK, 1024)),
                  full((1, MLA_Q_RANK)), full((1, MLA_KV_RANK)), full((8, LANES)), full((8, LANES))],
        out_specs=(col(PROJ_W, 0), col(1024, 0), col(1024, 0), full((1, MLA_Q_RANK)), full((1, MLA_KV_RANK))),
        compiler_params=_params(("arbitrary",)),
    )(dqa, dka, dva, dqb, dkb, dvb, dqm, dga, dgb, dgm, proj, pos, w_uq, w_ukv, g_cq, g_ckv, rope_a, rope_b)


def _adamw(g, w, m, v, tr, name):
    r, cols = w.shape

    def body(g_ref, w_ref, m_ref, v_ref, d_ref, nm_ref, nv_ref):
        gv = g_ref[...]
        m_new = ADAM_B1 * m_ref[...] + (1.0 - ADAM_B1) * gv
        v_new = ADAM_B2 * v_ref[...] + (1.0 - ADAM_B2) * (gv * gv)
        m_hat = m_new / (1.0 - ADAM_B1 ** ADAM_STEP)
        v_hat = v_new / (1.0 - ADAM_B2 ** ADAM_STEP)
        d_ref[...] = -ADAM_LR * (m_hat / (jnp.sqrt(v_hat) + ADAM_EPS) + ADAM_WD * w_ref[...])
        nm_ref[...] = m_new
        nv_ref[...] = v_new

    tile = pl.BlockSpec((tr, cols), lambda i: (i, 0))
    shape = jax.ShapeDtypeStruct((r, cols), F32)
    return pl.pallas_call(
        body, name=name, grid=(r // tr,),
        out_shape=(shape,) * 3, in_specs=[tile] * 4, out_specs=(tile,) * 3,
        compiler_params=_params(("parallel",)),
    )(g, w, m, v)


def _core_sum(g, recv, core, rows, tr, name, ride=None):
    cols = g.shape[2]
    nblk = rows // tr
    n_in = len(ride.args) if ride else 0
    n_out = len(ride.out_shapes) if ride else 0

    def body(c_ref, g_ref, r_ref, *rest):
        sf_ref, sb_ref = rest[n_in], rest[n_in + 1]
        if ride:
            j, i = pl.program_id(0), pl.program_id(1)
            ride.run((j == 0) & (i == 0), (j == 3) & (i == nblk - 1), rest[:n_in],
                     rest[n_in + 2:n_in + 2 + n_out], rest[-2], rest[-1])
        tot = g_ref[...] + r_ref[...]
        sf_ref[...] = tot
        sb_ref[...] = tot.astype(BF16)

    half = pl.BlockSpec((None, tr, cols), lambda j, i, c_ref: (j, i, 0))
    shapes = (jax.ShapeDtypeStruct((4, rows, cols), F32), jax.ShapeDtypeStruct((4, rows, cols), BF16))
    return pl.pallas_call(
        body, name=name,
        grid_spec=pltpu.PrefetchScalarGridSpec(
            num_scalar_prefetch=1, grid=(4, nblk),
            in_specs=[pl.BlockSpec((None, tr, cols), lambda j, i, c_ref: (j, c_ref[0] * nblk + i, 0)), half]
            + [ANY] * n_in,
            out_specs=(half, half) + (ANY,) * n_out,
            scratch_shapes=ride.scratch() if ride else []),
        out_shape=shapes + tuple(ride.out_shapes if ride else ()),
        compiler_params=_params(("arbitrary", "arbitrary") if ride else ("parallel", "parallel")),
    )(core, g, recv, *(ride.args if ride else ()))


def _half_to_sibling(g4):
    def copies(in_refs, out_refs, send_sems, recv_sems):
        x, y, c = _position()
        return [pltpu.make_async_remote_copy(
            src_ref=in_refs[0].at[:, 1 - c], dst_ref=out_refs[0], send_sem=send_sems.at[0],
            recv_sem=recv_sems.at[0], device_id=(x, y, 1 - c), device_id_type=MESH)]

    return _Ride([g4], [jax.ShapeDtypeStruct((4, g4.shape[2], 1024), F32)], 1, copies)


def _chip_sum(sf, recv, chip, rows, tr, name):
    cols = sf.shape[2]

    def body(me_ref, sf_ref, r_ref, out_ref):
        acc = sf_ref[...]
        for k in range(3):
            acc = acc + r_ref[k].astype(F32)
        out_ref[...] = acc

    return pl.pallas_call(
        body, name=name,
        grid_spec=pltpu.PrefetchScalarGridSpec(
            num_scalar_prefetch=1, grid=(rows // tr,),
            in_specs=[pl.BlockSpec((None, tr, cols), lambda i, me_ref: (me_ref[0], i, 0)),
                      pl.BlockSpec((3, tr, cols), lambda i, me_ref: (0, i, 0))],
            out_specs=pl.BlockSpec((tr, cols), lambda i, me_ref: (i, 0))),
        out_shape=jax.ShapeDtypeStruct((rows, cols), F32),
        compiler_params=_params(("parallel",)),
    )(chip, sf, recv)


def _position():
    return lax.axis_index("x"), lax.axis_index("y"), lax.axis_index("c")


def _gather_weights(w_in_b, rest_b):
    def body(in_ref, rest_ref, oin_ref, orest_ref, send_sems, recv_sems, local_sems):
        x, y, c = _position()
        me = 2 * x + y
        srcs = (in_ref, rest_ref)
        dsts = (oin_ref, orest_ref)
        local = [pltpu.make_async_copy(srcs[a], dsts[a].at[me], local_sems.at[a]) for a in range(2)]
        for cp in local:
            cp.start()

        def piece(a, chip, half):
            return dsts[a].at[chip, half]

        def from_chip(a, k):
            return pltpu.make_async_remote_copy(
                src_ref=srcs[a].at[c], dst_ref=piece(a, me, c),
                send_sem=send_sems.at[3 * a + k - 1], recv_sem=recv_sems.at[3 * a + k - 1],
                device_id=(x ^ (k >> 1), y ^ (k & 1), c), device_id_type=MESH)

        def arrived(a, k):
            return pltpu.make_async_remote_copy(
                src_ref=piece(a, me ^ k, c), dst_ref=piece(a, me ^ k, c),
                send_sem=send_sems.at[3 * a + k - 1], recv_sem=recv_sems.at[3 * a + k - 1],
                device_id=(x ^ (k >> 1), y ^ (k & 1), c), device_id_type=MESH)

        def to_sibling(a, k, half):
            return pltpu.make_async_remote_copy(
                src_ref=piece(a, me ^ k, half), dst_ref=piece(a, me ^ k, half),
                send_sem=send_sems.at[6 + 3 * a + k - 1], recv_sem=recv_sems.at[6 + 3 * a + k - 1],
                device_id=(x, y, 1 - c), device_id_type=MESH)

        pairs = [(a, k) for a in range(2) for k in (1, 2, 3)]
        sends = [from_chip(a, k) for a, k in pairs]
        for cp in sends:
            cp.start()
        passed = []
        for a, k in pairs:
            arrived(a, k).wait_recv()
            cp = to_sibling(a, k, c)
            cp.start()
            passed.append(cp)
        for a, k in pairs:
            to_sibling(a, k, 1 - c).wait_recv()
        for cp in sends + passed:
            cp.wait_send()
        for cp in local:
            cp.wait()

    return pl.pallas_call(
        body, name="gather_weights",
        out_shape=(jax.ShapeDtypeStruct((4,) + w_in_b.shape, BF16), jax.ShapeDtypeStruct((4,) + rest_b.shape, BF16)),
        in_specs=[IN_VMEM, IN_VMEM], out_specs=(ANY, ANY),
        scratch_shapes=[pltpu.SemaphoreType.DMA((12,)), pltpu.SemaphoreType.DMA((12,)), pltpu.SemaphoreType.DMA((2,))],
    )(w_in_b, rest_b)


def _dh_scatter(dproj, w_in_arr_t, x, dz, g, sb_in, sb_rest, tm=1024, tk=1024):
    t, d = x.shape
    nk = dproj.shape[1] // tk
    ni = t // tm

    def body(dp_ref, w_ref, x_ref, dz_ref, g_ref, sbin_ref, sbrest_ref,
             dx_ref, dg_ref, db_ref, rin_ref, rrest_ref, acc_ref, send_sems, recv_sems):
        i = pl.program_id(0)
        kk = pl.program_id(1)
        px, py, pc = _position()
        me = 2 * px + py
        srcs = (sbin_ref, sbrest_ref)
        dsts = (rin_ref, rrest_ref)

        def copy(a, k):
            return pltpu.make_async_remote_copy(
                src_ref=srcs[a].at[me ^ k], dst_ref=dsts[a].at[k - 1],
                send_sem=send_sems.at[3 * a + k - 1], recv_sem=recv_sems.at[3 * a + k - 1],
                device_id=(px ^ (k >> 1), py ^ (k & 1), pc), device_id_type=MESH)

        pairs = [(a, k) for a in range(2) for k in (1, 2, 3)]

        @pl.when((i == 0) & (kk == 0))
        def _():
            dg_ref[...] = jnp.zeros_like(dg_ref)
            db_ref[...] = jnp.zeros_like(db_ref)
            for a, k in pairs:
                copy(a, k).start()

        part = _dot(dp_ref[...], w_ref[...])

        @pl.when(kk == 0)
        def _():
            acc_ref[...] = part

        @pl.when(kk > 0)
        def _():
            acc_ref[...] += part

        @pl.when(kk == nk - 1)
        def _():
            xh, rstd = _ln_hat(x_ref[...])
            dht = acc_ref[...] + DEEPNORM_ALPHA * dz_ref[...]
            dg_ref[...] += _colsum(dht * xh)
            db_ref[...] += _colsum(dht)
            dx_ref[...] = _ln_bwd_rows(dht * g_ref[...], xh, rstd)

        @pl.when((i == ni - 1) & (kk == nk - 1))
        def _():
            for a, k in pairs:
                copy(a, k).wait_recv()
            for a, k in pairs:
                copy(a, k).wait_send()

    tile = pl.BlockSpec((tm, d), lambda i, kk: (i, 0))
    row = pl.BlockSpec((1, d), lambda i, kk: (0, 0))
    return pl.pallas_call(
        body, name="dh_scatter", grid=(ni, nk),
        out_shape=(jax.ShapeDtypeStruct((t, d), F32), jax.ShapeDtypeStruct((1, d), F32),
                   jax.ShapeDtypeStruct((1, d), F32),
                   jax.ShapeDtypeStruct((3, HALF_IN, 1024), BF16),
                   jax.ShapeDtypeStruct((3, HALF_REST, 1024), BF16)),
        in_specs=[pl.BlockSpec((tm, tk), lambda i, kk: (i, kk)), pl.BlockSpec((tk, d), lambda i, kk: (kk, 0)),
                  tile, tile, row, ANY, ANY],
        out_specs=(tile, row, row, ANY, ANY),
        scratch_shapes=[pltpu.VMEM((tm, d), F32), pltpu.SemaphoreType.DMA((6,)), pltpu.SemaphoreType.DMA((6,))],
        compiler_params=_params(("arbitrary", "arbitrary")),
    )(dproj, w_in_arr_t, x, dz, g, sb_in, sb_rest)


def _join_halves(gh_in, gh_rest):
    def body(hin_ref, hrest_ref, oin_ref, orest_ref, send_sems, recv_sems, local_sems):
        x, y, c = _position()
        srcs = (hin_ref, hrest_ref)
        dsts = (oin_ref, orest_ref)

        def rows(a, half):
            return dsts[a].at[half]

        local = [pltpu.make_async_copy(srcs[a], rows(a, c), local_sems.at[a]) for a in range(2)]
        remote = [pltpu.make_async_remote_copy(
            src_ref=srcs[a], dst_ref=rows(a, c), send_sem=send_sems.at[a], recv_sem=recv_sems.at[a],
            device_id=(x, y, 1 - c), device_id_type=MESH) for a in range(2)]
        for cp in local + remote:
            cp.start()
        for a in range(2):
            pltpu.make_async_remote_copy(
                src_ref=srcs[a], dst_ref=rows(a, 1 - c), send_sem=send_sems.at[a], recv_sem=recv_sems.at[a],
                device_id=(x, y, 1 - c), device_id_type=MESH).wait_recv()
        for cp in remote:
            cp.wait_send()
        for cp in local:
            cp.wait()

    return pl.pallas_call(
        body, name="join_halves",
        out_shape=(jax.ShapeDtypeStruct((2, HALF_IN, 1024), F32),
                   jax.ShapeDtypeStruct((2, HALF_REST, 1024), F32)),
        in_specs=[IN_VMEM, IN_VMEM], out_specs=(ANY, ANY),
        scratch_shapes=[pltpu.SemaphoreType.DMA((2,)), pltpu.SemaphoreType.DMA((2,)), pltpu.SemaphoreType.DMA((2,))],
    )(gh_in, gh_rest)


def _allreduce_small(vec):
    def body(vec_ref, out_ref, all_ref, send_sems, recv_sems):
        x, y, c = _position()
        me = 4 * x + 2 * y + c
        all_ref[me] = vec_ref[...]

        def copy(k, slot):
            return pltpu.make_async_remote_copy(
                src_ref=vec_ref, dst_ref=all_ref.at[slot], send_sem=send_sems.at[k - 1], recv_sem=recv_sems.at[k - 1],
                device_id=(x ^ (k >> 2), y ^ ((k >> 1) & 1), c ^ (k & 1)), device_id_type=MESH)

        copies = [copy(k, me) for k in range(1, 8)]
        for cp in copies:
            cp.start()
        for k in range(1, 8):
            copy(k, me ^ k).wait_recv()
        for cp in copies:
            cp.wait_send()
        total = all_ref[0]
        for d in range(1, 8):
            total = total + all_ref[d]
        out_ref[...] = total

    return pl.pallas_call(
        body, name="allreduce_small",
        out_shape=jax.ShapeDtypeStruct(vec.shape, vec.dtype),
        in_specs=[pl.BlockSpec(memory_space=pltpu.VMEM)], out_specs=pl.BlockSpec(memory_space=pltpu.VMEM),
        scratch_shapes=[pltpu.VMEM((8,) + vec.shape, vec.dtype), pltpu.SemaphoreType.DMA((7,)),
                        pltpu.SemaphoreType.DMA((7,))],
    )(vec)


def _pack_rest(w_uq, w_ukv, w_mem, w_out):
    rows = jnp.concatenate([w_uq[0].T.reshape(-1, 1024), w_ukv.reshape(-1, 1024), w_mem.reshape(-1, 1024),
                            w_out.reshape(-1, 1024)], axis=0)
    return jnp.pad(rows, ((0, ROWS_REST - ROWS_USED), (0, 0)))


def _unpack_rest(p):
    uq = p[0:ROWS_UQ].reshape(192, 256).T[None]
    o = ROWS_UQ
    out = [uq]
    for rows, shape in ((ROWS_UKV, (1, 128, 256)), (ROWS_MEM, (1, 256, 1024)), (ROWS_OUT, (1, 512, 1024))):
        out.append(p[o:o + rows].reshape(shape))
        o += rows
    return out


def _full_weights(g_in, g_rest):
    z = functools.partial(jnp.zeros, dtype=g_in.dtype)
    cut = 4480 - 2 * SHARD_ROWS
    w_in_arr_t = jnp.concatenate(
        [g_in[0, :SHARD_ROWS], g_in[1, :SHARD_ROWS], g_in[2, :cut], z((64, 1024)), g_in[2, cut:cut + 32],
         z((32, 1024)), g_in[2, cut + 32:SHARD_ROWS], g_in[3, :SHARD_ROWS]], axis=0)
    w_uq_t = g_rest[:, 0:ROWS_UQ].reshape(768, 256)
    w_uq_pad_t = jnp.pad(w_uq_t.reshape(MLA_HEADS, MLA_QK_DIM, 256), ((0, 0), (0, 32), (0, 0))).reshape(1024, 256)
    w_ukv = jnp.concatenate([g_rest[j, ROWS_UQ:ROWS_UQ + ROWS_UKV].reshape(128, 256) for j in range(4)], axis=1)
    lo = ROWS_UQ + ROWS_UKV
    w_mem = g_rest[:, lo:lo + ROWS_MEM].reshape(4 * ROWS_MEM, 1024)
    w_out = g_rest[:, lo + ROWS_MEM:lo + ROWS_MEM + ROWS_OUT].reshape(4 * ROWS_OUT, 1024)
    return w_in_arr_t, w_uq_pad_t, w_ukv, w_mem, w_out


def _split_in(dw_in_arr_t):
    dw_in_t = jnp.concatenate([dw_in_arr_t[:4480], dw_in_arr_t[4544:4576], dw_in_arr_t[4608:]], axis=0)
    return jnp.pad(dw_in_t.reshape(4, SHARD_ROWS, 1024), ((0, 0), (0, ROWS_IN - SHARD_ROWS), (0, 0)))


def _split_rest(dw_uq_pad_t, dw_ukv, dw_mem, dw_out):
    dw_uq_t = dw_uq_pad_t.reshape(MLA_HEADS, LANES, 256)[:, :MLA_QK_DIM].reshape(4, ROWS_UQ, 1024)
    parts = [dw_uq_t, dw_ukv.reshape(128, 4, 256).transpose(1, 0, 2).reshape(4, ROWS_UKV, 1024),
             dw_mem.reshape(4, ROWS_MEM, 1024), dw_out.reshape(4, ROWS_OUT, 1024)]
    return jnp.pad(jnp.concatenate(parts, axis=1), ((0, 0), (0, ROWS_REST - ROWS_USED), (0, 0)))


def _rope_consts(rot, first, period):
    half = rot // 2
    inv_freq = np.float32(ROPE_THETA) ** (-(np.arange(0, rot, 2, dtype=np.float32) / np.float32(rot)))
    lane = np.arange(LANES) % period - first
    in_rot = (lane >= 0) & (lane < rot)
    out = np.zeros((8, LANES), np.float32)
    out[0] = np.where(in_rot, inv_freq[np.clip(lane, 0, rot - 1) % half], 0.0)
    out[1] = in_rot & (lane < half)
    out[2] = in_rot & (lane >= half)
    return jnp.asarray(out)


def _band_bias(s):
    nblk = s // BAND_Q
    starts = np.array([_band_start(i, s) for i in range(nblk)])
    uq = (np.arange(nblk)[:, None] * BAND_Q + np.arange(BAND_Q)[None, :])[:, :, None]
    uk = (starts[:, None] + np.arange(BAND_WIN)[None, :])[:, None, :]
    tiles, index, seen = [], [], {}
    for _, d in DILATED:
        length = s // d
        ok = (uq // length == uk // length) & (np.abs(uq - uk) <= 64)
        row = []
        for i in range(nblk):
            key = ok[i].tobytes()
            if key not in seen:
                seen[key] = len(tiles)
                tiles.append(np.where(ok[i], 0.0, NEG_INF).astype(np.float32))
            row.append(seen[key])
        index.append(row)
    return jnp.asarray(np.stack(tiles, axis=0)), index


def _forward_backward(x, mem, positions, target, weights, gains):
    w_in_arr_t, w_uq_pad_t, w_ukv, w_mem, w_out = weights
    g_emb, b_emb, g_cq, g_ckv, g_out_a, g_out_b, g_out_m, g_post, b_post = gains
    nb, s, d = x.shape
    t = nb * s
    x2 = x.reshape(t, d)
    mem2 = mem.reshape(nb * N_MEM, d)
    tgt2 = target.reshape(t, d)
    pos = positions.reshape(t, 1).astype(F32)
    rope_a = _rope_consts(16, 0, 64)
    rope_b = _rope_consts(32, 64, 128)
    bias, bias_index = _band_bias(s)
    scales = (0.125, MLA_QK_DIM ** -0.5, 128 ** -0.5)

    h = _ln_fwd(x2, g_emb, b_emb)
    proj = _mm(h, w_in_arr_t, F32, 1024, 1024, 1024, "in_proj", mode="nt")
    qa, ka, va, qb, kb, vb, qm, cqn, ckvn = _prep(proj, pos, w_uq_pad_t, w_ukv, g_cq, g_ckv, rope_a, rope_b, scales)
    mkv = _mm(mem2, w_mem, BF16, nb * N_MEM, 1024, 1024, "mem_kv")

    cfg_b = dict(nb=nb, s=s, sk=s, heads=8, hpb=2, voff=0, bq=256)
    cfg_m = dict(nb=nb, s=s, sk=N_MEM, heads=4, hpb=2, voff=4, bq=1024)
    ya, lse_a = _dilated_fwd(qa, ka, va, bias, bias_index, nb=nb, s=s, name="attn_a_fwd")
    yb, lse_b = _attn_fwd(qb, kb, vb, name="attn_b_fwd", **cfg_b)
    ym, lse_m = _attn_fwd(qm, mkv, mkv, name="attn_m_fwd", **cfg_m)

    (y, dz, doa, dob, dom, dga, dgb, dgm, loss, dg_post, db_post, dg_a, dg_b, dg_m) = _post(
        x2, ya, yb, ym, proj, tgt2, w_out, g_emb, b_emb, g_out_a, g_out_b, g_out_m, g_post, b_post)

    dqa, dka, dva = _dilated_bwd(qa, ka, va, ya, doa, lse_a, bias, bias_index, nb=nb, s=s, scale=scales[0],
                                 name="attn_a_bwd")
    dqb, dkb, dvb = _attn_bwd(qb, kb, vb, yb, dob, lse_b, name="attn_b_bwd", scale=scales[1], **cfg_b)
    dqm, dmk, dmv = _attn_bwd(qm, mkv, mkv, ym, dom, lse_m, name="attn_m_bwd", scale=scales[2], **cfg_m)
    dmkv = jnp.concatenate([dmk, dmv], axis=1)

    dproj, dqf, dkv, dg_cq, dg_ckv = _prep_bwd(
        dqa, dka, dva, dqb, dkb, dvb, dqm, dga, dgb, dgm, proj, pos, w_uq_pad_t, w_ukv, g_cq, g_ckv, rope_a, rope_b)

    small_rows = (dg_cq, dg_ckv, loss, dg_a, dg_b, dg_m, dg_post, db_post)
    return (dproj, h, y, dz, dqf, cqn, ckvn, dkv, mem2, dmkv), x2, small_rows


def _weight_grads(operands, core):
    dproj, h, y, dz, dqf, cqn, ckvn, dkv, mem2, dmkv = operands
    dw_in_arr_t = _mm(dproj, h, F32, 1024, 1024, 1024, "dw_in", mode="tn")
    g_in = _split_in(dw_in_arr_t)
    dw_out, r_in = _mm(y, dz, F32, 1024, 1024, 1024, "dw_out", mode="tn",
                       ride=_half_to_sibling(g_in.reshape(4, 2, HALF_IN, 1024)))
    dw_uq_pad_t = _mm(dqf, cqn, F32, 1024, 256, 1024, "dw_uq", mode="tn")
    dw_ukv = _mm(ckvn, dkv, F32, 128, 1024, 1024, "dw_ukv", mode="tn")
    dw_mem = _mm(mem2, dmkv, F32, 1024, 1024, mem2.shape[0], "dw_mem", mode="tn")
    g_rest = _split_rest(dw_uq_pad_t, dw_ukv, dw_mem, dw_out)
    sf_in, sb_in, r_rest = _core_sum(g_in, r_in, core, HALF_IN, HALF_IN // 2, "core_sum_in",
                                     ride=_half_to_sibling(g_rest.reshape(4, 2, HALF_REST, 1024)))
    sf_rest, sb_rest = _core_sum(g_rest, r_rest, core, HALF_REST, HALF_REST, "core_sum_rest")
    return sf_in, sb_in, sf_rest, sb_rest


def _small_block(dg_emb, db_emb, small_rows):
    dg_cq, dg_ckv, loss, dg_a, dg_b, dg_m, dg_post, db_post = small_rows
    row2 = jnp.concatenate([dg_cq, dg_ckv, loss, jnp.zeros((1, 512), F32)], axis=1)
    return jnp.concatenate([dg_emb, db_emb, row2, dg_a, jnp.concatenate([dg_b, dg_m], axis=1), dg_post, db_post,
                            jnp.zeros((1, 1024), F32)], axis=0)


def _pack_small(g_emb, b_emb, g_cq, g_ckv, g_out_a, g_out_b, g_out_m, g_post, b_post):
    row2 = jnp.concatenate([g_cq.reshape(1, -1), g_ckv.reshape(1, -1), jnp.zeros((1, 640), F32)], axis=1)
    return jnp.concatenate([g_emb.reshape(1, -1), b_emb.reshape(1, -1), row2, g_out_a.reshape(1, -1),
                            jnp.concatenate([g_out_b.reshape(1, -1), g_out_m.reshape(1, -1)], axis=1),
                            g_post.reshape(1, -1), b_post.reshape(1, -1), jnp.zeros((1, 1024), F32)], axis=0)


def _unpack_small(p):
    return [p[0], p[1], p[2:3, 0:256], p[2:3, 256:384], p[3:4], p[4:5, 0:512], p[4:5, 512:1024], p[5:6], p[6:7]]


def kernel(x, mem, positions, g_emb, b_emb, w_in, g_cq, g_ckv, w_uq, w_ukv, w_mem_kv, g_out_a, g_out_b, g_out_m, w_out, g_post, b_post, loss_target, m_g_emb, m_b_emb, m_w_in, m_g_cq, m_g_ckv, m_w_uq, m_w_ukv, m_w_mem_kv, m_g_out_a, m_g_out_b, m_g_out_m, m_w_out, m_g_post, m_b_post, v_g_emb, v_b_emb, v_w_in, v_g_cq, v_g_ckv, v_w_uq, v_w_ukv, v_w_mem_kv, v_g_out_a, v_g_out_b, v_g_out_m, v_w_out, v_g_post, v_b_post):
    w_rest = _pack_rest(w_uq, w_ukv, w_mem_kv, w_out)
    w_in_t = w_in[0].T
    w_in_b = jnp.pad(w_in_t.astype(BF16), ((0, ROWS_IN - SHARD_ROWS), (0, 0)))
    gathered_in, gathered_rest = _gather_weights(w_in_b.reshape(2, HALF_IN, 1024),
                                                 w_rest.astype(BF16).reshape(2, HALF_REST, 1024))
    weights = _full_weights(gathered_in.reshape(4, ROWS_IN, 1024), gathered_rest.reshape(4, ROWS_REST, 1024))
    gains = (g_emb.reshape(1, -1), b_emb.reshape(1, -1), g_cq, g_ckv, g_out_a, g_out_b, g_out_m, g_post, b_post)
    operands, x2, small_rows = _forward_backward(x, mem, positions, loss_target, weights, gains)

    core = lax.axis_index("c").astype(jnp.int32).reshape(1)
    chip = (2 * lax.axis_index("x") + lax.axis_index("y")).astype(jnp.int32).reshape(1)
    sf_in, sb_in, sf_rest, sb_rest = _weight_grads(operands, core)
    grad_x, dg_emb, db_emb, rb_in, rb_rest = _dh_scatter(operands[0], weights[0], x2, operands[3], gains[0],
                                                         sb_in, sb_rest)
    gh_in = _chip_sum(sf_in, rb_in, chip, HALF_IN, HALF_IN // 2, "chip_sum_in")
    gh_rest = _chip_sum(sf_rest, rb_rest, chip, HALF_REST, HALF_REST, "chip_sum_rest")
    grad_in, grad_rest = _join_halves(gh_in, gh_rest)
    grad_in = grad_in.reshape(ROWS_IN, 1024)
    grad_rest = grad_rest.reshape(ROWS_REST, 1024)

    d_in, m_in, v_in = _adamw(grad_in, w_in_t, m_w_in[0].T, v_w_in[0].T, SHARD_ROWS // 3, "adamw_in")
    d_rest, m_rest, v_rest = _adamw(
        grad_rest, w_rest, _pack_rest(m_w_uq, m_w_ukv, m_w_mem_kv, m_w_out),
        _pack_rest(v_w_uq, v_w_ukv, v_w_mem_kv, v_w_out), HALF_REST, "adamw_rest")
    small_sum = _allreduce_small(_small_block(dg_emb, db_emb, small_rows))
    d_sm, m_sm, v_sm = _adamw(
        small_sum,
        _pack_small(g_emb, b_emb, g_cq, g_ckv, g_out_a, g_out_b, g_out_m, g_post, b_post),
        _pack_small(m_g_emb, m_b_emb, m_g_cq, m_g_ckv, m_g_out_a, m_g_out_b, m_g_out_m, m_g_post, m_b_post),
        _pack_small(v_g_emb, v_b_emb, v_g_cq, v_g_ckv, v_g_out_a, v_g_out_b, v_g_out_m, v_g_post, v_b_post),
        SMALL_ROWS, "adamw_small")
    loss = small_sum[2, 384]

    def ordered(big_in, rest, sm):
        b_uq, b_ukv, b_mem, b_out = _unpack_rest(rest)
        s_gemb, s_bemb, s_gcq, s_gckv, s_ga, s_gb, s_gm, s_gpost, s_bpost = _unpack_small(sm)
        return [s_gemb, s_bemb, big_in[:SHARD_ROWS].T[None], s_gcq, s_gckv, b_uq, b_ukv, b_mem, s_ga, s_gb, s_gm,
                b_out, s_gpost, s_bpost]

    return (loss, grad_x.reshape(x.shape), *ordered(grad_in, grad_rest, small_sum), *ordered(d_in, d_rest, d_sm),
            *ordered(m_in, m_rest, m_sm), *ordered(v_in, v_rest, v_sm))
```

```python
import functools
import math

import jax
import jax.numpy as jnp
import numpy as np
from jax import lax
from jax.experimental import pallas as pl
from jax.experimental.pallas import tpu as pltpu

F32 = jnp.float32
BF16 = jnp.bfloat16
MESH = pl.DeviceIdType.MESH
ANY = pl.BlockSpec(memory_space=pl.ANY)
IN_VMEM = pl.BlockSpec(memory_space=pltpu.VMEM)

D_MODEL = 1024
A_WIDTH = 1024
MLA_HEADS = 8
MLA_Q_RANK = 256
MLA_KV_RANK = 128
MLA_QK_DIM = 96
MEM_WIDTH = 512
N_MEM = 256
ROPE_THETA = 500000.0
NORM_EPS = 1e-5
NEG_INF = -1e30
DEEPNORM_ALPHA = 2.0 ** 0.25
DILATED = ((64, 1), (256, 4), (1024, 16))

ADAM_LR = 0.001
ADAM_B1 = 0.9
ADAM_B2 = 0.999
ADAM_EPS = 1e-08
ADAM_WD = 0.01
ADAM_STEP = 10

LANES = 128
VMEM_LIMIT = 56 * 1024 * 1024
LOG2E = math.log2(math.e)
LN2 = math.log(2.0)

PROJ_W = 6144
COL_CQ = 4096
COL_BG = 4608
COL_MQ = 5120
COL_MG = 5632

SHARD_ROWS = 1512
ROWS_IN = 1536
ROWS_UQ, ROWS_UKV, ROWS_MEM, ROWS_OUT = 48, 32, 256, 512
ROWS_USED = ROWS_UQ + ROWS_UKV + ROWS_MEM + ROWS_OUT
ROWS_REST = 864
HALF_IN = ROWS_IN // 2
HALF_REST = ROWS_REST // 2
SMALL_ROWS = 8


def _params(sem=None, vmem=VMEM_LIMIT):
    return pltpu.CompilerParams(dimension_semantics=sem, vmem_limit_bytes=vmem)


def _dot(a, b):
    return jnp.dot(a, b, preferred_element_type=F32)


def _dot_nt(a, b):
    return lax.dot_general(a, b, (((1,), (1,)), ((), ())), preferred_element_type=F32)


def _dot_tn(a, b):
    return lax.dot_general(a, b, (((0,), (0,)), ((), ())), preferred_element_type=F32)


def _ln_hat(x):
    mu = jnp.mean(x, axis=-1, keepdims=True)
    xc = x - mu
    var = jnp.mean(xc * xc, axis=-1, keepdims=True)
    rstd = lax.rsqrt(var + NORM_EPS)
    return xc * rstd, rstd


def _ln_bwd_rows(dxh, xh, rstd):
    return rstd * (dxh - jnp.mean(dxh, axis=-1, keepdims=True) - xh * jnp.mean(dxh * xh, axis=-1, keepdims=True))


def _rms_hat(x, width):
    ms = jnp.sum(x * x, axis=-1, keepdims=True) * (1.0 / width)
    r = lax.rsqrt(ms + NORM_EPS)
    return x * r, r


def _rms_bwd(u, xh, r, width):
    return r * (u - xh * (jnp.sum(u * xh, axis=-1, keepdims=True) * (1.0 / width)))


def _colsum(v):
    return jnp.sum(v, axis=0, keepdims=True)


def _rope_tables(pos, consts):
    ang = pos * consts[0:1, :]
    c = jnp.cos(ang)
    s = jnp.sin(ang)
    return c, s * consts[2:3, :], -s * consts[1:2, :]


def _rope(x, tables, half, inverse=False):
    c, s_up, s_dn = tables
    if inverse:
        s_up, s_dn = -s_up, -s_dn
    return x * c + pltpu.roll(x, half, 1) * s_up + pltpu.roll(x, LANES - half, 1) * s_dn


def _ln_fwd(x, g, b, tm=512, ride=None):
    t, d = x.shape
    n_in = len(ride.args) if ride else 0
    n_out = len(ride.out_shapes) if ride else 0
    steps = t // tm

    def body(x_ref, g_ref, b_ref, *rest):
        if ride:
            i = pl.program_id(0)
            ride.run(i == 0, i == steps - 1, rest[:n_in], rest[n_in + 1:n_in + 1 + n_out], rest[n_in + 1 + n_out:])
        xh, _ = _ln_hat(x_ref[...])
        rest[n_in][...] = (xh * g_ref[...] + b_ref[...]).astype(BF16)

    row = pl.BlockSpec((1, d), lambda i: (0, 0))
    tile = pl.BlockSpec((tm, d), lambda i: (i, 0))
    h_shape = jax.ShapeDtypeStruct((t, d), BF16)
    if not ride:
        return pl.pallas_call(
            body, name="ln_fwd", grid=(steps,), out_shape=h_shape, in_specs=[tile, row, row], out_specs=tile,
            compiler_params=_params(("parallel",)),
        )(x, g, b)
    return pl.pallas_call(
        body, name="ln_fwd", grid=(steps,),
        out_shape=(h_shape, *ride.out_shapes),
        in_specs=[tile, row, row] + ride.in_specs, out_specs=(tile,) + (ANY,) * n_out,
        scratch_shapes=ride.scratch(),
        compiler_params=_params(("arbitrary",)),
    )(x, g, b, *ride.args)


class _Ride:
    def __init__(self, args, out_shapes, sem_counts, plan, in_specs=None):
        self.args, self.out_shapes, self.plan = list(args), list(out_shapes), plan
        self.sem_counts = sem_counts
        self.in_specs = in_specs or [ANY] * len(self.args)

    def scratch(self):
        return [pltpu.SemaphoreType.DMA((n,)) for n in self.sem_counts]

    def run(self, first, last, in_refs, out_refs, sems):
        @pl.when(first)
        def _():
            self.plan(in_refs, out_refs, *sems)[0]()

        @pl.when(last)
        def _():
            self.plan(in_refs, out_refs, *sems)[1]()


def _mm(a, b, out_dtype, tm, tn, tk, name, mode="nn", ride=None):
    if mode == "tn":
        k, m = a.shape
    else:
        m, k = a.shape
    n = b.shape[0] if mode == "nt" else b.shape[1]
    nk = k // tk
    nj, ni = n // tn, m // tm
    n_in = len(ride.args) if ride else 0
    n_out = len(ride.out_shapes) if ride else 0

    def body(a_ref, b_ref, *rest):
        o_ref = rest[n_in]
        acc_ref = rest[n_in + 1 + n_out]
        if ride:
            j, i, kk = pl.program_id(0), pl.program_id(1), pl.program_id(2)
            ride.run((j == 0) & (i == 0) & (kk == 0), (j == nj - 1) & (i == ni - 1) & (kk == nk - 1),
                     rest[:n_in], rest[n_in + 1:n_in + 1 + n_out], rest[n_in + 2 + n_out:])
        av = a_ref[...].astype(BF16)
        bv = b_ref[...].astype(BF16)
        part = _dot_tn(av, bv) if mode == "tn" else _dot_nt(av, bv) if mode == "nt" else _dot(av, bv)
        if nk == 1:
            o_ref[...] = part.astype(out_dtype)
        else:
            kk = pl.program_id(2)

            @pl.when(kk == 0)
            def _():
                acc_ref[...] = part

            @pl.when(kk > 0)
            def _():
                acc_ref[...] += part

            @pl.when(kk == nk - 1)
            def _():
                o_ref[...] = acc_ref[...].astype(out_dtype)

    a_spec = (pl.BlockSpec((tk, tm), lambda j, i, kk: (kk, i)) if mode == "tn"
              else pl.BlockSpec((tm, tk), lambda j, i, kk: (i, kk)))
    b_spec = (pl.BlockSpec((tn, tk), lambda j, i, kk: (j, kk)) if mode == "nt"
              else pl.BlockSpec((tk, tn), lambda j, i, kk: (kk, j)))
    o_spec = pl.BlockSpec((tm, tn), lambda j, i, kk: (i, j))
    o_shape = jax.ShapeDtypeStruct((m, n), out_dtype)
    if not ride:
        return pl.pallas_call(
            body, name=name, grid=(nj, ni, nk), out_shape=o_shape, in_specs=[a_spec, b_spec], out_specs=o_spec,
            scratch_shapes=[pltpu.VMEM((tm, tn), F32)],
            compiler_params=_params(("parallel", "parallel", "arbitrary")),
        )(a, b)
    return pl.pallas_call(
        body, name=name, grid=(nj, ni, nk),
        out_shape=(o_shape, *ride.out_shapes),
        in_specs=[a_spec, b_spec] + ride.in_specs,
        out_specs=(o_spec,) + (ANY,) * n_out,
        scratch_shapes=[pltpu.VMEM((tm, tn), F32)] + ride.scratch(),
        compiler_params=_params(("arbitrary", "arbitrary", "arbitrary")),
    )(a, b, *ride.args)


def _prep(proj, pos, w_uq, w_ukv, g_cq, g_ckv, rope_a, rope_b, scales, tm=256):
    t = proj.shape[0]
    sc_a, sc_b, sc_m = (s * LOG2E for s in scales)

    def body(aq_ref, ak_ref, av_ref, bs_ref, mq_ref, pos_ref, wuq_ref, wukv_ref, gcq_ref, gckv_ref,
             ra_ref, rb_ref, qa_ref, ka_ref, va_ref, qb_ref, kb_ref, vb_ref, qm_ref, cqn_ref, ckvn_ref):
        pos_c = pos_ref[...]
        ta = _rope_tables(pos_c, ra_ref[...])
        tb = _rope_tables(pos_c, rb_ref[...])
        for j in range(A_WIDTH // LANES):
            sl = slice(j * LANES, (j + 1) * LANES)
            qa_ref[:, sl] = (_rope(aq_ref[:, sl], ta, 8) * sc_a).astype(BF16)
            ka_ref[:, sl] = _rope(ak_ref[:, sl], ta, 8).astype(BF16)
        va_ref[...] = av_ref[...].astype(BF16)
        qm_ref[...] = (mq_ref[...] * sc_m).astype(BF16)

        cq_hat, _ = _rms_hat(bs_ref[:, 0:MLA_Q_RANK], MLA_Q_RANK)
        cqn = (cq_hat * gcq_ref[...]).astype(BF16)
        cqn_ref[...] = cqn
        ckv_hat, _ = _rms_hat(bs_ref[:, MLA_Q_RANK:MLA_Q_RANK + MLA_KV_RANK], MLA_KV_RANK)
        ckvn = (ckv_hat * gckv_ref[...]).astype(BF16)
        ckvn_ref[...] = ckvn
        qfull = _dot_nt(cqn, wuq_ref[...])
        kv = _dot(ckvn, wukv_ref[...])
        kr = _rope(bs_ref[:, 384:512], tb, 16)
        lane = lax.broadcasted_iota(jnp.int32, (1, LANES), 1)
        low = lane < 64
        for h in range(MLA_HEADS):
            sl = slice(h * LANES, (h + 1) * LANES)
            qb_ref[:, sl] = (_rope(qfull[:, sl], tb, 16) * sc_b).astype(BF16)
            kb_ref[:, sl] = jnp.where(low, kv[:, sl], kr).astype(BF16)
            vb_ref[:, sl] = jnp.where(low, 0.0, kv[:, sl]).astype(BF16)

    def col(width, idx):
        return pl.BlockSpec((tm, width), lambda i: (i, idx))

    def full(shape):
        return pl.BlockSpec(shape, lambda i: (0, 0))

    wide = jax.ShapeDtypeStruct((t, 1024), BF16)
    return pl.pallas_call(
        body, name="prep", grid=(t // tm,),
        out_shape=(wide, wide, wide, wide, wide, wide,
                   jax.ShapeDtypeStruct((t, MEM_WIDTH), BF16),
                   jax.ShapeDtypeStruct((t, MLA_Q_RANK), BF16),
                   jax.ShapeDtypeStruct((t, MLA_KV_RANK), BF16)),
        in_specs=[col(1024, 0), col(1024, 1), col(1024, 2), col(512, COL_CQ // 512), col(512, COL_MQ // 512),
                  pl.BlockSpec((tm, 1), lambda i: (i, 0)),
                  full((1024, MLA_Q_RANK)), full((MLA_KV_RANK, 1024)),
                  full((1, MLA_Q_RANK)), full((1, MLA_KV_RANK)), full((8, LANES)), full((8, LANES))],
        out_specs=(col(1024, 0),) * 6 + (col(MEM_WIDTH, 0), col(MLA_Q_RANK, 0), col(MLA_KV_RANK, 0)),
        compiler_params=_params(("parallel",)),
    )(proj, proj, proj, proj, proj, pos, w_uq, w_ukv, g_cq, g_ckv, rope_a, rope_b)


def _attn_fwd(q, k, v, *, nb, s, sk, heads, hpb, voff, bq, name):
    nq = s // bq
    width = hpb * LANES
    vblk = voff // hpb

    def body(q_ref, k_ref, v_ref, o_ref, lse_ref):
        for h in range(hpb):
            sl = slice(h * LANES, (h + 1) * LANES)
            sc = _dot_nt(q_ref[:, sl], k_ref[:, sl])
            m = jnp.max(sc, axis=1, keepdims=True)
            p = jnp.exp2(sc - m)
            l = jnp.sum(p, axis=1, keepdims=True)
            o_ref[:, sl] = _dot(p.astype(BF16), v_ref[:, sl]) / l
            lse_ref[:, sl] = jnp.broadcast_to(m + jnp.log(l) * LOG2E, (bq, LANES))

    out = jax.ShapeDtypeStruct((nb * s, heads * LANES), F32)
    ospec = pl.BlockSpec((bq, width), lambda b, i, g: (b * nq + i, g))
    return pl.pallas_call(
        body, name=name, grid=(nb, nq, heads // hpb),
        out_shape=(out, out),
        in_specs=[ospec, pl.BlockSpec((sk, width), lambda b, i, g: (b, g)),
                  pl.BlockSpec((sk, width), lambda b, i, g: (b, vblk + g))],
        out_specs=(ospec, ospec),
        compiler_params=_params(("parallel", "parallel", "parallel")),
    )(q, k, v)


def _attn_bwd(q, k, v, o, do, lse, *, nb, s, sk, heads, hpb, voff, scale, bq, name):
    nq = s // bq
    width = hpb * LANES
    vblk = voff // hpb

    def body(q_ref, k_ref, v_ref, o_ref, do_ref, lse_ref, dq_ref, dk_ref, dv_ref):
        i = pl.program_id(2)

        @pl.when(i == 0)
        def _():
            dk_ref[...] = jnp.zeros_like(dk_ref)
            dv_ref[...] = jnp.zeros_like(dv_ref)

        for h in range(hpb):
            sl = slice(h * LANES, (h + 1) * LANES)
            qh = q_ref[:, sl]
            kk = k_ref[:, sl]
            doh = do_ref[:, sl]
            delta = jnp.sum(doh.astype(F32) * o_ref[:, sl], axis=1, keepdims=True)
            p = jnp.exp2(_dot_nt(qh, kk) - lse_ref[:, h * LANES:h * LANES + 1])
            ds = (p * (_dot_nt(doh, v_ref[:, sl]) - delta)).astype(BF16)
            dq_ref[:, sl] = _dot(ds, kk) * scale
            dk_ref[:, sl] += _dot_tn(ds, qh)
            dv_ref[:, sl] += _dot_tn(p.astype(BF16), doh)

        @pl.when(i == nq - 1)
        def _():
            dk_ref[...] = dk_ref[...] * LN2

    qspec = pl.BlockSpec((bq, width), lambda b, g, i: (b * nq + i, g))
    kv_spec = pl.BlockSpec((sk, width), lambda b, g, i: (b, g))
    dq_shape = jax.ShapeDtypeStruct((nb * s, heads * LANES), F32)
    dkv_shape = jax.ShapeDtypeStruct((nb * sk, heads * LANES), F32)
    return pl.pallas_call(
        body, name=name, grid=(nb, heads // hpb, nq),
        out_shape=(dq_shape, dkv_shape, dkv_shape),
        in_specs=[qspec, kv_spec, pl.BlockSpec((sk, width), lambda b, g, i: (b, vblk + g)), qspec, qspec, qspec],
        out_specs=(qspec, kv_spec, kv_spec),
        compiler_params=_params(("parallel", "parallel", "arbitrary")),
    )(q, k, v, o, do, lse)


BAND_Q = 128
BAND_WIN = 256


def _band_start(i, s):
    return min(max(i * BAND_Q - 64, 0), s - BAND_WIN)


def _to_pattern_order(src_ref, dst_ref, stage_ref, s, d):
    length = s // d
    stage_ref[...] = src_ref[...].astype(F32)
    for r in range(d):
        dst_ref[r * length:(r + 1) * length, :] = stage_ref[pl.ds(r, length, stride=d), :].astype(dst_ref.dtype)


def _dilated_fwd(q, k, v, bias, bias_index, *, nb, s, name):
    nblk = s // BAND_Q
    npat = len(DILATED)

    def body(q_ref, k_ref, v_ref, bias_ref, o_ref, lse_ref, stage_ref, qp_ref, kp_ref, vp_ref, op_ref, lp_ref,
             on_ref, ln_ref):
        lane = lax.broadcasted_iota(jnp.int32, (1, LANES), 1)
        first = lane < 64
        for p, (_, d) in enumerate(DILATED):
            if d == 1:
                qs, ks, vs = q_ref, k_ref, v_ref
            else:
                for src, dst in ((q_ref, qp_ref), (k_ref, kp_ref), (v_ref, vp_ref)):
                    _to_pattern_order(src, dst, stage_ref, s, d)
                qs, ks, vs = qp_ref, kp_ref, vp_ref
            for i in range(nblk):
                u0 = i * BAND_Q
                st = _band_start(i, s)
                qi = qs[u0:u0 + BAND_Q, :]
                kw = ks[st:st + BAND_WIN, :]
                vw = vs[st:st + BAND_WIN, :]
                zero = jnp.zeros_like(qi)
                q2 = jnp.concatenate([jnp.where(first, qi, zero), jnp.where(first, zero, qi)], axis=0)
                sc = _dot_nt(q2, kw)
                b = bias_ref[bias_index[p][i]]
                halves = []
                for h in range(2):
                    sh = sc[h * BAND_Q:(h + 1) * BAND_Q] + b
                    m = jnp.max(sh, axis=1, keepdims=True)
                    pr = jnp.exp2(sh - m)
                    l = jnp.sum(pr, axis=1, keepdims=True)
                    halves.append((pr.astype(BF16), l, m + jnp.log(l) * LOG2E))
                o2 = _dot(jnp.concatenate([halves[0][0], halves[1][0]], axis=0), vw)
                o_blk = jnp.where(first, o2[:BAND_Q] / halves[0][1], o2[BAND_Q:] / halves[1][1])
                lse_blk = jnp.where(first, jnp.broadcast_to(halves[0][2], (BAND_Q, LANES)),
                                    jnp.broadcast_to(halves[1][2], (BAND_Q, LANES)))
                op_ref[p, u0:u0 + BAND_Q, :] = o_blk
                lp_ref[p, u0:u0 + BAND_Q, :] = lse_blk
            if d > 1:
                length = s // d
                for r in range(d):
                    on_ref.at[p - 1][pl.ds(r, length, stride=d), :] = op_ref[p, r * length:(r + 1) * length, :]
                    ln_ref.at[p - 1][pl.ds(r, length, stride=d), :] = lp_ref[p, r * length:(r + 1) * length, :]
        lses = [lp_ref[0]] + [ln_ref[p] for p in range(npat - 1)]
        outs = [op_ref[0]] + [on_ref[p] for p in range(npat - 1)]
        m = functools.reduce(jnp.maximum, lses)
        ws = [jnp.exp2(l - m) for l in lses]
        den = functools.reduce(lambda a, c: a + c, ws)
        o_ref[...] = functools.reduce(lambda a, c: a + c, [w * o for w, o in zip(ws, outs)]) / den
        lse_ref[...] = m + jnp.log(den) * LOG2E

    blk = pl.BlockSpec((s, LANES), lambda b, g: (b, g))
    out = jax.ShapeDtypeStruct((nb * s, A_WIDTH), F32)
    return pl.pallas_call(
        body, name=name, grid=(nb, A_WIDTH // LANES),
        out_shape=(out, out),
        in_specs=[blk, blk, blk, pl.BlockSpec(bias.shape, lambda b, g: (0, 0, 0))],
        out_specs=(blk, blk),
        scratch_shapes=[pltpu.VMEM((s, LANES), F32), pltpu.VMEM((s, LANES), BF16), pltpu.VMEM((s, LANES), BF16),
                        pltpu.VMEM((s, LANES), BF16), pltpu.VMEM((npat, s, LANES), F32),
                        pltpu.VMEM((npat, s, LANES), F32), pltpu.VMEM((npat - 1, s, LANES), F32),
                        pltpu.VMEM((npat - 1, s, LANES), F32)],
        compiler_params=_params(("parallel", "parallel")),
    )(q, k, v, bias)


def _dilated_bwd(q, k, v, o, do, lse, bias, bias_index, *, nb, s, scale, name):
    nblk = s // BAND_Q
    npat = len(DILATED)

    def body(q_ref, k_ref, v_ref, o_ref, do_ref, lse_ref, bias_ref, dq_ref, dk_ref, dv_ref,
             stage_ref, dl_ref, qp_ref, kp_ref, vp_ref, dop_ref, lsp_ref, dlp_ref, dqp_ref, dkp_ref, dvp_ref):
        lane = lax.broadcasted_iota(jnp.int32, (1, LANES), 1)
        first = lane < 64
        prod = do_ref[...].astype(F32) * o_ref[...]
        d0 = jnp.sum(jnp.where(first, prod, 0.0), axis=1, keepdims=True)
        d1 = jnp.sum(jnp.where(first, 0.0, prod), axis=1, keepdims=True)
        dl_ref[...] = jnp.where(first, jnp.broadcast_to(d0, (s, LANES)), jnp.broadcast_to(d1, (s, LANES)))
        for p, (_, d) in enumerate(DILATED):
            length = s // d
            if d == 1:
                qs, ks, vs, dos, lss, dls = q_ref, k_ref, v_ref, do_ref, lse_ref, dl_ref
                dqs, dks, dvs = dq_ref, dk_ref, dv_ref
            else:
                for src, dst in ((q_ref, qp_ref), (k_ref, kp_ref), (v_ref, vp_ref), (do_ref, dop_ref),
                                 (lse_ref, lsp_ref), (dl_ref, dlp_ref)):
                    _to_pattern_order(src, dst, stage_ref, s, d)
                qs, ks, vs, dos, lss, dls = qp_ref, kp_ref, vp_ref, dop_ref, lsp_ref, dlp_ref
                dqs, dks, dvs = dqp_ref, dkp_ref, dvp_ref
            dks[...] = jnp.zeros((s, LANES), F32)
            dvs[...] = jnp.zeros((s, LANES), F32)
            for i in range(nblk):
                u0 = i * BAND_Q
                st = _band_start(i, s)
                qi = qs[u0:u0 + BAND_Q, :]
                doi = dos[u0:u0 + BAND_Q, :]
                kw = ks[st:st + BAND_WIN, :]
                vw = vs[st:st + BAND_WIN, :]
                zero = jnp.zeros_like(qi)
                q2 = jnp.concatenate([jnp.where(first, qi, zero), jnp.where(first, zero, qi)], axis=0)
                do2 = jnp.concatenate([jnp.where(first, doi, zero), jnp.where(first, zero, doi)], axis=0)
                sc = _dot_nt(q2, kw)
                dp = _dot_nt(do2, vw)
                b = bias_ref[bias_index[p][i]]
                lse_i = lss[u0:u0 + BAND_Q, :]
                dl_i = dls[u0:u0 + BAND_Q, :]
                ps, dss = [], []
                for h in range(2):
                    rows = slice(h * BAND_Q, (h + 1) * BAND_Q)
                    pr = jnp.exp2(sc[rows] + b - lse_i[:, 64 * h:64 * h + 1])
                    ps.append(pr.astype(BF16))
                    dss.append((pr * (dp[rows] - dl_i[:, 64 * h:64 * h + 1])).astype(BF16))
                p2 = jnp.concatenate(ps, axis=0)
                ds2 = jnp.concatenate(dss, axis=0)
                dq2 = _dot(ds2, kw)
                dqs[u0:u0 + BAND_Q, :] = jnp.where(first, dq2[:BAND_Q], dq2[BAND_Q:]) * scale
                dks[st:st + BAND_WIN, :] += _dot_tn(ds2, q2)
                dvs[st:st + BAND_WIN, :] += _dot_tn(p2, do2)
            if d > 1:
                for dst, src in ((dq_ref, dqp_ref), (dk_ref, dkp_ref), (dv_ref, dvp_ref)):
                    for r in range(d):
                        dst[pl.ds(r, length, stride=d), :] += src[r * length:(r + 1) * length, :]
        dk_ref[...] = dk_ref[...] * LN2

    blk = pl.BlockSpec((s, LANES), lambda b, g: (b, g))
    out = jax.ShapeDtypeStruct((nb * s, A_WIDTH), F32)
    f32_buf = pltpu.VMEM((s, LANES), F32)
    bf_buf = pltpu.VMEM((s, LANES), BF16)
    return pl.pallas_call(
        body, name=name, grid=(nb, A_WIDTH // LANES),
        out_shape=(out, out, out),
        in_specs=[blk] * 6 + [pl.BlockSpec(bias.shape, lambda b, g: (0, 0, 0))],
        out_specs=(blk, blk, blk),
        scratch_shapes=[f32_buf, f32_buf, bf_buf, bf_buf, bf_buf, bf_buf, f32_buf, f32_buf, f32_buf, f32_buf, f32_buf],
        compiler_params=_params(("parallel", "parallel")),
    )(q, k, v, o, do, lse, bias)


def _post(x, ya, ybp, ym, proj, target, w_out, g_emb, b_emb, g_a, g_b, g_m, g_post, b_post, tm=256):
    t = x.shape[0]

    def body(x_ref, ya_ref, yb_ref, ym_ref, ga_ref, gb_ref, gm_ref, tg_ref, wo_ref,
             ge_ref, be_ref, goa_ref, gob_ref, gom_ref, gp_ref, bp_ref,
             y_ref, dz_ref, doa_ref, dob_ref, dom_ref, dga_ref, dgb_ref, dgm_ref,
             loss_ref, dgp_ref, dbp_ref, dgoa_ref, dgob_ref, dgom_ref):
        i = pl.program_id(0)

        @pl.when(i == 0)
        def _():
            for r in (loss_ref, dgp_ref, dbp_ref, dgoa_ref, dgob_ref, dgom_ref):
                r[...] = jnp.zeros_like(r)

        lane = lax.broadcasted_iota(jnp.int32, (1, LANES), 1)
        low = lane < 64
        xh0, _ = _ln_hat(x_ref[...])
        h = xh0 * ge_ref[...] + be_ref[...]

        ybp_v = yb_ref[...]
        yb = jnp.concatenate(
            [jnp.where(low, pltpu.roll(ybp_v[:, 2 * j * LANES:(2 * j + 1) * LANES], 64, 1),
                       ybp_v[:, (2 * j + 1) * LANES:(2 * j + 2) * LANES]) for j in range(4)], axis=1)

        def gated(raw, gate, gain, width):
            xh, r = _rms_hat(raw, width)
            n = xh * gain
            sg = 1.0 / (1.0 + jnp.exp(-gate))
            return xh, r, n, sg, n * (gate * sg)

        gate_a, gate_b, gate_m = ga_ref[...], gb_ref[...], gm_ref[...]
        xh_a, r_a, n_a, sg_a, y_a = gated(ya_ref[...], gate_a, goa_ref[...], A_WIDTH)
        xh_b, r_b, n_b, sg_b, y_b = gated(yb, gate_b, gob_ref[...], 512)
        xh_m, r_m, n_m, sg_m, y_m = gated(ym_ref[...], gate_m, gom_ref[...], 512)
        y = jnp.concatenate([y_a, y_b, y_m], axis=1).astype(BF16)
        y_ref[...] = y
        z = DEEPNORM_ALPHA * h + _dot(y, wo_ref[...])
        zh, rstd = _ln_hat(z)
        err = zh * gp_ref[...] + bp_ref[...] - tg_ref[...]
        rows = jnp.sum(err * err, axis=1, keepdims=True)
        loss_ref[...] += jnp.broadcast_to(jnp.sum(rows, axis=0, keepdims=True) * (0.5 / D_MODEL), (1, LANES))
        dout = err * (1.0 / D_MODEL)
        dgp_ref[...] += _colsum(dout * zh)
        dbp_ref[...] += _colsum(dout)
        dz = _ln_bwd_rows(dout * gp_ref[...], zh, rstd)
        dz_ref[...] = dz
        dy = _dot_nt(dz.astype(BF16), wo_ref[...])

        def gated_bwd(dyg, xh, r, n, sg, gate, gain, width, dgain_ref):
            dn = dyg * (gate * sg)
            dgate = dyg * n * (sg * (1.0 + gate * (1.0 - sg)))
            dgain_ref[...] += _colsum(dn * xh)
            return _rms_bwd(dn * gain, xh, r, width), dgate

        dya, dgate_a = gated_bwd(dy[:, 0:1024], xh_a, r_a, n_a, sg_a, gate_a, goa_ref[...], A_WIDTH, dgoa_ref)
        dyb, dgate_b = gated_bwd(dy[:, 1024:1536], xh_b, r_b, n_b, sg_b, gate_b, gob_ref[...], 512, dgob_ref)
        dym, dgate_m = gated_bwd(dy[:, 1536:2048], xh_m, r_m, n_m, sg_m, gate_m, gom_ref[...], 512, dgom_ref)
        doa_ref[...] = dya.astype(BF16)
        dom_ref[...] = dym.astype(BF16)
        dga_ref[...] = dgate_a.astype(BF16)
        dgb_ref[...] = dgate_b.astype(BF16)
        dgm_ref[...] = dgate_m.astype(BF16)
        for j in range(4):
            blk = dyb[:, j * LANES:(j + 1) * LANES]
            dob_ref[:, 2 * j * LANES:(2 * j + 1) * LANES] = jnp.where(low, 0.0, pltpu.roll(blk, 64, 1)).astype(BF16)
            dob_ref[:, (2 * j + 1) * LANES:(2 * j + 2) * LANES] = jnp.where(low, 0.0, blk).astype(BF16)

    def col(width, idx):
        return pl.BlockSpec((tm, width), lambda i: (i, idx))

    def full(shape):
        return pl.BlockSpec(shape, lambda i: (0, 0))

    def acc(width):
        return jax.ShapeDtypeStruct((1, width), F32)

    return pl.pallas_call(
        body, name="post", grid=(t // tm,),
        out_shape=(jax.ShapeDtypeStruct((t, 2048), BF16), jax.ShapeDtypeStruct((t, 1024), F32),
                   jax.ShapeDtypeStruct((t, 1024), BF16), jax.ShapeDtypeStruct((t, 1024), BF16),
                   jax.ShapeDtypeStruct((t, 512), BF16),
                   jax.ShapeDtypeStruct((t, 1024), BF16), jax.ShapeDtypeStruct((t, 512), BF16),
                   jax.ShapeDtypeStruct((t, 512), BF16),
                   acc(LANES), acc(1024), acc(1024), acc(1024), acc(512), acc(512)),
        in_specs=[col(1024, 0), col(1024, 0), col(1024, 0), col(512, 0),
                  col(1024, 3), col(512, COL_BG // 512), col(512, COL_MG // 512), col(1024, 0),
                  full((2048, 1024)),
                  full((1, 1024)), full((1, 1024)), full((1, 1024)), full((1, 512)), full((1, 512)),
                  full((1, 1024)), full((1, 1024))],
        out_specs=(col(2048, 0), col(1024, 0), col(1024, 0), col(1024, 0), col(512, 0),
                   col(1024, 0), col(512, 0), col(512, 0),
                   full((1, LANES)), full((1, 1024)), full((1, 1024)), full((1, 1024)), full((1, 512)),
                   full((1, 512))),
        compiler_params=_params(("arbitrary",)),
    )(x, ya, ybp, ym, proj, proj, proj, target, w_out, g_emb, b_emb, g_a, g_b, g_m, g_post, b_post)


def _prep_bwd(dqa, dka, dva, dqb, dkb, dvb, dqm, dga, dgb, dgm, proj, pos, w_uq, w_ukv, g_cq, g_ckv,
              rope_a, rope_b, tm=256):
    t = proj.shape[0]

    def body(dqa_ref, dka_ref, dva_ref, dqb_ref, dkb_ref, dvb_ref, dqm_ref, dga_ref, dgb_ref, dgm_ref,
             bs_ref, pos_ref, wuq_ref, wukv_ref, gcq_ref, gckv_ref, ra_ref, rb_ref,
             dproj_ref, dqf_ref, dkv_ref, dgcq_ref, dgckv_ref):
        i = pl.program_id(0)

        @pl.when(i == 0)
        def _():
            dgcq_ref[...] = jnp.zeros_like(dgcq_ref)
            dgckv_ref[...] = jnp.zeros_like(dgckv_ref)

        pos_c = pos_ref[...]
        ta = _rope_tables(pos_c, ra_ref[...])
        tb = _rope_tables(pos_c, rb_ref[...])
        for j in range(A_WIDTH // LANES):
            sl = slice(j * LANES, (j + 1) * LANES)
            dproj_ref[:, j * LANES:(j + 1) * LANES] = _rope(dqa_ref[:, sl], ta, 8, inverse=True).astype(BF16)
            dproj_ref[:, 1024 + j * LANES:1024 + (j + 1) * LANES] = (
                _rope(dka_ref[:, sl], ta, 8, inverse=True).astype(BF16))
        dproj_ref[:, 2048:3072] = dva_ref[...].astype(BF16)
        dproj_ref[:, 3072:4096] = dga_ref[...]

        lane = lax.broadcasted_iota(jnp.int32, (1, LANES), 1)
        low = lane < 64
        rope_lanes = (lane >= 64) & (lane < 96)
        dkr = jnp.zeros((tm, LANES), F32)
        for h in range(MLA_HEADS):
            sl = slice(h * LANES, (h + 1) * LANES)
            dqf_ref[:, sl] = _rope(dqb_ref[:, sl], tb, 16, inverse=True).astype(BF16)
            dk_h = dkb_ref[:, sl]
            dkv_ref[:, sl] = jnp.where(low, dk_h, dvb_ref[:, sl]).astype(BF16)
            dkr = dkr + jnp.where(rope_lanes, dk_h, 0.0)
        dkr = _rope(dkr, tb, 16, inverse=True)

        cq_hat, r_q = _rms_hat(bs_ref[:, 0:MLA_Q_RANK], MLA_Q_RANK)
        dcqn = _dot(dqf_ref[...], wuq_ref[...])
        dgcq_ref[...] += _colsum(dcqn * cq_hat)
        dproj_ref[:, COL_CQ:COL_CQ + 256] = _rms_bwd(dcqn * gcq_ref[...], cq_hat, r_q, MLA_Q_RANK).astype(BF16)
        ckv_hat, r_kv = _rms_hat(bs_ref[:, MLA_Q_RANK:MLA_Q_RANK + MLA_KV_RANK], MLA_KV_RANK)
        dckvn = _dot_nt(dkv_ref[...], wukv_ref[...])
        dgckv_ref[...] += _colsum(dckvn * ckv_hat)
        dproj_ref[:, COL_CQ + 256:COL_CQ + 384] = (
            _rms_bwd(dckvn * gckv_ref[...], ckv_hat, r_kv, MLA_KV_RANK).astype(BF16))
        dproj_ref[:, COL_CQ + 384:COL_CQ + 512] = dkr.astype(BF16)
        dproj_ref[:, COL_BG:COL_BG + 512] = dgb_ref[...]
        dproj_ref[:, COL_MQ:COL_MQ + 512] = dqm_ref[...].astype(BF16)
        dproj_ref[:, COL_MG:COL_MG + 512] = dgm_ref[...]

    def col(width, idx):
        return pl.BlockSpec((tm, width), lambda i: (i, idx))

    def full(shape):
        return pl.BlockSpec(shape, lambda i: (0, 0))

    return pl.pallas_call(
        body, name="prep_bwd", grid=(t // tm,),
        out_shape=(jax.ShapeDtypeStruct((t, PROJ_W), BF16), jax.ShapeDtypeStruct((t, 1024), BF16),
                   jax.ShapeDtypeStruct((t, 1024), BF16),
                   jax.ShapeDtypeStruct((1, MLA_Q_RANK), F32), jax.ShapeDtypeStruct((1, MLA_KV_RANK), F32)),
        in_specs=[col(1024, 0)] * 6 + [col(512, 0), col(1024, 0), col(512, 0), col(512, 0),
                  col(512, COL_CQ // 512), pl.BlockSpec((tm, 1), lambda i: (i, 0)),
                  full((1024, MLA_Q_RANK)), full((MLA_KV_RANK, 1024)),
                  full((1, MLA_Q_RANK)), full((1, MLA_KV_RANK)), full((8, LANES)), full((8, LANES))],
        out_specs=(col(PROJ_W, 0), col(1024, 0), col(1024, 0), full((1, MLA_Q_RANK)), full((1, MLA_KV_RANK))),
        compiler_params=_params(("arbitrary",)),
    )(dqa, dka, dva, dqb, dkb, dvb, dqm, dga, dgb, dgm, proj, pos, w_uq, w_ukv, g_cq, g_ckv, rope_a, rope_b)


def _adamw(g, w, m, v, tr, name):
    r, cols = w.shape

    def body(g_ref, w_ref, m_ref, v_ref, d_ref, nm_ref, nv_ref):
        gv = g_ref[...]
        m_new = ADAM_B1 * m_ref[...] + (1.0 - ADAM_B1) * gv
        v_new = ADAM_B2 * v_ref[...] + (1.0 - ADAM_B2) * (gv * gv)
        m_hat = m_new / (1.0 - ADAM_B1 ** ADAM_STEP)
        v_hat = v_new / (1.0 - ADAM_B2 ** ADAM_STEP)
        d_ref[...] = -ADAM_LR * (m_hat / (jnp.sqrt(v_hat) + ADAM_EPS) + ADAM_WD * w_ref[...])
        nm_ref[...] = m_new
        nv_ref[...] = v_new

    tile = pl.BlockSpec((tr, cols), lambda i: (i, 0))
    shape = jax.ShapeDtypeStruct((r, cols), F32)
    return pl.pallas_call(
        body, name=name, grid=(r // tr,),
        out_shape=(shape,) * 3, in_specs=[tile] * 4, out_specs=(tile,) * 3,
        compiler_params=_params(("parallel",)),
    )(g, w, m, v)


def _core_sum(g, recv, core, rows, tr, name, ride=None):
    cols = g.shape[2]
    nblk = rows // tr
    n_in = len(ride.args) if ride else 0
    n_out = len(ride.out_shapes) if ride else 0

    def body(c_ref, g_ref, r_ref, *rest):
        sf_ref, sb_ref = rest[n_in], rest[n_in + 1]
        if ride:
            j, i = pl.program_id(0), pl.program_id(1)
            ride.run((j == 0) & (i == 0), (j == 3) & (i == nblk - 1), rest[:n_in],
                     rest[n_in + 2:n_in + 2 + n_out], rest[n_in + 2 + n_out:])
        tot = g_ref[...] + r_ref[...]
        sf_ref[...] = tot
        sb_ref[...] = tot.astype(BF16)

    half = pl.BlockSpec((None, tr, cols), lambda j, i, c_ref: (j, i, 0))
    shapes = (jax.ShapeDtypeStruct((4, rows, cols), F32), jax.ShapeDtypeStruct((4, rows, cols), BF16))
    return pl.pallas_call(
        body, name=name,
        grid_spec=pltpu.PrefetchScalarGridSpec(
            num_scalar_prefetch=1, grid=(4, nblk),
            in_specs=[pl.BlockSpec((None, tr, cols), lambda j, i, c_ref: (j, c_ref[0] * nblk + i, 0)), half]
            + (ride.in_specs if ride else []),
            out_specs=(half, half) + (ANY,) * n_out,
            scratch_shapes=ride.scratch() if ride else []),
        out_shape=shapes + tuple(ride.out_shapes if ride else ()),
        compiler_params=_params(("arbitrary", "arbitrary") if ride else ("parallel", "parallel")),
    )(core, g, recv, *(ride.args if ride else ()))


def _half_to_sibling(g4):
    def plan(in_refs, out_refs, send_sems, recv_sems):
        x, y, c = _position()
        cp = pltpu.make_async_remote_copy(
            src_ref=in_refs[0].at[:, 1 - c], dst_ref=out_refs[0], send_sem=send_sems.at[0],
            recv_sem=recv_sems.at[0], device_id=(x, y, 1 - c), device_id_type=MESH)

        def finish():
            cp.wait_recv()
            cp.wait_send()

        return cp.start, finish

    return _Ride([g4], [jax.ShapeDtypeStruct((4, g4.shape[2], 1024), F32)], (1, 1), plan)


def _gather_plan(src_ref, dst_ref, send_sems, recv_sems, local_sems):
    x, y, c = _position()
    me = 2 * x + y
    local = pltpu.make_async_copy(src_ref, dst_ref.at[me], local_sems.at[0])

    def over_ici(k, src, chip):
        return pltpu.make_async_remote_copy(
            src_ref=src, dst_ref=dst_ref.at[chip, c], send_sem=send_sems.at[k - 1], recv_sem=recv_sems.at[k - 1],
            device_id=(x ^ (k >> 1), y ^ (k & 1), c), device_id_type=MESH)

    def to_sibling(k, half):
        piece = dst_ref.at[me ^ k, half]
        return pltpu.make_async_remote_copy(
            src_ref=piece, dst_ref=piece, send_sem=send_sems.at[2 + k], recv_sem=recv_sems.at[2 + k],
            device_id=(x, y, 1 - c), device_id_type=MESH)

    sends = [over_ici(k, src_ref.at[c], me) for k in (1, 2, 3)]

    def start():
        local.start()
        for cp in sends:
            cp.start()

    def finish():
        passed = []
        for k in (1, 2, 3):
            over_ici(k, dst_ref.at[me ^ k, c], me ^ k).wait_recv()
            cp = to_sibling(k, c)
            cp.start()
            passed.append(cp)
        for k in (1, 2, 3):
            to_sibling(k, 1 - c).wait_recv()
        for cp in sends + passed:
            cp.wait_send()
        local.wait()

    return start, finish


def _gather_ride(shard):
    def plan(in_refs, out_refs, send_sems, recv_sems, local_sems):
        return _gather_plan(in_refs[0], out_refs[0], send_sems, recv_sems, local_sems)

    return _Ride([shard], [jax.ShapeDtypeStruct((4,) + shard.shape, shard.dtype)], (6, 6, 1), plan,
                 in_specs=[IN_VMEM])


def _chip_sum(sf, recv, chip, rows, tr, name):
    cols = sf.shape[2]

    def body(me_ref, sf_ref, r_ref, out_ref):
        acc = sf_ref[...]
        for k in range(3):
            acc = acc + r_ref[k].astype(F32)
        out_ref[...] = acc

    return pl.pallas_call(
        body, name=name,
        grid_spec=pltpu.PrefetchScalarGridSpec(
            num_scalar_prefetch=1, grid=(rows // tr,),
            in_specs=[pl.BlockSpec((None, tr, cols), lambda i, me_ref: (me_ref[0], i, 0)),
                      pl.BlockSpec((3, tr, cols), lambda i, me_ref: (0, i, 0))],
            out_specs=pl.BlockSpec((tr, cols), lambda i, me_ref: (i, 0))),
        out_shape=jax.ShapeDtypeStruct((rows, cols), F32),
        compiler_params=_params(("parallel",)),
    )(chip, sf, recv)


def _position():
    return lax.axis_index("x"), lax.axis_index("y"), lax.axis_index("c")


def _dh_scatter(dproj, w_in_arr_t, x, dz, g, sb_in, sb_rest, tm=1024, tk=1024):
    t, d = x.shape
    nk = dproj.shape[1] // tk
    ni = t // tm

    def body(dp_ref, w_ref, x_ref, dz_ref, g_ref, sbin_ref, sbrest_ref,
             dx_ref, dg_ref, db_ref, rin_ref, rrest_ref, acc_ref, send_sems, recv_sems):
        i = pl.program_id(0)
        kk = pl.program_id(1)
        px, py, pc = _position()
        me = 2 * px + py
        srcs = (sbin_ref, sbrest_ref)
        dsts = (rin_ref, rrest_ref)

        def copy(a, k):
            return pltpu.make_async_remote_copy(
                src_ref=srcs[a].at[me ^ k], dst_ref=dsts[a].at[k - 1],
                send_sem=send_sems.at[3 * a + k - 1], recv_sem=recv_sems.at[3 * a + k - 1],
                device_id=(px ^ (k >> 1), py ^ (k & 1), pc), device_id_type=MESH)

        pairs = [(a, k) for a in range(2) for k in (1, 2, 3)]

        @pl.when((i == 0) & (kk == 0))
        def _():
            dg_ref[...] = jnp.zeros_like(dg_ref)
            db_ref[...] = jnp.zeros_like(db_ref)
            for a, k in pairs:
                copy(a, k).start()

        part = _dot(dp_ref[...], w_ref[...])

        @pl.when(kk == 0)
        def _():
            acc_ref[...] = part

        @pl.when(kk > 0)
        def _():
            acc_ref[...] += part

        @pl.when(kk == nk - 1)
        def _():
            xh, rstd = _ln_hat(x_ref[...])
            dht = acc_ref[...] + DEEPNORM_ALPHA * dz_ref[...]
            dg_ref[...] += _colsum(dht * xh)
            db_ref[...] += _colsum(dht)
            dx_ref[...] = _ln_bwd_rows(dht * g_ref[...], xh, rstd)

        @pl.when((i == ni - 1) & (kk == nk - 1))
        def _():
            for a, k in pairs:
                copy(a, k).wait_recv()
            for a, k in pairs:
                copy(a, k).wait_send()

    tile = pl.BlockSpec((tm, d), lambda i, kk: (i, 0))
    row = pl.BlockSpec((1, d), lambda i, kk: (0, 0))
    return pl.pallas_call(
        body, name="dh_scatter", grid=(ni, nk),
        out_shape=(jax.ShapeDtypeStruct((t, d), F32), jax.ShapeDtypeStruct((1, d), F32),
                   jax.ShapeDtypeStruct((1, d), F32),
                   jax.ShapeDtypeStruct((3, HALF_IN, 1024), BF16),
                   jax.ShapeDtypeStruct((3, HALF_REST, 1024), BF16)),
        in_specs=[pl.BlockSpec((tm, tk), lambda i, kk: (i, kk)), pl.BlockSpec((tk, d), lambda i, kk: (kk, 0)),
                  tile, tile, row, ANY, ANY],
        out_specs=(tile, row, row, ANY, ANY),
        scratch_shapes=[pltpu.VMEM((tm, d), F32), pltpu.SemaphoreType.DMA((6,)), pltpu.SemaphoreType.DMA((6,))],
        compiler_params=_params(("arbitrary", "arbitrary")),
    )(dproj, w_in_arr_t, x, dz, g, sb_in, sb_rest)


def _join_halves(gh_in, gh_rest):
    def body(hin_ref, hrest_ref, oin_ref, orest_ref, send_sems, recv_sems, local_sems):
        x, y, c = _position()
        srcs = (hin_ref, hrest_ref)
        dsts = (oin_ref, orest_ref)

        def rows(a, half):
            return dsts[a].at[half]

        local = [pltpu.make_async_copy(srcs[a], rows(a, c), local_sems.at[a]) for a in range(2)]
        remote = [pltpu.make_async_remote_copy(
            src_ref=srcs[a], dst_ref=rows(a, c), send_sem=send_sems.at[a], recv_sem=recv_sems.at[a],
            device_id=(x, y, 1 - c), device_id_type=MESH) for a in range(2)]
        for cp in local + remote:
            cp.start()
        for a in range(2):
            pltpu.make_async_remote_copy(
                src_ref=srcs[a], dst_ref=rows(a, 1 - c), send_sem=send_sems.at[a], recv_sem=recv_sems.at[a],
                device_id=(x, y, 1 - c), device_id_type=MESH).wait_recv()
        for cp in remote:
            cp.wait_send()
        for cp in local:
            cp.wait()

    return pl.pallas_call(
        body, name="join_halves",
        out_shape=(jax.ShapeDtypeStruct((2, HALF_IN, 1024), F32),
                   jax.ShapeDtypeStruct((2, HALF_REST, 1024), F32)),
        in_specs=[IN_VMEM, IN_VMEM], out_specs=(ANY, ANY),
        scratch_shapes=[pltpu.SemaphoreType.DMA((2,)), pltpu.SemaphoreType.DMA((2,)), pltpu.SemaphoreType.DMA((2,))],
    )(gh_in, gh_rest)


def _allreduce_small(vec):
    def body(vec_ref, out_ref, all_ref, send_sems, recv_sems):
        x, y, c = _position()
        me = 4 * x + 2 * y + c
        all_ref[me] = vec_ref[...]

        def copy(k, slot):
            return pltpu.make_async_remote_copy(
                src_ref=vec_ref, dst_ref=all_ref.at[slot], send_sem=send_sems.at[k - 1], recv_sem=recv_sems.at[k - 1],
                device_id=(x ^ (k >> 2), y ^ ((k >> 1) & 1), c ^ (k & 1)), device_id_type=MESH)

        copies = [copy(k, me) for k in range(1, 8)]
        for cp in copies:
            cp.start()
        for k in range(1, 8):
            copy(k, me ^ k).wait_recv()
        for cp in copies:
            cp.wait_send()
        total = all_ref[0]
        for d in range(1, 8):
            total = total + all_ref[d]
        out_ref[...] = total

    return pl.pallas_call(
        body, name="allreduce_small",
        out_shape=jax.ShapeDtypeStruct(vec.shape, vec.dtype),
        in_specs=[pl.BlockSpec(memory_space=pltpu.VMEM)], out_specs=pl.BlockSpec(memory_space=pltpu.VMEM),
        scratch_shapes=[pltpu.VMEM((8,) + vec.shape, vec.dtype), pltpu.SemaphoreType.DMA((7,)),
                        pltpu.SemaphoreType.DMA((7,))],
    )(vec)


def _pack_rest(w_uq, w_ukv, w_mem, w_out):
    rows = jnp.concatenate([w_uq[0].T.reshape(-1, 1024), w_ukv.reshape(-1, 1024), w_mem.reshape(-1, 1024),
                            w_out.reshape(-1, 1024)], axis=0)
    return jnp.pad(rows, ((0, ROWS_REST - ROWS_USED), (0, 0)))


def _unpack_rest(p):
    uq = p[0:ROWS_UQ].reshape(192, 256).T[None]
    o = ROWS_UQ
    out = [uq]
    for rows, shape in ((ROWS_UKV, (1, 128, 256)), (ROWS_MEM, (1, 256, 1024)), (ROWS_OUT, (1, 512, 1024))):
        out.append(p[o:o + rows].reshape(shape))
        o += rows
    return out


def _arranged_w_in(g_in):
    z = functools.partial(jnp.zeros, dtype=g_in.dtype)
    cut = 4480 - 2 * SHARD_ROWS
    return jnp.concatenate(
        [g_in[0, :SHARD_ROWS], g_in[1, :SHARD_ROWS], g_in[2, :cut], z((64, 1024)), g_in[2, cut:cut + 32],
         z((32, 1024)), g_in[2, cut + 32:SHARD_ROWS], g_in[3, :SHARD_ROWS]], axis=0)


def _rest_weights(g_rest):
    w_uq_t = g_rest[:, 0:ROWS_UQ].reshape(768, 256)
    w_uq_pad_t = jnp.pad(w_uq_t.reshape(MLA_HEADS, MLA_QK_DIM, 256), ((0, 0), (0, 32), (0, 0))).reshape(1024, 256)
    w_ukv = jnp.concatenate([g_rest[j, ROWS_UQ:ROWS_UQ + ROWS_UKV].reshape(128, 256) for j in range(4)], axis=1)
    lo = ROWS_UQ + ROWS_UKV
    w_mem = g_rest[:, lo:lo + ROWS_MEM].reshape(4 * ROWS_MEM, 1024)
    w_out = g_rest[:, lo + ROWS_MEM:lo + ROWS_MEM + ROWS_OUT].reshape(4 * ROWS_OUT, 1024)
    return w_uq_pad_t, w_ukv, w_mem, w_out


def _split_in(dw_in_arr_t):
    dw_in_t = jnp.concatenate([dw_in_arr_t[:4480], dw_in_arr_t[4544:4576], dw_in_arr_t[4608:]], axis=0)
    return jnp.pad(dw_in_t.reshape(4, SHARD_ROWS, 1024), ((0, 0), (0, ROWS_IN - SHARD_ROWS), (0, 0)))


def _split_rest(dw_uq_pad_t, dw_ukv, dw_mem, dw_out):
    dw_uq_t = dw_uq_pad_t.reshape(MLA_HEADS, LANES, 256)[:, :MLA_QK_DIM].reshape(4, ROWS_UQ, 1024)
    parts = [dw_uq_t, dw_ukv.reshape(128, 4, 256).transpose(1, 0, 2).reshape(4, ROWS_UKV, 1024),
             dw_mem.reshape(4, ROWS_MEM, 1024), dw_out.reshape(4, ROWS_OUT, 1024)]
    return jnp.pad(jnp.concatenate(parts, axis=1), ((0, 0), (0, ROWS_REST - ROWS_USED), (0, 0)))


def _rope_consts(rot, first, period):
    half = rot // 2
    inv_freq = np.float32(ROPE_THETA) ** (-(np.arange(0, rot, 2, dtype=np.float32) / np.float32(rot)))
    lane = np.arange(LANES) % period - first
    in_rot = (lane >= 0) & (lane < rot)
    out = np.zeros((8, LANES), np.float32)
    out[0] = np.where(in_rot, inv_freq[np.clip(lane, 0, rot - 1) % half], 0.0)
    out[1] = in_rot & (lane < half)
    out[2] = in_rot & (lane >= half)
    return jnp.asarray(out)


def _band_bias(s):
    nblk = s // BAND_Q
    starts = np.array([_band_start(i, s) for i in range(nblk)])
    uq = (np.arange(nblk)[:, None] * BAND_Q + np.arange(BAND_Q)[None, :])[:, :, None]
    uk = (starts[:, None] + np.arange(BAND_WIN)[None, :])[:, None, :]
    tiles, index, seen = [], [], {}
    for _, d in DILATED:
        length = s // d
        ok = (uq // length == uk // length) & (np.abs(uq - uk) <= 64)
        row = []
        for i in range(nblk):
            key = ok[i].tobytes()
            if key not in seen:
                seen[key] = len(tiles)
                tiles.append(np.where(ok[i], 0.0, NEG_INF).astype(np.float32))
            row.append(seen[key])
        index.append(row)
    return jnp.asarray(np.stack(tiles, axis=0)), index


def _forward_backward(h, proj, x, mem, positions, target, weights, gains):
    w_uq_pad_t, w_ukv, w_mem, w_out = weights
    g_emb, b_emb, g_cq, g_ckv, g_out_a, g_out_b, g_out_m, g_post, b_post = gains
    nb, s, d = x.shape
    t = nb * s
    x2 = x.reshape(t, d)
    mem2 = mem.reshape(nb * N_MEM, d)
    tgt2 = target.reshape(t, d)
    pos = positions.reshape(t, 1).astype(F32)
    rope_a = _rope_consts(16, 0, 64)
    rope_b = _rope_consts(32, 64, 128)
    bias, bias_index = _band_bias(s)
    scales = (0.125, MLA_QK_DIM ** -0.5, 128 ** -0.5)

    qa, ka, va, qb, kb, vb, qm, cqn, ckvn = _prep(proj, pos, w_uq_pad_t, w_ukv, g_cq, g_ckv, rope_a, rope_b, scales)
    mkv = _mm(mem2, w_mem, BF16, nb * N_MEM, 1024, 1024, "mem_kv")

    cfg_b = dict(nb=nb, s=s, sk=s, heads=8, hpb=2, voff=0, bq=256)
    cfg_m = dict(nb=nb, s=s, sk=N_MEM, heads=4, hpb=2, voff=4, bq=1024)
    ya, lse_a = _dilated_fwd(qa, ka, va, bias, bias_index, nb=nb, s=s, name="attn_a_fwd")
    yb, lse_b = _attn_fwd(qb, kb, vb, name="attn_b_fwd", **cfg_b)
    ym, lse_m = _attn_fwd(qm, mkv, mkv, name="attn_m_fwd", **cfg_m)

    (y, dz, doa, dob, dom, dga, dgb, dgm, loss, dg_post, db_post, dg_a, dg_b, dg_m) = _post(
        x2, ya, yb, ym, proj, tgt2, w_out, g_emb, b_emb, g_out_a, g_out_b, g_out_m, g_post, b_post)

    dqa, dka, dva = _dilated_bwd(qa, ka, va, ya, doa, lse_a, bias, bias_index, nb=nb, s=s, scale=scales[0],
                                 name="attn_a_bwd")
    dqb, dkb, dvb = _attn_bwd(qb, kb, vb, yb, dob, lse_b, name="attn_b_bwd", scale=scales[1], **cfg_b)
    dqm, dmk, dmv = _attn_bwd(qm, mkv, mkv, ym, dom, lse_m, name="attn_m_bwd", scale=scales[2], **cfg_m)
    dmkv = jnp.concatenate([dmk, dmv], axis=1)

    dproj, dqf, dkv, dg_cq, dg_ckv = _prep_bwd(
        dqa, dka, dva, dqb, dkb, dvb, dqm, dga, dgb, dgm, proj, pos, w_uq_pad_t, w_ukv, g_cq, g_ckv, rope_a, rope_b)

    small_rows = (dg_cq, dg_ckv, loss, dg_a, dg_b, dg_m, dg_post, db_post)
    return (dproj, h, y, dz, dqf, cqn, ckvn, dkv, mem2, dmkv), x2, small_rows


def _weight_grads(operands, core):
    dproj, h, y, dz, dqf, cqn, ckvn, dkv, mem2, dmkv = operands
    dw_in_arr_t = _mm(dproj, h, F32, 1024, 1024, 1024, "dw_in", mode="tn")
    g_in = _split_in(dw_in_arr_t)
    dw_out, r_in = _mm(y, dz, F32, 1024, 1024, 1024, "dw_out", mode="tn",
                       ride=_half_to_sibling(g_in.reshape(4, 2, HALF_IN, 1024)))
    dw_uq_pad_t = _mm(dqf, cqn, F32, 1024, 256, 1024, "dw_uq", mode="tn")
    dw_ukv = _mm(ckvn, dkv, F32, 128, 1024, 1024, "dw_ukv", mode="tn")
    dw_mem = _mm(mem2, dmkv, F32, 1024, 1024, mem2.shape[0], "dw_mem", mode="tn")
    g_rest = _split_rest(dw_uq_pad_t, dw_ukv, dw_mem, dw_out)
    sf_in, sb_in, r_rest = _core_sum(g_in, r_in, core, HALF_IN, HALF_IN // 2, "core_sum_in",
                                     ride=_half_to_sibling(g_rest.reshape(4, 2, HALF_REST, 1024)))
    sf_rest, sb_rest = _core_sum(g_rest, r_rest, core, HALF_REST, HALF_REST, "core_sum_rest")
    return sf_in, sb_in, sf_rest, sb_rest


def _small_block(dg_emb, db_emb, small_rows):
    dg_cq, dg_ckv, loss, dg_a, dg_b, dg_m, dg_post, db_post = small_rows
    row2 = jnp.concatenate([dg_cq, dg_ckv, loss, jnp.zeros((1, 512), F32)], axis=1)
    return jnp.concatenate([dg_emb, db_emb, row2, dg_a, jnp.concatenate([dg_b, dg_m], axis=1), dg_post, db_post,
                            jnp.zeros((1, 1024), F32)], axis=0)


def _pack_small(g_emb, b_emb, g_cq, g_ckv, g_out_a, g_out_b, g_out_m, g_post, b_post):
    row2 = jnp.concatenate([g_cq.reshape(1, -1), g_ckv.reshape(1, -1), jnp.zeros((1, 640), F32)], axis=1)
    return jnp.concatenate([g_emb.reshape(1, -1), b_emb.reshape(1, -1), row2, g_out_a.reshape(1, -1),
                            jnp.concatenate([g_out_b.reshape(1, -1), g_out_m.reshape(1, -1)], axis=1),
                            g_post.reshape(1, -1), b_post.reshape(1, -1), jnp.zeros((1, 1024), F32)], axis=0)


def _unpack_small(p):
    return [p[0], p[1], p[2:3, 0:256], p[2:3, 256:384], p[3:4], p[4:5, 0:512], p[4:5, 512:1024], p[5:6], p[6:7]]


def kernel(x, mem, positions, g_emb, b_emb, w_in, g_cq, g_ckv, w_uq, w_ukv, w_mem_kv, g_out_a, g_out_b, g_out_m, w_out, g_post, b_post, loss_target, m_g_emb, m_b_emb, m_w_in, m_g_cq, m_g_ckv, m_w_uq, m_w_ukv, m_w_mem_kv, m_g_out_a, m_g_out_b, m_g_out_m, m_w_out, m_g_post, m_b_post, v_g_emb, v_b_emb, v_w_in, v_g_cq, v_g_ckv, v_w_uq, v_w_ukv, v_w_mem_kv, v_g_out_a, v_g_out_b, v_g_out_m, v_w_out, v_g_post, v_b_post):
    w_rest = _pack_rest(w_uq, w_ukv, w_mem_kv, w_out)
    w_in_t = w_in[0].T
    w_in_b = jnp.pad(w_in_t.astype(BF16), ((0, ROWS_IN - SHARD_ROWS), (0, 0)))
    gains = (g_emb.reshape(1, -1), b_emb.reshape(1, -1), g_cq, g_ckv, g_out_a, g_out_b, g_out_m, g_post, b_post)
    h, gathered_in = _ln_fwd(x.reshape(-1, D_MODEL), gains[0], gains[1],
                             ride=_gather_ride(w_in_b.reshape(2, HALF_IN, 1024)))
    w_in_arr_t = _arranged_w_in(gathered_in.reshape(4, ROWS_IN, 1024))
    proj, gathered_rest = _mm(h, w_in_arr_t, F32, 1024, 1024, 1024, "in_proj", mode="nt",
                              ride=_gather_ride(w_rest.astype(BF16).reshape(2, HALF_REST, 1024)))
    weights = _rest_weights(gathered_rest.reshape(4, ROWS_REST, 1024))
    operands, x2, small_rows = _forward_backward(h, proj, x, mem, positions, loss_target, weights, gains)

    core = lax.axis_index("c").astype(jnp.int32).reshape(1)
    chip = (2 * lax.axis_index("x") + lax.axis_index("y")).astype(jnp.int32).reshape(1)
    sf_in, sb_in, sf_rest, sb_rest = _weight_grads(operands, core)
    grad_x, dg_emb, db_emb, rb_in, rb_rest = _dh_scatter(operands[0], w_in_arr_t, x2, operands[3], gains[0],
                                                         sb_in, sb_rest)
    gh_in = _chip_sum(sf_in, rb_in, chip, HALF_IN, HALF_IN // 2, "chip_sum_in")
    gh_rest = _chip_sum(sf_rest, rb_rest, chip, HALF_REST, HALF_REST, "chip_sum_rest")
    grad_in, grad_rest = _join_halves(gh_in, gh_rest)
    grad_in = grad_in.reshape(ROWS_IN, 1024)
    grad_rest = grad_rest.reshape(ROWS_REST, 1024)

    d_in, m_in, v_in = _adamw(grad_in, w_in_t, m_w_in[0].T, v_w_in[0].T, SHARD_ROWS // 3, "adamw_in")
    d_rest, m_rest, v_rest = _adamw(
        grad_rest, w_rest, _pack_rest(m_w_uq, m_w_ukv, m_w_mem_kv, m_w_out),
        _pack_rest(v_w_uq, v_w_ukv, v_w_mem_kv, v_w_out), HALF_REST, "adamw_rest")
    small_sum = _allreduce_small(_small_block(dg_emb, db_emb, small_rows))
    d_sm, m_sm, v_sm = _adamw(
        small_sum,
        _pack_small(g_emb, b_emb, g_cq, g_ckv, g_out_a, g_out_b, g_out_m, g_post, b_post),
        _pack_small(m_g_emb, m_b_emb, m_g_cq, m_g_ckv, m_g_out_a, m_g_out_b, m_g_out_m, m_g_post, m_b_post),
        _pack_small(v_g_emb, v_b_emb, v_g_cq, v_g_ckv, v_g_out_a, v_g_out_b, v_g_out_m, v_g_post, v_b_post),
        SMALL_ROWS, "adamw_small")
    loss = small_sum[2, 384]

    def ordered(big_in, rest, sm):
        b_uq, b_ukv, b_mem, b_out = _unpack_rest(rest)
        s_gemb, s_bemb, s_gcq, s_gckv, s_ga, s_gb, s_gm, s_gpost, s_bpost = _unpack_small(sm)
        return [s_gemb, s_bemb, big_in[:SHARD_ROWS].T[None], s_gcq, s_gckv, b_uq, b_ukv, b_mem, s_ga, s_gb, s_gm,
                b_out, s_gpost, s_bpost]

    return (loss, grad_x.reshape(x.shape), *ordered(grad_in, grad_rest, small_sum), *ordered(d_in, d_rest, d_sm),
            *ordered(m_in, m_rest, m_sm), *ordered(v_in, v_rest, v_sm))
```

```python
import functools
import math

import jax
import jax.numpy as jnp
import numpy as np
from jax import lax
from jax.experimental import pallas as pl
from jax.experimental.pallas import tpu as pltpu

F32 = jnp.float32
BF16 = jnp.bfloat16
MESH = pl.DeviceIdType.MESH
ANY = pl.BlockSpec(memory_space=pl.ANY)
IN_VMEM = pl.BlockSpec(memory_space=pltpu.VMEM)

D_MODEL = 1024
A_WIDTH = 1024
MLA_HEADS = 8
MLA_Q_RANK = 256
MLA_KV_RANK = 128
MLA_QK_DIM = 96
MEM_WIDTH = 512
N_MEM = 256
ROPE_THETA = 500000.0
NORM_EPS = 1e-5
NEG_INF = -1e30
DEEPNORM_ALPHA = 2.0 ** 0.25
DILATED = ((64, 1), (256, 4), (1024, 16))

ADAM_LR = 0.001
ADAM_B1 = 0.9
ADAM_B2 = 0.999
ADAM_EPS = 1e-08
ADAM_WD = 0.01
ADAM_STEP = 10

LANES = 128
VMEM_LIMIT = 56 * 1024 * 1024
LOG2E = math.log2(math.e)
LN2 = math.log(2.0)

PROJ_W = 6144
COL_CQ = 4096
COL_BG = 4608
COL_MQ = 5120
COL_MG = 5632

SHARD_ROWS = 1512
ROWS_IN = 1536
ROWS_UQ, ROWS_UKV, ROWS_MEM, ROWS_OUT = 48, 32, 256, 512
ROWS_USED = ROWS_UQ + ROWS_UKV + ROWS_MEM + ROWS_OUT
ROWS_REST = 864
HALF_IN = ROWS_IN // 2
HALF_REST = ROWS_REST // 2
REST_PIECES = ((0, 48, 0, 1024), (48, 80, 0, 1024), (80, 336, 0, 1024), (336, 848, 0, 1024))
SMALL_PIECES = ((0, 1, 0, 1024), (1, 2, 0, 1024), (2, 3, 0, 256), (2, 3, 256, 384), (3, 4, 0, 1024), (4, 5, 0, 512),
                (4, 5, 512, 1024), (5, 6, 0, 1024), (6, 7, 0, 1024))


def _params(sem=None, vmem=VMEM_LIMIT):
    return pltpu.CompilerParams(dimension_semantics=sem, vmem_limit_bytes=vmem)


def _dot(a, b):
    return jnp.dot(a, b, preferred_element_type=F32)


def _dot_nt(a, b):
    return lax.dot_general(a, b, (((1,), (1,)), ((), ())), preferred_element_type=F32)


def _dot_tn(a, b):
    return lax.dot_general(a, b, (((0,), (0,)), ((), ())), preferred_element_type=F32)


def _ln_hat(x):
    mu = jnp.mean(x, axis=-1, keepdims=True)
    xc = x - mu
    var = jnp.mean(xc * xc, axis=-1, keepdims=True)
    rstd = lax.rsqrt(var + NORM_EPS)
    return xc * rstd, rstd


def _ln_bwd_rows(dxh, xh, rstd):
    return rstd * (dxh - jnp.mean(dxh, axis=-1, keepdims=True) - xh * jnp.mean(dxh * xh, axis=-1, keepdims=True))


def _rms_hat(x, width):
    ms = jnp.sum(x * x, axis=-1, keepdims=True) * (1.0 / width)
    r = lax.rsqrt(ms + NORM_EPS)
    return x * r, r


def _rms_bwd(u, xh, r, width):
    return r * (u - xh * (jnp.sum(u * xh, axis=-1, keepdims=True) * (1.0 / width)))


def _colsum(v):
    return jnp.sum(v, axis=0, keepdims=True)


def _rope_tables(pos, consts):
    ang = pos * consts[0:1, :]
    c = jnp.cos(ang)
    s = jnp.sin(ang)
    return c, s * consts[2:3, :], -s * consts[1:2, :]


def _rope(x, tables, half, inverse=False):
    c, s_up, s_dn = tables
    if inverse:
        s_up, s_dn = -s_up, -s_dn
    return x * c + pltpu.roll(x, half, 1) * s_up + pltpu.roll(x, LANES - half, 1) * s_dn


def _ln_fwd(x, g, b, tm=512, ride=None):
    t, d = x.shape
    n_in = len(ride.args) if ride else 0
    n_out = len(ride.out_shapes) if ride else 0
    steps = t // tm

    def body(x_ref, g_ref, b_ref, *rest):
        if ride:
            i = pl.program_id(0)
            ride.run(i == 0, i == steps - 1, rest[:n_in], rest[n_in + 1:n_in + 1 + n_out], rest[n_in + 1 + n_out:])
        xh, _ = _ln_hat(x_ref[...])
        rest[n_in][...] = (xh * g_ref[...] + b_ref[...]).astype(BF16)

    row = pl.BlockSpec((1, d), lambda i: (0, 0))
    tile = pl.BlockSpec((tm, d), lambda i: (i, 0))
    h_shape = jax.ShapeDtypeStruct((t, d), BF16)
    if not ride:
        return pl.pallas_call(
            body, name="ln_fwd", grid=(steps,), out_shape=h_shape, in_specs=[tile, row, row], out_specs=tile,
            compiler_params=_params(("parallel",)),
        )(x, g, b)
    return pl.pallas_call(
        body, name="ln_fwd", grid=(steps,),
        out_shape=(h_shape, *ride.out_shapes),
        in_specs=[tile, row, row] + ride.in_specs, out_specs=(tile,) + (ANY,) * n_out,
        scratch_shapes=ride.scratch(),
        compiler_params=_params(("arbitrary",)),
    )(x, g, b, *ride.args)


class _Ride:
    def __init__(self, args, out_shapes, sem_counts, plan, in_specs=None):
        self.args, self.out_shapes, self.plan = list(args), list(out_shapes), plan
        self.sem_counts = sem_counts
        self.in_specs = in_specs or [ANY] * len(self.args)

    def scratch(self):
        return [pltpu.SemaphoreType.DMA((n,)) for n in self.sem_counts]

    def run(self, first, last, in_refs, out_refs, sems):
        @pl.when(first)
        def _():
            self.plan(in_refs, out_refs, *sems)[0]()

        @pl.when(last)
        def _():
            self.plan(in_refs, out_refs, *sems)[1]()


def _mm(a, b, out_dtype, tm, tn, tk, name, mode="nn", ride=None):
    if mode == "tn":
        k, m = a.shape
    else:
        m, k = a.shape
    n = b.shape[0] if mode == "nt" else b.shape[1]
    nk = k // tk
    nj, ni = n // tn, m // tm
    n_in = len(ride.args) if ride else 0
    n_out = len(ride.out_shapes) if ride else 0

    def body(a_ref, b_ref, *rest):
        o_ref = rest[n_in]
        acc_ref = rest[n_in + 1 + n_out]
        if ride:
            j, i, kk = pl.program_id(0), pl.program_id(1), pl.program_id(2)
            ride.run((j == 0) & (i == 0) & (kk == 0), (j == nj - 1) & (i == ni - 1) & (kk == nk - 1),
                     rest[:n_in], rest[n_in + 1:n_in + 1 + n_out], rest[n_in + 2 + n_out:])
        av = a_ref[...].astype(BF16)
        bv = b_ref[...].astype(BF16)
        part = _dot_tn(av, bv) if mode == "tn" else _dot_nt(av, bv) if mode == "nt" else _dot(av, bv)
        if nk == 1:
            o_ref[...] = part.astype(out_dtype)
        else:
            kk = pl.program_id(2)

            @pl.when(kk == 0)
            def _():
                acc_ref[...] = part

            @pl.when(kk > 0)
            def _():
                acc_ref[...] += part

            @pl.when(kk == nk - 1)
            def _():
                o_ref[...] = acc_ref[...].astype(out_dtype)

    a_spec = (pl.BlockSpec((tk, tm), lambda j, i, kk: (kk, i)) if mode == "tn"
              else pl.BlockSpec((tm, tk), lambda j, i, kk: (i, kk)))
    b_spec = (pl.BlockSpec((tn, tk), lambda j, i, kk: (j, kk)) if mode == "nt"
              else pl.BlockSpec((tk, tn), lambda j, i, kk: (kk, j)))
    o_spec = pl.BlockSpec((tm, tn), lambda j, i, kk: (i, j))
    o_shape = jax.ShapeDtypeStruct((m, n), out_dtype)
    if not ride:
        return pl.pallas_call(
            body, name=name, grid=(nj, ni, nk), out_shape=o_shape, in_specs=[a_spec, b_spec], out_specs=o_spec,
            scratch_shapes=[pltpu.VMEM((tm, tn), F32)],
            compiler_params=_params(("parallel", "parallel", "arbitrary")),
        )(a, b)
    return pl.pallas_call(
        body, name=name, grid=(nj, ni, nk),
        out_shape=(o_shape, *ride.out_shapes),
        in_specs=[a_spec, b_spec] + ride.in_specs,
        out_specs=(o_spec,) + (ANY,) * n_out,
        scratch_shapes=[pltpu.VMEM((tm, tn), F32)] + ride.scratch(),
        compiler_params=_params(("arbitrary", "arbitrary", "arbitrary")),
    )(a, b, *ride.args)


def _prep(proj, pos, w_uq, w_ukv, g_cq, g_ckv, rope_a, rope_b, scales, tm=256):
    t = proj.shape[0]
    sc_a, sc_b, sc_m = (s * LOG2E for s in scales)

    def body(aq_ref, ak_ref, av_ref, bs_ref, mq_ref, pos_ref, wuq_ref, wukv_ref, gcq_ref, gckv_ref,
             ra_ref, rb_ref, qa_ref, ka_ref, va_ref, qb_ref, kb_ref, vb_ref, qm_ref, cqn_ref, ckvn_ref):
        pos_c = pos_ref[...]
        ta = _rope_tables(pos_c, ra_ref[...])
        tb = _rope_tables(pos_c, rb_ref[...])
        for j in range(A_WIDTH // LANES):
            sl = slice(j * LANES, (j + 1) * LANES)
            qa_ref[:, sl] = (_rope(aq_ref[:, sl], ta, 8) * sc_a).astype(BF16)
            ka_ref[:, sl] = _rope(ak_ref[:, sl], ta, 8).astype(BF16)
        va_ref[...] = av_ref[...].astype(BF16)
        qm_ref[...] = (mq_ref[...] * sc_m).astype(BF16)

        cq_hat, _ = _rms_hat(bs_ref[:, 0:MLA_Q_RANK], MLA_Q_RANK)
        cqn = (cq_hat * gcq_ref[...]).astype(BF16)
        cqn_ref[...] = cqn
        ckv_hat, _ = _rms_hat(bs_ref[:, MLA_Q_RANK:MLA_Q_RANK + MLA_KV_RANK], MLA_KV_RANK)
        ckvn = (ckv_hat * gckv_ref[...]).astype(BF16)
        ckvn_ref[...] = ckvn
        qfull = _dot_nt(cqn, wuq_ref[...])
        kv = _dot(ckvn, wukv_ref[...])
        kr = _rope(bs_ref[:, 384:512], tb, 16)
        lane = lax.broadcasted_iota(jnp.int32, (1, LANES), 1)
        low = lane < 64
        for h in range(MLA_HEADS):
            sl = slice(h * LANES, (h + 1) * LANES)
            qb_ref[:, sl] = (_rope(qfull[:, sl], tb, 16) * sc_b).astype(BF16)
            kb_ref[:, sl] = jnp.where(low, kv[:, sl], kr).astype(BF16)
            vb_ref[:, sl] = jnp.where(low, 0.0, kv[:, sl]).astype(BF16)

    def col(width, idx):
        return pl.BlockSpec((tm, width), lambda i: (i, idx))

    def full(shape):
        return pl.BlockSpec(shape, lambda i: (0, 0))

    wide = jax.ShapeDtypeStruct((t, 1024), BF16)
    return pl.pallas_call(
        body, name="prep", grid=(t // tm,),
        out_shape=(wide, wide, wide, wide, wide, wide,
                   jax.ShapeDtypeStruct((t, MEM_WIDTH), BF16),
                   jax.ShapeDtypeStruct((t, MLA_Q_RANK), BF16),
                   jax.ShapeDtypeStruct((t, MLA_KV_RANK), BF16)),
        in_specs=[col(1024, 0), col(1024, 1), col(1024, 2), col(512, COL_CQ // 512), col(512, COL_MQ // 512),
                  pl.BlockSpec((tm, 1), lambda i: (i, 0)),
                  full((1024, MLA_Q_RANK)), full((MLA_KV_RANK, 1024)),
                  full((1, MLA_Q_RANK)), full((1, MLA_KV_RANK)), full((8, LANES)), full((8, LANES))],
        out_specs=(col(1024, 0),) * 6 + (col(MEM_WIDTH, 0), col(MLA_Q_RANK, 0), col(MLA_KV_RANK, 0)),
        compiler_params=_params(("parallel",)),
    )(proj, proj, proj, proj, proj, pos, w_uq, w_ukv, g_cq, g_ckv, rope_a, rope_b)


def _attn_fwd(q, k, v, *, nb, s, sk, heads, hpb, voff, bq, name):
    nq = s // bq
    width = hpb * LANES
    vblk = voff // hpb

    def body(q_ref, k_ref, v_ref, o_ref, lse_ref):
        for h in range(hpb):
            sl = slice(h * LANES, (h + 1) * LANES)
            sc = _dot_nt(q_ref[:, sl], k_ref[:, sl])
            m = jnp.max(sc, axis=1, keepdims=True)
            p = jnp.exp2(sc - m)
            l = jnp.sum(p, axis=1, keepdims=True)
            o_ref[:, sl] = _dot(p.astype(BF16), v_ref[:, sl]) / l
            lse_ref[:, sl] = jnp.broadcast_to(m + jnp.log(l) * LOG2E, (bq, LANES))

    out = jax.ShapeDtypeStruct((nb * s, heads * LANES), F32)
    ospec = pl.BlockSpec((bq, width), lambda b, i, g: (b * nq + i, g))
    return pl.pallas_call(
        body, name=name, grid=(nb, nq, heads // hpb),
        out_shape=(out, out),
        in_specs=[ospec, pl.BlockSpec((sk, width), lambda b, i, g: (b, g)),
                  pl.BlockSpec((sk, width), lambda b, i, g: (b, vblk + g))],
        out_specs=(ospec, ospec),
        compiler_params=_params(("parallel", "parallel", "parallel")),
    )(q, k, v)


def _attn_bwd(q, k, v, o, do, lse, *, nb, s, sk, heads, hpb, voff, scale, bq, name):
    nq = s // bq
    width = hpb * LANES
    vblk = voff // hpb

    def body(q_ref, k_ref, v_ref, o_ref, do_ref, lse_ref, dq_ref, dk_ref, dv_ref):
        i = pl.program_id(2)

        @pl.when(i == 0)
        def _():
            dk_ref[...] = jnp.zeros_like(dk_ref)
            dv_ref[...] = jnp.zeros_like(dv_ref)

        for h in range(hpb):
            sl = slice(h * LANES, (h + 1) * LANES)
            qh = q_ref[:, sl]
            kk = k_ref[:, sl]
            doh = do_ref[:, sl]
            delta = jnp.sum(doh.astype(F32) * o_ref[:, sl], axis=1, keepdims=True)
            p = jnp.exp2(_dot_nt(qh, kk) - lse_ref[:, h * LANES:h * LANES + 1])
            ds = (p * (_dot_nt(doh, v_ref[:, sl]) - delta)).astype(BF16)
            dq_ref[:, sl] = _dot(ds, kk) * scale
            dk_ref[:, sl] += _dot_tn(ds, qh)
            dv_ref[:, sl] += _dot_tn(p.astype(BF16), doh)

        @pl.when(i == nq - 1)
        def _():
            dk_ref[...] = dk_ref[...] * LN2

    qspec = pl.BlockSpec((bq, width), lambda b, g, i: (b * nq + i, g))
    kv_spec = pl.BlockSpec((sk, width), lambda b, g, i: (b, g))
    dq_shape = jax.ShapeDtypeStruct((nb * s, heads * LANES), F32)
    dkv_shape = jax.ShapeDtypeStruct((nb * sk, heads * LANES), F32)
    return pl.pallas_call(
        body, name=name, grid=(nb, heads // hpb, nq),
        out_shape=(dq_shape, dkv_shape, dkv_shape),
        in_specs=[qspec, kv_spec, pl.BlockSpec((sk, width), lambda b, g, i: (b, vblk + g)), qspec, qspec, qspec],
        out_specs=(qspec, kv_spec, kv_spec),
        compiler_params=_params(("parallel", "parallel", "arbitrary")),
    )(q, k, v, o, do, lse)


BAND_Q = 128
BAND_WIN = 256


def _band_start(i, s):
    return min(max(i * BAND_Q - 64, 0), s - BAND_WIN)


def _to_pattern_order(src_ref, dst_ref, stage_ref, s, d):
    length = s // d
    stage_ref[...] = src_ref[...].astype(F32)
    for r in range(d):
        dst_ref[r * length:(r + 1) * length, :] = stage_ref[pl.ds(r, length, stride=d), :].astype(dst_ref.dtype)


def _dilated_fwd(q, k, v, bias, bias_index, *, nb, s, name):
    nblk = s // BAND_Q
    npat = len(DILATED)

    def body(q_ref, k_ref, v_ref, bias_ref, o_ref, lse_ref, stage_ref, qp_ref, kp_ref, vp_ref, op_ref, lp_ref,
             on_ref, ln_ref):
        lane = lax.broadcasted_iota(jnp.int32, (1, LANES), 1)
        first = lane < 64
        for p, (_, d) in enumerate(DILATED):
            if d == 1:
                qs, ks, vs = q_ref, k_ref, v_ref
            else:
                for src, dst in ((q_ref, qp_ref), (k_ref, kp_ref), (v_ref, vp_ref)):
                    _to_pattern_order(src, dst, stage_ref, s, d)
                qs, ks, vs = qp_ref, kp_ref, vp_ref
            for i in range(nblk):
                u0 = i * BAND_Q
                st = _band_start(i, s)
                qi = qs[u0:u0 + BAND_Q, :]
                kw = ks[st:st + BAND_WIN, :]
                vw = vs[st:st + BAND_WIN, :]
                zero = jnp.zeros_like(qi)
                q2 = jnp.concatenate([jnp.where(first, qi, zero), jnp.where(first, zero, qi)], axis=0)
                sc = _dot_nt(q2, kw)
                b = bias_ref[bias_index[p][i]]
                halves = []
                for h in range(2):
                    sh = sc[h * BAND_Q:(h + 1) * BAND_Q] + b
                    m = jnp.max(sh, axis=1, keepdims=True)
                    pr = jnp.exp2(sh - m)
                    l = jnp.sum(pr, axis=1, keepdims=True)
                    halves.append((pr.astype(BF16), l, m + jnp.log(l) * LOG2E))
                o2 = _dot(jnp.concatenate([halves[0][0], halves[1][0]], axis=0), vw)
                o_blk = jnp.where(first, o2[:BAND_Q] / halves[0][1], o2[BAND_Q:] / halves[1][1])
                lse_blk = jnp.where(first, jnp.broadcast_to(halves[0][2], (BAND_Q, LANES)),
                                    jnp.broadcast_to(halves[1][2], (BAND_Q, LANES)))
                op_ref[p, u0:u0 + BAND_Q, :] = o_blk
                lp_ref[p, u0:u0 + BAND_Q, :] = lse_blk
            if d > 1:
                length = s // d
                for r in range(d):
                    on_ref.at[p - 1][pl.ds(r, length, stride=d), :] = op_ref[p, r * length:(r + 1) * length, :]
                    ln_ref.at[p - 1][pl.ds(r, length, stride=d), :] = lp_ref[p, r * length:(r + 1) * length, :]
        lses = [lp_ref[0]] + [ln_ref[p] for p in range(npat - 1)]
        outs = [op_ref[0]] + [on_ref[p] for p in range(npat - 1)]
        m = functools.reduce(jnp.maximum, lses)
        ws = [jnp.exp2(l - m) for l in lses]
        den = functools.reduce(lambda a, c: a + c, ws)
        o_ref[...] = functools.reduce(lambda a, c: a + c, [w * o for w, o in zip(ws, outs)]) / den
        lse_ref[...] = m + jnp.log(den) * LOG2E

    blk = pl.BlockSpec((s, LANES), lambda b, g: (b, g))
    out = jax.ShapeDtypeStruct((nb * s, A_WIDTH), F32)
    return pl.pallas_call(
        body, name=name, grid=(nb, A_WIDTH // LANES),
        out_shape=(out, out),
        in_specs=[blk, blk, blk, pl.BlockSpec(bias.shape, lambda b, g: (0, 0, 0))],
        out_specs=(blk, blk),
        scratch_shapes=[pltpu.VMEM((s, LANES), F32), pltpu.VMEM((s, LANES), BF16), pltpu.VMEM((s, LANES), BF16),
                        pltpu.VMEM((s, LANES), BF16), pltpu.VMEM((npat, s, LANES), F32),
                        pltpu.VMEM((npat, s, LANES), F32), pltpu.VMEM((npat - 1, s, LANES), F32),
                        pltpu.VMEM((npat - 1, s, LANES), F32)],
        compiler_params=_params(("parallel", "parallel")),
    )(q, k, v, bias)


def _dilated_bwd(q, k, v, o, do, lse, bias, bias_index, *, nb, s, scale, name):
    nblk = s // BAND_Q
    npat = len(DILATED)

    def body(q_ref, k_ref, v_ref, o_ref, do_ref, lse_ref, bias_ref, dq_ref, dk_ref, dv_ref,
             stage_ref, dl_ref, qp_ref, kp_ref, vp_ref, dop_ref, lsp_ref, dlp_ref, dqp_ref, dkp_ref, dvp_ref):
        lane = lax.broadcasted_iota(jnp.int32, (1, LANES), 1)
        first = lane < 64
        prod = do_ref[...].astype(F32) * o_ref[...]
        d0 = jnp.sum(jnp.where(first, prod, 0.0), axis=1, keepdims=True)
        d1 = jnp.sum(jnp.where(first, 0.0, prod), axis=1, keepdims=True)
        dl_ref[...] = jnp.where(first, jnp.broadcast_to(d0, (s, LANES)), jnp.broadcast_to(d1, (s, LANES)))
        for p, (_, d) in enumerate(DILATED):
            length = s // d
            if d == 1:
                qs, ks, vs, dos, lss, dls = q_ref, k_ref, v_ref, do_ref, lse_ref, dl_ref
                dqs, dks, dvs = dq_ref, dk_ref, dv_ref
            else:
                for src, dst in ((q_ref, qp_ref), (k_ref, kp_ref), (v_ref, vp_ref), (do_ref, dop_ref),
                                 (lse_ref, lsp_ref), (dl_ref, dlp_ref)):
                    _to_pattern_order(src, dst, stage_ref, s, d)
                qs, ks, vs, dos, lss, dls = qp_ref, kp_ref, vp_ref, dop_ref, lsp_ref, dlp_ref
                dqs, dks, dvs = dqp_ref, dkp_ref, dvp_ref
            dks[...] = jnp.zeros((s, LANES), F32)
            dvs[...] = jnp.zeros((s, LANES), F32)
            for i in range(nblk):
                u0 = i * BAND_Q
                st = _band_start(i, s)
                qi = qs[u0:u0 + BAND_Q, :]
                doi = dos[u0:u0 + BAND_Q, :]
                kw = ks[st:st + BAND_WIN, :]
                vw = vs[st:st + BAND_WIN, :]
                zero = jnp.zeros_like(qi)
                q2 = jnp.concatenate([jnp.where(first, qi, zero), jnp.where(first, zero, qi)], axis=0)
                do2 = jnp.concatenate([jnp.where(first, doi, zero), jnp.where(first, zero, doi)], axis=0)
                sc = _dot_nt(q2, kw)
                dp = _dot_nt(do2, vw)
                b = bias_ref[bias_index[p][i]]
                lse_i = lss[u0:u0 + BAND_Q, :]
                dl_i = dls[u0:u0 + BAND_Q, :]
                ps, dss = [], []
                for h in range(2):
                    rows = slice(h * BAND_Q, (h + 1) * BAND_Q)
                    pr = jnp.exp2(sc[rows] + b - lse_i[:, 64 * h:64 * h + 1])
                    ps.append(pr.astype(BF16))
                    dss.append((pr * (dp[rows] - dl_i[:, 64 * h:64 * h + 1])).astype(BF16))
                p2 = jnp.concatenate(ps, axis=0)
                ds2 = jnp.concatenate(dss, axis=0)
                dq2 = _dot(ds2, kw)
                dqs[u0:u0 + BAND_Q, :] = jnp.where(first, dq2[:BAND_Q], dq2[BAND_Q:]) * scale
                dks[st:st + BAND_WIN, :] += _dot_tn(ds2, q2)
                dvs[st:st + BAND_WIN, :] += _dot_tn(p2, do2)
            if d > 1:
                for dst, src in ((dq_ref, dqp_ref), (dk_ref, dkp_ref), (dv_ref, dvp_ref)):
                    for r in range(d):
                        dst[pl.ds(r, length, stride=d), :] += src[r * length:(r + 1) * length, :]
        dk_ref[...] = dk_ref[...] * LN2

    blk = pl.BlockSpec((s, LANES), lambda b, g: (b, g))
    out = jax.ShapeDtypeStruct((nb * s, A_WIDTH), F32)
    f32_buf = pltpu.VMEM((s, LANES), F32)
    bf_buf = pltpu.VMEM((s, LANES), BF16)
    return pl.pallas_call(
        body, name=name, grid=(nb, A_WIDTH // LANES),
        out_shape=(out, out, out),
        in_specs=[blk] * 6 + [pl.BlockSpec(bias.shape, lambda b, g: (0, 0, 0))],
        out_specs=(blk, blk, blk),
        scratch_shapes=[f32_buf, f32_buf, bf_buf, bf_buf, bf_buf, bf_buf, f32_buf, f32_buf, f32_buf, f32_buf, f32_buf],
        compiler_params=_params(("parallel", "parallel")),
    )(q, k, v, o, do, lse, bias)


def _post(x, ya, ybp, ym, proj, target, w_out, g_emb, b_emb, g_a, g_b, g_m, g_post, b_post, tm=256):
    t = x.shape[0]

    def body(x_ref, ya_ref, yb_ref, ym_ref, ga_ref, gb_ref, gm_ref, tg_ref, wo_ref,
             ge_ref, be_ref, goa_ref, gob_ref, gom_ref, gp_ref, bp_ref,
             y_ref, dz_ref, doa_ref, dob_ref, dom_ref, dga_ref, dgb_ref, dgm_ref,
             loss_ref, dgp_ref, dbp_ref, dgoa_ref, dgob_ref, dgom_ref):
        i = pl.program_id(0)

        @pl.when(i == 0)
        def _():
            for r in (loss_ref, dgp_ref, dbp_ref, dgoa_ref, dgob_ref, dgom_ref):
                r[...] = jnp.zeros_like(r)

        lane = lax.broadcasted_iota(jnp.int32, (1, LANES), 1)
        low = lane < 64
        xh0, _ = _ln_hat(x_ref[...])
        h = xh0 * ge_ref[...] + be_ref[...]

        ybp_v = yb_ref[...]
        yb = jnp.concatenate(
            [jnp.where(low, pltpu.roll(ybp_v[:, 2 * j * LANES:(2 * j + 1) * LANES], 64, 1),
                       ybp_v[:, (2 * j + 1) * LANES:(2 * j + 2) * LANES]) for j in range(4)], axis=1)

        def gated(raw, gate, gain, width):
            xh, r = _rms_hat(raw, width)
            n = xh * gain
            sg = 1.0 / (1.0 + jnp.exp(-gate))
            return xh, r, n, sg, n * (gate * sg)

        gate_a, gate_b, gate_m = ga_ref[...], gb_ref[...], gm_ref[...]
        xh_a, r_a, n_a, sg_a, y_a = gated(ya_ref[...], gate_a, goa_ref[...], A_WIDTH)
        xh_b, r_b, n_b, sg_b, y_b = gated(yb, gate_b, gob_ref[...], 512)
        xh_m, r_m, n_m, sg_m, y_m = gated(ym_ref[...], gate_m, gom_ref[...], 512)
        y = jnp.concatenate([y_a, y_b, y_m], axis=1).astype(BF16)
        y_ref[...] = y
        z = DEEPNORM_ALPHA * h + _dot(y, wo_ref[...])
        zh, rstd = _ln_hat(z)
        err = zh * gp_ref[...] + bp_ref[...] - tg_ref[...]
        rows = jnp.sum(err * err, axis=1, keepdims=True)
        loss_ref[...] += jnp.broadcast_to(jnp.sum(rows, axis=0, keepdims=True) * (0.5 / D_MODEL), (1, LANES))
        dout = err * (1.0 / D_MODEL)
        dgp_ref[...] += _colsum(dout * zh)
        dbp_ref[...] += _colsum(dout)
        dz = _ln_bwd_rows(dout * gp_ref[...], zh, rstd)
        dz_ref[...] = dz
        dy = _dot_nt(dz.astype(BF16), wo_ref[...])

        def gated_bwd(dyg, xh, r, n, sg, gate, gain, width, dgain_ref):
            dn = dyg * (gate * sg)
            dgate = dyg * n * (sg * (1.0 + gate * (1.0 - sg)))
            dgain_ref[...] += _colsum(dn * xh)
            return _rms_bwd(dn * gain, xh, r, width), dgate

        dya, dgate_a = gated_bwd(dy[:, 0:1024], xh_a, r_a, n_a, sg_a, gate_a, goa_ref[...], A_WIDTH, dgoa_ref)
        dyb, dgate_b = gated_bwd(dy[:, 1024:1536], xh_b, r_b, n_b, sg_b, gate_b, gob_ref[...], 512, dgob_ref)
        dym, dgate_m = gated_bwd(dy[:, 1536:2048], xh_m, r_m, n_m, sg_m, gate_m, gom_ref[...], 512, dgom_ref)
        doa_ref[...] = dya.astype(BF16)
        dom_ref[...] = dym.astype(BF16)
        dga_ref[...] = dgate_a.astype(BF16)
        dgb_ref[...] = dgate_b.astype(BF16)
        dgm_ref[...] = dgate_m.astype(BF16)
        for j in range(4):
            blk = dyb[:, j * LANES:(j + 1) * LANES]
            dob_ref[:, 2 * j * LANES:(2 * j + 1) * LANES] = jnp.where(low, 0.0, pltpu.roll(blk, 64, 1)).astype(BF16)
            dob_ref[:, (2 * j + 1) * LANES:(2 * j + 2) * LANES] = jnp.where(low, 0.0, blk).astype(BF16)

    def col(width, idx):
        return pl.BlockSpec((tm, width), lambda i: (i, idx))

    def full(shape):
        return pl.BlockSpec(shape, lambda i: (0, 0))

    def acc(width):
        return jax.ShapeDtypeStruct((1, width), F32)

    return pl.pallas_call(
        body, name="post", grid=(t // tm,),
        out_shape=(jax.ShapeDtypeStruct((t, 2048), BF16), jax.ShapeDtypeStruct((t, 1024), F32),
                   jax.ShapeDtypeStruct((t, 1024), BF16), jax.ShapeDtypeStruct((t, 1024), BF16),
                   jax.ShapeDtypeStruct((t, 512), BF16),
                   jax.ShapeDtypeStruct((t, 1024), BF16), jax.ShapeDtypeStruct((t, 512), BF16),
                   jax.ShapeDtypeStruct((t, 512), BF16),
                   acc(LANES), acc(1024), acc(1024), acc(1024), acc(512), acc(512)),
        in_specs=[col(1024, 0), col(1024, 0), col(1024, 0), col(512, 0),
                  col(1024, 3), col(512, COL_BG // 512), col(512, COL_MG // 512), col(1024, 0),
                  full((2048, 1024)),
                  full((1, 1024)), full((1, 1024)), full((1, 1024)), full((1, 512)), full((1, 512)),
                  full((1, 1024)), full((1, 1024))],
        out_specs=(col(2048, 0), col(1024, 0), col(1024, 0), col(1024, 0), col(512, 0),
                   col(1024, 0), col(512, 0), col(512, 0),
                   full((1, LANES)), full((1, 1024)), full((1, 1024)), full((1, 1024)), full((1, 512)),
                   full((1, 512))),
        compiler_params=_params(("arbitrary",)),
    )(x, ya, ybp, ym, proj, proj, proj, target, w_out, g_emb, b_emb, g_a, g_b, g_m, g_post, b_post)


def _prep_bwd(dqa, dka, dva, dqb, dkb, dvb, dqm, dga, dgb, dgm, proj, pos, w_uq, w_ukv, g_cq, g_ckv,
              rope_a, rope_b, tm=256):
    t = proj.shape[0]

    def body(dqa_ref, dka_ref, dva_ref, dqb_ref, dkb_ref, dvb_ref, dqm_ref, dga_ref, dgb_ref, dgm_ref,
             bs_ref, pos_ref, wuq_ref, wukv_ref, gcq_ref, gckv_ref, ra_ref, rb_ref,
             dproj_ref, dqf_ref, dkv_ref, dgcq_ref, dgckv_ref):
        i = pl.program_id(0)

        @pl.when(i == 0)
        def _():
            dgcq_ref[...] = jnp.zeros_like(dgcq_ref)
            dgckv_ref[...] = jnp.zeros_like(dgckv_ref)

        pos_c = pos_ref[...]
        ta = _rope_tables(pos_c, ra_ref[...])
        tb = _rope_tables(pos_c, rb_ref[...])
        for j in range(A_WIDTH // LANES):
            sl = slice(j * LANES, (j + 1) * LANES)
            dproj_ref[:, j * LANES:(j + 1) * LANES] = _rope(dqa_ref[:, sl], ta, 8, inverse=True).astype(BF16)
            dproj_ref[:, 1024 + j * LANES:1024 + (j + 1) * LANES] = (
                _rope(dka_ref[:, sl], ta, 8, inverse=True).astype(BF16))
        dproj_ref[:, 2048:3072] = dva_ref[...].astype(BF16)
        dproj_ref[:, 3072:4096] = dga_ref[...]

        lane = lax.broadcasted_iota(jnp.int32, (1, LANES), 1)
        low = lane < 64
        rope_lanes = (lane >= 64) & (lane < 96)
        dkr = jnp.zeros((tm, LANES), F32)
        for h in range(MLA_HEADS):
            sl = slice(h * LANES, (h + 1) * LANES)
            dqf_ref[:, sl] = _rope(dqb_ref[:, sl], tb, 16, inverse=True).astype(BF16)
            dk_h = dkb_ref[:, sl]
            dkv_ref[:, sl] = jnp.where(low, dk_h, dvb_ref[:, sl]).astype(BF16)
            dkr = dkr + jnp.where(rope_lanes, dk_h, 0.0)
        dkr = _rope(dkr, tb, 16, inverse=True)

        cq_hat, r_q = _rms_hat(bs_ref[:, 0:MLA_Q_RANK], MLA_Q_RANK)
        dcqn = _dot(dqf_ref[...], wuq_ref[...])
        dgcq_ref[...] += _colsum(dcqn * cq_hat)
        dproj_ref[:, COL_CQ:COL_CQ + 256] = _rms_bwd(dcqn * gcq_ref[...], cq_hat, r_q, MLA_Q_RANK).astype(BF16)
        ckv_hat, r_kv = _rms_hat(bs_ref[:, MLA_Q_RANK:MLA_Q_RANK + MLA_KV_RANK], MLA_KV_RANK)
        dckvn = _dot_nt(dkv_ref[...], wukv_ref[...])
        dgckv_ref[...] += _colsum(dckvn * ckv_hat)
        dproj_ref[:, COL_CQ + 256:COL_CQ + 384] = (
            _rms_bwd(dckvn * gckv_ref[...], ckv_hat, r_kv, MLA_KV_RANK).astype(BF16))
        dproj_ref[:, COL_CQ + 384:COL_CQ + 512] = dkr.astype(BF16)
        dproj_ref[:, COL_BG:COL_BG + 512] = dgb_ref[...]
        dproj_ref[:, COL_MQ:COL_MQ + 512] = dqm_ref[...].astype(BF16)
        dproj_ref[:, COL_MG:COL_MG + 512] = dgm_ref[...]

    def col(width, idx):
        return pl.BlockSpec((tm, width), lambda i: (i, idx))

    def full(shape):
        return pl.BlockSpec(shape, lambda i: (0, 0))

    return pl.pallas_call(
        body, name="prep_bwd", grid=(t // tm,),
        out_shape=(jax.ShapeDtypeStruct((t, PROJ_W), BF16), jax.ShapeDtypeStruct((t, 1024), BF16),
                   jax.ShapeDtypeStruct((t, 1024), BF16),
                   jax.ShapeDtypeStruct((1, MLA_Q_RANK), F32), jax.ShapeDtypeStruct((1, MLA_KV_RANK), F32)),
        in_specs=[col(1024, 0)] * 6 + [col(512, 0), col(1024, 0), col(512, 0), col(512, 0),
                  col(512, COL_CQ // 512), pl.BlockSpec((tm, 1), lambda i: (i, 0)),
                  full((1024, MLA_Q_RANK)), full((MLA_KV_RANK, 1024)),
                  full((1, MLA_Q_RANK)), full((1, MLA_KV_RANK)), full((8, LANES)), full((8, LANES))],
        out_specs=(col(PROJ_W, 0), col(1024, 0), col(1024, 0), full((1, MLA_Q_RANK)), full((1, MLA_KV_RANK))),
        compiler_params=_params(("arbitrary",)),
    )(dqa, dka, dva, dqb, dkb, dvb, dqm, dga, dgb, dgm, proj, pos, w_uq, w_ukv, g_cq, g_ckv, rope_a, rope_b)


def _adamw_math(gv, w, m, v):
    m_new = ADAM_B1 * m + (1.0 - ADAM_B1) * gv
    v_new = ADAM_B2 * v + (1.0 - ADAM_B2) * (gv * gv)
    m_hat = m_new / (1.0 - ADAM_B1 ** ADAM_STEP)
    v_hat = v_new / (1.0 - ADAM_B2 ** ADAM_STEP)
    return -ADAM_LR * (m_hat / (jnp.sqrt(v_hat) + ADAM_EPS) + ADAM_WD * w), m_new, v_new


def _adamw(g, w, m, v, tr, name):
    r, cols = w.shape

    def body(g_ref, w_ref, m_ref, v_ref, go_ref, d_ref, nm_ref, nv_ref):
        gv = g_ref[...]
        go_ref[...] = gv
        d_ref[...], nm_ref[...], nv_ref[...] = _adamw_math(gv, w_ref[...], m_ref[...], v_ref[...])

    tile = pl.BlockSpec((tr, cols), lambda i: (i, 0))
    shape = jax.ShapeDtypeStruct((r, cols), F32)
    return pl.pallas_call(
        body, name=name, grid=(r // tr,),
        out_shape=(shape,) * 4, in_specs=[tile] * 4, out_specs=(tile,) * 4,
        compiler_params=_params(("parallel",)),
    )(g, w, m, v)


def _adamw_pieces(g, w, m, v, pieces, name):
    shapes = [jax.ShapeDtypeStruct((r1 - r0, c1 - c0), F32) for r0, r1, c0, c1 in pieces]

    def body(g_ref, w_ref, m_ref, v_ref, *outs):
        gv = g_ref[...]
        results = (gv,) + _adamw_math(gv, w_ref[...], m_ref[...], v_ref[...])
        for kind, full in enumerate(results):
            for p, (r0, r1, c0, c1) in enumerate(pieces):
                outs[kind * len(pieces) + p][...] = full[r0:r1, c0:c1]

    flat = pl.pallas_call(
        body, name=name, out_shape=tuple(shapes) * 4,
        in_specs=[IN_VMEM] * 4, out_specs=tuple([IN_VMEM] * (4 * len(pieces))),
        compiler_params=_params(None),
    )(g, w, m, v)
    return [[flat[kind * len(pieces) + p] for kind in range(4)] for p in range(len(pieces))]


def _core_sum(g, recv, core, rows, tr, name, ride=None):
    cols = g.shape[2]
    nblk = rows // tr
    n_in = len(ride.args) if ride else 0
    n_out = len(ride.out_shapes) if ride else 0

    def body(c_ref, g_ref, r_ref, *rest):
        sf_ref, sb_ref = rest[n_in], rest[n_in + 1]
        if ride:
            j, i = pl.program_id(0), pl.program_id(1)
            ride.run((j == 0) & (i == 0), (j == 3) & (i == nblk - 1), rest[:n_in],
                     rest[n_in + 2:n_in + 2 + n_out], rest[n_in + 2 + n_out:])
        tot = g_ref[...] + r_ref[...]
        sf_ref[...] = tot
        sb_ref[...] = tot.astype(BF16)

    half = pl.BlockSpec((None, tr, cols), lambda j, i, c_ref: (j, i, 0))
    shapes = (jax.ShapeDtypeStruct((4, rows, cols), F32), jax.ShapeDtypeStruct((4, rows, cols), BF16))
    return pl.pallas_call(
        body, name=name,
        grid_spec=pltpu.PrefetchScalarGridSpec(
            num_scalar_prefetch=1, grid=(4, nblk),
            in_specs=[pl.BlockSpec((None, tr, cols), lambda j, i, c_ref: (j, c_ref[0] * nblk + i, 0)), half]
            + (ride.in_specs if ride else []),
            out_specs=(half, half) + (ANY,) * n_out,
            scratch_shapes=ride.scratch() if ride else []),
        out_shape=shapes + tuple(ride.out_shapes if ride else ()),
        compiler_params=_params(("arbitrary", "arbitrary") if ride else ("parallel", "parallel")),
    )(core, g, recv, *(ride.args if ride else ()))


def _half_to_sibling(g4):
    def plan(in_refs, out_refs, send_sems, recv_sems):
        x, y, c = _position()
        cp = pltpu.make_async_remote_copy(
            src_ref=in_refs[0].at[:, 1 - c], dst_ref=out_refs[0], send_sem=send_sems.at[0],
            recv_sem=recv_sems.at[0], device_id=(x, y, 1 - c), device_id_type=MESH)

        def finish():
            cp.wait_recv()
            cp.wait_send()

        return cp.start, finish

    return _Ride([g4], [jax.ShapeDtypeStruct((4, g4.shape[2], 1024), F32)], (1, 1), plan)


def _gather_plan(src_ref, dst_ref, send_sems, recv_sems, local_sems):
    x, y, c = _position()
    me = 2 * x + y
    local = pltpu.make_async_copy(src_ref, dst_ref.at[me], local_sems.at[0])

    def over_ici(k, src, chip):
        return pltpu.make_async_remote_copy(
            src_ref=src, dst_ref=dst_ref.at[chip, c], send_sem=send_sems.at[k - 1], recv_sem=recv_sems.at[k - 1],
            device_id=(x ^ (k >> 1), y ^ (k & 1), c), device_id_type=MESH)

    def to_sibling(k, half):
        piece = dst_ref.at[me ^ k, half]
        return pltpu.make_async_remote_copy(
            src_ref=piece, dst_ref=piece, send_sem=send_sems.at[2 + k], recv_sem=recv_sems.at[2 + k],
            device_id=(x, y, 1 - c), device_id_type=MESH)

    sends = [over_ici(k, src_ref.at[c], me) for k in (1, 2, 3)]

    def start():
        local.start()
        for cp in sends:
            cp.start()

    def finish():
        passed = []
        for k in (1, 2, 3):
            over_ici(k, dst_ref.at[me ^ k, c], me ^ k).wait_recv()
            cp = to_sibling(k, c)
            cp.start()
            passed.append(cp)
        for k in (1, 2, 3):
            to_sibling(k, 1 - c).wait_recv()
        for cp in sends + passed:
            cp.wait_send()
        local.wait()

    return start, finish


def _gather_ride(shard):
    def plan(in_refs, out_refs, send_sems, recv_sems, local_sems):
        return _gather_plan(in_refs[0], out_refs[0], send_sems, recv_sems, local_sems)

    return _Ride([shard], [jax.ShapeDtypeStruct((4,) + shard.shape, shard.dtype)], (6, 6, 1), plan,
                 in_specs=[IN_VMEM])


def _chip_sum(sf, recv, chip, rows, tr, name):
    cols = sf.shape[2]

    def body(me_ref, sf_ref, r_ref, out_ref):
        acc = sf_ref[...]
        for k in range(3):
            acc = acc + r_ref[k].astype(F32)
        out_ref[...] = acc

    return pl.pallas_call(
        body, name=name,
        grid_spec=pltpu.PrefetchScalarGridSpec(
            num_scalar_prefetch=1, grid=(rows // tr,),
            in_specs=[pl.BlockSpec((None, tr, cols), lambda i, me_ref: (me_ref[0], i, 0)),
                      pl.BlockSpec((3, tr, cols), lambda i, me_ref: (0, i, 0))],
            out_specs=pl.BlockSpec((tr, cols), lambda i, me_ref: (i, 0))),
        out_shape=jax.ShapeDtypeStruct((rows, cols), F32),
        compiler_params=_params(("parallel",)),
    )(chip, sf, recv)


def _position():
    return lax.axis_index("x"), lax.axis_index("y"), lax.axis_index("c")


def _dh_scatter(dproj, w_in_arr_t, x, dz, g, sb_in, sb_rest, tm=1024, tk=1024):
    t, d = x.shape
    nk = dproj.shape[1] // tk
    ni = t // tm

    def body(dp_ref, w_ref, x_ref, dz_ref, g_ref, sbin_ref, sbrest_ref,
             dx_ref, dg_ref, db_ref, rin_ref, rrest_ref, acc_ref, send_sems, recv_sems):
        i = pl.program_id(0)
        kk = pl.program_id(1)
        px, py, pc = _position()
        me = 2 * px + py
        srcs = (sbin_ref, sbrest_ref)
        dsts = (rin_ref, rrest_ref)

        def copy(a, k):
            return pltpu.make_async_remote_copy(
                src_ref=srcs[a].at[me ^ k], dst_ref=dsts[a].at[k - 1],
                send_sem=send_sems.at[3 * a + k - 1], recv_sem=recv_sems.at[3 * a + k - 1],
                device_id=(px ^ (k >> 1), py ^ (k & 1), pc), device_id_type=MESH)

        pairs = [(a, k) for a in range(2) for k in (1, 2, 3)]

        @pl.when((i == 0) & (kk == 0))
        def _():
            dg_ref[...] = jnp.zeros_like(dg_ref)
            db_ref[...] = jnp.zeros_like(db_ref)
            for a, k in pairs:
                copy(a, k).start()

        part = _dot(dp_ref[...], w_ref[...])

        @pl.when(kk == 0)
        def _():
            acc_ref[...] = part

        @pl.when(kk > 0)
        def _():
            acc_ref[...] += part

        @pl.when(kk == nk - 1)
        def _():
            xh, rstd = _ln_hat(x_ref[...])
            dht = acc_ref[...] + DEEPNORM_ALPHA * dz_ref[...]
            dg_ref[...] += _colsum(dht * xh)
            db_ref[...] += _colsum(dht)
            dx_ref[...] = _ln_bwd_rows(dht * g_ref[...], xh, rstd)

        @pl.when((i == ni - 1) & (kk == nk - 1))
        def _():
            for a, k in pairs:
                copy(a, k).wait_recv()
            for a, k in pairs:
                copy(a, k).wait_send()

    tile = pl.BlockSpec((tm, d), lambda i, kk: (i, 0))
    row = pl.BlockSpec((1, d), lambda i, kk: (0, 0))
    return pl.pallas_call(
        body, name="dh_scatter", grid=(ni, nk),
        out_shape=(jax.ShapeDtypeStruct((t, d), F32), jax.ShapeDtypeStruct((1, d), F32),
                   jax.ShapeDtypeStruct((1, d), F32),
                   jax.ShapeDtypeStruct((3, HALF_IN, 1024), BF16),
                   jax.ShapeDtypeStruct((3, HALF_REST, 1024), BF16)),
        in_specs=[pl.BlockSpec((tm, tk), lambda i, kk: (i, kk)), pl.BlockSpec((tk, d), lambda i, kk: (kk, 0)),
                  tile, tile, row, ANY, ANY],
        out_specs=(tile, row, row, ANY, ANY),
        scratch_shapes=[pltpu.VMEM((tm, d), F32), pltpu.SemaphoreType.DMA((6,)), pltpu.SemaphoreType.DMA((6,))],
        compiler_params=_params(("arbitrary", "arbitrary")),
    )(dproj, w_in_arr_t, x, dz, g, sb_in, sb_rest)


def _join_halves(gh_in, gh_rest):
    def body(hin_ref, hrest_ref, oin_ref, orest_ref, send_sems, recv_sems, local_sems):
        x, y, c = _position()
        srcs = (hin_ref, hrest_ref)
        dsts = (oin_ref, orest_ref)

        def rows(a, half):
            return dsts[a].at[half]

        local = [pltpu.make_async_copy(srcs[a], rows(a, c), local_sems.at[a]) for a in range(2)]
        remote = [pltpu.make_async_remote_copy(
            src_ref=srcs[a], dst_ref=rows(a, c), send_sem=send_sems.at[a], recv_sem=recv_sems.at[a],
            device_id=(x, y, 1 - c), device_id_type=MESH) for a in range(2)]
        for cp in local + remote:
            cp.start()
        for a in range(2):
            pltpu.make_async_remote_copy(
                src_ref=srcs[a], dst_ref=rows(a, 1 - c), send_sem=send_sems.at[a], recv_sem=recv_sems.at[a],
                device_id=(x, y, 1 - c), device_id_type=MESH).wait_recv()
        for cp in remote:
            cp.wait_send()
        for cp in local:
            cp.wait()

    return pl.pallas_call(
        body, name="join_halves",
        out_shape=(jax.ShapeDtypeStruct((2, HALF_IN, 1024), F32),
                   jax.ShapeDtypeStruct((2, HALF_REST, 1024), F32)),
        in_specs=[IN_VMEM, IN_VMEM], out_specs=(ANY, ANY),
        scratch_shapes=[pltpu.SemaphoreType.DMA((2,)), pltpu.SemaphoreType.DMA((2,)), pltpu.SemaphoreType.DMA((2,))],
    )(gh_in, gh_rest)


def _allreduce_small(vec):
    def body(vec_ref, out_ref, all_ref, send_sems, recv_sems):
        x, y, c = _position()
        me = 4 * x + 2 * y + c
        all_ref[me] = vec_ref[...]

        def copy(k, slot):
            return pltpu.make_async_remote_copy(
                src_ref=vec_ref, dst_ref=all_ref.at[slot], send_sem=send_sems.at[k - 1], recv_sem=recv_sems.at[k - 1],
                device_id=(x ^ (k >> 2), y ^ ((k >> 1) & 1), c ^ (k & 1)), device_id_type=MESH)

        copies = [copy(k, me) for k in range(1, 8)]
        for cp in copies:
            cp.start()
        for k in range(1, 8):
            copy(k, me ^ k).wait_recv()
        for cp in copies:
            cp.wait_send()
        total = all_ref[0]
        for d in range(1, 8):
            total = total + all_ref[d]
        out_ref[...] = total

    return pl.pallas_call(
        body, name="allreduce_small",
        out_shape=jax.ShapeDtypeStruct(vec.shape, vec.dtype),
        in_specs=[pl.BlockSpec(memory_space=pltpu.VMEM)], out_specs=pl.BlockSpec(memory_space=pltpu.VMEM),
        scratch_shapes=[pltpu.VMEM((8,) + vec.shape, vec.dtype), pltpu.SemaphoreType.DMA((7,)),
                        pltpu.SemaphoreType.DMA((7,))],
    )(vec)


def _pack_rest(w_uq, w_ukv, w_mem, w_out):
    rows = jnp.concatenate([w_uq[0].T.reshape(-1, 1024), w_ukv.reshape(-1, 1024), w_mem.reshape(-1, 1024),
                            w_out.reshape(-1, 1024)], axis=0)
    return jnp.pad(rows, ((0, ROWS_REST - ROWS_USED), (0, 0)))


def _arranged_w_in(g_in):
    z = functools.partial(jnp.zeros, dtype=g_in.dtype)
    cut = 4480 - 2 * SHARD_ROWS
    return jnp.concatenate(
        [g_in[0, :SHARD_ROWS], g_in[1, :SHARD_ROWS], g_in[2, :cut], z((64, 1024)), g_in[2, cut:cut + 32],
         z((32, 1024)), g_in[2, cut + 32:SHARD_ROWS], g_in[3, :SHARD_ROWS]], axis=0)


def _rest_weights(g_rest):
    w_uq_t = g_rest[:, 0:ROWS_UQ].reshape(768, 256)
    w_uq_pad_t = jnp.pad(w_uq_t.reshape(MLA_HEADS, MLA_QK_DIM, 256), ((0, 0), (0, 32), (0, 0))).reshape(1024, 256)
    w_ukv = jnp.concatenate([g_rest[j, ROWS_UQ:ROWS_UQ + ROWS_UKV].reshape(128, 256) for j in range(4)], axis=1)
    lo = ROWS_UQ + ROWS_UKV
    w_mem = g_rest[:, lo:lo + ROWS_MEM].reshape(4 * ROWS_MEM, 1024)
    w_out = g_rest[:, lo + ROWS_MEM:lo + ROWS_MEM + ROWS_OUT].reshape(4 * ROWS_OUT, 1024)
    return w_uq_pad_t, w_ukv, w_mem, w_out


def _split_in(dw_in_arr_t):
    a = dw_in_arr_t
    gap = jnp.zeros((ROWS_IN - SHARD_ROWS, 1024), a.dtype)
    nat = 4608 - 96
    pieces = [a[:SHARD_ROWS], gap, a[SHARD_ROWS:2 * SHARD_ROWS], gap,
              a[2 * SHARD_ROWS:4480], a[4544:4576], a[4608:4608 + 3 * SHARD_ROWS - nat], gap,
              a[4608 + 3 * SHARD_ROWS - nat:], gap]
    return jnp.concatenate(pieces, axis=0).reshape(4, ROWS_IN, 1024)


def _split_rest(dw_uq_pad_t, dw_ukv, dw_mem, dw_out):
    dw_uq_t = dw_uq_pad_t.reshape(MLA_HEADS, LANES, 256)[:, :MLA_QK_DIM].reshape(4, ROWS_UQ, 1024)
    parts = [dw_uq_t, dw_ukv.reshape(128, 4, 256).transpose(1, 0, 2).reshape(4, ROWS_UKV, 1024),
             dw_mem.reshape(4, ROWS_MEM, 1024), dw_out.reshape(4, ROWS_OUT, 1024)]
    return jnp.pad(jnp.concatenate(parts, axis=1), ((0, 0), (0, ROWS_REST - ROWS_USED), (0, 0)))


def _rope_consts(rot, first, period):
    half = rot // 2
    inv_freq = np.float32(ROPE_THETA) ** (-(np.arange(0, rot, 2, dtype=np.float32) / np.float32(rot)))
    lane = np.arange(LANES) % period - first
    in_rot = (lane >= 0) & (lane < rot)
    out = np.zeros((8, LANES), np.float32)
    out[0] = np.where(in_rot, inv_freq[np.clip(lane, 0, rot - 1) % half], 0.0)
    out[1] = in_rot & (lane < half)
    out[2] = in_rot & (lane >= half)
    return jnp.asarray(out)


def _band_bias(s):
    nblk = s // BAND_Q
    starts = np.array([_band_start(i, s) for i in range(nblk)])
    uq = (np.arange(nblk)[:, None] * BAND_Q + np.arange(BAND_Q)[None, :])[:, :, None]
    uk = (starts[:, None] + np.arange(BAND_WIN)[None, :])[:, None, :]
    tiles, index, seen = [], [], {}
    for _, d in DILATED:
        length = s // d
        ok = (uq // length == uk // length) & (np.abs(uq - uk) <= 64)
        row = []
        for i in range(nblk):
            key = ok[i].tobytes()
            if key not in seen:
                seen[key] = len(tiles)
                tiles.append(np.where(ok[i], 0.0, NEG_INF).astype(np.float32))
            row.append(seen[key])
        index.append(row)
    return jnp.asarray(np.stack(tiles, axis=0)), index


def _forward_backward(h, proj, x, mem, positions, target, weights, gains):
    w_uq_pad_t, w_ukv, w_mem, w_out = weights
    g_emb, b_emb, g_cq, g_ckv, g_out_a, g_out_b, g_out_m, g_post, b_post = gains
    nb, s, d = x.shape
    t = nb * s
    x2 = x.reshape(t, d)
    mem2 = mem.reshape(nb * N_MEM, d)
    tgt2 = target.reshape(t, d)
    pos = positions.reshape(t, 1).astype(F32)
    rope_a = _rope_consts(16, 0, 64)
    rope_b = _rope_consts(32, 64, 128)
    bias, bias_index = _band_bias(s)
    scales = (0.125, MLA_QK_DIM ** -0.5, 128 ** -0.5)

    qa, ka, va, qb, kb, vb, qm, cqn, ckvn = _prep(proj, pos, w_uq_pad_t, w_ukv, g_cq, g_ckv, rope_a, rope_b, scales)
    mkv = _mm(mem2, w_mem, BF16, nb * N_MEM, 1024, 1024, "mem_kv")

    cfg_b = dict(nb=nb, s=s, sk=s, heads=8, hpb=2, voff=0, bq=256)
    cfg_m = dict(nb=nb, s=s, sk=N_MEM, heads=4, hpb=2, voff=4, bq=1024)
    ya, lse_a = _dilated_fwd(qa, ka, va, bias, bias_index, nb=nb, s=s, name="attn_a_fwd")
    yb, lse_b = _attn_fwd(qb, kb, vb, name="attn_b_fwd", **cfg_b)
    ym, lse_m = _attn_fwd(qm, mkv, mkv, name="attn_m_fwd", **cfg_m)

    (y, dz, doa, dob, dom, dga, dgb, dgm, loss, dg_post, db_post, dg_a, dg_b, dg_m) = _post(
        x2, ya, yb, ym, proj, tgt2, w_out, g_emb, b_emb, g_out_a, g_out_b, g_out_m, g_post, b_post)

    dqa, dka, dva = _dilated_bwd(qa, ka, va, ya, doa, lse_a, bias, bias_index, nb=nb, s=s, scale=scales[0],
                                 name="attn_a_bwd")
    dqb, dkb, dvb = _attn_bwd(qb, kb, vb, yb, dob, lse_b, name="attn_b_bwd", scale=scales[1], **cfg_b)
    dqm, dmk, dmv = _attn_bwd(qm, mkv, mkv, ym, dom, lse_m, name="attn_m_bwd", scale=scales[2], **cfg_m)
    dmkv = jnp.concatenate([dmk, dmv], axis=1)

    dproj, dqf, dkv, dg_cq, dg_ckv = _prep_bwd(
        dqa, dka, dva, dqb, dkb, dvb, dqm, dga, dgb, dgm, proj, pos, w_uq_pad_t, w_ukv, g_cq, g_ckv, rope_a, rope_b)

    small_rows = (dg_cq, dg_ckv, loss, dg_a, dg_b, dg_m, dg_post, db_post)
    return (dproj, h, y, dz, dqf, cqn, ckvn, dkv, mem2, dmkv), x2, small_rows


def _weight_grads(operands, core):
    dproj, h, y, dz, dqf, cqn, ckvn, dkv, mem2, dmkv = operands
    dw_in_arr_t = _mm(dproj, h, F32, 1024, 1024, 1024, "dw_in", mode="tn")
    g_in = _split_in(dw_in_arr_t)
    dw_out, r_in = _mm(y, dz, F32, 1024, 1024, 1024, "dw_out", mode="tn",
                       ride=_half_to_sibling(g_in.reshape(4, 2, HALF_IN, 1024)))
    dw_uq_pad_t = _mm(dqf, cqn, F32, 1024, 256, 1024, "dw_uq", mode="tn")
    dw_ukv = _mm(ckvn, dkv, F32, 128, 1024, 1024, "dw_ukv", mode="tn")
    dw_mem = _mm(mem2, dmkv, F32, 1024, 1024, mem2.shape[0], "dw_mem", mode="tn")
    g_rest = _split_rest(dw_uq_pad_t, dw_ukv, dw_mem, dw_out)
    sf_in, sb_in, r_rest = _core_sum(g_in, r_in, core, HALF_IN, HALF_IN // 2, "core_sum_in",
                                     ride=_half_to_sibling(g_rest.reshape(4, 2, HALF_REST, 1024)))
    sf_rest, sb_rest = _core_sum(g_rest, r_rest, core, HALF_REST, HALF_REST, "core_sum_rest")
    return sf_in, sb_in, sf_rest, sb_rest


def _small_block(dg_emb, db_emb, small_rows):
    dg_cq, dg_ckv, loss, dg_a, dg_b, dg_m, dg_post, db_post = small_rows
    row2 = jnp.concatenate([dg_cq, dg_ckv, loss, jnp.zeros((1, 512), F32)], axis=1)
    return jnp.concatenate([dg_emb, db_emb, row2, dg_a, jnp.concatenate([dg_b, dg_m], axis=1), dg_post, db_post,
                            jnp.zeros((1, 1024), F32)], axis=0)


def _pack_small(g_emb, b_emb, g_cq, g_ckv, g_out_a, g_out_b, g_out_m, g_post, b_post):
    row2 = jnp.concatenate([g_cq.reshape(1, -1), g_ckv.reshape(1, -1), jnp.zeros((1, 640), F32)], axis=1)
    return jnp.concatenate([g_emb.reshape(1, -1), b_emb.reshape(1, -1), row2, g_out_a.reshape(1, -1),
                            jnp.concatenate([g_out_b.reshape(1, -1), g_out_m.reshape(1, -1)], axis=1),
                            g_post.reshape(1, -1), b_post.reshape(1, -1), jnp.zeros((1, 1024), F32)], axis=0)


def kernel(x, mem, positions, g_emb, b_emb, w_in, g_cq, g_ckv, w_uq, w_ukv, w_mem_kv, g_out_a, g_out_b, g_out_m, w_out, g_post, b_post, loss_target, m_g_emb, m_b_emb, m_w_in, m_g_cq, m_g_ckv, m_w_uq, m_w_ukv, m_w_mem_kv, m_g_out_a, m_g_out_b, m_g_out_m, m_w_out, m_g_post, m_b_post, v_g_emb, v_b_emb, v_w_in, v_g_cq, v_g_ckv, v_w_uq, v_w_ukv, v_w_mem_kv, v_g_out_a, v_g_out_b, v_g_out_m, v_w_out, v_g_post, v_b_post):
    w_rest = _pack_rest(w_uq, w_ukv, w_mem_kv, w_out)
    w_in_t = w_in[0].T
    w_in_b = jnp.pad(w_in_t.astype(BF16), ((0, ROWS_IN - SHARD_ROWS), (0, 0)))
    gains = (g_emb.reshape(1, -1), b_emb.reshape(1, -1), g_cq, g_ckv, g_out_a, g_out_b, g_out_m, g_post, b_post)
    h, gathered_in = _ln_fwd(x.reshape(-1, D_MODEL), gains[0], gains[1],
                             ride=_gather_ride(w_in_b.reshape(2, HALF_IN, 1024)))
    w_in_arr_t = _arranged_w_in(gathered_in.reshape(4, ROWS_IN, 1024))
    proj, gathered_rest = _mm(h, w_in_arr_t, F32, 1024, 1024, 1024, "in_proj", mode="nt",
                              ride=_gather_ride(w_rest.astype(BF16).reshape(2, HALF_REST, 1024)))
    weights = _rest_weights(gathered_rest.reshape(4, ROWS_REST, 1024))
    operands, x2, small_rows = _forward_backward(h, proj, x, mem, positions, loss_target, weights, gains)

    core = lax.axis_index("c").astype(jnp.int32).reshape(1)
    chip = (2 * lax.axis_index("x") + lax.axis_index("y")).astype(jnp.int32).reshape(1)
    sf_in, sb_in, sf_rest, sb_rest = _weight_grads(operands, core)
    grad_x, dg_emb, db_emb, rb_in, rb_rest = _dh_scatter(operands[0], w_in_arr_t, x2, operands[3], gains[0],
                                                         sb_in, sb_rest)
    gh_in = _chip_sum(sf_in, rb_in, chip, HALF_IN, HALF_IN // 2, "chip_sum_in")
    gh_rest = _chip_sum(sf_rest, rb_rest, chip, HALF_REST, HALF_REST, "chip_sum_rest")
    grad_in, grad_rest = _join_halves(gh_in, gh_rest)
    grad_in = grad_in.reshape(ROWS_IN, 1024)
    grad_rest = grad_rest.reshape(ROWS_REST, 1024)

    big_in = _adamw(grad_in, w_in_t, m_w_in[0].T, v_w_in[0].T, SHARD_ROWS // 3, "adamw_in")
    uq, ukv, wmem, wout = _adamw_pieces(
        grad_rest, w_rest, _pack_rest(m_w_uq, m_w_ukv, m_w_mem_kv, m_w_out),
        _pack_rest(v_w_uq, v_w_ukv, v_w_mem_kv, v_w_out), REST_PIECES, "adamw_rest")
    small_sum = _allreduce_small(_small_block(dg_emb, db_emb, small_rows))
    sm = _adamw_pieces(
        small_sum,
        _pack_small(g_emb, b_emb, g_cq, g_ckv, g_out_a, g_out_b, g_out_m, g_post, b_post),
        _pack_small(m_g_emb, m_b_emb, m_g_cq, m_g_ckv, m_g_out_a, m_g_out_b, m_g_out_m, m_g_post, m_b_post),
        _pack_small(v_g_emb, v_b_emb, v_g_cq, v_g_ckv, v_g_out_a, v_g_out_b, v_g_out_m, v_g_post, v_b_post),
        SMALL_PIECES, "adamw_small")
    loss = small_sum[2, 384]

    def ordered(kind):
        s_gemb, s_bemb, s_gcq, s_gckv, s_ga, s_gb, s_gm, s_gpost, s_bpost = [piece[kind] for piece in sm]
        return [s_gemb.reshape(-1), s_bemb.reshape(-1), big_in[kind].T[None], s_gcq, s_gckv,
                uq[kind].reshape(192, 256).T[None], ukv[kind].reshape(1, 128, 256), wmem[kind][None], s_ga, s_gb,
                s_gm, wout[kind][None], s_gpost, s_bpost]

    return (loss, grad_x.reshape(x.shape), *ordered(0), *ordered(1), *ordered(2), *ordered(3))
```

```python
import functools
import math

import jax
import jax.numpy as jnp
import numpy as np
from jax import lax
from jax.experimental import pallas as pl
from jax.experimental.pallas import tpu as pltpu

F32 = jnp.float32
BF16 = jnp.bfloat16
MESH = pl.DeviceIdType.MESH
ANY = pl.BlockSpec(memory_space=pl.ANY)
IN_VMEM = pl.BlockSpec(memory_space=pltpu.VMEM)

D_MODEL = 1024
A_WIDTH = 1024
MLA_HEADS = 8
MLA_Q_RANK = 256
MLA_KV_RANK = 128
MLA_QK_DIM = 96
MEM_WIDTH = 512
N_MEM = 256
ROPE_THETA = 500000.0
NORM_EPS = 1e-5
NEG_INF = -1e30
DEEPNORM_ALPHA = 2.0 ** 0.25
DILATED = ((64, 1), (256, 4), (1024, 16))

ADAM_LR = 0.001
ADAM_B1 = 0.9
ADAM_B2 = 0.999
ADAM_EPS = 1e-08
ADAM_WD = 0.01
ADAM_STEP = 10

LANES = 128
VMEM_LIMIT = 56 * 1024 * 1024
LOG2E = math.log2(math.e)
LN2 = math.log(2.0)

PROJ_W = 6144
COL_CQ = 4096
COL_BG = 4608
COL_MQ = 5120
COL_MG = 5632

SHARD_ROWS = 1512
ROWS_IN = 1536
ROWS_UQ, ROWS_UKV, ROWS_MEM, ROWS_OUT = 48, 32, 256, 512
ROWS_USED = ROWS_UQ + ROWS_UKV + ROWS_MEM + ROWS_OUT
ROWS_REST = 864
HALF_IN = ROWS_IN // 2
HALF_REST = ROWS_REST // 2
REST_PIECES = ((0, 48, 0, 1024), (48, 80, 0, 1024), (80, 336, 0, 1024), (336, 848, 0, 1024))
SMALL_PIECES = ((0, 1, 0, 1024), (1, 2, 0, 1024), (2, 3, 0, 256), (2, 3, 256, 384), (3, 4, 0, 1024), (4, 5, 0, 512),
                (4, 5, 512, 1024), (5, 6, 0, 1024), (6, 7, 0, 1024))


def _params(sem=None, vmem=VMEM_LIMIT):
    return pltpu.CompilerParams(dimension_semantics=sem, vmem_limit_bytes=vmem)


def _dot(a, b):
    return jnp.dot(a, b, preferred_element_type=F32)


def _dot_nt(a, b):
    return lax.dot_general(a, b, (((1,), (1,)), ((), ())), preferred_element_type=F32)


def _dot_tn(a, b):
    return lax.dot_general(a, b, (((0,), (0,)), ((), ())), preferred_element_type=F32)


def _ln_hat(x):
    mu = jnp.mean(x, axis=-1, keepdims=True)
    xc = x - mu
    var = jnp.mean(xc * xc, axis=-1, keepdims=True)
    rstd = lax.rsqrt(var + NORM_EPS)
    return xc * rstd, rstd


def _ln_bwd_rows(dxh, xh, rstd):
    return rstd * (dxh - jnp.mean(dxh, axis=-1, keepdims=True) - xh * jnp.mean(dxh * xh, axis=-1, keepdims=True))


def _rms_hat(x, width):
    ms = jnp.sum(x * x, axis=-1, keepdims=True) * (1.0 / width)
    r = lax.rsqrt(ms + NORM_EPS)
    return x * r, r


def _rms_bwd(u, xh, r, width):
    return r * (u - xh * (jnp.sum(u * xh, axis=-1, keepdims=True) * (1.0 / width)))


def _colsum(v):
    return jnp.sum(v, axis=0, keepdims=True)


def _rope_tables(pos, consts):
    ang = pos * consts[0:1, :]
    c = jnp.cos(ang)
    s = jnp.sin(ang)
    return c, s * consts[2:3, :], -s * consts[1:2, :]


def _rope(x, tables, half, inverse=False):
    c, s_up, s_dn = tables
    if inverse:
        s_up, s_dn = -s_up, -s_dn
    return x * c + pltpu.roll(x, half, 1) * s_up + pltpu.roll(x, LANES - half, 1) * s_dn


def _ln_fwd(x, g, b, tm=512, ride=None):
    t, d = x.shape
    n_in = len(ride.args) if ride else 0
    n_out = len(ride.out_shapes) if ride else 0
    steps = t // tm

    def body(x_ref, g_ref, b_ref, *rest):
        if ride:
            i = pl.program_id(0)
            ride.run(i == 0, i == steps - 1, rest[:n_in], rest[n_in + 1:n_in + 1 + n_out], rest[n_in + 1 + n_out:])
        xh, _ = _ln_hat(x_ref[...])
        rest[n_in][...] = (xh * g_ref[...] + b_ref[...]).astype(BF16)

    row = pl.BlockSpec((1, d), lambda i: (0, 0))
    tile = pl.BlockSpec((tm, d), lambda i: (i, 0))
    h_shape = jax.ShapeDtypeStruct((t, d), BF16)
    if not ride:
        return pl.pallas_call(
            body, name="ln_fwd", grid=(steps,), out_shape=h_shape, in_specs=[tile, row, row], out_specs=tile,
            compiler_params=_params(("parallel",)),
        )(x, g, b)
    return pl.pallas_call(
        body, name="ln_fwd", grid=(steps,),
        out_shape=(h_shape, *ride.out_shapes),
        in_specs=[tile, row, row] + ride.in_specs, out_specs=(tile,) + (ANY,) * n_out,
        scratch_shapes=ride.scratch(),
        compiler_params=_params(("arbitrary",)),
    )(x, g, b, *ride.args)


class _Ride:
    def __init__(self, args, out_shapes, sem_counts, plan, in_specs=None):
        self.args, self.out_shapes, self.plan = list(args), list(out_shapes), plan
        self.sem_counts = sem_counts
        self.in_specs = in_specs or [ANY] * len(self.args)

    def scratch(self):
        return [pltpu.SemaphoreType.DMA((n,)) for n in self.sem_counts]

    def run(self, first, last, in_refs, out_refs, sems):
        @pl.when(first)
        def _():
            self.plan(in_refs, out_refs, *sems)[0]()

        @pl.when(last)
        def _():
            self.plan(in_refs, out_refs, *sems)[1]()


def _mm(a, b, out_dtype, tm, tn, tk, name, mode="nn", ride=None):
    if mode == "tn":
        k, m = a.shape
    else:
        m, k = a.shape
    n = b.shape[0] if mode == "nt" else b.shape[1]
    nk = k // tk
    nj, ni = n // tn, m // tm
    n_in = len(ride.args) if ride else 0
    n_out = len(ride.out_shapes) if ride else 0

    def body(a_ref, b_ref, *rest):
        o_ref = rest[n_in]
        acc_ref = rest[n_in + 1 + n_out]
        if ride:
            j, i, kk = pl.program_id(0), pl.program_id(1), pl.program_id(2)
            ride.run((j == 0) & (i == 0) & (kk == 0), (j == nj - 1) & (i == ni - 1) & (kk == nk - 1),
                     rest[:n_in], rest[n_in + 1:n_in + 1 + n_out], rest[n_in + 2 + n_out:])
        av = a_ref[...].astype(BF16)
        bv = b_ref[...].astype(BF16)
        part = _dot_tn(av, bv) if mode == "tn" else _dot_nt(av, bv) if mode == "nt" else _dot(av, bv)
        if nk == 1:
            o_ref[...] = part.astype(out_dtype)
        else:
            kk = pl.program_id(2)

            @pl.when(kk == 0)
            def _():
                acc_ref[...] = part

            @pl.when(kk > 0)
            def _():
                acc_ref[...] += part

            @pl.when(kk == nk - 1)
            def _():
                o_ref[...] = acc_ref[...].astype(out_dtype)

    a_spec = (pl.BlockSpec((tk, tm), lambda j, i, kk: (kk, i)) if mode == "tn"
              else pl.BlockSpec((tm, tk), lambda j, i, kk: (i, kk)))
    b_spec = (pl.BlockSpec((tn, tk), lambda j, i, kk: (j, kk)) if mode == "nt"
              else pl.BlockSpec((tk, tn), lambda j, i, kk: (kk, j)))
    o_spec = pl.BlockSpec((tm, tn), lambda j, i, kk: (i, j))
    o_shape = jax.ShapeDtypeStruct((m, n), out_dtype)
    if not ride:
        return pl.pallas_call(
            body, name=name, grid=(nj, ni, nk), out_shape=o_shape, in_specs=[a_spec, b_spec], out_specs=o_spec,
            scratch_shapes=[pltpu.VMEM((tm, tn), F32)],
            compiler_params=_params(("parallel", "parallel", "arbitrary")),
        )(a, b)
    return pl.pallas_call(
        body, name=name, grid=(nj, ni, nk),
        out_shape=(o_shape, *ride.out_shapes),
        in_specs=[a_spec, b_spec] + ride.in_specs,
        out_specs=(o_spec,) + (ANY,) * n_out,
        scratch_shapes=[pltpu.VMEM((tm, tn), F32)] + ride.scratch(),
        compiler_params=_params(("arbitrary", "arbitrary", "arbitrary")),
    )(a, b, *ride.args)


def _prep(proj, pos, w_uq, w_ukv, g_cq, g_ckv, rope_a, rope_b, scales, tm=256):
    t = proj.shape[0]
    sc_a, sc_b, sc_m = (s * LOG2E for s in scales)

    def body(aq_ref, ak_ref, av_ref, bs_ref, mq_ref, pos_ref, wuq_ref, wukv_ref, gcq_ref, gckv_ref,
             ra_ref, rb_ref, qa_ref, ka_ref, va_ref, qb_ref, kb_ref, vb_ref, qm_ref, cqn_ref, ckvn_ref):
        pos_c = pos_ref[...]
        ta = _rope_tables(pos_c, ra_ref[...])
        tb = _rope_tables(pos_c, rb_ref[...])
        for j in range(A_WIDTH // LANES):
            sl = slice(j * LANES, (j + 1) * LANES)
            qa_ref[:, sl] = (_rope(aq_ref[:, sl], ta, 8) * sc_a).astype(BF16)
            ka_ref[:, sl] = _rope(ak_ref[:, sl], ta, 8).astype(BF16)
        va_ref[...] = av_ref[...].astype(BF16)
        qm_ref[...] = (mq_ref[...] * sc_m).astype(BF16)

        cq_hat, _ = _rms_hat(bs_ref[:, 0:MLA_Q_RANK], MLA_Q_RANK)
        cqn = (cq_hat * gcq_ref[...]).astype(BF16)
        cqn_ref[...] = cqn
        ckv_hat, _ = _rms_hat(bs_ref[:, MLA_Q_RANK:MLA_Q_RANK + MLA_KV_RANK], MLA_KV_RANK)
        ckvn = (ckv_hat * gckv_ref[...]).astype(BF16)
        ckvn_ref[...] = ckvn
        qfull = _dot_nt(cqn, wuq_ref[...])
        kv = _dot(ckvn, wukv_ref[...])
        kr = _rope(bs_ref[:, 384:512], tb, 16)
        lane = lax.broadcasted_iota(jnp.int32, (1, LANES), 1)
        low = lane < 64
        for h in range(MLA_HEADS):
            sl = slice(h * LANES, (h + 1) * LANES)
            qb_ref[:, sl] = (_rope(qfull[:, sl], tb, 16) * sc_b).astype(BF16)
            kb_ref[:, sl] = jnp.where(low, kv[:, sl], kr).astype(BF16)
            vb_ref[:, sl] = jnp.where(low, 0.0, kv[:, sl]).astype(BF16)

    def col(width, idx):
        return pl.BlockSpec((tm, width), lambda i: (i, idx))

    def full(shape):
        return pl.BlockSpec(shape, lambda i: (0, 0))

    wide = jax.ShapeDtypeStruct((t, 1024), BF16)
    return pl.pallas_call(
        body, name="prep", grid=(t // tm,),
        out_shape=(wide, wide, wide, wide, wide, wide,
                   jax.ShapeDtypeStruct((t, MEM_WIDTH), BF16),
                   jax.ShapeDtypeStruct((t, MLA_Q_RANK), BF16),
                   jax.ShapeDtypeStruct((t, MLA_KV_RANK), BF16)),
        in_specs=[col(1024, 0), col(1024, 1), col(1024, 2), col(512, COL_CQ // 512), col(512, COL_MQ // 512),
                  pl.BlockSpec((tm, 1), lambda i: (i, 0)),
                  full((1024, MLA_Q_RANK)), full((MLA_KV_RANK, 1024)),
                  full((1, MLA_Q_RANK)), full((1, MLA_KV_RANK)), full((8, LANES)), full((8, LANES))],
        out_specs=(col(1024, 0),) * 6 + (col(MEM_WIDTH, 0), col(MLA_Q_RANK, 0), col(MLA_KV_RANK, 0)),
        compiler_params=_params(("parallel",)),
    )(proj, proj, proj, proj, proj, pos, w_uq, w_ukv, g_cq, g_ckv, rope_a, rope_b)


def _attn_fwd(q, k, v, *, nb, s, sk, heads, hpb, voff, bq, name):
    nq = s // bq
    width = hpb * LANES
    vblk = voff // hpb

    def body(q_ref, k_ref, v_ref, o_ref, lse_ref):
        for h in range(hpb):
            sl = slice(h * LANES, (h + 1) * LANES)
            sc = _dot_nt(q_ref[:, sl], k_ref[:, sl])
            m = jnp.max(sc, axis=1, keepdims=True)
            p = jnp.exp2(sc - m)
            l = jnp.sum(p, axis=1, keepdims=True)
            o_ref[:, sl] = _dot(p.astype(BF16), v_ref[:, sl]) / l
            lse_ref[:, sl] = jnp.broadcast_to(m + jnp.log(l) * LOG2E, (bq, LANES))

    out = jax.ShapeDtypeStruct((nb * s, heads * LANES), F32)
    ospec = pl.BlockSpec((bq, width), lambda b, i, g: (b * nq + i, g))
    return pl.pallas_call(
        body, name=name, grid=(nb, nq, heads // hpb),
        out_shape=(out, out),
        in_specs=[ospec, pl.BlockSpec((sk, width), lambda b, i, g: (b, g)),
                  pl.BlockSpec((sk, width), lambda b, i, g: (b, vblk + g))],
        out_specs=(ospec, ospec),
        compiler_params=_params(("parallel", "parallel", "parallel")),
    )(q, k, v)


def _attn_bwd(q, k, v, o, do, lse, *, nb, s, sk, heads, hpb, voff, scale, bq, name):
    nq = s // bq
    width = hpb * LANES
    vblk = voff // hpb

    def body(q_ref, k_ref, v_ref, o_ref, do_ref, lse_ref, dq_ref, dk_ref, dv_ref, dk_acc, dv_acc):
        i = pl.program_id(2)

        @pl.when(i == 0)
        def _():
            dk_acc[...] = jnp.zeros_like(dk_acc)
            dv_acc[...] = jnp.zeros_like(dv_acc)

        for h in range(hpb):
            sl = slice(h * LANES, (h + 1) * LANES)
            qh = q_ref[:, sl]
            kk = k_ref[:, sl]
            doh = do_ref[:, sl]
            delta = jnp.sum(doh.astype(F32) * o_ref[:, sl], axis=1, keepdims=True)
            p = jnp.exp2(_dot_nt(qh, kk) - lse_ref[:, h * LANES:h * LANES + 1])
            ds = (p * (_dot_nt(doh, v_ref[:, sl]) - delta)).astype(BF16)
            dq_ref[:, sl] = (_dot(ds, kk) * scale).astype(BF16)
            dk_acc[:, sl] += _dot_tn(ds, qh)
            dv_acc[:, sl] += _dot_tn(p.astype(BF16), doh)

        @pl.when(i == nq - 1)
        def _():
            dk_ref[...] = (dk_acc[...] * LN2).astype(BF16)
            dv_ref[...] = dv_acc[...].astype(BF16)

    qspec = pl.BlockSpec((bq, width), lambda b, g, i: (b * nq + i, g))
    kv_spec = pl.BlockSpec((sk, width), lambda b, g, i: (b, g))
    dq_shape = jax.ShapeDtypeStruct((nb * s, heads * LANES), BF16)
    dkv_shape = jax.ShapeDtypeStruct((nb * sk, heads * LANES), BF16)
    return pl.pallas_call(
        body, name=name, grid=(nb, heads // hpb, nq),
        out_shape=(dq_shape, dkv_shape, dkv_shape),
        in_specs=[qspec, kv_spec, pl.BlockSpec((sk, width), lambda b, g, i: (b, vblk + g)), qspec, qspec, qspec],
        out_specs=(qspec, kv_spec, kv_spec),
        scratch_shapes=[pltpu.VMEM((sk, width), F32), pltpu.VMEM((sk, width), F32)],
        compiler_params=_params(("parallel", "parallel", "arbitrary")),
    )(q, k, v, o, do, lse)


BAND_Q = 128
BAND_WIN = 256


def _band_start(i, s):
    return min(max(i * BAND_Q - 64, 0), s - BAND_WIN)


def _to_pattern_order(src_ref, dst_ref, stage_ref, s, d):
    length = s // d
    stage_ref[...] = src_ref[...].astype(F32)
    for r in range(d):
        dst_ref[r * length:(r + 1) * length, :] = stage_ref[pl.ds(r, length, stride=d), :].astype(dst_ref.dtype)


def _dilated_fwd(q, k, v, bias, bias_index, *, nb, s, name):
    nblk = s // BAND_Q
    npat = len(DILATED)

    def body(q_ref, k_ref, v_ref, bias_ref, o_ref, lse_ref, stage_ref, qp_ref, kp_ref, vp_ref, op_ref, lp_ref,
             on_ref, ln_ref):
        lane = lax.broadcasted_iota(jnp.int32, (1, LANES), 1)
        first = lane < 64
        for p, (_, d) in enumerate(DILATED):
            if d == 1:
                qs, ks, vs = q_ref, k_ref, v_ref
            else:
                for src, dst in ((q_ref, qp_ref), (k_ref, kp_ref), (v_ref, vp_ref)):
                    _to_pattern_order(src, dst, stage_ref, s, d)
                qs, ks, vs = qp_ref, kp_ref, vp_ref
            for i in range(nblk):
                u0 = i * BAND_Q
                st = _band_start(i, s)
                qi = qs[u0:u0 + BAND_Q, :]
                kw = ks[st:st + BAND_WIN, :]
                vw = vs[st:st + BAND_WIN, :]
                zero = jnp.zeros_like(qi)
                q2 = jnp.concatenate([jnp.where(first, qi, zero), jnp.where(first, zero, qi)], axis=0)
                sc = _dot_nt(q2, kw)
                b = bias_ref[bias_index[p][i]]
                halves = []
                for h in range(2):
                    sh = sc[h * BAND_Q:(h + 1) * BAND_Q] + b
                    m = jnp.max(sh, axis=1, keepdims=True)
                    pr = jnp.exp2(sh - m)
                    l = jnp.sum(pr, axis=1, keepdims=True)
                    halves.append((pr.astype(BF16), l, m + jnp.log(l) * LOG2E))
                o2 = _dot(jnp.concatenate([halves[0][0], halves[1][0]], axis=0), vw)
                o_blk = jnp.where(first, o2[:BAND_Q] / halves[0][1], o2[BAND_Q:] / halves[1][1])
                lse_blk = jnp.where(first, jnp.broadcast_to(halves[0][2], (BAND_Q, LANES)),
                                    jnp.broadcast_to(halves[1][2], (BAND_Q, LANES)))
                op_ref[p, u0:u0 + BAND_Q, :] = o_blk
                lp_ref[p, u0:u0 + BAND_Q, :] = lse_blk
            if d > 1:
                length = s // d
                for r in range(d):
                    on_ref.at[p - 1][pl.ds(r, length, stride=d), :] = op_ref[p, r * length:(r + 1) * length, :]
                    ln_ref.at[p - 1][pl.ds(r, length, stride=d), :] = lp_ref[p, r * length:(r + 1) * length, :]
        lses = [lp_ref[0]] + [ln_ref[p] for p in range(npat - 1)]
        outs = [op_ref[0]] + [on_ref[p] for p in range(npat - 1)]
        m = functools.reduce(jnp.maximum, lses)
        ws = [jnp.exp2(l - m) for l in lses]
        den = functools.reduce(lambda a, c: a + c, ws)
        o_ref[...] = functools.reduce(lambda a, c: a + c, [w * o for w, o in zip(ws, outs)]) / den
        lse_ref[...] = m + jnp.log(den) * LOG2E

    blk = pl.BlockSpec((s, LANES), lambda b, g: (b, g))
    out = jax.ShapeDtypeStruct((nb * s, A_WIDTH), F32)
    return pl.pallas_call(
        body, name=name, grid=(nb, A_WIDTH // LANES),
        out_shape=(out, out),
        in_specs=[blk, blk, blk, pl.BlockSpec(bias.shape, lambda b, g: (0, 0, 0))],
        out_specs=(blk, blk),
        scratch_shapes=[pltpu.VMEM((s, LANES), F32), pltpu.VMEM((s, LANES), BF16), pltpu.VMEM((s, LANES), BF16),
                        pltpu.VMEM((s, LANES), BF16), pltpu.VMEM((npat, s, LANES), F32),
                        pltpu.VMEM((npat, s, LANES), F32), pltpu.VMEM((npat - 1, s, LANES), F32),
                        pltpu.VMEM((npat - 1, s, LANES), F32)],
        compiler_params=_params(("parallel", "parallel")),
    )(q, k, v, bias)


def _dilated_bwd(q, k, v, o, do, lse, bias, bias_index, *, nb, s, scale, name):
    nblk = s // BAND_Q
    npat = len(DILATED)

    def body(q_ref, k_ref, v_ref, o_ref, do_ref, lse_ref, bias_ref, dq_out, dk_out, dv_out,
             stage_ref, dl_ref, qp_ref, kp_ref, vp_ref, dop_ref, lsp_ref, dlp_ref, dqp_ref, dkp_ref, dvp_ref,
             dq_ref, dk_ref, dv_ref):
        lane = lax.broadcasted_iota(jnp.int32, (1, LANES), 1)
        first = lane < 64
        prod = do_ref[...].astype(F32) * o_ref[...]
        d0 = jnp.sum(jnp.where(first, prod, 0.0), axis=1, keepdims=True)
        d1 = jnp.sum(jnp.where(first, 0.0, prod), axis=1, keepdims=True)
        dl_ref[...] = jnp.where(first, jnp.broadcast_to(d0, (s, LANES)), jnp.broadcast_to(d1, (s, LANES)))
        for p, (_, d) in enumerate(DILATED):
            length = s // d
            if d == 1:
                qs, ks, vs, dos, lss, dls = q_ref, k_ref, v_ref, do_ref, lse_ref, dl_ref
                dqs, dks, dvs = dq_ref, dk_ref, dv_ref
            else:
                for src, dst in ((q_ref, qp_ref), (k_ref, kp_ref), (v_ref, vp_ref), (do_ref, dop_ref),
                                 (lse_ref, lsp_ref), (dl_ref, dlp_ref)):
                    _to_pattern_order(src, dst, stage_ref, s, d)
                qs, ks, vs, dos, lss, dls = qp_ref, kp_ref, vp_ref, dop_ref, lsp_ref, dlp_ref
                dqs, dks, dvs = dqp_ref, dkp_ref, dvp_ref
            dks[...] = jnp.zeros((s, LANES), F32)
            dvs[...] = jnp.zeros((s, LANES), F32)
            for i in range(nblk):
                u0 = i * BAND_Q
                st = _band_start(i, s)
                qi = qs[u0:u0 + BAND_Q, :]
                doi = dos[u0:u0 + BAND_Q, :]
                kw = ks[st:st + BAND_WIN, :]
                vw = vs[st:st + BAND_WIN, :]
                zero = jnp.zeros_like(qi)
                q2 = jnp.concatenate([jnp.where(first, qi, zero), jnp.where(first, zero, qi)], axis=0)
                do2 = jnp.concatenate([jnp.where(first, doi, zero), jnp.where(first, zero, doi)], axis=0)
                sc = _dot_nt(q2, kw)
                dp = _dot_nt(do2, vw)
                b = bias_ref[bias_index[p][i]]
                lse_i = lss[u0:u0 + BAND_Q, :]
                dl_i = dls[u0:u0 + BAND_Q, :]
                ps, dss = [], []
                for h in range(2):
                    rows = slice(h * BAND_Q, (h + 1) * BAND_Q)
                    pr = jnp.exp2(sc[rows] + b - lse_i[:, 64 * h:64 * h + 1])
                    ps.append(pr.astype(BF16))
                    dss.append((pr * (dp[rows] - dl_i[:, 64 * h:64 * h + 1])).astype(BF16))
                p2 = jnp.concatenate(ps, axis=0)
                ds2 = jnp.concatenate(dss, axis=0)
                dq2 = _dot(ds2, kw)
                dqs[u0:u0 + BAND_Q, :] = jnp.where(first, dq2[:BAND_Q], dq2[BAND_Q:]) * scale
                dks[st:st + BAND_WIN, :] += _dot_tn(ds2, q2)
                dvs[st:st + BAND_WIN, :] += _dot_tn(p2, do2)
            if d > 1:
                for dst, src in ((dq_ref, dqp_ref), (dk_ref, dkp_ref), (dv_ref, dvp_ref)):
                    for r in range(d):
                        dst[pl.ds(r, length, stride=d), :] += src[r * length:(r + 1) * length, :]
        dq_out[...] = dq_ref[...].astype(BF16)
        dk_out[...] = (dk_ref[...] * LN2).astype(BF16)
        dv_out[...] = dv_ref[...].astype(BF16)

    blk = pl.BlockSpec((s, LANES), lambda b, g: (b, g))
    out = jax.ShapeDtypeStruct((nb * s, A_WIDTH), BF16)
    f32_buf = pltpu.VMEM((s, LANES), F32)
    bf_buf = pltpu.VMEM((s, LANES), BF16)
    return pl.pallas_call(
        body, name=name, grid=(nb, A_WIDTH // LANES),
        out_shape=(out, out, out),
        in_specs=[blk] * 6 + [pl.BlockSpec(bias.shape, lambda b, g: (0, 0, 0))],
        out_specs=(blk, blk, blk),
        scratch_shapes=[f32_buf, f32_buf, bf_buf, bf_buf, bf_buf, bf_buf] + [f32_buf] * 8,
        compiler_params=_params(("parallel", "parallel")),
    )(q, k, v, o, do, lse, bias)


def _post(x, ya, ybp, ym, proj, target, w_out, g_emb, b_emb, g_a, g_b, g_m, g_post, b_post, tm=256):
    t = x.shape[0]

    def body(x_ref, ya_ref, yb_ref, ym_ref, ga_ref, gb_ref, gm_ref, tg_ref, wo_ref,
             ge_ref, be_ref, goa_ref, gob_ref, gom_ref, gp_ref, bp_ref,
             y_ref, dz_ref, doa_ref, dob_ref, dom_ref, dga_ref, dgb_ref, dgm_ref,
             loss_ref, dgp_ref, dbp_ref, dgoa_ref, dgob_ref, dgom_ref):
        i = pl.program_id(0)

        @pl.when(i == 0)
        def _():
            for r in (loss_ref, dgp_ref, dbp_ref, dgoa_ref, dgob_ref, dgom_ref):
                r[...] = jnp.zeros_like(r)

        lane = lax.broadcasted_iota(jnp.int32, (1, LANES), 1)
        low = lane < 64
        xh0, _ = _ln_hat(x_ref[...])
        h = xh0 * ge_ref[...] + be_ref[...]

        ybp_v = yb_ref[...]
        yb = jnp.concatenate(
            [jnp.where(low, pltpu.roll(ybp_v[:, 2 * j * LANES:(2 * j + 1) * LANES], 64, 1),
                       ybp_v[:, (2 * j + 1) * LANES:(2 * j + 2) * LANES]) for j in range(4)], axis=1)

        def gated(raw, gate, gain, width):
            xh, r = _rms_hat(raw, width)
            n = xh * gain
            sg = 1.0 / (1.0 + jnp.exp(-gate))
            return xh, r, n, sg, n * (gate * sg)

        gate_a, gate_b, gate_m = ga_ref[...], gb_ref[...], gm_ref[...]
        xh_a, r_a, n_a, sg_a, y_a = gated(ya_ref[...], gate_a, goa_ref[...], A_WIDTH)
        xh_b, r_b, n_b, sg_b, y_b = gated(yb, gate_b, gob_ref[...], 512)
        xh_m, r_m, n_m, sg_m, y_m = gated(ym_ref[...], gate_m, gom_ref[...], 512)
        y = jnp.concatenate([y_a, y_b, y_m], axis=1).astype(BF16)
        y_ref[...] = y
        z = DEEPNORM_ALPHA * h + _dot(y, wo_ref[...])
        zh, rstd = _ln_hat(z)
        err = zh * gp_ref[...] + bp_ref[...] - tg_ref[...]
        rows = jnp.sum(err * err, axis=1, keepdims=True)
        loss_ref[...] += jnp.broadcast_to(jnp.sum(rows, axis=0, keepdims=True) * (0.5 / D_MODEL), (1, LANES))
        dout = err * (1.0 / D_MODEL)
        dgp_ref[...] += _colsum(dout * zh)
        dbp_ref[...] += _colsum(dout)
        dz = _ln_bwd_rows(dout * gp_ref[...], zh, rstd)
        dz_ref[...] = dz
        dy = _dot_nt(dz.astype(BF16), wo_ref[...])

        def gated_bwd(dyg, xh, r, n, sg, gate, gain, width, dgain_ref):
            dn = dyg * (gate * sg)
            dgate = dyg * n * (sg * (1.0 + gate * (1.0 - sg)))
            dgain_ref[...] += _colsum(dn * xh)
            return _rms_bwd(dn * gain, xh, r, width), dgate

        dya, dgate_a = gated_bwd(dy[:, 0:1024], xh_a, r_a, n_a, sg_a, gate_a, goa_ref[...], A_WIDTH, dgoa_ref)
        dyb, dgate_b = gated_bwd(dy[:, 1024:1536], xh_b, r_b, n_b, sg_b, gate_b, gob_ref[...], 512, dgob_ref)
        dym, dgate_m = gated_bwd(dy[:, 1536:2048], xh_m, r_m, n_m, sg_m, gate_m, gom_ref[...], 512, dgom_ref)
        doa_ref[...] = dya.astype(BF16)
        dom_ref[...] = dym.astype(BF16)
        dga_ref[...] = dgate_a.astype(BF16)
        dgb_ref[...] = dgate_b.astype(BF16)
        dgm_ref[...] = dgate_m.astype(BF16)
        for j in range(4):
            blk = dyb[:, j * LANES:(j + 1) * LANES]
            dob_ref[:, 2 * j * LANES:(2 * j + 1) * LANES] = jnp.where(low, 0.0, pltpu.roll(blk, 64, 1)).astype(BF16)
            dob_ref[:, (2 * j + 1) * LANES:(2 * j + 2) * LANES] = jnp.where(low, 0.0, blk).astype(BF16)

    def col(width, idx):
        return pl.BlockSpec((tm, width), lambda i: (i, idx))

    def full(shape):
        return pl.BlockSpec(shape, lambda i: (0, 0))

    def acc(width):
        return jax.ShapeDtypeStruct((1, width), F32)

    return pl.pallas_call(
        body, name="post", grid=(t // tm,),
        out_shape=(jax.ShapeDtypeStruct((t, 2048), BF16), jax.ShapeDtypeStruct((t, 1024), F32),
                   jax.ShapeDtypeStruct((t, 1024), BF16), jax.ShapeDtypeStruct((t, 1024), BF16),
                   jax.ShapeDtypeStruct((t, 512), BF16),
                   jax.ShapeDtypeStruct((t, 1024), BF16), jax.ShapeDtypeStruct((t, 512), BF16),
                   jax.ShapeDtypeStruct((t, 512), BF16),
                   acc(LANES), acc(1024), acc(1024), acc(1024), acc(512), acc(512)),
        in_specs=[col(1024, 0), col(1024, 0), col(1024, 0), col(512, 0),
                  col(1024, 3), col(512, COL_BG // 512), col(512, COL_MG // 512), col(1024, 0),
                  full((2048, 1024)),
                  full((1, 1024)), full((1, 1024)), full((1, 1024)), full((1, 512)), full((1, 512)),
                  full((1, 1024)), full((1, 1024))],
        out_specs=(col(2048, 0), col(1024, 0), col(1024, 0), col(1024, 0), col(512, 0),
                   col(1024, 0), col(512, 0), col(512, 0),
                   full((1, LANES)), full((1, 1024)), full((1, 1024)), full((1, 1024)), full((1, 512)),
                   full((1, 512))),
        compiler_params=_params(("arbitrary",)),
    )(x, ya, ybp, ym, proj, proj, proj, target, w_out, g_emb, b_emb, g_a, g_b, g_m, g_post, b_post)


def _prep_bwd(dqa, dka, dva, dqb, dkb, dvb, dqm, dga, dgb, dgm, proj, pos, w_uq, w_ukv, g_cq, g_ckv,
              rope_a, rope_b, tm=256):
    t = proj.shape[0]

    def body(dqa_ref, dka_ref, dva_ref, dqb_ref, dkb_ref, dvb_ref, dqm_ref, dga_ref, dgb_ref, dgm_ref,
             bs_ref, pos_ref, wuq_ref, wukv_ref, gcq_ref, gckv_ref, ra_ref, rb_ref,
             dproj_ref, dqf_ref, dkv_ref, dgcq_ref, dgckv_ref):
        i = pl.program_id(0)

        @pl.when(i == 0)
        def _():
            dgcq_ref[...] = jnp.zeros_like(dgcq_ref)
            dgckv_ref[...] = jnp.zeros_like(dgckv_ref)

        pos_c = pos_ref[...]
        ta = _rope_tables(pos_c, ra_ref[...])
        tb = _rope_tables(pos_c, rb_ref[...])
        for j in range(A_WIDTH // LANES):
            sl = slice(j * LANES, (j + 1) * LANES)
            dproj_ref[:, j * LANES:(j + 1) * LANES] = (
                _rope(dqa_ref[:, sl].astype(F32), ta, 8, inverse=True).astype(BF16))
            dproj_ref[:, 1024 + j * LANES:1024 + (j + 1) * LANES] = (
                _rope(dka_ref[:, sl].astype(F32), ta, 8, inverse=True).astype(BF16))
        dproj_ref[:, 2048:3072] = dva_ref[...]
        dproj_ref[:, 3072:4096] = dga_ref[...]

        lane = lax.broadcasted_iota(jnp.int32, (1, LANES), 1)
        low = lane < 64
        rope_lanes = (lane >= 64) & (lane < 96)
        dkr = jnp.zeros((tm, LANES), F32)
        for h in range(MLA_HEADS):
            sl = slice(h * LANES, (h + 1) * LANES)
            dqf_ref[:, sl] = _rope(dqb_ref[:, sl].astype(F32), tb, 16, inverse=True).astype(BF16)
            dk_h = dkb_ref[:, sl]
            dkv_ref[:, sl] = jnp.where(low, dk_h, dvb_ref[:, sl])
            dkr = dkr + jnp.where(rope_lanes, dk_h.astype(F32), 0.0)
        dkr = _rope(dkr, tb, 16, inverse=True)

        cq_hat, r_q = _rms_hat(bs_ref[:, 0:MLA_Q_RANK], MLA_Q_RANK)
        dcqn = _dot(dqf_ref[...], wuq_ref[...])
        dgcq_ref[...] += _colsum(dcqn * cq_hat)
        dproj_ref[:, COL_CQ:COL_CQ + 256] = _rms_bwd(dcqn * gcq_ref[...], cq_hat, r_q, MLA_Q_RANK).astype(BF16)
        ckv_hat, r_kv = _rms_hat(bs_ref[:, MLA_Q_RANK:MLA_Q_RANK + MLA_KV_RANK], MLA_KV_RANK)
        dckvn = _dot_nt(dkv_ref[...], wukv_ref[...])
        dgckv_ref[...] += _colsum(dckvn * ckv_hat)
        dproj_ref[:, COL_CQ + 256:COL_CQ + 384] = (
            _rms_bwd(dckvn * gckv_ref[...], ckv_hat, r_kv, MLA_KV_RANK).astype(BF16))
        dproj_ref[:, COL_CQ + 384:COL_CQ + 512] = dkr.astype(BF16)
        dproj_ref[:, COL_BG:COL_BG + 512] = dgb_ref[...]
        dproj_ref[:, COL_MQ:COL_MQ + 512] = dqm_ref[...]
        dproj_ref[:, COL_MG:COL_MG + 512] = dgm_ref[...]

    def col(width, idx):
        return pl.BlockSpec((tm, width), lambda i: (i, idx))

    def full(shape):
        return pl.BlockSpec(shape, lambda i: (0, 0))

    return pl.pallas_call(
        body, name="prep_bwd", grid=(t // tm,),
        out_shape=(jax.ShapeDtypeStruct((t, PROJ_W), BF16), jax.ShapeDtypeStruct((t, 1024), BF16),
                   jax.ShapeDtypeStruct((t, 1024), BF16),
                   jax.ShapeDtypeStruct((1, MLA_Q_RANK), F32), jax.ShapeDtypeStruct((1, MLA_KV_RANK), F32)),
        in_specs=[col(1024, 0)] * 6 + [col(512, 0), col(1024, 0), col(512, 0), col(512, 0),
                  col(512, COL_CQ // 512), pl.BlockSpec((tm, 1), lambda i: (i, 0)),
                  full((1024, MLA_Q_RANK)), full((MLA_KV_RANK, 1024)),
                  full((1, MLA_Q_RANK)), full((1, MLA_KV_RANK)), full((8, LANES)), full((8, LANES))],
        out_specs=(col(PROJ_W, 0), col(1024, 0), col(1024, 0), full((1, MLA_Q_RANK)), full((1, MLA_KV_RANK))),
        compiler_params=_params(("arbitrary",)),
    )(dqa, dka, dva, dqb, dkb, dvb, dqm, dga, dgb, dgm, proj, pos, w_uq, w_ukv, g_cq, g_ckv, rope_a, rope_b)


def _adamw_math(gv, w, m, v):
    m_new = ADAM_B1 * m + (1.0 - ADAM_B1) * gv
    v_new = ADAM_B2 * v + (1.0 - ADAM_B2) * (gv * gv)
    m_hat = m_new / (1.0 - ADAM_B1 ** ADAM_STEP)
    v_hat = v_new / (1.0 - ADAM_B2 ** ADAM_STEP)
    return -ADAM_LR * (m_hat / (jnp.sqrt(v_hat) + ADAM_EPS) + ADAM_WD * w), m_new, v_new


def _adamw(g, w, m, v, tr, name):
    r, cols = w.shape

    def body(g_ref, w_ref, m_ref, v_ref, go_ref, d_ref, nm_ref, nv_ref):
        gv = g_ref[...]
        go_ref[...] = gv
        d_ref[...], nm_ref[...], nv_ref[...] = _adamw_math(gv, w_ref[...], m_ref[...], v_ref[...])

    tile = pl.BlockSpec((tr, cols), lambda i: (i, 0))
    shape = jax.ShapeDtypeStruct((r, cols), F32)
    return pl.pallas_call(
        body, name=name, grid=(r // tr,),
        out_shape=(shape,) * 4, in_specs=[tile] * 4, out_specs=(tile,) * 4,
        compiler_params=_params(("parallel",)),
    )(g, w, m, v)


def _adamw_pieces(g, w, m, v, pieces, name):
    shapes = [jax.ShapeDtypeStruct((r1 - r0, c1 - c0), F32) for r0, r1, c0, c1 in pieces]

    def body(g_ref, w_ref, m_ref, v_ref, *outs):
        gv = g_ref[...]
        results = (gv,) + _adamw_math(gv, w_ref[...], m_ref[...], v_ref[...])
        for kind, full in enumerate(results):
            for p, (r0, r1, c0, c1) in enumerate(pieces):
                outs[kind * len(pieces) + p][...] = full[r0:r1, c0:c1]

    flat = pl.pallas_call(
        body, name=name, out_shape=tuple(shapes) * 4,
        in_specs=[IN_VMEM] * 4, out_specs=tuple([IN_VMEM] * (4 * len(pieces))),
        compiler_params=_params(None),
    )(g, w, m, v)
    return [[flat[kind * len(pieces) + p] for kind in range(4)] for p in range(len(pieces))]


def _core_sum(g, recv, core, rows, tr, name, ride=None):
    cols = g.shape[2]
    nblk = rows // tr
    n_in = len(ride.args) if ride else 0
    n_out = len(ride.out_shapes) if ride else 0

    def body(c_ref, g_ref, r_ref, *rest):
        sf_ref, sb_ref = rest[n_in], rest[n_in + 1]
        if ride:
            j, i = pl.program_id(0), pl.program_id(1)
            ride.run((j == 0) & (i == 0), (j == 3) & (i == nblk - 1), rest[:n_in],
                     rest[n_in + 2:n_in + 2 + n_out], rest[n_in + 2 + n_out:])
        tot = g_ref[...] + r_ref[...]
        sf_ref[...] = tot
        sb_ref[...] = tot.astype(BF16)

    half = pl.BlockSpec((None, tr, cols), lambda j, i, c_ref: (j, i, 0))
    shapes = (jax.ShapeDtypeStruct((4, rows, cols), F32), jax.ShapeDtypeStruct((4, rows, cols), BF16))
    return pl.pallas_call(
        body, name=name,
        grid_spec=pltpu.PrefetchScalarGridSpec(
            num_scalar_prefetch=1, grid=(4, nblk),
            in_specs=[pl.BlockSpec((None, tr, cols), lambda j, i, c_ref: (j, c_ref[0] * nblk + i, 0)), half]
            + (ride.in_specs if ride else []),
            out_specs=(half, half) + (ANY,) * n_out,
            scratch_shapes=ride.scratch() if ride else []),
        out_shape=shapes + tuple(ride.out_shapes if ride else ()),
        compiler_params=_params(("arbitrary", "arbitrary") if ride else ("parallel", "parallel")),
    )(core, g, recv, *(ride.args if ride else ()))


def _half_to_sibling(g4):
    def plan(in_refs, out_refs, send_sems, recv_sems):
        x, y, c = _position()
        cp = pltpu.make_async_remote_copy(
            src_ref=in_refs[0].at[:, 1 - c], dst_ref=out_refs[0], send_sem=send_sems.at[0],
            recv_sem=recv_sems.at[0], device_id=(x, y, 1 - c), device_id_type=MESH)

        def finish():
            cp.wait_recv()
            cp.wait_send()

        return cp.start, finish

    return _Ride([g4], [jax.ShapeDtypeStruct((4, g4.shape[2], 1024), F32)], (1, 1), plan)


def _gather_plan(src_ref, dst_ref, send_sems, recv_sems, local_sems):
    x, y, c = _position()
    me = 2 * x + y
    local = pltpu.make_async_copy(src_ref, dst_ref.at[me], local_sems.at[0])

    def over_ici(k, src, chip):
        return pltpu.make_async_remote_copy(
            src_ref=src, dst_ref=dst_ref.at[chip, c], send_sem=send_sems.at[k - 1], recv_sem=recv_sems.at[k - 1],
            device_id=(x ^ (k >> 1), y ^ (k & 1), c), device_id_type=MESH)

    def to_sibling(k, half):
        piece = dst_ref.at[me ^ k, half]
        return pltpu.make_async_remote_copy(
            src_ref=piece, dst_ref=piece, send_sem=send_sems.at[2 + k], recv_sem=recv_sems.at[2 + k],
            device_id=(x, y, 1 - c), device_id_type=MESH)

    sends = [over_ici(k, src_ref.at[c], me) for k in (1, 2, 3)]

    def start():
        local.start()
        for cp in sends:
            cp.start()

    def finish():
        passed = []
        for k in (1, 2, 3):
            over_ici(k, dst_ref.at[me ^ k, c], me ^ k).wait_recv()
            cp = to_sibling(k, c)
            cp.start()
            passed.append(cp)
        for k in (1, 2, 3):
            to_sibling(k, 1 - c).wait_recv()
        for cp in sends + passed:
            cp.wait_send()
        local.wait()

    return start, finish


def _gather_ride(shard):
    def plan(in_refs, out_refs, send_sems, recv_sems, local_sems):
        return _gather_plan(in_refs[0], out_refs[0], send_sems, recv_sems, local_sems)

    return _Ride([shard], [jax.ShapeDtypeStruct((4,) + shard.shape, shard.dtype)], (6, 6, 1), plan,
                 in_specs=[IN_VMEM])


def _chip_sum(sf, recv, chip, rows, tr, name):
    cols = sf.shape[2]

    def body(me_ref, sf_ref, r_ref, out_ref):
        acc = sf_ref[...]
        for k in range(3):
            acc = acc + r_ref[k].astype(F32)
        out_ref[...] = acc

    return pl.pallas_call(
        body, name=name,
        grid_spec=pltpu.PrefetchScalarGridSpec(
            num_scalar_prefetch=1, grid=(rows // tr,),
            in_specs=[pl.BlockSpec((None, tr, cols), lambda i, me_ref: (me_ref[0], i, 0)),
                      pl.BlockSpec((3, tr, cols), lambda i, me_ref: (0, i, 0))],
            out_specs=pl.BlockSpec((tr, cols), lambda i, me_ref: (i, 0))),
        out_shape=jax.ShapeDtypeStruct((rows, cols), F32),
        compiler_params=_params(("parallel",)),
    )(chip, sf, recv)


def _position():
    return lax.axis_index("x"), lax.axis_index("y"), lax.axis_index("c")


def _dh_scatter(dproj, w_in_arr_t, x, dz, g, sb_in, sb_rest, tm=1024, tk=1024):
    t, d = x.shape
    nk = dproj.shape[1] // tk
    ni = t // tm

    def body(dp_ref, w_ref, x_ref, dz_ref, g_ref, sbin_ref, sbrest_ref,
             dx_ref, dg_ref, db_ref, rin_ref, rrest_ref, acc_ref, send_sems, recv_sems):
        i = pl.program_id(0)
        kk = pl.program_id(1)
        px, py, pc = _position()
        me = 2 * px + py
        srcs = (sbin_ref, sbrest_ref)
        dsts = (rin_ref, rrest_ref)

        def copy(a, k):
            return pltpu.make_async_remote_copy(
                src_ref=srcs[a].at[me ^ k], dst_ref=dsts[a].at[k - 1],
                send_sem=send_sems.at[3 * a + k - 1], recv_sem=recv_sems.at[3 * a + k - 1],
                device_id=(px ^ (k >> 1), py ^ (k & 1), pc), device_id_type=MESH)

        pairs = [(a, k) for a in range(2) for k in (1, 2, 3)]

        @pl.when((i == 0) & (kk == 0))
        def _():
            dg_ref[...] = jnp.zeros_like(dg_ref)
            db_ref[...] = jnp.zeros_like(db_ref)
            for a, k in pairs:
                copy(a, k).start()

        part = _dot(dp_ref[...], w_ref[...])

        @pl.when(kk == 0)
        def _():
            acc_ref[...] = part

        @pl.when(kk > 0)
        def _():
            acc_ref[...] += part

        @pl.when(kk == nk - 1)
        def _():
            xh, rstd = _ln_hat(x_ref[...])
            dht = acc_ref[...] + DEEPNORM_ALPHA * dz_ref[...]
            dg_ref[...] += _colsum(dht * xh)
            db_ref[...] += _colsum(dht)
            dx_ref[...] = _ln_bwd_rows(dht * g_ref[...], xh, rstd)

        @pl.when((i == ni - 1) & (kk == nk - 1))
        def _():
            for a, k in pairs:
                copy(a, k).wait_recv()
            for a, k in pairs:
                copy(a, k).wait_send()

    tile = pl.BlockSpec((tm, d), lambda i, kk: (i, 0))
    row = pl.BlockSpec((1, d), lambda i, kk: (0, 0))
    return pl.pallas_call(
        body, name="dh_scatter", grid=(ni, nk),
        out_shape=(jax.ShapeDtypeStruct((t, d), F32), jax.ShapeDtypeStruct((1, d), F32),
                   jax.ShapeDtypeStruct((1, d), F32),
                   jax.ShapeDtypeStruct((3, HALF_IN, 1024), BF16),
                   jax.ShapeDtypeStruct((3, HALF_REST, 1024), BF16)),
        in_specs=[pl.BlockSpec((tm, tk), lambda i, kk: (i, kk)), pl.BlockSpec((tk, d), lambda i, kk: (kk, 0)),
                  tile, tile, row, ANY, ANY],
        out_specs=(tile, row, row, ANY, ANY),
        scratch_shapes=[pltpu.VMEM((tm, d), F32), pltpu.SemaphoreType.DMA((6,)), pltpu.SemaphoreType.DMA((6,))],
        compiler_params=_params(("arbitrary", "arbitrary")),
    )(dproj, w_in_arr_t, x, dz, g, sb_in, sb_rest)


def _join_halves(gh_in, gh_rest):
    def body(hin_ref, hrest_ref, oin_ref, orest_ref, send_sems, recv_sems, local_sems):
        x, y, c = _position()
        srcs = (hin_ref, hrest_ref)
        dsts = (oin_ref, orest_ref)

        def rows(a, half):
            return dsts[a].at[half]

        local = [pltpu.make_async_copy(srcs[a], rows(a, c), local_sems.at[a]) for a in range(2)]
        remote = [pltpu.make_async_remote_copy(
            src_ref=srcs[a], dst_ref=rows(a, c), send_sem=send_sems.at[a], recv_sem=recv_sems.at[a],
            device_id=(x, y, 1 - c), device_id_type=MESH) for a in range(2)]
        for cp in local + remote:
            cp.start()
        for a in range(2):
            pltpu.make_async_remote_copy(
                src_ref=srcs[a], dst_ref=rows(a, 1 - c), send_sem=send_sems.at[a], recv_sem=recv_sems.at[a],
                device_id=(x, y, 1 - c), device_id_type=MESH).wait_recv()
        for cp in remote:
            cp.wait_send()
        for cp in local:
            cp.wait()

    return pl.pallas_call(
        body, name="join_halves",
        out_shape=(jax.ShapeDtypeStruct((2, HALF_IN, 1024), F32),
                   jax.ShapeDtypeStruct((2, HALF_REST, 1024), F32)),
        in_specs=[IN_VMEM, IN_VMEM], out_specs=(ANY, ANY),
        scratch_shapes=[pltpu.SemaphoreType.DMA((2,)), pltpu.SemaphoreType.DMA((2,)), pltpu.SemaphoreType.DMA((2,))],
    )(gh_in, gh_rest)


def _allreduce_small(vec):
    def body(vec_ref, out_ref, all_ref, send_sems, recv_sems):
        x, y, c = _position()
        me = 4 * x + 2 * y + c
        all_ref[me] = vec_ref[...]

        def copy(k, slot):
            return pltpu.make_async_remote_copy(
                src_ref=vec_ref, dst_ref=all_ref.at[slot], send_sem=send_sems.at[k - 1], recv_sem=recv_sems.at[k - 1],
                device_id=(x ^ (k >> 2), y ^ ((k >> 1) & 1), c ^ (k & 1)), device_id_type=MESH)

        copies = [copy(k, me) for k in range(1, 8)]
        for cp in copies:
            cp.start()
        for k in range(1, 8):
            copy(k, me ^ k).wait_recv()
        for cp in copies:
            cp.wait_send()
        total = all_ref[0]
        for d in range(1, 8):
            total = total + all_ref[d]
        out_ref[...] = total

    return pl.pallas_call(
        body, name="allreduce_small",
        out_shape=jax.ShapeDtypeStruct(vec.shape, vec.dtype),
        in_specs=[pl.BlockSpec(memory_space=pltpu.VMEM)], out_specs=pl.BlockSpec(memory_space=pltpu.VMEM),
        scratch_shapes=[pltpu.VMEM((8,) + vec.shape, vec.dtype), pltpu.SemaphoreType.DMA((7,)),
                        pltpu.SemaphoreType.DMA((7,))],
    )(vec)


def _pack_rest(w_uq, w_ukv, w_mem, w_out):
    rows = jnp.concatenate([w_uq[0].T.reshape(-1, 1024), w_ukv.reshape(-1, 1024), w_mem.reshape(-1, 1024),
                            w_out.reshape(-1, 1024)], axis=0)
    return jnp.pad(rows, ((0, ROWS_REST - ROWS_USED), (0, 0)))


def _arranged_w_in(g_in):
    z = functools.partial(jnp.zeros, dtype=g_in.dtype)
    cut = 4480 - 2 * SHARD_ROWS
    return jnp.concatenate(
        [g_in[0, :SHARD_ROWS], g_in[1, :SHARD_ROWS], g_in[2, :cut], z((64, 1024)), g_in[2, cut:cut + 32],
         z((32, 1024)), g_in[2, cut + 32:SHARD_ROWS], g_in[3, :SHARD_ROWS]], axis=0)


def _rest_weights(g_rest):
    w_uq_t = g_rest[:, 0:ROWS_UQ].reshape(768, 256)
    w_uq_pad_t = jnp.pad(w_uq_t.reshape(MLA_HEADS, MLA_QK_DIM, 256), ((0, 0), (0, 32), (0, 0))).reshape(1024, 256)
    w_ukv = jnp.concatenate([g_rest[j, ROWS_UQ:ROWS_UQ + ROWS_UKV].reshape(128, 256) for j in range(4)], axis=1)
    lo = ROWS_UQ + ROWS_UKV
    w_mem = g_rest[:, lo:lo + ROWS_MEM].reshape(4 * ROWS_MEM, 1024)
    w_out = g_rest[:, lo + ROWS_MEM:lo + ROWS_MEM + ROWS_OUT].reshape(4 * ROWS_OUT, 1024)
    return w_uq_pad_t, w_ukv, w_mem, w_out


def _split_in(dw_in_arr_t):
    a = dw_in_arr_t
    gap = jnp.zeros((ROWS_IN - SHARD_ROWS, 1024), a.dtype)
    nat = 4608 - 96
    pieces = [a[:SHARD_ROWS], gap, a[SHARD_ROWS:2 * SHARD_ROWS], gap,
              a[2 * SHARD_ROWS:4480], a[4544:4576], a[4608:4608 + 3 * SHARD_ROWS - nat], gap,
              a[4608 + 3 * SHARD_ROWS - nat:], gap]
    return jnp.concatenate(pieces, axis=0).reshape(4, ROWS_IN, 1024)


def _split_rest(dw_uq_pad_t, dw_ukv, dw_mem, dw_out):
    dw_uq_t = dw_uq_pad_t.reshape(MLA_HEADS, LANES, 256)[:, :MLA_QK_DIM].reshape(4, ROWS_UQ, 1024)
    parts = [dw_uq_t, dw_ukv.reshape(128, 4, 256).transpose(1, 0, 2).reshape(4, ROWS_UKV, 1024),
             dw_mem.reshape(4, ROWS_MEM, 1024), dw_out.reshape(4, ROWS_OUT, 1024)]
    return jnp.pad(jnp.concatenate(parts, axis=1), ((0, 0), (0, ROWS_REST - ROWS_USED), (0, 0)))


def _rope_consts(rot, first, period):
    half = rot // 2
    inv_freq = np.float32(ROPE_THETA) ** (-(np.arange(0, rot, 2, dtype=np.float32) / np.float32(rot)))
    lane = np.arange(LANES) % period - first
    in_rot = (lane >= 0) & (lane < rot)
    out = np.zeros((8, LANES), np.float32)
    out[0] = np.where(in_rot, inv_freq[np.clip(lane, 0, rot - 1) % half], 0.0)
    out[1] = in_rot & (lane < half)
    out[2] = in_rot & (lane >= half)
    return jnp.asarray(out)


def _band_bias(s):
    nblk = s // BAND_Q
    starts = np.array([_band_start(i, s) for i in range(nblk)])
    uq = (np.arange(nblk)[:, None] * BAND_Q + np.arange(BAND_Q)[None, :])[:, :, None]
    uk = (starts[:, None] + np.arange(BAND_WIN)[None, :])[:, None, :]
    tiles, index, seen = [], [], {}
    for _, d in DILATED:
        length = s // d
        ok = (uq // length == uk // length) & (np.abs(uq - uk) <= 64)
        row = []
        for i in range(nblk):
            key = ok[i].tobytes()
            if key not in seen:
                seen[key] = len(tiles)
                tiles.append(np.where(ok[i], 0.0, NEG_INF).astype(np.float32))
            row.append(seen[key])
        index.append(row)
    return jnp.asarray(np.stack(tiles, axis=0)), index


def _forward_backward(h, proj, x, mem, positions, target, weights, gains):
    w_uq_pad_t, w_ukv, w_mem, w_out = weights
    g_emb, b_emb, g_cq, g_ckv, g_out_a, g_out_b, g_out_m, g_post, b_post = gains
    nb, s, d = x.shape
    t = nb * s
    x2 = x.reshape(t, d)
    mem2 = mem.reshape(nb * N_MEM, d)
    tgt2 = target.reshape(t, d)
    pos = positions.reshape(t, 1).astype(F32)
    rope_a = _rope_consts(16, 0, 64)
    rope_b = _rope_consts(32, 64, 128)
    bias, bias_index = _band_bias(s)
    scales = (0.125, MLA_QK_DIM ** -0.5, 128 ** -0.5)

    qa, ka, va, qb, kb, vb, qm, cqn, ckvn = _prep(proj, pos, w_uq_pad_t, w_ukv, g_cq, g_ckv, rope_a, rope_b, scales)
    mkv = _mm(mem2, w_mem, BF16, nb * N_MEM, 1024, 1024, "mem_kv")

    cfg_b = dict(nb=nb, s=s, sk=s, heads=8, hpb=2, voff=0, bq=256)
    cfg_m = dict(nb=nb, s=s, sk=N_MEM, heads=4, hpb=2, voff=4, bq=1024)
    ya, lse_a = _dilated_fwd(qa, ka, va, bias, bias_index, nb=nb, s=s, name="attn_a_fwd")
    yb, lse_b = _attn_fwd(qb, kb, vb, name="attn_b_fwd", **cfg_b)
    ym, lse_m = _attn_fwd(qm, mkv, mkv, name="attn_m_fwd", **cfg_m)

    (y, dz, doa, dob, dom, dga, dgb, dgm, loss, dg_post, db_post, dg_a, dg_b, dg_m) = _post(
        x2, ya, yb, ym, proj, tgt2, w_out, g_emb, b_emb, g_out_a, g_out_b, g_out_m, g_post, b_post)

    dqa, dka, dva = _dilated_bwd(qa, ka, va, ya, doa, lse_a, bias, bias_index, nb=nb, s=s, scale=scales[0],
                                 name="attn_a_bwd")
    dqb, dkb, dvb = _attn_bwd(qb, kb, vb, yb, dob, lse_b, name="attn_b_bwd", scale=scales[1], **cfg_b)
    dqm, dmk, dmv = _attn_bwd(qm, mkv, mkv, ym, dom, lse_m, name="attn_m_bwd", scale=scales[2], **cfg_m)
    dmkv = jnp.concatenate([dmk, dmv], axis=1)

    dproj, dqf, dkv, dg_cq, dg_ckv = _prep_bwd(
        dqa, dka, dva, dqb, dkb, dvb, dqm, dga, dgb, dgm, proj, pos, w_uq_pad_t, w_ukv, g_cq, g_ckv, rope_a, rope_b)

    small_rows = (dg_cq, dg_ckv, loss, dg_a, dg_b, dg_m, dg_post, db_post)
    return (dproj, h, y, dz, dqf, cqn, ckvn, dkv, mem2, dmkv), x2, small_rows


def _weight_grads(operands, core):
    dproj, h, y, dz, dqf, cqn, ckvn, dkv, mem2, dmkv = operands
    dw_in_arr_t = _mm(dproj, h, F32, 1024, 1024, 1024, "dw_in", mode="tn")
    g_in = _split_in(dw_in_arr_t)
    dw_out, r_in = _mm(y, dz, F32, 1024, 1024, 1024, "dw_out", mode="tn",
                       ride=_half_to_sibling(g_in.reshape(4, 2, HALF_IN, 1024)))
    dw_uq_pad_t = _mm(dqf, cqn, F32, 1024, 256, 1024, "dw_uq", mode="tn")
    dw_ukv = _mm(ckvn, dkv, F32, 128, 1024, 1024, "dw_ukv", mode="tn")
    dw_mem = _mm(mem2, dmkv, F32, 1024, 1024, mem2.shape[0], "dw_mem", mode="tn")
    g_rest = _split_rest(dw_uq_pad_t, dw_ukv, dw_mem, dw_out)
    sf_in, sb_in, r_rest = _core_sum(g_in, r_in, core, HALF_IN, HALF_IN // 2, "core_sum_in",
                                     ride=_half_to_sibling(g_rest.reshape(4, 2, HALF_REST, 1024)))
    sf_rest, sb_rest = _core_sum(g_rest, r_rest, core, HALF_REST, HALF_REST, "core_sum_rest")
    return sf_in, sb_in, sf_rest, sb_rest


def _small_block(dg_emb, db_emb, small_rows):
    dg_cq, dg_ckv, loss, dg_a, dg_b, dg_m, dg_post, db_post = small_rows
    row2 = jnp.concatenate([dg_cq, dg_ckv, loss, jnp.zeros((1, 512), F32)], axis=1)
    return jnp.concatenate([dg_emb, db_emb, row2, dg_a, jnp.concatenate([dg_b, dg_m], axis=1), dg_post, db_post,
                            jnp.zeros((1, 1024), F32)], axis=0)


def _pack_small(g_emb, b_emb, g_cq, g_ckv, g_out_a, g_out_b, g_out_m, g_post, b_post):
    row2 = jnp.concatenate([g_cq.reshape(1, -1), g_ckv.reshape(1, -1), jnp.zeros((1, 640), F32)], axis=1)
    return jnp.concatenate([g_emb.reshape(1, -1), b_emb.reshape(1, -1), row2, g_out_a.reshape(1, -1),
                            jnp.concatenate([g_out_b.reshape(1, -1), g_out_m.reshape(1, -1)], axis=1),
                            g_post.reshape(1, -1), b_post.reshape(1, -1), jnp.zeros((1, 1024), F32)], axis=0)


def kernel(x, mem, positions, g_emb, b_emb, w_in, g_cq, g_ckv, w_uq, w_ukv, w_mem_kv, g_out_a, g_out_b, g_out_m, w_out, g_post, b_post, loss_target, m_g_emb, m_b_emb, m_w_in, m_g_cq, m_g_ckv, m_w_uq, m_w_ukv, m_w_mem_kv, m_g_out_a, m_g_out_b, m_g_out_m, m_w_out, m_g_post, m_b_post, v_g_emb, v_b_emb, v_w_in, v_g_cq, v_g_ckv, v_w_uq, v_w_ukv, v_w_mem_kv, v_g_out_a, v_g_out_b, v_g_out_m, v_w_out, v_g_post, v_b_post):
    w_rest = _pack_rest(w_uq, w_ukv, w_mem_kv, w_out)
    w_in_t = w_in[0].T
    w_in_b = jnp.pad(w_in_t.astype(BF16), ((0, ROWS_IN - SHARD_ROWS), (0, 0)))
    gains = (g_emb.reshape(1, -1), b_emb.reshape(1, -1), g_cq, g_ckv, g_out_a, g_out_b, g_out_m, g_post, b_post)
    h, gathered_in = _ln_fwd(x.reshape(-1, D_MODEL), gains[0], gains[1],
                             ride=_gather_ride(w_in_b.reshape(2, HALF_IN, 1024)))
    w_in_arr_t = _arranged_w_in(gathered_in.reshape(4, ROWS_IN, 1024))
    proj, gathered_rest = _mm(h, w_in_arr_t, F32, 1024, 1024, 1024, "in_proj", mode="nt",
                              ride=_gather_ride(w_rest.astype(BF16).reshape(2, HALF_REST, 1024)))
    weights = _rest_weights(gathered_rest.reshape(4, ROWS_REST, 1024))
    operands, x2, small_rows = _forward_backward(h, proj, x, mem, positions, loss_target, weights, gains)

    core = lax.axis_index("c").astype(jnp.int32).reshape(1)
    chip = (2 * lax.axis_index("x") + lax.axis_index("y")).astype(jnp.int32).reshape(1)
    sf_in, sb_in, sf_rest, sb_rest = _weight_grads(operands, core)
    grad_x, dg_emb, db_emb, rb_in, rb_rest = _dh_scatter(operands[0], w_in_arr_t, x2, operands[3], gains[0],
                                                         sb_in, sb_rest)
    gh_in = _chip_sum(sf_in, rb_in, chip, HALF_IN, HALF_IN // 2, "chip_sum_in")
    gh_rest = _chip_sum(sf_rest, rb_rest, chip, HALF_REST, HALF_REST, "chip_sum_rest")
    grad_in, grad_rest = _join_halves(gh_in, gh_rest)
    grad_in = grad_in.reshape(ROWS_IN, 1024)
    grad_rest = grad_rest.reshape(ROWS_REST, 1024)

    big_in = _adamw(grad_in, w_in_t, m_w_in[0].T, v_w_in[0].T, SHARD_ROWS // 3, "adamw_in")
    uq, ukv, wmem, wout = _adamw_pieces(
        grad_rest, w_rest, _pack_rest(m_w_uq, m_w_ukv, m_w_mem_kv, m_w_out),
        _pack_rest(v_w_uq, v_w_ukv, v_w_mem_kv, v_w_out), REST_PIECES, "adamw_rest")
    small_sum = _allreduce_small(_small_block(dg_emb, db_emb, small_rows))
    sm = _adamw_pieces(
        small_sum,
        _pack_small(g_emb, b_emb, g_cq, g_ckv, g_out_a, g_out_b, g_out_m, g_post, b_post),
        _pack_small(m_g_emb, m_b_emb, m_g_cq, m_g_ckv, m_g_out_a, m_g_out_b, m_g_out_m, m_g_post, m_b_post),
        _pack_small(v_g_emb, v_b_emb, v_g_cq, v_g_ckv, v_g_out_a, v_g_out_b, v_g_out_m, v_g_post, v_b_post),
        SMALL_PIECES, "adamw_small")
    loss = small_sum[2, 384]

    def ordered(kind):
        s_gemb, s_bemb, s_gcq, s_gckv, s_ga, s_gb, s_gm, s_gpost, s_bpost = [piece[kind] for piece in sm]
        return [s_gemb.reshape(-1), s_bemb.reshape(-1), big_in[kind].T[None], s_gcq, s_gckv,
                uq[kind].reshape(192, 256).T[None], ukv[kind].reshape(1, 128, 256), wmem[kind][None], s_ga, s_gb,
                s_gm, wout[kind][None], s_gpost, s_bpost]

    return (loss, grad_x.reshape(x.shape), *ordered(0), *ordered(1), *ordered(2), *ordered(3))
```

```python
import functools
import math

import jax
import jax.numpy as jnp
import numpy as np
from jax import lax
from jax.experimental import pallas as pl
from jax.experimental.pallas import tpu as pltpu

F32 = jnp.float32
BF16 = jnp.bfloat16
MESH = pl.DeviceIdType.MESH
ANY = pl.BlockSpec(memory_space=pl.ANY)
IN_VMEM = pl.BlockSpec(memory_space=pltpu.VMEM)

D_MODEL = 1024
A_WIDTH = 1024
MLA_HEADS = 8
MLA_Q_RANK = 256
MLA_KV_RANK = 128
MLA_QK_DIM = 96
MEM_WIDTH = 512
N_MEM = 256
ROPE_THETA = 500000.0
NORM_EPS = 1e-5
NEG_INF = -1e30
DEEPNORM_ALPHA = 2.0 ** 0.25
DILATED = ((64, 1), (256, 4), (1024, 16))

ADAM_LR = 0.001
ADAM_B1 = 0.9
ADAM_B2 = 0.999
ADAM_EPS = 1e-08
ADAM_WD = 0.01
ADAM_STEP = 10

LANES = 128
VMEM_LIMIT = 56 * 1024 * 1024
LOG2E = math.log2(math.e)
LN2 = math.log(2.0)

PROJ_W = 6144
COL_CQ = 4096
COL_BG = 4608
COL_MQ = 5120
COL_MG = 5632

SHARD_ROWS = 1512
ROWS_IN = 1536
ROWS_UQ, ROWS_UKV, ROWS_MEM, ROWS_OUT = 48, 32, 256, 512
ROWS_USED = ROWS_UQ + ROWS_UKV + ROWS_MEM + ROWS_OUT
ROWS_REST = 864
HALF_IN = ROWS_IN // 2
HALF_REST = ROWS_REST // 2
REST_PIECES = ((0, 48, 0, 1024), (48, 80, 0, 1024), (80, 336, 0, 1024), (336, 848, 0, 1024))
SMALL_PIECES = ((0, 1, 0, 1024), (1, 2, 0, 1024), (2, 3, 0, 256), (2, 3, 256, 384), (3, 4, 0, 1024), (4, 5, 0, 512),
                (4, 5, 512, 1024), (5, 6, 0, 1024), (6, 7, 0, 1024))


def _params(sem=None, vmem=VMEM_LIMIT):
    return pltpu.CompilerParams(dimension_semantics=sem, vmem_limit_bytes=vmem)


def _dot(a, b):
    return jnp.dot(a, b, preferred_element_type=F32)


def _dot_nt(a, b):
    return lax.dot_general(a, b, (((1,), (1,)), ((), ())), preferred_element_type=F32)


def _dot_tn(a, b):
    return lax.dot_general(a, b, (((0,), (0,)), ((), ())), preferred_element_type=F32)


def _ln_hat(x):
    mu = jnp.mean(x, axis=-1, keepdims=True)
    xc = x - mu
    var = jnp.mean(xc * xc, axis=-1, keepdims=True)
    rstd = lax.rsqrt(var + NORM_EPS)
    return xc * rstd, rstd


def _ln_bwd_rows(dxh, xh, rstd):
    return rstd * (dxh - jnp.mean(dxh, axis=-1, keepdims=True) - xh * jnp.mean(dxh * xh, axis=-1, keepdims=True))


def _rms_hat(x, width):
    ms = jnp.sum(x * x, axis=-1, keepdims=True) * (1.0 / width)
    r = lax.rsqrt(ms + NORM_EPS)
    return x * r, r


def _rms_bwd(u, xh, r, width):
    return r * (u - xh * (jnp.sum(u * xh, axis=-1, keepdims=True) * (1.0 / width)))


def _colsum(v):
    return jnp.sum(v, axis=0, keepdims=True)


def _rope_tables(cos, sin, consts):
    return cos, sin * consts[2:3, :], -sin * consts[1:2, :]


def _rope(x, tables, half, inverse=False):
    c, s_up, s_dn = tables
    if inverse:
        s_up, s_dn = -s_up, -s_dn
    return x * c + pltpu.roll(x, half, 1) * s_up + pltpu.roll(x, LANES - half, 1) * s_dn


def _ln_fwd(x, g, b, pos, rope_a, rope_b, tm=512, ride=None):
    t, d = x.shape
    n_in = len(ride.args) if ride else 0
    n_out = len(ride.out_shapes) if ride else 0
    steps = t // tm

    def body(x_ref, g_ref, b_ref, pos_ref, ra_ref, rb_ref, *rest):
        h_ref, trig_ref = rest[n_in], rest[n_in + 1]
        if ride:
            i = pl.program_id(0)
            ride.run(i == 0, i == steps - 1, rest[:n_in], rest[n_in + 2:n_in + 2 + n_out], rest[n_in + 2 + n_out:])
        xh, _ = _ln_hat(x_ref[...])
        h_ref[...] = (xh * g_ref[...] + b_ref[...]).astype(BF16)
        for j, consts in enumerate((ra_ref, rb_ref)):
            ang = pos_ref[...] * consts[0:1, :]
            trig_ref[:, 2 * j * LANES:(2 * j + 1) * LANES] = jnp.cos(ang)
            trig_ref[:, (2 * j + 1) * LANES:(2 * j + 2) * LANES] = jnp.sin(ang)

    row = pl.BlockSpec((1, d), lambda i: (0, 0))
    tile = pl.BlockSpec((tm, d), lambda i: (i, 0))
    consts = pl.BlockSpec((8, LANES), lambda i: (0, 0))
    trig_tile = pl.BlockSpec((tm, 4 * LANES), lambda i: (i, 0))
    in_specs = [tile, row, row, pl.BlockSpec((tm, 1), lambda i: (i, 0)), consts, consts]
    shapes = (jax.ShapeDtypeStruct((t, d), BF16), jax.ShapeDtypeStruct((t, 4 * LANES), F32))
    if not ride:
        return pl.pallas_call(
            body, name="ln_fwd", grid=(steps,), out_shape=shapes, in_specs=in_specs, out_specs=(tile, trig_tile),
            compiler_params=_params(("parallel",)),
        )(x, g, b, pos, rope_a, rope_b)
    return pl.pallas_call(
        body, name="ln_fwd", grid=(steps,),
        out_shape=(*shapes, *ride.out_shapes),
        in_specs=in_specs + ride.in_specs, out_specs=(tile, trig_tile) + (ANY,) * n_out,
        scratch_shapes=ride.scratch(),
        compiler_params=_params(("arbitrary",)),
    )(x, g, b, pos, rope_a, rope_b, *ride.args)


class _Ride:
    def __init__(self, args, out_shapes, sem_counts, plan, in_specs=None):
        self.args, self.out_shapes, self.plan = list(args), list(out_shapes), plan
        self.sem_counts = sem_counts
        self.in_specs = in_specs or [ANY] * len(self.args)

    def scratch(self):
        return [pltpu.SemaphoreType.DMA((n,)) for n in self.sem_counts]

    def run(self, first, last, in_refs, out_refs, sems, middle=None):
        def stage(k):
            stages = self.plan(in_refs, out_refs, *sems)
            if k == 0 or len(stages) == 3:
                return stages[k]
            return (lambda: None) if k == 1 else stages[1]

        @pl.when(first)
        def _():
            stage(0)()

        if middle is not None:
            @pl.when(middle)
            def _():
                stage(1)()

        @pl.when(last)
        def _():
            if middle is None:
                stage(1)()
            stage(2)()


def _mm(a, b, out_dtype, tm, tn, tk, name, mode="nn", ride=None):
    if mode == "tn":
        k, m = a.shape
    else:
        m, k = a.shape
    n = b.shape[0] if mode == "nt" else b.shape[1]
    nk = k // tk
    nj, ni = n // tn, m // tm
    n_in = len(ride.args) if ride else 0
    n_out = len(ride.out_shapes) if ride else 0

    def body(a_ref, b_ref, *rest):
        o_ref = rest[n_in]
        acc_ref = rest[n_in + 1 + n_out]
        if ride:
            j, i, kk = pl.program_id(0), pl.program_id(1), pl.program_id(2)
            step = (j * ni + i) * nk + kk
            total = nj * ni * nk
            ride.run(step == 0, step == total - 1, rest[:n_in], rest[n_in + 1:n_in + 1 + n_out],
                     rest[n_in + 2 + n_out:], middle=(step == (2 * total) // 3) if total >= 3 else None)
        av = a_ref[...].astype(BF16)
        bv = b_ref[...].astype(BF16)
        part = _dot_tn(av, bv) if mode == "tn" else _dot_nt(av, bv) if mode == "nt" else _dot(av, bv)
        if nk == 1:
            o_ref[...] = part.astype(out_dtype)
        else:
            kk = pl.program_id(2)

            @pl.when(kk == 0)
            def _():
                acc_ref[...] = part

            @pl.when(kk > 0)
            def _():
                acc_ref[...] += part

            @pl.when(kk == nk - 1)
            def _():
                o_ref[...] = acc_ref[...].astype(out_dtype)

    a_spec = (pl.BlockSpec((tk, tm), lambda j, i, kk: (kk, i)) if mode == "tn"
              else pl.BlockSpec((tm, tk), lambda j, i, kk: (i, kk)))
    b_spec = (pl.BlockSpec((tn, tk), lambda j, i, kk: (j, kk)) if mode == "nt"
              else pl.BlockSpec((tk, tn), lambda j, i, kk: (kk, j)))
    o_spec = pl.BlockSpec((tm, tn), lambda j, i, kk: (i, j))
    o_shape = jax.ShapeDtypeStruct((m, n), out_dtype)
    if not ride:
        return pl.pallas_call(
            body, name=name, grid=(nj, ni, nk), out_shape=o_shape, in_specs=[a_spec, b_spec], out_specs=o_spec,
            scratch_shapes=[pltpu.VMEM((tm, tn), F32)],
            compiler_params=_params(("parallel", "parallel", "arbitrary")),
        )(a, b)
    return pl.pallas_call(
        body, name=name, grid=(nj, ni, nk),
        out_shape=(o_shape, *ride.out_shapes),
        in_specs=[a_spec, b_spec] + ride.in_specs,
        out_specs=(o_spec,) + (ANY,) * n_out,
        scratch_shapes=[pltpu.VMEM((tm, tn), F32)] + ride.scratch(),
        compiler_params=_params(("arbitrary", "arbitrary", "arbitrary")),
    )(a, b, *ride.args)


def _prep(proj, trig, w_uq, w_ukv, g_cq, g_ckv, rope_a, rope_b, scales, tm=256):
    t = proj.shape[0]
    sc_a, sc_b, sc_m = (s * LOG2E for s in scales)

    def body(aq_ref, ak_ref, av_ref, bs_ref, mq_ref, trig_ref, wuq_ref, wukv_ref, gcq_ref, gckv_ref,
             ra_ref, rb_ref, qa_ref, ka_ref, va_ref, qb_ref, kb_ref, vb_ref, qm_ref, cqn_ref, ckvn_ref):
        ta = _rope_tables(trig_ref[:, 0:LANES], trig_ref[:, LANES:2 * LANES], ra_ref[...])
        tb = _rope_tables(trig_ref[:, 2 * LANES:3 * LANES], trig_ref[:, 3 * LANES:4 * LANES], rb_ref[...])
        for j in range(A_WIDTH // LANES):
            sl = slice(j * LANES, (j + 1) * LANES)
            qa_ref[:, sl] = (_rope(aq_ref[:, sl], ta, 8) * sc_a).astype(BF16)
            ka_ref[:, sl] = _rope(ak_ref[:, sl], ta, 8).astype(BF16)
        va_ref[...] = av_ref[...].astype(BF16)
        qm_ref[...] = (mq_ref[...] * sc_m).astype(BF16)

        cq_hat, _ = _rms_hat(bs_ref[:, 0:MLA_Q_RANK], MLA_Q_RANK)
        cqn = (cq_hat * gcq_ref[...]).astype(BF16)
        cqn_ref[...] = cqn
        ckv_hat, _ = _rms_hat(bs_ref[:, MLA_Q_RANK:MLA_Q_RANK + MLA_KV_RANK], MLA_KV_RANK)
        ckvn = (ckv_hat * gckv_ref[...]).astype(BF16)
        ckvn_ref[...] = ckvn
        qfull = _dot_nt(cqn, wuq_ref[...])
        kv = _dot(ckvn, wukv_ref[...])
        kr = _rope(bs_ref[:, 384:512], tb, 16)
        lane = lax.broadcasted_iota(jnp.int32, (1, LANES), 1)
        low = lane < 64
        for h in range(MLA_HEADS):
            sl = slice(h * LANES, (h + 1) * LANES)
            qb_ref[:, sl] = (_rope(qfull[:, sl], tb, 16) * sc_b).astype(BF16)
            kb_ref[:, sl] = jnp.where(low, kv[:, sl], kr).astype(BF16)
            vb_ref[:, sl] = jnp.where(low, 0.0, kv[:, sl]).astype(BF16)

    def col(width, idx):
        return pl.BlockSpec((tm, width), lambda i: (i, idx))

    def full(shape):
        return pl.BlockSpec(shape, lambda i: (0, 0))

    wide = jax.ShapeDtypeStruct((t, 1024), BF16)
    return pl.pallas_call(
        body, name="prep", grid=(t // tm,),
        out_shape=(wide, wide, wide, wide, wide, wide,
                   jax.ShapeDtypeStruct((t, MEM_WIDTH), BF16),
                   jax.ShapeDtypeStruct((t, MLA_Q_RANK), BF16),
                   jax.ShapeDtypeStruct((t, MLA_KV_RANK), BF16)),
        in_specs=[col(1024, 0), col(1024, 1), col(1024, 2), col(512, COL_CQ // 512), col(512, COL_MQ // 512),
                  pl.BlockSpec((tm, 4 * LANES), lambda i: (i, 0)),
                  full((1024, MLA_Q_RANK)), full((MLA_KV_RANK, 1024)),
                  full((1, MLA_Q_RANK)), full((1, MLA_KV_RANK)), full((8, LANES)), full((8, LANES))],
        out_specs=(col(1024, 0),) * 6 + (col(MEM_WIDTH, 0), col(MLA_Q_RANK, 0), col(MLA_KV_RANK, 0)),
        compiler_params=_params(("parallel",)),
    )(proj, proj, proj, proj, proj, trig, w_uq, w_ukv, g_cq, g_ckv, rope_a, rope_b)


def _attn_fwd(q, k, v, *, nb, s, sk, heads, hpb, voff, bq, name):
    nq = s // bq
    width = hpb * LANES
    vblk = voff // hpb

    def body(q_ref, k_ref, v_ref, o_ref, lse_ref):
        for h in range(hpb):
            sl = slice(h * LANES, (h + 1) * LANES)
            sc = _dot_nt(q_ref[:, sl], k_ref[:, sl])
            m = jnp.max(sc, axis=1, keepdims=True)
            p = jnp.exp2(sc - m)
            l = jnp.sum(p, axis=1, keepdims=True)
            o_ref[:, sl] = _dot(p.astype(BF16), v_ref[:, sl]) / l
            lse_ref[:, sl] = jnp.broadcast_to(m + jnp.log(l) * LOG2E, (bq, LANES))

    out = jax.ShapeDtypeStruct((nb * s, heads * LANES), F32)
    ospec = pl.BlockSpec((bq, width), lambda b, i, g: (b * nq + i, g))
    return pl.pallas_call(
        body, name=name, grid=(nb, nq, heads // hpb),
        out_shape=(out, out),
        in_specs=[ospec, pl.BlockSpec((sk, width), lambda b, i, g: (b, g)),
                  pl.BlockSpec((sk, width), lambda b, i, g: (b, vblk + g))],
        out_specs=(ospec, ospec),
        compiler_params=_params(("parallel", "parallel", "parallel")),
    )(q, k, v)


def _attn_bwd(q, k, v, o, do, lse, *, nb, s, sk, heads, hpb, voff, scale, bq, name):
    nq = s // bq
    width = hpb * LANES
    vblk = voff // hpb

    def body(q_ref, k_ref, v_ref, o_ref, do_ref, lse_ref, dq_ref, dk_ref, dv_ref, dk_acc, dv_acc):
        i = pl.program_id(2)

        @pl.when(i == 0)
        def _():
            dk_acc[...] = jnp.zeros_like(dk_acc)
            dv_acc[...] = jnp.zeros_like(dv_acc)

        for h in range(hpb):
            sl = slice(h * LANES, (h + 1) * LANES)
            qh = q_ref[:, sl]
            kk = k_ref[:, sl]
            doh = do_ref[:, sl]
            delta = jnp.sum(doh.astype(F32) * o_ref[:, sl], axis=1, keepdims=True)
            p = jnp.exp2(_dot_nt(qh, kk) - lse_ref[:, h * LANES:h * LANES + 1])
            ds = (p * (_dot_nt(doh, v_ref[:, sl]) - delta)).astype(BF16)
            dq_ref[:, sl] = (_dot(ds, kk) * scale).astype(BF16)
            dk_acc[:, sl] += _dot_tn(ds, qh)
            dv_acc[:, sl] += _dot_tn(p.astype(BF16), doh)

        @pl.when(i == nq - 1)
        def _():
            dk_ref[...] = (dk_acc[...] * LN2).astype(BF16)
            dv_ref[...] = dv_acc[...].astype(BF16)

    qspec = pl.BlockSpec((bq, width), lambda b, g, i: (b * nq + i, g))
    kv_spec = pl.BlockSpec((sk, width), lambda b, g, i: (b, g))
    dq_shape = jax.ShapeDtypeStruct((nb * s, heads * LANES), BF16)
    dkv_shape = jax.ShapeDtypeStruct((nb * sk, heads * LANES), BF16)
    return pl.pallas_call(
        body, name=name, grid=(nb, heads // hpb, nq),
        out_shape=(dq_shape, dkv_shape, dkv_shape),
        in_specs=[qspec, kv_spec, pl.BlockSpec((sk, width), lambda b, g, i: (b, vblk + g)), qspec, qspec, qspec],
        out_specs=(qspec, kv_spec, kv_spec),
        scratch_shapes=[pltpu.VMEM((sk, width), F32), pltpu.VMEM((sk, width), F32)],
        compiler_params=_params(("parallel", "parallel", "arbitrary")),
    )(q, k, v, o, do, lse)


BAND_Q = 128
BAND_WIN = 256


def _band_start(i, s):
    return min(max(i * BAND_Q - 64, 0), s - BAND_WIN)


def _to_pattern_order(src_ref, dst_ref, stage_ref, s, d):
    length = s // d
    stage_ref[...] = src_ref[...].astype(F32)
    for r in range(d):
        dst_ref[r * length:(r + 1) * length, :] = stage_ref[pl.ds(r, length, stride=d), :].astype(dst_ref.dtype)


def _dilated_fwd(q, k, v, bias, bias_index, *, nb, s, name):
    nblk = s // BAND_Q
    npat = len(DILATED)

    def body(q_ref, k_ref, v_ref, bias_ref, o_ref, lse_ref, stage_ref, qp_ref, kp_ref, vp_ref, op_ref, lp_ref,
             on_ref, ln_ref):
        lane = lax.broadcasted_iota(jnp.int32, (1, LANES), 1)
        first = lane < 64
        for p, (_, d) in enumerate(DILATED):
            if d == 1:
                qs, ks, vs = q_ref, k_ref, v_ref
            else:
                for src, dst in ((q_ref, qp_ref), (k_ref, kp_ref), (v_ref, vp_ref)):
                    _to_pattern_order(src, dst, stage_ref, s, d)
                qs, ks, vs = qp_ref, kp_ref, vp_ref
            for i in range(nblk):
                u0 = i * BAND_Q
                st = _band_start(i, s)
                qi = qs[u0:u0 + BAND_Q, :]
                kw = ks[st:st + BAND_WIN, :]
                vw = vs[st:st + BAND_WIN, :]
                zero = jnp.zeros_like(qi)
                q2 = jnp.concatenate([jnp.where(first, qi, zero), jnp.where(first, zero, qi)], axis=0)
                sc = _dot_nt(q2, kw)
                b = bias_ref[bias_index[p][i]]
                halves = []
                for h in range(2):
                    sh = sc[h * BAND_Q:(h + 1) * BAND_Q] + b
                    m = jnp.max(sh, axis=1, keepdims=True)
                    pr = jnp.exp2(sh - m)
                    l = jnp.sum(pr, axis=1, keepdims=True)
                    halves.append((pr.astype(BF16), l, m + jnp.log(l) * LOG2E))
                o2 = _dot(jnp.concatenate([halves[0][0], halves[1][0]], axis=0), vw)
                o_blk = jnp.where(first, o2[:BAND_Q] / halves[0][1], o2[BAND_Q:] / halves[1][1])
                lse_blk = jnp.where(first, jnp.broadcast_to(halves[0][2], (BAND_Q, LANES)),
                                    jnp.broadcast_to(halves[1][2], (BAND_Q, LANES)))
                op_ref[p, u0:u0 + BAND_Q, :] = o_blk
                lp_ref[p, u0:u0 + BAND_Q, :] = lse_blk
            if d > 1:
                length = s // d
                for r in range(d):
                    on_ref.at[p - 1][pl.ds(r, length, stride=d), :] = op_ref[p, r * length:(r + 1) * length, :]
                    ln_ref.at[p - 1][pl.ds(r, length, stride=d), :] = lp_ref[p, r * length:(r + 1) * length, :]
        lses = [lp_ref[0]] + [ln_ref[p] for p in range(npat - 1)]
        outs = [op_ref[0]] + [on_ref[p] for p in range(npat - 1)]
        m = functools.reduce(jnp.maximum, lses)
        ws = [jnp.exp2(l - m) for l in lses]
        den = functools.reduce(lambda a, c: a + c, ws)
        o_ref[...] = functools.reduce(lambda a, c: a + c, [w * o for w, o in zip(ws, outs)]) / den
        lse_ref[...] = m + jnp.log(den) * LOG2E

    blk = pl.BlockSpec((s, LANES), lambda b, g: (b, g))
    out = jax.ShapeDtypeStruct((nb * s, A_WIDTH), F32)
    return pl.pallas_call(
        body, name=name, grid=(nb, A_WIDTH // LANES),
        out_shape=(out, out),
        in_specs=[blk, blk, blk, pl.BlockSpec(bias.shape, lambda b, g: (0, 0, 0))],
        out_specs=(blk, blk),
        scratch_shapes=[pltpu.VMEM((s, LANES), F32), pltpu.VMEM((s, LANES), BF16), pltpu.VMEM((s, LANES), BF16),
                        pltpu.VMEM((s, LANES), BF16), pltpu.VMEM((npat, s, LANES), F32),
                        pltpu.VMEM((npat, s, LANES), F32), pltpu.VMEM((npat - 1, s, LANES), F32),
                        pltpu.VMEM((npat - 1, s, LANES), F32)],
        compiler_params=_params(("parallel", "parallel")),
    )(q, k, v, bias)


def _dilated_bwd(q, k, v, o, do, lse, bias, bias_index, *, nb, s, scale, name):
    nblk = s // BAND_Q
    npat = len(DILATED)

    def body(q_ref, k_ref, v_ref, o_ref, do_ref, lse_ref, bias_ref, dq_out, dk_out, dv_out,
             stage_ref, dl_ref, qp_ref, kp_ref, vp_ref, dop_ref, lsp_ref, dlp_ref, dqp_ref, dkp_ref, dvp_ref,
             dq_ref, dk_ref, dv_ref):
        lane = lax.broadcasted_iota(jnp.int32, (1, LANES), 1)
        first = lane < 64
        prod = do_ref[...].astype(F32) * o_ref[...]
        d0 = jnp.sum(jnp.where(first, prod, 0.0), axis=1, keepdims=True)
        d1 = jnp.sum(jnp.where(first, 0.0, prod), axis=1, keepdims=True)
        dl_ref[...] = jnp.where(first, jnp.broadcast_to(d0, (s, LANES)), jnp.broadcast_to(d1, (s, LANES)))
        for p, (_, d) in enumerate(DILATED):
            length = s // d
            if d == 1:
                qs, ks, vs, dos, lss, dls = q_ref, k_ref, v_ref, do_ref, lse_ref, dl_ref
                dqs, dks, dvs = dq_ref, dk_ref, dv_ref
            else:
                for src, dst in ((q_ref, qp_ref), (k_ref, kp_ref), (v_ref, vp_ref), (do_ref, dop_ref),
                                 (lse_ref, lsp_ref), (dl_ref, dlp_ref)):
                    _to_pattern_order(src, dst, stage_ref, s, d)
                qs, ks, vs, dos, lss, dls = qp_ref, kp_ref, vp_ref, dop_ref, lsp_ref, dlp_ref
                dqs, dks, dvs = dqp_ref, dkp_ref, dvp_ref
            dks[...] = jnp.zeros((s, LANES), F32)
            dvs[...] = jnp.zeros((s, LANES), F32)
            for i in range(nblk):
                u0 = i * BAND_Q
                st = _band_start(i, s)
                qi = qs[u0:u0 + BAND_Q, :]
                doi = dos[u0:u0 + BAND_Q, :]
                kw = ks[st:st + BAND_WIN, :]
                vw = vs[st:st + BAND_WIN, :]
                zero = jnp.zeros_like(qi)
                q2 = jnp.concatenate([jnp.where(first, qi, zero), jnp.where(first, zero, qi)], axis=0)
                do2 = jnp.concatenate([jnp.where(first, doi, zero), jnp.where(first, zero, doi)], axis=0)
                sc = _dot_nt(q2, kw)
                dp = _dot_nt(do2, vw)
                b = bias_ref[bias_index[p][i]]
                lse_i = lss[u0:u0 + BAND_Q, :]
                dl_i = dls[u0:u0 + BAND_Q, :]
                ps, dss = [], []
                for h in range(2):
                    rows = slice(h * BAND_Q, (h + 1) * BAND_Q)
                    pr = jnp.exp2(sc[rows] + b - lse_i[:, 64 * h:64 * h + 1])
                    ps.append(pr.astype(BF16))
                    dss.append((pr * (dp[rows] - dl_i[:, 64 * h:64 * h + 1])).astype(BF16))
                p2 = jnp.concatenate(ps, axis=0)
                ds2 = jnp.concatenate(dss, axis=0)
                dq2 = _dot(ds2, kw)
                dqs[u0:u0 + BAND_Q, :] = jnp.where(first, dq2[:BAND_Q], dq2[BAND_Q:]) * scale
                dks[st:st + BAND_WIN, :] += _dot_tn(ds2, q2)
                dvs[st:st + BAND_WIN, :] += _dot_tn(p2, do2)
            if d > 1:
                for dst, src in ((dq_ref, dqp_ref), (dk_ref, dkp_ref), (dv_ref, dvp_ref)):
                    for r in range(d):
                        dst[pl.ds(r, length, stride=d), :] += src[r * length:(r + 1) * length, :]
        dq_out[...] = dq_ref[...].astype(BF16)
        dk_out[...] = (dk_ref[...] * LN2).astype(BF16)
        dv_out[...] = dv_ref[...].astype(BF16)

    blk = pl.BlockSpec((s, LANES), lambda b, g: (b, g))
    out = jax.ShapeDtypeStruct((nb * s, A_WIDTH), BF16)
    f32_buf = pltpu.VMEM((s, LANES), F32)
    bf_buf = pltpu.VMEM((s, LANES), BF16)
    return pl.pallas_call(
        body, name=name, grid=(nb, A_WIDTH // LANES),
        out_shape=(out, out, out),
        in_specs=[blk] * 6 + [pl.BlockSpec(bias.shape, lambda b, g: (0, 0, 0))],
        out_specs=(blk, blk, blk),
        scratch_shapes=[f32_buf, f32_buf, bf_buf, bf_buf, bf_buf, bf_buf] + [f32_buf] * 8,
        compiler_params=_params(("parallel", "parallel")),
    )(q, k, v, o, do, lse, bias)


def _post(x, ya, ybp, ym, proj, target, w_out, g_emb, b_emb, g_a, g_b, g_m, g_post, b_post, tm=256):
    t = x.shape[0]

    def body(x_ref, ya_ref, yb_ref, ym_ref, ga_ref, gb_ref, gm_ref, tg_ref, wo_ref,
             ge_ref, be_ref, goa_ref, gob_ref, gom_ref, gp_ref, bp_ref,
             y_ref, dz_ref, doa_ref, dob_ref, dom_ref, dga_ref, dgb_ref, dgm_ref,
             loss_ref, dgp_ref, dbp_ref, dgoa_ref, dgob_ref, dgom_ref):
        i = pl.program_id(0)

        @pl.when(i == 0)
        def _():
            for r in (loss_ref, dgp_ref, dbp_ref, dgoa_ref, dgob_ref, dgom_ref):
                r[...] = jnp.zeros_like(r)

        lane = lax.broadcasted_iota(jnp.int32, (1, LANES), 1)
        low = lane < 64
        xh0, _ = _ln_hat(x_ref[...])
        h = xh0 * ge_ref[...] + be_ref[...]

        ybp_v = yb_ref[...]
        yb = jnp.concatenate(
            [jnp.where(low, pltpu.roll(ybp_v[:, 2 * j * LANES:(2 * j + 1) * LANES], 64, 1),
                       ybp_v[:, (2 * j + 1) * LANES:(2 * j + 2) * LANES]) for j in range(4)], axis=1)

        def gated(raw, gate, gain, width):
            xh, r = _rms_hat(raw, width)
            n = xh * gain
            sg = 1.0 / (1.0 + jnp.exp(-gate))
            return xh, r, n, sg, n * (gate * sg)

        gate_a, gate_b, gate_m = ga_ref[...], gb_ref[...], gm_ref[...]
        xh_a, r_a, n_a, sg_a, y_a = gated(ya_ref[...], gate_a, goa_ref[...], A_WIDTH)
        xh_b, r_b, n_b, sg_b, y_b = gated(yb, gate_b, gob_ref[...], 512)
        xh_m, r_m, n_m, sg_m, y_m = gated(ym_ref[...], gate_m, gom_ref[...], 512)
        y = jnp.concatenate([y_a, y_b, y_m], axis=1).astype(BF16)
        y_ref[...] = y
        z = DEEPNORM_ALPHA * h + _dot(y, wo_ref[...])
        zh, rstd = _ln_hat(z)
        err = zh * gp_ref[...] + bp_ref[...] - tg_ref[...]
        rows = jnp.sum(err * err, axis=1, keepdims=True)
        loss_ref[...] += jnp.broadcast_to(jnp.sum(rows, axis=0, keepdims=True) * (0.5 / D_MODEL), (1, LANES))
        dout = err * (1.0 / D_MODEL)
        dgp_ref[...] += _colsum(dout * zh)
        dbp_ref[...] += _colsum(dout)
        dz = _ln_bwd_rows(dout * gp_ref[...], zh, rstd)
        dz_ref[...] = dz
        dy = _dot_nt(dz.astype(BF16), wo_ref[...])

        def gated_bwd(dyg, xh, r, n, sg, gate, gain, width, dgain_ref):
            dn = dyg * (gate * sg)
            dgate = dyg * n * (sg * (1.0 + gate * (1.0 - sg)))
            dgain_ref[...] += _colsum(dn * xh)
            return _rms_bwd(dn * gain, xh, r, width), dgate

        dya, dgate_a = gated_bwd(dy[:, 0:1024], xh_a, r_a, n_a, sg_a, gate_a, goa_ref[...], A_WIDTH, dgoa_ref)
        dyb, dgate_b = gated_bwd(dy[:, 1024:1536], xh_b, r_b, n_b, sg_b, gate_b, gob_ref[...], 512, dgob_ref)
        dym, dgate_m = gated_bwd(dy[:, 1536:2048], xh_m, r_m, n_m, sg_m, gate_m, gom_ref[...], 512, dgom_ref)
        doa_ref[...] = dya.astype(BF16)
        dom_ref[...] = dym.astype(BF16)
        dga_ref[...] = dgate_a.astype(BF16)
        dgb_ref[...] = dgate_b.astype(BF16)
        dgm_ref[...] = dgate_m.astype(BF16)
        for j in range(4):
            blk = dyb[:, j * LANES:(j + 1) * LANES]
            dob_ref[:, 2 * j * LANES:(2 * j + 1) * LANES] = jnp.where(low, 0.0, pltpu.roll(blk, 64, 1)).astype(BF16)
            dob_ref[:, (2 * j + 1) * LANES:(2 * j + 2) * LANES] = jnp.where(low, 0.0, blk).astype(BF16)

    def col(width, idx):
        return pl.BlockSpec((tm, width), lambda i: (i, idx))

    def full(shape):
        return pl.BlockSpec(shape, lambda i: (0, 0))

    def acc(width):
        return jax.ShapeDtypeStruct((1, width), F32)

    return pl.pallas_call(
        body, name="post", grid=(t // tm,),
        out_shape=(jax.ShapeDtypeStruct((t, 2048), BF16), jax.ShapeDtypeStruct((t, 1024), F32),
                   jax.ShapeDtypeStruct((t, 1024), BF16), jax.ShapeDtypeStruct((t, 1024), BF16),
                   jax.ShapeDtypeStruct((t, 512), BF16),
                   jax.ShapeDtypeStruct((t, 1024), BF16), jax.ShapeDtypeStruct((t, 512), BF16),
                   jax.ShapeDtypeStruct((t, 512), BF16),
                   acc(LANES), acc(1024), acc(1024), acc(1024), acc(512), acc(512)),
        in_specs=[col(1024, 0), col(1024, 0), col(1024, 0), col(512, 0),
                  col(1024, 3), col(512, COL_BG // 512), col(512, COL_MG // 512), col(1024, 0),
                  full((2048, 1024)),
                  full((1, 1024)), full((1, 1024)), full((1, 1024)), full((1, 512)), full((1, 512)),
                  full((1, 1024)), full((1, 1024))],
        out_specs=(col(2048, 0), col(1024, 0), col(1024, 0), col(1024, 0), col(512, 0),
                   col(1024, 0), col(512, 0), col(512, 0),
                   full((1, LANES)), full((1, 1024)), full((1, 1024)), full((1, 1024)), full((1, 512)),
                   full((1, 512))),
        compiler_params=_params(("arbitrary",)),
    )(x, ya, ybp, ym, proj, proj, proj, target, w_out, g_emb, b_emb, g_a, g_b, g_m, g_post, b_post)


def _prep_bwd(dqa, dka, dva, dqb, dkb, dvb, dqm, dga, dgb, dgm, proj, trig, w_uq, w_ukv, g_cq, g_ckv,
              rope_a, rope_b, tm=256):
    t = proj.shape[0]

    def body(dqa_ref, dka_ref, dva_ref, dqb_ref, dkb_ref, dvb_ref, dqm_ref, dga_ref, dgb_ref, dgm_ref,
             bs_ref, trig_ref, wuq_ref, wukv_ref, gcq_ref, gckv_ref, ra_ref, rb_ref,
             dproj_ref, dqf_ref, dkv_ref, dgcq_ref, dgckv_ref):
        i = pl.program_id(0)

        @pl.when(i == 0)
        def _():
            dgcq_ref[...] = jnp.zeros_like(dgcq_ref)
            dgckv_ref[...] = jnp.zeros_like(dgckv_ref)

        ta = _rope_tables(trig_ref[:, 0:LANES], trig_ref[:, LANES:2 * LANES], ra_ref[...])
        tb = _rope_tables(trig_ref[:, 2 * LANES:3 * LANES], trig_ref[:, 3 * LANES:4 * LANES], rb_ref[...])
        for j in range(A_WIDTH // LANES):
            sl = slice(j * LANES, (j + 1) * LANES)
            dproj_ref[:, j * LANES:(j + 1) * LANES] = (
                _rope(dqa_ref[:, sl].astype(F32), ta, 8, inverse=True).astype(BF16))
            dproj_ref[:, 1024 + j * LANES:1024 + (j + 1) * LANES] = (
                _rope(dka_ref[:, sl].astype(F32), ta, 8, inverse=True).astype(BF16))
        dproj_ref[:, 2048:3072] = dva_ref[...]
        dproj_ref[:, 3072:4096] = dga_ref[...]

        lane = lax.broadcasted_iota(jnp.int32, (1, LANES), 1)
        low = lane < 64
        rope_lanes = (lane >= 64) & (lane < 96)
        dkr = jnp.zeros((tm, LANES), F32)
        for h in range(MLA_HEADS):
            sl = slice(h * LANES, (h + 1) * LANES)
            dqf_ref[:, sl] = _rope(dqb_ref[:, sl].astype(F32), tb, 16, inverse=True).astype(BF16)
            dk_h = dkb_ref[:, sl]
            dkv_ref[:, sl] = jnp.where(low, dk_h, dvb_ref[:, sl])
            dkr = dkr + jnp.where(rope_lanes, dk_h.astype(F32), 0.0)
        dkr = _rope(dkr, tb, 16, inverse=True)

        cq_hat, r_q = _rms_hat(bs_ref[:, 0:MLA_Q_RANK], MLA_Q_RANK)
        dcqn = _dot(dqf_ref[...], wuq_ref[...])
        dgcq_ref[...] += _colsum(dcqn * cq_hat)
        dproj_ref[:, COL_CQ:COL_CQ + 256] = _rms_bwd(dcqn * gcq_ref[...], cq_hat, r_q, MLA_Q_RANK).astype(BF16)
        ckv_hat, r_kv = _rms_hat(bs_ref[:, MLA_Q_RANK:MLA_Q_RANK + MLA_KV_RANK], MLA_KV_RANK)
        dckvn = _dot_nt(dkv_ref[...], wukv_ref[...])
        dgckv_ref[...] += _colsum(dckvn * ckv_hat)
        dproj_ref[:, COL_CQ + 256:COL_CQ + 384] = (
            _rms_bwd(dckvn * gckv_ref[...], ckv_hat, r_kv, MLA_KV_RANK).astype(BF16))
        dproj_ref[:, COL_CQ + 384:COL_CQ + 512] = dkr.astype(BF16)
        dproj_ref[:, COL_BG:COL_BG + 512] = dgb_ref[...]
        dproj_ref[:, COL_MQ:COL_MQ + 512] = dqm_ref[...]
        dproj_ref[:, COL_MG:COL_MG + 512] = dgm_ref[...]

    def col(width, idx):
        return pl.BlockSpec((tm, width), lambda i: (i, idx))

    def full(shape):
        return pl.BlockSpec(shape, lambda i: (0, 0))

    return pl.pallas_call(
        body, name="prep_bwd", grid=(t // tm,),
        out_shape=(jax.ShapeDtypeStruct((t, PROJ_W), BF16), jax.ShapeDtypeStruct((t, 1024), BF16),
                   jax.ShapeDtypeStruct((t, 1024), BF16),
                   jax.ShapeDtypeStruct((1, MLA_Q_RANK), F32), jax.ShapeDtypeStruct((1, MLA_KV_RANK), F32)),
        in_specs=[col(1024, 0)] * 6 + [col(512, 0), col(1024, 0), col(512, 0), col(512, 0),
                  col(512, COL_CQ // 512), pl.BlockSpec((tm, 4 * LANES), lambda i: (i, 0)),
                  full((1024, MLA_Q_RANK)), full((MLA_KV_RANK, 1024)),
                  full((1, MLA_Q_RANK)), full((1, MLA_KV_RANK)), full((8, LANES)), full((8, LANES))],
        out_specs=(col(PROJ_W, 0), col(1024, 0), col(1024, 0), full((1, MLA_Q_RANK)), full((1, MLA_KV_RANK))),
        compiler_params=_params(("arbitrary",)),
    )(dqa, dka, dva, dqb, dkb, dvb, dqm, dga, dgb, dgm, proj, trig, w_uq, w_ukv, g_cq, g_ckv, rope_a, rope_b)


def _adamw_math(gv, w, m, v):
    m_new = ADAM_B1 * m + (1.0 - ADAM_B1) * gv
    v_new = ADAM_B2 * v + (1.0 - ADAM_B2) * (gv * gv)
    m_hat = m_new / (1.0 - ADAM_B1 ** ADAM_STEP)
    v_hat = v_new / (1.0 - ADAM_B2 ** ADAM_STEP)
    return -ADAM_LR * (m_hat / (jnp.sqrt(v_hat) + ADAM_EPS) + ADAM_WD * w), m_new, v_new


def _adamw(g, w, m, v, tr, name):
    r, cols = w.shape

    def body(g_ref, w_ref, m_ref, v_ref, go_ref, d_ref, nm_ref, nv_ref):
        gv = g_ref[...]
        go_ref[...] = gv
        d_ref[...], nm_ref[...], nv_ref[...] = _adamw_math(gv, w_ref[...], m_ref[...], v_ref[...])

    tile = pl.BlockSpec((tr, cols), lambda i: (i, 0))
    shape = jax.ShapeDtypeStruct((r, cols), F32)
    return pl.pallas_call(
        body, name=name, grid=(r // tr,),
        out_shape=(shape,) * 4, in_specs=[tile] * 4, out_specs=(tile,) * 4,
        compiler_params=_params(("parallel",)),
    )(g, w, m, v)


def _adamw_pieces(g, w, m, v, pieces, name):
    shapes = [jax.ShapeDtypeStruct((r1 - r0, c1 - c0), F32) for r0, r1, c0, c1 in pieces]

    def body(g_ref, w_ref, m_ref, v_ref, *outs):
        gv = g_ref[...]
        results = (gv,) + _adamw_math(gv, w_ref[...], m_ref[...], v_ref[...])
        for kind, full in enumerate(results):
            for p, (r0, r1, c0, c1) in enumerate(pieces):
                outs[kind * len(pieces) + p][...] = full[r0:r1, c0:c1]

    flat = pl.pallas_call(
        body, name=name, out_shape=tuple(shapes) * 4,
        in_specs=[IN_VMEM] * 4, out_specs=tuple([IN_VMEM] * (4 * len(pieces))),
        compiler_params=_params(None),
    )(g, w, m, v)
    return [[flat[kind * len(pieces) + p] for kind in range(4)] for p in range(len(pieces))]


def _core_sum(g, recv, core, rows, tr, name, ride=None):
    cols = g.shape[2]
    nblk = rows // tr
    n_in = len(ride.args) if ride else 0
    n_out = len(ride.out_shapes) if ride else 0

    def body(c_ref, g_ref, r_ref, *rest):
        sf_ref, sb_ref = rest[n_in], rest[n_in + 1]
        if ride:
            j, i = pl.program_id(0), pl.program_id(1)
            ride.run((j == 0) & (i == 0), (j == 3) & (i == nblk - 1), rest[:n_in],
                     rest[n_in + 2:n_in + 2 + n_out], rest[n_in + 2 + n_out:])
        tot = g_ref[...] + r_ref[...]
        sf_ref[...] = tot
        sb_ref[...] = tot.astype(BF16)

    half = pl.BlockSpec((None, tr, cols), lambda j, i, c_ref: (j, i, 0))
    shapes = (jax.ShapeDtypeStruct((4, rows, cols), F32), jax.ShapeDtypeStruct((4, rows, cols), BF16))
    return pl.pallas_call(
        body, name=name,
        grid_spec=pltpu.PrefetchScalarGridSpec(
            num_scalar_prefetch=1, grid=(4, nblk),
            in_specs=[pl.BlockSpec((None, tr, cols), lambda j, i, c_ref: (j, c_ref[0] * nblk + i, 0)), half]
            + (ride.in_specs if ride else []),
            out_specs=(half, half) + (ANY,) * n_out,
            scratch_shapes=ride.scratch() if ride else []),
        out_shape=shapes + tuple(ride.out_shapes if ride else ()),
        compiler_params=_params(("arbitrary", "arbitrary") if ride else ("parallel", "parallel")),
    )(core, g, recv, *(ride.args if ride else ()))


def _half_to_sibling(g4):
    def plan(in_refs, out_refs, send_sems, recv_sems):
        x, y, c = _position()
        cp = pltpu.make_async_remote_copy(
            src_ref=in_refs[0].at[:, 1 - c], dst_ref=out_refs[0], send_sem=send_sems.at[0],
            recv_sem=recv_sems.at[0], device_id=(x, y, 1 - c), device_id_type=MESH)

        def finish():
            cp.wait_recv()
            cp.wait_send()

        return cp.start, finish

    return _Ride([g4], [jax.ShapeDtypeStruct((4, g4.shape[2], 1024), F32)], (1, 1), plan)


def _gather_plan(src_ref, dst_ref, send_sems, recv_sems, local_sems):
    x, y, c = _position()
    me = 2 * x + y
    local = pltpu.make_async_copy(src_ref, dst_ref.at[me], local_sems.at[0])

    def over_ici(k, src, chip):
        return pltpu.make_async_remote_copy(
            src_ref=src, dst_ref=dst_ref.at[chip, c], send_sem=send_sems.at[k - 1], recv_sem=recv_sems.at[k - 1],
            device_id=(x ^ (k >> 1), y ^ (k & 1), c), device_id_type=MESH)

    def to_sibling(k, half):
        piece = dst_ref.at[me ^ k, half]
        return pltpu.make_async_remote_copy(
            src_ref=piece, dst_ref=piece, send_sem=send_sems.at[2 + k], recv_sem=recv_sems.at[2 + k],
            device_id=(x, y, 1 - c), device_id_type=MESH)

    sends = [over_ici(k, src_ref.at[c], me) for k in (1, 2, 3)]

    def start():
        local.start()
        for cp in sends:
            cp.start()

    def relay():
        for k in (1, 2, 3):
            over_ici(k, dst_ref.at[me ^ k, c], me ^ k).wait_recv()
            to_sibling(k, c).start()

    def finish():
        for k in (1, 2, 3):
            to_sibling(k, 1 - c).wait_recv()
        for cp in sends + [to_sibling(k, c) for k in (1, 2, 3)]:
            cp.wait_send()
        local.wait()

    return start, relay, finish


def _gather_ride(shard):
    def plan(in_refs, out_refs, send_sems, recv_sems, local_sems):
        return _gather_plan(in_refs[0], out_refs[0], send_sems, recv_sems, local_sems)

    return _Ride([shard], [jax.ShapeDtypeStruct((4,) + shard.shape, shard.dtype)], (6, 6, 1), plan,
                 in_specs=[IN_VMEM])


def _chip_sum(sf, recv, chip, rows, tr, name):
    cols = sf.shape[2]

    def body(me_ref, sf_ref, r_ref, out_ref):
        acc = sf_ref[...]
        for k in range(3):
            acc = acc + r_ref[k].astype(F32)
        out_ref[...] = acc

    return pl.pallas_call(
        body, name=name,
        grid_spec=pltpu.PrefetchScalarGridSpec(
            num_scalar_prefetch=1, grid=(rows // tr,),
            in_specs=[pl.BlockSpec((None, tr, cols), lambda i, me_ref: (me_ref[0], i, 0)),
                      pl.BlockSpec((3, tr, cols), lambda i, me_ref: (0, i, 0))],
            out_specs=pl.BlockSpec((tr, cols), lambda i, me_ref: (i, 0))),
        out_shape=jax.ShapeDtypeStruct((rows, cols), F32),
        compiler_params=_params(("parallel",)),
    )(chip, sf, recv)


def _position():
    return lax.axis_index("x"), lax.axis_index("y"), lax.axis_index("c")


def _dh_scatter(dproj, w_in_arr_t, x, dz, g, sb_in, sb_rest, tm=1024, tk=1024):
    t, d = x.shape
    nk = dproj.shape[1] // tk
    ni = t // tm

    def body(dp_ref, w_ref, x_ref, dz_ref, g_ref, sbin_ref, sbrest_ref,
             dx_ref, dg_ref, db_ref, rin_ref, rrest_ref, acc_ref, send_sems, recv_sems):
        i = pl.program_id(0)
        kk = pl.program_id(1)
        px, py, pc = _position()
        me = 2 * px + py
        srcs = (sbin_ref, sbrest_ref)
        dsts = (rin_ref, rrest_ref)

        def copy(a, k):
            return pltpu.make_async_remote_copy(
                src_ref=srcs[a].at[me ^ k], dst_ref=dsts[a].at[k - 1],
                send_sem=send_sems.at[3 * a + k - 1], recv_sem=recv_sems.at[3 * a + k - 1],
                device_id=(px ^ (k >> 1), py ^ (k & 1), pc), device_id_type=MESH)

        pairs = [(a, k) for a in range(2) for k in (1, 2, 3)]

        @pl.when((i == 0) & (kk == 0))
        def _():
            dg_ref[...] = jnp.zeros_like(dg_ref)
            db_ref[...] = jnp.zeros_like(db_ref)
            for a, k in pairs:
                copy(a, k).start()

        part = _dot(dp_ref[...], w_ref[...])

        @pl.when(kk == 0)
        def _():
            acc_ref[...] = part

        @pl.when(kk > 0)
        def _():
            acc_ref[...] += part

        @pl.when(kk == nk - 1)
        def _():
            xh, rstd = _ln_hat(x_ref[...])
            dht = acc_ref[...] + DEEPNORM_ALPHA * dz_ref[...]
            dg_ref[...] += _colsum(dht * xh)
            db_ref[...] += _colsum(dht)
            dx_ref[...] = _ln_bwd_rows(dht * g_ref[...], xh, rstd)

        @pl.when((i == ni - 1) & (kk == nk - 1))
        def _():
            for a, k in pairs:
                copy(a, k).wait_recv()
            for a, k in pairs:
                copy(a, k).wait_send()

    tile = pl.BlockSpec((tm, d), lambda i, kk: (i, 0))
    row = pl.BlockSpec((1, d), lambda i, kk: (0, 0))
    return pl.pallas_call(
        body, name="dh_scatter", grid=(ni, nk),
        out_shape=(jax.ShapeDtypeStruct((t, d), F32), jax.ShapeDtypeStruct((1, d), F32),
                   jax.ShapeDtypeStruct((1, d), F32),
                   jax.ShapeDtypeStruct((3, HALF_IN, 1024), BF16),
                   jax.ShapeDtypeStruct((3, HALF_REST, 1024), BF16)),
        in_specs=[pl.BlockSpec((tm, tk), lambda i, kk: (i, kk)), pl.BlockSpec((tk, d), lambda i, kk: (kk, 0)),
                  tile, tile, row, ANY, ANY],
        out_specs=(tile, row, row, ANY, ANY),
        scratch_shapes=[pltpu.VMEM((tm, d), F32), pltpu.SemaphoreType.DMA((6,)), pltpu.SemaphoreType.DMA((6,))],
        compiler_params=_params(("arbitrary", "arbitrary")),
    )(dproj, w_in_arr_t, x, dz, g, sb_in, sb_rest)


def _join_halves(gh_in, gh_rest):
    def body(hin_ref, hrest_ref, oin_ref, orest_ref, send_sems, recv_sems, local_sems):
        x, y, c = _position()
        srcs = (hin_ref, hrest_ref)
        dsts = (oin_ref, orest_ref)

        def rows(a, half):
            return dsts[a].at[half]

        local = [pltpu.make_async_copy(srcs[a], rows(a, c), local_sems.at[a]) for a in range(2)]
        remote = [pltpu.make_async_remote_copy(
            src_ref=srcs[a], dst_ref=rows(a, c), send_sem=send_sems.at[a], recv_sem=recv_sems.at[a],
            device_id=(x, y, 1 - c), device_id_type=MESH) for a in range(2)]
        for cp in local + remote:
            cp.start()
        for a in range(2):
            pltpu.make_async_remote_copy(
                src_ref=srcs[a], dst_ref=rows(a, 1 - c), send_sem=send_sems.at[a], recv_sem=recv_sems.at[a],
                device_id=(x, y, 1 - c), device_id_type=MESH).wait_recv()
        for cp in remote:
            cp.wait_send()
        for cp in local:
            cp.wait()

    return pl.pallas_call(
        body, name="join_halves",
        out_shape=(jax.ShapeDtypeStruct((2, HALF_IN, 1024), F32),
                   jax.ShapeDtypeStruct((2, HALF_REST, 1024), F32)),
        in_specs=[IN_VMEM, IN_VMEM], out_specs=(ANY, ANY),
        scratch_shapes=[pltpu.SemaphoreType.DMA((2,)), pltpu.SemaphoreType.DMA((2,)), pltpu.SemaphoreType.DMA((2,))],
    )(gh_in, gh_rest)


def _allreduce_small(vec):
    def body(vec_ref, out_ref, all_ref, send_sems, recv_sems):
        x, y, c = _position()
        me = 4 * x + 2 * y + c
        all_ref[me] = vec_ref[...]

        def copy(k, slot):
            return pltpu.make_async_remote_copy(
                src_ref=vec_ref, dst_ref=all_ref.at[slot], send_sem=send_sems.at[k - 1], recv_sem=recv_sems.at[k - 1],
                device_id=(x ^ (k >> 2), y ^ ((k >> 1) & 1), c ^ (k & 1)), device_id_type=MESH)

        copies = [copy(k, me) for k in range(1, 8)]
        for cp in copies:
            cp.start()
        for k in range(1, 8):
            copy(k, me ^ k).wait_recv()
        for cp in copies:
            cp.wait_send()
        total = all_ref[0]
        for d in range(1, 8):
            total = total + all_ref[d]
        out_ref[...] = total

    return pl.pallas_call(
        body, name="allreduce_small",
        out_shape=jax.ShapeDtypeStruct(vec.shape, vec.dtype),
        in_specs=[pl.BlockSpec(memory_space=pltpu.VMEM)], out_specs=pl.BlockSpec(memory_space=pltpu.VMEM),
        scratch_shapes=[pltpu.VMEM((8,) + vec.shape, vec.dtype), pltpu.SemaphoreType.DMA((7,)),
                        pltpu.SemaphoreType.DMA((7,))],
    )(vec)


def _pack_rest(w_uq, w_ukv, w_mem, w_out):
    rows = jnp.concatenate([w_uq[0].T.reshape(-1, 1024), w_ukv.reshape(-1, 1024), w_mem.reshape(-1, 1024),
                            w_out.reshape(-1, 1024)], axis=0)
    return jnp.pad(rows, ((0, ROWS_REST - ROWS_USED), (0, 0)))


def _arranged_w_in(g_in):
    z = functools.partial(jnp.zeros, dtype=g_in.dtype)
    cut = 4480 - 2 * SHARD_ROWS
    return jnp.concatenate(
        [g_in[0, :SHARD_ROWS], g_in[1, :SHARD_ROWS], g_in[2, :cut], z((64, 1024)), g_in[2, cut:cut + 32],
         z((32, 1024)), g_in[2, cut + 32:SHARD_ROWS], g_in[3, :SHARD_ROWS]], axis=0)


def _rest_weights(g_rest):
    w_uq_t = g_rest[:, 0:ROWS_UQ].reshape(768, 256)
    w_uq_pad_t = jnp.pad(w_uq_t.reshape(MLA_HEADS, MLA_QK_DIM, 256), ((0, 0), (0, 32), (0, 0))).reshape(1024, 256)
    w_ukv = jnp.concatenate([g_rest[j, ROWS_UQ:ROWS_UQ + ROWS_UKV].reshape(128, 256) for j in range(4)], axis=1)
    lo = ROWS_UQ + ROWS_UKV
    w_mem = g_rest[:, lo:lo + ROWS_MEM].reshape(4 * ROWS_MEM, 1024)
    w_out = g_rest[:, lo + ROWS_MEM:lo + ROWS_MEM + ROWS_OUT].reshape(4 * ROWS_OUT, 1024)
    return w_uq_pad_t, w_ukv, w_mem, w_out


def _split_in(dw_in_arr_t):
    a = dw_in_arr_t
    gap = jnp.zeros((ROWS_IN - SHARD_ROWS, 1024), a.dtype)
    nat = 4608 - 96
    pieces = [a[:SHARD_ROWS], gap, a[SHARD_ROWS:2 * SHARD_ROWS], gap,
              a[2 * SHARD_ROWS:4480], a[4544:4576], a[4608:4608 + 3 * SHARD_ROWS - nat], gap,
              a[4608 + 3 * SHARD_ROWS - nat:], gap]
    return jnp.concatenate(pieces, axis=0).reshape(4, ROWS_IN, 1024)


def _split_rest(dw_uq_pad_t, dw_ukv, dw_mem, dw_out):
    dw_uq_t = dw_uq_pad_t.reshape(MLA_HEADS, LANES, 256)[:, :MLA_QK_DIM].reshape(4, ROWS_UQ, 1024)
    parts = [dw_uq_t, dw_ukv.reshape(128, 4, 256).transpose(1, 0, 2).reshape(4, ROWS_UKV, 1024),
             dw_mem.reshape(4, ROWS_MEM, 1024), dw_out.reshape(4, ROWS_OUT, 1024)]
    return jnp.pad(jnp.concatenate(parts, axis=1), ((0, 0), (0, ROWS_REST - ROWS_USED), (0, 0)))


def _rope_consts(rot, first, period):
    half = rot // 2
    inv_freq = np.float32(ROPE_THETA) ** (-(np.arange(0, rot, 2, dtype=np.float32) / np.float32(rot)))
    lane = np.arange(LANES) % period - first
    in_rot = (lane >= 0) & (lane < rot)
    out = np.zeros((8, LANES), np.float32)
    out[0] = np.where(in_rot, inv_freq[np.clip(lane, 0, rot - 1) % half], 0.0)
    out[1] = in_rot & (lane < half)
    out[2] = in_rot & (lane >= half)
    return jnp.asarray(out)


def _band_bias(s):
    nblk = s // BAND_Q
    starts = np.array([_band_start(i, s) for i in range(nblk)])
    uq = (np.arange(nblk)[:, None] * BAND_Q + np.arange(BAND_Q)[None, :])[:, :, None]
    uk = (starts[:, None] + np.arange(BAND_WIN)[None, :])[:, None, :]
    tiles, index, seen = [], [], {}
    for _, d in DILATED:
        length = s // d
        ok = (uq // length == uk // length) & (np.abs(uq - uk) <= 64)
        row = []
        for i in range(nblk):
            key = ok[i].tobytes()
            if key not in seen:
                seen[key] = len(tiles)
                tiles.append(np.where(ok[i], 0.0, NEG_INF).astype(np.float32))
            row.append(seen[key])
        index.append(row)
    return jnp.asarray(np.stack(tiles, axis=0)), index


def _forward_backward(h, proj, trig, rope_consts, x, mem, target, weights, gains):
    w_uq_pad_t, w_ukv, w_mem, w_out = weights
    g_emb, b_emb, g_cq, g_ckv, g_out_a, g_out_b, g_out_m, g_post, b_post = gains
    nb, s, d = x.shape
    t = nb * s
    x2 = x.reshape(t, d)
    mem2 = mem.reshape(nb * N_MEM, d)
    tgt2 = target.reshape(t, d)
    rope_a, rope_b = rope_consts
    bias, bias_index = _band_bias(s)
    scales = (0.125, MLA_QK_DIM ** -0.5, 128 ** -0.5)

    qa, ka, va, qb, kb, vb, qm, cqn, ckvn = _prep(proj, trig, w_uq_pad_t, w_ukv, g_cq, g_ckv, rope_a, rope_b, scales)
    mkv = _mm(mem2, w_mem, BF16, nb * N_MEM, 1024, 1024, "mem_kv")

    cfg_b = dict(nb=nb, s=s, sk=s, heads=8, hpb=2, voff=0, bq=256)
    cfg_m = dict(nb=nb, s=s, sk=N_MEM, heads=4, hpb=2, voff=4, bq=1024)
    ya, lse_a = _dilated_fwd(qa, ka, va, bias, bias_index, nb=nb, s=s, name="attn_a_fwd")
    yb, lse_b = _attn_fwd(qb, kb, vb, name="attn_b_fwd", **cfg_b)
    ym, lse_m = _attn_fwd(qm, mkv, mkv, name="attn_m_fwd", **cfg_m)

    (y, dz, doa, dob, dom, dga, dgb, dgm, loss, dg_post, db_post, dg_a, dg_b, dg_m) = _post(
        x2, ya, yb, ym, proj, tgt2, w_out, g_emb, b_emb, g_out_a, g_out_b, g_out_m, g_post, b_post)

    dqa, dka, dva = _dilated_bwd(qa, ka, va, ya, doa, lse_a, bias, bias_index, nb=nb, s=s, scale=scales[0],
                                 name="attn_a_bwd")
    dqb, dkb, dvb = _attn_bwd(qb, kb, vb, yb, dob, lse_b, name="attn_b_bwd", scale=scales[1], **cfg_b)
    dqm, dmk, dmv = _attn_bwd(qm, mkv, mkv, ym, dom, lse_m, name="attn_m_bwd", scale=scales[2], **cfg_m)
    dmkv = jnp.concatenate([dmk, dmv], axis=1)

    dproj, dqf, dkv, dg_cq, dg_ckv = _prep_bwd(
        dqa, dka, dva, dqb, dkb, dvb, dqm, dga, dgb, dgm, proj, trig, w_uq_pad_t, w_ukv, g_cq, g_ckv, rope_a, rope_b)

    small_rows = (dg_cq, dg_ckv, loss, dg_a, dg_b, dg_m, dg_post, db_post)
    return (dproj, h, y, dz, dqf, cqn, ckvn, dkv, mem2, dmkv), x2, small_rows


def _weight_grads(operands, core):
    dproj, h, y, dz, dqf, cqn, ckvn, dkv, mem2, dmkv = operands
    dw_in_arr_t = _mm(dproj, h, F32, 1024, 1024, 1024, "dw_in", mode="tn")
    g_in = _split_in(dw_in_arr_t)
    dw_out, r_in = _mm(y, dz, F32, 1024, 1024, 1024, "dw_out", mode="tn",
                       ride=_half_to_sibling(g_in.reshape(4, 2, HALF_IN, 1024)))
    dw_uq_pad_t = _mm(dqf, cqn, F32, 1024, 256, 1024, "dw_uq", mode="tn")
    dw_ukv = _mm(ckvn, dkv, F32, 128, 1024, 1024, "dw_ukv", mode="tn")
    dw_mem = _mm(mem2, dmkv, F32, 1024, 1024, mem2.shape[0], "dw_mem", mode="tn")
    g_rest = _split_rest(dw_uq_pad_t, dw_ukv, dw_mem, dw_out)
    sf_in, sb_in, r_rest = _core_sum(g_in, r_in, core, HALF_IN, HALF_IN // 2, "core_sum_in",
                                     ride=_half_to_sibling(g_rest.reshape(4, 2, HALF_REST, 1024)))
    sf_rest, sb_rest = _core_sum(g_rest, r_rest, core, HALF_REST, HALF_REST, "core_sum_rest")
    return sf_in, sb_in, sf_rest, sb_rest


def _small_block(dg_emb, db_emb, small_rows):
    dg_cq, dg_ckv, loss, dg_a, dg_b, dg_m, dg_post, db_post = small_rows
    row2 = jnp.concatenate([dg_cq, dg_ckv, loss, jnp.zeros((1, 512), F32)], axis=1)
    return jnp.concatenate([dg_emb, db_emb, row2, dg_a, jnp.concatenate([dg_b, dg_m], axis=1), dg_post, db_post,
                            jnp.zeros((1, 1024), F32)], axis=0)


def _pack_small(g_emb, b_emb, g_cq, g_ckv, g_out_a, g_out_b, g_out_m, g_post, b_post):
    row2 = jnp.concatenate([g_cq.reshape(1, -1), g_ckv.reshape(1, -1), jnp.zeros((1, 640), F32)], axis=1)
    return jnp.concatenate([g_emb.reshape(1, -1), b_emb.reshape(1, -1), row2, g_out_a.reshape(1, -1),
                            jnp.concatenate([g_out_b.reshape(1, -1), g_out_m.reshape(1, -1)], axis=1),
                            g_post.reshape(1, -1), b_post.reshape(1, -1), jnp.zeros((1, 1024), F32)], axis=0)


def kernel(x, mem, positions, g_emb, b_emb, w_in, g_cq, g_ckv, w_uq, w_ukv, w_mem_kv, g_out_a, g_out_b, g_out_m, w_out, g_post, b_post, loss_target, m_g_emb, m_b_emb, m_w_in, m_g_cq, m_g_ckv, m_w_uq, m_w_ukv, m_w_mem_kv, m_g_out_a, m_g_out_b, m_g_out_m, m_w_out, m_g_post, m_b_post, v_g_emb, v_b_emb, v_w_in, v_g_cq, v_g_ckv, v_w_uq, v_w_ukv, v_w_mem_kv, v_g_out_a, v_g_out_b, v_g_out_m, v_w_out, v_g_post, v_b_post):
    w_rest = _pack_rest(w_uq, w_ukv, w_mem_kv, w_out)
    w_in_t = w_in[0].T
    w_in_b = jnp.pad(w_in_t.astype(BF16), ((0, ROWS_IN - SHARD_ROWS), (0, 0)))
    gains = (g_emb.reshape(1, -1), b_emb.reshape(1, -1), g_cq, g_ckv, g_out_a, g_out_b, g_out_m, g_post, b_post)
    rope_consts = (_rope_consts(16, 0, 64), _rope_consts(32, 64, 128))
    h, trig, gathered_in = _ln_fwd(x.reshape(-1, D_MODEL), gains[0], gains[1],
                                   positions.reshape(-1, 1).astype(F32), *rope_consts,
                                   ride=_gather_ride(w_in_b.reshape(2, HALF_IN, 1024)))
    w_in_arr_t = _arranged_w_in(gathered_in.reshape(4, ROWS_IN, 1024))
    proj, gathered_rest = _mm(h, w_in_arr_t, F32, 1024, 1024, 1024, "in_proj", mode="nt",
                              ride=_gather_ride(w_rest.astype(BF16).reshape(2, HALF_REST, 1024)))
    weights = _rest_weights(gathered_rest.reshape(4, ROWS_REST, 1024))
    operands, x2, small_rows = _forward_backward(h, proj, trig, rope_consts, x, mem, loss_target, weights, gains)

    core = lax.axis_index("c").astype(jnp.int32).reshape(1)
    chip = (2 * lax.axis_index("x") + lax.axis_index("y")).astype(jnp.int32).reshape(1)
    sf_in, sb_in, sf_rest, sb_rest = _weight_grads(operands, core)
    grad_x, dg_emb, db_emb, rb_in, rb_rest = _dh_scatter(operands[0], w_in_arr_t, x2, operands[3], gains[0],
                                                         sb_in, sb_rest)
    gh_in = _chip_sum(sf_in, rb_in, chip, HALF_IN, HALF_IN // 2, "chip_sum_in")
    gh_rest = _chip_sum(sf_rest, rb_rest, chip, HALF_REST, HALF_REST, "chip_sum_rest")
    grad_in, grad_rest = _join_halves(gh_in, gh_rest)
    grad_in = grad_in.reshape(ROWS_IN, 1024)
    grad_rest = grad_rest.reshape(ROWS_REST, 1024)

    big_in = _adamw(grad_in, w_in_t, m_w_in[0].T, v_w_in[0].T, SHARD_ROWS // 3, "adamw_in")
    uq, ukv, wmem, wout = _adamw_pieces(
        grad_rest, w_rest, _pack_rest(m_w_uq, m_w_ukv, m_w_mem_kv, m_w_out),
        _pack_rest(v_w_uq, v_w_ukv, v_w_mem_kv, v_w_out), REST_PIECES, "adamw_rest")
    small_sum = _allreduce_small(_small_block(dg_emb, db_emb, small_rows))
    sm = _adamw_pieces(
        small_sum,
        _pack_small(g_emb, b_emb, g_cq, g_ckv, g_out_a, g_out_b, g_out_m, g_post, b_post),
        _pack_small(m_g_emb, m_b_emb, m_g_cq, m_g_ckv, m_g_out_a, m_g_out_b, m_g_out_m, m_g_post, m_b_post),
        _pack_small(v_g_emb, v_b_emb, v_g_cq, v_g_ckv, v_g_out_a, v_g_out_b, v_g_out_m, v_g_post, v_b_post),
        SMALL_PIECES, "adamw_small")
    loss = small_sum[2, 384]

    def ordered(kind):
        s_gemb, s_bemb, s_gcq, s_gckv, s_ga, s_gb, s_gm, s_gpost, s_bpost = [piece[kind] for piece in sm]
        return [s_gemb.reshape(-1), s_bemb.reshape(-1), big_in[kind].T[None], s_gcq, s_gckv,
                uq[kind].reshape(192, 256).T[None], ukv[kind].reshape(1, 128, 256), wmem[kind][None], s_ga, s_gb,
                s_gm, wout[kind][None], s_gpost, s_bpost]

    return (loss, grad_x.reshape(x.shape), *ordered(0), *ordered(1), *ordered(2), *ordered(3))
```

```python
import functools
import math

import jax
import jax.numpy as jnp
import numpy as np
from jax import lax
from jax.experimental import pallas as pl
from jax.experimental.pallas import tpu as pltpu

F32 = jnp.float32
BF16 = jnp.bfloat16
MESH = pl.DeviceIdType.MESH
ANY = pl.BlockSpec(memory_space=pl.ANY)
IN_VMEM = pl.BlockSpec(memory_space=pltpu.VMEM)

D_MODEL = 1024
A_WIDTH = 1024
MLA_HEADS = 8
MLA_Q_RANK = 256
MLA_KV_RANK = 128
MLA_QK_DIM = 96
MEM_WIDTH = 512
N_MEM = 256
ROPE_THETA = 500000.0
NORM_EPS = 1e-5
NEG_INF = -1e30
DEEPNORM_ALPHA = 2.0 ** 0.25
DILATED = ((64, 1), (256, 4), (1024, 16))

ADAM_LR = 0.001
ADAM_B1 = 0.9
ADAM_B2 = 0.999
ADAM_EPS = 1e-08
ADAM_WD = 0.01
ADAM_STEP = 10

LANES = 128
VMEM_LIMIT = 56 * 1024 * 1024
LOG2E = math.log2(math.e)
LN2 = math.log(2.0)

PROJ_W = 6144
COL_CQ = 4096
COL_BG = 4608
COL_MQ = 5120
COL_MG = 5632

SHARD_ROWS = 1512
ROWS_IN = 1536
ROWS_UQ, ROWS_UKV, ROWS_MEM, ROWS_OUT = 48, 32, 256, 512
ROWS_USED = ROWS_UQ + ROWS_UKV + ROWS_MEM + ROWS_OUT
ROWS_REST = 864
HALF_IN = ROWS_IN // 2
HALF_REST = ROWS_REST // 2
REST_PIECES = ((0, 48, 0, 1024), (48, 80, 0, 1024), (80, 336, 0, 1024), (336, 848, 0, 1024))
SMALL_PIECES = ((0, 1, 0, 1024), (1, 2, 0, 1024), (2, 3, 0, 256), (2, 3, 256, 384), (3, 4, 0, 1024), (4, 5, 0, 512),
                (4, 5, 512, 1024), (5, 6, 0, 1024), (6, 7, 0, 1024))


def _params(sem=None, vmem=VMEM_LIMIT):
    return pltpu.CompilerParams(dimension_semantics=sem, vmem_limit_bytes=vmem)


def _dot(a, b):
    return jnp.dot(a, b, preferred_element_type=F32)


def _dot_nt(a, b):
    return lax.dot_general(a, b, (((1,), (1,)), ((), ())), preferred_element_type=F32)


def _dot_tn(a, b):
    return lax.dot_general(a, b, (((0,), (0,)), ((), ())), preferred_element_type=F32)


def _ln_hat(x):
    mu = jnp.mean(x, axis=-1, keepdims=True)
    xc = x - mu
    var = jnp.mean(xc * xc, axis=-1, keepdims=True)
    rstd = lax.rsqrt(var + NORM_EPS)
    return xc * rstd, rstd


def _ln_bwd_rows(dxh, xh, rstd):
    return rstd * (dxh - jnp.mean(dxh, axis=-1, keepdims=True) - xh * jnp.mean(dxh * xh, axis=-1, keepdims=True))


def _rms_hat(x, width):
    ms = jnp.sum(x * x, axis=-1, keepdims=True) * (1.0 / width)
    r = lax.rsqrt(ms + NORM_EPS)
    return x * r, r


def _rms_bwd(u, xh, r, width):
    return r * (u - xh * (jnp.sum(u * xh, axis=-1, keepdims=True) * (1.0 / width)))


def _colsum(v):
    return jnp.sum(v, axis=0, keepdims=True)


def _rope_tables(cos, sin, consts):
    return cos, sin * consts[2:3, :], -sin * consts[1:2, :]


def _rope(x, tables, half, inverse=False):
    c, s_up, s_dn = tables
    if inverse:
        s_up, s_dn = -s_up, -s_dn
    return x * c + pltpu.roll(x, half, 1) * s_up + pltpu.roll(x, LANES - half, 1) * s_dn


def _ln_fwd(x, g, b, pos, rope_a, rope_b, tm=512, ride=None):
    t, d = x.shape
    n_in = len(ride.args) if ride else 0
    n_out = len(ride.out_shapes) if ride else 0
    steps = t // tm

    def body(x_ref, g_ref, b_ref, pos_ref, ra_ref, rb_ref, *rest):
        h_ref, trig_ref = rest[n_in], rest[n_in + 1]
        if ride:
            i = pl.program_id(0)
            ride.run(i == 0, i == steps - 1, rest[:n_in], rest[n_in + 2:n_in + 2 + n_out], rest[n_in + 2 + n_out:])
        xh, _ = _ln_hat(x_ref[...])
        h_ref[...] = (xh * g_ref[...] + b_ref[...]).astype(BF16)
        for j, consts in enumerate((ra_ref, rb_ref)):
            ang = pos_ref[...] * consts[0:1, :]
            trig_ref[:, 2 * j * LANES:(2 * j + 1) * LANES] = jnp.cos(ang)
            trig_ref[:, (2 * j + 1) * LANES:(2 * j + 2) * LANES] = jnp.sin(ang)

    row = pl.BlockSpec((1, d), lambda i: (0, 0))
    tile = pl.BlockSpec((tm, d), lambda i: (i, 0))
    consts = pl.BlockSpec((8, LANES), lambda i: (0, 0))
    trig_tile = pl.BlockSpec((tm, 4 * LANES), lambda i: (i, 0))
    in_specs = [tile, row, row, pl.BlockSpec((tm, 1), lambda i: (i, 0)), consts, consts]
    shapes = (jax.ShapeDtypeStruct((t, d), BF16), jax.ShapeDtypeStruct((t, 4 * LANES), F32))
    if not ride:
        return pl.pallas_call(
            body, name="ln_fwd", grid=(steps,), out_shape=shapes, in_specs=in_specs, out_specs=(tile, trig_tile),
            compiler_params=_params(("parallel",)),
        )(x, g, b, pos, rope_a, rope_b)
    return pl.pallas_call(
        body, name="ln_fwd", grid=(steps,),
        out_shape=(*shapes, *ride.out_shapes),
        in_specs=in_specs + ride.in_specs, out_specs=(tile, trig_tile) + (ANY,) * n_out,
        scratch_shapes=ride.scratch(),
        compiler_params=_params(("arbitrary",)),
    )(x, g, b, pos, rope_a, rope_b, *ride.args)


class _Ride:
    def __init__(self, args, out_shapes, sem_counts, plan, in_specs=None):
        self.args, self.out_shapes, self.plan = list(args), list(out_shapes), plan
        self.sem_counts = sem_counts
        self.in_specs = in_specs or [ANY] * len(self.args)

    def scratch(self):
        return [pltpu.SemaphoreType.DMA((n,)) for n in self.sem_counts]

    def run(self, first, last, in_refs, out_refs, sems, middle=None):
        def stage(k):
            stages = self.plan(in_refs, out_refs, *sems)
            if k == 0 or len(stages) == 3:
                return stages[k]
            return (lambda: None) if k == 1 else stages[1]

        @pl.when(first)
        def _():
            stage(0)()

        if middle is not None:
            @pl.when(middle)
            def _():
                stage(1)()

        @pl.when(last)
        def _():
            if middle is None:
                stage(1)()
            stage(2)()


def _mm(a, b, out_dtype, tm, tn, tk, name, mode="nn", ride=None):
    if mode == "tn":
        k, m = a.shape
    else:
        m, k = a.shape
    n = b.shape[0] if mode == "nt" else b.shape[1]
    nk = k // tk
    nj, ni = n // tn, m // tm
    n_in = len(ride.args) if ride else 0
    n_out = len(ride.out_shapes) if ride else 0

    def body(a_ref, b_ref, *rest):
        o_ref = rest[n_in]
        acc_ref = rest[n_in + 1 + n_out]
        if ride:
            j, i, kk = pl.program_id(0), pl.program_id(1), pl.program_id(2)
            step = (j * ni + i) * nk + kk
            total = nj * ni * nk
            ride.run(step == 0, step == total - 1, rest[:n_in], rest[n_in + 1:n_in + 1 + n_out],
                     rest[n_in + 2 + n_out:], middle=(step == (2 * total) // 3) if total >= 3 else None)
        av = a_ref[...].astype(BF16)
        bv = b_ref[...].astype(BF16)
        part = _dot_tn(av, bv) if mode == "tn" else _dot_nt(av, bv) if mode == "nt" else _dot(av, bv)
        if nk == 1:
            o_ref[...] = part.astype(out_dtype)
        else:
            kk = pl.program_id(2)

            @pl.when(kk == 0)
            def _():
                acc_ref[...] = part

            @pl.when(kk > 0)
            def _():
                acc_ref[...] += part

            @pl.when(kk == nk - 1)
            def _():
                o_ref[...] = acc_ref[...].astype(out_dtype)

    a_spec = (pl.BlockSpec((tk, tm), lambda j, i, kk: (kk, i)) if mode == "tn"
              else pl.BlockSpec((tm, tk), lambda j, i, kk: (i, kk)))
    b_spec = (pl.BlockSpec((tn, tk), lambda j, i, kk: (j, kk)) if mode == "nt"
              else pl.BlockSpec((tk, tn), lambda j, i, kk: (kk, j)))
    o_spec = pl.BlockSpec((tm, tn), lambda j, i, kk: (i, j))
    o_shape = jax.ShapeDtypeStruct((m, n), out_dtype)
    if not ride:
        return pl.pallas_call(
            body, name=name, grid=(nj, ni, nk), out_shape=o_shape, in_specs=[a_spec, b_spec], out_specs=o_spec,
            scratch_shapes=[pltpu.VMEM((tm, tn), F32)],
            compiler_params=_params(("parallel", "parallel", "arbitrary")),
        )(a, b)
    return pl.pallas_call(
        body, name=name, grid=(nj, ni, nk),
        out_shape=(o_shape, *ride.out_shapes),
        in_specs=[a_spec, b_spec] + ride.in_specs,
        out_specs=(o_spec,) + (ANY,) * n_out,
        scratch_shapes=[pltpu.VMEM((tm, tn), F32)] + ride.scratch(),
        compiler_params=_params(("arbitrary", "arbitrary", "arbitrary")),
    )(a, b, *ride.args)


def _prep(proj, trig, w_uq, w_ukv, g_cq, g_ckv, rope_a, rope_b, scales, tm=256):
    t = proj.shape[0]
    sc_a, sc_b, sc_m = (s * LOG2E for s in scales)

    def body(aq_ref, ak_ref, av_ref, bs_ref, mq_ref, trig_ref, wuq_ref, wukv_ref, gcq_ref, gckv_ref,
             ra_ref, rb_ref, qa_ref, ka_ref, va_ref, qb_ref, kb_ref, vb_ref, qm_ref, cqn_ref, ckvn_ref):
        ta = _rope_tables(trig_ref[:, 0:LANES], trig_ref[:, LANES:2 * LANES], ra_ref[...])
        tb = _rope_tables(trig_ref[:, 2 * LANES:3 * LANES], trig_ref[:, 3 * LANES:4 * LANES], rb_ref[...])
        for j in range(A_WIDTH // LANES):
            sl = slice(j * LANES, (j + 1) * LANES)
            qa_ref[:, sl] = (_rope(aq_ref[:, sl], ta, 8) * sc_a).astype(BF16)
            ka_ref[:, sl] = _rope(ak_ref[:, sl], ta, 8).astype(BF16)
        va_ref[...] = av_ref[...].astype(BF16)
        qm_ref[...] = (mq_ref[...] * sc_m).astype(BF16)

        cq_hat, _ = _rms_hat(bs_ref[:, 0:MLA_Q_RANK], MLA_Q_RANK)
        cqn = (cq_hat * gcq_ref[...]).astype(BF16)
        cqn_ref[...] = cqn
        ckv_hat, _ = _rms_hat(bs_ref[:, MLA_Q_RANK:MLA_Q_RANK + MLA_KV_RANK], MLA_KV_RANK)
        ckvn = (ckv_hat * gckv_ref[...]).astype(BF16)
        ckvn_ref[...] = ckvn
        qfull = _dot_nt(cqn, wuq_ref[...])
        kv = _dot(ckvn, wukv_ref[...])
        kr = _rope(bs_ref[:, 384:512], tb, 16)
        lane = lax.broadcasted_iota(jnp.int32, (1, LANES), 1)
        low = lane < 64
        for h in range(MLA_HEADS):
            sl = slice(h * LANES, (h + 1) * LANES)
            qb_ref[:, sl] = (_rope(qfull[:, sl], tb, 16) * sc_b).astype(BF16)
            kb_ref[:, sl] = jnp.where(low, kv[:, sl], kr).astype(BF16)
            vb_ref[:, sl] = jnp.where(low, 0.0, kv[:, sl]).astype(BF16)

    def col(width, idx):
        return pl.BlockSpec((tm, width), lambda i: (i, idx))

    def full(shape):
        return pl.BlockSpec(shape, lambda i: (0, 0))

    wide = jax.ShapeDtypeStruct((t, 1024), BF16)
    return pl.pallas_call(
        body, name="prep", grid=(t // tm,),
        out_shape=(wide, wide, wide, wide, wide, wide,
                   jax.ShapeDtypeStruct((t, MEM_WIDTH), BF16),
                   jax.ShapeDtypeStruct((t, MLA_Q_RANK), BF16),
                   jax.ShapeDtypeStruct((t, MLA_KV_RANK), BF16)),
        in_specs=[col(1024, 0), col(1024, 1), col(1024, 2), col(512, COL_CQ // 512), col(512, COL_MQ // 512),
                  pl.BlockSpec((tm, 4 * LANES), lambda i: (i, 0)),
                  full((1024, MLA_Q_RANK)), full((MLA_KV_RANK, 1024)),
                  full((1, MLA_Q_RANK)), full((1, MLA_KV_RANK)), full((8, LANES)), full((8, LANES))],
        out_specs=(col(1024, 0),) * 6 + (col(MEM_WIDTH, 0), col(MLA_Q_RANK, 0), col(MLA_KV_RANK, 0)),
        compiler_params=_params(("parallel",)),
    )(proj, proj, proj, proj, proj, trig, w_uq, w_ukv, g_cq, g_ckv, rope_a, rope_b)


def _attn_fwd(q, k, v, *, nb, s, sk, heads, hpb, voff, bq, name):
    nq = s // bq
    width = hpb * LANES
    vblk = voff // hpb

    def body(q_ref, k_ref, v_ref, o_ref, lse_ref):
        for h in range(hpb):
            sl = slice(h * LANES, (h + 1) * LANES)
            sc = _dot_nt(q_ref[:, sl], k_ref[:, sl])
            m = jnp.max(sc, axis=1, keepdims=True)
            p = jnp.exp2(sc - m)
            l = jnp.sum(p, axis=1, keepdims=True)
            o_ref[:, sl] = _dot(p.astype(BF16), v_ref[:, sl]) / l
            lse_ref[:, sl] = jnp.broadcast_to(m + jnp.log(l) * LOG2E, (bq, LANES))

    out = jax.ShapeDtypeStruct((nb * s, heads * LANES), F32)
    ospec = pl.BlockSpec((bq, width), lambda b, i, g: (b * nq + i, g))
    return pl.pallas_call(
        body, name=name, grid=(nb, nq, heads // hpb),
        out_shape=(out, out),
        in_specs=[ospec, pl.BlockSpec((sk, width), lambda b, i, g: (b, g)),
                  pl.BlockSpec((sk, width), lambda b, i, g: (b, vblk + g))],
        out_specs=(ospec, ospec),
        compiler_params=_params(("parallel", "parallel", "parallel")),
    )(q, k, v)


def _attn_bwd(q, k, v, o, do, lse, *, nb, s, sk, heads, hpb, voff, scale, bq, name):
    nq = s // bq
    width = hpb * LANES
    vblk = voff // hpb

    def body(q_ref, k_ref, v_ref, o_ref, do_ref, lse_ref, dq_ref, dk_ref, dv_ref, dk_acc, dv_acc):
        i = pl.program_id(2)

        @pl.when(i == 0)
        def _():
            dk_acc[...] = jnp.zeros_like(dk_acc)
            dv_acc[...] = jnp.zeros_like(dv_acc)

        for h in range(hpb):
            sl = slice(h * LANES, (h + 1) * LANES)
            qh = q_ref[:, sl]
            kk = k_ref[:, sl]
            doh = do_ref[:, sl]
            delta = jnp.sum(doh.astype(F32) * o_ref[:, sl], axis=1, keepdims=True)
            p = jnp.exp2(_dot_nt(qh, kk) - lse_ref[:, h * LANES:h * LANES + 1])
            ds = (p * (_dot_nt(doh, v_ref[:, sl]) - delta)).astype(BF16)
            dq_ref[:, sl] = (_dot(ds, kk) * scale).astype(BF16)
            dk_acc[:, sl] += _dot_tn(ds, qh)
            dv_acc[:, sl] += _dot_tn(p.astype(BF16), doh)

        @pl.when(i == nq - 1)
        def _():
            dk_ref[...] = (dk_acc[...] * LN2).astype(BF16)
            dv_ref[...] = dv_acc[...].astype(BF16)

    qspec = pl.BlockSpec((bq, width), lambda b, g, i: (b * nq + i, g))
    kv_spec = pl.BlockSpec((sk, width), lambda b, g, i: (b, g))
    dq_shape = jax.ShapeDtypeStruct((nb * s, heads * LANES), BF16)
    dkv_shape = jax.ShapeDtypeStruct((nb * sk, heads * LANES), BF16)
    return pl.pallas_call(
        body, name=name, grid=(nb, heads // hpb, nq),
        out_shape=(dq_shape, dkv_shape, dkv_shape),
        in_specs=[qspec, kv_spec, pl.BlockSpec((sk, width), lambda b, g, i: (b, vblk + g)), qspec, qspec, qspec],
        out_specs=(qspec, kv_spec, kv_spec),
        scratch_shapes=[pltpu.VMEM((sk, width), F32), pltpu.VMEM((sk, width), F32)],
        compiler_params=_params(("parallel", "parallel", "arbitrary")),
    )(q, k, v, o, do, lse)


BAND_Q = 128
BAND_WIN = 256


def _band_start(i, s):
    return min(max(i * BAND_Q - 64, 0), s - BAND_WIN)


def _to_pattern_order(src_ref, dst_ref, stage_ref, s, d):
    length = s // d
    stage_ref[...] = src_ref[...].astype(F32)
    for r in range(d):
        dst_ref[r * length:(r + 1) * length, :] = stage_ref[pl.ds(r, length, stride=d), :].astype(dst_ref.dtype)


def _dilated_fwd(q, k, v, bias, bias_index, *, nb, s, name):
    nblk = s // BAND_Q
    npat = len(DILATED)

    def body(q_ref, k_ref, v_ref, bias_ref, o_ref, lse_ref, stage_ref, qp_ref, kp_ref, vp_ref, op_ref, lp_ref,
             on_ref, ln_ref):
        lane = lax.broadcasted_iota(jnp.int32, (1, LANES), 1)
        first = lane < 64
        for p, (_, d) in enumerate(DILATED):
            if d == 1:
                qs, ks, vs = q_ref, k_ref, v_ref
            else:
                for src, dst in ((q_ref, qp_ref), (k_ref, kp_ref), (v_ref, vp_ref)):
                    _to_pattern_order(src, dst, stage_ref, s, d)
                qs, ks, vs = qp_ref, kp_ref, vp_ref
            for i in range(nblk):
                u0 = i * BAND_Q
                st = _band_start(i, s)
                qi = qs[u0:u0 + BAND_Q, :]
                kw = ks[st:st + BAND_WIN, :]
                vw = vs[st:st + BAND_WIN, :]
                zero = jnp.zeros_like(qi)
                q2 = jnp.concatenate([jnp.where(first, qi, zero), jnp.where(first, zero, qi)], axis=0)
                sc = _dot_nt(q2, kw)
                b = bias_ref[bias_index[p][i]]
                halves = []
                for h in range(2):
                    sh = sc[h * BAND_Q:(h + 1) * BAND_Q] + b
                    m = jnp.max(sh, axis=1, keepdims=True)
                    pr = jnp.exp2(sh - m)
                    l = jnp.sum(pr, axis=1, keepdims=True)
                    halves.append((pr.astype(BF16), l, m + jnp.log(l) * LOG2E))
                o2 = _dot(jnp.concatenate([halves[0][0], halves[1][0]], axis=0), vw)
                o_blk = jnp.where(first, o2[:BAND_Q] / halves[0][1], o2[BAND_Q:] / halves[1][1])
                lse_blk = jnp.where(first, jnp.broadcast_to(halves[0][2], (BAND_Q, LANES)),
                                    jnp.broadcast_to(halves[1][2], (BAND_Q, LANES)))
                op_ref[p, u0:u0 + BAND_Q, :] = o_blk
                lp_ref[p, u0:u0 + BAND_Q, :] = lse_blk
            if d > 1:
                length = s // d
                for r in range(d):
                    on_ref.at[p - 1][pl.ds(r, length, stride=d), :] = op_ref[p, r * length:(r + 1) * length, :]
                    ln_ref.at[p - 1][pl.ds(r, length, stride=d), :] = lp_ref[p, r * length:(r + 1) * length, :]
        lses = [lp_ref[0]] + [ln_ref[p] for p in range(npat - 1)]
        outs = [op_ref[0]] + [on_ref[p] for p in range(npat - 1)]
        m = functools.reduce(jnp.maximum, lses)
        ws = [jnp.exp2(l - m) for l in lses]
        den = functools.reduce(lambda a, c: a + c, ws)
        o_ref[...] = functools.reduce(lambda a, c: a + c, [w * o for w, o in zip(ws, outs)]) / den
        lse_ref[...] = m + jnp.log(den) * LOG2E

    blk = pl.BlockSpec((s, LANES), lambda b, g: (b, g))
    out = jax.ShapeDtypeStruct((nb * s, A_WIDTH), F32)
    return pl.pallas_call(
        body, name=name, grid=(nb, A_WIDTH // LANES),
        out_shape=(out, out),
        in_specs=[blk, blk, blk, pl.BlockSpec(bias.shape, lambda b, g: (0, 0, 0))],
        out_specs=(blk, blk),
        scratch_shapes=[pltpu.VMEM((s, LANES), F32), pltpu.VMEM((s, LANES), BF16), pltpu.VMEM((s, LANES), BF16),
                        pltpu.VMEM((s, LANES), BF16), pltpu.VMEM((npat, s, LANES), F32),
                        pltpu.VMEM((npat, s, LANES), F32), pltpu.VMEM((npat - 1, s, LANES), F32),
                        pltpu.VMEM((npat - 1, s, LANES), F32)],
        compiler_params=_params(("parallel", "parallel")),
    )(q, k, v, bias)


def _dilated_bwd(q, k, v, o, do, lse, bias, bias_index, *, nb, s, scale, name):
    nblk = s // BAND_Q
    npat = len(DILATED)

    def body(q_ref, k_ref, v_ref, o_ref, do_ref, lse_ref, bias_ref, dq_out, dk_out, dv_out,
             stage_ref, dl_ref, qp_ref, kp_ref, vp_ref, dop_ref, lsp_ref, dlp_ref, dqp_ref, dkp_ref, dvp_ref,
             dq_ref, dk_ref, dv_ref):
        lane = lax.broadcasted_iota(jnp.int32, (1, LANES), 1)
        first = lane < 64
        prod = do_ref[...].astype(F32) * o_ref[...]
        d0 = jnp.sum(jnp.where(first, prod, 0.0), axis=1, keepdims=True)
        d1 = jnp.sum(jnp.where(first, 0.0, prod), axis=1, keepdims=True)
        dl_ref[...] = jnp.where(first, jnp.broadcast_to(d0, (s, LANES)), jnp.broadcast_to(d1, (s, LANES)))
        for p, (_, d) in enumerate(DILATED):
            length = s // d
            if d == 1:
                qs, ks, vs, dos, lss, dls = q_ref, k_ref, v_ref, do_ref, lse_ref, dl_ref
                dqs, dks, dvs = dq_ref, dk_ref, dv_ref
            else:
                for src, dst in ((q_ref, qp_ref), (k_ref, kp_ref), (v_ref, vp_ref), (do_ref, dop_ref),
                                 (lse_ref, lsp_ref), (dl_ref, dlp_ref)):
                    _to_pattern_order(src, dst, stage_ref, s, d)
                qs, ks, vs, dos, lss, dls = qp_ref, kp_ref, vp_ref, dop_ref, lsp_ref, dlp_ref
                dqs, dks, dvs = dqp_ref, dkp_ref, dvp_ref
            dks[...] = jnp.zeros((s, LANES), F32)
            dvs[...] = jnp.zeros((s, LANES), F32)
            for i in range(nblk):
                u0 = i * BAND_Q
                st = _band_start(i, s)
                qi = qs[u0:u0 + BAND_Q, :]
                doi = dos[u0:u0 + BAND_Q, :]
                kw = ks[st:st + BAND_WIN, :]
                vw = vs[st:st + BAND_WIN, :]
                zero = jnp.zeros_like(qi)
                q2 = jnp.concatenate([jnp.where(first, qi, zero), jnp.where(first, zero, qi)], axis=0)
                do2 = jnp.concatenate([jnp.where(first, doi, zero), jnp.where(first, zero, doi)], axis=0)
                sc = _dot_nt(q2, kw)
                dp = _dot_nt(do2, vw)
                b = bias_ref[bias_index[p][i]]
                lse_i = lss[u0:u0 + BAND_Q, :]
                dl_i = dls[u0:u0 + BAND_Q, :]
                ps, dss = [], []
                for h in range(2):
                    rows = slice(h * BAND_Q, (h + 1) * BAND_Q)
                    pr = jnp.exp2(sc[rows] + b - lse_i[:, 64 * h:64 * h + 1])
                    ps.append(pr.astype(BF16))
                    dss.append((pr * (dp[rows] - dl_i[:, 64 * h:64 * h + 1])).astype(BF16))
                p2 = jnp.concatenate(ps, axis=0)
                ds2 = jnp.concatenate(dss, axis=0)
                dq2 = _dot(ds2, kw)
                dqs[u0:u0 + BAND_Q, :] = jnp.where(first, dq2[:BAND_Q], dq2[BAND_Q:]) * scale
                dks[st:st + BAND_WIN, :] += _dot_tn(ds2, q2)
                dvs[st:st + BAND_WIN, :] += _dot_tn(p2, do2)
            if d > 1:
                for dst, src in ((dq_ref, dqp_ref), (dk_ref, dkp_ref), (dv_ref, dvp_ref)):
                    for r in range(d):
                        dst[pl.ds(r, length, stride=d), :] += src[r * length:(r + 1) * length, :]
        dq_out[...] = dq_ref[...].astype(BF16)
        dk_out[...] = (dk_ref[...] * LN2).astype(BF16)
        dv_out[...] = dv_ref[...].astype(BF16)

    blk = pl.BlockSpec((s, LANES), lambda b, g: (b, g))
    out = jax.ShapeDtypeStruct((nb * s, A_WIDTH), BF16)
    f32_buf = pltpu.VMEM((s, LANES), F32)
    bf_buf = pltpu.VMEM((s, LANES), BF16)
    return pl.pallas_call(
        body, name=name, grid=(nb, A_WIDTH // LANES),
        out_shape=(out, out, out),
        in_specs=[blk] * 6 + [pl.BlockSpec(bias.shape, lambda b, g: (0, 0, 0))],
        out_specs=(blk, blk, blk),
        scratch_shapes=[f32_buf, f32_buf, bf_buf, bf_buf, bf_buf, bf_buf] + [f32_buf] * 8,
        compiler_params=_params(("parallel", "parallel")),
    )(q, k, v, o, do, lse, bias)


def _post(x, ya, ybp, ym, proj, target, w_out, g_emb, b_emb, g_a, g_b, g_m, g_post, b_post, tm=256):
    t = x.shape[0]

    def body(x_ref, ya_ref, yb_ref, ym_ref, ga_ref, gb_ref, gm_ref, tg_ref, wo_ref,
             ge_ref, be_ref, goa_ref, gob_ref, gom_ref, gp_ref, bp_ref,
             y_ref, dz_ref, doa_ref, dob_ref, dom_ref, dga_ref, dgb_ref, dgm_ref,
             loss_ref, dgp_ref, dbp_ref, dgoa_ref, dgob_ref, dgom_ref):
        i = pl.program_id(0)

        @pl.when(i == 0)
        def _():
            for r in (loss_ref, dgp_ref, dbp_ref, dgoa_ref, dgob_ref, dgom_ref):
                r[...] = jnp.zeros_like(r)

        lane = lax.broadcasted_iota(jnp.int32, (1, LANES), 1)
        low = lane < 64
        xh0, _ = _ln_hat(x_ref[...])
        h = xh0 * ge_ref[...] + be_ref[...]

        ybp_v = yb_ref[...]
        yb = jnp.concatenate(
            [jnp.where(low, pltpu.roll(ybp_v[:, 2 * j * LANES:(2 * j + 1) * LANES], 64, 1),
                       ybp_v[:, (2 * j + 1) * LANES:(2 * j + 2) * LANES]) for j in range(4)], axis=1)

        def gated(raw, gate, gain, width):
            xh, r = _rms_hat(raw, width)
            n = xh * gain
            sg = 1.0 / (1.0 + jnp.exp(-gate))
            return xh, r, n, sg, n * (gate * sg)

        gate_a, gate_b, gate_m = ga_ref[...], gb_ref[...], gm_ref[...]
        xh_a, r_a, n_a, sg_a, y_a = gated(ya_ref[...], gate_a, goa_ref[...], A_WIDTH)
        xh_b, r_b, n_b, sg_b, y_b = gated(yb, gate_b, gob_ref[...], 512)
        xh_m, r_m, n_m, sg_m, y_m = gated(ym_ref[...], gate_m, gom_ref[...], 512)
        y = jnp.concatenate([y_a, y_b, y_m], axis=1).astype(BF16)
        y_ref[...] = y
        z = DEEPNORM_ALPHA * h + _dot(y, wo_ref[...])
        zh, rstd = _ln_hat(z)
        err = zh * gp_ref[...] + bp_ref[...] - tg_ref[...]
        rows = jnp.sum(err * err, axis=1, keepdims=True)
        loss_ref[...] += jnp.broadcast_to(jnp.sum(rows, axis=0, keepdims=True) * (0.5 / D_MODEL), (1, LANES))
        dout = err * (1.0 / D_MODEL)
        dgp_ref[...] += _colsum(dout * zh)
        dbp_ref[...] += _colsum(dout)
        dz = _ln_bwd_rows(dout * gp_ref[...], zh, rstd)
        dz_ref[...] = dz
        dy = _dot_nt(dz.astype(BF16), wo_ref[...])

        def gated_bwd(dyg, xh, r, n, sg, gate, gain, width, dgain_ref):
            dn = dyg * (gate * sg)
            dgate = dyg * n * (sg * (1.0 + gate * (1.0 - sg)))
            dgain_ref[...] += _colsum(dn * xh)
            return _rms_bwd(dn * gain, xh, r, width), dgate

        dya, dgate_a = gated_bwd(dy[:, 0:1024], xh_a, r_a, n_a, sg_a, gate_a, goa_ref[...], A_WIDTH, dgoa_ref)
        dyb, dgate_b = gated_bwd(dy[:, 1024:1536], xh_b, r_b, n_b, sg_b, gate_b, gob_ref[...], 512, dgob_ref)
        dym, dgate_m = gated_bwd(dy[:, 1536:2048], xh_m, r_m, n_m, sg_m, gate_m, gom_ref[...], 512, dgom_ref)
        doa_ref[...] = dya.astype(BF16)
        dom_ref[...] = dym.astype(BF16)
        dga_ref[...] = dgate_a.astype(BF16)
        dgb_ref[...] = dgate_b.astype(BF16)
        dgm_ref[...] = dgate_m.astype(BF16)
        for j in range(4):
            blk = dyb[:, j * LANES:(j + 1) * LANES]
            dob_ref[:, 2 * j * LANES:(2 * j + 1) * LANES] = jnp.where(low, 0.0, pltpu.roll(blk, 64, 1)).astype(BF16)
            dob_ref[:, (2 * j + 1) * LANES:(2 * j + 2) * LANES] = jnp.where(low, 0.0, blk).astype(BF16)

    def col(width, idx):
        return pl.BlockSpec((tm, width), lambda i: (i, idx))

    def full(shape):
        return pl.BlockSpec(shape, lambda i: (0, 0))

    def acc(width):
        return jax.ShapeDtypeStruct((1, width), F32)

    return pl.pallas_call(
        body, name="post", grid=(t // tm,),
        out_shape=(jax.ShapeDtypeStruct((t, 2048), BF16), jax.ShapeDtypeStruct((t, 1024), F32),
                   jax.ShapeDtypeStruct((t, 1024), BF16), jax.ShapeDtypeStruct((t, 1024), BF16),
                   jax.ShapeDtypeStruct((t, 512), BF16),
                   jax.ShapeDtypeStruct((t, 1024), BF16), jax.ShapeDtypeStruct((t, 512), BF16),
                   jax.ShapeDtypeStruct((t, 512), BF16),
                   acc(LANES), acc(1024), acc(1024), acc(1024), acc(512), acc(512)),
        in_specs=[col(1024, 0), col(1024, 0), col(1024, 0), col(512, 0),
                  col(1024, 3), col(512, COL_BG // 512), col(512, COL_MG // 512), col(1024, 0),
                  full((2048, 1024)),
                  full((1, 1024)), full((1, 1024)), full((1, 1024)), full((1, 512)), full((1, 512)),
                  full((1, 1024)), full((1, 1024))],
        out_specs=(col(2048, 0), col(1024, 0), col(1024, 0), col(1024, 0), col(512, 0),
                   col(1024, 0), col(512, 0), col(512, 0),
                   full((1, LANES)), full((1, 1024)), full((1, 1024)), full((1, 1024)), full((1, 512)),
                   full((1, 512))),
        compiler_params=_params(("arbitrary",)),
    )(x, ya, ybp, ym, proj, proj, proj, target, w_out, g_emb, b_emb, g_a, g_b, g_m, g_post, b_post)


def _prep_bwd(dqa, dka, dva, dqb, dkb, dvb, dqm, dga, dgb, dgm, proj, trig, w_uq, w_ukv, g_cq, g_ckv,
              rope_a, rope_b, tm=256):
    t = proj.shape[0]

    def body(dqa_ref, dka_ref, dva_ref, dqb_ref, dkb_ref, dvb_ref, dqm_ref, dga_ref, dgb_ref, dgm_ref,
             bs_ref, trig_ref, wuq_ref, wukv_ref, gcq_ref, gckv_ref, ra_ref, rb_ref,
             dproj_ref, dqf_ref, dkv_ref, dgcq_ref, dgckv_ref):
        i = pl.program_id(0)

        @pl.when(i == 0)
        def _():
            dgcq_ref[...] = jnp.zeros_like(dgcq_ref)
            dgckv_ref[...] = jnp.zeros_like(dgckv_ref)

        ta = _rope_tables(trig_ref[:, 0:LANES], trig_ref[:, LANES:2 * LANES], ra_ref[...])
        tb = _rope_tables(trig_ref[:, 2 * LANES:3 * LANES], trig_ref[:, 3 * LANES:4 * LANES], rb_ref[...])
        for j in range(A_WIDTH // LANES):
            sl = slice(j * LANES, (j + 1) * LANES)
            dproj_ref[:, j * LANES:(j + 1) * LANES] = (
                _rope(dqa_ref[:, sl].astype(F32), ta, 8, inverse=True).astype(BF16))
            dproj_ref[:, 1024 + j * LANES:1024 + (j + 1) * LANES] = (
                _rope(dka_ref[:, sl].astype(F32), ta, 8, inverse=True).astype(BF16))
        dproj_ref[:, 2048:3072] = dva_ref[...]
        dproj_ref[:, 3072:4096] = dga_ref[...]

        lane = lax.broadcasted_iota(jnp.int32, (1, LANES), 1)
        low = lane < 64
        rope_lanes = (lane >= 64) & (lane < 96)
        dkr = jnp.zeros((tm, LANES), F32)
        for h in range(MLA_HEADS):
            sl = slice(h * LANES, (h + 1) * LANES)
            dqf_ref[:, sl] = _rope(dqb_ref[:, sl].astype(F32), tb, 16, inverse=True).astype(BF16)
            dk_h = dkb_ref[:, sl]
            dkv_ref[:, sl] = jnp.where(low, dk_h, dvb_ref[:, sl])
            dkr = dkr + jnp.where(rope_lanes, dk_h.astype(F32), 0.0)
        dkr = _rope(dkr, tb, 16, inverse=True)

        cq_hat, r_q = _rms_hat(bs_ref[:, 0:MLA_Q_RANK], MLA_Q_RANK)
        dcqn = _dot(dqf_ref[...], wuq_ref[...])
        dgcq_ref[...] += _colsum(dcqn * cq_hat)
        dproj_ref[:, COL_CQ:COL_CQ + 256] = _rms_bwd(dcqn * gcq_ref[...], cq_hat, r_q, MLA_Q_RANK).astype(BF16)
        ckv_hat, r_kv = _rms_hat(bs_ref[:, MLA_Q_RANK:MLA_Q_RANK + MLA_KV_RANK], MLA_KV_RANK)
        dckvn = _dot_nt(dkv_ref[...], wukv_ref[...])
        dgckv_ref[...] += _colsum(dckvn * ckv_hat)
        dproj_ref[:, COL_CQ + 256:COL_CQ + 384] = (
            _rms_bwd(dckvn * gckv_ref[...], ckv_hat, r_kv, MLA_KV_RANK).astype(BF16))
        dproj_ref[:, COL_CQ + 384:COL_CQ + 512] = dkr.astype(BF16)
        dproj_ref[:, COL_BG:COL_BG + 512] = dgb_ref[...]
        dproj_ref[:, COL_MQ:COL_MQ + 512] = dqm_ref[...]
        dproj_ref[:, COL_MG:COL_MG + 512] = dgm_ref[...]

    def col(width, idx):
        return pl.BlockSpec((tm, width), lambda i: (i, idx))

    def full(shape):
        return pl.BlockSpec(shape, lambda i: (0, 0))

    return pl.pallas_call(
        body, name="prep_bwd", grid=(t // tm,),
        out_shape=(jax.ShapeDtypeStruct((t, PROJ_W), BF16), jax.ShapeDtypeStruct((t, 1024), BF16),
                   jax.ShapeDtypeStruct((t, 1024), BF16),
                   jax.ShapeDtypeStruct((1, MLA_Q_RANK), F32), jax.ShapeDtypeStruct((1, MLA_KV_RANK), F32)),
        in_specs=[col(1024, 0)] * 6 + [col(512, 0), col(1024, 0), col(512, 0), col(512, 0),
                  col(512, COL_CQ // 512), pl.BlockSpec((tm, 4 * LANES), lambda i: (i, 0)),
                  full((1024, MLA_Q_RANK)), full((MLA_KV_RANK, 1024)),
                  full((1, MLA_Q_RANK)), full((1, MLA_KV_RANK)), full((8, LANES)), full((8, LANES))],
        out_specs=(col(PROJ_W, 0), col(1024, 0), col(1024, 0), full((1, MLA_Q_RANK)), full((1, MLA_KV_RANK))),
        compiler_params=_params(("arbitrary",)),
    )(dqa, dka, dva, dqb, dkb, dvb, dqm, dga, dgb, dgm, proj, trig, w_uq, w_ukv, g_cq, g_ckv, rope_a, rope_b)


def _adamw_math(gv, w, m, v):
    m_new = ADAM_B1 * m + (1.0 - ADAM_B1) * gv
    v_new = ADAM_B2 * v + (1.0 - ADAM_B2) * (gv * gv)
    m_hat = m_new / (1.0 - ADAM_B1 ** ADAM_STEP)
    v_hat = v_new / (1.0 - ADAM_B2 ** ADAM_STEP)
    return -ADAM_LR * (m_hat / (jnp.sqrt(v_hat) + ADAM_EPS) + ADAM_WD * w), m_new, v_new


def _adamw(g, w, m, v, tr, name):
    r, cols = w.shape

    def body(g_ref, w_ref, m_ref, v_ref, go_ref, d_ref, nm_ref, nv_ref):
        gv = g_ref[...]
        go_ref[...] = gv
        d_ref[...], nm_ref[...], nv_ref[...] = _adamw_math(gv, w_ref[...], m_ref[...], v_ref[...])

    tile = pl.BlockSpec((tr, cols), lambda i: (i, 0))
    shape = jax.ShapeDtypeStruct((r, cols), F32)
    return pl.pallas_call(
        body, name=name, grid=(r // tr,),
        out_shape=(shape,) * 4, in_specs=[tile] * 4, out_specs=(tile,) * 4,
        compiler_params=_params(("parallel",)),
    )(g, w, m, v)


def _adamw_pieces(g, w, m, v, pieces, name):
    shapes = [jax.ShapeDtypeStruct((r1 - r0, c1 - c0), F32) for r0, r1, c0, c1 in pieces]

    def body(g_ref, w_ref, m_ref, v_ref, *outs):
        gv = g_ref[...]
        results = (gv,) + _adamw_math(gv, w_ref[...], m_ref[...], v_ref[...])
        for kind, full in enumerate(results):
            for p, (r0, r1, c0, c1) in enumerate(pieces):
                outs[kind * len(pieces) + p][...] = full[r0:r1, c0:c1]

    flat = pl.pallas_call(
        body, name=name, out_shape=tuple(shapes) * 4,
        in_specs=[IN_VMEM] * 4, out_specs=tuple([IN_VMEM] * (4 * len(pieces))),
        compiler_params=_params(None),
    )(g, w, m, v)
    return [[flat[kind * len(pieces) + p] for kind in range(4)] for p in range(len(pieces))]


def _core_sum(g, recv, core, rows, tr, name, ride=None):
    cols = g.shape[2]
    nblk = rows // tr
    n_in = len(ride.args) if ride else 0
    n_out = len(ride.out_shapes) if ride else 0

    def body(c_ref, g_ref, r_ref, *rest):
        sf_ref, sb_ref = rest[n_in], rest[n_in + 1]
        if ride:
            j, i = pl.program_id(0), pl.program_id(1)
            ride.run((j == 0) & (i == 0), (j == 3) & (i == nblk - 1), rest[:n_in],
                     rest[n_in + 2:n_in + 2 + n_out], rest[n_in + 2 + n_out:])
        tot = g_ref[...] + r_ref[...]
        sf_ref[...] = tot
        sb_ref[...] = tot.astype(BF16)

    half = pl.BlockSpec((None, tr, cols), lambda j, i, c_ref: (j, i, 0))
    shapes = (jax.ShapeDtypeStruct((4, rows, cols), F32), jax.ShapeDtypeStruct((4, rows, cols), BF16))
    return pl.pallas_call(
        body, name=name,
        grid_spec=pltpu.PrefetchScalarGridSpec(
            num_scalar_prefetch=1, grid=(4, nblk),
            in_specs=[pl.BlockSpec((None, tr, cols), lambda j, i, c_ref: (j, c_ref[0] * nblk + i, 0)), half]
            + (ride.in_specs if ride else []),
            out_specs=(half, half) + (ANY,) * n_out,
            scratch_shapes=ride.scratch() if ride else []),
        out_shape=shapes + tuple(ride.out_shapes if ride else ()),
        compiler_params=_params(("arbitrary", "arbitrary") if ride else ("parallel", "parallel")),
    )(core, g, recv, *(ride.args if ride else ()))


def _half_to_sibling(g4):
    def plan(in_refs, out_refs, send_sems, recv_sems):
        x, y, c = _position()
        cp = pltpu.make_async_remote_copy(
            src_ref=in_refs[0].at[:, 1 - c], dst_ref=out_refs[0], send_sem=send_sems.at[0],
            recv_sem=recv_sems.at[0], device_id=(x, y, 1 - c), device_id_type=MESH)

        def finish():
            cp.wait_recv()
            cp.wait_send()

        return cp.start, finish

    return _Ride([g4], [jax.ShapeDtypeStruct((4, g4.shape[2], 1024), F32)], (1, 1), plan)


def _gather_plan(src_ref, dst_ref, send_sems, recv_sems, local_sems):
    x, y, c = _position()
    me = 2 * x + y
    local = pltpu.make_async_copy(src_ref, dst_ref.at[me], local_sems.at[0])

    def over_ici(k, src, chip):
        return pltpu.make_async_remote_copy(
            src_ref=src, dst_ref=dst_ref.at[chip, c], send_sem=send_sems.at[k - 1], recv_sem=recv_sems.at[k - 1],
            device_id=(x ^ (k >> 1), y ^ (k & 1), c), device_id_type=MESH)

    def to_sibling(k, half):
        piece = dst_ref.at[me ^ k, half]
        return pltpu.make_async_remote_copy(
            src_ref=piece, dst_ref=piece, send_sem=send_sems.at[2 + k], recv_sem=recv_sems.at[2 + k],
            device_id=(x, y, 1 - c), device_id_type=MESH)

    sends = [over_ici(k, src_ref.at[c], me) for k in (1, 2, 3)]

    def start():
        local.start()
        for cp in sends:
            cp.start()

    def relay():
        for k in (1, 2, 3):
            over_ici(k, dst_ref.at[me ^ k, c], me ^ k).wait_recv()
            to_sibling(k, c).start()

    def finish():
        for k in (1, 2, 3):
            to_sibling(k, 1 - c).wait_recv()
        for cp in sends + [to_sibling(k, c) for k in (1, 2, 3)]:
            cp.wait_send()
        local.wait()

    return start, relay, finish


def _gather_ride(shard):
    def plan(in_refs, out_refs, send_sems, recv_sems, local_sems):
        return _gather_plan(in_refs[0], out_refs[0], send_sems, recv_sems, local_sems)

    return _Ride([shard], [jax.ShapeDtypeStruct((4,) + shard.shape, shard.dtype)], (6, 6, 1), plan,
                 in_specs=[IN_VMEM])


def _chip_sum(sf, recv, chip, rows, tr, name):
    cols = sf.shape[2]

    def body(me_ref, sf_ref, r_ref, out_ref):
        acc = sf_ref[...]
        for k in range(3):
            acc = acc + r_ref[k].astype(F32)
        out_ref[...] = acc

    return pl.pallas_call(
        body, name=name,
        grid_spec=pltpu.PrefetchScalarGridSpec(
            num_scalar_prefetch=1, grid=(rows // tr,),
            in_specs=[pl.BlockSpec((None, tr, cols), lambda i, me_ref: (me_ref[0], i, 0)),
                      pl.BlockSpec((3, tr, cols), lambda i, me_ref: (0, i, 0))],
            out_specs=pl.BlockSpec((tr, cols), lambda i, me_ref: (i, 0))),
        out_shape=jax.ShapeDtypeStruct((rows, cols), F32),
        compiler_params=_params(("parallel",)),
    )(chip, sf, recv)


def _position():
    return lax.axis_index("x"), lax.axis_index("y"), lax.axis_index("c")


def _dh_scatter(dproj, w_in_arr_t, x, dz, g, sb_in, sb_rest, tm=1024, tk=1024):
    t, d = x.shape
    nk = dproj.shape[1] // tk
    ni = t // tm

    def body(dp_ref, w_ref, x_ref, dz_ref, g_ref, sbin_ref, sbrest_ref,
             dx_ref, dg_ref, db_ref, rin_ref, rrest_ref, acc_ref, send_sems, recv_sems):
        i = pl.program_id(0)
        kk = pl.program_id(1)
        px, py, pc = _position()
        me = 2 * px + py
        srcs = (sbin_ref, sbrest_ref)
        dsts = (rin_ref, rrest_ref)

        def copy(a, k):
            return pltpu.make_async_remote_copy(
                src_ref=srcs[a].at[me ^ k], dst_ref=dsts[a].at[k - 1],
                send_sem=send_sems.at[3 * a + k - 1], recv_sem=recv_sems.at[3 * a + k - 1],
                device_id=(px ^ (k >> 1), py ^ (k & 1), pc), device_id_type=MESH)

        pairs = [(a, k) for a in range(2) for k in (1, 2, 3)]

        @pl.when((i == 0) & (kk == 0))
        def _():
            dg_ref[...] = jnp.zeros_like(dg_ref)
            db_ref[...] = jnp.zeros_like(db_ref)
            for a, k in pairs:
                copy(a, k).start()

        part = _dot(dp_ref[...], w_ref[...])

        @pl.when(kk == 0)
        def _():
            acc_ref[...] = part

        @pl.when(kk > 0)
        def _():
            acc_ref[...] += part

        @pl.when(kk == nk - 1)
        def _():
            xh, rstd = _ln_hat(x_ref[...])
            dht = acc_ref[...] + DEEPNORM_ALPHA * dz_ref[...]
            dg_ref[...] += _colsum(dht * xh)
            db_ref[...] += _colsum(dht)
            dx_ref[...] = _ln_bwd_rows(dht * g_ref[...], xh, rstd)

        @pl.when((i == ni - 1) & (kk == nk - 1))
        def _():
            for a, k in pairs:
                copy(a, k).wait_recv()
            for a, k in pairs:
                copy(a, k).wait_send()

    tile = pl.BlockSpec((tm, d), lambda i, kk: (i, 0))
    row = pl.BlockSpec((1, d), lambda i, kk: (0, 0))
    return pl.pallas_call(
        body, name="dh_scatter", grid=(ni, nk),
        out_shape=(jax.ShapeDtypeStruct((t, d), F32), jax.ShapeDtypeStruct((1, d), F32),
                   jax.ShapeDtypeStruct((1, d), F32),
                   jax.ShapeDtypeStruct((3, HALF_IN, 1024), BF16),
                   jax.ShapeDtypeStruct((3, HALF_REST, 1024), BF16)),
        in_specs=[pl.BlockSpec((tm, tk), lambda i, kk: (i, kk)), pl.BlockSpec((tk, d), lambda i, kk: (kk, 0)),
                  tile, tile, row, ANY, ANY],
        out_specs=(tile, row, row, ANY, ANY),
        scratch_shapes=[pltpu.VMEM((tm, d), F32), pltpu.SemaphoreType.DMA((6,)), pltpu.SemaphoreType.DMA((6,))],
        compiler_params=_params(("arbitrary", "arbitrary")),
    )(dproj, w_in_arr_t, x, dz, g, sb_in, sb_rest)


def _join_halves(gh_in, gh_rest):
    def body(hin_ref, hrest_ref, oin_ref, orest_ref, send_sems, recv_sems, local_sems):
        x, y, c = _position()
        srcs = (hin_ref, hrest_ref)
        dsts = (oin_ref, orest_ref)

        def rows(a, half):
            return dsts[a].at[half]

        local = [pltpu.make_async_copy(srcs[a], rows(a, c), local_sems.at[a]) for a in range(2)]
        remote = [pltpu.make_async_remote_copy(
            src_ref=srcs[a], dst_ref=rows(a, c), send_sem=send_sems.at[a], recv_sem=recv_sems.at[a],
            device_id=(x, y, 1 - c), device_id_type=MESH) for a in range(2)]
        for cp in local + remote:
            cp.start()
        for a in range(2):
            pltpu.make_async_remote_copy(
                src_ref=srcs[a], dst_ref=rows(a, 1 - c), send_sem=send_sems.at[a], recv_sem=recv_sems.at[a],
                device_id=(x, y, 1 - c), device_id_type=MESH).wait_recv()
        for cp in remote:
            cp.wait_send()
        for cp in local:
            cp.wait()

    return pl.pallas_call(
        body, name="join_halves",
        out_shape=(jax.ShapeDtypeStruct((2, HALF_IN, 1024), F32),
                   jax.ShapeDtypeStruct((2, HALF_REST, 1024), F32)),
        in_specs=[IN_VMEM, IN_VMEM], out_specs=(ANY, ANY),
        scratch_shapes=[pltpu.SemaphoreType.DMA((2,)), pltpu.SemaphoreType.DMA((2,)), pltpu.SemaphoreType.DMA((2,))],
    )(gh_in, gh_rest)


def _allreduce_small(vec):
    def body(vec_ref, out_ref, all_ref, send_sems, recv_sems):
        x, y, c = _position()
        me = 4 * x + 2 * y + c
        all_ref[me] = vec_ref[...]

        def copy(k, slot):
            return pltpu.make_async_remote_copy(
                src_ref=vec_ref, dst_ref=all_ref.at[slot], send_sem=send_sems.at[k - 1], recv_sem=recv_sems.at[k - 1],
                device_id=(x ^ (k >> 2), y ^ ((k >> 1) & 1), c ^ (k & 1)), device_id_type=MESH)

        copies = [copy(k, me) for k in range(1, 8)]
        for cp in copies:
            cp.start()
        for k in range(1, 8):
            copy(k, me ^ k).wait_recv()
        for cp in copies:
            cp.wait_send()
        total = all_ref[0]
        for d in range(1, 8):
            total = total + all_ref[d]
        out_ref[...] = total

    return pl.pallas_call(
        body, name="allreduce_small",
        out_shape=jax.ShapeDtypeStruct(vec.shape, vec.dtype),
        in_specs=[pl.BlockSpec(memory_space=pltpu.VMEM)], out_specs=pl.BlockSpec(memory_space=pltpu.VMEM),
        scratch_shapes=[pltpu.VMEM((8,) + vec.shape, vec.dtype), pltpu.SemaphoreType.DMA((7,)),
                        pltpu.SemaphoreType.DMA((7,))],
    )(vec)


def _pack_rest(w_uq, w_ukv, w_mem, w_out):
    rows = jnp.concatenate([w_uq[0].T.reshape(-1, 1024), w_ukv.reshape(-1, 1024), w_mem.reshape(-1, 1024),
                            w_out.reshape(-1, 1024)], axis=0)
    return jnp.pad(rows, ((0, ROWS_REST - ROWS_USED), (0, 0)))


def _arranged_w_in(g_in):
    z = functools.partial(jnp.zeros, dtype=g_in.dtype)
    cut = 4480 - 2 * SHARD_ROWS
    return jnp.concatenate(
        [g_in[0, :SHARD_ROWS], g_in[1, :SHARD_ROWS], g_in[2, :cut], z((64, 1024)), g_in[2, cut:cut + 32],
         z((32, 1024)), g_in[2, cut + 32:SHARD_ROWS], g_in[3, :SHARD_ROWS]], axis=0)


def _rest_weights(g_rest):
    w_uq_t = g_rest[:, 0:ROWS_UQ].reshape(768, 256)
    w_uq_pad_t = jnp.pad(w_uq_t.reshape(MLA_HEADS, MLA_QK_DIM, 256), ((0, 0), (0, 32), (0, 0))).reshape(1024, 256)
    w_ukv = jnp.concatenate([g_rest[j, ROWS_UQ:ROWS_UQ + ROWS_UKV].reshape(128, 256) for j in range(4)], axis=1)
    lo = ROWS_UQ + ROWS_UKV
    w_mem = g_rest[:, lo:lo + ROWS_MEM].reshape(4 * ROWS_MEM, 1024)
    w_out = g_rest[:, lo + ROWS_MEM:lo + ROWS_MEM + ROWS_OUT].reshape(4 * ROWS_OUT, 1024)
    return w_uq_pad_t, w_ukv, w_mem, w_out


def _split_in(dw_in_arr_t):
    a = dw_in_arr_t
    gap = jnp.zeros((ROWS_IN - SHARD_ROWS, 1024), a.dtype)
    nat = 4608 - 96
    pieces = [a[:SHARD_ROWS], gap, a[SHARD_ROWS:2 * SHARD_ROWS], gap,
              a[2 * SHARD_ROWS:4480], a[4544:4576], a[4608:4608 + 3 * SHARD_ROWS - nat], gap,
              a[4608 + 3 * SHARD_ROWS - nat:], gap]
    return jnp.concatenate(pieces, axis=0).reshape(4, ROWS_IN, 1024)


def _split_rest(dw_uq_pad_t, dw_ukv, dw_mem, dw_out):
    dw_uq_t = dw_uq_pad_t.reshape(MLA_HEADS, LANES, 256)[:, :MLA_QK_DIM].reshape(4, ROWS_UQ, 1024)
    parts = [dw_uq_t, dw_ukv.reshape(128, 4, 256).transpose(1, 0, 2).reshape(4, ROWS_UKV, 1024),
             dw_mem.reshape(4, ROWS_MEM, 1024), dw_out.reshape(4, ROWS_OUT, 1024)]
    return jnp.pad(jnp.concatenate(parts, axis=1), ((0, 0), (0, ROWS_REST - ROWS_USED), (0, 0)))


def _rope_consts(rot, first, period):
    half = rot // 2
    inv_freq = np.float32(ROPE_THETA) ** (-(np.arange(0, rot, 2, dtype=np.float32) / np.float32(rot)))
    lane = np.arange(LANES) % period - first
    in_rot = (lane >= 0) & (lane < rot)
    out = np.zeros((8, LANES), np.float32)
    out[0] = np.where(in_rot, inv_freq[np.clip(lane, 0, rot - 1) % half], 0.0)
    out[1] = in_rot & (lane < half)
    out[2] = in_rot & (lane >= half)
    return jnp.asarray(out)


def _band_bias(s):
    nblk = s // BAND_Q
    starts = np.array([_band_start(i, s) for i in range(nblk)])
    uq = (np.arange(nblk)[:, None] * BAND_Q + np.arange(BAND_Q)[None, :])[:, :, None]
    uk = (starts[:, None] + np.arange(BAND_WIN)[None, :])[:, None, :]
    tiles, index, seen = [], [], {}
    for _, d in DILATED:
        length = s // d
        ok = (uq // length == uk // length) & (np.abs(uq - uk) <= 64)
        row = []
        for i in range(nblk):
            key = ok[i].tobytes()
            if key not in seen:
                seen[key] = len(tiles)
                tiles.append(np.where(ok[i], 0.0, NEG_INF).astype(np.float32))
            row.append(seen[key])
        index.append(row)
    return jnp.asarray(np.stack(tiles, axis=0)), index


def _forward_backward(h, proj, trig, rope_consts, x, mem, target, weights, gains):
    w_uq_pad_t, w_ukv, w_mem, w_out = weights
    g_emb, b_emb, g_cq, g_ckv, g_out_a, g_out_b, g_out_m, g_post, b_post = gains
    nb, s, d = x.shape
    t = nb * s
    x2 = x.reshape(t, d)
    mem2 = mem.reshape(nb * N_MEM, d)
    tgt2 = target.reshape(t, d)
    rope_a, rope_b = rope_consts
    bias, bias_index = _band_bias(s)
    scales = (0.125, MLA_QK_DIM ** -0.5, 128 ** -0.5)

    qa, ka, va, qb, kb, vb, qm, cqn, ckvn = _prep(proj, trig, w_uq_pad_t, w_ukv, g_cq, g_ckv, rope_a, rope_b, scales)
    mkv = _mm(mem2, w_mem, BF16, nb * N_MEM, 1024, 1024, "mem_kv")

    cfg_b = dict(nb=nb, s=s, sk=s, heads=8, voff=0, bq=256)
    cfg_m = dict(nb=nb, s=s, sk=N_MEM, heads=4, hpb=2, voff=4, bq=1024)
    ya, lse_a = _dilated_fwd(qa, ka, va, bias, bias_index, nb=nb, s=s, name="attn_a_fwd")
    yb, lse_b = _attn_fwd(qb, kb, vb, name="attn_b_fwd", hpb=4, **cfg_b)
    ym, lse_m = _attn_fwd(qm, mkv, mkv, name="attn_m_fwd", **cfg_m)

    (y, dz, doa, dob, dom, dga, dgb, dgm, loss, dg_post, db_post, dg_a, dg_b, dg_m) = _post(
        x2, ya, yb, ym, proj, tgt2, w_out, g_emb, b_emb, g_out_a, g_out_b, g_out_m, g_post, b_post)

    dqa, dka, dva = _dilated_bwd(qa, ka, va, ya, doa, lse_a, bias, bias_index, nb=nb, s=s, scale=scales[0],
                                 name="attn_a_bwd")
    dqb, dkb, dvb = _attn_bwd(qb, kb, vb, yb, dob, lse_b, name="attn_b_bwd", scale=scales[1], hpb=2, **cfg_b)
    dqm, dmk, dmv = _attn_bwd(qm, mkv, mkv, ym, dom, lse_m, name="attn_m_bwd", scale=scales[2], **cfg_m)
    dmkv = jnp.concatenate([dmk, dmv], axis=1)

    dproj, dqf, dkv, dg_cq, dg_ckv = _prep_bwd(
        dqa, dka, dva, dqb, dkb, dvb, dqm, dga, dgb, dgm, proj, trig, w_uq_pad_t, w_ukv, g_cq, g_ckv, rope_a, rope_b)

    small_rows = (dg_cq, dg_ckv, loss, dg_a, dg_b, dg_m, dg_post, db_post)
    return (dproj, h, y, dz, dqf, cqn, ckvn, dkv, mem2, dmkv), x2, small_rows


def _weight_grads(operands, core):
    dproj, h, y, dz, dqf, cqn, ckvn, dkv, mem2, dmkv = operands
    dw_in_arr_t = _mm(dproj, h, F32, 1024, 1024, 1024, "dw_in", mode="tn")
    g_in = _split_in(dw_in_arr_t)
    dw_out, r_in = _mm(y, dz, F32, 1024, 1024, 1024, "dw_out", mode="tn",
                       ride=_half_to_sibling(g_in.reshape(4, 2, HALF_IN, 1024)))
    dw_uq_pad_t = _mm(dqf, cqn, F32, 1024, 256, 1024, "dw_uq", mode="tn")
    dw_ukv = _mm(ckvn, dkv, F32, 128, 1024, 1024, "dw_ukv", mode="tn")
    dw_mem = _mm(mem2, dmkv, F32, 1024, 1024, mem2.shape[0], "dw_mem", mode="tn")
    g_rest = _split_rest(dw_uq_pad_t, dw_ukv, dw_mem, dw_out)
    sf_in, sb_in, r_rest = _core_sum(g_in, r_in, core, HALF_IN, HALF_IN // 2, "core_sum_in",
                                     ride=_half_to_sibling(g_rest.reshape(4, 2, HALF_REST, 1024)))
    sf_rest, sb_rest = _core_sum(g_rest, r_rest, core, HALF_REST, HALF_REST, "core_sum_rest")
    return sf_in, sb_in, sf_rest, sb_rest


def _small_block(dg_emb, db_emb, small_rows):
    dg_cq, dg_ckv, loss, dg_a, dg_b, dg_m, dg_post, db_post = small_rows
    row2 = jnp.concatenate([dg_cq, dg_ckv, loss, jnp.zeros((1, 512), F32)], axis=1)
    return jnp.concatenate([dg_emb, db_emb, row2, dg_a, jnp.concatenate([dg_b, dg_m], axis=1), dg_post, db_post,
                            jnp.zeros((1, 1024), F32)], axis=0)


def _pack_small(g_emb, b_emb, g_cq, g_ckv, g_out_a, g_out_b, g_out_m, g_post, b_post):
    row2 = jnp.concatenate([g_cq.reshape(1, -1), g_ckv.reshape(1, -1), jnp.zeros((1, 640), F32)], axis=1)
    return jnp.concatenate([g_emb.reshape(1, -1), b_emb.reshape(1, -1), row2, g_out_a.reshape(1, -1),
                            jnp.concatenate([g_out_b.reshape(1, -1), g_out_m.reshape(1, -1)], axis=1),
                            g_post.reshape(1, -1), b_post.reshape(1, -1), jnp.zeros((1, 1024), F32)], axis=0)


def kernel(x, mem, positions, g_emb, b_emb, w_in, g_cq, g_ckv, w_uq, w_ukv, w_mem_kv, g_out_a, g_out_b, g_out_m, w_out, g_post, b_post, loss_target, m_g_emb, m_b_emb, m_w_in, m_g_cq, m_g_ckv, m_w_uq, m_w_ukv, m_w_mem_kv, m_g_out_a, m_g_out_b, m_g_out_m, m_w_out, m_g_post, m_b_post, v_g_emb, v_b_emb, v_w_in, v_g_cq, v_g_ckv, v_w_uq, v_w_ukv, v_w_mem_kv, v_g_out_a, v_g_out_b, v_g_out_m, v_w_out, v_g_post, v_b_post):
    w_rest = _pack_rest(w_uq, w_ukv, w_mem_kv, w_out)
    w_in_t = w_in[0].T
    w_in_b = jnp.pad(w_in_t.astype(BF16), ((0, ROWS_IN - SHARD_ROWS), (0, 0)))
    gains = (g_emb.reshape(1, -1), b_emb.reshape(1, -1), g_cq, g_ckv, g_out_a, g_out_b, g_out_m, g_post, b_post)
    rope_consts = (_rope_consts(16, 0, 64), _rope_consts(32, 64, 128))
    h, trig, gathered_in = _ln_fwd(x.reshape(-1, D_MODEL), gains[0], gains[1],
                                   positions.reshape(-1, 1).astype(F32), *rope_consts,
                                   ride=_gather_ride(w_in_b.reshape(2, HALF_IN, 1024)))
    w_in_arr_t = _arranged_w_in(gathered_in.reshape(4, ROWS_IN, 1024))
    proj, gathered_rest = _mm(h, w_in_arr_t, F32, 1024, 1024, 1024, "in_proj", mode="nt",
                              ride=_gather_ride(w_rest.astype(BF16).reshape(2, HALF_REST, 1024)))
    weights = _rest_weights(gathered_rest.reshape(4, ROWS_REST, 1024))
    operands, x2, small_rows = _forward_backward(h, proj, trig, rope_consts, x, mem, loss_target, weights, gains)

    core = lax.axis_index("c").astype(jnp.int32).reshape(1)
    chip = (2 * lax.axis_index("x") + lax.axis_index("y")).astype(jnp.int32).reshape(1)
    sf_in, sb_in, sf_rest, sb_rest = _weight_grads(operands, core)
    grad_x, dg_emb, db_emb, rb_in, rb_rest = _dh_scatter(operands[0], w_in_arr_t, x2, operands[3], gains[0],
                                                         sb_in, sb_rest)
    gh_in = _chip_sum(sf_in, rb_in, chip, HALF_IN, HALF_IN // 2, "chip_sum_in")
    gh_rest = _chip_sum(sf_rest, rb_rest, chip, HALF_REST, HALF_REST, "chip_sum_rest")
    grad_in, grad_rest = _join_halves(gh_in, gh_rest)
    grad_in = grad_in.reshape(ROWS_IN, 1024)
    grad_rest = grad_rest.reshape(ROWS_REST, 1024)

    big_in = _adamw(grad_in, w_in_t, m_w_in[0].T, v_w_in[0].T, SHARD_ROWS // 3, "adamw_in")
    uq, ukv, wmem, wout = _adamw_pieces(
        grad_rest, w_rest, _pack_rest(m_w_uq, m_w_ukv, m_w_mem_kv, m_w_out),
        _pack_rest(v_w_uq, v_w_ukv, v_w_mem_kv, v_w_out), REST_PIECES, "adamw_rest")
    small_sum = _allreduce_small(_small_block(dg_emb, db_emb, small_rows))
    sm = _adamw_pieces(
        small_sum,
        _pack_small(g_emb, b_emb, g_cq, g_ckv, g_out_a, g_out_b, g_out_m, g_post, b_post),
        _pack_small(m_g_emb, m_b_emb, m_g_cq, m_g_ckv, m_g_out_a, m_g_out_b, m_g_out_m, m_g_post, m_b_post),
        _pack_small(v_g_emb, v_b_emb, v_g_cq, v_g_ckv, v_g_out_a, v_g_out_b, v_g_out_m, v_g_post, v_b_post),
        SMALL_PIECES, "adamw_small")
    loss = small_sum[2, 384]

    def ordered(kind):
        s_gemb, s_bemb, s_gcq, s_gckv, s_ga, s_gb, s_gm, s_gpost, s_bpost = [piece[kind] for piece in sm]
        return [s_gemb.reshape(-1), s_bemb.reshape(-1), big_in[kind].T[None], s_gcq, s_gckv,
                uq[kind].reshape(192, 256).T[None], ukv[kind].reshape(1, 128, 256), wmem[kind][None], s_ga, s_gb,
                s_gm, wout[kind][None], s_gpost, s_bpost]

    return (loss, grad_x.reshape(x.shape), *ordered(0), *ordered(1), *ordered(2), *ordered(3))
```

```python
import functools
import math

import jax
import jax.numpy as jnp
import numpy as np
from jax import lax
from jax.experimental import pallas as pl
from jax.experimental.pallas import tpu as pltpu

F32 = jnp.float32
BF16 = jnp.bfloat16
MESH = pl.DeviceIdType.MESH
ANY = pl.BlockSpec(memory_space=pl.ANY)
IN_VMEM = pl.BlockSpec(memory_space=pltpu.VMEM)

D_MODEL = 1024
A_WIDTH = 1024
MLA_HEADS = 8
MLA_Q_RANK = 256
MLA_KV_RANK = 128
MLA_QK_DIM = 96
MEM_WIDTH = 512
N_MEM = 256
ROPE_THETA = 500000.0
NORM_EPS = 1e-5
NEG_INF = -1e30
DEEPNORM_ALPHA = 2.0 ** 0.25
DILATED = ((64, 1), (256, 4), (1024, 16))

ADAM_LR = 0.001
ADAM_B1 = 0.9
ADAM_B2 = 0.999
ADAM_EPS = 1e-08
ADAM_WD = 0.01
ADAM_STEP = 10

LANES = 128
VMEM_LIMIT = 56 * 1024 * 1024
LOG2E = math.log2(math.e)
LN2 = math.log(2.0)

PROJ_W = 6144
COL_CQ = 4096
COL_BG = 4608
COL_MQ = 5120
COL_MG = 5632

SHARD_ROWS = 1512
ROWS_IN = 1536
ROWS_UQ, ROWS_UKV, ROWS_MEM, ROWS_OUT = 48, 32, 256, 512
ROWS_USED = ROWS_UQ + ROWS_UKV + ROWS_MEM + ROWS_OUT
ROWS_REST = 864
HALF_IN = ROWS_IN // 2
HALF_REST = ROWS_REST // 2
REST_PIECES = ((0, 48, 0, 1024), (48, 80, 0, 1024), (80, 336, 0, 1024), (336, 848, 0, 1024))
SMALL_PIECES = ((0, 1, 0, 1024), (1, 2, 0, 1024), (2, 3, 0, 256), (2, 3, 256, 384), (3, 4, 0, 1024), (4, 5, 0, 512),
                (4, 5, 512, 1024), (5, 6, 0, 1024), (6, 7, 0, 1024))


def _params(sem=None, vmem=VMEM_LIMIT):
    return pltpu.CompilerParams(dimension_semantics=sem, vmem_limit_bytes=vmem)


def _dot(a, b):
    return jnp.dot(a, b, preferred_element_type=F32)


def _dot_nt(a, b):
    return lax.dot_general(a, b, (((1,), (1,)), ((), ())), preferred_element_type=F32)


def _dot_tn(a, b):
    return lax.dot_general(a, b, (((0,), (0,)), ((), ())), preferred_element_type=F32)


def _ln_hat(x):
    mu = jnp.mean(x, axis=-1, keepdims=True)
    xc = x - mu
    var = jnp.mean(xc * xc, axis=-1, keepdims=True)
    rstd = lax.rsqrt(var + NORM_EPS)
    return xc * rstd, rstd


def _ln_bwd_rows(dxh, xh, rstd):
    return rstd * (dxh - jnp.mean(dxh, axis=-1, keepdims=True) - xh * jnp.mean(dxh * xh, axis=-1, keepdims=True))


def _rms_hat(x, width):
    ms = jnp.sum(x * x, axis=-1, keepdims=True) * (1.0 / width)
    r = lax.rsqrt(ms + NORM_EPS)
    return x * r, r


def _rms_bwd(u, xh, r, width):
    return r * (u - xh * (jnp.sum(u * xh, axis=-1, keepdims=True) * (1.0 / width)))


def _colsum(v):
    return jnp.sum(v, axis=0, keepdims=True)


def _rope_tables(cos, sin, consts):
    return cos, sin * consts[2:3, :], -sin * consts[1:2, :]


def _rope(x, tables, half, inverse=False):
    c, s_up, s_dn = tables
    if inverse:
        s_up, s_dn = -s_up, -s_dn
    return x * c + pltpu.roll(x, half, 1) * s_up + pltpu.roll(x, LANES - half, 1) * s_dn


def _ln_fwd(x, g, b, pos, rope_a, rope_b, tm=512, ride=None):
    t, d = x.shape
    n_in = len(ride.args) if ride else 0
    n_out = len(ride.out_shapes) if ride else 0
    steps = t // tm

    def body(x_ref, g_ref, b_ref, pos_ref, ra_ref, rb_ref, *rest):
        h_ref, trig_ref = rest[n_in], rest[n_in + 1]
        if ride:
            i = pl.program_id(0)
            ride.run(i == 0, i == steps - 1, rest[:n_in], rest[n_in + 2:n_in + 2 + n_out], rest[n_in + 2 + n_out:])
        xh, _ = _ln_hat(x_ref[...])
        h_ref[...] = (xh * g_ref[...] + b_ref[...]).astype(BF16)
        for j, consts in enumerate((ra_ref, rb_ref)):
            ang = pos_ref[...] * consts[0:1, :]
            trig_ref[:, 2 * j * LANES:(2 * j + 1) * LANES] = jnp.cos(ang)
            trig_ref[:, (2 * j + 1) * LANES:(2 * j + 2) * LANES] = jnp.sin(ang)

    row = pl.BlockSpec((1, d), lambda i: (0, 0))
    tile = pl.BlockSpec((tm, d), lambda i: (i, 0))
    consts = pl.BlockSpec((8, LANES), lambda i: (0, 0))
    trig_tile = pl.BlockSpec((tm, 4 * LANES), lambda i: (i, 0))
    in_specs = [tile, row, row, pl.BlockSpec((tm, 1), lambda i: (i, 0)), consts, consts]
    shapes = (jax.ShapeDtypeStruct((t, d), BF16), jax.ShapeDtypeStruct((t, 4 * LANES), F32))
    if not ride:
        return pl.pallas_call(
            body, name="ln_fwd", grid=(steps,), out_shape=shapes, in_specs=in_specs, out_specs=(tile, trig_tile),
            compiler_params=_params(("parallel",)),
        )(x, g, b, pos, rope_a, rope_b)
    return pl.pallas_call(
        body, name="ln_fwd", grid=(steps,),
        out_shape=(*shapes, *ride.out_shapes),
        in_specs=in_specs + ride.in_specs, out_specs=(tile, trig_tile) + (ANY,) * n_out,
        scratch_shapes=ride.scratch(),
        compiler_params=_params(("arbitrary",)),
    )(x, g, b, pos, rope_a, rope_b, *ride.args)


class _Ride:
    def __init__(self, args, out_shapes, sem_counts, plan, in_specs=None):
        self.args, self.out_shapes, self.plan = list(args), list(out_shapes), plan
        self.sem_counts = sem_counts
        self.in_specs = in_specs or [ANY] * len(self.args)

    def scratch(self):
        return [pltpu.SemaphoreType.DMA((n,)) for n in self.sem_counts]

    def run(self, first, last, in_refs, out_refs, sems, middle=None):
        def stage(k):
            stages = self.plan(in_refs, out_refs, *sems)
            if k == 0 or len(stages) == 3:
                return stages[k]
            return (lambda: None) if k == 1 else stages[1]

        @pl.when(first)
        def _():
            stage(0)()

        if middle is not None:
            @pl.when(middle)
            def _():
                stage(1)()

        @pl.when(last)
        def _():
            if middle is None:
                stage(1)()
            stage(2)()


def _mm(a, b, out_dtype, tm, tn, tk, name, mode="nn", ride=None):
    if mode == "tn":
        k, m = a.shape
    else:
        m, k = a.shape
    n = b.shape[0] if mode == "nt" else b.shape[1]
    nk = k // tk
    nj, ni = n // tn, m // tm
    n_in = len(ride.args) if ride else 0
    n_out = len(ride.out_shapes) if ride else 0

    def body(a_ref, b_ref, *rest):
        o_ref = rest[n_in]
        acc_ref = rest[n_in + 1 + n_out]
        if ride:
            j, i, kk = pl.program_id(0), pl.program_id(1), pl.program_id(2)
            step = (j * ni + i) * nk + kk
            total = nj * ni * nk
            ride.run(step == 0, step == total - 1, rest[:n_in], rest[n_in + 1:n_in + 1 + n_out],
                     rest[n_in + 2 + n_out:], middle=(step == (2 * total) // 3) if total >= 3 else None)
        av = a_ref[...].astype(BF16)
        bv = b_ref[...].astype(BF16)
        part = _dot_tn(av, bv) if mode == "tn" else _dot_nt(av, bv) if mode == "nt" else _dot(av, bv)
        if nk == 1:
            o_ref[...] = part.astype(out_dtype)
        else:
            kk = pl.program_id(2)

            @pl.when(kk == 0)
            def _():
                acc_ref[...] = part

            @pl.when(kk > 0)
            def _():
                acc_ref[...] += part

            @pl.when(kk == nk - 1)
            def _():
                o_ref[...] = acc_ref[...].astype(out_dtype)

    a_spec = (pl.BlockSpec((tk, tm), lambda j, i, kk: (kk, i)) if mode == "tn"
              else pl.BlockSpec((tm, tk), lambda j, i, kk: (i, kk)))
    b_spec = (pl.BlockSpec((tn, tk), lambda j, i, kk: (j, kk)) if mode == "nt"
              else pl.BlockSpec((tk, tn), lambda j, i, kk: (kk, j)))
    o_spec = pl.BlockSpec((tm, tn), lambda j, i, kk: (i, j))
    o_shape = jax.ShapeDtypeStruct((m, n), out_dtype)
    if not ride:
        return pl.pallas_call(
            body, name=name, grid=(nj, ni, nk), out_shape=o_shape, in_specs=[a_spec, b_spec], out_specs=o_spec,
            scratch_shapes=[pltpu.VMEM((tm, tn), F32)],
            compiler_params=_params(("parallel", "parallel", "arbitrary")),
        )(a, b)
    return pl.pallas_call(
        body, name=name, grid=(nj, ni, nk),
        out_shape=(o_shape, *ride.out_shapes),
        in_specs=[a_spec, b_spec] + ride.in_specs,
        out_specs=(o_spec,) + (ANY,) * n_out,
        scratch_shapes=[pltpu.VMEM((tm, tn), F32)] + ride.scratch(),
        compiler_params=_params(("arbitrary", "arbitrary", "arbitrary")),
    )(a, b, *ride.args)


def _prep(proj, trig, w_uq, w_ukv, g_cq, g_ckv, rope_a, rope_b, scales, tm=256):
    t = proj.shape[0]
    sc_a, sc_b, sc_m = (s * LOG2E for s in scales)

    def body(aq_ref, ak_ref, av_ref, bs_ref, mq_ref, trig_ref, wuq_ref, wukv_ref, gcq_ref, gckv_ref,
             ra_ref, rb_ref, qa_ref, ka_ref, va_ref, qb_ref, kb_ref, vb_ref, qm_ref, cqn_ref, ckvn_ref):
        ta = _rope_tables(trig_ref[:, 0:LANES], trig_ref[:, LANES:2 * LANES], ra_ref[...])
        tb = _rope_tables(trig_ref[:, 2 * LANES:3 * LANES], trig_ref[:, 3 * LANES:4 * LANES], rb_ref[...])
        for j in range(A_WIDTH // LANES):
            sl = slice(j * LANES, (j + 1) * LANES)
            qa_ref[:, sl] = (_rope(aq_ref[:, sl], ta, 8) * sc_a).astype(BF16)
            ka_ref[:, sl] = _rope(ak_ref[:, sl], ta, 8).astype(BF16)
        va_ref[...] = av_ref[...].astype(BF16)
        qm_ref[...] = (mq_ref[...] * sc_m).astype(BF16)

        cq_hat, _ = _rms_hat(bs_ref[:, 0:MLA_Q_RANK], MLA_Q_RANK)
        cqn = (cq_hat * gcq_ref[...]).astype(BF16)
        cqn_ref[...] = cqn
        ckv_hat, _ = _rms_hat(bs_ref[:, MLA_Q_RANK:MLA_Q_RANK + MLA_KV_RANK], MLA_KV_RANK)
        ckvn = (ckv_hat * gckv_ref[...]).astype(BF16)
        ckvn_ref[...] = ckvn
        qfull = _dot_nt(cqn, wuq_ref[...])
        kv = _dot(ckvn, wukv_ref[...])
        kr = _rope(bs_ref[:, 384:512], tb, 16)
        lane = lax.broadcasted_iota(jnp.int32, (1, LANES), 1)
        low = lane < 64
        for h in range(MLA_HEADS):
            sl = slice(h * LANES, (h + 1) * LANES)
            qb_ref[:, sl] = (_rope(qfull[:, sl], tb, 16) * sc_b).astype(BF16)
            kb_ref[:, sl] = jnp.where(low, kv[:, sl], kr).astype(BF16)
            vb_ref[:, sl] = jnp.where(low, 0.0, kv[:, sl]).astype(BF16)

    def col(width, idx):
        return pl.BlockSpec((tm, width), lambda i: (i, idx))

    def full(shape):
        return pl.BlockSpec(shape, lambda i: (0, 0))

    wide = jax.ShapeDtypeStruct((t, 1024), BF16)
    return pl.pallas_call(
        body, name="prep", grid=(t // tm,),
        out_shape=(wide, wide, wide, wide, wide, wide,
                   jax.ShapeDtypeStruct((t, MEM_WIDTH), BF16),
                   jax.ShapeDtypeStruct((t, MLA_Q_RANK), BF16),
                   jax.ShapeDtypeStruct((t, MLA_KV_RANK), BF16)),
        in_specs=[col(1024, 0), col(1024, 1), col(1024, 2), col(512, COL_CQ // 512), col(512, COL_MQ // 512),
                  pl.BlockSpec((tm, 4 * LANES), lambda i: (i, 0)),
                  full((1024, MLA_Q_RANK)), full((MLA_KV_RANK, 1024)),
                  full((1, MLA_Q_RANK)), full((1, MLA_KV_RANK)), full((8, LANES)), full((8, LANES))],
        out_specs=(col(1024, 0),) * 6 + (col(MEM_WIDTH, 0), col(MLA_Q_RANK, 0), col(MLA_KV_RANK, 0)),
        compiler_params=_params(("parallel",)),
    )(proj, proj, proj, proj, proj, trig, w_uq, w_ukv, g_cq, g_ckv, rope_a, rope_b)


def _attn_fwd(q, k, v, *, nb, s, sk, heads, hpb, voff, bq, name):
    nq = s // bq
    width = hpb * LANES
    vblk = voff // hpb

    def body(q_ref, k_ref, v_ref, o_ref, lse_ref):
        for h in range(hpb):
            sl = slice(h * LANES, (h + 1) * LANES)
            sc = _dot_nt(q_ref[:, sl], k_ref[:, sl])
            m = jnp.max(sc, axis=1, keepdims=True)
            p = jnp.exp2(sc - m)
            l = jnp.sum(p, axis=1, keepdims=True)
            o_ref[:, sl] = _dot(p.astype(BF16), v_ref[:, sl]) / l
            lse_ref[:, sl] = jnp.broadcast_to(m + jnp.log(l) * LOG2E, (bq, LANES))

    out = jax.ShapeDtypeStruct((nb * s, heads * LANES), F32)
    ospec = pl.BlockSpec((bq, width), lambda b, i, g: (b * nq + i, g))
    return pl.pallas_call(
        body, name=name, grid=(nb, nq, heads // hpb),
        out_shape=(out, out),
        in_specs=[ospec, pl.BlockSpec((sk, width), lambda b, i, g: (b, g)),
                  pl.BlockSpec((sk, width), lambda b, i, g: (b, vblk + g))],
        out_specs=(ospec, ospec),
        compiler_params=_params(("parallel", "parallel", "parallel")),
    )(q, k, v)


def _attn_bwd(q, k, v, o, do, lse, *, nb, s, sk, heads, hpb, voff, scale, bq, name):
    nq = s // bq
    width = hpb * LANES
    vblk = voff // hpb

    def body(q_ref, k_ref, v_ref, o_ref, do_ref, lse_ref, dq_ref, dk_ref, dv_ref, dk_acc, dv_acc):
        i = pl.program_id(2)

        @pl.when(i == 0)
        def _():
            dk_acc[...] = jnp.zeros_like(dk_acc)
            dv_acc[...] = jnp.zeros_like(dv_acc)

        for h in range(hpb):
            sl = slice(h * LANES, (h + 1) * LANES)
            qh = q_ref[:, sl]
            kk = k_ref[:, sl]
            doh = do_ref[:, sl]
            delta = jnp.sum(doh.astype(F32) * o_ref[:, sl], axis=1, keepdims=True)
            p = jnp.exp2(_dot_nt(qh, kk) - lse_ref[:, h * LANES:h * LANES + 1])
            ds = (p * (_dot_nt(doh, v_ref[:, sl]) - delta)).astype(BF16)
            dq_ref[:, sl] = (_dot(ds, kk) * scale).astype(BF16)
            dk_acc[:, sl] += _dot_tn(ds, qh)
            dv_acc[:, sl] += _dot_tn(p.astype(BF16), doh)

        @pl.when(i == nq - 1)
        def _():
            dk_ref[...] = (dk_acc[...] * LN2).astype(BF16)
            dv_ref[...] = dv_acc[...].astype(BF16)

    qspec = pl.BlockSpec((bq, width), lambda b, g, i: (b * nq + i, g))
    kv_spec = pl.BlockSpec((sk, width), lambda b, g, i: (b, g))
    dq_shape = jax.ShapeDtypeStruct((nb * s, heads * LANES), BF16)
    dkv_shape = jax.ShapeDtypeStruct((nb * sk, heads * LANES), BF16)
    return pl.pallas_call(
        body, name=name, grid=(nb, heads // hpb, nq),
        out_shape=(dq_shape, dkv_shape, dkv_shape),
        in_specs=[qspec, kv_spec, pl.BlockSpec((sk, width), lambda b, g, i: (b, vblk + g)), qspec, qspec, qspec],
        out_specs=(qspec, kv_spec, kv_spec),
        scratch_shapes=[pltpu.VMEM((sk, width), F32), pltpu.VMEM((sk, width), F32)],
        compiler_params=_params(("parallel", "parallel", "arbitrary")),
    )(q, k, v, o, do, lse)


BAND_Q = 128
BAND_WIN = 256


def _band_start(i, s):
    return min(max(i * BAND_Q - 64, 0), s - BAND_WIN)


def _to_pattern_order(src_ref, dst_ref, stage_ref, s, d):
    length = s // d
    stage_ref[...] = src_ref[...].astype(F32)
    for r in range(d):
        dst_ref[r * length:(r + 1) * length, :] = stage_ref[pl.ds(r, length, stride=d), :].astype(dst_ref.dtype)


def _dilated_fwd(q, k, v, bias, bias_index, *, nb, s, name):
    nblk = s // BAND_Q
    npat = len(DILATED)

    def body(q_ref, k_ref, v_ref, bias_ref, o_ref, lse_ref, stage_ref, qp_ref, kp_ref, vp_ref, op_ref, lp_ref,
             on_ref, ln_ref):
        lane = lax.broadcasted_iota(jnp.int32, (1, LANES), 1)
        first = lane < 64
        for p, (_, d) in enumerate(DILATED):
            if d == 1:
                qs, ks, vs = q_ref, k_ref, v_ref
            else:
                for src, dst in ((q_ref, qp_ref), (k_ref, kp_ref), (v_ref, vp_ref)):
                    _to_pattern_order(src, dst, stage_ref, s, d)
                qs, ks, vs = qp_ref, kp_ref, vp_ref
            for i in range(nblk):
                u0 = i * BAND_Q
                st = _band_start(i, s)
                qi = qs[u0:u0 + BAND_Q, :]
                kw = ks[st:st + BAND_WIN, :]
                vw = vs[st:st + BAND_WIN, :]
                zero = jnp.zeros_like(qi)
                q2 = jnp.concatenate([jnp.where(first, qi, zero), jnp.where(first, zero, qi)], axis=0)
                sc = _dot_nt(q2, kw)
                b = bias_ref[bias_index[p][i]]
                halves = []
                for h in range(2):
                    sh = sc[h * BAND_Q:(h + 1) * BAND_Q] + b
                    m = jnp.max(sh, axis=1, keepdims=True)
                    pr = jnp.exp2(sh - m)
                    l = jnp.sum(pr, axis=1, keepdims=True)
                    halves.append((pr.astype(BF16), l, m + jnp.log(l) * LOG2E))
                o2 = _dot(jnp.concatenate([halves[0][0], halves[1][0]], axis=0), vw)
                o_blk = jnp.where(first, o2[:BAND_Q] / halves[0][1], o2[BAND_Q:] / halves[1][1])
                lse_blk = jnp.where(first, jnp.broadcast_to(halves[0][2], (BAND_Q, LANES)),
                                    jnp.broadcast_to(halves[1][2], (BAND_Q, LANES)))
                op_ref[p, u0:u0 + BAND_Q, :] = o_blk
                lp_ref[p, u0:u0 + BAND_Q, :] = lse_blk
            if d > 1:
                length = s // d
                for r in range(d):
                    on_ref.at[p - 1][pl.ds(r, length, stride=d), :] = op_ref[p, r * length:(r + 1) * length, :]
                    ln_ref.at[p - 1][pl.ds(r, length, stride=d), :] = lp_ref[p, r * length:(r + 1) * length, :]
        lses = [lp_ref[0]] + [ln_ref[p] for p in range(npat - 1)]
        outs = [op_ref[0]] + [on_ref[p] for p in range(npat - 1)]
        m = functools.reduce(jnp.maximum, lses)
        ws = [jnp.exp2(l - m) for l in lses]
        den = functools.reduce(lambda a, c: a + c, ws)
        o_ref[...] = functools.reduce(lambda a, c: a + c, [w * o for w, o in zip(ws, outs)]) / den
        lse_ref[...] = m + jnp.log(den) * LOG2E

    blk = pl.BlockSpec((s, LANES), lambda b, g: (b, g))
    out = jax.ShapeDtypeStruct((nb * s, A_WIDTH), F32)
    return pl.pallas_call(
        body, name=name, grid=(nb, A_WIDTH // LANES),
        out_shape=(out, out),
        in_specs=[blk, blk, blk, pl.BlockSpec(bias.shape, lambda b, g: (0, 0, 0))],
        out_specs=(blk, blk),
        scratch_shapes=[pltpu.VMEM((s, LANES), F32), pltpu.VMEM((s, LANES), BF16), pltpu.VMEM((s, LANES), BF16),
                        pltpu.VMEM((s, LANES), BF16), pltpu.VMEM((npat, s, LANES), F32),
                        pltpu.VMEM((npat, s, LANES), F32), pltpu.VMEM((npat - 1, s, LANES), F32),
                        pltpu.VMEM((npat - 1, s, LANES), F32)],
        compiler_params=_params(("parallel", "parallel")),
    )(q, k, v, bias)


def _dilated_bwd(q, k, v, o, do, lse, bias, bias_index, *, nb, s, scale, name):
    nblk = s // BAND_Q
    npat = len(DILATED)

    def body(q_ref, k_ref, v_ref, o_ref, do_ref, lse_ref, bias_ref, dq_out, dk_out, dv_out,
             stage_ref, dl_ref, qp_ref, kp_ref, vp_ref, dop_ref, lsp_ref, dlp_ref, dqp_ref, dkp_ref, dvp_ref,
             dq_ref, dk_ref, dv_ref):
        lane = lax.broadcasted_iota(jnp.int32, (1, LANES), 1)
        first = lane < 64
        prod = do_ref[...].astype(F32) * o_ref[...]
        d0 = jnp.sum(jnp.where(first, prod, 0.0), axis=1, keepdims=True)
        d1 = jnp.sum(jnp.where(first, 0.0, prod), axis=1, keepdims=True)
        dl_ref[...] = jnp.where(first, jnp.broadcast_to(d0, (s, LANES)), jnp.broadcast_to(d1, (s, LANES)))
        for p, (_, d) in enumerate(DILATED):
            length = s // d
            if d == 1:
                qs, ks, vs, dos, lss, dls = q_ref, k_ref, v_ref, do_ref, lse_ref, dl_ref
                dqs, dks, dvs = dq_ref, dk_ref, dv_ref
            else:
                for src, dst in ((q_ref, qp_ref), (k_ref, kp_ref), (v_ref, vp_ref), (do_ref, dop_ref),
                                 (lse_ref, lsp_ref), (dl_ref, dlp_ref)):
                    _to_pattern_order(src, dst, stage_ref, s, d)
                qs, ks, vs, dos, lss, dls = qp_ref, kp_ref, vp_ref, dop_ref, lsp_ref, dlp_ref
                dqs, dks, dvs = dqp_ref, dkp_ref, dvp_ref
            dks[...] = jnp.zeros((s, LANES), F32)
            dvs[...] = jnp.zeros((s, LANES), F32)
            for i in range(nblk):
                u0 = i * BAND_Q
                st = _band_start(i, s)
                qi = qs[u0:u0 + BAND_Q, :]
                doi = dos[u0:u0 + BAND_Q, :]
                kw = ks[st:st + BAND_WIN, :]
                vw = vs[st:st + BAND_WIN, :]
                zero = jnp.zeros_like(qi)
                q2 = jnp.concatenate([jnp.where(first, qi, zero), jnp.where(first, zero, qi)], axis=0)
                do2 = jnp.concatenate([jnp.where(first, doi, zero), jnp.where(first, zero, doi)], axis=0)
                sc = _dot_nt(q2, kw)
                dp = _dot_nt(do2, vw)
                b = bias_ref[bias_index[p][i]]
                lse_i = lss[u0:u0 + BAND_Q, :]
                dl_i = dls[u0:u0 + BAND_Q, :]
                ps, dss = [], []
                for h in range(2):
                    rows = slice(h * BAND_Q, (h + 1) * BAND_Q)
                    pr = jnp.exp2(sc[rows] + b - lse_i[:, 64 * h:64 * h + 1])
                    ps.append(pr.astype(BF16))
                    dss.append((pr * (dp[rows] - dl_i[:, 64 * h:64 * h + 1])).astype(BF16))
                p2 = jnp.concatenate(ps, axis=0)
                ds2 = jnp.concatenate(dss, axis=0)
                dq2 = _dot(ds2, kw)
                dqs[u0:u0 + BAND_Q, :] = jnp.where(first, dq2[:BAND_Q], dq2[BAND_Q:]) * scale
                dks[st:st + BAND_WIN, :] += _dot_tn(ds2, q2)
                dvs[st:st + BAND_WIN, :] += _dot_tn(p2, do2)
            if d > 1:
                for dst, src in ((dq_ref, dqp_ref), (dk_ref, dkp_ref), (dv_ref, dvp_ref)):
                    for r in range(d):
                        dst[pl.ds(r, length, stride=d), :] += src[r * length:(r + 1) * length, :]
        dq_out[...] = dq_ref[...].astype(BF16)
        dk_out[...] = (dk_ref[...] * LN2).astype(BF16)
        dv_out[...] = dv_ref[...].astype(BF16)

    blk = pl.BlockSpec((s, LANES), lambda b, g: (b, g))
    out = jax.ShapeDtypeStruct((nb * s, A_WIDTH), BF16)
    f32_buf = pltpu.VMEM((s, LANES), F32)
    bf_buf = pltpu.VMEM((s, LANES), BF16)
    return pl.pallas_call(
        body, name=name, grid=(nb, A_WIDTH // LANES),
        out_shape=(out, out, out),
        in_specs=[blk] * 6 + [pl.BlockSpec(bias.shape, lambda b, g: (0, 0, 0))],
        out_specs=(blk, blk, blk),
        scratch_shapes=[f32_buf, f32_buf, bf_buf, bf_buf, bf_buf, bf_buf] + [f32_buf] * 8,
        compiler_params=_params(("parallel", "parallel")),
    )(q, k, v, o, do, lse, bias)


def _post(x, ya, ybp, ym, proj, target, w_out, g_emb, b_emb, g_a, g_b, g_m, g_post, b_post, tm=256):
    t = x.shape[0]

    def body(x_ref, ya_ref, yb_ref, ym_ref, ga_ref, gb_ref, gm_ref, tg_ref, wo_ref,
             ge_ref, be_ref, goa_ref, gob_ref, gom_ref, gp_ref, bp_ref,
             y_ref, dz_ref, doa_ref, dob_ref, dom_ref, dga_ref, dgb_ref, dgm_ref,
             loss_ref, dgp_ref, dbp_ref, dgoa_ref, dgob_ref, dgom_ref):
        i = pl.program_id(0)

        @pl.when(i == 0)
        def _():
            for r in (loss_ref, dgp_ref, dbp_ref, dgoa_ref, dgob_ref, dgom_ref):
                r[...] = jnp.zeros_like(r)

        lane = lax.broadcasted_iota(jnp.int32, (1, LANES), 1)
        low = lane < 64
        xh0, _ = _ln_hat(x_ref[...])
        h = xh0 * ge_ref[...] + be_ref[...]

        ybp_v = yb_ref[...]
        yb = jnp.concatenate(
            [jnp.where(low, pltpu.roll(ybp_v[:, 2 * j * LANES:(2 * j + 1) * LANES], 64, 1),
                       ybp_v[:, (2 * j + 1) * LANES:(2 * j + 2) * LANES]) for j in range(4)], axis=1)

        def gated(raw, gate, gain, width):
            xh, r = _rms_hat(raw, width)
            n = xh * gain
            sg = 1.0 / (1.0 + jnp.exp(-gate))
            return xh, r, n, sg, n * (gate * sg)

        gate_a, gate_b, gate_m = ga_ref[...], gb_ref[...], gm_ref[...]
        xh_a, r_a, n_a, sg_a, y_a = gated(ya_ref[...], gate_a, goa_ref[...], A_WIDTH)
        xh_b, r_b, n_b, sg_b, y_b = gated(yb, gate_b, gob_ref[...], 512)
        xh_m, r_m, n_m, sg_m, y_m = gated(ym_ref[...], gate_m, gom_ref[...], 512)
        y = jnp.concatenate([y_a, y_b, y_m], axis=1).astype(BF16)
        y_ref[...] = y
        z = DEEPNORM_ALPHA * h + _dot(y, wo_ref[...])
        zh, rstd = _ln_hat(z)
        err = zh * gp_ref[...] + bp_ref[...] - tg_ref[...]
        rows = jnp.sum(err * err, axis=1, keepdims=True)
        loss_ref[...] += jnp.broadcast_to(jnp.sum(rows, axis=0, keepdims=True) * (0.5 / D_MODEL), (1, LANES))
        dout = err * (1.0 / D_MODEL)
        dgp_ref[...] += _colsum(dout * zh)
        dbp_ref[...] += _colsum(dout)
        dz = _ln_bwd_rows(dout * gp_ref[...], zh, rstd)
        dz_ref[...] = dz
        dy = _dot_nt(dz.astype(BF16), wo_ref[...])

        def gated_bwd(dyg, xh, r, n, sg, gate, gain, width, dgain_ref):
            dn = dyg * (gate * sg)
            dgate = dyg * n * (sg * (1.0 + gate * (1.0 - sg)))
            dgain_ref[...] += _colsum(dn * xh)
            return _rms_bwd(dn * gain, xh, r, width), dgate

        dya, dgate_a = gated_bwd(dy[:, 0:1024], xh_a, r_a, n_a, sg_a, gate_a, goa_ref[...], A_WIDTH, dgoa_ref)
        dyb, dgate_b = gated_bwd(dy[:, 1024:1536], xh_b, r_b, n_b, sg_b, gate_b, gob_ref[...], 512, dgob_ref)
        dym, dgate_m = gated_bwd(dy[:, 1536:2048], xh_m, r_m, n_m, sg_m, gate_m, gom_ref[...], 512, dgom_ref)
        doa_ref[...] = dya.astype(BF16)
        dom_ref[...] = dym.astype(BF16)
        dga_ref[...] = dgate_a.astype(BF16)
        dgb_ref[...] = dgate_b.astype(BF16)
        dgm_ref[...] = dgate_m.astype(BF16)
        for j in range(4):
            blk = dyb[:, j * LANES:(j + 1) * LANES]
            dob_ref[:, 2 * j * LANES:(2 * j + 1) * LANES] = jnp.where(low, 0.0, pltpu.roll(blk, 64, 1)).astype(BF16)
            dob_ref[:, (2 * j + 1) * LANES:(2 * j + 2) * LANES] = jnp.where(low, 0.0, blk).astype(BF16)

    def col(width, idx):
        return pl.BlockSpec((tm, width), lambda i: (i, idx))

    def full(shape):
        return pl.BlockSpec(shape, lambda i: (0, 0))

    def acc(width):
        return jax.ShapeDtypeStruct((1, width), F32)

    return pl.pallas_call(
        body, name="post", grid=(t // tm,),
        out_shape=(jax.ShapeDtypeStruct((t, 2048), BF16), jax.ShapeDtypeStruct((t, 1024), F32),
                   jax.ShapeDtypeStruct((t, 1024), BF16), jax.ShapeDtypeStruct((t, 1024), BF16),
                   jax.ShapeDtypeStruct((t, 512), BF16),
                   jax.ShapeDtypeStruct((t, 1024), BF16), jax.ShapeDtypeStruct((t, 512), BF16),
                   jax.ShapeDtypeStruct((t, 512), BF16),
                   acc(LANES), acc(1024), acc(1024), acc(1024), acc(512), acc(512)),
        in_specs=[col(1024, 0), col(1024, 0), col(1024, 0), col(512, 0),
                  col(1024, 3), col(512, COL_BG // 512), col(512, COL_MG // 512), col(1024, 0),
                  full((2048, 1024)),
                  full((1, 1024)), full((1, 1024)), full((1, 1024)), full((1, 512)), full((1, 512)),
                  full((1, 1024)), full((1, 1024))],
        out_specs=(col(2048, 0), col(1024, 0), col(1024, 0), col(1024, 0), col(512, 0),
                   col(1024, 0), col(512, 0), col(512, 0),
                   full((1, LANES)), full((1, 1024)), full((1, 1024)), full((1, 1024)), full((1, 512)),
                   full((1, 512))),
        compiler_params=_params(("arbitrary",)),
    )(x, ya, ybp, ym, proj, proj, proj, target, w_out, g_emb, b_emb, g_a, g_b, g_m, g_post, b_post)


def _prep_bwd(dqa, dka, dva, dqb, dkb, dvb, dqm, dga, dgb, dgm, proj, trig, w_uq, w_ukv, g_cq, g_ckv,
              rope_a, rope_b, tm=256):
    t = proj.shape[0]

    def body(dqa_ref, dka_ref, dva_ref, dqb_ref, dkb_ref, dvb_ref, dqm_ref, dga_ref, dgb_ref, dgm_ref,
             bs_ref, trig_ref, wuq_ref, wukv_ref, gcq_ref, gckv_ref, ra_ref, rb_ref,
             dproj_ref, dqf_ref, dkv_ref, dgcq_ref, dgckv_ref):
        i = pl.program_id(0)

        @pl.when(i == 0)
        def _():
            dgcq_ref[...] = jnp.zeros_like(dgcq_ref)
            dgckv_ref[...] = jnp.zeros_like(dgckv_ref)

        ta = _rope_tables(trig_ref[:, 0:LANES], trig_ref[:, LANES:2 * LANES], ra_ref[...])
        tb = _rope_tables(trig_ref[:, 2 * LANES:3 * LANES], trig_ref[:, 3 * LANES:4 * LANES], rb_ref[...])
        for j in range(A_WIDTH // LANES):
            sl = slice(j * LANES, (j + 1) * LANES)
            dproj_ref[:, j * LANES:(j + 1) * LANES] = (
                _rope(dqa_ref[:, sl].astype(F32), ta, 8, inverse=True).astype(BF16))
            dproj_ref[:, 1024 + j * LANES:1024 + (j + 1) * LANES] = (
                _rope(dka_ref[:, sl].astype(F32), ta, 8, inverse=True).astype(BF16))
        dproj_ref[:, 2048:3072] = dva_ref[...]
        dproj_ref[:, 3072:4096] = dga_ref[...]

        lane = lax.broadcasted_iota(jnp.int32, (1, LANES), 1)
        low = lane < 64
        rope_lanes = (lane >= 64) & (lane < 96)
        dkr = jnp.zeros((tm, LANES), F32)
        for h in range(MLA_HEADS):
            sl = slice(h * LANES, (h + 1) * LANES)
            dqf_ref[:, sl] = _rope(dqb_ref[:, sl].astype(F32), tb, 16, inverse=True).astype(BF16)
            dk_h = dkb_ref[:, sl]
            dkv_ref[:, sl] = jnp.where(low, dk_h, dvb_ref[:, sl])
            dkr = dkr + jnp.where(rope_lanes, dk_h.astype(F32), 0.0)
        dkr = _rope(dkr, tb, 16, inverse=True)

        cq_hat, r_q = _rms_hat(bs_ref[:, 0:MLA_Q_RANK], MLA_Q_RANK)
        dcqn = _dot(dqf_ref[...], wuq_ref[...])
        dgcq_ref[...] += _colsum(dcqn * cq_hat)
        dproj_ref[:, COL_CQ:COL_CQ + 256] = _rms_bwd(dcqn * gcq_ref[...], cq_hat, r_q, MLA_Q_RANK).astype(BF16)
        ckv_hat, r_kv = _rms_hat(bs_ref[:, MLA_Q_RANK:MLA_Q_RANK + MLA_KV_RANK], MLA_KV_RANK)
        dckvn = _dot_nt(dkv_ref[...], wukv_ref[...])
        dgckv_ref[...] += _colsum(dckvn * ckv_hat)
        dproj_ref[:, COL_CQ + 256:COL_CQ + 384] = (
            _rms_bwd(dckvn * gckv_ref[...], ckv_hat, r_kv, MLA_KV_RANK).astype(BF16))
        dproj_ref[:, COL_CQ + 384:COL_CQ + 512] = dkr.astype(BF16)
        dproj_ref[:, COL_BG:COL_BG + 512] = dgb_ref[...]
        dproj_ref[:, COL_MQ:COL_MQ + 512] = dqm_ref[...]
        dproj_ref[:, COL_MG:COL_MG + 512] = dgm_ref[...]

    def col(width, idx):
        return pl.BlockSpec((tm, width), lambda i: (i, idx))

    def full(shape):
        return pl.BlockSpec(shape, lambda i: (0, 0))

    return pl.pallas_call(
        body, name="prep_bwd", grid=(t // tm,),
        out_shape=(jax.ShapeDtypeStruct((t, PROJ_W), BF16), jax.ShapeDtypeStruct((t, 1024), BF16),
                   jax.ShapeDtypeStruct((t, 1024), BF16),
                   jax.ShapeDtypeStruct((1, MLA_Q_RANK), F32), jax.ShapeDtypeStruct((1, MLA_KV_RANK), F32)),
        in_specs=[col(1024, 0)] * 6 + [col(512, 0), col(1024, 0), col(512, 0), col(512, 0),
                  col(512, COL_CQ // 512), pl.BlockSpec((tm, 4 * LANES), lambda i: (i, 0)),
                  full((1024, MLA_Q_RANK)), full((MLA_KV_RANK, 1024)),
                  full((1, MLA_Q_RANK)), full((1, MLA_KV_RANK)), full((8, LANES)), full((8, LANES))],
        out_specs=(col(PROJ_W, 0), col(1024, 0), col(1024, 0), full((1, MLA_Q_RANK)), full((1, MLA_KV_RANK))),
        compiler_params=_params(("arbitrary",)),
    )(dqa, dka, dva, dqb, dkb, dvb, dqm, dga, dgb, dgm, proj, trig, w_uq, w_ukv, g_cq, g_ckv, rope_a, rope_b)


def _adamw_math(gv, w, m, v):
    m_new = ADAM_B1 * m + (1.0 - ADAM_B1) * gv
    v_new = ADAM_B2 * v + (1.0 - ADAM_B2) * (gv * gv)
    m_hat = m_new / (1.0 - ADAM_B1 ** ADAM_STEP)
    v_hat = v_new / (1.0 - ADAM_B2 ** ADAM_STEP)
    return -ADAM_LR * (m_hat / (jnp.sqrt(v_hat) + ADAM_EPS) + ADAM_WD * w), m_new, v_new


def _adamw(g, w, m, v, tr, name):
    r, cols = w.shape

    def body(g_ref, w_ref, m_ref, v_ref, go_ref, d_ref, nm_ref, nv_ref):
        gv = g_ref[...]
        go_ref[...] = gv
        d_ref[...], nm_ref[...], nv_ref[...] = _adamw_math(gv, w_ref[...], m_ref[...], v_ref[...])

    tile = pl.BlockSpec((tr, cols), lambda i: (i, 0))
    shape = jax.ShapeDtypeStruct((r, cols), F32)
    return pl.pallas_call(
        body, name=name, grid=(r // tr,),
        out_shape=(shape,) * 4, in_specs=[tile] * 4, out_specs=(tile,) * 4,
        compiler_params=_params(("parallel",)),
    )(g, w, m, v)


def _adamw_pieces(g, w, m, v, pieces, name):
    shapes = [jax.ShapeDtypeStruct((r1 - r0, c1 - c0), F32) for r0, r1, c0, c1 in pieces]

    def body(g_ref, w_ref, m_ref, v_ref, *outs):
        gv = g_ref[...]
        results = (gv,) + _adamw_math(gv, w_ref[...], m_ref[...], v_ref[...])
        for kind, full in enumerate(results):
            for p, (r0, r1, c0, c1) in enumerate(pieces):
                outs[kind * len(pieces) + p][...] = full[r0:r1, c0:c1]

    flat = pl.pallas_call(
        body, name=name, out_shape=tuple(shapes) * 4,
        in_specs=[IN_VMEM] * 4, out_specs=tuple([IN_VMEM] * (4 * len(pieces))),
        compiler_params=_params(None),
    )(g, w, m, v)
    return [[flat[kind * len(pieces) + p] for kind in range(4)] for p in range(len(pieces))]


def _core_sum(g, recv, core, rows, tr, name, ride=None):
    cols = g.shape[2]
    nblk = rows // tr
    n_in = len(ride.args) if ride else 0
    n_out = len(ride.out_shapes) if ride else 0

    def body(c_ref, g_ref, r_ref, *rest):
        sf_ref, sb_ref = rest[n_in], rest[n_in + 1]
        if ride:
            j, i = pl.program_id(0), pl.program_id(1)
            ride.run((j == 0) & (i == 0), (j == 3) & (i == nblk - 1), rest[:n_in],
                     rest[n_in + 2:n_in + 2 + n_out], rest[n_in + 2 + n_out:])
        tot = g_ref[...] + r_ref[...]
        sf_ref[...] = tot
        sb_ref[...] = tot.astype(BF16)

    half = pl.BlockSpec((None, tr, cols), lambda j, i, c_ref: (j, i, 0))
    shapes = (jax.ShapeDtypeStruct((4, rows, cols), F32), jax.ShapeDtypeStruct((4, rows, cols), BF16))
    return pl.pallas_call(
        body, name=name,
        grid_spec=pltpu.PrefetchScalarGridSpec(
            num_scalar_prefetch=1, grid=(4, nblk),
            in_specs=[pl.BlockSpec((None, tr, cols), lambda j, i, c_ref: (j, c_ref[0] * nblk + i, 0)), half]
            + (ride.in_specs if ride else []),
            out_specs=(half, half) + (ANY,) * n_out,
            scratch_shapes=ride.scratch() if ride else []),
        out_shape=shapes + tuple(ride.out_shapes if ride else ()),
        compiler_params=_params(("arbitrary", "arbitrary") if ride else ("parallel", "parallel")),
    )(core, g, recv, *(ride.args if ride else ()))


def _half_to_sibling(g4):
    def plan(in_refs, out_refs, send_sems, recv_sems):
        x, y, c = _position()
        cp = pltpu.make_async_remote_copy(
            src_ref=in_refs[0].at[:, 1 - c], dst_ref=out_refs[0], send_sem=send_sems.at[0],
            recv_sem=recv_sems.at[0], device_id=(x, y, 1 - c), device_id_type=MESH)

        def finish():
            cp.wait_recv()
            cp.wait_send()

        return cp.start, finish

    return _Ride([g4], [jax.ShapeDtypeStruct((4, g4.shape[2], 1024), F32)], (1, 1), plan)


def _gather_plan(src_ref, dst_ref, send_sems, recv_sems, local_sems):
    x, y, c = _position()
    me = 2 * x + y
    local = pltpu.make_async_copy(src_ref, dst_ref.at[me], local_sems.at[0])

    def over_ici(k, src, chip):
        return pltpu.make_async_remote_copy(
            src_ref=src, dst_ref=dst_ref.at[chip, c], send_sem=send_sems.at[k - 1], recv_sem=recv_sems.at[k - 1],
            device_id=(x ^ (k >> 1), y ^ (k & 1), c), device_id_type=MESH)

    def to_sibling(k, half):
        piece = dst_ref.at[me ^ k, half]
        return pltpu.make_async_remote_copy(
            src_ref=piece, dst_ref=piece, send_sem=send_sems.at[2 + k], recv_sem=recv_sems.at[2 + k],
            device_id=(x, y, 1 - c), device_id_type=MESH)

    sends = [over_ici(k, src_ref.at[c], me) for k in (1, 2, 3)]

    def start():
        local.start()
        for cp in sends:
            cp.start()

    def relay():
        for k in (1, 2, 3):
            over_ici(k, dst_ref.at[me ^ k, c], me ^ k).wait_recv()
            to_sibling(k, c).start()

    def finish():
        for k in (1, 2, 3):
            to_sibling(k, 1 - c).wait_recv()
        for cp in sends + [to_sibling(k, c) for k in (1, 2, 3)]:
            cp.wait_send()
        local.wait()

    return start, relay, finish


def _gather_ride(shard):
    def plan(in_refs, out_refs, send_sems, recv_sems, local_sems):
        return _gather_plan(in_refs[0], out_refs[0], send_sems, recv_sems, local_sems)

    return _Ride([shard], [jax.ShapeDtypeStruct((4,) + shard.shape, shard.dtype)], (6, 6, 1), plan,
                 in_specs=[IN_VMEM])


def _chip_sum(sf, recv, chip, rows, tr, name):
    cols = sf.shape[2]

    def body(me_ref, sf_ref, r_ref, out_ref):
        acc = sf_ref[...]
        for k in range(3):
            acc = acc + r_ref[k].astype(F32)
        out_ref[...] = acc

    return pl.pallas_call(
        body, name=name,
        grid_spec=pltpu.PrefetchScalarGridSpec(
            num_scalar_prefetch=1, grid=(rows // tr,),
            in_specs=[pl.BlockSpec((None, tr, cols), lambda i, me_ref: (me_ref[0], i, 0)),
                      pl.BlockSpec((3, tr, cols), lambda i, me_ref: (0, i, 0))],
            out_specs=pl.BlockSpec((tr, cols), lambda i, me_ref: (i, 0))),
        out_shape=jax.ShapeDtypeStruct((rows, cols), F32),
        compiler_params=_params(("parallel",)),
    )(chip, sf, recv)


def _position():
    return lax.axis_index("x"), lax.axis_index("y"), lax.axis_index("c")


def _dh_scatter(dproj, w_in_arr_t, x, dz, g, sb_in, sb_rest, tm=1024, tk=1024):
    t, d = x.shape
    nk = dproj.shape[1] // tk
    ni = t // tm

    def body(dp_ref, w_ref, x_ref, dz_ref, g_ref, sbin_ref, sbrest_ref,
             dx_ref, dg_ref, db_ref, rin_ref, rrest_ref, acc_ref, send_sems, recv_sems):
        i = pl.program_id(0)
        kk = pl.program_id(1)
        px, py, pc = _position()
        me = 2 * px + py
        srcs = (sbin_ref, sbrest_ref)
        dsts = (rin_ref, rrest_ref)

        def copy(a, k):
            return pltpu.make_async_remote_copy(
                src_ref=srcs[a].at[me ^ k], dst_ref=dsts[a].at[k - 1],
                send_sem=send_sems.at[3 * a + k - 1], recv_sem=recv_sems.at[3 * a + k - 1],
                device_id=(px ^ (k >> 1), py ^ (k & 1), pc), device_id_type=MESH)

        pairs = [(a, k) for a in range(2) for k in (1, 2, 3)]

        @pl.when((i == 0) & (kk == 0))
        def _():
            dg_ref[...] = jnp.zeros_like(dg_ref)
            db_ref[...] = jnp.zeros_like(db_ref)
            for a, k in pairs:
                copy(a, k).start()

        part = _dot(dp_ref[...], w_ref[...])

        @pl.when(kk == 0)
        def _():
            acc_ref[...] = part

        @pl.when(kk > 0)
        def _():
            acc_ref[...] += part

        @pl.when(kk == nk - 1)
        def _():
            xh, rstd = _ln_hat(x_ref[...])
            dht = acc_ref[...] + DEEPNORM_ALPHA * dz_ref[...]
            dg_ref[...] += _colsum(dht * xh)
            db_ref[...] += _colsum(dht)
            dx_ref[...] = _ln_bwd_rows(dht * g_ref[...], xh, rstd)

        @pl.when((i == ni - 1) & (kk == nk - 1))
        def _():
            for a, k in pairs:
                copy(a, k).wait_recv()
            for a, k in pairs:
                copy(a, k).wait_send()

    tile = pl.BlockSpec((tm, d), lambda i, kk: (i, 0))
    row = pl.BlockSpec((1, d), lambda i, kk: (0, 0))
    return pl.pallas_call(
        body, name="dh_scatter", grid=(ni, nk),
        out_shape=(jax.ShapeDtypeStruct((t, d), F32), jax.ShapeDtypeStruct((1, d), F32),
                   jax.ShapeDtypeStruct((1, d), F32),
                   jax.ShapeDtypeStruct((3, HALF_IN, 1024), BF16),
                   jax.ShapeDtypeStruct((3, HALF_REST, 1024), BF16)),
        in_specs=[pl.BlockSpec((tm, tk), lambda i, kk: (i, kk)), pl.BlockSpec((tk, d), lambda i, kk: (kk, 0)),
                  tile, tile, row, ANY, ANY],
        out_specs=(tile, row, row, ANY, ANY),
        scratch_shapes=[pltpu.VMEM((tm, d), F32), pltpu.SemaphoreType.DMA((6,)), pltpu.SemaphoreType.DMA((6,))],
        compiler_params=_params(("arbitrary", "arbitrary")),
    )(dproj, w_in_arr_t, x, dz, g, sb_in, sb_rest)


def _join_halves(gh_in, gh_rest):
    def body(hin_ref, hrest_ref, oin_ref, orest_ref, send_sems, recv_sems, local_sems):
        x, y, c = _position()
        srcs = (hin_ref, hrest_ref)
        dsts = (oin_ref, orest_ref)

        def rows(a, half):
            return dsts[a].at[half]

        local = [pltpu.make_async_copy(srcs[a], rows(a, c), local_sems.at[a]) for a in range(2)]
        remote = [pltpu.make_async_remote_copy(
            src_ref=srcs[a], dst_ref=rows(a, c), send_sem=send_sems.at[a], recv_sem=recv_sems.at[a],
            device_id=(x, y, 1 - c), device_id_type=MESH) for a in range(2)]
        for cp in local + remote:
            cp.start()
        for a in range(2):
            pltpu.make_async_remote_copy(
                src_ref=srcs[a], dst_ref=rows(a, 1 - c), send_sem=send_sems.at[a], recv_sem=recv_sems.at[a],
                device_id=(x, y, 1 - c), device_id_type=MESH).wait_recv()
        for cp in remote:
            cp.wait_send()
        for cp in local:
            cp.wait()

    return pl.pallas_call(
        body, name="join_halves",
        out_shape=(jax.ShapeDtypeStruct((2, HALF_IN, 1024), F32),
                   jax.ShapeDtypeStruct((2, HALF_REST, 1024), F32)),
        in_specs=[IN_VMEM, IN_VMEM], out_specs=(ANY, ANY),
        scratch_shapes=[pltpu.SemaphoreType.DMA((2,)), pltpu.SemaphoreType.DMA((2,)), pltpu.SemaphoreType.DMA((2,))],
    )(gh_in, gh_rest)


def _allreduce_small(vec):
    def body(vec_ref, out_ref, all_ref, send_sems, recv_sems):
        x, y, c = _position()
        me = 4 * x + 2 * y + c
        all_ref[me] = vec_ref[...]

        def copy(k, slot):
            return pltpu.make_async_remote_copy(
                src_ref=vec_ref, dst_ref=all_ref.at[slot], send_sem=send_sems.at[k - 1], recv_sem=recv_sems.at[k - 1],
                device_id=(x ^ (k >> 2), y ^ ((k >> 1) & 1), c ^ (k & 1)), device_id_type=MESH)

        copies = [copy(k, me) for k in range(1, 8)]
        for cp in copies:
            cp.start()
        for k in range(1, 8):
            copy(k, me ^ k).wait_recv()
        for cp in copies:
            cp.wait_send()
        total = all_ref[0]
        for d in range(1, 8):
            total = total + all_ref[d]
        out_ref[...] = total

    return pl.pallas_call(
        body, name="allreduce_small",
        out_shape=jax.ShapeDtypeStruct(vec.shape, vec.dtype),
        in_specs=[pl.BlockSpec(memory_space=pltpu.VMEM)], out_specs=pl.BlockSpec(memory_space=pltpu.VMEM),
        scratch_shapes=[pltpu.VMEM((8,) + vec.shape, vec.dtype), pltpu.SemaphoreType.DMA((7,)),
                        pltpu.SemaphoreType.DMA((7,))],
    )(vec)


def _pack_rest(w_uq, w_ukv, w_mem, w_out):
    rows = jnp.concatenate([w_uq[0].T.reshape(-1, 1024), w_ukv.reshape(-1, 1024), w_mem.reshape(-1, 1024),
                            w_out.reshape(-1, 1024)], axis=0)
    return jnp.pad(rows, ((0, ROWS_REST - ROWS_USED), (0, 0)))


def _arranged_w_in(g_in):
    z = functools.partial(jnp.zeros, dtype=g_in.dtype)
    cut = 4480 - 2 * SHARD_ROWS
    return jnp.concatenate(
        [g_in[0, :SHARD_ROWS], g_in[1, :SHARD_ROWS], g_in[2, :cut], z((64, 1024)), g_in[2, cut:cut + 32],
         z((32, 1024)), g_in[2, cut + 32:SHARD_ROWS], g_in[3, :SHARD_ROWS]], axis=0)


def _rest_weights(g_rest):
    w_uq_t = g_rest[:, 0:ROWS_UQ].reshape(768, 256)
    w_uq_pad_t = jnp.pad(w_uq_t.reshape(MLA_HEADS, MLA_QK_DIM, 256), ((0, 0), (0, 32), (0, 0))).reshape(1024, 256)
    w_ukv = jnp.concatenate([g_rest[j, ROWS_UQ:ROWS_UQ + ROWS_UKV].reshape(128, 256) for j in range(4)], axis=1)
    lo = ROWS_UQ + ROWS_UKV
    w_mem = g_rest[:, lo:lo + ROWS_MEM].reshape(4 * ROWS_MEM, 1024)
    w_out = g_rest[:, lo + ROWS_MEM:lo + ROWS_MEM + ROWS_OUT].reshape(4 * ROWS_OUT, 1024)
    return w_uq_pad_t, w_ukv, w_mem, w_out


def _split_in(dw_in_arr_t):
    a = dw_in_arr_t
    gap = jnp.zeros((ROWS_IN - SHARD_ROWS, 1024), a.dtype)
    nat = 4608 - 96
    pieces = [a[:SHARD_ROWS], gap, a[SHARD_ROWS:2 * SHARD_ROWS], gap,
              a[2 * SHARD_ROWS:4480], a[4544:4576], a[4608:4608 + 3 * SHARD_ROWS - nat], gap,
              a[4608 + 3 * SHARD_ROWS - nat:], gap]
    return jnp.concatenate(pieces, axis=0).reshape(4, ROWS_IN, 1024)


def _split_rest(dw_uq_pad_t, dw_ukv, dw_mem, dw_out):
    dw_uq_t = dw_uq_pad_t.reshape(MLA_HEADS, LANES, 256)[:, :MLA_QK_DIM].reshape(4, ROWS_UQ, 1024)
    parts = [dw_uq_t, dw_ukv.reshape(128, 4, 256).transpose(1, 0, 2).reshape(4, ROWS_UKV, 1024),
             dw_mem.reshape(4, ROWS_MEM, 1024), dw_out.reshape(4, ROWS_OUT, 1024)]
    return jnp.pad(jnp.concatenate(parts, axis=1), ((0, 0), (0, ROWS_REST - ROWS_USED), (0, 0)))


def _rope_consts(rot, first, period):
    half = rot // 2
    inv_freq = np.float32(ROPE_THETA) ** (-(np.arange(0, rot, 2, dtype=np.float32) / np.float32(rot)))
    lane = np.arange(LANES) % period - first
    in_rot = (lane >= 0) & (lane < rot)
    out = np.zeros((8, LANES), np.float32)
    out[0] = np.where(in_rot, inv_freq[np.clip(lane, 0, rot - 1) % half], 0.0)
    out[1] = in_rot & (lane < half)
    out[2] = in_rot & (lane >= half)
    return jnp.asarray(out)


def _band_bias(s):
    nblk = s // BAND_Q
    starts = np.array([_band_start(i, s) for i in range(nblk)])
    uq = (np.arange(nblk)[:, None] * BAND_Q + np.arange(BAND_Q)[None, :])[:, :, None]
    uk = (starts[:, None] + np.arange(BAND_WIN)[None, :])[:, None, :]
    tiles, index, seen = [], [], {}
    for _, d in DILATED:
        length = s // d
        ok = (uq // length == uk // length) & (np.abs(uq - uk) <= 64)
        row = []
        for i in range(nblk):
            key = ok[i].tobytes()
            if key not in seen:
                seen[key] = len(tiles)
                tiles.append(np.where(ok[i], 0.0, NEG_INF).astype(np.float32))
            row.append(seen[key])
        index.append(row)
    return jnp.asarray(np.stack(tiles, axis=0)), index


def _forward_backward(h, proj, trig, rope_consts, x, mem, target, weights, gains):
    w_uq_pad_t, w_ukv, w_mem, w_out = weights
    g_emb, b_emb, g_cq, g_ckv, g_out_a, g_out_b, g_out_m, g_post, b_post = gains
    nb, s, d = x.shape
    t = nb * s
    x2 = x.reshape(t, d)
    mem2 = mem.reshape(nb * N_MEM, d)
    tgt2 = target.reshape(t, d)
    rope_a, rope_b = rope_consts
    bias, bias_index = _band_bias(s)
    scales = (0.125, MLA_QK_DIM ** -0.5, 128 ** -0.5)

    qa, ka, va, qb, kb, vb, qm, cqn, ckvn = _prep(proj, trig, w_uq_pad_t, w_ukv, g_cq, g_ckv, rope_a, rope_b, scales)
    mkv = _mm(mem2, w_mem, BF16, nb * N_MEM, 1024, 1024, "mem_kv")

    cfg_b = dict(nb=nb, s=s, sk=s, heads=8, voff=0, bq=256)
    cfg_m = dict(nb=nb, s=s, sk=N_MEM, heads=4, hpb=2, voff=4, bq=1024)
    ya, lse_a = _dilated_fwd(qa, ka, va, bias, bias_index, nb=nb, s=s, name="attn_a_fwd")
    yb, lse_b = _attn_fwd(qb, kb, vb, name="attn_b_fwd", hpb=4, **cfg_b)
    ym, lse_m = _attn_fwd(qm, mkv, mkv, name="attn_m_fwd", **cfg_m)

    (y, dz, doa, dob, dom, dga, dgb, dgm, loss, dg_post, db_post, dg_a, dg_b, dg_m) = _post(
        x2, ya, yb, ym, proj, tgt2, w_out, g_emb, b_emb, g_out_a, g_out_b, g_out_m, g_post, b_post)

    dqa, dka, dva = _dilated_bwd(qa, ka, va, ya, doa, lse_a, bias, bias_index, nb=nb, s=s, scale=scales[0],
                                 name="attn_a_bwd")
    dqb, dkb, dvb = _attn_bwd(qb, kb, vb, yb, dob, lse_b, name="attn_b_bwd", scale=scales[1], hpb=2, **cfg_b)
    dqm, dmk, dmv = _attn_bwd(qm, mkv, mkv, ym, dom, lse_m, name="attn_m_bwd", scale=scales[2], **cfg_m)
    dmkv = jnp.concatenate([dmk, dmv], axis=1)

    dproj, dqf, dkv, dg_cq, dg_ckv = _prep_bwd(
        dqa, dka, dva, dqb, dkb, dvb, dqm, dga, dgb, dgm, proj, trig, w_uq_pad_t, w_ukv, g_cq, g_ckv, rope_a, rope_b)

    small_rows = (dg_cq, dg_ckv, loss, dg_a, dg_b, dg_m, dg_post, db_post)
    return (dproj, h, y, dz, dqf, cqn, ckvn, dkv, mem2, dmkv), x2, small_rows


def _weight_grads(operands, core):
    dproj, h, y, dz, dqf, cqn, ckvn, dkv, mem2, dmkv = operands
    dw_in_arr_t = _mm(dproj, h, F32, 1024, 1024, 4096, "dw_in", mode="tn")
    g_in = _split_in(dw_in_arr_t)
    dw_out, r_in = _mm(y, dz, F32, 1024, 1024, 2048, "dw_out", mode="tn",
                       ride=_half_to_sibling(g_in.reshape(4, 2, HALF_IN, 1024)))
    dw_uq_pad_t = _mm(dqf, cqn, F32, 1024, 256, 1024, "dw_uq", mode="tn")
    dw_ukv = _mm(ckvn, dkv, F32, 128, 1024, 1024, "dw_ukv", mode="tn")
    dw_mem = _mm(mem2, dmkv, F32, 1024, 1024, mem2.shape[0], "dw_mem", mode="tn")
    g_rest = _split_rest(dw_uq_pad_t, dw_ukv, dw_mem, dw_out)
    sf_in, sb_in, r_rest = _core_sum(g_in, r_in, core, HALF_IN, HALF_IN // 2, "core_sum_in",
                                     ride=_half_to_sibling(g_rest.reshape(4, 2, HALF_REST, 1024)))
    sf_rest, sb_rest = _core_sum(g_rest, r_rest, core, HALF_REST, HALF_REST, "core_sum_rest")
    return sf_in, sb_in, sf_rest, sb_rest


def _small_block(dg_emb, db_emb, small_rows):
    dg_cq, dg_ckv, loss, dg_a, dg_b, dg_m, dg_post, db_post = small_rows
    row2 = jnp.concatenate([dg_cq, dg_ckv, loss, jnp.zeros((1, 512), F32)], axis=1)
    return jnp.concatenate([dg_emb, db_emb, row2, dg_a, jnp.concatenate([dg_b, dg_m], axis=1), dg_post, db_post,
                            jnp.zeros((1, 1024), F32)], axis=0)


def _pack_small(g_emb, b_emb, g_cq, g_ckv, g_out_a, g_out_b, g_out_m, g_post, b_post):
    row2 = jnp.concatenate([g_cq.reshape(1, -1), g_ckv.reshape(1, -1), jnp.zeros((1, 640), F32)], axis=1)
    return jnp.concatenate([g_emb.reshape(1, -1), b_emb.reshape(1, -1), row2, g_out_a.reshape(1, -1),
                            jnp.concatenate([g_out_b.reshape(1, -1), g_out_m.reshape(1, -1)], axis=1),
                            g_post.reshape(1, -1), b_post.reshape(1, -1), jnp.zeros((1, 1024), F32)], axis=0)


def kernel(x, mem, positions, g_emb, b_emb, w_in, g_cq, g_ckv, w_uq, w_ukv, w_mem_kv, g_out_a, g_out_b, g_out_m, w_out, g_post, b_post, loss_target, m_g_emb, m_b_emb, m_w_in, m_g_cq, m_g_ckv, m_w_uq, m_w_ukv, m_w_mem_kv, m_g_out_a, m_g_out_b, m_g_out_m, m_w_out, m_g_post, m_b_post, v_g_emb, v_b_emb, v_w_in, v_g_cq, v_g_ckv, v_w_uq, v_w_ukv, v_w_mem_kv, v_g_out_a, v_g_out_b, v_g_out_m, v_w_out, v_g_post, v_b_post):
    w_rest = _pack_rest(w_uq, w_ukv, w_mem_kv, w_out)
    w_in_t = w_in[0].T
    w_in_b = jnp.pad(w_in_t.astype(BF16), ((0, ROWS_IN - SHARD_ROWS), (0, 0)))
    gains = (g_emb.reshape(1, -1), b_emb.reshape(1, -1), g_cq, g_ckv, g_out_a, g_out_b, g_out_m, g_post, b_post)
    rope_consts = (_rope_consts(16, 0, 64), _rope_consts(32, 64, 128))
    h, trig, gathered_in = _ln_fwd(x.reshape(-1, D_MODEL), gains[0], gains[1],
                                   positions.reshape(-1, 1).astype(F32), *rope_consts,
                                   ride=_gather_ride(w_in_b.reshape(2, HALF_IN, 1024)))
    w_in_arr_t = _arranged_w_in(gathered_in.reshape(4, ROWS_IN, 1024))
    proj, gathered_rest = _mm(h, w_in_arr_t, F32, 1024, 2048, 1024, "in_proj", mode="nt",
                              ride=_gather_ride(w_rest.astype(BF16).reshape(2, HALF_REST, 1024)))
    weights = _rest_weights(gathered_rest.reshape(4, ROWS_REST, 1024))
    operands, x2, small_rows = _forward_backward(h, proj, trig, rope_consts, x, mem, loss_target, weights, gains)

    core = lax.axis_index("c").astype(jnp.int32).reshape(1)
    chip = (2 * lax.axis_index("x") + lax.axis_index("y")).astype(jnp.int32).reshape(1)
    sf_in, sb_in, sf_rest, sb_rest = _weight_grads(operands, core)
    grad_x, dg_emb, db_emb, rb_in, rb_rest = _dh_scatter(operands[0], w_in_arr_t, x2, operands[3], gains[0],
                                                         sb_in, sb_rest)
    gh_in = _chip_sum(sf_in, rb_in, chip, HALF_IN, HALF_IN // 2, "chip_sum_in")
    gh_rest = _chip_sum(sf_rest, rb_rest, chip, HALF_REST, HALF_REST, "chip_sum_rest")
    grad_in, grad_rest = _join_halves(gh_in, gh_rest)
    grad_in = grad_in.reshape(ROWS_IN, 1024)
    grad_rest = grad_rest.reshape(ROWS_REST, 1024)

    big_in = _adamw(grad_in, w_in_t, m_w_in[0].T, v_w_in[0].T, SHARD_ROWS // 3, "adamw_in")
    uq, ukv, wmem, wout = _adamw_pieces(
        grad_rest, w_rest, _pack_rest(m_w_uq, m_w_ukv, m_w_mem_kv, m_w_out),
        _pack_rest(v_w_uq, v_w_ukv, v_w_mem_kv, v_w_out), REST_PIECES, "adamw_rest")
    small_sum = _allreduce_small(_small_block(dg_emb, db_emb, small_rows))
    sm = _adamw_pieces(
        small_sum,
        _pack_small(g_emb, b_emb, g_cq, g_ckv, g_out_a, g_out_b, g_out_m, g_post, b_post),
        _pack_small(m_g_emb, m_b_emb, m_g_cq, m_g_ckv, m_g_out_a, m_g_out_b, m_g_out_m, m_g_post, m_b_post),
        _pack_small(v_g_emb, v_b_emb, v_g_cq, v_g_ckv, v_g_out_a, v_g_out_b, v_g_out_m, v_g_post, v_b_post),
        SMALL_PIECES, "adamw_small")
    loss = small_sum[2, 384]

    def ordered(kind):
        s_gemb, s_bemb, s_gcq, s_gckv, s_ga, s_gb, s_gm, s_gpost, s_bpost = [piece[kind] for piece in sm]
        return [s_gemb.reshape(-1), s_bemb.reshape(-1), big_in[kind].T[None], s_gcq, s_gckv,
                uq[kind].reshape(192, 256).T[None], ukv[kind].reshape(1, 128, 256), wmem[kind][None], s_ga, s_gb,
                s_gm, wout[kind][None], s_gpost, s_bpost]

    return (loss, grad_x.reshape(x.shape), *ordered(0), *ordered(1), *ordered(2), *ordered(3))
```

```python
import functools
import math

import jax
import jax.numpy as jnp
import numpy as np
from jax import lax
from jax.experimental import pallas as pl
from jax.experimental.pallas import tpu as pltpu

F32 = jnp.float32
BF16 = jnp.bfloat16
MESH = pl.DeviceIdType.MESH
ANY = pl.BlockSpec(memory_space=pl.ANY)
IN_VMEM = pl.BlockSpec(memory_space=pltpu.VMEM)

D_MODEL = 1024
A_WIDTH = 1024
MLA_HEADS = 8
MLA_Q_RANK = 256
MLA_KV_RANK = 128
MLA_QK_DIM = 96
MEM_WIDTH = 512
N_MEM = 256
ROPE_THETA = 500000.0
NORM_EPS = 1e-5
NEG_INF = -1e30
DEEPNORM_ALPHA = 2.0 ** 0.25
DILATED = ((64, 1), (256, 4), (1024, 16))

ADAM_LR = 0.001
ADAM_B1 = 0.9
ADAM_B2 = 0.999
ADAM_EPS = 1e-08
ADAM_WD = 0.01
ADAM_STEP = 10

LANES = 128
VMEM_LIMIT = 56 * 1024 * 1024
LOG2E = math.log2(math.e)
LN2 = math.log(2.0)

PROJ_W = 6144
COL_CQ = 4096
COL_BG = 4608
COL_MQ = 5120
COL_MG = 5632

SHARD_ROWS = 1512
ROWS_IN = 1536
ROWS_UQ, ROWS_UKV, ROWS_MEM, ROWS_OUT = 48, 32, 256, 512
ROWS_USED = ROWS_UQ + ROWS_UKV + ROWS_MEM + ROWS_OUT
ROWS_REST = 864
HALF_IN = ROWS_IN // 2
HALF_REST = ROWS_REST // 2
REST_PIECES = ((0, 48, 0, 1024), (48, 80, 0, 1024), (80, 336, 0, 1024), (336, 848, 0, 1024))
SMALL_PIECES = ((0, 1, 0, 1024), (1, 2, 0, 1024), (2, 3, 0, 256), (2, 3, 256, 384), (3, 4, 0, 1024), (4, 5, 0, 512),
                (4, 5, 512, 1024), (5, 6, 0, 1024), (6, 7, 0, 1024))


def _params(sem=None, vmem=VMEM_LIMIT):
    return pltpu.CompilerParams(dimension_semantics=sem, vmem_limit_bytes=vmem)


def _dot(a, b):
    return jnp.dot(a, b, preferred_element_type=F32)


def _dot_nt(a, b):
    return lax.dot_general(a, b, (((1,), (1,)), ((), ())), preferred_element_type=F32)


def _dot_tn(a, b):
    return lax.dot_general(a, b, (((0,), (0,)), ((), ())), preferred_element_type=F32)


def _ln_hat(x):
    mu = jnp.mean(x, axis=-1, keepdims=True)
    xc = x - mu
    var = jnp.mean(xc * xc, axis=-1, keepdims=True)
    rstd = lax.rsqrt(var + NORM_EPS)
    return xc * rstd, rstd


def _ln_bwd_rows(dxh, xh, rstd):
    return rstd * (dxh - jnp.mean(dxh, axis=-1, keepdims=True) - xh * jnp.mean(dxh * xh, axis=-1, keepdims=True))


def _rms_hat(x, width):
    ms = jnp.sum(x * x, axis=-1, keepdims=True) * (1.0 / width)
    r = lax.rsqrt(ms + NORM_EPS)
    return x * r, r


def _rms_bwd(u, xh, r, width):
    return r * (u - xh * (jnp.sum(u * xh, axis=-1, keepdims=True) * (1.0 / width)))


def _colsum(v):
    return jnp.sum(v, axis=0, keepdims=True)


def _rope_tables(cos, sin, consts):
    return cos, sin * consts[2:3, :], -sin * consts[1:2, :]


def _rope(x, tables, half, inverse=False):
    c, s_up, s_dn = tables
    if inverse:
        s_up, s_dn = -s_up, -s_dn
    return x * c + pltpu.roll(x, half, 1) * s_up + pltpu.roll(x, LANES - half, 1) * s_dn


def _ln_fwd(x, g, b, pos, rope_a, rope_b, tm=512, ride=None):
    t, d = x.shape
    n_in = len(ride.args) if ride else 0
    n_out = len(ride.out_shapes) if ride else 0
    steps = t // tm

    def body(x_ref, g_ref, b_ref, pos_ref, ra_ref, rb_ref, *rest):
        h_ref, trig_ref = rest[n_in], rest[n_in + 1]
        if ride:
            i = pl.program_id(0)
            ride.run(i == 0, i == steps - 1, rest[:n_in], rest[n_in + 2:n_in + 2 + n_out], rest[n_in + 2 + n_out:])
        xh, _ = _ln_hat(x_ref[...])
        h_ref[...] = (xh * g_ref[...] + b_ref[...]).astype(BF16)
        for j, consts in enumerate((ra_ref, rb_ref)):
            ang = pos_ref[...] * consts[0:1, :]
            trig_ref[:, 2 * j * LANES:(2 * j + 1) * LANES] = jnp.cos(ang)
            trig_ref[:, (2 * j + 1) * LANES:(2 * j + 2) * LANES] = jnp.sin(ang)

    row = pl.BlockSpec((1, d), lambda i: (0, 0))
    tile = pl.BlockSpec((tm, d), lambda i: (i, 0))
    consts = pl.BlockSpec((8, LANES), lambda i: (0, 0))
    trig_tile = pl.BlockSpec((tm, 4 * LANES), lambda i: (i, 0))
    in_specs = [tile, row, row, pl.BlockSpec((tm, 1), lambda i: (i, 0)), consts, consts]
    shapes = (jax.ShapeDtypeStruct((t, d), BF16), jax.ShapeDtypeStruct((t, 4 * LANES), F32))
    if not ride:
        return pl.pallas_call(
            body, name="ln_fwd", grid=(steps,), out_shape=shapes, in_specs=in_specs, out_specs=(tile, trig_tile),
            compiler_params=_params(("parallel",)),
        )(x, g, b, pos, rope_a, rope_b)
    return pl.pallas_call(
        body, name="ln_fwd", grid=(steps,),
        out_shape=(*shapes, *ride.out_shapes),
        in_specs=in_specs + ride.in_specs, out_specs=(tile, trig_tile) + (ANY,) * n_out,
        scratch_shapes=ride.scratch(),
        compiler_params=_params(("arbitrary",)),
    )(x, g, b, pos, rope_a, rope_b, *ride.args)


class _Ride:
    def __init__(self, args, out_shapes, sem_counts, plan, in_specs=None):
        self.args, self.out_shapes, self.plan = list(args), list(out_shapes), plan
        self.sem_counts = sem_counts
        self.in_specs = in_specs or [ANY] * len(self.args)

    def scratch(self):
        return [pltpu.SemaphoreType.DMA((n,)) for n in self.sem_counts]

    def run(self, first, last, in_refs, out_refs, sems, middle=None):
        def stage(k):
            stages = self.plan(in_refs, out_refs, *sems)
            if k == 0 or len(stages) == 3:
                return stages[k]
            return (lambda: None) if k == 1 else stages[1]

        @pl.when(first)
        def _():
            stage(0)()

        if middle is not None:
            @pl.when(middle)
            def _():
                stage(1)()

        @pl.when(last)
        def _():
            if middle is None:
                stage(1)()
            stage(2)()


def _mm(a, b, out_dtype, tm, tn, tk, name, mode="nn", ride=None):
    if mode == "tn":
        k, m = a.shape
    else:
        m, k = a.shape
    n = b.shape[0] if mode == "nt" else b.shape[1]
    nk = k // tk
    nj, ni = n // tn, m // tm
    n_in = len(ride.args) if ride else 0
    n_out = len(ride.out_shapes) if ride else 0

    def body(a_ref, b_ref, *rest):
        o_ref = rest[n_in]
        acc_ref = rest[n_in + 1 + n_out]
        if ride:
            j, i, kk = pl.program_id(0), pl.program_id(1), pl.program_id(2)
            step = (j * ni + i) * nk + kk
            total = nj * ni * nk
            ride.run(step == 0, step == total - 1, rest[:n_in], rest[n_in + 1:n_in + 1 + n_out],
                     rest[n_in + 2 + n_out:], middle=(step == (2 * total) // 3) if total >= 3 else None)
        av = a_ref[...].astype(BF16)
        bv = b_ref[...].astype(BF16)
        part = _dot_tn(av, bv) if mode == "tn" else _dot_nt(av, bv) if mode == "nt" else _dot(av, bv)
        if nk == 1:
            o_ref[...] = part.astype(out_dtype)
        else:
            kk = pl.program_id(2)

            @pl.when(kk == 0)
            def _():
                acc_ref[...] = part

            @pl.when(kk > 0)
            def _():
                acc_ref[...] += part

            @pl.when(kk == nk - 1)
            def _():
                o_ref[...] = acc_ref[...].astype(out_dtype)

    a_spec = (pl.BlockSpec((tk, tm), lambda j, i, kk: (kk, i)) if mode == "tn"
              else pl.BlockSpec((tm, tk), lambda j, i, kk: (i, kk)))
    b_spec = (pl.BlockSpec((tn, tk), lambda j, i, kk: (j, kk)) if mode == "nt"
              else pl.BlockSpec((tk, tn), lambda j, i, kk: (kk, j)))
    o_spec = pl.BlockSpec((tm, tn), lambda j, i, kk: (i, j))
    o_shape = jax.ShapeDtypeStruct((m, n), out_dtype)
    if not ride:
        return pl.pallas_call(
            body, name=name, grid=(nj, ni, nk), out_shape=o_shape, in_specs=[a_spec, b_spec], out_specs=o_spec,
            scratch_shapes=[pltpu.VMEM((tm, tn), F32)],
            compiler_params=_params(("parallel", "parallel", "arbitrary")),
        )(a, b)
    return pl.pallas_call(
        body, name=name, grid=(nj, ni, nk),
        out_shape=(o_shape, *ride.out_shapes),
        in_specs=[a_spec, b_spec] + ride.in_specs,
        out_specs=(o_spec,) + (ANY,) * n_out,
        scratch_shapes=[pltpu.VMEM((tm, tn), F32)] + ride.scratch(),
        compiler_params=_params(("arbitrary", "arbitrary", "arbitrary")),
    )(a, b, *ride.args)


def _prep(proj, trig, w_uq, w_ukv, g_cq, g_ckv, rope_a, rope_b, scales, tm=256):
    t = proj.shape[0]
    sc_a, sc_b, sc_m = (s * LOG2E for s in scales)

    def body(aq_ref, ak_ref, av_ref, bs_ref, mq_ref, trig_ref, wuq_ref, wukv_ref, gcq_ref, gckv_ref,
             ra_ref, rb_ref, qa_ref, ka_ref, va_ref, qb_ref, kb_ref, vb_ref, qm_ref, cqn_ref, ckvn_ref):
        ta = _rope_tables(trig_ref[:, 0:LANES], trig_ref[:, LANES:2 * LANES], ra_ref[...])
        tb = _rope_tables(trig_ref[:, 2 * LANES:3 * LANES], trig_ref[:, 3 * LANES:4 * LANES], rb_ref[...])
        for j in range(A_WIDTH // LANES):
            sl = slice(j * LANES, (j + 1) * LANES)
            qa_ref[:, sl] = (_rope(aq_ref[:, sl], ta, 8) * sc_a).astype(BF16)
            ka_ref[:, sl] = _rope(ak_ref[:, sl], ta, 8).astype(BF16)
        va_ref[...] = av_ref[...].astype(BF16)
        qm_ref[...] = (mq_ref[...] * sc_m).astype(BF16)

        cq_hat, _ = _rms_hat(bs_ref[:, 0:MLA_Q_RANK], MLA_Q_RANK)
        cqn = (cq_hat * gcq_ref[...]).astype(BF16)
        cqn_ref[...] = cqn
        ckv_hat, _ = _rms_hat(bs_ref[:, MLA_Q_RANK:MLA_Q_RANK + MLA_KV_RANK], MLA_KV_RANK)
        ckvn = (ckv_hat * gckv_ref[...]).astype(BF16)
        ckvn_ref[...] = ckvn
        qfull = _dot_nt(cqn, wuq_ref[...])
        kv = _dot(ckvn, wukv_ref[...])
        kr = _rope(bs_ref[:, 384:512], tb, 16)
        lane = lax.broadcasted_iota(jnp.int32, (1, LANES), 1)
        low = lane < 64
        for h in range(MLA_HEADS):
            sl = slice(h * LANES, (h + 1) * LANES)
            qb_ref[:, sl] = (_rope(qfull[:, sl], tb, 16) * sc_b).astype(BF16)
            kb_ref[:, sl] = jnp.where(low, kv[:, sl], kr).astype(BF16)
            vb_ref[:, sl] = jnp.where(low, 0.0, kv[:, sl]).astype(BF16)

    def col(width, idx):
        return pl.BlockSpec((tm, width), lambda i: (i, idx))

    def full(shape):
        return pl.BlockSpec(shape, lambda i: (0, 0))

    wide = jax.ShapeDtypeStruct((t, 1024), BF16)
    return pl.pallas_call(
        body, name="prep", grid=(t // tm,),
        out_shape=(wide, wide, wide, wide, wide, wide,
                   jax.ShapeDtypeStruct((t, MEM_WIDTH), BF16),
                   jax.ShapeDtypeStruct((t, MLA_Q_RANK), BF16),
                   jax.ShapeDtypeStruct((t, MLA_KV_RANK), BF16)),
        in_specs=[col(1024, 0), col(1024, 1), col(1024, 2), col(512, COL_CQ // 512), col(512, COL_MQ // 512),
                  pl.BlockSpec((tm, 4 * LANES), lambda i: (i, 0)),
                  full((1024, MLA_Q_RANK)), full((MLA_KV_RANK, 1024)),
                  full((1, MLA_Q_RANK)), full((1, MLA_KV_RANK)), full((8, LANES)), full((8, LANES))],
        out_specs=(col(1024, 0),) * 6 + (col(MEM_WIDTH, 0), col(MLA_Q_RANK, 0), col(MLA_KV_RANK, 0)),
        compiler_params=_params(("parallel",)),
    )(proj, proj, proj, proj, proj, trig, w_uq, w_ukv, g_cq, g_ckv, rope_a, rope_b)


def _attn_fwd(q, k, v, *, nb, s, sk, heads, hpb, voff, bq, name):
    nq = s // bq
    width = hpb * LANES
    vblk = voff // hpb

    def body(q_ref, k_ref, v_ref, o_ref, lse_ref):
        for h in range(hpb):
            sl = slice(h * LANES, (h + 1) * LANES)
            sc = _dot_nt(q_ref[:, sl], k_ref[:, sl])
            m = jnp.max(sc, axis=1, keepdims=True)
            p = jnp.exp2(sc - m)
            l = jnp.sum(p, axis=1, keepdims=True)
            o_ref[:, sl] = _dot(p.astype(BF16), v_ref[:, sl]) / l
            lse_ref[:, sl] = jnp.broadcast_to(m + jnp.log(l) * LOG2E, (bq, LANES))

    out = jax.ShapeDtypeStruct((nb * s, heads * LANES), F32)
    ospec = pl.BlockSpec((bq, width), lambda b, i, g: (b * nq + i, g))
    return pl.pallas_call(
        body, name=name, grid=(nb, nq, heads // hpb),
        out_shape=(out, out),
        in_specs=[ospec, pl.BlockSpec((sk, width), lambda b, i, g: (b, g)),
                  pl.BlockSpec((sk, width), lambda b, i, g: (b, vblk + g))],
        out_specs=(ospec, ospec),
        compiler_params=_params(("parallel", "parallel", "parallel")),
    )(q, k, v)


def _attn_bwd(q, k, v, o, do, lse, *, nb, s, sk, heads, hpb, voff, scale, bq, name):
    nq = s // bq
    width = hpb * LANES
    vblk = voff // hpb

    def body(q_ref, k_ref, v_ref, o_ref, do_ref, lse_ref, dq_ref, dk_ref, dv_ref, dk_acc, dv_acc):
        i = pl.program_id(2)

        @pl.when(i == 0)
        def _():
            dk_acc[...] = jnp.zeros_like(dk_acc)
            dv_acc[...] = jnp.zeros_like(dv_acc)

        for h in range(hpb):
            sl = slice(h * LANES, (h + 1) * LANES)
            qh = q_ref[:, sl]
            kk = k_ref[:, sl]
            doh = do_ref[:, sl]
            delta = jnp.sum(doh.astype(F32) * o_ref[:, sl], axis=1, keepdims=True)
            p = jnp.exp2(_dot_nt(qh, kk) - lse_ref[:, h * LANES:h * LANES + 1])
            ds = (p * (_dot_nt(doh, v_ref[:, sl]) - delta)).astype(BF16)
            dq_ref[:, sl] = (_dot(ds, kk) * scale).astype(BF16)
            dk_acc[:, sl] += _dot_tn(ds, qh)
            dv_acc[:, sl] += _dot_tn(p.astype(BF16), doh)

        @pl.when(i == nq - 1)
        def _():
            dk_ref[...] = (dk_acc[...] * LN2).astype(BF16)
            dv_ref[...] = dv_acc[...].astype(BF16)

    qspec = pl.BlockSpec((bq, width), lambda b, g, i: (b * nq + i, g))
    kv_spec = pl.BlockSpec((sk, width), lambda b, g, i: (b, g))
    dq_shape = jax.ShapeDtypeStruct((nb * s, heads * LANES), BF16)
    dkv_shape = jax.ShapeDtypeStruct((nb * sk, heads * LANES), BF16)
    return pl.pallas_call(
        body, name=name, grid=(nb, heads // hpb, nq),
        out_shape=(dq_shape, dkv_shape, dkv_shape),
        in_specs=[qspec, kv_spec, pl.BlockSpec((sk, width), lambda b, g, i: (b, vblk + g)), qspec, qspec, qspec],
        out_specs=(qspec, kv_spec, kv_spec),
        scratch_shapes=[pltpu.VMEM((sk, width), F32), pltpu.VMEM((sk, width), F32)],
        compiler_params=_params(("parallel", "parallel", "arbitrary")),
    )(q, k, v, o, do, lse)


BAND_Q = 128
BAND_WIN = 256


def _band_start(i, s):
    return min(max(i * BAND_Q - 64, 0), s - BAND_WIN)


def _to_pattern_order(src_ref, dst_ref, stage_ref, s, d):
    length = s // d
    stage_ref[...] = src_ref[...].astype(F32)
    for r in range(d):
        dst_ref[r * length:(r + 1) * length, :] = stage_ref[pl.ds(r, length, stride=d), :].astype(dst_ref.dtype)


def _dilated_fwd(q, k, v, bias, bias_index, *, nb, s, name):
    nblk = s // BAND_Q
    npat = len(DILATED)

    def body(q_ref, k_ref, v_ref, bias_ref, o_ref, lse_ref, stage_ref, qp_ref, kp_ref, vp_ref, op_ref, lp_ref,
             on_ref, ln_ref):
        lane = lax.broadcasted_iota(jnp.int32, (1, LANES), 1)
        first = lane < 64
        for p, (_, d) in enumerate(DILATED):
            if d == 1:
                qs, ks, vs = q_ref, k_ref, v_ref
            else:
                for src, dst in ((q_ref, qp_ref), (k_ref, kp_ref), (v_ref, vp_ref)):
                    _to_pattern_order(src, dst, stage_ref, s, d)
                qs, ks, vs = qp_ref, kp_ref, vp_ref
            for i in range(nblk):
                u0 = i * BAND_Q
                st = _band_start(i, s)
                qi = qs[u0:u0 + BAND_Q, :]
                kw = ks[st:st + BAND_WIN, :]
                vw = vs[st:st + BAND_WIN, :]
                zero = jnp.zeros_like(qi)
                q2 = jnp.concatenate([jnp.where(first, qi, zero), jnp.where(first, zero, qi)], axis=0)
                sc = _dot_nt(q2, kw)
                b = bias_ref[bias_index[p][i]]
                halves = []
                for h in range(2):
                    sh = sc[h * BAND_Q:(h + 1) * BAND_Q] + b
                    m = jnp.max(sh, axis=1, keepdims=True)
                    pr = jnp.exp2(sh - m)
                    l = jnp.sum(pr, axis=1, keepdims=True)
                    halves.append((pr.astype(BF16), l, m + jnp.log(l) * LOG2E))
                o2 = _dot(jnp.concatenate([halves[0][0], halves[1][0]], axis=0), vw)
                o_blk = jnp.where(first, o2[:BAND_Q] / halves[0][1], o2[BAND_Q:] / halves[1][1])
                lse_blk = jnp.where(first, jnp.broadcast_to(halves[0][2], (BAND_Q, LANES)),
                                    jnp.broadcast_to(halves[1][2], (BAND_Q, LANES)))
                op_ref[p, u0:u0 + BAND_Q, :] = o_blk
                lp_ref[p, u0:u0 + BAND_Q, :] = lse_blk
            if d > 1:
                length = s // d
                for r in range(d):
                    on_ref.at[p - 1][pl.ds(r, length, stride=d), :] = op_ref[p, r * length:(r + 1) * length, :]
                    ln_ref.at[p - 1][pl.ds(r, length, stride=d), :] = lp_ref[p, r * length:(r + 1) * length, :]
        lses = [lp_ref[0]] + [ln_ref[p] for p in range(npat - 1)]
        outs = [op_ref[0]] + [on_ref[p] for p in range(npat - 1)]
        m = functools.reduce(jnp.maximum, lses)
        ws = [jnp.exp2(l - m) for l in lses]
        den = functools.reduce(lambda a, c: a + c, ws)
        o_ref[...] = functools.reduce(lambda a, c: a + c, [w * o for w, o in zip(ws, outs)]) / den
        lse_ref[...] = m + jnp.log(den) * LOG2E

    blk = pl.BlockSpec((s, LANES), lambda b, g: (b, g))
    out = jax.ShapeDtypeStruct((nb * s, A_WIDTH), F32)
    return pl.pallas_call(
        body, name=name, grid=(nb, A_WIDTH // LANES),
        out_shape=(out, out),
        in_specs=[blk, blk, blk, pl.BlockSpec(bias.shape, lambda b, g: (0, 0, 0))],
        out_specs=(blk, blk),
        scratch_shapes=[pltpu.VMEM((s, LANES), F32), pltpu.VMEM((s, LANES), BF16), pltpu.VMEM((s, LANES), BF16),
                        pltpu.VMEM((s, LANES), BF16), pltpu.VMEM((npat, s, LANES), F32),
                        pltpu.VMEM((npat, s, LANES), F32), pltpu.VMEM((npat - 1, s, LANES), F32),
                        pltpu.VMEM((npat - 1, s, LANES), F32)],
        compiler_params=_params(("parallel", "parallel")),
    )(q, k, v, bias)


def _dilated_bwd(q, k, v, o, do, lse, bias, bias_index, *, nb, s, scale, name):
    nblk = s // BAND_Q
    npat = len(DILATED)

    def body(q_ref, k_ref, v_ref, o_ref, do_ref, lse_ref, bias_ref, dq_out, dk_out, dv_out,
             stage_ref, rs_ref, qp_ref, kp_ref, vp_ref, dop_ref, rsp_ref, dqp_ref, dkp_ref, dvp_ref,
             dq_ref, dk_ref, dv_ref):
        lane = lax.broadcasted_iota(jnp.int32, (1, LANES), 1)
        first = lane < 64
        prod = do_ref[...].astype(F32) * o_ref[...]
        d0 = jnp.sum(jnp.where(first, prod, 0.0), axis=1, keepdims=True)
        d1 = jnp.sum(jnp.where(first, 0.0, prod), axis=1, keepdims=True)
        delta = jnp.where(first, jnp.broadcast_to(d0, (s, LANES)), jnp.broadcast_to(d1, (s, LANES)))
        rs_ref[...] = jnp.where((lane & 32) == 0, lse_ref[...], delta)
        for p, (_, d) in enumerate(DILATED):
            length = s // d
            if d == 1:
                qs, ks, vs, dos, rss = q_ref, k_ref, v_ref, do_ref, rs_ref
                dqs, dks, dvs = dq_ref, dk_ref, dv_ref
            else:
                for src, dst in ((q_ref, qp_ref), (k_ref, kp_ref), (v_ref, vp_ref), (do_ref, dop_ref),
                                 (rs_ref, rsp_ref)):
                    _to_pattern_order(src, dst, stage_ref, s, d)
                qs, ks, vs, dos, rss = qp_ref, kp_ref, vp_ref, dop_ref, rsp_ref
                dqs, dks, dvs = dqp_ref, dkp_ref, dvp_ref
            dks[...] = jnp.zeros((s, LANES), F32)
            dvs[...] = jnp.zeros((s, LANES), F32)
            for i in range(nblk):
                u0 = i * BAND_Q
                st = _band_start(i, s)
                qi = qs[u0:u0 + BAND_Q, :]
                doi = dos[u0:u0 + BAND_Q, :]
                kw = ks[st:st + BAND_WIN, :]
                vw = vs[st:st + BAND_WIN, :]
                zero = jnp.zeros_like(qi)
                q2 = jnp.concatenate([jnp.where(first, qi, zero), jnp.where(first, zero, qi)], axis=0)
                do2 = jnp.concatenate([jnp.where(first, doi, zero), jnp.where(first, zero, doi)], axis=0)
                sc = _dot_nt(q2, kw)
                dp = _dot_nt(do2, vw)
                b = bias_ref[bias_index[p][i]]
                rs_i = rss[u0:u0 + BAND_Q, :]
                ps, dss = [], []
                for h in range(2):
                    rows = slice(h * BAND_Q, (h + 1) * BAND_Q)
                    pr = jnp.exp2(sc[rows] + b - rs_i[:, 64 * h:64 * h + 1])
                    ps.append(pr.astype(BF16))
                    dss.append((pr * (dp[rows] - rs_i[:, 64 * h + 32:64 * h + 33])).astype(BF16))
                p2 = jnp.concatenate(ps, axis=0)
                ds2 = jnp.concatenate(dss, axis=0)
                dq2 = _dot(ds2, kw)
                dqs[u0:u0 + BAND_Q, :] = jnp.where(first, dq2[:BAND_Q], dq2[BAND_Q:]) * scale
                dks[st:st + BAND_WIN, :] += _dot_tn(ds2, q2)
                dvs[st:st + BAND_WIN, :] += _dot_tn(p2, do2)
            if d > 1:
                for dst, src in ((dq_ref, dqp_ref), (dk_ref, dkp_ref), (dv_ref, dvp_ref)):
                    for r in range(d):
                        dst[pl.ds(r, length, stride=d), :] += src[r * length:(r + 1) * length, :]
        dq_out[...] = dq_ref[...].astype(BF16)
        dk_out[...] = (dk_ref[...] * LN2).astype(BF16)
        dv_out[...] = dv_ref[...].astype(BF16)

    blk = pl.BlockSpec((s, LANES), lambda b, g: (b, g))
    out = jax.ShapeDtypeStruct((nb * s, A_WIDTH), BF16)
    f32_buf = pltpu.VMEM((s, LANES), F32)
    bf_buf = pltpu.VMEM((s, LANES), BF16)
    return pl.pallas_call(
        body, name=name, grid=(nb, A_WIDTH // LANES),
        out_shape=(out, out, out),
        in_specs=[blk] * 6 + [pl.BlockSpec(bias.shape, lambda b, g: (0, 0, 0))],
        out_specs=(blk, blk, blk),
        scratch_shapes=[f32_buf, f32_buf, bf_buf, bf_buf, bf_buf, bf_buf] + [f32_buf] * 7,
        compiler_params=_params(("parallel", "parallel")),
    )(q, k, v, o, do, lse, bias)


def _post(x, ya, ybp, ym, proj, target, w_out, g_emb, b_emb, g_a, g_b, g_m, g_post, b_post, tm=256):
    t = x.shape[0]

    def body(x_ref, ya_ref, yb_ref, ym_ref, ga_ref, gb_ref, gm_ref, tg_ref, wo_ref,
             ge_ref, be_ref, goa_ref, gob_ref, gom_ref, gp_ref, bp_ref,
             y_ref, dz_ref, doa_ref, dob_ref, dom_ref, dga_ref, dgb_ref, dgm_ref,
             loss_ref, dgp_ref, dbp_ref, dgoa_ref, dgob_ref, dgom_ref):
        i = pl.program_id(0)

        @pl.when(i == 0)
        def _():
            for r in (loss_ref, dgp_ref, dbp_ref, dgoa_ref, dgob_ref, dgom_ref):
                r[...] = jnp.zeros_like(r)

        lane = lax.broadcasted_iota(jnp.int32, (1, LANES), 1)
        low = lane < 64
        xh0, _ = _ln_hat(x_ref[...])
        h = xh0 * ge_ref[...] + be_ref[...]

        ybp_v = yb_ref[...]
        yb = jnp.concatenate(
            [jnp.where(low, pltpu.roll(ybp_v[:, 2 * j * LANES:(2 * j + 1) * LANES], 64, 1),
                       ybp_v[:, (2 * j + 1) * LANES:(2 * j + 2) * LANES]) for j in range(4)], axis=1)

        def gated(raw, gate, gain, width):
            xh, r = _rms_hat(raw, width)
            n = xh * gain
            sg = 1.0 / (1.0 + jnp.exp(-gate))
            return xh, r, n, sg, n * (gate * sg)

        gate_a, gate_b, gate_m = ga_ref[...], gb_ref[...], gm_ref[...]
        xh_a, r_a, n_a, sg_a, y_a = gated(ya_ref[...], gate_a, goa_ref[...], A_WIDTH)
        xh_b, r_b, n_b, sg_b, y_b = gated(yb, gate_b, gob_ref[...], 512)
        xh_m, r_m, n_m, sg_m, y_m = gated(ym_ref[...], gate_m, gom_ref[...], 512)
        y = jnp.concatenate([y_a, y_b, y_m], axis=1).astype(BF16)
        y_ref[...] = y
        z = DEEPNORM_ALPHA * h + _dot(y, wo_ref[...])
        zh, rstd = _ln_hat(z)
        err = zh * gp_ref[...] + bp_ref[...] - tg_ref[...]
        rows = jnp.sum(err * err, axis=1, keepdims=True)
        loss_ref[...] += jnp.broadcast_to(jnp.sum(rows, axis=0, keepdims=True) * (0.5 / D_MODEL), (1, LANES))
        dout = err * (1.0 / D_MODEL)
        dgp_ref[...] += _colsum(dout * zh)
        dbp_ref[...] += _colsum(dout)
        dz = _ln_bwd_rows(dout * gp_ref[...], zh, rstd)
        dz_ref[...] = dz
        dy = _dot_nt(dz.astype(BF16), wo_ref[...])

        def gated_bwd(dyg, xh, r, n, sg, gate, gain, width, dgain_ref):
            dn = dyg * (gate * sg)
            dgate = dyg * n * (sg * (1.0 + gate * (1.0 - sg)))
            dgain_ref[...] += _colsum(dn * xh)
            return _rms_bwd(dn * gain, xh, r, width), dgate

        dya, dgate_a = gated_bwd(dy[:, 0:1024], xh_a, r_a, n_a, sg_a, gate_a, goa_ref[...], A_WIDTH, dgoa_ref)
        dyb, dgate_b = gated_bwd(dy[:, 1024:1536], xh_b, r_b, n_b, sg_b, gate_b, gob_ref[...], 512, dgob_ref)
        dym, dgate_m = gated_bwd(dy[:, 1536:2048], xh_m, r_m, n_m, sg_m, gate_m, gom_ref[...], 512, dgom_ref)
        doa_ref[...] = dya.astype(BF16)
        dom_ref[...] = dym.astype(BF16)
        dga_ref[...] = dgate_a.astype(BF16)
        dgb_ref[...] = dgate_b.astype(BF16)
        dgm_ref[...] = dgate_m.astype(BF16)
        for j in range(4):
            blk = dyb[:, j * LANES:(j + 1) * LANES]
            dob_ref[:, 2 * j * LANES:(2 * j + 1) * LANES] = jnp.where(low, 0.0, pltpu.roll(blk, 64, 1)).astype(BF16)
            dob_ref[:, (2 * j + 1) * LANES:(2 * j + 2) * LANES] = jnp.where(low, 0.0, blk).astype(BF16)

    def col(width, idx):
        return pl.BlockSpec((tm, width), lambda i: (i, idx))

    def full(shape):
        return pl.BlockSpec(shape, lambda i: (0, 0))

    def acc(width):
        return jax.ShapeDtypeStruct((1, width), F32)

    return pl.pallas_call(
        body, name="post", grid=(t // tm,),
        out_shape=(jax.ShapeDtypeStruct((t, 2048), BF16), jax.ShapeDtypeStruct((t, 1024), F32),
                   jax.ShapeDtypeStruct((t, 1024), BF16), jax.ShapeDtypeStruct((t, 1024), BF16),
                   jax.ShapeDtypeStruct((t, 512), BF16),
                   jax.ShapeDtypeStruct((t, 1024), BF16), jax.ShapeDtypeStruct((t, 512), BF16),
                   jax.ShapeDtypeStruct((t, 512), BF16),
                   acc(LANES), acc(1024), acc(1024), acc(1024), acc(512), acc(512)),
        in_specs=[col(1024, 0), col(1024, 0), col(1024, 0), col(512, 0),
                  col(1024, 3), col(512, COL_BG // 512), col(512, COL_MG // 512), col(1024, 0),
                  full((2048, 1024)),
                  full((1, 1024)), full((1, 1024)), full((1, 1024)), full((1, 512)), full((1, 512)),
                  full((1, 1024)), full((1, 1024))],
        out_specs=(col(2048, 0), col(1024, 0), col(1024, 0), col(1024, 0), col(512, 0),
                   col(1024, 0), col(512, 0), col(512, 0),
                   full((1, LANES)), full((1, 1024)), full((1, 1024)), full((1, 1024)), full((1, 512)),
                   full((1, 512))),
        compiler_params=_params(("arbitrary",)),
    )(x, ya, ybp, ym, proj, proj, proj, target, w_out, g_emb, b_emb, g_a, g_b, g_m, g_post, b_post)


def _prep_bwd(dqa, dka, dva, dqb, dkb, dvb, dqm, dga, dgb, dgm, proj, trig, w_uq, w_ukv, g_cq, g_ckv,
              rope_a, rope_b, tm=256):
    t = proj.shape[0]

    def body(dqa_ref, dka_ref, dva_ref, dqb_ref, dkb_ref, dvb_ref, dqm_ref, dga_ref, dgb_ref, dgm_ref,
             bs_ref, trig_ref, wuq_ref, wukv_ref, gcq_ref, gckv_ref, ra_ref, rb_ref,
             dproj_ref, dqf_ref, dkv_ref, dgcq_ref, dgckv_ref):
        i = pl.program_id(0)

        @pl.when(i == 0)
        def _():
            dgcq_ref[...] = jnp.zeros_like(dgcq_ref)
            dgckv_ref[...] = jnp.zeros_like(dgckv_ref)

        ta = _rope_tables(trig_ref[:, 0:LANES], trig_ref[:, LANES:2 * LANES], ra_ref[...])
        tb = _rope_tables(trig_ref[:, 2 * LANES:3 * LANES], trig_ref[:, 3 * LANES:4 * LANES], rb_ref[...])
        for j in range(A_WIDTH // LANES):
            sl = slice(j * LANES, (j + 1) * LANES)
            dproj_ref[:, j * LANES:(j + 1) * LANES] = (
                _rope(dqa_ref[:, sl].astype(F32), ta, 8, inverse=True).astype(BF16))
            dproj_ref[:, 1024 + j * LANES:1024 + (j + 1) * LANES] = (
                _rope(dka_ref[:, sl].astype(F32), ta, 8, inverse=True).astype(BF16))
        dproj_ref[:, 2048:3072] = dva_ref[...]
        dproj_ref[:, 3072:4096] = dga_ref[...]

        lane = lax.broadcasted_iota(jnp.int32, (1, LANES), 1)
        low = lane < 64
        rope_lanes = (lane >= 64) & (lane < 96)
        dkr = jnp.zeros((tm, LANES), F32)
        for h in range(MLA_HEADS):
            sl = slice(h * LANES, (h + 1) * LANES)
            dqf_ref[:, sl] = _rope(dqb_ref[:, sl].astype(F32), tb, 16, inverse=True).astype(BF16)
            dk_h = dkb_ref[:, sl]
            dkv_ref[:, sl] = jnp.where(low, dk_h, dvb_ref[:, sl])
            dkr = dkr + jnp.where(rope_lanes, dk_h.astype(F32), 0.0)
        dkr = _rope(dkr, tb, 16, inverse=True)

        cq_hat, r_q = _rms_hat(bs_ref[:, 0:MLA_Q_RANK], MLA_Q_RANK)
        dcqn = _dot(dqf_ref[...], wuq_ref[...])
        dgcq_ref[...] += _colsum(dcqn * cq_hat)
        dproj_ref[:, COL_CQ:COL_CQ + 256] = _rms_bwd(dcqn * gcq_ref[...], cq_hat, r_q, MLA_Q_RANK).astype(BF16)
        ckv_hat, r_kv = _rms_hat(bs_ref[:, MLA_Q_RANK:MLA_Q_RANK + MLA_KV_RANK], MLA_KV_RANK)
        dckvn = _dot_nt(dkv_ref[...], wukv_ref[...])
        dgckv_ref[...] += _colsum(dckvn * ckv_hat)
        dproj_ref[:, COL_CQ + 256:COL_CQ + 384] = (
            _rms_bwd(dckvn * gckv_ref[...], ckv_hat, r_kv, MLA_KV_RANK).astype(BF16))
        dproj_ref[:, COL_CQ + 384:COL_CQ + 512] = dkr.astype(BF16)
        dproj_ref[:, COL_BG:COL_BG + 512] = dgb_ref[...]
        dproj_ref[:, COL_MQ:COL_MQ + 512] = dqm_ref[...]
        dproj_ref[:, COL_MG:COL_MG + 512] = dgm_ref[...]

    def col(width, idx):
        return pl.BlockSpec((tm, width), lambda i: (i, idx))

    def full(shape):
        return pl.BlockSpec(shape, lambda i: (0, 0))

    return pl.pallas_call(
        body, name="prep_bwd", grid=(t // tm,),
        out_shape=(jax.ShapeDtypeStruct((t, PROJ_W), BF16), jax.ShapeDtypeStruct((t, 1024), BF16),
                   jax.ShapeDtypeStruct((t, 1024), BF16),
                   jax.ShapeDtypeStruct((1, MLA_Q_RANK), F32), jax.ShapeDtypeStruct((1, MLA_KV_RANK), F32)),
        in_specs=[col(1024, 0)] * 6 + [col(512, 0), col(1024, 0), col(512, 0), col(512, 0),
                  col(512, COL_CQ // 512), pl.BlockSpec((tm, 4 * LANES), lambda i: (i, 0)),
                  full((1024, MLA_Q_RANK)), full((MLA_KV_RANK, 1024)),
                  full((1, MLA_Q_RANK)), full((1, MLA_KV_RANK)), full((8, LANES)), full((8, LANES))],
        out_specs=(col(PROJ_W, 0), col(1024, 0), col(1024, 0), full((1, MLA_Q_RANK)), full((1, MLA_KV_RANK))),
        compiler_params=_params(("arbitrary",)),
    )(dqa, dka, dva, dqb, dkb, dvb, dqm, dga, dgb, dgm, proj, trig, w_uq, w_ukv, g_cq, g_ckv, rope_a, rope_b)


def _adamw_math(gv, w, m, v):
    m_new = ADAM_B1 * m + (1.0 - ADAM_B1) * gv
    v_new = ADAM_B2 * v + (1.0 - ADAM_B2) * (gv * gv)
    m_hat = m_new / (1.0 - ADAM_B1 ** ADAM_STEP)
    v_hat = v_new / (1.0 - ADAM_B2 ** ADAM_STEP)
    return -ADAM_LR * (m_hat / (jnp.sqrt(v_hat) + ADAM_EPS) + ADAM_WD * w), m_new, v_new


def _adamw(g, w, m, v, tr, name):
    r, cols = w.shape

    def body(g_ref, w_ref, m_ref, v_ref, go_ref, d_ref, nm_ref, nv_ref):
        gv = g_ref[...]
        go_ref[...] = gv
        d_ref[...], nm_ref[...], nv_ref[...] = _adamw_math(gv, w_ref[...], m_ref[...], v_ref[...])

    tile = pl.BlockSpec((tr, cols), lambda i: (i, 0))
    shape = jax.ShapeDtypeStruct((r, cols), F32)
    return pl.pallas_call(
        body, name=name, grid=(r // tr,),
        out_shape=(shape,) * 4, in_specs=[tile] * 4, out_specs=(tile,) * 4,
        compiler_params=_params(("parallel",)),
    )(g, w, m, v)


def _adamw_pieces(g, w, m, v, pieces, name):
    shapes = [jax.ShapeDtypeStruct((r1 - r0, c1 - c0), F32) for r0, r1, c0, c1 in pieces]

    def body(g_ref, w_ref, m_ref, v_ref, *outs):
        gv = g_ref[...]
        results = (gv,) + _adamw_math(gv, w_ref[...], m_ref[...], v_ref[...])
        for kind, full in enumerate(results):
            for p, (r0, r1, c0, c1) in enumerate(pieces):
                outs[kind * len(pieces) + p][...] = full[r0:r1, c0:c1]

    flat = pl.pallas_call(
        body, name=name, out_shape=tuple(shapes) * 4,
        in_specs=[IN_VMEM] * 4, out_specs=tuple([IN_VMEM] * (4 * len(pieces))),
        compiler_params=_params(None),
    )(g, w, m, v)
    return [[flat[kind * len(pieces) + p] for kind in range(4)] for p in range(len(pieces))]


def _core_sum(g, recv, core, rows, tr, name, ride=None):
    cols = g.shape[2]
    nblk = rows // tr
    n_in = len(ride.args) if ride else 0
    n_out = len(ride.out_shapes) if ride else 0

    def body(c_ref, g_ref, r_ref, *rest):
        sf_ref, sb_ref = rest[n_in], rest[n_in + 1]
        if ride:
            j, i = pl.program_id(0), pl.program_id(1)
            ride.run((j == 0) & (i == 0), (j == 3) & (i == nblk - 1), rest[:n_in],
                     rest[n_in + 2:n_in + 2 + n_out], rest[n_in + 2 + n_out:])
        tot = g_ref[...] + r_ref[...]
        sf_ref[...] = tot
        sb_ref[...] = tot.astype(BF16)

    half = pl.BlockSpec((None, tr, cols), lambda j, i, c_ref: (j, i, 0))
    shapes = (jax.ShapeDtypeStruct((4, rows, cols), F32), jax.ShapeDtypeStruct((4, rows, cols), BF16))
    return pl.pallas_call(
        body, name=name,
        grid_spec=pltpu.PrefetchScalarGridSpec(
            num_scalar_prefetch=1, grid=(4, nblk),
            in_specs=[pl.BlockSpec((None, tr, cols), lambda j, i, c_ref: (j, c_ref[0] * nblk + i, 0)), half]
            + (ride.in_specs if ride else []),
            out_specs=(half, half) + (ANY,) * n_out,
            scratch_shapes=ride.scratch() if ride else []),
        out_shape=shapes + tuple(ride.out_shapes if ride else ()),
        compiler_params=_params(("arbitrary", "arbitrary") if ride else ("parallel", "parallel")),
    )(core, g, recv, *(ride.args if ride else ()))


def _half_to_sibling(g4):
    def plan(in_refs, out_refs, send_sems, recv_sems):
        x, y, c = _position()
        cp = pltpu.make_async_remote_copy(
            src_ref=in_refs[0].at[:, 1 - c], dst_ref=out_refs[0], send_sem=send_sems.at[0],
            recv_sem=recv_sems.at[0], device_id=(x, y, 1 - c), device_id_type=MESH)

        def finish():
            cp.wait_recv()
            cp.wait_send()

        return cp.start, finish

    return _Ride([g4], [jax.ShapeDtypeStruct((4, g4.shape[2], 1024), F32)], (1, 1), plan)


def _gather_plan(src_ref, dst_ref, send_sems, recv_sems, local_sems):
    x, y, c = _position()
    me = 2 * x + y
    local = pltpu.make_async_copy(src_ref, dst_ref.at[me], local_sems.at[0])

    def over_ici(k, src, chip):
        return pltpu.make_async_remote_copy(
            src_ref=src, dst_ref=dst_ref.at[chip, c], send_sem=send_sems.at[k - 1], recv_sem=recv_sems.at[k - 1],
            device_id=(x ^ (k >> 1), y ^ (k & 1), c), device_id_type=MESH)

    def to_sibling(k, half):
        piece = dst_ref.at[me ^ k, half]
        return pltpu.make_async_remote_copy(
            src_ref=piece, dst_ref=piece, send_sem=send_sems.at[2 + k], recv_sem=recv_sems.at[2 + k],
            device_id=(x, y, 1 - c), device_id_type=MESH)

    sends = [over_ici(k, src_ref.at[c], me) for k in (1, 2, 3)]

    def start():
        local.start()
        for cp in sends:
            cp.start()

    def relay():
        for k in (1, 2, 3):
            over_ici(k, dst_ref.at[me ^ k, c], me ^ k).wait_recv()
            to_sibling(k, c).start()

    def finish():
        for k in (1, 2, 3):
            to_sibling(k, 1 - c).wait_recv()
        for cp in sends + [to_sibling(k, c) for k in (1, 2, 3)]:
            cp.wait_send()
        local.wait()

    return start, relay, finish


def _gather_ride(shard):
    def plan(in_refs, out_refs, send_sems, recv_sems, local_sems):
        return _gather_plan(in_refs[0], out_refs[0], send_sems, recv_sems, local_sems)

    return _Ride([shard], [jax.ShapeDtypeStruct((4,) + shard.shape, shard.dtype)], (6, 6, 1), plan,
                 in_specs=[IN_VMEM])


def _chip_sum(sf, recv, chip, rows, tr, name):
    cols = sf.shape[2]

    def body(me_ref, sf_ref, r_ref, out_ref):
        acc = sf_ref[...]
        for k in range(3):
            acc = acc + r_ref[k].astype(F32)
        out_ref[...] = acc

    return pl.pallas_call(
        body, name=name,
        grid_spec=pltpu.PrefetchScalarGridSpec(
            num_scalar_prefetch=1, grid=(rows // tr,),
            in_specs=[pl.BlockSpec((None, tr, cols), lambda i, me_ref: (me_ref[0], i, 0)),
                      pl.BlockSpec((3, tr, cols), lambda i, me_ref: (0, i, 0))],
            out_specs=pl.BlockSpec((tr, cols), lambda i, me_ref: (i, 0))),
        out_shape=jax.ShapeDtypeStruct((rows, cols), F32),
        compiler_params=_params(("parallel",)),
    )(chip, sf, recv)


def _position():
    return lax.axis_index("x"), lax.axis_index("y"), lax.axis_index("c")


def _dh_scatter(dproj, w_in_arr_t, x, dz, g, sb_in, sb_rest, tm=1024, tk=1024):
    t, d = x.shape
    nk = dproj.shape[1] // tk
    ni = t // tm

    def body(dp_ref, w_ref, x_ref, dz_ref, g_ref, sbin_ref, sbrest_ref,
             dx_ref, dg_ref, db_ref, rin_ref, rrest_ref, acc_ref, send_sems, recv_sems):
        i = pl.program_id(0)
        kk = pl.program_id(1)
        px, py, pc = _position()
        me = 2 * px + py
        srcs = (sbin_ref, sbrest_ref)
        dsts = (rin_ref, rrest_ref)

        def copy(a, k):
            return pltpu.make_async_remote_copy(
                src_ref=srcs[a].at[me ^ k], dst_ref=dsts[a].at[k - 1],
                send_sem=send_sems.at[3 * a + k - 1], recv_sem=recv_sems.at[3 * a + k - 1],
                device_id=(px ^ (k >> 1), py ^ (k & 1), pc), device_id_type=MESH)

        pairs = [(a, k) for a in range(2) for k in (1, 2, 3)]

        @pl.when((i == 0) & (kk == 0))
        def _():
            dg_ref[...] = jnp.zeros_like(dg_ref)
            db_ref[...] = jnp.zeros_like(db_ref)
            for a, k in pairs:
                copy(a, k).start()

        part = _dot(dp_ref[...], w_ref[...])

        @pl.when(kk == 0)
        def _():
            acc_ref[...] = part

        @pl.when(kk > 0)
        def _():
            acc_ref[...] += part

        @pl.when(kk == nk - 1)
        def _():
            xh, rstd = _ln_hat(x_ref[...])
            dht = acc_ref[...] + DEEPNORM_ALPHA * dz_ref[...]
            dg_ref[...] += _colsum(dht * xh)
            db_ref[...] += _colsum(dht)
            dx_ref[...] = _ln_bwd_rows(dht * g_ref[...], xh, rstd)

        @pl.when((i == ni - 1) & (kk == nk - 1))
        def _():
            for a, k in pairs:
                copy(a, k).wait_recv()
            for a, k in pairs:
                copy(a, k).wait_send()

    tile = pl.BlockSpec((tm, d), lambda i, kk: (i, 0))
    row = pl.BlockSpec((1, d), lambda i, kk: (0, 0))
    return pl.pallas_call(
        body, name="dh_scatter", grid=(ni, nk),
        out_shape=(jax.ShapeDtypeStruct((t, d), F32), jax.ShapeDtypeStruct((1, d), F32),
                   jax.ShapeDtypeStruct((1, d), F32),
                   jax.ShapeDtypeStruct((3, HALF_IN, 1024), BF16),
                   jax.ShapeDtypeStruct((3, HALF_REST, 1024), BF16)),
        in_specs=[pl.BlockSpec((tm, tk), lambda i, kk: (i, kk)), pl.BlockSpec((tk, d), lambda i, kk: (kk, 0)),
                  tile, tile, row, ANY, ANY],
        out_specs=(tile, row, row, ANY, ANY),
        scratch_shapes=[pltpu.VMEM((tm, d), F32), pltpu.SemaphoreType.DMA((6,)), pltpu.SemaphoreType.DMA((6,))],
        compiler_params=_params(("arbitrary", "arbitrary")),
    )(dproj, w_in_arr_t, x, dz, g, sb_in, sb_rest)


def _join_halves(gh_in, gh_rest):
    def body(hin_ref, hrest_ref, oin_ref, orest_ref, send_sems, recv_sems, local_sems):
        x, y, c = _position()
        srcs = (hin_ref, hrest_ref)
        dsts = (oin_ref, orest_ref)

        def rows(a, half):
            return dsts[a].at[half]

        local = [pltpu.make_async_copy(srcs[a], rows(a, c), local_sems.at[a]) for a in range(2)]
        remote = [pltpu.make_async_remote_copy(
            src_ref=srcs[a], dst_ref=rows(a, c), send_sem=send_sems.at[a], recv_sem=recv_sems.at[a],
            device_id=(x, y, 1 - c), device_id_type=MESH) for a in range(2)]
        for cp in local + remote:
            cp.start()
        for a in range(2):
            pltpu.make_async_remote_copy(
                src_ref=srcs[a], dst_ref=rows(a, 1 - c), send_sem=send_sems.at[a], recv_sem=recv_sems.at[a],
                device_id=(x, y, 1 - c), device_id_type=MESH).wait_recv()
        for cp in remote:
            cp.wait_send()
        for cp in local:
            cp.wait()

    return pl.pallas_call(
        body, name="join_halves",
        out_shape=(jax.ShapeDtypeStruct((2, HALF_IN, 1024), F32),
                   jax.ShapeDtypeStruct((2, HALF_REST, 1024), F32)),
        in_specs=[IN_VMEM, IN_VMEM], out_specs=(ANY, ANY),
        scratch_shapes=[pltpu.SemaphoreType.DMA((2,)), pltpu.SemaphoreType.DMA((2,)), pltpu.SemaphoreType.DMA((2,))],
    )(gh_in, gh_rest)


def _allreduce_small(vec):
    def body(vec_ref, out_ref, all_ref, send_sems, recv_sems):
        x, y, c = _position()
        me = 4 * x + 2 * y + c
        all_ref[me] = vec_ref[...]

        def copy(k, slot):
            return pltpu.make_async_remote_copy(
                src_ref=vec_ref, dst_ref=all_ref.at[slot], send_sem=send_sems.at[k - 1], recv_sem=recv_sems.at[k - 1],
                device_id=(x ^ (k >> 2), y ^ ((k >> 1) & 1), c ^ (k & 1)), device_id_type=MESH)

        copies = [copy(k, me) for k in range(1, 8)]
        for cp in copies:
            cp.start()
        for k in range(1, 8):
            copy(k, me ^ k).wait_recv()
        for cp in copies:
            cp.wait_send()
        total = all_ref[0]
        for d in range(1, 8):
            total = total + all_ref[d]
        out_ref[...] = total

    return pl.pallas_call(
        body, name="allreduce_small",
        out_shape=jax.ShapeDtypeStruct(vec.shape, vec.dtype),
        in_specs=[pl.BlockSpec(memory_space=pltpu.VMEM)], out_specs=pl.BlockSpec(memory_space=pltpu.VMEM),
        scratch_shapes=[pltpu.VMEM((8,) + vec.shape, vec.dtype), pltpu.SemaphoreType.DMA((7,)),
                        pltpu.SemaphoreType.DMA((7,))],
    )(vec)


def _pack_rest(w_uq, w_ukv, w_mem, w_out):
    rows = jnp.concatenate([w_uq[0].T.reshape(-1, 1024), w_ukv.reshape(-1, 1024), w_mem.reshape(-1, 1024),
                            w_out.reshape(-1, 1024)], axis=0)
    return jnp.pad(rows, ((0, ROWS_REST - ROWS_USED), (0, 0)))


def _arranged_w_in(g_in):
    z = functools.partial(jnp.zeros, dtype=g_in.dtype)
    cut = 4480 - 2 * SHARD_ROWS
    return jnp.concatenate(
        [g_in[0, :SHARD_ROWS], g_in[1, :SHARD_ROWS], g_in[2, :cut], z((64, 1024)), g_in[2, cut:cut + 32],
         z((32, 1024)), g_in[2, cut + 32:SHARD_ROWS], g_in[3, :SHARD_ROWS]], axis=0)


def _rest_weights(g_rest):
    w_uq_t = g_rest[:, 0:ROWS_UQ].reshape(768, 256)
    w_uq_pad_t = jnp.pad(w_uq_t.reshape(MLA_HEADS, MLA_QK_DIM, 256), ((0, 0), (0, 32), (0, 0))).reshape(1024, 256)
    w_ukv = jnp.concatenate([g_rest[j, ROWS_UQ:ROWS_UQ + ROWS_UKV].reshape(128, 256) for j in range(4)], axis=1)
    lo = ROWS_UQ + ROWS_UKV
    w_mem = g_rest[:, lo:lo + ROWS_MEM].reshape(4 * ROWS_MEM, 1024)
    w_out = g_rest[:, lo + ROWS_MEM:lo + ROWS_MEM + ROWS_OUT].reshape(4 * ROWS_OUT, 1024)
    return w_uq_pad_t, w_ukv, w_mem, w_out


def _split_in(dw_in_arr_t):
    a = dw_in_arr_t
    gap = jnp.zeros((ROWS_IN - SHARD_ROWS, 1024), a.dtype)
    nat = 4608 - 96
    pieces = [a[:SHARD_ROWS], gap, a[SHARD_ROWS:2 * SHARD_ROWS], gap,
              a[2 * SHARD_ROWS:4480], a[4544:4576], a[4608:4608 + 3 * SHARD_ROWS - nat], gap,
              a[4608 + 3 * SHARD_ROWS - nat:], gap]
    return jnp.concatenate(pieces, axis=0).reshape(4, ROWS_IN, 1024)


def _split_rest(dw_uq_pad_t, dw_ukv, dw_mem, dw_out):
    dw_uq_t = dw_uq_pad_t.reshape(MLA_HEADS, LANES, 256)[:, :MLA_QK_DIM].reshape(4, ROWS_UQ, 1024)
    parts = [dw_uq_t, dw_ukv.reshape(128, 4, 256).transpose(1, 0, 2).reshape(4, ROWS_UKV, 1024),
             dw_mem.reshape(4, ROWS_MEM, 1024), dw_out.reshape(4, ROWS_OUT, 1024)]
    return jnp.pad(jnp.concatenate(parts, axis=1), ((0, 0), (0, ROWS_REST - ROWS_USED), (0, 0)))


def _rope_consts(rot, first, period):
    half = rot // 2
    inv_freq = np.float32(ROPE_THETA) ** (-(np.arange(0, rot, 2, dtype=np.float32) / np.float32(rot)))
    lane = np.arange(LANES) % period - first
    in_rot = (lane >= 0) & (lane < rot)
    out = np.zeros((8, LANES), np.float32)
    out[0] = np.where(in_rot, inv_freq[np.clip(lane, 0, rot - 1) % half], 0.0)
    out[1] = in_rot & (lane < half)
    out[2] = in_rot & (lane >= half)
    return jnp.asarray(out)


def _band_bias(s):
    nblk = s // BAND_Q
    starts = np.array([_band_start(i, s) for i in range(nblk)])
    uq = (np.arange(nblk)[:, None] * BAND_Q + np.arange(BAND_Q)[None, :])[:, :, None]
    uk = (starts[:, None] + np.arange(BAND_WIN)[None, :])[:, None, :]
    tiles, index, seen = [], [], {}
    for _, d in DILATED:
        length = s // d
        ok = (uq // length == uk // length) & (np.abs(uq - uk) <= 64)
        row = []
        for i in range(nblk):
            key = ok[i].tobytes()
            if key not in seen:
                seen[key] = len(tiles)
                tiles.append(np.where(ok[i], 0.0, NEG_INF).astype(np.float32))
            row.append(seen[key])
        index.append(row)
    return jnp.asarray(np.stack(tiles, axis=0)), index


def _forward_backward(h, proj, trig, rope_consts, x, mem, target, weights, gains):
    w_uq_pad_t, w_ukv, w_mem, w_out = weights
    g_emb, b_emb, g_cq, g_ckv, g_out_a, g_out_b, g_out_m, g_post, b_post = gains
    nb, s, d = x.shape
    t = nb * s
    x2 = x.reshape(t, d)
    mem2 = mem.reshape(nb * N_MEM, d)
    tgt2 = target.reshape(t, d)
    rope_a, rope_b = rope_consts
    bias, bias_index = _band_bias(s)
    scales = (0.125, MLA_QK_DIM ** -0.5, 128 ** -0.5)

    qa, ka, va, qb, kb, vb, qm, cqn, ckvn = _prep(proj, trig, w_uq_pad_t, w_ukv, g_cq, g_ckv, rope_a, rope_b, scales)
    mkv = _mm(mem2, w_mem, BF16, nb * N_MEM, 1024, 1024, "mem_kv")

    cfg_b = dict(nb=nb, s=s, sk=s, heads=8, voff=0, bq=256)
    cfg_m = dict(nb=nb, s=s, sk=N_MEM, heads=4, hpb=2, voff=4, bq=1024)
    ya, lse_a = _dilated_fwd(qa, ka, va, bias, bias_index, nb=nb, s=s, name="attn_a_fwd")
    yb, lse_b = _attn_fwd(qb, kb, vb, name="attn_b_fwd", hpb=4, **cfg_b)
    ym, lse_m = _attn_fwd(qm, mkv, mkv, name="attn_m_fwd", **cfg_m)

    (y, dz, doa, dob, dom, dga, dgb, dgm, loss, dg_post, db_post, dg_a, dg_b, dg_m) = _post(
        x2, ya, yb, ym, proj, tgt2, w_out, g_emb, b_emb, g_out_a, g_out_b, g_out_m, g_post, b_post)

    dqa, dka, dva = _dilated_bwd(qa, ka, va, ya, doa, lse_a, bias, bias_index, nb=nb, s=s, scale=scales[0],
                                 name="attn_a_bwd")
    dqb, dkb, dvb = _attn_bwd(qb, kb, vb, yb, dob, lse_b, name="attn_b_bwd", scale=scales[1], hpb=2, **cfg_b)
    dqm, dmk, dmv = _attn_bwd(qm, mkv, mkv, ym, dom, lse_m, name="attn_m_bwd", scale=scales[2], **cfg_m)
    dmkv = jnp.concatenate([dmk, dmv], axis=1)

    dproj, dqf, dkv, dg_cq, dg_ckv = _prep_bwd(
        dqa, dka, dva, dqb, dkb, dvb, dqm, dga, dgb, dgm, proj, trig, w_uq_pad_t, w_ukv, g_cq, g_ckv, rope_a, rope_b)

    small_rows = (dg_cq, dg_ckv, loss, dg_a, dg_b, dg_m, dg_post, db_post)
    return (dproj, h, y, dz, dqf, cqn, ckvn, dkv, mem2, dmkv), x2, small_rows


def _weight_grads(operands, core):
    dproj, h, y, dz, dqf, cqn, ckvn, dkv, mem2, dmkv = operands
    dw_in_arr_t = _mm(dproj, h, F32, 1024, 1024, 4096, "dw_in", mode="tn")
    g_in = _split_in(dw_in_arr_t)
    dw_out, r_in = _mm(y, dz, F32, 1024, 1024, 2048, "dw_out", mode="tn",
                       ride=_half_to_sibling(g_in.reshape(4, 2, HALF_IN, 1024)))
    dw_uq_pad_t = _mm(dqf, cqn, F32, 1024, 256, 4096, "dw_uq", mode="tn")
    dw_ukv = _mm(ckvn, dkv, F32, 128, 1024, 4096, "dw_ukv", mode="tn")
    dw_mem = _mm(mem2, dmkv, F32, 1024, 1024, mem2.shape[0], "dw_mem", mode="tn")
    g_rest = _split_rest(dw_uq_pad_t, dw_ukv, dw_mem, dw_out)
    sf_in, sb_in, r_rest = _core_sum(g_in, r_in, core, HALF_IN, HALF_IN // 2, "core_sum_in",
                                     ride=_half_to_sibling(g_rest.reshape(4, 2, HALF_REST, 1024)))
    sf_rest, sb_rest = _core_sum(g_rest, r_rest, core, HALF_REST, HALF_REST, "core_sum_rest")
    return sf_in, sb_in, sf_rest, sb_rest


def _small_block(dg_emb, db_emb, small_rows):
    dg_cq, dg_ckv, loss, dg_a, dg_b, dg_m, dg_post, db_post = small_rows
    row2 = jnp.concatenate([dg_cq, dg_ckv, loss, jnp.zeros((1, 512), F32)], axis=1)
    return jnp.concatenate([dg_emb, db_emb, row2, dg_a, jnp.concatenate([dg_b, dg_m], axis=1), dg_post, db_post,
                            jnp.zeros((1, 1024), F32)], axis=0)


def _pack_small(g_emb, b_emb, g_cq, g_ckv, g_out_a, g_out_b, g_out_m, g_post, b_post):
    row2 = jnp.concatenate([g_cq.reshape(1, -1), g_ckv.reshape(1, -1), jnp.zeros((1, 640), F32)], axis=1)
    return jnp.concatenate([g_emb.reshape(1, -1), b_emb.reshape(1, -1), row2, g_out_a.reshape(1, -1),
                            jnp.concatenate([g_out_b.reshape(1, -1), g_out_m.reshape(1, -1)], axis=1),
                            g_post.reshape(1, -1), b_post.reshape(1, -1), jnp.zeros((1, 1024), F32)], axis=0)


def kernel(x, mem, positions, g_emb, b_emb, w_in, g_cq, g_ckv, w_uq, w_ukv, w_mem_kv, g_out_a, g_out_b, g_out_m, w_out, g_post, b_post, loss_target, m_g_emb, m_b_emb, m_w_in, m_g_cq, m_g_ckv, m_w_uq, m_w_ukv, m_w_mem_kv, m_g_out_a, m_g_out_b, m_g_out_m, m_w_out, m_g_post, m_b_post, v_g_emb, v_b_emb, v_w_in, v_g_cq, v_g_ckv, v_w_uq, v_w_ukv, v_w_mem_kv, v_g_out_a, v_g_out_b, v_g_out_m, v_w_out, v_g_post, v_b_post):
    w_rest = _pack_rest(w_uq, w_ukv, w_mem_kv, w_out)
    w_in_t = w_in[0].T
    w_in_b = jnp.pad(w_in_t.astype(BF16), ((0, ROWS_IN - SHARD_ROWS), (0, 0)))
    gains = (g_emb.reshape(1, -1), b_emb.reshape(1, -1), g_cq, g_ckv, g_out_a, g_out_b, g_out_m, g_post, b_post)
    rope_consts = (_rope_consts(16, 0, 64), _rope_consts(32, 64, 128))
    h, trig, gathered_in = _ln_fwd(x.reshape(-1, D_MODEL), gains[0], gains[1],
                                   positions.reshape(-1, 1).astype(F32), *rope_consts,
                                   ride=_gather_ride(w_in_b.reshape(2, HALF_IN, 1024)))
    w_in_arr_t = _arranged_w_in(gathered_in.reshape(4, ROWS_IN, 1024))
    proj, gathered_rest = _mm(h, w_in_arr_t, F32, 1024, 2048, 1024, "in_proj", mode="nt",
                              ride=_gather_ride(w_rest.astype(BF16).reshape(2, HALF_REST, 1024)))
    weights = _rest_weights(gathered_rest.reshape(4, ROWS_REST, 1024))
    operands, x2, small_rows = _forward_backward(h, proj, trig, rope_consts, x, mem, loss_target, weights, gains)

    core = lax.axis_index("c").astype(jnp.int32).reshape(1)
    chip = (2 * lax.axis_index("x") + lax.axis_index("y")).astype(jnp.int32).reshape(1)
    sf_in, sb_in, sf_rest, sb_rest = _weight_grads(operands, core)
    grad_x, dg_emb, db_emb, rb_in, rb_rest = _dh_scatter(operands[0], w_in_arr_t, x2, operands[3], gains[0],
                                                         sb_in, sb_rest)
    gh_in = _chip_sum(sf_in, rb_in, chip, HALF_IN, HALF_IN // 2, "chip_sum_in")
    gh_rest = _chip_sum(sf_rest, rb_rest, chip, HALF_REST, HALF_REST, "chip_sum_rest")
    grad_in, grad_rest = _join_halves(gh_in, gh_rest)
    grad_in = grad_in.reshape(ROWS_IN, 1024)
    grad_rest = grad_rest.reshape(ROWS_REST, 1024)

    big_in = _adamw(grad_in, w_in_t, m_w_in[0].T, v_w_in[0].T, SHARD_ROWS // 3, "adamw_in")
    uq, ukv, wmem, wout = _adamw_pieces(
        grad_rest, w_rest, _pack_rest(m_w_uq, m_w_ukv, m_w_mem_kv, m_w_out),
        _pack_rest(v_w_uq, v_w_ukv, v_w_mem_kv, v_w_out), REST_PIECES, "adamw_rest")
    small_sum = _allreduce_small(_small_block(dg_emb, db_emb, small_rows))
    sm = _adamw_pieces(
        small_sum,
        _pack_small(g_emb, b_emb, g_cq, g_ckv, g_out_a, g_out_b, g_out_m, g_post, b_post),
        _pack_small(m_g_emb, m_b_emb, m_g_cq, m_g_ckv, m_g_out_a, m_g_out_b, m_g_out_m, m_g_post, m_b_post),
        _pack_small(v_g_emb, v_b_emb, v_g_cq, v_g_ckv, v_g_out_a, v_g_out_b, v_g_out_m, v_g_post, v_b_post),
        SMALL_PIECES, "adamw_small")
    loss = small_sum[2, 384]

    def ordered(kind):
        s_gemb, s_bemb, s_gcq, s_gckv, s_ga, s_gb, s_gm, s_gpost, s_bpost = [piece[kind] for piece in sm]
        return [s_gemb.reshape(-1), s_bemb.reshape(-1), big_in[kind].T[None], s_gcq, s_gckv,
                uq[kind].reshape(192, 256).T[None], ukv[kind].reshape(1, 128, 256), wmem[kind][None], s_ga, s_gb,
                s_gm, wout[kind][None], s_gpost, s_bpost]

    return (loss, grad_x.reshape(x.shape), *ordered(0), *ordered(1), *ordered(2), *ordered(3))
```

```python
import functools
import math

import jax
import jax.numpy as jnp
import numpy as np
from jax import lax
from jax.experimental import pallas as pl
from jax.experimental.pallas import tpu as pltpu

F32 = jnp.float32
BF16 = jnp.bfloat16
MESH = pl.DeviceIdType.MESH
ANY = pl.BlockSpec(memory_space=pl.ANY)
IN_VMEM = pl.BlockSpec(memory_space=pltpu.VMEM)

D_MODEL = 1024
A_WIDTH = 1024
MLA_HEADS = 8
MLA_Q_RANK = 256
MLA_KV_RANK = 128
MLA_QK_DIM = 96
MEM_WIDTH = 512
N_MEM = 256
ROPE_THETA = 500000.0
NORM_EPS = 1e-5
NEG_INF = -1e30
DEEPNORM_ALPHA = 2.0 ** 0.25
DILATED = ((64, 1), (256, 4), (1024, 16))

ADAM_LR = 0.001
ADAM_B1 = 0.9
ADAM_B2 = 0.999
ADAM_EPS = 1e-08
ADAM_WD = 0.01
ADAM_STEP = 10

LANES = 128
VMEM_LIMIT = 56 * 1024 * 1024
LOG2E = math.log2(math.e)
LN2 = math.log(2.0)

PROJ_W = 6144
COL_CQ = 4096
COL_BG = 4608
COL_MQ = 5120
COL_MG = 5632

SHARD_ROWS = 1512
ROWS_IN = 1536
ROWS_UQ, ROWS_UKV, ROWS_MEM, ROWS_OUT = 48, 32, 256, 512
ROWS_USED = ROWS_UQ + ROWS_UKV + ROWS_MEM + ROWS_OUT
ROWS_REST = 864
HALF_IN = ROWS_IN // 2
HALF_REST = ROWS_REST // 2
REST_PIECES = ((0, 48, 0, 1024), (48, 80, 0, 1024), (80, 336, 0, 1024), (336, 848, 0, 1024))
SMALL_PIECES = ((0, 1, 0, 1024), (1, 2, 0, 1024), (2, 3, 0, 256), (2, 3, 256, 384), (3, 4, 0, 1024), (4, 5, 0, 512),
                (4, 5, 512, 1024), (5, 6, 0, 1024), (6, 7, 0, 1024))


def _params(sem=None, vmem=VMEM_LIMIT):
    return pltpu.CompilerParams(dimension_semantics=sem, vmem_limit_bytes=vmem)


def _dot(a, b):
    return jnp.dot(a, b, preferred_element_type=F32)


def _dot_nt(a, b):
    return lax.dot_general(a, b, (((1,), (1,)), ((), ())), preferred_element_type=F32)


def _dot_tn(a, b):
    return lax.dot_general(a, b, (((0,), (0,)), ((), ())), preferred_element_type=F32)


def _ln_hat(x):
    mu = jnp.mean(x, axis=-1, keepdims=True)
    xc = x - mu
    var = jnp.mean(xc * xc, axis=-1, keepdims=True)
    rstd = lax.rsqrt(var + NORM_EPS)
    return xc * rstd, rstd


def _ln_bwd_rows(dxh, xh, rstd):
    return rstd * (dxh - jnp.mean(dxh, axis=-1, keepdims=True) - xh * jnp.mean(dxh * xh, axis=-1, keepdims=True))


def _rms_hat(x, width):
    ms = jnp.sum(x * x, axis=-1, keepdims=True) * (1.0 / width)
    r = lax.rsqrt(ms + NORM_EPS)
    return x * r, r


def _rms_bwd(u, xh, r, width):
    return r * (u - xh * (jnp.sum(u * xh, axis=-1, keepdims=True) * (1.0 / width)))


def _colsum(v):
    return jnp.sum(v, axis=0, keepdims=True)


def _rope_tables(cos, sin, consts):
    return cos, sin * consts[2:3, :], -sin * consts[1:2, :]


def _rope(x, tables, half, inverse=False):
    c, s_up, s_dn = tables
    if inverse:
        s_up, s_dn = -s_up, -s_dn
    return x * c + pltpu.roll(x, half, 1) * s_up + pltpu.roll(x, LANES - half, 1) * s_dn


def _ln_fwd(x, g, b, pos, rope_a, rope_b, tm=512, ride=None):
    t, d = x.shape
    n_in = len(ride.args) if ride else 0
    n_out = len(ride.out_shapes) if ride else 0
    steps = t // tm

    def body(x_ref, g_ref, b_ref, pos_ref, ra_ref, rb_ref, *rest):
        h_ref, trig_ref = rest[n_in], rest[n_in + 1]
        if ride:
            i = pl.program_id(0)
            ride.run(i == 0, i == steps - 1, rest[:n_in], rest[n_in + 2:n_in + 2 + n_out], rest[n_in + 2 + n_out:])
        xh, _ = _ln_hat(x_ref[...])
        h_ref[...] = (xh * g_ref[...] + b_ref[...]).astype(BF16)
        for j, consts in enumerate((ra_ref, rb_ref)):
            ang = pos_ref[...] * consts[0:1, :]
            trig_ref[:, 2 * j * LANES:(2 * j + 1) * LANES] = jnp.cos(ang)
            trig_ref[:, (2 * j + 1) * LANES:(2 * j + 2) * LANES] = jnp.sin(ang)

    row = pl.BlockSpec((1, d), lambda i: (0, 0))
    tile = pl.BlockSpec((tm, d), lambda i: (i, 0))
    consts = pl.BlockSpec((8, LANES), lambda i: (0, 0))
    trig_tile = pl.BlockSpec((tm, 4 * LANES), lambda i: (i, 0))
    in_specs = [tile, row, row, pl.BlockSpec((tm, 1), lambda i: (i, 0)), consts, consts]
    shapes = (jax.ShapeDtypeStruct((t, d), BF16), jax.ShapeDtypeStruct((t, 4 * LANES), F32))
    if not ride:
        return pl.pallas_call(
            body, name="ln_fwd", grid=(steps,), out_shape=shapes, in_specs=in_specs, out_specs=(tile, trig_tile),
            compiler_params=_params(("parallel",)),
        )(x, g, b, pos, rope_a, rope_b)
    return pl.pallas_call(
        body, name="ln_fwd", grid=(steps,),
        out_shape=(*shapes, *ride.out_shapes),
        in_specs=in_specs + ride.in_specs, out_specs=(tile, trig_tile) + (ANY,) * n_out,
        scratch_shapes=ride.scratch(),
        compiler_params=_params(("arbitrary",)),
    )(x, g, b, pos, rope_a, rope_b, *ride.args)


class _Ride:
    def __init__(self, args, out_shapes, sem_counts, plan, in_specs=None):
        self.args, self.out_shapes, self.plan = list(args), list(out_shapes), plan
        self.sem_counts = sem_counts
        self.in_specs = in_specs or [ANY] * len(self.args)

    def scratch(self):
        return [pltpu.SemaphoreType.DMA((n,)) for n in self.sem_counts]

    def run(self, first, last, in_refs, out_refs, sems, middle=None):
        def stage(k):
            stages = self.plan(in_refs, out_refs, *sems)
            if k == 0 or len(stages) == 3:
                return stages[k]
            return (lambda: None) if k == 1 else stages[1]

        @pl.when(first)
        def _():
            stage(0)()

        if middle is not None:
            @pl.when(middle)
            def _():
                stage(1)()

        @pl.when(last)
        def _():
            if middle is None:
                stage(1)()
            stage(2)()


def _mm(a, b, out_dtype, tm, tn, tk, name, mode="nn", ride=None):
    if mode == "tn":
        k, m = a.shape
    else:
        m, k = a.shape
    n = b.shape[0] if mode == "nt" else b.shape[1]
    nk = k // tk
    nj, ni = n // tn, m // tm
    n_in = len(ride.args) if ride else 0
    n_out = len(ride.out_shapes) if ride else 0

    def body(a_ref, b_ref, *rest):
        o_ref = rest[n_in]
        acc_ref = rest[n_in + 1 + n_out]
        if ride:
            j, i, kk = pl.program_id(0), pl.program_id(1), pl.program_id(2)
            step = (j * ni + i) * nk + kk
            total = nj * ni * nk
            ride.run(step == 0, step == total - 1, rest[:n_in], rest[n_in + 1:n_in + 1 + n_out],
                     rest[n_in + 2 + n_out:], middle=(step == (2 * total) // 3) if total >= 3 else None)
        av = a_ref[...].astype(BF16)
        bv = b_ref[...].astype(BF16)
        part = _dot_tn(av, bv) if mode == "tn" else _dot_nt(av, bv) if mode == "nt" else _dot(av, bv)
        if nk == 1:
            o_ref[...] = part.astype(out_dtype)
        else:
            kk = pl.program_id(2)

            @pl.when(kk == 0)
            def _():
                acc_ref[...] = part

            @pl.when(kk > 0)
            def _():
                acc_ref[...] += part

            @pl.when(kk == nk - 1)
            def _():
                o_ref[...] = acc_ref[...].astype(out_dtype)

    a_spec = (pl.BlockSpec((tk, tm), lambda j, i, kk: (kk, i)) if mode == "tn"
              else pl.BlockSpec((tm, tk), lambda j, i, kk: (i, kk)))
    b_spec = (pl.BlockSpec((tn, tk), lambda j, i, kk: (j, kk)) if mode == "nt"
              else pl.BlockSpec((tk, tn), lambda j, i, kk: (kk, j)))
    o_spec = pl.BlockSpec((tm, tn), lambda j, i, kk: (i, j))
    o_shape = jax.ShapeDtypeStruct((m, n), out_dtype)
    if not ride:
        return pl.pallas_call(
            body, name=name, grid=(nj, ni, nk), out_shape=o_shape, in_specs=[a_spec, b_spec], out_specs=o_spec,
            scratch_shapes=[pltpu.VMEM((tm, tn), F32)],
            compiler_params=_params(("parallel", "parallel", "arbitrary")),
        )(a, b)
    return pl.pallas_call(
        body, name=name, grid=(nj, ni, nk),
        out_shape=(o_shape, *ride.out_shapes),
        in_specs=[a_spec, b_spec] + ride.in_specs,
        out_specs=(o_spec,) + (ANY,) * n_out,
        scratch_shapes=[pltpu.VMEM((tm, tn), F32)] + ride.scratch(),
        compiler_params=_params(("arbitrary", "arbitrary", "arbitrary")),
    )(a, b, *ride.args)


def _prep(proj, trig, w_uq, w_ukv, g_cq, g_ckv, rope_a, rope_b, scales, tm=256):
    t = proj.shape[0]
    sc_a, sc_b, sc_m = (s * LOG2E for s in scales)

    def body(aq_ref, ak_ref, av_ref, bs_ref, mq_ref, trig_ref, wuq_ref, wukv_ref, gcq_ref, gckv_ref,
             ra_ref, rb_ref, qa_ref, ka_ref, va_ref, qb_ref, kb_ref, vb_ref, qm_ref, cqn_ref, ckvn_ref):
        ta = _rope_tables(trig_ref[:, 0:LANES], trig_ref[:, LANES:2 * LANES], ra_ref[...])
        tb = _rope_tables(trig_ref[:, 2 * LANES:3 * LANES], trig_ref[:, 3 * LANES:4 * LANES], rb_ref[...])
        for j in range(A_WIDTH // LANES):
            sl = slice(j * LANES, (j + 1) * LANES)
            qa_ref[:, sl] = (_rope(aq_ref[:, sl], ta, 8) * sc_a).astype(BF16)
            ka_ref[:, sl] = _rope(ak_ref[:, sl], ta, 8).astype(BF16)
        va_ref[...] = av_ref[...].astype(BF16)
        qm_ref[...] = (mq_ref[...] * sc_m).astype(BF16)

        cq_hat, _ = _rms_hat(bs_ref[:, 0:MLA_Q_RANK], MLA_Q_RANK)
        cqn = (cq_hat * gcq_ref[...]).astype(BF16)
        cqn_ref[...] = cqn
        ckv_hat, _ = _rms_hat(bs_ref[:, MLA_Q_RANK:MLA_Q_RANK + MLA_KV_RANK], MLA_KV_RANK)
        ckvn = (ckv_hat * gckv_ref[...]).astype(BF16)
        ckvn_ref[...] = ckvn
        qfull = _dot_nt(cqn, wuq_ref[...])
        kv = _dot(ckvn, wukv_ref[...])
        kr = _rope(bs_ref[:, 384:512], tb, 16)
        lane = lax.broadcasted_iota(jnp.int32, (1, LANES), 1)
        low = lane < 64
        for h in range(MLA_HEADS):
            sl = slice(h * LANES, (h + 1) * LANES)
            qb_ref[:, sl] = (_rope(qfull[:, sl], tb, 16) * sc_b).astype(BF16)
            kb_ref[:, sl] = jnp.where(low, kv[:, sl], kr).astype(BF16)
            vb_ref[:, sl] = jnp.where(low, 0.0, kv[:, sl]).astype(BF16)

    def col(width, idx):
        return pl.BlockSpec((tm, width), lambda i: (i, idx))

    def full(shape):
        return pl.BlockSpec(shape, lambda i: (0, 0))

    wide = jax.ShapeDtypeStruct((t, 1024), BF16)
    return pl.pallas_call(
        body, name="prep", grid=(t // tm,),
        out_shape=(wide, wide, wide, wide, wide, wide,
                   jax.ShapeDtypeStruct((t, MEM_WIDTH), BF16),
                   jax.ShapeDtypeStruct((t, MLA_Q_RANK), BF16),
                   jax.ShapeDtypeStruct((t, MLA_KV_RANK), BF16)),
        in_specs=[col(1024, 0), col(1024, 1), col(1024, 2), col(512, COL_CQ // 512), col(512, COL_MQ // 512),
                  pl.BlockSpec((tm, 4 * LANES), lambda i: (i, 0)),
                  full((1024, MLA_Q_RANK)), full((MLA_KV_RANK, 1024)),
                  full((1, MLA_Q_RANK)), full((1, MLA_KV_RANK)), full((8, LANES)), full((8, LANES))],
        out_specs=(col(1024, 0),) * 6 + (col(MEM_WIDTH, 0), col(MLA_Q_RANK, 0), col(MLA_KV_RANK, 0)),
        compiler_params=_params(("parallel",)),
    )(proj, proj, proj, proj, proj, trig, w_uq, w_ukv, g_cq, g_ckv, rope_a, rope_b)


def _attn_fwd(q, k, v, *, nb, s, sk, heads, hpb, voff, bq, name):
    nq = s // bq
    width = hpb * LANES
    vblk = voff // hpb

    def body(q_ref, k_ref, v_ref, o_ref, lse_ref):
        for h in range(hpb):
            sl = slice(h * LANES, (h + 1) * LANES)
            sc = _dot_nt(q_ref[:, sl], k_ref[:, sl])
            m = jnp.max(sc, axis=1, keepdims=True)
            p = jnp.exp2(sc - m)
            l = jnp.sum(p, axis=1, keepdims=True)
            o_ref[:, sl] = _dot(p.astype(BF16), v_ref[:, sl]) / l
            lse_ref[:, sl] = jnp.broadcast_to(m + jnp.log(l) * LOG2E, (bq, LANES))

    out = jax.ShapeDtypeStruct((nb * s, heads * LANES), F32)
    ospec = pl.BlockSpec((bq, width), lambda b, i, g: (b * nq + i, g))
    return pl.pallas_call(
        body, name=name, grid=(nb, nq, heads // hpb),
        out_shape=(out, out),
        in_specs=[ospec, pl.BlockSpec((sk, width), lambda b, i, g: (b, g)),
                  pl.BlockSpec((sk, width), lambda b, i, g: (b, vblk + g))],
        out_specs=(ospec, ospec),
        compiler_params=_params(("parallel", "parallel", "parallel")),
    )(q, k, v)


def _attn_bwd(q, k, v, o, do, lse, *, nb, s, sk, heads, hpb, voff, scale, bq, name):
    nq = s // bq
    width = hpb * LANES
    vblk = voff // hpb

    def body(q_ref, k_ref, v_ref, o_ref, do_ref, lse_ref, dq_ref, dk_ref, dv_ref, dk_acc, dv_acc):
        i = pl.program_id(2)

        @pl.when(i == 0)
        def _():
            dk_acc[...] = jnp.zeros_like(dk_acc)
            dv_acc[...] = jnp.zeros_like(dv_acc)

        for h in range(hpb):
            sl = slice(h * LANES, (h + 1) * LANES)
            qh = q_ref[:, sl]
            kk = k_ref[:, sl]
            doh = do_ref[:, sl]
            delta = jnp.sum(doh.astype(F32) * o_ref[:, sl], axis=1, keepdims=True)
            p = jnp.exp2(_dot_nt(qh, kk) - lse_ref[:, h * LANES:h * LANES + 1])
            ds = (p * (_dot_nt(doh, v_ref[:, sl]) - delta)).astype(BF16)
            dq_ref[:, sl] = (_dot(ds, kk) * scale).astype(BF16)
            dk_acc[:, sl] += _dot_tn(ds, qh)
            dv_acc[:, sl] += _dot_tn(p.astype(BF16), doh)

        @pl.when(i == nq - 1)
        def _():
            dk_ref[...] = (dk_acc[...] * LN2).astype(BF16)
            dv_ref[...] = dv_acc[...].astype(BF16)

    qspec = pl.BlockSpec((bq, width), lambda b, g, i: (b * nq + i, g))
    kv_spec = pl.BlockSpec((sk, width), lambda b, g, i: (b, g))
    dq_shape = jax.ShapeDtypeStruct((nb * s, heads * LANES), BF16)
    dkv_shape = jax.ShapeDtypeStruct((nb * sk, heads * LANES), BF16)
    return pl.pallas_call(
        body, name=name, grid=(nb, heads // hpb, nq),
        out_shape=(dq_shape, dkv_shape, dkv_shape),
        in_specs=[qspec, kv_spec, pl.BlockSpec((sk, width), lambda b, g, i: (b, vblk + g)), qspec, qspec, qspec],
        out_specs=(qspec, kv_spec, kv_spec),
        scratch_shapes=[pltpu.VMEM((sk, width), F32), pltpu.VMEM((sk, width), F32)],
        compiler_params=_params(("parallel", "parallel", "arbitrary")),
    )(q, k, v, o, do, lse)


BAND_Q = 128
BAND_WIN = 256


def _band_start(i, s):
    return min(max(i * BAND_Q - 64, 0), s - BAND_WIN)


def _to_pattern_order(src_ref, dst_ref, stage_ref, s, d):
    length = s // d
    stage_ref[...] = src_ref[...].astype(F32)
    for r in range(d):
        dst_ref[r * length:(r + 1) * length, :] = stage_ref[pl.ds(r, length, stride=d), :].astype(dst_ref.dtype)


def _dilated_fwd(q, k, v, bias, bias_index, *, nb, s, name):
    nblk = s // BAND_Q
    npat = len(DILATED)

    def body(q_ref, k_ref, v_ref, bias_ref, o_ref, lse_ref, *rest):
        ordered = rest[:3 * (npat - 1)]
        stage_ref, op_ref, lp_ref, on_ref, ln_ref = rest[3 * (npat - 1):]
        lane = lax.broadcasted_iota(jnp.int32, (1, LANES), 1)
        first = lane < 64
        for p, (_, d) in enumerate(DILATED):
            if d == 1:
                qs, ks, vs = q_ref, k_ref, v_ref
            else:
                qs, ks, vs = ordered[3 * (p - 1):3 * p]
                for src, dst in ((q_ref, qs), (k_ref, ks), (v_ref, vs)):
                    _to_pattern_order(src, dst, stage_ref, s, d)
            for i in range(nblk):
                u0 = i * BAND_Q
                st = _band_start(i, s)
                qi = qs[u0:u0 + BAND_Q, :]
                kw = ks[st:st + BAND_WIN, :]
                vw = vs[st:st + BAND_WIN, :]
                zero = jnp.zeros_like(qi)
                q2 = jnp.concatenate([jnp.where(first, qi, zero), jnp.where(first, zero, qi)], axis=0)
                sc = _dot_nt(q2, kw)
                b = bias_ref[bias_index[p][i]]
                halves = []
                for h in range(2):
                    sh = sc[h * BAND_Q:(h + 1) * BAND_Q] + b
                    m = jnp.max(sh, axis=1, keepdims=True)
                    pr = jnp.exp2(sh - m)
                    l = jnp.sum(pr, axis=1, keepdims=True)
                    halves.append((pr.astype(BF16), l, m + jnp.log(l) * LOG2E))
                o2 = _dot(jnp.concatenate([halves[0][0], halves[1][0]], axis=0), vw)
                o_blk = jnp.where(first, o2[:BAND_Q] / halves[0][1], o2[BAND_Q:] / halves[1][1])
                lse_blk = jnp.where(first, jnp.broadcast_to(halves[0][2], (BAND_Q, LANES)),
                                    jnp.broadcast_to(halves[1][2], (BAND_Q, LANES)))
                op_ref[p, u0:u0 + BAND_Q, :] = o_blk
                lp_ref[p, u0:u0 + BAND_Q, :] = lse_blk
            if d > 1:
                length = s // d
                for r in range(d):
                    on_ref.at[p - 1][pl.ds(r, length, stride=d), :] = op_ref[p, r * length:(r + 1) * length, :]
                    ln_ref.at[p - 1][pl.ds(r, length, stride=d), :] = lp_ref[p, r * length:(r + 1) * length, :]
        lses = [lp_ref[0]] + [ln_ref[p] for p in range(npat - 1)]
        outs = [op_ref[0]] + [on_ref[p] for p in range(npat - 1)]
        m = functools.reduce(jnp.maximum, lses)
        ws = [jnp.exp2(l - m) for l in lses]
        den = functools.reduce(lambda a, c: a + c, ws)
        o_ref[...] = functools.reduce(lambda a, c: a + c, [w * o for w, o in zip(ws, outs)]) / den
        lse_ref[...] = m + jnp.log(den) * LOG2E

    blk = pl.BlockSpec((s, LANES), lambda b, g: (b, g))
    out = jax.ShapeDtypeStruct((nb * s, A_WIDTH), F32)
    copy = jax.ShapeDtypeStruct((nb * s, A_WIDTH), BF16)
    n_copies = 3 * (npat - 1)
    res = pl.pallas_call(
        body, name=name, grid=(nb, A_WIDTH // LANES),
        out_shape=(out, out) + (copy,) * n_copies,
        in_specs=[blk, blk, blk, pl.BlockSpec(bias.shape, lambda b, g: (0, 0, 0))],
        out_specs=(blk, blk) + (blk,) * n_copies,
        scratch_shapes=[pltpu.VMEM((s, LANES), F32), pltpu.VMEM((npat, s, LANES), F32),
                        pltpu.VMEM((npat, s, LANES), F32), pltpu.VMEM((npat - 1, s, LANES), F32),
                        pltpu.VMEM((npat - 1, s, LANES), F32)],
        compiler_params=_params(("parallel", "parallel")),
    )(q, k, v, bias)
    return res[0], res[1], res[2:]


def _dilated_bwd(q, k, v, ordered, o, do, lse, bias, bias_index, *, nb, s, scale, name):
    nblk = s // BAND_Q
    npat = len(DILATED)
    n_copies = 3 * (npat - 1)

    def body(q_ref, k_ref, v_ref, *rest):
        ordered_refs = rest[:n_copies]
        (o_ref, do_ref, lse_ref, bias_ref, dq_out, dk_out, dv_out, stage_ref, rs_ref, dop_ref, rsp_ref,
         dqp_ref, dkp_ref, dvp_ref, dq_ref, dk_ref, dv_ref) = rest[n_copies:]
        lane = lax.broadcasted_iota(jnp.int32, (1, LANES), 1)
        first = lane < 64
        prod = do_ref[...].astype(F32) * o_ref[...]
        d0 = jnp.sum(jnp.where(first, prod, 0.0), axis=1, keepdims=True)
        d1 = jnp.sum(jnp.where(first, 0.0, prod), axis=1, keepdims=True)
        delta = jnp.where(first, jnp.broadcast_to(d0, (s, LANES)), jnp.broadcast_to(d1, (s, LANES)))
        rs_ref[...] = jnp.where((lane & 32) == 0, lse_ref[...], delta)
        for p, (_, d) in enumerate(DILATED):
            length = s // d
            if d == 1:
                qs, ks, vs, dos, rss = q_ref, k_ref, v_ref, do_ref, rs_ref
                dqs, dks, dvs = dq_ref, dk_ref, dv_ref
            else:
                for src, dst in ((do_ref, dop_ref), (rs_ref, rsp_ref)):
                    _to_pattern_order(src, dst, stage_ref, s, d)
                qs, ks, vs = ordered_refs[3 * (p - 1):3 * p]
                dos, rss = dop_ref, rsp_ref
                dqs, dks, dvs = dqp_ref, dkp_ref, dvp_ref
            dks[...] = jnp.zeros((s, LANES), F32)
            dvs[...] = jnp.zeros((s, LANES), F32)
            for i in range(nblk):
                u0 = i * BAND_Q
                st = _band_start(i, s)
                qi = qs[u0:u0 + BAND_Q, :]
                doi = dos[u0:u0 + BAND_Q, :]
                kw = ks[st:st + BAND_WIN, :]
                vw = vs[st:st + BAND_WIN, :]
                zero = jnp.zeros_like(qi)
                q2 = jnp.concatenate([jnp.where(first, qi, zero), jnp.where(first, zero, qi)], axis=0)
                do2 = jnp.concatenate([jnp.where(first, doi, zero), jnp.where(first, zero, doi)], axis=0)
                sc = _dot_nt(q2, kw)
                dp = _dot_nt(do2, vw)
                b = bias_ref[bias_index[p][i]]
                rs_i = rss[u0:u0 + BAND_Q, :]
                ps, dss = [], []
                for h in range(2):
                    rows = slice(h * BAND_Q, (h + 1) * BAND_Q)
                    pr = jnp.exp2(sc[rows] + b - rs_i[:, 64 * h:64 * h + 1])
                    ps.append(pr.astype(BF16))
                    dss.append((pr * (dp[rows] - rs_i[:, 64 * h + 32:64 * h + 33])).astype(BF16))
                p2 = jnp.concatenate(ps, axis=0)
                ds2 = jnp.concatenate(dss, axis=0)
                dq2 = _dot(ds2, kw)
                dqs[u0:u0 + BAND_Q, :] = jnp.where(first, dq2[:BAND_Q], dq2[BAND_Q:]) * scale
                dks[st:st + BAND_WIN, :] += _dot_tn(ds2, q2)
                dvs[st:st + BAND_WIN, :] += _dot_tn(p2, do2)
            if d > 1:
                for dst, src in ((dq_ref, dqp_ref), (dk_ref, dkp_ref), (dv_ref, dvp_ref)):
                    for r in range(d):
                        dst[pl.ds(r, length, stride=d), :] += src[r * length:(r + 1) * length, :]
        dq_out[...] = dq_ref[...].astype(BF16)
        dk_out[...] = (dk_ref[...] * LN2).astype(BF16)
        dv_out[...] = dv_ref[...].astype(BF16)

    blk = pl.BlockSpec((s, LANES), lambda b, g: (b, g))
    out = jax.ShapeDtypeStruct((nb * s, A_WIDTH), BF16)
    f32_buf = pltpu.VMEM((s, LANES), F32)
    bf_buf = pltpu.VMEM((s, LANES), BF16)
    return pl.pallas_call(
        body, name=name, grid=(nb, A_WIDTH // LANES),
        out_shape=(out, out, out),
        in_specs=[blk] * (6 + n_copies) + [pl.BlockSpec(bias.shape, lambda b, g: (0, 0, 0))],
        out_specs=(blk, blk, blk),
        scratch_shapes=[f32_buf, f32_buf, bf_buf] + [f32_buf] * 7,
        compiler_params=_params(("parallel", "parallel")),
    )(q, k, v, *ordered, o, do, lse, bias)


def _post(x, ya, ybp, ym, proj, target, w_out, g_emb, b_emb, g_a, g_b, g_m, g_post, b_post, tm=256):
    t = x.shape[0]

    def body(x_ref, ya_ref, yb_ref, ym_ref, ga_ref, gb_ref, gm_ref, tg_ref, wo_ref,
             ge_ref, be_ref, goa_ref, gob_ref, gom_ref, gp_ref, bp_ref,
             y_ref, dz_ref, doa_ref, dob_ref, dom_ref, dga_ref, dgb_ref, dgm_ref,
             loss_ref, dgp_ref, dbp_ref, dgoa_ref, dgob_ref, dgom_ref):
        i = pl.program_id(0)

        @pl.when(i == 0)
        def _():
            for r in (loss_ref, dgp_ref, dbp_ref, dgoa_ref, dgob_ref, dgom_ref):
                r[...] = jnp.zeros_like(r)

        lane = lax.broadcasted_iota(jnp.int32, (1, LANES), 1)
        low = lane < 64
        xh0, _ = _ln_hat(x_ref[...])
        h = xh0 * ge_ref[...] + be_ref[...]

        ybp_v = yb_ref[...]
        yb = jnp.concatenate(
            [jnp.where(low, pltpu.roll(ybp_v[:, 2 * j * LANES:(2 * j + 1) * LANES], 64, 1),
                       ybp_v[:, (2 * j + 1) * LANES:(2 * j + 2) * LANES]) for j in range(4)], axis=1)

        def gated(raw, gate, gain, width):
            xh, r = _rms_hat(raw, width)
            n = xh * gain
            sg = 1.0 / (1.0 + jnp.exp(-gate))
            return xh, r, n, sg, n * (gate * sg)

        gate_a, gate_b, gate_m = ga_ref[...], gb_ref[...], gm_ref[...]
        xh_a, r_a, n_a, sg_a, y_a = gated(ya_ref[...], gate_a, goa_ref[...], A_WIDTH)
        xh_b, r_b, n_b, sg_b, y_b = gated(yb, gate_b, gob_ref[...], 512)
        xh_m, r_m, n_m, sg_m, y_m = gated(ym_ref[...], gate_m, gom_ref[...], 512)
        y = jnp.concatenate([y_a, y_b, y_m], axis=1).astype(BF16)
        y_ref[...] = y
        z = DEEPNORM_ALPHA * h + _dot(y, wo_ref[...])
        zh, rstd = _ln_hat(z)
        err = zh * gp_ref[...] + bp_ref[...] - tg_ref[...]
        rows = jnp.sum(err * err, axis=1, keepdims=True)
        loss_ref[...] += jnp.broadcast_to(jnp.sum(rows, axis=0, keepdims=True) * (0.5 / D_MODEL), (1, LANES))
        dout = err * (1.0 / D_MODEL)
        dgp_ref[...] += _colsum(dout * zh)
        dbp_ref[...] += _colsum(dout)
        dz = _ln_bwd_rows(dout * gp_ref[...], zh, rstd)
        dz_ref[...] = dz
        dy = _dot_nt(dz.astype(BF16), wo_ref[...])

        def gated_bwd(dyg, xh, r, n, sg, gate, gain, width, dgain_ref):
            dn = dyg * (gate * sg)
            dgate = dyg * n * (sg * (1.0 + gate * (1.0 - sg)))
            dgain_ref[...] += _colsum(dn * xh)
            return _rms_bwd(dn * gain, xh, r, width), dgate

        dya, dgate_a = gated_bwd(dy[:, 0:1024], xh_a, r_a, n_a, sg_a, gate_a, goa_ref[...], A_WIDTH, dgoa_ref)
        dyb, dgate_b = gated_bwd(dy[:, 1024:1536], xh_b, r_b, n_b, sg_b, gate_b, gob_ref[...], 512, dgob_ref)
        dym, dgate_m = gated_bwd(dy[:, 1536:2048], xh_m, r_m, n_m, sg_m, gate_m, gom_ref[...], 512, dgom_ref)
        doa_ref[...] = dya.astype(BF16)
        dom_ref[...] = dym.astype(BF16)
        dga_ref[...] = dgate_a.astype(BF16)
        dgb_ref[...] = dgate_b.astype(BF16)
        dgm_ref[...] = dgate_m.astype(BF16)
        for j in range(4):
            blk = dyb[:, j * LANES:(j + 1) * LANES]
            dob_ref[:, 2 * j * LANES:(2 * j + 1) * LANES] = jnp.where(low, 0.0, pltpu.roll(blk, 64, 1)).astype(BF16)
            dob_ref[:, (2 * j + 1) * LANES:(2 * j + 2) * LANES] = jnp.where(low, 0.0, blk).astype(BF16)

    def col(width, idx):
        return pl.BlockSpec((tm, width), lambda i: (i, idx))

    def full(shape):
        return pl.BlockSpec(shape, lambda i: (0, 0))

    def acc(width):
        return jax.ShapeDtypeStruct((1, width), F32)

    return pl.pallas_call(
        body, name="post", grid=(t // tm,),
        out_shape=(jax.ShapeDtypeStruct((t, 2048), BF16), jax.ShapeDtypeStruct((t, 1024), F32),
                   jax.ShapeDtypeStruct((t, 1024), BF16), jax.ShapeDtypeStruct((t, 1024), BF16),
                   jax.ShapeDtypeStruct((t, 512), BF16),
                   jax.ShapeDtypeStruct((t, 1024), BF16), jax.ShapeDtypeStruct((t, 512), BF16),
                   jax.ShapeDtypeStruct((t, 512), BF16),
                   acc(LANES), acc(1024), acc(1024), acc(1024), acc(512), acc(512)),
        in_specs=[col(1024, 0), col(1024, 0), col(1024, 0), col(512, 0),
                  col(1024, 3), col(512, COL_BG // 512), col(512, COL_MG // 512), col(1024, 0),
                  full((2048, 1024)),
                  full((1, 1024)), full((1, 1024)), full((1, 1024)), full((1, 512)), full((1, 512)),
                  full((1, 1024)), full((1, 1024))],
        out_specs=(col(2048, 0), col(1024, 0), col(1024, 0), col(1024, 0), col(512, 0),
                   col(1024, 0), col(512, 0), col(512, 0),
                   full((1, LANES)), full((1, 1024)), full((1, 1024)), full((1, 1024)), full((1, 512)),
                   full((1, 512))),
        compiler_params=_params(("arbitrary",)),
    )(x, ya, ybp, ym, proj, proj, proj, target, w_out, g_emb, b_emb, g_a, g_b, g_m, g_post, b_post)


def _prep_bwd(dqa, dka, dva, dqb, dkb, dvb, dqm, dga, dgb, dgm, proj, trig, w_uq, w_ukv, g_cq, g_ckv,
              rope_a, rope_b, tm=256):
    t = proj.shape[0]

    def body(dqa_ref, dka_ref, dva_ref, dqb_ref, dkb_ref, dvb_ref, dqm_ref, dga_ref, dgb_ref, dgm_ref,
             bs_ref, trig_ref, wuq_ref, wukv_ref, gcq_ref, gckv_ref, ra_ref, rb_ref,
             dproj_ref, dqf_ref, dkv_ref, dgcq_ref, dgckv_ref):
        i = pl.program_id(0)

        @pl.when(i == 0)
        def _():
            dgcq_ref[...] = jnp.zeros_like(dgcq_ref)
            dgckv_ref[...] = jnp.zeros_like(dgckv_ref)

        ta = _rope_tables(trig_ref[:, 0:LANES], trig_ref[:, LANES:2 * LANES], ra_ref[...])
        tb = _rope_tables(trig_ref[:, 2 * LANES:3 * LANES], trig_ref[:, 3 * LANES:4 * LANES], rb_ref[...])
        for j in range(A_WIDTH // LANES):
            sl = slice(j * LANES, (j + 1) * LANES)
            dproj_ref[:, j * LANES:(j + 1) * LANES] = (
                _rope(dqa_ref[:, sl].astype(F32), ta, 8, inverse=True).astype(BF16))
            dproj_ref[:, 1024 + j * LANES:1024 + (j + 1) * LANES] = (
                _rope(dka_ref[:, sl].astype(F32), ta, 8, inverse=True).astype(BF16))
        dproj_ref[:, 2048:3072] = dva_ref[...]
        dproj_ref[:, 3072:4096] = dga_ref[...]

        lane = lax.broadcasted_iota(jnp.int32, (1, LANES), 1)
        low = lane < 64
        rope_lanes = (lane >= 64) & (lane < 96)
        dkr = jnp.zeros((tm, LANES), F32)
        for h in range(MLA_HEADS):
            sl = slice(h * LANES, (h + 1) * LANES)
            dqf_ref[:, sl] = _rope(dqb_ref[:, sl].astype(F32), tb, 16, inverse=True).astype(BF16)
            dk_h = dkb_ref[:, sl]
            dkv_ref[:, sl] = jnp.where(low, dk_h, dvb_ref[:, sl])
            dkr = dkr + jnp.where(rope_lanes, dk_h.astype(F32), 0.0)
        dkr = _rope(dkr, tb, 16, inverse=True)

        cq_hat, r_q = _rms_hat(bs_ref[:, 0:MLA_Q_RANK], MLA_Q_RANK)
        dcqn = _dot(dqf_ref[...], wuq_ref[...])
        dgcq_ref[...] += _colsum(dcqn * cq_hat)
        dproj_ref[:, COL_CQ:COL_CQ + 256] = _rms_bwd(dcqn * gcq_ref[...], cq_hat, r_q, MLA_Q_RANK).astype(BF16)
        ckv_hat, r_kv = _rms_hat(bs_ref[:, MLA_Q_RANK:MLA_Q_RANK + MLA_KV_RANK], MLA_KV_RANK)
        dckvn = _dot_nt(dkv_ref[...], wukv_ref[...])
        dgckv_ref[...] += _colsum(dckvn * ckv_hat)
        dproj_ref[:, COL_CQ + 256:COL_CQ + 384] = (
            _rms_bwd(dckvn * gckv_ref[...], ckv_hat, r_kv, MLA_KV_RANK).astype(BF16))
        dproj_ref[:, COL_CQ + 384:COL_CQ + 512] = dkr.astype(BF16)
        dproj_ref[:, COL_BG:COL_BG + 512] = dgb_ref[...]
        dproj_ref[:, COL_MQ:COL_MQ + 512] = dqm_ref[...]
        dproj_ref[:, COL_MG:COL_MG + 512] = dgm_ref[...]

    def col(width, idx):
        return pl.BlockSpec((tm, width), lambda i: (i, idx))

    def full(shape):
        return pl.BlockSpec(shape, lambda i: (0, 0))

    return pl.pallas_call(
        body, name="prep_bwd", grid=(t // tm,),
        out_shape=(jax.ShapeDtypeStruct((t, PROJ_W), BF16), jax.ShapeDtypeStruct((t, 1024), BF16),
                   jax.ShapeDtypeStruct((t, 1024), BF16),
                   jax.ShapeDtypeStruct((1, MLA_Q_RANK), F32), jax.ShapeDtypeStruct((1, MLA_KV_RANK), F32)),
        in_specs=[col(1024, 0)] * 6 + [col(512, 0), col(1024, 0), col(512, 0), col(512, 0),
                  col(512, COL_CQ // 512), pl.BlockSpec((tm, 4 * LANES), lambda i: (i, 0)),
                  full((1024, MLA_Q_RANK)), full((MLA_KV_RANK, 1024)),
                  full((1, MLA_Q_RANK)), full((1, MLA_KV_RANK)), full((8, LANES)), full((8, LANES))],
        out_specs=(col(PROJ_W, 0), col(1024, 0), col(1024, 0), full((1, MLA_Q_RANK)), full((1, MLA_KV_RANK))),
        compiler_params=_params(("arbitrary",)),
    )(dqa, dka, dva, dqb, dkb, dvb, dqm, dga, dgb, dgm, proj, trig, w_uq, w_ukv, g_cq, g_ckv, rope_a, rope_b)


def _adamw_math(gv, w, m, v):
    m_new = ADAM_B1 * m + (1.0 - ADAM_B1) * gv
    v_new = ADAM_B2 * v + (1.0 - ADAM_B2) * (gv * gv)
    m_hat = m_new / (1.0 - ADAM_B1 ** ADAM_STEP)
    v_hat = v_new / (1.0 - ADAM_B2 ** ADAM_STEP)
    return -ADAM_LR * (m_hat / (jnp.sqrt(v_hat) + ADAM_EPS) + ADAM_WD * w), m_new, v_new


def _adamw(g, w, m, v, tr, name):
    r, cols = w.shape

    def body(g_ref, w_ref, m_ref, v_ref, go_ref, d_ref, nm_ref, nv_ref):
        gv = g_ref[...]
        go_ref[...] = gv
        d_ref[...], nm_ref[...], nv_ref[...] = _adamw_math(gv, w_ref[...], m_ref[...], v_ref[...])

    tile = pl.BlockSpec((tr, cols), lambda i: (i, 0))
    shape = jax.ShapeDtypeStruct((r, cols), F32)
    return pl.pallas_call(
        body, name=name, grid=(r // tr,),
        out_shape=(shape,) * 4, in_specs=[tile] * 4, out_specs=(tile,) * 4,
        compiler_params=_params(("parallel",)),
    )(g, w, m, v)


def _adamw_pieces(g, w, m, v, pieces, name):
    shapes = [jax.ShapeDtypeStruct((r1 - r0, c1 - c0), F32) for r0, r1, c0, c1 in pieces]

    def body(g_ref, w_ref, m_ref, v_ref, *outs):
        gv = g_ref[...]
        results = (gv,) + _adamw_math(gv, w_ref[...], m_ref[...], v_ref[...])
        for kind, full in enumerate(results):
            for p, (r0, r1, c0, c1) in enumerate(pieces):
                outs[kind * len(pieces) + p][...] = full[r0:r1, c0:c1]

    flat = pl.pallas_call(
        body, name=name, out_shape=tuple(shapes) * 4,
        in_specs=[IN_VMEM] * 4, out_specs=tuple([IN_VMEM] * (4 * len(pieces))),
        compiler_params=_params(None),
    )(g, w, m, v)
    return [[flat[kind * len(pieces) + p] for kind in range(4)] for p in range(len(pieces))]


def _core_sum(g, recv, core, rows, tr, name, ride=None):
    cols = g.shape[2]
    nblk = rows // tr
    n_in = len(ride.args) if ride else 0
    n_out = len(ride.out_shapes) if ride else 0

    def body(c_ref, g_ref, r_ref, *rest):
        sf_ref, sb_ref = rest[n_in], rest[n_in + 1]
        if ride:
            j, i = pl.program_id(0), pl.program_id(1)
            ride.run((j == 0) & (i == 0), (j == 3) & (i == nblk - 1), rest[:n_in],
                     rest[n_in + 2:n_in + 2 + n_out], rest[n_in + 2 + n_out:])
        tot = g_ref[...] + r_ref[...]
        sf_ref[...] = tot
        sb_ref[...] = tot.astype(BF16)

    half = pl.BlockSpec((None, tr, cols), lambda j, i, c_ref: (j, i, 0))
    shapes = (jax.ShapeDtypeStruct((4, rows, cols), F32), jax.ShapeDtypeStruct((4, rows, cols), BF16))
    return pl.pallas_call(
        body, name=name,
        grid_spec=pltpu.PrefetchScalarGridSpec(
            num_scalar_prefetch=1, grid=(4, nblk),
            in_specs=[pl.BlockSpec((None, tr, cols), lambda j, i, c_ref: (j, c_ref[0] * nblk + i, 0)), half]
            + (ride.in_specs if ride else []),
            out_specs=(half, half) + (ANY,) * n_out,
            scratch_shapes=ride.scratch() if ride else []),
        out_shape=shapes + tuple(ride.out_shapes if ride else ()),
        compiler_params=_params(("arbitrary", "arbitrary") if ride else ("parallel", "parallel")),
    )(core, g, recv, *(ride.args if ride else ()))


def _half_to_sibling(g4):
    def plan(in_refs, out_refs, send_sems, recv_sems):
        x, y, c = _position()
        cp = pltpu.make_async_remote_copy(
            src_ref=in_refs[0].at[:, 1 - c], dst_ref=out_refs[0], send_sem=send_sems.at[0],
            recv_sem=recv_sems.at[0], device_id=(x, y, 1 - c), device_id_type=MESH)

        def finish():
            cp.wait_recv()
            cp.wait_send()

        return cp.start, finish

    return _Ride([g4], [jax.ShapeDtypeStruct((4, g4.shape[2], 1024), F32)], (1, 1), plan)


def _gather_plan(src_ref, dst_ref, send_sems, recv_sems, local_sems):
    x, y, c = _position()
    me = 2 * x + y
    local = pltpu.make_async_copy(src_ref, dst_ref.at[me], local_sems.at[0])

    def over_ici(k, src, chip):
        return pltpu.make_async_remote_copy(
            src_ref=src, dst_ref=dst_ref.at[chip, c], send_sem=send_sems.at[k - 1], recv_sem=recv_sems.at[k - 1],
            device_id=(x ^ (k >> 1), y ^ (k & 1), c), device_id_type=MESH)

    def to_sibling(k, half):
        piece = dst_ref.at[me ^ k, half]
        return pltpu.make_async_remote_copy(
            src_ref=piece, dst_ref=piece, send_sem=send_sems.at[2 + k], recv_sem=recv_sems.at[2 + k],
            device_id=(x, y, 1 - c), device_id_type=MESH)

    sends = [over_ici(k, src_ref.at[c], me) for k in (1, 2, 3)]

    def start():
        local.start()
        for cp in sends:
            cp.start()

    def relay():
        for k in (1, 2, 3):
            over_ici(k, dst_ref.at[me ^ k, c], me ^ k).wait_recv()
            to_sibling(k, c).start()

    def finish():
        for k in (1, 2, 3):
            to_sibling(k, 1 - c).wait_recv()
        for cp in sends + [to_sibling(k, c) for k in (1, 2, 3)]:
            cp.wait_send()
        local.wait()

    return start, relay, finish


def _gather_ride(shard):
    def plan(in_refs, out_refs, send_sems, recv_sems, local_sems):
        return _gather_plan(in_refs[0], out_refs[0], send_sems, recv_sems, local_sems)

    return _Ride([shard], [jax.ShapeDtypeStruct((4,) + shard.shape, shard.dtype)], (6, 6, 1), plan,
                 in_specs=[IN_VMEM])


def _chip_sum(sf, recv, chip, rows, tr, name):
    cols = sf.shape[2]

    def body(me_ref, sf_ref, r_ref, out_ref):
        acc = sf_ref[...]
        for k in range(3):
            acc = acc + r_ref[k].astype(F32)
        out_ref[...] = acc

    return pl.pallas_call(
        body, name=name,
        grid_spec=pltpu.PrefetchScalarGridSpec(
            num_scalar_prefetch=1, grid=(rows // tr,),
            in_specs=[pl.BlockSpec((None, tr, cols), lambda i, me_ref: (me_ref[0], i, 0)),
                      pl.BlockSpec((3, tr, cols), lambda i, me_ref: (0, i, 0))],
            out_specs=pl.BlockSpec((tr, cols), lambda i, me_ref: (i, 0))),
        out_shape=jax.ShapeDtypeStruct((rows, cols), F32),
        compiler_params=_params(("parallel",)),
    )(chip, sf, recv)


def _position():
    return lax.axis_index("x"), lax.axis_index("y"), lax.axis_index("c")


def _dh_scatter(dproj, w_in_arr_t, x, dz, g, sb_in, sb_rest, tm=1024, tk=1024):
    t, d = x.shape
    nk = dproj.shape[1] // tk
    ni = t // tm

    def body(dp_ref, w_ref, x_ref, dz_ref, g_ref, sbin_ref, sbrest_ref,
             dx_ref, dg_ref, db_ref, rin_ref, rrest_ref, acc_ref, send_sems, recv_sems):
        i = pl.program_id(0)
        kk = pl.program_id(1)
        px, py, pc = _position()
        me = 2 * px + py
        srcs = (sbin_ref, sbrest_ref)
        dsts = (rin_ref, rrest_ref)

        def copy(a, k):
            return pltpu.make_async_remote_copy(
                src_ref=srcs[a].at[me ^ k], dst_ref=dsts[a].at[k - 1],
                send_sem=send_sems.at[3 * a + k - 1], recv_sem=recv_sems.at[3 * a + k - 1],
                device_id=(px ^ (k >> 1), py ^ (k & 1), pc), device_id_type=MESH)

        pairs = [(a, k) for a in range(2) for k in (1, 2, 3)]

        @pl.when((i == 0) & (kk == 0))
        def _():
            dg_ref[...] = jnp.zeros_like(dg_ref)
            db_ref[...] = jnp.zeros_like(db_ref)
            for a, k in pairs:
                copy(a, k).start()

        part = _dot(dp_ref[...], w_ref[...])

        @pl.when(kk == 0)
        def _():
            acc_ref[...] = part

        @pl.when(kk > 0)
        def _():
            acc_ref[...] += part

        @pl.when(kk == nk - 1)
        def _():
            xh, rstd = _ln_hat(x_ref[...])
            dht = acc_ref[...] + DEEPNORM_ALPHA * dz_ref[...]
            dg_ref[...] += _colsum(dht * xh)
            db_ref[...] += _colsum(dht)
            dx_ref[...] = _ln_bwd_rows(dht * g_ref[...], xh, rstd)

        @pl.when((i == ni - 1) & (kk == nk - 1))
        def _():
            for a, k in pairs:
                copy(a, k).wait_recv()
            for a, k in pairs:
                copy(a, k).wait_send()

    tile = pl.BlockSpec((tm, d), lambda i, kk: (i, 0))
    row = pl.BlockSpec((1, d), lambda i, kk: (0, 0))
    return pl.pallas_call(
        body, name="dh_scatter", grid=(ni, nk),
        out_shape=(jax.ShapeDtypeStruct((t, d), F32), jax.ShapeDtypeStruct((1, d), F32),
                   jax.ShapeDtypeStruct((1, d), F32),
                   jax.ShapeDtypeStruct((3, HALF_IN, 1024), BF16),
                   jax.ShapeDtypeStruct((3, HALF_REST, 1024), BF16)),
        in_specs=[pl.BlockSpec((tm, tk), lambda i, kk: (i, kk)), pl.BlockSpec((tk, d), lambda i, kk: (kk, 0)),
                  tile, tile, row, ANY, ANY],
        out_specs=(tile, row, row, ANY, ANY),
        scratch_shapes=[pltpu.VMEM((tm, d), F32), pltpu.SemaphoreType.DMA((6,)), pltpu.SemaphoreType.DMA((6,))],
        compiler_params=_params(("arbitrary", "arbitrary")),
    )(dproj, w_in_arr_t, x, dz, g, sb_in, sb_rest)


def _join_halves(gh_in, gh_rest):
    def body(hin_ref, hrest_ref, oin_ref, orest_ref, send_sems, recv_sems, local_sems):
        x, y, c = _position()
        srcs = (hin_ref, hrest_ref)
        dsts = (oin_ref, orest_ref)

        def rows(a, half):
            return dsts[a].at[half]

        local = [pltpu.make_async_copy(srcs[a], rows(a, c), local_sems.at[a]) for a in range(2)]
        remote = [pltpu.make_async_remote_copy(
            src_ref=srcs[a], dst_ref=rows(a, c), send_sem=send_sems.at[a], recv_sem=recv_sems.at[a],
            device_id=(x, y, 1 - c), device_id_type=MESH) for a in range(2)]
        for cp in local + remote:
            cp.start()
        for a in range(2):
            pltpu.make_async_remote_copy(
                src_ref=srcs[a], dst_ref=rows(a, 1 - c), send_sem=send_sems.at[a], recv_sem=recv_sems.at[a],
                device_id=(x, y, 1 - c), device_id_type=MESH).wait_recv()
        for cp in remote:
            cp.wait_send()
        for cp in local:
            cp.wait()

    return pl.pallas_call(
        body, name="join_halves",
        out_shape=(jax.ShapeDtypeStruct((2, HALF_IN, 1024), F32),
                   jax.ShapeDtypeStruct((2, HALF_REST, 1024), F32)),
        in_specs=[IN_VMEM, IN_VMEM], out_specs=(ANY, ANY),
        scratch_shapes=[pltpu.SemaphoreType.DMA((2,)), pltpu.SemaphoreType.DMA((2,)), pltpu.SemaphoreType.DMA((2,))],
    )(gh_in, gh_rest)


def _allreduce_small(vec):
    def body(vec_ref, out_ref, all_ref, send_sems, recv_sems):
        x, y, c = _position()
        me = 4 * x + 2 * y + c
        all_ref[me] = vec_ref[...]

        def copy(k, slot):
            return pltpu.make_async_remote_copy(
                src_ref=vec_ref, dst_ref=all_ref.at[slot], send_sem=send_sems.at[k - 1], recv_sem=recv_sems.at[k - 1],
                device_id=(x ^ (k >> 2), y ^ ((k >> 1) & 1), c ^ (k & 1)), device_id_type=MESH)

        copies = [copy(k, me) for k in range(1, 8)]
        for cp in copies:
            cp.start()
        for k in range(1, 8):
            copy(k, me ^ k).wait_recv()
        for cp in copies:
            cp.wait_send()
        total = all_ref[0]
        for d in range(1, 8):
            total = total + all_ref[d]
        out_ref[...] = total

    return pl.pallas_call(
        body, name="allreduce_small",
        out_shape=jax.ShapeDtypeStruct(vec.shape, vec.dtype),
        in_specs=[pl.BlockSpec(memory_space=pltpu.VMEM)], out_specs=pl.BlockSpec(memory_space=pltpu.VMEM),
        scratch_shapes=[pltpu.VMEM((8,) + vec.shape, vec.dtype), pltpu.SemaphoreType.DMA((7,)),
                        pltpu.SemaphoreType.DMA((7,))],
    )(vec)


def _pack_rest(w_uq, w_ukv, w_mem, w_out):
    rows = jnp.concatenate([w_uq[0].T.reshape(-1, 1024), w_ukv.reshape(-1, 1024), w_mem.reshape(-1, 1024),
                            w_out.reshape(-1, 1024)], axis=0)
    return jnp.pad(rows, ((0, ROWS_REST - ROWS_USED), (0, 0)))


def _arranged_w_in(g_in):
    z = functools.partial(jnp.zeros, dtype=g_in.dtype)
    cut = 4480 - 2 * SHARD_ROWS
    return jnp.concatenate(
        [g_in[0, :SHARD_ROWS], g_in[1, :SHARD_ROWS], g_in[2, :cut], z((64, 1024)), g_in[2, cut:cut + 32],
         z((32, 1024)), g_in[2, cut + 32:SHARD_ROWS], g_in[3, :SHARD_ROWS]], axis=0)


def _rest_weights(g_rest):
    w_uq_t = g_rest[:, 0:ROWS_UQ].reshape(768, 256)
    w_uq_pad_t = jnp.pad(w_uq_t.reshape(MLA_HEADS, MLA_QK_DIM, 256), ((0, 0), (0, 32), (0, 0))).reshape(1024, 256)
    w_ukv = jnp.concatenate([g_rest[j, ROWS_UQ:ROWS_UQ + ROWS_UKV].reshape(128, 256) for j in range(4)], axis=1)
    lo = ROWS_UQ + ROWS_UKV
    w_mem = g_rest[:, lo:lo + ROWS_MEM].reshape(4 * ROWS_MEM, 1024)
    w_out = g_rest[:, lo + ROWS_MEM:lo + ROWS_MEM + ROWS_OUT].reshape(4 * ROWS_OUT, 1024)
    return w_uq_pad_t, w_ukv, w_mem, w_out


def _split_in(dw_in_arr_t):
    a = dw_in_arr_t
    gap = jnp.zeros((ROWS_IN - SHARD_ROWS, 1024), a.dtype)
    nat = 4608 - 96
    pieces = [a[:SHARD_ROWS], gap, a[SHARD_ROWS:2 * SHARD_ROWS], gap,
              a[2 * SHARD_ROWS:4480], a[4544:4576], a[4608:4608 + 3 * SHARD_ROWS - nat], gap,
              a[4608 + 3 * SHARD_ROWS - nat:], gap]
    return jnp.concatenate(pieces, axis=0).reshape(4, ROWS_IN, 1024)


def _split_rest(dw_uq_pad_t, dw_ukv, dw_mem, dw_out):
    dw_uq_t = dw_uq_pad_t.reshape(MLA_HEADS, LANES, 256)[:, :MLA_QK_DIM].reshape(4, ROWS_UQ, 1024)
    parts = [dw_uq_t, dw_ukv.reshape(128, 4, 256).transpose(1, 0, 2).reshape(4, ROWS_UKV, 1024),
             dw_mem.reshape(4, ROWS_MEM, 1024), dw_out.reshape(4, ROWS_OUT, 1024)]
    return jnp.pad(jnp.concatenate(parts, axis=1), ((0, 0), (0, ROWS_REST - ROWS_USED), (0, 0)))


def _rope_consts(rot, first, period):
    half = rot // 2
    inv_freq = np.float32(ROPE_THETA) ** (-(np.arange(0, rot, 2, dtype=np.float32) / np.float32(rot)))
    lane = np.arange(LANES) % period - first
    in_rot = (lane >= 0) & (lane < rot)
    out = np.zeros((8, LANES), np.float32)
    out[0] = np.where(in_rot, inv_freq[np.clip(lane, 0, rot - 1) % half], 0.0)
    out[1] = in_rot & (lane < half)
    out[2] = in_rot & (lane >= half)
    return jnp.asarray(out)


def _band_bias(s):
    nblk = s // BAND_Q
    starts = np.array([_band_start(i, s) for i in range(nblk)])
    uq = (np.arange(nblk)[:, None] * BAND_Q + np.arange(BAND_Q)[None, :])[:, :, None]
    uk = (starts[:, None] + np.arange(BAND_WIN)[None, :])[:, None, :]
    tiles, index, seen = [], [], {}
    for _, d in DILATED:
        length = s // d
        ok = (uq // length == uk // length) & (np.abs(uq - uk) <= 64)
        row = []
        for i in range(nblk):
            key = ok[i].tobytes()
            if key not in seen:
                seen[key] = len(tiles)
                tiles.append(np.where(ok[i], 0.0, NEG_INF).astype(np.float32))
            row.append(seen[key])
        index.append(row)
    return jnp.asarray(np.stack(tiles, axis=0)), index


def _forward_backward(h, proj, trig, rope_consts, x, mem, target, weights, gains):
    w_uq_pad_t, w_ukv, w_mem, w_out = weights
    g_emb, b_emb, g_cq, g_ckv, g_out_a, g_out_b, g_out_m, g_post, b_post = gains
    nb, s, d = x.shape
    t = nb * s
    x2 = x.reshape(t, d)
    mem2 = mem.reshape(nb * N_MEM, d)
    tgt2 = target.reshape(t, d)
    rope_a, rope_b = rope_consts
    bias, bias_index = _band_bias(s)
    scales = (0.125, MLA_QK_DIM ** -0.5, 128 ** -0.5)

    qa, ka, va, qb, kb, vb, qm, cqn, ckvn = _prep(proj, trig, w_uq_pad_t, w_ukv, g_cq, g_ckv, rope_a, rope_b, scales)
    mkv = _mm(mem2, w_mem, BF16, nb * N_MEM, 1024, 1024, "mem_kv")

    cfg_b = dict(nb=nb, s=s, sk=s, heads=8, voff=0, bq=256)
    cfg_m = dict(nb=nb, s=s, sk=N_MEM, heads=4, hpb=2, voff=4, bq=1024)
    ya, lse_a, qkv_ordered = _dilated_fwd(qa, ka, va, bias, bias_index, nb=nb, s=s, name="attn_a_fwd")
    yb, lse_b = _attn_fwd(qb, kb, vb, name="attn_b_fwd", hpb=4, **cfg_b)
    ym, lse_m = _attn_fwd(qm, mkv, mkv, name="attn_m_fwd", **cfg_m)

    (y, dz, doa, dob, dom, dga, dgb, dgm, loss, dg_post, db_post, dg_a, dg_b, dg_m) = _post(
        x2, ya, yb, ym, proj, tgt2, w_out, g_emb, b_emb, g_out_a, g_out_b, g_out_m, g_post, b_post)

    dqa, dka, dva = _dilated_bwd(qa, ka, va, qkv_ordered, ya, doa, lse_a, bias, bias_index, nb=nb, s=s, scale=scales[0],
                                 name="attn_a_bwd")
    dqb, dkb, dvb = _attn_bwd(qb, kb, vb, yb, dob, lse_b, name="attn_b_bwd", scale=scales[1], hpb=2, **cfg_b)
    dqm, dmk, dmv = _attn_bwd(qm, mkv, mkv, ym, dom, lse_m, name="attn_m_bwd", scale=scales[2], **cfg_m)
    dmkv = jnp.concatenate([dmk, dmv], axis=1)

    dproj, dqf, dkv, dg_cq, dg_ckv = _prep_bwd(
        dqa, dka, dva, dqb, dkb, dvb, dqm, dga, dgb, dgm, proj, trig, w_uq_pad_t, w_ukv, g_cq, g_ckv, rope_a, rope_b)

    small_rows = (dg_cq, dg_ckv, loss, dg_a, dg_b, dg_m, dg_post, db_post)
    return (dproj, h, y, dz, dqf, cqn, ckvn, dkv, mem2, dmkv), x2, small_rows


def _weight_grads(operands, core):
    dproj, h, y, dz, dqf, cqn, ckvn, dkv, mem2, dmkv = operands
    dw_in_arr_t = _mm(dproj, h, F32, 1024, 1024, 4096, "dw_in", mode="tn")
    g_in = _split_in(dw_in_arr_t)
    dw_out, r_in = _mm(y, dz, F32, 1024, 1024, 2048, "dw_out", mode="tn",
                       ride=_half_to_sibling(g_in.reshape(4, 2, HALF_IN, 1024)))
    dw_uq_pad_t = _mm(dqf, cqn, F32, 1024, 256, 4096, "dw_uq", mode="tn")
    dw_ukv = _mm(ckvn, dkv, F32, 128, 1024, 4096, "dw_ukv", mode="tn")
    dw_mem = _mm(mem2, dmkv, F32, 1024, 1024, mem2.shape[0], "dw_mem", mode="tn")
    g_rest = _split_rest(dw_uq_pad_t, dw_ukv, dw_mem, dw_out)
    sf_in, sb_in, r_rest = _core_sum(g_in, r_in, core, HALF_IN, HALF_IN // 2, "core_sum_in",
                                     ride=_half_to_sibling(g_rest.reshape(4, 2, HALF_REST, 1024)))
    sf_rest, sb_rest = _core_sum(g_rest, r_rest, core, HALF_REST, HALF_REST, "core_sum_rest")
    return sf_in, sb_in, sf_rest, sb_rest


def _small_block(dg_emb, db_emb, small_rows):
    dg_cq, dg_ckv, loss, dg_a, dg_b, dg_m, dg_post, db_post = small_rows
    row2 = jnp.concatenate([dg_cq, dg_ckv, loss, jnp.zeros((1, 512), F32)], axis=1)
    return jnp.concatenate([dg_emb, db_emb, row2, dg_a, jnp.concatenate([dg_b, dg_m], axis=1), dg_post, db_post,
                            jnp.zeros((1, 1024), F32)], axis=0)


def _pack_small(g_emb, b_emb, g_cq, g_ckv, g_out_a, g_out_b, g_out_m, g_post, b_post):
    row2 = jnp.concatenate([g_cq.reshape(1, -1), g_ckv.reshape(1, -1), jnp.zeros((1, 640), F32)], axis=1)
    return jnp.concatenate([g_emb.reshape(1, -1), b_emb.reshape(1, -1), row2, g_out_a.reshape(1, -1),
                            jnp.concatenate([g_out_b.reshape(1, -1), g_out_m.reshape(1, -1)], axis=1),
                            g_post.reshape(1, -1), b_post.reshape(1, -1), jnp.zeros((1, 1024), F32)], axis=0)


def kernel(x, mem, positions, g_emb, b_emb, w_in, g_cq, g_ckv, w_uq, w_ukv, w_mem_kv, g_out_a, g_out_b, g_out_m, w_out, g_post, b_post, loss_target, m_g_emb, m_b_emb, m_w_in, m_g_cq, m_g_ckv, m_w_uq, m_w_ukv, m_w_mem_kv, m_g_out_a, m_g_out_b, m_g_out_m, m_w_out, m_g_post, m_b_post, v_g_emb, v_b_emb, v_w_in, v_g_cq, v_g_ckv, v_w_uq, v_w_ukv, v_w_mem_kv, v_g_out_a, v_g_out_b, v_g_out_m, v_w_out, v_g_post, v_b_post):
    w_rest = _pack_rest(w_uq, w_ukv, w_mem_kv, w_out)
    w_in_t = w_in[0].T
    w_in_b = jnp.pad(w_in_t.astype(BF16), ((0, ROWS_IN - SHARD_ROWS), (0, 0)))
    gains = (g_emb.reshape(1, -1), b_emb.reshape(1, -1), g_cq, g_ckv, g_out_a, g_out_b, g_out_m, g_post, b_post)
    rope_consts = (_rope_consts(16, 0, 64), _rope_consts(32, 64, 128))
    h, trig, gathered_in = _ln_fwd(x.reshape(-1, D_MODEL), gains[0], gains[1],
                                   positions.reshape(-1, 1).astype(F32), *rope_consts,
                                   ride=_gather_ride(w_in_b.reshape(2, HALF_IN, 1024)))
    w_in_arr_t = _arranged_w_in(gathered_in.reshape(4, ROWS_IN, 1024))
    proj, gathered_rest = _mm(h, w_in_arr_t, F32, 1024, 2048, 1024, "in_proj", mode="nt",
                              ride=_gather_ride(w_rest.astype(BF16).reshape(2, HALF_REST, 1024)))
    weights = _rest_weights(gathered_rest.reshape(4, ROWS_REST, 1024))
    operands, x2, small_rows = _forward_backward(h, proj, trig, rope_consts, x, mem, loss_target, weights, gains)

    core = lax.axis_index("c").astype(jnp.int32).reshape(1)
    chip = (2 * lax.axis_index("x") + lax.axis_index("y")).astype(jnp.int32).reshape(1)
    sf_in, sb_in, sf_rest, sb_rest = _weight_grads(operands, core)
    grad_x, dg_emb, db_emb, rb_in, rb_rest = _dh_scatter(operands[0], w_in_arr_t, x2, operands[3], gains[0],
                                                         sb_in, sb_rest)
    gh_in = _chip_sum(sf_in, rb_in, chip, HALF_IN, HALF_IN // 2, "chip_sum_in")
    gh_rest = _chip_sum(sf_rest, rb_rest, chip, HALF_REST, HALF_REST, "chip_sum_rest")
    grad_in, grad_rest = _join_halves(gh_in, gh_rest)
    grad_in = grad_in.reshape(ROWS_IN, 1024)
    grad_rest = grad_rest.reshape(ROWS_REST, 1024)

    big_in = _adamw(grad_in, w_in_t, m_w_in[0].T, v_w_in[0].T, SHARD_ROWS // 3, "adamw_in")
    uq, ukv, wmem, wout = _adamw_pieces(
        grad_rest, w_rest, _pack_rest(m_w_uq, m_w_ukv, m_w_mem_kv, m_w_out),
        _pack_rest(v_w_uq, v_w_ukv, v_w_mem_kv, v_w_out), REST_PIECES, "adamw_rest")
    small_sum = _allreduce_small(_small_block(dg_emb, db_emb, small_rows))
    sm = _adamw_pieces(
        small_sum,
        _pack_small(g_emb, b_emb, g_cq, g_ckv, g_out_a, g_out_b, g_out_m, g_post, b_post),
        _pack_small(m_g_emb, m_b_emb, m_g_cq, m_g_ckv, m_g_out_a, m_g_out_b, m_g_out_m, m_g_post, m_b_post),
        _pack_small(v_g_emb, v_b_emb, v_g_cq, v_g_ckv, v_g_out_a, v_g_out_b, v_g_out_m, v_g_post, v_b_post),
        SMALL_PIECES, "adamw_small")
    loss = small_sum[2, 384]

    def ordered(kind):
        s_gemb, s_bemb, s_gcq, s_gckv, s_ga, s_gb, s_gm, s_gpost, s_bpost = [piece[kind] for piece in sm]
        return [s_gemb.reshape(-1), s_bemb.reshape(-1), big_in[kind].T[None], s_gcq, s_gckv,
                uq[kind].reshape(192, 256).T[None], ukv[kind].reshape(1, 128, 256), wmem[kind][None], s_ga, s_gb,
                s_gm, wout[kind][None], s_gpost, s_bpost]

    return (loss, grad_x.reshape(x.shape), *ordered(0), *ordered(1), *ordered(2), *ordered(3))
```

```python
import functools
import math

import jax
import jax.numpy as jnp
import numpy as np
from jax import lax
from jax.experimental import pallas as pl
from jax.experimental.pallas import tpu as pltpu

F32 = jnp.float32
BF16 = jnp.bfloat16
MESH = pl.DeviceIdType.MESH
ANY = pl.BlockSpec(memory_space=pl.ANY)
IN_VMEM = pl.BlockSpec(memory_space=pltpu.VMEM)

D_MODEL = 1024
A_WIDTH = 1024
MLA_HEADS = 8
MLA_Q_RANK = 256
MLA_KV_RANK = 128
MLA_QK_DIM = 96
MEM_WIDTH = 512
N_MEM = 256
ROPE_THETA = 500000.0
NORM_EPS = 1e-5
NEG_INF = -1e30
DEEPNORM_ALPHA = 2.0 ** 0.25
DILATED = ((64, 1), (256, 4), (1024, 16))

ADAM_LR = 0.001
ADAM_B1 = 0.9
ADAM_B2 = 0.999
ADAM_EPS = 1e-08
ADAM_WD = 0.01
ADAM_STEP = 10

LANES = 128
VMEM_LIMIT = 56 * 1024 * 1024
LOG2E = math.log2(math.e)
LN2 = math.log(2.0)

PROJ_W = 6144
COL_CQ = 4096
COL_BG = 4608
COL_MQ = 5120
COL_MG = 5632

SHARD_ROWS = 1512
ROWS_IN = 1536
ROWS_UQ, ROWS_UKV, ROWS_MEM, ROWS_OUT = 48, 32, 256, 512
ROWS_USED = ROWS_UQ + ROWS_UKV + ROWS_MEM + ROWS_OUT
ROWS_REST = 864
HALF_IN = ROWS_IN // 2
HALF_REST = ROWS_REST // 2
REST_PIECES = ((0, 48, 0, 1024), (48, 80, 0, 1024), (80, 336, 0, 1024), (336, 848, 0, 1024))
SMALL_PIECES = ((0, 1, 0, 1024), (1, 2, 0, 1024), (2, 3, 0, 256), (2, 3, 256, 384), (3, 4, 0, 1024), (4, 5, 0, 512),
                (4, 5, 512, 1024), (5, 6, 0, 1024), (6, 7, 0, 1024))


def _params(sem=None, vmem=VMEM_LIMIT):
    return pltpu.CompilerParams(dimension_semantics=sem, vmem_limit_bytes=vmem)


def _dot(a, b):
    return jnp.dot(a, b, preferred_element_type=F32)


def _dot_nt(a, b):
    return lax.dot_general(a, b, (((1,), (1,)), ((), ())), preferred_element_type=F32)


def _dot_tn(a, b):
    return lax.dot_general(a, b, (((0,), (0,)), ((), ())), preferred_element_type=F32)


def _ln_hat(x):
    mu = jnp.mean(x, axis=-1, keepdims=True)
    xc = x - mu
    var = jnp.mean(xc * xc, axis=-1, keepdims=True)
    rstd = lax.rsqrt(var + NORM_EPS)
    return xc * rstd, rstd


def _ln_bwd_rows(dxh, xh, rstd):
    return rstd * (dxh - jnp.mean(dxh, axis=-1, keepdims=True) - xh * jnp.mean(dxh * xh, axis=-1, keepdims=True))


def _rms_hat(x, width):
    ms = jnp.sum(x * x, axis=-1, keepdims=True) * (1.0 / width)
    r = lax.rsqrt(ms + NORM_EPS)
    return x * r, r


def _rms_bwd(u, xh, r, width):
    return r * (u - xh * (jnp.sum(u * xh, axis=-1, keepdims=True) * (1.0 / width)))


def _colsum(v):
    return jnp.sum(v, axis=0, keepdims=True)


def _rope_tables(cos, sin, consts):
    return cos, sin * consts[2:3, :], -sin * consts[1:2, :]


def _rope(x, tables, half, inverse=False):
    c, s_up, s_dn = tables
    if inverse:
        s_up, s_dn = -s_up, -s_dn
    return x * c + pltpu.roll(x, half, 1) * s_up + pltpu.roll(x, LANES - half, 1) * s_dn


def _ln_fwd(x, g, b, pos, rope_a, rope_b, tm=512, ride=None):
    t, d = x.shape
    n_in = len(ride.args) if ride else 0
    n_out = len(ride.out_shapes) if ride else 0
    steps = t // tm

    def body(x_ref, g_ref, b_ref, pos_ref, ra_ref, rb_ref, *rest):
        h_ref, trig_ref = rest[n_in], rest[n_in + 1]
        if ride:
            i = pl.program_id(0)
            ride.run(i == 0, i == steps - 1, rest[:n_in], rest[n_in + 2:n_in + 2 + n_out], rest[n_in + 2 + n_out:])
        xh, _ = _ln_hat(x_ref[...])
        h_ref[...] = (xh * g_ref[...] + b_ref[...]).astype(BF16)
        for j, consts in enumerate((ra_ref, rb_ref)):
            ang = pos_ref[...] * consts[0:1, :]
            trig_ref[:, 2 * j * LANES:(2 * j + 1) * LANES] = jnp.cos(ang)
            trig_ref[:, (2 * j + 1) * LANES:(2 * j + 2) * LANES] = jnp.sin(ang)

    row = pl.BlockSpec((1, d), lambda i: (0, 0))
    tile = pl.BlockSpec((tm, d), lambda i: (i, 0))
    consts = pl.BlockSpec((8, LANES), lambda i: (0, 0))
    trig_tile = pl.BlockSpec((tm, 4 * LANES), lambda i: (i, 0))
    in_specs = [tile, row, row, pl.BlockSpec((tm, 1), lambda i: (i, 0)), consts, consts]
    shapes = (jax.ShapeDtypeStruct((t, d), BF16), jax.ShapeDtypeStruct((t, 4 * LANES), F32))
    if not ride:
        return pl.pallas_call(
            body, name="ln_fwd", grid=(steps,), out_shape=shapes, in_specs=in_specs, out_specs=(tile, trig_tile),
            compiler_params=_params(("parallel",)),
        )(x, g, b, pos, rope_a, rope_b)
    return pl.pallas_call(
        body, name="ln_fwd", grid=(steps,),
        out_shape=(*shapes, *ride.out_shapes),
        in_specs=in_specs + ride.in_specs, out_specs=(tile, trig_tile) + (ANY,) * n_out,
        scratch_shapes=ride.scratch(),
        compiler_params=_params(("arbitrary",)),
    )(x, g, b, pos, rope_a, rope_b, *ride.args)


class _Ride:
    def __init__(self, args, out_shapes, sem_counts, plan, in_specs=None):
        self.args, self.out_shapes, self.plan = list(args), list(out_shapes), plan
        self.sem_counts = sem_counts
        self.in_specs = in_specs or [ANY] * len(self.args)

    def scratch(self):
        return [pltpu.SemaphoreType.DMA((n,)) for n in self.sem_counts]

    def run(self, first, last, in_refs, out_refs, sems, middle=None):
        def stage(k):
            stages = self.plan(in_refs, out_refs, *sems)
            if k == 0 or len(stages) == 3:
                return stages[k]
            return (lambda: None) if k == 1 else stages[1]

        @pl.when(first)
        def _():
            stage(0)()

        if middle is not None:
            @pl.when(middle)
            def _():
                stage(1)()

        @pl.when(last)
        def _():
            if middle is None:
                stage(1)()
            stage(2)()


def _mm(a, b, out_dtype, tm, tn, tk, name, mode="nn", ride=None):
    if mode == "tn":
        k, m = a.shape
    else:
        m, k = a.shape
    n = b.shape[0] if mode == "nt" else b.shape[1]
    nk = k // tk
    nj, ni = n // tn, m // tm
    n_in = len(ride.args) if ride else 0
    n_out = len(ride.out_shapes) if ride else 0

    def body(a_ref, b_ref, *rest):
        o_ref = rest[n_in]
        acc_ref = rest[n_in + 1 + n_out]
        if ride:
            j, i, kk = pl.program_id(0), pl.program_id(1), pl.program_id(2)
            step = (j * ni + i) * nk + kk
            total = nj * ni * nk
            ride.run(step == 0, step == total - 1, rest[:n_in], rest[n_in + 1:n_in + 1 + n_out],
                     rest[n_in + 2 + n_out:], middle=(step == (2 * total) // 3) if total >= 3 else None)
        av = a_ref[...].astype(BF16)
        bv = b_ref[...].astype(BF16)
        part = _dot_tn(av, bv) if mode == "tn" else _dot_nt(av, bv) if mode == "nt" else _dot(av, bv)
        if nk == 1:
            o_ref[...] = part.astype(out_dtype)
        else:
            kk = pl.program_id(2)

            @pl.when(kk == 0)
            def _():
                acc_ref[...] = part

            @pl.when(kk > 0)
            def _():
                acc_ref[...] += part

            @pl.when(kk == nk - 1)
            def _():
                o_ref[...] = acc_ref[...].astype(out_dtype)

    a_spec = (pl.BlockSpec((tk, tm), lambda j, i, kk: (kk, i)) if mode == "tn"
              else pl.BlockSpec((tm, tk), lambda j, i, kk: (i, kk)))
    b_spec = (pl.BlockSpec((tn, tk), lambda j, i, kk: (j, kk)) if mode == "nt"
              else pl.BlockSpec((tk, tn), lambda j, i, kk: (kk, j)))
    o_spec = pl.BlockSpec((tm, tn), lambda j, i, kk: (i, j))
    o_shape = jax.ShapeDtypeStruct((m, n), out_dtype)
    if not ride:
        return pl.pallas_call(
            body, name=name, grid=(nj, ni, nk), out_shape=o_shape, in_specs=[a_spec, b_spec], out_specs=o_spec,
            scratch_shapes=[pltpu.VMEM((tm, tn), F32)],
            compiler_params=_params(("parallel", "parallel", "arbitrary")),
        )(a, b)
    return pl.pallas_call(
        body, name=name, grid=(nj, ni, nk),
        out_shape=(o_shape, *ride.out_shapes),
        in_specs=[a_spec, b_spec] + ride.in_specs,
        out_specs=(o_spec,) + (ANY,) * n_out,
        scratch_shapes=[pltpu.VMEM((tm, tn), F32)] + ride.scratch(),
        compiler_params=_params(("arbitrary", "arbitrary", "arbitrary")),
    )(a, b, *ride.args)


def _prep(proj, trig, w_uq, w_ukv, g_cq, g_ckv, rope_a, rope_b, scales, tm=256):
    t = proj.shape[0]
    sc_a, sc_b, sc_m = (s * LOG2E for s in scales)

    def body(aq_ref, ak_ref, av_ref, bs_ref, mq_ref, trig_ref, wuq_ref, wukv_ref, gcq_ref, gckv_ref,
             ra_ref, rb_ref, qa_ref, ka_ref, va_ref, qb_ref, kb_ref, vb_ref, qm_ref, cqn_ref, ckvn_ref):
        ta = _rope_tables(trig_ref[:, 0:LANES], trig_ref[:, LANES:2 * LANES], ra_ref[...])
        tb = _rope_tables(trig_ref[:, 2 * LANES:3 * LANES], trig_ref[:, 3 * LANES:4 * LANES], rb_ref[...])
        for j in range(A_WIDTH // LANES):
            sl = slice(j * LANES, (j + 1) * LANES)
            qa_ref[:, sl] = (_rope(aq_ref[:, sl], ta, 8) * sc_a).astype(BF16)
            ka_ref[:, sl] = _rope(ak_ref[:, sl], ta, 8).astype(BF16)
        va_ref[...] = av_ref[...].astype(BF16)
        qm_ref[...] = (mq_ref[...] * sc_m).astype(BF16)

        cq_hat, _ = _rms_hat(bs_ref[:, 0:MLA_Q_RANK], MLA_Q_RANK)
        cqn = (cq_hat * gcq_ref[...]).astype(BF16)
        cqn_ref[...] = cqn
        ckv_hat, _ = _rms_hat(bs_ref[:, MLA_Q_RANK:MLA_Q_RANK + MLA_KV_RANK], MLA_KV_RANK)
        ckvn = (ckv_hat * gckv_ref[...]).astype(BF16)
        ckvn_ref[...] = ckvn
        qfull = _dot_nt(cqn, wuq_ref[...])
        kv = _dot(ckvn, wukv_ref[...])
        kr = _rope(bs_ref[:, 384:512], tb, 16)
        lane = lax.broadcasted_iota(jnp.int32, (1, LANES), 1)
        low = lane < 64
        for h in range(MLA_HEADS):
            sl = slice(h * LANES, (h + 1) * LANES)
            qb_ref[:, sl] = (_rope(qfull[:, sl], tb, 16) * sc_b).astype(BF16)
            kb_ref[:, sl] = jnp.where(low, kv[:, sl], kr).astype(BF16)
            vb_ref[:, sl] = jnp.where(low, 0.0, kv[:, sl]).astype(BF16)

    def col(width, idx):
        return pl.BlockSpec((tm, width), lambda i: (i, idx))

    def full(shape):
        return pl.BlockSpec(shape, lambda i: (0, 0))

    wide = jax.ShapeDtypeStruct((t, 1024), BF16)
    return pl.pallas_call(
        body, name="prep", grid=(t // tm,),
        out_shape=(wide, wide, wide, wide, wide, wide,
                   jax.ShapeDtypeStruct((t, MEM_WIDTH), BF16),
                   jax.ShapeDtypeStruct((t, MLA_Q_RANK), BF16),
                   jax.ShapeDtypeStruct((t, MLA_KV_RANK), BF16)),
        in_specs=[col(1024, 0), col(1024, 1), col(1024, 2), col(512, COL_CQ // 512), col(512, COL_MQ // 512),
                  pl.BlockSpec((tm, 4 * LANES), lambda i: (i, 0)),
                  full((1024, MLA_Q_RANK)), full((MLA_KV_RANK, 1024)),
                  full((1, MLA_Q_RANK)), full((1, MLA_KV_RANK)), full((8, LANES)), full((8, LANES))],
        out_specs=(col(1024, 0),) * 6 + (col(MEM_WIDTH, 0), col(MLA_Q_RANK, 0), col(MLA_KV_RANK, 0)),
        compiler_params=_params(("parallel",)),
    )(proj, proj, proj, proj, proj, trig, w_uq, w_ukv, g_cq, g_ckv, rope_a, rope_b)


def _attn_fwd(q, k, v, *, nb, s, sk, heads, hpb, voff, bq, name):
    nq = s // bq
    width = hpb * LANES
    vblk = voff // hpb

    def body(q_ref, k_ref, v_ref, o_ref, lse_ref):
        for h in range(hpb):
            sl = slice(h * LANES, (h + 1) * LANES)
            sc = _dot_nt(q_ref[:, sl], k_ref[:, sl])
            m = jnp.max(sc, axis=1, keepdims=True)
            p = jnp.exp2(sc - m)
            l = jnp.sum(p, axis=1, keepdims=True)
            o_ref[:, sl] = _dot(p.astype(BF16), v_ref[:, sl]) / l
            lse_ref[:, sl] = jnp.broadcast_to(m + jnp.log(l) * LOG2E, (bq, LANES))

    out = jax.ShapeDtypeStruct((nb * s, heads * LANES), F32)
    ospec = pl.BlockSpec((bq, width), lambda b, i, g: (b * nq + i, g))
    return pl.pallas_call(
        body, name=name, grid=(nb, nq, heads // hpb),
        out_shape=(out, out),
        in_specs=[ospec, pl.BlockSpec((sk, width), lambda b, i, g: (b, g)),
                  pl.BlockSpec((sk, width), lambda b, i, g: (b, vblk + g))],
        out_specs=(ospec, ospec),
        compiler_params=_params(("parallel", "parallel", "parallel")),
    )(q, k, v)


def _attn_bwd(q, k, v, o, do, lse, *, nb, s, sk, heads, hpb, voff, scale, bq, name):
    nq = s // bq
    width = hpb * LANES
    vblk = voff // hpb

    def body(q_ref, k_ref, v_ref, o_ref, do_ref, lse_ref, dq_ref, dk_ref, dv_ref, dk_acc, dv_acc):
        i = pl.program_id(2)

        @pl.when(i == 0)
        def _():
            dk_acc[...] = jnp.zeros_like(dk_acc)
            dv_acc[...] = jnp.zeros_like(dv_acc)

        for h in range(hpb):
            sl = slice(h * LANES, (h + 1) * LANES)
            qh = q_ref[:, sl]
            kk = k_ref[:, sl]
            doh = do_ref[:, sl]
            delta = jnp.sum(doh.astype(F32) * o_ref[:, sl], axis=1, keepdims=True)
            p = jnp.exp2(_dot_nt(qh, kk) - lse_ref[:, h * LANES:h * LANES + 1])
            ds = (p * (_dot_nt(doh, v_ref[:, sl]) - delta)).astype(BF16)
            dq_ref[:, sl] = (_dot(ds, kk) * scale).astype(BF16)
            dk_acc[:, sl] += _dot_tn(ds, qh)
            dv_acc[:, sl] += _dot_tn(p.astype(BF16), doh)

        @pl.when(i == nq - 1)
        def _():
            dk_ref[...] = (dk_acc[...] * LN2).astype(BF16)
            dv_ref[...] = dv_acc[...].astype(BF16)

    qspec = pl.BlockSpec((bq, width), lambda b, g, i: (b * nq + i, g))
    kv_spec = pl.BlockSpec((sk, width), lambda b, g, i: (b, g))
    dq_shape = jax.ShapeDtypeStruct((nb * s, heads * LANES), BF16)
    dkv_shape = jax.ShapeDtypeStruct((nb * sk, heads * LANES), BF16)
    return pl.pallas_call(
        body, name=name, grid=(nb, heads // hpb, nq),
        out_shape=(dq_shape, dkv_shape, dkv_shape),
        in_specs=[qspec, kv_spec, pl.BlockSpec((sk, width), lambda b, g, i: (b, vblk + g)), qspec, qspec, qspec],
        out_specs=(qspec, kv_spec, kv_spec),
        scratch_shapes=[pltpu.VMEM((sk, width), F32), pltpu.VMEM((sk, width), F32)],
        compiler_params=_params(("parallel", "parallel", "arbitrary")),
    )(q, k, v, o, do, lse)


BAND_Q = 128
BAND_WIN = 256


def _band_start(i, s):
    return min(max(i * BAND_Q - 64, 0), s - BAND_WIN)


def _to_pattern_order(src_ref, dst_ref, stage_ref, s, d):
    length = s // d
    stage_ref[...] = src_ref[...].astype(F32)
    for r in range(d):
        dst_ref[r * length:(r + 1) * length, :] = stage_ref[pl.ds(r, length, stride=d), :].astype(dst_ref.dtype)


def _dilated_fwd(q, k, v, bias, bias_index, *, nb, s, name):
    nblk = s // BAND_Q
    npat = len(DILATED)

    def body(q_ref, k_ref, v_ref, bias_ref, o_ref, lse_ref, *rest):
        ordered = rest[:3 * (npat - 1)]
        stage_ref, op_ref, lp_ref, on_ref, ln_ref = rest[3 * (npat - 1):]
        lane = lax.broadcasted_iota(jnp.int32, (1, LANES), 1)
        first = lane < 64
        for p, (_, d) in enumerate(DILATED):
            if d == 1:
                qs, ks, vs = q_ref, k_ref, v_ref
            else:
                qs, ks, vs = ordered[3 * (p - 1):3 * p]
                for src, dst in ((q_ref, qs), (k_ref, ks), (v_ref, vs)):
                    _to_pattern_order(src, dst, stage_ref, s, d)
            for i in range(nblk):
                u0 = i * BAND_Q
                st = _band_start(i, s)
                qi = qs[u0:u0 + BAND_Q, :]
                kw = ks[st:st + BAND_WIN, :]
                vw = vs[st:st + BAND_WIN, :]
                zero = jnp.zeros_like(qi)
                q2 = jnp.concatenate([jnp.where(first, qi, zero), jnp.where(first, zero, qi)], axis=0)
                sc = _dot_nt(q2, kw)
                b = bias_ref[bias_index[p][i]]
                halves = []
                for h in range(2):
                    sh = sc[h * BAND_Q:(h + 1) * BAND_Q] + b
                    m = jnp.max(sh, axis=1, keepdims=True)
                    pr = jnp.exp2(sh - m)
                    l = jnp.sum(pr, axis=1, keepdims=True)
                    halves.append((pr.astype(BF16), l, m + jnp.log(l) * LOG2E))
                o2 = _dot(jnp.concatenate([halves[0][0], halves[1][0]], axis=0), vw)
                o_blk = jnp.where(first, o2[:BAND_Q] / halves[0][1], o2[BAND_Q:] / halves[1][1])
                lse_blk = jnp.where(first, jnp.broadcast_to(halves[0][2], (BAND_Q, LANES)),
                                    jnp.broadcast_to(halves[1][2], (BAND_Q, LANES)))
                op_ref[p, u0:u0 + BAND_Q, :] = o_blk
                lp_ref[p, u0:u0 + BAND_Q, :] = lse_blk
            if d > 1:
                length = s // d
                for r in range(d):
                    on_ref.at[p - 1][pl.ds(r, length, stride=d), :] = op_ref[p, r * length:(r + 1) * length, :]
                    ln_ref.at[p - 1][pl.ds(r, length, stride=d), :] = lp_ref[p, r * length:(r + 1) * length, :]
        lses = [lp_ref[0]] + [ln_ref[p] for p in range(npat - 1)]
        outs = [op_ref[0]] + [on_ref[p] for p in range(npat - 1)]
        m = functools.reduce(jnp.maximum, lses)
        ws = [jnp.exp2(l - m) for l in lses]
        den = functools.reduce(lambda a, c: a + c, ws)
        o_ref[...] = functools.reduce(lambda a, c: a + c, [w * o for w, o in zip(ws, outs)]) / den
        lse_ref[...] = m + jnp.log(den) * LOG2E

    blk = pl.BlockSpec((s, LANES), lambda b, g: (b, g))
    out = jax.ShapeDtypeStruct((nb * s, A_WIDTH), F32)
    copy = jax.ShapeDtypeStruct((nb * s, A_WIDTH), BF16)
    n_copies = 3 * (npat - 1)
    res = pl.pallas_call(
        body, name=name, grid=(nb, A_WIDTH // LANES),
        out_shape=(out, out) + (copy,) * n_copies,
        in_specs=[blk, blk, blk, pl.BlockSpec(bias.shape, lambda b, g: (0, 0, 0))],
        out_specs=(blk, blk) + (blk,) * n_copies,
        scratch_shapes=[pltpu.VMEM((s, LANES), F32), pltpu.VMEM((npat, s, LANES), F32),
                        pltpu.VMEM((npat, s, LANES), F32), pltpu.VMEM((npat - 1, s, LANES), F32),
                        pltpu.VMEM((npat - 1, s, LANES), F32)],
        compiler_params=_params(("parallel", "parallel")),
    )(q, k, v, bias)
    return res[0], res[1], res[2:]


def _dilated_bwd(q, k, v, ordered, o, do, lse, bias, bias_index, *, nb, s, scale, name):
    nblk = s // BAND_Q
    npat = len(DILATED)
    n_copies = 3 * (npat - 1)

    def body(q_ref, k_ref, v_ref, *rest):
        ordered_refs = rest[:n_copies]
        (o_ref, do_ref, lse_ref, bias_ref, dq_out, dk_out, dv_out, stage_ref, rs_ref, dop_ref, rsp_ref,
         dqp_ref, dkp_ref, dvp_ref, dq_ref, dk_ref, dv_ref) = rest[n_copies:]
        lane = lax.broadcasted_iota(jnp.int32, (1, LANES), 1)
        first = lane < 64
        prod = do_ref[...].astype(F32) * o_ref[...]
        d0 = jnp.sum(jnp.where(first, prod, 0.0), axis=1, keepdims=True)
        d1 = jnp.sum(jnp.where(first, 0.0, prod), axis=1, keepdims=True)
        delta = jnp.where(first, jnp.broadcast_to(d0, (s, LANES)), jnp.broadcast_to(d1, (s, LANES)))
        rs_ref[...] = jnp.where((lane & 32) == 0, lse_ref[...], delta)
        for p, (_, d) in enumerate(DILATED):
            length = s // d
            if d == 1:
                qs, ks, vs, dos, rss = q_ref, k_ref, v_ref, do_ref, rs_ref
                dqs, dks, dvs = dq_ref, dk_ref, dv_ref
            else:
                for src, dst in ((do_ref, dop_ref), (rs_ref, rsp_ref)):
                    _to_pattern_order(src, dst, stage_ref, s, d)
                qs, ks, vs = ordered_refs[3 * (p - 1):3 * p]
                dos, rss = dop_ref, rsp_ref
                dqs, dks, dvs = dqp_ref, dkp_ref, dvp_ref
            dks[...] = jnp.zeros((s, LANES), F32)
            dvs[...] = jnp.zeros((s, LANES), F32)
            for i in range(nblk):
                u0 = i * BAND_Q
                st = _band_start(i, s)
                qi = qs[u0:u0 + BAND_Q, :]
                doi = dos[u0:u0 + BAND_Q, :]
                kw = ks[st:st + BAND_WIN, :]
                vw = vs[st:st + BAND_WIN, :]
                zero = jnp.zeros_like(qi)
                q2 = jnp.concatenate([jnp.where(first, qi, zero), jnp.where(first, zero, qi)], axis=0)
                do2 = jnp.concatenate([jnp.where(first, doi, zero), jnp.where(first, zero, doi)], axis=0)
                sc = _dot_nt(q2, kw)
                dp = _dot_nt(do2, vw)
                b = bias_ref[bias_index[p][i]]
                rs_i = rss[u0:u0 + BAND_Q, :]
                ps, dss = [], []
                for h in range(2):
                    rows = slice(h * BAND_Q, (h + 1) * BAND_Q)
                    pr = jnp.exp2(sc[rows] + b - rs_i[:, 64 * h:64 * h + 1])
                    ps.append(pr.astype(BF16))
                    dss.append((pr * (dp[rows] - rs_i[:, 64 * h + 32:64 * h + 33])).astype(BF16))
                p2 = jnp.concatenate(ps, axis=0)
                ds2 = jnp.concatenate(dss, axis=0)
                dq2 = _dot(ds2, kw)
                dqs[u0:u0 + BAND_Q, :] = jnp.where(first, dq2[:BAND_Q], dq2[BAND_Q:]) * scale
                dks[st:st + BAND_WIN, :] += _dot_tn(ds2, q2)
                dvs[st:st + BAND_WIN, :] += _dot_tn(p2, do2)
            if d > 1:
                for dst, src in ((dq_ref, dqp_ref), (dk_ref, dkp_ref), (dv_ref, dvp_ref)):
                    for r in range(d):
                        dst[pl.ds(r, length, stride=d), :] += src[r * length:(r + 1) * length, :]
        dq_out[...] = dq_ref[...].astype(BF16)
        dk_out[...] = (dk_ref[...] * LN2).astype(BF16)
        dv_out[...] = dv_ref[...].astype(BF16)

    blk = pl.BlockSpec((s, LANES), lambda b, g: (b, g))
    out = jax.ShapeDtypeStruct((nb * s, A_WIDTH), BF16)
    f32_buf = pltpu.VMEM((s, LANES), F32)
    bf_buf = pltpu.VMEM((s, LANES), BF16)
    return pl.pallas_call(
        body, name=name, grid=(nb, A_WIDTH // LANES),
        out_shape=(out, out, out),
        in_specs=[blk] * (6 + n_copies) + [pl.BlockSpec(bias.shape, lambda b, g: (0, 0, 0))],
        out_specs=(blk, blk, blk),
        scratch_shapes=[f32_buf, f32_buf, bf_buf] + [f32_buf] * 7,
        compiler_params=_params(("parallel", "parallel")),
    )(q, k, v, *ordered, o, do, lse, bias)


def _post(x, ya, ybp, ym, proj, target, w_out, g_emb, b_emb, g_a, g_b, g_m, g_post, b_post, tm=256):
    t = x.shape[0]

    def body(x_ref, ya_ref, yb_ref, ym_ref, ga_ref, gb_ref, gm_ref, tg_ref, wo_ref,
             ge_ref, be_ref, goa_ref, gob_ref, gom_ref, gp_ref, bp_ref,
             y_ref, dz_ref, doa_ref, dob_ref, dom_ref, dga_ref, dgb_ref, dgm_ref,
             loss_ref, dgp_ref, dbp_ref, dgoa_ref, dgob_ref, dgom_ref):
        i = pl.program_id(0)

        @pl.when(i == 0)
        def _():
            for r in (loss_ref, dgp_ref, dbp_ref, dgoa_ref, dgob_ref, dgom_ref):
                r[...] = jnp.zeros_like(r)

        lane = lax.broadcasted_iota(jnp.int32, (1, LANES), 1)
        low = lane < 64
        xh0, _ = _ln_hat(x_ref[...])
        h = xh0 * ge_ref[...] + be_ref[...]

        ybp_v = yb_ref[...]
        yb = jnp.concatenate(
            [jnp.where(low, pltpu.roll(ybp_v[:, 2 * j * LANES:(2 * j + 1) * LANES], 64, 1),
                       ybp_v[:, (2 * j + 1) * LANES:(2 * j + 2) * LANES]) for j in range(4)], axis=1)

        def gated(raw, gate, gain, width):
            xh, r = _rms_hat(raw, width)
            n = xh * gain
            sg = 1.0 / (1.0 + jnp.exp(-gate))
            return xh, r, n, sg, n * (gate * sg)

        gate_a, gate_b, gate_m = ga_ref[...], gb_ref[...], gm_ref[...]
        xh_a, r_a, n_a, sg_a, y_a = gated(ya_ref[...], gate_a, goa_ref[...], A_WIDTH)
        xh_b, r_b, n_b, sg_b, y_b = gated(yb, gate_b, gob_ref[...], 512)
        xh_m, r_m, n_m, sg_m, y_m = gated(ym_ref[...], gate_m, gom_ref[...], 512)
        y = jnp.concatenate([y_a, y_b, y_m], axis=1).astype(BF16)
        y_ref[...] = y
        z = DEEPNORM_ALPHA * h + _dot(y, wo_ref[...])
        zh, rstd = _ln_hat(z)
        err = zh * gp_ref[...] + bp_ref[...] - tg_ref[...]
        rows = jnp.sum(err * err, axis=1, keepdims=True)
        loss_ref[...] += jnp.broadcast_to(jnp.sum(rows, axis=0, keepdims=True) * (0.5 / D_MODEL), (1, LANES))
        dout = err * (1.0 / D_MODEL)
        dgp_ref[...] += _colsum(dout * zh)
        dbp_ref[...] += _colsum(dout)
        dz = _ln_bwd_rows(dout * gp_ref[...], zh, rstd)
        dz_ref[...] = dz
        dy = _dot_nt(dz.astype(BF16), wo_ref[...])

        def gated_bwd(dyg, xh, r, n, sg, gate, gain, width, dgain_ref):
            dn = dyg * (gate * sg)
            dgate = dyg * n * (sg * (1.0 + gate * (1.0 - sg)))
            dgain_ref[...] += _colsum(dn * xh)
            return _rms_bwd(dn * gain, xh, r, width), dgate

        dya, dgate_a = gated_bwd(dy[:, 0:1024], xh_a, r_a, n_a, sg_a, gate_a, goa_ref[...], A_WIDTH, dgoa_ref)
        dyb, dgate_b = gated_bwd(dy[:, 1024:1536], xh_b, r_b, n_b, sg_b, gate_b, gob_ref[...], 512, dgob_ref)
        dym, dgate_m = gated_bwd(dy[:, 1536:2048], xh_m, r_m, n_m, sg_m, gate_m, gom_ref[...], 512, dgom_ref)
        doa_ref[...] = dya.astype(BF16)
        dom_ref[...] = dym.astype(BF16)
        dga_ref[...] = dgate_a.astype(BF16)
        dgb_ref[...] = dgate_b.astype(BF16)
        dgm_ref[...] = dgate_m.astype(BF16)
        for j in range(4):
            blk = dyb[:, j * LANES:(j + 1) * LANES]
            dob_ref[:, 2 * j * LANES:(2 * j + 1) * LANES] = jnp.where(low, 0.0, pltpu.roll(blk, 64, 1)).astype(BF16)
            dob_ref[:, (2 * j + 1) * LANES:(2 * j + 2) * LANES] = jnp.where(low, 0.0, blk).astype(BF16)

    def col(width, idx):
        return pl.BlockSpec((tm, width), lambda i: (i, idx))

    def full(shape):
        return pl.BlockSpec(shape, lambda i: (0, 0))

    def acc(width):
        return jax.ShapeDtypeStruct((1, width), F32)

    return pl.pallas_call(
        body, name="post", grid=(t // tm,),
        out_shape=(jax.ShapeDtypeStruct((t, 2048), BF16), jax.ShapeDtypeStruct((t, 1024), F32),
                   jax.ShapeDtypeStruct((t, 1024), BF16), jax.ShapeDtypeStruct((t, 1024), BF16),
                   jax.ShapeDtypeStruct((t, 512), BF16),
                   jax.ShapeDtypeStruct((t, 1024), BF16), jax.ShapeDtypeStruct((t, 512), BF16),
                   jax.ShapeDtypeStruct((t, 512), BF16),
                   acc(LANES), acc(1024), acc(1024), acc(1024), acc(512), acc(512)),
        in_specs=[col(1024, 0), col(1024, 0), col(1024, 0), col(512, 0),
                  col(1024, 3), col(512, COL_BG // 512), col(512, COL_MG // 512), col(1024, 0),
                  full((2048, 1024)),
                  full((1, 1024)), full((1, 1024)), full((1, 1024)), full((1, 512)), full((1, 512)),
                  full((1, 1024)), full((1, 1024))],
        out_specs=(col(2048, 0), col(1024, 0), col(1024, 0), col(1024, 0), col(512, 0),
                   col(1024, 0), col(512, 0), col(512, 0),
                   full((1, LANES)), full((1, 1024)), full((1, 1024)), full((1, 1024)), full((1, 512)),
                   full((1, 512))),
        compiler_params=_params(("arbitrary",)),
    )(x, ya, ybp, ym, proj, proj, proj, target, w_out, g_emb, b_emb, g_a, g_b, g_m, g_post, b_post)


def _prep_bwd(dqa, dka, dva, dqb, dkb, dvb, dqm, dga, dgb, dgm, proj, trig, w_uq, w_ukv, g_cq, g_ckv,
              rope_a, rope_b, tm=256):
    t = proj.shape[0]

    def body(dqa_ref, dka_ref, dva_ref, dqb_ref, dkb_ref, dvb_ref, dqm_ref, dga_ref, dgb_ref, dgm_ref,
             bs_ref, trig_ref, wuq_ref, wukv_ref, gcq_ref, gckv_ref, ra_ref, rb_ref,
             dproj_ref, dqf_ref, dkv_ref, dgcq_ref, dgckv_ref):
        i = pl.program_id(0)

        @pl.when(i == 0)
        def _():
            dgcq_ref[...] = jnp.zeros_like(dgcq_ref)
            dgckv_ref[...] = jnp.zeros_like(dgckv_ref)

        ta = _rope_tables(trig_ref[:, 0:LANES], trig_ref[:, LANES:2 * LANES], ra_ref[...])
        tb = _rope_tables(trig_ref[:, 2 * LANES:3 * LANES], trig_ref[:, 3 * LANES:4 * LANES], rb_ref[...])
        for j in range(A_WIDTH // LANES):
            sl = slice(j * LANES, (j + 1) * LANES)
            dproj_ref[:, j * LANES:(j + 1) * LANES] = (
                _rope(dqa_ref[:, sl].astype(F32), ta, 8, inverse=True).astype(BF16))
            dproj_ref[:, 1024 + j * LANES:1024 + (j + 1) * LANES] = (
                _rope(dka_ref[:, sl].astype(F32), ta, 8, inverse=True).astype(BF16))
        dproj_ref[:, 2048:3072] = dva_ref[...]
        dproj_ref[:, 3072:4096] = dga_ref[...]

        lane = lax.broadcasted_iota(jnp.int32, (1, LANES), 1)
        low = lane < 64
        rope_lanes = (lane >= 64) & (lane < 96)
        dkr = jnp.zeros((tm, LANES), F32)
        for h in range(MLA_HEADS):
            sl = slice(h * LANES, (h + 1) * LANES)
            dqf_ref[:, sl] = _rope(dqb_ref[:, sl].astype(F32), tb, 16, inverse=True).astype(BF16)
            dk_h = dkb_ref[:, sl]
            dkv_ref[:, sl] = jnp.where(low, dk_h, dvb_ref[:, sl])
            dkr = dkr + jnp.where(rope_lanes, dk_h.astype(F32), 0.0)
        dkr = _rope(dkr, tb, 16, inverse=True)

        cq_hat, r_q = _rms_hat(bs_ref[:, 0:MLA_Q_RANK], MLA_Q_RANK)
        dcqn = _dot(dqf_ref[...], wuq_ref[...])
        dgcq_ref[...] += _colsum(dcqn * cq_hat)
        dproj_ref[:, COL_CQ:COL_CQ + 256] = _rms_bwd(dcqn * gcq_ref[...], cq_hat, r_q, MLA_Q_RANK).astype(BF16)
        ckv_hat, r_kv = _rms_hat(bs_ref[:, MLA_Q_RANK:MLA_Q_RANK + MLA_KV_RANK], MLA_KV_RANK)
        dckvn = _dot_nt(dkv_ref[...], wukv_ref[...])
        dgckv_ref[...] += _colsum(dckvn * ckv_hat)
        dproj_ref[:, COL_CQ + 256:COL_CQ + 384] = (
            _rms_bwd(dckvn * gckv_ref[...], ckv_hat, r_kv, MLA_KV_RANK).astype(BF16))
        dproj_ref[:, COL_CQ + 384:COL_CQ + 512] = dkr.astype(BF16)
        dproj_ref[:, COL_BG:COL_BG + 512] = dgb_ref[...]
        dproj_ref[:, COL_MQ:COL_MQ + 512] = dqm_ref[...]
        dproj_ref[:, COL_MG:COL_MG + 512] = dgm_ref[...]

    def col(width, idx):
        return pl.BlockSpec((tm, width), lambda i: (i, idx))

    def full(shape):
        return pl.BlockSpec(shape, lambda i: (0, 0))

    return pl.pallas_call(
        body, name="prep_bwd", grid=(t // tm,),
        out_shape=(jax.ShapeDtypeStruct((t, PROJ_W), BF16), jax.ShapeDtypeStruct((t, 1024), BF16),
                   jax.ShapeDtypeStruct((t, 1024), BF16),
                   jax.ShapeDtypeStruct((1, MLA_Q_RANK), F32), jax.ShapeDtypeStruct((1, MLA_KV_RANK), F32)),
        in_specs=[col(1024, 0)] * 6 + [col(512, 0), col(1024, 0), col(512, 0), col(512, 0),
                  col(512, COL_CQ // 512), pl.BlockSpec((tm, 4 * LANES), lambda i: (i, 0)),
                  full((1024, MLA_Q_RANK)), full((MLA_KV_RANK, 1024)),
                  full((1, MLA_Q_RANK)), full((1, MLA_KV_RANK)), full((8, LANES)), full((8, LANES))],
        out_specs=(col(PROJ_W, 0), col(1024, 0), col(1024, 0), full((1, MLA_Q_RANK)), full((1, MLA_KV_RANK))),
        compiler_params=_params(("arbitrary",)),
    )(dqa, dka, dva, dqb, dkb, dvb, dqm, dga, dgb, dgm, proj, trig, w_uq, w_ukv, g_cq, g_ckv, rope_a, rope_b)


def _adamw_math(gv, w, m, v):
    m_new = ADAM_B1 * m + (1.0 - ADAM_B1) * gv
    v_new = ADAM_B2 * v + (1.0 - ADAM_B2) * (gv * gv)
    m_hat = m_new / (1.0 - ADAM_B1 ** ADAM_STEP)
    v_hat = v_new / (1.0 - ADAM_B2 ** ADAM_STEP)
    return -ADAM_LR * (m_hat / (jnp.sqrt(v_hat) + ADAM_EPS) + ADAM_WD * w), m_new, v_new


def _adamw(g, w, m, v, tr, name):
    r, cols = w.shape

    def body(g_ref, w_ref, m_ref, v_ref, go_ref, d_ref, nm_ref, nv_ref):
        gv = g_ref[...]
        go_ref[...] = gv
        d_ref[...], nm_ref[...], nv_ref[...] = _adamw_math(gv, w_ref[...], m_ref[...], v_ref[...])

    tile = pl.BlockSpec((tr, cols), lambda i: (i, 0))
    shape = jax.ShapeDtypeStruct((r, cols), F32)
    return pl.pallas_call(
        body, name=name, grid=(r // tr,),
        out_shape=(shape,) * 4, in_specs=[tile] * 4, out_specs=(tile,) * 4,
        compiler_params=_params(("parallel",)),
    )(g, w, m, v)


def _adamw_pieces(g, w, m, v, pieces, name):
    shapes = [jax.ShapeDtypeStruct((r1 - r0, c1 - c0), F32) for r0, r1, c0, c1 in pieces]

    def body(g_ref, w_ref, m_ref, v_ref, *outs):
        gv = g_ref[...]
        results = (gv,) + _adamw_math(gv, w_ref[...], m_ref[...], v_ref[...])
        for kind, full in enumerate(results):
            for p, (r0, r1, c0, c1) in enumerate(pieces):
                outs[kind * len(pieces) + p][...] = full[r0:r1, c0:c1]

    flat = pl.pallas_call(
        body, name=name, out_shape=tuple(shapes) * 4,
        in_specs=[IN_VMEM] * 4, out_specs=tuple([IN_VMEM] * (4 * len(pieces))),
        compiler_params=_params(None),
    )(g, w, m, v)
    return [[flat[kind * len(pieces) + p] for kind in range(4)] for p in range(len(pieces))]


def _core_sum(g, recv, core, rows, tr, name, ride=None):
    cols = g.shape[2]
    nblk = rows // tr
    n_in = len(ride.args) if ride else 0
    n_out = len(ride.out_shapes) if ride else 0

    def body(c_ref, g_ref, r_ref, *rest):
        sf_ref, sb_ref = rest[n_in], rest[n_in + 1]
        if ride:
            j, i = pl.program_id(0), pl.program_id(1)
            ride.run((j == 0) & (i == 0), (j == 3) & (i == nblk - 1), rest[:n_in],
                     rest[n_in + 2:n_in + 2 + n_out], rest[n_in + 2 + n_out:])
        tot = g_ref[...] + r_ref[...]
        sf_ref[...] = tot
        sb_ref[...] = tot.astype(BF16)

    half = pl.BlockSpec((None, tr, cols), lambda j, i, c_ref: (j, i, 0))
    shapes = (jax.ShapeDtypeStruct((4, rows, cols), F32), jax.ShapeDtypeStruct((4, rows, cols), BF16))
    return pl.pallas_call(
        body, name=name,
        grid_spec=pltpu.PrefetchScalarGridSpec(
            num_scalar_prefetch=1, grid=(4, nblk),
            in_specs=[pl.BlockSpec((None, tr, cols), lambda j, i, c_ref: (j, c_ref[0] * nblk + i, 0)), half]
            + (ride.in_specs if ride else []),
            out_specs=(half, half) + (ANY,) * n_out,
            scratch_shapes=ride.scratch() if ride else []),
        out_shape=shapes + tuple(ride.out_shapes if ride else ()),
        compiler_params=_params(("arbitrary", "arbitrary") if ride else ("parallel", "parallel")),
    )(core, g, recv, *(ride.args if ride else ()))


def _half_to_sibling(g4):
    def plan(in_refs, out_refs, send_sems, recv_sems):
        x, y, c = _position()
        cp = pltpu.make_async_remote_copy(
            src_ref=in_refs[0].at[:, 1 - c], dst_ref=out_refs[0], send_sem=send_sems.at[0],
            recv_sem=recv_sems.at[0], device_id=(x, y, 1 - c), device_id_type=MESH)

        def finish():
            cp.wait_recv()
            cp.wait_send()

        return cp.start, finish

    return _Ride([g4], [jax.ShapeDtypeStruct((4, g4.shape[2], 1024), F32)], (1, 1), plan)


def _gather_plan(src_ref, dst_ref, send_sems, recv_sems, local_sems):
    x, y, c = _position()
    me = 2 * x + y
    rows = src_ref.shape[1]
    cut = -(-rows // 32) * 16
    pieces = (pl.ds(0, cut), pl.ds(cut, rows - cut))
    local = pltpu.make_async_copy(src_ref, dst_ref.at[me], local_sems.at[0])

    def over_ici(sem, k, chip, t, src=None):
        where = dst_ref.at[chip, c, pieces[t]]
        return pltpu.make_async_remote_copy(
            src_ref=where if src is None else src, dst_ref=where, send_sem=send_sems.at[sem],
            recv_sem=recv_sems.at[sem], device_id=(x ^ (k >> 1), y ^ (k & 1), c), device_id_type=MESH)

    def mine_to(k, t):
        return over_ici(2 * (k - 1) + t, k, me, t, src=src_ref.at[c, pieces[t]])

    def from_neighbour(k, t):
        return over_ici(2 * (k - 1) + t, k, me ^ k, t)

    def to_sibling(k, half):
        piece = dst_ref.at[me ^ k, half]
        return pltpu.make_async_remote_copy(
            src_ref=piece, dst_ref=piece, send_sem=send_sems.at[5 + k], recv_sem=recv_sems.at[5 + k],
            device_id=(x, y, 1 - c), device_id_type=MESH)

    sends = [mine_to(2, 0), mine_to(1, 1), mine_to(2, 1), mine_to(1, 0)]
    onward = [over_ici(4, 1, me ^ 2, 0), over_ici(5, 2, me ^ 1, 1)]

    def start():
        local.start()
        for cp in sends:
            cp.start()

    def relay():
        from_neighbour(2, 0).wait_recv()
        onward[0].start()
        from_neighbour(1, 1).wait_recv()
        onward[1].start()
        from_neighbour(2, 1).wait_recv()
        to_sibling(2, c).start()
        from_neighbour(1, 0).wait_recv()
        to_sibling(1, c).start()
        over_ici(4, 1, me ^ 3, 0).wait_recv()
        over_ici(5, 2, me ^ 3, 1).wait_recv()
        to_sibling(3, c).start()

    def finish():
        for k in (1, 2, 3):
            to_sibling(k, 1 - c).wait_recv()
        for cp in sends + onward + [to_sibling(k, c) for k in (1, 2, 3)]:
            cp.wait_send()
        local.wait()

    return start, relay, finish


def _gather_ride(shard):
    def plan(in_refs, out_refs, send_sems, recv_sems, local_sems):
        return _gather_plan(in_refs[0], out_refs[0], send_sems, recv_sems, local_sems)

    return _Ride([shard], [jax.ShapeDtypeStruct((4,) + shard.shape, shard.dtype)], (9, 9, 1), plan,
                 in_specs=[IN_VMEM])


def _chip_sum(sf, recv, chip, rows, tr, name):
    cols = sf.shape[2]

    def body(me_ref, sf_ref, r_ref, out_ref):
        acc = sf_ref[...]
        for k in range(3):
            acc = acc + r_ref[k].astype(F32)
        out_ref[...] = acc

    return pl.pallas_call(
        body, name=name,
        grid_spec=pltpu.PrefetchScalarGridSpec(
            num_scalar_prefetch=1, grid=(rows // tr,),
            in_specs=[pl.BlockSpec((None, tr, cols), lambda i, me_ref: (me_ref[0], i, 0)),
                      pl.BlockSpec((3, tr, cols), lambda i, me_ref: (0, i, 0))],
            out_specs=pl.BlockSpec((tr, cols), lambda i, me_ref: (i, 0))),
        out_shape=jax.ShapeDtypeStruct((rows, cols), F32),
        compiler_params=_params(("parallel",)),
    )(chip, sf, recv)


def _position():
    return lax.axis_index("x"), lax.axis_index("y"), lax.axis_index("c")


def _dh_scatter(dproj, w_in_arr_t, x, dz, g, sb_in, sb_rest, tm=1024, tk=1024):
    t, d = x.shape
    nk = dproj.shape[1] // tk
    ni = t // tm

    def body(dp_ref, w_ref, x_ref, dz_ref, g_ref, sbin_ref, sbrest_ref,
             dx_ref, dg_ref, db_ref, rin_ref, rrest_ref, acc_ref, send_sems, recv_sems):
        i = pl.program_id(0)
        kk = pl.program_id(1)
        px, py, pc = _position()
        me = 2 * px + py
        srcs = (sbin_ref, sbrest_ref)
        dsts = (rin_ref, rrest_ref)

        def copy(a, k):
            return pltpu.make_async_remote_copy(
                src_ref=srcs[a].at[me ^ k], dst_ref=dsts[a].at[k - 1],
                send_sem=send_sems.at[3 * a + k - 1], recv_sem=recv_sems.at[3 * a + k - 1],
                device_id=(px ^ (k >> 1), py ^ (k & 1), pc), device_id_type=MESH)

        pairs = [(a, k) for a in range(2) for k in (1, 2, 3)]

        @pl.when((i == 0) & (kk == 0))
        def _():
            dg_ref[...] = jnp.zeros_like(dg_ref)
            db_ref[...] = jnp.zeros_like(db_ref)
            for a, k in pairs:
                copy(a, k).start()

        part = _dot(dp_ref[...], w_ref[...])

        @pl.when(kk == 0)
        def _():
            acc_ref[...] = part

        @pl.when(kk > 0)
        def _():
            acc_ref[...] += part

        @pl.when(kk == nk - 1)
        def _():
            xh, rstd = _ln_hat(x_ref[...])
            dht = acc_ref[...] + DEEPNORM_ALPHA * dz_ref[...]
            dg_ref[...] += _colsum(dht * xh)
            db_ref[...] += _colsum(dht)
            dx_ref[...] = _ln_bwd_rows(dht * g_ref[...], xh, rstd)

        @pl.when((i == ni - 1) & (kk == nk - 1))
        def _():
            for a, k in pairs:
                copy(a, k).wait_recv()
            for a, k in pairs:
                copy(a, k).wait_send()

    tile = pl.BlockSpec((tm, d), lambda i, kk: (i, 0))
    row = pl.BlockSpec((1, d), lambda i, kk: (0, 0))
    return pl.pallas_call(
        body, name="dh_scatter", grid=(ni, nk),
        out_shape=(jax.ShapeDtypeStruct((t, d), F32), jax.ShapeDtypeStruct((1, d), F32),
                   jax.ShapeDtypeStruct((1, d), F32),
                   jax.ShapeDtypeStruct((3, HALF_IN, 1024), BF16),
                   jax.ShapeDtypeStruct((3, HALF_REST, 1024), BF16)),
        in_specs=[pl.BlockSpec((tm, tk), lambda i, kk: (i, kk)), pl.BlockSpec((tk, d), lambda i, kk: (kk, 0)),
                  tile, tile, row, ANY, ANY],
        out_specs=(tile, row, row, ANY, ANY),
        scratch_shapes=[pltpu.VMEM((tm, d), F32), pltpu.SemaphoreType.DMA((6,)), pltpu.SemaphoreType.DMA((6,))],
        compiler_params=_params(("arbitrary", "arbitrary")),
    )(dproj, w_in_arr_t, x, dz, g, sb_in, sb_rest)


def _join_halves(gh_in, gh_rest):
    def body(hin_ref, hrest_ref, oin_ref, orest_ref, send_sems, recv_sems, local_sems):
        x, y, c = _position()
        srcs = (hin_ref, hrest_ref)
        dsts = (oin_ref, orest_ref)

        def rows(a, half):
            return dsts[a].at[half]

        local = [pltpu.make_async_copy(srcs[a], rows(a, c), local_sems.at[a]) for a in range(2)]
        remote = [pltpu.make_async_remote_copy(
            src_ref=srcs[a], dst_ref=rows(a, c), send_sem=send_sems.at[a], recv_sem=recv_sems.at[a],
            device_id=(x, y, 1 - c), device_id_type=MESH) for a in range(2)]
        for cp in local + remote:
            cp.start()
        for a in range(2):
            pltpu.make_async_remote_copy(
                src_ref=srcs[a], dst_ref=rows(a, 1 - c), send_sem=send_sems.at[a], recv_sem=recv_sems.at[a],
                device_id=(x, y, 1 - c), device_id_type=MESH).wait_recv()
        for cp in remote:
            cp.wait_send()
        for cp in local:
            cp.wait()

    return pl.pallas_call(
        body, name="join_halves",
        out_shape=(jax.ShapeDtypeStruct((2, HALF_IN, 1024), F32),
                   jax.ShapeDtypeStruct((2, HALF_REST, 1024), F32)),
        in_specs=[IN_VMEM, IN_VMEM], out_specs=(ANY, ANY),
        scratch_shapes=[pltpu.SemaphoreType.DMA((2,)), pltpu.SemaphoreType.DMA((2,)), pltpu.SemaphoreType.DMA((2,))],
    )(gh_in, gh_rest)


def _allreduce_small(vec):
    def body(vec_ref, out_ref, all_ref, send_sems, recv_sems):
        x, y, c = _position()
        me = 4 * x + 2 * y + c
        all_ref[me] = vec_ref[...]

        def copy(k, slot):
            return pltpu.make_async_remote_copy(
                src_ref=vec_ref, dst_ref=all_ref.at[slot], send_sem=send_sems.at[k - 1], recv_sem=recv_sems.at[k - 1],
                device_id=(x ^ (k >> 2), y ^ ((k >> 1) & 1), c ^ (k & 1)), device_id_type=MESH)

        copies = [copy(k, me) for k in range(1, 8)]
        for cp in copies:
            cp.start()
        for k in range(1, 8):
            copy(k, me ^ k).wait_recv()
        for cp in copies:
            cp.wait_send()
        total = all_ref[0]
        for d in range(1, 8):
            total = total + all_ref[d]
        out_ref[...] = total

    return pl.pallas_call(
        body, name="allreduce_small",
        out_shape=jax.ShapeDtypeStruct(vec.shape, vec.dtype),
        in_specs=[pl.BlockSpec(memory_space=pltpu.VMEM)], out_specs=pl.BlockSpec(memory_space=pltpu.VMEM),
        scratch_shapes=[pltpu.VMEM((8,) + vec.shape, vec.dtype), pltpu.SemaphoreType.DMA((7,)),
                        pltpu.SemaphoreType.DMA((7,))],
    )(vec)


def _pack_rest(w_uq, w_ukv, w_mem, w_out):
    rows = jnp.concatenate([w_uq[0].T.reshape(-1, 1024), w_ukv.reshape(-1, 1024), w_mem.reshape(-1, 1024),
                            w_out.reshape(-1, 1024)], axis=0)
    return jnp.pad(rows, ((0, ROWS_REST - ROWS_USED), (0, 0)))


def _arranged_w_in(g_in):
    z = functools.partial(jnp.zeros, dtype=g_in.dtype)
    cut = 4480 - 2 * SHARD_ROWS
    return jnp.concatenate(
        [g_in[0, :SHARD_ROWS], g_in[1, :SHARD_ROWS], g_in[2, :cut], z((64, 1024)), g_in[2, cut:cut + 32],
         z((32, 1024)), g_in[2, cut + 32:SHARD_ROWS], g_in[3, :SHARD_ROWS]], axis=0)


def _rest_weights(g_rest):
    w_uq_t = g_rest[:, 0:ROWS_UQ].reshape(768, 256)
    w_uq_pad_t = jnp.pad(w_uq_t.reshape(MLA_HEADS, MLA_QK_DIM, 256), ((0, 0), (0, 32), (0, 0))).reshape(1024, 256)
    w_ukv = jnp.concatenate([g_rest[j, ROWS_UQ:ROWS_UQ + ROWS_UKV].reshape(128, 256) for j in range(4)], axis=1)
    lo = ROWS_UQ + ROWS_UKV
    w_mem = g_rest[:, lo:lo + ROWS_MEM].reshape(4 * ROWS_MEM, 1024)
    w_out = g_rest[:, lo + ROWS_MEM:lo + ROWS_MEM + ROWS_OUT].reshape(4 * ROWS_OUT, 1024)
    return w_uq_pad_t, w_ukv, w_mem, w_out


def _split_in(dw_in_arr_t):
    a = dw_in_arr_t
    gap = jnp.zeros((ROWS_IN - SHARD_ROWS, 1024), a.dtype)
    nat = 4608 - 96
    pieces = [a[:SHARD_ROWS], gap, a[SHARD_ROWS:2 * SHARD_ROWS], gap,
              a[2 * SHARD_ROWS:4480], a[4544:4576], a[4608:4608 + 3 * SHARD_ROWS - nat], gap,
              a[4608 + 3 * SHARD_ROWS - nat:], gap]
    return jnp.concatenate(pieces, axis=0).reshape(4, ROWS_IN, 1024)


def _split_rest(dw_uq_pad_t, dw_ukv, dw_mem, dw_out):
    dw_uq_t = dw_uq_pad_t.reshape(MLA_HEADS, LANES, 256)[:, :MLA_QK_DIM].reshape(4, ROWS_UQ, 1024)
    parts = [dw_uq_t, dw_ukv.reshape(128, 4, 256).transpose(1, 0, 2).reshape(4, ROWS_UKV, 1024),
             dw_mem.reshape(4, ROWS_MEM, 1024), dw_out.reshape(4, ROWS_OUT, 1024)]
    return jnp.pad(jnp.concatenate(parts, axis=1), ((0, 0), (0, ROWS_REST - ROWS_USED), (0, 0)))


def _rope_consts(rot, first, period):
    half = rot // 2
    inv_freq = np.float32(ROPE_THETA) ** (-(np.arange(0, rot, 2, dtype=np.float32) / np.float32(rot)))
    lane = np.arange(LANES) % period - first
    in_rot = (lane >= 0) & (lane < rot)
    out = np.zeros((8, LANES), np.float32)
    out[0] = np.where(in_rot, inv_freq[np.clip(lane, 0, rot - 1) % half], 0.0)
    out[1] = in_rot & (lane < half)
    out[2] = in_rot & (lane >= half)
    return jnp.asarray(out)


def _band_bias(s):
    nblk = s // BAND_Q
    starts = np.array([_band_start(i, s) for i in range(nblk)])
    uq = (np.arange(nblk)[:, None] * BAND_Q + np.arange(BAND_Q)[None, :])[:, :, None]
    uk = (starts[:, None] + np.arange(BAND_WIN)[None, :])[:, None, :]
    tiles, index, seen = [], [], {}
    for _, d in DILATED:
        length = s // d
        ok = (uq // length == uk // length) & (np.abs(uq - uk) <= 64)
        row = []
        for i in range(nblk):
            key = ok[i].tobytes()
            if key not in seen:
                seen[key] = len(tiles)
                tiles.append(np.where(ok[i], 0.0, NEG_INF).astype(np.float32))
            row.append(seen[key])
        index.append(row)
    return jnp.asarray(np.stack(tiles, axis=0)), index


def _forward_backward(h, proj, trig, rope_consts, x, mem, target, weights, gains):
    w_uq_pad_t, w_ukv, w_mem, w_out = weights
    g_emb, b_emb, g_cq, g_ckv, g_out_a, g_out_b, g_out_m, g_post, b_post = gains
    nb, s, d = x.shape
    t = nb * s
    x2 = x.reshape(t, d)
    mem2 = mem.reshape(nb * N_MEM, d)
    tgt2 = target.reshape(t, d)
    rope_a, rope_b = rope_consts
    bias, bias_index = _band_bias(s)
    scales = (0.125, MLA_QK_DIM ** -0.5, 128 ** -0.5)

    qa, ka, va, qb, kb, vb, qm, cqn, ckvn = _prep(proj, trig, w_uq_pad_t, w_ukv, g_cq, g_ckv, rope_a, rope_b, scales)
    mkv = _mm(mem2, w_mem, BF16, nb * N_MEM, 1024, 1024, "mem_kv")

    cfg_b = dict(nb=nb, s=s, sk=s, heads=8, voff=0, bq=256)
    cfg_m = dict(nb=nb, s=s, sk=N_MEM, heads=4, hpb=2, voff=4, bq=1024)
    ya, lse_a, qkv_ordered = _dilated_fwd(qa, ka, va, bias, bias_index, nb=nb, s=s, name="attn_a_fwd")
    yb, lse_b = _attn_fwd(qb, kb, vb, name="attn_b_fwd", hpb=4, **cfg_b)
    ym, lse_m = _attn_fwd(qm, mkv, mkv, name="attn_m_fwd", **cfg_m)

    (y, dz, doa, dob, dom, dga, dgb, dgm, loss, dg_post, db_post, dg_a, dg_b, dg_m) = _post(
        x2, ya, yb, ym, proj, tgt2, w_out, g_emb, b_emb, g_out_a, g_out_b, g_out_m, g_post, b_post)

    dqa, dka, dva = _dilated_bwd(qa, ka, va, qkv_ordered, ya, doa, lse_a, bias, bias_index, nb=nb, s=s, scale=scales[0],
                                 name="attn_a_bwd")
    dqb, dkb, dvb = _attn_bwd(qb, kb, vb, yb, dob, lse_b, name="attn_b_bwd", scale=scales[1], hpb=2, **cfg_b)
    dqm, dmk, dmv = _attn_bwd(qm, mkv, mkv, ym, dom, lse_m, name="attn_m_bwd", scale=scales[2], **cfg_m)
    dmkv = jnp.concatenate([dmk, dmv], axis=1)

    dproj, dqf, dkv, dg_cq, dg_ckv = _prep_bwd(
        dqa, dka, dva, dqb, dkb, dvb, dqm, dga, dgb, dgm, proj, trig, w_uq_pad_t, w_ukv, g_cq, g_ckv, rope_a, rope_b)

    small_rows = (dg_cq, dg_ckv, loss, dg_a, dg_b, dg_m, dg_post, db_post)
    return (dproj, h, y, dz, dqf, cqn, ckvn, dkv, mem2, dmkv), x2, small_rows


def _weight_grads(operands, core):
    dproj, h, y, dz, dqf, cqn, ckvn, dkv, mem2, dmkv = operands
    dw_in_arr_t = _mm(dproj, h, F32, 1024, 1024, 4096, "dw_in", mode="tn")
    g_in = _split_in(dw_in_arr_t)
    dw_out, r_in = _mm(y, dz, F32, 1024, 1024, 2048, "dw_out", mode="tn",
                       ride=_half_to_sibling(g_in.reshape(4, 2, HALF_IN, 1024)))
    dw_uq_pad_t = _mm(dqf, cqn, F32, 1024, 256, 4096, "dw_uq", mode="tn")
    dw_ukv = _mm(ckvn, dkv, F32, 128, 1024, 4096, "dw_ukv", mode="tn")
    dw_mem = _mm(mem2, dmkv, F32, 1024, 1024, mem2.shape[0], "dw_mem", mode="tn")
    g_rest = _split_rest(dw_uq_pad_t, dw_ukv, dw_mem, dw_out)
    sf_in, sb_in, r_rest = _core_sum(g_in, r_in, core, HALF_IN, HALF_IN // 2, "core_sum_in",
                                     ride=_half_to_sibling(g_rest.reshape(4, 2, HALF_REST, 1024)))
    sf_rest, sb_rest = _core_sum(g_rest, r_rest, core, HALF_REST, HALF_REST, "core_sum_rest")
    return sf_in, sb_in, sf_rest, sb_rest


def _small_block(dg_emb, db_emb, small_rows):
    dg_cq, dg_ckv, loss, dg_a, dg_b, dg_m, dg_post, db_post = small_rows
    row2 = jnp.concatenate([dg_cq, dg_ckv, loss, jnp.zeros((1, 512), F32)], axis=1)
    return jnp.concatenate([dg_emb, db_emb, row2, dg_a, jnp.concatenate([dg_b, dg_m], axis=1), dg_post, db_post,
                            jnp.zeros((1, 1024), F32)], axis=0)


def _pack_small(g_emb, b_emb, g_cq, g_ckv, g_out_a, g_out_b, g_out_m, g_post, b_post):
    row2 = jnp.concatenate([g_cq.reshape(1, -1), g_ckv.reshape(1, -1), jnp.zeros((1, 640), F32)], axis=1)
    return jnp.concatenate([g_emb.reshape(1, -1), b_emb.reshape(1, -1), row2, g_out_a.reshape(1, -1),
                            jnp.concatenate([g_out_b.reshape(1, -1), g_out_m.reshape(1, -1)], axis=1),
                            g_post.reshape(1, -1), b_post.reshape(1, -1), jnp.zeros((1, 1024), F32)], axis=0)


def kernel(x, mem, positions, g_emb, b_emb, w_in, g_cq, g_ckv, w_uq, w_ukv, w_mem_kv, g_out_a, g_out_b, g_out_m, w_out, g_post, b_post, loss_target, m_g_emb, m_b_emb, m_w_in, m_g_cq, m_g_ckv, m_w_uq, m_w_ukv, m_w_mem_kv, m_g_out_a, m_g_out_b, m_g_out_m, m_w_out, m_g_post, m_b_post, v_g_emb, v_b_emb, v_w_in, v_g_cq, v_g_ckv, v_w_uq, v_w_ukv, v_w_mem_kv, v_g_out_a, v_g_out_b, v_g_out_m, v_w_out, v_g_post, v_b_post):
    w_rest = _pack_rest(w_uq, w_ukv, w_mem_kv, w_out)
    w_in_t = w_in[0].T
    w_in_b = jnp.pad(w_in_t.astype(BF16), ((0, ROWS_IN - SHARD_ROWS), (0, 0)))
    gains = (g_emb.reshape(1, -1), b_emb.reshape(1, -1), g_cq, g_ckv, g_out_a, g_out_b, g_out_m, g_post, b_post)
    rope_consts = (_rope_consts(16, 0, 64), _rope_consts(32, 64, 128))
    h, trig, gathered_in = _ln_fwd(x.reshape(-1, D_MODEL), gains[0], gains[1],
                                   positions.reshape(-1, 1).astype(F32), *rope_consts,
                                   ride=_gather_ride(w_in_b.reshape(2, HALF_IN, 1024)))
    w_in_arr_t = _arranged_w_in(gathered_in.reshape(4, ROWS_IN, 1024))
    proj, gathered_rest = _mm(h, w_in_arr_t, F32, 1024, 2048, 1024, "in_proj", mode="nt",
                              ride=_gather_ride(w_rest.astype(BF16).reshape(2, HALF_REST, 1024)))
    weights = _rest_weights(gathered_rest.reshape(4, ROWS_REST, 1024))
    operands, x2, small_rows = _forward_backward(h, proj, trig, rope_consts, x, mem, loss_target, weights, gains)

    core = lax.axis_index("c").astype(jnp.int32).reshape(1)
    chip = (2 * lax.axis_index("x") + lax.axis_index("y")).astype(jnp.int32).reshape(1)
    sf_in, sb_in, sf_rest, sb_rest = _weight_grads(operands, core)
    grad_x, dg_emb, db_emb, rb_in, rb_rest = _dh_scatter(operands[0], w_in_arr_t, x2, operands[3], gains[0],
                                                         sb_in, sb_rest)
    gh_in = _chip_sum(sf_in, rb_in, chip, HALF_IN, HALF_IN // 2, "chip_sum_in")
    gh_rest = _chip_sum(sf_rest, rb_rest, chip, HALF_REST, HALF_REST, "chip_sum_rest")
    grad_in, grad_rest = _join_halves(gh_in, gh_rest)
    grad_in = grad_in.reshape(ROWS_IN, 1024)
    grad_rest = grad_rest.reshape(ROWS_REST, 1024)

    big_in = _adamw(grad_in, w_in_t, m_w_in[0].T, v_w_in[0].T, SHARD_ROWS // 3, "adamw_in")
    uq, ukv, wmem, wout = _adamw_pieces(
        grad_rest, w_rest, _pack_rest(m_w_uq, m_w_ukv, m_w_mem_kv, m_w_out),
        _pack_rest(v_w_uq, v_w_ukv, v_w_mem_kv, v_w_out), REST_PIECES, "adamw_rest")
    small_sum = _allreduce_small(_small_block(dg_emb, db_emb, small_rows))
    sm = _adamw_pieces(
        small_sum,
        _pack_small(g_emb, b_emb, g_cq, g_ckv, g_out_a, g_out_b, g_out_m, g_post, b_post),
        _pack_small(m_g_emb, m_b_emb, m_g_cq, m_g_ckv, m_g_out_a, m_g_out_b, m_g_out_m, m_g_post, m_b_post),
        _pack_small(v_g_emb, v_b_emb, v_g_cq, v_g_ckv, v_g_out_a, v_g_out_b, v_g_out_m, v_g_post, v_b_post),
        SMALL_PIECES, "adamw_small")
    loss = small_sum[2, 384]

    def ordered(kind):
        s_gemb, s_bemb, s_gcq, s_gckv, s_ga, s_gb, s_gm, s_gpost, s_bpost = [piece[kind] for piece in sm]
        return [s_gemb.reshape(-1), s_bemb.reshape(-1), big_in[kind].T[None], s_gcq, s_gckv,
                uq[kind].reshape(192, 256).T[None], ukv[kind].reshape(1, 128, 256), wmem[kind][None], s_ga, s_gb,
                s_gm, wout[kind][None], s_gpost, s_bpost]

    return (loss, grad_x.reshape(x.shape), *ordered(0), *ordered(1), *ordered(2), *ordered(3))
```

```python
import functools
import math

import jax
import jax.numpy as jnp
import numpy as np
from jax import lax
from jax.experimental import pallas as pl
from jax.experimental.pallas import tpu as pltpu

F32 = jnp.float32
BF16 = jnp.bfloat16
MESH = pl.DeviceIdType.MESH
ANY = pl.BlockSpec(memory_space=pl.ANY)
IN_VMEM = pl.BlockSpec(memory_space=pltpu.VMEM)

D_MODEL = 1024
A_WIDTH = 1024
MLA_HEADS = 8
MLA_Q_RANK = 256
MLA_KV_RANK = 128
MLA_QK_DIM = 96
MEM_WIDTH = 512
N_MEM = 256
ROPE_THETA = 500000.0
NORM_EPS = 1e-5
NEG_INF = -1e30
DEEPNORM_ALPHA = 2.0 ** 0.25
DILATED = ((64, 1), (256, 4), (1024, 16))

ADAM_LR = 0.001
ADAM_B1 = 0.9
ADAM_B2 = 0.999
ADAM_EPS = 1e-08
ADAM_WD = 0.01
ADAM_STEP = 10

LANES = 128
VMEM_LIMIT = 56 * 1024 * 1024
LOG2E = math.log2(math.e)
LN2 = math.log(2.0)

PROJ_W = 6144
COL_CQ = 4096
COL_BG = 4608
COL_MQ = 5120
COL_MG = 5632

SHARD_ROWS = 1512
ROWS_IN = 1536
ROWS_UQ, ROWS_UKV, ROWS_MEM, ROWS_OUT = 48, 32, 256, 512
ROWS_USED = ROWS_UQ + ROWS_UKV + ROWS_MEM + ROWS_OUT
ROWS_REST = 864
HALF_IN = ROWS_IN // 2
HALF_REST = ROWS_REST // 2
REST_PIECES = ((0, 48, 0, 1024), (48, 80, 0, 1024), (80, 336, 0, 1024), (336, 848, 0, 1024))
SMALL_PIECES = ((0, 1, 0, 1024), (1, 2, 0, 1024), (2, 3, 0, 256), (2, 3, 256, 384), (3, 4, 0, 1024), (4, 5, 0, 512),
                (4, 5, 512, 1024), (5, 6, 0, 1024), (6, 7, 0, 1024))


def _params(sem=None, vmem=VMEM_LIMIT):
    return pltpu.CompilerParams(dimension_semantics=sem, vmem_limit_bytes=vmem)


def _dot(a, b):
    return jnp.dot(a, b, preferred_element_type=F32)


def _dot_nt(a, b):
    return lax.dot_general(a, b, (((1,), (1,)), ((), ())), preferred_element_type=F32)


def _dot_tn(a, b):
    return lax.dot_general(a, b, (((0,), (0,)), ((), ())), preferred_element_type=F32)


def _ln_hat(x):
    mu = jnp.mean(x, axis=-1, keepdims=True)
    xc = x - mu
    var = jnp.mean(xc * xc, axis=-1, keepdims=True)
    rstd = lax.rsqrt(var + NORM_EPS)
    return xc * rstd, rstd


def _ln_bwd_rows(dxh, xh, rstd):
    return rstd * (dxh - jnp.mean(dxh, axis=-1, keepdims=True) - xh * jnp.mean(dxh * xh, axis=-1, keepdims=True))


def _rms_hat(x, width):
    ms = jnp.sum(x * x, axis=-1, keepdims=True) * (1.0 / width)
    r = lax.rsqrt(ms + NORM_EPS)
    return x * r, r


def _rms_bwd(u, xh, r, width):
    return r * (u - xh * (jnp.sum(u * xh, axis=-1, keepdims=True) * (1.0 / width)))


def _colsum(v):
    return jnp.sum(v, axis=0, keepdims=True)


def _rope_tables(cos, sin, consts):
    return cos, sin * consts[2:3, :], -sin * consts[1:2, :]


def _rope(x, tables, half, inverse=False):
    c, s_up, s_dn = tables
    if inverse:
        s_up, s_dn = -s_up, -s_dn
    return x * c + pltpu.roll(x, half, 1) * s_up + pltpu.roll(x, LANES - half, 1) * s_dn


def _ln_fwd(x, g, b, pos, rope_a, rope_b, tm=512, ride=None):
    t, d = x.shape
    n_in = len(ride.args) if ride else 0
    n_out = len(ride.out_shapes) if ride else 0
    steps = t // tm

    def body(x_ref, g_ref, b_ref, pos_ref, ra_ref, rb_ref, *rest):
        h_ref, trig_ref = rest[n_in], rest[n_in + 1]
        if ride:
            i = pl.program_id(0)
            ride.run(i, steps, rest[:n_in], rest[n_in + 2:n_in + 2 + n_out], rest[n_in + 2 + n_out:])
        xh, _ = _ln_hat(x_ref[...])
        h_ref[...] = (xh * g_ref[...] + b_ref[...]).astype(BF16)
        for j, consts in enumerate((ra_ref, rb_ref)):
            ang = pos_ref[...] * consts[0:1, :]
            trig_ref[:, 2 * j * LANES:(2 * j + 1) * LANES] = jnp.cos(ang)
            trig_ref[:, (2 * j + 1) * LANES:(2 * j + 2) * LANES] = jnp.sin(ang)

    row = pl.BlockSpec((1, d), lambda i: (0, 0))
    tile = pl.BlockSpec((tm, d), lambda i: (i, 0))
    consts = pl.BlockSpec((8, LANES), lambda i: (0, 0))
    trig_tile = pl.BlockSpec((tm, 4 * LANES), lambda i: (i, 0))
    in_specs = [tile, row, row, pl.BlockSpec((tm, 1), lambda i: (i, 0)), consts, consts]
    shapes = (jax.ShapeDtypeStruct((t, d), BF16), jax.ShapeDtypeStruct((t, 4 * LANES), F32))
    if not ride:
        return pl.pallas_call(
            body, name="ln_fwd", grid=(steps,), out_shape=shapes, in_specs=in_specs, out_specs=(tile, trig_tile),
            compiler_params=_params(("parallel",)),
        )(x, g, b, pos, rope_a, rope_b)
    return pl.pallas_call(
        body, name="ln_fwd", grid=(steps,),
        out_shape=(*shapes, *ride.out_shapes),
        in_specs=in_specs + ride.in_specs, out_specs=(tile, trig_tile) + (ANY,) * n_out,
        scratch_shapes=ride.scratch(),
        compiler_params=_params(("arbitrary",)),
    )(x, g, b, pos, rope_a, rope_b, *ride.args)


class _Ride:
    def __init__(self, args, out_shapes, sem_counts, plan, in_specs=None):
        self.args, self.out_shapes, self.plan = list(args), list(out_shapes), plan
        self.sem_counts = sem_counts
        self.in_specs = in_specs or [ANY] * len(self.args)

    def scratch(self):
        return [pltpu.SemaphoreType.DMA((n,)) for n in self.sem_counts]

    def run(self, step, total, in_refs, out_refs, sems):
        count = len(self.plan(in_refs, out_refs, *sems))
        at = [(k * (total - 1)) // (count - 1) for k in range(count)]
        for when in sorted(set(at)):
            @pl.when(step == when)
            def _(when=when):
                stages = self.plan(in_refs, out_refs, *sems)
                for k in range(count):
                    if at[k] == when:
                        stages[k]()


def _mm(a, b, out_dtype, tm, tn, tk, name, mode="nn", ride=None):
    if mode == "tn":
        k, m = a.shape
    else:
        m, k = a.shape
    n = b.shape[0] if mode == "nt" else b.shape[1]
    nk = k // tk
    nj, ni = n // tn, m // tm
    n_in = len(ride.args) if ride else 0
    n_out = len(ride.out_shapes) if ride else 0

    def body(a_ref, b_ref, *rest):
        o_ref = rest[n_in]
        acc_ref = rest[n_in + 1 + n_out]
        if ride:
            j, i, kk = pl.program_id(0), pl.program_id(1), pl.program_id(2)
            ride.run((j * ni + i) * nk + kk, nj * ni * nk, rest[:n_in], rest[n_in + 1:n_in + 1 + n_out],
                     rest[n_in + 2 + n_out:])
        av = a_ref[...].astype(BF16)
        bv = b_ref[...].astype(BF16)
        part = _dot_tn(av, bv) if mode == "tn" else _dot_nt(av, bv) if mode == "nt" else _dot(av, bv)
        if nk == 1:
            o_ref[...] = part.astype(out_dtype)
        else:
            kk = pl.program_id(2)

            @pl.when(kk == 0)
            def _():
                acc_ref[...] = part

            @pl.when(kk > 0)
            def _():
                acc_ref[...] += part

            @pl.when(kk == nk - 1)
            def _():
                o_ref[...] = acc_ref[...].astype(out_dtype)

    a_spec = (pl.BlockSpec((tk, tm), lambda j, i, kk: (kk, i)) if mode == "tn"
              else pl.BlockSpec((tm, tk), lambda j, i, kk: (i, kk)))
    b_spec = (pl.BlockSpec((tn, tk), lambda j, i, kk: (j, kk)) if mode == "nt"
              else pl.BlockSpec((tk, tn), lambda j, i, kk: (kk, j)))
    o_spec = pl.BlockSpec((tm, tn), lambda j, i, kk: (i, j))
    o_shape = jax.ShapeDtypeStruct((m, n), out_dtype)
    if not ride:
        return pl.pallas_call(
            body, name=name, grid=(nj, ni, nk), out_shape=o_shape, in_specs=[a_spec, b_spec], out_specs=o_spec,
            scratch_shapes=[pltpu.VMEM((tm, tn), F32)],
            compiler_params=_params(("parallel", "parallel", "arbitrary")),
        )(a, b)
    return pl.pallas_call(
        body, name=name, grid=(nj, ni, nk),
        out_shape=(o_shape, *ride.out_shapes),
        in_specs=[a_spec, b_spec] + ride.in_specs,
        out_specs=(o_spec,) + (ANY,) * n_out,
        scratch_shapes=[pltpu.VMEM((tm, tn), F32)] + ride.scratch(),
        compiler_params=_params(("arbitrary", "arbitrary", "arbitrary")),
    )(a, b, *ride.args)


def _prep(proj, trig, w_uq, w_ukv, g_cq, g_ckv, rope_a, rope_b, scales, tm=256):
    t = proj.shape[0]
    sc_a, sc_b, sc_m = (s * LOG2E for s in scales)

    def body(aq_ref, ak_ref, av_ref, bs_ref, mq_ref, trig_ref, wuq_ref, wukv_ref, gcq_ref, gckv_ref,
             ra_ref, rb_ref, qa_ref, ka_ref, va_ref, qb_ref, kb_ref, vb_ref, qm_ref, cqn_ref, ckvn_ref):
        ta = _rope_tables(trig_ref[:, 0:LANES], trig_ref[:, LANES:2 * LANES], ra_ref[...])
        tb = _rope_tables(trig_ref[:, 2 * LANES:3 * LANES], trig_ref[:, 3 * LANES:4 * LANES], rb_ref[...])
        for j in range(A_WIDTH // LANES):
            sl = slice(j * LANES, (j + 1) * LANES)
            qa_ref[:, sl] = (_rope(aq_ref[:, sl], ta, 8) * sc_a).astype(BF16)
            ka_ref[:, sl] = _rope(ak_ref[:, sl], ta, 8).astype(BF16)
        va_ref[...] = av_ref[...].astype(BF16)
        qm_ref[...] = (mq_ref[...] * sc_m).astype(BF16)

        cq_hat, _ = _rms_hat(bs_ref[:, 0:MLA_Q_RANK], MLA_Q_RANK)
        cqn = (cq_hat * gcq_ref[...]).astype(BF16)
        cqn_ref[...] = cqn
        ckv_hat, _ = _rms_hat(bs_ref[:, MLA_Q_RANK:MLA_Q_RANK + MLA_KV_RANK], MLA_KV_RANK)
        ckvn = (ckv_hat * gckv_ref[...]).astype(BF16)
        ckvn_ref[...] = ckvn
        qfull = _dot_nt(cqn, wuq_ref[...])
        kv = _dot(ckvn, wukv_ref[...])
        kr = _rope(bs_ref[:, 384:512], tb, 16)
        lane = lax.broadcasted_iota(jnp.int32, (1, LANES), 1)
        low = lane < 64
        for h in range(MLA_HEADS):
            sl = slice(h * LANES, (h + 1) * LANES)
            qb_ref[:, sl] = (_rope(qfull[:, sl], tb, 16) * sc_b).astype(BF16)
            kb_ref[:, sl] = jnp.where(low, kv[:, sl], kr).astype(BF16)
            vb_ref[:, sl] = jnp.where(low, 0.0, kv[:, sl]).astype(BF16)

    def col(width, idx):
        return pl.BlockSpec((tm, width), lambda i: (i, idx))

    def full(shape):
        return pl.BlockSpec(shape, lambda i: (0, 0))

    wide = jax.ShapeDtypeStruct((t, 1024), BF16)
    return pl.pallas_call(
        body, name="prep", grid=(t // tm,),
        out_shape=(wide, wide, wide, wide, wide, wide,
                   jax.ShapeDtypeStruct((t, MEM_WIDTH), BF16),
                   jax.ShapeDtypeStruct((t, MLA_Q_RANK), BF16),
                   jax.ShapeDtypeStruct((t, MLA_KV_RANK), BF16)),
        in_specs=[col(1024, 0), col(1024, 1), col(1024, 2), col(512, COL_CQ // 512), col(512, COL_MQ // 512),
                  pl.BlockSpec((tm, 4 * LANES), lambda i: (i, 0)),
                  full((1024, MLA_Q_RANK)), full((MLA_KV_RANK, 1024)),
                  full((1, MLA_Q_RANK)), full((1, MLA_KV_RANK)), full((8, LANES)), full((8, LANES))],
        out_specs=(col(1024, 0),) * 6 + (col(MEM_WIDTH, 0), col(MLA_Q_RANK, 0), col(MLA_KV_RANK, 0)),
        compiler_params=_params(("parallel",)),
    )(proj, proj, proj, proj, proj, trig, w_uq, w_ukv, g_cq, g_ckv, rope_a, rope_b)


def _attn_fwd(q, k, v, *, nb, s, sk, heads, hpb, voff, bq, name):
    nq = s // bq
    width = hpb * LANES
    vblk = voff // hpb

    def body(q_ref, k_ref, v_ref, o_ref, lse_ref):
        for h in range(hpb):
            sl = slice(h * LANES, (h + 1) * LANES)
            sc = _dot_nt(q_ref[:, sl], k_ref[:, sl])
            m = jnp.max(sc, axis=1, keepdims=True)
            p = jnp.exp2(sc - m)
            l = jnp.sum(p, axis=1, keepdims=True)
            o_ref[:, sl] = _dot(p.astype(BF16), v_ref[:, sl]) / l
            lse_ref[:, sl] = jnp.broadcast_to(m + jnp.log(l) * LOG2E, (bq, LANES))

    out = jax.ShapeDtypeStruct((nb * s, heads * LANES), F32)
    ospec = pl.BlockSpec((bq, width), lambda b, i, g: (b * nq + i, g))
    return pl.pallas_call(
        body, name=name, grid=(nb, nq, heads // hpb),
        out_shape=(out, out),
        in_specs=[ospec, pl.BlockSpec((sk, width), lambda b, i, g: (b, g)),
                  pl.BlockSpec((sk, width), lambda b, i, g: (b, vblk + g))],
        out_specs=(ospec, ospec),
        compiler_params=_params(("parallel", "parallel", "parallel")),
    )(q, k, v)


def _attn_bwd(q, k, v, o, do, lse, *, nb, s, sk, heads, hpb, voff, scale, bq, name):
    nq = s // bq
    width = hpb * LANES
    vblk = voff // hpb

    def body(q_ref, k_ref, v_ref, o_ref, do_ref, lse_ref, dq_ref, dk_ref, dv_ref, dk_acc, dv_acc):
        i = pl.program_id(2)

        @pl.when(i == 0)
        def _():
            dk_acc[...] = jnp.zeros_like(dk_acc)
            dv_acc[...] = jnp.zeros_like(dv_acc)

        for h in range(hpb):
            sl = slice(h * LANES, (h + 1) * LANES)
            qh = q_ref[:, sl]
            kk = k_ref[:, sl]
            doh = do_ref[:, sl]
            delta = jnp.sum(doh.astype(F32) * o_ref[:, sl], axis=1, keepdims=True)
            p = jnp.exp2(_dot_nt(qh, kk) - lse_ref[:, h * LANES:h * LANES + 1])
            ds = (p * (_dot_nt(doh, v_ref[:, sl]) - delta)).astype(BF16)
            dq_ref[:, sl] = (_dot(ds, kk) * scale).astype(BF16)
            dk_acc[:, sl] += _dot_tn(ds, qh)
            dv_acc[:, sl] += _dot_tn(p.astype(BF16), doh)

        @pl.when(i == nq - 1)
        def _():
            dk_ref[...] = (dk_acc[...] * LN2).astype(BF16)
            dv_ref[...] = dv_acc[...].astype(BF16)

    qspec = pl.BlockSpec((bq, width), lambda b, g, i: (b * nq + i, g))
    kv_spec = pl.BlockSpec((sk, width), lambda b, g, i: (b, g))
    dq_shape = jax.ShapeDtypeStruct((nb * s, heads * LANES), BF16)
    dkv_shape = jax.ShapeDtypeStruct((nb * sk, heads * LANES), BF16)
    return pl.pallas_call(
        body, name=name, grid=(nb, heads // hpb, nq),
        out_shape=(dq_shape, dkv_shape, dkv_shape),
        in_specs=[qspec, kv_spec, pl.BlockSpec((sk, width), lambda b, g, i: (b, vblk + g)), qspec, qspec, qspec],
        out_specs=(qspec, kv_spec, kv_spec),
        scratch_shapes=[pltpu.VMEM((sk, width), F32), pltpu.VMEM((sk, width), F32)],
        compiler_params=_params(("parallel", "parallel", "arbitrary")),
    )(q, k, v, o, do, lse)


BAND_Q = 128
BAND_WIN = 256


def _band_start(i, s):
    return min(max(i * BAND_Q - 64, 0), s - BAND_WIN)


def _to_pattern_order(src_ref, dst_ref, stage_ref, s, d):
    length = s // d
    stage_ref[...] = src_ref[...].astype(F32)
    for r in range(d):
        dst_ref[r * length:(r + 1) * length, :] = stage_ref[pl.ds(r, length, stride=d), :].astype(dst_ref.dtype)


def _dilated_fwd(q, k, v, bias, bias_index, *, nb, s, name):
    nblk = s // BAND_Q
    npat = len(DILATED)

    def body(q_ref, k_ref, v_ref, bias_ref, o_ref, lse_ref, *rest):
        ordered = rest[:3 * (npat - 1)]
        stage_ref, op_ref, lp_ref, on_ref, ln_ref = rest[3 * (npat - 1):]
        lane = lax.broadcasted_iota(jnp.int32, (1, LANES), 1)
        first = lane < 64
        for p, (_, d) in enumerate(DILATED):
            if d == 1:
                qs, ks, vs = q_ref, k_ref, v_ref
            else:
                qs, ks, vs = ordered[3 * (p - 1):3 * p]
                for src, dst in ((q_ref, qs), (k_ref, ks), (v_ref, vs)):
                    _to_pattern_order(src, dst, stage_ref, s, d)
            for i in range(nblk):
                u0 = i * BAND_Q
                st = _band_start(i, s)
                qi = qs[u0:u0 + BAND_Q, :]
                kw = ks[st:st + BAND_WIN, :]
                vw = vs[st:st + BAND_WIN, :]
                zero = jnp.zeros_like(qi)
                q2 = jnp.concatenate([jnp.where(first, qi, zero), jnp.where(first, zero, qi)], axis=0)
                sc = _dot_nt(q2, kw)
                b = bias_ref[bias_index[p][i]]
                halves = []
                for h in range(2):
                    sh = sc[h * BAND_Q:(h + 1) * BAND_Q] + b
                    m = jnp.max(sh, axis=1, keepdims=True)
                    pr = jnp.exp2(sh - m)
                    l = jnp.sum(pr, axis=1, keepdims=True)
                    halves.append((pr.astype(BF16), l, m + jnp.log(l) * LOG2E))
                o2 = _dot(jnp.concatenate([halves[0][0], halves[1][0]], axis=0), vw)
                o_blk = jnp.where(first, o2[:BAND_Q] / halves[0][1], o2[BAND_Q:] / halves[1][1])
                lse_blk = jnp.where(first, jnp.broadcast_to(halves[0][2], (BAND_Q, LANES)),
                                    jnp.broadcast_to(halves[1][2], (BAND_Q, LANES)))
                op_ref[p, u0:u0 + BAND_Q, :] = o_blk
                lp_ref[p, u0:u0 + BAND_Q, :] = lse_blk
            if d > 1:
                length = s // d
                for r in range(d):
                    on_ref.at[p - 1][pl.ds(r, length, stride=d), :] = op_ref[p, r * length:(r + 1) * length, :]
                    ln_ref.at[p - 1][pl.ds(r, length, stride=d), :] = lp_ref[p, r * length:(r + 1) * length, :]
        lses = [lp_ref[0]] + [ln_ref[p] for p in range(npat - 1)]
        outs = [op_ref[0]] + [on_ref[p] for p in range(npat - 1)]
        m = functools.reduce(jnp.maximum, lses)
        ws = [jnp.exp2(l - m) for l in lses]
        den = functools.reduce(lambda a, c: a + c, ws)
        o_ref[...] = functools.reduce(lambda a, c: a + c, [w * o for w, o in zip(ws, outs)]) / den
        lse_ref[...] = m + jnp.log(den) * LOG2E

    blk = pl.BlockSpec((s, LANES), lambda b, g: (b, g))
    out = jax.ShapeDtypeStruct((nb * s, A_WIDTH), F32)
    copy = jax.ShapeDtypeStruct((nb * s, A_WIDTH), BF16)
    n_copies = 3 * (npat - 1)
    res = pl.pallas_call(
        body, name=name, grid=(nb, A_WIDTH // LANES),
        out_shape=(out, out) + (copy,) * n_copies,
        in_specs=[blk, blk, blk, pl.BlockSpec(bias.shape, lambda b, g: (0, 0, 0))],
        out_specs=(blk, blk) + (blk,) * n_copies,
        scratch_shapes=[pltpu.VMEM((s, LANES), F32), pltpu.VMEM((npat, s, LANES), F32),
                        pltpu.VMEM((npat, s, LANES), F32), pltpu.VMEM((npat - 1, s, LANES), F32),
                        pltpu.VMEM((npat - 1, s, LANES), F32)],
        compiler_params=_params(("parallel", "parallel")),
    )(q, k, v, bias)
    return res[0], res[1], res[2:]


def _dilated_bwd(q, k, v, ordered, o, do, lse, bias, bias_index, *, nb, s, scale, name):
    nblk = s // BAND_Q
    npat = len(DILATED)
    n_copies = 3 * (npat - 1)

    def body(q_ref, k_ref, v_ref, *rest):
        ordered_refs = rest[:n_copies]
        (o_ref, do_ref, lse_ref, bias_ref, dq_out, dk_out, dv_out, stage_ref, rs_ref, dop_ref, rsp_ref,
         dqp_ref, dkp_ref, dvp_ref, dq_ref, dk_ref, dv_ref) = rest[n_copies:]
        lane = lax.broadcasted_iota(jnp.int32, (1, LANES), 1)
        first = lane < 64
        prod = do_ref[...].astype(F32) * o_ref[...]
        d0 = jnp.sum(jnp.where(first, prod, 0.0), axis=1, keepdims=True)
        d1 = jnp.sum(jnp.where(first, 0.0, prod), axis=1, keepdims=True)
        delta = jnp.where(first, jnp.broadcast_to(d0, (s, LANES)), jnp.broadcast_to(d1, (s, LANES)))
        rs_ref[...] = jnp.where((lane & 32) == 0, lse_ref[...], delta)
        for p, (_, d) in enumerate(DILATED):
            length = s // d
            if d == 1:
                qs, ks, vs, dos, rss = q_ref, k_ref, v_ref, do_ref, rs_ref
                dqs, dks, dvs = dq_ref, dk_ref, dv_ref
            else:
                for src, dst in ((do_ref, dop_ref), (rs_ref, rsp_ref)):
                    _to_pattern_order(src, dst, stage_ref, s, d)
                qs, ks, vs = ordered_refs[3 * (p - 1):3 * p]
                dos, rss = dop_ref, rsp_ref
                dqs, dks, dvs = dqp_ref, dkp_ref, dvp_ref
            dks[...] = jnp.zeros((s, LANES), F32)
            dvs[...] = jnp.zeros((s, LANES), F32)
            for i in range(nblk):
                u0 = i * BAND_Q
                st = _band_start(i, s)
                qi = qs[u0:u0 + BAND_Q, :]
                doi = dos[u0:u0 + BAND_Q, :]
                kw = ks[st:st + BAND_WIN, :]
                vw = vs[st:st + BAND_WIN, :]
                zero = jnp.zeros_like(qi)
                q2 = jnp.concatenate([jnp.where(first, qi, zero), jnp.where(first, zero, qi)], axis=0)
                do2 = jnp.concatenate([jnp.where(first, doi, zero), jnp.where(first, zero, doi)], axis=0)
                sc = _dot_nt(q2, kw)
                dp = _dot_nt(do2, vw)
                b = bias_ref[bias_index[p][i]]
                rs_i = rss[u0:u0 + BAND_Q, :]
                ps, dss = [], []
                for h in range(2):
                    rows = slice(h * BAND_Q, (h + 1) * BAND_Q)
                    pr = jnp.exp2(sc[rows] + b - rs_i[:, 64 * h:64 * h + 1])
                    ps.append(pr.astype(BF16))
                    dss.append((pr * (dp[rows] - rs_i[:, 64 * h + 32:64 * h + 33])).astype(BF16))
                p2 = jnp.concatenate(ps, axis=0)
                ds2 = jnp.concatenate(dss, axis=0)
                dq2 = _dot(ds2, kw)
                dqs[u0:u0 + BAND_Q, :] = jnp.where(first, dq2[:BAND_Q], dq2[BAND_Q:]) * scale
                dks[st:st + BAND_WIN, :] += _dot_tn(ds2, q2)
                dvs[st:st + BAND_WIN, :] += _dot_tn(p2, do2)
            if d > 1:
                for dst, src in ((dq_ref, dqp_ref), (dk_ref, dkp_ref), (dv_ref, dvp_ref)):
                    for r in range(d):
                        dst[pl.ds(r, length, stride=d), :] += src[r * length:(r + 1) * length, :]
        dq_out[...] = dq_ref[...].astype(BF16)
        dk_out[...] = (dk_ref[...] * LN2).astype(BF16)
        dv_out[...] = dv_ref[...].astype(BF16)

    blk = pl.BlockSpec((s, LANES), lambda b, g: (b, g))
    out = jax.ShapeDtypeStruct((nb * s, A_WIDTH), BF16)
    f32_buf = pltpu.VMEM((s, LANES), F32)
    bf_buf = pltpu.VMEM((s, LANES), BF16)
    return pl.pallas_call(
        body, name=name, grid=(nb, A_WIDTH // LANES),
        out_shape=(out, out, out),
        in_specs=[blk] * (6 + n_copies) + [pl.BlockSpec(bias.shape, lambda b, g: (0, 0, 0))],
        out_specs=(blk, blk, blk),
        scratch_shapes=[f32_buf, f32_buf, bf_buf] + [f32_buf] * 7,
        compiler_params=_params(("parallel", "parallel")),
    )(q, k, v, *ordered, o, do, lse, bias)


def _post(x, ya, ybp, ym, proj, target, w_out, g_emb, b_emb, g_a, g_b, g_m, g_post, b_post, tm=256):
    t = x.shape[0]

    def body(x_ref, ya_ref, yb_ref, ym_ref, ga_ref, gb_ref, gm_ref, tg_ref, wo_ref,
             ge_ref, be_ref, goa_ref, gob_ref, gom_ref, gp_ref, bp_ref,
             y_ref, dz_ref, doa_ref, dob_ref, dom_ref, dga_ref, dgb_ref, dgm_ref,
             loss_ref, dgp_ref, dbp_ref, dgoa_ref, dgob_ref, dgom_ref):
        i = pl.program_id(0)

        @pl.when(i == 0)
        def _():
            for r in (loss_ref, dgp_ref, dbp_ref, dgoa_ref, dgob_ref, dgom_ref):
                r[...] = jnp.zeros_like(r)

        lane = lax.broadcasted_iota(jnp.int32, (1, LANES), 1)
        low = lane < 64
        xh0, _ = _ln_hat(x_ref[...])
        h = xh0 * ge_ref[...] + be_ref[...]

        ybp_v = yb_ref[...]
        yb = jnp.concatenate(
            [jnp.where(low, pltpu.roll(ybp_v[:, 2 * j * LANES:(2 * j + 1) * LANES], 64, 1),
                       ybp_v[:, (2 * j + 1) * LANES:(2 * j + 2) * LANES]) for j in range(4)], axis=1)

        def gated(raw, gate, gain, width):
            xh, r = _rms_hat(raw, width)
            n = xh * gain
            sg = 1.0 / (1.0 + jnp.exp(-gate))
            return xh, r, n, sg, n * (gate * sg)

        gate_a, gate_b, gate_m = ga_ref[...], gb_ref[...], gm_ref[...]
        xh_a, r_a, n_a, sg_a, y_a = gated(ya_ref[...], gate_a, goa_ref[...], A_WIDTH)
        xh_b, r_b, n_b, sg_b, y_b = gated(yb, gate_b, gob_ref[...], 512)
        xh_m, r_m, n_m, sg_m, y_m = gated(ym_ref[...], gate_m, gom_ref[...], 512)
        y = jnp.concatenate([y_a, y_b, y_m], axis=1).astype(BF16)
        y_ref[...] = y
        z = DEEPNORM_ALPHA * h + _dot(y, wo_ref[...])
        zh, rstd = _ln_hat(z)
        err = zh * gp_ref[...] + bp_ref[...] - tg_ref[...]
        rows = jnp.sum(err * err, axis=1, keepdims=True)
        loss_ref[...] += jnp.broadcast_to(jnp.sum(rows, axis=0, keepdims=True) * (0.5 / D_MODEL), (1, LANES))
        dout = err * (1.0 / D_MODEL)
        dgp_ref[...] += _colsum(dout * zh)
        dbp_ref[...] += _colsum(dout)
        dz = _ln_bwd_rows(dout * gp_ref[...], zh, rstd)
        dz_ref[...] = dz
        dy = _dot_nt(dz.astype(BF16), wo_ref[...])

        def gated_bwd(dyg, xh, r, n, sg, gate, gain, width, dgain_ref):
            dn = dyg * (gate * sg)
            dgate = dyg * n * (sg * (1.0 + gate * (1.0 - sg)))
            dgain_ref[...] += _colsum(dn * xh)
            return _rms_bwd(dn * gain, xh, r, width), dgate

        dya, dgate_a = gated_bwd(dy[:, 0:1024], xh_a, r_a, n_a, sg_a, gate_a, goa_ref[...], A_WIDTH, dgoa_ref)
        dyb, dgate_b = gated_bwd(dy[:, 1024:1536], xh_b, r_b, n_b, sg_b, gate_b, gob_ref[...], 512, dgob_ref)
        dym, dgate_m = gated_bwd(dy[:, 1536:2048], xh_m, r_m, n_m, sg_m, gate_m, gom_ref[...], 512, dgom_ref)
        doa_ref[...] = dya.astype(BF16)
        dom_ref[...] = dym.astype(BF16)
        dga_ref[...] = dgate_a.astype(BF16)
        dgb_ref[...] = dgate_b.astype(BF16)
        dgm_ref[...] = dgate_m.astype(BF16)
        for j in range(4):
            blk = dyb[:, j * LANES:(j + 1) * LANES]
            dob_ref[:, 2 * j * LANES:(2 * j + 1) * LANES] = jnp.where(low, 0.0, pltpu.roll(blk, 64, 1)).astype(BF16)
            dob_ref[:, (2 * j + 1) * LANES:(2 * j + 2) * LANES] = jnp.where(low, 0.0, blk).astype(BF16)

    def col(width, idx):
        return pl.BlockSpec((tm, width), lambda i: (i, idx))

    def full(shape):
        return pl.BlockSpec(shape, lambda i: (0, 0))

    def acc(width):
        return jax.ShapeDtypeStruct((1, width), F32)

    return pl.pallas_call(
        body, name="post", grid=(t // tm,),
        out_shape=(jax.ShapeDtypeStruct((t, 2048), BF16), jax.ShapeDtypeStruct((t, 1024), F32),
                   jax.ShapeDtypeStruct((t, 1024), BF16), jax.ShapeDtypeStruct((t, 1024), BF16),
                   jax.ShapeDtypeStruct((t, 512), BF16),
                   jax.ShapeDtypeStruct((t, 1024), BF16), jax.ShapeDtypeStruct((t, 512), BF16),
                   jax.ShapeDtypeStruct((t, 512), BF16),
                   acc(LANES), acc(1024), acc(1024), acc(1024), acc(512), acc(512)),
        in_specs=[col(1024, 0), col(1024, 0), col(1024, 0), col(512, 0),
                  col(1024, 3), col(512, COL_BG // 512), col(512, COL_MG // 512), col(1024, 0),
                  full((2048, 1024)),
                  full((1, 1024)), full((1, 1024)), full((1, 1024)), full((1, 512)), full((1, 512)),
                  full((1, 1024)), full((1, 1024))],
        out_specs=(col(2048, 0), col(1024, 0), col(1024, 0), col(1024, 0), col(512, 0),
                   col(1024, 0), col(512, 0), col(512, 0),
                   full((1, LANES)), full((1, 1024)), full((1, 1024)), full((1, 1024)), full((1, 512)),
                   full((1, 512))),
        compiler_params=_params(("arbitrary",)),
    )(x, ya, ybp, ym, proj, proj, proj, target, w_out, g_emb, b_emb, g_a, g_b, g_m, g_post, b_post)


def _prep_bwd(dqa, dka, dva, dqb, dkb, dvb, dqm, dga, dgb, dgm, proj, trig, w_uq, w_ukv, g_cq, g_ckv,
              rope_a, rope_b, tm=256):
    t = proj.shape[0]

    def body(dqa_ref, dka_ref, dva_ref, dqb_ref, dkb_ref, dvb_ref, dqm_ref, dga_ref, dgb_ref, dgm_ref,
             bs_ref, trig_ref, wuq_ref, wukv_ref, gcq_ref, gckv_ref, ra_ref, rb_ref,
             dproj_ref, dqf_ref, dkv_ref, dgcq_ref, dgckv_ref):
        i = pl.program_id(0)

        @pl.when(i == 0)
        def _():
            dgcq_ref[...] = jnp.zeros_like(dgcq_ref)
            dgckv_ref[...] = jnp.zeros_like(dgckv_ref)

        ta = _rope_tables(trig_ref[:, 0:LANES], trig_ref[:, LANES:2 * LANES], ra_ref[...])
        tb = _rope_tables(trig_ref[:, 2 * LANES:3 * LANES], trig_ref[:, 3 * LANES:4 * LANES], rb_ref[...])
        for j in range(A_WIDTH // LANES):
            sl = slice(j * LANES, (j + 1) * LANES)
            dproj_ref[:, j * LANES:(j + 1) * LANES] = (
                _rope(dqa_ref[:, sl].astype(F32), ta, 8, inverse=True).astype(BF16))
            dproj_ref[:, 1024 + j * LANES:1024 + (j + 1) * LANES] = (
                _rope(dka_ref[:, sl].astype(F32), ta, 8, inverse=True).astype(BF16))
        dproj_ref[:, 2048:3072] = dva_ref[...]
        dproj_ref[:, 3072:4096] = dga_ref[...]

        lane = lax.broadcasted_iota(jnp.int32, (1, LANES), 1)
        low = lane < 64
        rope_lanes = (lane >= 64) & (lane < 96)
        dkr = jnp.zeros((tm, LANES), F32)
        for h in range(MLA_HEADS):
            sl = slice(h * LANES, (h + 1) * LANES)
            dqf_ref[:, sl] = _rope(dqb_ref[:, sl].astype(F32), tb, 16, inverse=True).astype(BF16)
            dk_h = dkb_ref[:, sl]
            dkv_ref[:, sl] = jnp.where(low, dk_h, dvb_ref[:, sl])
            dkr = dkr + jnp.where(rope_lanes, dk_h.astype(F32), 0.0)
        dkr = _rope(dkr, tb, 16, inverse=True)

        cq_hat, r_q = _rms_hat(bs_ref[:, 0:MLA_Q_RANK], MLA_Q_RANK)
        dcqn = _dot(dqf_ref[...], wuq_ref[...])
        dgcq_ref[...] += _colsum(dcqn * cq_hat)
        dproj_ref[:, COL_CQ:COL_CQ + 256] = _rms_bwd(dcqn * gcq_ref[...], cq_hat, r_q, MLA_Q_RANK).astype(BF16)
        ckv_hat, r_kv = _rms_hat(bs_ref[:, MLA_Q_RANK:MLA_Q_RANK + MLA_KV_RANK], MLA_KV_RANK)
        dckvn = _dot_nt(dkv_ref[...], wukv_ref[...])
        dgckv_ref[...] += _colsum(dckvn * ckv_hat)
        dproj_ref[:, COL_CQ + 256:COL_CQ + 384] = (
            _rms_bwd(dckvn * gckv_ref[...], ckv_hat, r_kv, MLA_KV_RANK).astype(BF16))
        dproj_ref[:, COL_CQ + 384:COL_CQ + 512] = dkr.astype(BF16)
        dproj_ref[:, COL_BG:COL_BG + 512] = dgb_ref[...]
        dproj_ref[:, COL_MQ:COL_MQ + 512] = dqm_ref[...]
        dproj_ref[:, COL_MG:COL_MG + 512] = dgm_ref[...]

    def col(width, idx):
        return pl.BlockSpec((tm, width), lambda i: (i, idx))

    def full(shape):
        return pl.BlockSpec(shape, lambda i: (0, 0))

    return pl.pallas_call(
        body, name="prep_bwd", grid=(t // tm,),
        out_shape=(jax.ShapeDtypeStruct((t, PROJ_W), BF16), jax.ShapeDtypeStruct((t, 1024), BF16),
                   jax.ShapeDtypeStruct((t, 1024), BF16),
                   jax.ShapeDtypeStruct((1, MLA_Q_RANK), F32), jax.ShapeDtypeStruct((1, MLA_KV_RANK), F32)),
        in_specs=[col(1024, 0)] * 6 + [col(512, 0), col(1024, 0), col(512, 0), col(512, 0),
                  col(512, COL_CQ // 512), pl.BlockSpec((tm, 4 * LANES), lambda i: (i, 0)),
                  full((1024, MLA_Q_RANK)), full((MLA_KV_RANK, 1024)),
                  full((1, MLA_Q_RANK)), full((1, MLA_KV_RANK)), full((8, LANES)), full((8, LANES))],
        out_specs=(col(PROJ_W, 0), col(1024, 0), col(1024, 0), full((1, MLA_Q_RANK)), full((1, MLA_KV_RANK))),
        compiler_params=_params(("arbitrary",)),
    )(dqa, dka, dva, dqb, dkb, dvb, dqm, dga, dgb, dgm, proj, trig, w_uq, w_ukv, g_cq, g_ckv, rope_a, rope_b)


def _adamw_math(gv, w, m, v):
    m_new = ADAM_B1 * m + (1.0 - ADAM_B1) * gv
    v_new = ADAM_B2 * v + (1.0 - ADAM_B2) * (gv * gv)
    m_hat = m_new / (1.0 - ADAM_B1 ** ADAM_STEP)
    v_hat = v_new / (1.0 - ADAM_B2 ** ADAM_STEP)
    return -ADAM_LR * (m_hat / (jnp.sqrt(v_hat) + ADAM_EPS) + ADAM_WD * w), m_new, v_new


def _adamw(g, w, m, v, tr, name):
    r, cols = w.shape

    def body(g_ref, w_ref, m_ref, v_ref, go_ref, d_ref, nm_ref, nv_ref):
        gv = g_ref[...]
        go_ref[...] = gv
        d_ref[...], nm_ref[...], nv_ref[...] = _adamw_math(gv, w_ref[...], m_ref[...], v_ref[...])

    tile = pl.BlockSpec((tr, cols), lambda i: (i, 0))
    shape = jax.ShapeDtypeStruct((r, cols), F32)
    return pl.pallas_call(
        body, name=name, grid=(r // tr,),
        out_shape=(shape,) * 4, in_specs=[tile] * 4, out_specs=(tile,) * 4,
        compiler_params=_params(("parallel",)),
    )(g, w, m, v)


def _adamw_pieces(g, w, m, v, pieces, name):
    shapes = [jax.ShapeDtypeStruct((r1 - r0, c1 - c0), F32) for r0, r1, c0, c1 in pieces]

    def body(g_ref, w_ref, m_ref, v_ref, *outs):
        gv = g_ref[...]
        results = (gv,) + _adamw_math(gv, w_ref[...], m_ref[...], v_ref[...])
        for kind, full in enumerate(results):
            for p, (r0, r1, c0, c1) in enumerate(pieces):
                outs[kind * len(pieces) + p][...] = full[r0:r1, c0:c1]

    flat = pl.pallas_call(
        body, name=name, out_shape=tuple(shapes) * 4,
        in_specs=[IN_VMEM] * 4, out_specs=tuple([IN_VMEM] * (4 * len(pieces))),
        compiler_params=_params(None),
    )(g, w, m, v)
    return [[flat[kind * len(pieces) + p] for kind in range(4)] for p in range(len(pieces))]


def _core_sum(g, recv, core, rows, tr, name, ride=None):
    cols = g.shape[2]
    nblk = rows // tr
    n_in = len(ride.args) if ride else 0
    n_out = len(ride.out_shapes) if ride else 0

    def body(c_ref, g_ref, r_ref, *rest):
        sf_ref, sb_ref = rest[n_in], rest[n_in + 1]
        if ride:
            j, i = pl.program_id(0), pl.program_id(1)
            ride.run(j * nblk + i, 4 * nblk, rest[:n_in], rest[n_in + 2:n_in + 2 + n_out],
                     rest[n_in + 2 + n_out:])
        tot = g_ref[...] + r_ref[...]
        sf_ref[...] = tot
        sb_ref[...] = tot.astype(BF16)

    half = pl.BlockSpec((None, tr, cols), lambda j, i, c_ref: (j, i, 0))
    shapes = (jax.ShapeDtypeStruct((4, rows, cols), F32), jax.ShapeDtypeStruct((4, rows, cols), BF16))
    return pl.pallas_call(
        body, name=name,
        grid_spec=pltpu.PrefetchScalarGridSpec(
            num_scalar_prefetch=1, grid=(4, nblk),
            in_specs=[pl.BlockSpec((None, tr, cols), lambda j, i, c_ref: (j, c_ref[0] * nblk + i, 0)), half]
            + (ride.in_specs if ride else []),
            out_specs=(half, half) + (ANY,) * n_out,
            scratch_shapes=ride.scratch() if ride else []),
        out_shape=shapes + tuple(ride.out_shapes if ride else ()),
        compiler_params=_params(("arbitrary", "arbitrary") if ride else ("parallel", "parallel")),
    )(core, g, recv, *(ride.args if ride else ()))


def _half_to_sibling(g4):
    def plan(in_refs, out_refs, send_sems, recv_sems):
        x, y, c = _position()
        cp = pltpu.make_async_remote_copy(
            src_ref=in_refs[0].at[:, 1 - c], dst_ref=out_refs[0], send_sem=send_sems.at[0],
            recv_sem=recv_sems.at[0], device_id=(x, y, 1 - c), device_id_type=MESH)

        def finish():
            cp.wait_recv()
            cp.wait_send()

        return cp.start, finish

    return _Ride([g4], [jax.ShapeDtypeStruct((4, g4.shape[2], 1024), F32)], (1, 1), plan)


def _gather_plan(src_ref, dst_ref, send_sems, recv_sems, local_sems):
    x, y, c = _position()
    me = 2 * x + y
    rows = src_ref.shape[1]
    cut = -(-rows // 32) * 16
    pieces = (pl.ds(0, cut), pl.ds(cut, rows - cut))
    local = pltpu.make_async_copy(src_ref, dst_ref.at[me], local_sems.at[0])

    def over_ici(sem, k, chip, t, src=None):
        where = dst_ref.at[chip, c, pieces[t]]
        return pltpu.make_async_remote_copy(
            src_ref=where if src is None else src, dst_ref=where, send_sem=send_sems.at[sem],
            recv_sem=recv_sems.at[sem], device_id=(x ^ (k >> 1), y ^ (k & 1), c), device_id_type=MESH)

    def mine_to(k, t):
        return over_ici(2 * (k - 1) + t, k, me, t, src=src_ref.at[c, pieces[t]])

    def from_neighbour(k, t):
        return over_ici(2 * (k - 1) + t, k, me ^ k, t)

    def to_sibling(k, half):
        piece = dst_ref.at[me ^ k, half]
        return pltpu.make_async_remote_copy(
            src_ref=piece, dst_ref=piece, send_sem=send_sems.at[5 + k], recv_sem=recv_sems.at[5 + k],
            device_id=(x, y, 1 - c), device_id_type=MESH)

    sends = [mine_to(2, 0), mine_to(1, 1), mine_to(2, 1), mine_to(1, 0)]
    onward = [over_ici(4, 1, me ^ 2, 0), over_ici(5, 2, me ^ 1, 1)]

    def start():
        local.start()
        for cp in sends:
            cp.start()

    def pass_on():
        from_neighbour(2, 0).wait_recv()
        onward[0].start()
        from_neighbour(1, 1).wait_recv()
        onward[1].start()

    def to_other_core():
        from_neighbour(2, 1).wait_recv()
        to_sibling(2, c).start()
        from_neighbour(1, 0).wait_recv()
        to_sibling(1, c).start()
        over_ici(4, 1, me ^ 3, 0).wait_recv()
        over_ici(5, 2, me ^ 3, 1).wait_recv()
        to_sibling(3, c).start()

    def finish():
        for k in (1, 2, 3):
            to_sibling(k, 1 - c).wait_recv()
        for cp in sends + onward + [to_sibling(k, c) for k in (1, 2, 3)]:
            cp.wait_send()
        local.wait()

    return start, pass_on, to_other_core, finish


def _gather_ride(shard):
    def plan(in_refs, out_refs, send_sems, recv_sems, local_sems):
        return _gather_plan(in_refs[0], out_refs[0], send_sems, recv_sems, local_sems)

    return _Ride([shard], [jax.ShapeDtypeStruct((4,) + shard.shape, shard.dtype)], (9, 9, 1), plan,
                 in_specs=[IN_VMEM])


def _chip_sum(sf, recv, chip, rows, tr, name):
    cols = sf.shape[2]

    def body(me_ref, sf_ref, r_ref, out_ref):
        acc = sf_ref[...]
        for k in range(3):
            acc = acc + r_ref[k].astype(F32)
        out_ref[...] = acc

    return pl.pallas_call(
        body, name=name,
        grid_spec=pltpu.PrefetchScalarGridSpec(
            num_scalar_prefetch=1, grid=(rows // tr,),
            in_specs=[pl.BlockSpec((None, tr, cols), lambda i, me_ref: (me_ref[0], i, 0)),
                      pl.BlockSpec((3, tr, cols), lambda i, me_ref: (0, i, 0))],
            out_specs=pl.BlockSpec((tr, cols), lambda i, me_ref: (i, 0))),
        out_shape=jax.ShapeDtypeStruct((rows, cols), F32),
        compiler_params=_params(("parallel",)),
    )(chip, sf, recv)


def _position():
    return lax.axis_index("x"), lax.axis_index("y"), lax.axis_index("c")


def _dh_scatter(dproj, w_in_arr_t, x, dz, g, sb_in, sb_rest, tm=1024, tk=1024):
    t, d = x.shape
    nk = dproj.shape[1] // tk
    ni = t // tm

    def body(dp_ref, w_ref, x_ref, dz_ref, g_ref, sbin_ref, sbrest_ref,
             dx_ref, dg_ref, db_ref, rin_ref, rrest_ref, acc_ref, send_sems, recv_sems):
        i = pl.program_id(0)
        kk = pl.program_id(1)
        px, py, pc = _position()
        me = 2 * px + py
        srcs = (sbin_ref, sbrest_ref)
        dsts = (rin_ref, rrest_ref)

        def copy(a, k):
            return pltpu.make_async_remote_copy(
                src_ref=srcs[a].at[me ^ k], dst_ref=dsts[a].at[k - 1],
                send_sem=send_sems.at[3 * a + k - 1], recv_sem=recv_sems.at[3 * a + k - 1],
                device_id=(px ^ (k >> 1), py ^ (k & 1), pc), device_id_type=MESH)

        pairs = [(a, k) for a in range(2) for k in (1, 2, 3)]

        @pl.when((i == 0) & (kk == 0))
        def _():
            dg_ref[...] = jnp.zeros_like(dg_ref)
            db_ref[...] = jnp.zeros_like(db_ref)
            for a, k in pairs:
                copy(a, k).start()

        part = _dot(dp_ref[...], w_ref[...])

        @pl.when(kk == 0)
        def _():
            acc_ref[...] = part

        @pl.when(kk > 0)
        def _():
            acc_ref[...] += part

        @pl.when(kk == nk - 1)
        def _():
            xh, rstd = _ln_hat(x_ref[...])
            dht = acc_ref[...] + DEEPNORM_ALPHA * dz_ref[...]
            dg_ref[...] += _colsum(dht * xh)
            db_ref[...] += _colsum(dht)
            dx_ref[...] = _ln_bwd_rows(dht * g_ref[...], xh, rstd)

        @pl.when((i == ni - 1) & (kk == nk - 1))
        def _():
            for a, k in pairs:
                copy(a, k).wait_recv()
            for a, k in pairs:
                copy(a, k).wait_send()

    tile = pl.BlockSpec((tm, d), lambda i, kk: (i, 0))
    row = pl.BlockSpec((1, d), lambda i, kk: (0, 0))
    return pl.pallas_call(
        body, name="dh_scatter", grid=(ni, nk),
        out_shape=(jax.ShapeDtypeStruct((t, d), F32), jax.ShapeDtypeStruct((1, d), F32),
                   jax.ShapeDtypeStruct((1, d), F32),
                   jax.ShapeDtypeStruct((3, HALF_IN, 1024), BF16),
                   jax.ShapeDtypeStruct((3, HALF_REST, 1024), BF16)),
        in_specs=[pl.BlockSpec((tm, tk), lambda i, kk: (i, kk)), pl.BlockSpec((tk, d), lambda i, kk: (kk, 0)),
                  tile, tile, row, ANY, ANY],
        out_specs=(tile, row, row, ANY, ANY),
        scratch_shapes=[pltpu.VMEM((tm, d), F32), pltpu.SemaphoreType.DMA((6,)), pltpu.SemaphoreType.DMA((6,))],
        compiler_params=_params(("arbitrary", "arbitrary")),
    )(dproj, w_in_arr_t, x, dz, g, sb_in, sb_rest)


def _join_halves(gh_in, gh_rest):
    def body(hin_ref, hrest_ref, oin_ref, orest_ref, send_sems, recv_sems, local_sems):
        x, y, c = _position()
        srcs = (hin_ref, hrest_ref)
        dsts = (oin_ref, orest_ref)

        def rows(a, half):
            return dsts[a].at[half]

        local = [pltpu.make_async_copy(srcs[a], rows(a, c), local_sems.at[a]) for a in range(2)]
        remote = [pltpu.make_async_remote_copy(
            src_ref=srcs[a], dst_ref=rows(a, c), send_sem=send_sems.at[a], recv_sem=recv_sems.at[a],
            device_id=(x, y, 1 - c), device_id_type=MESH) for a in range(2)]
        for cp in local + remote:
            cp.start()
        for a in range(2):
            pltpu.make_async_remote_copy(
                src_ref=srcs[a], dst_ref=rows(a, 1 - c), send_sem=send_sems.at[a], recv_sem=recv_sems.at[a],
                device_id=(x, y, 1 - c), device_id_type=MESH).wait_recv()
        for cp in remote:
            cp.wait_send()
        for cp in local:
            cp.wait()

    return pl.pallas_call(
        body, name="join_halves",
        out_shape=(jax.ShapeDtypeStruct((2, HALF_IN, 1024), F32),
                   jax.ShapeDtypeStruct((2, HALF_REST, 1024), F32)),
        in_specs=[IN_VMEM, IN_VMEM], out_specs=(ANY, ANY),
        scratch_shapes=[pltpu.SemaphoreType.DMA((2,)), pltpu.SemaphoreType.DMA((2,)), pltpu.SemaphoreType.DMA((2,))],
    )(gh_in, gh_rest)


def _allreduce_small(vec):
    def body(vec_ref, out_ref, all_ref, send_sems, recv_sems):
        x, y, c = _position()
        me = 4 * x + 2 * y + c
        all_ref[me] = vec_ref[...]

        def copy(k, slot):
            return pltpu.make_async_remote_copy(
                src_ref=vec_ref, dst_ref=all_ref.at[slot], send_sem=send_sems.at[k - 1], recv_sem=recv_sems.at[k - 1],
                device_id=(x ^ (k >> 2), y ^ ((k >> 1) & 1), c ^ (k & 1)), device_id_type=MESH)

        copies = [copy(k, me) for k in range(1, 8)]
        for cp in copies:
            cp.start()
        for k in range(1, 8):
            copy(k, me ^ k).wait_recv()
        for cp in copies:
            cp.wait_send()
        total = all_ref[0]
        for d in range(1, 8):
            total = total + all_ref[d]
        out_ref[...] = total

    return pl.pallas_call(
        body, name="allreduce_small",
        out_shape=jax.ShapeDtypeStruct(vec.shape, vec.dtype),
        in_specs=[pl.BlockSpec(memory_space=pltpu.VMEM)], out_specs=pl.BlockSpec(memory_space=pltpu.VMEM),
        scratch_shapes=[pltpu.VMEM((8,) + vec.shape, vec.dtype), pltpu.SemaphoreType.DMA((7,)),
                        pltpu.SemaphoreType.DMA((7,))],
    )(vec)


def _pack_rest(w_uq, w_ukv, w_mem, w_out):
    rows = jnp.concatenate([w_uq[0].T.reshape(-1, 1024), w_ukv.reshape(-1, 1024), w_mem.reshape(-1, 1024),
                            w_out.reshape(-1, 1024)], axis=0)
    return jnp.pad(rows, ((0, ROWS_REST - ROWS_USED), (0, 0)))


def _arranged_w_in(g_in):
    z = functools.partial(jnp.zeros, dtype=g_in.dtype)
    cut = 4480 - 2 * SHARD_ROWS
    return jnp.concatenate(
        [g_in[0, :SHARD_ROWS], g_in[1, :SHARD_ROWS], g_in[2, :cut], z((64, 1024)), g_in[2, cut:cut + 32],
         z((32, 1024)), g_in[2, cut + 32:SHARD_ROWS], g_in[3, :SHARD_ROWS]], axis=0)


def _rest_weights(g_rest):
    w_uq_t = g_rest[:, 0:ROWS_UQ].reshape(768, 256)
    w_uq_pad_t = jnp.pad(w_uq_t.reshape(MLA_HEADS, MLA_QK_DIM, 256), ((0, 0), (0, 32), (0, 0))).reshape(1024, 256)
    w_ukv = jnp.concatenate([g_rest[j, ROWS_UQ:ROWS_UQ + ROWS_UKV].reshape(128, 256) for j in range(4)], axis=1)
    lo = ROWS_UQ + ROWS_UKV
    w_mem = g_rest[:, lo:lo + ROWS_MEM].reshape(4 * ROWS_MEM, 1024)
    w_out = g_rest[:, lo + ROWS_MEM:lo + ROWS_MEM + ROWS_OUT].reshape(4 * ROWS_OUT, 1024)
    return w_uq_pad_t, w_ukv, w_mem, w_out


def _split_in(dw_in_arr_t):
    a = dw_in_arr_t
    gap = jnp.zeros((ROWS_IN - SHARD_ROWS, 1024), a.dtype)
    nat = 4608 - 96
    pieces = [a[:SHARD_ROWS], gap, a[SHARD_ROWS:2 * SHARD_ROWS], gap,
              a[2 * SHARD_ROWS:4480], a[4544:4576], a[4608:4608 + 3 * SHARD_ROWS - nat], gap,
              a[4608 + 3 * SHARD_ROWS - nat:], gap]
    return jnp.concatenate(pieces, axis=0).reshape(4, ROWS_IN, 1024)


def _split_rest(dw_uq_pad_t, dw_ukv, dw_mem, dw_out):
    dw_uq_t = dw_uq_pad_t.reshape(MLA_HEADS, LANES, 256)[:, :MLA_QK_DIM].reshape(4, ROWS_UQ, 1024)
    parts = [dw_uq_t, dw_ukv.reshape(128, 4, 256).transpose(1, 0, 2).reshape(4, ROWS_UKV, 1024),
             dw_mem.reshape(4, ROWS_MEM, 1024), dw_out.reshape(4, ROWS_OUT, 1024)]
    return jnp.pad(jnp.concatenate(parts, axis=1), ((0, 0), (0, ROWS_REST - ROWS_USED), (0, 0)))


def _rope_consts(rot, first, period):
    half = rot // 2
    inv_freq = np.float32(ROPE_THETA) ** (-(np.arange(0, rot, 2, dtype=np.float32) / np.float32(rot)))
    lane = np.arange(LANES) % period - first
    in_rot = (lane >= 0) & (lane < rot)
    out = np.zeros((8, LANES), np.float32)
    out[0] = np.where(in_rot, inv_freq[np.clip(lane, 0, rot - 1) % half], 0.0)
    out[1] = in_rot & (lane < half)
    out[2] = in_rot & (lane >= half)
    return jnp.asarray(out)


def _band_bias(s):
    nblk = s // BAND_Q
    starts = np.array([_band_start(i, s) for i in range(nblk)])
    uq = (np.arange(nblk)[:, None] * BAND_Q + np.arange(BAND_Q)[None, :])[:, :, None]
    uk = (starts[:, None] + np.arange(BAND_WIN)[None, :])[:, None, :]
    tiles, index, seen = [], [], {}
    for _, d in DILATED:
        length = s // d
        ok = (uq // length == uk // length) & (np.abs(uq - uk) <= 64)
        row = []
        for i in range(nblk):
            key = ok[i].tobytes()
            if key not in seen:
                seen[key] = len(tiles)
                tiles.append(np.where(ok[i], 0.0, NEG_INF).astype(np.float32))
            row.append(seen[key])
        index.append(row)
    return jnp.asarray(np.stack(tiles, axis=0)), index


def _forward_backward(h, proj, trig, rope_consts, x, mem, target, weights, gains):
    w_uq_pad_t, w_ukv, w_mem, w_out = weights
    g_emb, b_emb, g_cq, g_ckv, g_out_a, g_out_b, g_out_m, g_post, b_post = gains
    nb, s, d = x.shape
    t = nb * s
    x2 = x.reshape(t, d)
    mem2 = mem.reshape(nb * N_MEM, d)
    tgt2 = target.reshape(t, d)
    rope_a, rope_b = rope_consts
    bias, bias_index = _band_bias(s)
    scales = (0.125, MLA_QK_DIM ** -0.5, 128 ** -0.5)

    qa, ka, va, qb, kb, vb, qm, cqn, ckvn = _prep(proj, trig, w_uq_pad_t, w_ukv, g_cq, g_ckv, rope_a, rope_b, scales)
    mkv = _mm(mem2, w_mem, BF16, nb * N_MEM, 1024, 1024, "mem_kv")

    cfg_b = dict(nb=nb, s=s, sk=s, heads=8, voff=0, bq=256)
    cfg_m = dict(nb=nb, s=s, sk=N_MEM, heads=4, hpb=2, voff=4, bq=1024)
    ya, lse_a, qkv_ordered = _dilated_fwd(qa, ka, va, bias, bias_index, nb=nb, s=s, name="attn_a_fwd")
    yb, lse_b = _attn_fwd(qb, kb, vb, name="attn_b_fwd", hpb=4, **cfg_b)
    ym, lse_m = _attn_fwd(qm, mkv, mkv, name="attn_m_fwd", **cfg_m)

    (y, dz, doa, dob, dom, dga, dgb, dgm, loss, dg_post, db_post, dg_a, dg_b, dg_m) = _post(
        x2, ya, yb, ym, proj, tgt2, w_out, g_emb, b_emb, g_out_a, g_out_b, g_out_m, g_post, b_post)

    dqa, dka, dva = _dilated_bwd(qa, ka, va, qkv_ordered, ya, doa, lse_a, bias, bias_index, nb=nb, s=s, scale=scales[0],
                                 name="attn_a_bwd")
    dqb, dkb, dvb = _attn_bwd(qb, kb, vb, yb, dob, lse_b, name="attn_b_bwd", scale=scales[1], hpb=2, **cfg_b)
    dqm, dmk, dmv = _attn_bwd(qm, mkv, mkv, ym, dom, lse_m, name="attn_m_bwd", scale=scales[2], **cfg_m)
    dmkv = jnp.concatenate([dmk, dmv], axis=1)

    dproj, dqf, dkv, dg_cq, dg_ckv = _prep_bwd(
        dqa, dka, dva, dqb, dkb, dvb, dqm, dga, dgb, dgm, proj, trig, w_uq_pad_t, w_ukv, g_cq, g_ckv, rope_a, rope_b)

    small_rows = (dg_cq, dg_ckv, loss, dg_a, dg_b, dg_m, dg_post, db_post)
    return (dproj, h, y, dz, dqf, cqn, ckvn, dkv, mem2, dmkv), x2, small_rows


def _weight_grads(operands, core):
    dproj, h, y, dz, dqf, cqn, ckvn, dkv, mem2, dmkv = operands
    dw_in_arr_t = _mm(dproj, h, F32, 1024, 1024, 4096, "dw_in", mode="tn")
    g_in = _split_in(dw_in_arr_t)
    dw_out, r_in = _mm(y, dz, F32, 1024, 1024, 2048, "dw_out", mode="tn",
                       ride=_half_to_sibling(g_in.reshape(4, 2, HALF_IN, 1024)))
    dw_uq_pad_t = _mm(dqf, cqn, F32, 1024, 256, 4096, "dw_uq", mode="tn")
    dw_ukv = _mm(ckvn, dkv, F32, 128, 1024, 4096, "dw_ukv", mode="tn")
    dw_mem = _mm(mem2, dmkv, F32, 1024, 1024, mem2.shape[0], "dw_mem", mode="tn")
    g_rest = _split_rest(dw_uq_pad_t, dw_ukv, dw_mem, dw_out)
    sf_in, sb_in, r_rest = _core_sum(g_in, r_in, core, HALF_IN, HALF_IN // 2, "core_sum_in",
                                     ride=_half_to_sibling(g_rest.reshape(4, 2, HALF_REST, 1024)))
    sf_rest, sb_rest = _core_sum(g_rest, r_rest, core, HALF_REST, HALF_REST, "core_sum_rest")
    return sf_in, sb_in, sf_rest, sb_rest


def _small_block(dg_emb, db_emb, small_rows):
    dg_cq, dg_ckv, loss, dg_a, dg_b, dg_m, dg_post, db_post = small_rows
    row2 = jnp.concatenate([dg_cq, dg_ckv, loss, jnp.zeros((1, 512), F32)], axis=1)
    return jnp.concatenate([dg_emb, db_emb, row2, dg_a, jnp.concatenate([dg_b, dg_m], axis=1), dg_post, db_post,
                            jnp.zeros((1, 1024), F32)], axis=0)


def _pack_small(g_emb, b_emb, g_cq, g_ckv, g_out_a, g_out_b, g_out_m, g_post, b_post):
    row2 = jnp.concatenate([g_cq.reshape(1, -1), g_ckv.reshape(1, -1), jnp.zeros((1, 640), F32)], axis=1)
    return jnp.concatenate([g_emb.reshape(1, -1), b_emb.reshape(1, -1), row2, g_out_a.reshape(1, -1),
                            jnp.concatenate([g_out_b.reshape(1, -1), g_out_m.reshape(1, -1)], axis=1),
                            g_post.reshape(1, -1), b_post.reshape(1, -1), jnp.zeros((1, 1024), F32)], axis=0)


def kernel(x, mem, positions, g_emb, b_emb, w_in, g_cq, g_ckv, w_uq, w_ukv, w_mem_kv, g_out_a, g_out_b, g_out_m, w_out, g_post, b_post, loss_target, m_g_emb, m_b_emb, m_w_in, m_g_cq, m_g_ckv, m_w_uq, m_w_ukv, m_w_mem_kv, m_g_out_a, m_g_out_b, m_g_out_m, m_w_out, m_g_post, m_b_post, v_g_emb, v_b_emb, v_w_in, v_g_cq, v_g_ckv, v_w_uq, v_w_ukv, v_w_mem_kv, v_g_out_a, v_g_out_b, v_g_out_m, v_w_out, v_g_post, v_b_post):
    w_rest = _pack_rest(w_uq, w_ukv, w_mem_kv, w_out)
    w_in_t = w_in[0].T
    w_in_b = jnp.pad(w_in_t.astype(BF16), ((0, ROWS_IN - SHARD_ROWS), (0, 0)))
    gains = (g_emb.reshape(1, -1), b_emb.reshape(1, -1), g_cq, g_ckv, g_out_a, g_out_b, g_out_m, g_post, b_post)
    rope_consts = (_rope_consts(16, 0, 64), _rope_consts(32, 64, 128))
    h, trig, gathered_in = _ln_fwd(x.reshape(-1, D_MODEL), gains[0], gains[1],
                                   positions.reshape(-1, 1).astype(F32), *rope_consts,
                                   ride=_gather_ride(w_in_b.reshape(2, HALF_IN, 1024)))
    w_in_arr_t = _arranged_w_in(gathered_in.reshape(4, ROWS_IN, 1024))
    proj, gathered_rest = _mm(h, w_in_arr_t, F32, 1024, 2048, 1024, "in_proj", mode="nt",
                              ride=_gather_ride(w_rest.astype(BF16).reshape(2, HALF_REST, 1024)))
    weights = _rest_weights(gathered_rest.reshape(4, ROWS_REST, 1024))
    operands, x2, small_rows = _forward_backward(h, proj, trig, rope_consts, x, mem, loss_target, weights, gains)

    core = lax.axis_index("c").astype(jnp.int32).reshape(1)
    chip = (2 * lax.axis_index("x") + lax.axis_index("y")).astype(jnp.int32).reshape(1)
    sf_in, sb_in, sf_rest, sb_rest = _weight_grads(operands, core)
    grad_x, dg_emb, db_emb, rb_in, rb_rest = _dh_scatter(operands[0], w_in_arr_t, x2, operands[3], gains[0],
                                                         sb_in, sb_rest)
    gh_in = _chip_sum(sf_in, rb_in, chip, HALF_IN, HALF_IN // 2, "chip_sum_in")
    gh_rest = _chip_sum(sf_rest, rb_rest, chip, HALF_REST, HALF_REST, "chip_sum_rest")
    grad_in, grad_rest = _join_halves(gh_in, gh_rest)
    grad_in = grad_in.reshape(ROWS_IN, 1024)
    grad_rest = grad_rest.reshape(ROWS_REST, 1024)

    big_in = _adamw(grad_in, w_in_t, m_w_in[0].T, v_w_in[0].T, SHARD_ROWS // 3, "adamw_in")
    uq, ukv, wmem, wout = _adamw_pieces(
        grad_rest, w_rest, _pack_rest(m_w_uq, m_w_ukv, m_w_mem_kv, m_w_out),
        _pack_rest(v_w_uq, v_w_ukv, v_w_mem_kv, v_w_out), REST_PIECES, "adamw_rest")
    small_sum = _allreduce_small(_small_block(dg_emb, db_emb, small_rows))
    sm = _adamw_pieces(
        small_sum,
        _pack_small(g_emb, b_emb, g_cq, g_ckv, g_out_a, g_out_b, g_out_m, g_post, b_post),
        _pack_small(m_g_emb, m_b_emb, m_g_cq, m_g_ckv, m_g_out_a, m_g_out_b, m_g_out_m, m_g_post, m_b_post),
        _pack_small(v_g_emb, v_b_emb, v_g_cq, v_g_ckv, v_g_out_a, v_g_out_b, v_g_out_m, v_g_post, v_b_post),
        SMALL_PIECES, "adamw_small")
    loss = small_sum[2, 384]

    def ordered(kind):
        s_gemb, s_bemb, s_gcq, s_gckv, s_ga, s_gb, s_gm, s_gpost, s_bpost = [piece[kind] for piece in sm]
        return [s_gemb.reshape(-1), s_bemb.reshape(-1), big_in[kind].T[None], s_gcq, s_gckv,
                uq[kind].reshape(192, 256).T[None], ukv[kind].reshape(1, 128, 256), wmem[kind][None], s_ga, s_gb,
                s_gm, wout[kind][None], s_gpost, s_bpost]

    return (loss, grad_x.reshape(x.shape), *ordered(0), *ordered(1), *ordered(2), *ordered(3))
```

```python
import functools
import math

import jax
import jax.numpy as jnp
import numpy as np
from jax import lax
from jax.experimental import pallas as pl
from jax.experimental.pallas import tpu as pltpu

F32 = jnp.float32
BF16 = jnp.bfloat16
MESH = pl.DeviceIdType.MESH
ANY = pl.BlockSpec(memory_space=pl.ANY)
IN_VMEM = pl.BlockSpec(memory_space=pltpu.VMEM)

D_MODEL = 1024
A_WIDTH = 1024
MLA_HEADS = 8
MLA_Q_RANK = 256
MLA_KV_RANK = 128
MLA_QK_DIM = 96
MEM_WIDTH = 512
N_MEM = 256
ROPE_THETA = 500000.0
NORM_EPS = 1e-5
NEG_INF = -1e30
DEEPNORM_ALPHA = 2.0 ** 0.25
DILATED = ((64, 1), (256, 4), (1024, 16))

ADAM_LR = 0.001
ADAM_B1 = 0.9
ADAM_B2 = 0.999
ADAM_EPS = 1e-08
ADAM_WD = 0.01
ADAM_STEP = 10

LANES = 128
VMEM_LIMIT = 56 * 1024 * 1024
LOG2E = math.log2(math.e)
LN2 = math.log(2.0)

PROJ_W = 6144
COL_CQ = 4096
COL_BG = 4608
COL_MQ = 5120
COL_MG = 5632

SHARD_ROWS = 1512
ROWS_IN = 1536
ROWS_UQ, ROWS_UKV, ROWS_MEM, ROWS_OUT = 48, 32, 256, 512
ROWS_USED = ROWS_UQ + ROWS_UKV + ROWS_MEM + ROWS_OUT
ROWS_REST = 864
HALF_IN = ROWS_IN // 2
HALF_REST = ROWS_REST // 2
REST_PIECES = ((0, 48, 0, 1024), (48, 80, 0, 1024), (80, 336, 0, 1024), (336, 848, 0, 1024))
SMALL_PIECES = ((0, 1, 0, 1024), (1, 2, 0, 1024), (2, 3, 0, 256), (2, 3, 256, 384), (3, 4, 0, 1024), (4, 5, 0, 512),
                (4, 5, 512, 1024), (5, 6, 0, 1024), (6, 7, 0, 1024))


def _params(sem=None, vmem=VMEM_LIMIT):
    return pltpu.CompilerParams(dimension_semantics=sem, vmem_limit_bytes=vmem)


def _dot(a, b):
    return jnp.dot(a, b, preferred_element_type=F32)


def _dot_nt(a, b):
    return lax.dot_general(a, b, (((1,), (1,)), ((), ())), preferred_element_type=F32)


def _dot_tn(a, b):
    return lax.dot_general(a, b, (((0,), (0,)), ((), ())), preferred_element_type=F32)


def _ln_hat(x):
    mu = jnp.mean(x, axis=-1, keepdims=True)
    xc = x - mu
    var = jnp.mean(xc * xc, axis=-1, keepdims=True)
    rstd = lax.rsqrt(var + NORM_EPS)
    return xc * rstd, rstd


def _ln_bwd_rows(dxh, xh, rstd):
    return rstd * (dxh - jnp.mean(dxh, axis=-1, keepdims=True) - xh * jnp.mean(dxh * xh, axis=-1, keepdims=True))


def _rms_hat(x, width):
    ms = jnp.sum(x * x, axis=-1, keepdims=True) * (1.0 / width)
    r = lax.rsqrt(ms + NORM_EPS)
    return x * r, r


def _rms_bwd(u, xh, r, width):
    return r * (u - xh * (jnp.sum(u * xh, axis=-1, keepdims=True) * (1.0 / width)))


def _colsum(v):
    return jnp.sum(v, axis=0, keepdims=True)


def _rope_tables(cos, sin, consts):
    return cos, sin * consts[2:3, :], -sin * consts[1:2, :]


def _rope(x, tables, half, inverse=False):
    c, s_up, s_dn = tables
    if inverse:
        s_up, s_dn = -s_up, -s_dn
    return x * c + pltpu.roll(x, half, 1) * s_up + pltpu.roll(x, LANES - half, 1) * s_dn


def _ln_fwd(x, g, b, pos, rope_a, rope_b, tm=512, ride=None):
    t, d = x.shape
    n_in = len(ride.args) if ride else 0
    n_out = len(ride.out_shapes) if ride else 0
    steps = t // tm

    def body(x_ref, g_ref, b_ref, pos_ref, ra_ref, rb_ref, *rest):
        h_ref, trig_ref = rest[n_in], rest[n_in + 1]
        if ride:
            i = pl.program_id(0)
            ride.run(i, steps, rest[:n_in], rest[n_in + 2:n_in + 2 + n_out], rest[n_in + 2 + n_out:])
        xh, _ = _ln_hat(x_ref[...])
        h_ref[...] = (xh * g_ref[...] + b_ref[...]).astype(BF16)
        for j, consts in enumerate((ra_ref, rb_ref)):
            ang = pos_ref[...] * consts[0:1, :]
            trig_ref[:, 2 * j * LANES:(2 * j + 1) * LANES] = jnp.cos(ang)
            trig_ref[:, (2 * j + 1) * LANES:(2 * j + 2) * LANES] = jnp.sin(ang)

    row = pl.BlockSpec((1, d), lambda i: (0, 0))
    tile = pl.BlockSpec((tm, d), lambda i: (i, 0))
    consts = pl.BlockSpec((8, LANES), lambda i: (0, 0))
    trig_tile = pl.BlockSpec((tm, 4 * LANES), lambda i: (i, 0))
    in_specs = [tile, row, row, pl.BlockSpec((tm, 1), lambda i: (i, 0)), consts, consts]
    shapes = (jax.ShapeDtypeStruct((t, d), BF16), jax.ShapeDtypeStruct((t, 4 * LANES), F32))
    if not ride:
        return pl.pallas_call(
            body, name="ln_fwd", grid=(steps,), out_shape=shapes, in_specs=in_specs, out_specs=(tile, trig_tile),
            compiler_params=_params(("parallel",)),
        )(x, g, b, pos, rope_a, rope_b)
    return pl.pallas_call(
        body, name="ln_fwd", grid=(steps,),
        out_shape=(*shapes, *ride.out_shapes),
        in_specs=in_specs + ride.in_specs, out_specs=(tile, trig_tile) + (ANY,) * n_out,
        scratch_shapes=ride.scratch(),
        compiler_params=_params(("arbitrary",)),
    )(x, g, b, pos, rope_a, rope_b, *ride.args)


class _Ride:
    def __init__(self, args, out_shapes, sem_counts, plan, in_specs=None, spread=True):
        self.args, self.out_shapes, self.plan = list(args), list(out_shapes), plan
        self.sem_counts = sem_counts
        self.in_specs = in_specs or [ANY] * len(self.args)
        self.spread = spread

    def scratch(self):
        return [pltpu.SemaphoreType.DMA((n,)) for n in self.sem_counts]

    def run(self, step, total, in_refs, out_refs, sems):
        count = len(self.plan(in_refs, out_refs, *sems))
        at = [(k * (total - 1)) // (count - 1) if self.spread or k == 0 else total - 1 for k in range(count)]
        for when in sorted(set(at)):
            @pl.when(step == when)
            def _(when=when):
                stages = self.plan(in_refs, out_refs, *sems)
                for k in range(count):
                    if at[k] == when:
                        stages[k]()


def _mm(a, b, out_dtype, tm, tn, tk, name, mode="nn", ride=None):
    if mode == "tn":
        k, m = a.shape
    else:
        m, k = a.shape
    n = b.shape[0] if mode == "nt" else b.shape[1]
    nk = k // tk
    nj, ni = n // tn, m // tm
    n_in = len(ride.args) if ride else 0
    n_out = len(ride.out_shapes) if ride else 0

    def body(a_ref, b_ref, *rest):
        o_ref = rest[n_in]
        acc_ref = rest[n_in + 1 + n_out]
        if ride:
            j, i, kk = pl.program_id(0), pl.program_id(1), pl.program_id(2)
            ride.run((j * ni + i) * nk + kk, nj * ni * nk, rest[:n_in], rest[n_in + 1:n_in + 1 + n_out],
                     rest[n_in + 2 + n_out:])
        av = a_ref[...].astype(BF16)
        bv = b_ref[...].astype(BF16)
        part = _dot_tn(av, bv) if mode == "tn" else _dot_nt(av, bv) if mode == "nt" else _dot(av, bv)
        if nk == 1:
            o_ref[...] = part.astype(out_dtype)
        else:
            kk = pl.program_id(2)

            @pl.when(kk == 0)
            def _():
                acc_ref[...] = part

            @pl.when(kk > 0)
            def _():
                acc_ref[...] += part

            @pl.when(kk == nk - 1)
            def _():
                o_ref[...] = acc_ref[...].astype(out_dtype)

    a_spec = (pl.BlockSpec((tk, tm), lambda j, i, kk: (kk, i)) if mode == "tn"
              else pl.BlockSpec((tm, tk), lambda j, i, kk: (i, kk)))
    b_spec = (pl.BlockSpec((tn, tk), lambda j, i, kk: (j, kk)) if mode == "nt"
              else pl.BlockSpec((tk, tn), lambda j, i, kk: (kk, j)))
    o_spec = pl.BlockSpec((tm, tn), lambda j, i, kk: (i, j))
    o_shape = jax.ShapeDtypeStruct((m, n), out_dtype)
    if not ride:
        return pl.pallas_call(
            body, name=name, grid=(nj, ni, nk), out_shape=o_shape, in_specs=[a_spec, b_spec], out_specs=o_spec,
            scratch_shapes=[pltpu.VMEM((tm, tn), F32)],
            compiler_params=_params(("parallel", "parallel", "arbitrary")),
        )(a, b)
    return pl.pallas_call(
        body, name=name, grid=(nj, ni, nk),
        out_shape=(o_shape, *ride.out_shapes),
        in_specs=[a_spec, b_spec] + ride.in_specs,
        out_specs=(o_spec,) + (ANY,) * n_out,
        scratch_shapes=[pltpu.VMEM((tm, tn), F32)] + ride.scratch(),
        compiler_params=_params(("arbitrary", "arbitrary", "arbitrary")),
    )(a, b, *ride.args)


def _prep(proj, trig, w_uq, w_ukv, g_cq, g_ckv, rope_a, rope_b, scales, tm=256):
    t = proj.shape[0]
    sc_a, sc_b, sc_m = (s * LOG2E for s in scales)

    def body(aq_ref, ak_ref, av_ref, bs_ref, mq_ref, trig_ref, wuq_ref, wukv_ref, gcq_ref, gckv_ref,
             ra_ref, rb_ref, qa_ref, ka_ref, va_ref, qb_ref, kb_ref, vb_ref, qm_ref, cqn_ref, ckvn_ref):
        ta = _rope_tables(trig_ref[:, 0:LANES], trig_ref[:, LANES:2 * LANES], ra_ref[...])
        tb = _rope_tables(trig_ref[:, 2 * LANES:3 * LANES], trig_ref[:, 3 * LANES:4 * LANES], rb_ref[...])
        for j in range(A_WIDTH // LANES):
            sl = slice(j * LANES, (j + 1) * LANES)
            qa_ref[:, sl] = (_rope(aq_ref[:, sl], ta, 8) * sc_a).astype(BF16)
            ka_ref[:, sl] = _rope(ak_ref[:, sl], ta, 8).astype(BF16)
        va_ref[...] = av_ref[...].astype(BF16)
        qm_ref[...] = (mq_ref[...] * sc_m).astype(BF16)

        cq_hat, _ = _rms_hat(bs_ref[:, 0:MLA_Q_RANK], MLA_Q_RANK)
        cqn = (cq_hat * gcq_ref[...]).astype(BF16)
        cqn_ref[...] = cqn
        ckv_hat, _ = _rms_hat(bs_ref[:, MLA_Q_RANK:MLA_Q_RANK + MLA_KV_RANK], MLA_KV_RANK)
        ckvn = (ckv_hat * gckv_ref[...]).astype(BF16)
        ckvn_ref[...] = ckvn
        qfull = _dot_nt(cqn, wuq_ref[...])
        kv = _dot(ckvn, wukv_ref[...])
        kr = _rope(bs_ref[:, 384:512], tb, 16)
        lane = lax.broadcasted_iota(jnp.int32, (1, LANES), 1)
        low = lane < 64
        for h in range(MLA_HEADS):
            sl = slice(h * LANES, (h + 1) * LANES)
            qb_ref[:, sl] = (_rope(qfull[:, sl], tb, 16) * sc_b).astype(BF16)
            kb_ref[:, sl] = jnp.where(low, kv[:, sl], kr).astype(BF16)
            vb_ref[:, sl] = jnp.where(low, 0.0, kv[:, sl]).astype(BF16)

    def col(width, idx):
        return pl.BlockSpec((tm, width), lambda i: (i, idx))

    def full(shape):
        return pl.BlockSpec(shape, lambda i: (0, 0))

    wide = jax.ShapeDtypeStruct((t, 1024), BF16)
    return pl.pallas_call(
        body, name="prep", grid=(t // tm,),
        out_shape=(wide, wide, wide, wide, wide, wide,
                   jax.ShapeDtypeStruct((t, MEM_WIDTH), BF16),
                   jax.ShapeDtypeStruct((t, MLA_Q_RANK), BF16),
                   jax.ShapeDtypeStruct((t, MLA_KV_RANK), BF16)),
        in_specs=[col(1024, 0), col(1024, 1), col(1024, 2), col(512, COL_CQ // 512), col(512, COL_MQ // 512),
                  pl.BlockSpec((tm, 4 * LANES), lambda i: (i, 0)),
                  full((1024, MLA_Q_RANK)), full((MLA_KV_RANK, 1024)),
                  full((1, MLA_Q_RANK)), full((1, MLA_KV_RANK)), full((8, LANES)), full((8, LANES))],
        out_specs=(col(1024, 0),) * 6 + (col(MEM_WIDTH, 0), col(MLA_Q_RANK, 0), col(MLA_KV_RANK, 0)),
        compiler_params=_params(("parallel",)),
    )(proj, proj, proj, proj, proj, trig, w_uq, w_ukv, g_cq, g_ckv, rope_a, rope_b)


def _attn_fwd(q, k, v, *, nb, s, sk, heads, hpb, voff, bq, name):
    nq = s // bq
    width = hpb * LANES
    vblk = voff // hpb

    def body(q_ref, k_ref, v_ref, o_ref, lse_ref):
        for h in range(hpb):
            sl = slice(h * LANES, (h + 1) * LANES)
            sc = _dot_nt(q_ref[:, sl], k_ref[:, sl])
            m = jnp.max(sc, axis=1, keepdims=True)
            p = jnp.exp2(sc - m)
            l = jnp.sum(p, axis=1, keepdims=True)
            o_ref[:, sl] = _dot(p.astype(BF16), v_ref[:, sl]) / l
            lse_ref[:, sl] = jnp.broadcast_to(m + jnp.log(l) * LOG2E, (bq, LANES))

    out = jax.ShapeDtypeStruct((nb * s, heads * LANES), F32)
    ospec = pl.BlockSpec((bq, width), lambda b, i, g: (b * nq + i, g))
    return pl.pallas_call(
        body, name=name, grid=(nb, nq, heads // hpb),
        out_shape=(out, out),
        in_specs=[ospec, pl.BlockSpec((sk, width), lambda b, i, g: (b, g)),
                  pl.BlockSpec((sk, width), lambda b, i, g: (b, vblk + g))],
        out_specs=(ospec, ospec),
        compiler_params=_params(("parallel", "parallel", "parallel")),
    )(q, k, v)


def _attn_bwd(q, k, v, o, do, lse, *, nb, s, sk, heads, hpb, voff, scale, bq, name):
    nq = s // bq
    width = hpb * LANES
    vblk = voff // hpb

    def body(q_ref, k_ref, v_ref, o_ref, do_ref, lse_ref, dq_ref, dk_ref, dv_ref, dk_acc, dv_acc):
        i = pl.program_id(2)

        @pl.when(i == 0)
        def _():
            dk_acc[...] = jnp.zeros_like(dk_acc)
            dv_acc[...] = jnp.zeros_like(dv_acc)

        for h in range(hpb):
            sl = slice(h * LANES, (h + 1) * LANES)
            qh = q_ref[:, sl]
            kk = k_ref[:, sl]
            doh = do_ref[:, sl]
            delta = jnp.sum(doh.astype(F32) * o_ref[:, sl], axis=1, keepdims=True)
            p = jnp.exp2(_dot_nt(qh, kk) - lse_ref[:, h * LANES:h * LANES + 1])
            ds = (p * (_dot_nt(doh, v_ref[:, sl]) - delta)).astype(BF16)
            dq_ref[:, sl] = (_dot(ds, kk) * scale).astype(BF16)
            dk_acc[:, sl] += _dot_tn(ds, qh)
            dv_acc[:, sl] += _dot_tn(p.astype(BF16), doh)

        @pl.when(i == nq - 1)
        def _():
            dk_ref[...] = (dk_acc[...] * LN2).astype(BF16)
            dv_ref[...] = dv_acc[...].astype(BF16)

    qspec = pl.BlockSpec((bq, width), lambda b, g, i: (b * nq + i, g))
    kv_spec = pl.BlockSpec((sk, width), lambda b, g, i: (b, g))
    dq_shape = jax.ShapeDtypeStruct((nb * s, heads * LANES), BF16)
    dkv_shape = jax.ShapeDtypeStruct((nb * sk, heads * LANES), BF16)
    return pl.pallas_call(
        body, name=name, grid=(nb, heads // hpb, nq),
        out_shape=(dq_shape, dkv_shape, dkv_shape),
        in_specs=[qspec, kv_spec, pl.BlockSpec((sk, width), lambda b, g, i: (b, vblk + g)), qspec, qspec, qspec],
        out_specs=(qspec, kv_spec, kv_spec),
        scratch_shapes=[pltpu.VMEM((sk, width), F32), pltpu.VMEM((sk, width), F32)],
        compiler_params=_params(("parallel", "parallel", "arbitrary")),
    )(q, k, v, o, do, lse)


BAND_Q = 128
BAND_WIN = 256


def _band_start(i, s):
    return min(max(i * BAND_Q - 64, 0), s - BAND_WIN)


def _to_pattern_order(src_ref, dst_ref, stage_ref, s, d):
    length = s // d
    stage_ref[...] = src_ref[...].astype(F32)
    for r in range(d):
        dst_ref[r * length:(r + 1) * length, :] = stage_ref[pl.ds(r, length, stride=d), :].astype(dst_ref.dtype)


def _dilated_fwd(q, k, v, bias, bias_index, *, nb, s, name):
    nblk = s // BAND_Q
    npat = len(DILATED)

    def body(q_ref, k_ref, v_ref, bias_ref, o_ref, lse_ref, *rest):
        ordered = rest[:3 * (npat - 1)]
        stage_ref, op_ref, lp_ref, on_ref, ln_ref = rest[3 * (npat - 1):]
        lane = lax.broadcasted_iota(jnp.int32, (1, LANES), 1)
        first = lane < 64
        for p, (_, d) in enumerate(DILATED):
            if d == 1:
                qs, ks, vs = q_ref, k_ref, v_ref
            else:
                qs, ks, vs = ordered[3 * (p - 1):3 * p]
                for src, dst in ((q_ref, qs), (k_ref, ks), (v_ref, vs)):
                    _to_pattern_order(src, dst, stage_ref, s, d)
            for i in range(nblk):
                u0 = i * BAND_Q
                st = _band_start(i, s)
                qi = qs[u0:u0 + BAND_Q, :]
                kw = ks[st:st + BAND_WIN, :]
                vw = vs[st:st + BAND_WIN, :]
                zero = jnp.zeros_like(qi)
                q2 = jnp.concatenate([jnp.where(first, qi, zero), jnp.where(first, zero, qi)], axis=0)
                sc = _dot_nt(q2, kw)
                b = bias_ref[bias_index[p][i]]
                halves = []
                for h in range(2):
                    sh = sc[h * BAND_Q:(h + 1) * BAND_Q] + b
                    m = jnp.max(sh, axis=1, keepdims=True)
                    pr = jnp.exp2(sh - m)
                    l = jnp.sum(pr, axis=1, keepdims=True)
                    halves.append((pr.astype(BF16), l, m + jnp.log(l) * LOG2E))
                o2 = _dot(jnp.concatenate([halves[0][0], halves[1][0]], axis=0), vw)
                o_blk = jnp.where(first, o2[:BAND_Q] / halves[0][1], o2[BAND_Q:] / halves[1][1])
                lse_blk = jnp.where(first, jnp.broadcast_to(halves[0][2], (BAND_Q, LANES)),
                                    jnp.broadcast_to(halves[1][2], (BAND_Q, LANES)))
                op_ref[p, u0:u0 + BAND_Q, :] = o_blk
                lp_ref[p, u0:u0 + BAND_Q, :] = lse_blk
            if d > 1:
                length = s // d
                for r in range(d):
                    on_ref.at[p - 1][pl.ds(r, length, stride=d), :] = op_ref[p, r * length:(r + 1) * length, :]
                    ln_ref.at[p - 1][pl.ds(r, length, stride=d), :] = lp_ref[p, r * length:(r + 1) * length, :]
        lses = [lp_ref[0]] + [ln_ref[p] for p in range(npat - 1)]
        outs = [op_ref[0]] + [on_ref[p] for p in range(npat - 1)]
        m = functools.reduce(jnp.maximum, lses)
        ws = [jnp.exp2(l - m) for l in lses]
        den = functools.reduce(lambda a, c: a + c, ws)
        o_ref[...] = functools.reduce(lambda a, c: a + c, [w * o for w, o in zip(ws, outs)]) / den
        lse_ref[...] = m + jnp.log(den) * LOG2E

    blk = pl.BlockSpec((s, LANES), lambda b, g: (b, g))
    out = jax.ShapeDtypeStruct((nb * s, A_WIDTH), F32)
    copy = jax.ShapeDtypeStruct((nb * s, A_WIDTH), BF16)
    n_copies = 3 * (npat - 1)
    res = pl.pallas_call(
        body, name=name, grid=(nb, A_WIDTH // LANES),
        out_shape=(out, out) + (copy,) * n_copies,
        in_specs=[blk, blk, blk, pl.BlockSpec(bias.shape, lambda b, g: (0, 0, 0))],
        out_specs=(blk, blk) + (blk,) * n_copies,
        scratch_shapes=[pltpu.VMEM((s, LANES), F32), pltpu.VMEM((npat, s, LANES), F32),
                        pltpu.VMEM((npat, s, LANES), F32), pltpu.VMEM((npat - 1, s, LANES), F32),
                        pltpu.VMEM((npat - 1, s, LANES), F32)],
        compiler_params=_params(("parallel", "parallel")),
    )(q, k, v, bias)
    return res[0], res[1], res[2:]


def _dilated_bwd(q, k, v, ordered, o, do, lse, bias, bias_index, *, nb, s, scale, name):
    nblk = s // BAND_Q
    npat = len(DILATED)
    n_copies = 3 * (npat - 1)

    def body(q_ref, k_ref, v_ref, *rest):
        ordered_refs = rest[:n_copies]
        (o_ref, do_ref, lse_ref, bias_ref, dq_out, dk_out, dv_out, stage_ref, rs_ref, dop_ref, rsp_ref,
         dqp_ref, dkp_ref, dvp_ref, dq_ref, dk_ref, dv_ref) = rest[n_copies:]
        lane = lax.broadcasted_iota(jnp.int32, (1, LANES), 1)
        first = lane < 64
        prod = do_ref[...].astype(F32) * o_ref[...]
        d0 = jnp.sum(jnp.where(first, prod, 0.0), axis=1, keepdims=True)
        d1 = jnp.sum(jnp.where(first, 0.0, prod), axis=1, keepdims=True)
        delta = jnp.where(first, jnp.broadcast_to(d0, (s, LANES)), jnp.broadcast_to(d1, (s, LANES)))
        rs_ref[...] = jnp.where((lane & 32) == 0, lse_ref[...], delta)
        for p, (_, d) in enumerate(DILATED):
            length = s // d
            if d == 1:
                qs, ks, vs, dos, rss = q_ref, k_ref, v_ref, do_ref, rs_ref
                dqs, dks, dvs = dq_ref, dk_ref, dv_ref
            else:
                for src, dst in ((do_ref, dop_ref), (rs_ref, rsp_ref)):
                    _to_pattern_order(src, dst, stage_ref, s, d)
                qs, ks, vs = ordered_refs[3 * (p - 1):3 * p]
                dos, rss = dop_ref, rsp_ref
                dqs, dks, dvs = dqp_ref, dkp_ref, dvp_ref
            dks[...] = jnp.zeros((s, LANES), F32)
            dvs[...] = jnp.zeros((s, LANES), F32)
            for i in range(nblk):
                u0 = i * BAND_Q
                st = _band_start(i, s)
                qi = qs[u0:u0 + BAND_Q, :]
                doi = dos[u0:u0 + BAND_Q, :]
                kw = ks[st:st + BAND_WIN, :]
                vw = vs[st:st + BAND_WIN, :]
                zero = jnp.zeros_like(qi)
                q2 = jnp.concatenate([jnp.where(first, qi, zero), jnp.where(first, zero, qi)], axis=0)
                do2 = jnp.concatenate([jnp.where(first, doi, zero), jnp.where(first, zero, doi)], axis=0)
                sc = _dot_nt(q2, kw)
                dp = _dot_nt(do2, vw)
                b = bias_ref[bias_index[p][i]]
                rs_i = rss[u0:u0 + BAND_Q, :]
                ps, dss = [], []
                for h in range(2):
                    rows = slice(h * BAND_Q, (h + 1) * BAND_Q)
                    pr = jnp.exp2(sc[rows] + b - rs_i[:, 64 * h:64 * h + 1])
                    ps.append(pr.astype(BF16))
                    dss.append((pr * (dp[rows] - rs_i[:, 64 * h + 32:64 * h + 33])).astype(BF16))
                p2 = jnp.concatenate(ps, axis=0)
                ds2 = jnp.concatenate(dss, axis=0)
                dq2 = _dot(ds2, kw)
                dqs[u0:u0 + BAND_Q, :] = jnp.where(first, dq2[:BAND_Q], dq2[BAND_Q:]) * scale
                dks[st:st + BAND_WIN, :] += _dot_tn(ds2, q2)
                dvs[st:st + BAND_WIN, :] += _dot_tn(p2, do2)
            if d > 1:
                for dst, src in ((dq_ref, dqp_ref), (dk_ref, dkp_ref), (dv_ref, dvp_ref)):
                    for r in range(d):
                        dst[pl.ds(r, length, stride=d), :] += src[r * length:(r + 1) * length, :]
        dq_out[...] = dq_ref[...].astype(BF16)
        dk_out[...] = (dk_ref[...] * LN2).astype(BF16)
        dv_out[...] = dv_ref[...].astype(BF16)

    blk = pl.BlockSpec((s, LANES), lambda b, g: (b, g))
    out = jax.ShapeDtypeStruct((nb * s, A_WIDTH), BF16)
    f32_buf = pltpu.VMEM((s, LANES), F32)
    bf_buf = pltpu.VMEM((s, LANES), BF16)
    return pl.pallas_call(
        body, name=name, grid=(nb, A_WIDTH // LANES),
        out_shape=(out, out, out),
        in_specs=[blk] * (6 + n_copies) + [pl.BlockSpec(bias.shape, lambda b, g: (0, 0, 0))],
        out_specs=(blk, blk, blk),
        scratch_shapes=[f32_buf, f32_buf, bf_buf] + [f32_buf] * 7,
        compiler_params=_params(("parallel", "parallel")),
    )(q, k, v, *ordered, o, do, lse, bias)


def _post(x, ya, ybp, ym, proj, target, w_out, g_emb, b_emb, g_a, g_b, g_m, g_post, b_post, tm=256):
    t = x.shape[0]

    def body(x_ref, ya_ref, yb_ref, ym_ref, ga_ref, gb_ref, gm_ref, tg_ref, wo_ref,
             ge_ref, be_ref, goa_ref, gob_ref, gom_ref, gp_ref, bp_ref,
             y_ref, dz_ref, doa_ref, dob_ref, dom_ref, dga_ref, dgb_ref, dgm_ref,
             loss_ref, dgp_ref, dbp_ref, dgoa_ref, dgob_ref, dgom_ref):
        i = pl.program_id(0)

        @pl.when(i == 0)
        def _():
            for r in (loss_ref, dgp_ref, dbp_ref, dgoa_ref, dgob_ref, dgom_ref):
                r[...] = jnp.zeros_like(r)

        lane = lax.broadcasted_iota(jnp.int32, (1, LANES), 1)
        low = lane < 64
        xh0, _ = _ln_hat(x_ref[...])
        h = xh0 * ge_ref[...] + be_ref[...]

        ybp_v = yb_ref[...]
        yb = jnp.concatenate(
            [jnp.where(low, pltpu.roll(ybp_v[:, 2 * j * LANES:(2 * j + 1) * LANES], 64, 1),
                       ybp_v[:, (2 * j + 1) * LANES:(2 * j + 2) * LANES]) for j in range(4)], axis=1)

        def gated(raw, gate, gain, width):
            xh, r = _rms_hat(raw, width)
            n = xh * gain
            sg = 1.0 / (1.0 + jnp.exp(-gate))
            return xh, r, n, sg, n * (gate * sg)

        gate_a, gate_b, gate_m = ga_ref[...], gb_ref[...], gm_ref[...]
        xh_a, r_a, n_a, sg_a, y_a = gated(ya_ref[...], gate_a, goa_ref[...], A_WIDTH)
        xh_b, r_b, n_b, sg_b, y_b = gated(yb, gate_b, gob_ref[...], 512)
        xh_m, r_m, n_m, sg_m, y_m = gated(ym_ref[...], gate_m, gom_ref[...], 512)
        y = jnp.concatenate([y_a, y_b, y_m], axis=1).astype(BF16)
        y_ref[...] = y
        z = DEEPNORM_ALPHA * h + _dot(y, wo_ref[...])
        zh, rstd = _ln_hat(z)
        err = zh * gp_ref[...] + bp_ref[...] - tg_ref[...]
        rows = jnp.sum(err * err, axis=1, keepdims=True)
        loss_ref[...] += jnp.broadcast_to(jnp.sum(rows, axis=0, keepdims=True) * (0.5 / D_MODEL), (1, LANES))
        dout = err * (1.0 / D_MODEL)
        dgp_ref[...] += _colsum(dout * zh)
        dbp_ref[...] += _colsum(dout)
        dz = _ln_bwd_rows(dout * gp_ref[...], zh, rstd)
        dz_ref[...] = dz
        dy = _dot_nt(dz.astype(BF16), wo_ref[...])

        def gated_bwd(dyg, xh, r, n, sg, gate, gain, width, dgain_ref):
            dn = dyg * (gate * sg)
            dgate = dyg * n * (sg * (1.0 + gate * (1.0 - sg)))
            dgain_ref[...] += _colsum(dn * xh)
            return _rms_bwd(dn * gain, xh, r, width), dgate

        dya, dgate_a = gated_bwd(dy[:, 0:1024], xh_a, r_a, n_a, sg_a, gate_a, goa_ref[...], A_WIDTH, dgoa_ref)
        dyb, dgate_b = gated_bwd(dy[:, 1024:1536], xh_b, r_b, n_b, sg_b, gate_b, gob_ref[...], 512, dgob_ref)
        dym, dgate_m = gated_bwd(dy[:, 1536:2048], xh_m, r_m, n_m, sg_m, gate_m, gom_ref[...], 512, dgom_ref)
        doa_ref[...] = dya.astype(BF16)
        dom_ref[...] = dym.astype(BF16)
        dga_ref[...] = dgate_a.astype(BF16)
        dgb_ref[...] = dgate_b.astype(BF16)
        dgm_ref[...] = dgate_m.astype(BF16)
        for j in range(4):
            blk = dyb[:, j * LANES:(j + 1) * LANES]
            dob_ref[:, 2 * j * LANES:(2 * j + 1) * LANES] = jnp.where(low, 0.0, pltpu.roll(blk, 64, 1)).astype(BF16)
            dob_ref[:, (2 * j + 1) * LANES:(2 * j + 2) * LANES] = jnp.where(low, 0.0, blk).astype(BF16)

    def col(width, idx):
        return pl.BlockSpec((tm, width), lambda i: (i, idx))

    def full(shape):
        return pl.BlockSpec(shape, lambda i: (0, 0))

    def acc(width):
        return jax.ShapeDtypeStruct((1, width), F32)

    return pl.pallas_call(
        body, name="post", grid=(t // tm,),
        out_shape=(jax.ShapeDtypeStruct((t, 2048), BF16), jax.ShapeDtypeStruct((t, 1024), F32),
                   jax.ShapeDtypeStruct((t, 1024), BF16), jax.ShapeDtypeStruct((t, 1024), BF16),
                   jax.ShapeDtypeStruct((t, 512), BF16),
                   jax.ShapeDtypeStruct((t, 1024), BF16), jax.ShapeDtypeStruct((t, 512), BF16),
                   jax.ShapeDtypeStruct((t, 512), BF16),
                   acc(LANES), acc(1024), acc(1024), acc(1024), acc(512), acc(512)),
        in_specs=[col(1024, 0), col(1024, 0), col(1024, 0), col(512, 0),
                  col(1024, 3), col(512, COL_BG // 512), col(512, COL_MG // 512), col(1024, 0),
                  full((2048, 1024)),
                  full((1, 1024)), full((1, 1024)), full((1, 1024)), full((1, 512)), full((1, 512)),
                  full((1, 1024)), full((1, 1024))],
        out_specs=(col(2048, 0), col(1024, 0), col(1024, 0), col(1024, 0), col(512, 0),
                   col(1024, 0), col(512, 0), col(512, 0),
                   full((1, LANES)), full((1, 1024)), full((1, 1024)), full((1, 1024)), full((1, 512)),
                   full((1, 512))),
        compiler_params=_params(("arbitrary",)),
    )(x, ya, ybp, ym, proj, proj, proj, target, w_out, g_emb, b_emb, g_a, g_b, g_m, g_post, b_post)


def _prep_bwd(dqa, dka, dva, dqb, dkb, dvb, dqm, dga, dgb, dgm, proj, trig, w_uq, w_ukv, g_cq, g_ckv,
              rope_a, rope_b, tm=256):
    t = proj.shape[0]

    def body(dqa_ref, dka_ref, dva_ref, dqb_ref, dkb_ref, dvb_ref, dqm_ref, dga_ref, dgb_ref, dgm_ref,
             bs_ref, trig_ref, wuq_ref, wukv_ref, gcq_ref, gckv_ref, ra_ref, rb_ref,
             dproj_ref, dqf_ref, dkv_ref, dgcq_ref, dgckv_ref):
        i = pl.program_id(0)

        @pl.when(i == 0)
        def _():
            dgcq_ref[...] = jnp.zeros_like(dgcq_ref)
            dgckv_ref[...] = jnp.zeros_like(dgckv_ref)

        ta = _rope_tables(trig_ref[:, 0:LANES], trig_ref[:, LANES:2 * LANES], ra_ref[...])
        tb = _rope_tables(trig_ref[:, 2 * LANES:3 * LANES], trig_ref[:, 3 * LANES:4 * LANES], rb_ref[...])
        for j in range(A_WIDTH // LANES):
            sl = slice(j * LANES, (j + 1) * LANES)
            dproj_ref[:, j * LANES:(j + 1) * LANES] = (
                _rope(dqa_ref[:, sl].astype(F32), ta, 8, inverse=True).astype(BF16))
            dproj_ref[:, 1024 + j * LANES:1024 + (j + 1) * LANES] = (
                _rope(dka_ref[:, sl].astype(F32), ta, 8, inverse=True).astype(BF16))
        dproj_ref[:, 2048:3072] = dva_ref[...]
        dproj_ref[:, 3072:4096] = dga_ref[...]

        lane = lax.broadcasted_iota(jnp.int32, (1, LANES), 1)
        low = lane < 64
        rope_lanes = (lane >= 64) & (lane < 96)
        dkr = jnp.zeros((tm, LANES), F32)
        for h in range(MLA_HEADS):
            sl = slice(h * LANES, (h + 1) * LANES)
            dqf_ref[:, sl] = _rope(dqb_ref[:, sl].astype(F32), tb, 16, inverse=True).astype(BF16)
            dk_h = dkb_ref[:, sl]
            dkv_ref[:, sl] = jnp.where(low, dk_h, dvb_ref[:, sl])
            dkr = dkr + jnp.where(rope_lanes, dk_h.astype(F32), 0.0)
        dkr = _rope(dkr, tb, 16, inverse=True)

        cq_hat, r_q = _rms_hat(bs_ref[:, 0:MLA_Q_RANK], MLA_Q_RANK)
        dcqn = _dot(dqf_ref[...], wuq_ref[...])
        dgcq_ref[...] += _colsum(dcqn * cq_hat)
        dproj_ref[:, COL_CQ:COL_CQ + 256] = _rms_bwd(dcqn * gcq_ref[...], cq_hat, r_q, MLA_Q_RANK).astype(BF16)
        ckv_hat, r_kv = _rms_hat(bs_ref[:, MLA_Q_RANK:MLA_Q_RANK + MLA_KV_RANK], MLA_KV_RANK)
        dckvn = _dot_nt(dkv_ref[...], wukv_ref[...])
        dgckv_ref[...] += _colsum(dckvn * ckv_hat)
        dproj_ref[:, COL_CQ + 256:COL_CQ + 384] = (
            _rms_bwd(dckvn * gckv_ref[...], ckv_hat, r_kv, MLA_KV_RANK).astype(BF16))
        dproj_ref[:, COL_CQ + 384:COL_CQ + 512] = dkr.astype(BF16)
        dproj_ref[:, COL_BG:COL_BG + 512] = dgb_ref[...]
        dproj_ref[:, COL_MQ:COL_MQ + 512] = dqm_ref[...]
        dproj_ref[:, COL_MG:COL_MG + 512] = dgm_ref[...]

    def col(width, idx):
        return pl.BlockSpec((tm, width), lambda i: (i, idx))

    def full(shape):
        return pl.BlockSpec(shape, lambda i: (0, 0))

    return pl.pallas_call(
        body, name="prep_bwd", grid=(t // tm,),
        out_shape=(jax.ShapeDtypeStruct((t, PROJ_W), BF16), jax.ShapeDtypeStruct((t, 1024), BF16),
                   jax.ShapeDtypeStruct((t, 1024), BF16),
                   jax.ShapeDtypeStruct((1, MLA_Q_RANK), F32), jax.ShapeDtypeStruct((1, MLA_KV_RANK), F32)),
        in_specs=[col(1024, 0)] * 6 + [col(512, 0), col(1024, 0), col(512, 0), col(512, 0),
                  col(512, COL_CQ // 512), pl.BlockSpec((tm, 4 * LANES), lambda i: (i, 0)),
                  full((1024, MLA_Q_RANK)), full((MLA_KV_RANK, 1024)),
                  full((1, MLA_Q_RANK)), full((1, MLA_KV_RANK)), full((8, LANES)), full((8, LANES))],
        out_specs=(col(PROJ_W, 0), col(1024, 0), col(1024, 0), full((1, MLA_Q_RANK)), full((1, MLA_KV_RANK))),
        compiler_params=_params(("arbitrary",)),
    )(dqa, dka, dva, dqb, dkb, dvb, dqm, dga, dgb, dgm, proj, trig, w_uq, w_ukv, g_cq, g_ckv, rope_a, rope_b)


def _adamw_math(gv, w, m, v):
    m_new = ADAM_B1 * m + (1.0 - ADAM_B1) * gv
    v_new = ADAM_B2 * v + (1.0 - ADAM_B2) * (gv * gv)
    m_hat = m_new / (1.0 - ADAM_B1 ** ADAM_STEP)
    v_hat = v_new / (1.0 - ADAM_B2 ** ADAM_STEP)
    return -ADAM_LR * (m_hat / (jnp.sqrt(v_hat) + ADAM_EPS) + ADAM_WD * w), m_new, v_new


def _adamw(g, w, m, v, tr, name):
    r, cols = w.shape

    def body(g_ref, w_ref, m_ref, v_ref, go_ref, d_ref, nm_ref, nv_ref):
        gv = g_ref[...]
        go_ref[...] = gv
        d_ref[...], nm_ref[...], nv_ref[...] = _adamw_math(gv, w_ref[...], m_ref[...], v_ref[...])

    tile = pl.BlockSpec((tr, cols), lambda i: (i, 0))
    shape = jax.ShapeDtypeStruct((r, cols), F32)
    return pl.pallas_call(
        body, name=name, grid=(r // tr,),
        out_shape=(shape,) * 4, in_specs=[tile] * 4, out_specs=(tile,) * 4,
        compiler_params=_params(("parallel",)),
    )(g, w, m, v)


def _adamw_pieces(g, w, m, v, pieces, name):
    shapes = [jax.ShapeDtypeStruct((r1 - r0, c1 - c0), F32) for r0, r1, c0, c1 in pieces]

    def body(g_ref, w_ref, m_ref, v_ref, *outs):
        gv = g_ref[...]
        results = (gv,) + _adamw_math(gv, w_ref[...], m_ref[...], v_ref[...])
        for kind, full in enumerate(results):
            for p, (r0, r1, c0, c1) in enumerate(pieces):
                outs[kind * len(pieces) + p][...] = full[r0:r1, c0:c1]

    flat = pl.pallas_call(
        body, name=name, out_shape=tuple(shapes) * 4,
        in_specs=[IN_VMEM] * 4, out_specs=tuple([IN_VMEM] * (4 * len(pieces))),
        compiler_params=_params(None),
    )(g, w, m, v)
    return [[flat[kind * len(pieces) + p] for kind in range(4)] for p in range(len(pieces))]


def _core_sum(g, recv, core, rows, tr, name, ride=None):
    cols = g.shape[2]
    nblk = rows // tr
    n_in = len(ride.args) if ride else 0
    n_out = len(ride.out_shapes) if ride else 0

    def body(c_ref, g_ref, r_ref, *rest):
        sf_ref, sb_ref = rest[n_in], rest[n_in + 1]
        if ride:
            j, i = pl.program_id(0), pl.program_id(1)
            ride.run(j * nblk + i, 4 * nblk, rest[:n_in], rest[n_in + 2:n_in + 2 + n_out],
                     rest[n_in + 2 + n_out:])
        tot = g_ref[...] + r_ref[...]
        sf_ref[...] = tot
        sb_ref[...] = tot.astype(BF16)

    half = pl.BlockSpec((None, tr, cols), lambda j, i, c_ref: (j, i, 0))
    shapes = (jax.ShapeDtypeStruct((4, rows, cols), F32), jax.ShapeDtypeStruct((4, rows, cols), BF16))
    return pl.pallas_call(
        body, name=name,
        grid_spec=pltpu.PrefetchScalarGridSpec(
            num_scalar_prefetch=1, grid=(4, nblk),
            in_specs=[pl.BlockSpec((None, tr, cols), lambda j, i, c_ref: (j, c_ref[0] * nblk + i, 0)), half]
            + (ride.in_specs if ride else []),
            out_specs=(half, half) + (ANY,) * n_out,
            scratch_shapes=ride.scratch() if ride else []),
        out_shape=shapes + tuple(ride.out_shapes if ride else ()),
        compiler_params=_params(("arbitrary", "arbitrary") if ride else ("parallel", "parallel")),
    )(core, g, recv, *(ride.args if ride else ()))


def _half_to_sibling(g4):
    def plan(in_refs, out_refs, send_sems, recv_sems):
        x, y, c = _position()
        cp = pltpu.make_async_remote_copy(
            src_ref=in_refs[0].at[:, 1 - c], dst_ref=out_refs[0], send_sem=send_sems.at[0],
            recv_sem=recv_sems.at[0], device_id=(x, y, 1 - c), device_id_type=MESH)

        def finish():
            cp.wait_recv()
            cp.wait_send()

        return cp.start, finish

    return _Ride([g4], [jax.ShapeDtypeStruct((4, g4.shape[2], 1024), F32)], (1, 1), plan)


def _gather_plan(src_ref, dst_ref, send_sems, recv_sems, local_sems):
    x, y, c = _position()
    me = 2 * x + y
    rows = src_ref.shape[1]
    cut = -(-rows // 32) * 16
    pieces = (pl.ds(0, cut), pl.ds(cut, rows - cut))
    local = pltpu.make_async_copy(src_ref, dst_ref.at[me], local_sems.at[0])

    def over_ici(sem, k, chip, t, src=None):
        where = dst_ref.at[chip, c, pieces[t]]
        return pltpu.make_async_remote_copy(
            src_ref=where if src is None else src, dst_ref=where, send_sem=send_sems.at[sem],
            recv_sem=recv_sems.at[sem], device_id=(x ^ (k >> 1), y ^ (k & 1), c), device_id_type=MESH)

    def mine_to(k, t):
        return over_ici(2 * (k - 1) + t, k, me, t, src=src_ref.at[c, pieces[t]])

    def from_neighbour(k, t):
        return over_ici(2 * (k - 1) + t, k, me ^ k, t)

    def to_sibling(k, half):
        piece = dst_ref.at[me ^ k, half]
        return pltpu.make_async_remote_copy(
            src_ref=piece, dst_ref=piece, send_sem=send_sems.at[5 + k], recv_sem=recv_sems.at[5 + k],
            device_id=(x, y, 1 - c), device_id_type=MESH)

    sends = [mine_to(2, 0), mine_to(1, 1), mine_to(2, 1), mine_to(1, 0)]
    onward = [over_ici(4, 1, me ^ 2, 0), over_ici(5, 2, me ^ 1, 1)]

    def start():
        local.start()
        for cp in sends:
            cp.start()

    def pass_on():
        from_neighbour(2, 0).wait_recv()
        onward[0].start()
        from_neighbour(1, 1).wait_recv()
        onward[1].start()

    def to_other_core():
        from_neighbour(2, 1).wait_recv()
        to_sibling(2, c).start()
        from_neighbour(1, 0).wait_recv()
        to_sibling(1, c).start()
        over_ici(4, 1, me ^ 3, 0).wait_recv()
        over_ici(5, 2, me ^ 3, 1).wait_recv()
        to_sibling(3, c).start()

    def finish():
        for k in (1, 2, 3):
            to_sibling(k, 1 - c).wait_recv()
        for cp in sends + onward + [to_sibling(k, c) for k in (1, 2, 3)]:
            cp.wait_send()
        local.wait()

    return start, pass_on, to_other_core, finish


def _gather_ride(shard, spread):
    def plan(in_refs, out_refs, send_sems, recv_sems, local_sems):
        return _gather_plan(in_refs[0], out_refs[0], send_sems, recv_sems, local_sems)

    return _Ride([shard], [jax.ShapeDtypeStruct((4,) + shard.shape, shard.dtype)], (9, 9, 1), plan,
                 in_specs=[IN_VMEM], spread=spread)


def _chip_sum(sf, recv, chip, rows, tr, name):
    cols = sf.shape[2]

    def body(me_ref, sf_ref, r_ref, out_ref):
        acc = sf_ref[...]
        for k in range(3):
            acc = acc + r_ref[k].astype(F32)
        out_ref[...] = acc

    return pl.pallas_call(
        body, name=name,
        grid_spec=pltpu.PrefetchScalarGridSpec(
            num_scalar_prefetch=1, grid=(rows // tr,),
            in_specs=[pl.BlockSpec((None, tr, cols), lambda i, me_ref: (me_ref[0], i, 0)),
                      pl.BlockSpec((3, tr, cols), lambda i, me_ref: (0, i, 0))],
            out_specs=pl.BlockSpec((tr, cols), lambda i, me_ref: (i, 0))),
        out_shape=jax.ShapeDtypeStruct((rows, cols), F32),
        compiler_params=_params(("parallel",)),
    )(chip, sf, recv)


def _position():
    return lax.axis_index("x"), lax.axis_index("y"), lax.axis_index("c")


def _dh_scatter(dproj, w_in_arr_t, x, dz, g, sb_in, sb_rest, tm=1024, tk=1024):
    t, d = x.shape
    nk = dproj.shape[1] // tk
    ni = t // tm

    def body(dp_ref, w_ref, x_ref, dz_ref, g_ref, sbin_ref, sbrest_ref,
             dx_ref, dg_ref, db_ref, rin_ref, rrest_ref, acc_ref, send_sems, recv_sems):
        i = pl.program_id(0)
        kk = pl.program_id(1)
        px, py, pc = _position()
        me = 2 * px + py
        srcs = (sbin_ref, sbrest_ref)
        dsts = (rin_ref, rrest_ref)

        def copy(a, k):
            return pltpu.make_async_remote_copy(
                src_ref=srcs[a].at[me ^ k], dst_ref=dsts[a].at[k - 1],
                send_sem=send_sems.at[3 * a + k - 1], recv_sem=recv_sems.at[3 * a + k - 1],
                device_id=(px ^ (k >> 1), py ^ (k & 1), pc), device_id_type=MESH)

        pairs = [(a, k) for a in range(2) for k in (1, 2, 3)]

        @pl.when((i == 0) & (kk == 0))
        def _():
            dg_ref[...] = jnp.zeros_like(dg_ref)
            db_ref[...] = jnp.zeros_like(db_ref)
            for a, k in pairs:
                copy(a, k).start()

        part = _dot(dp_ref[...], w_ref[...])

        @pl.when(kk == 0)
        def _():
            acc_ref[...] = part

        @pl.when(kk > 0)
        def _():
            acc_ref[...] += part

        @pl.when(kk == nk - 1)
        def _():
            xh, rstd = _ln_hat(x_ref[...])
            dht = acc_ref[...] + DEEPNORM_ALPHA * dz_ref[...]
            dg_ref[...] += _colsum(dht * xh)
            db_ref[...] += _colsum(dht)
            dx_ref[...] = _ln_bwd_rows(dht * g_ref[...], xh, rstd)

        @pl.when((i == ni - 1) & (kk == nk - 1))
        def _():
            for a, k in pairs:
                copy(a, k).wait_recv()
            for a, k in pairs:
                copy(a, k).wait_send()

    tile = pl.BlockSpec((tm, d), lambda i, kk: (i, 0))
    row = pl.BlockSpec((1, d), lambda i, kk: (0, 0))
    return pl.pallas_call(
        body, name="dh_scatter", grid=(ni, nk),
        out_shape=(jax.ShapeDtypeStruct((t, d), F32), jax.ShapeDtypeStruct((1, d), F32),
                   jax.ShapeDtypeStruct((1, d), F32),
                   jax.ShapeDtypeStruct((3, HALF_IN, 1024), BF16),
                   jax.ShapeDtypeStruct((3, HALF_REST, 1024), BF16)),
        in_specs=[pl.BlockSpec((tm, tk), lambda i, kk: (i, kk)), pl.BlockSpec((tk, d), lambda i, kk: (kk, 0)),
                  tile, tile, row, ANY, ANY],
        out_specs=(tile, row, row, ANY, ANY),
        scratch_shapes=[pltpu.VMEM((tm, d), F32), pltpu.SemaphoreType.DMA((6,)), pltpu.SemaphoreType.DMA((6,))],
        compiler_params=_params(("arbitrary", "arbitrary")),
    )(dproj, w_in_arr_t, x, dz, g, sb_in, sb_rest)


def _join_halves(gh_in, gh_rest):
    def body(hin_ref, hrest_ref, oin_ref, orest_ref, send_sems, recv_sems, local_sems):
        x, y, c = _position()
        srcs = (hin_ref, hrest_ref)
        dsts = (oin_ref, orest_ref)

        def rows(a, half):
            return dsts[a].at[half]

        local = [pltpu.make_async_copy(srcs[a], rows(a, c), local_sems.at[a]) for a in range(2)]
        remote = [pltpu.make_async_remote_copy(
            src_ref=srcs[a], dst_ref=rows(a, c), send_sem=send_sems.at[a], recv_sem=recv_sems.at[a],
            device_id=(x, y, 1 - c), device_id_type=MESH) for a in range(2)]
        for cp in local + remote:
            cp.start()
        for a in range(2):
            pltpu.make_async_remote_copy(
                src_ref=srcs[a], dst_ref=rows(a, 1 - c), send_sem=send_sems.at[a], recv_sem=recv_sems.at[a],
                device_id=(x, y, 1 - c), device_id_type=MESH).wait_recv()
        for cp in remote:
            cp.wait_send()
        for cp in local:
            cp.wait()

    return pl.pallas_call(
        body, name="join_halves",
        out_shape=(jax.ShapeDtypeStruct((2, HALF_IN, 1024), F32),
                   jax.ShapeDtypeStruct((2, HALF_REST, 1024), F32)),
        in_specs=[IN_VMEM, IN_VMEM], out_specs=(ANY, ANY),
        scratch_shapes=[pltpu.SemaphoreType.DMA((2,)), pltpu.SemaphoreType.DMA((2,)), pltpu.SemaphoreType.DMA((2,))],
    )(gh_in, gh_rest)


def _allreduce_small(vec):
    def body(vec_ref, out_ref, all_ref, send_sems, recv_sems):
        x, y, c = _position()
        me = 4 * x + 2 * y + c
        all_ref[me] = vec_ref[...]

        def copy(k, slot):
            return pltpu.make_async_remote_copy(
                src_ref=vec_ref, dst_ref=all_ref.at[slot], send_sem=send_sems.at[k - 1], recv_sem=recv_sems.at[k - 1],
                device_id=(x ^ (k >> 2), y ^ ((k >> 1) & 1), c ^ (k & 1)), device_id_type=MESH)

        copies = [copy(k, me) for k in range(1, 8)]
        for cp in copies:
            cp.start()
        for k in range(1, 8):
            copy(k, me ^ k).wait_recv()
        for cp in copies:
            cp.wait_send()
        total = all_ref[0]
        for d in range(1, 8):
            total = total + all_ref[d]
        out_ref[...] = total

    return pl.pallas_call(
        body, name="allreduce_small",
        out_shape=jax.ShapeDtypeStruct(vec.shape, vec.dtype),
        in_specs=[pl.BlockSpec(memory_space=pltpu.VMEM)], out_specs=pl.BlockSpec(memory_space=pltpu.VMEM),
        scratch_shapes=[pltpu.VMEM((8,) + vec.shape, vec.dtype), pltpu.SemaphoreType.DMA((7,)),
                        pltpu.SemaphoreType.DMA((7,))],
    )(vec)


def _pack_rest(w_uq, w_ukv, w_mem, w_out):
    rows = jnp.concatenate([w_uq[0].T.reshape(-1, 1024), w_ukv.reshape(-1, 1024), w_mem.reshape(-1, 1024),
                            w_out.reshape(-1, 1024)], axis=0)
    return jnp.pad(rows, ((0, ROWS_REST - ROWS_USED), (0, 0)))


def _arranged_w_in(g_in):
    z = functools.partial(jnp.zeros, dtype=g_in.dtype)
    cut = 4480 - 2 * SHARD_ROWS
    return jnp.concatenate(
        [g_in[0, :SHARD_ROWS], g_in[1, :SHARD_ROWS], g_in[2, :cut], z((64, 1024)), g_in[2, cut:cut + 32],
         z((32, 1024)), g_in[2, cut + 32:SHARD_ROWS], g_in[3, :SHARD_ROWS]], axis=0)


def _rest_weights(g_rest):
    w_uq_t = g_rest[:, 0:ROWS_UQ].reshape(768, 256)
    w_uq_pad_t = jnp.pad(w_uq_t.reshape(MLA_HEADS, MLA_QK_DIM, 256), ((0, 0), (0, 32), (0, 0))).reshape(1024, 256)
    w_ukv = jnp.concatenate([g_rest[j, ROWS_UQ:ROWS_UQ + ROWS_UKV].reshape(128, 256) for j in range(4)], axis=1)
    lo = ROWS_UQ + ROWS_UKV
    w_mem = g_rest[:, lo:lo + ROWS_MEM].reshape(4 * ROWS_MEM, 1024)
    w_out = g_rest[:, lo + ROWS_MEM:lo + ROWS_MEM + ROWS_OUT].reshape(4 * ROWS_OUT, 1024)
    return w_uq_pad_t, w_ukv, w_mem, w_out


def _split_in(dw_in_arr_t):
    a = dw_in_arr_t
    gap = jnp.zeros((ROWS_IN - SHARD_ROWS, 1024), a.dtype)
    nat = 4608 - 96
    pieces = [a[:SHARD_ROWS], gap, a[SHARD_ROWS:2 * SHARD_ROWS], gap,
              a[2 * SHARD_ROWS:4480], a[4544:4576], a[4608:4608 + 3 * SHARD_ROWS - nat], gap,
              a[4608 + 3 * SHARD_ROWS - nat:], gap]
    return jnp.concatenate(pieces, axis=0).reshape(4, ROWS_IN, 1024)


def _split_rest(dw_uq_pad_t, dw_ukv, dw_mem, dw_out):
    dw_uq_t = dw_uq_pad_t.reshape(MLA_HEADS, LANES, 256)[:, :MLA_QK_DIM].reshape(4, ROWS_UQ, 1024)
    parts = [dw_uq_t, dw_ukv.reshape(128, 4, 256).transpose(1, 0, 2).reshape(4, ROWS_UKV, 1024),
             dw_mem.reshape(4, ROWS_MEM, 1024), dw_out.reshape(4, ROWS_OUT, 1024)]
    return jnp.pad(jnp.concatenate(parts, axis=1), ((0, 0), (0, ROWS_REST - ROWS_USED), (0, 0)))


def _rope_consts(rot, first, period):
    half = rot // 2
    inv_freq = np.float32(ROPE_THETA) ** (-(np.arange(0, rot, 2, dtype=np.float32) / np.float32(rot)))
    lane = np.arange(LANES) % period - first
    in_rot = (lane >= 0) & (lane < rot)
    out = np.zeros((8, LANES), np.float32)
    out[0] = np.where(in_rot, inv_freq[np.clip(lane, 0, rot - 1) % half], 0.0)
    out[1] = in_rot & (lane < half)
    out[2] = in_rot & (lane >= half)
    return jnp.asarray(out)


def _band_bias(s):
    nblk = s // BAND_Q
    starts = np.array([_band_start(i, s) for i in range(nblk)])
    uq = (np.arange(nblk)[:, None] * BAND_Q + np.arange(BAND_Q)[None, :])[:, :, None]
    uk = (starts[:, None] + np.arange(BAND_WIN)[None, :])[:, None, :]
    tiles, index, seen = [], [], {}
    for _, d in DILATED:
        length = s // d
        ok = (uq // length == uk // length) & (np.abs(uq - uk) <= 64)
        row = []
        for i in range(nblk):
            key = ok[i].tobytes()
            if key not in seen:
                seen[key] = len(tiles)
                tiles.append(np.where(ok[i], 0.0, NEG_INF).astype(np.float32))
            row.append(seen[key])
        index.append(row)
    return jnp.asarray(np.stack(tiles, axis=0)), index


def _forward_backward(h, proj, trig, rope_consts, x, mem, target, weights, gains):
    w_uq_pad_t, w_ukv, w_mem, w_out = weights
    g_emb, b_emb, g_cq, g_ckv, g_out_a, g_out_b, g_out_m, g_post, b_post = gains
    nb, s, d = x.shape
    t = nb * s
    x2 = x.reshape(t, d)
    mem2 = mem.reshape(nb * N_MEM, d)
    tgt2 = target.reshape(t, d)
    rope_a, rope_b = rope_consts
    bias, bias_index = _band_bias(s)
    scales = (0.125, MLA_QK_DIM ** -0.5, 128 ** -0.5)

    qa, ka, va, qb, kb, vb, qm, cqn, ckvn = _prep(proj, trig, w_uq_pad_t, w_ukv, g_cq, g_ckv, rope_a, rope_b, scales)
    mkv = _mm(mem2, w_mem, BF16, nb * N_MEM, 1024, 1024, "mem_kv")

    cfg_b = dict(nb=nb, s=s, sk=s, heads=8, voff=0, bq=256)
    cfg_m = dict(nb=nb, s=s, sk=N_MEM, heads=4, hpb=2, voff=4, bq=1024)
    ya, lse_a, qkv_ordered = _dilated_fwd(qa, ka, va, bias, bias_index, nb=nb, s=s, name="attn_a_fwd")
    yb, lse_b = _attn_fwd(qb, kb, vb, name="attn_b_fwd", hpb=4, **cfg_b)
    ym, lse_m = _attn_fwd(qm, mkv, mkv, name="attn_m_fwd", **cfg_m)

    (y, dz, doa, dob, dom, dga, dgb, dgm, loss, dg_post, db_post, dg_a, dg_b, dg_m) = _post(
        x2, ya, yb, ym, proj, tgt2, w_out, g_emb, b_emb, g_out_a, g_out_b, g_out_m, g_post, b_post)

    dqa, dka, dva = _dilated_bwd(qa, ka, va, qkv_ordered, ya, doa, lse_a, bias, bias_index, nb=nb, s=s, scale=scales[0],
                                 name="attn_a_bwd")
    dqb, dkb, dvb = _attn_bwd(qb, kb, vb, yb, dob, lse_b, name="attn_b_bwd", scale=scales[1], hpb=4, **cfg_b)
    dqm, dmk, dmv = _attn_bwd(qm, mkv, mkv, ym, dom, lse_m, name="attn_m_bwd", scale=scales[2], **cfg_m)
    dmkv = jnp.concatenate([dmk, dmv], axis=1)

    dproj, dqf, dkv, dg_cq, dg_ckv = _prep_bwd(
        dqa, dka, dva, dqb, dkb, dvb, dqm, dga, dgb, dgm, proj, trig, w_uq_pad_t, w_ukv, g_cq, g_ckv, rope_a, rope_b)

    small_rows = (dg_cq, dg_ckv, loss, dg_a, dg_b, dg_m, dg_post, db_post)
    return (dproj, h, y, dz, dqf, cqn, ckvn, dkv, mem2, dmkv), x2, small_rows


def _weight_grads(operands, core):
    dproj, h, y, dz, dqf, cqn, ckvn, dkv, mem2, dmkv = operands
    dw_in_arr_t = _mm(dproj, h, F32, 1024, 1024, 4096, "dw_in", mode="tn")
    g_in = _split_in(dw_in_arr_t)
    dw_out, r_in = _mm(y, dz, F32, 1024, 1024, 2048, "dw_out", mode="tn",
                       ride=_half_to_sibling(g_in.reshape(4, 2, HALF_IN, 1024)))
    dw_uq_pad_t = _mm(dqf, cqn, F32, 1024, 256, 4096, "dw_uq", mode="tn")
    dw_ukv = _mm(ckvn, dkv, F32, 128, 1024, 4096, "dw_ukv", mode="tn")
    dw_mem = _mm(mem2, dmkv, F32, 1024, 1024, mem2.shape[0], "dw_mem", mode="tn")
    g_rest = _split_rest(dw_uq_pad_t, dw_ukv, dw_mem, dw_out)
    sf_in, sb_in, r_rest = _core_sum(g_in, r_in, core, HALF_IN, HALF_IN // 2, "core_sum_in",
                                     ride=_half_to_sibling(g_rest.reshape(4, 2, HALF_REST, 1024)))
    sf_rest, sb_rest = _core_sum(g_rest, r_rest, core, HALF_REST, HALF_REST, "core_sum_rest")
    return sf_in, sb_in, sf_rest, sb_rest


def _small_block(dg_emb, db_emb, small_rows):
    dg_cq, dg_ckv, loss, dg_a, dg_b, dg_m, dg_post, db_post = small_rows
    row2 = jnp.concatenate([dg_cq, dg_ckv, loss, jnp.zeros((1, 512), F32)], axis=1)
    return jnp.concatenate([dg_emb, db_emb, row2, dg_a, jnp.concatenate([dg_b, dg_m], axis=1), dg_post, db_post,
                            jnp.zeros((1, 1024), F32)], axis=0)


def _pack_small(g_emb, b_emb, g_cq, g_ckv, g_out_a, g_out_b, g_out_m, g_post, b_post):
    row2 = jnp.concatenate([g_cq.reshape(1, -1), g_ckv.reshape(1, -1), jnp.zeros((1, 640), F32)], axis=1)
    return jnp.concatenate([g_emb.reshape(1, -1), b_emb.reshape(1, -1), row2, g_out_a.reshape(1, -1),
                            jnp.concatenate([g_out_b.reshape(1, -1), g_out_m.reshape(1, -1)], axis=1),
                            g_post.reshape(1, -1), b_post.reshape(1, -1), jnp.zeros((1, 1024), F32)], axis=0)


def kernel(x, mem, positions, g_emb, b_emb, w_in, g_cq, g_ckv, w_uq, w_ukv, w_mem_kv, g_out_a, g_out_b, g_out_m, w_out, g_post, b_post, loss_target, m_g_emb, m_b_emb, m_w_in, m_g_cq, m_g_ckv, m_w_uq, m_w_ukv, m_w_mem_kv, m_g_out_a, m_g_out_b, m_g_out_m, m_w_out, m_g_post, m_b_post, v_g_emb, v_b_emb, v_w_in, v_g_cq, v_g_ckv, v_w_uq, v_w_ukv, v_w_mem_kv, v_g_out_a, v_g_out_b, v_g_out_m, v_w_out, v_g_post, v_b_post):
    w_rest = _pack_rest(w_uq, w_ukv, w_mem_kv, w_out)
    w_in_t = w_in[0].T
    w_in_b = jnp.pad(w_in_t.astype(BF16), ((0, ROWS_IN - SHARD_ROWS), (0, 0)))
    gains = (g_emb.reshape(1, -1), b_emb.reshape(1, -1), g_cq, g_ckv, g_out_a, g_out_b, g_out_m, g_post, b_post)
    rope_consts = (_rope_consts(16, 0, 64), _rope_consts(32, 64, 128))
    h, trig, gathered_in = _ln_fwd(x.reshape(-1, D_MODEL), gains[0], gains[1],
                                   positions.reshape(-1, 1).astype(F32), *rope_consts,
                                   ride=_gather_ride(w_in_b.reshape(2, HALF_IN, 1024), spread=False))
    w_in_arr_t = _arranged_w_in(gathered_in.reshape(4, ROWS_IN, 1024))
    proj, gathered_rest = _mm(h, w_in_arr_t, F32, 1024, 2048, 1024, "in_proj", mode="nt",
                              ride=_gather_ride(w_rest.astype(BF16).reshape(2, HALF_REST, 1024), spread=True))
    weights = _rest_weights(gathered_rest.reshape(4, ROWS_REST, 1024))
    operands, x2, small_rows = _forward_backward(h, proj, trig, rope_consts, x, mem, loss_target, weights, gains)

    core = lax.axis_index("c").astype(jnp.int32).reshape(1)
    chip = (2 * lax.axis_index("x") + lax.axis_index("y")).astype(jnp.int32).reshape(1)
    sf_in, sb_in, sf_rest, sb_rest = _weight_grads(operands, core)
    grad_x, dg_emb, db_emb, rb_in, rb_rest = _dh_scatter(operands[0], w_in_arr_t, x2, operands[3], gains[0],
                                                         sb_in, sb_rest)
    gh_in = _chip_sum(sf_in, rb_in, chip, HALF_IN, HALF_IN // 2, "chip_sum_in")
    gh_rest = _chip_sum(sf_rest, rb_rest, chip, HALF_REST, HALF_REST, "chip_sum_rest")
    grad_in, grad_rest = _join_halves(gh_in, gh_rest)
    grad_in = grad_in.reshape(ROWS_IN, 1024)
    grad_rest = grad_rest.reshape(ROWS_REST, 1024)

    big_in = _adamw(grad_in, w_in_t, m_w_in[0].T, v_w_in[0].T, SHARD_ROWS // 3, "adamw_in")
    uq, ukv, wmem, wout = _adamw_pieces(
        grad_rest, w_rest, _pack_rest(m_w_uq, m_w_ukv, m_w_mem_kv, m_w_out),
        _pack_rest(v_w_uq, v_w_ukv, v_w_mem_kv, v_w_out), REST_PIECES, "adamw_rest")
    small_sum = _allreduce_small(_small_block(dg_emb, db_emb, small_rows))
    sm = _adamw_pieces(
        small_sum,
        _pack_small(g_emb, b_emb, g_cq, g_ckv, g_out_a, g_out_b, g_out_m, g_post, b_post),
        _pack_small(m_g_emb, m_b_emb, m_g_cq, m_g_ckv, m_g_out_a, m_g_out_b, m_g_out_m, m_g_post, m_b_post),
        _pack_small(v_g_emb, v_b_emb, v_g_cq, v_g_ckv, v_g_out_a, v_g_out_b, v_g_out_m, v_g_post, v_b_post),
        SMALL_PIECES, "adamw_small")
    loss = small_sum[2, 384]

    def ordered(kind):
        s_gemb, s_bemb, s_gcq, s_gckv, s_ga, s_gb, s_gm, s_gpost, s_bpost = [piece[kind] for piece in sm]
        return [s_gemb.reshape(-1), s_bemb.reshape(-1), big_in[kind].T[None], s_gcq, s_gckv,
                uq[kind].reshape(192, 256).T[None], ukv[kind].reshape(1, 128, 256), wmem[kind][None], s_ga, s_gb,
                s_gm, wout[kind][None], s_gpost, s_bpost]

    return (loss, grad_x.reshape(x.shape), *ordered(0), *ordered(1), *ordered(2), *ordered(3))
```

```python
import functools
import math

import jax
import jax.numpy as jnp
import numpy as np
from jax import lax
from jax.experimental import pallas as pl
from jax.experimental.pallas import tpu as pltpu

F32 = jnp.float32
BF16 = jnp.bfloat16
MESH = pl.DeviceIdType.MESH
ANY = pl.BlockSpec(memory_space=pl.ANY)
IN_VMEM = pl.BlockSpec(memory_space=pltpu.VMEM)

D_MODEL = 1024
A_WIDTH = 1024
MLA_HEADS = 8
MLA_Q_RANK = 256
MLA_KV_RANK = 128
MLA_QK_DIM = 96
MEM_WIDTH = 512
N_MEM = 256
ROPE_THETA = 500000.0
NORM_EPS = 1e-5
NEG_INF = -1e30
DEEPNORM_ALPHA = 2.0 ** 0.25
DILATED = ((64, 1), (256, 4), (1024, 16))

ADAM_LR = 0.001
ADAM_B1 = 0.9
ADAM_B2 = 0.999
ADAM_EPS = 1e-08
ADAM_WD = 0.01
ADAM_STEP = 10

LANES = 128
VMEM_LIMIT = 56 * 1024 * 1024
LOG2E = math.log2(math.e)
LN2 = math.log(2.0)

PROJ_W = 6144
COL_CQ = 4096
COL_BG = 4608
COL_MQ = 5120
COL_MG = 5632

SHARD_ROWS = 1512
ROWS_IN = 1536
ROWS_UQ, ROWS_UKV, ROWS_MEM, ROWS_OUT = 48, 32, 256, 512
ROWS_USED = ROWS_UQ + ROWS_UKV + ROWS_MEM + ROWS_OUT
ROWS_REST = 864
HALF_IN = ROWS_IN // 2
HALF_REST = ROWS_REST // 2
REST_PIECES = ((0, 48, 0, 1024), (48, 80, 0, 1024), (80, 336, 0, 1024), (336, 848, 0, 1024))
SMALL_PIECES = ((0, 1, 0, 1024), (1, 2, 0, 1024), (2, 3, 0, 256), (2, 3, 256, 384), (3, 4, 0, 1024), (4, 5, 0, 512),
                (4, 5, 512, 1024), (5, 6, 0, 1024), (6, 7, 0, 1024))


def _params(sem=None, vmem=VMEM_LIMIT):
    return pltpu.CompilerParams(dimension_semantics=sem, vmem_limit_bytes=vmem)


def _dot(a, b):
    return jnp.dot(a, b, preferred_element_type=F32)


def _dot_nt(a, b):
    return lax.dot_general(a, b, (((1,), (1,)), ((), ())), preferred_element_type=F32)


def _dot_tn(a, b):
    return lax.dot_general(a, b, (((0,), (0,)), ((), ())), preferred_element_type=F32)


def _ln_hat(x):
    mu = jnp.mean(x, axis=-1, keepdims=True)
    xc = x - mu
    var = jnp.mean(xc * xc, axis=-1, keepdims=True)
    rstd = lax.rsqrt(var + NORM_EPS)
    return xc * rstd, rstd


def _ln_bwd_rows(dxh, xh, rstd):
    return rstd * (dxh - jnp.mean(dxh, axis=-1, keepdims=True) - xh * jnp.mean(dxh * xh, axis=-1, keepdims=True))


def _rms_hat(x, width):
    ms = jnp.sum(x * x, axis=-1, keepdims=True) * (1.0 / width)
    r = lax.rsqrt(ms + NORM_EPS)
    return x * r, r


def _rms_bwd(u, xh, r, width):
    return r * (u - xh * (jnp.sum(u * xh, axis=-1, keepdims=True) * (1.0 / width)))


def _colsum(v):
    return jnp.sum(v, axis=0, keepdims=True)


def _rope_tables(cos, sin, consts):
    return cos, sin * consts[2:3, :], -sin * consts[1:2, :]


def _rope(x, tables, half, inverse=False):
    c, s_up, s_dn = tables
    if inverse:
        s_up, s_dn = -s_up, -s_dn
    return x * c + pltpu.roll(x, half, 1) * s_up + pltpu.roll(x, LANES - half, 1) * s_dn


def _ln_fwd(x, g, b, pos, rope_a, rope_b, tm=512, ride=None):
    t, d = x.shape
    n_in = len(ride.args) if ride else 0
    n_out = len(ride.out_shapes) if ride else 0
    steps = t // tm

    def body(x_ref, g_ref, b_ref, pos_ref, ra_ref, rb_ref, *rest):
        h_ref, trig_ref = rest[n_in], rest[n_in + 1]
        if ride:
            i = pl.program_id(0)
            ride.run(i, steps, rest[:n_in], rest[n_in + 2:n_in + 2 + n_out], rest[n_in + 2 + n_out:])
        xh, _ = _ln_hat(x_ref[...])
        h_ref[...] = (xh * g_ref[...] + b_ref[...]).astype(BF16)
        for j, consts in enumerate((ra_ref, rb_ref)):
            ang = pos_ref[...] * consts[0:1, :]
            trig_ref[:, 2 * j * LANES:(2 * j + 1) * LANES] = jnp.cos(ang)
            trig_ref[:, (2 * j + 1) * LANES:(2 * j + 2) * LANES] = jnp.sin(ang)

    row = pl.BlockSpec((1, d), lambda i: (0, 0))
    tile = pl.BlockSpec((tm, d), lambda i: (i, 0))
    consts = pl.BlockSpec((8, LANES), lambda i: (0, 0))
    trig_tile = pl.BlockSpec((tm, 4 * LANES), lambda i: (i, 0))
    in_specs = [tile, row, row, pl.BlockSpec((tm, 1), lambda i: (i, 0)), consts, consts]
    shapes = (jax.ShapeDtypeStruct((t, d), BF16), jax.ShapeDtypeStruct((t, 4 * LANES), F32))
    if not ride:
        return pl.pallas_call(
            body, name="ln_fwd", grid=(steps,), out_shape=shapes, in_specs=in_specs, out_specs=(tile, trig_tile),
            compiler_params=_params(("parallel",)),
        )(x, g, b, pos, rope_a, rope_b)
    return pl.pallas_call(
        body, name="ln_fwd", grid=(steps,),
        out_shape=(*shapes, *ride.out_shapes),
        in_specs=in_specs + ride.in_specs, out_specs=(tile, trig_tile) + (ANY,) * n_out,
        scratch_shapes=ride.scratch(),
        compiler_params=_params(("arbitrary",)),
    )(x, g, b, pos, rope_a, rope_b, *ride.args)


class _Ride:
    def __init__(self, args, out_shapes, sem_counts, plan, in_specs=None, spread=True):
        self.args, self.out_shapes, self.plan = list(args), list(out_shapes), plan
        self.sem_counts = sem_counts
        self.in_specs = in_specs or [ANY] * len(self.args)
        self.spread = spread

    def scratch(self):
        return [pltpu.SemaphoreType.DMA((n,)) for n in self.sem_counts]

    def run(self, step, total, in_refs, out_refs, sems):
        count = len(self.plan(in_refs, out_refs, *sems))
        at = [(k * (total - 1)) // (count - 1) if self.spread or k == 0 else total - 1 for k in range(count)]
        for when in sorted(set(at)):
            @pl.when(step == when)
            def _(when=when):
                stages = self.plan(in_refs, out_refs, *sems)
                for k in range(count):
                    if at[k] == when:
                        stages[k]()


def _mm(a, b, out_dtype, tm, tn, tk, name, mode="nn", ride=None):
    if mode == "tn":
        k, m = a.shape
    else:
        m, k = a.shape
    n = b.shape[0] if mode == "nt" else b.shape[1]
    nk = k // tk
    nj, ni = n // tn, m // tm
    n_in = len(ride.args) if ride else 0
    n_out = len(ride.out_shapes) if ride else 0

    def body(a_ref, b_ref, *rest):
        o_ref = rest[n_in]
        acc_ref = rest[n_in + 1 + n_out]
        if ride:
            j, i, kk = pl.program_id(0), pl.program_id(1), pl.program_id(2)
            ride.run((j * ni + i) * nk + kk, nj * ni * nk, rest[:n_in], rest[n_in + 1:n_in + 1 + n_out],
                     rest[n_in + 2 + n_out:])
        av = a_ref[...].astype(BF16)
        bv = b_ref[...].astype(BF16)
        part = _dot_tn(av, bv) if mode == "tn" else _dot_nt(av, bv) if mode == "nt" else _dot(av, bv)
        if nk == 1:
            o_ref[...] = part.astype(out_dtype)
        else:
            kk = pl.program_id(2)

            @pl.when(kk == 0)
            def _():
                acc_ref[...] = part

            @pl.when(kk > 0)
            def _():
                acc_ref[...] += part

            @pl.when(kk == nk - 1)
            def _():
                o_ref[...] = acc_ref[...].astype(out_dtype)

    a_spec = (pl.BlockSpec((tk, tm), lambda j, i, kk: (kk, i)) if mode == "tn"
              else pl.BlockSpec((tm, tk), lambda j, i, kk: (i, kk)))
    b_spec = (pl.BlockSpec((tn, tk), lambda j, i, kk: (j, kk)) if mode == "nt"
              else pl.BlockSpec((tk, tn), lambda j, i, kk: (kk, j)))
    o_spec = pl.BlockSpec((tm, tn), lambda j, i, kk: (i, j))
    o_shape = jax.ShapeDtypeStruct((m, n), out_dtype)
    if not ride:
        return pl.pallas_call(
            body, name=name, grid=(nj, ni, nk), out_shape=o_shape, in_specs=[a_spec, b_spec], out_specs=o_spec,
            scratch_shapes=[pltpu.VMEM((tm, tn), F32)],
            compiler_params=_params(("parallel", "parallel", "arbitrary")),
        )(a, b)
    return pl.pallas_call(
        body, name=name, grid=(nj, ni, nk),
        out_shape=(o_shape, *ride.out_shapes),
        in_specs=[a_spec, b_spec] + ride.in_specs,
        out_specs=(o_spec,) + (ANY,) * n_out,
        scratch_shapes=[pltpu.VMEM((tm, tn), F32)] + ride.scratch(),
        compiler_params=_params(("arbitrary", "arbitrary", "arbitrary")),
    )(a, b, *ride.args)


def _prep(proj, trig, w_uq, w_ukv, g_cq, g_ckv, rope_a, rope_b, scales, tm=256):
    t = proj.shape[0]
    sc_a, sc_b, sc_m = (s * LOG2E for s in scales)

    def body(aq_ref, ak_ref, av_ref, bs_ref, mq_ref, trig_ref, wuq_ref, wukv_ref, gcq_ref, gckv_ref,
             ra_ref, rb_ref, qa_ref, ka_ref, va_ref, qb_ref, kb_ref, vb_ref, qm_ref, cqn_ref, ckvn_ref):
        ta = _rope_tables(trig_ref[:, 0:LANES], trig_ref[:, LANES:2 * LANES], ra_ref[...])
        tb = _rope_tables(trig_ref[:, 2 * LANES:3 * LANES], trig_ref[:, 3 * LANES:4 * LANES], rb_ref[...])
        for j in range(A_WIDTH // LANES):
            sl = slice(j * LANES, (j + 1) * LANES)
            qa_ref[:, sl] = (_rope(aq_ref[:, sl], ta, 8) * sc_a).astype(BF16)
            ka_ref[:, sl] = _rope(ak_ref[:, sl], ta, 8).astype(BF16)
        va_ref[...] = av_ref[...].astype(BF16)
        qm_ref[...] = (mq_ref[...] * sc_m).astype(BF16)

        cq_hat, _ = _rms_hat(bs_ref[:, 0:MLA_Q_RANK], MLA_Q_RANK)
        cqn = (cq_hat * gcq_ref[...]).astype(BF16)
        cqn_ref[...] = cqn
        ckv_hat, _ = _rms_hat(bs_ref[:, MLA_Q_RANK:MLA_Q_RANK + MLA_KV_RANK], MLA_KV_RANK)
        ckvn = (ckv_hat * gckv_ref[...]).astype(BF16)
        ckvn_ref[...] = ckvn
        qfull = _dot_nt(cqn, wuq_ref[...])
        kv = _dot(ckvn, wukv_ref[...])
        kr = _rope(bs_ref[:, 384:512], tb, 16)
        lane = lax.broadcasted_iota(jnp.int32, (1, LANES), 1)
        low = lane < 64
        for h in range(MLA_HEADS):
            sl = slice(h * LANES, (h + 1) * LANES)
            qb_ref[:, sl] = (_rope(qfull[:, sl], tb, 16) * sc_b).astype(BF16)
            kb_ref[:, sl] = jnp.where(low, kv[:, sl], kr).astype(BF16)
            vb_ref[:, sl] = jnp.where(low, 0.0, kv[:, sl]).astype(BF16)

    def col(width, idx):
        return pl.BlockSpec((tm, width), lambda i: (i, idx))

    def full(shape):
        return pl.BlockSpec(shape, lambda i: (0, 0))

    wide = jax.ShapeDtypeStruct((t, 1024), BF16)
    return pl.pallas_call(
        body, name="prep", grid=(t // tm,),
        out_shape=(wide, wide, wide, wide, wide, wide,
                   jax.ShapeDtypeStruct((t, MEM_WIDTH), BF16),
                   jax.ShapeDtypeStruct((t, MLA_Q_RANK), BF16),
                   jax.ShapeDtypeStruct((t, MLA_KV_RANK), BF16)),
        in_specs=[col(1024, 0), col(1024, 1), col(1024, 2), col(512, COL_CQ // 512), col(512, COL_MQ // 512),
                  pl.BlockSpec((tm, 4 * LANES), lambda i: (i, 0)),
                  full((1024, MLA_Q_RANK)), full((MLA_KV_RANK, 1024)),
                  full((1, MLA_Q_RANK)), full((1, MLA_KV_RANK)), full((8, LANES)), full((8, LANES))],
        out_specs=(col(1024, 0),) * 6 + (col(MEM_WIDTH, 0), col(MLA_Q_RANK, 0), col(MLA_KV_RANK, 0)),
        compiler_params=_params(("parallel",)),
    )(proj, proj, proj, proj, proj, trig, w_uq, w_ukv, g_cq, g_ckv, rope_a, rope_b)


def _attn_fwd(q, k, v, *, nb, s, sk, heads, hpb, voff, bq, name):
    nq = s // bq
    width = hpb * LANES
    vblk = voff // hpb

    def body(q_ref, k_ref, v_ref, o_ref, lse_ref):
        for h in range(hpb):
            sl = slice(h * LANES, (h + 1) * LANES)
            sc = _dot_nt(q_ref[:, sl], k_ref[:, sl])
            m = jnp.max(sc, axis=1, keepdims=True)
            p = jnp.exp2(sc - m)
            l = jnp.sum(p, axis=1, keepdims=True)
            o_ref[:, sl] = _dot(p.astype(BF16), v_ref[:, sl]) / l
            lse_ref[:, sl] = jnp.broadcast_to(m + jnp.log(l) * LOG2E, (bq, LANES))

    out = jax.ShapeDtypeStruct((nb * s, heads * LANES), F32)
    ospec = pl.BlockSpec((bq, width), lambda b, i, g: (b * nq + i, g))
    return pl.pallas_call(
        body, name=name, grid=(nb, nq, heads // hpb),
        out_shape=(out, out),
        in_specs=[ospec, pl.BlockSpec((sk, width), lambda b, i, g: (b, g)),
                  pl.BlockSpec((sk, width), lambda b, i, g: (b, vblk + g))],
        out_specs=(ospec, ospec),
        compiler_params=_params(("parallel", "parallel", "parallel")),
    )(q, k, v)


def _attn_bwd(q, k, v, o, do, lse, *, nb, s, sk, heads, hpb, voff, scale, bq, name):
    nq = s // bq
    width = hpb * LANES
    vblk = voff // hpb

    def body(q_ref, k_ref, v_ref, o_ref, do_ref, lse_ref, dq_ref, dk_ref, dv_ref, dk_acc, dv_acc):
        i = pl.program_id(2)

        @pl.when(i == 0)
        def _():
            dk_acc[...] = jnp.zeros_like(dk_acc)
            dv_acc[...] = jnp.zeros_like(dv_acc)

        for h in range(hpb):
            sl = slice(h * LANES, (h + 1) * LANES)
            qh = q_ref[:, sl]
            kk = k_ref[:, sl]
            doh = do_ref[:, sl]
            delta = jnp.sum(doh.astype(F32) * o_ref[:, sl], axis=1, keepdims=True)
            p = jnp.exp2(_dot_nt(qh, kk) - lse_ref[:, h * LANES:h * LANES + 1])
            ds = (p * (_dot_nt(doh, v_ref[:, sl]) - delta)).astype(BF16)
            dq_ref[:, sl] = (_dot(ds, kk) * scale).astype(BF16)
            dk_acc[:, sl] += _dot_tn(ds, qh)
            dv_acc[:, sl] += _dot_tn(p.astype(BF16), doh)

        @pl.when(i == nq - 1)
        def _():
            dk_ref[...] = (dk_acc[...] * LN2).astype(BF16)
            dv_ref[...] = dv_acc[...].astype(BF16)

    qspec = pl.BlockSpec((bq, width), lambda b, g, i: (b * nq + i, g))
    kv_spec = pl.BlockSpec((sk, width), lambda b, g, i: (b, g))
    dq_shape = jax.ShapeDtypeStruct((nb * s, heads * LANES), BF16)
    dkv_shape = jax.ShapeDtypeStruct((nb * sk, heads * LANES), BF16)
    return pl.pallas_call(
        body, name=name, grid=(nb, heads // hpb, nq),
        out_shape=(dq_shape, dkv_shape, dkv_shape),
        in_specs=[qspec, kv_spec, pl.BlockSpec((sk, width), lambda b, g, i: (b, vblk + g)), qspec, qspec, qspec],
        out_specs=(qspec, kv_spec, kv_spec),
        scratch_shapes=[pltpu.VMEM((sk, width), F32), pltpu.VMEM((sk, width), F32)],
        compiler_params=_params(("parallel", "parallel", "arbitrary")),
    )(q, k, v, o, do, lse)


BAND_Q = 128
BAND_WIN = 256


def _band_start(i, s):
    return min(max(i * BAND_Q - 64, 0), s - BAND_WIN)


def _to_pattern_order(src_ref, dst_ref, stage_ref, s, d):
    length = s // d
    stage_ref[...] = src_ref[...].astype(F32)
    for r in range(d):
        dst_ref[r * length:(r + 1) * length, :] = stage_ref[pl.ds(r, length, stride=d), :].astype(dst_ref.dtype)


def _dilated_fwd(q, k, v, bias, bias_index, *, nb, s, name):
    nblk = s // BAND_Q
    npat = len(DILATED)

    def body(q_ref, k_ref, v_ref, bias_ref, o_ref, lse_ref, *rest):
        ordered = rest[:3 * (npat - 1)]
        stage_ref, op_ref, lp_ref, on_ref, ln_ref = rest[3 * (npat - 1):]
        lane = lax.broadcasted_iota(jnp.int32, (1, LANES), 1)
        first = lane < 64
        for p, (_, d) in enumerate(DILATED):
            if d == 1:
                qs, ks, vs = q_ref, k_ref, v_ref
            else:
                qs, ks, vs = ordered[3 * (p - 1):3 * p]
                for src, dst in ((q_ref, qs), (k_ref, ks), (v_ref, vs)):
                    _to_pattern_order(src, dst, stage_ref, s, d)
            for i in range(nblk):
                u0 = i * BAND_Q
                st = _band_start(i, s)
                qi = qs[u0:u0 + BAND_Q, :]
                kw = ks[st:st + BAND_WIN, :]
                vw = vs[st:st + BAND_WIN, :]
                zero = jnp.zeros_like(qi)
                q2 = jnp.concatenate([jnp.where(first, qi, zero), jnp.where(first, zero, qi)], axis=0)
                sc = _dot_nt(q2, kw)
                b = bias_ref[bias_index[p][i]]
                halves = []
                for h in range(2):
                    sh = sc[h * BAND_Q:(h + 1) * BAND_Q] + b
                    m = jnp.max(sh, axis=1, keepdims=True)
                    pr = jnp.exp2(sh - m)
                    l = jnp.sum(pr, axis=1, keepdims=True)
                    halves.append((pr.astype(BF16), l, m + jnp.log(l) * LOG2E))
                o2 = _dot(jnp.concatenate([halves[0][0], halves[1][0]], axis=0), vw)
                o_blk = jnp.where(first, o2[:BAND_Q] / halves[0][1], o2[BAND_Q:] / halves[1][1])
                lse_blk = jnp.where(first, jnp.broadcast_to(halves[0][2], (BAND_Q, LANES)),
                                    jnp.broadcast_to(halves[1][2], (BAND_Q, LANES)))
                op_ref[p, u0:u0 + BAND_Q, :] = o_blk
                lp_ref[p, u0:u0 + BAND_Q, :] = lse_blk
            if d > 1:
                length = s // d
                for r in range(d):
                    on_ref.at[p - 1][pl.ds(r, length, stride=d), :] = op_ref[p, r * length:(r + 1) * length, :]
                    ln_ref.at[p - 1][pl.ds(r, length, stride=d), :] = lp_ref[p, r * length:(r + 1) * length, :]
        lses = [lp_ref[0]] + [ln_ref[p] for p in range(npat - 1)]
        outs = [op_ref[0]] + [on_ref[p] for p in range(npat - 1)]
        m = functools.reduce(jnp.maximum, lses)
        ws = [jnp.exp2(l - m) for l in lses]
        den = functools.reduce(lambda a, c: a + c, ws)
        o_ref[...] = functools.reduce(lambda a, c: a + c, [w * o for w, o in zip(ws, outs)]) / den
        lse_ref[...] = m + jnp.log(den) * LOG2E

    blk = pl.BlockSpec((s, LANES), lambda b, g: (b, g))
    out = jax.ShapeDtypeStruct((nb * s, A_WIDTH), F32)
    copy = jax.ShapeDtypeStruct((nb * s, A_WIDTH), BF16)
    n_copies = 3 * (npat - 1)
    res = pl.pallas_call(
        body, name=name, grid=(nb, A_WIDTH // LANES),
        out_shape=(out, out) + (copy,) * n_copies,
        in_specs=[blk, blk, blk, pl.BlockSpec(bias.shape, lambda b, g: (0, 0, 0))],
        out_specs=(blk, blk) + (blk,) * n_copies,
        scratch_shapes=[pltpu.VMEM((s, LANES), F32), pltpu.VMEM((npat, s, LANES), F32),
                        pltpu.VMEM((npat, s, LANES), F32), pltpu.VMEM((npat - 1, s, LANES), F32),
                        pltpu.VMEM((npat - 1, s, LANES), F32)],
        compiler_params=_params(("parallel", "parallel")),
    )(q, k, v, bias)
    return res[0], res[1], res[2:]


def _dilated_bwd(q, k, v, ordered, o, do, lse, bias, bias_index, *, nb, s, scale, name):
    nblk = s // BAND_Q
    npat = len(DILATED)
    n_copies = 3 * (npat - 1)

    def body(q_ref, k_ref, v_ref, *rest):
        ordered_refs = rest[:n_copies]
        (o_ref, do_ref, lse_ref, bias_ref, dq_out, dk_out, dv_out, stage_ref, rs_ref, dop_ref, rsp_ref,
         dqp_ref, dkp_ref, dvp_ref, dq_ref, dk_ref, dv_ref, nat_ref) = rest[n_copies:]
        lane = lax.broadcasted_iota(jnp.int32, (1, LANES), 1)
        first = lane < 64
        prod = do_ref[...].astype(F32) * o_ref[...]
        d0 = jnp.sum(jnp.where(first, prod, 0.0), axis=1, keepdims=True)
        d1 = jnp.sum(jnp.where(first, 0.0, prod), axis=1, keepdims=True)
        delta = jnp.where(first, jnp.broadcast_to(d0, (s, LANES)), jnp.broadcast_to(d1, (s, LANES)))
        rs_ref[...] = jnp.where((lane & 32) == 0, lse_ref[...], delta)
        for p, (_, d) in enumerate(DILATED):
            length = s // d
            if d == 1:
                qs, ks, vs, dos, rss = q_ref, k_ref, v_ref, do_ref, rs_ref
                dqs, dks, dvs = dq_ref, dk_ref, dv_ref
            else:
                for src, dst in ((do_ref, dop_ref), (rs_ref, rsp_ref)):
                    _to_pattern_order(src, dst, stage_ref, s, d)
                qs, ks, vs = ordered_refs[3 * (p - 1):3 * p]
                dos, rss = dop_ref, rsp_ref
                dqs, dks, dvs = dqp_ref, dkp_ref, dvp_ref
            dks[...] = jnp.zeros((s, LANES), F32)
            dvs[...] = jnp.zeros((s, LANES), F32)
            for i in range(nblk):
                u0 = i * BAND_Q
                st = _band_start(i, s)
                qi = qs[u0:u0 + BAND_Q, :]
                doi = dos[u0:u0 + BAND_Q, :]
                kw = ks[st:st + BAND_WIN, :]
                vw = vs[st:st + BAND_WIN, :]
                zero = jnp.zeros_like(qi)
                q2 = jnp.concatenate([jnp.where(first, qi, zero), jnp.where(first, zero, qi)], axis=0)
                do2 = jnp.concatenate([jnp.where(first, doi, zero), jnp.where(first, zero, doi)], axis=0)
                sc = _dot_nt(q2, kw)
                dp = _dot_nt(do2, vw)
                b = bias_ref[bias_index[p][i]]
                rs_i = rss[u0:u0 + BAND_Q, :]
                ps, dss = [], []
                for h in range(2):
                    rows = slice(h * BAND_Q, (h + 1) * BAND_Q)
                    pr = jnp.exp2(sc[rows] + b - rs_i[:, 64 * h:64 * h + 1])
                    ps.append(pr.astype(BF16))
                    dss.append((pr * (dp[rows] - rs_i[:, 64 * h + 32:64 * h + 33])).astype(BF16))
                p2 = jnp.concatenate(ps, axis=0)
                ds2 = jnp.concatenate(dss, axis=0)
                dq2 = _dot(ds2, kw)
                dqs[u0:u0 + BAND_Q, :] = jnp.where(first, dq2[:BAND_Q], dq2[BAND_Q:]) * scale
                dks[st:st + BAND_WIN, :] += _dot_tn(ds2, q2)
                dvs[st:st + BAND_WIN, :] += _dot_tn(p2, do2)
            if d > 1:
                for j, src in enumerate((dqp_ref, dkp_ref, dvp_ref)):
                    for r in range(d):
                        nat_ref.at[p - 1, j][pl.ds(r, length, stride=d), :] = src[r * length:(r + 1) * length, :]

        def total(j, first_ref):
            return functools.reduce(lambda a, c: a + c, [first_ref[...]] + [nat_ref[p, j] for p in range(npat - 1)])

        dq_out[...] = total(0, dq_ref).astype(BF16)
        dk_out[...] = (total(1, dk_ref) * LN2).astype(BF16)
        dv_out[...] = total(2, dv_ref).astype(BF16)

    blk = pl.BlockSpec((s, LANES), lambda b, g: (b, g))
    out = jax.ShapeDtypeStruct((nb * s, A_WIDTH), BF16)
    f32_buf = pltpu.VMEM((s, LANES), F32)
    bf_buf = pltpu.VMEM((s, LANES), BF16)
    return pl.pallas_call(
        body, name=name, grid=(nb, A_WIDTH // LANES),
        out_shape=(out, out, out),
        in_specs=[blk] * (6 + n_copies) + [pl.BlockSpec(bias.shape, lambda b, g: (0, 0, 0))],
        out_specs=(blk, blk, blk),
        scratch_shapes=[f32_buf, f32_buf, bf_buf] + [f32_buf] * 7 + [pltpu.VMEM((npat - 1, 3, s, LANES), F32)],
        compiler_params=_params(("parallel", "parallel")),
    )(q, k, v, *ordered, o, do, lse, bias)


def _post(x, ya, ybp, ym, proj, target, w_out, g_emb, b_emb, g_a, g_b, g_m, g_post, b_post, tm=256):
    t = x.shape[0]

    def body(x_ref, ya_ref, yb_ref, ym_ref, ga_ref, gb_ref, gm_ref, tg_ref, wo_ref,
             ge_ref, be_ref, goa_ref, gob_ref, gom_ref, gp_ref, bp_ref,
             y_ref, dz_ref, doa_ref, dob_ref, dom_ref, dga_ref, dgb_ref, dgm_ref,
             loss_ref, dgp_ref, dbp_ref, dgoa_ref, dgob_ref, dgom_ref):
        i = pl.program_id(0)

        @pl.when(i == 0)
        def _():
            for r in (loss_ref, dgp_ref, dbp_ref, dgoa_ref, dgob_ref, dgom_ref):
                r[...] = jnp.zeros_like(r)

        lane = lax.broadcasted_iota(jnp.int32, (1, LANES), 1)
        low = lane < 64
        xh0, _ = _ln_hat(x_ref[...])
        h = xh0 * ge_ref[...] + be_ref[...]

        ybp_v = yb_ref[...]
        yb = jnp.concatenate(
            [jnp.where(low, pltpu.roll(ybp_v[:, 2 * j * LANES:(2 * j + 1) * LANES], 64, 1),
                       ybp_v[:, (2 * j + 1) * LANES:(2 * j + 2) * LANES]) for j in range(4)], axis=1)

        def gated(raw, gate, gain, width):
            xh, r = _rms_hat(raw, width)
            n = xh * gain
            sg = 1.0 / (1.0 + jnp.exp(-gate))
            return xh, r, n, sg, n * (gate * sg)

        gate_a, gate_b, gate_m = ga_ref[...], gb_ref[...], gm_ref[...]
        xh_a, r_a, n_a, sg_a, y_a = gated(ya_ref[...], gate_a, goa_ref[...], A_WIDTH)
        xh_b, r_b, n_b, sg_b, y_b = gated(yb, gate_b, gob_ref[...], 512)
        xh_m, r_m, n_m, sg_m, y_m = gated(ym_ref[...], gate_m, gom_ref[...], 512)
        y = jnp.concatenate([y_a, y_b, y_m], axis=1).astype(BF16)
        y_ref[...] = y
        z = DEEPNORM_ALPHA * h + _dot(y, wo_ref[...])
        zh, rstd = _ln_hat(z)
        err = zh * gp_ref[...] + bp_ref[...] - tg_ref[...]
        rows = jnp.sum(err * err, axis=1, keepdims=True)
        loss_ref[...] += jnp.broadcast_to(jnp.sum(rows, axis=0, keepdims=True) * (0.5 / D_MODEL), (1, LANES))
        dout = err * (1.0 / D_MODEL)
        dgp_ref[...] += _colsum(dout * zh)
        dbp_ref[...] += _colsum(dout)
        dz = _ln_bwd_rows(dout * gp_ref[...], zh, rstd)
        dz_ref[...] = dz
        dy = _dot_nt(dz.astype(BF16), wo_ref[...])

        def gated_bwd(dyg, xh, r, n, sg, gate, gain, width, dgain_ref):
            dn = dyg * (gate * sg)
            dgate = dyg * n * (sg * (1.0 + gate * (1.0 - sg)))
            dgain_ref[...] += _colsum(dn * xh)
            return _rms_bwd(dn * gain, xh, r, width), dgate

        dya, dgate_a = gated_bwd(dy[:, 0:1024], xh_a, r_a, n_a, sg_a, gate_a, goa_ref[...], A_WIDTH, dgoa_ref)
        dyb, dgate_b = gated_bwd(dy[:, 1024:1536], xh_b, r_b, n_b, sg_b, gate_b, gob_ref[...], 512, dgob_ref)
        dym, dgate_m = gated_bwd(dy[:, 1536:2048], xh_m, r_m, n_m, sg_m, gate_m, gom_ref[...], 512, dgom_ref)
        doa_ref[...] = dya.astype(BF16)
        dom_ref[...] = dym.astype(BF16)
        dga_ref[...] = dgate_a.astype(BF16)
        dgb_ref[...] = dgate_b.astype(BF16)
        dgm_ref[...] = dgate_m.astype(BF16)
        for j in range(4):
            blk = dyb[:, j * LANES:(j + 1) * LANES]
            dob_ref[:, 2 * j * LANES:(2 * j + 1) * LANES] = jnp.where(low, 0.0, pltpu.roll(blk, 64, 1)).astype(BF16)
            dob_ref[:, (2 * j + 1) * LANES:(2 * j + 2) * LANES] = jnp.where(low, 0.0, blk).astype(BF16)

    def col(width, idx):
        return pl.BlockSpec((tm, width), lambda i: (i, idx))

    def full(shape):
        return pl.BlockSpec(shape, lambda i: (0, 0))

    def acc(width):
        return jax.ShapeDtypeStruct((1, width), F32)

    return pl.pallas_call(
        body, name="post", grid=(t // tm,),
        out_shape=(jax.ShapeDtypeStruct((t, 2048), BF16), jax.ShapeDtypeStruct((t, 1024), F32),
                   jax.ShapeDtypeStruct((t, 1024), BF16), jax.ShapeDtypeStruct((t, 1024), BF16),
                   jax.ShapeDtypeStruct((t, 512), BF16),
                   jax.ShapeDtypeStruct((t, 1024), BF16), jax.ShapeDtypeStruct((t, 512), BF16),
                   jax.ShapeDtypeStruct((t, 512), BF16),
                   acc(LANES), acc(1024), acc(1024), acc(1024), acc(512), acc(512)),
        in_specs=[col(1024, 0), col(1024, 0), col(1024, 0), col(512, 0),
                  col(1024, 3), col(512, COL_BG // 512), col(512, COL_MG // 512), col(1024, 0),
                  full((2048, 1024)),
                  full((1, 1024)), full((1, 1024)), full((1, 1024)), full((1, 512)), full((1, 512)),
                  full((1, 1024)), full((1, 1024))],
        out_specs=(col(2048, 0), col(1024, 0), col(1024, 0), col(1024, 0), col(512, 0),
                   col(1024, 0), col(512, 0), col(512, 0),
                   full((1, LANES)), full((1, 1024)), full((1, 1024)), full((1, 1024)), full((1, 512)),
                   full((1, 512))),
        compiler_params=_params(("arbitrary",)),
    )(x, ya, ybp, ym, proj, proj, proj, target, w_out, g_emb, b_emb, g_a, g_b, g_m, g_post, b_post)


def _prep_bwd(dqa, dka, dva, dqb, dkb, dvb, dqm, dga, dgb, dgm, proj, trig, w_uq, w_ukv, g_cq, g_ckv,
              rope_a, rope_b, tm=256):
    t = proj.shape[0]

    def body(dqa_ref, dka_ref, dva_ref, dqb_ref, dkb_ref, dvb_ref, dqm_ref, dga_ref, dgb_ref, dgm_ref,
             bs_ref, trig_ref, wuq_ref, wukv_ref, gcq_ref, gckv_ref, ra_ref, rb_ref,
             dproj_ref, dqf_ref, dkv_ref, dgcq_ref, dgckv_ref):
        i = pl.program_id(0)

        @pl.when(i == 0)
        def _():
            dgcq_ref[...] = jnp.zeros_like(dgcq_ref)
            dgckv_ref[...] = jnp.zeros_like(dgckv_ref)

        ta = _rope_tables(trig_ref[:, 0:LANES], trig_ref[:, LANES:2 * LANES], ra_ref[...])
        tb = _rope_tables(trig_ref[:, 2 * LANES:3 * LANES], trig_ref[:, 3 * LANES:4 * LANES], rb_ref[...])
        for j in range(A_WIDTH // LANES):
            sl = slice(j * LANES, (j + 1) * LANES)
            dproj_ref[:, j * LANES:(j + 1) * LANES] = (
                _rope(dqa_ref[:, sl].astype(F32), ta, 8, inverse=True).astype(BF16))
            dproj_ref[:, 1024 + j * LANES:1024 + (j + 1) * LANES] = (
                _rope(dka_ref[:, sl].astype(F32), ta, 8, inverse=True).astype(BF16))
        dproj_ref[:, 2048:3072] = dva_ref[...]
        dproj_ref[:, 3072:4096] = dga_ref[...]

        lane = lax.broadcasted_iota(jnp.int32, (1, LANES), 1)
        low = lane < 64
        rope_lanes = (lane >= 64) & (lane < 96)
        dkr = jnp.zeros((tm, LANES), F32)
        for h in range(MLA_HEADS):
            sl = slice(h * LANES, (h + 1) * LANES)
            dqf_ref[:, sl] = _rope(dqb_ref[:, sl].astype(F32), tb, 16, inverse=True).astype(BF16)
            dk_h = dkb_ref[:, sl]
            dkv_ref[:, sl] = jnp.where(low, dk_h, dvb_ref[:, sl])
            dkr = dkr + jnp.where(rope_lanes, dk_h.astype(F32), 0.0)
        dkr = _rope(dkr, tb, 16, inverse=True)

        cq_hat, r_q = _rms_hat(bs_ref[:, 0:MLA_Q_RANK], MLA_Q_RANK)
        dcqn = _dot(dqf_ref[...], wuq_ref[...])
        dgcq_ref[...] += _colsum(dcqn * cq_hat)
        dproj_ref[:, COL_CQ:COL_CQ + 256] = _rms_bwd(dcqn * gcq_ref[...], cq_hat, r_q, MLA_Q_RANK).astype(BF16)
        ckv_hat, r_kv = _rms_hat(bs_ref[:, MLA_Q_RANK:MLA_Q_RANK + MLA_KV_RANK], MLA_KV_RANK)
        dckvn = _dot_nt(dkv_ref[...], wukv_ref[...])
        dgckv_ref[...] += _colsum(dckvn * ckv_hat)
        dproj_ref[:, COL_CQ + 256:COL_CQ + 384] = (
            _rms_bwd(dckvn * gckv_ref[...], ckv_hat, r_kv, MLA_KV_RANK).astype(BF16))
        dproj_ref[:, COL_CQ + 384:COL_CQ + 512] = dkr.astype(BF16)
        dproj_ref[:, COL_BG:COL_BG + 512] = dgb_ref[...]
        dproj_ref[:, COL_MQ:COL_MQ + 512] = dqm_ref[...]
        dproj_ref[:, COL_MG:COL_MG + 512] = dgm_ref[...]

    def col(width, idx):
        return pl.BlockSpec((tm, width), lambda i: (i, idx))

    def full(shape):
        return pl.BlockSpec(shape, lambda i: (0, 0))

    return pl.pallas_call(
        body, name="prep_bwd", grid=(t // tm,),
        out_shape=(jax.ShapeDtypeStruct((t, PROJ_W), BF16), jax.ShapeDtypeStruct((t, 1024), BF16),
                   jax.ShapeDtypeStruct((t, 1024), BF16),
                   jax.ShapeDtypeStruct((1, MLA_Q_RANK), F32), jax.ShapeDtypeStruct((1, MLA_KV_RANK), F32)),
        in_specs=[col(1024, 0)] * 6 + [col(512, 0), col(1024, 0), col(512, 0), col(512, 0),
                  col(512, COL_CQ // 512), pl.BlockSpec((tm, 4 * LANES), lambda i: (i, 0)),
                  full((1024, MLA_Q_RANK)), full((MLA_KV_RANK, 1024)),
                  full((1, MLA_Q_RANK)), full((1, MLA_KV_RANK)), full((8, LANES)), full((8, LANES))],
        out_specs=(col(PROJ_W, 0), col(1024, 0), col(1024, 0), full((1, MLA_Q_RANK)), full((1, MLA_KV_RANK))),
        compiler_params=_params(("arbitrary",)),
    )(dqa, dka, dva, dqb, dkb, dvb, dqm, dga, dgb, dgm, proj, trig, w_uq, w_ukv, g_cq, g_ckv, rope_a, rope_b)


def _adamw_math(gv, w, m, v):
    m_new = ADAM_B1 * m + (1.0 - ADAM_B1) * gv
    v_new = ADAM_B2 * v + (1.0 - ADAM_B2) * (gv * gv)
    m_hat = m_new / (1.0 - ADAM_B1 ** ADAM_STEP)
    v_hat = v_new / (1.0 - ADAM_B2 ** ADAM_STEP)
    return -ADAM_LR * (m_hat / (jnp.sqrt(v_hat) + ADAM_EPS) + ADAM_WD * w), m_new, v_new


def _adamw(g, w, m, v, tr, name):
    r, cols = w.shape

    def body(g_ref, w_ref, m_ref, v_ref, go_ref, d_ref, nm_ref, nv_ref):
        gv = g_ref[...]
        go_ref[...] = gv
        d_ref[...], nm_ref[...], nv_ref[...] = _adamw_math(gv, w_ref[...], m_ref[...], v_ref[...])

    tile = pl.BlockSpec((tr, cols), lambda i: (i, 0))
    shape = jax.ShapeDtypeStruct((r, cols), F32)
    return pl.pallas_call(
        body, name=name, grid=(r // tr,),
        out_shape=(shape,) * 4, in_specs=[tile] * 4, out_specs=(tile,) * 4,
        compiler_params=_params(("parallel",)),
    )(g, w, m, v)


def _adamw_pieces(g, w, m, v, pieces, name):
    shapes = [jax.ShapeDtypeStruct((r1 - r0, c1 - c0), F32) for r0, r1, c0, c1 in pieces]

    def body(g_ref, w_ref, m_ref, v_ref, *outs):
        gv = g_ref[...]
        results = (gv,) + _adamw_math(gv, w_ref[...], m_ref[...], v_ref[...])
        for kind, full in enumerate(results):
            for p, (r0, r1, c0, c1) in enumerate(pieces):
                outs[kind * len(pieces) + p][...] = full[r0:r1, c0:c1]

    flat = pl.pallas_call(
        body, name=name, out_shape=tuple(shapes) * 4,
        in_specs=[IN_VMEM] * 4, out_specs=tuple([IN_VMEM] * (4 * len(pieces))),
        compiler_params=_params(None),
    )(g, w, m, v)
    return [[flat[kind * len(pieces) + p] for kind in range(4)] for p in range(len(pieces))]


def _core_sum(g, recv, core, rows, tr, name, ride=None):
    cols = g.shape[2]
    nblk = rows // tr
    n_in = len(ride.args) if ride else 0
    n_out = len(ride.out_shapes) if ride else 0

    def body(c_ref, g_ref, r_ref, *rest):
        sf_ref, sb_ref = rest[n_in], rest[n_in + 1]
        if ride:
            j, i = pl.program_id(0), pl.program_id(1)
            ride.run(j * nblk + i, 4 * nblk, rest[:n_in], rest[n_in + 2:n_in + 2 + n_out],
                     rest[n_in + 2 + n_out:])
        tot = g_ref[...] + r_ref[...]
        sf_ref[...] = tot
        sb_ref[...] = tot.astype(BF16)

    half = pl.BlockSpec((None, tr, cols), lambda j, i, c_ref: (j, i, 0))
    shapes = (jax.ShapeDtypeStruct((4, rows, cols), F32), jax.ShapeDtypeStruct((4, rows, cols), BF16))
    return pl.pallas_call(
        body, name=name,
        grid_spec=pltpu.PrefetchScalarGridSpec(
            num_scalar_prefetch=1, grid=(4, nblk),
            in_specs=[pl.BlockSpec((None, tr, cols), lambda j, i, c_ref: (j, c_ref[0] * nblk + i, 0)), half]
            + (ride.in_specs if ride else []),
            out_specs=(half, half) + (ANY,) * n_out,
            scratch_shapes=ride.scratch() if ride else []),
        out_shape=shapes + tuple(ride.out_shapes if ride else ()),
        compiler_params=_params(("arbitrary", "arbitrary") if ride else ("parallel", "parallel")),
    )(core, g, recv, *(ride.args if ride else ()))


def _half_to_sibling(g4):
    def plan(in_refs, out_refs, send_sems, recv_sems):
        x, y, c = _position()
        cp = pltpu.make_async_remote_copy(
            src_ref=in_refs[0].at[:, 1 - c], dst_ref=out_refs[0], send_sem=send_sems.at[0],
            recv_sem=recv_sems.at[0], device_id=(x, y, 1 - c), device_id_type=MESH)

        def finish():
            cp.wait_recv()
            cp.wait_send()

        return cp.start, finish

    return _Ride([g4], [jax.ShapeDtypeStruct((4, g4.shape[2], 1024), F32)], (1, 1), plan)


def _gather_plan(src_ref, dst_ref, send_sems, recv_sems, local_sems):
    x, y, c = _position()
    me = 2 * x + y
    rows = src_ref.shape[1]
    cut = -(-rows // 32) * 16
    pieces = (pl.ds(0, cut), pl.ds(cut, rows - cut))
    local = pltpu.make_async_copy(src_ref, dst_ref.at[me], local_sems.at[0])

    def over_ici(sem, k, chip, t, src=None):
        where = dst_ref.at[chip, c, pieces[t]]
        return pltpu.make_async_remote_copy(
            src_ref=where if src is None else src, dst_ref=where, send_sem=send_sems.at[sem],
            recv_sem=recv_sems.at[sem], device_id=(x ^ (k >> 1), y ^ (k & 1), c), device_id_type=MESH)

    def mine_to(k, t):
        return over_ici(2 * (k - 1) + t, k, me, t, src=src_ref.at[c, pieces[t]])

    def from_neighbour(k, t):
        return over_ici(2 * (k - 1) + t, k, me ^ k, t)

    def to_sibling(k, half):
        piece = dst_ref.at[me ^ k, half]
        return pltpu.make_async_remote_copy(
            src_ref=piece, dst_ref=piece, send_sem=send_sems.at[5 + k], recv_sem=recv_sems.at[5 + k],
            device_id=(x, y, 1 - c), device_id_type=MESH)

    sends = [mine_to(2, 0), mine_to(1, 1), mine_to(2, 1), mine_to(1, 0)]
    onward = [over_ici(4, 1, me ^ 2, 0), over_ici(5, 2, me ^ 1, 1)]

    def start():
        local.start()
        for cp in sends:
            cp.start()

    def pass_on():
        from_neighbour(2, 0).wait_recv()
        onward[0].start()
        from_neighbour(1, 1).wait_recv()
        onward[1].start()

    def to_other_core():
        from_neighbour(2, 1).wait_recv()
        to_sibling(2, c).start()
        from_neighbour(1, 0).wait_recv()
        to_sibling(1, c).start()
        over_ici(4, 1, me ^ 3, 0).wait_recv()
        over_ici(5, 2, me ^ 3, 1).wait_recv()
        to_sibling(3, c).start()

    def finish():
        for k in (1, 2, 3):
            to_sibling(k, 1 - c).wait_recv()
        for cp in sends + onward + [to_sibling(k, c) for k in (1, 2, 3)]:
            cp.wait_send()
        local.wait()

    return start, pass_on, to_other_core, finish


def _gather_ride(shard, spread):
    def plan(in_refs, out_refs, send_sems, recv_sems, local_sems):
        return _gather_plan(in_refs[0], out_refs[0], send_sems, recv_sems, local_sems)

    return _Ride([shard], [jax.ShapeDtypeStruct((4,) + shard.shape, shard.dtype)], (9, 9, 1), plan,
                 in_specs=[IN_VMEM], spread=spread)


def _chip_sum(sf, recv, chip, rows, tr, name):
    cols = sf.shape[2]

    def body(me_ref, sf_ref, r_ref, out_ref):
        acc = sf_ref[...]
        for k in range(3):
            acc = acc + r_ref[k].astype(F32)
        out_ref[...] = acc

    return pl.pallas_call(
        body, name=name,
        grid_spec=pltpu.PrefetchScalarGridSpec(
            num_scalar_prefetch=1, grid=(rows // tr,),
            in_specs=[pl.BlockSpec((None, tr, cols), lambda i, me_ref: (me_ref[0], i, 0)),
                      pl.BlockSpec((3, tr, cols), lambda i, me_ref: (0, i, 0))],
            out_specs=pl.BlockSpec((tr, cols), lambda i, me_ref: (i, 0))),
        out_shape=jax.ShapeDtypeStruct((rows, cols), F32),
        compiler_params=_params(("parallel",)),
    )(chip, sf, recv)


def _position():
    return lax.axis_index("x"), lax.axis_index("y"), lax.axis_index("c")


def _dh_scatter(dproj, w_in_arr_t, x, dz, g, sb_in, sb_rest, tm=1024, tk=1024):
    t, d = x.shape
    nk = dproj.shape[1] // tk
    ni = t // tm

    def body(dp_ref, w_ref, x_ref, dz_ref, g_ref, sbin_ref, sbrest_ref,
             dx_ref, dg_ref, db_ref, rin_ref, rrest_ref, acc_ref, send_sems, recv_sems):
        i = pl.program_id(0)
        kk = pl.program_id(1)
        px, py, pc = _position()
        me = 2 * px + py
        srcs = (sbin_ref, sbrest_ref)
        dsts = (rin_ref, rrest_ref)

        def copy(a, k):
            return pltpu.make_async_remote_copy(
                src_ref=srcs[a].at[me ^ k], dst_ref=dsts[a].at[k - 1],
                send_sem=send_sems.at[3 * a + k - 1], recv_sem=recv_sems.at[3 * a + k - 1],
                device_id=(px ^ (k >> 1), py ^ (k & 1), pc), device_id_type=MESH)

        pairs = [(a, k) for a in range(2) for k in (1, 2, 3)]

        @pl.when((i == 0) & (kk == 0))
        def _():
            dg_ref[...] = jnp.zeros_like(dg_ref)
            db_ref[...] = jnp.zeros_like(db_ref)
            for a, k in pairs:
                copy(a, k).start()

        part = _dot(dp_ref[...], w_ref[...])

        @pl.when(kk == 0)
        def _():
            acc_ref[...] = part

        @pl.when(kk > 0)
        def _():
            acc_ref[...] += part

        @pl.when(kk == nk - 1)
        def _():
            xh, rstd = _ln_hat(x_ref[...])
            dht = acc_ref[...] + DEEPNORM_ALPHA * dz_ref[...]
            dg_ref[...] += _colsum(dht * xh)
            db_ref[...] += _colsum(dht)
            dx_ref[...] = _ln_bwd_rows(dht * g_ref[...], xh, rstd)

        @pl.when((i == ni - 1) & (kk == nk - 1))
        def _():
            for a, k in pairs:
                copy(a, k).wait_recv()
            for a, k in pairs:
                copy(a, k).wait_send()

    tile = pl.BlockSpec((tm, d), lambda i, kk: (i, 0))
    row = pl.BlockSpec((1, d), lambda i, kk: (0, 0))
    return pl.pallas_call(
        body, name="dh_scatter", grid=(ni, nk),
        out_shape=(jax.ShapeDtypeStruct((t, d), F32), jax.ShapeDtypeStruct((1, d), F32),
                   jax.ShapeDtypeStruct((1, d), F32),
                   jax.ShapeDtypeStruct((3, HALF_IN, 1024), BF16),
                   jax.ShapeDtypeStruct((3, HALF_REST, 1024), BF16)),
        in_specs=[pl.BlockSpec((tm, tk), lambda i, kk: (i, kk)), pl.BlockSpec((tk, d), lambda i, kk: (kk, 0)),
                  tile, tile, row, ANY, ANY],
        out_specs=(tile, row, row, ANY, ANY),
        scratch_shapes=[pltpu.VMEM((tm, d), F32), pltpu.SemaphoreType.DMA((6,)), pltpu.SemaphoreType.DMA((6,))],
        compiler_params=_params(("arbitrary", "arbitrary")),
    )(dproj, w_in_arr_t, x, dz, g, sb_in, sb_rest)


def _join_halves(gh_in, gh_rest):
    def body(hin_ref, hrest_ref, oin_ref, orest_ref, send_sems, recv_sems, local_sems):
        x, y, c = _position()
        srcs = (hin_ref, hrest_ref)
        dsts = (oin_ref, orest_ref)

        def rows(a, half):
            return dsts[a].at[half]

        local = [pltpu.make_async_copy(srcs[a], rows(a, c), local_sems.at[a]) for a in range(2)]
        remote = [pltpu.make_async_remote_copy(
            src_ref=srcs[a], dst_ref=rows(a, c), send_sem=send_sems.at[a], recv_sem=recv_sems.at[a],
            device_id=(x, y, 1 - c), device_id_type=MESH) for a in range(2)]
        for cp in local + remote:
            cp.start()
        for a in range(2):
            pltpu.make_async_remote_copy(
                src_ref=srcs[a], dst_ref=rows(a, 1 - c), send_sem=send_sems.at[a], recv_sem=recv_sems.at[a],
                device_id=(x, y, 1 - c), device_id_type=MESH).wait_recv()
        for cp in remote:
            cp.wait_send()
        for cp in local:
            cp.wait()

    return pl.pallas_call(
        body, name="join_halves",
        out_shape=(jax.ShapeDtypeStruct((2, HALF_IN, 1024), F32),
                   jax.ShapeDtypeStruct((2, HALF_REST, 1024), F32)),
        in_specs=[IN_VMEM, IN_VMEM], out_specs=(ANY, ANY),
        scratch_shapes=[pltpu.SemaphoreType.DMA((2,)), pltpu.SemaphoreType.DMA((2,)), pltpu.SemaphoreType.DMA((2,))],
    )(gh_in, gh_rest)


def _allreduce_small(vec):
    def body(vec_ref, out_ref, all_ref, send_sems, recv_sems):
        x, y, c = _position()
        me = 4 * x + 2 * y + c
        all_ref[me] = vec_ref[...]

        def copy(k, slot):
            return pltpu.make_async_remote_copy(
                src_ref=vec_ref, dst_ref=all_ref.at[slot], send_sem=send_sems.at[k - 1], recv_sem=recv_sems.at[k - 1],
                device_id=(x ^ (k >> 2), y ^ ((k >> 1) & 1), c ^ (k & 1)), device_id_type=MESH)

        copies = [copy(k, me) for k in range(1, 8)]
        for cp in copies:
            cp.start()
        for k in range(1, 8):
            copy(k, me ^ k).wait_recv()
        for cp in copies:
            cp.wait_send()
        total = all_ref[0]
        for d in range(1, 8):
            total = total + all_ref[d]
        out_ref[...] = total

    return pl.pallas_call(
        body, name="allreduce_small",
        out_shape=jax.ShapeDtypeStruct(vec.shape, vec.dtype),
        in_specs=[pl.BlockSpec(memory_space=pltpu.VMEM)], out_specs=pl.BlockSpec(memory_space=pltpu.VMEM),
        scratch_shapes=[pltpu.VMEM((8,) + vec.shape, vec.dtype), pltpu.SemaphoreType.DMA((7,)),
                        pltpu.SemaphoreType.DMA((7,))],
    )(vec)


def _pack_rest(w_uq, w_ukv, w_mem, w_out):
    rows = jnp.concatenate([w_uq[0].T.reshape(-1, 1024), w_ukv.reshape(-1, 1024), w_mem.reshape(-1, 1024),
                            w_out.reshape(-1, 1024)], axis=0)
    return jnp.pad(rows, ((0, ROWS_REST - ROWS_USED), (0, 0)))


def _arranged_w_in(g_in):
    z = functools.partial(jnp.zeros, dtype=g_in.dtype)
    cut = 4480 - 2 * SHARD_ROWS
    return jnp.concatenate(
        [g_in[0, :SHARD_ROWS], g_in[1, :SHARD_ROWS], g_in[2, :cut], z((64, 1024)), g_in[2, cut:cut + 32],
         z((32, 1024)), g_in[2, cut + 32:SHARD_ROWS], g_in[3, :SHARD_ROWS]], axis=0)


def _rest_weights(g_rest):
    w_uq_t = g_rest[:, 0:ROWS_UQ].reshape(768, 256)
    w_uq_pad_t = jnp.pad(w_uq_t.reshape(MLA_HEADS, MLA_QK_DIM, 256), ((0, 0), (0, 32), (0, 0))).reshape(1024, 256)
    w_ukv = jnp.concatenate([g_rest[j, ROWS_UQ:ROWS_UQ + ROWS_UKV].reshape(128, 256) for j in range(4)], axis=1)
    lo = ROWS_UQ + ROWS_UKV
    w_mem = g_rest[:, lo:lo + ROWS_MEM].reshape(4 * ROWS_MEM, 1024)
    w_out = g_rest[:, lo + ROWS_MEM:lo + ROWS_MEM + ROWS_OUT].reshape(4 * ROWS_OUT, 1024)
    return w_uq_pad_t, w_ukv, w_mem, w_out


def _split_in(dw_in_arr_t):
    a = dw_in_arr_t
    gap = jnp.zeros((ROWS_IN - SHARD_ROWS, 1024), a.dtype)
    nat = 4608 - 96
    pieces = [a[:SHARD_ROWS], gap, a[SHARD_ROWS:2 * SHARD_ROWS], gap,
              a[2 * SHARD_ROWS:4480], a[4544:4576], a[4608:4608 + 3 * SHARD_ROWS - nat], gap,
              a[4608 + 3 * SHARD_ROWS - nat:], gap]
    return jnp.concatenate(pieces, axis=0).reshape(4, ROWS_IN, 1024)


def _split_rest(dw_uq_pad_t, dw_ukv, dw_mem, dw_out):
    dw_uq_t = dw_uq_pad_t.reshape(MLA_HEADS, LANES, 256)[:, :MLA_QK_DIM].reshape(4, ROWS_UQ, 1024)
    parts = [dw_uq_t, dw_ukv.reshape(128, 4, 256).transpose(1, 0, 2).reshape(4, ROWS_UKV, 1024),
             dw_mem.reshape(4, ROWS_MEM, 1024), dw_out.reshape(4, ROWS_OUT, 1024)]
    return jnp.pad(jnp.concatenate(parts, axis=1), ((0, 0), (0, ROWS_REST - ROWS_USED), (0, 0)))


def _rope_consts(rot, first, period):
    half = rot // 2
    inv_freq = np.float32(ROPE_THETA) ** (-(np.arange(0, rot, 2, dtype=np.float32) / np.float32(rot)))
    lane = np.arange(LANES) % period - first
    in_rot = (lane >= 0) & (lane < rot)
    out = np.zeros((8, LANES), np.float32)
    out[0] = np.where(in_rot, inv_freq[np.clip(lane, 0, rot - 1) % half], 0.0)
    out[1] = in_rot & (lane < half)
    out[2] = in_rot & (lane >= half)
    return jnp.asarray(out)


def _band_bias(s):
    nblk = s // BAND_Q
    starts = np.array([_band_start(i, s) for i in range(nblk)])
    uq = (np.arange(nblk)[:, None] * BAND_Q + np.arange(BAND_Q)[None, :])[:, :, None]
    uk = (starts[:, None] + np.arange(BAND_WIN)[None, :])[:, None, :]
    tiles, index, seen = [], [], {}
    for _, d in DILATED:
        length = s // d
        ok = (uq // length == uk // length) & (np.abs(uq - uk) <= 64)
        row = []
        for i in range(nblk):
            key = ok[i].tobytes()
            if key not in seen:
                seen[key] = len(tiles)
                tiles.append(np.where(ok[i], 0.0, NEG_INF).astype(np.float32))
            row.append(seen[key])
        index.append(row)
    return jnp.asarray(np.stack(tiles, axis=0)), index


def _forward_backward(h, proj, trig, rope_consts, x, mem, target, weights, gains):
    w_uq_pad_t, w_ukv, w_mem, w_out = weights
    g_emb, b_emb, g_cq, g_ckv, g_out_a, g_out_b, g_out_m, g_post, b_post = gains
    nb, s, d = x.shape
    t = nb * s
    x2 = x.reshape(t, d)
    mem2 = mem.reshape(nb * N_MEM, d)
    tgt2 = target.reshape(t, d)
    rope_a, rope_b = rope_consts
    bias, bias_index = _band_bias(s)
    scales = (0.125, MLA_QK_DIM ** -0.5, 128 ** -0.5)

    qa, ka, va, qb, kb, vb, qm, cqn, ckvn = _prep(proj, trig, w_uq_pad_t, w_ukv, g_cq, g_ckv, rope_a, rope_b, scales)
    mkv = _mm(mem2, w_mem, BF16, nb * N_MEM, 1024, 1024, "mem_kv")

    cfg_b = dict(nb=nb, s=s, sk=s, heads=8, voff=0, bq=256)
    cfg_m = dict(nb=nb, s=s, sk=N_MEM, heads=4, hpb=2, voff=4, bq=1024)
    ya, lse_a, qkv_ordered = _dilated_fwd(qa, ka, va, bias, bias_index, nb=nb, s=s, name="attn_a_fwd")
    yb, lse_b = _attn_fwd(qb, kb, vb, name="attn_b_fwd", hpb=4, **cfg_b)
    ym, lse_m = _attn_fwd(qm, mkv, mkv, name="attn_m_fwd", **cfg_m)

    (y, dz, doa, dob, dom, dga, dgb, dgm, loss, dg_post, db_post, dg_a, dg_b, dg_m) = _post(
        x2, ya, yb, ym, proj, tgt2, w_out, g_emb, b_emb, g_out_a, g_out_b, g_out_m, g_post, b_post)

    dqa, dka, dva = _dilated_bwd(qa, ka, va, qkv_ordered, ya, doa, lse_a, bias, bias_index, nb=nb, s=s, scale=scales[0],
                                 name="attn_a_bwd")
    dqb, dkb, dvb = _attn_bwd(qb, kb, vb, yb, dob, lse_b, name="attn_b_bwd", scale=scales[1], hpb=4, **cfg_b)
    dqm, dmk, dmv = _attn_bwd(qm, mkv, mkv, ym, dom, lse_m, name="attn_m_bwd", scale=scales[2], **cfg_m)
    dmkv = jnp.concatenate([dmk, dmv], axis=1)

    dproj, dqf, dkv, dg_cq, dg_ckv = _prep_bwd(
        dqa, dka, dva, dqb, dkb, dvb, dqm, dga, dgb, dgm, proj, trig, w_uq_pad_t, w_ukv, g_cq, g_ckv, rope_a, rope_b)

    small_rows = (dg_cq, dg_ckv, loss, dg_a, dg_b, dg_m, dg_post, db_post)
    return (dproj, h, y, dz, dqf, cqn, ckvn, dkv, mem2, dmkv), x2, small_rows


def _weight_grads(operands, core):
    dproj, h, y, dz, dqf, cqn, ckvn, dkv, mem2, dmkv = operands
    dw_in_arr_t = _mm(dproj, h, F32, 1024, 1024, 4096, "dw_in", mode="tn")
    g_in = _split_in(dw_in_arr_t)
    dw_out, r_in = _mm(y, dz, F32, 1024, 1024, 2048, "dw_out", mode="tn",
                       ride=_half_to_sibling(g_in.reshape(4, 2, HALF_IN, 1024)))
    dw_uq_pad_t = _mm(dqf, cqn, F32, 1024, 256, 4096, "dw_uq", mode="tn")
    dw_ukv = _mm(ckvn, dkv, F32, 128, 1024, 4096, "dw_ukv", mode="tn")
    dw_mem = _mm(mem2, dmkv, F32, 1024, 1024, mem2.shape[0], "dw_mem", mode="tn")
    g_rest = _split_rest(dw_uq_pad_t, dw_ukv, dw_mem, dw_out)
    sf_in, sb_in, r_rest = _core_sum(g_in, r_in, core, HALF_IN, HALF_IN // 2, "core_sum_in",
                                     ride=_half_to_sibling(g_rest.reshape(4, 2, HALF_REST, 1024)))
    sf_rest, sb_rest = _core_sum(g_rest, r_rest, core, HALF_REST, HALF_REST, "core_sum_rest")
    return sf_in, sb_in, sf_rest, sb_rest


def _small_block(dg_emb, db_emb, small_rows):
    dg_cq, dg_ckv, loss, dg_a, dg_b, dg_m, dg_post, db_post = small_rows
    row2 = jnp.concatenate([dg_cq, dg_ckv, loss, jnp.zeros((1, 512), F32)], axis=1)
    return jnp.concatenate([dg_emb, db_emb, row2, dg_a, jnp.concatenate([dg_b, dg_m], axis=1), dg_post, db_post,
                            jnp.zeros((1, 1024), F32)], axis=0)


def _pack_small(g_emb, b_emb, g_cq, g_ckv, g_out_a, g_out_b, g_out_m, g_post, b_post):
    row2 = jnp.concatenate([g_cq.reshape(1, -1), g_ckv.reshape(1, -1), jnp.zeros((1, 640), F32)], axis=1)
    return jnp.concatenate([g_emb.reshape(1, -1), b_emb.reshape(1, -1), row2, g_out_a.reshape(1, -1),
                            jnp.concatenate([g_out_b.reshape(1, -1), g_out_m.reshape(1, -1)], axis=1),
                            g_post.reshape(1, -1), b_post.reshape(1, -1), jnp.zeros((1, 1024), F32)], axis=0)


def kernel(x, mem, positions, g_emb, b_emb, w_in, g_cq, g_ckv, w_uq, w_ukv, w_mem_kv, g_out_a, g_out_b, g_out_m, w_out, g_post, b_post, loss_target, m_g_emb, m_b_emb, m_w_in, m_g_cq, m_g_ckv, m_w_uq, m_w_ukv, m_w_mem_kv, m_g_out_a, m_g_out_b, m_g_out_m, m_w_out, m_g_post, m_b_post, v_g_emb, v_b_emb, v_w_in, v_g_cq, v_g_ckv, v_w_uq, v_w_ukv, v_w_mem_kv, v_g_out_a, v_g_out_b, v_g_out_m, v_w_out, v_g_post, v_b_post):
    w_rest = _pack_rest(w_uq, w_ukv, w_mem_kv, w_out)
    w_in_t = w_in[0].T
    w_in_b = jnp.pad(w_in_t.astype(BF16), ((0, ROWS_IN - SHARD_ROWS), (0, 0)))
    gains = (g_emb.reshape(1, -1), b_emb.reshape(1, -1), g_cq, g_ckv, g_out_a, g_out_b, g_out_m, g_post, b_post)
    rope_consts = (_rope_consts(16, 0, 64), _rope_consts(32, 64, 128))
    h, trig, gathered_in = _ln_fwd(x.reshape(-1, D_MODEL), gains[0], gains[1],
                                   positions.reshape(-1, 1).astype(F32), *rope_consts,
                                   ride=_gather_ride(w_in_b.reshape(2, HALF_IN, 1024), spread=False))
    w_in_arr_t = _arranged_w_in(gathered_in.reshape(4, ROWS_IN, 1024))
    proj, gathered_rest = _mm(h, w_in_arr_t, F32, 1024, 2048, 1024, "in_proj", mode="nt",
                              ride=_gather_ride(w_rest.astype(BF16).reshape(2, HALF_REST, 1024), spread=True))
    weights = _rest_weights(gathered_rest.reshape(4, ROWS_REST, 1024))
    operands, x2, small_rows = _forward_backward(h, proj, trig, rope_consts, x, mem, loss_target, weights, gains)

    core = lax.axis_index("c").astype(jnp.int32).reshape(1)
    chip = (2 * lax.axis_index("x") + lax.axis_index("y")).astype(jnp.int32).reshape(1)
    sf_in, sb_in, sf_rest, sb_rest = _weight_grads(operands, core)
    grad_x, dg_emb, db_emb, rb_in, rb_rest = _dh_scatter(operands[0], w_in_arr_t, x2, operands[3], gains[0],
                                                         sb_in, sb_rest)
    gh_in = _chip_sum(sf_in, rb_in, chip, HALF_IN, HALF_IN // 2, "chip_sum_in")
    gh_rest = _chip_sum(sf_rest, rb_rest, chip, HALF_REST, HALF_REST, "chip_sum_rest")
    grad_in, grad_rest = _join_halves(gh_in, gh_rest)
    grad_in = grad_in.reshape(ROWS_IN, 1024)
    grad_rest = grad_rest.reshape(ROWS_REST, 1024)

    big_in = _adamw(grad_in, w_in_t, m_w_in[0].T, v_w_in[0].T, SHARD_ROWS // 3, "adamw_in")
    uq, ukv, wmem, wout = _adamw_pieces(
        grad_rest, w_rest, _pack_rest(m_w_uq, m_w_ukv, m_w_mem_kv, m_w_out),
        _pack_rest(v_w_uq, v_w_ukv, v_w_mem_kv, v_w_out), REST_PIECES, "adamw_rest")
    small_sum = _allreduce_small(_small_block(dg_emb, db_emb, small_rows))
    sm = _adamw_pieces(
        small_sum,
        _pack_small(g_emb, b_emb, g_cq, g_ckv, g_out_a, g_out_b, g_out_m, g_post, b_post),
        _pack_small(m_g_emb, m_b_emb, m_g_cq, m_g_ckv, m_g_out_a, m_g_out_b, m_g_out_m, m_g_post, m_b_post),
        _pack_small(v_g_emb, v_b_emb, v_g_cq, v_g_ckv, v_g_out_a, v_g_out_b, v_g_out_m, v_g_post, v_b_post),
        SMALL_PIECES, "adamw_small")
    loss = small_sum[2, 384]

    def ordered(kind):
        s_gemb, s_bemb, s_gcq, s_gckv, s_ga, s_gb, s_gm, s_gpost, s_bpost = [piece[kind] for piece in sm]
        return [s_gemb.reshape(-1), s_bemb.reshape(-1), big_in[kind].T[None], s_gcq, s_gckv,
                uq[kind].reshape(192, 256).T[None], ukv[kind].reshape(1, 128, 256), wmem[kind][None], s_ga, s_gb,
                s_gm, wout[kind][None], s_gpost, s_bpost]

    return (loss, grad_x.reshape(x.shape), *ordered(0), *ordered(1), *ordered(2), *ordered(3))
```

```python
import functools
import math

import jax
import jax.numpy as jnp
import numpy as np
from jax import lax
from jax.experimental import pallas as pl
from jax.experimental.pallas import tpu as pltpu

F32 = jnp.float32
BF16 = jnp.bfloat16
MESH = pl.DeviceIdType.MESH
ANY = pl.BlockSpec(memory_space=pl.ANY)
IN_VMEM = pl.BlockSpec(memory_space=pltpu.VMEM)

D_MODEL = 1024
A_WIDTH = 1024
MLA_HEADS = 8
MLA_Q_RANK = 256
MLA_KV_RANK = 128
MLA_QK_DIM = 96
MEM_WIDTH = 512
N_MEM = 256
ROPE_THETA = 500000.0
NORM_EPS = 1e-5
NEG_INF = -1e30
DEEPNORM_ALPHA = 2.0 ** 0.25
DILATED = ((64, 1), (256, 4), (1024, 16))

ADAM_LR = 0.001
ADAM_B1 = 0.9
ADAM_B2 = 0.999
ADAM_EPS = 1e-08
ADAM_WD = 0.01
ADAM_STEP = 10

LANES = 128
VMEM_LIMIT = 56 * 1024 * 1024
LOG2E = math.log2(math.e)
LN2 = math.log(2.0)

PROJ_W = 6144
COL_CQ = 4096
COL_BG = 4608
COL_MQ = 5120
COL_MG = 5632

SHARD_ROWS = 1512
ROWS_IN = 1536
ROWS_UQ, ROWS_UKV, ROWS_MEM, ROWS_OUT = 48, 32, 256, 512
ROWS_USED = ROWS_UQ + ROWS_UKV + ROWS_MEM + ROWS_OUT
ROWS_REST = 864
HALF_IN = ROWS_IN // 2
HALF_REST = ROWS_REST // 2
REST_PIECES = ((0, 48, 0, 1024), (48, 80, 0, 1024), (80, 336, 0, 1024), (336, 848, 0, 1024))
SMALL_PIECES = ((0, 1, 0, 1024), (1, 2, 0, 1024), (2, 3, 0, 256), (2, 3, 256, 384), (3, 4, 0, 1024), (4, 5, 0, 512),
                (4, 5, 512, 1024), (5, 6, 0, 1024), (6, 7, 0, 1024))


def _params(sem=None, vmem=VMEM_LIMIT):
    return pltpu.CompilerParams(dimension_semantics=sem, vmem_limit_bytes=vmem)


def _dot(a, b):
    return jnp.dot(a, b, preferred_element_type=F32)


def _dot_nt(a, b):
    return lax.dot_general(a, b, (((1,), (1,)), ((), ())), preferred_element_type=F32)


def _dot_tn(a, b):
    return lax.dot_general(a, b, (((0,), (0,)), ((), ())), preferred_element_type=F32)


def _ln_hat(x):
    mu = jnp.mean(x, axis=-1, keepdims=True)
    xc = x - mu
    var = jnp.mean(xc * xc, axis=-1, keepdims=True)
    rstd = lax.rsqrt(var + NORM_EPS)
    return xc * rstd, rstd


def _ln_bwd_rows(dxh, xh, rstd):
    return rstd * (dxh - jnp.mean(dxh, axis=-1, keepdims=True) - xh * jnp.mean(dxh * xh, axis=-1, keepdims=True))


def _rms_hat(x, width):
    ms = jnp.sum(x * x, axis=-1, keepdims=True) * (1.0 / width)
    r = lax.rsqrt(ms + NORM_EPS)
    return x * r, r


def _rms_bwd(u, xh, r, width):
    return r * (u - xh * (jnp.sum(u * xh, axis=-1, keepdims=True) * (1.0 / width)))


def _colsum(v):
    return jnp.sum(v, axis=0, keepdims=True)


def _rope_tables(cos, sin, consts):
    return cos, sin * consts[2:3, :], -sin * consts[1:2, :]


def _rope(x, tables, half, inverse=False):
    c, s_up, s_dn = tables
    if inverse:
        s_up, s_dn = -s_up, -s_dn
    return x * c + pltpu.roll(x, half, 1) * s_up + pltpu.roll(x, LANES - half, 1) * s_dn


def _ln_fwd(x, g, b, pos, rope_a, rope_b, tm=512, ride=None):
    t, d = x.shape
    n_in = len(ride.args) if ride else 0
    n_out = len(ride.out_shapes) if ride else 0
    steps = t // tm

    def body(x_ref, g_ref, b_ref, pos_ref, ra_ref, rb_ref, *rest):
        h_ref, trig_ref = rest[n_in], rest[n_in + 1]
        if ride:
            i = pl.program_id(0)
            ride.run(i, steps, rest[:n_in], rest[n_in + 2:n_in + 2 + n_out], rest[n_in + 2 + n_out:])
        xh, _ = _ln_hat(x_ref[...])
        h_ref[...] = (xh * g_ref[...] + b_ref[...]).astype(BF16)
        for j, consts in enumerate((ra_ref, rb_ref)):
            ang = pos_ref[...] * consts[0:1, :]
            trig_ref[:, 2 * j * LANES:(2 * j + 1) * LANES] = jnp.cos(ang)
            trig_ref[:, (2 * j + 1) * LANES:(2 * j + 2) * LANES] = jnp.sin(ang)

    row = pl.BlockSpec((1, d), lambda i: (0, 0))
    tile = pl.BlockSpec((tm, d), lambda i: (i, 0))
    consts = pl.BlockSpec((8, LANES), lambda i: (0, 0))
    trig_tile = pl.BlockSpec((tm, 4 * LANES), lambda i: (i, 0))
    in_specs = [tile, row, row, pl.BlockSpec((tm, 1), lambda i: (i, 0)), consts, consts]
    shapes = (jax.ShapeDtypeStruct((t, d), BF16), jax.ShapeDtypeStruct((t, 4 * LANES), F32))
    if not ride:
        return pl.pallas_call(
            body, name="ln_fwd", grid=(steps,), out_shape=shapes, in_specs=in_specs, out_specs=(tile, trig_tile),
            compiler_params=_params(("parallel",)),
        )(x, g, b, pos, rope_a, rope_b)
    return pl.pallas_call(
        body, name="ln_fwd", grid=(steps,),
        out_shape=(*shapes, *ride.out_shapes),
        in_specs=in_specs + ride.in_specs, out_specs=(tile, trig_tile) + (ANY,) * n_out,
        scratch_shapes=ride.scratch(),
        compiler_params=_params(("arbitrary",)),
    )(x, g, b, pos, rope_a, rope_b, *ride.args)


class _Ride:
    def __init__(self, args, out_shapes, sem_counts, plan, in_specs=None, spread=True):
        self.args, self.out_shapes, self.plan = list(args), list(out_shapes), plan
        self.sem_counts = sem_counts
        self.in_specs = in_specs or [ANY] * len(self.args)
        self.spread = spread

    def scratch(self):
        return [pltpu.SemaphoreType.DMA((n,)) for n in self.sem_counts]

    def run(self, step, total, in_refs, out_refs, sems):
        count = len(self.plan(in_refs, out_refs, *sems))
        at = [(k * (total - 1)) // (count - 1) if self.spread or k == 0 else total - 1 for k in range(count)]
        for when in sorted(set(at)):
            @pl.when(step == when)
            def _(when=when):
                stages = self.plan(in_refs, out_refs, *sems)
                for k in range(count):
                    if at[k] == when:
                        stages[k]()


def _mm(a, b, out_dtype, tm, tn, tk, name, mode="nn", ride=None):
    if mode == "tn":
        k, m = a.shape
    else:
        m, k = a.shape
    n = b.shape[0] if mode == "nt" else b.shape[1]
    nk = k // tk
    nj, ni = n // tn, m // tm
    n_in = len(ride.args) if ride else 0
    n_out = len(ride.out_shapes) if ride else 0

    def body(a_ref, b_ref, *rest):
        o_ref = rest[n_in]
        acc_ref = rest[n_in + 1 + n_out]
        if ride:
            j, i, kk = pl.program_id(0), pl.program_id(1), pl.program_id(2)
            ride.run((j * ni + i) * nk + kk, nj * ni * nk, rest[:n_in], rest[n_in + 1:n_in + 1 + n_out],
                     rest[n_in + 2 + n_out:])
        av = a_ref[...].astype(BF16)
        bv = b_ref[...].astype(BF16)
        part = _dot_tn(av, bv) if mode == "tn" else _dot_nt(av, bv) if mode == "nt" else _dot(av, bv)
        if nk == 1:
            o_ref[...] = part.astype(out_dtype)
        else:
            kk = pl.program_id(2)

            @pl.when(kk == 0)
            def _():
                acc_ref[...] = part

            @pl.when(kk > 0)
            def _():
                acc_ref[...] += part

            @pl.when(kk == nk - 1)
            def _():
                o_ref[...] = acc_ref[...].astype(out_dtype)

    a_spec = (pl.BlockSpec((tk, tm), lambda j, i, kk: (kk, i)) if mode == "tn"
              else pl.BlockSpec((tm, tk), lambda j, i, kk: (i, kk)))
    b_spec = (pl.BlockSpec((tn, tk), lambda j, i, kk: (j, kk)) if mode == "nt"
              else pl.BlockSpec((tk, tn), lambda j, i, kk: (kk, j)))
    o_spec = pl.BlockSpec((tm, tn), lambda j, i, kk: (i, j))
    o_shape = jax.ShapeDtypeStruct((m, n), out_dtype)
    if not ride:
        return pl.pallas_call(
            body, name=name, grid=(nj, ni, nk), out_shape=o_shape, in_specs=[a_spec, b_spec], out_specs=o_spec,
            scratch_shapes=[pltpu.VMEM((tm, tn), F32)],
            compiler_params=_params(("parallel", "parallel", "arbitrary")),
        )(a, b)
    return pl.pallas_call(
        body, name=name, grid=(nj, ni, nk),
        out_shape=(o_shape, *ride.out_shapes),
        in_specs=[a_spec, b_spec] + ride.in_specs,
        out_specs=(o_spec,) + (ANY,) * n_out,
        scratch_shapes=[pltpu.VMEM((tm, tn), F32)] + ride.scratch(),
        compiler_params=_params(("arbitrary", "arbitrary", "arbitrary")),
    )(a, b, *ride.args)


def _prep(proj, trig, w_uq, w_ukv, g_cq, g_ckv, rope_a, rope_b, scales, tm=256):
    t = proj.shape[0]
    sc_a, sc_b, sc_m = (s * LOG2E for s in scales)

    def body(aq_ref, ak_ref, av_ref, bs_ref, mq_ref, trig_ref, wuq_ref, wukv_ref, gcq_ref, gckv_ref,
             ra_ref, rb_ref, qa_ref, ka_ref, va_ref, qb_ref, kb_ref, vb_ref, qm_ref, cqn_ref, ckvn_ref):
        ta = _rope_tables(trig_ref[:, 0:LANES], trig_ref[:, LANES:2 * LANES], ra_ref[...])
        tb = _rope_tables(trig_ref[:, 2 * LANES:3 * LANES], trig_ref[:, 3 * LANES:4 * LANES], rb_ref[...])
        for j in range(A_WIDTH // LANES):
            sl = slice(j * LANES, (j + 1) * LANES)
            qa_ref[:, sl] = (_rope(aq_ref[:, sl], ta, 8) * sc_a).astype(BF16)
            ka_ref[:, sl] = _rope(ak_ref[:, sl], ta, 8).astype(BF16)
        va_ref[...] = av_ref[...].astype(BF16)
        qm_ref[...] = (mq_ref[...] * sc_m).astype(BF16)

        cq_hat, _ = _rms_hat(bs_ref[:, 0:MLA_Q_RANK], MLA_Q_RANK)
        cqn = (cq_hat * gcq_ref[...]).astype(BF16)
        cqn_ref[...] = cqn
        ckv_hat, _ = _rms_hat(bs_ref[:, MLA_Q_RANK:MLA_Q_RANK + MLA_KV_RANK], MLA_KV_RANK)
        ckvn = (ckv_hat * gckv_ref[...]).astype(BF16)
        ckvn_ref[...] = ckvn
        qfull = _dot_nt(cqn, wuq_ref[...])
        kv = _dot(ckvn, wukv_ref[...])
        kr = _rope(bs_ref[:, 384:512], tb, 16)
        lane = lax.broadcasted_iota(jnp.int32, (1, LANES), 1)
        low = lane < 64
        for h in range(MLA_HEADS):
            sl = slice(h * LANES, (h + 1) * LANES)
            qb_ref[:, sl] = (_rope(qfull[:, sl], tb, 16) * sc_b).astype(BF16)
            kb_ref[:, sl] = jnp.where(low, kv[:, sl], kr).astype(BF16)
            vb_ref[:, sl] = jnp.where(low, 0.0, kv[:, sl]).astype(BF16)

    def col(width, idx):
        return pl.BlockSpec((tm, width), lambda i: (i, idx))

    def full(shape):
        return pl.BlockSpec(shape, lambda i: (0, 0))

    wide = jax.ShapeDtypeStruct((t, 1024), BF16)
    return pl.pallas_call(
        body, name="prep", grid=(t // tm,),
        out_shape=(wide, wide, wide, wide, wide, wide,
                   jax.ShapeDtypeStruct((t, MEM_WIDTH), BF16),
                   jax.ShapeDtypeStruct((t, MLA_Q_RANK), BF16),
                   jax.ShapeDtypeStruct((t, MLA_KV_RANK), BF16)),
        in_specs=[col(1024, 0), col(1024, 1), col(1024, 2), col(512, COL_CQ // 512), col(512, COL_MQ // 512),
                  pl.BlockSpec((tm, 4 * LANES), lambda i: (i, 0)),
                  full((1024, MLA_Q_RANK)), full((MLA_KV_RANK, 1024)),
                  full((1, MLA_Q_RANK)), full((1, MLA_KV_RANK)), full((8, LANES)), full((8, LANES))],
        out_specs=(col(1024, 0),) * 6 + (col(MEM_WIDTH, 0), col(MLA_Q_RANK, 0), col(MLA_KV_RANK, 0)),
        compiler_params=_params(("parallel",)),
    )(proj, proj, proj, proj, proj, trig, w_uq, w_ukv, g_cq, g_ckv, rope_a, rope_b)


def _attn_fwd(q, k, v, *, nb, s, sk, heads, hpb, voff, bq, name):
    nq = s // bq
    width = hpb * LANES
    vblk = voff // hpb

    def body(q_ref, k_ref, v_ref, o_ref, lse_ref):
        for h in range(hpb):
            sl = slice(h * LANES, (h + 1) * LANES)
            sc = _dot_nt(q_ref[:, sl], k_ref[:, sl])
            m = jnp.max(sc, axis=1, keepdims=True)
            p = jnp.exp2(sc - m)
            l = jnp.sum(p, axis=1, keepdims=True)
            o_ref[:, sl] = _dot(p.astype(BF16), v_ref[:, sl]) / l
            lse_ref[:, sl] = jnp.broadcast_to(m + jnp.log(l) * LOG2E, (bq, LANES))

    out = jax.ShapeDtypeStruct((nb * s, heads * LANES), F32)
    ospec = pl.BlockSpec((bq, width), lambda b, i, g: (b * nq + i, g))
    return pl.pallas_call(
        body, name=name, grid=(nb, nq, heads // hpb),
        out_shape=(out, out),
        in_specs=[ospec, pl.BlockSpec((sk, width), lambda b, i, g: (b, g)),
                  pl.BlockSpec((sk, width), lambda b, i, g: (b, vblk + g))],
        out_specs=(ospec, ospec),
        compiler_params=_params(("parallel", "parallel", "parallel")),
    )(q, k, v)


def _attn_bwd(q, k, v, o, do, lse, *, nb, s, sk, heads, hpb, voff, scale, bq, name):
    nq = s // bq
    width = hpb * LANES
    vblk = voff // hpb

    def body(q_ref, k_ref, v_ref, o_ref, do_ref, lse_ref, dq_ref, dk_ref, dv_ref, dk_acc, dv_acc):
        i = pl.program_id(2)

        @pl.when(i == 0)
        def _():
            dk_acc[...] = jnp.zeros_like(dk_acc)
            dv_acc[...] = jnp.zeros_like(dv_acc)

        for h in range(hpb):
            sl = slice(h * LANES, (h + 1) * LANES)
            qh = q_ref[:, sl]
            kk = k_ref[:, sl]
            doh = do_ref[:, sl]
            delta = jnp.sum(doh.astype(F32) * o_ref[:, sl], axis=1, keepdims=True)
            p = jnp.exp2(_dot_nt(qh, kk) - lse_ref[:, h * LANES:h * LANES + 1])
            ds = (p * (_dot_nt(doh, v_ref[:, sl]) - delta)).astype(BF16)
            dq_ref[:, sl] = (_dot(ds, kk) * scale).astype(BF16)
            dk_acc[:, sl] += _dot_tn(ds, qh)
            dv_acc[:, sl] += _dot_tn(p.astype(BF16), doh)

        @pl.when(i == nq - 1)
        def _():
            dk_ref[...] = (dk_acc[...] * LN2).astype(BF16)
            dv_ref[...] = dv_acc[...].astype(BF16)

    qspec = pl.BlockSpec((bq, width), lambda b, g, i: (b * nq + i, g))
    kv_spec = pl.BlockSpec((sk, width), lambda b, g, i: (b, g))
    dq_shape = jax.ShapeDtypeStruct((nb * s, heads * LANES), BF16)
    dkv_shape = jax.ShapeDtypeStruct((nb * sk, heads * LANES), BF16)
    return pl.pallas_call(
        body, name=name, grid=(nb, heads // hpb, nq),
        out_shape=(dq_shape, dkv_shape, dkv_shape),
        in_specs=[qspec, kv_spec, pl.BlockSpec((sk, width), lambda b, g, i: (b, vblk + g)), qspec, qspec, qspec],
        out_specs=(qspec, kv_spec, kv_spec),
        scratch_shapes=[pltpu.VMEM((sk, width), F32), pltpu.VMEM((sk, width), F32)],
        compiler_params=_params(("parallel", "parallel", "arbitrary")),
    )(q, k, v, o, do, lse)


BAND_Q = 128
BAND_WIN = 256


def _band_start(i, s):
    return min(max(i * BAND_Q - 64, 0), s - BAND_WIN)


def _to_pattern_order(src_ref, dst_ref, stage_ref, s, d):
    length = s // d
    stage_ref[...] = src_ref[...].astype(F32)
    for r in range(d):
        dst_ref[r * length:(r + 1) * length, :] = stage_ref[pl.ds(r, length, stride=d), :].astype(dst_ref.dtype)


def _dilated_fwd(q, k, v, bias, bias_index, *, nb, s, name):
    nblk = s // BAND_Q
    npat = len(DILATED)

    def body(q_ref, k_ref, v_ref, bias_ref, o_ref, lse_ref, *rest):
        ordered = rest[:3 * (npat - 1)]
        stage_ref, op_ref, lp_ref, on_ref, ln_ref = rest[3 * (npat - 1):]
        lane = lax.broadcasted_iota(jnp.int32, (1, LANES), 1)
        first = lane < 64
        for p, (_, d) in enumerate(DILATED):
            if d == 1:
                qs, ks, vs = q_ref, k_ref, v_ref
            else:
                qs, ks, vs = ordered[3 * (p - 1):3 * p]
                for src, dst in ((q_ref, qs), (k_ref, ks), (v_ref, vs)):
                    _to_pattern_order(src, dst, stage_ref, s, d)
            for i in range(nblk):
                u0 = i * BAND_Q
                st = _band_start(i, s)
                qi = qs[u0:u0 + BAND_Q, :]
                kw = ks[st:st + BAND_WIN, :]
                vw = vs[st:st + BAND_WIN, :]
                zero = jnp.zeros_like(qi)
                q2 = jnp.concatenate([jnp.where(first, qi, zero), jnp.where(first, zero, qi)], axis=0)
                sc = _dot_nt(q2, kw)
                b = bias_ref[bias_index[p][i]]
                halves = []
                for h in range(2):
                    sh = sc[h * BAND_Q:(h + 1) * BAND_Q] + b
                    m = jnp.max(sh, axis=1, keepdims=True)
                    pr = jnp.exp2(sh - m)
                    l = jnp.sum(pr, axis=1, keepdims=True)
                    halves.append((pr.astype(BF16), l, m + jnp.log(l) * LOG2E))
                o2 = _dot(jnp.concatenate([halves[0][0], halves[1][0]], axis=0), vw)
                o_blk = jnp.where(first, o2[:BAND_Q] / halves[0][1], o2[BAND_Q:] / halves[1][1])
                lse_blk = jnp.where(first, jnp.broadcast_to(halves[0][2], (BAND_Q, LANES)),
                                    jnp.broadcast_to(halves[1][2], (BAND_Q, LANES)))
                op_ref[p, u0:u0 + BAND_Q, :] = o_blk
                lp_ref[p, u0:u0 + BAND_Q, :] = lse_blk
            if d > 1:
                length = s // d
                for r in range(d):
                    on_ref.at[p - 1][pl.ds(r, length, stride=d), :] = op_ref[p, r * length:(r + 1) * length, :]
                    ln_ref.at[p - 1][pl.ds(r, length, stride=d), :] = lp_ref[p, r * length:(r + 1) * length, :]
        lses = [lp_ref[0]] + [ln_ref[p] for p in range(npat - 1)]
        outs = [op_ref[0]] + [on_ref[p] for p in range(npat - 1)]
        m = functools.reduce(jnp.maximum, lses)
        ws = [jnp.exp2(l - m) for l in lses]
        den = functools.reduce(lambda a, c: a + c, ws)
        o_ref[...] = functools.reduce(lambda a, c: a + c, [w * o for w, o in zip(ws, outs)]) / den
        lse_ref[...] = m + jnp.log(den) * LOG2E

    blk = pl.BlockSpec((s, LANES), lambda b, g: (b, g))
    out = jax.ShapeDtypeStruct((nb * s, A_WIDTH), F32)
    copy = jax.ShapeDtypeStruct((nb * s, A_WIDTH), BF16)
    n_copies = 3 * (npat - 1)
    res = pl.pallas_call(
        body, name=name, grid=(nb, A_WIDTH // LANES),
        out_shape=(out, out) + (copy,) * n_copies,
        in_specs=[blk, blk, blk, pl.BlockSpec(bias.shape, lambda b, g: (0, 0, 0))],
        out_specs=(blk, blk) + (blk,) * n_copies,
        scratch_shapes=[pltpu.VMEM((s, LANES), F32), pltpu.VMEM((npat, s, LANES), F32),
                        pltpu.VMEM((npat, s, LANES), F32), pltpu.VMEM((npat - 1, s, LANES), F32),
                        pltpu.VMEM((npat - 1, s, LANES), F32)],
        compiler_params=_params(("parallel", "parallel")),
    )(q, k, v, bias)
    return res[0], res[1], res[2:]


def _dilated_bwd(q, k, v, ordered, o, do, lse, bias, bias_index, *, nb, s, scale, name):
    nblk = s // BAND_Q
    npat = len(DILATED)
    n_copies = 3 * (npat - 1)

    def body(q_ref, k_ref, v_ref, *rest):
        ordered_refs = rest[:n_copies]
        (o_ref, do_ref, lse_ref, bias_ref, dq_out, dk_out, dv_out, stage_ref, rs_ref, dop_ref, rsp_ref,
         dqp_ref, dkp_ref, dvp_ref, dq_ref, dk_ref, dv_ref, nat_ref) = rest[n_copies:]
        lane = lax.broadcasted_iota(jnp.int32, (1, LANES), 1)
        first = lane < 64
        prod = do_ref[...].astype(F32) * o_ref[...]
        d0 = jnp.sum(jnp.where(first, prod, 0.0), axis=1, keepdims=True)
        d1 = jnp.sum(jnp.where(first, 0.0, prod), axis=1, keepdims=True)
        delta = jnp.where(first, jnp.broadcast_to(d0, (s, LANES)), jnp.broadcast_to(d1, (s, LANES)))
        rs_ref[...] = jnp.where((lane & 32) == 0, lse_ref[...], delta)
        for p, (_, d) in enumerate(DILATED):
            length = s // d
            if d == 1:
                qs, ks, vs, dos, rss = q_ref, k_ref, v_ref, do_ref, rs_ref
                dqs, dks, dvs = dq_ref, dk_ref, dv_ref
            else:
                for src, dst in ((do_ref, dop_ref), (rs_ref, rsp_ref)):
                    _to_pattern_order(src, dst, stage_ref, s, d)
                qs, ks, vs = ordered_refs[3 * (p - 1):3 * p]
                dos, rss = dop_ref, rsp_ref
                dqs, dks, dvs = dqp_ref, dkp_ref, dvp_ref
            dks[...] = jnp.zeros((s, LANES), F32)
            dvs[...] = jnp.zeros((s, LANES), F32)
            for i in range(nblk):
                u0 = i * BAND_Q
                st = _band_start(i, s)
                qi = qs[u0:u0 + BAND_Q, :]
                doi = dos[u0:u0 + BAND_Q, :]
                kw = ks[st:st + BAND_WIN, :]
                vw = vs[st:st + BAND_WIN, :]
                zero = jnp.zeros_like(qi)
                q2 = jnp.concatenate([jnp.where(first, qi, zero), jnp.where(first, zero, qi)], axis=0)
                do2 = jnp.concatenate([jnp.where(first, doi, zero), jnp.where(first, zero, doi)], axis=0)
                sc = _dot_nt(q2, kw)
                dp = _dot_nt(do2, vw)
                b = bias_ref[bias_index[p][i]]
                rs_i = rss[u0:u0 + BAND_Q, :]
                ps, dss = [], []
                for h in range(2):
                    rows = slice(h * BAND_Q, (h + 1) * BAND_Q)
                    pr = jnp.exp2(sc[rows] + b - rs_i[:, 64 * h:64 * h + 1])
                    ps.append(pr.astype(BF16))
                    dss.append((pr * (dp[rows] - rs_i[:, 64 * h + 32:64 * h + 33])).astype(BF16))
                p2 = jnp.concatenate(ps, axis=0)
                ds2 = jnp.concatenate(dss, axis=0)
                dq2 = _dot(ds2, kw)
                dqs[u0:u0 + BAND_Q, :] = jnp.where(first, dq2[:BAND_Q], dq2[BAND_Q:]) * scale
                dks[st:st + BAND_WIN, :] += _dot_tn(ds2, q2)
                dvs[st:st + BAND_WIN, :] += _dot_tn(p2, do2)
            if d > 1:
                for j, src in enumerate((dqp_ref, dkp_ref, dvp_ref)):
                    for r in range(d):
                        nat_ref.at[p - 1, j][pl.ds(r, length, stride=d), :] = src[r * length:(r + 1) * length, :]

        def total(j, first_ref):
            return functools.reduce(lambda a, c: a + c, [first_ref[...]] + [nat_ref[p, j] for p in range(npat - 1)])

        dq_out[...] = total(0, dq_ref).astype(BF16)
        dk_out[...] = (total(1, dk_ref) * LN2).astype(BF16)
        dv_out[...] = total(2, dv_ref).astype(BF16)

    blk = pl.BlockSpec((s, LANES), lambda b, g: (b, g))
    out = jax.ShapeDtypeStruct((nb * s, A_WIDTH), BF16)
    f32_buf = pltpu.VMEM((s, LANES), F32)
    bf_buf = pltpu.VMEM((s, LANES), BF16)
    return pl.pallas_call(
        body, name=name, grid=(nb, A_WIDTH // LANES),
        out_shape=(out, out, out),
        in_specs=[blk] * (6 + n_copies) + [pl.BlockSpec(bias.shape, lambda b, g: (0, 0, 0))],
        out_specs=(blk, blk, blk),
        scratch_shapes=[f32_buf, f32_buf, bf_buf] + [f32_buf] * 7 + [pltpu.VMEM((npat - 1, 3, s, LANES), F32)],
        compiler_params=_params(("parallel", "parallel")),
    )(q, k, v, *ordered, o, do, lse, bias)


def _post(x, ya, ybp, ym, proj, target, w_out, g_emb, b_emb, g_a, g_b, g_m, g_post, b_post, tm=256):
    t = x.shape[0]

    def body(x_ref, ya_ref, yb_ref, ym_ref, ga_ref, gb_ref, gm_ref, tg_ref, wo_ref,
             ge_ref, be_ref, goa_ref, gob_ref, gom_ref, gp_ref, bp_ref,
             y_ref, dz_ref, doa_ref, dob_ref, dom_ref, dga_ref, dgb_ref, dgm_ref,
             loss_ref, dgp_ref, dbp_ref, dgoa_ref, dgob_ref, dgom_ref):
        i = pl.program_id(0)

        @pl.when(i == 0)
        def _():
            for r in (loss_ref, dgp_ref, dbp_ref, dgoa_ref, dgob_ref, dgom_ref):
                r[...] = jnp.zeros_like(r)

        lane = lax.broadcasted_iota(jnp.int32, (1, LANES), 1)
        low = lane < 64
        xh0, _ = _ln_hat(x_ref[...])
        h = xh0 * ge_ref[...] + be_ref[...]

        ybp_v = yb_ref[...]
        yb = jnp.concatenate(
            [jnp.where(low, pltpu.roll(ybp_v[:, 2 * j * LANES:(2 * j + 1) * LANES], 64, 1),
                       ybp_v[:, (2 * j + 1) * LANES:(2 * j + 2) * LANES]) for j in range(4)], axis=1)

        def gated(raw, gate, gain, width):
            xh, r = _rms_hat(raw, width)
            n = xh * gain
            sg = 1.0 / (1.0 + jnp.exp(-gate))
            return xh, r, n, sg, n * (gate * sg)

        gate_a, gate_b, gate_m = ga_ref[...], gb_ref[...], gm_ref[...]
        xh_a, r_a, n_a, sg_a, y_a = gated(ya_ref[...], gate_a, goa_ref[...], A_WIDTH)
        xh_b, r_b, n_b, sg_b, y_b = gated(yb, gate_b, gob_ref[...], 512)
        xh_m, r_m, n_m, sg_m, y_m = gated(ym_ref[...], gate_m, gom_ref[...], 512)
        y = jnp.concatenate([y_a, y_b, y_m], axis=1).astype(BF16)
        y_ref[...] = y
        z = DEEPNORM_ALPHA * h + _dot(y, wo_ref[...])
        zh, rstd = _ln_hat(z)
        err = zh * gp_ref[...] + bp_ref[...] - tg_ref[...]
        rows = jnp.sum(err * err, axis=1, keepdims=True)
        loss_ref[...] += jnp.broadcast_to(jnp.sum(rows, axis=0, keepdims=True) * (0.5 / D_MODEL), (1, LANES))
        dout = err * (1.0 / D_MODEL)
        dgp_ref[...] += _colsum(dout * zh)
        dbp_ref[...] += _colsum(dout)
        dz = _ln_bwd_rows(dout * gp_ref[...], zh, rstd)
        dz_ref[...] = dz
        dy = _dot_nt(dz.astype(BF16), wo_ref[...])

        def gated_bwd(dyg, xh, r, n, sg, gate, gain, width, dgain_ref):
            dn = dyg * (gate * sg)
            dgate = dyg * n * (sg * (1.0 + gate * (1.0 - sg)))
            dgain_ref[...] += _colsum(dn * xh)
            return _rms_bwd(dn * gain, xh, r, width), dgate

        dya, dgate_a = gated_bwd(dy[:, 0:1024], xh_a, r_a, n_a, sg_a, gate_a, goa_ref[...], A_WIDTH, dgoa_ref)
        dyb, dgate_b = gated_bwd(dy[:, 1024:1536], xh_b, r_b, n_b, sg_b, gate_b, gob_ref[...], 512, dgob_ref)
        dym, dgate_m = gated_bwd(dy[:, 1536:2048], xh_m, r_m, n_m, sg_m, gate_m, gom_ref[...], 512, dgom_ref)
        doa_ref[...] = dya.astype(BF16)
        dom_ref[...] = dym.astype(BF16)
        dga_ref[...] = dgate_a.astype(BF16)
        dgb_ref[...] = dgate_b.astype(BF16)
        dgm_ref[...] = dgate_m.astype(BF16)
        for j in range(4):
            blk = dyb[:, j * LANES:(j + 1) * LANES]
            dob_ref[:, 2 * j * LANES:(2 * j + 1) * LANES] = jnp.where(low, 0.0, pltpu.roll(blk, 64, 1)).astype(BF16)
            dob_ref[:, (2 * j + 1) * LANES:(2 * j + 2) * LANES] = jnp.where(low, 0.0, blk).astype(BF16)

    def col(width, idx):
        return pl.BlockSpec((tm, width), lambda i: (i, idx))

    def full(shape):
        return pl.BlockSpec(shape, lambda i: (0, 0))

    def acc(width):
        return jax.ShapeDtypeStruct((1, width), F32)

    return pl.pallas_call(
        body, name="post", grid=(t // tm,),
        out_shape=(jax.ShapeDtypeStruct((t, 2048), BF16), jax.ShapeDtypeStruct((t, 1024), F32),
                   jax.ShapeDtypeStruct((t, 1024), BF16), jax.ShapeDtypeStruct((t, 1024), BF16),
                   jax.ShapeDtypeStruct((t, 512), BF16),
                   jax.ShapeDtypeStruct((t, 1024), BF16), jax.ShapeDtypeStruct((t, 512), BF16),
                   jax.ShapeDtypeStruct((t, 512), BF16),
                   acc(LANES), acc(1024), acc(1024), acc(1024), acc(512), acc(512)),
        in_specs=[col(1024, 0), col(1024, 0), col(1024, 0), col(512, 0),
                  col(1024, 3), col(512, COL_BG // 512), col(512, COL_MG // 512), col(1024, 0),
                  full((2048, 1024)),
                  full((1, 1024)), full((1, 1024)), full((1, 1024)), full((1, 512)), full((1, 512)),
                  full((1, 1024)), full((1, 1024))],
        out_specs=(col(2048, 0), col(1024, 0), col(1024, 0), col(1024, 0), col(512, 0),
                   col(1024, 0), col(512, 0), col(512, 0),
                   full((1, LANES)), full((1, 1024)), full((1, 1024)), full((1, 1024)), full((1, 512)),
                   full((1, 512))),
        compiler_params=_params(("arbitrary",)),
    )(x, ya, ybp, ym, proj, proj, proj, target, w_out, g_emb, b_emb, g_a, g_b, g_m, g_post, b_post)


def _prep_bwd(dqa, dka, dva, dqb, dkb, dvb, dqm, dga, dgb, dgm, proj, trig, w_uq, w_ukv, g_cq, g_ckv,
              rope_a, rope_b, tm=256):
    t = proj.shape[0]

    def body(dqa_ref, dka_ref, dva_ref, dqb_ref, dkb_ref, dvb_ref, dqm_ref, dga_ref, dgb_ref, dgm_ref,
             bs_ref, trig_ref, wuq_ref, wukv_ref, gcq_ref, gckv_ref, ra_ref, rb_ref,
             dproj_ref, dqf_ref, dkv_ref, dgcq_ref, dgckv_ref):
        i = pl.program_id(0)

        @pl.when(i == 0)
        def _():
            dgcq_ref[...] = jnp.zeros_like(dgcq_ref)
            dgckv_ref[...] = jnp.zeros_like(dgckv_ref)

        ta = _rope_tables(trig_ref[:, 0:LANES], trig_ref[:, LANES:2 * LANES], ra_ref[...])
        tb = _rope_tables(trig_ref[:, 2 * LANES:3 * LANES], trig_ref[:, 3 * LANES:4 * LANES], rb_ref[...])
        for j in range(A_WIDTH // LANES):
            sl = slice(j * LANES, (j + 1) * LANES)
            dproj_ref[:, j * LANES:(j + 1) * LANES] = (
                _rope(dqa_ref[:, sl].astype(F32), ta, 8, inverse=True).astype(BF16))
            dproj_ref[:, 1024 + j * LANES:1024 + (j + 1) * LANES] = (
                _rope(dka_ref[:, sl].astype(F32), ta, 8, inverse=True).astype(BF16))
        dproj_ref[:, 2048:3072] = dva_ref[...]
        dproj_ref[:, 3072:4096] = dga_ref[...]

        lane = lax.broadcasted_iota(jnp.int32, (1, LANES), 1)
        low = lane < 64
        rope_lanes = (lane >= 64) & (lane < 96)
        dkr = jnp.zeros((tm, LANES), F32)
        for h in range(MLA_HEADS):
            sl = slice(h * LANES, (h + 1) * LANES)
            dqf_ref[:, sl] = _rope(dqb_ref[:, sl].astype(F32), tb, 16, inverse=True).astype(BF16)
            dk_h = dkb_ref[:, sl]
            dkv_ref[:, sl] = jnp.where(low, dk_h, dvb_ref[:, sl])
            dkr = dkr + jnp.where(rope_lanes, dk_h.astype(F32), 0.0)
        dkr = _rope(dkr, tb, 16, inverse=True)

        cq_hat, r_q = _rms_hat(bs_ref[:, 0:MLA_Q_RANK], MLA_Q_RANK)
        dcqn = _dot(dqf_ref[...], wuq_ref[...])
        dgcq_ref[...] += _colsum(dcqn * cq_hat)
        dproj_ref[:, COL_CQ:COL_CQ + 256] = _rms_bwd(dcqn * gcq_ref[...], cq_hat, r_q, MLA_Q_RANK).astype(BF16)
        ckv_hat, r_kv = _rms_hat(bs_ref[:, MLA_Q_RANK:MLA_Q_RANK + MLA_KV_RANK], MLA_KV_RANK)
        dckvn = _dot_nt(dkv_ref[...], wukv_ref[...])
        dgckv_ref[...] += _colsum(dckvn * ckv_hat)
        dproj_ref[:, COL_CQ + 256:COL_CQ + 384] = (
            _rms_bwd(dckvn * gckv_ref[...], ckv_hat, r_kv, MLA_KV_RANK).astype(BF16))
        dproj_ref[:, COL_CQ + 384:COL_CQ + 512] = dkr.astype(BF16)
        dproj_ref[:, COL_BG:COL_BG + 512] = dgb_ref[...]
        dproj_ref[:, COL_MQ:COL_MQ + 512] = dqm_ref[...]
        dproj_ref[:, COL_MG:COL_MG + 512] = dgm_ref[...]

    def col(width, idx):
        return pl.BlockSpec((tm, width), lambda i: (i, idx))

    def full(shape):
        return pl.BlockSpec(shape, lambda i: (0, 0))

    return pl.pallas_call(
        body, name="prep_bwd", grid=(t // tm,),
        out_shape=(jax.ShapeDtypeStruct((t, PROJ_W), BF16), jax.ShapeDtypeStruct((t, 1024), BF16),
                   jax.ShapeDtypeStruct((t, 1024), BF16),
                   jax.ShapeDtypeStruct((1, MLA_Q_RANK), F32), jax.ShapeDtypeStruct((1, MLA_KV_RANK), F32)),
        in_specs=[col(1024, 0)] * 6 + [col(512, 0), col(1024, 0), col(512, 0), col(512, 0),
                  col(512, COL_CQ // 512), pl.BlockSpec((tm, 4 * LANES), lambda i: (i, 0)),
                  full((1024, MLA_Q_RANK)), full((MLA_KV_RANK, 1024)),
                  full((1, MLA_Q_RANK)), full((1, MLA_KV_RANK)), full((8, LANES)), full((8, LANES))],
        out_specs=(col(PROJ_W, 0), col(1024, 0), col(1024, 0), full((1, MLA_Q_RANK)), full((1, MLA_KV_RANK))),
        compiler_params=_params(("arbitrary",)),
    )(dqa, dka, dva, dqb, dkb, dvb, dqm, dga, dgb, dgm, proj, trig, w_uq, w_ukv, g_cq, g_ckv, rope_a, rope_b)


def _adamw_math(gv, w, m, v):
    m_new = ADAM_B1 * m + (1.0 - ADAM_B1) * gv
    v_new = ADAM_B2 * v + (1.0 - ADAM_B2) * (gv * gv)
    m_hat = m_new / (1.0 - ADAM_B1 ** ADAM_STEP)
    v_hat = v_new / (1.0 - ADAM_B2 ** ADAM_STEP)
    return -ADAM_LR * (m_hat / (jnp.sqrt(v_hat) + ADAM_EPS) + ADAM_WD * w), m_new, v_new


def _adamw(g, w, m, v, tr, name):
    r, cols = w.shape

    def body(g_ref, w_ref, m_ref, v_ref, go_ref, d_ref, nm_ref, nv_ref):
        gv = g_ref[...]
        go_ref[...] = gv
        d_ref[...], nm_ref[...], nv_ref[...] = _adamw_math(gv, w_ref[...], m_ref[...], v_ref[...])

    tile = pl.BlockSpec((tr, cols), lambda i: (i, 0))
    shape = jax.ShapeDtypeStruct((r, cols), F32)
    return pl.pallas_call(
        body, name=name, grid=(r // tr,),
        out_shape=(shape,) * 4, in_specs=[tile] * 4, out_specs=(tile,) * 4,
        compiler_params=_params(("parallel",)),
    )(g, w, m, v)


def _adamw_pieces(g, w, m, v, pieces, name):
    n = len(pieces)
    per_piece = isinstance(w, (list, tuple))
    shapes = [jax.ShapeDtypeStruct((r1 - r0, c1 - c0), F32) for r0, r1, c0, c1 in pieces]
    args = (g, *w, *m, *v) if per_piece else (g, w, m, v)

    def body(g_ref, *refs):
        ins, outs = refs[:len(args) - 1], refs[len(args) - 1:]
        gv = g_ref[...]
        if not per_piece:
            results = (gv,) + _adamw_math(gv, ins[0][...], ins[1][...], ins[2][...])
        for p, (r0, r1, c0, c1) in enumerate(pieces):
            if per_piece:
                gp = gv[r0:r1, c0:c1]
                vals = (gp,) + _adamw_math(gp, ins[p][...], ins[n + p][...], ins[2 * n + p][...])
            else:
                vals = [full[r0:r1, c0:c1] for full in results]
            for kind, val in enumerate(vals):
                outs[kind * n + p][...] = val

    flat = pl.pallas_call(
        body, name=name, out_shape=tuple(shapes) * 4,
        in_specs=[IN_VMEM] * len(args), out_specs=tuple([IN_VMEM] * (4 * n)),
        compiler_params=_params(None),
    )(*args)
    return [[flat[kind * n + p] for kind in range(4)] for p in range(n)]


def _core_sum(g, recv, core, rows, tr, name, ride=None):
    cols = g.shape[2]
    nblk = rows // tr
    n_in = len(ride.args) if ride else 0
    n_out = len(ride.out_shapes) if ride else 0

    def body(c_ref, g_ref, r_ref, *rest):
        sf_ref, sb_ref = rest[n_in], rest[n_in + 1]
        if ride:
            j, i = pl.program_id(0), pl.program_id(1)
            ride.run(j * nblk + i, 4 * nblk, rest[:n_in], rest[n_in + 2:n_in + 2 + n_out],
                     rest[n_in + 2 + n_out:])
        tot = g_ref[...] + r_ref[...]
        sf_ref[...] = tot
        sb_ref[...] = tot.astype(BF16)

    half = pl.BlockSpec((None, tr, cols), lambda j, i, c_ref: (j, i, 0))
    shapes = (jax.ShapeDtypeStruct((4, rows, cols), F32), jax.ShapeDtypeStruct((4, rows, cols), BF16))
    return pl.pallas_call(
        body, name=name,
        grid_spec=pltpu.PrefetchScalarGridSpec(
            num_scalar_prefetch=1, grid=(4, nblk),
            in_specs=[pl.BlockSpec((None, tr, cols), lambda j, i, c_ref: (j, c_ref[0] * nblk + i, 0)), half]
            + (ride.in_specs if ride else []),
            out_specs=(half, half) + (ANY,) * n_out,
            scratch_shapes=ride.scratch() if ride else []),
        out_shape=shapes + tuple(ride.out_shapes if ride else ()),
        compiler_params=_params(("arbitrary", "arbitrary") if ride else ("parallel", "parallel")),
    )(core, g, recv, *(ride.args if ride else ()))


def _half_to_sibling(g4):
    def plan(in_refs, out_refs, send_sems, recv_sems):
        x, y, c = _position()
        cp = pltpu.make_async_remote_copy(
            src_ref=in_refs[0].at[:, 1 - c], dst_ref=out_refs[0], send_sem=send_sems.at[0],
            recv_sem=recv_sems.at[0], device_id=(x, y, 1 - c), device_id_type=MESH)

        def finish():
            cp.wait_recv()
            cp.wait_send()

        return cp.start, finish

    return _Ride([g4], [jax.ShapeDtypeStruct((4, g4.shape[2], 1024), F32)], (1, 1), plan)


def _gather_plan(src_ref, dst_ref, send_sems, recv_sems, local_sems):
    x, y, c = _position()
    me = 2 * x + y
    rows = src_ref.shape[1]
    cut = -(-rows // 32) * 16
    pieces = (pl.ds(0, cut), pl.ds(cut, rows - cut))
    local = pltpu.make_async_copy(src_ref, dst_ref.at[me], local_sems.at[0])

    def over_ici(sem, k, chip, t, src=None):
        where = dst_ref.at[chip, c, pieces[t]]
        return pltpu.make_async_remote_copy(
            src_ref=where if src is None else src, dst_ref=where, send_sem=send_sems.at[sem],
            recv_sem=recv_sems.at[sem], device_id=(x ^ (k >> 1), y ^ (k & 1), c), device_id_type=MESH)

    def mine_to(k, t):
        return over_ici(2 * (k - 1) + t, k, me, t, src=src_ref.at[c, pieces[t]])

    def from_neighbour(k, t):
        return over_ici(2 * (k - 1) + t, k, me ^ k, t)

    def to_sibling(k, half):
        piece = dst_ref.at[me ^ k, half]
        return pltpu.make_async_remote_copy(
            src_ref=piece, dst_ref=piece, send_sem=send_sems.at[5 + k], recv_sem=recv_sems.at[5 + k],
            device_id=(x, y, 1 - c), device_id_type=MESH)

    sends = [mine_to(2, 0), mine_to(1, 1), mine_to(2, 1), mine_to(1, 0)]
    onward = [over_ici(4, 1, me ^ 2, 0), over_ici(5, 2, me ^ 1, 1)]

    def start():
        local.start()
        for cp in sends:
            cp.start()

    def pass_on():
        from_neighbour(2, 0).wait_recv()
        onward[0].start()
        from_neighbour(1, 1).wait_recv()
        onward[1].start()

    def to_other_core():
        from_neighbour(2, 1).wait_recv()
        to_sibling(2, c).start()
        from_neighbour(1, 0).wait_recv()
        to_sibling(1, c).start()
        over_ici(4, 1, me ^ 3, 0).wait_recv()
        over_ici(5, 2, me ^ 3, 1).wait_recv()
        to_sibling(3, c).start()

    def finish():
        for k in (1, 2, 3):
            to_sibling(k, 1 - c).wait_recv()
        for cp in sends + onward + [to_sibling(k, c) for k in (1, 2, 3)]:
            cp.wait_send()
        local.wait()

    return start, pass_on, to_other_core, finish


def _gather_ride(shard, spread):
    def plan(in_refs, out_refs, send_sems, recv_sems, local_sems):
        return _gather_plan(in_refs[0], out_refs[0], send_sems, recv_sems, local_sems)

    return _Ride([shard], [jax.ShapeDtypeStruct((4,) + shard.shape, shard.dtype)], (9, 9, 1), plan,
                 in_specs=[IN_VMEM], spread=spread)


def _chip_sum(sf, recv, chip, rows, tr, name):
    cols = sf.shape[2]

    def body(me_ref, sf_ref, r_ref, out_ref):
        acc = sf_ref[...]
        for k in range(3):
            acc = acc + r_ref[k].astype(F32)
        out_ref[...] = acc

    return pl.pallas_call(
        body, name=name,
        grid_spec=pltpu.PrefetchScalarGridSpec(
            num_scalar_prefetch=1, grid=(rows // tr,),
            in_specs=[pl.BlockSpec((None, tr, cols), lambda i, me_ref: (me_ref[0], i, 0)),
                      pl.BlockSpec((3, tr, cols), lambda i, me_ref: (0, i, 0))],
            out_specs=pl.BlockSpec((tr, cols), lambda i, me_ref: (i, 0))),
        out_shape=jax.ShapeDtypeStruct((rows, cols), F32),
        compiler_params=_params(("parallel",)),
    )(chip, sf, recv)


def _position():
    return lax.axis_index("x"), lax.axis_index("y"), lax.axis_index("c")


def _dh_scatter(dproj, w_in_arr_t, x, dz, g, sb_in, sb_rest, tm=1024, tk=1024):
    t, d = x.shape
    nk = dproj.shape[1] // tk
    ni = t // tm

    def body(dp_ref, w_ref, x_ref, dz_ref, g_ref, sbin_ref, sbrest_ref,
             dx_ref, dg_ref, db_ref, rin_ref, rrest_ref, acc_ref, send_sems, recv_sems):
        i = pl.program_id(0)
        kk = pl.program_id(1)
        px, py, pc = _position()
        me = 2 * px + py
        srcs = (sbin_ref, sbrest_ref)
        dsts = (rin_ref, rrest_ref)

        def copy(a, k):
            return pltpu.make_async_remote_copy(
                src_ref=srcs[a].at[me ^ k], dst_ref=dsts[a].at[k - 1],
                send_sem=send_sems.at[3 * a + k - 1], recv_sem=recv_sems.at[3 * a + k - 1],
                device_id=(px ^ (k >> 1), py ^ (k & 1), pc), device_id_type=MESH)

        pairs = [(a, k) for a in range(2) for k in (1, 2, 3)]

        @pl.when((i == 0) & (kk == 0))
        def _():
            dg_ref[...] = jnp.zeros_like(dg_ref)
            db_ref[...] = jnp.zeros_like(db_ref)
            for a, k in pairs:
                copy(a, k).start()

        part = _dot(dp_ref[...], w_ref[...])

        @pl.when(kk == 0)
        def _():
            acc_ref[...] = part

        @pl.when(kk > 0)
        def _():
            acc_ref[...] += part

        @pl.when(kk == nk - 1)
        def _():
            xh, rstd = _ln_hat(x_ref[...])
            dht = acc_ref[...] + DEEPNORM_ALPHA * dz_ref[...]
            dg_ref[...] += _colsum(dht * xh)
            db_ref[...] += _colsum(dht)
            dx_ref[...] = _ln_bwd_rows(dht * g_ref[...], xh, rstd)

        @pl.when((i == ni - 1) & (kk == nk - 1))
        def _():
            for a, k in pairs:
                copy(a, k).wait_recv()
            for a, k in pairs:
                copy(a, k).wait_send()

    tile = pl.BlockSpec((tm, d), lambda i, kk: (i, 0))
    row = pl.BlockSpec((1, d), lambda i, kk: (0, 0))
    return pl.pallas_call(
        body, name="dh_scatter", grid=(ni, nk),
        out_shape=(jax.ShapeDtypeStruct((t, d), F32), jax.ShapeDtypeStruct((1, d), F32),
                   jax.ShapeDtypeStruct((1, d), F32),
                   jax.ShapeDtypeStruct((3, HALF_IN, 1024), BF16),
                   jax.ShapeDtypeStruct((3, HALF_REST, 1024), BF16)),
        in_specs=[pl.BlockSpec((tm, tk), lambda i, kk: (i, kk)), pl.BlockSpec((tk, d), lambda i, kk: (kk, 0)),
                  tile, tile, row, ANY, ANY],
        out_specs=(tile, row, row, ANY, ANY),
        scratch_shapes=[pltpu.VMEM((tm, d), F32), pltpu.SemaphoreType.DMA((6,)), pltpu.SemaphoreType.DMA((6,))],
        compiler_params=_params(("arbitrary", "arbitrary")),
    )(dproj, w_in_arr_t, x, dz, g, sb_in, sb_rest)


def _join_halves(gh_in, gh_rest):
    def body(hin_ref, hrest_ref, oin_ref, orest_ref, send_sems, recv_sems, local_sems):
        x, y, c = _position()
        srcs = (hin_ref, hrest_ref)
        dsts = (oin_ref, orest_ref)

        def rows(a, half):
            return dsts[a].at[half]

        local = [pltpu.make_async_copy(srcs[a], rows(a, c), local_sems.at[a]) for a in range(2)]
        remote = [pltpu.make_async_remote_copy(
            src_ref=srcs[a], dst_ref=rows(a, c), send_sem=send_sems.at[a], recv_sem=recv_sems.at[a],
            device_id=(x, y, 1 - c), device_id_type=MESH) for a in range(2)]
        for cp in local + remote:
            cp.start()
        for a in range(2):
            pltpu.make_async_remote_copy(
                src_ref=srcs[a], dst_ref=rows(a, 1 - c), send_sem=send_sems.at[a], recv_sem=recv_sems.at[a],
                device_id=(x, y, 1 - c), device_id_type=MESH).wait_recv()
        for cp in remote:
            cp.wait_send()
        for cp in local:
            cp.wait()

    return pl.pallas_call(
        body, name="join_halves",
        out_shape=(jax.ShapeDtypeStruct((2, HALF_IN, 1024), F32),
                   jax.ShapeDtypeStruct((2, HALF_REST, 1024), F32)),
        in_specs=[IN_VMEM, IN_VMEM], out_specs=(ANY, ANY),
        scratch_shapes=[pltpu.SemaphoreType.DMA((2,)), pltpu.SemaphoreType.DMA((2,)), pltpu.SemaphoreType.DMA((2,))],
    )(gh_in, gh_rest)


def _allreduce_small(vec):
    def body(vec_ref, out_ref, all_ref, send_sems, recv_sems):
        x, y, c = _position()
        me = 4 * x + 2 * y + c
        all_ref[me] = vec_ref[...]

        def copy(k, slot):
            return pltpu.make_async_remote_copy(
                src_ref=vec_ref, dst_ref=all_ref.at[slot], send_sem=send_sems.at[k - 1], recv_sem=recv_sems.at[k - 1],
                device_id=(x ^ (k >> 2), y ^ ((k >> 1) & 1), c ^ (k & 1)), device_id_type=MESH)

        copies = [copy(k, me) for k in range(1, 8)]
        for cp in copies:
            cp.start()
        for k in range(1, 8):
            copy(k, me ^ k).wait_recv()
        for cp in copies:
            cp.wait_send()
        total = all_ref[0]
        for d in range(1, 8):
            total = total + all_ref[d]
        out_ref[...] = total

    return pl.pallas_call(
        body, name="allreduce_small",
        out_shape=jax.ShapeDtypeStruct(vec.shape, vec.dtype),
        in_specs=[pl.BlockSpec(memory_space=pltpu.VMEM)], out_specs=pl.BlockSpec(memory_space=pltpu.VMEM),
        scratch_shapes=[pltpu.VMEM((8,) + vec.shape, vec.dtype), pltpu.SemaphoreType.DMA((7,)),
                        pltpu.SemaphoreType.DMA((7,))],
    )(vec)


def _pack_rest(w_uq, w_ukv, w_mem, w_out):
    rows = jnp.concatenate([w_uq[0].T.reshape(-1, 1024), w_ukv.reshape(-1, 1024), w_mem.reshape(-1, 1024),
                            w_out.reshape(-1, 1024)], axis=0)
    return jnp.pad(rows, ((0, ROWS_REST - ROWS_USED), (0, 0)))


def _arranged_w_in(g_in):
    z = functools.partial(jnp.zeros, dtype=g_in.dtype)
    cut = 4480 - 2 * SHARD_ROWS
    return jnp.concatenate(
        [g_in[0, :SHARD_ROWS], g_in[1, :SHARD_ROWS], g_in[2, :cut], z((64, 1024)), g_in[2, cut:cut + 32],
         z((32, 1024)), g_in[2, cut + 32:SHARD_ROWS], g_in[3, :SHARD_ROWS]], axis=0)


def _rest_weights(g_rest):
    w_uq_t = g_rest[:, 0:ROWS_UQ].reshape(768, 256)
    w_uq_pad_t = jnp.pad(w_uq_t.reshape(MLA_HEADS, MLA_QK_DIM, 256), ((0, 0), (0, 32), (0, 0))).reshape(1024, 256)
    w_ukv = jnp.concatenate([g_rest[j, ROWS_UQ:ROWS_UQ + ROWS_UKV].reshape(128, 256) for j in range(4)], axis=1)
    lo = ROWS_UQ + ROWS_UKV
    w_mem = g_rest[:, lo:lo + ROWS_MEM].reshape(4 * ROWS_MEM, 1024)
    w_out = g_rest[:, lo + ROWS_MEM:lo + ROWS_MEM + ROWS_OUT].reshape(4 * ROWS_OUT, 1024)
    return w_uq_pad_t, w_ukv, w_mem, w_out


def _split_in(dw_in_arr_t):
    a = dw_in_arr_t
    gap = jnp.zeros((ROWS_IN - SHARD_ROWS, 1024), a.dtype)
    nat = 4608 - 96
    pieces = [a[:SHARD_ROWS], gap, a[SHARD_ROWS:2 * SHARD_ROWS], gap,
              a[2 * SHARD_ROWS:4480], a[4544:4576], a[4608:4608 + 3 * SHARD_ROWS - nat], gap,
              a[4608 + 3 * SHARD_ROWS - nat:], gap]
    return jnp.concatenate(pieces, axis=0).reshape(4, ROWS_IN, 1024)


def _split_rest(dw_uq_pad_t, dw_ukv, dw_mem, dw_out):
    dw_uq_t = dw_uq_pad_t.reshape(MLA_HEADS, LANES, 256)[:, :MLA_QK_DIM].reshape(4, ROWS_UQ, 1024)
    parts = [dw_uq_t, dw_ukv.reshape(128, 4, 256).transpose(1, 0, 2).reshape(4, ROWS_UKV, 1024),
             dw_mem.reshape(4, ROWS_MEM, 1024), dw_out.reshape(4, ROWS_OUT, 1024)]
    return jnp.pad(jnp.concatenate(parts, axis=1), ((0, 0), (0, ROWS_REST - ROWS_USED), (0, 0)))


def _rope_consts(rot, first, period):
    half = rot // 2
    inv_freq = np.float32(ROPE_THETA) ** (-(np.arange(0, rot, 2, dtype=np.float32) / np.float32(rot)))
    lane = np.arange(LANES) % period - first
    in_rot = (lane >= 0) & (lane < rot)
    out = np.zeros((8, LANES), np.float32)
    out[0] = np.where(in_rot, inv_freq[np.clip(lane, 0, rot - 1) % half], 0.0)
    out[1] = in_rot & (lane < half)
    out[2] = in_rot & (lane >= half)
    return jnp.asarray(out)


def _band_bias(s):
    nblk = s // BAND_Q
    starts = np.array([_band_start(i, s) for i in range(nblk)])
    uq = (np.arange(nblk)[:, None] * BAND_Q + np.arange(BAND_Q)[None, :])[:, :, None]
    uk = (starts[:, None] + np.arange(BAND_WIN)[None, :])[:, None, :]
    tiles, index, seen = [], [], {}
    for _, d in DILATED:
        length = s // d
        ok = (uq // length == uk // length) & (np.abs(uq - uk) <= 64)
        row = []
        for i in range(nblk):
            key = ok[i].tobytes()
            if key not in seen:
                seen[key] = len(tiles)
                tiles.append(np.where(ok[i], 0.0, NEG_INF).astype(np.float32))
            row.append(seen[key])
        index.append(row)
    return jnp.asarray(np.stack(tiles, axis=0)), index


def _forward_backward(h, proj, trig, rope_consts, x, mem, target, weights, gains):
    w_uq_pad_t, w_ukv, w_mem, w_out = weights
    g_emb, b_emb, g_cq, g_ckv, g_out_a, g_out_b, g_out_m, g_post, b_post = gains
    nb, s, d = x.shape
    t = nb * s
    x2 = x.reshape(t, d)
    mem2 = mem.reshape(nb * N_MEM, d)
    tgt2 = target.reshape(t, d)
    rope_a, rope_b = rope_consts
    bias, bias_index = _band_bias(s)
    scales = (0.125, MLA_QK_DIM ** -0.5, 128 ** -0.5)

    qa, ka, va, qb, kb, vb, qm, cqn, ckvn = _prep(proj, trig, w_uq_pad_t, w_ukv, g_cq, g_ckv, rope_a, rope_b, scales)
    mkv = _mm(mem2, w_mem, BF16, nb * N_MEM, 1024, 1024, "mem_kv")

    cfg_b = dict(nb=nb, s=s, sk=s, heads=8, voff=0, bq=256)
    cfg_m = dict(nb=nb, s=s, sk=N_MEM, heads=4, hpb=2, voff=4, bq=1024)
    ya, lse_a, qkv_ordered = _dilated_fwd(qa, ka, va, bias, bias_index, nb=nb, s=s, name="attn_a_fwd")
    yb, lse_b = _attn_fwd(qb, kb, vb, name="attn_b_fwd", hpb=4, **cfg_b)
    ym, lse_m = _attn_fwd(qm, mkv, mkv, name="attn_m_fwd", **cfg_m)

    (y, dz, doa, dob, dom, dga, dgb, dgm, loss, dg_post, db_post, dg_a, dg_b, dg_m) = _post(
        x2, ya, yb, ym, proj, tgt2, w_out, g_emb, b_emb, g_out_a, g_out_b, g_out_m, g_post, b_post)

    dqa, dka, dva = _dilated_bwd(qa, ka, va, qkv_ordered, ya, doa, lse_a, bias, bias_index, nb=nb, s=s, scale=scales[0],
                                 name="attn_a_bwd")
    dqb, dkb, dvb = _attn_bwd(qb, kb, vb, yb, dob, lse_b, name="attn_b_bwd", scale=scales[1], hpb=4, **cfg_b)
    dqm, dmk, dmv = _attn_bwd(qm, mkv, mkv, ym, dom, lse_m, name="attn_m_bwd", scale=scales[2], **cfg_m)
    dmkv = jnp.concatenate([dmk, dmv], axis=1)

    dproj, dqf, dkv, dg_cq, dg_ckv = _prep_bwd(
        dqa, dka, dva, dqb, dkb, dvb, dqm, dga, dgb, dgm, proj, trig, w_uq_pad_t, w_ukv, g_cq, g_ckv, rope_a, rope_b)

    small_rows = (dg_cq, dg_ckv, loss, dg_a, dg_b, dg_m, dg_post, db_post)
    return (dproj, h, y, dz, dqf, cqn, ckvn, dkv, mem2, dmkv), x2, small_rows


def _weight_grads(operands, core):
    dproj, h, y, dz, dqf, cqn, ckvn, dkv, mem2, dmkv = operands
    dw_in_arr_t = _mm(dproj, h, F32, 1024, 1024, 4096, "dw_in", mode="tn")
    g_in = _split_in(dw_in_arr_t)
    dw_out, r_in = _mm(y, dz, F32, 1024, 1024, 2048, "dw_out", mode="tn",
                       ride=_half_to_sibling(g_in.reshape(4, 2, HALF_IN, 1024)))
    dw_uq_pad_t = _mm(dqf, cqn, F32, 1024, 256, 4096, "dw_uq", mode="tn")
    dw_ukv = _mm(ckvn, dkv, F32, 128, 1024, 4096, "dw_ukv", mode="tn")
    dw_mem = _mm(mem2, dmkv, F32, 1024, 1024, mem2.shape[0], "dw_mem", mode="tn")
    g_rest = _split_rest(dw_uq_pad_t, dw_ukv, dw_mem, dw_out)
    sf_in, sb_in, r_rest = _core_sum(g_in, r_in, core, HALF_IN, HALF_IN // 2, "core_sum_in",
                                     ride=_half_to_sibling(g_rest.reshape(4, 2, HALF_REST, 1024)))
    sf_rest, sb_rest = _core_sum(g_rest, r_rest, core, HALF_REST, HALF_REST, "core_sum_rest")
    return sf_in, sb_in, sf_rest, sb_rest


def _small_block(dg_emb, db_emb, small_rows):
    dg_cq, dg_ckv, loss, dg_a, dg_b, dg_m, dg_post, db_post = small_rows
    row2 = jnp.concatenate([dg_cq, dg_ckv, loss, jnp.zeros((1, 512), F32)], axis=1)
    return jnp.concatenate([dg_emb, db_emb, row2, dg_a, jnp.concatenate([dg_b, dg_m], axis=1), dg_post, db_post,
                            jnp.zeros((1, 1024), F32)], axis=0)


def _pack_small(g_emb, b_emb, g_cq, g_ckv, g_out_a, g_out_b, g_out_m, g_post, b_post):
    row2 = jnp.concatenate([g_cq.reshape(1, -1), g_ckv.reshape(1, -1), jnp.zeros((1, 640), F32)], axis=1)
    return jnp.concatenate([g_emb.reshape(1, -1), b_emb.reshape(1, -1), row2, g_out_a.reshape(1, -1),
                            jnp.concatenate([g_out_b.reshape(1, -1), g_out_m.reshape(1, -1)], axis=1),
                            g_post.reshape(1, -1), b_post.reshape(1, -1), jnp.zeros((1, 1024), F32)], axis=0)


def kernel(x, mem, positions, g_emb, b_emb, w_in, g_cq, g_ckv, w_uq, w_ukv, w_mem_kv, g_out_a, g_out_b, g_out_m, w_out, g_post, b_post, loss_target, m_g_emb, m_b_emb, m_w_in, m_g_cq, m_g_ckv, m_w_uq, m_w_ukv, m_w_mem_kv, m_g_out_a, m_g_out_b, m_g_out_m, m_w_out, m_g_post, m_b_post, v_g_emb, v_b_emb, v_w_in, v_g_cq, v_g_ckv, v_w_uq, v_w_ukv, v_w_mem_kv, v_g_out_a, v_g_out_b, v_g_out_m, v_w_out, v_g_post, v_b_post):
    w_rest = _pack_rest(w_uq, w_ukv, w_mem_kv, w_out)
    w_in_t = w_in[0].T
    w_in_b = jnp.pad(w_in_t.astype(BF16), ((0, ROWS_IN - SHARD_ROWS), (0, 0)))
    gains = (g_emb.reshape(1, -1), b_emb.reshape(1, -1), g_cq, g_ckv, g_out_a, g_out_b, g_out_m, g_post, b_post)
    rope_consts = (_rope_consts(16, 0, 64), _rope_consts(32, 64, 128))
    h, trig, gathered_in = _ln_fwd(x.reshape(-1, D_MODEL), gains[0], gains[1],
                                   positions.reshape(-1, 1).astype(F32), *rope_consts,
                                   ride=_gather_ride(w_in_b.reshape(2, HALF_IN, 1024), spread=False))
    w_in_arr_t = _arranged_w_in(gathered_in.reshape(4, ROWS_IN, 1024))
    proj, gathered_rest = _mm(h, w_in_arr_t, F32, 1024, 2048, 1024, "in_proj", mode="nt",
                              ride=_gather_ride(w_rest.astype(BF16).reshape(2, HALF_REST, 1024), spread=True))
    weights = _rest_weights(gathered_rest.reshape(4, ROWS_REST, 1024))
    operands, x2, small_rows = _forward_backward(h, proj, trig, rope_consts, x, mem, loss_target, weights, gains)

    core = lax.axis_index("c").astype(jnp.int32).reshape(1)
    chip = (2 * lax.axis_index("x") + lax.axis_index("y")).astype(jnp.int32).reshape(1)
    sf_in, sb_in, sf_rest, sb_rest = _weight_grads(operands, core)
    grad_x, dg_emb, db_emb, rb_in, rb_rest = _dh_scatter(operands[0], w_in_arr_t, x2, operands[3], gains[0],
                                                         sb_in, sb_rest)
    gh_in = _chip_sum(sf_in, rb_in, chip, HALF_IN, HALF_IN // 2, "chip_sum_in")
    gh_rest = _chip_sum(sf_rest, rb_rest, chip, HALF_REST, HALF_REST, "chip_sum_rest")
    grad_in, grad_rest = _join_halves(gh_in, gh_rest)
    grad_in = grad_in.reshape(ROWS_IN, 1024)
    grad_rest = grad_rest.reshape(ROWS_REST, 1024)

    big_in = _adamw(grad_in, w_in_t, m_w_in[0].T, v_w_in[0].T, SHARD_ROWS // 3, "adamw_in")
    def rest_parts(a_uq, a_ukv, a_mem, a_out):
        return [a_uq[0].T.reshape(ROWS_UQ, 1024), a_ukv.reshape(ROWS_UKV, 1024), a_mem[0], a_out[0]]

    uq, ukv, wmem, wout = _adamw_pieces(
        grad_rest, rest_parts(w_uq, w_ukv, w_mem_kv, w_out), rest_parts(m_w_uq, m_w_ukv, m_w_mem_kv, m_w_out),
        rest_parts(v_w_uq, v_w_ukv, v_w_mem_kv, v_w_out), REST_PIECES, "adamw_rest")
    small_sum = _allreduce_small(_small_block(dg_emb, db_emb, small_rows))
    sm = _adamw_pieces(
        small_sum,
        _pack_small(g_emb, b_emb, g_cq, g_ckv, g_out_a, g_out_b, g_out_m, g_post, b_post),
        _pack_small(m_g_emb, m_b_emb, m_g_cq, m_g_ckv, m_g_out_a, m_g_out_b, m_g_out_m, m_g_post, m_b_post),
        _pack_small(v_g_emb, v_b_emb, v_g_cq, v_g_ckv, v_g_out_a, v_g_out_b, v_g_out_m, v_g_post, v_b_post),
        SMALL_PIECES, "adamw_small")
    loss = small_sum[2, 384]

    def ordered(kind):
        s_gemb, s_bemb, s_gcq, s_gckv, s_ga, s_gb, s_gm, s_gpost, s_bpost = [piece[kind] for piece in sm]
        return [s_gemb.reshape(-1), s_bemb.reshape(-1), big_in[kind].T[None], s_gcq, s_gckv,
                uq[kind].reshape(192, 256).T[None], ukv[kind].reshape(1, 128, 256), wmem[kind][None], s_ga, s_gb,
                s_gm, wout[kind][None], s_gpost, s_bpost]

    return (loss, grad_x.reshape(x.shape), *ordered(0), *ordered(1), *ordered(2), *ordered(3))
```

```python
import functools
import math

import jax
import jax.numpy as jnp
import numpy as np
from jax import lax
from jax.experimental import pallas as pl
from jax.experimental.pallas import tpu as pltpu

F32 = jnp.float32
BF16 = jnp.bfloat16
MESH = pl.DeviceIdType.MESH
ANY = pl.BlockSpec(memory_space=pl.ANY)
IN_VMEM = pl.BlockSpec(memory_space=pltpu.VMEM)

D_MODEL = 1024
A_WIDTH = 1024
MLA_HEADS = 8
MLA_Q_RANK = 256
MLA_KV_RANK = 128
MLA_QK_DIM = 96
MEM_WIDTH = 512
N_MEM = 256
ROPE_THETA = 500000.0
NORM_EPS = 1e-5
NEG_INF = -1e30
DEEPNORM_ALPHA = 2.0 ** 0.25
DILATED = ((64, 1), (256, 4), (1024, 16))

ADAM_LR = 0.001
ADAM_B1 = 0.9
ADAM_B2 = 0.999
ADAM_EPS = 1e-08
ADAM_WD = 0.01
ADAM_STEP = 10

LANES = 128
VMEM_LIMIT = 56 * 1024 * 1024
LOG2E = math.log2(math.e)
LN2 = math.log(2.0)

PROJ_W = 6144
COL_CQ = 4096
COL_BG = 4608
COL_MQ = 5120
COL_MG = 5632

SHARD_ROWS = 1512
ROWS_IN = 1536
ROWS_UQ, ROWS_UKV, ROWS_MEM, ROWS_OUT = 48, 32, 256, 512
ROWS_USED = ROWS_UQ + ROWS_UKV + ROWS_MEM + ROWS_OUT
ROWS_REST = 864
HALF_IN = ROWS_IN // 2
HALF_REST = ROWS_REST // 2
REST_PIECES = ((0, 48, 0, 1024), (48, 80, 0, 1024), (80, 336, 0, 1024), (336, 848, 0, 1024))
SMALL_PIECES = ((0, 1, 0, 1024), (1, 2, 0, 1024), (2, 3, 0, 256), (2, 3, 256, 384), (3, 4, 0, 1024), (4, 5, 0, 512),
                (4, 5, 512, 1024), (5, 6, 0, 1024), (6, 7, 0, 1024))


def _params(sem=None, vmem=VMEM_LIMIT):
    return pltpu.CompilerParams(dimension_semantics=sem, vmem_limit_bytes=vmem)


def _dot(a, b):
    return jnp.dot(a, b, preferred_element_type=F32)


def _dot_nt(a, b):
    return lax.dot_general(a, b, (((1,), (1,)), ((), ())), preferred_element_type=F32)


def _dot_tn(a, b):
    return lax.dot_general(a, b, (((0,), (0,)), ((), ())), preferred_element_type=F32)


def _ln_hat(x):
    mu = jnp.mean(x, axis=-1, keepdims=True)
    xc = x - mu
    var = jnp.mean(xc * xc, axis=-1, keepdims=True)
    rstd = lax.rsqrt(var + NORM_EPS)
    return xc * rstd, rstd


def _ln_bwd_rows(dxh, xh, rstd):
    return rstd * (dxh - jnp.mean(dxh, axis=-1, keepdims=True) - xh * jnp.mean(dxh * xh, axis=-1, keepdims=True))


def _rms_hat(x, width):
    ms = jnp.sum(x * x, axis=-1, keepdims=True) * (1.0 / width)
    r = lax.rsqrt(ms + NORM_EPS)
    return x * r, r


def _rms_bwd(u, xh, r, width):
    return r * (u - xh * (jnp.sum(u * xh, axis=-1, keepdims=True) * (1.0 / width)))


def _colsum(v):
    return jnp.sum(v, axis=0, keepdims=True)


def _rope_tables(cos, sin, consts):
    return cos, sin * consts[2:3, :], -sin * consts[1:2, :]


def _rope(x, tables, half, inverse=False):
    c, s_up, s_dn = tables
    if inverse:
        s_up, s_dn = -s_up, -s_dn
    return x * c + pltpu.roll(x, half, 1) * s_up + pltpu.roll(x, LANES - half, 1) * s_dn


def _ln_fwd(x, g, b, pos, rope_a, rope_b, tm=512, ride=None):
    t, d = x.shape
    n_in = len(ride.args) if ride else 0
    n_out = len(ride.out_shapes) if ride else 0
    steps = t // tm

    def body(x_ref, g_ref, b_ref, pos_ref, ra_ref, rb_ref, *rest):
        h_ref, h32_ref, trig_ref = rest[n_in:n_in + 3]
        if ride:
            i = pl.program_id(0)
            ride.run(i, steps, rest[:n_in], rest[n_in + 3:n_in + 3 + n_out], rest[n_in + 3 + n_out:])
        xh, _ = _ln_hat(x_ref[...])
        h = xh * g_ref[...] + b_ref[...]
        h32_ref[...] = h
        h_ref[...] = h.astype(BF16)
        for j, consts in enumerate((ra_ref, rb_ref)):
            ang = pos_ref[...] * consts[0:1, :]
            trig_ref[:, 2 * j * LANES:(2 * j + 1) * LANES] = jnp.cos(ang)
            trig_ref[:, (2 * j + 1) * LANES:(2 * j + 2) * LANES] = jnp.sin(ang)

    row = pl.BlockSpec((1, d), lambda i: (0, 0))
    tile = pl.BlockSpec((tm, d), lambda i: (i, 0))
    consts = pl.BlockSpec((8, LANES), lambda i: (0, 0))
    trig_tile = pl.BlockSpec((tm, 4 * LANES), lambda i: (i, 0))
    in_specs = [tile, row, row, pl.BlockSpec((tm, 1), lambda i: (i, 0)), consts, consts]
    shapes = (jax.ShapeDtypeStruct((t, d), BF16), jax.ShapeDtypeStruct((t, d), F32),
              jax.ShapeDtypeStruct((t, 4 * LANES), F32))
    if not ride:
        return pl.pallas_call(
            body, name="ln_fwd", grid=(steps,), out_shape=shapes, in_specs=in_specs,
            out_specs=(tile, tile, trig_tile), compiler_params=_params(("parallel",)),
        )(x, g, b, pos, rope_a, rope_b)
    return pl.pallas_call(
        body, name="ln_fwd", grid=(steps,),
        out_shape=(*shapes, *ride.out_shapes),
        in_specs=in_specs + ride.in_specs, out_specs=(tile, tile, trig_tile) + (ANY,) * n_out,
        scratch_shapes=ride.scratch(),
        compiler_params=_params(("arbitrary",)),
    )(x, g, b, pos, rope_a, rope_b, *ride.args)


class _Ride:
    def __init__(self, args, out_shapes, sem_counts, plan, in_specs=None, spread=True):
        self.args, self.out_shapes, self.plan = list(args), list(out_shapes), plan
        self.sem_counts = sem_counts
        self.in_specs = in_specs or [ANY] * len(self.args)
        self.spread = spread

    def scratch(self):
        return [pltpu.SemaphoreType.DMA((n,)) for n in self.sem_counts]

    def run(self, step, total, in_refs, out_refs, sems):
        count = len(self.plan(in_refs, out_refs, *sems))
        at = [(k * (total - 1)) // (count - 1) if self.spread or k == 0 else total - 1 for k in range(count)]
        for when in sorted(set(at)):
            @pl.when(step == when)
            def _(when=when):
                stages = self.plan(in_refs, out_refs, *sems)
                for k in range(count):
                    if at[k] == when:
                        stages[k]()


def _mm(a, b, out_dtype, tm, tn, tk, name, mode="nn", ride=None):
    if mode == "tn":
        k, m = a.shape
    else:
        m, k = a.shape
    n = b.shape[0] if mode == "nt" else b.shape[1]
    nk = k // tk
    nj, ni = n // tn, m // tm
    n_in = len(ride.args) if ride else 0
    n_out = len(ride.out_shapes) if ride else 0

    def body(a_ref, b_ref, *rest):
        o_ref = rest[n_in]
        acc_ref = rest[n_in + 1 + n_out]
        if ride:
            j, i, kk = pl.program_id(0), pl.program_id(1), pl.program_id(2)
            ride.run((j * ni + i) * nk + kk, nj * ni * nk, rest[:n_in], rest[n_in + 1:n_in + 1 + n_out],
                     rest[n_in + 2 + n_out:])
        av = a_ref[...].astype(BF16)
        bv = b_ref[...].astype(BF16)
        part = _dot_tn(av, bv) if mode == "tn" else _dot_nt(av, bv) if mode == "nt" else _dot(av, bv)
        if nk == 1:
            o_ref[...] = part.astype(out_dtype)
        else:
            kk = pl.program_id(2)

            @pl.when(kk == 0)
            def _():
                acc_ref[...] = part

            @pl.when(kk > 0)
            def _():
                acc_ref[...] += part

            @pl.when(kk == nk - 1)
            def _():
                o_ref[...] = acc_ref[...].astype(out_dtype)

    a_spec = (pl.BlockSpec((tk, tm), lambda j, i, kk: (kk, i)) if mode == "tn"
              else pl.BlockSpec((tm, tk), lambda j, i, kk: (i, kk)))
    b_spec = (pl.BlockSpec((tn, tk), lambda j, i, kk: (j, kk)) if mode == "nt"
              else pl.BlockSpec((tk, tn), lambda j, i, kk: (kk, j)))
    o_spec = pl.BlockSpec((tm, tn), lambda j, i, kk: (i, j))
    o_shape = jax.ShapeDtypeStruct((m, n), out_dtype)
    if not ride:
        return pl.pallas_call(
            body, name=name, grid=(nj, ni, nk), out_shape=o_shape, in_specs=[a_spec, b_spec], out_specs=o_spec,
            scratch_shapes=[pltpu.VMEM((tm, tn), F32)],
            compiler_params=_params(("parallel", "parallel", "arbitrary")),
        )(a, b)
    return pl.pallas_call(
        body, name=name, grid=(nj, ni, nk),
        out_shape=(o_shape, *ride.out_shapes),
        in_specs=[a_spec, b_spec] + ride.in_specs,
        out_specs=(o_spec,) + (ANY,) * n_out,
        scratch_shapes=[pltpu.VMEM((tm, tn), F32)] + ride.scratch(),
        compiler_params=_params(("arbitrary", "arbitrary", "arbitrary")),
    )(a, b, *ride.args)


def _prep(proj, trig, w_uq, w_ukv, g_cq, g_ckv, rope_a, rope_b, scales, tm=256):
    t = proj.shape[0]
    sc_a, sc_b, sc_m = (s * LOG2E for s in scales)

    def body(aq_ref, ak_ref, av_ref, bs_ref, mq_ref, trig_ref, wuq_ref, wukv_ref, gcq_ref, gckv_ref,
             ra_ref, rb_ref, qa_ref, ka_ref, va_ref, qb_ref, kb_ref, vb_ref, qm_ref, cqn_ref, ckvn_ref):
        ta = _rope_tables(trig_ref[:, 0:LANES], trig_ref[:, LANES:2 * LANES], ra_ref[...])
        tb = _rope_tables(trig_ref[:, 2 * LANES:3 * LANES], trig_ref[:, 3 * LANES:4 * LANES], rb_ref[...])
        for j in range(A_WIDTH // LANES):
            sl = slice(j * LANES, (j + 1) * LANES)
            qa_ref[:, sl] = (_rope(aq_ref[:, sl], ta, 8) * sc_a).astype(BF16)
            ka_ref[:, sl] = _rope(ak_ref[:, sl], ta, 8).astype(BF16)
        va_ref[...] = av_ref[...].astype(BF16)
        qm_ref[...] = (mq_ref[...] * sc_m).astype(BF16)

        cq_hat, _ = _rms_hat(bs_ref[:, 0:MLA_Q_RANK], MLA_Q_RANK)
        cqn = (cq_hat * gcq_ref[...]).astype(BF16)
        cqn_ref[...] = cqn
        ckv_hat, _ = _rms_hat(bs_ref[:, MLA_Q_RANK:MLA_Q_RANK + MLA_KV_RANK], MLA_KV_RANK)
        ckvn = (ckv_hat * gckv_ref[...]).astype(BF16)
        ckvn_ref[...] = ckvn
        qfull = _dot_nt(cqn, wuq_ref[...])
        kv = _dot(ckvn, wukv_ref[...])
        kr = _rope(bs_ref[:, 384:512], tb, 16)
        lane = lax.broadcasted_iota(jnp.int32, (1, LANES), 1)
        low = lane < 64
        for h in range(MLA_HEADS):
            sl = slice(h * LANES, (h + 1) * LANES)
            qb_ref[:, sl] = (_rope(qfull[:, sl], tb, 16) * sc_b).astype(BF16)
            kb_ref[:, sl] = jnp.where(low, kv[:, sl], kr).astype(BF16)
            vb_ref[:, sl] = jnp.where(low, 0.0, kv[:, sl]).astype(BF16)

    def col(width, idx):
        return pl.BlockSpec((tm, width), lambda i: (i, idx))

    def full(shape):
        return pl.BlockSpec(shape, lambda i: (0, 0))

    wide = jax.ShapeDtypeStruct((t, 1024), BF16)
    return pl.pallas_call(
        body, name="prep", grid=(t // tm,),
        out_shape=(wide, wide, wide, wide, wide, wide,
                   jax.ShapeDtypeStruct((t, MEM_WIDTH), BF16),
                   jax.ShapeDtypeStruct((t, MLA_Q_RANK), BF16),
                   jax.ShapeDtypeStruct((t, MLA_KV_RANK), BF16)),
        in_specs=[col(1024, 0), col(1024, 1), col(1024, 2), col(512, COL_CQ // 512), col(512, COL_MQ // 512),
                  pl.BlockSpec((tm, 4 * LANES), lambda i: (i, 0)),
                  full((1024, MLA_Q_RANK)), full((MLA_KV_RANK, 1024)),
                  full((1, MLA_Q_RANK)), full((1, MLA_KV_RANK)), full((8, LANES)), full((8, LANES))],
        out_specs=(col(1024, 0),) * 6 + (col(MEM_WIDTH, 0), col(MLA_Q_RANK, 0), col(MLA_KV_RANK, 0)),
        compiler_params=_params(("parallel",)),
    )(proj, proj, proj, proj, proj, trig, w_uq, w_ukv, g_cq, g_ckv, rope_a, rope_b)


def _attn_fwd(q, k, v, *, nb, s, sk, heads, hpb, voff, bq, name):
    nq = s // bq
    width = hpb * LANES
    vblk = voff // hpb

    def body(q_ref, k_ref, v_ref, o_ref, lse_ref):
        for h in range(hpb):
            sl = slice(h * LANES, (h + 1) * LANES)
            sc = _dot_nt(q_ref[:, sl], k_ref[:, sl])
            m = jnp.max(sc, axis=1, keepdims=True)
            p = jnp.exp2(sc - m)
            l = jnp.sum(p, axis=1, keepdims=True)
            o_ref[:, sl] = _dot(p.astype(BF16), v_ref[:, sl]) / l
            lse_ref[:, sl] = jnp.broadcast_to(m + jnp.log(l) * LOG2E, (bq, LANES))

    out = jax.ShapeDtypeStruct((nb * s, heads * LANES), F32)
    ospec = pl.BlockSpec((bq, width), lambda b, i, g: (b * nq + i, g))
    return pl.pallas_call(
        body, name=name, grid=(nb, nq, heads // hpb),
        out_shape=(out, out),
        in_specs=[ospec, pl.BlockSpec((sk, width), lambda b, i, g: (b, g)),
                  pl.BlockSpec((sk, width), lambda b, i, g: (b, vblk + g))],
        out_specs=(ospec, ospec),
        compiler_params=_params(("parallel", "parallel", "parallel")),
    )(q, k, v)


def _attn_bwd(q, k, v, o, do, lse, *, nb, s, sk, heads, hpb, voff, scale, bq, name):
    nq = s // bq
    width = hpb * LANES
    vblk = voff // hpb

    def body(q_ref, k_ref, v_ref, o_ref, do_ref, lse_ref, dq_ref, dk_ref, dv_ref, dk_acc, dv_acc):
        i = pl.program_id(2)

        @pl.when(i == 0)
        def _():
            dk_acc[...] = jnp.zeros_like(dk_acc)
            dv_acc[...] = jnp.zeros_like(dv_acc)

        for h in range(hpb):
            sl = slice(h * LANES, (h + 1) * LANES)
            qh = q_ref[:, sl]
            kk = k_ref[:, sl]
            doh = do_ref[:, sl]
            delta = jnp.sum(doh.astype(F32) * o_ref[:, sl], axis=1, keepdims=True)
            p = jnp.exp2(_dot_nt(qh, kk) - lse_ref[:, h * LANES:h * LANES + 1])
            ds = (p * (_dot_nt(doh, v_ref[:, sl]) - delta)).astype(BF16)
            dq_ref[:, sl] = (_dot(ds, kk) * scale).astype(BF16)
            dk_acc[:, sl] += _dot_tn(ds, qh)
            dv_acc[:, sl] += _dot_tn(p.astype(BF16), doh)

        @pl.when(i == nq - 1)
        def _():
            dk_ref[...] = (dk_acc[...] * LN2).astype(BF16)
            dv_ref[...] = dv_acc[...].astype(BF16)

    qspec = pl.BlockSpec((bq, width), lambda b, g, i: (b * nq + i, g))
    kv_spec = pl.BlockSpec((sk, width), lambda b, g, i: (b, g))
    dq_shape = jax.ShapeDtypeStruct((nb * s, heads * LANES), BF16)
    dkv_shape = jax.ShapeDtypeStruct((nb * sk, heads * LANES), BF16)
    return pl.pallas_call(
        body, name=name, grid=(nb, heads // hpb, nq),
        out_shape=(dq_shape, dkv_shape, dkv_shape),
        in_specs=[qspec, kv_spec, pl.BlockSpec((sk, width), lambda b, g, i: (b, vblk + g)), qspec, qspec, qspec],
        out_specs=(qspec, kv_spec, kv_spec),
        scratch_shapes=[pltpu.VMEM((sk, width), F32), pltpu.VMEM((sk, width), F32)],
        compiler_params=_params(("parallel", "parallel", "arbitrary")),
    )(q, k, v, o, do, lse)


BAND_Q = 128
BAND_WIN = 256


def _band_start(i, s):
    return min(max(i * BAND_Q - 64, 0), s - BAND_WIN)


def _to_pattern_order(src_ref, dst_ref, stage_ref, s, d):
    length = s // d
    stage_ref[...] = src_ref[...].astype(F32)
    for r in range(d):
        dst_ref[r * length:(r + 1) * length, :] = stage_ref[pl.ds(r, length, stride=d), :].astype(dst_ref.dtype)


def _dilated_fwd(q, k, v, bias, bias_index, *, nb, s, name):
    nblk = s // BAND_Q
    npat = len(DILATED)

    def body(q_ref, k_ref, v_ref, bias_ref, o_ref, lse_ref, *rest):
        ordered = rest[:3 * (npat - 1)]
        stage_ref, op_ref, lp_ref, on_ref, ln_ref = rest[3 * (npat - 1):]
        lane = lax.broadcasted_iota(jnp.int32, (1, LANES), 1)
        first = lane < 64
        for p, (_, d) in enumerate(DILATED):
            if d == 1:
                qs, ks, vs = q_ref, k_ref, v_ref
            else:
                qs, ks, vs = ordered[3 * (p - 1):3 * p]
                for src, dst in ((q_ref, qs), (k_ref, ks), (v_ref, vs)):
                    _to_pattern_order(src, dst, stage_ref, s, d)
            for i in range(nblk):
                u0 = i * BAND_Q
                st = _band_start(i, s)
                qi = qs[u0:u0 + BAND_Q, :]
                kw = ks[st:st + BAND_WIN, :]
                vw = vs[st:st + BAND_WIN, :]
                zero = jnp.zeros_like(qi)
                q2 = jnp.concatenate([jnp.where(first, qi, zero), jnp.where(first, zero, qi)], axis=0)
                sc = _dot_nt(q2, kw)
                b = bias_ref[bias_index[p][i]]
                halves = []
                for h in range(2):
                    sh = sc[h * BAND_Q:(h + 1) * BAND_Q] + b
                    m = jnp.max(sh, axis=1, keepdims=True)
                    pr = jnp.exp2(sh - m)
                    l = jnp.sum(pr, axis=1, keepdims=True)
                    halves.append((pr.astype(BF16), l, m + jnp.log(l) * LOG2E))
                o2 = _dot(jnp.concatenate([halves[0][0], halves[1][0]], axis=0), vw)
                o_blk = jnp.where(first, o2[:BAND_Q] / halves[0][1], o2[BAND_Q:] / halves[1][1])
                lse_blk = jnp.where(first, jnp.broadcast_to(halves[0][2], (BAND_Q, LANES)),
                                    jnp.broadcast_to(halves[1][2], (BAND_Q, LANES)))
                op_ref[p, u0:u0 + BAND_Q, :] = o_blk
                lp_ref[p, u0:u0 + BAND_Q, :] = lse_blk
            if d > 1:
                length = s // d
                for r in range(d):
                    on_ref.at[p - 1][pl.ds(r, length, stride=d), :] = op_ref[p, r * length:(r + 1) * length, :]
                    ln_ref.at[p - 1][pl.ds(r, length, stride=d), :] = lp_ref[p, r * length:(r + 1) * length, :]
        lses = [lp_ref[0]] + [ln_ref[p] for p in range(npat - 1)]
        outs = [op_ref[0]] + [on_ref[p] for p in range(npat - 1)]
        m = functools.reduce(jnp.maximum, lses)
        ws = [jnp.exp2(l - m) for l in lses]
        den = functools.reduce(lambda a, c: a + c, ws)
        o_ref[...] = functools.reduce(lambda a, c: a + c, [w * o for w, o in zip(ws, outs)]) / den
        lse_ref[...] = m + jnp.log(den) * LOG2E

    blk = pl.BlockSpec((s, LANES), lambda b, g: (b, g))
    out = jax.ShapeDtypeStruct((nb * s, A_WIDTH), F32)
    copy = jax.ShapeDtypeStruct((nb * s, A_WIDTH), BF16)
    n_copies = 3 * (npat - 1)
    res = pl.pallas_call(
        body, name=name, grid=(nb, A_WIDTH // LANES),
        out_shape=(out, out) + (copy,) * n_copies,
        in_specs=[blk, blk, blk, pl.BlockSpec(bias.shape, lambda b, g: (0, 0, 0))],
        out_specs=(blk, blk) + (blk,) * n_copies,
        scratch_shapes=[pltpu.VMEM((s, LANES), F32), pltpu.VMEM((npat, s, LANES), F32),
                        pltpu.VMEM((npat, s, LANES), F32), pltpu.VMEM((npat - 1, s, LANES), F32),
                        pltpu.VMEM((npat - 1, s, LANES), F32)],
        compiler_params=_params(("parallel", "parallel")),
    )(q, k, v, bias)
    return res[0], res[1], res[2:]


def _dilated_bwd(q, k, v, ordered, o, do, lse, bias, bias_index, *, nb, s, scale, name):
    nblk = s // BAND_Q
    npat = len(DILATED)
    n_copies = 3 * (npat - 1)

    def body(q_ref, k_ref, v_ref, *rest):
        ordered_refs = rest[:n_copies]
        (o_ref, do_ref, lse_ref, bias_ref, dq_out, dk_out, dv_out, stage_ref, rs_ref, dop_ref, rsp_ref,
         dqp_ref, dkp_ref, dvp_ref, dq_ref, dk_ref, dv_ref, nat_ref) = rest[n_copies:]
        lane = lax.broadcasted_iota(jnp.int32, (1, LANES), 1)
        first = lane < 64
        prod = do_ref[...].astype(F32) * o_ref[...]
        d0 = jnp.sum(jnp.where(first, prod, 0.0), axis=1, keepdims=True)
        d1 = jnp.sum(jnp.where(first, 0.0, prod), axis=1, keepdims=True)
        delta = jnp.where(first, jnp.broadcast_to(d0, (s, LANES)), jnp.broadcast_to(d1, (s, LANES)))
        rs_ref[...] = jnp.where((lane & 32) == 0, lse_ref[...], delta)
        for p, (_, d) in enumerate(DILATED):
            length = s // d
            if d == 1:
                qs, ks, vs, dos, rss = q_ref, k_ref, v_ref, do_ref, rs_ref
                dqs, dks, dvs = dq_ref, dk_ref, dv_ref
            else:
                for src, dst in ((do_ref, dop_ref), (rs_ref, rsp_ref)):
                    _to_pattern_order(src, dst, stage_ref, s, d)
                qs, ks, vs = ordered_refs[3 * (p - 1):3 * p]
                dos, rss = dop_ref, rsp_ref
                dqs, dks, dvs = dqp_ref, dkp_ref, dvp_ref
            dks[...] = jnp.zeros((s, LANES), F32)
            dvs[...] = jnp.zeros((s, LANES), F32)
            for i in range(nblk):
                u0 = i * BAND_Q
                st = _band_start(i, s)
                qi = qs[u0:u0 + BAND_Q, :]
                doi = dos[u0:u0 + BAND_Q, :]
                kw = ks[st:st + BAND_WIN, :]
                vw = vs[st:st + BAND_WIN, :]
                zero = jnp.zeros_like(qi)
                q2 = jnp.concatenate([jnp.where(first, qi, zero), jnp.where(first, zero, qi)], axis=0)
                do2 = jnp.concatenate([jnp.where(first, doi, zero), jnp.where(first, zero, doi)], axis=0)
                sc = _dot_nt(q2, kw)
                dp = _dot_nt(do2, vw)
                b = bias_ref[bias_index[p][i]]
                rs_i = rss[u0:u0 + BAND_Q, :]
                ps, dss = [], []
                for h in range(2):
                    rows = slice(h * BAND_Q, (h + 1) * BAND_Q)
                    pr = jnp.exp2(sc[rows] + b - rs_i[:, 64 * h:64 * h + 1])
                    ps.append(pr.astype(BF16))
                    dss.append((pr * (dp[rows] - rs_i[:, 64 * h + 32:64 * h + 33])).astype(BF16))
                p2 = jnp.concatenate(ps, axis=0)
                ds2 = jnp.concatenate(dss, axis=0)
                dq2 = _dot(ds2, kw)
                dqs[u0:u0 + BAND_Q, :] = jnp.where(first, dq2[:BAND_Q], dq2[BAND_Q:]) * scale
                dks[st:st + BAND_WIN, :] += _dot_tn(ds2, q2)
                dvs[st:st + BAND_WIN, :] += _dot_tn(p2, do2)
            if d > 1:
                for j, src in enumerate((dqp_ref, dkp_ref, dvp_ref)):
                    for r in range(d):
                        nat_ref.at[p - 1, j][pl.ds(r, length, stride=d), :] = src[r * length:(r + 1) * length, :]

        def total(j, first_ref):
            return functools.reduce(lambda a, c: a + c, [first_ref[...]] + [nat_ref[p, j] for p in range(npat - 1)])

        dq_out[...] = total(0, dq_ref).astype(BF16)
        dk_out[...] = (total(1, dk_ref) * LN2).astype(BF16)
        dv_out[...] = total(2, dv_ref).astype(BF16)

    blk = pl.BlockSpec((s, LANES), lambda b, g: (b, g))
    out = jax.ShapeDtypeStruct((nb * s, A_WIDTH), BF16)
    f32_buf = pltpu.VMEM((s, LANES), F32)
    bf_buf = pltpu.VMEM((s, LANES), BF16)
    return pl.pallas_call(
        body, name=name, grid=(nb, A_WIDTH // LANES),
        out_shape=(out, out, out),
        in_specs=[blk] * (6 + n_copies) + [pl.BlockSpec(bias.shape, lambda b, g: (0, 0, 0))],
        out_specs=(blk, blk, blk),
        scratch_shapes=[f32_buf, f32_buf, bf_buf] + [f32_buf] * 7 + [pltpu.VMEM((npat - 1, 3, s, LANES), F32)],
        compiler_params=_params(("parallel", "parallel")),
    )(q, k, v, *ordered, o, do, lse, bias)


def _post(h32, ya, ybp, ym, proj, target, w_out, g_a, g_b, g_m, g_post, b_post, tm=256):
    t = h32.shape[0]

    def body(h_ref, ya_ref, yb_ref, ym_ref, ga_ref, gb_ref, gm_ref, tg_ref, wo_ref,
             goa_ref, gob_ref, gom_ref, gp_ref, bp_ref,
             y_ref, dz_ref, doa_ref, dob_ref, dom_ref, dga_ref, dgb_ref, dgm_ref,
             loss_ref, dgp_ref, dbp_ref, dgoa_ref, dgob_ref, dgom_ref):
        i = pl.program_id(0)

        @pl.when(i == 0)
        def _():
            for r in (loss_ref, dgp_ref, dbp_ref, dgoa_ref, dgob_ref, dgom_ref):
                r[...] = jnp.zeros_like(r)

        lane = lax.broadcasted_iota(jnp.int32, (1, LANES), 1)
        low = lane < 64
        h = h_ref[...]

        ybp_v = yb_ref[...]
        yb = jnp.concatenate(
            [jnp.where(low, pltpu.roll(ybp_v[:, 2 * j * LANES:(2 * j + 1) * LANES], 64, 1),
                       ybp_v[:, (2 * j + 1) * LANES:(2 * j + 2) * LANES]) for j in range(4)], axis=1)

        def gated(raw, gate, gain, width):
            xh, r = _rms_hat(raw, width)
            n = xh * gain
            sg = 1.0 / (1.0 + jnp.exp(-gate))
            return xh, r, n, sg, n * (gate * sg)

        gate_a, gate_b, gate_m = ga_ref[...], gb_ref[...], gm_ref[...]
        xh_a, r_a, n_a, sg_a, y_a = gated(ya_ref[...], gate_a, goa_ref[...], A_WIDTH)
        xh_b, r_b, n_b, sg_b, y_b = gated(yb, gate_b, gob_ref[...], 512)
        xh_m, r_m, n_m, sg_m, y_m = gated(ym_ref[...], gate_m, gom_ref[...], 512)
        y = jnp.concatenate([y_a, y_b, y_m], axis=1).astype(BF16)
        y_ref[...] = y
        z = DEEPNORM_ALPHA * h + _dot(y, wo_ref[...])
        zh, rstd = _ln_hat(z)
        err = zh * gp_ref[...] + bp_ref[...] - tg_ref[...]
        rows = jnp.sum(err * err, axis=1, keepdims=True)
        loss_ref[...] += jnp.broadcast_to(jnp.sum(rows, axis=0, keepdims=True) * (0.5 / D_MODEL), (1, LANES))
        dout = err * (1.0 / D_MODEL)
        dgp_ref[...] += _colsum(dout * zh)
        dbp_ref[...] += _colsum(dout)
        dz = _ln_bwd_rows(dout * gp_ref[...], zh, rstd)
        dz_ref[...] = dz
        dy = _dot_nt(dz.astype(BF16), wo_ref[...])

        def gated_bwd(dyg, xh, r, n, sg, gate, gain, width, dgain_ref):
            dn = dyg * (gate * sg)
            dgate = dyg * n * (sg * (1.0 + gate * (1.0 - sg)))
            dgain_ref[...] += _colsum(dn * xh)
            return _rms_bwd(dn * gain, xh, r, width), dgate

        dya, dgate_a = gated_bwd(dy[:, 0:1024], xh_a, r_a, n_a, sg_a, gate_a, goa_ref[...], A_WIDTH, dgoa_ref)
        dyb, dgate_b = gated_bwd(dy[:, 1024:1536], xh_b, r_b, n_b, sg_b, gate_b, gob_ref[...], 512, dgob_ref)
        dym, dgate_m = gated_bwd(dy[:, 1536:2048], xh_m, r_m, n_m, sg_m, gate_m, gom_ref[...], 512, dgom_ref)
        doa_ref[...] = dya.astype(BF16)
        dom_ref[...] = dym.astype(BF16)
        dga_ref[...] = dgate_a.astype(BF16)
        dgb_ref[...] = dgate_b.astype(BF16)
        dgm_ref[...] = dgate_m.astype(BF16)
        for j in range(4):
            blk = dyb[:, j * LANES:(j + 1) * LANES]
            dob_ref[:, 2 * j * LANES:(2 * j + 1) * LANES] = jnp.where(low, 0.0, pltpu.roll(blk, 64, 1)).astype(BF16)
            dob_ref[:, (2 * j + 1) * LANES:(2 * j + 2) * LANES] = jnp.where(low, 0.0, blk).astype(BF16)

    def col(width, idx):
        return pl.BlockSpec((tm, width), lambda i: (i, idx))

    def full(shape):
        return pl.BlockSpec(shape, lambda i: (0, 0))

    def acc(width):
        return jax.ShapeDtypeStruct((1, width), F32)

    return pl.pallas_call(
        body, name="post", grid=(t // tm,),
        out_shape=(jax.ShapeDtypeStruct((t, 2048), BF16), jax.ShapeDtypeStruct((t, 1024), F32),
                   jax.ShapeDtypeStruct((t, 1024), BF16), jax.ShapeDtypeStruct((t, 1024), BF16),
                   jax.ShapeDtypeStruct((t, 512), BF16),
                   jax.ShapeDtypeStruct((t, 1024), BF16), jax.ShapeDtypeStruct((t, 512), BF16),
                   jax.ShapeDtypeStruct((t, 512), BF16),
                   acc(LANES), acc(1024), acc(1024), acc(1024), acc(512), acc(512)),
        in_specs=[col(1024, 0), col(1024, 0), col(1024, 0), col(512, 0),
                  col(1024, 3), col(512, COL_BG // 512), col(512, COL_MG // 512), col(1024, 0),
                  full((2048, 1024)),
                  full((1, 1024)), full((1, 512)), full((1, 512)), full((1, 1024)), full((1, 1024))],
        out_specs=(col(2048, 0), col(1024, 0), col(1024, 0), col(1024, 0), col(512, 0),
                   col(1024, 0), col(512, 0), col(512, 0),
                   full((1, LANES)), full((1, 1024)), full((1, 1024)), full((1, 1024)), full((1, 512)),
                   full((1, 512))),
        compiler_params=_params(("arbitrary",)),
    )(h32, ya, ybp, ym, proj, proj, proj, target, w_out, g_a, g_b, g_m, g_post, b_post)


def _prep_bwd(dqa, dka, dva, dqb, dkb, dvb, dqm, dga, dgb, dgm, proj, trig, w_uq, w_ukv, g_cq, g_ckv,
              rope_a, rope_b, tm=256):
    t = proj.shape[0]

    def body(dqa_ref, dka_ref, dva_ref, dqb_ref, dkb_ref, dvb_ref, dqm_ref, dga_ref, dgb_ref, dgm_ref,
             bs_ref, trig_ref, wuq_ref, wukv_ref, gcq_ref, gckv_ref, ra_ref, rb_ref,
             dproj_ref, dqf_ref, dkv_ref, dgcq_ref, dgckv_ref):
        i = pl.program_id(0)

        @pl.when(i == 0)
        def _():
            dgcq_ref[...] = jnp.zeros_like(dgcq_ref)
            dgckv_ref[...] = jnp.zeros_like(dgckv_ref)

        ta = _rope_tables(trig_ref[:, 0:LANES], trig_ref[:, LANES:2 * LANES], ra_ref[...])
        tb = _rope_tables(trig_ref[:, 2 * LANES:3 * LANES], trig_ref[:, 3 * LANES:4 * LANES], rb_ref[...])
        for j in range(A_WIDTH // LANES):
            sl = slice(j * LANES, (j + 1) * LANES)
            dproj_ref[:, j * LANES:(j + 1) * LANES] = (
                _rope(dqa_ref[:, sl].astype(F32), ta, 8, inverse=True).astype(BF16))
            dproj_ref[:, 1024 + j * LANES:1024 + (j + 1) * LANES] = (
                _rope(dka_ref[:, sl].astype(F32), ta, 8, inverse=True).astype(BF16))
        dproj_ref[:, 2048:3072] = dva_ref[...]
        dproj_ref[:, 3072:4096] = dga_ref[...]

        lane = lax.broadcasted_iota(jnp.int32, (1, LANES), 1)
        low = lane < 64
        rope_lanes = (lane >= 64) & (lane < 96)
        dkr = jnp.zeros((tm, LANES), F32)
        for h in range(MLA_HEADS):
            sl = slice(h * LANES, (h + 1) * LANES)
            dqf_ref[:, sl] = _rope(dqb_ref[:, sl].astype(F32), tb, 16, inverse=True).astype(BF16)
            dk_h = dkb_ref[:, sl]
            dkv_ref[:, sl] = jnp.where(low, dk_h, dvb_ref[:, sl])
            dkr = dkr + jnp.where(rope_lanes, dk_h.astype(F32), 0.0)
        dkr = _rope(dkr, tb, 16, inverse=True)

        cq_hat, r_q = _rms_hat(bs_ref[:, 0:MLA_Q_RANK], MLA_Q_RANK)
        dcqn = _dot(dqf_ref[...], wuq_ref[...])
        dgcq_ref[...] += _colsum(dcqn * cq_hat)
        dproj_ref[:, COL_CQ:COL_CQ + 256] = _rms_bwd(dcqn * gcq_ref[...], cq_hat, r_q, MLA_Q_RANK).astype(BF16)
        ckv_hat, r_kv = _rms_hat(bs_ref[:, MLA_Q_RANK:MLA_Q_RANK + MLA_KV_RANK], MLA_KV_RANK)
        dckvn = _dot_nt(dkv_ref[...], wukv_ref[...])
        dgckv_ref[...] += _colsum(dckvn * ckv_hat)
        dproj_ref[:, COL_CQ + 256:COL_CQ + 384] = (
            _rms_bwd(dckvn * gckv_ref[...], ckv_hat, r_kv, MLA_KV_RANK).astype(BF16))
        dproj_ref[:, COL_CQ + 384:COL_CQ + 512] = dkr.astype(BF16)
        dproj_ref[:, COL_BG:COL_BG + 512] = dgb_ref[...]
        dproj_ref[:, COL_MQ:COL_MQ + 512] = dqm_ref[...]
        dproj_ref[:, COL_MG:COL_MG + 512] = dgm_ref[...]

    def col(width, idx):
        return pl.BlockSpec((tm, width), lambda i: (i, idx))

    def full(shape):
        return pl.BlockSpec(shape, lambda i: (0, 0))

    return pl.pallas_call(
        body, name="prep_bwd", grid=(t // tm,),
        out_shape=(jax.ShapeDtypeStruct((t, PROJ_W), BF16), jax.ShapeDtypeStruct((t, 1024), BF16),
                   jax.ShapeDtypeStruct((t, 1024), BF16),
                   jax.ShapeDtypeStruct((1, MLA_Q_RANK), F32), jax.ShapeDtypeStruct((1, MLA_KV_RANK), F32)),
        in_specs=[col(1024, 0)] * 6 + [col(512, 0), col(1024, 0), col(512, 0), col(512, 0),
                  col(512, COL_CQ // 512), pl.BlockSpec((tm, 4 * LANES), lambda i: (i, 0)),
                  full((1024, MLA_Q_RANK)), full((MLA_KV_RANK, 1024)),
                  full((1, MLA_Q_RANK)), full((1, MLA_KV_RANK)), full((8, LANES)), full((8, LANES))],
        out_specs=(col(PROJ_W, 0), col(1024, 0), col(1024, 0), full((1, MLA_Q_RANK)), full((1, MLA_KV_RANK))),
        compiler_params=_params(("arbitrary",)),
    )(dqa, dka, dva, dqb, dkb, dvb, dqm, dga, dgb, dgm, proj, trig, w_uq, w_ukv, g_cq, g_ckv, rope_a, rope_b)


def _adamw_math(gv, w, m, v):
    m_new = ADAM_B1 * m + (1.0 - ADAM_B1) * gv
    v_new = ADAM_B2 * v + (1.0 - ADAM_B2) * (gv * gv)
    m_hat = m_new / (1.0 - ADAM_B1 ** ADAM_STEP)
    v_hat = v_new / (1.0 - ADAM_B2 ** ADAM_STEP)
    return -ADAM_LR * (m_hat / (jnp.sqrt(v_hat) + ADAM_EPS) + ADAM_WD * w), m_new, v_new


def _adamw(g, w, m, v, tr, name):
    r, cols = w.shape

    def body(g_ref, w_ref, m_ref, v_ref, go_ref, d_ref, nm_ref, nv_ref):
        gv = g_ref[...]
        go_ref[...] = gv
        d_ref[...], nm_ref[...], nv_ref[...] = _adamw_math(gv, w_ref[...], m_ref[...], v_ref[...])

    tile = pl.BlockSpec((tr, cols), lambda i: (i, 0))
    shape = jax.ShapeDtypeStruct((r, cols), F32)
    return pl.pallas_call(
        body, name=name, grid=(r // tr,),
        out_shape=(shape,) * 4, in_specs=[tile] * 4, out_specs=(tile,) * 4,
        compiler_params=_params(("parallel",)),
    )(g, w, m, v)


def _adamw_pieces(g, w, m, v, pieces, name):
    n = len(pieces)
    per_piece = isinstance(w, (list, tuple))
    shapes = [jax.ShapeDtypeStruct((r1 - r0, c1 - c0), F32) for r0, r1, c0, c1 in pieces]
    args = (g, *w, *m, *v) if per_piece else (g, w, m, v)

    def body(g_ref, *refs):
        ins, outs = refs[:len(args) - 1], refs[len(args) - 1:]
        gv = g_ref[...]
        if not per_piece:
            results = (gv,) + _adamw_math(gv, ins[0][...], ins[1][...], ins[2][...])
        for p, (r0, r1, c0, c1) in enumerate(pieces):
            if per_piece:
                gp = gv[r0:r1, c0:c1]
                vals = (gp,) + _adamw_math(gp, ins[p][...], ins[n + p][...], ins[2 * n + p][...])
            else:
                vals = [full[r0:r1, c0:c1] for full in results]
            for kind, val in enumerate(vals):
                outs[kind * n + p][...] = val

    flat = pl.pallas_call(
        body, name=name, out_shape=tuple(shapes) * 4,
        in_specs=[IN_VMEM] * len(args), out_specs=tuple([IN_VMEM] * (4 * n)),
        compiler_params=_params(None),
    )(*args)
    return [[flat[kind * n + p] for kind in range(4)] for p in range(n)]


def _core_sum(g, recv, core, rows, tr, name, ride=None):
    cols = g.shape[2]
    nblk = rows // tr
    n_in = len(ride.args) if ride else 0
    n_out = len(ride.out_shapes) if ride else 0

    def body(c_ref, g_ref, r_ref, *rest):
        sf_ref, sb_ref = rest[n_in], rest[n_in + 1]
        if ride:
            j, i = pl.program_id(0), pl.program_id(1)
            ride.run(j * nblk + i, 4 * nblk, rest[:n_in], rest[n_in + 2:n_in + 2 + n_out],
                     rest[n_in + 2 + n_out:])
        tot = g_ref[...] + r_ref[...]
        sf_ref[...] = tot
        sb_ref[...] = tot.astype(BF16)

    half = pl.BlockSpec((None, tr, cols), lambda j, i, c_ref: (j, i, 0))
    shapes = (jax.ShapeDtypeStruct((4, rows, cols), F32), jax.ShapeDtypeStruct((4, rows, cols), BF16))
    return pl.pallas_call(
        body, name=name,
        grid_spec=pltpu.PrefetchScalarGridSpec(
            num_scalar_prefetch=1, grid=(4, nblk),
            in_specs=[pl.BlockSpec((None, tr, cols), lambda j, i, c_ref: (j, c_ref[0] * nblk + i, 0)), half]
            + (ride.in_specs if ride else []),
            out_specs=(half, half) + (ANY,) * n_out,
            scratch_shapes=ride.scratch() if ride else []),
        out_shape=shapes + tuple(ride.out_shapes if ride else ()),
        compiler_params=_params(("arbitrary", "arbitrary") if ride else ("parallel", "parallel")),
    )(core, g, recv, *(ride.args if ride else ()))


def _half_to_sibling(g4):
    def plan(in_refs, out_refs, send_sems, recv_sems):
        x, y, c = _position()
        cp = pltpu.make_async_remote_copy(
            src_ref=in_refs[0].at[:, 1 - c], dst_ref=out_refs[0], send_sem=send_sems.at[0],
            recv_sem=recv_sems.at[0], device_id=(x, y, 1 - c), device_id_type=MESH)

        def finish():
            cp.wait_recv()
            cp.wait_send()

        return cp.start, finish

    return _Ride([g4], [jax.ShapeDtypeStruct((4, g4.shape[2], 1024), F32)], (1, 1), plan)


def _gather_plan(src_ref, dst_ref, send_sems, recv_sems, local_sems):
    x, y, c = _position()
    me = 2 * x + y
    rows = src_ref.shape[1]
    cut = -(-rows // 32) * 16
    pieces = (pl.ds(0, cut), pl.ds(cut, rows - cut))
    local = pltpu.make_async_copy(src_ref, dst_ref.at[me], local_sems.at[0])

    def over_ici(sem, k, chip, t, src=None):
        where = dst_ref.at[chip, c, pieces[t]]
        return pltpu.make_async_remote_copy(
            src_ref=where if src is None else src, dst_ref=where, send_sem=send_sems.at[sem],
            recv_sem=recv_sems.at[sem], device_id=(x ^ (k >> 1), y ^ (k & 1), c), device_id_type=MESH)

    def mine_to(k, t):
        return over_ici(2 * (k - 1) + t, k, me, t, src=src_ref.at[c, pieces[t]])

    def from_neighbour(k, t):
        return over_ici(2 * (k - 1) + t, k, me ^ k, t)

    def to_sibling(k, half):
        piece = dst_ref.at[me ^ k, half]
        return pltpu.make_async_remote_copy(
            src_ref=piece, dst_ref=piece, send_sem=send_sems.at[5 + k], recv_sem=recv_sems.at[5 + k],
            device_id=(x, y, 1 - c), device_id_type=MESH)

    sends = [mine_to(2, 0), mine_to(1, 1), mine_to(2, 1), mine_to(1, 0)]
    onward = [over_ici(4, 1, me ^ 2, 0), over_ici(5, 2, me ^ 1, 1)]

    def start():
        local.start()
        for cp in sends:
            cp.start()

    def pass_on():
        from_neighbour(2, 0).wait_recv()
        onward[0].start()
        from_neighbour(1, 1).wait_recv()
        onward[1].start()

    def to_other_core():
        from_neighbour(2, 1).wait_recv()
        to_sibling(2, c).start()
        from_neighbour(1, 0).wait_recv()
        to_sibling(1, c).start()
        over_ici(4, 1, me ^ 3, 0).wait_recv()
        over_ici(5, 2, me ^ 3, 1).wait_recv()
        to_sibling(3, c).start()

    def finish():
        for k in (1, 2, 3):
            to_sibling(k, 1 - c).wait_recv()
        for cp in sends + onward + [to_sibling(k, c) for k in (1, 2, 3)]:
            cp.wait_send()
        local.wait()

    return start, pass_on, to_other_core, finish


def _gather_ride(shard, spread):
    def plan(in_refs, out_refs, send_sems, recv_sems, local_sems):
        return _gather_plan(in_refs[0], out_refs[0], send_sems, recv_sems, local_sems)

    return _Ride([shard], [jax.ShapeDtypeStruct((4,) + shard.shape, shard.dtype)], (9, 9, 1), plan,
                 in_specs=[IN_VMEM], spread=spread)


def _chip_sum(sf, recv, chip, rows, tr, name):
    cols = sf.shape[2]

    def body(me_ref, sf_ref, r_ref, out_ref):
        acc = sf_ref[...]
        for k in range(3):
            acc = acc + r_ref[k].astype(F32)
        out_ref[...] = acc

    return pl.pallas_call(
        body, name=name,
        grid_spec=pltpu.PrefetchScalarGridSpec(
            num_scalar_prefetch=1, grid=(rows // tr,),
            in_specs=[pl.BlockSpec((None, tr, cols), lambda i, me_ref: (me_ref[0], i, 0)),
                      pl.BlockSpec((3, tr, cols), lambda i, me_ref: (0, i, 0))],
            out_specs=pl.BlockSpec((tr, cols), lambda i, me_ref: (i, 0))),
        out_shape=jax.ShapeDtypeStruct((rows, cols), F32),
        compiler_params=_params(("parallel",)),
    )(chip, sf, recv)


def _position():
    return lax.axis_index("x"), lax.axis_index("y"), lax.axis_index("c")


def _dh_scatter(dproj, w_in_arr_t, x, dz, g, sb_in, sb_rest, tm=1024, tk=1024):
    t, d = x.shape
    nk = dproj.shape[1] // tk
    ni = t // tm

    def body(dp_ref, w_ref, x_ref, dz_ref, g_ref, sbin_ref, sbrest_ref,
             dx_ref, dg_ref, db_ref, rin_ref, rrest_ref, acc_ref, send_sems, recv_sems):
        i = pl.program_id(0)
        kk = pl.program_id(1)
        px, py, pc = _position()
        me = 2 * px + py
        srcs = (sbin_ref, sbrest_ref)
        dsts = (rin_ref, rrest_ref)

        def copy(a, k):
            return pltpu.make_async_remote_copy(
                src_ref=srcs[a].at[me ^ k], dst_ref=dsts[a].at[k - 1],
                send_sem=send_sems.at[3 * a + k - 1], recv_sem=recv_sems.at[3 * a + k - 1],
                device_id=(px ^ (k >> 1), py ^ (k & 1), pc), device_id_type=MESH)

        pairs = [(a, k) for a in range(2) for k in (1, 2, 3)]

        @pl.when((i == 0) & (kk == 0))
        def _():
            dg_ref[...] = jnp.zeros_like(dg_ref)
            db_ref[...] = jnp.zeros_like(db_ref)
            for a, k in pairs:
                copy(a, k).start()

        part = _dot(dp_ref[...], w_ref[...])

        @pl.when(kk == 0)
        def _():
            acc_ref[...] = part

        @pl.when(kk > 0)
        def _():
            acc_ref[...] += part

        @pl.when(kk == nk - 1)
        def _():
            xh, rstd = _ln_hat(x_ref[...])
            dht = acc_ref[...] + DEEPNORM_ALPHA * dz_ref[...]
            dg_ref[...] += _colsum(dht * xh)
            db_ref[...] += _colsum(dht)
            dx_ref[...] = _ln_bwd_rows(dht * g_ref[...], xh, rstd)

        @pl.when((i == ni - 1) & (kk == nk - 1))
        def _():
            for a, k in pairs:
                copy(a, k).wait_recv()
            for a, k in pairs:
                copy(a, k).wait_send()

    tile = pl.BlockSpec((tm, d), lambda i, kk: (i, 0))
    row = pl.BlockSpec((1, d), lambda i, kk: (0, 0))
    return pl.pallas_call(
        body, name="dh_scatter", grid=(ni, nk),
        out_shape=(jax.ShapeDtypeStruct((t, d), F32), jax.ShapeDtypeStruct((1, d), F32),
                   jax.ShapeDtypeStruct((1, d), F32),
                   jax.ShapeDtypeStruct((3, HALF_IN, 1024), BF16),
                   jax.ShapeDtypeStruct((3, HALF_REST, 1024), BF16)),
        in_specs=[pl.BlockSpec((tm, tk), lambda i, kk: (i, kk)), pl.BlockSpec((tk, d), lambda i, kk: (kk, 0)),
                  tile, tile, row, ANY, ANY],
        out_specs=(tile, row, row, ANY, ANY),
        scratch_shapes=[pltpu.VMEM((tm, d), F32), pltpu.SemaphoreType.DMA((6,)), pltpu.SemaphoreType.DMA((6,))],
        compiler_params=_params(("arbitrary", "arbitrary")),
    )(dproj, w_in_arr_t, x, dz, g, sb_in, sb_rest)


def _join_halves(gh_in, gh_rest):
    def body(hin_ref, hrest_ref, oin_ref, orest_ref, send_sems, recv_sems, local_sems):
        x, y, c = _position()
        srcs = (hin_ref, hrest_ref)
        dsts = (oin_ref, orest_ref)

        def rows(a, half):
            return dsts[a].at[half]

        local = [pltpu.make_async_copy(srcs[a], rows(a, c), local_sems.at[a]) for a in range(2)]
        remote = [pltpu.make_async_remote_copy(
            src_ref=srcs[a], dst_ref=rows(a, c), send_sem=send_sems.at[a], recv_sem=recv_sems.at[a],
            device_id=(x, y, 1 - c), device_id_type=MESH) for a in range(2)]
        for cp in local + remote:
            cp.start()
        for a in range(2):
            pltpu.make_async_remote_copy(
                src_ref=srcs[a], dst_ref=rows(a, 1 - c), send_sem=send_sems.at[a], recv_sem=recv_sems.at[a],
                device_id=(x, y, 1 - c), device_id_type=MESH).wait_recv()
        for cp in remote:
            cp.wait_send()
        for cp in local:
            cp.wait()

    return pl.pallas_call(
        body, name="join_halves",
        out_shape=(jax.ShapeDtypeStruct((2, HALF_IN, 1024), F32),
                   jax.ShapeDtypeStruct((2, HALF_REST, 1024), F32)),
        in_specs=[IN_VMEM, IN_VMEM], out_specs=(ANY, ANY),
        scratch_shapes=[pltpu.SemaphoreType.DMA((2,)), pltpu.SemaphoreType.DMA((2,)), pltpu.SemaphoreType.DMA((2,))],
    )(gh_in, gh_rest)


def _allreduce_small(vec):
    def body(vec_ref, out_ref, all_ref, send_sems, recv_sems):
        x, y, c = _position()
        me = 4 * x + 2 * y + c
        all_ref[me] = vec_ref[...]

        def copy(k, slot):
            return pltpu.make_async_remote_copy(
                src_ref=vec_ref, dst_ref=all_ref.at[slot], send_sem=send_sems.at[k - 1], recv_sem=recv_sems.at[k - 1],
                device_id=(x ^ (k >> 2), y ^ ((k >> 1) & 1), c ^ (k & 1)), device_id_type=MESH)

        copies = [copy(k, me) for k in range(1, 8)]
        for cp in copies:
            cp.start()
        for k in range(1, 8):
            copy(k, me ^ k).wait_recv()
        for cp in copies:
            cp.wait_send()
        total = all_ref[0]
        for d in range(1, 8):
            total = total + all_ref[d]
        out_ref[...] = total

    return pl.pallas_call(
        body, name="allreduce_small",
        out_shape=jax.ShapeDtypeStruct(vec.shape, vec.dtype),
        in_specs=[pl.BlockSpec(memory_space=pltpu.VMEM)], out_specs=pl.BlockSpec(memory_space=pltpu.VMEM),
        scratch_shapes=[pltpu.VMEM((8,) + vec.shape, vec.dtype), pltpu.SemaphoreType.DMA((7,)),
                        pltpu.SemaphoreType.DMA((7,))],
    )(vec)


def _pack_rest(w_uq, w_ukv, w_mem, w_out):
    rows = jnp.concatenate([w_uq[0].T.reshape(-1, 1024), w_ukv.reshape(-1, 1024), w_mem.reshape(-1, 1024),
                            w_out.reshape(-1, 1024)], axis=0)
    return jnp.pad(rows, ((0, ROWS_REST - ROWS_USED), (0, 0)))


def _arranged_w_in(g_in):
    z = functools.partial(jnp.zeros, dtype=g_in.dtype)
    cut = 4480 - 2 * SHARD_ROWS
    return jnp.concatenate(
        [g_in[0, :SHARD_ROWS], g_in[1, :SHARD_ROWS], g_in[2, :cut], z((64, 1024)), g_in[2, cut:cut + 32],
         z((32, 1024)), g_in[2, cut + 32:SHARD_ROWS], g_in[3, :SHARD_ROWS]], axis=0)


def _rest_weights(g_rest):
    w_uq_t = g_rest[:, 0:ROWS_UQ].reshape(768, 256)
    w_uq_pad_t = jnp.pad(w_uq_t.reshape(MLA_HEADS, MLA_QK_DIM, 256), ((0, 0), (0, 32), (0, 0))).reshape(1024, 256)
    w_ukv = jnp.concatenate([g_rest[j, ROWS_UQ:ROWS_UQ + ROWS_UKV].reshape(128, 256) for j in range(4)], axis=1)
    lo = ROWS_UQ + ROWS_UKV
    w_mem = g_rest[:, lo:lo + ROWS_MEM].reshape(4 * ROWS_MEM, 1024)
    w_out = g_rest[:, lo + ROWS_MEM:lo + ROWS_MEM + ROWS_OUT].reshape(4 * ROWS_OUT, 1024)
    return w_uq_pad_t, w_ukv, w_mem, w_out


def _split_in(dw_in_arr_t):
    a = dw_in_arr_t
    gap = jnp.zeros((ROWS_IN - SHARD_ROWS, 1024), a.dtype)
    nat = 4608 - 96
    pieces = [a[:SHARD_ROWS], gap, a[SHARD_ROWS:2 * SHARD_ROWS], gap,
              a[2 * SHARD_ROWS:4480], a[4544:4576], a[4608:4608 + 3 * SHARD_ROWS - nat], gap,
              a[4608 + 3 * SHARD_ROWS - nat:], gap]
    return jnp.concatenate(pieces, axis=0).reshape(4, ROWS_IN, 1024)


def _split_rest(dw_uq_pad_t, dw_ukv, dw_mem, dw_out):
    dw_uq_t = dw_uq_pad_t.reshape(MLA_HEADS, LANES, 256)[:, :MLA_QK_DIM].reshape(4, ROWS_UQ, 1024)
    parts = [dw_uq_t, dw_ukv.reshape(128, 4, 256).transpose(1, 0, 2).reshape(4, ROWS_UKV, 1024),
             dw_mem.reshape(4, ROWS_MEM, 1024), dw_out.reshape(4, ROWS_OUT, 1024)]
    return jnp.pad(jnp.concatenate(parts, axis=1), ((0, 0), (0, ROWS_REST - ROWS_USED), (0, 0)))


def _rope_consts(rot, first, period):
    half = rot // 2
    inv_freq = np.float32(ROPE_THETA) ** (-(np.arange(0, rot, 2, dtype=np.float32) / np.float32(rot)))
    lane = np.arange(LANES) % period - first
    in_rot = (lane >= 0) & (lane < rot)
    out = np.zeros((8, LANES), np.float32)
    out[0] = np.where(in_rot, inv_freq[np.clip(lane, 0, rot - 1) % half], 0.0)
    out[1] = in_rot & (lane < half)
    out[2] = in_rot & (lane >= half)
    return jnp.asarray(out)


def _band_bias(s):
    nblk = s // BAND_Q
    starts = np.array([_band_start(i, s) for i in range(nblk)])
    uq = (np.arange(nblk)[:, None] * BAND_Q + np.arange(BAND_Q)[None, :])[:, :, None]
    uk = (starts[:, None] + np.arange(BAND_WIN)[None, :])[:, None, :]
    tiles, index, seen = [], [], {}
    for _, d in DILATED:
        length = s // d
        ok = (uq // length == uk // length) & (np.abs(uq - uk) <= 64)
        row = []
        for i in range(nblk):
            key = ok[i].tobytes()
            if key not in seen:
                seen[key] = len(tiles)
                tiles.append(np.where(ok[i], 0.0, NEG_INF).astype(np.float32))
            row.append(seen[key])
        index.append(row)
    return jnp.asarray(np.stack(tiles, axis=0)), index


def _forward_backward(h, h32, proj, trig, rope_consts, x, mem, target, weights, gains):
    w_uq_pad_t, w_ukv, w_mem, w_out = weights
    g_emb, b_emb, g_cq, g_ckv, g_out_a, g_out_b, g_out_m, g_post, b_post = gains
    nb, s, d = x.shape
    t = nb * s
    x2 = x.reshape(t, d)
    mem2 = mem.reshape(nb * N_MEM, d)
    tgt2 = target.reshape(t, d)
    rope_a, rope_b = rope_consts
    bias, bias_index = _band_bias(s)
    scales = (0.125, MLA_QK_DIM ** -0.5, 128 ** -0.5)

    qa, ka, va, qb, kb, vb, qm, cqn, ckvn = _prep(proj, trig, w_uq_pad_t, w_ukv, g_cq, g_ckv, rope_a, rope_b, scales)
    mkv = _mm(mem2, w_mem, BF16, nb * N_MEM, 1024, 1024, "mem_kv")

    cfg_b = dict(nb=nb, s=s, sk=s, heads=8, voff=0, bq=256)
    cfg_m = dict(nb=nb, s=s, sk=N_MEM, heads=4, hpb=2, voff=4, bq=1024)
    ya, lse_a, qkv_ordered = _dilated_fwd(qa, ka, va, bias, bias_index, nb=nb, s=s, name="attn_a_fwd")
    yb, lse_b = _attn_fwd(qb, kb, vb, name="attn_b_fwd", hpb=4, **cfg_b)
    ym, lse_m = _attn_fwd(qm, mkv, mkv, name="attn_m_fwd", **cfg_m)

    (y, dz, doa, dob, dom, dga, dgb, dgm, loss, dg_post, db_post, dg_a, dg_b, dg_m) = _post(
        h32, ya, yb, ym, proj, tgt2, w_out, g_out_a, g_out_b, g_out_m, g_post, b_post)

    dqa, dka, dva = _dilated_bwd(qa, ka, va, qkv_ordered, ya, doa, lse_a, bias, bias_index, nb=nb, s=s, scale=scales[0],
                                 name="attn_a_bwd")
    dqb, dkb, dvb = _attn_bwd(qb, kb, vb, yb, dob, lse_b, name="attn_b_bwd", scale=scales[1], hpb=4, **cfg_b)
    dqm, dmk, dmv = _attn_bwd(qm, mkv, mkv, ym, dom, lse_m, name="attn_m_bwd", scale=scales[2], **cfg_m)
    dmkv = jnp.concatenate([dmk, dmv], axis=1)

    dproj, dqf, dkv, dg_cq, dg_ckv = _prep_bwd(
        dqa, dka, dva, dqb, dkb, dvb, dqm, dga, dgb, dgm, proj, trig, w_uq_pad_t, w_ukv, g_cq, g_ckv, rope_a, rope_b)

    small_rows = (dg_cq, dg_ckv, loss, dg_a, dg_b, dg_m, dg_post, db_post)
    return (dproj, h, y, dz, dqf, cqn, ckvn, dkv, mem2, dmkv), x2, small_rows


def _weight_grads(operands, core):
    dproj, h, y, dz, dqf, cqn, ckvn, dkv, mem2, dmkv = operands
    dw_in_arr_t = _mm(dproj, h, F32, 1024, 1024, 4096, "dw_in", mode="tn")
    g_in = _split_in(dw_in_arr_t)
    dw_out, r_in = _mm(y, dz, F32, 1024, 1024, 2048, "dw_out", mode="tn",
                       ride=_half_to_sibling(g_in.reshape(4, 2, HALF_IN, 1024)))
    dw_uq_pad_t = _mm(dqf, cqn, F32, 1024, 256, 4096, "dw_uq", mode="tn")
    dw_ukv = _mm(ckvn, dkv, F32, 128, 1024, 4096, "dw_ukv", mode="tn")
    dw_mem = _mm(mem2, dmkv, F32, 1024, 1024, mem2.shape[0], "dw_mem", mode="tn")
    g_rest = _split_rest(dw_uq_pad_t, dw_ukv, dw_mem, dw_out)
    sf_in, sb_in, r_rest = _core_sum(g_in, r_in, core, HALF_IN, HALF_IN // 2, "core_sum_in",
                                     ride=_half_to_sibling(g_rest.reshape(4, 2, HALF_REST, 1024)))
    sf_rest, sb_rest = _core_sum(g_rest, r_rest, core, HALF_REST, HALF_REST, "core_sum_rest")
    return sf_in, sb_in, sf_rest, sb_rest


def _small_block(dg_emb, db_emb, small_rows):
    dg_cq, dg_ckv, loss, dg_a, dg_b, dg_m, dg_post, db_post = small_rows
    row2 = jnp.concatenate([dg_cq, dg_ckv, loss, jnp.zeros((1, 512), F32)], axis=1)
    return jnp.concatenate([dg_emb, db_emb, row2, dg_a, jnp.concatenate([dg_b, dg_m], axis=1), dg_post, db_post,
                            jnp.zeros((1, 1024), F32)], axis=0)


def _pack_small(g_emb, b_emb, g_cq, g_ckv, g_out_a, g_out_b, g_out_m, g_post, b_post):
    row2 = jnp.concatenate([g_cq.reshape(1, -1), g_ckv.reshape(1, -1), jnp.zeros((1, 640), F32)], axis=1)
    return jnp.concatenate([g_emb.reshape(1, -1), b_emb.reshape(1, -1), row2, g_out_a.reshape(1, -1),
                            jnp.concatenate([g_out_b.reshape(1, -1), g_out_m.reshape(1, -1)], axis=1),
                            g_post.reshape(1, -1), b_post.reshape(1, -1), jnp.zeros((1, 1024), F32)], axis=0)


def kernel(x, mem, positions, g_emb, b_emb, w_in, g_cq, g_ckv, w_uq, w_ukv, w_mem_kv, g_out_a, g_out_b, g_out_m, w_out, g_post, b_post, loss_target, m_g_emb, m_b_emb, m_w_in, m_g_cq, m_g_ckv, m_w_uq, m_w_ukv, m_w_mem_kv, m_g_out_a, m_g_out_b, m_g_out_m, m_w_out, m_g_post, m_b_post, v_g_emb, v_b_emb, v_w_in, v_g_cq, v_g_ckv, v_w_uq, v_w_ukv, v_w_mem_kv, v_g_out_a, v_g_out_b, v_g_out_m, v_w_out, v_g_post, v_b_post):
    w_rest = _pack_rest(w_uq, w_ukv, w_mem_kv, w_out)
    w_in_t = w_in[0].T
    w_in_b = jnp.pad(w_in_t.astype(BF16), ((0, ROWS_IN - SHARD_ROWS), (0, 0)))
    gains = (g_emb.reshape(1, -1), b_emb.reshape(1, -1), g_cq, g_ckv, g_out_a, g_out_b, g_out_m, g_post, b_post)
    rope_consts = (_rope_consts(16, 0, 64), _rope_consts(32, 64, 128))
    h, h32, trig, gathered_in = _ln_fwd(x.reshape(-1, D_MODEL), gains[0], gains[1],
                                        positions.reshape(-1, 1).astype(F32), *rope_consts,
                                        ride=_gather_ride(w_in_b.reshape(2, HALF_IN, 1024), spread=False))
    w_in_arr_t = _arranged_w_in(gathered_in.reshape(4, ROWS_IN, 1024))
    proj, gathered_rest = _mm(h, w_in_arr_t, F32, 1024, 2048, 1024, "in_proj", mode="nt",
                              ride=_gather_ride(w_rest.astype(BF16).reshape(2, HALF_REST, 1024), spread=True))
    weights = _rest_weights(gathered_rest.reshape(4, ROWS_REST, 1024))
    operands, x2, small_rows = _forward_backward(h, h32, proj, trig, rope_consts, x, mem, loss_target, weights,
                                                 gains)

    core = lax.axis_index("c").astype(jnp.int32).reshape(1)
    chip = (2 * lax.axis_index("x") + lax.axis_index("y")).astype(jnp.int32).reshape(1)
    sf_in, sb_in, sf_rest, sb_rest = _weight_grads(operands, core)
    grad_x, dg_emb, db_emb, rb_in, rb_rest = _dh_scatter(operands[0], w_in_arr_t, x2, operands[3], gains[0],
                                                         sb_in, sb_rest)
    gh_in = _chip_sum(sf_in, rb_in, chip, HALF_IN, HALF_IN // 2, "chip_sum_in")
    gh_rest = _chip_sum(sf_rest, rb_rest, chip, HALF_REST, HALF_REST, "chip_sum_rest")
    grad_in, grad_rest = _join_halves(gh_in, gh_rest)
    grad_in = grad_in.reshape(ROWS_IN, 1024)
    grad_rest = grad_rest.reshape(ROWS_REST, 1024)

    big_in = _adamw(grad_in, w_in_t, m_w_in[0].T, v_w_in[0].T, SHARD_ROWS // 3, "adamw_in")
    def rest_parts(a_uq, a_ukv, a_mem, a_out):
        return [a_uq[0].T.reshape(ROWS_UQ, 1024), a_ukv.reshape(ROWS_UKV, 1024), a_mem[0], a_out[0]]

    uq, ukv, wmem, wout = _adamw_pieces(
        grad_rest, rest_parts(w_uq, w_ukv, w_mem_kv, w_out), rest_parts(m_w_uq, m_w_ukv, m_w_mem_kv, m_w_out),
        rest_parts(v_w_uq, v_w_ukv, v_w_mem_kv, v_w_out), REST_PIECES, "adamw_rest")
    small_sum = _allreduce_small(_small_block(dg_emb, db_emb, small_rows))
    sm = _adamw_pieces(
        small_sum,
        _pack_small(g_emb, b_emb, g_cq, g_ckv, g_out_a, g_out_b, g_out_m, g_post, b_post),
        _pack_small(m_g_emb, m_b_emb, m_g_cq, m_g_ckv, m_g_out_a, m_g_out_b, m_g_out_m, m_g_post, m_b_post),
        _pack_small(v_g_emb, v_b_emb, v_g_cq, v_g_ckv, v_g_out_a, v_g_out_b, v_g_out_m, v_g_post, v_b_post),
        SMALL_PIECES, "adamw_small")
    loss = small_sum[2, 384]

    def ordered(kind):
        s_gemb, s_bemb, s_gcq, s_gckv, s_ga, s_gb, s_gm, s_gpost, s_bpost = [piece[kind] for piece in sm]
        return [s_gemb.reshape(-1), s_bemb.reshape(-1), big_in[kind].T[None], s_gcq, s_gckv,
                uq[kind].reshape(192, 256).T[None], ukv[kind].reshape(1, 128, 256), wmem[kind][None], s_ga, s_gb,
                s_gm, wout[kind][None], s_gpost, s_bpost]

    return (loss, grad_x.reshape(x.shape), *ordered(0), *ordered(1), *ordered(2), *ordered(3))
```

```python
import functools
import math

import jax
import jax.numpy as jnp
import numpy as np
from jax import lax
from jax.experimental import pallas as pl
from jax.experimental.pallas import tpu as pltpu

F32 = jnp.float32
BF16 = jnp.bfloat16
MESH = pl.DeviceIdType.MESH
ANY = pl.BlockSpec(memory_space=pl.ANY)
IN_VMEM = pl.BlockSpec(memory_space=pltpu.VMEM)

D_MODEL = 1024
A_WIDTH = 1024
MLA_HEADS = 8
MLA_Q_RANK = 256
MLA_KV_RANK = 128
MLA_QK_DIM = 96
MEM_WIDTH = 512
N_MEM = 256
ROPE_THETA = 500000.0
NORM_EPS = 1e-5
NEG_INF = -1e30
DEEPNORM_ALPHA = 2.0 ** 0.25
DILATED = ((64, 1), (256, 4), (1024, 16))

ADAM_LR = 0.001
ADAM_B1 = 0.9
ADAM_B2 = 0.999
ADAM_EPS = 1e-08
ADAM_WD = 0.01
ADAM_STEP = 10

LANES = 128
VMEM_LIMIT = 56 * 1024 * 1024
LOG2E = math.log2(math.e)
LN2 = math.log(2.0)

PROJ_W = 6144
COL_CQ = 4096
COL_BG = 4608
COL_MQ = 5120
COL_MG = 5632

SHARD_ROWS = 1512
ROWS_IN = 1536
ROWS_UQ, ROWS_UKV, ROWS_MEM, ROWS_OUT = 48, 32, 256, 512
ROWS_USED = ROWS_UQ + ROWS_UKV + ROWS_MEM + ROWS_OUT
ROWS_REST = 864
HALF_IN = ROWS_IN // 2
HALF_REST = ROWS_REST // 2
REST_PIECES = ((0, 48, 0, 1024), (48, 80, 0, 1024), (80, 336, 0, 1024), (336, 848, 0, 1024))
SMALL_PIECES = ((0, 1, 0, 1024), (1, 2, 0, 1024), (2, 3, 0, 256), (2, 3, 256, 384), (3, 4, 0, 1024), (4, 5, 0, 512),
                (4, 5, 512, 1024), (5, 6, 0, 1024), (6, 7, 0, 1024))


def _params(sem=None, vmem=VMEM_LIMIT):
    return pltpu.CompilerParams(dimension_semantics=sem, vmem_limit_bytes=vmem)


def _dot(a, b):
    return jnp.dot(a, b, preferred_element_type=F32)


def _dot_nt(a, b):
    return lax.dot_general(a, b, (((1,), (1,)), ((), ())), preferred_element_type=F32)


def _dot_tn(a, b):
    return lax.dot_general(a, b, (((0,), (0,)), ((), ())), preferred_element_type=F32)


def _ln_hat(x):
    mu = jnp.mean(x, axis=-1, keepdims=True)
    xc = x - mu
    var = jnp.mean(xc * xc, axis=-1, keepdims=True)
    rstd = lax.rsqrt(var + NORM_EPS)
    return xc * rstd, rstd


def _ln_bwd_rows(dxh, xh, rstd):
    return rstd * (dxh - jnp.mean(dxh, axis=-1, keepdims=True) - xh * jnp.mean(dxh * xh, axis=-1, keepdims=True))


def _rms_hat(x, width):
    ms = jnp.sum(x * x, axis=-1, keepdims=True) * (1.0 / width)
    r = lax.rsqrt(ms + NORM_EPS)
    return x * r, r


def _rms_bwd(u, xh, r, width):
    return r * (u - xh * (jnp.sum(u * xh, axis=-1, keepdims=True) * (1.0 / width)))


def _colsum(v):
    return jnp.sum(v, axis=0, keepdims=True)


def _rope_tables(cos, sin, consts):
    return cos, sin * consts[2:3, :], -sin * consts[1:2, :]


def _rope(x, tables, half, inverse=False):
    c, s_up, s_dn = tables
    if inverse:
        s_up, s_dn = -s_up, -s_dn
    return x * c + pltpu.roll(x, half, 1) * s_up + pltpu.roll(x, LANES - half, 1) * s_dn


def _ln_fwd(x, g, b, pos, rope_a, rope_b, tm=512, ride=None):
    t, d = x.shape
    n_in = len(ride.args) if ride else 0
    n_out = len(ride.out_shapes) if ride else 0
    steps = t // tm

    def body(x_ref, g_ref, b_ref, pos_ref, ra_ref, rb_ref, *rest):
        h_ref, h32_ref, trig_ref = rest[n_in:n_in + 3]
        if ride:
            i = pl.program_id(0)
            ride.run(i, steps, rest[:n_in], rest[n_in + 3:n_in + 3 + n_out], rest[n_in + 3 + n_out:])
        xh, _ = _ln_hat(x_ref[...])
        h = xh * g_ref[...] + b_ref[...]
        h32_ref[...] = h
        h_ref[...] = h.astype(BF16)
        for j, consts in enumerate((ra_ref, rb_ref)):
            ang = pos_ref[...] * consts[0:1, :]
            trig_ref[:, 2 * j * LANES:(2 * j + 1) * LANES] = jnp.cos(ang)
            trig_ref[:, (2 * j + 1) * LANES:(2 * j + 2) * LANES] = jnp.sin(ang)

    row = pl.BlockSpec((1, d), lambda i: (0, 0))
    tile = pl.BlockSpec((tm, d), lambda i: (i, 0))
    consts = pl.BlockSpec((8, LANES), lambda i: (0, 0))
    trig_tile = pl.BlockSpec((tm, 4 * LANES), lambda i: (i, 0))
    in_specs = [tile, row, row, pl.BlockSpec((tm, 1), lambda i: (i, 0)), consts, consts]
    shapes = (jax.ShapeDtypeStruct((t, d), BF16), jax.ShapeDtypeStruct((t, d), F32),
              jax.ShapeDtypeStruct((t, 4 * LANES), F32))
    if not ride:
        return pl.pallas_call(
            body, name="ln_fwd", grid=(steps,), out_shape=shapes, in_specs=in_specs,
            out_specs=(tile, tile, trig_tile), compiler_params=_params(("parallel",)),
        )(x, g, b, pos, rope_a, rope_b)
    return pl.pallas_call(
        body, name="ln_fwd", grid=(steps,),
        out_shape=(*shapes, *ride.out_shapes),
        in_specs=in_specs + ride.in_specs, out_specs=(tile, tile, trig_tile) + (ANY,) * n_out,
        scratch_shapes=ride.scratch(),
        compiler_params=_params(("arbitrary",)),
    )(x, g, b, pos, rope_a, rope_b, *ride.args)


class _Ride:
    def __init__(self, args, out_shapes, sem_counts, plan, in_specs=None, spread=True):
        self.args, self.out_shapes, self.plan = list(args), list(out_shapes), plan
        self.sem_counts = sem_counts
        self.in_specs = in_specs or [ANY] * len(self.args)
        self.spread = spread

    def scratch(self):
        return [pltpu.SemaphoreType.DMA((n,)) for n in self.sem_counts]

    def run(self, step, total, in_refs, out_refs, sems):
        count = len(self.plan(in_refs, out_refs, *sems))
        at = [(k * (total - 1)) // (count - 1) if self.spread or k == 0 else total - 1 for k in range(count)]
        for when in sorted(set(at)):
            @pl.when(step == when)
            def _(when=when):
                stages = self.plan(in_refs, out_refs, *sems)
                for k in range(count):
                    if at[k] == when:
                        stages[k]()


def _mm(a, b, out_dtype, tm, tn, tk, name, mode="nn", ride=None):
    if mode == "tn":
        k, m = a.shape
    else:
        m, k = a.shape
    n = b.shape[0] if mode == "nt" else b.shape[1]
    nk = k // tk
    nj, ni = n // tn, m // tm
    n_in = len(ride.args) if ride else 0
    n_out = len(ride.out_shapes) if ride else 0

    def body(a_ref, b_ref, *rest):
        o_ref = rest[n_in]
        acc_ref = rest[n_in + 1 + n_out]
        if ride:
            j, i, kk = pl.program_id(0), pl.program_id(1), pl.program_id(2)
            ride.run((j * ni + i) * nk + kk, nj * ni * nk, rest[:n_in], rest[n_in + 1:n_in + 1 + n_out],
                     rest[n_in + 2 + n_out:])
        av = a_ref[...].astype(BF16)
        bv = b_ref[...].astype(BF16)
        part = _dot_tn(av, bv) if mode == "tn" else _dot_nt(av, bv) if mode == "nt" else _dot(av, bv)
        if nk == 1:
            o_ref[...] = part.astype(out_dtype)
        else:
            kk = pl.program_id(2)

            @pl.when(kk == 0)
            def _():
                acc_ref[...] = part

            @pl.when(kk > 0)
            def _():
                acc_ref[...] += part

            @pl.when(kk == nk - 1)
            def _():
                o_ref[...] = acc_ref[...].astype(out_dtype)

    a_spec = (pl.BlockSpec((tk, tm), lambda j, i, kk: (kk, i)) if mode == "tn"
              else pl.BlockSpec((tm, tk), lambda j, i, kk: (i, kk)))
    b_spec = (pl.BlockSpec((tn, tk), lambda j, i, kk: (j, kk)) if mode == "nt"
              else pl.BlockSpec((tk, tn), lambda j, i, kk: (kk, j)))
    o_spec = pl.BlockSpec((tm, tn), lambda j, i, kk: (i, j))
    o_shape = jax.ShapeDtypeStruct((m, n), out_dtype)
    if not ride:
        return pl.pallas_call(
            body, name=name, grid=(nj, ni, nk), out_shape=o_shape, in_specs=[a_spec, b_spec], out_specs=o_spec,
            scratch_shapes=[pltpu.VMEM((tm, tn), F32)],
            compiler_params=_params(("parallel", "parallel", "arbitrary")),
        )(a, b)
    return pl.pallas_call(
        body, name=name, grid=(nj, ni, nk),
        out_shape=(o_shape, *ride.out_shapes),
        in_specs=[a_spec, b_spec] + ride.in_specs,
        out_specs=(o_spec,) + (ANY,) * n_out,
        scratch_shapes=[pltpu.VMEM((tm, tn), F32)] + ride.scratch(),
        compiler_params=_params(("arbitrary", "arbitrary", "arbitrary")),
    )(a, b, *ride.args)


def _prep(proj, trig, w_uq, w_ukv, g_cq, g_ckv, rope_a, rope_b, scales, tm=256):
    t = proj.shape[0]
    sc_a, sc_b, sc_m = (s * LOG2E for s in scales)

    def body(aq_ref, ak_ref, av_ref, bs_ref, mq_ref, trig_ref, wuq_ref, wukv_ref, gcq_ref, gckv_ref,
             ra_ref, rb_ref, qa_ref, ka_ref, va_ref, qb_ref, kb_ref, vb_ref, qm_ref, cqn_ref, ckvn_ref):
        ta = _rope_tables(trig_ref[:, 0:LANES], trig_ref[:, LANES:2 * LANES], ra_ref[...])
        tb = _rope_tables(trig_ref[:, 2 * LANES:3 * LANES], trig_ref[:, 3 * LANES:4 * LANES], rb_ref[...])
        for j in range(A_WIDTH // LANES):
            sl = slice(j * LANES, (j + 1) * LANES)
            qa_ref[:, sl] = (_rope(aq_ref[:, sl], ta, 8) * sc_a).astype(BF16)
            ka_ref[:, sl] = _rope(ak_ref[:, sl], ta, 8).astype(BF16)
        va_ref[...] = av_ref[...].astype(BF16)
        qm_ref[...] = (mq_ref[...] * sc_m).astype(BF16)

        cq_hat, _ = _rms_hat(bs_ref[:, 0:MLA_Q_RANK], MLA_Q_RANK)
        cqn = (cq_hat * gcq_ref[...]).astype(BF16)
        cqn_ref[...] = cqn
        ckv_hat, _ = _rms_hat(bs_ref[:, MLA_Q_RANK:MLA_Q_RANK + MLA_KV_RANK], MLA_KV_RANK)
        ckvn = (ckv_hat * gckv_ref[...]).astype(BF16)
        ckvn_ref[...] = ckvn
        qfull = _dot_nt(cqn, wuq_ref[...])
        kv = _dot(ckvn, wukv_ref[...])
        kr = _rope(bs_ref[:, 384:512], tb, 16)
        lane = lax.broadcasted_iota(jnp.int32, (1, LANES), 1)
        low = lane < 64
        for h in range(MLA_HEADS):
            sl = slice(h * LANES, (h + 1) * LANES)
            qb_ref[:, sl] = (_rope(qfull[:, sl], tb, 16) * sc_b).astype(BF16)
            kb_ref[:, sl] = jnp.where(low, kv[:, sl], kr).astype(BF16)
            vb_ref[:, sl] = jnp.where(low, 0.0, kv[:, sl]).astype(BF16)

    def col(width, idx):
        return pl.BlockSpec((tm, width), lambda i: (i, idx))

    def full(shape):
        return pl.BlockSpec(shape, lambda i: (0, 0))

    wide = jax.ShapeDtypeStruct((t, 1024), BF16)
    return pl.pallas_call(
        body, name="prep", grid=(t // tm,),
        out_shape=(wide, wide, wide, wide, wide, wide,
                   jax.ShapeDtypeStruct((t, MEM_WIDTH), BF16),
                   jax.ShapeDtypeStruct((t, MLA_Q_RANK), BF16),
                   jax.ShapeDtypeStruct((t, MLA_KV_RANK), BF16)),
        in_specs=[col(1024, 0), col(1024, 1), col(1024, 2), col(512, COL_CQ // 512), col(512, COL_MQ // 512),
                  pl.BlockSpec((tm, 4 * LANES), lambda i: (i, 0)),
                  full((1024, MLA_Q_RANK)), full((MLA_KV_RANK, 1024)),
                  full((1, MLA_Q_RANK)), full((1, MLA_KV_RANK)), full((8, LANES)), full((8, LANES))],
        out_specs=(col(1024, 0),) * 6 + (col(MEM_WIDTH, 0), col(MLA_Q_RANK, 0), col(MLA_KV_RANK, 0)),
        compiler_params=_params(("parallel",)),
    )(proj, proj, proj, proj, proj, trig, w_uq, w_ukv, g_cq, g_ckv, rope_a, rope_b)


def _attn_fwd(q, k, v, *, nb, s, sk, heads, hpb, voff, bq, name):
    nq = s // bq
    width = hpb * LANES
    vblk = voff // hpb

    def body(q_ref, k_ref, v_ref, o_ref, lse_ref):
        for h in range(hpb):
            sl = slice(h * LANES, (h + 1) * LANES)
            sc = _dot_nt(q_ref[:, sl], k_ref[:, sl])
            m = jnp.max(sc, axis=1, keepdims=True)
            p = jnp.exp2(sc - m)
            l = jnp.sum(p, axis=1, keepdims=True)
            o_ref[:, sl] = _dot(p.astype(BF16), v_ref[:, sl]) / l
            lse_ref[:, sl] = jnp.broadcast_to(m + jnp.log(l) * LOG2E, (bq, LANES))

    out = jax.ShapeDtypeStruct((nb * s, heads * LANES), F32)
    ospec = pl.BlockSpec((bq, width), lambda b, i, g: (b * nq + i, g))
    return pl.pallas_call(
        body, name=name, grid=(nb, nq, heads // hpb),
        out_shape=(out, out),
        in_specs=[ospec, pl.BlockSpec((sk, width), lambda b, i, g: (b, g)),
                  pl.BlockSpec((sk, width), lambda b, i, g: (b, vblk + g))],
        out_specs=(ospec, ospec),
        compiler_params=_params(("parallel", "parallel", "parallel")),
    )(q, k, v)


def _attn_bwd(q, k, v, o, do, lse, *, nb, s, sk, heads, hpb, voff, scale, bq, name):
    nq = s // bq
    width = hpb * LANES
    vblk = voff // hpb

    def body(q_ref, k_ref, v_ref, o_ref, do_ref, lse_ref, dq_ref, dk_ref, dv_ref, dk_acc, dv_acc):
        i = pl.program_id(2)

        @pl.when(i == 0)
        def _():
            dk_acc[...] = jnp.zeros_like(dk_acc)
            dv_acc[...] = jnp.zeros_like(dv_acc)

        for h in range(hpb):
            sl = slice(h * LANES, (h + 1) * LANES)
            qh = q_ref[:, sl]
            kk = k_ref[:, sl]
            doh = do_ref[:, sl]
            delta = jnp.sum(doh.astype(F32) * o_ref[:, sl], axis=1, keepdims=True)
            p = jnp.exp2(_dot_nt(qh, kk) - lse_ref[:, h * LANES:h * LANES + 1])
            ds = (p * (_dot_nt(doh, v_ref[:, sl]) - delta)).astype(BF16)
            dq_ref[:, sl] = (_dot(ds, kk) * scale).astype(BF16)
            dk_acc[:, sl] += _dot_tn(ds, qh)
            dv_acc[:, sl] += _dot_tn(p.astype(BF16), doh)

        @pl.when(i == nq - 1)
        def _():
            dk_ref[...] = (dk_acc[...] * LN2).astype(BF16)
            dv_ref[...] = dv_acc[...].astype(BF16)

    qspec = pl.BlockSpec((bq, width), lambda b, g, i: (b * nq + i, g))
    kv_spec = pl.BlockSpec((sk, width), lambda b, g, i: (b, g))
    dq_shape = jax.ShapeDtypeStruct((nb * s, heads * LANES), BF16)
    dkv_shape = jax.ShapeDtypeStruct((nb * sk, heads * LANES), BF16)
    return pl.pallas_call(
        body, name=name, grid=(nb, heads // hpb, nq),
        out_shape=(dq_shape, dkv_shape, dkv_shape),
        in_specs=[qspec, kv_spec, pl.BlockSpec((sk, width), lambda b, g, i: (b, vblk + g)), qspec, qspec, qspec],
        out_specs=(qspec, kv_spec, kv_spec),
        scratch_shapes=[pltpu.VMEM((sk, width), F32), pltpu.VMEM((sk, width), F32)],
        compiler_params=_params(("parallel", "parallel", "arbitrary")),
    )(q, k, v, o, do, lse)


BAND_Q = 128
BAND_WIN = 256


def _band_start(i, s):
    return min(max(i * BAND_Q - 64, 0), s - BAND_WIN)


def _to_pattern_order(src_ref, dst_ref, stage_ref, s, d):
    length = s // d
    stage_ref[...] = src_ref[...].astype(F32)
    for r in range(d):
        dst_ref[r * length:(r + 1) * length, :] = stage_ref[pl.ds(r, length, stride=d), :].astype(dst_ref.dtype)


def _dilated_fwd(q, k, v, bias, bias_index, *, nb, s, name):
    nblk = s // BAND_Q
    npat = len(DILATED)

    def body(q_ref, k_ref, v_ref, bias_ref, o_ref, lse_ref, *rest):
        ordered = rest[:3 * (npat - 1)]
        stage_ref, op_ref, lp_ref, on_ref, ln_ref = rest[3 * (npat - 1):]
        lane = lax.broadcasted_iota(jnp.int32, (1, LANES), 1)
        first = lane < 64
        for p, (_, d) in enumerate(DILATED):
            if d == 1:
                qs, ks, vs = q_ref, k_ref, v_ref
            else:
                qs, ks, vs = ordered[3 * (p - 1):3 * p]
                for src, dst in ((q_ref, qs), (k_ref, ks), (v_ref, vs)):
                    _to_pattern_order(src, dst, stage_ref, s, d)
            for i in range(nblk):
                u0 = i * BAND_Q
                st = _band_start(i, s)
                qi = qs[u0:u0 + BAND_Q, :]
                kw = ks[st:st + BAND_WIN, :]
                vw = vs[st:st + BAND_WIN, :]
                zero = jnp.zeros_like(qi)
                q2 = jnp.concatenate([jnp.where(first, qi, zero), jnp.where(first, zero, qi)], axis=0)
                sc = _dot_nt(q2, kw)
                b = bias_ref[bias_index[p][i]]
                halves = []
                for h in range(2):
                    sh = sc[h * BAND_Q:(h + 1) * BAND_Q] + b
                    m = jnp.max(sh, axis=1, keepdims=True)
                    pr = jnp.exp2(sh - m)
                    l = jnp.sum(pr, axis=1, keepdims=True)
                    halves.append((pr.astype(BF16), l, m + jnp.log(l) * LOG2E))
                o2 = _dot(jnp.concatenate([halves[0][0], halves[1][0]], axis=0), vw)
                o_blk = jnp.where(first, o2[:BAND_Q] / halves[0][1], o2[BAND_Q:] / halves[1][1])
                lse_blk = jnp.where(first, jnp.broadcast_to(halves[0][2], (BAND_Q, LANES)),
                                    jnp.broadcast_to(halves[1][2], (BAND_Q, LANES)))
                op_ref[p, u0:u0 + BAND_Q, :] = o_blk
                lp_ref[p, u0:u0 + BAND_Q, :] = lse_blk
            if d > 1:
                length = s // d
                for r in range(d):
                    on_ref.at[p - 1][pl.ds(r, length, stride=d), :] = op_ref[p, r * length:(r + 1) * length, :]
                    ln_ref.at[p - 1][pl.ds(r, length, stride=d), :] = lp_ref[p, r * length:(r + 1) * length, :]
        lses = [lp_ref[0]] + [ln_ref[p] for p in range(npat - 1)]
        outs = [op_ref[0]] + [on_ref[p] for p in range(npat - 1)]
        m = functools.reduce(jnp.maximum, lses)
        ws = [jnp.exp2(l - m) for l in lses]
        den = functools.reduce(lambda a, c: a + c, ws)
        o_ref[...] = functools.reduce(lambda a, c: a + c, [w * o for w, o in zip(ws, outs)]) / den
        lse_ref[...] = m + jnp.log(den) * LOG2E

    blk = pl.BlockSpec((s, LANES), lambda b, g: (b, g))
    out = jax.ShapeDtypeStruct((nb * s, A_WIDTH), F32)
    copy = jax.ShapeDtypeStruct((nb * s, A_WIDTH), BF16)
    n_copies = 3 * (npat - 1)
    res = pl.pallas_call(
        body, name=name, grid=(nb, A_WIDTH // LANES),
        out_shape=(out, out) + (copy,) * n_copies,
        in_specs=[blk, blk, blk, pl.BlockSpec(bias.shape, lambda b, g: (0, 0, 0))],
        out_specs=(blk, blk) + (blk,) * n_copies,
        scratch_shapes=[pltpu.VMEM((s, LANES), F32), pltpu.VMEM((npat, s, LANES), F32),
                        pltpu.VMEM((npat, s, LANES), F32), pltpu.VMEM((npat - 1, s, LANES), F32),
                        pltpu.VMEM((npat - 1, s, LANES), F32)],
        compiler_params=_params(("parallel", "parallel")),
    )(q, k, v, bias)
    return res[0], res[1], res[2:]


def _dilated_bwd(q, k, v, ordered, o, do, lse, bias, bias_index, *, nb, s, scale, name):
    nblk = s // BAND_Q
    npat = len(DILATED)
    n_copies = 3 * (npat - 1)

    def body(q_ref, k_ref, v_ref, *rest):
        ordered_refs = rest[:n_copies]
        (o_ref, do_ref, lse_ref, bias_ref, dq_out, dk_out, dv_out, stage_ref, rs_ref, dop_ref, rsp_ref,
         dqp_ref, dkp_ref, dvp_ref, dq_ref, dk_ref, dv_ref, nat_ref) = rest[n_copies:]
        lane = lax.broadcasted_iota(jnp.int32, (1, LANES), 1)
        first = lane < 64
        prod = do_ref[...].astype(F32) * o_ref[...]
        d0 = jnp.sum(jnp.where(first, prod, 0.0), axis=1, keepdims=True)
        d1 = jnp.sum(jnp.where(first, 0.0, prod), axis=1, keepdims=True)
        delta = jnp.where(first, jnp.broadcast_to(d0, (s, LANES)), jnp.broadcast_to(d1, (s, LANES)))
        rs_ref[...] = jnp.where((lane & 32) == 0, lse_ref[...], delta)
        for p, (_, d) in enumerate(DILATED):
            length = s // d
            if d == 1:
                qs, ks, vs, dos, rss = q_ref, k_ref, v_ref, do_ref, rs_ref
                dqs, dks, dvs = dq_ref, dk_ref, dv_ref
            else:
                for src, dst in ((do_ref, dop_ref), (rs_ref, rsp_ref)):
                    _to_pattern_order(src, dst, stage_ref, s, d)
                qs, ks, vs = ordered_refs[3 * (p - 1):3 * p]
                dos, rss = dop_ref, rsp_ref
                dqs, dks, dvs = dqp_ref, dkp_ref, dvp_ref
            dks[...] = jnp.zeros((s, LANES), F32)
            dvs[...] = jnp.zeros((s, LANES), F32)
            for i in range(nblk):
                u0 = i * BAND_Q
                st = _band_start(i, s)
                qi = qs[u0:u0 + BAND_Q, :]
                doi = dos[u0:u0 + BAND_Q, :]
                kw = ks[st:st + BAND_WIN, :]
                vw = vs[st:st + BAND_WIN, :]
                zero = jnp.zeros_like(qi)
                q2 = jnp.concatenate([jnp.where(first, qi, zero), jnp.where(first, zero, qi)], axis=0)
                do2 = jnp.concatenate([jnp.where(first, doi, zero), jnp.where(first, zero, doi)], axis=0)
                sc = _dot_nt(q2, kw)
                dp = _dot_nt(do2, vw)
                b = bias_ref[bias_index[p][i]]
                rs_i = rss[u0:u0 + BAND_Q, :]
                ps, dss = [], []
                for h in range(2):
                    rows = slice(h * BAND_Q, (h + 1) * BAND_Q)
                    pr = jnp.exp2(sc[rows] + b - rs_i[:, 64 * h:64 * h + 1])
                    ps.append(pr.astype(BF16))
                    dss.append((pr * (dp[rows] - rs_i[:, 64 * h + 32:64 * h + 33])).astype(BF16))
                p2 = jnp.concatenate(ps, axis=0)
                ds2 = jnp.concatenate(dss, axis=0)
                dq2 = _dot(ds2, kw)
                dqs[u0:u0 + BAND_Q, :] = jnp.where(first, dq2[:BAND_Q], dq2[BAND_Q:]) * scale
                dks[st:st + BAND_WIN, :] += _dot_tn(ds2, q2)
                dvs[st:st + BAND_WIN, :] += _dot_tn(p2, do2)
            if d > 1:
                for j, src in enumerate((dqp_ref, dkp_ref, dvp_ref)):
                    for r in range(d):
                        nat_ref.at[p - 1, j][pl.ds(r, length, stride=d), :] = src[r * length:(r + 1) * length, :]

        def total(j, first_ref):
            return functools.reduce(lambda a, c: a + c, [first_ref[...]] + [nat_ref[p, j] for p in range(npat - 1)])

        dq_out[...] = total(0, dq_ref).astype(BF16)
        dk_out[...] = (total(1, dk_ref) * LN2).astype(BF16)
        dv_out[...] = total(2, dv_ref).astype(BF16)

    blk = pl.BlockSpec((s, LANES), lambda b, g: (b, g))
    out = jax.ShapeDtypeStruct((nb * s, A_WIDTH), BF16)
    f32_buf = pltpu.VMEM((s, LANES), F32)
    bf_buf = pltpu.VMEM((s, LANES), BF16)
    return pl.pallas_call(
        body, name=name, grid=(nb, A_WIDTH // LANES),
        out_shape=(out, out, out),
        in_specs=[blk] * (6 + n_copies) + [pl.BlockSpec(bias.shape, lambda b, g: (0, 0, 0))],
        out_specs=(blk, blk, blk),
        scratch_shapes=[f32_buf, f32_buf, bf_buf] + [f32_buf] * 7 + [pltpu.VMEM((npat - 1, 3, s, LANES), F32)],
        compiler_params=_params(("parallel", "parallel")),
    )(q, k, v, *ordered, o, do, lse, bias)


def _post(h32, ya, ybp, ym, proj, target, w_out, g_a, g_b, g_m, g_post, b_post, tm=256):
    t = h32.shape[0]

    def body(h_ref, ya_ref, yb_ref, ym_ref, ga_ref, gb_ref, gm_ref, tg_ref, wo_ref,
             goa_ref, gob_ref, gom_ref, gp_ref, bp_ref,
             y_ref, dz_ref, doa_ref, dob_ref, dom_ref, dga_ref, dgb_ref, dgm_ref,
             loss_ref, dgp_ref, dbp_ref, dgoa_ref, dgob_ref, dgom_ref):
        i = pl.program_id(0)

        @pl.when(i == 0)
        def _():
            for r in (loss_ref, dgp_ref, dbp_ref, dgoa_ref, dgob_ref, dgom_ref):
                r[...] = jnp.zeros_like(r)

        lane = lax.broadcasted_iota(jnp.int32, (1, LANES), 1)
        low = lane < 64
        h = h_ref[...]

        ybp_v = yb_ref[...]
        yb = jnp.concatenate(
            [jnp.where(low, pltpu.roll(ybp_v[:, 2 * j * LANES:(2 * j + 1) * LANES], 64, 1),
                       ybp_v[:, (2 * j + 1) * LANES:(2 * j + 2) * LANES]) for j in range(4)], axis=1)

        def gated(raw, gate, gain, width):
            xh, r = _rms_hat(raw, width)
            n = xh * gain
            sg = 1.0 / (1.0 + jnp.exp(-gate))
            return xh, r, n, sg, n * (gate * sg)

        gate_a, gate_b, gate_m = ga_ref[...], gb_ref[...], gm_ref[...]
        xh_a, r_a, n_a, sg_a, y_a = gated(ya_ref[...], gate_a, goa_ref[...], A_WIDTH)
        xh_b, r_b, n_b, sg_b, y_b = gated(yb, gate_b, gob_ref[...], 512)
        xh_m, r_m, n_m, sg_m, y_m = gated(ym_ref[...], gate_m, gom_ref[...], 512)
        y = jnp.concatenate([y_a, y_b, y_m], axis=1).astype(BF16)
        y_ref[...] = y
        z = DEEPNORM_ALPHA * h + _dot(y, wo_ref[...])
        zh, rstd = _ln_hat(z)
        err = zh * gp_ref[...] + bp_ref[...] - tg_ref[...]
        rows = jnp.sum(err * err, axis=1, keepdims=True)
        loss_ref[...] += jnp.broadcast_to(jnp.sum(rows, axis=0, keepdims=True) * (0.5 / D_MODEL), (1, LANES))
        dout = err * (1.0 / D_MODEL)
        dgp_ref[...] += _colsum(dout * zh)
        dbp_ref[...] += _colsum(dout)
        dz = _ln_bwd_rows(dout * gp_ref[...], zh, rstd)
        dz_ref[...] = dz
        dy = _dot_nt(dz.astype(BF16), wo_ref[...])

        def gated_bwd(dyg, xh, r, n, sg, gate, gain, width, dgain_ref):
            dn = dyg * (gate * sg)
            dgate = dyg * n * (sg * (1.0 + gate * (1.0 - sg)))
            dgain_ref[...] += _colsum(dn * xh)
            return _rms_bwd(dn * gain, xh, r, width), dgate

        dya, dgate_a = gated_bwd(dy[:, 0:1024], xh_a, r_a, n_a, sg_a, gate_a, goa_ref[...], A_WIDTH, dgoa_ref)
        dyb, dgate_b = gated_bwd(dy[:, 1024:1536], xh_b, r_b, n_b, sg_b, gate_b, gob_ref[...], 512, dgob_ref)
        dym, dgate_m = gated_bwd(dy[:, 1536:2048], xh_m, r_m, n_m, sg_m, gate_m, gom_ref[...], 512, dgom_ref)
        doa_ref[...] = dya.astype(BF16)
        dom_ref[...] = dym.astype(BF16)
        dga_ref[...] = dgate_a.astype(BF16)
        dgb_ref[...] = dgate_b.astype(BF16)
        dgm_ref[...] = dgate_m.astype(BF16)
        for j in range(4):
            blk = dyb[:, j * LANES:(j + 1) * LANES]
            dob_ref[:, 2 * j * LANES:(2 * j + 1) * LANES] = jnp.where(low, 0.0, pltpu.roll(blk, 64, 1)).astype(BF16)
            dob_ref[:, (2 * j + 1) * LANES:(2 * j + 2) * LANES] = jnp.where(low, 0.0, blk).astype(BF16)

    def col(width, idx):
        return pl.BlockSpec((tm, width), lambda i: (i, idx))

    def full(shape):
        return pl.BlockSpec(shape, lambda i: (0, 0))

    def acc(width):
        return jax.ShapeDtypeStruct((1, width), F32)

    return pl.pallas_call(
        body, name="post", grid=(t // tm,),
        out_shape=(jax.ShapeDtypeStruct((t, 2048), BF16), jax.ShapeDtypeStruct((t, 1024), F32),
                   jax.ShapeDtypeStruct((t, 1024), BF16), jax.ShapeDtypeStruct((t, 1024), BF16),
                   jax.ShapeDtypeStruct((t, 512), BF16),
                   jax.ShapeDtypeStruct((t, 1024), BF16), jax.ShapeDtypeStruct((t, 512), BF16),
                   jax.ShapeDtypeStruct((t, 512), BF16),
                   acc(LANES), acc(1024), acc(1024), acc(1024), acc(512), acc(512)),
        in_specs=[col(1024, 0), col(1024, 0), col(1024, 0), col(512, 0),
                  col(1024, 3), col(512, COL_BG // 512), col(512, COL_MG // 512), col(1024, 0),
                  full((2048, 1024)),
                  full((1, 1024)), full((1, 512)), full((1, 512)), full((1, 1024)), full((1, 1024))],
        out_specs=(col(2048, 0), col(1024, 0), col(1024, 0), col(1024, 0), col(512, 0),
                   col(1024, 0), col(512, 0), col(512, 0),
                   full((1, LANES)), full((1, 1024)), full((1, 1024)), full((1, 1024)), full((1, 512)),
                   full((1, 512))),
        compiler_params=_params(("arbitrary",)),
    )(h32, ya, ybp, ym, proj, proj, proj, target, w_out, g_a, g_b, g_m, g_post, b_post)


def _prep_bwd(dqa, dka, dva, dqb, dkb, dvb, dqm, dga, dgb, dgm, proj, trig, w_uq, w_ukv, g_cq, g_ckv,
              rope_a, rope_b, tm=256):
    t = proj.shape[0]

    def body(dqa_ref, dka_ref, dva_ref, dqb_ref, dkb_ref, dvb_ref, dqm_ref, dga_ref, dgb_ref, dgm_ref,
             bs_ref, trig_ref, wuq_ref, wukv_ref, gcq_ref, gckv_ref, ra_ref, rb_ref,
             dproj_ref, dqf_ref, dkv_ref, dgcq_ref, dgckv_ref):
        i = pl.program_id(0)

        @pl.when(i == 0)
        def _():
            dgcq_ref[...] = jnp.zeros_like(dgcq_ref)
            dgckv_ref[...] = jnp.zeros_like(dgckv_ref)

        ta = _rope_tables(trig_ref[:, 0:LANES], trig_ref[:, LANES:2 * LANES], ra_ref[...])
        tb = _rope_tables(trig_ref[:, 2 * LANES:3 * LANES], trig_ref[:, 3 * LANES:4 * LANES], rb_ref[...])
        for j in range(A_WIDTH // LANES):
            sl = slice(j * LANES, (j + 1) * LANES)
            dproj_ref[:, j * LANES:(j + 1) * LANES] = (
                _rope(dqa_ref[:, sl].astype(F32), ta, 8, inverse=True).astype(BF16))
            dproj_ref[:, 1024 + j * LANES:1024 + (j + 1) * LANES] = (
                _rope(dka_ref[:, sl].astype(F32), ta, 8, inverse=True).astype(BF16))
        dproj_ref[:, 2048:3072] = dva_ref[...]
        dproj_ref[:, 3072:4096] = dga_ref[...]

        lane = lax.broadcasted_iota(jnp.int32, (1, LANES), 1)
        low = lane < 64
        rope_lanes = (lane >= 64) & (lane < 96)
        dkr = jnp.zeros((tm, LANES), F32)
        for h in range(MLA_HEADS):
            sl = slice(h * LANES, (h + 1) * LANES)
            dqf_ref[:, sl] = _rope(dqb_ref[:, sl].astype(F32), tb, 16, inverse=True).astype(BF16)
            dk_h = dkb_ref[:, sl]
            dkv_ref[:, sl] = jnp.where(low, dk_h, dvb_ref[:, sl])
            dkr = dkr + jnp.where(rope_lanes, dk_h.astype(F32), 0.0)
        dkr = _rope(dkr, tb, 16, inverse=True)

        cq_hat, r_q = _rms_hat(bs_ref[:, 0:MLA_Q_RANK], MLA_Q_RANK)
        dcqn = _dot(dqf_ref[...], wuq_ref[...])
        dgcq_ref[...] += _colsum(dcqn * cq_hat)
        dproj_ref[:, COL_CQ:COL_CQ + 256] = _rms_bwd(dcqn * gcq_ref[...], cq_hat, r_q, MLA_Q_RANK).astype(BF16)
        ckv_hat, r_kv = _rms_hat(bs_ref[:, MLA_Q_RANK:MLA_Q_RANK + MLA_KV_RANK], MLA_KV_RANK)
        dckvn = _dot_nt(dkv_ref[...], wukv_ref[...])
        dgckv_ref[...] += _colsum(dckvn * ckv_hat)
        dproj_ref[:, COL_CQ + 256:COL_CQ + 384] = (
            _rms_bwd(dckvn * gckv_ref[...], ckv_hat, r_kv, MLA_KV_RANK).astype(BF16))
        dproj_ref[:, COL_CQ + 384:COL_CQ + 512] = dkr.astype(BF16)
        dproj_ref[:, COL_BG:COL_BG + 512] = dgb_ref[...]
        dproj_ref[:, COL_MQ:COL_MQ + 512] = dqm_ref[...]
        dproj_ref[:, COL_MG:COL_MG + 512] = dgm_ref[...]

    def col(width, idx):
        return pl.BlockSpec((tm, width), lambda i: (i, idx))

    def full(shape):
        return pl.BlockSpec(shape, lambda i: (0, 0))

    return pl.pallas_call(
        body, name="prep_bwd", grid=(t // tm,),
        out_shape=(jax.ShapeDtypeStruct((t, PROJ_W), BF16), jax.ShapeDtypeStruct((t, 1024), BF16),
                   jax.ShapeDtypeStruct((t, 1024), BF16),
                   jax.ShapeDtypeStruct((1, MLA_Q_RANK), F32), jax.ShapeDtypeStruct((1, MLA_KV_RANK), F32)),
        in_specs=[col(1024, 0)] * 6 + [col(512, 0), col(1024, 0), col(512, 0), col(512, 0),
                  col(512, COL_CQ // 512), pl.BlockSpec((tm, 4 * LANES), lambda i: (i, 0)),
                  full((1024, MLA_Q_RANK)), full((MLA_KV_RANK, 1024)),
                  full((1, MLA_Q_RANK)), full((1, MLA_KV_RANK)), full((8, LANES)), full((8, LANES))],
        out_specs=(col(PROJ_W, 0), col(1024, 0), col(1024, 0), full((1, MLA_Q_RANK)), full((1, MLA_KV_RANK))),
        compiler_params=_params(("arbitrary",)),
    )(dqa, dka, dva, dqb, dkb, dvb, dqm, dga, dgb, dgm, proj, trig, w_uq, w_ukv, g_cq, g_ckv, rope_a, rope_b)


def _adamw_math(gv, w, m, v):
    m_new = ADAM_B1 * m + (1.0 - ADAM_B1) * gv
    v_new = ADAM_B2 * v + (1.0 - ADAM_B2) * (gv * gv)
    m_hat = m_new / (1.0 - ADAM_B1 ** ADAM_STEP)
    v_hat = v_new / (1.0 - ADAM_B2 ** ADAM_STEP)
    return -ADAM_LR * (m_hat / (jnp.sqrt(v_hat) + ADAM_EPS) + ADAM_WD * w), m_new, v_new


def _adamw(g, w, m, v, tr, name):
    r, cols = w.shape

    def body(g_ref, w_ref, m_ref, v_ref, go_ref, d_ref, nm_ref, nv_ref):
        gv = g_ref[...]
        go_ref[...] = gv
        d_ref[...], nm_ref[...], nv_ref[...] = _adamw_math(gv, w_ref[...], m_ref[...], v_ref[...])

    tile = pl.BlockSpec((tr, cols), lambda i: (i, 0))
    shape = jax.ShapeDtypeStruct((r, cols), F32)
    return pl.pallas_call(
        body, name=name, grid=(r // tr,),
        out_shape=(shape,) * 4, in_specs=[tile] * 4, out_specs=(tile,) * 4,
        compiler_params=_params(("parallel",)),
    )(g, w, m, v)


def _adamw_pieces(g, w, m, v, pieces, name):
    n = len(pieces)
    per_piece = isinstance(w, (list, tuple))
    shapes = [jax.ShapeDtypeStruct((r1 - r0, c1 - c0), F32) for r0, r1, c0, c1 in pieces]
    args = (g, *w, *m, *v) if per_piece else (g, w, m, v)

    def body(g_ref, *refs):
        ins, outs = refs[:len(args) - 1], refs[len(args) - 1:]
        gv = g_ref[...]
        if not per_piece:
            results = (gv,) + _adamw_math(gv, ins[0][...], ins[1][...], ins[2][...])
        for p, (r0, r1, c0, c1) in enumerate(pieces):
            if per_piece:
                gp = gv[r0:r1, c0:c1]
                vals = (gp,) + _adamw_math(gp, ins[p][...], ins[n + p][...], ins[2 * n + p][...])
            else:
                vals = [full[r0:r1, c0:c1] for full in results]
            for kind, val in enumerate(vals):
                outs[kind * n + p][...] = val

    flat = pl.pallas_call(
        body, name=name, out_shape=tuple(shapes) * 4,
        in_specs=[IN_VMEM] * len(args), out_specs=tuple([IN_VMEM] * (4 * n)),
        compiler_params=_params(None),
    )(*args)
    return [[flat[kind * n + p] for kind in range(4)] for p in range(n)]


def _core_sum(g, recv, core, rows, tr, name, ride=None):
    cols = g.shape[2]
    nblk = rows // tr
    n_in = len(ride.args) if ride else 0
    n_out = len(ride.out_shapes) if ride else 0

    def body(c_ref, g_ref, r_ref, *rest):
        sf_ref, sb_ref = rest[n_in], rest[n_in + 1]
        if ride:
            j, i = pl.program_id(0), pl.program_id(1)
            ride.run(j * nblk + i, 4 * nblk, rest[:n_in], rest[n_in + 2:n_in + 2 + n_out],
                     rest[n_in + 2 + n_out:])
        tot = g_ref[...] + r_ref[...]
        sf_ref[...] = tot
        sb_ref[...] = tot.astype(BF16)

    half = pl.BlockSpec((None, tr, cols), lambda j, i, c_ref: (j, i, 0))
    shapes = (jax.ShapeDtypeStruct((4, rows, cols), F32), jax.ShapeDtypeStruct((4, rows, cols), BF16))
    return pl.pallas_call(
        body, name=name,
        grid_spec=pltpu.PrefetchScalarGridSpec(
            num_scalar_prefetch=1, grid=(4, nblk),
            in_specs=[pl.BlockSpec((None, tr, cols), lambda j, i, c_ref: (j, c_ref[0] * nblk + i, 0)), half]
            + (ride.in_specs if ride else []),
            out_specs=(half, half) + (ANY,) * n_out,
            scratch_shapes=ride.scratch() if ride else []),
        out_shape=shapes + tuple(ride.out_shapes if ride else ()),
        compiler_params=_params(("arbitrary", "arbitrary") if ride else ("parallel", "parallel")),
    )(core, g, recv, *(ride.args if ride else ()))


def _half_to_sibling(g4):
    def plan(in_refs, out_refs, send_sems, recv_sems):
        x, y, c = _position()
        cp = pltpu.make_async_remote_copy(
            src_ref=in_refs[0].at[:, 1 - c], dst_ref=out_refs[0], send_sem=send_sems.at[0],
            recv_sem=recv_sems.at[0], device_id=(x, y, 1 - c), device_id_type=MESH)

        def finish():
            cp.wait_recv()
            cp.wait_send()

        return cp.start, finish

    return _Ride([g4], [jax.ShapeDtypeStruct((4, g4.shape[2], 1024), F32)], (1, 1), plan)


def _gather_plan(src_ref, dst_ref, send_sems, recv_sems, local_sems):
    x, y, c = _position()
    me = 2 * x + y
    rows = src_ref.shape[1]
    cut = -(-rows // 32) * 16
    pieces = (pl.ds(0, cut), pl.ds(cut, rows - cut))
    local = pltpu.make_async_copy(src_ref, dst_ref.at[me], local_sems.at[0])

    def over_ici(sem, k, chip, t, src=None):
        where = dst_ref.at[chip, c, pieces[t]]
        return pltpu.make_async_remote_copy(
            src_ref=where if src is None else src, dst_ref=where, send_sem=send_sems.at[sem],
            recv_sem=recv_sems.at[sem], device_id=(x ^ (k >> 1), y ^ (k & 1), c), device_id_type=MESH)

    def mine_to(k, t):
        return over_ici(2 * (k - 1) + t, k, me, t, src=src_ref.at[c, pieces[t]])

    def from_neighbour(k, t):
        return over_ici(2 * (k - 1) + t, k, me ^ k, t)

    def to_sibling(k, half):
        piece = dst_ref.at[me ^ k, half]
        return pltpu.make_async_remote_copy(
            src_ref=piece, dst_ref=piece, send_sem=send_sems.at[5 + k], recv_sem=recv_sems.at[5 + k],
            device_id=(x, y, 1 - c), device_id_type=MESH)

    sends = [mine_to(2, 0), mine_to(1, 1), mine_to(2, 1), mine_to(1, 0)]
    onward = [over_ici(4, 1, me ^ 2, 0), over_ici(5, 2, me ^ 1, 1)]

    def start():
        local.start()
        for cp in sends:
            cp.start()

    def pass_on():
        from_neighbour(2, 0).wait_recv()
        onward[0].start()
        from_neighbour(1, 1).wait_recv()
        onward[1].start()

    def to_other_core():
        from_neighbour(2, 1).wait_recv()
        to_sibling(2, c).start()
        from_neighbour(1, 0).wait_recv()
        to_sibling(1, c).start()
        over_ici(4, 1, me ^ 3, 0).wait_recv()
        over_ici(5, 2, me ^ 3, 1).wait_recv()
        to_sibling(3, c).start()

    def finish():
        for k in (1, 2, 3):
            to_sibling(k, 1 - c).wait_recv()
        for cp in sends + onward + [to_sibling(k, c) for k in (1, 2, 3)]:
            cp.wait_send()
        local.wait()

    return start, pass_on, to_other_core, finish


def _gather_ride(shard, spread):
    def plan(in_refs, out_refs, send_sems, recv_sems, local_sems):
        return _gather_plan(in_refs[0], out_refs[0], send_sems, recv_sems, local_sems)

    return _Ride([shard], [jax.ShapeDtypeStruct((4,) + shard.shape, shard.dtype)], (9, 9, 1), plan,
                 in_specs=[IN_VMEM], spread=spread)


def _chip_sum(sf, recv, chip, rows, tr, name):
    cols = sf.shape[2]
    n_recv = recv.shape[0]

    def body(me_ref, sf_ref, r_ref, out_ref):
        acc = sf_ref[...]
        for k in range(n_recv):
            acc = acc + r_ref[k].astype(F32)
        out_ref[...] = acc

    return pl.pallas_call(
        body, name=name,
        grid_spec=pltpu.PrefetchScalarGridSpec(
            num_scalar_prefetch=1, grid=(rows // tr,),
            in_specs=[pl.BlockSpec((None, tr, cols), lambda i, me_ref: (me_ref[0], i, 0)),
                      pl.BlockSpec((n_recv, tr, cols), lambda i, me_ref: (0, i, 0))],
            out_specs=pl.BlockSpec((tr, cols), lambda i, me_ref: (i, 0))),
        out_shape=jax.ShapeDtypeStruct((rows, cols), F32),
        compiler_params=_params(("parallel",)),
    )(chip, sf, recv)


def _position():
    return lax.axis_index("x"), lax.axis_index("y"), lax.axis_index("c")


def _dh_scatter(dproj, w_in_arr_t, x, dz, g, sb_in, sb_rest, tm=512, tk=3072):
    t, d = x.shape
    nk = dproj.shape[1] // tk
    ni = t // tm
    total = ni * nk
    halves = (HALF_IN, HALF_REST)
    cuts = tuple(-(-rows // 32) * 16 for rows in halves)

    def rows_of(a, p):
        return cuts[a] if p == 0 else halves[a] - cuts[a]

    def piece(a, p):
        return pl.ds(0, cuts[a]) if p == 0 else pl.ds(cuts[a], halves[a] - cuts[a])

    def body(dp_ref, w_ref, x_ref, dz_ref, g_ref, sbin_ref, sbrest_ref, dx_ref, dg_ref, db_ref, rin_ref, rrest_ref,
             acc_ref, pay_in0, pay_in1, pay_rest0, pay_rest1, own_in0, own_in1, own_rest0, own_rest1,
             send_sems, recv_sems, local_sems):
        step = pl.program_id(0) * nk + pl.program_id(1)
        kk = pl.program_id(1)
        px, py, pc = _position()
        me = 2 * px + py
        srcs = (sbin_ref, sbrest_ref)
        dsts = (rin_ref, rrest_ref)
        pays = ((pay_in0, pay_in1), (pay_rest0, pay_rest1))
        owns = ((own_in0, own_in1), (own_rest0, own_rest1))
        via = (2, 1)
        onto = (1, 2)

        def peer(k):
            return (px ^ (k >> 1), py ^ (k & 1), pc)

        def payload(a, p):
            return pltpu.make_async_remote_copy(
                src_ref=srcs[a].at[me ^ 3, piece(a, p)], dst_ref=pays[a][p], send_sem=send_sems.at[2 * a + p],
                recv_sem=recv_sems.at[2 * a + p], device_id=peer(via[p]), device_id_type=MESH)

        def direct(a, k, p, src):
            sem = 4 + 4 * a + 2 * (k - 1) + p
            return pltpu.make_async_remote_copy(
                src_ref=src, dst_ref=dsts[a].at[k - 1, piece(a, p)], send_sem=send_sems.at[sem],
                recv_sem=recv_sems.at[sem], device_id=peer(k), device_id_type=MESH)

        def plain(a, k, p):
            return direct(a, k, p, srcs[a].at[me ^ k, piece(a, p)])

        def stage(a, p):
            return pltpu.make_async_copy(srcs[a].at[me ^ onto[p], piece(a, p)], owns[a][p], local_sems.at[2 * a + p])

        @pl.when(step == 0)
        def _():
            dg_ref[...] = jnp.zeros_like(dg_ref)
            db_ref[...] = jnp.zeros_like(db_ref)
            for a in range(2):
                for p in range(2):
                    payload(a, p).start()
                    stage(a, p).start()
                plain(a, 1, 1).start()
                plain(a, 2, 0).start()

        @pl.when(step == (5 * total) // 8)
        def _():
            for a in range(2):
                for p in range(2):
                    payload(a, p).wait_recv()
                    stage(a, p).wait()
                    owns[a][p][...] = (owns[a][p][...].astype(F32) + pays[a][p][...].astype(F32)).astype(BF16)
                    direct(a, onto[p], p, owns[a][p]).start()

        part = _dot(dp_ref[...], w_ref[...])

        @pl.when(kk == 0)
        def _():
            acc_ref[...] = part

        @pl.when(kk > 0)
        def _():
            acc_ref[...] += part

        @pl.when(kk == nk - 1)
        def _():
            xh, rstd = _ln_hat(x_ref[...])
            dht = acc_ref[...] + DEEPNORM_ALPHA * dz_ref[...]
            dg_ref[...] += _colsum(dht * xh)
            db_ref[...] += _colsum(dht)
            dx_ref[...] = _ln_bwd_rows(dht * g_ref[...], xh, rstd)

        @pl.when(step == total - 1)
        def _():
            for a in range(2):
                for k in (1, 2):
                    for p in range(2):
                        plain(a, k, p).wait_recv()
            for a in range(2):
                for p in range(2):
                    payload(a, p).wait_send()
                    direct(a, onto[p], p, owns[a][p]).wait_send()
                plain(a, 1, 1).wait_send()
                plain(a, 2, 0).wait_send()

    tile = pl.BlockSpec((tm, d), lambda i, kk: (i, 0))
    row = pl.BlockSpec((1, d), lambda i, kk: (0, 0))
    pieces = [pltpu.VMEM((rows_of(a, p), 1024), BF16) for a in range(2) for p in range(2)]
    return pl.pallas_call(
        body, name="dh_scatter", grid=(ni, nk),
        out_shape=(jax.ShapeDtypeStruct((t, d), F32), jax.ShapeDtypeStruct((1, d), F32),
                   jax.ShapeDtypeStruct((1, d), F32),
                   jax.ShapeDtypeStruct((2, HALF_IN, 1024), BF16),
                   jax.ShapeDtypeStruct((2, HALF_REST, 1024), BF16)),
        in_specs=[pl.BlockSpec((tm, tk), lambda i, kk: (i, kk)), pl.BlockSpec((tk, d), lambda i, kk: (kk, 0)),
                  tile, tile, row, ANY, ANY],
        out_specs=(tile, row, row, ANY, ANY),
        scratch_shapes=[pltpu.VMEM((tm, d), F32)] + pieces + pieces
        + [pltpu.SemaphoreType.DMA((12,)), pltpu.SemaphoreType.DMA((12,)), pltpu.SemaphoreType.DMA((4,))],
        compiler_params=_params(("arbitrary", "arbitrary")),
    )(dproj, w_in_arr_t, x, dz, g, sb_in, sb_rest)


def _join_halves(gh_in, gh_rest):
    def body(hin_ref, hrest_ref, oin_ref, orest_ref, send_sems, recv_sems, local_sems):
        x, y, c = _position()
        srcs = (hin_ref, hrest_ref)
        dsts = (oin_ref, orest_ref)

        def rows(a, half):
            return dsts[a].at[half]

        local = [pltpu.make_async_copy(srcs[a], rows(a, c), local_sems.at[a]) for a in range(2)]
        remote = [pltpu.make_async_remote_copy(
            src_ref=srcs[a], dst_ref=rows(a, c), send_sem=send_sems.at[a], recv_sem=recv_sems.at[a],
            device_id=(x, y, 1 - c), device_id_type=MESH) for a in range(2)]
        for cp in local + remote:
            cp.start()
        for a in range(2):
            pltpu.make_async_remote_copy(
                src_ref=srcs[a], dst_ref=rows(a, 1 - c), send_sem=send_sems.at[a], recv_sem=recv_sems.at[a],
                device_id=(x, y, 1 - c), device_id_type=MESH).wait_recv()
        for cp in remote:
            cp.wait_send()
        for cp in local:
            cp.wait()

    return pl.pallas_call(
        body, name="join_halves",
        out_shape=(jax.ShapeDtypeStruct((2, HALF_IN, 1024), F32),
                   jax.ShapeDtypeStruct((2, HALF_REST, 1024), F32)),
        in_specs=[IN_VMEM, IN_VMEM], out_specs=(ANY, ANY),
        scratch_shapes=[pltpu.SemaphoreType.DMA((2,)), pltpu.SemaphoreType.DMA((2,)), pltpu.SemaphoreType.DMA((2,))],
    )(gh_in, gh_rest)


def _allreduce_small(vec):
    def body(vec_ref, out_ref, all_ref, send_sems, recv_sems):
        x, y, c = _position()
        me = 4 * x + 2 * y + c
        all_ref[me] = vec_ref[...]

        def copy(k, slot):
            return pltpu.make_async_remote_copy(
                src_ref=vec_ref, dst_ref=all_ref.at[slot], send_sem=send_sems.at[k - 1], recv_sem=recv_sems.at[k - 1],
                device_id=(x ^ (k >> 2), y ^ ((k >> 1) & 1), c ^ (k & 1)), device_id_type=MESH)

        copies = [copy(k, me) for k in range(1, 8)]
        for cp in copies:
            cp.start()
        for k in range(1, 8):
            copy(k, me ^ k).wait_recv()
        for cp in copies:
            cp.wait_send()
        total = all_ref[0]
        for d in range(1, 8):
            total = total + all_ref[d]
        out_ref[...] = total

    return pl.pallas_call(
        body, name="allreduce_small",
        out_shape=jax.ShapeDtypeStruct(vec.shape, vec.dtype),
        in_specs=[pl.BlockSpec(memory_space=pltpu.VMEM)], out_specs=pl.BlockSpec(memory_space=pltpu.VMEM),
        scratch_shapes=[pltpu.VMEM((8,) + vec.shape, vec.dtype), pltpu.SemaphoreType.DMA((7,)),
                        pltpu.SemaphoreType.DMA((7,))],
    )(vec)


def _pack_rest(w_uq, w_ukv, w_mem, w_out):
    rows = jnp.concatenate([w_uq[0].T.reshape(-1, 1024), w_ukv.reshape(-1, 1024), w_mem.reshape(-1, 1024),
                            w_out.reshape(-1, 1024)], axis=0)
    return jnp.pad(rows, ((0, ROWS_REST - ROWS_USED), (0, 0)))


def _arranged_w_in(g_in):
    z = functools.partial(jnp.zeros, dtype=g_in.dtype)
    cut = 4480 - 2 * SHARD_ROWS
    return jnp.concatenate(
        [g_in[0, :SHARD_ROWS], g_in[1, :SHARD_ROWS], g_in[2, :cut], z((64, 1024)), g_in[2, cut:cut + 32],
         z((32, 1024)), g_in[2, cut + 32:SHARD_ROWS], g_in[3, :SHARD_ROWS]], axis=0)


def _rest_weights(g_rest):
    w_uq_t = g_rest[:, 0:ROWS_UQ].reshape(768, 256)
    w_uq_pad_t = jnp.pad(w_uq_t.reshape(MLA_HEADS, MLA_QK_DIM, 256), ((0, 0), (0, 32), (0, 0))).reshape(1024, 256)
    w_ukv = jnp.concatenate([g_rest[j, ROWS_UQ:ROWS_UQ + ROWS_UKV].reshape(128, 256) for j in range(4)], axis=1)
    lo = ROWS_UQ + ROWS_UKV
    w_mem = g_rest[:, lo:lo + ROWS_MEM].reshape(4 * ROWS_MEM, 1024)
    w_out = g_rest[:, lo + ROWS_MEM:lo + ROWS_MEM + ROWS_OUT].reshape(4 * ROWS_OUT, 1024)
    return w_uq_pad_t, w_ukv, w_mem, w_out


def _split_in(dw_in_arr_t):
    a = dw_in_arr_t
    gap = jnp.zeros((ROWS_IN - SHARD_ROWS, 1024), a.dtype)
    nat = 4608 - 96
    pieces = [a[:SHARD_ROWS], gap, a[SHARD_ROWS:2 * SHARD_ROWS], gap,
              a[2 * SHARD_ROWS:4480], a[4544:4576], a[4608:4608 + 3 * SHARD_ROWS - nat], gap,
              a[4608 + 3 * SHARD_ROWS - nat:], gap]
    return jnp.concatenate(pieces, axis=0).reshape(4, ROWS_IN, 1024)


def _split_rest(dw_uq_pad_t, dw_ukv, dw_mem, dw_out):
    dw_uq_t = dw_uq_pad_t.reshape(MLA_HEADS, LANES, 256)[:, :MLA_QK_DIM].reshape(4, ROWS_UQ, 1024)
    parts = [dw_uq_t, dw_ukv.reshape(128, 4, 256).transpose(1, 0, 2).reshape(4, ROWS_UKV, 1024),
             dw_mem.reshape(4, ROWS_MEM, 1024), dw_out.reshape(4, ROWS_OUT, 1024)]
    return jnp.pad(jnp.concatenate(parts, axis=1), ((0, 0), (0, ROWS_REST - ROWS_USED), (0, 0)))


def _rope_consts(rot, first, period):
    half = rot // 2
    inv_freq = np.float32(ROPE_THETA) ** (-(np.arange(0, rot, 2, dtype=np.float32) / np.float32(rot)))
    lane = np.arange(LANES) % period - first
    in_rot = (lane >= 0) & (lane < rot)
    out = np.zeros((8, LANES), np.float32)
    out[0] = np.where(in_rot, inv_freq[np.clip(lane, 0, rot - 1) % half], 0.0)
    out[1] = in_rot & (lane < half)
    out[2] = in_rot & (lane >= half)
    return jnp.asarray(out)


def _band_bias(s):
    nblk = s // BAND_Q
    starts = np.array([_band_start(i, s) for i in range(nblk)])
    uq = (np.arange(nblk)[:, None] * BAND_Q + np.arange(BAND_Q)[None, :])[:, :, None]
    uk = (starts[:, None] + np.arange(BAND_WIN)[None, :])[:, None, :]
    tiles, index, seen = [], [], {}
    for _, d in DILATED:
        length = s // d
        ok = (uq // length == uk // length) & (np.abs(uq - uk) <= 64)
        row = []
        for i in range(nblk):
            key = ok[i].tobytes()
            if key not in seen:
                seen[key] = len(tiles)
                tiles.append(np.where(ok[i], 0.0, NEG_INF).astype(np.float32))
            row.append(seen[key])
        index.append(row)
    return jnp.asarray(np.stack(tiles, axis=0)), index


def _forward_backward(h, h32, proj, trig, rope_consts, x, mem, target, weights, gains):
    w_uq_pad_t, w_ukv, w_mem, w_out = weights
    g_emb, b_emb, g_cq, g_ckv, g_out_a, g_out_b, g_out_m, g_post, b_post = gains
    nb, s, d = x.shape
    t = nb * s
    x2 = x.reshape(t, d)
    mem2 = mem.reshape(nb * N_MEM, d)
    tgt2 = target.reshape(t, d)
    rope_a, rope_b = rope_consts
    bias, bias_index = _band_bias(s)
    scales = (0.125, MLA_QK_DIM ** -0.5, 128 ** -0.5)

    qa, ka, va, qb, kb, vb, qm, cqn, ckvn = _prep(proj, trig, w_uq_pad_t, w_ukv, g_cq, g_ckv, rope_a, rope_b, scales)
    mkv = _mm(mem2, w_mem, BF16, nb * N_MEM, 1024, 1024, "mem_kv")

    cfg_b = dict(nb=nb, s=s, sk=s, heads=8, voff=0, bq=256)
    cfg_m = dict(nb=nb, s=s, sk=N_MEM, heads=4, hpb=2, voff=4, bq=1024)
    ya, lse_a, qkv_ordered = _dilated_fwd(qa, ka, va, bias, bias_index, nb=nb, s=s, name="attn_a_fwd")
    yb, lse_b = _attn_fwd(qb, kb, vb, name="attn_b_fwd", hpb=4, **cfg_b)
    ym, lse_m = _attn_fwd(qm, mkv, mkv, name="attn_m_fwd", **cfg_m)

    (y, dz, doa, dob, dom, dga, dgb, dgm, loss, dg_post, db_post, dg_a, dg_b, dg_m) = _post(
        h32, ya, yb, ym, proj, tgt2, w_out, g_out_a, g_out_b, g_out_m, g_post, b_post)

    dqa, dka, dva = _dilated_bwd(qa, ka, va, qkv_ordered, ya, doa, lse_a, bias, bias_index, nb=nb, s=s, scale=scales[0],
                                 name="attn_a_bwd")
    dqb, dkb, dvb = _attn_bwd(qb, kb, vb, yb, dob, lse_b, name="attn_b_bwd", scale=scales[1], hpb=4, **cfg_b)
    dqm, dmk, dmv = _attn_bwd(qm, mkv, mkv, ym, dom, lse_m, name="attn_m_bwd", scale=scales[2], **cfg_m)
    dmkv = jnp.concatenate([dmk, dmv], axis=1)

    dproj, dqf, dkv, dg_cq, dg_ckv = _prep_bwd(
        dqa, dka, dva, dqb, dkb, dvb, dqm, dga, dgb, dgm, proj, trig, w_uq_pad_t, w_ukv, g_cq, g_ckv, rope_a, rope_b)

    small_rows = (dg_cq, dg_ckv, loss, dg_a, dg_b, dg_m, dg_post, db_post)
    return (dproj, h, y, dz, dqf, cqn, ckvn, dkv, mem2, dmkv), x2, small_rows


def _weight_grads(operands, core):
    dproj, h, y, dz, dqf, cqn, ckvn, dkv, mem2, dmkv = operands
    dw_in_arr_t = _mm(dproj, h, F32, 1024, 1024, 4096, "dw_in", mode="tn")
    g_in = _split_in(dw_in_arr_t)
    dw_out, r_in = _mm(y, dz, F32, 1024, 1024, 2048, "dw_out", mode="tn",
                       ride=_half_to_sibling(g_in.reshape(4, 2, HALF_IN, 1024)))
    dw_uq_pad_t = _mm(dqf, cqn, F32, 1024, 256, 4096, "dw_uq", mode="tn")
    dw_ukv = _mm(ckvn, dkv, F32, 128, 1024, 4096, "dw_ukv", mode="tn")
    dw_mem = _mm(mem2, dmkv, F32, 1024, 1024, mem2.shape[0], "dw_mem", mode="tn")
    g_rest = _split_rest(dw_uq_pad_t, dw_ukv, dw_mem, dw_out)
    sf_in, sb_in, r_rest = _core_sum(g_in, r_in, core, HALF_IN, HALF_IN // 2, "core_sum_in",
                                     ride=_half_to_sibling(g_rest.reshape(4, 2, HALF_REST, 1024)))
    sf_rest, sb_rest = _core_sum(g_rest, r_rest, core, HALF_REST, HALF_REST, "core_sum_rest")
    return sf_in, sb_in, sf_rest, sb_rest


def _small_block(dg_emb, db_emb, small_rows):
    dg_cq, dg_ckv, loss, dg_a, dg_b, dg_m, dg_post, db_post = small_rows
    row2 = jnp.concatenate([dg_cq, dg_ckv, loss, jnp.zeros((1, 512), F32)], axis=1)
    return jnp.concatenate([dg_emb, db_emb, row2, dg_a, jnp.concatenate([dg_b, dg_m], axis=1), dg_post, db_post,
                            jnp.zeros((1, 1024), F32)], axis=0)


def _pack_small(g_emb, b_emb, g_cq, g_ckv, g_out_a, g_out_b, g_out_m, g_post, b_post):
    row2 = jnp.concatenate([g_cq.reshape(1, -1), g_ckv.reshape(1, -1), jnp.zeros((1, 640), F32)], axis=1)
    return jnp.concatenate([g_emb.reshape(1, -1), b_emb.reshape(1, -1), row2, g_out_a.reshape(1, -1),
                            jnp.concatenate([g_out_b.reshape(1, -1), g_out_m.reshape(1, -1)], axis=1),
                            g_post.reshape(1, -1), b_post.reshape(1, -1), jnp.zeros((1, 1024), F32)], axis=0)


def kernel(x, mem, positions, g_emb, b_emb, w_in, g_cq, g_ckv, w_uq, w_ukv, w_mem_kv, g_out_a, g_out_b, g_out_m, w_out, g_post, b_post, loss_target, m_g_emb, m_b_emb, m_w_in, m_g_cq, m_g_ckv, m_w_uq, m_w_ukv, m_w_mem_kv, m_g_out_a, m_g_out_b, m_g_out_m, m_w_out, m_g_post, m_b_post, v_g_emb, v_b_emb, v_w_in, v_g_cq, v_g_ckv, v_w_uq, v_w_ukv, v_w_mem_kv, v_g_out_a, v_g_out_b, v_g_out_m, v_w_out, v_g_post, v_b_post):
    w_rest = _pack_rest(w_uq, w_ukv, w_mem_kv, w_out)
    w_in_t = w_in[0].T
    w_in_b = jnp.pad(w_in_t.astype(BF16), ((0, ROWS_IN - SHARD_ROWS), (0, 0)))
    gains = (g_emb.reshape(1, -1), b_emb.reshape(1, -1), g_cq, g_ckv, g_out_a, g_out_b, g_out_m, g_post, b_post)
    rope_consts = (_rope_consts(16, 0, 64), _rope_consts(32, 64, 128))
    h, h32, trig, gathered_in = _ln_fwd(x.reshape(-1, D_MODEL), gains[0], gains[1],
                                        positions.reshape(-1, 1).astype(F32), *rope_consts,
                                        ride=_gather_ride(w_in_b.reshape(2, HALF_IN, 1024), spread=False))
    w_in_arr_t = _arranged_w_in(gathered_in.reshape(4, ROWS_IN, 1024))
    proj, gathered_rest = _mm(h, w_in_arr_t, F32, 1024, 2048, 1024, "in_proj", mode="nt",
                              ride=_gather_ride(w_rest.astype(BF16).reshape(2, HALF_REST, 1024), spread=True))
    weights = _rest_weights(gathered_rest.reshape(4, ROWS_REST, 1024))
    operands, x2, small_rows = _forward_backward(h, h32, proj, trig, rope_consts, x, mem, loss_target, weights,
                                                 gains)

    core = lax.axis_index("c").astype(jnp.int32).reshape(1)
    chip = (2 * lax.axis_index("x") + lax.axis_index("y")).astype(jnp.int32).reshape(1)
    sf_in, sb_in, sf_rest, sb_rest = _weight_grads(operands, core)
    grad_x, dg_emb, db_emb, rb_in, rb_rest = _dh_scatter(operands[0], w_in_arr_t, x2, operands[3], gains[0],
                                                         sb_in, sb_rest)
    gh_in = _chip_sum(sf_in, rb_in, chip, HALF_IN, HALF_IN // 2, "chip_sum_in")
    gh_rest = _chip_sum(sf_rest, rb_rest, chip, HALF_REST, HALF_REST, "chip_sum_rest")
    grad_in, grad_rest = _join_halves(gh_in, gh_rest)
    grad_in = grad_in.reshape(ROWS_IN, 1024)
    grad_rest = grad_rest.reshape(ROWS_REST, 1024)

    big_in = _adamw(grad_in, w_in_t, m_w_in[0].T, v_w_in[0].T, SHARD_ROWS // 3, "adamw_in")
    def rest_parts(a_uq, a_ukv, a_mem, a_out):
        return [a_uq[0].T.reshape(ROWS_UQ, 1024), a_ukv.reshape(ROWS_UKV, 1024), a_mem[0], a_out[0]]

    uq, ukv, wmem, wout = _adamw_pieces(
        grad_rest, rest_parts(w_uq, w_ukv, w_mem_kv, w_out), rest_parts(m_w_uq, m_w_ukv, m_w_mem_kv, m_w_out),
        rest_parts(v_w_uq, v_w_ukv, v_w_mem_kv, v_w_out), REST_PIECES, "adamw_rest")
    small_sum = _allreduce_small(_small_block(dg_emb, db_emb, small_rows))
    sm = _adamw_pieces(
        small_sum,
        _pack_small(g_emb, b_emb, g_cq, g_ckv, g_out_a, g_out_b, g_out_m, g_post, b_post),
        _pack_small(m_g_emb, m_b_emb, m_g_cq, m_g_ckv, m_g_out_a, m_g_out_b, m_g_out_m, m_g_post, m_b_post),
        _pack_small(v_g_emb, v_b_emb, v_g_cq, v_g_ckv, v_g_out_a, v_g_out_b, v_g_out_m, v_g_post, v_b_post),
        SMALL_PIECES, "adamw_small")
    loss = small_sum[2, 384]

    def ordered(kind):
        s_gemb, s_bemb, s_gcq, s_gckv, s_ga, s_gb, s_gm, s_gpost, s_bpost = [piece[kind] for piece in sm]
        return [s_gemb.reshape(-1), s_bemb.reshape(-1), big_in[kind].T[None], s_gcq, s_gckv,
                uq[kind].reshape(192, 256).T[None], ukv[kind].reshape(1, 128, 256), wmem[kind][None], s_ga, s_gb,
                s_gm, wout[kind][None], s_gpost, s_bpost]

    return (loss, grad_x.reshape(x.shape), *ordered(0), *ordered(1), *ordered(2), *ordered(3))
```

```python
import functools
import math

import jax
import jax.numpy as jnp
import numpy as np
from jax import lax
from jax.experimental import pallas as pl
from jax.experimental.pallas import tpu as pltpu

F32 = jnp.float32
BF16 = jnp.bfloat16
MESH = pl.DeviceIdType.MESH
ANY = pl.BlockSpec(memory_space=pl.ANY)
IN_VMEM = pl.BlockSpec(memory_space=pltpu.VMEM)

D_MODEL = 1024
A_WIDTH = 1024
MLA_HEADS = 8
MLA_Q_RANK = 256
MLA_KV_RANK = 128
MLA_QK_DIM = 96
MEM_WIDTH = 512
N_MEM = 256
ROPE_THETA = 500000.0
NORM_EPS = 1e-5
NEG_INF = -1e30
DEEPNORM_ALPHA = 2.0 ** 0.25
DILATED = ((64, 1), (256, 4), (1024, 16))

ADAM_LR = 0.001
ADAM_B1 = 0.9
ADAM_B2 = 0.999
ADAM_EPS = 1e-08
ADAM_WD = 0.01
ADAM_STEP = 10

LANES = 128
VMEM_LIMIT = 56 * 1024 * 1024
LOG2E = math.log2(math.e)
LN2 = math.log(2.0)

PROJ_W = 6144
COL_CQ = 4096
COL_BG = 4608
COL_MQ = 5120
COL_MG = 5632

SHARD_ROWS = 1512
ROWS_IN = 1536
ROWS_UQ, ROWS_UKV, ROWS_MEM, ROWS_OUT = 48, 32, 256, 512
ROWS_USED = ROWS_UQ + ROWS_UKV + ROWS_MEM + ROWS_OUT
ROWS_REST = 864
HALF_IN = ROWS_IN // 2
HALF_REST = ROWS_REST // 2
REST_PIECES = ((0, 48, 0, 1024), (48, 80, 0, 1024), (80, 336, 0, 1024), (336, 848, 0, 1024))
SMALL_PIECES = ((0, 1, 0, 1024), (1, 2, 0, 1024), (2, 3, 0, 256), (2, 3, 256, 384), (3, 4, 0, 1024), (4, 5, 0, 512),
                (4, 5, 512, 1024), (5, 6, 0, 1024), (6, 7, 0, 1024))


def _params(sem=None, vmem=VMEM_LIMIT):
    return pltpu.CompilerParams(dimension_semantics=sem, vmem_limit_bytes=vmem)


def _dot(a, b):
    return jnp.dot(a, b, preferred_element_type=F32)


def _dot_nt(a, b):
    return lax.dot_general(a, b, (((1,), (1,)), ((), ())), preferred_element_type=F32)


def _dot_tn(a, b):
    return lax.dot_general(a, b, (((0,), (0,)), ((), ())), preferred_element_type=F32)


def _ln_hat(x):
    mu = jnp.mean(x, axis=-1, keepdims=True)
    xc = x - mu
    var = jnp.mean(xc * xc, axis=-1, keepdims=True)
    rstd = lax.rsqrt(var + NORM_EPS)
    return xc * rstd, rstd


def _ln_bwd_rows(dxh, xh, rstd):
    return rstd * (dxh - jnp.mean(dxh, axis=-1, keepdims=True) - xh * jnp.mean(dxh * xh, axis=-1, keepdims=True))


def _rms_hat(x, width):
    ms = jnp.sum(x * x, axis=-1, keepdims=True) * (1.0 / width)
    r = lax.rsqrt(ms + NORM_EPS)
    return x * r, r


def _rms_bwd(u, xh, r, width):
    return r * (u - xh * (jnp.sum(u * xh, axis=-1, keepdims=True) * (1.0 / width)))


def _colsum(v):
    return jnp.sum(v, axis=0, keepdims=True)


def _rope_tables(cos, sin, consts):
    return cos, sin * consts[2:3, :], -sin * consts[1:2, :]


def _rope(x, tables, half, inverse=False):
    c, s_up, s_dn = tables
    if inverse:
        s_up, s_dn = -s_up, -s_dn
    return x * c + pltpu.roll(x, half, 1) * s_up + pltpu.roll(x, LANES - half, 1) * s_dn


def _ln_fwd(x, g, b, pos, rope_a, rope_b, tm=512, ride=None):
    t, d = x.shape
    n_in = len(ride.args) if ride else 0
    n_out = len(ride.out_shapes) if ride else 0
    steps = t // tm

    def body(x_ref, g_ref, b_ref, pos_ref, ra_ref, rb_ref, *rest):
        h_ref, h32_ref, trig_ref = rest[n_in:n_in + 3]
        if ride:
            i = pl.program_id(0)
            ride.run(i, steps, rest[:n_in], rest[n_in + 3:n_in + 3 + n_out], rest[n_in + 3 + n_out:])
        xh, _ = _ln_hat(x_ref[...])
        h = xh * g_ref[...] + b_ref[...]
        h32_ref[...] = h
        h_ref[...] = h.astype(BF16)
        for j, consts in enumerate((ra_ref, rb_ref)):
            ang = pos_ref[...] * consts[0:1, :]
            trig_ref[:, 2 * j * LANES:(2 * j + 1) * LANES] = jnp.cos(ang)
            trig_ref[:, (2 * j + 1) * LANES:(2 * j + 2) * LANES] = jnp.sin(ang)

    row = pl.BlockSpec((1, d), lambda i: (0, 0))
    tile = pl.BlockSpec((tm, d), lambda i: (i, 0))
    consts = pl.BlockSpec((8, LANES), lambda i: (0, 0))
    trig_tile = pl.BlockSpec((tm, 4 * LANES), lambda i: (i, 0))
    in_specs = [tile, row, row, pl.BlockSpec((tm, 1), lambda i: (i, 0)), consts, consts]
    shapes = (jax.ShapeDtypeStruct((t, d), BF16), jax.ShapeDtypeStruct((t, d), F32),
              jax.ShapeDtypeStruct((t, 4 * LANES), F32))
    if not ride:
        return pl.pallas_call(
            body, name="ln_fwd", grid=(steps,), out_shape=shapes, in_specs=in_specs,
            out_specs=(tile, tile, trig_tile), compiler_params=_params(("parallel",)),
        )(x, g, b, pos, rope_a, rope_b)
    return pl.pallas_call(
        body, name="ln_fwd", grid=(steps,),
        out_shape=(*shapes, *ride.out_shapes),
        in_specs=in_specs + ride.in_specs, out_specs=(tile, tile, trig_tile) + (ANY,) * n_out,
        scratch_shapes=ride.scratch(),
        compiler_params=_params(("arbitrary",)),
    )(x, g, b, pos, rope_a, rope_b, *ride.args)


class _Ride:
    def __init__(self, args, out_shapes, sem_counts, plan, in_specs=None, spread=True):
        self.args, self.out_shapes, self.plan = list(args), list(out_shapes), plan
        self.sem_counts = sem_counts
        self.in_specs = in_specs or [ANY] * len(self.args)
        self.spread = spread

    def scratch(self):
        return [pltpu.SemaphoreType.DMA((n,)) for n in self.sem_counts]

    def run(self, step, total, in_refs, out_refs, sems):
        count = len(self.plan(in_refs, out_refs, *sems))
        at = [(k * (total - 1)) // (count - 1) if self.spread or k == 0 else total - 1 for k in range(count)]
        for when in sorted(set(at)):
            @pl.when(step == when)
            def _(when=when):
                stages = self.plan(in_refs, out_refs, *sems)
                for k in range(count):
                    if at[k] == when:
                        stages[k]()


def _mm(a, b, out_dtype, tm, tn, tk, name, mode="nn", ride=None):
    if mode == "tn":
        k, m = a.shape
    else:
        m, k = a.shape
    n = b.shape[0] if mode == "nt" else b.shape[1]
    nk = k // tk
    nj, ni = n // tn, m // tm
    n_in = len(ride.args) if ride else 0
    n_out = len(ride.out_shapes) if ride else 0

    def body(a_ref, b_ref, *rest):
        o_ref = rest[n_in]
        acc_ref = rest[n_in + 1 + n_out]
        if ride:
            j, i, kk = pl.program_id(0), pl.program_id(1), pl.program_id(2)
            ride.run((j * ni + i) * nk + kk, nj * ni * nk, rest[:n_in], rest[n_in + 1:n_in + 1 + n_out],
                     rest[n_in + 2 + n_out:])
        av = a_ref[...].astype(BF16)
        bv = b_ref[...].astype(BF16)
        part = _dot_tn(av, bv) if mode == "tn" else _dot_nt(av, bv) if mode == "nt" else _dot(av, bv)
        if nk == 1:
            o_ref[...] = part.astype(out_dtype)
        else:
            kk = pl.program_id(2)

            @pl.when(kk == 0)
            def _():
                acc_ref[...] = part

            @pl.when(kk > 0)
            def _():
                acc_ref[...] += part

            @pl.when(kk == nk - 1)
            def _():
                o_ref[...] = acc_ref[...].astype(out_dtype)

    a_spec = (pl.BlockSpec((tk, tm), lambda j, i, kk: (kk, i)) if mode == "tn"
              else pl.BlockSpec((tm, tk), lambda j, i, kk: (i, kk)))
    b_spec = (pl.BlockSpec((tn, tk), lambda j, i, kk: (j, kk)) if mode == "nt"
              else pl.BlockSpec((tk, tn), lambda j, i, kk: (kk, j)))
    o_spec = pl.BlockSpec((tm, tn), lambda j, i, kk: (i, j))
    o_shape = jax.ShapeDtypeStruct((m, n), out_dtype)
    if not ride:
        return pl.pallas_call(
            body, name=name, grid=(nj, ni, nk), out_shape=o_shape, in_specs=[a_spec, b_spec], out_specs=o_spec,
            scratch_shapes=[pltpu.VMEM((tm, tn), F32)],
            compiler_params=_params(("parallel", "parallel", "arbitrary")),
        )(a, b)
    return pl.pallas_call(
        body, name=name, grid=(nj, ni, nk),
        out_shape=(o_shape, *ride.out_shapes),
        in_specs=[a_spec, b_spec] + ride.in_specs,
        out_specs=(o_spec,) + (ANY,) * n_out,
        scratch_shapes=[pltpu.VMEM((tm, tn), F32)] + ride.scratch(),
        compiler_params=_params(("arbitrary", "arbitrary", "arbitrary")),
    )(a, b, *ride.args)


def _prep(proj, trig, w_uq, w_ukv, g_cq, g_ckv, rope_a, rope_b, scales, tm=256):
    t = proj.shape[0]
    sc_a, sc_b, sc_m = (s * LOG2E for s in scales)

    def body(aq_ref, ak_ref, av_ref, bs_ref, mq_ref, trig_ref, wuq_ref, wukv_ref, gcq_ref, gckv_ref,
             ra_ref, rb_ref, qa_ref, ka_ref, va_ref, qb_ref, kb_ref, vb_ref, qm_ref, cqn_ref, ckvn_ref):
        ta = _rope_tables(trig_ref[:, 0:LANES], trig_ref[:, LANES:2 * LANES], ra_ref[...])
        tb = _rope_tables(trig_ref[:, 2 * LANES:3 * LANES], trig_ref[:, 3 * LANES:4 * LANES], rb_ref[...])
        for j in range(A_WIDTH // LANES):
            sl = slice(j * LANES, (j + 1) * LANES)
            qa_ref[:, sl] = (_rope(aq_ref[:, sl], ta, 8) * sc_a).astype(BF16)
            ka_ref[:, sl] = _rope(ak_ref[:, sl], ta, 8).astype(BF16)
        va_ref[...] = av_ref[...].astype(BF16)
        qm_ref[...] = (mq_ref[...] * sc_m).astype(BF16)

        cq_hat, _ = _rms_hat(bs_ref[:, 0:MLA_Q_RANK], MLA_Q_RANK)
        cqn = (cq_hat * gcq_ref[...]).astype(BF16)
        cqn_ref[...] = cqn
        ckv_hat, _ = _rms_hat(bs_ref[:, MLA_Q_RANK:MLA_Q_RANK + MLA_KV_RANK], MLA_KV_RANK)
        ckvn = (ckv_hat * gckv_ref[...]).astype(BF16)
        ckvn_ref[...] = ckvn
        qfull = _dot_nt(cqn, wuq_ref[...])
        kv = _dot(ckvn, wukv_ref[...])
        kr = _rope(bs_ref[:, 384:512], tb, 16)
        lane = lax.broadcasted_iota(jnp.int32, (1, LANES), 1)
        low = lane < 64
        for h in range(MLA_HEADS):
            sl = slice(h * LANES, (h + 1) * LANES)
            qb_ref[:, sl] = (_rope(qfull[:, sl], tb, 16) * sc_b).astype(BF16)
            kb_ref[:, sl] = jnp.where(low, kv[:, sl], kr).astype(BF16)
            vb_ref[:, sl] = jnp.where(low, 0.0, kv[:, sl]).astype(BF16)

    def col(width, idx):
        return pl.BlockSpec((tm, width), lambda i: (i, idx))

    def full(shape):
        return pl.BlockSpec(shape, lambda i: (0, 0))

    wide = jax.ShapeDtypeStruct((t, 1024), BF16)
    return pl.pallas_call(
        body, name="prep", grid=(t // tm,),
        out_shape=(wide, wide, wide, wide, wide, wide,
                   jax.ShapeDtypeStruct((t, MEM_WIDTH), BF16),
                   jax.ShapeDtypeStruct((t, MLA_Q_RANK), BF16),
                   jax.ShapeDtypeStruct((t, MLA_KV_RANK), BF16)),
        in_specs=[col(1024, 0), col(1024, 1), col(1024, 2), col(512, COL_CQ // 512), col(512, COL_MQ // 512),
                  pl.BlockSpec((tm, 4 * LANES), lambda i: (i, 0)),
                  full((1024, MLA_Q_RANK)), full((MLA_KV_RANK, 1024)),
                  full((1, MLA_Q_RANK)), full((1, MLA_KV_RANK)), full((8, LANES)), full((8, LANES))],
        out_specs=(col(1024, 0),) * 6 + (col(MEM_WIDTH, 0), col(MLA_Q_RANK, 0), col(MLA_KV_RANK, 0)),
        compiler_params=_params(("parallel",)),
    )(proj, proj, proj, proj, proj, trig, w_uq, w_ukv, g_cq, g_ckv, rope_a, rope_b)


def _attn_fwd(q, k, v, *, nb, s, sk, heads, hpb, voff, bq, name):
    nq = s // bq
    width = hpb * LANES
    vblk = voff // hpb

    def body(q_ref, k_ref, v_ref, o_ref, lse_ref):
        for h in range(hpb):
            sl = slice(h * LANES, (h + 1) * LANES)
            sc = _dot_nt(q_ref[:, sl], k_ref[:, sl])
            m = jnp.max(sc, axis=1, keepdims=True)
            p = jnp.exp2(sc - m)
            l = jnp.sum(p, axis=1, keepdims=True)
            o_ref[:, sl] = _dot(p.astype(BF16), v_ref[:, sl]) / l
            lse_ref[:, sl] = jnp.broadcast_to(m + jnp.log(l) * LOG2E, (bq, LANES))

    out = jax.ShapeDtypeStruct((nb * s, heads * LANES), F32)
    ospec = pl.BlockSpec((bq, width), lambda b, i, g: (b * nq + i, g))
    return pl.pallas_call(
        body, name=name, grid=(nb, nq, heads // hpb),
        out_shape=(out, out),
        in_specs=[ospec, pl.BlockSpec((sk, width), lambda b, i, g: (b, g)),
                  pl.BlockSpec((sk, width), lambda b, i, g: (b, vblk + g))],
        out_specs=(ospec, ospec),
        compiler_params=_params(("parallel", "parallel", "parallel")),
    )(q, k, v)


def _attn_bwd(q, k, v, o, do, lse, *, nb, s, sk, heads, hpb, voff, scale, bq, name):
    nq = s // bq
    width = hpb * LANES
    vblk = voff // hpb

    def body(q_ref, k_ref, v_ref, o_ref, do_ref, lse_ref, dq_ref, dk_ref, dv_ref, dk_acc, dv_acc):
        i = pl.program_id(2)

        @pl.when(i == 0)
        def _():
            dk_acc[...] = jnp.zeros_like(dk_acc)
            dv_acc[...] = jnp.zeros_like(dv_acc)

        for h in range(hpb):
            sl = slice(h * LANES, (h + 1) * LANES)
            qh = q_ref[:, sl]
            kk = k_ref[:, sl]
            doh = do_ref[:, sl]
            delta = jnp.sum(doh.astype(F32) * o_ref[:, sl], axis=1, keepdims=True)
            p = jnp.exp2(_dot_nt(qh, kk) - lse_ref[:, h * LANES:h * LANES + 1])
            ds = (p * (_dot_nt(doh, v_ref[:, sl]) - delta)).astype(BF16)
            dq_ref[:, sl] = (_dot(ds, kk) * scale).astype(BF16)
            dk_acc[:, sl] += _dot_tn(ds, qh)
            dv_acc[:, sl] += _dot_tn(p.astype(BF16), doh)

        @pl.when(i == nq - 1)
        def _():
            dk_ref[...] = (dk_acc[...] * LN2).astype(BF16)
            dv_ref[...] = dv_acc[...].astype(BF16)

    qspec = pl.BlockSpec((bq, width), lambda b, g, i: (b * nq + i, g))
    kv_spec = pl.BlockSpec((sk, width), lambda b, g, i: (b, g))
    dq_shape = jax.ShapeDtypeStruct((nb * s, heads * LANES), BF16)
    dkv_shape = jax.ShapeDtypeStruct((nb * sk, heads * LANES), BF16)
    return pl.pallas_call(
        body, name=name, grid=(nb, heads // hpb, nq),
        out_shape=(dq_shape, dkv_shape, dkv_shape),
        in_specs=[qspec, kv_spec, pl.BlockSpec((sk, width), lambda b, g, i: (b, vblk + g)), qspec, qspec, qspec],
        out_specs=(qspec, kv_spec, kv_spec),
        scratch_shapes=[pltpu.VMEM((sk, width), F32), pltpu.VMEM((sk, width), F32)],
        compiler_params=_params(("parallel", "parallel", "arbitrary")),
    )(q, k, v, o, do, lse)


BAND_Q = 128
BAND_WIN = 256


def _band_start(i, s):
    return min(max(i * BAND_Q - 64, 0), s - BAND_WIN)


def _to_pattern_order(src_ref, dst_ref, stage_ref, s, d):
    length = s // d
    stage_ref[...] = src_ref[...].astype(F32)
    for r in range(d):
        dst_ref[r * length:(r + 1) * length, :] = stage_ref[pl.ds(r, length, stride=d), :].astype(dst_ref.dtype)


def _dilated_fwd(q, k, v, bias, bias_index, *, nb, s, name):
    nblk = s // BAND_Q
    npat = len(DILATED)

    def body(q_ref, k_ref, v_ref, bias_ref, o_ref, lse_ref, *rest):
        ordered = rest[:3 * (npat - 1)]
        stage_ref, op_ref, lp_ref, on_ref, ln_ref = rest[3 * (npat - 1):]
        lane = lax.broadcasted_iota(jnp.int32, (1, LANES), 1)
        first = lane < 64
        for p, (_, d) in enumerate(DILATED):
            if d == 1:
                qs, ks, vs = q_ref, k_ref, v_ref
            else:
                qs, ks, vs = ordered[3 * (p - 1):3 * p]
                for src, dst in ((q_ref, qs), (k_ref, ks), (v_ref, vs)):
                    _to_pattern_order(src, dst, stage_ref, s, d)
            for i in range(nblk):
                u0 = i * BAND_Q
                st = _band_start(i, s)
                qi = qs[u0:u0 + BAND_Q, :]
                kw = ks[st:st + BAND_WIN, :]
                vw = vs[st:st + BAND_WIN, :]
                zero = jnp.zeros_like(qi)
                q2 = jnp.concatenate([jnp.where(first, qi, zero), jnp.where(first, zero, qi)], axis=0)
                sc = _dot_nt(q2, kw)
                b = bias_ref[bias_index[p][i]]
                halves = []
                for h in range(2):
                    sh = sc[h * BAND_Q:(h + 1) * BAND_Q] + b
                    m = jnp.max(sh, axis=1, keepdims=True)
                    pr = jnp.exp2(sh - m)
                    l = jnp.sum(pr, axis=1, keepdims=True)
                    halves.append((pr.astype(BF16), l, m + jnp.log(l) * LOG2E))
                o2 = _dot(jnp.concatenate([halves[0][0], halves[1][0]], axis=0), vw)
                o_blk = jnp.where(first, o2[:BAND_Q] / halves[0][1], o2[BAND_Q:] / halves[1][1])
                lse_blk = jnp.where(first, jnp.broadcast_to(halves[0][2], (BAND_Q, LANES)),
                                    jnp.broadcast_to(halves[1][2], (BAND_Q, LANES)))
                op_ref[p, u0:u0 + BAND_Q, :] = o_blk
                lp_ref[p, u0:u0 + BAND_Q, :] = lse_blk
            if d > 1:
                length = s // d
                for r in range(d):
                    on_ref.at[p - 1][pl.ds(r, length, stride=d), :] = op_ref[p, r * length:(r + 1) * length, :]
                    ln_ref.at[p - 1][pl.ds(r, length, stride=d), :] = lp_ref[p, r * length:(r + 1) * length, :]
        lses = [lp_ref[0]] + [ln_ref[p] for p in range(npat - 1)]
        outs = [op_ref[0]] + [on_ref[p] for p in range(npat - 1)]
        m = functools.reduce(jnp.maximum, lses)
        ws = [jnp.exp2(l - m) for l in lses]
        den = functools.reduce(lambda a, c: a + c, ws)
        o_ref[...] = functools.reduce(lambda a, c: a + c, [w * o for w, o in zip(ws, outs)]) / den
        lse_ref[...] = m + jnp.log(den) * LOG2E

    blk = pl.BlockSpec((s, LANES), lambda b, g: (b, g))
    out = jax.ShapeDtypeStruct((nb * s, A_WIDTH), F32)
    copy = jax.ShapeDtypeStruct((nb * s, A_WIDTH), BF16)
    n_copies = 3 * (npat - 1)
    res = pl.pallas_call(
        body, name=name, grid=(nb, A_WIDTH // LANES),
        out_shape=(out, out) + (copy,) * n_copies,
        in_specs=[blk, blk, blk, pl.BlockSpec(bias.shape, lambda b, g: (0, 0, 0))],
        out_specs=(blk, blk) + (blk,) * n_copies,
        scratch_shapes=[pltpu.VMEM((s, LANES), F32), pltpu.VMEM((npat, s, LANES), F32),
                        pltpu.VMEM((npat, s, LANES), F32), pltpu.VMEM((npat - 1, s, LANES), F32),
                        pltpu.VMEM((npat - 1, s, LANES), F32)],
        compiler_params=_params(("parallel", "parallel")),
    )(q, k, v, bias)
    return res[0], res[1], res[2:]


def _dilated_bwd(q, k, v, ordered, o, do, lse, bias, bias_index, *, nb, s, scale, name):
    nblk = s // BAND_Q
    npat = len(DILATED)
    n_copies = 3 * (npat - 1)

    def body(q_ref, k_ref, v_ref, *rest):
        ordered_refs = rest[:n_copies]
        (o_ref, do_ref, lse_ref, bias_ref, dq_out, dk_out, dv_out, stage_ref, rs_ref, dop_ref, rsp_ref,
         dqp_ref, dkp_ref, dvp_ref, dq_ref, dk_ref, dv_ref, nat_ref) = rest[n_copies:]
        lane = lax.broadcasted_iota(jnp.int32, (1, LANES), 1)
        first = lane < 64
        prod = do_ref[...].astype(F32) * o_ref[...]
        d0 = jnp.sum(jnp.where(first, prod, 0.0), axis=1, keepdims=True)
        d1 = jnp.sum(jnp.where(first, 0.0, prod), axis=1, keepdims=True)
        delta = jnp.where(first, jnp.broadcast_to(d0, (s, LANES)), jnp.broadcast_to(d1, (s, LANES)))
        rs_ref[...] = jnp.where((lane & 32) == 0, lse_ref[...], delta)
        for p, (_, d) in enumerate(DILATED):
            length = s // d
            if d == 1:
                qs, ks, vs, dos, rss = q_ref, k_ref, v_ref, do_ref, rs_ref
                dqs, dks, dvs = dq_ref, dk_ref, dv_ref
            else:
                for src, dst in ((do_ref, dop_ref), (rs_ref, rsp_ref)):
                    _to_pattern_order(src, dst, stage_ref, s, d)
                qs, ks, vs = ordered_refs[3 * (p - 1):3 * p]
                dos, rss = dop_ref, rsp_ref
                dqs, dks, dvs = dqp_ref, dkp_ref, dvp_ref
            dks[...] = jnp.zeros((s, LANES), F32)
            dvs[...] = jnp.zeros((s, LANES), F32)
            for i in range(nblk):
                u0 = i * BAND_Q
                st = _band_start(i, s)
                qi = qs[u0:u0 + BAND_Q, :]
                doi = dos[u0:u0 + BAND_Q, :]
                kw = ks[st:st + BAND_WIN, :]
                vw = vs[st:st + BAND_WIN, :]
                zero = jnp.zeros_like(qi)
                q2 = jnp.concatenate([jnp.where(first, qi, zero), jnp.where(first, zero, qi)], axis=0)
                do2 = jnp.concatenate([jnp.where(first, doi, zero), jnp.where(first, zero, doi)], axis=0)
                sc = _dot_nt(q2, kw)
                dp = _dot_nt(do2, vw)
                b = bias_ref[bias_index[p][i]]
                rs_i = rss[u0:u0 + BAND_Q, :]
                ps, dss = [], []
                for h in range(2):
                    rows = slice(h * BAND_Q, (h + 1) * BAND_Q)
                    pr = jnp.exp2(sc[rows] + b - rs_i[:, 64 * h:64 * h + 1])
                    ps.append(pr.astype(BF16))
                    dss.append((pr * (dp[rows] - rs_i[:, 64 * h + 32:64 * h + 33])).astype(BF16))
                p2 = jnp.concatenate(ps, axis=0)
                ds2 = jnp.concatenate(dss, axis=0)
                dq2 = _dot(ds2, kw)
                dqs[u0:u0 + BAND_Q, :] = jnp.where(first, dq2[:BAND_Q], dq2[BAND_Q:]) * scale
                dks[st:st + BAND_WIN, :] += _dot_tn(ds2, q2)
                dvs[st:st + BAND_WIN, :] += _dot_tn(p2, do2)
            if d > 1:
                for j, src in enumerate((dqp_ref, dkp_ref, dvp_ref)):
                    for r in range(d):
                        nat_ref.at[p - 1, j][pl.ds(r, length, stride=d), :] = src[r * length:(r + 1) * length, :]

        def total(j, first_ref):
            return functools.reduce(lambda a, c: a + c, [first_ref[...]] + [nat_ref[p, j] for p in range(npat - 1)])

        dq_out[...] = total(0, dq_ref).astype(BF16)
        dk_out[...] = (total(1, dk_ref) * LN2).astype(BF16)
        dv_out[...] = total(2, dv_ref).astype(BF16)

    blk = pl.BlockSpec((s, LANES), lambda b, g: (b, g))
    out = jax.ShapeDtypeStruct((nb * s, A_WIDTH), BF16)
    f32_buf = pltpu.VMEM((s, LANES), F32)
    bf_buf = pltpu.VMEM((s, LANES), BF16)
    return pl.pallas_call(
        body, name=name, grid=(nb, A_WIDTH // LANES),
        out_shape=(out, out, out),
        in_specs=[blk] * (6 + n_copies) + [pl.BlockSpec(bias.shape, lambda b, g: (0, 0, 0))],
        out_specs=(blk, blk, blk),
        scratch_shapes=[f32_buf, f32_buf, bf_buf] + [f32_buf] * 7 + [pltpu.VMEM((npat - 1, 3, s, LANES), F32)],
        compiler_params=_params(("parallel", "parallel")),
    )(q, k, v, *ordered, o, do, lse, bias)


def _post(h32, ya, ybp, ym, proj, target, w_out, g_a, g_b, g_m, g_post, b_post, tm=256):
    t = h32.shape[0]

    def body(h_ref, ya_ref, yb_ref, ym_ref, ga_ref, gb_ref, gm_ref, tg_ref, wo_ref,
             goa_ref, gob_ref, gom_ref, gp_ref, bp_ref,
             y_ref, dz_ref, doa_ref, dob_ref, dom_ref, dga_ref, dgb_ref, dgm_ref,
             loss_ref, dgp_ref, dbp_ref, dgoa_ref, dgob_ref, dgom_ref):
        i = pl.program_id(0)

        @pl.when(i == 0)
        def _():
            for r in (loss_ref, dgp_ref, dbp_ref, dgoa_ref, dgob_ref, dgom_ref):
                r[...] = jnp.zeros_like(r)

        lane = lax.broadcasted_iota(jnp.int32, (1, LANES), 1)
        low = lane < 64
        h = h_ref[...]

        ybp_v = yb_ref[...]
        yb = jnp.concatenate(
            [jnp.where(low, pltpu.roll(ybp_v[:, 2 * j * LANES:(2 * j + 1) * LANES], 64, 1),
                       ybp_v[:, (2 * j + 1) * LANES:(2 * j + 2) * LANES]) for j in range(4)], axis=1)

        def gated(raw, gate, gain, width):
            xh, r = _rms_hat(raw, width)
            n = xh * gain
            sg = 1.0 / (1.0 + jnp.exp(-gate))
            return xh, r, n, sg, n * (gate * sg)

        gate_a, gate_b, gate_m = ga_ref[...], gb_ref[...], gm_ref[...]
        xh_a, r_a, n_a, sg_a, y_a = gated(ya_ref[...], gate_a, goa_ref[...], A_WIDTH)
        xh_b, r_b, n_b, sg_b, y_b = gated(yb, gate_b, gob_ref[...], 512)
        xh_m, r_m, n_m, sg_m, y_m = gated(ym_ref[...], gate_m, gom_ref[...], 512)
        y = jnp.concatenate([y_a, y_b, y_m], axis=1).astype(BF16)
        y_ref[...] = y
        z = DEEPNORM_ALPHA * h + _dot(y, wo_ref[...])
        zh, rstd = _ln_hat(z)
        err = zh * gp_ref[...] + bp_ref[...] - tg_ref[...]
        rows = jnp.sum(err * err, axis=1, keepdims=True)
        loss_ref[...] += jnp.broadcast_to(jnp.sum(rows, axis=0, keepdims=True) * (0.5 / D_MODEL), (1, LANES))
        dout = err * (1.0 / D_MODEL)
        dgp_ref[...] += _colsum(dout * zh)
        dbp_ref[...] += _colsum(dout)
        dz = _ln_bwd_rows(dout * gp_ref[...], zh, rstd)
        dz_ref[...] = dz
        dy = _dot_nt(dz.astype(BF16), wo_ref[...])

        def gated_bwd(dyg, xh, r, n, sg, gate, gain, width, dgain_ref):
            dn = dyg * (gate * sg)
            dgate = dyg * n * (sg * (1.0 + gate * (1.0 - sg)))
            dgain_ref[...] += _colsum(dn * xh)
            return _rms_bwd(dn * gain, xh, r, width), dgate

        dya, dgate_a = gated_bwd(dy[:, 0:1024], xh_a, r_a, n_a, sg_a, gate_a, goa_ref[...], A_WIDTH, dgoa_ref)
        dyb, dgate_b = gated_bwd(dy[:, 1024:1536], xh_b, r_b, n_b, sg_b, gate_b, gob_ref[...], 512, dgob_ref)
        dym, dgate_m = gated_bwd(dy[:, 1536:2048], xh_m, r_m, n_m, sg_m, gate_m, gom_ref[...], 512, dgom_ref)
        doa_ref[...] = dya.astype(BF16)
        dom_ref[...] = dym.astype(BF16)
        dga_ref[...] = dgate_a.astype(BF16)
        dgb_ref[...] = dgate_b.astype(BF16)
        dgm_ref[...] = dgate_m.astype(BF16)
        for j in range(4):
            blk = dyb[:, j * LANES:(j + 1) * LANES]
            dob_ref[:, 2 * j * LANES:(2 * j + 1) * LANES] = jnp.where(low, 0.0, pltpu.roll(blk, 64, 1)).astype(BF16)
            dob_ref[:, (2 * j + 1) * LANES:(2 * j + 2) * LANES] = jnp.where(low, 0.0, blk).astype(BF16)

    def col(width, idx):
        return pl.BlockSpec((tm, width), lambda i: (i, idx))

    def full(shape):
        return pl.BlockSpec(shape, lambda i: (0, 0))

    def acc(width):
        return jax.ShapeDtypeStruct((1, width), F32)

    return pl.pallas_call(
        body, name="post", grid=(t // tm,),
        out_shape=(jax.ShapeDtypeStruct((t, 2048), BF16), jax.ShapeDtypeStruct((t, 1024), F32),
                   jax.ShapeDtypeStruct((t, 1024), BF16), jax.ShapeDtypeStruct((t, 1024), BF16),
                   jax.ShapeDtypeStruct((t, 512), BF16),
                   jax.ShapeDtypeStruct((t, 1024), BF16), jax.ShapeDtypeStruct((t, 512), BF16),
                   jax.ShapeDtypeStruct((t, 512), BF16),
                   acc(LANES), acc(1024), acc(1024), acc(1024), acc(512), acc(512)),
        in_specs=[col(1024, 0), col(1024, 0), col(1024, 0), col(512, 0),
                  col(1024, 3), col(512, COL_BG // 512), col(512, COL_MG // 512), col(1024, 0),
                  full((2048, 1024)),
                  full((1, 1024)), full((1, 512)), full((1, 512)), full((1, 1024)), full((1, 1024))],
        out_specs=(col(2048, 0), col(1024, 0), col(1024, 0), col(1024, 0), col(512, 0),
                   col(1024, 0), col(512, 0), col(512, 0),
                   full((1, LANES)), full((1, 1024)), full((1, 1024)), full((1, 1024)), full((1, 512)),
                   full((1, 512))),
        compiler_params=_params(("arbitrary",)),
    )(h32, ya, ybp, ym, proj, proj, proj, target, w_out, g_a, g_b, g_m, g_post, b_post)


def _prep_bwd(dqa, dka, dva, dqb, dkb, dvb, dqm, dga, dgb, dgm, proj, trig, w_uq, w_ukv, g_cq, g_ckv,
              rope_a, rope_b, tm=256):
    t = proj.shape[0]

    def body(dqa_ref, dka_ref, dva_ref, dqb_ref, dkb_ref, dvb_ref, dqm_ref, dga_ref, dgb_ref, dgm_ref,
             bs_ref, trig_ref, wuq_ref, wukv_ref, gcq_ref, gckv_ref, ra_ref, rb_ref,
             dproj_ref, dqf_ref, dkv_ref, dgcq_ref, dgckv_ref):
        i = pl.program_id(0)

        @pl.when(i == 0)
        def _():
            dgcq_ref[...] = jnp.zeros_like(dgcq_ref)
            dgckv_ref[...] = jnp.zeros_like(dgckv_ref)

        ta = _rope_tables(trig_ref[:, 0:LANES], trig_ref[:, LANES:2 * LANES], ra_ref[...])
        tb = _rope_tables(trig_ref[:, 2 * LANES:3 * LANES], trig_ref[:, 3 * LANES:4 * LANES], rb_ref[...])
        for j in range(A_WIDTH // LANES):
            sl = slice(j * LANES, (j + 1) * LANES)
            dproj_ref[:, j * LANES:(j + 1) * LANES] = (
                _rope(dqa_ref[:, sl].astype(F32), ta, 8, inverse=True).astype(BF16))
            dproj_ref[:, 1024 + j * LANES:1024 + (j + 1) * LANES] = (
                _rope(dka_ref[:, sl].astype(F32), ta, 8, inverse=True).astype(BF16))
        dproj_ref[:, 2048:3072] = dva_ref[...]
        dproj_ref[:, 3072:4096] = dga_ref[...]

        lane = lax.broadcasted_iota(jnp.int32, (1, LANES), 1)
        low = lane < 64
        rope_lanes = (lane >= 64) & (lane < 96)
        dkr = jnp.zeros((tm, LANES), F32)
        for h in range(MLA_HEADS):
            sl = slice(h * LANES, (h + 1) * LANES)
            dqf_ref[:, sl] = _rope(dqb_ref[:, sl].astype(F32), tb, 16, inverse=True).astype(BF16)
            dk_h = dkb_ref[:, sl]
            dkv_ref[:, sl] = jnp.where(low, dk_h, dvb_ref[:, sl])
            dkr = dkr + jnp.where(rope_lanes, dk_h.astype(F32), 0.0)
        dkr = _rope(dkr, tb, 16, inverse=True)

        cq_hat, r_q = _rms_hat(bs_ref[:, 0:MLA_Q_RANK], MLA_Q_RANK)
        dcqn = _dot(dqf_ref[...], wuq_ref[...])
        dgcq_ref[...] += _colsum(dcqn * cq_hat)
        dproj_ref[:, COL_CQ:COL_CQ + 256] = _rms_bwd(dcqn * gcq_ref[...], cq_hat, r_q, MLA_Q_RANK).astype(BF16)
        ckv_hat, r_kv = _rms_hat(bs_ref[:, MLA_Q_RANK:MLA_Q_RANK + MLA_KV_RANK], MLA_KV_RANK)
        dckvn = _dot_nt(dkv_ref[...], wukv_ref[...])
        dgckv_ref[...] += _colsum(dckvn * ckv_hat)
        dproj_ref[:, COL_CQ + 256:COL_CQ + 384] = (
            _rms_bwd(dckvn * gckv_ref[...], ckv_hat, r_kv, MLA_KV_RANK).astype(BF16))
        dproj_ref[:, COL_CQ + 384:COL_CQ + 512] = dkr.astype(BF16)
        dproj_ref[:, COL_BG:COL_BG + 512] = dgb_ref[...]
        dproj_ref[:, COL_MQ:COL_MQ + 512] = dqm_ref[...]
        dproj_ref[:, COL_MG:COL_MG + 512] = dgm_ref[...]

    def col(width, idx):
        return pl.BlockSpec((tm, width), lambda i: (i, idx))

    def full(shape):
        return pl.BlockSpec(shape, lambda i: (0, 0))

    return pl.pallas_call(
        body, name="prep_bwd", grid=(t // tm,),
        out_shape=(jax.ShapeDtypeStruct((t, PROJ_W), BF16), jax.ShapeDtypeStruct((t, 1024), BF16),
                   jax.ShapeDtypeStruct((t, 1024), BF16),
                   jax.ShapeDtypeStruct((1, MLA_Q_RANK), F32), jax.ShapeDtypeStruct((1, MLA_KV_RANK), F32)),
        in_specs=[col(1024, 0)] * 6 + [col(512, 0), col(1024, 0), col(512, 0), col(512, 0),
                  col(512, COL_CQ // 512), pl.BlockSpec((tm, 4 * LANES), lambda i: (i, 0)),
                  full((1024, MLA_Q_RANK)), full((MLA_KV_RANK, 1024)),
                  full((1, MLA_Q_RANK)), full((1, MLA_KV_RANK)), full((8, LANES)), full((8, LANES))],
        out_specs=(col(PROJ_W, 0), col(1024, 0), col(1024, 0), full((1, MLA_Q_RANK)), full((1, MLA_KV_RANK))),
        compiler_params=_params(("arbitrary",)),
    )(dqa, dka, dva, dqb, dkb, dvb, dqm, dga, dgb, dgm, proj, trig, w_uq, w_ukv, g_cq, g_ckv, rope_a, rope_b)


def _adamw_math(gv, w, m, v):
    m_new = ADAM_B1 * m + (1.0 - ADAM_B1) * gv
    v_new = ADAM_B2 * v + (1.0 - ADAM_B2) * (gv * gv)
    m_hat = m_new / (1.0 - ADAM_B1 ** ADAM_STEP)
    v_hat = v_new / (1.0 - ADAM_B2 ** ADAM_STEP)
    return -ADAM_LR * (m_hat / (jnp.sqrt(v_hat) + ADAM_EPS) + ADAM_WD * w), m_new, v_new


def _adamw(g, w, m, v, tr, name):
    r, cols = w.shape

    def body(g_ref, w_ref, m_ref, v_ref, go_ref, d_ref, nm_ref, nv_ref):
        gv = g_ref[...]
        go_ref[...] = gv
        d_ref[...], nm_ref[...], nv_ref[...] = _adamw_math(gv, w_ref[...], m_ref[...], v_ref[...])

    tile = pl.BlockSpec((tr, cols), lambda i: (i, 0))
    shape = jax.ShapeDtypeStruct((r, cols), F32)
    return pl.pallas_call(
        body, name=name, grid=(r // tr,),
        out_shape=(shape,) * 4, in_specs=[tile] * 4, out_specs=(tile,) * 4,
        compiler_params=_params(("parallel",)),
    )(g, w, m, v)


def _adamw_pieces(g, w, m, v, pieces, name):
    n = len(pieces)
    per_piece = isinstance(w, (list, tuple))
    shapes = [jax.ShapeDtypeStruct((r1 - r0, c1 - c0), F32) for r0, r1, c0, c1 in pieces]
    args = (g, *w, *m, *v) if per_piece else (g, w, m, v)

    def body(g_ref, *refs):
        ins, outs = refs[:len(args) - 1], refs[len(args) - 1:]
        gv = g_ref[...]
        if not per_piece:
            results = (gv,) + _adamw_math(gv, ins[0][...], ins[1][...], ins[2][...])
        for p, (r0, r1, c0, c1) in enumerate(pieces):
            if per_piece:
                gp = gv[r0:r1, c0:c1]
                vals = (gp,) + _adamw_math(gp, ins[p][...], ins[n + p][...], ins[2 * n + p][...])
            else:
                vals = [full[r0:r1, c0:c1] for full in results]
            for kind, val in enumerate(vals):
                outs[kind * n + p][...] = val

    flat = pl.pallas_call(
        body, name=name, out_shape=tuple(shapes) * 4,
        in_specs=[IN_VMEM] * len(args), out_specs=tuple([IN_VMEM] * (4 * n)),
        compiler_params=_params(None),
    )(*args)
    return [[flat[kind * n + p] for kind in range(4)] for p in range(n)]


def _core_sum(g, recv, core, rows, tr, name, ride=None):
    cols = g.shape[2]
    nblk = rows // tr
    n_in = len(ride.args) if ride else 0
    n_out = len(ride.out_shapes) if ride else 0
    both = g.shape[1] == 2 * rows

    def body(c_ref, g_ref, r_ref, *rest):
        sf_ref, sb_ref = rest[n_in], rest[n_in + 1]
        if ride:
            j, i = pl.program_id(0), pl.program_id(1)
            ride.run(j * nblk + i, 4 * nblk, rest[:n_in], rest[n_in + 2:n_in + 2 + n_out],
                     rest[n_in + 2 + n_out:])
        tot = g_ref[...] + r_ref[...].astype(F32)
        sf_ref[...] = tot
        sb_ref[...] = tot.astype(BF16)

    half = pl.BlockSpec((None, tr, cols), lambda j, i, c_ref: (j, i, 0))
    shapes = (jax.ShapeDtypeStruct((4, rows, cols), F32), jax.ShapeDtypeStruct((4, rows, cols), BF16))
    return pl.pallas_call(
        body, name=name,
        grid_spec=pltpu.PrefetchScalarGridSpec(
            num_scalar_prefetch=1, grid=(4, nblk),
            in_specs=[pl.BlockSpec((None, tr, cols), lambda j, i, c_ref: (j, c_ref[0] * nblk * both + i, 0)), half]
            + (ride.in_specs if ride else []),
            out_specs=(half, half) + (ANY,) * n_out,
            scratch_shapes=ride.scratch() if ride else []),
        out_shape=shapes + tuple(ride.out_shapes if ride else ()),
        compiler_params=_params(("arbitrary", "arbitrary") if ride else ("parallel", "parallel")),
    )(core, g, recv, *(ride.args if ride else ()))


def _half_to_sibling(g):
    def plan(in_refs, out_refs, send_sems, recv_sems):
        x, y, c = _position()
        cp = pltpu.make_async_remote_copy(
            src_ref=in_refs[0].at[:, 1 - c] if g.ndim == 4 else in_refs[0], dst_ref=out_refs[0],
            send_sem=send_sems.at[0], recv_sem=recv_sems.at[0], device_id=(x, y, 1 - c), device_id_type=MESH)

        def finish():
            cp.wait_recv()
            cp.wait_send()

        return cp.start, finish

    return _Ride([g], [jax.ShapeDtypeStruct((4, g.shape[-2], 1024), g.dtype)], (1, 1), plan)


def _gather_plan(src_ref, dst_ref, send_sems, recv_sems, local_sems):
    x, y, c = _position()
    me = 2 * x + y
    rows = src_ref.shape[1]
    cut = -(-rows // 32) * 16
    pieces = (pl.ds(0, cut), pl.ds(cut, rows - cut))
    local = pltpu.make_async_copy(src_ref, dst_ref.at[me], local_sems.at[0])

    def over_ici(sem, k, chip, t, src=None):
        where = dst_ref.at[chip, c, pieces[t]]
        return pltpu.make_async_remote_copy(
            src_ref=where if src is None else src, dst_ref=where, send_sem=send_sems.at[sem],
            recv_sem=recv_sems.at[sem], device_id=(x ^ (k >> 1), y ^ (k & 1), c), device_id_type=MESH)

    def mine_to(k, t):
        return over_ici(2 * (k - 1) + t, k, me, t, src=src_ref.at[c, pieces[t]])

    def from_neighbour(k, t):
        return over_ici(2 * (k - 1) + t, k, me ^ k, t)

    def to_sibling(k, half):
        piece = dst_ref.at[me ^ k, half]
        return pltpu.make_async_remote_copy(
            src_ref=piece, dst_ref=piece, send_sem=send_sems.at[5 + k], recv_sem=recv_sems.at[5 + k],
            device_id=(x, y, 1 - c), device_id_type=MESH)

    sends = [mine_to(2, 0), mine_to(1, 1), mine_to(2, 1), mine_to(1, 0)]
    onward = [over_ici(4, 1, me ^ 2, 0), over_ici(5, 2, me ^ 1, 1)]

    def start():
        local.start()
        for cp in sends:
            cp.start()

    def pass_on():
        from_neighbour(2, 0).wait_recv()
        onward[0].start()
        from_neighbour(1, 1).wait_recv()
        onward[1].start()

    def to_other_core():
        from_neighbour(2, 1).wait_recv()
        to_sibling(2, c).start()
        from_neighbour(1, 0).wait_recv()
        to_sibling(1, c).start()
        over_ici(4, 1, me ^ 3, 0).wait_recv()
        over_ici(5, 2, me ^ 3, 1).wait_recv()
        to_sibling(3, c).start()

    def finish():
        for k in (1, 2, 3):
            to_sibling(k, 1 - c).wait_recv()
        for cp in sends + onward + [to_sibling(k, c) for k in (1, 2, 3)]:
            cp.wait_send()
        local.wait()

    return start, pass_on, to_other_core, finish


def _gather_ride(shard, spread):
    def plan(in_refs, out_refs, send_sems, recv_sems, local_sems):
        return _gather_plan(in_refs[0], out_refs[0], send_sems, recv_sems, local_sems)

    return _Ride([shard], [jax.ShapeDtypeStruct((4,) + shard.shape, shard.dtype)], (9, 9, 1), plan,
                 in_specs=[IN_VMEM], spread=spread)


def _chip_sum(sf, recv, chip, rows, tr, name):
    cols = sf.shape[2]
    n_recv = recv.shape[0]

    def body(me_ref, sf_ref, r_ref, out_ref):
        acc = sf_ref[...]
        for k in range(n_recv):
            acc = acc + r_ref[k].astype(F32)
        out_ref[...] = acc

    return pl.pallas_call(
        body, name=name,
        grid_spec=pltpu.PrefetchScalarGridSpec(
            num_scalar_prefetch=1, grid=(rows // tr,),
            in_specs=[pl.BlockSpec((None, tr, cols), lambda i, me_ref: (me_ref[0], i, 0)),
                      pl.BlockSpec((n_recv, tr, cols), lambda i, me_ref: (0, i, 0))],
            out_specs=pl.BlockSpec((tr, cols), lambda i, me_ref: (i, 0))),
        out_shape=jax.ShapeDtypeStruct((rows, cols), F32),
        compiler_params=_params(("parallel",)),
    )(chip, sf, recv)


def _position():
    return lax.axis_index("x"), lax.axis_index("y"), lax.axis_index("c")


def _dh_scatter(dproj, w_in_arr_t, x, dz, g, sb_in, sb_rest, tm=512, tk=3072):
    t, d = x.shape
    nk = dproj.shape[1] // tk
    ni = t // tm
    total = ni * nk
    halves = (HALF_IN, HALF_REST)
    cuts = tuple(-(-rows // 32) * 16 for rows in halves)

    def rows_of(a, p):
        return cuts[a] if p == 0 else halves[a] - cuts[a]

    def piece(a, p):
        return pl.ds(0, cuts[a]) if p == 0 else pl.ds(cuts[a], halves[a] - cuts[a])

    def body(dp_ref, w_ref, x_ref, dz_ref, g_ref, sbin_ref, sbrest_ref, dx_ref, dg_ref, db_ref, rin_ref, rrest_ref,
             acc_ref, pay_in0, pay_in1, pay_rest0, pay_rest1, own_in0, own_in1, own_rest0, own_rest1,
             send_sems, recv_sems, local_sems):
        step = pl.program_id(0) * nk + pl.program_id(1)
        kk = pl.program_id(1)
        px, py, pc = _position()
        me = 2 * px + py
        srcs = (sbin_ref, sbrest_ref)
        dsts = (rin_ref, rrest_ref)
        pays = ((pay_in0, pay_in1), (pay_rest0, pay_rest1))
        owns = ((own_in0, own_in1), (own_rest0, own_rest1))
        via = (2, 1)
        onto = (1, 2)

        def peer(k):
            return (px ^ (k >> 1), py ^ (k & 1), pc)

        def payload(a, p):
            return pltpu.make_async_remote_copy(
                src_ref=srcs[a].at[me ^ 3, piece(a, p)], dst_ref=pays[a][p], send_sem=send_sems.at[2 * a + p],
                recv_sem=recv_sems.at[2 * a + p], device_id=peer(via[p]), device_id_type=MESH)

        def direct(a, k, p, src):
            sem = 4 + 4 * a + 2 * (k - 1) + p
            return pltpu.make_async_remote_copy(
                src_ref=src, dst_ref=dsts[a].at[k - 1, piece(a, p)], send_sem=send_sems.at[sem],
                recv_sem=recv_sems.at[sem], device_id=peer(k), device_id_type=MESH)

        def plain(a, k, p):
            return direct(a, k, p, srcs[a].at[me ^ k, piece(a, p)])

        def stage(a, p):
            return pltpu.make_async_copy(srcs[a].at[me ^ onto[p], piece(a, p)], owns[a][p], local_sems.at[2 * a + p])

        @pl.when(step == 0)
        def _():
            dg_ref[...] = jnp.zeros_like(dg_ref)
            db_ref[...] = jnp.zeros_like(db_ref)
            for a in range(2):
                for p in range(2):
                    payload(a, p).start()
                    stage(a, p).start()
                plain(a, 1, 1).start()
                plain(a, 2, 0).start()

        @pl.when(step == (5 * total) // 8)
        def _():
            for a in range(2):
                for p in range(2):
                    payload(a, p).wait_recv()
                    stage(a, p).wait()
                    owns[a][p][...] = (owns[a][p][...].astype(F32) + pays[a][p][...].astype(F32)).astype(BF16)
                    direct(a, onto[p], p, owns[a][p]).start()

        part = _dot(dp_ref[...], w_ref[...])

        @pl.when(kk == 0)
        def _():
            acc_ref[...] = part

        @pl.when(kk > 0)
        def _():
            acc_ref[...] += part

        @pl.when(kk == nk - 1)
        def _():
            xh, rstd = _ln_hat(x_ref[...])
            dht = acc_ref[...] + DEEPNORM_ALPHA * dz_ref[...]
            dg_ref[...] += _colsum(dht * xh)
            db_ref[...] += _colsum(dht)
            dx_ref[...] = _ln_bwd_rows(dht * g_ref[...], xh, rstd)

        @pl.when(step == total - 1)
        def _():
            for a in range(2):
                for k in (1, 2):
                    for p in range(2):
                        plain(a, k, p).wait_recv()
            for a in range(2):
                for p in range(2):
                    payload(a, p).wait_send()
                    direct(a, onto[p], p, owns[a][p]).wait_send()
                plain(a, 1, 1).wait_send()
                plain(a, 2, 0).wait_send()

    tile = pl.BlockSpec((tm, d), lambda i, kk: (i, 0))
    row = pl.BlockSpec((1, d), lambda i, kk: (0, 0))
    pieces = [pltpu.VMEM((rows_of(a, p), 1024), BF16) for a in range(2) for p in range(2)]
    return pl.pallas_call(
        body, name="dh_scatter", grid=(ni, nk),
        out_shape=(jax.ShapeDtypeStruct((t, d), F32), jax.ShapeDtypeStruct((1, d), F32),
                   jax.ShapeDtypeStruct((1, d), F32),
                   jax.ShapeDtypeStruct((2, HALF_IN, 1024), BF16),
                   jax.ShapeDtypeStruct((2, HALF_REST, 1024), BF16)),
        in_specs=[pl.BlockSpec((tm, tk), lambda i, kk: (i, kk)), pl.BlockSpec((tk, d), lambda i, kk: (kk, 0)),
                  tile, tile, row, ANY, ANY],
        out_specs=(tile, row, row, ANY, ANY),
        scratch_shapes=[pltpu.VMEM((tm, d), F32)] + pieces + pieces
        + [pltpu.SemaphoreType.DMA((12,)), pltpu.SemaphoreType.DMA((12,)), pltpu.SemaphoreType.DMA((4,))],
        compiler_params=_params(("arbitrary", "arbitrary")),
    )(dproj, w_in_arr_t, x, dz, g, sb_in, sb_rest)


def _join_halves(gh_in, gh_rest):
    def body(hin_ref, hrest_ref, oin_ref, orest_ref, send_sems, recv_sems, local_sems):
        x, y, c = _position()
        srcs = (hin_ref, hrest_ref)
        dsts = (oin_ref, orest_ref)

        def rows(a, half):
            return dsts[a].at[half]

        local = [pltpu.make_async_copy(srcs[a], rows(a, c), local_sems.at[a]) for a in range(2)]
        remote = [pltpu.make_async_remote_copy(
            src_ref=srcs[a], dst_ref=rows(a, c), send_sem=send_sems.at[a], recv_sem=recv_sems.at[a],
            device_id=(x, y, 1 - c), device_id_type=MESH) for a in range(2)]
        for cp in local + remote:
            cp.start()
        for a in range(2):
            pltpu.make_async_remote_copy(
                src_ref=srcs[a], dst_ref=rows(a, 1 - c), send_sem=send_sems.at[a], recv_sem=recv_sems.at[a],
                device_id=(x, y, 1 - c), device_id_type=MESH).wait_recv()
        for cp in remote:
            cp.wait_send()
        for cp in local:
            cp.wait()

    return pl.pallas_call(
        body, name="join_halves",
        out_shape=(jax.ShapeDtypeStruct((2, HALF_IN, 1024), F32),
                   jax.ShapeDtypeStruct((2, HALF_REST, 1024), F32)),
        in_specs=[IN_VMEM, IN_VMEM], out_specs=(ANY, ANY),
        scratch_shapes=[pltpu.SemaphoreType.DMA((2,)), pltpu.SemaphoreType.DMA((2,)), pltpu.SemaphoreType.DMA((2,))],
    )(gh_in, gh_rest)


def _allreduce_small(vec):
    def body(vec_ref, out_ref, all_ref, send_sems, recv_sems):
        x, y, c = _position()
        me = 4 * x + 2 * y + c
        all_ref[me] = vec_ref[...]

        def copy(k, slot):
            return pltpu.make_async_remote_copy(
                src_ref=vec_ref, dst_ref=all_ref.at[slot], send_sem=send_sems.at[k - 1], recv_sem=recv_sems.at[k - 1],
                device_id=(x ^ (k >> 2), y ^ ((k >> 1) & 1), c ^ (k & 1)), device_id_type=MESH)

        copies = [copy(k, me) for k in range(1, 8)]
        for cp in copies:
            cp.start()
        for k in range(1, 8):
            copy(k, me ^ k).wait_recv()
        for cp in copies:
            cp.wait_send()
        total = all_ref[0]
        for d in range(1, 8):
            total = total + all_ref[d]
        out_ref[...] = total

    return pl.pallas_call(
        body, name="allreduce_small",
        out_shape=jax.ShapeDtypeStruct(vec.shape, vec.dtype),
        in_specs=[pl.BlockSpec(memory_space=pltpu.VMEM)], out_specs=pl.BlockSpec(memory_space=pltpu.VMEM),
        scratch_shapes=[pltpu.VMEM((8,) + vec.shape, vec.dtype), pltpu.SemaphoreType.DMA((7,)),
                        pltpu.SemaphoreType.DMA((7,))],
    )(vec)


def _pack_rest(w_uq, w_ukv, w_mem, w_out):
    rows = jnp.concatenate([w_uq[0].T.reshape(-1, 1024), w_ukv.reshape(-1, 1024), w_mem.reshape(-1, 1024),
                            w_out.reshape(-1, 1024)], axis=0)
    return jnp.pad(rows, ((0, ROWS_REST - ROWS_USED), (0, 0)))


def _arranged_w_in(g_in):
    z = functools.partial(jnp.zeros, dtype=g_in.dtype)
    cut = 4480 - 2 * SHARD_ROWS
    return jnp.concatenate(
        [g_in[0, :SHARD_ROWS], g_in[1, :SHARD_ROWS], g_in[2, :cut], z((64, 1024)), g_in[2, cut:cut + 32],
         z((32, 1024)), g_in[2, cut + 32:SHARD_ROWS], g_in[3, :SHARD_ROWS]], axis=0)


def _rest_weights(g_rest):
    w_uq_t = g_rest[:, 0:ROWS_UQ].reshape(768, 256)
    w_uq_pad_t = jnp.pad(w_uq_t.reshape(MLA_HEADS, MLA_QK_DIM, 256), ((0, 0), (0, 32), (0, 0))).reshape(1024, 256)
    w_ukv = jnp.concatenate([g_rest[j, ROWS_UQ:ROWS_UQ + ROWS_UKV].reshape(128, 256) for j in range(4)], axis=1)
    lo = ROWS_UQ + ROWS_UKV
    w_mem = g_rest[:, lo:lo + ROWS_MEM].reshape(4 * ROWS_MEM, 1024)
    w_out = g_rest[:, lo + ROWS_MEM:lo + ROWS_MEM + ROWS_OUT].reshape(4 * ROWS_OUT, 1024)
    return w_uq_pad_t, w_ukv, w_mem, w_out


def _split_in(dw_in_arr_t, core):
    a = dw_in_arr_t
    gap = jnp.zeros((ROWS_IN - SHARD_ROWS, 1024), a.dtype)
    third = 3 * SHARD_ROWS + 96
    first = jnp.stack([a[0:HALF_IN], a[SHARD_ROWS:SHARD_ROWS + HALF_IN],
                       a[2 * SHARD_ROWS:2 * SHARD_ROWS + HALF_IN], a[third:third + HALF_IN]])
    second = jnp.stack([
        jnp.concatenate([a[HALF_IN:SHARD_ROWS], gap]),
        jnp.concatenate([a[SHARD_ROWS + HALF_IN:2 * SHARD_ROWS], gap]),
        jnp.concatenate([a[2 * SHARD_ROWS + HALF_IN:4480], a[4544:4576], a[4608:third], gap]),
        jnp.concatenate([a[third + HALF_IN:], gap])])
    mine_is_first = core[0] == 0
    own = jnp.where(mine_is_first, first, second)
    other = jnp.where(mine_is_first, second, first).astype(BF16)
    return own, other


def _split_rest(dw_uq_pad_t, dw_ukv, dw_mem, dw_out):
    dw_uq_t = dw_uq_pad_t.reshape(MLA_HEADS, LANES, 256)[:, :MLA_QK_DIM].reshape(4, ROWS_UQ, 1024)
    parts = [dw_uq_t, dw_ukv.reshape(128, 4, 256).transpose(1, 0, 2).reshape(4, ROWS_UKV, 1024),
             dw_mem.reshape(4, ROWS_MEM, 1024), dw_out.reshape(4, ROWS_OUT, 1024)]
    return jnp.pad(jnp.concatenate(parts, axis=1), ((0, 0), (0, ROWS_REST - ROWS_USED), (0, 0)))


def _rope_consts(rot, first, period):
    half = rot // 2
    inv_freq = np.float32(ROPE_THETA) ** (-(np.arange(0, rot, 2, dtype=np.float32) / np.float32(rot)))
    lane = np.arange(LANES) % period - first
    in_rot = (lane >= 0) & (lane < rot)
    out = np.zeros((8, LANES), np.float32)
    out[0] = np.where(in_rot, inv_freq[np.clip(lane, 0, rot - 1) % half], 0.0)
    out[1] = in_rot & (lane < half)
    out[2] = in_rot & (lane >= half)
    return jnp.asarray(out)


def _band_bias(s):
    nblk = s // BAND_Q
    starts = np.array([_band_start(i, s) for i in range(nblk)])
    uq = (np.arange(nblk)[:, None] * BAND_Q + np.arange(BAND_Q)[None, :])[:, :, None]
    uk = (starts[:, None] + np.arange(BAND_WIN)[None, :])[:, None, :]
    tiles, index, seen = [], [], {}
    for _, d in DILATED:
        length = s // d
        ok = (uq // length == uk // length) & (np.abs(uq - uk) <= 64)
        row = []
        for i in range(nblk):
            key = ok[i].tobytes()
            if key not in seen:
                seen[key] = len(tiles)
                tiles.append(np.where(ok[i], 0.0, NEG_INF).astype(np.float32))
            row.append(seen[key])
        index.append(row)
    return jnp.asarray(np.stack(tiles, axis=0)), index


def _forward_backward(h, h32, proj, trig, rope_consts, x, mem, target, weights, gains):
    w_uq_pad_t, w_ukv, w_mem, w_out = weights
    g_emb, b_emb, g_cq, g_ckv, g_out_a, g_out_b, g_out_m, g_post, b_post = gains
    nb, s, d = x.shape
    t = nb * s
    x2 = x.reshape(t, d)
    mem2 = mem.reshape(nb * N_MEM, d)
    tgt2 = target.reshape(t, d)
    rope_a, rope_b = rope_consts
    bias, bias_index = _band_bias(s)
    scales = (0.125, MLA_QK_DIM ** -0.5, 128 ** -0.5)

    qa, ka, va, qb, kb, vb, qm, cqn, ckvn = _prep(proj, trig, w_uq_pad_t, w_ukv, g_cq, g_ckv, rope_a, rope_b, scales)
    mkv = _mm(mem2, w_mem, BF16, nb * N_MEM, 1024, 1024, "mem_kv")

    cfg_b = dict(nb=nb, s=s, sk=s, heads=8, voff=0, bq=256)
    cfg_m = dict(nb=nb, s=s, sk=N_MEM, heads=4, hpb=2, voff=4, bq=1024)
    ya, lse_a, qkv_ordered = _dilated_fwd(qa, ka, va, bias, bias_index, nb=nb, s=s, name="attn_a_fwd")
    yb, lse_b = _attn_fwd(qb, kb, vb, name="attn_b_fwd", hpb=4, **cfg_b)
    ym, lse_m = _attn_fwd(qm, mkv, mkv, name="attn_m_fwd", **cfg_m)

    (y, dz, doa, dob, dom, dga, dgb, dgm, loss, dg_post, db_post, dg_a, dg_b, dg_m) = _post(
        h32, ya, yb, ym, proj, tgt2, w_out, g_out_a, g_out_b, g_out_m, g_post, b_post)

    dqa, dka, dva = _dilated_bwd(qa, ka, va, qkv_ordered, ya, doa, lse_a, bias, bias_index, nb=nb, s=s, scale=scales[0],
                                 name="attn_a_bwd")
    dqb, dkb, dvb = _attn_bwd(qb, kb, vb, yb, dob, lse_b, name="attn_b_bwd", scale=scales[1], hpb=4, **cfg_b)
    dqm, dmk, dmv = _attn_bwd(qm, mkv, mkv, ym, dom, lse_m, name="attn_m_bwd", scale=scales[2], **cfg_m)
    dmkv = jnp.concatenate([dmk, dmv], axis=1)

    dproj, dqf, dkv, dg_cq, dg_ckv = _prep_bwd(
        dqa, dka, dva, dqb, dkb, dvb, dqm, dga, dgb, dgm, proj, trig, w_uq_pad_t, w_ukv, g_cq, g_ckv, rope_a, rope_b)

    small_rows = (dg_cq, dg_ckv, loss, dg_a, dg_b, dg_m, dg_post, db_post)
    return (dproj, h, y, dz, dqf, cqn, ckvn, dkv, mem2, dmkv), x2, small_rows


def _weight_grads(operands, core):
    dproj, h, y, dz, dqf, cqn, ckvn, dkv, mem2, dmkv = operands
    dw_in_arr_t = _mm(dproj, h, F32, 1024, 1024, 4096, "dw_in", mode="tn")
    g_in, g_in_other = _split_in(dw_in_arr_t, core)
    dw_out, r_in = _mm(y, dz, F32, 1024, 1024, 2048, "dw_out", mode="tn", ride=_half_to_sibling(g_in_other))
    dw_uq_pad_t = _mm(dqf, cqn, F32, 1024, 256, 4096, "dw_uq", mode="tn")
    dw_ukv = _mm(ckvn, dkv, F32, 128, 1024, 4096, "dw_ukv", mode="tn")
    dw_mem = _mm(mem2, dmkv, F32, 1024, 1024, mem2.shape[0], "dw_mem", mode="tn")
    g_rest = _split_rest(dw_uq_pad_t, dw_ukv, dw_mem, dw_out)
    sf_in, sb_in, r_rest = _core_sum(g_in, r_in, core, HALF_IN, HALF_IN // 2, "core_sum_in",
                                     ride=_half_to_sibling(g_rest.reshape(4, 2, HALF_REST, 1024)))
    sf_rest, sb_rest = _core_sum(g_rest, r_rest, core, HALF_REST, HALF_REST, "core_sum_rest")
    return sf_in, sb_in, sf_rest, sb_rest


def _small_block(dg_emb, db_emb, small_rows):
    dg_cq, dg_ckv, loss, dg_a, dg_b, dg_m, dg_post, db_post = small_rows
    row2 = jnp.concatenate([dg_cq, dg_ckv, loss, jnp.zeros((1, 512), F32)], axis=1)
    return jnp.concatenate([dg_emb, db_emb, row2, dg_a, jnp.concatenate([dg_b, dg_m], axis=1), dg_post, db_post,
                            jnp.zeros((1, 1024), F32)], axis=0)


def _pack_small(g_emb, b_emb, g_cq, g_ckv, g_out_a, g_out_b, g_out_m, g_post, b_post):
    row2 = jnp.concatenate([g_cq.reshape(1, -1), g_ckv.reshape(1, -1), jnp.zeros((1, 640), F32)], axis=1)
    return jnp.concatenate([g_emb.reshape(1, -1), b_emb.reshape(1, -1), row2, g_out_a.reshape(1, -1),
                            jnp.concatenate([g_out_b.reshape(1, -1), g_out_m.reshape(1, -1)], axis=1),
                            g_post.reshape(1, -1), b_post.reshape(1, -1), jnp.zeros((1, 1024), F32)], axis=0)


def kernel(x, mem, positions, g_emb, b_emb, w_in, g_cq, g_ckv, w_uq, w_ukv, w_mem_kv, g_out_a, g_out_b, g_out_m, w_out, g_post, b_post, loss_target, m_g_emb, m_b_emb, m_w_in, m_g_cq, m_g_ckv, m_w_uq, m_w_ukv, m_w_mem_kv, m_g_out_a, m_g_out_b, m_g_out_m, m_w_out, m_g_post, m_b_post, v_g_emb, v_b_emb, v_w_in, v_g_cq, v_g_ckv, v_w_uq, v_w_ukv, v_w_mem_kv, v_g_out_a, v_g_out_b, v_g_out_m, v_w_out, v_g_post, v_b_post):
    w_rest = _pack_rest(w_uq, w_ukv, w_mem_kv, w_out)
    w_in_t = w_in[0].T
    w_in_b = jnp.pad(w_in_t.astype(BF16), ((0, ROWS_IN - SHARD_ROWS), (0, 0)))
    gains = (g_emb.reshape(1, -1), b_emb.reshape(1, -1), g_cq, g_ckv, g_out_a, g_out_b, g_out_m, g_post, b_post)
    rope_consts = (_rope_consts(16, 0, 64), _rope_consts(32, 64, 128))
    h, h32, trig, gathered_in = _ln_fwd(x.reshape(-1, D_MODEL), gains[0], gains[1],
                                        positions.reshape(-1, 1).astype(F32), *rope_consts,
                                        ride=_gather_ride(w_in_b.reshape(2, HALF_IN, 1024), spread=False))
    w_in_arr_t = _arranged_w_in(gathered_in.reshape(4, ROWS_IN, 1024))
    proj, gathered_rest = _mm(h, w_in_arr_t, F32, 1024, 2048, 1024, "in_proj", mode="nt",
                              ride=_gather_ride(w_rest.astype(BF16).reshape(2, HALF_REST, 1024), spread=True))
    weights = _rest_weights(gathered_rest.reshape(4, ROWS_REST, 1024))
    operands, x2, small_rows = _forward_backward(h, h32, proj, trig, rope_consts, x, mem, loss_target, weights,
                                                 gains)

    core = lax.axis_index("c").astype(jnp.int32).reshape(1)
    chip = (2 * lax.axis_index("x") + lax.axis_index("y")).astype(jnp.int32).reshape(1)
    sf_in, sb_in, sf_rest, sb_rest = _weight_grads(operands, core)
    grad_x, dg_emb, db_emb, rb_in, rb_rest = _dh_scatter(operands[0], w_in_arr_t, x2, operands[3], gains[0],
                                                         sb_in, sb_rest)
    gh_in = _chip_sum(sf_in, rb_in, chip, HALF_IN, HALF_IN // 2, "chip_sum_in")
    gh_rest = _chip_sum(sf_rest, rb_rest, chip, HALF_REST, HALF_REST, "chip_sum_rest")
    grad_in, grad_rest = _join_halves(gh_in, gh_rest)
    grad_in = grad_in.reshape(ROWS_IN, 1024)
    grad_rest = grad_rest.reshape(ROWS_REST, 1024)

    big_in = _adamw(grad_in, w_in_t, m_w_in[0].T, v_w_in[0].T, SHARD_ROWS // 3, "adamw_in")
    def rest_parts(a_uq, a_ukv, a_mem, a_out):
        return [a_uq[0].T.reshape(ROWS_UQ, 1024), a_ukv.reshape(ROWS_UKV, 1024), a_mem[0], a_out[0]]

    uq, ukv, wmem, wout = _adamw_pieces(
        grad_rest, rest_parts(w_uq, w_ukv, w_mem_kv, w_out), rest_parts(m_w_uq, m_w_ukv, m_w_mem_kv, m_w_out),
        rest_parts(v_w_uq, v_w_ukv, v_w_mem_kv, v_w_out), REST_PIECES, "adamw_rest")
    small_sum = _allreduce_small(_small_block(dg_emb, db_emb, small_rows))
    sm = _adamw_pieces(
        small_sum,
        _pack_small(g_emb, b_emb, g_cq, g_ckv, g_out_a, g_out_b, g_out_m, g_post, b_post),
        _pack_small(m_g_emb, m_b_emb, m_g_cq, m_g_ckv, m_g_out_a, m_g_out_b, m_g_out_m, m_g_post, m_b_post),
        _pack_small(v_g_emb, v_b_emb, v_g_cq, v_g_ckv, v_g_out_a, v_g_out_b, v_g_out_m, v_g_post, v_b_post),
        SMALL_PIECES, "adamw_small")
    loss = small_sum[2, 384]

    def ordered(kind):
        s_gemb, s_bemb, s_gcq, s_gckv, s_ga, s_gb, s_gm, s_gpost, s_bpost = [piece[kind] for piece in sm]
        return [s_gemb.reshape(-1), s_bemb.reshape(-1), big_in[kind].T[None], s_gcq, s_gckv,
                uq[kind].reshape(192, 256).T[None], ukv[kind].reshape(1, 128, 256), wmem[kind][None], s_ga, s_gb,
                s_gm, wout[kind][None], s_gpost, s_bpost]

    return (loss, grad_x.reshape(x.shape), *ordered(0), *ordered(1), *ordered(2), *ordered(3))
```

```python
import functools
import math

import jax
import jax.numpy as jnp
import numpy as np
from jax import lax
from jax.experimental import pallas as pl
from jax.experimental.pallas import tpu as pltpu

F32 = jnp.float32
BF16 = jnp.bfloat16
MESH = pl.DeviceIdType.MESH
ANY = pl.BlockSpec(memory_space=pl.ANY)
IN_VMEM = pl.BlockSpec(memory_space=pltpu.VMEM)

D_MODEL = 1024
A_WIDTH = 1024
MLA_HEADS = 8
MLA_Q_RANK = 256
MLA_KV_RANK = 128
MLA_QK_DIM = 96
MEM_WIDTH = 512
N_MEM = 256
ROPE_THETA = 500000.0
NORM_EPS = 1e-5
NEG_INF = -1e30
DEEPNORM_ALPHA = 2.0 ** 0.25
DILATED = ((64, 1), (256, 4), (1024, 16))

ADAM_LR = 0.001
ADAM_B1 = 0.9
ADAM_B2 = 0.999
ADAM_EPS = 1e-08
ADAM_WD = 0.01
ADAM_STEP = 10

LANES = 128
VMEM_LIMIT = 56 * 1024 * 1024
LOG2E = math.log2(math.e)
LN2 = math.log(2.0)

PROJ_W = 6144
COL_CQ = 4096
COL_BG = 4608
COL_MQ = 5120
COL_MG = 5632

SHARD_ROWS = 1512
ROWS_IN = 1536
ROWS_UQ, ROWS_UKV, ROWS_MEM, ROWS_OUT = 48, 32, 256, 512
ROWS_USED = ROWS_UQ + ROWS_UKV + ROWS_MEM + ROWS_OUT
ROWS_REST = 864
HALF_IN = ROWS_IN // 2
HALF_REST = ROWS_REST // 2
REST_PIECES = ((0, 48, 0, 1024), (48, 80, 0, 1024), (80, 336, 0, 1024), (336, 848, 0, 1024))
SMALL_PIECES = ((0, 1, 0, 1024), (1, 2, 0, 1024), (2, 3, 0, 256), (2, 3, 256, 384), (3, 4, 0, 1024), (4, 5, 0, 512),
                (4, 5, 512, 1024), (5, 6, 0, 1024), (6, 7, 0, 1024))


def _params(sem=None, vmem=VMEM_LIMIT):
    return pltpu.CompilerParams(dimension_semantics=sem, vmem_limit_bytes=vmem)


def _dot(a, b):
    return jnp.dot(a, b, preferred_element_type=F32)


def _dot_nt(a, b):
    return lax.dot_general(a, b, (((1,), (1,)), ((), ())), preferred_element_type=F32)


def _dot_tn(a, b):
    return lax.dot_general(a, b, (((0,), (0,)), ((), ())), preferred_element_type=F32)


def _ln_hat(x):
    mu = jnp.mean(x, axis=-1, keepdims=True)
    xc = x - mu
    var = jnp.mean(xc * xc, axis=-1, keepdims=True)
    rstd = lax.rsqrt(var + NORM_EPS)
    return xc * rstd, rstd


def _ln_bwd_rows(dxh, xh, rstd):
    return rstd * (dxh - jnp.mean(dxh, axis=-1, keepdims=True) - xh * jnp.mean(dxh * xh, axis=-1, keepdims=True))


def _rms_hat(x, width):
    ms = jnp.sum(x * x, axis=-1, keepdims=True) * (1.0 / width)
    r = lax.rsqrt(ms + NORM_EPS)
    return x * r, r


def _rms_bwd(u, xh, r, width):
    return r * (u - xh * (jnp.sum(u * xh, axis=-1, keepdims=True) * (1.0 / width)))


def _colsum(v):
    return jnp.sum(v, axis=0, keepdims=True)


def _rope_tables(cos, sin, consts):
    return cos, sin * consts[2:3, :], -sin * consts[1:2, :]


def _rope(x, tables, half, inverse=False):
    c, s_up, s_dn = tables
    if inverse:
        s_up, s_dn = -s_up, -s_dn
    return x * c + pltpu.roll(x, half, 1) * s_up + pltpu.roll(x, LANES - half, 1) * s_dn


def _ln_fwd(x, g, b, pos, rope_a, rope_b, tm=512, ride=None):
    t, d = x.shape
    n_in = len(ride.args) if ride else 0
    n_out = len(ride.out_shapes) if ride else 0
    steps = t // tm

    def body(x_ref, g_ref, b_ref, pos_ref, ra_ref, rb_ref, *rest):
        h_ref, h32_ref, trig_ref = rest[n_in:n_in + 3]
        if ride:
            i = pl.program_id(0)
            ride.run(i, steps, rest[:n_in], rest[n_in + 3:n_in + 3 + n_out], rest[n_in + 3 + n_out:])
        xh, _ = _ln_hat(x_ref[...])
        h = xh * g_ref[...] + b_ref[...]
        h32_ref[...] = h
        h_ref[...] = h.astype(BF16)
        for j, consts in enumerate((ra_ref, rb_ref)):
            ang = pos_ref[...] * consts[0:1, :]
            trig_ref[:, 2 * j * LANES:(2 * j + 1) * LANES] = jnp.cos(ang)
            trig_ref[:, (2 * j + 1) * LANES:(2 * j + 2) * LANES] = jnp.sin(ang)

    row = pl.BlockSpec((1, d), lambda i: (0, 0))
    tile = pl.BlockSpec((tm, d), lambda i: (i, 0))
    consts = pl.BlockSpec((8, LANES), lambda i: (0, 0))
    trig_tile = pl.BlockSpec((tm, 4 * LANES), lambda i: (i, 0))
    in_specs = [tile, row, row, pl.BlockSpec((tm, 1), lambda i: (i, 0)), consts, consts]
    shapes = (jax.ShapeDtypeStruct((t, d), BF16), jax.ShapeDtypeStruct((t, d), F32),
              jax.ShapeDtypeStruct((t, 4 * LANES), F32))
    if not ride:
        return pl.pallas_call(
            body, name="ln_fwd", grid=(steps,), out_shape=shapes, in_specs=in_specs,
            out_specs=(tile, tile, trig_tile), compiler_params=_params(("parallel",)),
        )(x, g, b, pos, rope_a, rope_b)
    return pl.pallas_call(
        body, name="ln_fwd", grid=(steps,),
        out_shape=(*shapes, *ride.out_shapes),
        in_specs=in_specs + ride.in_specs, out_specs=(tile, tile, trig_tile) + (ANY,) * n_out,
        scratch_shapes=ride.scratch(),
        compiler_params=_params(("arbitrary",)),
    )(x, g, b, pos, rope_a, rope_b, *ride.args)


class _Ride:
    def __init__(self, args, out_shapes, sem_counts, plan, in_specs=None, spread=True):
        self.args, self.out_shapes, self.plan = list(args), list(out_shapes), plan
        self.sem_counts = sem_counts
        self.in_specs = in_specs or [ANY] * len(self.args)
        self.spread = spread

    def scratch(self):
        return [pltpu.SemaphoreType.DMA((n,)) for n in self.sem_counts]

    def run(self, step, total, in_refs, out_refs, sems):
        count = len(self.plan(in_refs, out_refs, *sems))
        at = [(k * (total - 1)) // (count - 1) if self.spread or k == 0 else total - 1 for k in range(count)]
        for when in sorted(set(at)):
            @pl.when(step == when)
            def _(when=when):
                stages = self.plan(in_refs, out_refs, *sems)
                for k in range(count):
                    if at[k] == when:
                        stages[k]()


def _mm(a, b, out_dtype, tm, tn, tk, name, mode="nn", ride=None):
    if mode == "tn":
        k, m = a.shape
    else:
        m, k = a.shape
    n = b.shape[0] if mode == "nt" else b.shape[1]
    nk = k // tk
    nj, ni = n // tn, m // tm
    n_in = len(ride.args) if ride else 0
    n_out = len(ride.out_shapes) if ride else 0

    def body(a_ref, b_ref, *rest):
        o_ref = rest[n_in]
        acc_ref = rest[n_in + 1 + n_out]
        if ride:
            j, i, kk = pl.program_id(0), pl.program_id(1), pl.program_id(2)
            ride.run((j * ni + i) * nk + kk, nj * ni * nk, rest[:n_in], rest[n_in + 1:n_in + 1 + n_out],
                     rest[n_in + 2 + n_out:])
        av = a_ref[...].astype(BF16)
        bv = b_ref[...].astype(BF16)
        part = _dot_tn(av, bv) if mode == "tn" else _dot_nt(av, bv) if mode == "nt" else _dot(av, bv)
        if nk == 1:
            o_ref[...] = part.astype(out_dtype)
        else:
            kk = pl.program_id(2)

            @pl.when(kk == 0)
            def _():
                acc_ref[...] = part

            @pl.when(kk > 0)
            def _():
                acc_ref[...] += part

            @pl.when(kk == nk - 1)
            def _():
                o_ref[...] = acc_ref[...].astype(out_dtype)

    a_spec = (pl.BlockSpec((tk, tm), lambda j, i, kk: (kk, i)) if mode == "tn"
              else pl.BlockSpec((tm, tk), lambda j, i, kk: (i, kk)))
    b_spec = (pl.BlockSpec((tn, tk), lambda j, i, kk: (j, kk)) if mode == "nt"
              else pl.BlockSpec((tk, tn), lambda j, i, kk: (kk, j)))
    o_spec = pl.BlockSpec((tm, tn), lambda j, i, kk: (i, j))
    o_shape = jax.ShapeDtypeStruct((m, n), out_dtype)
    if not ride:
        return pl.pallas_call(
            body, name=name, grid=(nj, ni, nk), out_shape=o_shape, in_specs=[a_spec, b_spec], out_specs=o_spec,
            scratch_shapes=[pltpu.VMEM((tm, tn), F32)],
            compiler_params=_params(("parallel", "parallel", "arbitrary")),
        )(a, b)
    return pl.pallas_call(
        body, name=name, grid=(nj, ni, nk),
        out_shape=(o_shape, *ride.out_shapes),
        in_specs=[a_spec, b_spec] + ride.in_specs,
        out_specs=(o_spec,) + (ANY,) * n_out,
        scratch_shapes=[pltpu.VMEM((tm, tn), F32)] + ride.scratch(),
        compiler_params=_params(("arbitrary", "arbitrary", "arbitrary")),
    )(a, b, *ride.args)


def _prep(proj, trig, w_uq, w_ukv, g_cq, g_ckv, rope_a, rope_b, scales, tm=512):
    t = proj.shape[0]
    sc_a, sc_b, sc_m = (s * LOG2E for s in scales)

    def body(aq_ref, ak_ref, av_ref, bs_ref, mq_ref, trig_ref, wuq_ref, wukv_ref, gcq_ref, gckv_ref,
             ra_ref, rb_ref, qa_ref, ka_ref, va_ref, qb_ref, kb_ref, vb_ref, qm_ref, cqn_ref, ckvn_ref):
        ta = _rope_tables(trig_ref[:, 0:LANES], trig_ref[:, LANES:2 * LANES], ra_ref[...])
        tb = _rope_tables(trig_ref[:, 2 * LANES:3 * LANES], trig_ref[:, 3 * LANES:4 * LANES], rb_ref[...])
        for j in range(A_WIDTH // LANES):
            sl = slice(j * LANES, (j + 1) * LANES)
            qa_ref[:, sl] = (_rope(aq_ref[:, sl], ta, 8) * sc_a).astype(BF16)
            ka_ref[:, sl] = _rope(ak_ref[:, sl], ta, 8).astype(BF16)
        va_ref[...] = av_ref[...].astype(BF16)
        qm_ref[...] = (mq_ref[...] * sc_m).astype(BF16)

        cq_hat, _ = _rms_hat(bs_ref[:, 0:MLA_Q_RANK], MLA_Q_RANK)
        cqn = (cq_hat * gcq_ref[...]).astype(BF16)
        cqn_ref[...] = cqn
        ckv_hat, _ = _rms_hat(bs_ref[:, MLA_Q_RANK:MLA_Q_RANK + MLA_KV_RANK], MLA_KV_RANK)
        ckvn = (ckv_hat * gckv_ref[...]).astype(BF16)
        ckvn_ref[...] = ckvn
        qfull = _dot_nt(cqn, wuq_ref[...])
        kv = _dot(ckvn, wukv_ref[...])
        kr = _rope(bs_ref[:, 384:512], tb, 16)
        lane = lax.broadcasted_iota(jnp.int32, (1, LANES), 1)
        low = lane < 64
        for h in range(MLA_HEADS):
            sl = slice(h * LANES, (h + 1) * LANES)
            qb_ref[:, sl] = (_rope(qfull[:, sl], tb, 16) * sc_b).astype(BF16)
            kb_ref[:, sl] = jnp.where(low, kv[:, sl], kr).astype(BF16)
            vb_ref[:, sl] = jnp.where(low, 0.0, kv[:, sl]).astype(BF16)

    def col(width, idx):
        return pl.BlockSpec((tm, width), lambda i: (i, idx))

    def full(shape):
        return pl.BlockSpec(shape, lambda i: (0, 0))

    wide = jax.ShapeDtypeStruct((t, 1024), BF16)
    return pl.pallas_call(
        body, name="prep", grid=(t // tm,),
        out_shape=(wide, wide, wide, wide, wide, wide,
                   jax.ShapeDtypeStruct((t, MEM_WIDTH), BF16),
                   jax.ShapeDtypeStruct((t, MLA_Q_RANK), BF16),
                   jax.ShapeDtypeStruct((t, MLA_KV_RANK), BF16)),
        in_specs=[col(1024, 0), col(1024, 1), col(1024, 2), col(512, COL_CQ // 512), col(512, COL_MQ // 512),
                  pl.BlockSpec((tm, 4 * LANES), lambda i: (i, 0)),
                  full((1024, MLA_Q_RANK)), full((MLA_KV_RANK, 1024)),
                  full((1, MLA_Q_RANK)), full((1, MLA_KV_RANK)), full((8, LANES)), full((8, LANES))],
        out_specs=(col(1024, 0),) * 6 + (col(MEM_WIDTH, 0), col(MLA_Q_RANK, 0), col(MLA_KV_RANK, 0)),
        compiler_params=_params(("parallel",)),
    )(proj, proj, proj, proj, proj, trig, w_uq, w_ukv, g_cq, g_ckv, rope_a, rope_b)


def _attn_fwd(q, k, v, *, nb, s, sk, heads, hpb, voff, bq, name):
    nq = s // bq
    width = hpb * LANES
    vblk = voff // hpb

    def body(q_ref, k_ref, v_ref, o_ref, lse_ref):
        for h in range(hpb):
            sl = slice(h * LANES, (h + 1) * LANES)
            sc = _dot_nt(q_ref[:, sl], k_ref[:, sl])
            m = jnp.max(sc, axis=1, keepdims=True)
            p = jnp.exp2(sc - m)
            l = jnp.sum(p, axis=1, keepdims=True)
            o_ref[:, sl] = _dot(p.astype(BF16), v_ref[:, sl]) / l
            lse_ref[:, sl] = jnp.broadcast_to(m + jnp.log(l) * LOG2E, (bq, LANES))

    out = jax.ShapeDtypeStruct((nb * s, heads * LANES), F32)
    ospec = pl.BlockSpec((bq, width), lambda b, i, g: (b * nq + i, g))
    return pl.pallas_call(
        body, name=name, grid=(nb, nq, heads // hpb),
        out_shape=(out, out),
        in_specs=[ospec, pl.BlockSpec((sk, width), lambda b, i, g: (b, g)),
                  pl.BlockSpec((sk, width), lambda b, i, g: (b, vblk + g))],
        out_specs=(ospec, ospec),
        compiler_params=_params(("parallel", "parallel", "parallel")),
    )(q, k, v)


def _attn_bwd(q, k, v, o, do, lse, *, nb, s, sk, heads, hpb, voff, scale, bq, name):
    nq = s // bq
    width = hpb * LANES
    vblk = voff // hpb

    def body(q_ref, k_ref, v_ref, o_ref, do_ref, lse_ref, dq_ref, dk_ref, dv_ref, dk_acc, dv_acc):
        i = pl.program_id(2)

        @pl.when(i == 0)
        def _():
            dk_acc[...] = jnp.zeros_like(dk_acc)
            dv_acc[...] = jnp.zeros_like(dv_acc)

        for h in range(hpb):
            sl = slice(h * LANES, (h + 1) * LANES)
            qh = q_ref[:, sl]
            kk = k_ref[:, sl]
            doh = do_ref[:, sl]
            delta = jnp.sum(doh.astype(F32) * o_ref[:, sl], axis=1, keepdims=True)
            p = jnp.exp2(_dot_nt(qh, kk) - lse_ref[:, h * LANES:h * LANES + 1])
            ds = (p * (_dot_nt(doh, v_ref[:, sl]) - delta)).astype(BF16)
            dq_ref[:, sl] = (_dot(ds, kk) * scale).astype(BF16)
            dk_acc[:, sl] += _dot_tn(ds, qh)
            dv_acc[:, sl] += _dot_tn(p.astype(BF16), doh)

        @pl.when(i == nq - 1)
        def _():
            dk_ref[...] = (dk_acc[...] * LN2).astype(BF16)
            dv_ref[...] = dv_acc[...].astype(BF16)

    qspec = pl.BlockSpec((bq, width), lambda b, g, i: (b * nq + i, g))
    kv_spec = pl.BlockSpec((sk, width), lambda b, g, i: (b, g))
    dq_shape = jax.ShapeDtypeStruct((nb * s, heads * LANES), BF16)
    dkv_shape = jax.ShapeDtypeStruct((nb * sk, heads * LANES), BF16)
    return pl.pallas_call(
        body, name=name, grid=(nb, heads // hpb, nq),
        out_shape=(dq_shape, dkv_shape, dkv_shape),
        in_specs=[qspec, kv_spec, pl.BlockSpec((sk, width), lambda b, g, i: (b, vblk + g)), qspec, qspec, qspec],
        out_specs=(qspec, kv_spec, kv_spec),
        scratch_shapes=[pltpu.VMEM((sk, width), F32), pltpu.VMEM((sk, width), F32)],
        compiler_params=_params(("parallel", "parallel", "arbitrary")),
    )(q, k, v, o, do, lse)


BAND_Q = 128
BAND_WIN = 256


def _band_start(i, s):
    return min(max(i * BAND_Q - 64, 0), s - BAND_WIN)


def _to_pattern_order(src_ref, dst_ref, stage_ref, s, d):
    length = s // d
    stage_ref[...] = src_ref[...].astype(F32)
    for r in range(d):
        dst_ref[r * length:(r + 1) * length, :] = stage_ref[pl.ds(r, length, stride=d), :].astype(dst_ref.dtype)


def _dilated_fwd(q, k, v, bias, bias_index, *, nb, s, name):
    nblk = s // BAND_Q
    npat = len(DILATED)

    def body(q_ref, k_ref, v_ref, bias_ref, o_ref, lse_ref, *rest):
        ordered = rest[:3 * (npat - 1)]
        stage_ref, op_ref, lp_ref, on_ref, ln_ref = rest[3 * (npat - 1):]
        lane = lax.broadcasted_iota(jnp.int32, (1, LANES), 1)
        first = lane < 64
        for p, (_, d) in enumerate(DILATED):
            if d == 1:
                qs, ks, vs = q_ref, k_ref, v_ref
            else:
                qs, ks, vs = ordered[3 * (p - 1):3 * p]
                for src, dst in ((q_ref, qs), (k_ref, ks), (v_ref, vs)):
                    _to_pattern_order(src, dst, stage_ref, s, d)
            for i in range(nblk):
                u0 = i * BAND_Q
                st = _band_start(i, s)
                qi = qs[u0:u0 + BAND_Q, :]
                kw = ks[st:st + BAND_WIN, :]
                vw = vs[st:st + BAND_WIN, :]
                zero = jnp.zeros_like(qi)
                q2 = jnp.concatenate([jnp.where(first, qi, zero), jnp.where(first, zero, qi)], axis=0)
                sc = _dot_nt(q2, kw)
                b = bias_ref[bias_index[p][i]]
                halves = []
                for h in range(2):
                    sh = sc[h * BAND_Q:(h + 1) * BAND_Q] + b
                    m = jnp.max(sh, axis=1, keepdims=True)
                    pr = jnp.exp2(sh - m)
                    l = jnp.sum(pr, axis=1, keepdims=True)
                    halves.append((pr.astype(BF16), l, m + jnp.log(l) * LOG2E))
                o2 = _dot(jnp.concatenate([halves[0][0], halves[1][0]], axis=0), vw)
                o_blk = jnp.where(first, o2[:BAND_Q] / halves[0][1], o2[BAND_Q:] / halves[1][1])
                lse_blk = jnp.where(first, jnp.broadcast_to(halves[0][2], (BAND_Q, LANES)),
                                    jnp.broadcast_to(halves[1][2], (BAND_Q, LANES)))
                op_ref[p, u0:u0 + BAND_Q, :] = o_blk
                lp_ref[p, u0:u0 + BAND_Q, :] = lse_blk
            if d > 1:
                length = s // d
                for r in range(d):
                    on_ref.at[p - 1][pl.ds(r, length, stride=d), :] = op_ref[p, r * length:(r + 1) * length, :]
                    ln_ref.at[p - 1][pl.ds(r, length, stride=d), :] = lp_ref[p, r * length:(r + 1) * length, :]
        lses = [lp_ref[0]] + [ln_ref[p] for p in range(npat - 1)]
        outs = [op_ref[0]] + [on_ref[p] for p in range(npat - 1)]
        m = functools.reduce(jnp.maximum, lses)
        ws = [jnp.exp2(l - m) for l in lses]
        den = functools.reduce(lambda a, c: a + c, ws)
        o_ref[...] = functools.reduce(lambda a, c: a + c, [w * o for w, o in zip(ws, outs)]) / den
        lse_ref[...] = m + jnp.log(den) * LOG2E

    blk = pl.BlockSpec((s, LANES), lambda b, g: (b, g))
    out = jax.ShapeDtypeStruct((nb * s, A_WIDTH), F32)
    copy = jax.ShapeDtypeStruct((nb * s, A_WIDTH), BF16)
    n_copies = 3 * (npat - 1)
    res = pl.pallas_call(
        body, name=name, grid=(nb, A_WIDTH // LANES),
        out_shape=(out, out) + (copy,) * n_copies,
        in_specs=[blk, blk, blk, pl.BlockSpec(bias.shape, lambda b, g: (0, 0, 0))],
        out_specs=(blk, blk) + (blk,) * n_copies,
        scratch_shapes=[pltpu.VMEM((s, LANES), F32), pltpu.VMEM((npat, s, LANES), F32),
                        pltpu.VMEM((npat, s, LANES), F32), pltpu.VMEM((npat - 1, s, LANES), F32),
                        pltpu.VMEM((npat - 1, s, LANES), F32)],
        compiler_params=_params(("parallel", "parallel")),
    )(q, k, v, bias)
    return res[0], res[1], res[2:]


def _dilated_bwd(q, k, v, ordered, o, do, lse, bias, bias_index, *, nb, s, scale, name):
    nblk = s // BAND_Q
    npat = len(DILATED)
    n_copies = 3 * (npat - 1)

    def body(q_ref, k_ref, v_ref, *rest):
        ordered_refs = rest[:n_copies]
        (o_ref, do_ref, lse_ref, bias_ref, dq_out, dk_out, dv_out, stage_ref, rs_ref, dop_ref, rsp_ref,
         dqp_ref, dkp_ref, dvp_ref, dq_ref, dk_ref, dv_ref, nat_ref) = rest[n_copies:]
        lane = lax.broadcasted_iota(jnp.int32, (1, LANES), 1)
        first = lane < 64
        prod = do_ref[...].astype(F32) * o_ref[...]
        d0 = jnp.sum(jnp.where(first, prod, 0.0), axis=1, keepdims=True)
        d1 = jnp.sum(jnp.where(first, 0.0, prod), axis=1, keepdims=True)
        delta = jnp.where(first, jnp.broadcast_to(d0, (s, LANES)), jnp.broadcast_to(d1, (s, LANES)))
        rs_ref[...] = jnp.where((lane & 32) == 0, lse_ref[...], delta)
        for p, (_, d) in enumerate(DILATED):
            length = s // d
            if d == 1:
                qs, ks, vs, dos, rss = q_ref, k_ref, v_ref, do_ref, rs_ref
                dqs, dks, dvs = dq_ref, dk_ref, dv_ref
            else:
                for src, dst in ((do_ref, dop_ref), (rs_ref, rsp_ref)):
                    _to_pattern_order(src, dst, stage_ref, s, d)
                qs, ks, vs = ordered_refs[3 * (p - 1):3 * p]
                dos, rss = dop_ref, rsp_ref
                dqs, dks, dvs = dqp_ref, dkp_ref, dvp_ref
            dks[...] = jnp.zeros((s, LANES), F32)
            dvs[...] = jnp.zeros((s, LANES), F32)
            for i in range(nblk):
                u0 = i * BAND_Q
                st = _band_start(i, s)
                qi = qs[u0:u0 + BAND_Q, :]
                doi = dos[u0:u0 + BAND_Q, :]
                kw = ks[st:st + BAND_WIN, :]
                vw = vs[st:st + BAND_WIN, :]
                zero = jnp.zeros_like(qi)
                q2 = jnp.concatenate([jnp.where(first, qi, zero), jnp.where(first, zero, qi)], axis=0)
                do2 = jnp.concatenate([jnp.where(first, doi, zero), jnp.where(first, zero, doi)], axis=0)
                sc = _dot_nt(q2, kw)
                dp = _dot_nt(do2, vw)
                b = bias_ref[bias_index[p][i]]
                rs_i = rss[u0:u0 + BAND_Q, :]
                ps, dss = [], []
                for h in range(2):
                    rows = slice(h * BAND_Q, (h + 1) * BAND_Q)
                    pr = jnp.exp2(sc[rows] + b - rs_i[:, 64 * h:64 * h + 1])
                    ps.append(pr.astype(BF16))
                    dss.append((pr * (dp[rows] - rs_i[:, 64 * h + 32:64 * h + 33])).astype(BF16))
                p2 = jnp.concatenate(ps, axis=0)
                ds2 = jnp.concatenate(dss, axis=0)
                dq2 = _dot(ds2, kw)
                dqs[u0:u0 + BAND_Q, :] = jnp.where(first, dq2[:BAND_Q], dq2[BAND_Q:]) * scale
                dks[st:st + BAND_WIN, :] += _dot_tn(ds2, q2)
                dvs[st:st + BAND_WIN, :] += _dot_tn(p2, do2)
            if d > 1:
                for j, src in enumerate((dqp_ref, dkp_ref, dvp_ref)):
                    for r in range(d):
                        nat_ref.at[p - 1, j][pl.ds(r, length, stride=d), :] = src[r * length:(r + 1) * length, :]

        def total(j, first_ref):
            return functools.reduce(lambda a, c: a + c, [first_ref[...]] + [nat_ref[p, j] for p in range(npat - 1)])

        dq_out[...] = total(0, dq_ref).astype(BF16)
        dk_out[...] = (total(1, dk_ref) * LN2).astype(BF16)
        dv_out[...] = total(2, dv_ref).astype(BF16)

    blk = pl.BlockSpec((s, LANES), lambda b, g: (b, g))
    out = jax.ShapeDtypeStruct((nb * s, A_WIDTH), BF16)
    f32_buf = pltpu.VMEM((s, LANES), F32)
    bf_buf = pltpu.VMEM((s, LANES), BF16)
    return pl.pallas_call(
        body, name=name, grid=(nb, A_WIDTH // LANES),
        out_shape=(out, out, out),
        in_specs=[blk] * (6 + n_copies) + [pl.BlockSpec(bias.shape, lambda b, g: (0, 0, 0))],
        out_specs=(blk, blk, blk),
        scratch_shapes=[f32_buf, f32_buf, bf_buf] + [f32_buf] * 7 + [pltpu.VMEM((npat - 1, 3, s, LANES), F32)],
        compiler_params=_params(("parallel", "parallel")),
    )(q, k, v, *ordered, o, do, lse, bias)


def _post(h32, ya, ybp, ym, proj, target, w_out, g_a, g_b, g_m, g_post, b_post, tm=256):
    t = h32.shape[0]

    def body(h_ref, ya_ref, yb_ref, ym_ref, ga_ref, gb_ref, gm_ref, tg_ref, wo_ref,
             goa_ref, gob_ref, gom_ref, gp_ref, bp_ref,
             y_ref, dz_ref, doa_ref, dob_ref, dom_ref, dga_ref, dgb_ref, dgm_ref,
             loss_ref, dgp_ref, dbp_ref, dgoa_ref, dgob_ref, dgom_ref):
        i = pl.program_id(0)

        @pl.when(i == 0)
        def _():
            for r in (loss_ref, dgp_ref, dbp_ref, dgoa_ref, dgob_ref, dgom_ref):
                r[...] = jnp.zeros_like(r)

        lane = lax.broadcasted_iota(jnp.int32, (1, LANES), 1)
        low = lane < 64
        h = h_ref[...]

        ybp_v = yb_ref[...]
        yb = jnp.concatenate(
            [jnp.where(low, pltpu.roll(ybp_v[:, 2 * j * LANES:(2 * j + 1) * LANES], 64, 1),
                       ybp_v[:, (2 * j + 1) * LANES:(2 * j + 2) * LANES]) for j in range(4)], axis=1)

        def gated(raw, gate, gain, width):
            xh, r = _rms_hat(raw, width)
            n = xh * gain
            sg = 1.0 / (1.0 + jnp.exp(-gate))
            return xh, r, n, sg, n * (gate * sg)

        gate_a, gate_b, gate_m = ga_ref[...], gb_ref[...], gm_ref[...]
        xh_a, r_a, n_a, sg_a, y_a = gated(ya_ref[...], gate_a, goa_ref[...], A_WIDTH)
        xh_b, r_b, n_b, sg_b, y_b = gated(yb, gate_b, gob_ref[...], 512)
        xh_m, r_m, n_m, sg_m, y_m = gated(ym_ref[...], gate_m, gom_ref[...], 512)
        y = jnp.concatenate([y_a, y_b, y_m], axis=1).astype(BF16)
        y_ref[...] = y
        z = DEEPNORM_ALPHA * h + _dot(y, wo_ref[...])
        zh, rstd = _ln_hat(z)
        err = zh * gp_ref[...] + bp_ref[...] - tg_ref[...]
        rows = jnp.sum(err * err, axis=1, keepdims=True)
        loss_ref[...] += jnp.broadcast_to(jnp.sum(rows, axis=0, keepdims=True) * (0.5 / D_MODEL), (1, LANES))
        dout = err * (1.0 / D_MODEL)
        dgp_ref[...] += _colsum(dout * zh)
        dbp_ref[...] += _colsum(dout)
        dz = _ln_bwd_rows(dout * gp_ref[...], zh, rstd)
        dz_ref[...] = dz
        dy = _dot_nt(dz.astype(BF16), wo_ref[...])

        def gated_bwd(dyg, xh, r, n, sg, gate, gain, width, dgain_ref):
            dn = dyg * (gate * sg)
            dgate = dyg * n * (sg * (1.0 + gate * (1.0 - sg)))
            dgain_ref[...] += _colsum(dn * xh)
            return _rms_bwd(dn * gain, xh, r, width), dgate

        dya, dgate_a = gated_bwd(dy[:, 0:1024], xh_a, r_a, n_a, sg_a, gate_a, goa_ref[...], A_WIDTH, dgoa_ref)
        dyb, dgate_b = gated_bwd(dy[:, 1024:1536], xh_b, r_b, n_b, sg_b, gate_b, gob_ref[...], 512, dgob_ref)
        dym, dgate_m = gated_bwd(dy[:, 1536:2048], xh_m, r_m, n_m, sg_m, gate_m, gom_ref[...], 512, dgom_ref)
        doa_ref[...] = dya.astype(BF16)
        dom_ref[...] = dym.astype(BF16)
        dga_ref[...] = dgate_a.astype(BF16)
        dgb_ref[...] = dgate_b.astype(BF16)
        dgm_ref[...] = dgate_m.astype(BF16)
        for j in range(4):
            blk = dyb[:, j * LANES:(j + 1) * LANES]
            dob_ref[:, 2 * j * LANES:(2 * j + 1) * LANES] = jnp.where(low, 0.0, pltpu.roll(blk, 64, 1)).astype(BF16)
            dob_ref[:, (2 * j + 1) * LANES:(2 * j + 2) * LANES] = jnp.where(low, 0.0, blk).astype(BF16)

    def col(width, idx):
        return pl.BlockSpec((tm, width), lambda i: (i, idx))

    def full(shape):
        return pl.BlockSpec(shape, lambda i: (0, 0))

    def acc(width):
        return jax.ShapeDtypeStruct((1, width), F32)

    return pl.pallas_call(
        body, name="post", grid=(t // tm,),
        out_shape=(jax.ShapeDtypeStruct((t, 2048), BF16), jax.ShapeDtypeStruct((t, 1024), F32),
                   jax.ShapeDtypeStruct((t, 1024), BF16), jax.ShapeDtypeStruct((t, 1024), BF16),
                   jax.ShapeDtypeStruct((t, 512), BF16),
                   jax.ShapeDtypeStruct((t, 1024), BF16), jax.ShapeDtypeStruct((t, 512), BF16),
                   jax.ShapeDtypeStruct((t, 512), BF16),
                   acc(LANES), acc(1024), acc(1024), acc(1024), acc(512), acc(512)),
        in_specs=[col(1024, 0), col(1024, 0), col(1024, 0), col(512, 0),
                  col(1024, 3), col(512, COL_BG // 512), col(512, COL_MG // 512), col(1024, 0),
                  full((2048, 1024)),
                  full((1, 1024)), full((1, 512)), full((1, 512)), full((1, 1024)), full((1, 1024))],
        out_specs=(col(2048, 0), col(1024, 0), col(1024, 0), col(1024, 0), col(512, 0),
                   col(1024, 0), col(512, 0), col(512, 0),
                   full((1, LANES)), full((1, 1024)), full((1, 1024)), full((1, 1024)), full((1, 512)),
                   full((1, 512))),
        compiler_params=_params(("arbitrary",)),
    )(h32, ya, ybp, ym, proj, proj, proj, target, w_out, g_a, g_b, g_m, g_post, b_post)


def _prep_bwd(dqa, dka, dva, dqb, dkb, dvb, dqm, dga, dgb, dgm, proj, trig, w_uq, w_ukv, g_cq, g_ckv,
              rope_a, rope_b, tm=512):
    t = proj.shape[0]

    def body(dqa_ref, dka_ref, dva_ref, dqb_ref, dkb_ref, dvb_ref, dqm_ref, dga_ref, dgb_ref, dgm_ref,
             bs_ref, trig_ref, wuq_ref, wukv_ref, gcq_ref, gckv_ref, ra_ref, rb_ref,
             dproj_ref, dqf_ref, dkv_ref, dgcq_ref, dgckv_ref):
        i = pl.program_id(0)

        @pl.when(i == 0)
        def _():
            dgcq_ref[...] = jnp.zeros_like(dgcq_ref)
            dgckv_ref[...] = jnp.zeros_like(dgckv_ref)

        ta = _rope_tables(trig_ref[:, 0:LANES], trig_ref[:, LANES:2 * LANES], ra_ref[...])
        tb = _rope_tables(trig_ref[:, 2 * LANES:3 * LANES], trig_ref[:, 3 * LANES:4 * LANES], rb_ref[...])
        for j in range(A_WIDTH // LANES):
            sl = slice(j * LANES, (j + 1) * LANES)
            dproj_ref[:, j * LANES:(j + 1) * LANES] = (
                _rope(dqa_ref[:, sl].astype(F32), ta, 8, inverse=True).astype(BF16))
            dproj_ref[:, 1024 + j * LANES:1024 + (j + 1) * LANES] = (
                _rope(dka_ref[:, sl].astype(F32), ta, 8, inverse=True).astype(BF16))
        dproj_ref[:, 2048:3072] = dva_ref[...]
        dproj_ref[:, 3072:4096] = dga_ref[...]

        lane = lax.broadcasted_iota(jnp.int32, (1, LANES), 1)
        low = lane < 64
        rope_lanes = (lane >= 64) & (lane < 96)
        dkr = jnp.zeros((tm, LANES), F32)
        for h in range(MLA_HEADS):
            sl = slice(h * LANES, (h + 1) * LANES)
            dqf_ref[:, sl] = _rope(dqb_ref[:, sl].astype(F32), tb, 16, inverse=True).astype(BF16)
            dk_h = dkb_ref[:, sl]
            dkv_ref[:, sl] = jnp.where(low, dk_h, dvb_ref[:, sl])
            dkr = dkr + jnp.where(rope_lanes, dk_h.astype(F32), 0.0)
        dkr = _rope(dkr, tb, 16, inverse=True)

        cq_hat, r_q = _rms_hat(bs_ref[:, 0:MLA_Q_RANK], MLA_Q_RANK)
        dcqn = _dot(dqf_ref[...], wuq_ref[...])
        dgcq_ref[...] += _colsum(dcqn * cq_hat)
        dproj_ref[:, COL_CQ:COL_CQ + 256] = _rms_bwd(dcqn * gcq_ref[...], cq_hat, r_q, MLA_Q_RANK).astype(BF16)
        ckv_hat, r_kv = _rms_hat(bs_ref[:, MLA_Q_RANK:MLA_Q_RANK + MLA_KV_RANK], MLA_KV_RANK)
        dckvn = _dot_nt(dkv_ref[...], wukv_ref[...])
        dgckv_ref[...] += _colsum(dckvn * ckv_hat)
        dproj_ref[:, COL_CQ + 256:COL_CQ + 384] = (
            _rms_bwd(dckvn * gckv_ref[...], ckv_hat, r_kv, MLA_KV_RANK).astype(BF16))
        dproj_ref[:, COL_CQ + 384:COL_CQ + 512] = dkr.astype(BF16)
        dproj_ref[:, COL_BG:COL_BG + 512] = dgb_ref[...]
        dproj_ref[:, COL_MQ:COL_MQ + 512] = dqm_ref[...]
        dproj_ref[:, COL_MG:COL_MG + 512] = dgm_ref[...]

    def col(width, idx):
        return pl.BlockSpec((tm, width), lambda i: (i, idx))

    def full(shape):
        return pl.BlockSpec(shape, lambda i: (0, 0))

    return pl.pallas_call(
        body, name="prep_bwd", grid=(t // tm,),
        out_shape=(jax.ShapeDtypeStruct((t, PROJ_W), BF16), jax.ShapeDtypeStruct((t, 1024), BF16),
                   jax.ShapeDtypeStruct((t, 1024), BF16),
                   jax.ShapeDtypeStruct((1, MLA_Q_RANK), F32), jax.ShapeDtypeStruct((1, MLA_KV_RANK), F32)),
        in_specs=[col(1024, 0)] * 6 + [col(512, 0), col(1024, 0), col(512, 0), col(512, 0),
                  col(512, COL_CQ // 512), pl.BlockSpec((tm, 4 * LANES), lambda i: (i, 0)),
                  full((1024, MLA_Q_RANK)), full((MLA_KV_RANK, 1024)),
                  full((1, MLA_Q_RANK)), full((1, MLA_KV_RANK)), full((8, LANES)), full((8, LANES))],
        out_specs=(col(PROJ_W, 0), col(1024, 0), col(1024, 0), full((1, MLA_Q_RANK)), full((1, MLA_KV_RANK))),
        compiler_params=_params(("arbitrary",)),
    )(dqa, dka, dva, dqb, dkb, dvb, dqm, dga, dgb, dgm, proj, trig, w_uq, w_ukv, g_cq, g_ckv, rope_a, rope_b)


def _adamw_math(gv, w, m, v):
    m_new = ADAM_B1 * m + (1.0 - ADAM_B1) * gv
    v_new = ADAM_B2 * v + (1.0 - ADAM_B2) * (gv * gv)
    m_hat = m_new / (1.0 - ADAM_B1 ** ADAM_STEP)
    v_hat = v_new / (1.0 - ADAM_B2 ** ADAM_STEP)
    return -ADAM_LR * (m_hat / (jnp.sqrt(v_hat) + ADAM_EPS) + ADAM_WD * w), m_new, v_new


def _adamw(g, w, m, v, tr, name):
    r, cols = w.shape

    def body(g_ref, w_ref, m_ref, v_ref, go_ref, d_ref, nm_ref, nv_ref):
        gv = g_ref[...]
        go_ref[...] = gv
        d_ref[...], nm_ref[...], nv_ref[...] = _adamw_math(gv, w_ref[...], m_ref[...], v_ref[...])

    tile = pl.BlockSpec((tr, cols), lambda i: (i, 0))
    shape = jax.ShapeDtypeStruct((r, cols), F32)
    return pl.pallas_call(
        body, name=name, grid=(r // tr,),
        out_shape=(shape,) * 4, in_specs=[tile] * 4, out_specs=(tile,) * 4,
        compiler_params=_params(("parallel",)),
    )(g, w, m, v)


def _adamw_pieces(g, w, m, v, pieces, name):
    n = len(pieces)
    per_piece = isinstance(w, (list, tuple))
    shapes = [jax.ShapeDtypeStruct((r1 - r0, c1 - c0), F32) for r0, r1, c0, c1 in pieces]
    args = (g, *w, *m, *v) if per_piece else (g, w, m, v)

    def body(g_ref, *refs):
        ins, outs = refs[:len(args) - 1], refs[len(args) - 1:]
        gv = g_ref[...]
        if not per_piece:
            results = (gv,) + _adamw_math(gv, ins[0][...], ins[1][...], ins[2][...])
        for p, (r0, r1, c0, c1) in enumerate(pieces):
            if per_piece:
                gp = gv[r0:r1, c0:c1]
                vals = (gp,) + _adamw_math(gp, ins[p][...], ins[n + p][...], ins[2 * n + p][...])
            else:
                vals = [full[r0:r1, c0:c1] for full in results]
            for kind, val in enumerate(vals):
                outs[kind * n + p][...] = val

    flat = pl.pallas_call(
        body, name=name, out_shape=tuple(shapes) * 4,
        in_specs=[IN_VMEM] * len(args), out_specs=tuple([IN_VMEM] * (4 * n)),
        compiler_params=_params(None),
    )(*args)
    return [[flat[kind * n + p] for kind in range(4)] for p in range(n)]


def _core_sum(g, recv, core, rows, tr, name, ride=None):
    cols = g.shape[2]
    nblk = rows // tr
    n_in = len(ride.args) if ride else 0
    n_out = len(ride.out_shapes) if ride else 0

    def body(c_ref, g_ref, r_ref, *rest):
        sf_ref, sb_ref = rest[n_in], rest[n_in + 1]
        if ride:
            j, i = pl.program_id(0), pl.program_id(1)
            ride.run(j * nblk + i, 4 * nblk, rest[:n_in], rest[n_in + 2:n_in + 2 + n_out],
                     rest[n_in + 2 + n_out:])
        tot = g_ref[...] + r_ref[...]
        sf_ref[...] = tot
        sb_ref[...] = tot.astype(BF16)

    half = pl.BlockSpec((None, tr, cols), lambda j, i, c_ref: (j, i, 0))
    shapes = (jax.ShapeDtypeStruct((4, rows, cols), F32), jax.ShapeDtypeStruct((4, rows, cols), BF16))
    return pl.pallas_call(
        body, name=name,
        grid_spec=pltpu.PrefetchScalarGridSpec(
            num_scalar_prefetch=1, grid=(4, nblk),
            in_specs=[pl.BlockSpec((None, tr, cols), lambda j, i, c_ref: (j, c_ref[0] * nblk + i, 0)), half]
            + (ride.in_specs if ride else []),
            out_specs=(half, half) + (ANY,) * n_out,
            scratch_shapes=ride.scratch() if ride else []),
        out_shape=shapes + tuple(ride.out_shapes if ride else ()),
        compiler_params=_params(("arbitrary", "arbitrary") if ride else ("parallel", "parallel")),
    )(core, g, recv, *(ride.args if ride else ()))


def _half_to_sibling(g4):
    def plan(in_refs, out_refs, send_sems, recv_sems):
        x, y, c = _position()
        cp = pltpu.make_async_remote_copy(
            src_ref=in_refs[0].at[:, 1 - c], dst_ref=out_refs[0], send_sem=send_sems.at[0],
            recv_sem=recv_sems.at[0], device_id=(x, y, 1 - c), device_id_type=MESH)

        def finish():
            cp.wait_recv()
            cp.wait_send()

        return cp.start, finish

    return _Ride([g4], [jax.ShapeDtypeStruct((4, g4.shape[2], 1024), F32)], (1, 1), plan)


def _gather_plan(src_ref, dst_ref, send_sems, recv_sems, local_sems):
    x, y, c = _position()
    me = 2 * x + y
    rows = src_ref.shape[1]
    cut = -(-rows // 32) * 16
    pieces = (pl.ds(0, cut), pl.ds(cut, rows - cut))
    local = pltpu.make_async_copy(src_ref, dst_ref.at[me], local_sems.at[0])

    def over_ici(sem, k, chip, t, src=None):
        where = dst_ref.at[chip, c, pieces[t]]
        return pltpu.make_async_remote_copy(
            src_ref=where if src is None else src, dst_ref=where, send_sem=send_sems.at[sem],
            recv_sem=recv_sems.at[sem], device_id=(x ^ (k >> 1), y ^ (k & 1), c), device_id_type=MESH)

    def mine_to(k, t):
        return over_ici(2 * (k - 1) + t, k, me, t, src=src_ref.at[c, pieces[t]])

    def from_neighbour(k, t):
        return over_ici(2 * (k - 1) + t, k, me ^ k, t)

    def to_sibling(k, half):
        piece = dst_ref.at[me ^ k, half]
        return pltpu.make_async_remote_copy(
            src_ref=piece, dst_ref=piece, send_sem=send_sems.at[5 + k], recv_sem=recv_sems.at[5 + k],
            device_id=(x, y, 1 - c), device_id_type=MESH)

    sends = [mine_to(2, 0), mine_to(1, 1), mine_to(2, 1), mine_to(1, 0)]
    onward = [over_ici(4, 1, me ^ 2, 0), over_ici(5, 2, me ^ 1, 1)]

    def start():
        local.start()
        for cp in sends:
            cp.start()

    def pass_on():
        from_neighbour(2, 0).wait_recv()
        onward[0].start()
        from_neighbour(1, 1).wait_recv()
        onward[1].start()

    def to_other_core():
        from_neighbour(2, 1).wait_recv()
        to_sibling(2, c).start()
        from_neighbour(1, 0).wait_recv()
        to_sibling(1, c).start()
        over_ici(4, 1, me ^ 3, 0).wait_recv()
        over_ici(5, 2, me ^ 3, 1).wait_recv()
        to_sibling(3, c).start()

    def finish():
        for k in (1, 2, 3):
            to_sibling(k, 1 - c).wait_recv()
        for cp in sends + onward + [to_sibling(k, c) for k in (1, 2, 3)]:
            cp.wait_send()
        local.wait()

    return start, pass_on, to_other_core, finish


def _gather_ride(shard, spread):
    def plan(in_refs, out_refs, send_sems, recv_sems, local_sems):
        return _gather_plan(in_refs[0], out_refs[0], send_sems, recv_sems, local_sems)

    return _Ride([shard], [jax.ShapeDtypeStruct((4,) + shard.shape, shard.dtype)], (9, 9, 1), plan,
                 in_specs=[IN_VMEM], spread=spread)


def _chip_sum(sf, recv, chip, rows, tr, name):
    cols = sf.shape[2]
    n_recv = recv.shape[0]

    def body(me_ref, sf_ref, r_ref, out_ref):
        acc = sf_ref[...]
        for k in range(n_recv):
            acc = acc + r_ref[k].astype(F32)
        out_ref[...] = acc

    return pl.pallas_call(
        body, name=name,
        grid_spec=pltpu.PrefetchScalarGridSpec(
            num_scalar_prefetch=1, grid=(rows // tr,),
            in_specs=[pl.BlockSpec((None, tr, cols), lambda i, me_ref: (me_ref[0], i, 0)),
                      pl.BlockSpec((n_recv, tr, cols), lambda i, me_ref: (0, i, 0))],
            out_specs=pl.BlockSpec((tr, cols), lambda i, me_ref: (i, 0))),
        out_shape=jax.ShapeDtypeStruct((rows, cols), F32),
        compiler_params=_params(("parallel",)),
    )(chip, sf, recv)


def _position():
    return lax.axis_index("x"), lax.axis_index("y"), lax.axis_index("c")


def _dh_scatter(dproj, w_in_arr_t, x, dz, g, sb_in, sb_rest, tm=512, tk=3072):
    t, d = x.shape
    nk = dproj.shape[1] // tk
    ni = t // tm
    total = ni * nk
    halves = (HALF_IN, HALF_REST)
    cuts = tuple(-(-rows // 32) * 16 for rows in halves)

    def rows_of(a, p):
        return cuts[a] if p == 0 else halves[a] - cuts[a]

    def piece(a, p):
        return pl.ds(0, cuts[a]) if p == 0 else pl.ds(cuts[a], halves[a] - cuts[a])

    def body(dp_ref, w_ref, x_ref, dz_ref, g_ref, sbin_ref, sbrest_ref, dx_ref, dg_ref, db_ref, rin_ref, rrest_ref,
             acc_ref, pay_in0, pay_in1, pay_rest0, pay_rest1, own_in0, own_in1, own_rest0, own_rest1,
             send_sems, recv_sems, local_sems):
        step = pl.program_id(0) * nk + pl.program_id(1)
        kk = pl.program_id(1)
        px, py, pc = _position()
        me = 2 * px + py
        srcs = (sbin_ref, sbrest_ref)
        dsts = (rin_ref, rrest_ref)
        pays = ((pay_in0, pay_in1), (pay_rest0, pay_rest1))
        owns = ((own_in0, own_in1), (own_rest0, own_rest1))
        via = (2, 1)
        onto = (1, 2)

        def peer(k):
            return (px ^ (k >> 1), py ^ (k & 1), pc)

        def payload(a, p):
            return pltpu.make_async_remote_copy(
                src_ref=srcs[a].at[me ^ 3, piece(a, p)], dst_ref=pays[a][p], send_sem=send_sems.at[2 * a + p],
                recv_sem=recv_sems.at[2 * a + p], device_id=peer(via[p]), device_id_type=MESH)

        def direct(a, k, p, src):
            sem = 4 + 4 * a + 2 * (k - 1) + p
            return pltpu.make_async_remote_copy(
                src_ref=src, dst_ref=dsts[a].at[k - 1, piece(a, p)], send_sem=send_sems.at[sem],
                recv_sem=recv_sems.at[sem], device_id=peer(k), device_id_type=MESH)

        def plain(a, k, p):
            return direct(a, k, p, srcs[a].at[me ^ k, piece(a, p)])

        def stage(a, p):
            return pltpu.make_async_copy(srcs[a].at[me ^ onto[p], piece(a, p)], owns[a][p], local_sems.at[2 * a + p])

        @pl.when(step == 0)
        def _():
            dg_ref[...] = jnp.zeros_like(dg_ref)
            db_ref[...] = jnp.zeros_like(db_ref)
            for a in range(2):
                for p in range(2):
                    payload(a, p).start()
                    stage(a, p).start()
                plain(a, 1, 1).start()
                plain(a, 2, 0).start()

        @pl.when(step == (5 * total) // 8)
        def _():
            for a in range(2):
                for p in range(2):
                    payload(a, p).wait_recv()
                    stage(a, p).wait()
                    owns[a][p][...] = (owns[a][p][...].astype(F32) + pays[a][p][...].astype(F32)).astype(BF16)
                    direct(a, onto[p], p, owns[a][p]).start()

        part = _dot(dp_ref[...], w_ref[...])

        @pl.when(kk == 0)
        def _():
            acc_ref[...] = part

        @pl.when(kk > 0)
        def _():
            acc_ref[...] += part

        @pl.when(kk == nk - 1)
        def _():
            xh, rstd = _ln_hat(x_ref[...])
            dht = acc_ref[...] + DEEPNORM_ALPHA * dz_ref[...]
            dg_ref[...] += _colsum(dht * xh)
            db_ref[...] += _colsum(dht)
            dx_ref[...] = _ln_bwd_rows(dht * g_ref[...], xh, rstd)

        @pl.when(step == total - 1)
        def _():
            for a in range(2):
                for k in (1, 2):
                    for p in range(2):
                        plain(a, k, p).wait_recv()
            for a in range(2):
                for p in range(2):
                    payload(a, p).wait_send()
                    direct(a, onto[p], p, owns[a][p]).wait_send()
                plain(a, 1, 1).wait_send()
                plain(a, 2, 0).wait_send()

    tile = pl.BlockSpec((tm, d), lambda i, kk: (i, 0))
    row = pl.BlockSpec((1, d), lambda i, kk: (0, 0))
    pieces = [pltpu.VMEM((rows_of(a, p), 1024), BF16) for a in range(2) for p in range(2)]
    return pl.pallas_call(
        body, name="dh_scatter", grid=(ni, nk),
        out_shape=(jax.ShapeDtypeStruct((t, d), F32), jax.ShapeDtypeStruct((1, d), F32),
                   jax.ShapeDtypeStruct((1, d), F32),
                   jax.ShapeDtypeStruct((2, HALF_IN, 1024), BF16),
                   jax.ShapeDtypeStruct((2, HALF_REST, 1024), BF16)),
        in_specs=[pl.BlockSpec((tm, tk), lambda i, kk: (i, kk)), pl.BlockSpec((tk, d), lambda i, kk: (kk, 0)),
                  tile, tile, row, ANY, ANY],
        out_specs=(tile, row, row, ANY, ANY),
        scratch_shapes=[pltpu.VMEM((tm, d), F32)] + pieces + pieces
        + [pltpu.SemaphoreType.DMA((12,)), pltpu.SemaphoreType.DMA((12,)), pltpu.SemaphoreType.DMA((4,))],
        compiler_params=_params(("arbitrary", "arbitrary")),
    )(dproj, w_in_arr_t, x, dz, g, sb_in, sb_rest)


def _join_halves(gh_in, gh_rest):
    def body(hin_ref, hrest_ref, oin_ref, orest_ref, send_sems, recv_sems, local_sems):
        x, y, c = _position()
        srcs = (hin_ref, hrest_ref)
        dsts = (oin_ref, orest_ref)

        def rows(a, half):
            return dsts[a].at[half]

        local = [pltpu.make_async_copy(srcs[a], rows(a, c), local_sems.at[a]) for a in range(2)]
        remote = [pltpu.make_async_remote_copy(
            src_ref=srcs[a], dst_ref=rows(a, c), send_sem=send_sems.at[a], recv_sem=recv_sems.at[a],
            device_id=(x, y, 1 - c), device_id_type=MESH) for a in range(2)]
        for cp in local + remote:
            cp.start()
        for a in range(2):
            pltpu.make_async_remote_copy(
                src_ref=srcs[a], dst_ref=rows(a, 1 - c), send_sem=send_sems.at[a], recv_sem=recv_sems.at[a],
                device_id=(x, y, 1 - c), device_id_type=MESH).wait_recv()
        for cp in remote:
            cp.wait_send()
        for cp in local:
            cp.wait()

    return pl.pallas_call(
        body, name="join_halves",
        out_shape=(jax.ShapeDtypeStruct((2, HALF_IN, 1024), F32),
                   jax.ShapeDtypeStruct((2, HALF_REST, 1024), F32)),
        in_specs=[IN_VMEM, IN_VMEM], out_specs=(ANY, ANY),
        scratch_shapes=[pltpu.SemaphoreType.DMA((2,)), pltpu.SemaphoreType.DMA((2,)), pltpu.SemaphoreType.DMA((2,))],
    )(gh_in, gh_rest)


def _allreduce_small(vec):
    def body(vec_ref, out_ref, all_ref, send_sems, recv_sems):
        x, y, c = _position()
        me = 4 * x + 2 * y + c
        all_ref[me] = vec_ref[...]

        def copy(k, slot):
            return pltpu.make_async_remote_copy(
                src_ref=vec_ref, dst_ref=all_ref.at[slot], send_sem=send_sems.at[k - 1], recv_sem=recv_sems.at[k - 1],
                device_id=(x ^ (k >> 2), y ^ ((k >> 1) & 1), c ^ (k & 1)), device_id_type=MESH)

        copies = [copy(k, me) for k in range(1, 8)]
        for cp in copies:
            cp.start()
        for k in range(1, 8):
            copy(k, me ^ k).wait_recv()
        for cp in copies:
            cp.wait_send()
        total = all_ref[0]
        for d in range(1, 8):
            total = total + all_ref[d]
        out_ref[...] = total

    return pl.pallas_call(
        body, name="allreduce_small",
        out_shape=jax.ShapeDtypeStruct(vec.shape, vec.dtype),
        in_specs=[pl.BlockSpec(memory_space=pltpu.VMEM)], out_specs=pl.BlockSpec(memory_space=pltpu.VMEM),
        scratch_shapes=[pltpu.VMEM((8,) + vec.shape, vec.dtype), pltpu.SemaphoreType.DMA((7,)),
                        pltpu.SemaphoreType.DMA((7,))],
    )(vec)


def _pack_rest(w_uq, w_ukv, w_mem, w_out):
    rows = jnp.concatenate([w_uq[0].T.reshape(-1, 1024), w_ukv.reshape(-1, 1024), w_mem.reshape(-1, 1024),
                            w_out.reshape(-1, 1024)], axis=0)
    return jnp.pad(rows, ((0, ROWS_REST - ROWS_USED), (0, 0)))


def _arranged_w_in(g_in):
    z = functools.partial(jnp.zeros, dtype=g_in.dtype)
    cut = 4480 - 2 * SHARD_ROWS
    return jnp.concatenate(
        [g_in[0, :SHARD_ROWS], g_in[1, :SHARD_ROWS], g_in[2, :cut], z((64, 1024)), g_in[2, cut:cut + 32],
         z((32, 1024)), g_in[2, cut + 32:SHARD_ROWS], g_in[3, :SHARD_ROWS]], axis=0)


def _rest_weights(g_rest):
    w_uq_t = g_rest[:, 0:ROWS_UQ].reshape(768, 256)
    w_uq_pad_t = jnp.pad(w_uq_t.reshape(MLA_HEADS, MLA_QK_DIM, 256), ((0, 0), (0, 32), (0, 0))).reshape(1024, 256)
    w_ukv = jnp.concatenate([g_rest[j, ROWS_UQ:ROWS_UQ + ROWS_UKV].reshape(128, 256) for j in range(4)], axis=1)
    lo = ROWS_UQ + ROWS_UKV
    w_mem = g_rest[:, lo:lo + ROWS_MEM].reshape(4 * ROWS_MEM, 1024)
    w_out = g_rest[:, lo + ROWS_MEM:lo + ROWS_MEM + ROWS_OUT].reshape(4 * ROWS_OUT, 1024)
    return w_uq_pad_t, w_ukv, w_mem, w_out


def _split_in(dw_in_arr_t):
    a = dw_in_arr_t
    gap = jnp.zeros((ROWS_IN - SHARD_ROWS, 1024), a.dtype)
    nat = 4608 - 96
    pieces = [a[:SHARD_ROWS], gap, a[SHARD_ROWS:2 * SHARD_ROWS], gap,
              a[2 * SHARD_ROWS:4480], a[4544:4576], a[4608:4608 + 3 * SHARD_ROWS - nat], gap,
              a[4608 + 3 * SHARD_ROWS - nat:], gap]
    return jnp.concatenate(pieces, axis=0).reshape(4, ROWS_IN, 1024)


def _split_rest(dw_uq_pad_t, dw_ukv, dw_mem, dw_out):
    dw_uq_t = dw_uq_pad_t.reshape(MLA_HEADS, LANES, 256)[:, :MLA_QK_DIM].reshape(4, ROWS_UQ, 1024)
    parts = [dw_uq_t, dw_ukv.reshape(128, 4, 256).transpose(1, 0, 2).reshape(4, ROWS_UKV, 1024),
             dw_mem.reshape(4, ROWS_MEM, 1024), dw_out.reshape(4, ROWS_OUT, 1024)]
    return jnp.pad(jnp.concatenate(parts, axis=1), ((0, 0), (0, ROWS_REST - ROWS_USED), (0, 0)))


def _rope_consts(rot, first, period):
    half = rot // 2
    inv_freq = np.float32(ROPE_THETA) ** (-(np.arange(0, rot, 2, dtype=np.float32) / np.float32(rot)))
    lane = np.arange(LANES) % period - first
    in_rot = (lane >= 0) & (lane < rot)
    out = np.zeros((8, LANES), np.float32)
    out[0] = np.where(in_rot, inv_freq[np.clip(lane, 0, rot - 1) % half], 0.0)
    out[1] = in_rot & (lane < half)
    out[2] = in_rot & (lane >= half)
    return jnp.asarray(out)


def _band_bias(s):
    nblk = s // BAND_Q
    starts = np.array([_band_start(i, s) for i in range(nblk)])
    uq = (np.arange(nblk)[:, None] * BAND_Q + np.arange(BAND_Q)[None, :])[:, :, None]
    uk = (starts[:, None] + np.arange(BAND_WIN)[None, :])[:, None, :]
    tiles, index, seen = [], [], {}
    for _, d in DILATED:
        length = s // d
        ok = (uq // length == uk // length) & (np.abs(uq - uk) <= 64)
        row = []
        for i in range(nblk):
            key = ok[i].tobytes()
            if key not in seen:
                seen[key] = len(tiles)
                tiles.append(np.where(ok[i], 0.0, NEG_INF).astype(np.float32))
            row.append(seen[key])
        index.append(row)
    return jnp.asarray(np.stack(tiles, axis=0)), index


def _forward_backward(h, h32, proj, trig, rope_consts, x, mem, target, weights, gains):
    w_uq_pad_t, w_ukv, w_mem, w_out = weights
    g_emb, b_emb, g_cq, g_ckv, g_out_a, g_out_b, g_out_m, g_post, b_post = gains
    nb, s, d = x.shape
    t = nb * s
    x2 = x.reshape(t, d)
    mem2 = mem.reshape(nb * N_MEM, d)
    tgt2 = target.reshape(t, d)
    rope_a, rope_b = rope_consts
    bias, bias_index = _band_bias(s)
    scales = (0.125, MLA_QK_DIM ** -0.5, 128 ** -0.5)

    qa, ka, va, qb, kb, vb, qm, cqn, ckvn = _prep(proj, trig, w_uq_pad_t, w_ukv, g_cq, g_ckv, rope_a, rope_b, scales)
    mkv = _mm(mem2, w_mem, BF16, nb * N_MEM, 1024, 1024, "mem_kv")

    cfg_b = dict(nb=nb, s=s, sk=s, heads=8, voff=0, bq=256)
    cfg_m = dict(nb=nb, s=s, sk=N_MEM, heads=4, hpb=2, voff=4, bq=1024)
    ya, lse_a, qkv_ordered = _dilated_fwd(qa, ka, va, bias, bias_index, nb=nb, s=s, name="attn_a_fwd")
    yb, lse_b = _attn_fwd(qb, kb, vb, name="attn_b_fwd", hpb=4, **cfg_b)
    ym, lse_m = _attn_fwd(qm, mkv, mkv, name="attn_m_fwd", **cfg_m)

    (y, dz, doa, dob, dom, dga, dgb, dgm, loss, dg_post, db_post, dg_a, dg_b, dg_m) = _post(
        h32, ya, yb, ym, proj, tgt2, w_out, g_out_a, g_out_b, g_out_m, g_post, b_post)

    dqa, dka, dva = _dilated_bwd(qa, ka, va, qkv_ordered, ya, doa, lse_a, bias, bias_index, nb=nb, s=s, scale=scales[0],
                                 name="attn_a_bwd")
    dqb, dkb, dvb = _attn_bwd(qb, kb, vb, yb, dob, lse_b, name="attn_b_bwd", scale=scales[1], hpb=4, **cfg_b)
    dqm, dmk, dmv = _attn_bwd(qm, mkv, mkv, ym, dom, lse_m, name="attn_m_bwd", scale=scales[2], **cfg_m)
    dmkv = jnp.concatenate([dmk, dmv], axis=1)

    dproj, dqf, dkv, dg_cq, dg_ckv = _prep_bwd(
        dqa, dka, dva, dqb, dkb, dvb, dqm, dga, dgb, dgm, proj, trig, w_uq_pad_t, w_ukv, g_cq, g_ckv, rope_a, rope_b)

    small_rows = (dg_cq, dg_ckv, loss, dg_a, dg_b, dg_m, dg_post, db_post)
    return (dproj, h, y, dz, dqf, cqn, ckvn, dkv, mem2, dmkv), x2, small_rows


def _weight_grads(operands, core):
    dproj, h, y, dz, dqf, cqn, ckvn, dkv, mem2, dmkv = operands
    dw_in_arr_t = _mm(dproj, h, F32, 1024, 1024, 4096, "dw_in", mode="tn")
    g_in = _split_in(dw_in_arr_t)
    dw_out, r_in = _mm(y, dz, F32, 1024, 1024, 2048, "dw_out", mode="tn",
                       ride=_half_to_sibling(g_in.reshape(4, 2, HALF_IN, 1024)))
    dw_uq_pad_t = _mm(dqf, cqn, F32, 1024, 256, 4096, "dw_uq", mode="tn")
    dw_ukv = _mm(ckvn, dkv, F32, 128, 1024, 4096, "dw_ukv", mode="tn")
    dw_mem = _mm(mem2, dmkv, F32, 1024, 1024, mem2.shape[0], "dw_mem", mode="tn")
    g_rest = _split_rest(dw_uq_pad_t, dw_ukv, dw_mem, dw_out)
    sf_in, sb_in, r_rest = _core_sum(g_in, r_in, core, HALF_IN, HALF_IN // 2, "core_sum_in",
                                     ride=_half_to_sibling(g_rest.reshape(4, 2, HALF_REST, 1024)))
    sf_rest, sb_rest = _core_sum(g_rest, r_rest, core, HALF_REST, HALF_REST, "core_sum_rest")
    return sf_in, sb_in, sf_rest, sb_rest


def _small_block(dg_emb, db_emb, small_rows):
    dg_cq, dg_ckv, loss, dg_a, dg_b, dg_m, dg_post, db_post = small_rows
    row2 = jnp.concatenate([dg_cq, dg_ckv, loss, jnp.zeros((1, 512), F32)], axis=1)
    return jnp.concatenate([dg_emb, db_emb, row2, dg_a, jnp.concatenate([dg_b, dg_m], axis=1), dg_post, db_post,
                            jnp.zeros((1, 1024), F32)], axis=0)


def _pack_small(g_emb, b_emb, g_cq, g_ckv, g_out_a, g_out_b, g_out_m, g_post, b_post):
    row2 = jnp.concatenate([g_cq.reshape(1, -1), g_ckv.reshape(1, -1), jnp.zeros((1, 640), F32)], axis=1)
    return jnp.concatenate([g_emb.reshape(1, -1), b_emb.reshape(1, -1), row2, g_out_a.reshape(1, -1),
                            jnp.concatenate([g_out_b.reshape(1, -1), g_out_m.reshape(1, -1)], axis=1),
                            g_post.reshape(1, -1), b_post.reshape(1, -1), jnp.zeros((1, 1024), F32)], axis=0)


def kernel(x, mem, positions, g_emb, b_emb, w_in, g_cq, g_ckv, w_uq, w_ukv, w_mem_kv, g_out_a, g_out_b, g_out_m, w_out, g_post, b_post, loss_target, m_g_emb, m_b_emb, m_w_in, m_g_cq, m_g_ckv, m_w_uq, m_w_ukv, m_w_mem_kv, m_g_out_a, m_g_out_b, m_g_out_m, m_w_out, m_g_post, m_b_post, v_g_emb, v_b_emb, v_w_in, v_g_cq, v_g_ckv, v_w_uq, v_w_ukv, v_w_mem_kv, v_g_out_a, v_g_out_b, v_g_out_m, v_w_out, v_g_post, v_b_post):
    w_rest = _pack_rest(w_uq, w_ukv, w_mem_kv, w_out)
    w_in_t = w_in[0].T
    w_in_b = jnp.pad(w_in_t.astype(BF16), ((0, ROWS_IN - SHARD_ROWS), (0, 0)))
    gains = (g_emb.reshape(1, -1), b_emb.reshape(1, -1), g_cq, g_ckv, g_out_a, g_out_b, g_out_m, g_post, b_post)
    rope_consts = (_rope_consts(16, 0, 64), _rope_consts(32, 64, 128))
    h, h32, trig, gathered_in = _ln_fwd(x.reshape(-1, D_MODEL), gains[0], gains[1],
                                        positions.reshape(-1, 1).astype(F32), *rope_consts,
                                        ride=_gather_ride(w_in_b.reshape(2, HALF_IN, 1024), spread=False))
    w_in_arr_t = _arranged_w_in(gathered_in.reshape(4, ROWS_IN, 1024))
    proj, gathered_rest = _mm(h, w_in_arr_t, F32, 1024, 2048, 1024, "in_proj", mode="nt",
                              ride=_gather_ride(w_rest.astype(BF16).reshape(2, HALF_REST, 1024), spread=True))
    weights = _rest_weights(gathered_rest.reshape(4, ROWS_REST, 1024))
    operands, x2, small_rows = _forward_backward(h, h32, proj, trig, rope_consts, x, mem, loss_target, weights,
                                                 gains)

    core = lax.axis_index("c").astype(jnp.int32).reshape(1)
    chip = (2 * lax.axis_index("x") + lax.axis_index("y")).astype(jnp.int32).reshape(1)
    sf_in, sb_in, sf_rest, sb_rest = _weight_grads(operands, core)
    grad_x, dg_emb, db_emb, rb_in, rb_rest = _dh_scatter(operands[0], w_in_arr_t, x2, operands[3], gains[0],
                                                         sb_in, sb_rest)
    gh_in = _chip_sum(sf_in, rb_in, chip, HALF_IN, HALF_IN // 2, "chip_sum_in")
    gh_rest = _chip_sum(sf_rest, rb_rest, chip, HALF_REST, HALF_REST, "chip_sum_rest")
    grad_in, grad_rest = _join_halves(gh_in, gh_rest)
    grad_in = grad_in.reshape(ROWS_IN, 1024)
    grad_rest = grad_rest.reshape(ROWS_REST, 1024)

    big_in = _adamw(grad_in, w_in_t, m_w_in[0].T, v_w_in[0].T, SHARD_ROWS // 3, "adamw_in")
    def rest_parts(a_uq, a_ukv, a_mem, a_out):
        return [a_uq[0].T.reshape(ROWS_UQ, 1024), a_ukv.reshape(ROWS_UKV, 1024), a_mem[0], a_out[0]]

    uq, ukv, wmem, wout = _adamw_pieces(
        grad_rest, rest_parts(w_uq, w_ukv, w_mem_kv, w_out), rest_parts(m_w_uq, m_w_ukv, m_w_mem_kv, m_w_out),
        rest_parts(v_w_uq, v_w_ukv, v_w_mem_kv, v_w_out), REST_PIECES, "adamw_rest")
    small_sum = _allreduce_small(_small_block(dg_emb, db_emb, small_rows))
    sm = _adamw_pieces(
        small_sum,
        _pack_small(g_emb, b_emb, g_cq, g_ckv, g_out_a, g_out_b, g_out_m, g_post, b_post),
        _pack_small(m_g_emb, m_b_emb, m_g_cq, m_g_ckv, m_g_out_a, m_g_out_b, m_g_out_m, m_g_post, m_b_post),
        _pack_small(v_g_emb, v_b_emb, v_g_cq, v_g_ckv, v_g_out_a, v_g_out_b, v_g_out_m, v_g_post, v_b_post),
        SMALL_PIECES, "adamw_small")
    loss = small_sum[2, 384]

    def ordered(kind):
        s_gemb, s_bemb, s_gcq, s_gckv, s_ga, s_gb, s_gm, s_gpost, s_bpost = [piece[kind] for piece in sm]
        return [s_gemb.reshape(-1), s_bemb.reshape(-1), big_in[kind].T[None], s_gcq, s_gckv,
                uq[kind].reshape(192, 256).T[None], ukv[kind].reshape(1, 128, 256), wmem[kind][None], s_ga, s_gb,
                s_gm, wout[kind][None], s_gpost, s_bpost]

    return (loss, grad_x.reshape(x.shape), *ordered(0), *ordered(1), *ordered(2), *ordered(3))
```

```python
import functools
import math

import jax
import jax.numpy as jnp
import numpy as np
from jax import lax
from jax.experimental import pallas as pl
from jax.experimental.pallas import tpu as pltpu

F32 = jnp.float32
BF16 = jnp.bfloat16
MESH = pl.DeviceIdType.MESH
ANY = pl.BlockSpec(memory_space=pl.ANY)
IN_VMEM = pl.BlockSpec(memory_space=pltpu.VMEM)

D_MODEL = 1024
A_WIDTH = 1024
MLA_HEADS = 8
MLA_Q_RANK = 256
MLA_KV_RANK = 128
MLA_QK_DIM = 96
MEM_WIDTH = 512
N_MEM = 256
ROPE_THETA = 500000.0
NORM_EPS = 1e-5
NEG_INF = -1e30
DEEPNORM_ALPHA = 2.0 ** 0.25
DILATED = ((64, 1), (256, 4), (1024, 16))

ADAM_LR = 0.001
ADAM_B1 = 0.9
ADAM_B2 = 0.999
ADAM_EPS = 1e-08
ADAM_WD = 0.01
ADAM_STEP = 10

LANES = 128
VMEM_LIMIT = 56 * 1024 * 1024
LOG2E = math.log2(math.e)
LN2 = math.log(2.0)

PROJ_W = 6144
COL_CQ = 4096
COL_BG = 4608
COL_MQ = 5120
COL_MG = 5632

SHARD_ROWS = 1512
ROWS_IN = 1536
ROWS_UQ, ROWS_UKV, ROWS_MEM, ROWS_OUT = 48, 32, 256, 512
ROWS_USED = ROWS_UQ + ROWS_UKV + ROWS_MEM + ROWS_OUT
ROWS_REST = 864
HALF_IN = ROWS_IN // 2
HALF_REST = ROWS_REST // 2
REST_PIECES = ((0, 48, 0, 1024), (48, 80, 0, 1024), (80, 336, 0, 1024), (336, 848, 0, 1024))
SMALL_PIECES = ((0, 1, 0, 1024), (1, 2, 0, 1024), (2, 3, 0, 256), (2, 3, 256, 384), (3, 4, 0, 1024), (4, 5, 0, 512),
                (4, 5, 512, 1024), (5, 6, 0, 1024), (6, 7, 0, 1024))


def _params(sem=None, vmem=VMEM_LIMIT):
    return pltpu.CompilerParams(dimension_semantics=sem, vmem_limit_bytes=vmem)


def _dot(a, b):
    return jnp.dot(a, b, preferred_element_type=F32)


def _dot_nt(a, b):
    return lax.dot_general(a, b, (((1,), (1,)), ((), ())), preferred_element_type=F32)


def _dot_tn(a, b):
    return lax.dot_general(a, b, (((0,), (0,)), ((), ())), preferred_element_type=F32)


def _ln_hat(x):
    mu = jnp.mean(x, axis=-1, keepdims=True)
    xc = x - mu
    var = jnp.mean(xc * xc, axis=-1, keepdims=True)
    rstd = lax.rsqrt(var + NORM_EPS)
    return xc * rstd, rstd


def _ln_bwd_rows(dxh, xh, rstd):
    return rstd * (dxh - jnp.mean(dxh, axis=-1, keepdims=True) - xh * jnp.mean(dxh * xh, axis=-1, keepdims=True))


def _rms_hat(x, width):
    ms = jnp.sum(x * x, axis=-1, keepdims=True) * (1.0 / width)
    r = lax.rsqrt(ms + NORM_EPS)
    return x * r, r


def _rms_bwd(u, xh, r, width):
    return r * (u - xh * (jnp.sum(u * xh, axis=-1, keepdims=True) * (1.0 / width)))


def _colsum(v):
    return jnp.sum(v, axis=0, keepdims=True)


def _rope_tables(cos, sin, consts):
    return cos, sin * consts[2:3, :], -sin * consts[1:2, :]


def _rope(x, tables, half, inverse=False):
    c, s_up, s_dn = tables
    if inverse:
        s_up, s_dn = -s_up, -s_dn
    return x * c + pltpu.roll(x, half, 1) * s_up + pltpu.roll(x, LANES - half, 1) * s_dn


def _ln_fwd(x, g, b, pos, rope_a, rope_b, tm=512, ride=None):
    t, d = x.shape
    n_in = len(ride.args) if ride else 0
    n_out = len(ride.out_shapes) if ride else 0
    steps = t // tm

    def body(x_ref, g_ref, b_ref, pos_ref, ra_ref, rb_ref, *rest):
        h_ref, h32_ref, trig_ref = rest[n_in:n_in + 3]
        if ride:
            i = pl.program_id(0)
            ride.run(i, steps, rest[:n_in], rest[n_in + 3:n_in + 3 + n_out], rest[n_in + 3 + n_out:])
        xh, _ = _ln_hat(x_ref[...])
        h = xh * g_ref[...] + b_ref[...]
        h32_ref[...] = h
        h_ref[...] = h.astype(BF16)
        for j, consts in enumerate((ra_ref, rb_ref)):
            ang = pos_ref[...] * consts[0:1, :]
            trig_ref[:, 2 * j * LANES:(2 * j + 1) * LANES] = jnp.cos(ang)
            trig_ref[:, (2 * j + 1) * LANES:(2 * j + 2) * LANES] = jnp.sin(ang)

    row = pl.BlockSpec((1, d), lambda i: (0, 0))
    tile = pl.BlockSpec((tm, d), lambda i: (i, 0))
    consts = pl.BlockSpec((8, LANES), lambda i: (0, 0))
    trig_tile = pl.BlockSpec((tm, 4 * LANES), lambda i: (i, 0))
    in_specs = [tile, row, row, pl.BlockSpec((tm, 1), lambda i: (i, 0)), consts, consts]
    shapes = (jax.ShapeDtypeStruct((t, d), BF16), jax.ShapeDtypeStruct((t, d), F32),
              jax.ShapeDtypeStruct((t, 4 * LANES), F32))
    if not ride:
        return pl.pallas_call(
            body, name="ln_fwd", grid=(steps,), out_shape=shapes, in_specs=in_specs,
            out_specs=(tile, tile, trig_tile), compiler_params=_params(("parallel",)),
        )(x, g, b, pos, rope_a, rope_b)
    return pl.pallas_call(
        body, name="ln_fwd", grid=(steps,),
        out_shape=(*shapes, *ride.out_shapes),
        in_specs=in_specs + ride.in_specs, out_specs=(tile, tile, trig_tile) + (ANY,) * n_out,
        scratch_shapes=ride.scratch(),
        compiler_params=_params(("arbitrary",)),
    )(x, g, b, pos, rope_a, rope_b, *ride.args)


class _Ride:
    def __init__(self, args, out_shapes, sem_counts, plan, in_specs=None, spread=True):
        self.args, self.out_shapes, self.plan = list(args), list(out_shapes), plan
        self.sem_counts = sem_counts
        self.in_specs = in_specs or [ANY] * len(self.args)
        self.spread = spread

    def scratch(self):
        return [pltpu.SemaphoreType.DMA((n,)) for n in self.sem_counts]

    def run(self, step, total, in_refs, out_refs, sems):
        count = len(self.plan(in_refs, out_refs, *sems))
        at = [(k * (total - 1)) // (count - 1) if self.spread or k == 0 else total - 1 for k in range(count)]
        for when in sorted(set(at)):
            @pl.when(step == when)
            def _(when=when):
                stages = self.plan(in_refs, out_refs, *sems)
                for k in range(count):
                    if at[k] == when:
                        stages[k]()


def _mm(a, b, out_dtype, tm, tn, tk, name, mode="nn", ride=None):
    if mode == "tn":
        k, m = a.shape
    else:
        m, k = a.shape
    n = b.shape[0] if mode == "nt" else b.shape[1]
    nk = k // tk
    nj, ni = n // tn, m // tm
    n_in = len(ride.args) if ride else 0
    n_out = len(ride.out_shapes) if ride else 0

    def body(a_ref, b_ref, *rest):
        o_ref = rest[n_in]
        acc_ref = rest[n_in + 1 + n_out]
        if ride:
            j, i, kk = pl.program_id(0), pl.program_id(1), pl.program_id(2)
            ride.run((j * ni + i) * nk + kk, nj * ni * nk, rest[:n_in], rest[n_in + 1:n_in + 1 + n_out],
                     rest[n_in + 2 + n_out:])
        av = a_ref[...].astype(BF16)
        bv = b_ref[...].astype(BF16)
        part = _dot_tn(av, bv) if mode == "tn" else _dot_nt(av, bv) if mode == "nt" else _dot(av, bv)
        if nk == 1:
            o_ref[...] = part.astype(out_dtype)
        else:
            kk = pl.program_id(2)

            @pl.when(kk == 0)
            def _():
                acc_ref[...] = part

            @pl.when(kk > 0)
            def _():
                acc_ref[...] += part

            @pl.when(kk == nk - 1)
            def _():
                o_ref[...] = acc_ref[...].astype(out_dtype)

    a_spec = (pl.BlockSpec((tk, tm), lambda j, i, kk: (kk, i)) if mode == "tn"
              else pl.BlockSpec((tm, tk), lambda j, i, kk: (i, kk)))
    b_spec = (pl.BlockSpec((tn, tk), lambda j, i, kk: (j, kk)) if mode == "nt"
              else pl.BlockSpec((tk, tn), lambda j, i, kk: (kk, j)))
    o_spec = pl.BlockSpec((tm, tn), lambda j, i, kk: (i, j))
    o_shape = jax.ShapeDtypeStruct((m, n), out_dtype)
    if not ride:
        return pl.pallas_call(
            body, name=name, grid=(nj, ni, nk), out_shape=o_shape, in_specs=[a_spec, b_spec], out_specs=o_spec,
            scratch_shapes=[pltpu.VMEM((tm, tn), F32)],
            compiler_params=_params(("parallel", "parallel", "arbitrary")),
        )(a, b)
    return pl.pallas_call(
        body, name=name, grid=(nj, ni, nk),
        out_shape=(o_shape, *ride.out_shapes),
        in_specs=[a_spec, b_spec] + ride.in_specs,
        out_specs=(o_spec,) + (ANY,) * n_out,
        scratch_shapes=[pltpu.VMEM((tm, tn), F32)] + ride.scratch(),
        compiler_params=_params(("arbitrary", "arbitrary", "arbitrary")),
    )(a, b, *ride.args)


def _prep(proj, trig, w_uq, w_ukv, g_cq, g_ckv, rope_a, rope_b, scales, tm=512):
    t = proj.shape[0]
    sc_a, sc_b, sc_m = (s * LOG2E for s in scales)

    def body(aq_ref, ak_ref, av_ref, bs_ref, mq_ref, trig_ref, wuq_ref, wukv_ref, gcq_ref, gckv_ref,
             ra_ref, rb_ref, qa_ref, ka_ref, va_ref, qb_ref, kb_ref, vb_ref, qm_ref, cqn_ref, ckvn_ref):
        ta = _rope_tables(trig_ref[:, 0:LANES], trig_ref[:, LANES:2 * LANES], ra_ref[...])
        tb = _rope_tables(trig_ref[:, 2 * LANES:3 * LANES], trig_ref[:, 3 * LANES:4 * LANES], rb_ref[...])
        for j in range(A_WIDTH // LANES):
            sl = slice(j * LANES, (j + 1) * LANES)
            qa_ref[:, sl] = (_rope(aq_ref[:, sl], ta, 8) * sc_a).astype(BF16)
            ka_ref[:, sl] = _rope(ak_ref[:, sl], ta, 8).astype(BF16)
        va_ref[...] = av_ref[...].astype(BF16)
        qm_ref[...] = (mq_ref[...] * sc_m).astype(BF16)

        cq_hat, _ = _rms_hat(bs_ref[:, 0:MLA_Q_RANK], MLA_Q_RANK)
        cqn = (cq_hat * gcq_ref[...]).astype(BF16)
        cqn_ref[...] = cqn
        ckv_hat, _ = _rms_hat(bs_ref[:, MLA_Q_RANK:MLA_Q_RANK + MLA_KV_RANK], MLA_KV_RANK)
        ckvn = (ckv_hat * gckv_ref[...]).astype(BF16)
        ckvn_ref[...] = ckvn
        qfull = _dot_nt(cqn, wuq_ref[...])
        kv = _dot(ckvn, wukv_ref[...])
        kr = _rope(bs_ref[:, 384:512], tb, 16)
        lane = lax.broadcasted_iota(jnp.int32, (1, LANES), 1)
        low = lane < 64
        for h in range(MLA_HEADS):
            sl = slice(h * LANES, (h + 1) * LANES)
            qb_ref[:, sl] = (_rope(qfull[:, sl], tb, 16) * sc_b).astype(BF16)
            kb_ref[:, sl] = jnp.where(low, kv[:, sl], kr).astype(BF16)
            vb_ref[:, sl] = jnp.where(low, 0.0, kv[:, sl]).astype(BF16)

    def col(width, idx):
        return pl.BlockSpec((tm, width), lambda i: (i, idx))

    def full(shape):
        return pl.BlockSpec(shape, lambda i: (0, 0))

    wide = jax.ShapeDtypeStruct((t, 1024), BF16)
    return pl.pallas_call(
        body, name="prep", grid=(t // tm,),
        out_shape=(wide, wide, wide, wide, wide, wide,
                   jax.ShapeDtypeStruct((t, MEM_WIDTH), BF16),
                   jax.ShapeDtypeStruct((t, MLA_Q_RANK), BF16),
                   jax.ShapeDtypeStruct((t, MLA_KV_RANK), BF16)),
        in_specs=[col(1024, 0), col(1024, 1), col(1024, 2), col(512, COL_CQ // 512), col(512, COL_MQ // 512),
                  pl.BlockSpec((tm, 4 * LANES), lambda i: (i, 0)),
                  full((1024, MLA_Q_RANK)), full((MLA_KV_RANK, 1024)),
                  full((1, MLA_Q_RANK)), full((1, MLA_KV_RANK)), full((8, LANES)), full((8, LANES))],
        out_specs=(col(1024, 0),) * 6 + (col(MEM_WIDTH, 0), col(MLA_Q_RANK, 0), col(MLA_KV_RANK, 0)),
        compiler_params=_params(("parallel",)),
    )(proj, proj, proj, proj, proj, trig, w_uq, w_ukv, g_cq, g_ckv, rope_a, rope_b)


def _attn_fwd(q, k, v, *, nb, s, sk, heads, hpb, voff, bq, name):
    nq = s // bq
    width = hpb * LANES
    vblk = voff // hpb

    def body(q_ref, k_ref, v_ref, o_ref, lse_ref):
        for h in range(hpb):
            sl = slice(h * LANES, (h + 1) * LANES)
            sc = _dot_nt(q_ref[:, sl], k_ref[:, sl])
            m = jnp.max(sc, axis=1, keepdims=True)
            p = jnp.exp2(sc - m)
            l = jnp.sum(p, axis=1, keepdims=True)
            o_ref[:, sl] = _dot(p.astype(BF16), v_ref[:, sl]) / l
            lse_ref[:, sl] = jnp.broadcast_to(m + jnp.log(l) * LOG2E, (bq, LANES))

    out = jax.ShapeDtypeStruct((nb * s, heads * LANES), F32)
    ospec = pl.BlockSpec((bq, width), lambda b, i, g: (b * nq + i, g))
    return pl.pallas_call(
        body, name=name, grid=(nb, nq, heads // hpb),
        out_shape=(out, out),
        in_specs=[ospec, pl.BlockSpec((sk, width), lambda b, i, g: (b, g)),
                  pl.BlockSpec((sk, width), lambda b, i, g: (b, vblk + g))],
        out_specs=(ospec, ospec),
        compiler_params=_params(("parallel", "parallel", "parallel")),
    )(q, k, v)


def _attn_bwd(q, k, v, o, do, lse, *, nb, s, sk, heads, hpb, voff, scale, bq, name):
    nq = s // bq
    width = hpb * LANES
    vblk = voff // hpb

    def body(q_ref, k_ref, v_ref, o_ref, do_ref, lse_ref, dq_ref, dk_ref, dv_ref, dk_acc, dv_acc):
        i = pl.program_id(2)

        @pl.when(i == 0)
        def _():
            dk_acc[...] = jnp.zeros_like(dk_acc)
            dv_acc[...] = jnp.zeros_like(dv_acc)

        for h in range(hpb):
            sl = slice(h * LANES, (h + 1) * LANES)
            qh = q_ref[:, sl]
            kk = k_ref[:, sl]
            doh = do_ref[:, sl]
            delta = jnp.sum(doh.astype(F32) * o_ref[:, sl], axis=1, keepdims=True)
            p = jnp.exp2(_dot_nt(qh, kk) - lse_ref[:, h * LANES:h * LANES + 1])
            ds = (p * (_dot_nt(doh, v_ref[:, sl]) - delta)).astype(BF16)
            dq_ref[:, sl] = (_dot(ds, kk) * scale).astype(BF16)
            dk_acc[:, sl] += _dot_tn(ds, qh)
            dv_acc[:, sl] += _dot_tn(p.astype(BF16), doh)

        @pl.when(i == nq - 1)
        def _():
            dk_ref[...] = (dk_acc[...] * LN2).astype(BF16)
            dv_ref[...] = dv_acc[...].astype(BF16)

    qspec = pl.BlockSpec((bq, width), lambda b, g, i: (b * nq + i, g))
    kv_spec = pl.BlockSpec((sk, width), lambda b, g, i: (b, g))
    dq_shape = jax.ShapeDtypeStruct((nb * s, heads * LANES), BF16)
    dkv_shape = jax.ShapeDtypeStruct((nb * sk, heads * LANES), BF16)
    return pl.pallas_call(
        body, name=name, grid=(nb, heads // hpb, nq),
        out_shape=(dq_shape, dkv_shape, dkv_shape),
        in_specs=[qspec, kv_spec, pl.BlockSpec((sk, width), lambda b, g, i: (b, vblk + g)), qspec, qspec, qspec],
        out_specs=(qspec, kv_spec, kv_spec),
        scratch_shapes=[pltpu.VMEM((sk, width), F32), pltpu.VMEM((sk, width), F32)],
        compiler_params=_params(("parallel", "parallel", "arbitrary")),
    )(q, k, v, o, do, lse)


BAND_Q = 128
BAND_WIN = 256


def _band_start(i, s):
    return min(max(i * BAND_Q - 64, 0), s - BAND_WIN)


def _to_pattern_order(src_ref, dst_ref, stage_ref, s, d):
    length = s // d
    stage_ref[...] = src_ref[...].astype(F32)
    for r in range(d):
        dst_ref[r * length:(r + 1) * length, :] = stage_ref[pl.ds(r, length, stride=d), :].astype(dst_ref.dtype)


def _dilated_fwd(q, k, v, bias, bias_index, *, nb, s, name):
    nblk = s // BAND_Q
    npat = len(DILATED)

    def body(q_ref, k_ref, v_ref, bias_ref, o_ref, lse_ref, *rest):
        ordered = rest[:3 * (npat - 1)]
        stage_ref, op_ref, lp_ref, on_ref, ln_ref = rest[3 * (npat - 1):]
        lane = lax.broadcasted_iota(jnp.int32, (1, LANES), 1)
        first = lane < 64
        for p, (_, d) in enumerate(DILATED):
            if d == 1:
                qs, ks, vs = q_ref, k_ref, v_ref
            else:
                qs, ks, vs = ordered[3 * (p - 1):3 * p]
                for src, dst in ((q_ref, qs), (k_ref, ks), (v_ref, vs)):
                    _to_pattern_order(src, dst, stage_ref, s, d)
            for i in range(nblk):
                u0 = i * BAND_Q
                st = _band_start(i, s)
                qi = qs[u0:u0 + BAND_Q, :]
                kw = ks[st:st + BAND_WIN, :]
                vw = vs[st:st + BAND_WIN, :]
                zero = jnp.zeros_like(qi)
                q2 = jnp.concatenate([jnp.where(first, qi, zero), jnp.where(first, zero, qi)], axis=0)
                sc = _dot_nt(q2, kw)
                b = bias_ref[bias_index[p][i]]
                halves = []
                for h in range(2):
                    sh = sc[h * BAND_Q:(h + 1) * BAND_Q] + b
                    m = jnp.max(sh, axis=1, keepdims=True)
                    pr = jnp.exp2(sh - m)
                    l = jnp.sum(pr, axis=1, keepdims=True)
                    halves.append((pr.astype(BF16), l, m + jnp.log(l) * LOG2E))
                o2 = _dot(jnp.concatenate([halves[0][0], halves[1][0]], axis=0), vw)
                o_blk = jnp.where(first, o2[:BAND_Q] / halves[0][1], o2[BAND_Q:] / halves[1][1])
                lse_blk = jnp.where(first, jnp.broadcast_to(halves[0][2], (BAND_Q, LANES)),
                                    jnp.broadcast_to(halves[1][2], (BAND_Q, LANES)))
                op_ref[p, u0:u0 + BAND_Q, :] = o_blk
                lp_ref[p, u0:u0 + BAND_Q, :] = lse_blk
            if d > 1:
                length = s // d
                for r in range(d):
                    on_ref.at[p - 1][pl.ds(r, length, stride=d), :] = op_ref[p, r * length:(r + 1) * length, :]
                    ln_ref.at[p - 1][pl.ds(r, length, stride=d), :] = lp_ref[p, r * length:(r + 1) * length, :]
        lses = [lp_ref[0]] + [ln_ref[p] for p in range(npat - 1)]
        outs = [op_ref[0]] + [on_ref[p] for p in range(npat - 1)]
        m = functools.reduce(jnp.maximum, lses)
        ws = [jnp.exp2(l - m) for l in lses]
        den = functools.reduce(lambda a, c: a + c, ws)
        o_ref[...] = functools.reduce(lambda a, c: a + c, [w * o for w, o in zip(ws, outs)]) / den
        lse_ref[...] = m + jnp.log(den) * LOG2E

    blk = pl.BlockSpec((s, LANES), lambda b, g: (b, g))
    out = jax.ShapeDtypeStruct((nb * s, A_WIDTH), F32)
    copy = jax.ShapeDtypeStruct((nb * s, A_WIDTH), BF16)
    n_copies = 3 * (npat - 1)
    res = pl.pallas_call(
        body, name=name, grid=(nb, A_WIDTH // LANES),
        out_shape=(out, out) + (copy,) * n_copies,
        in_specs=[blk, blk, blk, pl.BlockSpec(bias.shape, lambda b, g: (0, 0, 0))],
        out_specs=(blk, blk) + (blk,) * n_copies,
        scratch_shapes=[pltpu.VMEM((s, LANES), F32), pltpu.VMEM((npat, s, LANES), F32),
                        pltpu.VMEM((npat, s, LANES), F32), pltpu.VMEM((npat - 1, s, LANES), F32),
                        pltpu.VMEM((npat - 1, s, LANES), F32)],
        compiler_params=_params(("parallel", "parallel")),
    )(q, k, v, bias)
    return res[0], res[1], res[2:]


def _dilated_bwd(q, k, v, ordered, o, do, lse, bias, bias_index, *, nb, s, scale, name):
    nblk = s // BAND_Q
    npat = len(DILATED)
    n_copies = 3 * (npat - 1)

    def body(q_ref, k_ref, v_ref, *rest):
        ordered_refs = rest[:n_copies]
        (o_ref, do_ref, lse_ref, bias_ref, dq_out, dk_out, dv_out, stage_ref, rs_ref, dop_ref, rsp_ref,
         dqp_ref, dkp_ref, dvp_ref, dq_ref, dk_ref, dv_ref, nat_ref) = rest[n_copies:]
        lane = lax.broadcasted_iota(jnp.int32, (1, LANES), 1)
        first = lane < 64
        prod = do_ref[...].astype(F32) * o_ref[...]
        d0 = jnp.sum(jnp.where(first, prod, 0.0), axis=1, keepdims=True)
        d1 = jnp.sum(jnp.where(first, 0.0, prod), axis=1, keepdims=True)
        delta = jnp.where(first, jnp.broadcast_to(d0, (s, LANES)), jnp.broadcast_to(d1, (s, LANES)))
        rs_ref[...] = jnp.where((lane & 32) == 0, lse_ref[...], delta)
        for p, (_, d) in enumerate(DILATED):
            length = s // d
            if d == 1:
                qs, ks, vs, dos, rss = q_ref, k_ref, v_ref, do_ref, rs_ref
                dqs, dks, dvs = dq_ref, dk_ref, dv_ref
            else:
                for src, dst in ((do_ref, dop_ref), (rs_ref, rsp_ref)):
                    _to_pattern_order(src, dst, stage_ref, s, d)
                qs, ks, vs = ordered_refs[3 * (p - 1):3 * p]
                dos, rss = dop_ref, rsp_ref
                dqs, dks, dvs = dqp_ref, dkp_ref, dvp_ref
            dks[...] = jnp.zeros((s, LANES), F32)
            dvs[...] = jnp.zeros((s, LANES), F32)
            for i in range(nblk):
                u0 = i * BAND_Q
                st = _band_start(i, s)
                qi = qs[u0:u0 + BAND_Q, :]
                doi = dos[u0:u0 + BAND_Q, :]
                kw = ks[st:st + BAND_WIN, :]
                vw = vs[st:st + BAND_WIN, :]
                zero = jnp.zeros_like(qi)
                q2 = jnp.concatenate([jnp.where(first, qi, zero), jnp.where(first, zero, qi)], axis=0)
                do2 = jnp.concatenate([jnp.where(first, doi, zero), jnp.where(first, zero, doi)], axis=0)
                sc = _dot_nt(q2, kw)
                dp = _dot_nt(do2, vw)
                b = bias_ref[bias_index[p][i]]
                rs_i = rss[u0:u0 + BAND_Q, :]
                ps, dss = [], []
                for h in range(2):
                    rows = slice(h * BAND_Q, (h + 1) * BAND_Q)
                    pr = jnp.exp2(sc[rows] + b - rs_i[:, 64 * h:64 * h + 1])
                    ps.append(pr.astype(BF16))
                    dss.append((pr * (dp[rows] - rs_i[:, 64 * h + 32:64 * h + 33])).astype(BF16))
                p2 = jnp.concatenate(ps, axis=0)
                ds2 = jnp.concatenate(dss, axis=0)
                dq2 = _dot(ds2, kw)
                dqs[u0:u0 + BAND_Q, :] = jnp.where(first, dq2[:BAND_Q], dq2[BAND_Q:]) * scale
                dks[st:st + BAND_WIN, :] += _dot_tn(ds2, q2)
                dvs[st:st + BAND_WIN, :] += _dot_tn(p2, do2)
            if d > 1:
                for j, src in enumerate((dqp_ref, dkp_ref, dvp_ref)):
                    for r in range(d):
                        nat_ref.at[p - 1, j][pl.ds(r, length, stride=d), :] = src[r * length:(r + 1) * length, :]

        def total(j, first_ref):
            return functools.reduce(lambda a, c: a + c, [first_ref[...]] + [nat_ref[p, j] for p in range(npat - 1)])

        dq_out[...] = total(0, dq_ref).astype(BF16)
        dk_out[...] = (total(1, dk_ref) * LN2).astype(BF16)
        dv_out[...] = total(2, dv_ref).astype(BF16)

    blk = pl.BlockSpec((s, LANES), lambda b, g: (b, g))
    out = jax.ShapeDtypeStruct((nb * s, A_WIDTH), BF16)
    f32_buf = pltpu.VMEM((s, LANES), F32)
    bf_buf = pltpu.VMEM((s, LANES), BF16)
    return pl.pallas_call(
        body, name=name, grid=(nb, A_WIDTH // LANES),
        out_shape=(out, out, out),
        in_specs=[blk] * (6 + n_copies) + [pl.BlockSpec(bias.shape, lambda b, g: (0, 0, 0))],
        out_specs=(blk, blk, blk),
        scratch_shapes=[f32_buf, f32_buf, bf_buf] + [f32_buf] * 7 + [pltpu.VMEM((npat - 1, 3, s, LANES), F32)],
        compiler_params=_params(("parallel", "parallel")),
    )(q, k, v, *ordered, o, do, lse, bias)


def _post(h32, ya, ybp, ym, proj, target, w_out, g_a, g_b, g_m, g_post, b_post, tm=256):
    t = h32.shape[0]

    def body(h_ref, ya_ref, yb_ref, ym_ref, ga_ref, gb_ref, gm_ref, tg_ref, wo_ref,
             goa_ref, gob_ref, gom_ref, gp_ref, bp_ref,
             y_ref, dz_ref, doa_ref, dob_ref, dom_ref, dga_ref, dgb_ref, dgm_ref,
             loss_ref, dgp_ref, dbp_ref, dgoa_ref, dgob_ref, dgom_ref):
        i = pl.program_id(0)

        @pl.when(i == 0)
        def _():
            for r in (loss_ref, dgp_ref, dbp_ref, dgoa_ref, dgob_ref, dgom_ref):
                r[...] = jnp.zeros_like(r)

        lane = lax.broadcasted_iota(jnp.int32, (1, LANES), 1)
        low = lane < 64
        h = h_ref[...]

        ybp_v = yb_ref[...]
        yb = jnp.concatenate(
            [jnp.where(low, pltpu.roll(ybp_v[:, 2 * j * LANES:(2 * j + 1) * LANES], 64, 1),
                       ybp_v[:, (2 * j + 1) * LANES:(2 * j + 2) * LANES]) for j in range(4)], axis=1)

        def gated(raw, gate, gain, width):
            xh, r = _rms_hat(raw, width)
            n = xh * gain
            sg = 1.0 / (1.0 + jnp.exp(-gate))
            return xh, r, n, sg, n * (gate * sg)

        gate_a, gate_b, gate_m = ga_ref[...], gb_ref[...], gm_ref[...]
        xh_a, r_a, n_a, sg_a, y_a = gated(ya_ref[...], gate_a, goa_ref[...], A_WIDTH)
        xh_b, r_b, n_b, sg_b, y_b = gated(yb, gate_b, gob_ref[...], 512)
        xh_m, r_m, n_m, sg_m, y_m = gated(ym_ref[...], gate_m, gom_ref[...], 512)
        y = jnp.concatenate([y_a, y_b, y_m], axis=1).astype(BF16)
        y_ref[...] = y
        z = DEEPNORM_ALPHA * h + _dot(y, wo_ref[...])
        zh, rstd = _ln_hat(z)
        err = zh * gp_ref[...] + bp_ref[...] - tg_ref[...]
        rows = jnp.sum(err * err, axis=1, keepdims=True)
        loss_ref[...] += jnp.broadcast_to(jnp.sum(rows, axis=0, keepdims=True) * (0.5 / D_MODEL), (1, LANES))
        dout = err * (1.0 / D_MODEL)
        dgp_ref[...] += _colsum(dout * zh)
        dbp_ref[...] += _colsum(dout)
        dz = _ln_bwd_rows(dout * gp_ref[...], zh, rstd)
        dz_ref[...] = dz
        dy = _dot_nt(dz.astype(BF16), wo_ref[...])

        def gated_bwd(dyg, xh, r, n, sg, gate, gain, width, dgain_ref):
            dn = dyg * (gate * sg)
            dgate = dyg * n * (sg * (1.0 + gate * (1.0 - sg)))
            dgain_ref[...] += _colsum(dn * xh)
            return _rms_bwd(dn * gain, xh, r, width), dgate

        dya, dgate_a = gated_bwd(dy[:, 0:1024], xh_a, r_a, n_a, sg_a, gate_a, goa_ref[...], A_WIDTH, dgoa_ref)
        dyb, dgate_b = gated_bwd(dy[:, 1024:1536], xh_b, r_b, n_b, sg_b, gate_b, gob_ref[...], 512, dgob_ref)
        dym, dgate_m = gated_bwd(dy[:, 1536:2048], xh_m, r_m, n_m, sg_m, gate_m, gom_ref[...], 512, dgom_ref)
        doa_ref[...] = dya.astype(BF16)
        dom_ref[...] = dym.astype(BF16)
        dga_ref[...] = dgate_a.astype(BF16)
        dgb_ref[...] = dgate_b.astype(BF16)
        dgm_ref[...] = dgate_m.astype(BF16)
        for j in range(4):
            blk = dyb[:, j * LANES:(j + 1) * LANES]
            dob_ref[:, 2 * j * LANES:(2 * j + 1) * LANES] = jnp.where(low, 0.0, pltpu.roll(blk, 64, 1)).astype(BF16)
            dob_ref[:, (2 * j + 1) * LANES:(2 * j + 2) * LANES] = jnp.where(low, 0.0, blk).astype(BF16)

    def col(width, idx):
        return pl.BlockSpec((tm, width), lambda i: (i, idx))

    def full(shape):
        return pl.BlockSpec(shape, lambda i: (0, 0))

    def acc(width):
        return jax.ShapeDtypeStruct((1, width), F32)

    return pl.pallas_call(
        body, name="post", grid=(t // tm,),
        out_shape=(jax.ShapeDtypeStruct((t, 2048), BF16), jax.ShapeDtypeStruct((t, 1024), F32),
                   jax.ShapeDtypeStruct((t, 1024), BF16), jax.ShapeDtypeStruct((t, 1024), BF16),
                   jax.ShapeDtypeStruct((t, 512), BF16),
                   jax.ShapeDtypeStruct((t, 1024), BF16), jax.ShapeDtypeStruct((t, 512), BF16),
                   jax.ShapeDtypeStruct((t, 512), BF16),
                   acc(LANES), acc(1024), acc(1024), acc(1024), acc(512), acc(512)),
        in_specs=[col(1024, 0), col(1024, 0), col(1024, 0), col(512, 0),
                  col(1024, 3), col(512, COL_BG // 512), col(512, COL_MG // 512), col(1024, 0),
                  full((2048, 1024)),
                  full((1, 1024)), full((1, 512)), full((1, 512)), full((1, 1024)), full((1, 1024))],
        out_specs=(col(2048, 0), col(1024, 0), col(1024, 0), col(1024, 0), col(512, 0),
                   col(1024, 0), col(512, 0), col(512, 0),
                   full((1, LANES)), full((1, 1024)), full((1, 1024)), full((1, 1024)), full((1, 512)),
                   full((1, 512))),
        compiler_params=_params(("arbitrary",)),
    )(h32, ya, ybp, ym, proj, proj, proj, target, w_out, g_a, g_b, g_m, g_post, b_post)


def _prep_bwd(dqa, dka, dva, dqb, dkb, dvb, dqm, dga, dgb, dgm, proj, trig, w_uq, w_ukv, g_cq, g_ckv,
              rope_a, rope_b, tm=512):
    t = proj.shape[0]

    def body(dqa_ref, dka_ref, dva_ref, dqb_ref, dkb_ref, dvb_ref, dqm_ref, dga_ref, dgb_ref, dgm_ref,
             bs_ref, trig_ref, wuq_ref, wukv_ref, gcq_ref, gckv_ref, ra_ref, rb_ref,
             dproj_ref, dqf_ref, dkv_ref, dgcq_ref, dgckv_ref):
        i = pl.program_id(0)

        @pl.when(i == 0)
        def _():
            dgcq_ref[...] = jnp.zeros_like(dgcq_ref)
            dgckv_ref[...] = jnp.zeros_like(dgckv_ref)

        ta = _rope_tables(trig_ref[:, 0:LANES], trig_ref[:, LANES:2 * LANES], ra_ref[...])
        tb = _rope_tables(trig_ref[:, 2 * LANES:3 * LANES], trig_ref[:, 3 * LANES:4 * LANES], rb_ref[...])
        for j in range(A_WIDTH // LANES):
            sl = slice(j * LANES, (j + 1) * LANES)
            dproj_ref[:, j * LANES:(j + 1) * LANES] = (
                _rope(dqa_ref[:, sl].astype(F32), ta, 8, inverse=True).astype(BF16))
            dproj_ref[:, 1024 + j * LANES:1024 + (j + 1) * LANES] = (
                _rope(dka_ref[:, sl].astype(F32), ta, 8, inverse=True).astype(BF16))
        dproj_ref[:, 2048:3072] = dva_ref[...]
        dproj_ref[:, 3072:4096] = dga_ref[...]

        lane = lax.broadcasted_iota(jnp.int32, (1, LANES), 1)
        low = lane < 64
        rope_lanes = (lane >= 64) & (lane < 96)
        dkr = jnp.zeros((tm, LANES), F32)
        for h in range(MLA_HEADS):
            sl = slice(h * LANES, (h + 1) * LANES)
            dqf_ref[:, sl] = _rope(dqb_ref[:, sl].astype(F32), tb, 16, inverse=True).astype(BF16)
            dk_h = dkb_ref[:, sl]
            dkv_ref[:, sl] = jnp.where(low, dk_h, dvb_ref[:, sl])
            dkr = dkr + jnp.where(rope_lanes, dk_h.astype(F32), 0.0)
        dkr = _rope(dkr, tb, 16, inverse=True)

        cq_hat, r_q = _rms_hat(bs_ref[:, 0:MLA_Q_RANK], MLA_Q_RANK)
        dcqn = _dot(dqf_ref[...], wuq_ref[...])
        dgcq_ref[...] += _colsum(dcqn * cq_hat)
        dproj_ref[:, COL_CQ:COL_CQ + 256] = _rms_bwd(dcqn * gcq_ref[...], cq_hat, r_q, MLA_Q_RANK).astype(BF16)
        ckv_hat, r_kv = _rms_hat(bs_ref[:, MLA_Q_RANK:MLA_Q_RANK + MLA_KV_RANK], MLA_KV_RANK)
        dckvn = _dot_nt(dkv_ref[...], wukv_ref[...])
        dgckv_ref[...] += _colsum(dckvn * ckv_hat)
        dproj_ref[:, COL_CQ + 256:COL_CQ + 384] = (
            _rms_bwd(dckvn * gckv_ref[...], ckv_hat, r_kv, MLA_KV_RANK).astype(BF16))
        dproj_ref[:, COL_CQ + 384:COL_CQ + 512] = dkr.astype(BF16)
        dproj_ref[:, COL_BG:COL_BG + 512] = dgb_ref[...]
        dproj_ref[:, COL_MQ:COL_MQ + 512] = dqm_ref[...]
        dproj_ref[:, COL_MG:COL_MG + 512] = dgm_ref[...]

    def col(width, idx):
        return pl.BlockSpec((tm, width), lambda i: (i, idx))

    def full(shape):
        return pl.BlockSpec(shape, lambda i: (0, 0))

    return pl.pallas_call(
        body, name="prep_bwd", grid=(t // tm,),
        out_shape=(jax.ShapeDtypeStruct((t, PROJ_W), BF16), jax.ShapeDtypeStruct((t, 1024), BF16),
                   jax.ShapeDtypeStruct((t, 1024), BF16),
                   jax.ShapeDtypeStruct((1, MLA_Q_RANK), F32), jax.ShapeDtypeStruct((1, MLA_KV_RANK), F32)),
        in_specs=[col(1024, 0)] * 6 + [col(512, 0), col(1024, 0), col(512, 0), col(512, 0),
                  col(512, COL_CQ // 512), pl.BlockSpec((tm, 4 * LANES), lambda i: (i, 0)),
                  full((1024, MLA_Q_RANK)), full((MLA_KV_RANK, 1024)),
                  full((1, MLA_Q_RANK)), full((1, MLA_KV_RANK)), full((8, LANES)), full((8, LANES))],
        out_specs=(col(PROJ_W, 0), col(1024, 0), col(1024, 0), full((1, MLA_Q_RANK)), full((1, MLA_KV_RANK))),
        compiler_params=_params(("arbitrary",)),
    )(dqa, dka, dva, dqb, dkb, dvb, dqm, dga, dgb, dgm, proj, trig, w_uq, w_ukv, g_cq, g_ckv, rope_a, rope_b)


def _adamw_math(gv, w, m, v):
    m_new = ADAM_B1 * m + (1.0 - ADAM_B1) * gv
    v_new = ADAM_B2 * v + (1.0 - ADAM_B2) * (gv * gv)
    m_hat = m_new / (1.0 - ADAM_B1 ** ADAM_STEP)
    v_hat = v_new / (1.0 - ADAM_B2 ** ADAM_STEP)
    return -ADAM_LR * (m_hat / (jnp.sqrt(v_hat) + ADAM_EPS) + ADAM_WD * w), m_new, v_new


def _adamw(g, w, m, v, tr, name):
    r, cols = w.shape

    def body(g_ref, w_ref, m_ref, v_ref, go_ref, d_ref, nm_ref, nv_ref):
        gv = g_ref[...]
        go_ref[...] = gv
        d_ref[...], nm_ref[...], nv_ref[...] = _adamw_math(gv, w_ref[...], m_ref[...], v_ref[...])

    tile = pl.BlockSpec((tr, cols), lambda i: (i, 0))
    shape = jax.ShapeDtypeStruct((r, cols), F32)
    return pl.pallas_call(
        body, name=name, grid=(r // tr,),
        out_shape=(shape,) * 4, in_specs=[tile] * 4, out_specs=(tile,) * 4,
        compiler_params=_params(("parallel",)),
    )(g, w, m, v)


def _adamw_pieces(g, w, m, v, pieces, name):
    n = len(pieces)
    per_piece = isinstance(w, (list, tuple))
    shapes = [jax.ShapeDtypeStruct((r1 - r0, c1 - c0), F32) for r0, r1, c0, c1 in pieces]
    args = (g, *w, *m, *v) if per_piece else (g, w, m, v)

    def body(g_ref, *refs):
        ins, outs = refs[:len(args) - 1], refs[len(args) - 1:]
        gv = g_ref[...]
        if not per_piece:
            results = (gv,) + _adamw_math(gv, ins[0][...], ins[1][...], ins[2][...])
        for p, (r0, r1, c0, c1) in enumerate(pieces):
            if per_piece:
                gp = gv[r0:r1, c0:c1]
                vals = (gp,) + _adamw_math(gp, ins[p][...], ins[n + p][...], ins[2 * n + p][...])
            else:
                vals = [full[r0:r1, c0:c1] for full in results]
            for kind, val in enumerate(vals):
                outs[kind * n + p][...] = val

    flat = pl.pallas_call(
        body, name=name, out_shape=tuple(shapes) * 4,
        in_specs=[IN_VMEM] * len(args), out_specs=tuple([IN_VMEM] * (4 * n)),
        compiler_params=_params(None),
    )(*args)
    return [[flat[kind * n + p] for kind in range(4)] for p in range(n)]


def _core_sum(g, recv, core, rows, tr, name, ride=None):
    cols = g.shape[2]
    nblk = rows // tr
    n_in = len(ride.args) if ride else 0
    n_out = len(ride.out_shapes) if ride else 0

    def body(c_ref, g_ref, r_ref, *rest):
        sf_ref, sb_ref = rest[n_in], rest[n_in + 1]
        if ride:
            j, i = pl.program_id(0), pl.program_id(1)
            ride.run(j * nblk + i, 4 * nblk, rest[:n_in], rest[n_in + 2:n_in + 2 + n_out],
                     rest[n_in + 2 + n_out:])
        tot = g_ref[...] + r_ref[...]
        sf_ref[...] = tot
        sb_ref[...] = tot.astype(BF16)

    half = pl.BlockSpec((None, tr, cols), lambda j, i, c_ref: (j, i, 0))
    shapes = (jax.ShapeDtypeStruct((4, rows, cols), F32), jax.ShapeDtypeStruct((4, rows, cols), BF16))
    return pl.pallas_call(
        body, name=name,
        grid_spec=pltpu.PrefetchScalarGridSpec(
            num_scalar_prefetch=1, grid=(4, nblk),
            in_specs=[pl.BlockSpec((None, tr, cols), lambda j, i, c_ref: (j, c_ref[0] * nblk + i, 0)), half]
            + (ride.in_specs if ride else []),
            out_specs=(half, half) + (ANY,) * n_out,
            scratch_shapes=ride.scratch() if ride else []),
        out_shape=shapes + tuple(ride.out_shapes if ride else ()),
        compiler_params=_params(("arbitrary", "arbitrary") if ride else ("parallel", "parallel")),
    )(core, g, recv, *(ride.args if ride else ()))


def _half_to_sibling(g4):
    def plan(in_refs, out_refs, send_sems, recv_sems):
        x, y, c = _position()
        cp = pltpu.make_async_remote_copy(
            src_ref=in_refs[0].at[:, 1 - c], dst_ref=out_refs[0], send_sem=send_sems.at[0],
            recv_sem=recv_sems.at[0], device_id=(x, y, 1 - c), device_id_type=MESH)

        def finish():
            cp.wait_recv()
            cp.wait_send()

        return cp.start, finish

    return _Ride([g4], [jax.ShapeDtypeStruct((4, g4.shape[2], 1024), F32)], (1, 1), plan)


def _gather_plan(src_ref, dst_ref, send_sems, recv_sems, local_sems):
    x, y, c = _position()
    me = 2 * x + y
    rows = src_ref.shape[1]
    cut = -(-rows // 32) * 16
    pieces = (pl.ds(0, cut), pl.ds(cut, rows - cut))
    local = pltpu.make_async_copy(src_ref, dst_ref.at[me], local_sems.at[0])

    def over_ici(sem, k, chip, t, src=None):
        where = dst_ref.at[chip, c, pieces[t]]
        return pltpu.make_async_remote_copy(
            src_ref=where if src is None else src, dst_ref=where, send_sem=send_sems.at[sem],
            recv_sem=recv_sems.at[sem], device_id=(x ^ (k >> 1), y ^ (k & 1), c), device_id_type=MESH)

    def mine_to(k, t):
        return over_ici(2 * (k - 1) + t, k, me, t, src=src_ref.at[c, pieces[t]])

    def from_neighbour(k, t):
        return over_ici(2 * (k - 1) + t, k, me ^ k, t)

    def to_sibling(k, half):
        piece = dst_ref.at[me ^ k, half]
        return pltpu.make_async_remote_copy(
            src_ref=piece, dst_ref=piece, send_sem=send_sems.at[5 + k], recv_sem=recv_sems.at[5 + k],
            device_id=(x, y, 1 - c), device_id_type=MESH)

    sends = [mine_to(2, 0), mine_to(1, 1), mine_to(2, 1), mine_to(1, 0)]
    onward = [over_ici(4, 1, me ^ 2, 0), over_ici(5, 2, me ^ 1, 1)]

    def start():
        local.start()
        for cp in sends:
            cp.start()

    def pass_on():
        from_neighbour(2, 0).wait_recv()
        onward[0].start()
        from_neighbour(1, 1).wait_recv()
        onward[1].start()

    def to_other_core():
        from_neighbour(2, 1).wait_recv()
        to_sibling(2, c).start()
        from_neighbour(1, 0).wait_recv()
        to_sibling(1, c).start()
        over_ici(4, 1, me ^ 3, 0).wait_recv()
        over_ici(5, 2, me ^ 3, 1).wait_recv()
        to_sibling(3, c).start()

    def finish():
        for k in (1, 2, 3):
            to_sibling(k, 1 - c).wait_recv()
        for cp in sends + onward + [to_sibling(k, c) for k in (1, 2, 3)]:
            cp.wait_send()
        local.wait()

    return start, pass_on, to_other_core, finish


def _gather_ride(shard, spread):
    def plan(in_refs, out_refs, send_sems, recv_sems, local_sems):
        return _gather_plan(in_refs[0], out_refs[0], send_sems, recv_sems, local_sems)

    return _Ride([shard], [jax.ShapeDtypeStruct((4,) + shard.shape, shard.dtype)], (9, 9, 1), plan,
                 in_specs=[IN_VMEM], spread=spread)


def _chip_sum(sf, recv, chip, rows, tr, name):
    cols = sf.shape[2]
    n_recv = recv.shape[0]

    def body(me_ref, sf_ref, r_ref, out_ref):
        acc = sf_ref[...]
        for k in range(n_recv):
            acc = acc + r_ref[k].astype(F32)
        out_ref[...] = acc

    return pl.pallas_call(
        body, name=name,
        grid_spec=pltpu.PrefetchScalarGridSpec(
            num_scalar_prefetch=1, grid=(rows // tr,),
            in_specs=[pl.BlockSpec((None, tr, cols), lambda i, me_ref: (me_ref[0], i, 0)),
                      pl.BlockSpec((n_recv, tr, cols), lambda i, me_ref: (0, i, 0))],
            out_specs=pl.BlockSpec((tr, cols), lambda i, me_ref: (i, 0))),
        out_shape=jax.ShapeDtypeStruct((rows, cols), F32),
        compiler_params=_params(("parallel",)),
    )(chip, sf, recv)


def _position():
    return lax.axis_index("x"), lax.axis_index("y"), lax.axis_index("c")


def _dh_scatter(dproj, w_in_arr_t, x, dz, g, sb_in, sb_rest, tm=512, tk=3072):
    t, d = x.shape
    nk = dproj.shape[1] // tk
    ni = t // tm
    total = ni * nk
    halves = (HALF_IN, HALF_REST)
    cuts = tuple(-(-rows // 32) * 16 for rows in halves)

    def rows_of(a, p):
        return cuts[a] if p == 0 else halves[a] - cuts[a]

    def piece(a, p):
        return pl.ds(0, cuts[a]) if p == 0 else pl.ds(cuts[a], halves[a] - cuts[a])

    def body(dp_ref, w_ref, x_ref, dz_ref, g_ref, sbin_ref, sbrest_ref, dx_ref, dg_ref, db_ref, rin_ref, rrest_ref,
             acc_ref, pay_in0, pay_in1, pay_rest0, pay_rest1, own_in0, own_in1, own_rest0, own_rest1,
             send_sems, recv_sems, local_sems):
        step = pl.program_id(0) * nk + pl.program_id(1)
        kk = pl.program_id(1)
        px, py, pc = _position()
        me = 2 * px + py
        srcs = (sbin_ref, sbrest_ref)
        dsts = (rin_ref, rrest_ref)
        pays = ((pay_in0, pay_in1), (pay_rest0, pay_rest1))
        owns = ((own_in0, own_in1), (own_rest0, own_rest1))
        via = (2, 1)
        onto = (1, 2)

        def peer(k):
            return (px ^ (k >> 1), py ^ (k & 1), pc)

        def payload(a, p):
            return pltpu.make_async_remote_copy(
                src_ref=srcs[a].at[me ^ 3, piece(a, p)], dst_ref=pays[a][p], send_sem=send_sems.at[2 * a + p],
                recv_sem=recv_sems.at[2 * a + p], device_id=peer(via[p]), device_id_type=MESH)

        def direct(a, k, p, src):
            sem = 4 + 4 * a + 2 * (k - 1) + p
            return pltpu.make_async_remote_copy(
                src_ref=src, dst_ref=dsts[a].at[k - 1, piece(a, p)], send_sem=send_sems.at[sem],
                recv_sem=recv_sems.at[sem], device_id=peer(k), device_id_type=MESH)

        def plain(a, k, p):
            return direct(a, k, p, srcs[a].at[me ^ k, piece(a, p)])

        def stage(a, p):
            return pltpu.make_async_copy(srcs[a].at[me ^ onto[p], piece(a, p)], owns[a][p], local_sems.at[2 * a + p])

        @pl.when(step == 0)
        def _():
            dg_ref[...] = jnp.zeros_like(dg_ref)
            db_ref[...] = jnp.zeros_like(db_ref)
            for a in range(2):
                for p in range(2):
                    payload(a, p).start()
                    stage(a, p).start()
                plain(a, 1, 1).start()
                plain(a, 2, 0).start()

        @pl.when(step == (5 * total) // 8)
        def _():
            for a in range(2):
                for p in range(2):
                    payload(a, p).wait_recv()
                    stage(a, p).wait()
                    owns[a][p][...] = (owns[a][p][...].astype(F32) + pays[a][p][...].astype(F32)).astype(BF16)
                    direct(a, onto[p], p, owns[a][p]).start()

        part = _dot(dp_ref[...], w_ref[...])

        @pl.when(kk == 0)
        def _():
            acc_ref[...] = part

        @pl.when(kk > 0)
        def _():
            acc_ref[...] += part

        @pl.when(kk == nk - 1)
        def _():
            xh, rstd = _ln_hat(x_ref[...])
            dht = acc_ref[...] + DEEPNORM_ALPHA * dz_ref[...]
            dg_ref[...] += _colsum(dht * xh)
            db_ref[...] += _colsum(dht)
            dx_ref[...] = _ln_bwd_rows(dht * g_ref[...], xh, rstd)

        @pl.when(step == total - 1)
        def _():
            for a in range(2):
                for k in (1, 2):
                    for p in range(2):
                        plain(a, k, p).wait_recv()
            for a in range(2):
                for p in range(2):
                    payload(a, p).wait_send()
                    direct(a, onto[p], p, owns[a][p]).wait_send()
                plain(a, 1, 1).wait_send()
                plain(a, 2, 0).wait_send()

    tile = pl.BlockSpec((tm, d), lambda i, kk: (i, 0))
    row = pl.BlockSpec((1, d), lambda i, kk: (0, 0))
    pieces = [pltpu.VMEM((rows_of(a, p), 1024), BF16) for a in range(2) for p in range(2)]
    return pl.pallas_call(
        body, name="dh_scatter", grid=(ni, nk),
        out_shape=(jax.ShapeDtypeStruct((t, d), F32), jax.ShapeDtypeStruct((1, d), F32),
                   jax.ShapeDtypeStruct((1, d), F32),
                   jax.ShapeDtypeStruct((2, HALF_IN, 1024), BF16),
                   jax.ShapeDtypeStruct((2, HALF_REST, 1024), BF16)),
        in_specs=[pl.BlockSpec((tm, tk), lambda i, kk: (i, kk)), pl.BlockSpec((tk, d), lambda i, kk: (kk, 0)),
                  tile, tile, row, ANY, ANY],
        out_specs=(tile, row, row, ANY, ANY),
        scratch_shapes=[pltpu.VMEM((tm, d), F32)] + pieces + pieces
        + [pltpu.SemaphoreType.DMA((12,)), pltpu.SemaphoreType.DMA((12,)), pltpu.SemaphoreType.DMA((4,))],
        compiler_params=_params(("arbitrary", "arbitrary")),
    )(dproj, w_in_arr_t, x, dz, g, sb_in, sb_rest)


def _join_and_allreduce(gh_in, gh_rest, vec):
    def body(hin_ref, hrest_ref, vec_ref, oin_ref, orest_ref, sum_ref, all_ref, send_sems, recv_sems, local_sems):
        x, y, c = _position()
        srcs = (hin_ref, hrest_ref)
        dsts = (oin_ref, orest_ref)
        me = 4 * x + 2 * y + c
        all_ref[me] = vec_ref[...]

        def small(k, slot):
            return pltpu.make_async_remote_copy(
                src_ref=vec_ref, dst_ref=all_ref.at[slot], send_sem=send_sems.at[k + 1], recv_sem=recv_sems.at[k + 1],
                device_id=(x ^ (k >> 2), y ^ ((k >> 1) & 1), c ^ (k & 1)), device_id_type=MESH)

        def half(a, slot):
            return pltpu.make_async_remote_copy(
                src_ref=srcs[a], dst_ref=dsts[a].at[slot], send_sem=send_sems.at[a], recv_sem=recv_sems.at[a],
                device_id=(x, y, 1 - c), device_id_type=MESH)

        local = [pltpu.make_async_copy(srcs[a], dsts[a].at[c], local_sems.at[a]) for a in range(2)]
        remote = [half(a, c) for a in range(2)] + [small(k, me) for k in range(1, 8)]
        for cp in local + remote:
            cp.start()
        for k in range(1, 8):
            small(k, me ^ k).wait_recv()
        for a in range(2):
            half(a, 1 - c).wait_recv()
        for cp in remote:
            cp.wait_send()
        for cp in local:
            cp.wait()
        total = all_ref[0]
        for d in range(1, 8):
            total = total + all_ref[d]
        sum_ref[...] = total

    return pl.pallas_call(
        body, name="join_halves",
        out_shape=(jax.ShapeDtypeStruct((2, HALF_IN, 1024), F32),
                   jax.ShapeDtypeStruct((2, HALF_REST, 1024), F32),
                   jax.ShapeDtypeStruct(vec.shape, vec.dtype)),
        in_specs=[IN_VMEM, IN_VMEM, IN_VMEM], out_specs=(ANY, ANY, IN_VMEM),
        scratch_shapes=[pltpu.VMEM((8,) + vec.shape, vec.dtype), pltpu.SemaphoreType.DMA((9,)),
                        pltpu.SemaphoreType.DMA((9,)), pltpu.SemaphoreType.DMA((2,))],
    )(gh_in, gh_rest, vec)


def _pack_rest(w_uq, w_ukv, w_mem, w_out):
    rows = jnp.concatenate([w_uq[0].T.reshape(-1, 1024), w_ukv.reshape(-1, 1024), w_mem.reshape(-1, 1024),
                            w_out.reshape(-1, 1024)], axis=0)
    return jnp.pad(rows, ((0, ROWS_REST - ROWS_USED), (0, 0)))


def _arranged_w_in(g_in):
    z = functools.partial(jnp.zeros, dtype=g_in.dtype)
    cut = 4480 - 2 * SHARD_ROWS
    return jnp.concatenate(
        [g_in[0, :SHARD_ROWS], g_in[1, :SHARD_ROWS], g_in[2, :cut], z((64, 1024)), g_in[2, cut:cut + 32],
         z((32, 1024)), g_in[2, cut + 32:SHARD_ROWS], g_in[3, :SHARD_ROWS]], axis=0)


def _rest_weights(g_rest):
    w_uq_t = g_rest[:, 0:ROWS_UQ].reshape(768, 256)
    w_uq_pad_t = jnp.pad(w_uq_t.reshape(MLA_HEADS, MLA_QK_DIM, 256), ((0, 0), (0, 32), (0, 0))).reshape(1024, 256)
    w_ukv = jnp.concatenate([g_rest[j, ROWS_UQ:ROWS_UQ + ROWS_UKV].reshape(128, 256) for j in range(4)], axis=1)
    lo = ROWS_UQ + ROWS_UKV
    w_mem = g_rest[:, lo:lo + ROWS_MEM].reshape(4 * ROWS_MEM, 1024)
    w_out = g_rest[:, lo + ROWS_MEM:lo + ROWS_MEM + ROWS_OUT].reshape(4 * ROWS_OUT, 1024)
    return w_uq_pad_t, w_ukv, w_mem, w_out


def _split_in(dw_in_arr_t):
    a = dw_in_arr_t
    gap = jnp.zeros((ROWS_IN - SHARD_ROWS, 1024), a.dtype)
    nat = 4608 - 96
    pieces = [a[:SHARD_ROWS], gap, a[SHARD_ROWS:2 * SHARD_ROWS], gap,
              a[2 * SHARD_ROWS:4480], a[4544:4576], a[4608:4608 + 3 * SHARD_ROWS - nat], gap,
              a[4608 + 3 * SHARD_ROWS - nat:], gap]
    return jnp.concatenate(pieces, axis=0).reshape(4, ROWS_IN, 1024)


def _split_rest(dw_uq_pad_t, dw_ukv, dw_mem, dw_out):
    dw_uq_t = dw_uq_pad_t.reshape(MLA_HEADS, LANES, 256)[:, :MLA_QK_DIM].reshape(4, ROWS_UQ, 1024)
    parts = [dw_uq_t, dw_ukv.reshape(128, 4, 256).transpose(1, 0, 2).reshape(4, ROWS_UKV, 1024),
             dw_mem.reshape(4, ROWS_MEM, 1024), dw_out.reshape(4, ROWS_OUT, 1024)]
    return jnp.pad(jnp.concatenate(parts, axis=1), ((0, 0), (0, ROWS_REST - ROWS_USED), (0, 0)))


def _rope_consts(rot, first, period):
    half = rot // 2
    inv_freq = np.float32(ROPE_THETA) ** (-(np.arange(0, rot, 2, dtype=np.float32) / np.float32(rot)))
    lane = np.arange(LANES) % period - first
    in_rot = (lane >= 0) & (lane < rot)
    out = np.zeros((8, LANES), np.float32)
    out[0] = np.where(in_rot, inv_freq[np.clip(lane, 0, rot - 1) % half], 0.0)
    out[1] = in_rot & (lane < half)
    out[2] = in_rot & (lane >= half)
    return jnp.asarray(out)


def _band_bias(s):
    nblk = s // BAND_Q
    starts = np.array([_band_start(i, s) for i in range(nblk)])
    uq = (np.arange(nblk)[:, None] * BAND_Q + np.arange(BAND_Q)[None, :])[:, :, None]
    uk = (starts[:, None] + np.arange(BAND_WIN)[None, :])[:, None, :]
    tiles, index, seen = [], [], {}
    for _, d in DILATED:
        length = s // d
        ok = (uq // length == uk // length) & (np.abs(uq - uk) <= 64)
        row = []
        for i in range(nblk):
            key = ok[i].tobytes()
            if key not in seen:
                seen[key] = len(tiles)
                tiles.append(np.where(ok[i], 0.0, NEG_INF).astype(np.float32))
            row.append(seen[key])
        index.append(row)
    return jnp.asarray(np.stack(tiles, axis=0)), index


def _forward_backward(h, h32, proj, trig, rope_consts, x, mem, target, weights, gains):
    w_uq_pad_t, w_ukv, w_mem, w_out = weights
    g_emb, b_emb, g_cq, g_ckv, g_out_a, g_out_b, g_out_m, g_post, b_post = gains
    nb, s, d = x.shape
    t = nb * s
    x2 = x.reshape(t, d)
    mem2 = mem.reshape(nb * N_MEM, d)
    tgt2 = target.reshape(t, d)
    rope_a, rope_b = rope_consts
    bias, bias_index = _band_bias(s)
    scales = (0.125, MLA_QK_DIM ** -0.5, 128 ** -0.5)

    qa, ka, va, qb, kb, vb, qm, cqn, ckvn = _prep(proj, trig, w_uq_pad_t, w_ukv, g_cq, g_ckv, rope_a, rope_b, scales)
    mkv = _mm(mem2, w_mem, BF16, nb * N_MEM, 1024, 1024, "mem_kv")

    cfg_b = dict(nb=nb, s=s, sk=s, heads=8, voff=0, bq=256)
    cfg_m = dict(nb=nb, s=s, sk=N_MEM, heads=4, hpb=2, voff=4, bq=1024)
    ya, lse_a, qkv_ordered = _dilated_fwd(qa, ka, va, bias, bias_index, nb=nb, s=s, name="attn_a_fwd")
    yb, lse_b = _attn_fwd(qb, kb, vb, name="attn_b_fwd", hpb=4, **cfg_b)
    ym, lse_m = _attn_fwd(qm, mkv, mkv, name="attn_m_fwd", **cfg_m)

    (y, dz, doa, dob, dom, dga, dgb, dgm, loss, dg_post, db_post, dg_a, dg_b, dg_m) = _post(
        h32, ya, yb, ym, proj, tgt2, w_out, g_out_a, g_out_b, g_out_m, g_post, b_post)

    dqa, dka, dva = _dilated_bwd(qa, ka, va, qkv_ordered, ya, doa, lse_a, bias, bias_index, nb=nb, s=s, scale=scales[0],
                                 name="attn_a_bwd")
    dqb, dkb, dvb = _attn_bwd(qb, kb, vb, yb, dob, lse_b, name="attn_b_bwd", scale=scales[1], hpb=4, **cfg_b)
    dqm, dmk, dmv = _attn_bwd(qm, mkv, mkv, ym, dom, lse_m, name="attn_m_bwd", scale=scales[2], **cfg_m)
    dmkv = jnp.concatenate([dmk, dmv], axis=1)

    dproj, dqf, dkv, dg_cq, dg_ckv = _prep_bwd(
        dqa, dka, dva, dqb, dkb, dvb, dqm, dga, dgb, dgm, proj, trig, w_uq_pad_t, w_ukv, g_cq, g_ckv, rope_a, rope_b)

    small_rows = (dg_cq, dg_ckv, loss, dg_a, dg_b, dg_m, dg_post, db_post)
    return (dproj, h, y, dz, dqf, cqn, ckvn, dkv, mem2, dmkv), x2, small_rows


def _weight_grads(operands, core):
    dproj, h, y, dz, dqf, cqn, ckvn, dkv, mem2, dmkv = operands
    dw_in_arr_t = _mm(dproj, h, F32, 1024, 1024, 4096, "dw_in", mode="tn")
    g_in = _split_in(dw_in_arr_t)
    dw_out, r_in = _mm(y, dz, F32, 1024, 1024, 2048, "dw_out", mode="tn",
                       ride=_half_to_sibling(g_in.reshape(4, 2, HALF_IN, 1024)))
    dw_uq_pad_t = _mm(dqf, cqn, F32, 1024, 256, 4096, "dw_uq", mode="tn")
    dw_ukv = _mm(ckvn, dkv, F32, 128, 1024, 4096, "dw_ukv", mode="tn")
    dw_mem = _mm(mem2, dmkv, F32, 1024, 1024, mem2.shape[0], "dw_mem", mode="tn")
    g_rest = _split_rest(dw_uq_pad_t, dw_ukv, dw_mem, dw_out)
    sf_in, sb_in, r_rest = _core_sum(g_in, r_in, core, HALF_IN, HALF_IN // 2, "core_sum_in",
                                     ride=_half_to_sibling(g_rest.reshape(4, 2, HALF_REST, 1024)))
    sf_rest, sb_rest = _core_sum(g_rest, r_rest, core, HALF_REST, HALF_REST, "core_sum_rest")
    return sf_in, sb_in, sf_rest, sb_rest


def _small_block(dg_emb, db_emb, small_rows):
    dg_cq, dg_ckv, loss, dg_a, dg_b, dg_m, dg_post, db_post = small_rows
    row2 = jnp.concatenate([dg_cq, dg_ckv, loss, jnp.zeros((1, 512), F32)], axis=1)
    return jnp.concatenate([dg_emb, db_emb, row2, dg_a, jnp.concatenate([dg_b, dg_m], axis=1), dg_post, db_post,
                            jnp.zeros((1, 1024), F32)], axis=0)


def _pack_small(g_emb, b_emb, g_cq, g_ckv, g_out_a, g_out_b, g_out_m, g_post, b_post):
    row2 = jnp.concatenate([g_cq.reshape(1, -1), g_ckv.reshape(1, -1), jnp.zeros((1, 640), F32)], axis=1)
    return jnp.concatenate([g_emb.reshape(1, -1), b_emb.reshape(1, -1), row2, g_out_a.reshape(1, -1),
                            jnp.concatenate([g_out_b.reshape(1, -1), g_out_m.reshape(1, -1)], axis=1),
                            g_post.reshape(1, -1), b_post.reshape(1, -1), jnp.zeros((1, 1024), F32)], axis=0)


def kernel(x, mem, positions, g_emb, b_emb, w_in, g_cq, g_ckv, w_uq, w_ukv, w_mem_kv, g_out_a, g_out_b, g_out_m, w_out, g_post, b_post, loss_target, m_g_emb, m_b_emb, m_w_in, m_g_cq, m_g_ckv, m_w_uq, m_w_ukv, m_w_mem_kv, m_g_out_a, m_g_out_b, m_g_out_m, m_w_out, m_g_post, m_b_post, v_g_emb, v_b_emb, v_w_in, v_g_cq, v_g_ckv, v_w_uq, v_w_ukv, v_w_mem_kv, v_g_out_a, v_g_out_b, v_g_out_m, v_w_out, v_g_post, v_b_post):
    w_rest = _pack_rest(w_uq, w_ukv, w_mem_kv, w_out)
    w_in_t = w_in[0].T
    w_in_b = jnp.pad(w_in_t.astype(BF16), ((0, ROWS_IN - SHARD_ROWS), (0, 0)))
    gains = (g_emb.reshape(1, -1), b_emb.reshape(1, -1), g_cq, g_ckv, g_out_a, g_out_b, g_out_m, g_post, b_post)
    rope_consts = (_rope_consts(16, 0, 64), _rope_consts(32, 64, 128))
    h, h32, trig, gathered_in = _ln_fwd(x.reshape(-1, D_MODEL), gains[0], gains[1],
                                        positions.reshape(-1, 1).astype(F32), *rope_consts,
                                        ride=_gather_ride(w_in_b.reshape(2, HALF_IN, 1024), spread=False))
    w_in_arr_t = _arranged_w_in(gathered_in.reshape(4, ROWS_IN, 1024))
    proj, gathered_rest = _mm(h, w_in_arr_t, F32, 1024, 2048, 1024, "in_proj", mode="nt",
                              ride=_gather_ride(w_rest.astype(BF16).reshape(2, HALF_REST, 1024), spread=True))
    weights = _rest_weights(gathered_rest.reshape(4, ROWS_REST, 1024))
    operands, x2, small_rows = _forward_backward(h, h32, proj, trig, rope_consts, x, mem, loss_target, weights,
                                                 gains)

    core = lax.axis_index("c").astype(jnp.int32).reshape(1)
    chip = (2 * lax.axis_index("x") + lax.axis_index("y")).astype(jnp.int32).reshape(1)
    sf_in, sb_in, sf_rest, sb_rest = _weight_grads(operands, core)
    grad_x, dg_emb, db_emb, rb_in, rb_rest = _dh_scatter(operands[0], w_in_arr_t, x2, operands[3], gains[0],
                                                         sb_in, sb_rest)
    gh_in = _chip_sum(sf_in, rb_in, chip, HALF_IN, HALF_IN // 2, "chip_sum_in")
    gh_rest = _chip_sum(sf_rest, rb_rest, chip, HALF_REST, HALF_REST, "chip_sum_rest")
    grad_in, grad_rest, small_sum = _join_and_allreduce(gh_in, gh_rest, _small_block(dg_emb, db_emb, small_rows))
    grad_in = grad_in.reshape(ROWS_IN, 1024)
    grad_rest = grad_rest.reshape(ROWS_REST, 1024)

    big_in = _adamw(grad_in, w_in_t, m_w_in[0].T, v_w_in[0].T, SHARD_ROWS // 3, "adamw_in")
    def rest_parts(a_uq, a_ukv, a_mem, a_out):
        return [a_uq[0].T.reshape(ROWS_UQ, 1024), a_ukv.reshape(ROWS_UKV, 1024), a_mem[0], a_out[0]]

    uq, ukv, wmem, wout = _adamw_pieces(
        grad_rest, rest_parts(w_uq, w_ukv, w_mem_kv, w_out), rest_parts(m_w_uq, m_w_ukv, m_w_mem_kv, m_w_out),
        rest_parts(v_w_uq, v_w_ukv, v_w_mem_kv, v_w_out), REST_PIECES, "adamw_rest")
    sm = _adamw_pieces(
        small_sum,
        _pack_small(g_emb, b_emb, g_cq, g_ckv, g_out_a, g_out_b, g_out_m, g_post, b_post),
        _pack_small(m_g_emb, m_b_emb, m_g_cq, m_g_ckv, m_g_out_a, m_g_out_b, m_g_out_m, m_g_post, m_b_post),
        _pack_small(v_g_emb, v_b_emb, v_g_cq, v_g_ckv, v_g_out_a, v_g_out_b, v_g_out_m, v_g_post, v_b_post),
        SMALL_PIECES, "adamw_small")
    loss = small_sum[2, 384]

    def ordered(kind):
        s_gemb, s_bemb, s_gcq, s_gckv, s_ga, s_gb, s_gm, s_gpost, s_bpost = [piece[kind] for piece in sm]
        return [s_gemb.reshape(-1), s_bemb.reshape(-1), big_in[kind].T[None], s_gcq, s_gckv,
                uq[kind].reshape(192, 256).T[None], ukv[kind].reshape(1, 128, 256), wmem[kind][None], s_ga, s_gb,
                s_gm, wout[kind][None], s_gpost, s_bpost]

    return (loss, grad_x.reshape(x.shape), *ordered(0), *ordered(1), *ordered(2), *ordered(3))
```

```python
import functools
import math

import jax
import jax.numpy as jnp
import numpy as np
from jax import lax
from jax.experimental import pallas as pl
from jax.experimental.pallas import tpu as pltpu

F32 = jnp.float32
BF16 = jnp.bfloat16
MESH = pl.DeviceIdType.MESH
ANY = pl.BlockSpec(memory_space=pl.ANY)
IN_VMEM = pl.BlockSpec(memory_space=pltpu.VMEM)

D_MODEL = 1024
A_WIDTH = 1024
MLA_HEADS = 8
MLA_Q_RANK = 256
MLA_KV_RANK = 128
MLA_QK_DIM = 96
MEM_WIDTH = 512
N_MEM = 256
ROPE_THETA = 500000.0
NORM_EPS = 1e-5
NEG_INF = -1e30
DEEPNORM_ALPHA = 2.0 ** 0.25
DILATED = ((64, 1), (256, 4), (1024, 16))

ADAM_LR = 0.001
ADAM_B1 = 0.9
ADAM_B2 = 0.999
ADAM_EPS = 1e-08
ADAM_WD = 0.01
ADAM_STEP = 10

LANES = 128
VMEM_LIMIT = 56 * 1024 * 1024
LOG2E = math.log2(math.e)
LN2 = math.log(2.0)

PROJ_W = 6144
COL_CQ = 4096
COL_BG = 4608
COL_MQ = 5120
COL_MG = 5632

SHARD_ROWS = 1512
ROWS_IN = 1536
ROWS_UQ, ROWS_UKV, ROWS_MEM, ROWS_OUT = 48, 32, 256, 512
ROWS_USED = ROWS_UQ + ROWS_UKV + ROWS_MEM + ROWS_OUT
ROWS_REST = 864
HALF_IN = ROWS_IN // 2
HALF_REST = ROWS_REST // 2
REST_PIECES = ((0, 48, 0, 1024), (48, 80, 0, 1024), (80, 336, 0, 1024), (336, 848, 0, 1024))
SMALL_PIECES = ((0, 1, 0, 1024), (1, 2, 0, 1024), (2, 3, 0, 256), (2, 3, 256, 384), (3, 4, 0, 1024), (4, 5, 0, 512),
                (4, 5, 512, 1024), (5, 6, 0, 1024), (6, 7, 0, 1024))


def _params(sem=None, vmem=VMEM_LIMIT):
    return pltpu.CompilerParams(dimension_semantics=sem, vmem_limit_bytes=vmem)


def _dot(a, b):
    return jnp.dot(a, b, preferred_element_type=F32)


def _dot_nt(a, b):
    return lax.dot_general(a, b, (((1,), (1,)), ((), ())), preferred_element_type=F32)


def _dot_tn(a, b):
    return lax.dot_general(a, b, (((0,), (0,)), ((), ())), preferred_element_type=F32)


def _ln_hat(x):
    mu = jnp.mean(x, axis=-1, keepdims=True)
    xc = x - mu
    var = jnp.mean(xc * xc, axis=-1, keepdims=True)
    rstd = lax.rsqrt(var + NORM_EPS)
    return xc * rstd, rstd


def _ln_bwd_rows(dxh, xh, rstd):
    return rstd * (dxh - jnp.mean(dxh, axis=-1, keepdims=True) - xh * jnp.mean(dxh * xh, axis=-1, keepdims=True))


def _rms_hat(x, width):
    ms = jnp.sum(x * x, axis=-1, keepdims=True) * (1.0 / width)
    r = lax.rsqrt(ms + NORM_EPS)
    return x * r, r


def _rms_bwd(u, xh, r, width):
    return r * (u - xh * (jnp.sum(u * xh, axis=-1, keepdims=True) * (1.0 / width)))


def _colsum(v):
    return jnp.sum(v, axis=0, keepdims=True)


def _rope_tables(cos, sin, consts):
    return cos, sin * consts[2:3, :], -sin * consts[1:2, :]


def _rope(x, tables, half, inverse=False):
    c, s_up, s_dn = tables
    if inverse:
        s_up, s_dn = -s_up, -s_dn
    return x * c + pltpu.roll(x, half, 1) * s_up + pltpu.roll(x, LANES - half, 1) * s_dn


def _ln_fwd(x, g, b, pos, rope_a, rope_b, tm=512, ride=None):
    t, d = x.shape
    n_in = len(ride.args) if ride else 0
    n_out = len(ride.out_shapes) if ride else 0
    steps = t // tm

    def body(x_ref, g_ref, b_ref, pos_ref, ra_ref, rb_ref, *rest):
        h_ref, h32_ref, trig_ref = rest[n_in:n_in + 3]
        if ride:
            i = pl.program_id(0)
            ride.run(i, steps, rest[:n_in], rest[n_in + 3:n_in + 3 + n_out], rest[n_in + 3 + n_out:])
        xh, _ = _ln_hat(x_ref[...])
        h = xh * g_ref[...] + b_ref[...]
        h32_ref[...] = h
        h_ref[...] = h.astype(BF16)
        for j, consts in enumerate((ra_ref, rb_ref)):
            ang = pos_ref[...] * consts[0:1, :]
            trig_ref[:, 2 * j * LANES:(2 * j + 1) * LANES] = jnp.cos(ang)
            trig_ref[:, (2 * j + 1) * LANES:(2 * j + 2) * LANES] = jnp.sin(ang)

    row = pl.BlockSpec((1, d), lambda i: (0, 0))
    tile = pl.BlockSpec((tm, d), lambda i: (i, 0))
    consts = pl.BlockSpec((8, LANES), lambda i: (0, 0))
    trig_tile = pl.BlockSpec((tm, 4 * LANES), lambda i: (i, 0))
    in_specs = [tile, row, row, pl.BlockSpec((tm, 1), lambda i: (i, 0)), consts, consts]
    shapes = (jax.ShapeDtypeStruct((t, d), BF16), jax.ShapeDtypeStruct((t, d), F32),
              jax.ShapeDtypeStruct((t, 4 * LANES), F32))
    if not ride:
        return pl.pallas_call(
            body, name="ln_fwd", grid=(steps,), out_shape=shapes, in_specs=in_specs,
            out_specs=(tile, tile, trig_tile), compiler_params=_params(("parallel",)),
        )(x, g, b, pos, rope_a, rope_b)
    return pl.pallas_call(
        body, name="ln_fwd", grid=(steps,),
        out_shape=(*shapes, *ride.out_shapes),
        in_specs=in_specs + ride.in_specs, out_specs=(tile, tile, trig_tile) + (ANY,) * n_out,
        scratch_shapes=ride.scratch(),
        compiler_params=_params(("arbitrary",)),
    )(x, g, b, pos, rope_a, rope_b, *ride.args)


class _Ride:
    def __init__(self, args, out_shapes, sem_counts, plan, in_specs=None, spread=True):
        self.args, self.out_shapes, self.plan = list(args), list(out_shapes), plan
        self.sem_counts = sem_counts
        self.in_specs = in_specs or [ANY] * len(self.args)
        self.spread = spread

    def scratch(self):
        return [pltpu.SemaphoreType.DMA((n,)) for n in self.sem_counts]

    def run(self, step, total, in_refs, out_refs, sems):
        count = len(self.plan(in_refs, out_refs, *sems))
        at = [(k * (total - 1)) // (count - 1) if self.spread or k == 0 else total - 1 for k in range(count)]
        for when in sorted(set(at)):
            @pl.when(step == when)
            def _(when=when):
                stages = self.plan(in_refs, out_refs, *sems)
                for k in range(count):
                    if at[k] == when:
                        stages[k]()


def _mm(a, b, out_dtype, tm, tn, tk, name, mode="nn", ride=None):
    if mode == "tn":
        k, m = a.shape
    else:
        m, k = a.shape
    n = b.shape[0] if mode == "nt" else b.shape[1]
    nk = k // tk
    nj, ni = n // tn, m // tm
    n_in = len(ride.args) if ride else 0
    n_out = len(ride.out_shapes) if ride else 0

    def body(a_ref, b_ref, *rest):
        o_ref = rest[n_in]
        acc_ref = rest[n_in + 1 + n_out]
        if ride:
            j, i, kk = pl.program_id(0), pl.program_id(1), pl.program_id(2)
            ride.run((j * ni + i) * nk + kk, nj * ni * nk, rest[:n_in], rest[n_in + 1:n_in + 1 + n_out],
                     rest[n_in + 2 + n_out:])
        av = a_ref[...].astype(BF16)
        bv = b_ref[...].astype(BF16)
        part = _dot_tn(av, bv) if mode == "tn" else _dot_nt(av, bv) if mode == "nt" else _dot(av, bv)
        if nk == 1:
            o_ref[...] = part.astype(out_dtype)
        else:
            kk = pl.program_id(2)

            @pl.when(kk == 0)
            def _():
                acc_ref[...] = part

            @pl.when(kk > 0)
            def _():
                acc_ref[...] += part

            @pl.when(kk == nk - 1)
            def _():
                o_ref[...] = acc_ref[...].astype(out_dtype)

    a_spec = (pl.BlockSpec((tk, tm), lambda j, i, kk: (kk, i)) if mode == "tn"
              else pl.BlockSpec((tm, tk), lambda j, i, kk: (i, kk)))
    b_spec = (pl.BlockSpec((tn, tk), lambda j, i, kk: (j, kk)) if mode == "nt"
              else pl.BlockSpec((tk, tn), lambda j, i, kk: (kk, j)))
    o_spec = pl.BlockSpec((tm, tn), lambda j, i, kk: (i, j))
    o_shape = jax.ShapeDtypeStruct((m, n), out_dtype)
    if not ride:
        return pl.pallas_call(
            body, name=name, grid=(nj, ni, nk), out_shape=o_shape, in_specs=[a_spec, b_spec], out_specs=o_spec,
            scratch_shapes=[pltpu.VMEM((tm, tn), F32)],
            compiler_params=_params(("parallel", "parallel", "arbitrary")),
        )(a, b)
    return pl.pallas_call(
        body, name=name, grid=(nj, ni, nk),
        out_shape=(o_shape, *ride.out_shapes),
        in_specs=[a_spec, b_spec] + ride.in_specs,
        out_specs=(o_spec,) + (ANY,) * n_out,
        scratch_shapes=[pltpu.VMEM((tm, tn), F32)] + ride.scratch(),
        compiler_params=_params(("arbitrary", "arbitrary", "arbitrary")),
    )(a, b, *ride.args)


def _prep(proj, trig, w_uq, w_ukv, g_cq, g_ckv, rope_a, rope_b, scales, tm=512):
    t = proj.shape[0]
    sc_a, sc_b, sc_m = (s * LOG2E for s in scales)

    def body(aq_ref, ak_ref, av_ref, bs_ref, mq_ref, trig_ref, wuq_ref, wukv_ref, gcq_ref, gckv_ref,
             ra_ref, rb_ref, qa_ref, ka_ref, va_ref, qb_ref, kb_ref, vb_ref, qm_ref):
        ta = _rope_tables(trig_ref[:, 0:LANES], trig_ref[:, LANES:2 * LANES], ra_ref[...])
        tb = _rope_tables(trig_ref[:, 2 * LANES:3 * LANES], trig_ref[:, 3 * LANES:4 * LANES], rb_ref[...])
        for j in range(A_WIDTH // LANES):
            sl = slice(j * LANES, (j + 1) * LANES)
            qa_ref[:, sl] = (_rope(aq_ref[:, sl], ta, 8) * sc_a).astype(BF16)
            ka_ref[:, sl] = _rope(ak_ref[:, sl], ta, 8).astype(BF16)
        va_ref[...] = av_ref[...].astype(BF16)
        qm_ref[...] = (mq_ref[...] * sc_m).astype(BF16)

        cq_hat, _ = _rms_hat(bs_ref[:, 0:MLA_Q_RANK], MLA_Q_RANK)
        cqn = (cq_hat * gcq_ref[...]).astype(BF16)
        ckv_hat, _ = _rms_hat(bs_ref[:, MLA_Q_RANK:MLA_Q_RANK + MLA_KV_RANK], MLA_KV_RANK)
        ckvn = (ckv_hat * gckv_ref[...]).astype(BF16)
        qfull = _dot_nt(cqn, wuq_ref[...])
        kv = _dot(ckvn, wukv_ref[...])
        kr = _rope(bs_ref[:, 384:512], tb, 16)
        lane = lax.broadcasted_iota(jnp.int32, (1, LANES), 1)
        low = lane < 64
        for h in range(MLA_HEADS):
            sl = slice(h * LANES, (h + 1) * LANES)
            qb_ref[:, sl] = (_rope(qfull[:, sl], tb, 16) * sc_b).astype(BF16)
            kb_ref[:, sl] = jnp.where(low, kv[:, sl], kr).astype(BF16)
            vb_ref[:, sl] = jnp.where(low, 0.0, kv[:, sl]).astype(BF16)

    def col(width, idx):
        return pl.BlockSpec((tm, width), lambda i: (i, idx))

    def full(shape):
        return pl.BlockSpec(shape, lambda i: (0, 0))

    wide = jax.ShapeDtypeStruct((t, 1024), BF16)
    return pl.pallas_call(
        body, name="prep", grid=(t // tm,),
        out_shape=(wide, wide, wide, wide, wide, wide,
                   jax.ShapeDtypeStruct((t, MEM_WIDTH), BF16)),
        in_specs=[col(1024, 0), col(1024, 1), col(1024, 2), col(512, COL_CQ // 512), col(512, COL_MQ // 512),
                  pl.BlockSpec((tm, 4 * LANES), lambda i: (i, 0)),
                  full((1024, MLA_Q_RANK)), full((MLA_KV_RANK, 1024)),
                  full((1, MLA_Q_RANK)), full((1, MLA_KV_RANK)), full((8, LANES)), full((8, LANES))],
        out_specs=(col(1024, 0),) * 6 + (col(MEM_WIDTH, 0),),
        compiler_params=_params(("parallel",)),
    )(proj, proj, proj, proj, proj, trig, w_uq, w_ukv, g_cq, g_ckv, rope_a, rope_b)


def _attn_fwd(q, k, v, *, nb, s, sk, heads, hpb, voff, bq, name):
    nq = s // bq
    width = hpb * LANES
    vblk = voff // hpb

    def body(q_ref, k_ref, v_ref, o_ref, lse_ref):
        for h in range(hpb):
            sl = slice(h * LANES, (h + 1) * LANES)
            sc = _dot_nt(q_ref[:, sl], k_ref[:, sl])
            m = jnp.max(sc, axis=1, keepdims=True)
            p = jnp.exp2(sc - m)
            l = jnp.sum(p, axis=1, keepdims=True)
            o_ref[:, sl] = _dot(p.astype(BF16), v_ref[:, sl]) / l
            lse_ref[:, sl] = jnp.broadcast_to(m + jnp.log(l) * LOG2E, (bq, LANES))

    out = jax.ShapeDtypeStruct((nb * s, heads * LANES), F32)
    ospec = pl.BlockSpec((bq, width), lambda b, i, g: (b * nq + i, g))
    return pl.pallas_call(
        body, name=name, grid=(nb, nq, heads // hpb),
        out_shape=(out, out),
        in_specs=[ospec, pl.BlockSpec((sk, width), lambda b, i, g: (b, g)),
                  pl.BlockSpec((sk, width), lambda b, i, g: (b, vblk + g))],
        out_specs=(ospec, ospec),
        compiler_params=_params(("parallel", "parallel", "parallel")),
    )(q, k, v)


def _attn_bwd(q, k, v, o, do, lse, *, nb, s, sk, heads, hpb, voff, scale, bq, name):
    nq = s // bq
    width = hpb * LANES
    vblk = voff // hpb

    def body(q_ref, k_ref, v_ref, o_ref, do_ref, lse_ref, dq_ref, dk_ref, dv_ref, dk_acc, dv_acc):
        i = pl.program_id(2)

        @pl.when(i == 0)
        def _():
            dk_acc[...] = jnp.zeros_like(dk_acc)
            dv_acc[...] = jnp.zeros_like(dv_acc)

        for h in range(hpb):
            sl = slice(h * LANES, (h + 1) * LANES)
            qh = q_ref[:, sl]
            kk = k_ref[:, sl]
            doh = do_ref[:, sl]
            delta = jnp.sum(doh.astype(F32) * o_ref[:, sl], axis=1, keepdims=True)
            p = jnp.exp2(_dot_nt(qh, kk) - lse_ref[:, h * LANES:h * LANES + 1])
            ds = (p * (_dot_nt(doh, v_ref[:, sl]) - delta)).astype(BF16)
            dq_ref[:, sl] = (_dot(ds, kk) * scale).astype(BF16)
            dk_acc[:, sl] += _dot_tn(ds, qh)
            dv_acc[:, sl] += _dot_tn(p.astype(BF16), doh)

        @pl.when(i == nq - 1)
        def _():
            dk_ref[...] = (dk_acc[...] * LN2).astype(BF16)
            dv_ref[...] = dv_acc[...].astype(BF16)

    qspec = pl.BlockSpec((bq, width), lambda b, g, i: (b * nq + i, g))
    kv_spec = pl.BlockSpec((sk, width), lambda b, g, i: (b, g))
    dq_shape = jax.ShapeDtypeStruct((nb * s, heads * LANES), BF16)
    dkv_shape = jax.ShapeDtypeStruct((nb * sk, heads * LANES), BF16)
    return pl.pallas_call(
        body, name=name, grid=(nb, heads // hpb, nq),
        out_shape=(dq_shape, dkv_shape, dkv_shape),
        in_specs=[qspec, kv_spec, pl.BlockSpec((sk, width), lambda b, g, i: (b, vblk + g)), qspec, qspec, qspec],
        out_specs=(qspec, kv_spec, kv_spec),
        scratch_shapes=[pltpu.VMEM((sk, width), F32), pltpu.VMEM((sk, width), F32)],
        compiler_params=_params(("parallel", "parallel", "arbitrary")),
    )(q, k, v, o, do, lse)


BAND_Q = 128
BAND_WIN = 256


def _band_start(i, s):
    return min(max(i * BAND_Q - 64, 0), s - BAND_WIN)


def _to_pattern_order(src_ref, dst_ref, stage_ref, s, d):
    length = s // d
    stage_ref[...] = src_ref[...].astype(F32)
    for r in range(d):
        dst_ref[r * length:(r + 1) * length, :] = stage_ref[pl.ds(r, length, stride=d), :].astype(dst_ref.dtype)


def _dilated_fwd(q, k, v, bias, bias_index, *, nb, s, name):
    nblk = s // BAND_Q
    npat = len(DILATED)

    def body(q_ref, k_ref, v_ref, bias_ref, o_ref, lse_ref, *rest):
        ordered = rest[:3 * (npat - 1)]
        stage_ref, op_ref, lp_ref, on_ref, ln_ref = rest[3 * (npat - 1):]
        lane = lax.broadcasted_iota(jnp.int32, (1, LANES), 1)
        first = lane < 64
        for p, (_, d) in enumerate(DILATED):
            if d == 1:
                qs, ks, vs = q_ref, k_ref, v_ref
            else:
                qs, ks, vs = ordered[3 * (p - 1):3 * p]
                for src, dst in ((q_ref, qs), (k_ref, ks), (v_ref, vs)):
                    _to_pattern_order(src, dst, stage_ref, s, d)
            for i in range(nblk):
                u0 = i * BAND_Q
                st = _band_start(i, s)
                qi = qs[u0:u0 + BAND_Q, :]
                kw = ks[st:st + BAND_WIN, :]
                vw = vs[st:st + BAND_WIN, :]
                zero = jnp.zeros_like(qi)
                q2 = jnp.concatenate([jnp.where(first, qi, zero), jnp.where(first, zero, qi)], axis=0)
                sc = _dot_nt(q2, kw)
                b = bias_ref[bias_index[p][i]]
                halves = []
                for h in range(2):
                    sh = sc[h * BAND_Q:(h + 1) * BAND_Q] + b
                    m = jnp.max(sh, axis=1, keepdims=True)
                    pr = jnp.exp2(sh - m)
                    l = jnp.sum(pr, axis=1, keepdims=True)
                    halves.append((pr.astype(BF16), l, m + jnp.log(l) * LOG2E))
                o2 = _dot(jnp.concatenate([halves[0][0], halves[1][0]], axis=0), vw)
                o_blk = jnp.where(first, o2[:BAND_Q] / halves[0][1], o2[BAND_Q:] / halves[1][1])
                lse_blk = jnp.where(first, jnp.broadcast_to(halves[0][2], (BAND_Q, LANES)),
                                    jnp.broadcast_to(halves[1][2], (BAND_Q, LANES)))
                op_ref[p, u0:u0 + BAND_Q, :] = o_blk
                lp_ref[p, u0:u0 + BAND_Q, :] = lse_blk
            if d > 1:
                length = s // d
                for r in range(d):
                    on_ref.at[p - 1][pl.ds(r, length, stride=d), :] = op_ref[p, r * length:(r + 1) * length, :]
                    ln_ref.at[p - 1][pl.ds(r, length, stride=d), :] = lp_ref[p, r * length:(r + 1) * length, :]
        lses = [lp_ref[0]] + [ln_ref[p] for p in range(npat - 1)]
        outs = [op_ref[0]] + [on_ref[p] for p in range(npat - 1)]
        m = functools.reduce(jnp.maximum, lses)
        ws = [jnp.exp2(l - m) for l in lses]
        den = functools.reduce(lambda a, c: a + c, ws)
        o_ref[...] = functools.reduce(lambda a, c: a + c, [w * o for w, o in zip(ws, outs)]) / den
        lse_ref[...] = m + jnp.log(den) * LOG2E

    blk = pl.BlockSpec((s, LANES), lambda b, g: (b, g))
    out = jax.ShapeDtypeStruct((nb * s, A_WIDTH), F32)
    copy = jax.ShapeDtypeStruct((nb * s, A_WIDTH), BF16)
    n_copies = 3 * (npat - 1)
    res = pl.pallas_call(
        body, name=name, grid=(nb, A_WIDTH // LANES),
        out_shape=(out, out) + (copy,) * n_copies,
        in_specs=[blk, blk, blk, pl.BlockSpec(bias.shape, lambda b, g: (0, 0, 0))],
        out_specs=(blk, blk) + (blk,) * n_copies,
        scratch_shapes=[pltpu.VMEM((s, LANES), F32), pltpu.VMEM((npat, s, LANES), F32),
                        pltpu.VMEM((npat, s, LANES), F32), pltpu.VMEM((npat - 1, s, LANES), F32),
                        pltpu.VMEM((npat - 1, s, LANES), F32)],
        compiler_params=_params(("parallel", "parallel")),
    )(q, k, v, bias)
    return res[0], res[1], res[2:]


def _dilated_bwd(q, k, v, ordered, o, do, lse, bias, bias_index, *, nb, s, scale, name):
    nblk = s // BAND_Q
    npat = len(DILATED)
    n_copies = 3 * (npat - 1)

    def body(q_ref, k_ref, v_ref, *rest):
        ordered_refs = rest[:n_copies]
        (o_ref, do_ref, lse_ref, bias_ref, dq_out, dk_out, dv_out, stage_ref, rs_ref, dop_ref, rsp_ref,
         dqp_ref, dkp_ref, dvp_ref, dq_ref, dk_ref, dv_ref, nat_ref) = rest[n_copies:]
        lane = lax.broadcasted_iota(jnp.int32, (1, LANES), 1)
        first = lane < 64
        prod = do_ref[...].astype(F32) * o_ref[...]
        d0 = jnp.sum(jnp.where(first, prod, 0.0), axis=1, keepdims=True)
        d1 = jnp.sum(jnp.where(first, 0.0, prod), axis=1, keepdims=True)
        delta = jnp.where(first, jnp.broadcast_to(d0, (s, LANES)), jnp.broadcast_to(d1, (s, LANES)))
        rs_ref[...] = jnp.where((lane & 32) == 0, lse_ref[...], delta)
        for p, (_, d) in enumerate(DILATED):
            length = s // d
            if d == 1:
                qs, ks, vs, dos, rss = q_ref, k_ref, v_ref, do_ref, rs_ref
                dqs, dks, dvs = dq_ref, dk_ref, dv_ref
            else:
                for src, dst in ((do_ref, dop_ref), (rs_ref, rsp_ref)):
                    _to_pattern_order(src, dst, stage_ref, s, d)
                qs, ks, vs = ordered_refs[3 * (p - 1):3 * p]
                dos, rss = dop_ref, rsp_ref
                dqs, dks, dvs = dqp_ref, dkp_ref, dvp_ref
            dks[...] = jnp.zeros((s, LANES), F32)
            dvs[...] = jnp.zeros((s, LANES), F32)
            for i in range(nblk):
                u0 = i * BAND_Q
                st = _band_start(i, s)
                qi = qs[u0:u0 + BAND_Q, :]
                doi = dos[u0:u0 + BAND_Q, :]
                kw = ks[st:st + BAND_WIN, :]
                vw = vs[st:st + BAND_WIN, :]
                zero = jnp.zeros_like(qi)
                q2 = jnp.concatenate([jnp.where(first, qi, zero), jnp.where(first, zero, qi)], axis=0)
                do2 = jnp.concatenate([jnp.where(first, doi, zero), jnp.where(first, zero, doi)], axis=0)
                sc = _dot_nt(q2, kw)
                dp = _dot_nt(do2, vw)
                b = bias_ref[bias_index[p][i]]
                rs_i = rss[u0:u0 + BAND_Q, :]
                ps, dss = [], []
                for h in range(2):
                    rows = slice(h * BAND_Q, (h + 1) * BAND_Q)
                    pr = jnp.exp2(sc[rows] + b - rs_i[:, 64 * h:64 * h + 1])
                    ps.append(pr.astype(BF16))
                    dss.append((pr * (dp[rows] - rs_i[:, 64 * h + 32:64 * h + 33])).astype(BF16))
                p2 = jnp.concatenate(ps, axis=0)
                ds2 = jnp.concatenate(dss, axis=0)
                dq2 = _dot(ds2, kw)
                dqs[u0:u0 + BAND_Q, :] = jnp.where(first, dq2[:BAND_Q], dq2[BAND_Q:]) * scale
                dks[st:st + BAND_WIN, :] += _dot_tn(ds2, q2)
                dvs[st:st + BAND_WIN, :] += _dot_tn(p2, do2)
            if d > 1:
                for j, src in enumerate((dqp_ref, dkp_ref, dvp_ref)):
                    for r in range(d):
                        nat_ref.at[p - 1, j][pl.ds(r, length, stride=d), :] = src[r * length:(r + 1) * length, :]

        def total(j, first_ref):
            return functools.reduce(lambda a, c: a + c, [first_ref[...]] + [nat_ref[p, j] for p in range(npat - 1)])

        dq_out[...] = total(0, dq_ref).astype(BF16)
        dk_out[...] = (total(1, dk_ref) * LN2).astype(BF16)
        dv_out[...] = total(2, dv_ref).astype(BF16)

    blk = pl.BlockSpec((s, LANES), lambda b, g: (b, g))
    out = jax.ShapeDtypeStruct((nb * s, A_WIDTH), BF16)
    f32_buf = pltpu.VMEM((s, LANES), F32)
    bf_buf = pltpu.VMEM((s, LANES), BF16)
    return pl.pallas_call(
        body, name=name, grid=(nb, A_WIDTH // LANES),
        out_shape=(out, out, out),
        in_specs=[blk] * (6 + n_copies) + [pl.BlockSpec(bias.shape, lambda b, g: (0, 0, 0))],
        out_specs=(blk, blk, blk),
        scratch_shapes=[f32_buf, f32_buf, bf_buf] + [f32_buf] * 7 + [pltpu.VMEM((npat - 1, 3, s, LANES), F32)],
        compiler_params=_params(("parallel", "parallel")),
    )(q, k, v, *ordered, o, do, lse, bias)


def _post(h32, ya, ybp, ym, proj, target, w_out, g_a, g_b, g_m, g_post, b_post, tm=256):
    t = h32.shape[0]

    def body(h_ref, ya_ref, yb_ref, ym_ref, ga_ref, gb_ref, gm_ref, tg_ref, wo_ref,
             goa_ref, gob_ref, gom_ref, gp_ref, bp_ref,
             y_ref, dz_ref, doa_ref, dob_ref, dom_ref, dga_ref, dgb_ref, dgm_ref,
             loss_ref, dgp_ref, dbp_ref, dgoa_ref, dgob_ref, dgom_ref):
        i = pl.program_id(0)

        @pl.when(i == 0)
        def _():
            for r in (loss_ref, dgp_ref, dbp_ref, dgoa_ref, dgob_ref, dgom_ref):
                r[...] = jnp.zeros_like(r)

        lane = lax.broadcasted_iota(jnp.int32, (1, LANES), 1)
        low = lane < 64
        h = h_ref[...]

        ybp_v = yb_ref[...]
        yb = jnp.concatenate(
            [jnp.where(low, pltpu.roll(ybp_v[:, 2 * j * LANES:(2 * j + 1) * LANES], 64, 1),
                       ybp_v[:, (2 * j + 1) * LANES:(2 * j + 2) * LANES]) for j in range(4)], axis=1)

        def gated(raw, gate, gain, width):
            xh, r = _rms_hat(raw, width)
            n = xh * gain
            sg = 1.0 / (1.0 + jnp.exp(-gate))
            return xh, r, n, sg, n * (gate * sg)

        gate_a, gate_b, gate_m = ga_ref[...], gb_ref[...], gm_ref[...]
        xh_a, r_a, n_a, sg_a, y_a = gated(ya_ref[...], gate_a, goa_ref[...], A_WIDTH)
        xh_b, r_b, n_b, sg_b, y_b = gated(yb, gate_b, gob_ref[...], 512)
        xh_m, r_m, n_m, sg_m, y_m = gated(ym_ref[...], gate_m, gom_ref[...], 512)
        y = jnp.concatenate([y_a, y_b, y_m], axis=1).astype(BF16)
        y_ref[...] = y
        z = DEEPNORM_ALPHA * h + _dot(y, wo_ref[...])
        zh, rstd = _ln_hat(z)
        err = zh * gp_ref[...] + bp_ref[...] - tg_ref[...]
        rows = jnp.sum(err * err, axis=1, keepdims=True)
        loss_ref[...] += jnp.broadcast_to(jnp.sum(rows, axis=0, keepdims=True) * (0.5 / D_MODEL), (1, LANES))
        dout = err * (1.0 / D_MODEL)
        dgp_ref[...] += _colsum(dout * zh)
        dbp_ref[...] += _colsum(dout)
        dz = _ln_bwd_rows(dout * gp_ref[...], zh, rstd)
        dz_ref[...] = dz
        dy = _dot_nt(dz.astype(BF16), wo_ref[...])

        def gated_bwd(dyg, xh, r, n, sg, gate, gain, width, dgain_ref):
            dn = dyg * (gate * sg)
            dgate = dyg * n * (sg * (1.0 + gate * (1.0 - sg)))
            dgain_ref[...] += _colsum(dn * xh)
            return _rms_bwd(dn * gain, xh, r, width), dgate

        dya, dgate_a = gated_bwd(dy[:, 0:1024], xh_a, r_a, n_a, sg_a, gate_a, goa_ref[...], A_WIDTH, dgoa_ref)
        dyb, dgate_b = gated_bwd(dy[:, 1024:1536], xh_b, r_b, n_b, sg_b, gate_b, gob_ref[...], 512, dgob_ref)
        dym, dgate_m = gated_bwd(dy[:, 1536:2048], xh_m, r_m, n_m, sg_m, gate_m, gom_ref[...], 512, dgom_ref)
        doa_ref[...] = dya.astype(BF16)
        dom_ref[...] = dym.astype(BF16)
        dga_ref[...] = dgate_a.astype(BF16)
        dgb_ref[...] = dgate_b.astype(BF16)
        dgm_ref[...] = dgate_m.astype(BF16)
        for j in range(4):
            blk = dyb[:, j * LANES:(j + 1) * LANES]
            dob_ref[:, 2 * j * LANES:(2 * j + 1) * LANES] = jnp.where(low, 0.0, pltpu.roll(blk, 64, 1)).astype(BF16)
            dob_ref[:, (2 * j + 1) * LANES:(2 * j + 2) * LANES] = jnp.where(low, 0.0, blk).astype(BF16)

    def col(width, idx):
        return pl.BlockSpec((tm, width), lambda i: (i, idx))

    def full(shape):
        return pl.BlockSpec(shape, lambda i: (0, 0))

    def acc(width):
        return jax.ShapeDtypeStruct((1, width), F32)

    return pl.pallas_call(
        body, name="post", grid=(t // tm,),
        out_shape=(jax.ShapeDtypeStruct((t, 2048), BF16), jax.ShapeDtypeStruct((t, 1024), F32),
                   jax.ShapeDtypeStruct((t, 1024), BF16), jax.ShapeDtypeStruct((t, 1024), BF16),
                   jax.ShapeDtypeStruct((t, 512), BF16),
                   jax.ShapeDtypeStruct((t, 1024), BF16), jax.ShapeDtypeStruct((t, 512), BF16),
                   jax.ShapeDtypeStruct((t, 512), BF16),
                   acc(LANES), acc(1024), acc(1024), acc(1024), acc(512), acc(512)),
        in_specs=[col(1024, 0), col(1024, 0), col(1024, 0), col(512, 0),
                  col(1024, 3), col(512, COL_BG // 512), col(512, COL_MG // 512), col(1024, 0),
                  full((2048, 1024)),
                  full((1, 1024)), full((1, 512)), full((1, 512)), full((1, 1024)), full((1, 1024))],
        out_specs=(col(2048, 0), col(1024, 0), col(1024, 0), col(1024, 0), col(512, 0),
                   col(1024, 0), col(512, 0), col(512, 0),
                   full((1, LANES)), full((1, 1024)), full((1, 1024)), full((1, 1024)), full((1, 512)),
                   full((1, 512))),
        compiler_params=_params(("arbitrary",)),
    )(h32, ya, ybp, ym, proj, proj, proj, target, w_out, g_a, g_b, g_m, g_post, b_post)


def _prep_bwd(dqa, dka, dva, dqb, dkb, dvb, dqm, dga, dgb, dgm, proj, trig, w_uq, w_ukv, g_cq, g_ckv,
              rope_a, rope_b, tm=512):
    t = proj.shape[0]

    def body(dqa_ref, dka_ref, dva_ref, dqb_ref, dkb_ref, dvb_ref, dqm_ref, dga_ref, dgb_ref, dgm_ref,
             bs_ref, trig_ref, wuq_ref, wukv_ref, gcq_ref, gckv_ref, ra_ref, rb_ref,
             dproj_ref, dwuq_ref, dwukv_ref, dgcq_ref, dgckv_ref, dqf_ref, dkv_ref):
        i = pl.program_id(0)

        @pl.when(i == 0)
        def _():
            dwuq_ref[...] = jnp.zeros_like(dwuq_ref)
            dwukv_ref[...] = jnp.zeros_like(dwukv_ref)
            dgcq_ref[...] = jnp.zeros_like(dgcq_ref)
            dgckv_ref[...] = jnp.zeros_like(dgckv_ref)

        ta = _rope_tables(trig_ref[:, 0:LANES], trig_ref[:, LANES:2 * LANES], ra_ref[...])
        tb = _rope_tables(trig_ref[:, 2 * LANES:3 * LANES], trig_ref[:, 3 * LANES:4 * LANES], rb_ref[...])
        for j in range(A_WIDTH // LANES):
            sl = slice(j * LANES, (j + 1) * LANES)
            dproj_ref[:, j * LANES:(j + 1) * LANES] = (
                _rope(dqa_ref[:, sl].astype(F32), ta, 8, inverse=True).astype(BF16))
            dproj_ref[:, 1024 + j * LANES:1024 + (j + 1) * LANES] = (
                _rope(dka_ref[:, sl].astype(F32), ta, 8, inverse=True).astype(BF16))
        dproj_ref[:, 2048:3072] = dva_ref[...]
        dproj_ref[:, 3072:4096] = dga_ref[...]

        lane = lax.broadcasted_iota(jnp.int32, (1, LANES), 1)
        low = lane < 64
        rope_lanes = (lane >= 64) & (lane < 96)
        dkr = jnp.zeros((tm, LANES), F32)
        for h in range(MLA_HEADS):
            sl = slice(h * LANES, (h + 1) * LANES)
            dqf_ref[:, sl] = _rope(dqb_ref[:, sl].astype(F32), tb, 16, inverse=True).astype(BF16)
            dk_h = dkb_ref[:, sl]
            dkv_ref[:, sl] = jnp.where(low, dk_h, dvb_ref[:, sl])
            dkr = dkr + jnp.where(rope_lanes, dk_h.astype(F32), 0.0)
        dkr = _rope(dkr, tb, 16, inverse=True)

        cq_hat, r_q = _rms_hat(bs_ref[:, 0:MLA_Q_RANK], MLA_Q_RANK)
        dwuq_ref[...] += _dot_tn(dqf_ref[...], (cq_hat * gcq_ref[...]).astype(BF16))
        dcqn = _dot(dqf_ref[...], wuq_ref[...])
        dgcq_ref[...] += _colsum(dcqn * cq_hat)
        dproj_ref[:, COL_CQ:COL_CQ + 256] = _rms_bwd(dcqn * gcq_ref[...], cq_hat, r_q, MLA_Q_RANK).astype(BF16)
        ckv_hat, r_kv = _rms_hat(bs_ref[:, MLA_Q_RANK:MLA_Q_RANK + MLA_KV_RANK], MLA_KV_RANK)
        dwukv_ref[...] += _dot_tn((ckv_hat * gckv_ref[...]).astype(BF16), dkv_ref[...])
        dckvn = _dot_nt(dkv_ref[...], wukv_ref[...])
        dgckv_ref[...] += _colsum(dckvn * ckv_hat)
        dproj_ref[:, COL_CQ + 256:COL_CQ + 384] = (
            _rms_bwd(dckvn * gckv_ref[...], ckv_hat, r_kv, MLA_KV_RANK).astype(BF16))
        dproj_ref[:, COL_CQ + 384:COL_CQ + 512] = dkr.astype(BF16)
        dproj_ref[:, COL_BG:COL_BG + 512] = dgb_ref[...]
        dproj_ref[:, COL_MQ:COL_MQ + 512] = dqm_ref[...]
        dproj_ref[:, COL_MG:COL_MG + 512] = dgm_ref[...]

    def col(width, idx):
        return pl.BlockSpec((tm, width), lambda i: (i, idx))

    def full(shape):
        return pl.BlockSpec(shape, lambda i: (0, 0))

    return pl.pallas_call(
        body, name="prep_bwd", grid=(t // tm,),
        out_shape=(jax.ShapeDtypeStruct((t, PROJ_W), BF16), jax.ShapeDtypeStruct((1024, MLA_Q_RANK), F32),
                   jax.ShapeDtypeStruct((MLA_KV_RANK, 1024), F32),
                   jax.ShapeDtypeStruct((1, MLA_Q_RANK), F32), jax.ShapeDtypeStruct((1, MLA_KV_RANK), F32)),
        in_specs=[col(1024, 0)] * 6 + [col(512, 0), col(1024, 0), col(512, 0), col(512, 0),
                  col(512, COL_CQ // 512), pl.BlockSpec((tm, 4 * LANES), lambda i: (i, 0)),
                  full((1024, MLA_Q_RANK)), full((MLA_KV_RANK, 1024)),
                  full((1, MLA_Q_RANK)), full((1, MLA_KV_RANK)), full((8, LANES)), full((8, LANES))],
        out_specs=(col(PROJ_W, 0), full((1024, MLA_Q_RANK)), full((MLA_KV_RANK, 1024)),
                   full((1, MLA_Q_RANK)), full((1, MLA_KV_RANK))),
        scratch_shapes=[pltpu.VMEM((tm, 1024), BF16), pltpu.VMEM((tm, 1024), BF16)],
        compiler_params=_params(("arbitrary",)),
    )(dqa, dka, dva, dqb, dkb, dvb, dqm, dga, dgb, dgm, proj, trig, w_uq, w_ukv, g_cq, g_ckv, rope_a, rope_b)


def _adamw_math(gv, w, m, v):
    m_new = ADAM_B1 * m + (1.0 - ADAM_B1) * gv
    v_new = ADAM_B2 * v + (1.0 - ADAM_B2) * (gv * gv)
    m_hat = m_new / (1.0 - ADAM_B1 ** ADAM_STEP)
    v_hat = v_new / (1.0 - ADAM_B2 ** ADAM_STEP)
    return -ADAM_LR * (m_hat / (jnp.sqrt(v_hat) + ADAM_EPS) + ADAM_WD * w), m_new, v_new


def _adamw(g, w, m, v, tr, name):
    r, cols = w.shape

    def body(g_ref, w_ref, m_ref, v_ref, go_ref, d_ref, nm_ref, nv_ref):
        gv = g_ref[...]
        go_ref[...] = gv
        d_ref[...], nm_ref[...], nv_ref[...] = _adamw_math(gv, w_ref[...], m_ref[...], v_ref[...])

    tile = pl.BlockSpec((tr, cols), lambda i: (i, 0))
    shape = jax.ShapeDtypeStruct((r, cols), F32)
    return pl.pallas_call(
        body, name=name, grid=(r // tr,),
        out_shape=(shape,) * 4, in_specs=[tile] * 4, out_specs=(tile,) * 4,
        compiler_params=_params(("parallel",)),
    )(g, w, m, v)


def _adamw_pieces(g, w, m, v, pieces, name):
    n = len(pieces)
    per_piece = isinstance(w, (list, tuple))
    shapes = [jax.ShapeDtypeStruct((r1 - r0, c1 - c0), F32) for r0, r1, c0, c1 in pieces]
    args = (g, *w, *m, *v) if per_piece else (g, w, m, v)

    def body(g_ref, *refs):
        ins, outs = refs[:len(args) - 1], refs[len(args) - 1:]
        gv = g_ref[...]
        if not per_piece:
            results = (gv,) + _adamw_math(gv, ins[0][...], ins[1][...], ins[2][...])
        for p, (r0, r1, c0, c1) in enumerate(pieces):
            if per_piece:
                gp = gv[r0:r1, c0:c1]
                vals = (gp,) + _adamw_math(gp, ins[p][...], ins[n + p][...], ins[2 * n + p][...])
            else:
                vals = [full[r0:r1, c0:c1] for full in results]
            for kind, val in enumerate(vals):
                outs[kind * n + p][...] = val

    flat = pl.pallas_call(
        body, name=name, out_shape=tuple(shapes) * 4,
        in_specs=[IN_VMEM] * len(args), out_specs=tuple([IN_VMEM] * (4 * n)),
        compiler_params=_params(None),
    )(*args)
    return [[flat[kind * n + p] for kind in range(4)] for p in range(n)]


def _core_sum(g, recv, core, rows, tr, name, ride=None):
    cols = g.shape[2]
    nblk = rows // tr
    n_in = len(ride.args) if ride else 0
    n_out = len(ride.out_shapes) if ride else 0

    def body(c_ref, g_ref, r_ref, *rest):
        sf_ref, sb_ref = rest[n_in], rest[n_in + 1]
        if ride:
            j, i = pl.program_id(0), pl.program_id(1)
            ride.run(j * nblk + i, 4 * nblk, rest[:n_in], rest[n_in + 2:n_in + 2 + n_out],
                     rest[n_in + 2 + n_out:])
        tot = g_ref[...] + r_ref[...]
        sf_ref[...] = tot
        sb_ref[...] = tot.astype(BF16)

    half = pl.BlockSpec((None, tr, cols), lambda j, i, c_ref: (j, i, 0))
    shapes = (jax.ShapeDtypeStruct((4, rows, cols), F32), jax.ShapeDtypeStruct((4, rows, cols), BF16))
    return pl.pallas_call(
        body, name=name,
        grid_spec=pltpu.PrefetchScalarGridSpec(
            num_scalar_prefetch=1, grid=(4, nblk),
            in_specs=[pl.BlockSpec((None, tr, cols), lambda j, i, c_ref: (j, c_ref[0] * nblk + i, 0)), half]
            + (ride.in_specs if ride else []),
            out_specs=(half, half) + (ANY,) * n_out,
            scratch_shapes=ride.scratch() if ride else []),
        out_shape=shapes + tuple(ride.out_shapes if ride else ()),
        compiler_params=_params(("arbitrary", "arbitrary") if ride else ("parallel", "parallel")),
    )(core, g, recv, *(ride.args if ride else ()))


def _half_to_sibling(g4):
    def plan(in_refs, out_refs, send_sems, recv_sems):
        x, y, c = _position()
        cp = pltpu.make_async_remote_copy(
            src_ref=in_refs[0].at[:, 1 - c], dst_ref=out_refs[0], send_sem=send_sems.at[0],
            recv_sem=recv_sems.at[0], device_id=(x, y, 1 - c), device_id_type=MESH)

        def finish():
            cp.wait_recv()
            cp.wait_send()

        return cp.start, finish

    return _Ride([g4], [jax.ShapeDtypeStruct((4, g4.shape[2], 1024), F32)], (1, 1), plan)


def _gather_plan(src_ref, dst_ref, send_sems, recv_sems, local_sems):
    x, y, c = _position()
    me = 2 * x + y
    rows = src_ref.shape[1]
    cut = -(-rows // 32) * 16
    pieces = (pl.ds(0, cut), pl.ds(cut, rows - cut))
    local = pltpu.make_async_copy(src_ref, dst_ref.at[me], local_sems.at[0])

    def over_ici(sem, k, chip, t, src=None):
        where = dst_ref.at[chip, c, pieces[t]]
        return pltpu.make_async_remote_copy(
            src_ref=where if src is None else src, dst_ref=where, send_sem=send_sems.at[sem],
            recv_sem=recv_sems.at[sem], device_id=(x ^ (k >> 1), y ^ (k & 1), c), device_id_type=MESH)

    def mine_to(k, t):
        return over_ici(2 * (k - 1) + t, k, me, t, src=src_ref.at[c, pieces[t]])

    def from_neighbour(k, t):
        return over_ici(2 * (k - 1) + t, k, me ^ k, t)

    def to_sibling(k, half):
        piece = dst_ref.at[me ^ k, half]
        return pltpu.make_async_remote_copy(
            src_ref=piece, dst_ref=piece, send_sem=send_sems.at[5 + k], recv_sem=recv_sems.at[5 + k],
            device_id=(x, y, 1 - c), device_id_type=MESH)

    sends = [mine_to(2, 0), mine_to(1, 1), mine_to(2, 1), mine_to(1, 0)]
    onward = [over_ici(4, 1, me ^ 2, 0), over_ici(5, 2, me ^ 1, 1)]

    def start():
        local.start()
        for cp in sends:
            cp.start()

    def pass_on():
        from_neighbour(2, 0).wait_recv()
        onward[0].start()
        from_neighbour(1, 1).wait_recv()
        onward[1].start()

    def to_other_core():
        from_neighbour(2, 1).wait_recv()
        to_sibling(2, c).start()
        from_neighbour(1, 0).wait_recv()
        to_sibling(1, c).start()
        over_ici(4, 1, me ^ 3, 0).wait_recv()
        over_ici(5, 2, me ^ 3, 1).wait_recv()
        to_sibling(3, c).start()

    def finish():
        for k in (1, 2, 3):
            to_sibling(k, 1 - c).wait_recv()
        for cp in sends + onward + [to_sibling(k, c) for k in (1, 2, 3)]:
            cp.wait_send()
        local.wait()

    return start, pass_on, to_other_core, finish


def _gather_ride(shard, spread):
    def plan(in_refs, out_refs, send_sems, recv_sems, local_sems):
        return _gather_plan(in_refs[0], out_refs[0], send_sems, recv_sems, local_sems)

    return _Ride([shard], [jax.ShapeDtypeStruct((4,) + shard.shape, shard.dtype)], (9, 9, 1), plan,
                 in_specs=[IN_VMEM], spread=spread)


def _chip_sum(sf, recv, chip, rows, tr, name):
    cols = sf.shape[2]
    n_recv = recv.shape[0]

    def body(me_ref, sf_ref, r_ref, out_ref):
        acc = sf_ref[...]
        for k in range(n_recv):
            acc = acc + r_ref[k].astype(F32)
        out_ref[...] = acc

    return pl.pallas_call(
        body, name=name,
        grid_spec=pltpu.PrefetchScalarGridSpec(
            num_scalar_prefetch=1, grid=(rows // tr,),
            in_specs=[pl.BlockSpec((None, tr, cols), lambda i, me_ref: (me_ref[0], i, 0)),
                      pl.BlockSpec((n_recv, tr, cols), lambda i, me_ref: (0, i, 0))],
            out_specs=pl.BlockSpec((tr, cols), lambda i, me_ref: (i, 0))),
        out_shape=jax.ShapeDtypeStruct((rows, cols), F32),
        compiler_params=_params(("parallel",)),
    )(chip, sf, recv)


def _position():
    return lax.axis_index("x"), lax.axis_index("y"), lax.axis_index("c")


def _dh_scatter(dproj, w_in_arr_t, x, dz, g, sb_in, sb_rest, tm=512, tk=3072):
    t, d = x.shape
    nk = dproj.shape[1] // tk
    ni = t // tm
    total = ni * nk
    halves = (HALF_IN, HALF_REST)
    cuts = tuple(-(-rows // 32) * 16 for rows in halves)

    def rows_of(a, p):
        return cuts[a] if p == 0 else halves[a] - cuts[a]

    def piece(a, p):
        return pl.ds(0, cuts[a]) if p == 0 else pl.ds(cuts[a], halves[a] - cuts[a])

    def body(dp_ref, w_ref, x_ref, dz_ref, g_ref, sbin_ref, sbrest_ref, dx_ref, dg_ref, db_ref, rin_ref, rrest_ref,
             acc_ref, pay_in0, pay_in1, pay_rest0, pay_rest1, own_in0, own_in1, own_rest0, own_rest1,
             send_sems, recv_sems, local_sems):
        step = pl.program_id(0) * nk + pl.program_id(1)
        kk = pl.program_id(1)
        px, py, pc = _position()
        me = 2 * px + py
        srcs = (sbin_ref, sbrest_ref)
        dsts = (rin_ref, rrest_ref)
        pays = ((pay_in0, pay_in1), (pay_rest0, pay_rest1))
        owns = ((own_in0, own_in1), (own_rest0, own_rest1))
        via = (2, 1)
        onto = (1, 2)

        def peer(k):
            return (px ^ (k >> 1), py ^ (k & 1), pc)

        def payload(a, p):
            return pltpu.make_async_remote_copy(
                src_ref=srcs[a].at[me ^ 3, piece(a, p)], dst_ref=pays[a][p], send_sem=send_sems.at[2 * a + p],
                recv_sem=recv_sems.at[2 * a + p], device_id=peer(via[p]), device_id_type=MESH)

        def direct(a, k, p, src):
            sem = 4 + 4 * a + 2 * (k - 1) + p
            return pltpu.make_async_remote_copy(
                src_ref=src, dst_ref=dsts[a].at[k - 1, piece(a, p)], send_sem=send_sems.at[sem],
                recv_sem=recv_sems.at[sem], device_id=peer(k), device_id_type=MESH)

        def plain(a, k, p):
            return direct(a, k, p, srcs[a].at[me ^ k, piece(a, p)])

        def stage(a, p):
            return pltpu.make_async_copy(srcs[a].at[me ^ onto[p], piece(a, p)], owns[a][p], local_sems.at[2 * a + p])

        @pl.when(step == 0)
        def _():
            dg_ref[...] = jnp.zeros_like(dg_ref)
            db_ref[...] = jnp.zeros_like(db_ref)
            for a in range(2):
                for p in range(2):
                    payload(a, p).start()
                    stage(a, p).start()
                plain(a, 1, 1).start()
                plain(a, 2, 0).start()

        @pl.when(step == (5 * total) // 8)
        def _():
            for a in range(2):
                for p in range(2):
                    payload(a, p).wait_recv()
                    stage(a, p).wait()
                    owns[a][p][...] = (owns[a][p][...].astype(F32) + pays[a][p][...].astype(F32)).astype(BF16)
                    direct(a, onto[p], p, owns[a][p]).start()

        part = _dot(dp_ref[...], w_ref[...])

        @pl.when(kk == 0)
        def _():
            acc_ref[...] = part

        @pl.when(kk > 0)
        def _():
            acc_ref[...] += part

        @pl.when(kk == nk - 1)
        def _():
            xh, rstd = _ln_hat(x_ref[...])
            dht = acc_ref[...] + DEEPNORM_ALPHA * dz_ref[...]
            dg_ref[...] += _colsum(dht * xh)
            db_ref[...] += _colsum(dht)
            dx_ref[...] = _ln_bwd_rows(dht * g_ref[...], xh, rstd)

        @pl.when(step == total - 1)
        def _():
            for a in range(2):
                for k in (1, 2):
                    for p in range(2):
                        plain(a, k, p).wait_recv()
            for a in range(2):
                for p in range(2):
                    payload(a, p).wait_send()
                    direct(a, onto[p], p, owns[a][p]).wait_send()
                plain(a, 1, 1).wait_send()
                plain(a, 2, 0).wait_send()

    tile = pl.BlockSpec((tm, d), lambda i, kk: (i, 0))
    row = pl.BlockSpec((1, d), lambda i, kk: (0, 0))
    pieces = [pltpu.VMEM((rows_of(a, p), 1024), BF16) for a in range(2) for p in range(2)]
    return pl.pallas_call(
        body, name="dh_scatter", grid=(ni, nk),
        out_shape=(jax.ShapeDtypeStruct((t, d), F32), jax.ShapeDtypeStruct((1, d), F32),
                   jax.ShapeDtypeStruct((1, d), F32),
                   jax.ShapeDtypeStruct((2, HALF_IN, 1024), BF16),
                   jax.ShapeDtypeStruct((2, HALF_REST, 1024), BF16)),
        in_specs=[pl.BlockSpec((tm, tk), lambda i, kk: (i, kk)), pl.BlockSpec((tk, d), lambda i, kk: (kk, 0)),
                  tile, tile, row, ANY, ANY],
        out_specs=(tile, row, row, ANY, ANY),
        scratch_shapes=[pltpu.VMEM((tm, d), F32)] + pieces + pieces
        + [pltpu.SemaphoreType.DMA((12,)), pltpu.SemaphoreType.DMA((12,)), pltpu.SemaphoreType.DMA((4,))],
        compiler_params=_params(("arbitrary", "arbitrary")),
    )(dproj, w_in_arr_t, x, dz, g, sb_in, sb_rest)


def _join_and_allreduce(gh_in, gh_rest, vec):
    def body(hin_ref, hrest_ref, vec_ref, oin_ref, orest_ref, sum_ref, all_ref, send_sems, recv_sems, local_sems):
        x, y, c = _position()
        srcs = (hin_ref, hrest_ref)
        dsts = (oin_ref, orest_ref)
        me = 4 * x + 2 * y + c
        all_ref[me] = vec_ref[...]

        def small(k, slot):
            return pltpu.make_async_remote_copy(
                src_ref=vec_ref, dst_ref=all_ref.at[slot], send_sem=send_sems.at[k + 1], recv_sem=recv_sems.at[k + 1],
                device_id=(x ^ (k >> 2), y ^ ((k >> 1) & 1), c ^ (k & 1)), device_id_type=MESH)

        def half(a, slot):
            return pltpu.make_async_remote_copy(
                src_ref=srcs[a], dst_ref=dsts[a].at[slot], send_sem=send_sems.at[a], recv_sem=recv_sems.at[a],
                device_id=(x, y, 1 - c), device_id_type=MESH)

        local = [pltpu.make_async_copy(srcs[a], dsts[a].at[c], local_sems.at[a]) for a in range(2)]
        remote = [half(a, c) for a in range(2)] + [small(k, me) for k in range(1, 8)]
        for cp in local + remote:
            cp.start()
        for k in range(1, 8):
            small(k, me ^ k).wait_recv()
        for a in range(2):
            half(a, 1 - c).wait_recv()
        for cp in remote:
            cp.wait_send()
        for cp in local:
            cp.wait()
        total = all_ref[0]
        for d in range(1, 8):
            total = total + all_ref[d]
        sum_ref[...] = total

    return pl.pallas_call(
        body, name="join_halves",
        out_shape=(jax.ShapeDtypeStruct((2, HALF_IN, 1024), F32),
                   jax.ShapeDtypeStruct((2, HALF_REST, 1024), F32),
                   jax.ShapeDtypeStruct(vec.shape, vec.dtype)),
        in_specs=[IN_VMEM, IN_VMEM, IN_VMEM], out_specs=(ANY, ANY, IN_VMEM),
        scratch_shapes=[pltpu.VMEM((8,) + vec.shape, vec.dtype), pltpu.SemaphoreType.DMA((9,)),
                        pltpu.SemaphoreType.DMA((9,)), pltpu.SemaphoreType.DMA((2,))],
    )(gh_in, gh_rest, vec)


def _pack_rest(w_uq, w_ukv, w_mem, w_out):
    rows = jnp.concatenate([w_uq[0].T.reshape(-1, 1024), w_ukv.reshape(-1, 1024), w_mem.reshape(-1, 1024),
                            w_out.reshape(-1, 1024)], axis=0)
    return jnp.pad(rows, ((0, ROWS_REST - ROWS_USED), (0, 0)))


def _arranged_w_in(g_in):
    z = functools.partial(jnp.zeros, dtype=g_in.dtype)
    cut = 4480 - 2 * SHARD_ROWS
    return jnp.concatenate(
        [g_in[0, :SHARD_ROWS], g_in[1, :SHARD_ROWS], g_in[2, :cut], z((64, 1024)), g_in[2, cut:cut + 32],
         z((32, 1024)), g_in[2, cut + 32:SHARD_ROWS], g_in[3, :SHARD_ROWS]], axis=0)


def _rest_weights(g_rest):
    w_uq_t = g_rest[:, 0:ROWS_UQ].reshape(768, 256)
    w_uq_pad_t = jnp.pad(w_uq_t.reshape(MLA_HEADS, MLA_QK_DIM, 256), ((0, 0), (0, 32), (0, 0))).reshape(1024, 256)
    w_ukv = jnp.concatenate([g_rest[j, ROWS_UQ:ROWS_UQ + ROWS_UKV].reshape(128, 256) for j in range(4)], axis=1)
    lo = ROWS_UQ + ROWS_UKV
    w_mem = g_rest[:, lo:lo + ROWS_MEM].reshape(4 * ROWS_MEM, 1024)
    w_out = g_rest[:, lo + ROWS_MEM:lo + ROWS_MEM + ROWS_OUT].reshape(4 * ROWS_OUT, 1024)
    return w_uq_pad_t, w_ukv, w_mem, w_out


def _split_in(dw_in_arr_t):
    a = dw_in_arr_t
    gap = jnp.zeros((ROWS_IN - SHARD_ROWS, 1024), a.dtype)
    nat = 4608 - 96
    pieces = [a[:SHARD_ROWS], gap, a[SHARD_ROWS:2 * SHARD_ROWS], gap,
              a[2 * SHARD_ROWS:4480], a[4544:4576], a[4608:4608 + 3 * SHARD_ROWS - nat], gap,
              a[4608 + 3 * SHARD_ROWS - nat:], gap]
    return jnp.concatenate(pieces, axis=0).reshape(4, ROWS_IN, 1024)


def _split_rest(dw_uq_pad_t, dw_ukv, dw_mem, dw_out):
    dw_uq_t = dw_uq_pad_t.reshape(MLA_HEADS, LANES, 256)[:, :MLA_QK_DIM].reshape(4, ROWS_UQ, 1024)
    parts = [dw_uq_t, dw_ukv.reshape(128, 4, 256).transpose(1, 0, 2).reshape(4, ROWS_UKV, 1024),
             dw_mem.reshape(4, ROWS_MEM, 1024), dw_out.reshape(4, ROWS_OUT, 1024)]
    return jnp.pad(jnp.concatenate(parts, axis=1), ((0, 0), (0, ROWS_REST - ROWS_USED), (0, 0)))


def _rope_consts(rot, first, period):
    half = rot // 2
    inv_freq = np.float32(ROPE_THETA) ** (-(np.arange(0, rot, 2, dtype=np.float32) / np.float32(rot)))
    lane = np.arange(LANES) % period - first
    in_rot = (lane >= 0) & (lane < rot)
    out = np.zeros((8, LANES), np.float32)
    out[0] = np.where(in_rot, inv_freq[np.clip(lane, 0, rot - 1) % half], 0.0)
    out[1] = in_rot & (lane < half)
    out[2] = in_rot & (lane >= half)
    return jnp.asarray(out)


def _band_bias(s):
    nblk = s // BAND_Q
    starts = np.array([_band_start(i, s) for i in range(nblk)])
    uq = (np.arange(nblk)[:, None] * BAND_Q + np.arange(BAND_Q)[None, :])[:, :, None]
    uk = (starts[:, None] + np.arange(BAND_WIN)[None, :])[:, None, :]
    tiles, index, seen = [], [], {}
    for _, d in DILATED:
        length = s // d
        ok = (uq // length == uk // length) & (np.abs(uq - uk) <= 64)
        row = []
        for i in range(nblk):
            key = ok[i].tobytes()
            if key not in seen:
                seen[key] = len(tiles)
                tiles.append(np.where(ok[i], 0.0, NEG_INF).astype(np.float32))
            row.append(seen[key])
        index.append(row)
    return jnp.asarray(np.stack(tiles, axis=0)), index


def _forward_backward(h, h32, proj, trig, rope_consts, x, mem, target, weights, gains):
    w_uq_pad_t, w_ukv, w_mem, w_out = weights
    g_emb, b_emb, g_cq, g_ckv, g_out_a, g_out_b, g_out_m, g_post, b_post = gains
    nb, s, d = x.shape
    t = nb * s
    x2 = x.reshape(t, d)
    mem2 = mem.reshape(nb * N_MEM, d)
    tgt2 = target.reshape(t, d)
    rope_a, rope_b = rope_consts
    bias, bias_index = _band_bias(s)
    scales = (0.125, MLA_QK_DIM ** -0.5, 128 ** -0.5)

    qa, ka, va, qb, kb, vb, qm = _prep(proj, trig, w_uq_pad_t, w_ukv, g_cq, g_ckv, rope_a, rope_b, scales)
    mkv = _mm(mem2, w_mem, BF16, nb * N_MEM, 1024, 1024, "mem_kv")

    cfg_b = dict(nb=nb, s=s, sk=s, heads=8, voff=0, bq=256)
    cfg_m = dict(nb=nb, s=s, sk=N_MEM, heads=4, hpb=2, voff=4, bq=1024)
    ya, lse_a, qkv_ordered = _dilated_fwd(qa, ka, va, bias, bias_index, nb=nb, s=s, name="attn_a_fwd")
    yb, lse_b = _attn_fwd(qb, kb, vb, name="attn_b_fwd", hpb=4, **cfg_b)
    ym, lse_m = _attn_fwd(qm, mkv, mkv, name="attn_m_fwd", **cfg_m)

    (y, dz, doa, dob, dom, dga, dgb, dgm, loss, dg_post, db_post, dg_a, dg_b, dg_m) = _post(
        h32, ya, yb, ym, proj, tgt2, w_out, g_out_a, g_out_b, g_out_m, g_post, b_post)

    dqa, dka, dva = _dilated_bwd(qa, ka, va, qkv_ordered, ya, doa, lse_a, bias, bias_index, nb=nb, s=s, scale=scales[0],
                                 name="attn_a_bwd")
    dqb, dkb, dvb = _attn_bwd(qb, kb, vb, yb, dob, lse_b, name="attn_b_bwd", scale=scales[1], hpb=4, **cfg_b)
    dqm, dmk, dmv = _attn_bwd(qm, mkv, mkv, ym, dom, lse_m, name="attn_m_bwd", scale=scales[2], **cfg_m)
    dmkv = jnp.concatenate([dmk, dmv], axis=1)

    dproj, dw_uq_pad_t, dw_ukv, dg_cq, dg_ckv = _prep_bwd(
        dqa, dka, dva, dqb, dkb, dvb, dqm, dga, dgb, dgm, proj, trig, w_uq_pad_t, w_ukv, g_cq, g_ckv, rope_a, rope_b)

    small_rows = (dg_cq, dg_ckv, loss, dg_a, dg_b, dg_m, dg_post, db_post)
    return (dproj, h, y, dz, dw_uq_pad_t, dw_ukv, mem2, dmkv), x2, small_rows


def _weight_grads(operands, core):
    dproj, h, y, dz, dw_uq_pad_t, dw_ukv, mem2, dmkv = operands
    dw_in_arr_t = _mm(dproj, h, F32, 1024, 1024, 4096, "dw_in", mode="tn")
    g_in = _split_in(dw_in_arr_t)
    dw_out, r_in = _mm(y, dz, F32, 1024, 1024, 2048, "dw_out", mode="tn",
                       ride=_half_to_sibling(g_in.reshape(4, 2, HALF_IN, 1024)))
    dw_mem = _mm(mem2, dmkv, F32, 1024, 1024, mem2.shape[0], "dw_mem", mode="tn")
    g_rest = _split_rest(dw_uq_pad_t, dw_ukv, dw_mem, dw_out)
    sf_in, sb_in, r_rest = _core_sum(g_in, r_in, core, HALF_IN, HALF_IN // 2, "core_sum_in",
                                     ride=_half_to_sibling(g_rest.reshape(4, 2, HALF_REST, 1024)))
    sf_rest, sb_rest = _core_sum(g_rest, r_rest, core, HALF_REST, HALF_REST, "core_sum_rest")
    return sf_in, sb_in, sf_rest, sb_rest


def _small_block(dg_emb, db_emb, small_rows):
    dg_cq, dg_ckv, loss, dg_a, dg_b, dg_m, dg_post, db_post = small_rows
    row2 = jnp.concatenate([dg_cq, dg_ckv, loss, jnp.zeros((1, 512), F32)], axis=1)
    return jnp.concatenate([dg_emb, db_emb, row2, dg_a, jnp.concatenate([dg_b, dg_m], axis=1), dg_post, db_post,
                            jnp.zeros((1, 1024), F32)], axis=0)


def _pack_small(g_emb, b_emb, g_cq, g_ckv, g_out_a, g_out_b, g_out_m, g_post, b_post):
    row2 = jnp.concatenate([g_cq.reshape(1, -1), g_ckv.reshape(1, -1), jnp.zeros((1, 640), F32)], axis=1)
    return jnp.concatenate([g_emb.reshape(1, -1), b_emb.reshape(1, -1), row2, g_out_a.reshape(1, -1),
                            jnp.concatenate([g_out_b.reshape(1, -1), g_out_m.reshape(1, -1)], axis=1),
                            g_post.reshape(1, -1), b_post.reshape(1, -1), jnp.zeros((1, 1024), F32)], axis=0)


def kernel(x, mem, positions, g_emb, b_emb, w_in, g_cq, g_ckv, w_uq, w_ukv, w_mem_kv, g_out_a, g_out_b, g_out_m, w_out, g_post, b_post, loss_target, m_g_emb, m_b_emb, m_w_in, m_g_cq, m_g_ckv, m_w_uq, m_w_ukv, m_w_mem_kv, m_g_out_a, m_g_out_b, m_g_out_m, m_w_out, m_g_post, m_b_post, v_g_emb, v_b_emb, v_w_in, v_g_cq, v_g_ckv, v_w_uq, v_w_ukv, v_w_mem_kv, v_g_out_a, v_g_out_b, v_g_out_m, v_w_out, v_g_post, v_b_post):
    w_rest = _pack_rest(w_uq, w_ukv, w_mem_kv, w_out)
    w_in_t = w_in[0].T
    w_in_b = jnp.pad(w_in_t.astype(BF16), ((0, ROWS_IN - SHARD_ROWS), (0, 0)))
    gains = (g_emb.reshape(1, -1), b_emb.reshape(1, -1), g_cq, g_ckv, g_out_a, g_out_b, g_out_m, g_post, b_post)
    rope_consts = (_rope_consts(16, 0, 64), _rope_consts(32, 64, 128))
    h, h32, trig, gathered_in = _ln_fwd(x.reshape(-1, D_MODEL), gains[0], gains[1],
                                        positions.reshape(-1, 1).astype(F32), *rope_consts,
                                        ride=_gather_ride(w_in_b.reshape(2, HALF_IN, 1024), spread=False))
    w_in_arr_t = _arranged_w_in(gathered_in.reshape(4, ROWS_IN, 1024))
    proj, gathered_rest = _mm(h, w_in_arr_t, F32, 1024, 2048, 1024, "in_proj", mode="nt",
                              ride=_gather_ride(w_rest.astype(BF16).reshape(2, HALF_REST, 1024), spread=True))
    weights = _rest_weights(gathered_rest.reshape(4, ROWS_REST, 1024))
    operands, x2, small_rows = _forward_backward(h, h32, proj, trig, rope_consts, x, mem, loss_target, weights,
                                                 gains)

    core = lax.axis_index("c").astype(jnp.int32).reshape(1)
    chip = (2 * lax.axis_index("x") + lax.axis_index("y")).astype(jnp.int32).reshape(1)
    sf_in, sb_in, sf_rest, sb_rest = _weight_grads(operands, core)
    grad_x, dg_emb, db_emb, rb_in, rb_rest = _dh_scatter(operands[0], w_in_arr_t, x2, operands[3], gains[0],
                                                         sb_in, sb_rest)
    gh_in = _chip_sum(sf_in, rb_in, chip, HALF_IN, HALF_IN // 2, "chip_sum_in")
    gh_rest = _chip_sum(sf_rest, rb_rest, chip, HALF_REST, HALF_REST, "chip_sum_rest")
    grad_in, grad_rest, small_sum = _join_and_allreduce(gh_in, gh_rest, _small_block(dg_emb, db_emb, small_rows))
    grad_in = grad_in.reshape(ROWS_IN, 1024)
    grad_rest = grad_rest.reshape(ROWS_REST, 1024)

    big_in = _adamw(grad_in, w_in_t, m_w_in[0].T, v_w_in[0].T, SHARD_ROWS // 3, "adamw_in")
    def rest_parts(a_uq, a_ukv, a_mem, a_out):
        return [a_uq[0].T.reshape(ROWS_UQ, 1024), a_ukv.reshape(ROWS_UKV, 1024), a_mem[0], a_out[0]]

    uq, ukv, wmem, wout = _adamw_pieces(
        grad_rest, rest_parts(w_uq, w_ukv, w_mem_kv, w_out), rest_parts(m_w_uq, m_w_ukv, m_w_mem_kv, m_w_out),
        rest_parts(v_w_uq, v_w_ukv, v_w_mem_kv, v_w_out), REST_PIECES, "adamw_rest")
    sm = _adamw_pieces(
        small_sum,
        _pack_small(g_emb, b_emb, g_cq, g_ckv, g_out_a, g_out_b, g_out_m, g_post, b_post),
        _pack_small(m_g_emb, m_b_emb, m_g_cq, m_g_ckv, m_g_out_a, m_g_out_b, m_g_out_m, m_g_post, m_b_post),
        _pack_small(v_g_emb, v_b_emb, v_g_cq, v_g_ckv, v_g_out_a, v_g_out_b, v_g_out_m, v_g_post, v_b_post),
        SMALL_PIECES, "adamw_small")
    loss = small_sum[2, 384]

    def ordered(kind):
        s_gemb, s_bemb, s_gcq, s_gckv, s_ga, s_gb, s_gm, s_gpost, s_bpost = [piece[kind] for piece in sm]
        return [s_gemb.reshape(-1), s_bemb.reshape(-1), big_in[kind].T[None], s_gcq, s_gckv,
                uq[kind].reshape(192, 256).T[None], ukv[kind].reshape(1, 128, 256), wmem[kind][None], s_ga, s_gb,
                s_gm, wout[kind][None], s_gpost, s_bpost]

    return (loss, grad_x.reshape(x.shape), *ordered(0), *ordered(1), *ordered(2), *ordered(3))
```

```python
import functools
import math

import jax
import jax.numpy as jnp
import numpy as np
from jax import lax
from jax.experimental import pallas as pl
from jax.experimental.pallas import tpu as pltpu

F32 = jnp.float32
BF16 = jnp.bfloat16
MESH = pl.DeviceIdType.MESH
ANY = pl.BlockSpec(memory_space=pl.ANY)
IN_VMEM = pl.BlockSpec(memory_space=pltpu.VMEM)

D_MODEL = 1024
A_WIDTH = 1024
MLA_HEADS = 8
MLA_Q_RANK = 256
MLA_KV_RANK = 128
MLA_QK_DIM = 96
MEM_WIDTH = 512
N_MEM = 256
ROPE_THETA = 500000.0
NORM_EPS = 1e-5
NEG_INF = -1e30
DEEPNORM_ALPHA = 2.0 ** 0.25
DILATED = ((64, 1), (256, 4), (1024, 16))

ADAM_LR = 0.001
ADAM_B1 = 0.9
ADAM_B2 = 0.999
ADAM_EPS = 1e-08
ADAM_WD = 0.01
ADAM_STEP = 10

LANES = 128
VMEM_LIMIT = 56 * 1024 * 1024
LOG2E = math.log2(math.e)
LN2 = math.log(2.0)

PROJ_W = 6144
COL_CQ = 4096
COL_BG = 4608
COL_MQ = 5120
COL_MG = 5632

SHARD_ROWS = 1512
ROWS_IN = 1536
ROWS_UQ, ROWS_UKV, ROWS_MEM, ROWS_OUT = 48, 32, 256, 512
ROWS_USED = ROWS_UQ + ROWS_UKV + ROWS_MEM + ROWS_OUT
ROWS_REST = 864
HALF_IN = ROWS_IN // 2
HALF_REST = ROWS_REST // 2
REST_PIECES = ((0, 48, 0, 1024), (48, 80, 0, 1024), (80, 336, 0, 1024), (336, 848, 0, 1024))
SMALL_PIECES = ((0, 1, 0, 1024), (1, 2, 0, 1024), (2, 3, 0, 256), (2, 3, 256, 384), (3, 4, 0, 1024), (4, 5, 0, 512),
                (4, 5, 512, 1024), (5, 6, 0, 1024), (6, 7, 0, 1024))


def _params(sem=None, vmem=VMEM_LIMIT):
    return pltpu.CompilerParams(dimension_semantics=sem, vmem_limit_bytes=vmem)


def _dot(a, b):
    return jnp.dot(a, b, preferred_element_type=F32)


def _dot_nt(a, b):
    return lax.dot_general(a, b, (((1,), (1,)), ((), ())), preferred_element_type=F32)


def _dot_tn(a, b):
    return lax.dot_general(a, b, (((0,), (0,)), ((), ())), preferred_element_type=F32)


def _ln_hat(x):
    mu = jnp.mean(x, axis=-1, keepdims=True)
    xc = x - mu
    var = jnp.mean(xc * xc, axis=-1, keepdims=True)
    rstd = lax.rsqrt(var + NORM_EPS)
    return xc * rstd, rstd


def _ln_bwd_rows(dxh, xh, rstd):
    return rstd * (dxh - jnp.mean(dxh, axis=-1, keepdims=True) - xh * jnp.mean(dxh * xh, axis=-1, keepdims=True))


def _rms_hat(x, width):
    ms = jnp.sum(x * x, axis=-1, keepdims=True) * (1.0 / width)
    r = lax.rsqrt(ms + NORM_EPS)
    return x * r, r


def _rms_bwd(u, xh, r, width):
    return r * (u - xh * (jnp.sum(u * xh, axis=-1, keepdims=True) * (1.0 / width)))


def _colsum(v):
    return jnp.sum(v, axis=0, keepdims=True)


def _rope_tables(cos, sin, consts):
    return cos, sin * consts[2:3, :], -sin * consts[1:2, :]


def _rope(x, tables, half, inverse=False):
    c, s_up, s_dn = tables
    if inverse:
        s_up, s_dn = -s_up, -s_dn
    return x * c + pltpu.roll(x, half, 1) * s_up + pltpu.roll(x, LANES - half, 1) * s_dn


def _ln_fwd(x, g, b, pos, rope_a, rope_b, tm=512, ride=None):
    t, d = x.shape
    n_in = len(ride.args) if ride else 0
    n_out = len(ride.out_shapes) if ride else 0
    steps = t // tm

    def body(x_ref, g_ref, b_ref, pos_ref, ra_ref, rb_ref, *rest):
        h_ref, h32_ref, trig_ref = rest[n_in:n_in + 3]
        if ride:
            i = pl.program_id(0)
            ride.run(i, steps, rest[:n_in], rest[n_in + 3:n_in + 3 + n_out], rest[n_in + 3 + n_out:])
        xh, _ = _ln_hat(x_ref[...])
        h = xh * g_ref[...] + b_ref[...]
        h32_ref[...] = h
        h_ref[...] = h.astype(BF16)
        for j, consts in enumerate((ra_ref, rb_ref)):
            ang = pos_ref[...] * consts[0:1, :]
            trig_ref[:, 2 * j * LANES:(2 * j + 1) * LANES] = jnp.cos(ang)
            trig_ref[:, (2 * j + 1) * LANES:(2 * j + 2) * LANES] = jnp.sin(ang)

    row = pl.BlockSpec((1, d), lambda i: (0, 0))
    tile = pl.BlockSpec((tm, d), lambda i: (i, 0))
    consts = pl.BlockSpec((8, LANES), lambda i: (0, 0))
    trig_tile = pl.BlockSpec((tm, 4 * LANES), lambda i: (i, 0))
    in_specs = [tile, row, row, pl.BlockSpec((tm, 1), lambda i: (i, 0)), consts, consts]
    shapes = (jax.ShapeDtypeStruct((t, d), BF16), jax.ShapeDtypeStruct((t, d), F32),
              jax.ShapeDtypeStruct((t, 4 * LANES), F32))
    if not ride:
        return pl.pallas_call(
            body, name="ln_fwd", grid=(steps,), out_shape=shapes, in_specs=in_specs,
            out_specs=(tile, tile, trig_tile), compiler_params=_params(("parallel",)),
        )(x, g, b, pos, rope_a, rope_b)
    return pl.pallas_call(
        body, name="ln_fwd", grid=(steps,),
        out_shape=(*shapes, *ride.out_shapes),
        in_specs=in_specs + ride.in_specs, out_specs=(tile, tile, trig_tile) + (ANY,) * n_out,
        scratch_shapes=ride.scratch(),
        compiler_params=_params(("arbitrary",)),
    )(x, g, b, pos, rope_a, rope_b, *ride.args)


class _Ride:
    def __init__(self, args, out_shapes, sem_counts, plan, in_specs=None, spread=True):
        self.args, self.out_shapes, self.plan = list(args), list(out_shapes), plan
        self.sem_counts = sem_counts
        self.in_specs = in_specs or [ANY] * len(self.args)
        self.spread = spread

    def scratch(self):
        return [pltpu.SemaphoreType.DMA((n,)) for n in self.sem_counts]

    def run(self, step, total, in_refs, out_refs, sems):
        count = len(self.plan(in_refs, out_refs, *sems))
        at = [(k * (total - 1)) // (count - 1) if self.spread or k == 0 else total - 1 for k in range(count)]
        for when in sorted(set(at)):
            @pl.when(step == when)
            def _(when=when):
                stages = self.plan(in_refs, out_refs, *sems)
                for k in range(count):
                    if at[k] == when:
                        stages[k]()


def _mm(a, b, out_dtype, tm, tn, tk, name, mode="nn", ride=None):
    if mode == "tn":
        k, m = a.shape
    else:
        m, k = a.shape
    n = b.shape[0] if mode == "nt" else b.shape[1]
    nk = k // tk
    nj, ni = n // tn, m // tm
    n_in = len(ride.args) if ride else 0
    n_out = len(ride.out_shapes) if ride else 0

    def body(a_ref, b_ref, *rest):
        o_ref = rest[n_in]
        acc_ref = rest[n_in + 1 + n_out]
        if ride:
            j, i, kk = pl.program_id(0), pl.program_id(1), pl.program_id(2)
            ride.run((j * ni + i) * nk + kk, nj * ni * nk, rest[:n_in], rest[n_in + 1:n_in + 1 + n_out],
                     rest[n_in + 2 + n_out:])
        av = a_ref[...].astype(BF16)
        bv = b_ref[...].astype(BF16)
        part = _dot_tn(av, bv) if mode == "tn" else _dot_nt(av, bv) if mode == "nt" else _dot(av, bv)
        if nk == 1:
            o_ref[...] = part.astype(out_dtype)
        else:
            kk = pl.program_id(2)

            @pl.when(kk == 0)
            def _():
                acc_ref[...] = part

            @pl.when(kk > 0)
            def _():
                acc_ref[...] += part

            @pl.when(kk == nk - 1)
            def _():
                o_ref[...] = acc_ref[...].astype(out_dtype)

    a_spec = (pl.BlockSpec((tk, tm), lambda j, i, kk: (kk, i)) if mode == "tn"
              else pl.BlockSpec((tm, tk), lambda j, i, kk: (i, kk)))
    b_spec = (pl.BlockSpec((tn, tk), lambda j, i, kk: (j, kk)) if mode == "nt"
              else pl.BlockSpec((tk, tn), lambda j, i, kk: (kk, j)))
    o_spec = pl.BlockSpec((tm, tn), lambda j, i, kk: (i, j))
    o_shape = jax.ShapeDtypeStruct((m, n), out_dtype)
    if not ride:
        return pl.pallas_call(
            body, name=name, grid=(nj, ni, nk), out_shape=o_shape, in_specs=[a_spec, b_spec], out_specs=o_spec,
            scratch_shapes=[pltpu.VMEM((tm, tn), F32)],
            compiler_params=_params(("parallel", "parallel", "arbitrary")),
        )(a, b)
    return pl.pallas_call(
        body, name=name, grid=(nj, ni, nk),
        out_shape=(o_shape, *ride.out_shapes),
        in_specs=[a_spec, b_spec] + ride.in_specs,
        out_specs=(o_spec,) + (ANY,) * n_out,
        scratch_shapes=[pltpu.VMEM((tm, tn), F32)] + ride.scratch(),
        compiler_params=_params(("arbitrary", "arbitrary", "arbitrary")),
    )(a, b, *ride.args)


def _prep(proj, trig, w_uq, w_ukv, g_cq, g_ckv, rope_a, rope_b, scales, tm=512):
    t = proj.shape[0]
    sc_a, sc_b, sc_m = (s * LOG2E for s in scales)

    def body(aq_ref, ak_ref, av_ref, bs_ref, mq_ref, trig_ref, wuq_ref, wukv_ref, gcq_ref, gckv_ref,
             ra_ref, rb_ref, qa_ref, ka_ref, va_ref, qb_ref, kb_ref, vb_ref, qm_ref):
        ta = _rope_tables(trig_ref[:, 0:LANES], trig_ref[:, LANES:2 * LANES], ra_ref[...])
        tb = _rope_tables(trig_ref[:, 2 * LANES:3 * LANES], trig_ref[:, 3 * LANES:4 * LANES], rb_ref[...])
        for j in range(A_WIDTH // LANES):
            sl = slice(j * LANES, (j + 1) * LANES)
            qa_ref[:, sl] = (_rope(aq_ref[:, sl], ta, 8) * sc_a).astype(BF16)
            ka_ref[:, sl] = _rope(ak_ref[:, sl], ta, 8).astype(BF16)
        va_ref[...] = av_ref[...].astype(BF16)
        qm_ref[...] = (mq_ref[...] * sc_m).astype(BF16)

        cq_hat, _ = _rms_hat(bs_ref[:, 0:MLA_Q_RANK], MLA_Q_RANK)
        cqn = (cq_hat * gcq_ref[...]).astype(BF16)
        ckv_hat, _ = _rms_hat(bs_ref[:, MLA_Q_RANK:MLA_Q_RANK + MLA_KV_RANK], MLA_KV_RANK)
        ckvn = (ckv_hat * gckv_ref[...]).astype(BF16)
        qfull = _dot_nt(cqn, wuq_ref[...])
        kv = _dot(ckvn, wukv_ref[...])
        kr = _rope(bs_ref[:, 384:512], tb, 16)
        lane = lax.broadcasted_iota(jnp.int32, (1, LANES), 1)
        low = lane < 64
        for h in range(MLA_HEADS):
            sl = slice(h * LANES, (h + 1) * LANES)
            qb_ref[:, sl] = (_rope(qfull[:, sl], tb, 16) * sc_b).astype(BF16)
            kb_ref[:, sl] = jnp.where(low, kv[:, sl], kr).astype(BF16)
            vb_ref[:, sl] = jnp.where(low, 0.0, kv[:, sl]).astype(BF16)

    def col(width, idx):
        return pl.BlockSpec((tm, width), lambda i: (i, idx))

    def full(shape):
        return pl.BlockSpec(shape, lambda i: (0, 0))

    wide = jax.ShapeDtypeStruct((t, 1024), BF16)
    return pl.pallas_call(
        body, name="prep", grid=(t // tm,),
        out_shape=(wide, wide, wide, wide, wide, wide,
                   jax.ShapeDtypeStruct((t, MEM_WIDTH), BF16)),
        in_specs=[col(1024, 0), col(1024, 1), col(1024, 2), col(512, COL_CQ // 512), col(512, COL_MQ // 512),
                  pl.BlockSpec((tm, 4 * LANES), lambda i: (i, 0)),
                  full((1024, MLA_Q_RANK)), full((MLA_KV_RANK, 1024)),
                  full((1, MLA_Q_RANK)), full((1, MLA_KV_RANK)), full((8, LANES)), full((8, LANES))],
        out_specs=(col(1024, 0),) * 6 + (col(MEM_WIDTH, 0),),
        compiler_params=_params(("parallel",)),
    )(proj, proj, proj, proj, proj, trig, w_uq, w_ukv, g_cq, g_ckv, rope_a, rope_b)


def _attn_fwd(q, k, v, *, nb, s, sk, heads, hpb, voff, bq, name):
    nq = s // bq
    width = hpb * LANES
    vblk = voff // hpb

    def body(q_ref, k_ref, v_ref, o_ref, lse_ref):
        for h in range(hpb):
            sl = slice(h * LANES, (h + 1) * LANES)
            sc = _dot_nt(q_ref[:, sl], k_ref[:, sl])
            m = jnp.max(sc, axis=1, keepdims=True)
            p = jnp.exp2(sc - m)
            l = jnp.sum(p, axis=1, keepdims=True)
            o_ref[:, sl] = _dot(p.astype(BF16), v_ref[:, sl]) / l
            lse_ref[:, sl] = jnp.broadcast_to(m + jnp.log(l) * LOG2E, (bq, LANES))

    out = jax.ShapeDtypeStruct((nb * s, heads * LANES), F32)
    ospec = pl.BlockSpec((bq, width), lambda b, i, g: (b * nq + i, g))
    return pl.pallas_call(
        body, name=name, grid=(nb, nq, heads // hpb),
        out_shape=(out, out),
        in_specs=[ospec, pl.BlockSpec((sk, width), lambda b, i, g: (b, g)),
                  pl.BlockSpec((sk, width), lambda b, i, g: (b, vblk + g))],
        out_specs=(ospec, ospec),
        compiler_params=_params(("parallel", "parallel", "parallel")),
    )(q, k, v)


def _attn_bwd(q, k, v, o, do, lse, *, nb, s, sk, heads, hpb, voff, scale, bq, name):
    nq = s // bq
    width = hpb * LANES
    vblk = voff // hpb

    def body(q_ref, k_ref, v_ref, o_ref, do_ref, lse_ref, dq_ref, dk_ref, dv_ref, dk_acc, dv_acc):
        i = pl.program_id(2)

        @pl.when(i == 0)
        def _():
            dk_acc[...] = jnp.zeros_like(dk_acc)
            dv_acc[...] = jnp.zeros_like(dv_acc)

        for h in range(hpb):
            sl = slice(h * LANES, (h + 1) * LANES)
            qh = q_ref[:, sl]
            kk = k_ref[:, sl]
            doh = do_ref[:, sl]
            delta = jnp.sum(doh.astype(F32) * o_ref[:, sl], axis=1, keepdims=True)
            p = jnp.exp2(_dot_nt(qh, kk) - lse_ref[:, h * LANES:h * LANES + 1])
            ds = (p * (_dot_nt(doh, v_ref[:, sl]) - delta)).astype(BF16)
            dq_ref[:, sl] = (_dot(ds, kk) * scale).astype(BF16)
            dk_acc[:, sl] += _dot_tn(ds, qh)
            dv_acc[:, sl] += _dot_tn(p.astype(BF16), doh)

        @pl.when(i == nq - 1)
        def _():
            dk_ref[...] = (dk_acc[...] * LN2).astype(BF16)
            dv_ref[...] = dv_acc[...].astype(BF16)

    qspec = pl.BlockSpec((bq, width), lambda b, g, i: (b * nq + i, g))
    kv_spec = pl.BlockSpec((sk, width), lambda b, g, i: (b, g))
    dq_shape = jax.ShapeDtypeStruct((nb * s, heads * LANES), BF16)
    dkv_shape = jax.ShapeDtypeStruct((nb * sk, heads * LANES), BF16)
    return pl.pallas_call(
        body, name=name, grid=(nb, heads // hpb, nq),
        out_shape=(dq_shape, dkv_shape, dkv_shape),
        in_specs=[qspec, kv_spec, pl.BlockSpec((sk, width), lambda b, g, i: (b, vblk + g)), qspec, qspec, qspec],
        out_specs=(qspec, kv_spec, kv_spec),
        scratch_shapes=[pltpu.VMEM((sk, width), F32), pltpu.VMEM((sk, width), F32)],
        compiler_params=_params(("parallel", "parallel", "arbitrary")),
    )(q, k, v, o, do, lse)


BAND_Q = 128
BAND_WIN = 256


def _band_start(i, s):
    return min(max(i * BAND_Q - 64, 0), s - BAND_WIN)


def _to_pattern_order(src_ref, dst_ref, stage_ref, s, d):
    length = s // d
    stage_ref[...] = src_ref[...].astype(F32)
    for r in range(d):
        dst_ref[r * length:(r + 1) * length, :] = stage_ref[pl.ds(r, length, stride=d), :].astype(dst_ref.dtype)


def _dilated_fwd(q, k, v, bias, bias_index, *, nb, s, name):
    nblk = s // BAND_Q
    npat = len(DILATED)

    def body(q_ref, k_ref, v_ref, bias_ref, o_ref, lse_ref, *rest):
        ordered = rest[:3 * (npat - 1)]
        stage_ref, op_ref, lp_ref, on_ref, ln_ref = rest[3 * (npat - 1):]
        lane = lax.broadcasted_iota(jnp.int32, (1, LANES), 1)
        first = lane < 64
        for p, (_, d) in enumerate(DILATED):
            if d == 1:
                qs, ks, vs = q_ref, k_ref, v_ref
            else:
                qs, ks, vs = ordered[3 * (p - 1):3 * p]
                for src, dst in ((q_ref, qs), (k_ref, ks), (v_ref, vs)):
                    _to_pattern_order(src, dst, stage_ref, s, d)
            for i in range(nblk):
                u0 = i * BAND_Q
                st = _band_start(i, s)
                qi = qs[u0:u0 + BAND_Q, :]
                kw = ks[st:st + BAND_WIN, :]
                vw = vs[st:st + BAND_WIN, :]
                zero = jnp.zeros_like(qi)
                q2 = jnp.concatenate([jnp.where(first, qi, zero), jnp.where(first, zero, qi)], axis=0)
                sc = _dot_nt(q2, kw)
                b = bias_ref[bias_index[p][i]]
                halves = []
                for h in range(2):
                    sh = sc[h * BAND_Q:(h + 1) * BAND_Q] + b
                    m = jnp.max(sh, axis=1, keepdims=True)
                    pr = jnp.exp2(sh - m)
                    l = jnp.sum(pr, axis=1, keepdims=True)
                    halves.append((pr.astype(BF16), l, m + jnp.log(l) * LOG2E))
                o2 = _dot(jnp.concatenate([halves[0][0], halves[1][0]], axis=0), vw)
                o_blk = jnp.where(first, o2[:BAND_Q] / halves[0][1], o2[BAND_Q:] / halves[1][1])
                lse_blk = jnp.where(first, jnp.broadcast_to(halves[0][2], (BAND_Q, LANES)),
                                    jnp.broadcast_to(halves[1][2], (BAND_Q, LANES)))
                op_ref[p, u0:u0 + BAND_Q, :] = o_blk
                lp_ref[p, u0:u0 + BAND_Q, :] = lse_blk
            if d > 1:
                length = s // d
                for r in range(d):
                    on_ref.at[p - 1][pl.ds(r, length, stride=d), :] = op_ref[p, r * length:(r + 1) * length, :]
                    ln_ref.at[p - 1][pl.ds(r, length, stride=d), :] = lp_ref[p, r * length:(r + 1) * length, :]
        lses = [lp_ref[0]] + [ln_ref[p] for p in range(npat - 1)]
        outs = [op_ref[0]] + [on_ref[p] for p in range(npat - 1)]
        m = functools.reduce(jnp.maximum, lses)
        ws = [jnp.exp2(l - m) for l in lses]
        den = functools.reduce(lambda a, c: a + c, ws)
        o_ref[...] = functools.reduce(lambda a, c: a + c, [w * o for w, o in zip(ws, outs)]) / den
        lse_ref[...] = m + jnp.log(den) * LOG2E

    blk = pl.BlockSpec((s, LANES), lambda b, g: (b, g))
    out = jax.ShapeDtypeStruct((nb * s, A_WIDTH), F32)
    copy = jax.ShapeDtypeStruct((nb * s, A_WIDTH), BF16)
    n_copies = 3 * (npat - 1)
    res = pl.pallas_call(
        body, name=name, grid=(nb, A_WIDTH // LANES),
        out_shape=(out, out) + (copy,) * n_copies,
        in_specs=[blk, blk, blk, pl.BlockSpec(bias.shape, lambda b, g: (0, 0, 0))],
        out_specs=(blk, blk) + (blk,) * n_copies,
        scratch_shapes=[pltpu.VMEM((s, LANES), F32), pltpu.VMEM((npat, s, LANES), F32),
                        pltpu.VMEM((npat, s, LANES), F32), pltpu.VMEM((npat - 1, s, LANES), F32),
                        pltpu.VMEM((npat - 1, s, LANES), F32)],
        compiler_params=_params(("parallel", "parallel")),
    )(q, k, v, bias)
    return res[0], res[1], res[2:]


def _dilated_bwd(q, k, v, ordered, o, do, lse, bias, bias_index, *, nb, s, scale, name):
    nblk = s // BAND_Q
    npat = len(DILATED)
    n_copies = 3 * (npat - 1)

    def body(q_ref, k_ref, v_ref, *rest):
        ordered_refs = rest[:n_copies]
        (o_ref, do_ref, lse_ref, bias_ref, dq_out, dk_out, dv_out, stage_ref, rs_ref, dop_ref, rsp_ref,
         dqp_ref, dkp_ref, dvp_ref, dq_ref, dk_ref, dv_ref, nat_ref) = rest[n_copies:]
        lane = lax.broadcasted_iota(jnp.int32, (1, LANES), 1)
        first = lane < 64
        prod = do_ref[...].astype(F32) * o_ref[...]
        d0 = jnp.sum(jnp.where(first, prod, 0.0), axis=1, keepdims=True)
        d1 = jnp.sum(jnp.where(first, 0.0, prod), axis=1, keepdims=True)
        delta = jnp.where(first, jnp.broadcast_to(d0, (s, LANES)), jnp.broadcast_to(d1, (s, LANES)))
        rs_ref[...] = jnp.where((lane & 32) == 0, lse_ref[...], delta)
        for p, (_, d) in enumerate(DILATED):
            length = s // d
            if d == 1:
                qs, ks, vs, dos, rss = q_ref, k_ref, v_ref, do_ref, rs_ref
                dqs, dks, dvs = dq_ref, dk_ref, dv_ref
            else:
                for src, dst in ((do_ref, dop_ref), (rs_ref, rsp_ref)):
                    _to_pattern_order(src, dst, stage_ref, s, d)
                qs, ks, vs = ordered_refs[3 * (p - 1):3 * p]
                dos, rss = dop_ref, rsp_ref
                dqs, dks, dvs = dqp_ref, dkp_ref, dvp_ref
            dks[...] = jnp.zeros((s, LANES), F32)
            dvs[...] = jnp.zeros((s, LANES), F32)
            for i in range(nblk):
                u0 = i * BAND_Q
                st = _band_start(i, s)
                qi = qs[u0:u0 + BAND_Q, :]
                doi = dos[u0:u0 + BAND_Q, :]
                kw = ks[st:st + BAND_WIN, :]
                vw = vs[st:st + BAND_WIN, :]
                zero = jnp.zeros_like(qi)
                q2 = jnp.concatenate([jnp.where(first, qi, zero), jnp.where(first, zero, qi)], axis=0)
                do2 = jnp.concatenate([jnp.where(first, doi, zero), jnp.where(first, zero, doi)], axis=0)
                sc = _dot_nt(q2, kw)
                dp = _dot_nt(do2, vw)
                b = bias_ref[bias_index[p][i]]
                rs_i = rss[u0:u0 + BAND_Q, :]
                ps, dss = [], []
                for h in range(2):
                    rows = slice(h * BAND_Q, (h + 1) * BAND_Q)
                    pr = jnp.exp2(sc[rows] + b - rs_i[:, 64 * h:64 * h + 1])
                    ps.append(pr.astype(BF16))
                    dss.append((pr * (dp[rows] - rs_i[:, 64 * h + 32:64 * h + 33])).astype(BF16))
                p2 = jnp.concatenate(ps, axis=0)
                ds2 = jnp.concatenate(dss, axis=0)
                dq2 = _dot(ds2, kw)
                dqs[u0:u0 + BAND_Q, :] = jnp.where(first, dq2[:BAND_Q], dq2[BAND_Q:]) * scale
                dks[st:st + BAND_WIN, :] += _dot_tn(ds2, q2)
                dvs[st:st + BAND_WIN, :] += _dot_tn(p2, do2)
            if d > 1:
                for j, src in enumerate((dqp_ref, dkp_ref, dvp_ref)):
                    for r in range(d):
                        nat_ref.at[p - 1, j][pl.ds(r, length, stride=d), :] = src[r * length:(r + 1) * length, :]

        def total(j, first_ref):
            return functools.reduce(lambda a, c: a + c, [first_ref[...]] + [nat_ref[p, j] for p in range(npat - 1)])

        dq_out[...] = total(0, dq_ref).astype(BF16)
        dk_out[...] = (total(1, dk_ref) * LN2).astype(BF16)
        dv_out[...] = total(2, dv_ref).astype(BF16)

    blk = pl.BlockSpec((s, LANES), lambda b, g: (b, g))
    out = jax.ShapeDtypeStruct((nb * s, A_WIDTH), BF16)
    f32_buf = pltpu.VMEM((s, LANES), F32)
    bf_buf = pltpu.VMEM((s, LANES), BF16)
    return pl.pallas_call(
        body, name=name, grid=(nb, A_WIDTH // LANES),
        out_shape=(out, out, out),
        in_specs=[blk] * (6 + n_copies) + [pl.BlockSpec(bias.shape, lambda b, g: (0, 0, 0))],
        out_specs=(blk, blk, blk),
        scratch_shapes=[f32_buf, f32_buf, bf_buf] + [f32_buf] * 7 + [pltpu.VMEM((npat - 1, 3, s, LANES), F32)],
        compiler_params=_params(("parallel", "parallel")),
    )(q, k, v, *ordered, o, do, lse, bias)


def _post(h32, ya, ybp, ym, proj, target, w_out, g_a, g_b, g_m, g_post, b_post, tm=256):
    t = h32.shape[0]

    def body(h_ref, ya_ref, yb_ref, ym_ref, ga_ref, gb_ref, gm_ref, tg_ref, wo_ref,
             goa_ref, gob_ref, gom_ref, gp_ref, bp_ref,
             y_ref, dz_ref, doa_ref, dob_ref, dom_ref, dga_ref, dgb_ref, dgm_ref,
             loss_ref, dgp_ref, dbp_ref, dgoa_ref, dgob_ref, dgom_ref):
        i = pl.program_id(0)

        @pl.when(i == 0)
        def _():
            for r in (loss_ref, dgp_ref, dbp_ref, dgoa_ref, dgob_ref, dgom_ref):
                r[...] = jnp.zeros_like(r)

        lane = lax.broadcasted_iota(jnp.int32, (1, LANES), 1)
        low = lane < 64
        h = h_ref[...]

        ybp_v = yb_ref[...]
        yb = jnp.concatenate(
            [jnp.where(low, pltpu.roll(ybp_v[:, 2 * j * LANES:(2 * j + 1) * LANES], 64, 1),
                       ybp_v[:, (2 * j + 1) * LANES:(2 * j + 2) * LANES]) for j in range(4)], axis=1)

        def gated(raw, gate, gain, width):
            xh, r = _rms_hat(raw, width)
            n = xh * gain
            sg = 1.0 / (1.0 + jnp.exp(-gate))
            return xh, r, n, sg, n * (gate * sg)

        gate_a, gate_b, gate_m = ga_ref[...], gb_ref[...], gm_ref[...]
        xh_a, r_a, n_a, sg_a, y_a = gated(ya_ref[...], gate_a, goa_ref[...], A_WIDTH)
        xh_b, r_b, n_b, sg_b, y_b = gated(yb, gate_b, gob_ref[...], 512)
        xh_m, r_m, n_m, sg_m, y_m = gated(ym_ref[...], gate_m, gom_ref[...], 512)
        y = jnp.concatenate([y_a, y_b, y_m], axis=1).astype(BF16)
        y_ref[...] = y
        z = DEEPNORM_ALPHA * h + _dot(y, wo_ref[...])
        zh, rstd = _ln_hat(z)
        err = zh * gp_ref[...] + bp_ref[...] - tg_ref[...]
        rows = jnp.sum(err * err, axis=1, keepdims=True)
        loss_ref[...] += jnp.broadcast_to(jnp.sum(rows, axis=0, keepdims=True) * (0.5 / D_MODEL), (1, LANES))
        dout = err * (1.0 / D_MODEL)
        dgp_ref[...] += _colsum(dout * zh)
        dbp_ref[...] += _colsum(dout)
        dz = _ln_bwd_rows(dout * gp_ref[...], zh, rstd)
        dz_ref[...] = dz
        dy = _dot_nt(dz.astype(BF16), wo_ref[...])

        def gated_bwd(dyg, xh, r, n, sg, gate, gain, width, dgain_ref):
            dn = dyg * (gate * sg)
            dgate = dyg * n * (sg * (1.0 + gate * (1.0 - sg)))
            dgain_ref[...] += _colsum(dn * xh)
            return _rms_bwd(dn * gain, xh, r, width), dgate

        dya, dgate_a = gated_bwd(dy[:, 0:1024], xh_a, r_a, n_a, sg_a, gate_a, goa_ref[...], A_WIDTH, dgoa_ref)
        dyb, dgate_b = gated_bwd(dy[:, 1024:1536], xh_b, r_b, n_b, sg_b, gate_b, gob_ref[...], 512, dgob_ref)
        dym, dgate_m = gated_bwd(dy[:, 1536:2048], xh_m, r_m, n_m, sg_m, gate_m, gom_ref[...], 512, dgom_ref)
        doa_ref[...] = dya.astype(BF16)
        dom_ref[...] = dym.astype(BF16)
        dga_ref[...] = dgate_a.astype(BF16)
        dgb_ref[...] = dgate_b.astype(BF16)
        dgm_ref[...] = dgate_m.astype(BF16)
        for j in range(4):
            blk = dyb[:, j * LANES:(j + 1) * LANES]
            dob_ref[:, 2 * j * LANES:(2 * j + 1) * LANES] = jnp.where(low, 0.0, pltpu.roll(blk, 64, 1)).astype(BF16)
            dob_ref[:, (2 * j + 1) * LANES:(2 * j + 2) * LANES] = jnp.where(low, 0.0, blk).astype(BF16)

    def col(width, idx):
        return pl.BlockSpec((tm, width), lambda i: (i, idx))

    def full(shape):
        return pl.BlockSpec(shape, lambda i: (0, 0))

    def acc(width):
        return jax.ShapeDtypeStruct((1, width), F32)

    return pl.pallas_call(
        body, name="post", grid=(t // tm,),
        out_shape=(jax.ShapeDtypeStruct((t, 2048), BF16), jax.ShapeDtypeStruct((t, 1024), F32),
                   jax.ShapeDtypeStruct((t, 1024), BF16), jax.ShapeDtypeStruct((t, 1024), BF16),
                   jax.ShapeDtypeStruct((t, 512), BF16),
                   jax.ShapeDtypeStruct((t, 1024), BF16), jax.ShapeDtypeStruct((t, 512), BF16),
                   jax.ShapeDtypeStruct((t, 512), BF16),
                   acc(LANES), acc(1024), acc(1024), acc(1024), acc(512), acc(512)),
        in_specs=[col(1024, 0), col(1024, 0), col(1024, 0), col(512, 0),
                  col(1024, 3), col(512, COL_BG // 512), col(512, COL_MG // 512), col(1024, 0),
                  full((2048, 1024)),
                  full((1, 1024)), full((1, 512)), full((1, 512)), full((1, 1024)), full((1, 1024))],
        out_specs=(col(2048, 0), col(1024, 0), col(1024, 0), col(1024, 0), col(512, 0),
                   col(1024, 0), col(512, 0), col(512, 0),
                   full((1, LANES)), full((1, 1024)), full((1, 1024)), full((1, 1024)), full((1, 512)),
                   full((1, 512))),
        compiler_params=_params(("arbitrary",)),
    )(h32, ya, ybp, ym, proj, proj, proj, target, w_out, g_a, g_b, g_m, g_post, b_post)


def _prep_bwd(dqa, dka, dva, dqb, dkb, dvb, dqm, dga, dgb, dgm, proj, trig, w_uq, w_ukv, g_cq, g_ckv,
              rope_a, rope_b, tm=512):
    t = proj.shape[0]

    def body(dqa_ref, dka_ref, dva_ref, dqb_ref, dkb_ref, dvb_ref, dqm_ref, dga_ref, dgb_ref, dgm_ref,
             bs_ref, trig_ref, wuq_ref, wukv_ref, gcq_ref, gckv_ref, ra_ref, rb_ref,
             dproj_ref, dwuq_ref, dwukv_ref, dgcq_ref, dgckv_ref, dqf_ref, dkv_ref):
        i = pl.program_id(0)

        @pl.when(i == 0)
        def _():
            dwuq_ref[...] = jnp.zeros_like(dwuq_ref)
            dwukv_ref[...] = jnp.zeros_like(dwukv_ref)
            dgcq_ref[...] = jnp.zeros_like(dgcq_ref)
            dgckv_ref[...] = jnp.zeros_like(dgckv_ref)

        ta = _rope_tables(trig_ref[:, 0:LANES], trig_ref[:, LANES:2 * LANES], ra_ref[...])
        tb = _rope_tables(trig_ref[:, 2 * LANES:3 * LANES], trig_ref[:, 3 * LANES:4 * LANES], rb_ref[...])
        for j in range(A_WIDTH // LANES):
            sl = slice(j * LANES, (j + 1) * LANES)
            dproj_ref[:, j * LANES:(j + 1) * LANES] = (
                _rope(dqa_ref[:, sl].astype(F32), ta, 8, inverse=True).astype(BF16))
            dproj_ref[:, 1024 + j * LANES:1024 + (j + 1) * LANES] = (
                _rope(dka_ref[:, sl].astype(F32), ta, 8, inverse=True).astype(BF16))
        dproj_ref[:, 2048:3072] = dva_ref[...]
        dproj_ref[:, 3072:4096] = dga_ref[...]

        lane = lax.broadcasted_iota(jnp.int32, (1, LANES), 1)
        low = lane < 64
        rope_lanes = (lane >= 64) & (lane < 96)
        dkr = jnp.zeros((tm, LANES), F32)
        for h in range(MLA_HEADS):
            sl = slice(h * LANES, (h + 1) * LANES)
            dqf_ref[:, sl] = _rope(dqb_ref[:, sl].astype(F32), tb, 16, inverse=True).astype(BF16)
            dk_h = dkb_ref[:, sl]
            dkv_ref[:, sl] = jnp.where(low, dk_h, dvb_ref[:, sl])
            dkr = dkr + jnp.where(rope_lanes, dk_h.astype(F32), 0.0)
        dkr = _rope(dkr, tb, 16, inverse=True)

        cq_hat, r_q = _rms_hat(bs_ref[:, 0:MLA_Q_RANK], MLA_Q_RANK)
        dwuq_ref[...] += _dot_tn(dqf_ref[...], (cq_hat * gcq_ref[...]).astype(BF16))
        dcqn = _dot(dqf_ref[...], wuq_ref[...])
        dgcq_ref[...] += _colsum(dcqn * cq_hat)
        dproj_ref[:, COL_CQ:COL_CQ + 256] = _rms_bwd(dcqn * gcq_ref[...], cq_hat, r_q, MLA_Q_RANK).astype(BF16)
        ckv_hat, r_kv = _rms_hat(bs_ref[:, MLA_Q_RANK:MLA_Q_RANK + MLA_KV_RANK], MLA_KV_RANK)
        dwukv_ref[...] += _dot_tn((ckv_hat * gckv_ref[...]).astype(BF16), dkv_ref[...])
        dckvn = _dot_nt(dkv_ref[...], wukv_ref[...])
        dgckv_ref[...] += _colsum(dckvn * ckv_hat)
        dproj_ref[:, COL_CQ + 256:COL_CQ + 384] = (
            _rms_bwd(dckvn * gckv_ref[...], ckv_hat, r_kv, MLA_KV_RANK).astype(BF16))
        dproj_ref[:, COL_CQ + 384:COL_CQ + 512] = dkr.astype(BF16)
        dproj_ref[:, COL_BG:COL_BG + 512] = dgb_ref[...]
        dproj_ref[:, COL_MQ:COL_MQ + 512] = dqm_ref[...]
        dproj_ref[:, COL_MG:COL_MG + 512] = dgm_ref[...]

    def col(width, idx):
        return pl.BlockSpec((tm, width), lambda i: (i, idx))

    def full(shape):
        return pl.BlockSpec(shape, lambda i: (0, 0))

    return pl.pallas_call(
        body, name="prep_bwd", grid=(t // tm,),
        out_shape=(jax.ShapeDtypeStruct((t, PROJ_W), BF16), jax.ShapeDtypeStruct((1024, MLA_Q_RANK), F32),
                   jax.ShapeDtypeStruct((MLA_KV_RANK, 1024), F32),
                   jax.ShapeDtypeStruct((1, MLA_Q_RANK), F32), jax.ShapeDtypeStruct((1, MLA_KV_RANK), F32)),
        in_specs=[col(1024, 0)] * 6 + [col(512, 0), col(1024, 0), col(512, 0), col(512, 0),
                  col(512, COL_CQ // 512), pl.BlockSpec((tm, 4 * LANES), lambda i: (i, 0)),
                  full((1024, MLA_Q_RANK)), full((MLA_KV_RANK, 1024)),
                  full((1, MLA_Q_RANK)), full((1, MLA_KV_RANK)), full((8, LANES)), full((8, LANES))],
        out_specs=(col(PROJ_W, 0), full((1024, MLA_Q_RANK)), full((MLA_KV_RANK, 1024)),
                   full((1, MLA_Q_RANK)), full((1, MLA_KV_RANK))),
        scratch_shapes=[pltpu.VMEM((tm, 1024), BF16), pltpu.VMEM((tm, 1024), BF16)],
        compiler_params=_params(("arbitrary",)),
    )(dqa, dka, dva, dqb, dkb, dvb, dqm, dga, dgb, dgm, proj, trig, w_uq, w_ukv, g_cq, g_ckv, rope_a, rope_b)


def _adamw_math(gv, w, m, v):
    m_new = ADAM_B1 * m + (1.0 - ADAM_B1) * gv
    v_new = ADAM_B2 * v + (1.0 - ADAM_B2) * (gv * gv)
    m_hat = m_new / (1.0 - ADAM_B1 ** ADAM_STEP)
    v_hat = v_new / (1.0 - ADAM_B2 ** ADAM_STEP)
    return -ADAM_LR * (m_hat / (jnp.sqrt(v_hat) + ADAM_EPS) + ADAM_WD * w), m_new, v_new


def _adamw(g, w, m, v, tr, name):
    r, cols = w.shape

    def body(g_ref, w_ref, m_ref, v_ref, go_ref, d_ref, nm_ref, nv_ref):
        gv = g_ref[...]
        go_ref[...] = gv
        d_ref[...], nm_ref[...], nv_ref[...] = _adamw_math(gv, w_ref[...], m_ref[...], v_ref[...])

    tile = pl.BlockSpec((tr, cols), lambda i: (i, 0))
    shape = jax.ShapeDtypeStruct((r, cols), F32)
    return pl.pallas_call(
        body, name=name, grid=(r // tr,),
        out_shape=(shape,) * 4, in_specs=[tile] * 4, out_specs=(tile,) * 4,
        compiler_params=_params(("parallel",)),
    )(g, w, m, v)


def _adamw_pieces(g, w, m, v, pieces, name):
    n = len(pieces)
    per_piece = isinstance(w, (list, tuple))
    shapes = [jax.ShapeDtypeStruct((r1 - r0, c1 - c0), F32) for r0, r1, c0, c1 in pieces]
    args = (g, *w, *m, *v) if per_piece else (g, w, m, v)

    def body(g_ref, *refs):
        ins, outs = refs[:len(args) - 1], refs[len(args) - 1:]
        gv = g_ref[...]
        if not per_piece:
            results = (gv,) + _adamw_math(gv, ins[0][...], ins[1][...], ins[2][...])
        for p, (r0, r1, c0, c1) in enumerate(pieces):
            if per_piece:
                gp = gv[r0:r1, c0:c1]
                vals = (gp,) + _adamw_math(gp, ins[p][...], ins[n + p][...], ins[2 * n + p][...])
            else:
                vals = [full[r0:r1, c0:c1] for full in results]
            for kind, val in enumerate(vals):
                outs[kind * n + p][...] = val

    flat = pl.pallas_call(
        body, name=name, out_shape=tuple(shapes) * 4,
        in_specs=[IN_VMEM] * len(args), out_specs=tuple([IN_VMEM] * (4 * n)),
        compiler_params=_params(None),
    )(*args)
    return [[flat[kind * n + p] for kind in range(4)] for p in range(n)]


def _core_sum(g, recv, core, rows, tr, name, ride=None):
    cols = g.shape[2]
    nblk = rows // tr
    n_in = len(ride.args) if ride else 0
    n_out = len(ride.out_shapes) if ride else 0

    def body(c_ref, g_ref, r_ref, *rest):
        sf_ref, sb_ref = rest[n_in], rest[n_in + 1]
        if ride:
            j, i = pl.program_id(0), pl.program_id(1)
            ride.run(j * nblk + i, 4 * nblk, rest[:n_in], rest[n_in + 2:n_in + 2 + n_out],
                     rest[n_in + 2 + n_out:])
        tot = g_ref[...] + r_ref[...]
        sf_ref[...] = tot
        sb_ref[...] = tot.astype(BF16)

    half = pl.BlockSpec((None, tr, cols), lambda j, i, c_ref: (j, i, 0))
    shapes = (jax.ShapeDtypeStruct((4, rows, cols), F32), jax.ShapeDtypeStruct((4, rows, cols), BF16))
    return pl.pallas_call(
        body, name=name,
        grid_spec=pltpu.PrefetchScalarGridSpec(
            num_scalar_prefetch=1, grid=(4, nblk),
            in_specs=[pl.BlockSpec((None, tr, cols), lambda j, i, c_ref: (j, c_ref[0] * nblk + i, 0)), half]
            + (ride.in_specs if ride else []),
            out_specs=(half, half) + (ANY,) * n_out,
            scratch_shapes=ride.scratch() if ride else []),
        out_shape=shapes + tuple(ride.out_shapes if ride else ()),
        compiler_params=_params(("arbitrary", "arbitrary") if ride else ("parallel", "parallel")),
    )(core, g, recv, *(ride.args if ride else ()))


def _half_to_sibling(g4):
    def plan(in_refs, out_refs, send_sems, recv_sems):
        x, y, c = _position()
        cp = pltpu.make_async_remote_copy(
            src_ref=in_refs[0].at[:, 1 - c], dst_ref=out_refs[0], send_sem=send_sems.at[0],
            recv_sem=recv_sems.at[0], device_id=(x, y, 1 - c), device_id_type=MESH)

        def finish():
            cp.wait_recv()
            cp.wait_send()

        return cp.start, finish

    return _Ride([g4], [jax.ShapeDtypeStruct((4, g4.shape[2], 1024), F32)], (1, 1), plan)


def _gather_plan(src_ref, dst_ref, send_sems, recv_sems, local_sems):
    x, y, c = _position()
    me = 2 * x + y
    rows = src_ref.shape[1]
    cut = -(-rows // 32) * 16
    pieces = (pl.ds(0, cut), pl.ds(cut, rows - cut))
    local = pltpu.make_async_copy(src_ref, dst_ref.at[me], local_sems.at[0])

    def over_ici(sem, k, chip, t, src=None):
        where = dst_ref.at[chip, c, pieces[t]]
        return pltpu.make_async_remote_copy(
            src_ref=where if src is None else src, dst_ref=where, send_sem=send_sems.at[sem],
            recv_sem=recv_sems.at[sem], device_id=(x ^ (k >> 1), y ^ (k & 1), c), device_id_type=MESH)

    def mine_to(k, t):
        return over_ici(2 * (k - 1) + t, k, me, t, src=src_ref.at[c, pieces[t]])

    def from_neighbour(k, t):
        return over_ici(2 * (k - 1) + t, k, me ^ k, t)

    def to_sibling(k, half):
        piece = dst_ref.at[me ^ k, half]
        return pltpu.make_async_remote_copy(
            src_ref=piece, dst_ref=piece, send_sem=send_sems.at[5 + k], recv_sem=recv_sems.at[5 + k],
            device_id=(x, y, 1 - c), device_id_type=MESH)

    sends = [mine_to(2, 0), mine_to(1, 1), mine_to(2, 1), mine_to(1, 0)]
    onward = [over_ici(4, 1, me ^ 2, 0), over_ici(5, 2, me ^ 1, 1)]

    def start():
        local.start()
        for cp in sends:
            cp.start()

    def pass_on():
        from_neighbour(2, 0).wait_recv()
        onward[0].start()
        from_neighbour(1, 1).wait_recv()
        onward[1].start()

    def to_other_core():
        from_neighbour(2, 1).wait_recv()
        to_sibling(2, c).start()
        from_neighbour(1, 0).wait_recv()
        to_sibling(1, c).start()
        over_ici(4, 1, me ^ 3, 0).wait_recv()
        over_ici(5, 2, me ^ 3, 1).wait_recv()
        to_sibling(3, c).start()

    def finish():
        for k in (1, 2, 3):
            to_sibling(k, 1 - c).wait_recv()
        for cp in sends + onward + [to_sibling(k, c) for k in (1, 2, 3)]:
            cp.wait_send()
        local.wait()

    return start, pass_on, to_other_core, finish


def _gather_ride(shard, spread):
    def plan(in_refs, out_refs, send_sems, recv_sems, local_sems):
        return _gather_plan(in_refs[0], out_refs[0], send_sems, recv_sems, local_sems)

    return _Ride([shard], [jax.ShapeDtypeStruct((4,) + shard.shape, shard.dtype)], (9, 9, 1), plan,
                 in_specs=[IN_VMEM], spread=spread)


def _chip_sum(sf, recv, chip, rows, tr, name):
    cols = sf.shape[2]
    n_recv = recv.shape[0]

    def body(me_ref, sf_ref, r_ref, out_ref):
        acc = sf_ref[...]
        for k in range(n_recv):
            acc = acc + r_ref[k].astype(F32)
        out_ref[...] = acc

    return pl.pallas_call(
        body, name=name,
        grid_spec=pltpu.PrefetchScalarGridSpec(
            num_scalar_prefetch=1, grid=(rows // tr,),
            in_specs=[pl.BlockSpec((None, tr, cols), lambda i, me_ref: (me_ref[0], i, 0)),
                      pl.BlockSpec((n_recv, tr, cols), lambda i, me_ref: (0, i, 0))],
            out_specs=pl.BlockSpec((tr, cols), lambda i, me_ref: (i, 0))),
        out_shape=jax.ShapeDtypeStruct((rows, cols), F32),
        compiler_params=_params(("parallel",)),
    )(chip, sf, recv)


def _position():
    return lax.axis_index("x"), lax.axis_index("y"), lax.axis_index("c")


def _dh_scatter(dproj, w_in_arr_t, x, dz, g, sb_in, sb_rest, tm=512, tk=3072):
    t, d = x.shape
    nk = dproj.shape[1] // tk
    ni = t // tm
    total = ni * nk
    halves = (HALF_IN, HALF_REST)
    cuts = tuple(-(-rows // 32) * 16 for rows in halves)

    def rows_of(a, p):
        return cuts[a] if p == 0 else halves[a] - cuts[a]

    def piece(a, p):
        return pl.ds(0, cuts[a]) if p == 0 else pl.ds(cuts[a], halves[a] - cuts[a])

    def body(dp_ref, w_ref, x_ref, dz_ref, g_ref, sbin_ref, sbrest_ref, dx_ref, dg_ref, db_ref, rin_ref, rrest_ref,
             acc_ref, pay_in0, pay_in1, pay_rest0, pay_rest1, own_in0, own_in1, own_rest0, own_rest1,
             send_sems, recv_sems, local_sems):
        step = pl.program_id(0) * nk + pl.program_id(1)
        kk = pl.program_id(1)
        px, py, pc = _position()
        me = 2 * px + py
        srcs = (sbin_ref, sbrest_ref)
        dsts = (rin_ref, rrest_ref)
        pays = ((pay_in0, pay_in1), (pay_rest0, pay_rest1))
        owns = ((own_in0, own_in1), (own_rest0, own_rest1))
        via = (2, 1)
        onto = (1, 2)

        def peer(k):
            return (px ^ (k >> 1), py ^ (k & 1), pc)

        def payload(a, p):
            return pltpu.make_async_remote_copy(
                src_ref=srcs[a].at[me ^ 3, piece(a, p)], dst_ref=pays[a][p], send_sem=send_sems.at[2 * a + p],
                recv_sem=recv_sems.at[2 * a + p], device_id=peer(via[p]), device_id_type=MESH)

        def direct(a, k, p, src):
            sem = 4 + 4 * a + 2 * (k - 1) + p
            return pltpu.make_async_remote_copy(
                src_ref=src, dst_ref=dsts[a].at[k - 1, piece(a, p)], send_sem=send_sems.at[sem],
                recv_sem=recv_sems.at[sem], device_id=peer(k), device_id_type=MESH)

        def plain(a, k, p):
            return direct(a, k, p, srcs[a].at[me ^ k, piece(a, p)])

        def stage(a, p):
            return pltpu.make_async_copy(srcs[a].at[me ^ onto[p], piece(a, p)], owns[a][p], local_sems.at[2 * a + p])

        @pl.when(step == 0)
        def _():
            dg_ref[...] = jnp.zeros_like(dg_ref)
            db_ref[...] = jnp.zeros_like(db_ref)
            for a in range(2):
                for p in range(2):
                    payload(a, p).start()
                    stage(a, p).start()
                plain(a, 1, 1).start()
                plain(a, 2, 0).start()

        @pl.when(step == (5 * total) // 8)
        def _():
            for a in range(2):
                for p in range(2):
                    payload(a, p).wait_recv()
                    stage(a, p).wait()
                    owns[a][p][...] = (owns[a][p][...].astype(F32) + pays[a][p][...].astype(F32)).astype(BF16)
                    direct(a, onto[p], p, owns[a][p]).start()

        part = _dot(dp_ref[...], w_ref[...])

        @pl.when(kk == 0)
        def _():
            acc_ref[...] = part

        @pl.when(kk > 0)
        def _():
            acc_ref[...] += part

        @pl.when(kk == nk - 1)
        def _():
            xh, rstd = _ln_hat(x_ref[...])
            dht = acc_ref[...] + DEEPNORM_ALPHA * dz_ref[...]
            dg_ref[...] += _colsum(dht * xh)
            db_ref[...] += _colsum(dht)
            dx_ref[...] = _ln_bwd_rows(dht * g_ref[...], xh, rstd)

        @pl.when(step == total - 1)
        def _():
            for a in range(2):
                for k in (1, 2):
                    for p in range(2):
                        plain(a, k, p).wait_recv()
            for a in range(2):
                for p in range(2):
                    payload(a, p).wait_send()
                    direct(a, onto[p], p, owns[a][p]).wait_send()
                plain(a, 1, 1).wait_send()
                plain(a, 2, 0).wait_send()

    tile = pl.BlockSpec((tm, d), lambda i, kk: (i, 0))
    row = pl.BlockSpec((1, d), lambda i, kk: (0, 0))
    pieces = [pltpu.VMEM((rows_of(a, p), 1024), BF16) for a in range(2) for p in range(2)]
    return pl.pallas_call(
        body, name="dh_scatter", grid=(ni, nk),
        out_shape=(jax.ShapeDtypeStruct((t, d), F32), jax.ShapeDtypeStruct((1, d), F32),
                   jax.ShapeDtypeStruct((1, d), F32),
                   jax.ShapeDtypeStruct((2, HALF_IN, 1024), BF16),
                   jax.ShapeDtypeStruct((2, HALF_REST, 1024), BF16)),
        in_specs=[pl.BlockSpec((tm, tk), lambda i, kk: (i, kk)), pl.BlockSpec((tk, d), lambda i, kk: (kk, 0)),
                  tile, tile, row, ANY, ANY],
        out_specs=(tile, row, row, ANY, ANY),
        scratch_shapes=[pltpu.VMEM((tm, d), F32)] + pieces + pieces
        + [pltpu.SemaphoreType.DMA((12,)), pltpu.SemaphoreType.DMA((12,)), pltpu.SemaphoreType.DMA((4,))],
        compiler_params=_params(("arbitrary", "arbitrary")),
    )(dproj, w_in_arr_t, x, dz, g, sb_in, sb_rest)


def _join_and_allreduce(gh_in, gh_rest, vec):
    def body(hin_ref, hrest_ref, vec_ref, oin_ref, orest_ref, sum_ref, all_ref, send_sems, recv_sems, local_sems):
        x, y, c = _position()
        srcs = (hin_ref, hrest_ref)
        dsts = (oin_ref, orest_ref)
        me = 4 * x + 2 * y + c
        all_ref[me] = vec_ref[...]

        def small(k, slot):
            return pltpu.make_async_remote_copy(
                src_ref=vec_ref, dst_ref=all_ref.at[slot], send_sem=send_sems.at[k + 1], recv_sem=recv_sems.at[k + 1],
                device_id=(x ^ (k >> 2), y ^ ((k >> 1) & 1), c ^ (k & 1)), device_id_type=MESH)

        def half(a, slot):
            return pltpu.make_async_remote_copy(
                src_ref=srcs[a], dst_ref=dsts[a].at[slot], send_sem=send_sems.at[a], recv_sem=recv_sems.at[a],
                device_id=(x, y, 1 - c), device_id_type=MESH)

        local = [pltpu.make_async_copy(srcs[a], dsts[a].at[c], local_sems.at[a]) for a in range(2)]
        remote = [half(a, c) for a in range(2)] + [small(k, me) for k in range(1, 8)]
        for cp in local + remote:
            cp.start()
        for k in range(1, 8):
            small(k, me ^ k).wait_recv()
        for a in range(2):
            half(a, 1 - c).wait_recv()
        for cp in remote:
            cp.wait_send()
        for cp in local:
            cp.wait()
        total = all_ref[0]
        for d in range(1, 8):
            total = total + all_ref[d]
        sum_ref[...] = total

    return pl.pallas_call(
        body, name="join_halves",
        out_shape=(jax.ShapeDtypeStruct((2, HALF_IN, 1024), F32),
                   jax.ShapeDtypeStruct((2, HALF_REST, 1024), F32),
                   jax.ShapeDtypeStruct(vec.shape, vec.dtype)),
        in_specs=[IN_VMEM, IN_VMEM, IN_VMEM], out_specs=(ANY, ANY, IN_VMEM),
        scratch_shapes=[pltpu.VMEM((8,) + vec.shape, vec.dtype), pltpu.SemaphoreType.DMA((9,)),
                        pltpu.SemaphoreType.DMA((9,)), pltpu.SemaphoreType.DMA((2,))],
    )(gh_in, gh_rest, vec)


def _pack_rest(w_uq, w_ukv, w_mem, w_out):
    rows = jnp.concatenate([w_uq[0].T.reshape(-1, 1024), w_ukv.reshape(-1, 1024), w_mem.reshape(-1, 1024),
                            w_out.reshape(-1, 1024)], axis=0)
    return jnp.pad(rows, ((0, ROWS_REST - ROWS_USED), (0, 0)))


def _arranged_w_in(g_in):
    z = functools.partial(jnp.zeros, dtype=g_in.dtype)
    cut = 4480 - 2 * SHARD_ROWS
    return jnp.concatenate(
        [g_in[0, :SHARD_ROWS], g_in[1, :SHARD_ROWS], g_in[2, :cut], z((64, 1024)), g_in[2, cut:cut + 32],
         z((32, 1024)), g_in[2, cut + 32:SHARD_ROWS], g_in[3, :SHARD_ROWS]], axis=0)


def _rest_weights(g_rest):
    w_uq_t = g_rest[:, 0:ROWS_UQ].reshape(768, 256)
    w_uq_pad_t = jnp.pad(w_uq_t.reshape(MLA_HEADS, MLA_QK_DIM, 256), ((0, 0), (0, 32), (0, 0))).reshape(1024, 256)
    w_ukv = jnp.concatenate([g_rest[j, ROWS_UQ:ROWS_UQ + ROWS_UKV].reshape(128, 256) for j in range(4)], axis=1)
    lo = ROWS_UQ + ROWS_UKV
    w_mem = g_rest[:, lo:lo + ROWS_MEM].reshape(4 * ROWS_MEM, 1024)
    w_out = g_rest[:, lo + ROWS_MEM:lo + ROWS_MEM + ROWS_OUT].reshape(4 * ROWS_OUT, 1024)
    return w_uq_pad_t, w_ukv, w_mem, w_out


def _dw_in_split(dproj, h, tm=1024):
    t = dproj.shape[0]
    steps = PROJ_W // tm
    gap = ROWS_IN - SHARD_ROWS
    nat = 4608 - 96
    last = 4608 + 3 * SHARD_ROWS - nat
    segments = ((0, SHARD_ROWS, 0, 0), (SHARD_ROWS, 2 * SHARD_ROWS, 1, 0), (2 * SHARD_ROWS, 4480, 2, 0),
                (4544, 4576, 2, 4480 - 2 * SHARD_ROWS), (4608, last, 2, 4512 - 2 * SHARD_ROWS), (last, PROJ_W, 3, 0))

    def pieces(j):
        out = []
        for lo, hi, chip, dst in segments:
            a, b = max(lo, j * tm), min(hi, (j + 1) * tm)
            if a < b:
                out.append((a - j * tm, chip, dst + a - lo, b - a))
        return out

    n_sem = max(len(pieces(j)) for j in range(steps))

    def body(a_ref, b_ref, o_ref, tile_ref, zero_ref, sems, pad_sems):
        i = pl.program_id(0)

        def copies(j):
            return [pltpu.make_async_copy(tile_ref.at[j % 2, pl.ds(off, n)], o_ref.at[chip, pl.ds(dst, n)],
                                          sems.at[j % 2, q])
                    for q, (off, chip, dst, n) in enumerate(pieces(j))]

        def pad_copies():
            return [pltpu.make_async_copy(zero_ref, o_ref.at[chip, pl.ds(SHARD_ROWS, gap)], pad_sems.at[chip])
                    for chip in range(4)]

        @pl.when(i == 0)
        def _():
            zero_ref[...] = jnp.zeros_like(zero_ref)
            for c in pad_copies():
                c.start()

        for j in range(2, steps):
            @pl.when(i == j)
            def _(j=j):
                for c in copies(j - 2):
                    c.wait()

        tile_ref[i % 2] = _dot_tn(a_ref[...], b_ref[...])

        for j in range(steps):
            @pl.when(i == j)
            def _(j=j):
                for c in copies(j):
                    c.start()
                if j == steps - 1:
                    for c in copies(j - 1) + copies(j) + pad_copies():
                        c.wait()

    return pl.pallas_call(
        body, name="dw_in", grid=(steps,),
        out_shape=jax.ShapeDtypeStruct((4, ROWS_IN, 1024), F32),
        in_specs=[pl.BlockSpec((t, tm), lambda i: (0, i)), pl.BlockSpec((t, 1024), lambda i: (0, 0))],
        out_specs=ANY,
        scratch_shapes=[pltpu.VMEM((2, tm, 1024), F32), pltpu.VMEM((gap, 1024), F32),
                        pltpu.SemaphoreType.DMA((2, n_sem)), pltpu.SemaphoreType.DMA((4,))],
        compiler_params=_params(("arbitrary",)),
    )(dproj, h)


def _split_rest(dw_uq_pad_t, dw_ukv, dw_mem, dw_out):
    dw_uq_t = dw_uq_pad_t.reshape(MLA_HEADS, LANES, 256)[:, :MLA_QK_DIM].reshape(4, ROWS_UQ, 1024)
    parts = [dw_uq_t, dw_ukv.reshape(128, 4, 256).transpose(1, 0, 2).reshape(4, ROWS_UKV, 1024),
             dw_mem.reshape(4, ROWS_MEM, 1024), dw_out.reshape(4, ROWS_OUT, 1024)]
    return jnp.pad(jnp.concatenate(parts, axis=1), ((0, 0), (0, ROWS_REST - ROWS_USED), (0, 0)))


def _rope_consts(rot, first, period):
    half = rot // 2
    inv_freq = np.float32(ROPE_THETA) ** (-(np.arange(0, rot, 2, dtype=np.float32) / np.float32(rot)))
    lane = np.arange(LANES) % period - first
    in_rot = (lane >= 0) & (lane < rot)
    out = np.zeros((8, LANES), np.float32)
    out[0] = np.where(in_rot, inv_freq[np.clip(lane, 0, rot - 1) % half], 0.0)
    out[1] = in_rot & (lane < half)
    out[2] = in_rot & (lane >= half)
    return jnp.asarray(out)


def _band_bias(s):
    nblk = s // BAND_Q
    starts = np.array([_band_start(i, s) for i in range(nblk)])
    uq = (np.arange(nblk)[:, None] * BAND_Q + np.arange(BAND_Q)[None, :])[:, :, None]
    uk = (starts[:, None] + np.arange(BAND_WIN)[None, :])[:, None, :]
    tiles, index, seen = [], [], {}
    for _, d in DILATED:
        length = s // d
        ok = (uq // length == uk // length) & (np.abs(uq - uk) <= 64)
        row = []
        for i in range(nblk):
            key = ok[i].tobytes()
            if key not in seen:
                seen[key] = len(tiles)
                tiles.append(np.where(ok[i], 0.0, NEG_INF).astype(np.float32))
            row.append(seen[key])
        index.append(row)
    return jnp.asarray(np.stack(tiles, axis=0)), index


def _forward_backward(h, h32, proj, trig, rope_consts, x, mem, target, weights, gains):
    w_uq_pad_t, w_ukv, w_mem, w_out = weights
    g_emb, b_emb, g_cq, g_ckv, g_out_a, g_out_b, g_out_m, g_post, b_post = gains
    nb, s, d = x.shape
    t = nb * s
    x2 = x.reshape(t, d)
    mem2 = mem.reshape(nb * N_MEM, d)
    tgt2 = target.reshape(t, d)
    rope_a, rope_b = rope_consts
    bias, bias_index = _band_bias(s)
    scales = (0.125, MLA_QK_DIM ** -0.5, 128 ** -0.5)

    qa, ka, va, qb, kb, vb, qm = _prep(proj, trig, w_uq_pad_t, w_ukv, g_cq, g_ckv, rope_a, rope_b, scales)
    mkv = _mm(mem2, w_mem, BF16, nb * N_MEM, 1024, 1024, "mem_kv")

    cfg_b = dict(nb=nb, s=s, sk=s, heads=8, voff=0, bq=256)
    cfg_m = dict(nb=nb, s=s, sk=N_MEM, heads=4, hpb=2, voff=4, bq=1024)
    ya, lse_a, qkv_ordered = _dilated_fwd(qa, ka, va, bias, bias_index, nb=nb, s=s, name="attn_a_fwd")
    yb, lse_b = _attn_fwd(qb, kb, vb, name="attn_b_fwd", hpb=4, **cfg_b)
    ym, lse_m = _attn_fwd(qm, mkv, mkv, name="attn_m_fwd", **cfg_m)

    (y, dz, doa, dob, dom, dga, dgb, dgm, loss, dg_post, db_post, dg_a, dg_b, dg_m) = _post(
        h32, ya, yb, ym, proj, tgt2, w_out, g_out_a, g_out_b, g_out_m, g_post, b_post)

    dqa, dka, dva = _dilated_bwd(qa, ka, va, qkv_ordered, ya, doa, lse_a, bias, bias_index, nb=nb, s=s, scale=scales[0],
                                 name="attn_a_bwd")
    dqb, dkb, dvb = _attn_bwd(qb, kb, vb, yb, dob, lse_b, name="attn_b_bwd", scale=scales[1], hpb=4, **cfg_b)
    dqm, dmk, dmv = _attn_bwd(qm, mkv, mkv, ym, dom, lse_m, name="attn_m_bwd", scale=scales[2], **cfg_m)
    dmkv = jnp.concatenate([dmk, dmv], axis=1)

    dproj, dw_uq_pad_t, dw_ukv, dg_cq, dg_ckv = _prep_bwd(
        dqa, dka, dva, dqb, dkb, dvb, dqm, dga, dgb, dgm, proj, trig, w_uq_pad_t, w_ukv, g_cq, g_ckv, rope_a, rope_b)

    small_rows = (dg_cq, dg_ckv, loss, dg_a, dg_b, dg_m, dg_post, db_post)
    return (dproj, h, y, dz, dw_uq_pad_t, dw_ukv, mem2, dmkv), x2, small_rows


def _weight_grads(operands, core):
    dproj, h, y, dz, dw_uq_pad_t, dw_ukv, mem2, dmkv = operands
    g_in = _dw_in_split(dproj, h)
    dw_out, r_in = _mm(y, dz, F32, 1024, 1024, 2048, "dw_out", mode="tn",
                       ride=_half_to_sibling(g_in.reshape(4, 2, HALF_IN, 1024)))
    dw_mem = _mm(mem2, dmkv, F32, 1024, 1024, mem2.shape[0], "dw_mem", mode="tn")
    g_rest = _split_rest(dw_uq_pad_t, dw_ukv, dw_mem, dw_out)
    sf_in, sb_in, r_rest = _core_sum(g_in, r_in, core, HALF_IN, HALF_IN // 2, "core_sum_in",
                                     ride=_half_to_sibling(g_rest.reshape(4, 2, HALF_REST, 1024)))
    sf_rest, sb_rest = _core_sum(g_rest, r_rest, core, HALF_REST, HALF_REST, "core_sum_rest")
    return sf_in, sb_in, sf_rest, sb_rest


def _small_block(dg_emb, db_emb, small_rows):
    dg_cq, dg_ckv, loss, dg_a, dg_b, dg_m, dg_post, db_post = small_rows
    row2 = jnp.concatenate([dg_cq, dg_ckv, loss, jnp.zeros((1, 512), F32)], axis=1)
    return jnp.concatenate([dg_emb, db_emb, row2, dg_a, jnp.concatenate([dg_b, dg_m], axis=1), dg_post, db_post,
                            jnp.zeros((1, 1024), F32)], axis=0)


def _pack_small(g_emb, b_emb, g_cq, g_ckv, g_out_a, g_out_b, g_out_m, g_post, b_post):
    row2 = jnp.concatenate([g_cq.reshape(1, -1), g_ckv.reshape(1, -1), jnp.zeros((1, 640), F32)], axis=1)
    return jnp.concatenate([g_emb.reshape(1, -1), b_emb.reshape(1, -1), row2, g_out_a.reshape(1, -1),
                            jnp.concatenate([g_out_b.reshape(1, -1), g_out_m.reshape(1, -1)], axis=1),
                            g_post.reshape(1, -1), b_post.reshape(1, -1), jnp.zeros((1, 1024), F32)], axis=0)


def kernel(x, mem, positions, g_emb, b_emb, w_in, g_cq, g_ckv, w_uq, w_ukv, w_mem_kv, g_out_a, g_out_b, g_out_m, w_out, g_post, b_post, loss_target, m_g_emb, m_b_emb, m_w_in, m_g_cq, m_g_ckv, m_w_uq, m_w_ukv, m_w_mem_kv, m_g_out_a, m_g_out_b, m_g_out_m, m_w_out, m_g_post, m_b_post, v_g_emb, v_b_emb, v_w_in, v_g_cq, v_g_ckv, v_w_uq, v_w_ukv, v_w_mem_kv, v_g_out_a, v_g_out_b, v_g_out_m, v_w_out, v_g_post, v_b_post):
    w_rest = _pack_rest(w_uq, w_ukv, w_mem_kv, w_out)
    w_in_t = w_in[0].T
    w_in_b = jnp.pad(w_in_t.astype(BF16), ((0, ROWS_IN - SHARD_ROWS), (0, 0)))
    gains = (g_emb.reshape(1, -1), b_emb.reshape(1, -1), g_cq, g_ckv, g_out_a, g_out_b, g_out_m, g_post, b_post)
    rope_consts = (_rope_consts(16, 0, 64), _rope_consts(32, 64, 128))
    h, h32, trig, gathered_in = _ln_fwd(x.reshape(-1, D_MODEL), gains[0], gains[1],
                                        positions.reshape(-1, 1).astype(F32), *rope_consts,
                                        ride=_gather_ride(w_in_b.reshape(2, HALF_IN, 1024), spread=False))
    w_in_arr_t = _arranged_w_in(gathered_in.reshape(4, ROWS_IN, 1024))
    proj, gathered_rest = _mm(h, w_in_arr_t, F32, 1024, 2048, 1024, "in_proj", mode="nt",
                              ride=_gather_ride(w_rest.astype(BF16).reshape(2, HALF_REST, 1024), spread=True))
    weights = _rest_weights(gathered_rest.reshape(4, ROWS_REST, 1024))
    operands, x2, small_rows = _forward_backward(h, h32, proj, trig, rope_consts, x, mem, loss_target, weights,
                                                 gains)

    core = lax.axis_index("c").astype(jnp.int32).reshape(1)
    chip = (2 * lax.axis_index("x") + lax.axis_index("y")).astype(jnp.int32).reshape(1)
    sf_in, sb_in, sf_rest, sb_rest = _weight_grads(operands, core)
    grad_x, dg_emb, db_emb, rb_in, rb_rest = _dh_scatter(operands[0], w_in_arr_t, x2, operands[3], gains[0],
                                                         sb_in, sb_rest)
    gh_in = _chip_sum(sf_in, rb_in, chip, HALF_IN, HALF_IN // 2, "chip_sum_in")
    gh_rest = _chip_sum(sf_rest, rb_rest, chip, HALF_REST, HALF_REST, "chip_sum_rest")
    grad_in, grad_rest, small_sum = _join_and_allreduce(gh_in, gh_rest, _small_block(dg_emb, db_emb, small_rows))
    grad_in = grad_in.reshape(ROWS_IN, 1024)
    grad_rest = grad_rest.reshape(ROWS_REST, 1024)

    big_in = _adamw(grad_in, w_in_t, m_w_in[0].T, v_w_in[0].T, SHARD_ROWS // 3, "adamw_in")
    def rest_parts(a_uq, a_ukv, a_mem, a_out):
        return [a_uq[0].T.reshape(ROWS_UQ, 1024), a_ukv.reshape(ROWS_UKV, 1024), a_mem[0], a_out[0]]

    uq, ukv, wmem, wout = _adamw_pieces(
        grad_rest, rest_parts(w_uq, w_ukv, w_mem_kv, w_out), rest_parts(m_w_uq, m_w_ukv, m_w_mem_kv, m_w_out),
        rest_parts(v_w_uq, v_w_ukv, v_w_mem_kv, v_w_out), REST_PIECES, "adamw_rest")
    sm = _adamw_pieces(
        small_sum,
        _pack_small(g_emb, b_emb, g_cq, g_ckv, g_out_a, g_out_b, g_out_m, g_post, b_post),
        _pack_small(m_g_emb, m_b_emb, m_g_cq, m_g_ckv, m_g_out_a, m_g_out_b, m_g_out_m, m_g_post, m_b_post),
        _pack_small(v_g_emb, v_b_emb, v_g_cq, v_g_ckv, v_g_out_a, v_g_out_b, v_g_out_m, v_g_post, v_b_post),
        SMALL_PIECES, "adamw_small")
    loss = small_sum[2, 384]

    def ordered(kind):
        s_gemb, s_bemb, s_gcq, s_gckv, s_ga, s_gb, s_gm, s_gpost, s_bpost = [piece[kind] for piece in sm]
        return [s_gemb.reshape(-1), s_bemb.reshape(-1), big_in[kind].T[None], s_gcq, s_gckv,
                uq[kind].reshape(192, 256).T[None], ukv[kind].reshape(1, 128, 256), wmem[kind][None], s_ga, s_gb,
                s_gm, wout[kind][None], s_gpost, s_bpost]

    return (loss, grad_x.reshape(x.shape), *ordered(0), *ordered(1), *ordered(2), *ordered(3))
```

```python
import functools
import math

import jax
import jax.numpy as jnp
import numpy as np
from jax import lax
from jax.experimental import pallas as pl
from jax.experimental.pallas import tpu as pltpu

F32 = jnp.float32
BF16 = jnp.bfloat16
MESH = pl.DeviceIdType.MESH
ANY = pl.BlockSpec(memory_space=pl.ANY)
IN_VMEM = pl.BlockSpec(memory_space=pltpu.VMEM)

D_MODEL = 1024
A_WIDTH = 1024
MLA_HEADS = 8
MLA_Q_RANK = 256
MLA_KV_RANK = 128
MLA_QK_DIM = 96
MEM_WIDTH = 512
N_MEM = 256
ROPE_THETA = 500000.0
NORM_EPS = 1e-5
NEG_INF = -1e30
DEEPNORM_ALPHA = 2.0 ** 0.25
DILATED = ((64, 1), (256, 4), (1024, 16))

ADAM_LR = 0.001
ADAM_B1 = 0.9
ADAM_B2 = 0.999
ADAM_EPS = 1e-08
ADAM_WD = 0.01
ADAM_STEP = 10

LANES = 128
VMEM_LIMIT = 56 * 1024 * 1024
LOG2E = math.log2(math.e)
LN2 = math.log(2.0)

PROJ_W = 6144
COL_CQ = 4096
COL_BG = 4608
COL_MQ = 5120
COL_MG = 5632

SHARD_ROWS = 1512
ROWS_IN = 1536
ROWS_UQ, ROWS_UKV, ROWS_MEM, ROWS_OUT = 48, 32, 256, 512
ROWS_USED = ROWS_UQ + ROWS_UKV + ROWS_MEM + ROWS_OUT
ROWS_REST = 864
HALF_IN = ROWS_IN // 2
HALF_REST = ROWS_REST // 2
REST_PIECES = ((0, 48, 0, 1024), (48, 80, 0, 1024), (80, 336, 0, 1024), (336, 848, 0, 1024))
SMALL_PIECES = ((0, 1, 0, 1024), (1, 2, 0, 1024), (2, 3, 0, 256), (2, 3, 256, 384), (3, 4, 0, 1024), (4, 5, 0, 512),
                (4, 5, 512, 1024), (5, 6, 0, 1024), (6, 7, 0, 1024))


def _params(sem=None, vmem=VMEM_LIMIT):
    return pltpu.CompilerParams(dimension_semantics=sem, vmem_limit_bytes=vmem)


def _dot(a, b):
    return jnp.dot(a, b, preferred_element_type=F32)


def _dot_nt(a, b):
    return lax.dot_general(a, b, (((1,), (1,)), ((), ())), preferred_element_type=F32)


def _dot_tn(a, b):
    return lax.dot_general(a, b, (((0,), (0,)), ((), ())), preferred_element_type=F32)


def _ln_hat(x):
    mu = jnp.mean(x, axis=-1, keepdims=True)
    xc = x - mu
    var = jnp.mean(xc * xc, axis=-1, keepdims=True)
    rstd = lax.rsqrt(var + NORM_EPS)
    return xc * rstd, rstd


def _ln_bwd_rows(dxh, xh, rstd):
    return rstd * (dxh - jnp.mean(dxh, axis=-1, keepdims=True) - xh * jnp.mean(dxh * xh, axis=-1, keepdims=True))


def _rms_hat(x, width):
    ms = jnp.sum(x * x, axis=-1, keepdims=True) * (1.0 / width)
    r = lax.rsqrt(ms + NORM_EPS)
    return x * r, r


def _rms_bwd(u, xh, r, width):
    return r * (u - xh * (jnp.sum(u * xh, axis=-1, keepdims=True) * (1.0 / width)))


def _colsum(v):
    return jnp.sum(v, axis=0, keepdims=True)


def _rope_tables(cos, sin, consts):
    return cos, sin * consts[2:3, :], -sin * consts[1:2, :]


def _rope(x, tables, half, inverse=False):
    c, s_up, s_dn = tables
    if inverse:
        s_up, s_dn = -s_up, -s_dn
    return x * c + pltpu.roll(x, half, 1) * s_up + pltpu.roll(x, LANES - half, 1) * s_dn


def _ln_fwd(x, g, b, pos, rope_a, rope_b, tm=512, ride=None):
    t, d = x.shape
    n_in = len(ride.args) if ride else 0
    n_out = len(ride.out_shapes) if ride else 0
    steps = t // tm

    def body(x_ref, g_ref, b_ref, pos_ref, ra_ref, rb_ref, *rest):
        h_ref, h32_ref, trig_ref = rest[n_in:n_in + 3]
        if ride:
            i = pl.program_id(0)
            ride.run(i, steps, rest[:n_in], rest[n_in + 3:n_in + 3 + n_out], rest[n_in + 3 + n_out:])
        xh, _ = _ln_hat(x_ref[...])
        h = xh * g_ref[...] + b_ref[...]
        h32_ref[...] = h
        h_ref[...] = h.astype(BF16)
        for j, consts in enumerate((ra_ref, rb_ref)):
            ang = pos_ref[...] * consts[0:1, :]
            trig_ref[:, 2 * j * LANES:(2 * j + 1) * LANES] = jnp.cos(ang)
            trig_ref[:, (2 * j + 1) * LANES:(2 * j + 2) * LANES] = jnp.sin(ang)

    row = pl.BlockSpec((1, d), lambda i: (0, 0))
    tile = pl.BlockSpec((tm, d), lambda i: (i, 0))
    consts = pl.BlockSpec((8, LANES), lambda i: (0, 0))
    trig_tile = pl.BlockSpec((tm, 4 * LANES), lambda i: (i, 0))
    in_specs = [tile, row, row, pl.BlockSpec((tm, 1), lambda i: (i, 0)), consts, consts]
    shapes = (jax.ShapeDtypeStruct((t, d), BF16), jax.ShapeDtypeStruct((t, d), F32),
              jax.ShapeDtypeStruct((t, 4 * LANES), F32))
    if not ride:
        return pl.pallas_call(
            body, name="ln_fwd", grid=(steps,), out_shape=shapes, in_specs=in_specs,
            out_specs=(tile, tile, trig_tile), compiler_params=_params(("parallel",)),
        )(x, g, b, pos, rope_a, rope_b)
    return pl.pallas_call(
        body, name="ln_fwd", grid=(steps,),
        out_shape=(*shapes, *ride.out_shapes),
        in_specs=in_specs + ride.in_specs, out_specs=(tile, tile, trig_tile) + (ANY,) * n_out,
        scratch_shapes=ride.scratch(),
        compiler_params=_params(("arbitrary",)),
    )(x, g, b, pos, rope_a, rope_b, *ride.args)


class _Ride:
    def __init__(self, args, out_shapes, sem_counts, plan, in_specs=None, spread=True):
        self.args, self.out_shapes, self.plan = list(args), list(out_shapes), plan
        self.sem_counts = sem_counts
        self.in_specs = in_specs or [ANY] * len(self.args)
        self.spread = spread

    def scratch(self):
        return [pltpu.SemaphoreType.DMA((n,)) for n in self.sem_counts]

    def run(self, step, total, in_refs, out_refs, sems):
        count = len(self.plan(in_refs, out_refs, *sems))
        at = [(k * (total - 1)) // (count - 1) if self.spread or k == 0 else total - 1 for k in range(count)]
        for when in sorted(set(at)):
            @pl.when(step == when)
            def _(when=when):
                stages = self.plan(in_refs, out_refs, *sems)
                for k in range(count):
                    if at[k] == when:
                        stages[k]()


def _mm(a, b, out_dtype, tm, tn, tk, name, mode="nn", ride=None):
    if mode == "tn":
        k, m = a.shape
    else:
        m, k = a.shape
    n = b.shape[0] if mode == "nt" else b.shape[1]
    nk = k // tk
    nj, ni = n // tn, m // tm
    n_in = len(ride.args) if ride else 0
    n_out = len(ride.out_shapes) if ride else 0

    def body(a_ref, b_ref, *rest):
        o_ref = rest[n_in]
        acc_ref = rest[n_in + 1 + n_out]
        if ride:
            j, i, kk = pl.program_id(0), pl.program_id(1), pl.program_id(2)
            ride.run((j * ni + i) * nk + kk, nj * ni * nk, rest[:n_in], rest[n_in + 1:n_in + 1 + n_out],
                     rest[n_in + 2 + n_out:])
        av = a_ref[...].astype(BF16)
        bv = b_ref[...].astype(BF16)
        part = _dot_tn(av, bv) if mode == "tn" else _dot_nt(av, bv) if mode == "nt" else _dot(av, bv)
        if nk == 1:
            o_ref[...] = part.astype(out_dtype)
        else:
            kk = pl.program_id(2)

            @pl.when(kk == 0)
            def _():
                acc_ref[...] = part

            @pl.when(kk > 0)
            def _():
                acc_ref[...] += part

            @pl.when(kk == nk - 1)
            def _():
                o_ref[...] = acc_ref[...].astype(out_dtype)

    a_spec = (pl.BlockSpec((tk, tm), lambda j, i, kk: (kk, i)) if mode == "tn"
              else pl.BlockSpec((tm, tk), lambda j, i, kk: (i, kk)))
    b_spec = (pl.BlockSpec((tn, tk), lambda j, i, kk: (j, kk)) if mode == "nt"
              else pl.BlockSpec((tk, tn), lambda j, i, kk: (kk, j)))
    o_spec = pl.BlockSpec((tm, tn), lambda j, i, kk: (i, j))
    o_shape = jax.ShapeDtypeStruct((m, n), out_dtype)
    if not ride:
        return pl.pallas_call(
            body, name=name, grid=(nj, ni, nk), out_shape=o_shape, in_specs=[a_spec, b_spec], out_specs=o_spec,
            scratch_shapes=[pltpu.VMEM((tm, tn), F32)],
            compiler_params=_params(("parallel", "parallel", "arbitrary")),
        )(a, b)
    return pl.pallas_call(
        body, name=name, grid=(nj, ni, nk),
        out_shape=(o_shape, *ride.out_shapes),
        in_specs=[a_spec, b_spec] + ride.in_specs,
        out_specs=(o_spec,) + (ANY,) * n_out,
        scratch_shapes=[pltpu.VMEM((tm, tn), F32)] + ride.scratch(),
        compiler_params=_params(("arbitrary", "arbitrary", "arbitrary")),
    )(a, b, *ride.args)


def _prep(proj, trig, w_uq, w_ukv, g_cq, g_ckv, rope_a, rope_b, scales, tm=512):
    t = proj.shape[0]
    sc_a, sc_b, sc_m = (s * LOG2E for s in scales)

    def body(aq_ref, ak_ref, av_ref, bs_ref, mq_ref, trig_ref, wuq_ref, wukv_ref, gcq_ref, gckv_ref,
             ra_ref, rb_ref, qa_ref, ka_ref, va_ref, qb_ref, kb_ref, vb_ref, qm_ref):
        ta = _rope_tables(trig_ref[:, 0:LANES], trig_ref[:, LANES:2 * LANES], ra_ref[...])
        tb = _rope_tables(trig_ref[:, 2 * LANES:3 * LANES], trig_ref[:, 3 * LANES:4 * LANES], rb_ref[...])
        for j in range(A_WIDTH // LANES):
            sl = slice(j * LANES, (j + 1) * LANES)
            qa_ref[:, sl] = (_rope(aq_ref[:, sl], ta, 8) * sc_a).astype(BF16)
            ka_ref[:, sl] = _rope(ak_ref[:, sl], ta, 8).astype(BF16)
        va_ref[...] = av_ref[...].astype(BF16)
        qm_ref[...] = (mq_ref[...] * sc_m).astype(BF16)

        cq_hat, _ = _rms_hat(bs_ref[:, 0:MLA_Q_RANK], MLA_Q_RANK)
        cqn = (cq_hat * gcq_ref[...]).astype(BF16)
        ckv_hat, _ = _rms_hat(bs_ref[:, MLA_Q_RANK:MLA_Q_RANK + MLA_KV_RANK], MLA_KV_RANK)
        ckvn = (ckv_hat * gckv_ref[...]).astype(BF16)
        qfull = _dot_nt(cqn, wuq_ref[...])
        kv = _dot(ckvn, wukv_ref[...])
        kr = _rope(bs_ref[:, 384:512], tb, 16)
        lane = lax.broadcasted_iota(jnp.int32, (1, LANES), 1)
        low = lane < 64
        for h in range(MLA_HEADS):
            sl = slice(h * LANES, (h + 1) * LANES)
            qb_ref[:, sl] = (_rope(qfull[:, sl], tb, 16) * sc_b).astype(BF16)
            kb_ref[:, sl] = jnp.where(low, kv[:, sl], kr).astype(BF16)
            vb_ref[:, sl] = jnp.where(low, 0.0, kv[:, sl]).astype(BF16)

    def col(width, idx):
        return pl.BlockSpec((tm, width), lambda i: (i, idx))

    def full(shape):
        return pl.BlockSpec(shape, lambda i: (0, 0))

    wide = jax.ShapeDtypeStruct((t, 1024), BF16)
    return pl.pallas_call(
        body, name="prep", grid=(t // tm,),
        out_shape=(wide, wide, wide, wide, wide, wide,
                   jax.ShapeDtypeStruct((t, MEM_WIDTH), BF16)),
        in_specs=[col(1024, 0), col(1024, 1), col(1024, 2), col(512, COL_CQ // 512), col(512, COL_MQ // 512),
                  pl.BlockSpec((tm, 4 * LANES), lambda i: (i, 0)),
                  full((1024, MLA_Q_RANK)), full((MLA_KV_RANK, 1024)),
                  full((1, MLA_Q_RANK)), full((1, MLA_KV_RANK)), full((8, LANES)), full((8, LANES))],
        out_specs=(col(1024, 0),) * 6 + (col(MEM_WIDTH, 0),),
        compiler_params=_params(("parallel",)),
    )(proj, proj, proj, proj, proj, trig, w_uq, w_ukv, g_cq, g_ckv, rope_a, rope_b)


def _attn_fwd(q, k, v, *, nb, s, sk, heads, hpb, voff, bq, name):
    nq = s // bq
    width = hpb * LANES
    vblk = voff // hpb

    def body(q_ref, k_ref, v_ref, o_ref, lse_ref):
        for h in range(hpb):
            sl = slice(h * LANES, (h + 1) * LANES)
            sc = _dot_nt(q_ref[:, sl], k_ref[:, sl])
            m = jnp.max(sc, axis=1, keepdims=True)
            p = jnp.exp2(sc - m)
            l = jnp.sum(p, axis=1, keepdims=True)
            o_ref[:, sl] = _dot(p.astype(BF16), v_ref[:, sl]) / l
            lse_ref[:, sl] = jnp.broadcast_to(m + jnp.log(l) * LOG2E, (bq, LANES))

    out = jax.ShapeDtypeStruct((nb * s, heads * LANES), F32)
    ospec = pl.BlockSpec((bq, width), lambda b, i, g: (b * nq + i, g))
    return pl.pallas_call(
        body, name=name, grid=(nb, nq, heads // hpb),
        out_shape=(out, out),
        in_specs=[ospec, pl.BlockSpec((sk, width), lambda b, i, g: (b, g)),
                  pl.BlockSpec((sk, width), lambda b, i, g: (b, vblk + g))],
        out_specs=(ospec, ospec),
        compiler_params=_params(("parallel", "parallel", "parallel")),
    )(q, k, v)


def _attn_bwd(q, k, v, o, do, lse, *, nb, s, sk, heads, hpb, voff, scale, bq, name):
    nq = s // bq
    width = hpb * LANES
    vblk = voff // hpb

    def body(q_ref, k_ref, v_ref, o_ref, do_ref, lse_ref, dq_ref, dk_ref, dv_ref, dk_acc, dv_acc):
        i = pl.program_id(2)

        @pl.when(i == 0)
        def _():
            dk_acc[...] = jnp.zeros_like(dk_acc)
            dv_acc[...] = jnp.zeros_like(dv_acc)

        for h in range(hpb):
            sl = slice(h * LANES, (h + 1) * LANES)
            qh = q_ref[:, sl]
            kk = k_ref[:, sl]
            doh = do_ref[:, sl]
            delta = jnp.sum(doh.astype(F32) * o_ref[:, sl], axis=1, keepdims=True)
            p = jnp.exp2(_dot_nt(qh, kk) - lse_ref[:, h * LANES:h * LANES + 1])
            ds = (p * (_dot_nt(doh, v_ref[:, sl]) - delta)).astype(BF16)
            dq_ref[:, sl] = (_dot(ds, kk) * scale).astype(BF16)
            dk_acc[:, sl] += _dot_tn(ds, qh)
            dv_acc[:, sl] += _dot_tn(p.astype(BF16), doh)

        @pl.when(i == nq - 1)
        def _():
            dk_ref[...] = (dk_acc[...] * LN2).astype(BF16)
            dv_ref[...] = dv_acc[...].astype(BF16)

    qspec = pl.BlockSpec((bq, width), lambda b, g, i: (b * nq + i, g))
    kv_spec = pl.BlockSpec((sk, width), lambda b, g, i: (b, g))
    dq_shape = jax.ShapeDtypeStruct((nb * s, heads * LANES), BF16)
    dkv_shape = jax.ShapeDtypeStruct((nb * sk, heads * LANES), BF16)
    return pl.pallas_call(
        body, name=name, grid=(nb, heads // hpb, nq),
        out_shape=(dq_shape, dkv_shape, dkv_shape),
        in_specs=[qspec, kv_spec, pl.BlockSpec((sk, width), lambda b, g, i: (b, vblk + g)), qspec, qspec, qspec],
        out_specs=(qspec, kv_spec, kv_spec),
        scratch_shapes=[pltpu.VMEM((sk, width), F32), pltpu.VMEM((sk, width), F32)],
        compiler_params=_params(("parallel", "parallel", "arbitrary")),
    )(q, k, v, o, do, lse)


BAND_Q = 128
BAND_WIN = 256


def _band_start(i, s):
    return min(max(i * BAND_Q - 64, 0), s - BAND_WIN)


def _to_pattern_order(src_ref, dst_ref, stage_ref, s, d):
    length = s // d
    stage_ref[...] = src_ref[...].astype(F32)
    for r in range(d):
        dst_ref[r * length:(r + 1) * length, :] = stage_ref[pl.ds(r, length, stride=d), :].astype(dst_ref.dtype)


def _dilated_fwd(q, k, v, bias, bias_index, *, nb, s, name):
    nblk = s // BAND_Q
    npat = len(DILATED)

    def body(q_ref, k_ref, v_ref, bias_ref, o_ref, lse_ref, *rest):
        ordered = rest[:3 * (npat - 1)]
        stage_ref, op_ref, lp_ref, on_ref, ln_ref = rest[3 * (npat - 1):]
        lane = lax.broadcasted_iota(jnp.int32, (1, LANES), 1)
        first = lane < 64
        for p, (_, d) in enumerate(DILATED):
            if d == 1:
                qs, ks, vs = q_ref, k_ref, v_ref
            else:
                qs, ks, vs = ordered[3 * (p - 1):3 * p]
                for src, dst in ((q_ref, qs), (k_ref, ks), (v_ref, vs)):
                    _to_pattern_order(src, dst, stage_ref, s, d)
            for i in range(nblk):
                u0 = i * BAND_Q
                st = _band_start(i, s)
                qi = qs[u0:u0 + BAND_Q, :]
                kw = ks[st:st + BAND_WIN, :]
                vw = vs[st:st + BAND_WIN, :]
                zero = jnp.zeros_like(qi)
                q2 = jnp.concatenate([jnp.where(first, qi, zero), jnp.where(first, zero, qi)], axis=0)
                sc = _dot_nt(q2, kw)
                b = bias_ref[bias_index[p][i]]
                halves = []
                for h in range(2):
                    sh = sc[h * BAND_Q:(h + 1) * BAND_Q] + b
                    m = jnp.max(sh, axis=1, keepdims=True)
                    pr = jnp.exp2(sh - m)
                    l = jnp.sum(pr, axis=1, keepdims=True)
                    halves.append((pr.astype(BF16), l, m + jnp.log(l) * LOG2E))
                o2 = _dot(jnp.concatenate([halves[0][0], halves[1][0]], axis=0), vw)
                o_blk = jnp.where(first, o2[:BAND_Q] / halves[0][1], o2[BAND_Q:] / halves[1][1])
                lse_blk = jnp.where(first, jnp.broadcast_to(halves[0][2], (BAND_Q, LANES)),
                                    jnp.broadcast_to(halves[1][2], (BAND_Q, LANES)))
                op_ref[p, u0:u0 + BAND_Q, :] = o_blk
                lp_ref[p, u0:u0 + BAND_Q, :] = lse_blk
            if d > 1:
                length = s // d
                for r in range(d):
                    on_ref.at[p - 1][pl.ds(r, length, stride=d), :] = op_ref[p, r * length:(r + 1) * length, :]
                    ln_ref.at[p - 1][pl.ds(r, length, stride=d), :] = lp_ref[p, r * length:(r + 1) * length, :]
        lses = [lp_ref[0]] + [ln_ref[p] for p in range(npat - 1)]
        outs = [op_ref[0]] + [on_ref[p] for p in range(npat - 1)]
        m = functools.reduce(jnp.maximum, lses)
        ws = [jnp.exp2(l - m) for l in lses]
        den = functools.reduce(lambda a, c: a + c, ws)
        o_ref[...] = functools.reduce(lambda a, c: a + c, [w * o for w, o in zip(ws, outs)]) / den
        lse_ref[...] = m + jnp.log(den) * LOG2E

    blk = pl.BlockSpec((s, LANES), lambda b, g: (b, g))
    out = jax.ShapeDtypeStruct((nb * s, A_WIDTH), F32)
    copy = jax.ShapeDtypeStruct((nb * s, A_WIDTH), BF16)
    n_copies = 3 * (npat - 1)
    res = pl.pallas_call(
        body, name=name, grid=(nb, A_WIDTH // LANES),
        out_shape=(out, out) + (copy,) * n_copies,
        in_specs=[blk, blk, blk, pl.BlockSpec(bias.shape, lambda b, g: (0, 0, 0))],
        out_specs=(blk, blk) + (blk,) * n_copies,
        scratch_shapes=[pltpu.VMEM((s, LANES), F32), pltpu.VMEM((npat, s, LANES), F32),
                        pltpu.VMEM((npat, s, LANES), F32), pltpu.VMEM((npat - 1, s, LANES), F32),
                        pltpu.VMEM((npat - 1, s, LANES), F32)],
        compiler_params=_params(("parallel", "parallel")),
    )(q, k, v, bias)
    return res[0], res[1], res[2:]


def _dilated_bwd(q, k, v, ordered, o, do, lse, bias, bias_index, *, nb, s, scale, name):
    nblk = s // BAND_Q
    npat = len(DILATED)
    n_copies = 3 * (npat - 1)

    def body(q_ref, k_ref, v_ref, *rest):
        ordered_refs = rest[:n_copies]
        (o_ref, do_ref, lse_ref, bias_ref, dq_out, dk_out, dv_out, stage_ref, rs_ref, dop_ref, rsp_ref,
         dqp_ref, dkp_ref, dvp_ref, dq_ref, dk_ref, dv_ref, nat_ref) = rest[n_copies:]
        lane = lax.broadcasted_iota(jnp.int32, (1, LANES), 1)
        first = lane < 64
        prod = do_ref[...].astype(F32) * o_ref[...]
        d0 = jnp.sum(jnp.where(first, prod, 0.0), axis=1, keepdims=True)
        d1 = jnp.sum(jnp.where(first, 0.0, prod), axis=1, keepdims=True)
        delta = jnp.where(first, jnp.broadcast_to(d0, (s, LANES)), jnp.broadcast_to(d1, (s, LANES)))
        rs_ref[...] = jnp.where((lane & 32) == 0, lse_ref[...], delta)
        for p, (_, d) in enumerate(DILATED):
            length = s // d
            if d == 1:
                qs, ks, vs, dos, rss = q_ref, k_ref, v_ref, do_ref, rs_ref
                dqs, dks, dvs = dq_ref, dk_ref, dv_ref
            else:
                for src, dst in ((do_ref, dop_ref), (rs_ref, rsp_ref)):
                    _to_pattern_order(src, dst, stage_ref, s, d)
                qs, ks, vs = ordered_refs[3 * (p - 1):3 * p]
                dos, rss = dop_ref, rsp_ref
                dqs, dks, dvs = dqp_ref, dkp_ref, dvp_ref
            dks[...] = jnp.zeros((s, LANES), F32)
            dvs[...] = jnp.zeros((s, LANES), F32)
            for i in range(nblk):
                u0 = i * BAND_Q
                st = _band_start(i, s)
                qi = qs[u0:u0 + BAND_Q, :]
                doi = dos[u0:u0 + BAND_Q, :]
                kw = ks[st:st + BAND_WIN, :]
                vw = vs[st:st + BAND_WIN, :]
                zero = jnp.zeros_like(qi)
                q2 = jnp.concatenate([jnp.where(first, qi, zero), jnp.where(first, zero, qi)], axis=0)
                do2 = jnp.concatenate([jnp.where(first, doi, zero), jnp.where(first, zero, doi)], axis=0)
                sc = _dot_nt(q2, kw)
                dp = _dot_nt(do2, vw)
                b = bias_ref[bias_index[p][i]]
                rs_i = rss[u0:u0 + BAND_Q, :]
                ps, dss = [], []
                for h in range(2):
                    rows = slice(h * BAND_Q, (h + 1) * BAND_Q)
                    pr = jnp.exp2(sc[rows] + b - rs_i[:, 64 * h:64 * h + 1])
                    ps.append(pr.astype(BF16))
                    dss.append((pr * (dp[rows] - rs_i[:, 64 * h + 32:64 * h + 33])).astype(BF16))
                p2 = jnp.concatenate(ps, axis=0)
                ds2 = jnp.concatenate(dss, axis=0)
                dq2 = _dot(ds2, kw)
                dqs[u0:u0 + BAND_Q, :] = jnp.where(first, dq2[:BAND_Q], dq2[BAND_Q:]) * scale
                dks[st:st + BAND_WIN, :] += _dot_tn(ds2, q2)
                dvs[st:st + BAND_WIN, :] += _dot_tn(p2, do2)
            if d > 1:
                for j, src in enumerate((dqp_ref, dkp_ref, dvp_ref)):
                    for r in range(d):
                        nat_ref.at[p - 1, j][pl.ds(r, length, stride=d), :] = src[r * length:(r + 1) * length, :]

        def total(j, first_ref):
            return functools.reduce(lambda a, c: a + c, [first_ref[...]] + [nat_ref[p, j] for p in range(npat - 1)])

        dq_out[...] = total(0, dq_ref).astype(BF16)
        dk_out[...] = (total(1, dk_ref) * LN2).astype(BF16)
        dv_out[...] = total(2, dv_ref).astype(BF16)

    blk = pl.BlockSpec((s, LANES), lambda b, g: (b, g))
    out = jax.ShapeDtypeStruct((nb * s, A_WIDTH), BF16)
    f32_buf = pltpu.VMEM((s, LANES), F32)
    bf_buf = pltpu.VMEM((s, LANES), BF16)
    return pl.pallas_call(
        body, name=name, grid=(nb, A_WIDTH // LANES),
        out_shape=(out, out, out),
        in_specs=[blk] * (6 + n_copies) + [pl.BlockSpec(bias.shape, lambda b, g: (0, 0, 0))],
        out_specs=(blk, blk, blk),
        scratch_shapes=[f32_buf, f32_buf, bf_buf] + [f32_buf] * 7 + [pltpu.VMEM((npat - 1, 3, s, LANES), F32)],
        compiler_params=_params(("parallel", "parallel")),
    )(q, k, v, *ordered, o, do, lse, bias)


def _post(h32, ya, ybp, ym, proj, target, w_out, g_a, g_b, g_m, g_post, b_post, tm=256):
    t = h32.shape[0]

    def body(h_ref, ya_ref, yb_ref, ym_ref, ga_ref, gb_ref, gm_ref, tg_ref, wo_ref,
             goa_ref, gob_ref, gom_ref, gp_ref, bp_ref,
             y_ref, dz_ref, doa_ref, dob_ref, dom_ref, dga_ref, dgb_ref, dgm_ref,
             loss_ref, dgp_ref, dbp_ref, dgoa_ref, dgob_ref, dgom_ref):
        i = pl.program_id(0)

        @pl.when(i == 0)
        def _():
            for r in (loss_ref, dgp_ref, dbp_ref, dgoa_ref, dgob_ref, dgom_ref):
                r[...] = jnp.zeros_like(r)

        lane = lax.broadcasted_iota(jnp.int32, (1, LANES), 1)
        low = lane < 64
        h = h_ref[...]

        ybp_v = yb_ref[...]
        yb = jnp.concatenate(
            [jnp.where(low, pltpu.roll(ybp_v[:, 2 * j * LANES:(2 * j + 1) * LANES], 64, 1),
                       ybp_v[:, (2 * j + 1) * LANES:(2 * j + 2) * LANES]) for j in range(4)], axis=1)

        def gated(raw, gate, gain, width):
            xh, r = _rms_hat(raw, width)
            n = xh * gain
            sg = 1.0 / (1.0 + jnp.exp(-gate))
            return xh, r, n, sg, n * (gate * sg)

        gate_a, gate_b, gate_m = ga_ref[...], gb_ref[...], gm_ref[...]
        xh_a, r_a, n_a, sg_a, y_a = gated(ya_ref[...], gate_a, goa_ref[...], A_WIDTH)
        xh_b, r_b, n_b, sg_b, y_b = gated(yb, gate_b, gob_ref[...], 512)
        xh_m, r_m, n_m, sg_m, y_m = gated(ym_ref[...], gate_m, gom_ref[...], 512)
        y = jnp.concatenate([y_a, y_b, y_m], axis=1).astype(BF16)
        y_ref[...] = y
        z = DEEPNORM_ALPHA * h + _dot(y, wo_ref[...])
        zh, rstd = _ln_hat(z)
        err = zh * gp_ref[...] + bp_ref[...] - tg_ref[...]
        rows = jnp.sum(err * err, axis=1, keepdims=True)
        loss_ref[...] += jnp.broadcast_to(jnp.sum(rows, axis=0, keepdims=True) * (0.5 / D_MODEL), (1, LANES))
        dout = err * (1.0 / D_MODEL)
        dgp_ref[...] += _colsum(dout * zh)
        dbp_ref[...] += _colsum(dout)
        dz = _ln_bwd_rows(dout * gp_ref[...], zh, rstd)
        dz_ref[...] = dz
        dy = _dot_nt(dz.astype(BF16), wo_ref[...])

        def gated_bwd(dyg, xh, r, n, sg, gate, gain, width, dgain_ref):
            dn = dyg * (gate * sg)
            dgate = dyg * n * (sg * (1.0 + gate * (1.0 - sg)))
            dgain_ref[...] += _colsum(dn * xh)
            return _rms_bwd(dn * gain, xh, r, width), dgate

        dya, dgate_a = gated_bwd(dy[:, 0:1024], xh_a, r_a, n_a, sg_a, gate_a, goa_ref[...], A_WIDTH, dgoa_ref)
        dyb, dgate_b = gated_bwd(dy[:, 1024:1536], xh_b, r_b, n_b, sg_b, gate_b, gob_ref[...], 512, dgob_ref)
        dym, dgate_m = gated_bwd(dy[:, 1536:2048], xh_m, r_m, n_m, sg_m, gate_m, gom_ref[...], 512, dgom_ref)
        doa_ref[...] = dya.astype(BF16)
        dom_ref[...] = dym.astype(BF16)
        dga_ref[...] = dgate_a.astype(BF16)
        dgb_ref[...] = dgate_b.astype(BF16)
        dgm_ref[...] = dgate_m.astype(BF16)
        for j in range(4):
            blk = dyb[:, j * LANES:(j + 1) * LANES]
            dob_ref[:, 2 * j * LANES:(2 * j + 1) * LANES] = jnp.where(low, 0.0, pltpu.roll(blk, 64, 1)).astype(BF16)
            dob_ref[:, (2 * j + 1) * LANES:(2 * j + 2) * LANES] = jnp.where(low, 0.0, blk).astype(BF16)

    def col(width, idx):
        return pl.BlockSpec((tm, width), lambda i: (i, idx))

    def full(shape):
        return pl.BlockSpec(shape, lambda i: (0, 0))

    def acc(width):
        return jax.ShapeDtypeStruct((1, width), F32)

    return pl.pallas_call(
        body, name="post", grid=(t // tm,),
        out_shape=(jax.ShapeDtypeStruct((t, 2048), BF16), jax.ShapeDtypeStruct((t, 1024), F32),
                   jax.ShapeDtypeStruct((t, 1024), BF16), jax.ShapeDtypeStruct((t, 1024), BF16),
                   jax.ShapeDtypeStruct((t, 512), BF16),
                   jax.ShapeDtypeStruct((t, 1024), BF16), jax.ShapeDtypeStruct((t, 512), BF16),
                   jax.ShapeDtypeStruct((t, 512), BF16),
                   acc(LANES), acc(1024), acc(1024), acc(1024), acc(512), acc(512)),
        in_specs=[col(1024, 0), col(1024, 0), col(1024, 0), col(512, 0),
                  col(1024, 3), col(512, COL_BG // 512), col(512, COL_MG // 512), col(1024, 0),
                  full((2048, 1024)),
                  full((1, 1024)), full((1, 512)), full((1, 512)), full((1, 1024)), full((1, 1024))],
        out_specs=(col(2048, 0), col(1024, 0), col(1024, 0), col(1024, 0), col(512, 0),
                   col(1024, 0), col(512, 0), col(512, 0),
                   full((1, LANES)), full((1, 1024)), full((1, 1024)), full((1, 1024)), full((1, 512)),
                   full((1, 512))),
        compiler_params=_params(("arbitrary",)),
    )(h32, ya, ybp, ym, proj, proj, proj, target, w_out, g_a, g_b, g_m, g_post, b_post)


def _prep_bwd(dqa, dka, dva, dqb, dkb, dvb, dqm, dga, dgb, dgm, proj, trig, w_uq, w_ukv, g_cq, g_ckv,
              rope_a, rope_b, tm=512):
    t = proj.shape[0]

    def body(dqa_ref, dka_ref, dva_ref, dqb_ref, dkb_ref, dvb_ref, dqm_ref, dga_ref, dgb_ref, dgm_ref,
             bs_ref, trig_ref, wuq_ref, wukv_ref, gcq_ref, gckv_ref, ra_ref, rb_ref,
             dproj_ref, dwuq_ref, dwukv_ref, dgcq_ref, dgckv_ref, dqf_ref, dkv_ref):
        i = pl.program_id(0)

        @pl.when(i == 0)
        def _():
            dwuq_ref[...] = jnp.zeros_like(dwuq_ref)
            dwukv_ref[...] = jnp.zeros_like(dwukv_ref)
            dgcq_ref[...] = jnp.zeros_like(dgcq_ref)
            dgckv_ref[...] = jnp.zeros_like(dgckv_ref)

        ta = _rope_tables(trig_ref[:, 0:LANES], trig_ref[:, LANES:2 * LANES], ra_ref[...])
        tb = _rope_tables(trig_ref[:, 2 * LANES:3 * LANES], trig_ref[:, 3 * LANES:4 * LANES], rb_ref[...])
        for j in range(A_WIDTH // LANES):
            sl = slice(j * LANES, (j + 1) * LANES)
            dproj_ref[:, j * LANES:(j + 1) * LANES] = (
                _rope(dqa_ref[:, sl].astype(F32), ta, 8, inverse=True).astype(BF16))
            dproj_ref[:, 1024 + j * LANES:1024 + (j + 1) * LANES] = (
                _rope(dka_ref[:, sl].astype(F32), ta, 8, inverse=True).astype(BF16))
        dproj_ref[:, 2048:3072] = dva_ref[...]
        dproj_ref[:, 3072:4096] = dga_ref[...]

        lane = lax.broadcasted_iota(jnp.int32, (1, LANES), 1)
        low = lane < 64
        rope_lanes = (lane >= 64) & (lane < 96)
        dkr = jnp.zeros((tm, LANES), F32)
        for h in range(MLA_HEADS):
            sl = slice(h * LANES, (h + 1) * LANES)
            dqf_ref[:, sl] = _rope(dqb_ref[:, sl].astype(F32), tb, 16, inverse=True).astype(BF16)
            dk_h = dkb_ref[:, sl]
            dkv_ref[:, sl] = jnp.where(low, dk_h, dvb_ref[:, sl])
            dkr = dkr + jnp.where(rope_lanes, dk_h.astype(F32), 0.0)
        dkr = _rope(dkr, tb, 16, inverse=True)

        cq_hat, r_q = _rms_hat(bs_ref[:, 0:MLA_Q_RANK], MLA_Q_RANK)
        dwuq_ref[...] += _dot_tn(dqf_ref[...], (cq_hat * gcq_ref[...]).astype(BF16))
        dcqn = _dot(dqf_ref[...], wuq_ref[...])
        dgcq_ref[...] += _colsum(dcqn * cq_hat)
        dproj_ref[:, COL_CQ:COL_CQ + 256] = _rms_bwd(dcqn * gcq_ref[...], cq_hat, r_q, MLA_Q_RANK).astype(BF16)
        ckv_hat, r_kv = _rms_hat(bs_ref[:, MLA_Q_RANK:MLA_Q_RANK + MLA_KV_RANK], MLA_KV_RANK)
        dwukv_ref[...] += _dot_tn((ckv_hat * gckv_ref[...]).astype(BF16), dkv_ref[...])
        dckvn = _dot_nt(dkv_ref[...], wukv_ref[...])
        dgckv_ref[...] += _colsum(dckvn * ckv_hat)
        dproj_ref[:, COL_CQ + 256:COL_CQ + 384] = (
            _rms_bwd(dckvn * gckv_ref[...], ckv_hat, r_kv, MLA_KV_RANK).astype(BF16))
        dproj_ref[:, COL_CQ + 384:COL_CQ + 512] = dkr.astype(BF16)
        dproj_ref[:, COL_BG:COL_BG + 512] = dgb_ref[...]
        dproj_ref[:, COL_MQ:COL_MQ + 512] = dqm_ref[...]
        dproj_ref[:, COL_MG:COL_MG + 512] = dgm_ref[...]

    def col(width, idx):
        return pl.BlockSpec((tm, width), lambda i: (i, idx))

    def full(shape):
        return pl.BlockSpec(shape, lambda i: (0, 0))

    return pl.pallas_call(
        body, name="prep_bwd", grid=(t // tm,),
        out_shape=(jax.ShapeDtypeStruct((t, PROJ_W), BF16), jax.ShapeDtypeStruct((1024, MLA_Q_RANK), F32),
                   jax.ShapeDtypeStruct((MLA_KV_RANK, 1024), F32),
                   jax.ShapeDtypeStruct((1, MLA_Q_RANK), F32), jax.ShapeDtypeStruct((1, MLA_KV_RANK), F32)),
        in_specs=[col(1024, 0)] * 6 + [col(512, 0), col(1024, 0), col(512, 0), col(512, 0),
                  col(512, COL_CQ // 512), pl.BlockSpec((tm, 4 * LANES), lambda i: (i, 0)),
                  full((1024, MLA_Q_RANK)), full((MLA_KV_RANK, 1024)),
                  full((1, MLA_Q_RANK)), full((1, MLA_KV_RANK)), full((8, LANES)), full((8, LANES))],
        out_specs=(col(PROJ_W, 0), full((1024, MLA_Q_RANK)), full((MLA_KV_RANK, 1024)),
                   full((1, MLA_Q_RANK)), full((1, MLA_KV_RANK))),
        scratch_shapes=[pltpu.VMEM((tm, 1024), BF16), pltpu.VMEM((tm, 1024), BF16)],
        compiler_params=_params(("arbitrary",)),
    )(dqa, dka, dva, dqb, dkb, dvb, dqm, dga, dgb, dgm, proj, trig, w_uq, w_ukv, g_cq, g_ckv, rope_a, rope_b)


def _adamw_math(gv, w, m, v):
    m_new = ADAM_B1 * m + (1.0 - ADAM_B1) * gv
    v_new = ADAM_B2 * v + (1.0 - ADAM_B2) * (gv * gv)
    m_hat = m_new / (1.0 - ADAM_B1 ** ADAM_STEP)
    v_hat = v_new / (1.0 - ADAM_B2 ** ADAM_STEP)
    return -ADAM_LR * (m_hat / (jnp.sqrt(v_hat) + ADAM_EPS) + ADAM_WD * w), m_new, v_new


def _adamw(g, w, m, v, tr, name):
    r, cols = w.shape

    def body(g_ref, w_ref, m_ref, v_ref, go_ref, d_ref, nm_ref, nv_ref):
        gv = g_ref[...]
        go_ref[...] = gv
        d_ref[...], nm_ref[...], nv_ref[...] = _adamw_math(gv, w_ref[...], m_ref[...], v_ref[...])

    tile = pl.BlockSpec((tr, cols), lambda i: (i, 0))
    shape = jax.ShapeDtypeStruct((r, cols), F32)
    return pl.pallas_call(
        body, name=name, grid=(r // tr,),
        out_shape=(shape,) * 4, in_specs=[tile] * 4, out_specs=(tile,) * 4,
        compiler_params=_params(("parallel",)),
    )(g, w, m, v)


def _adamw_pieces(g, w, m, v, pieces, name):
    n = len(pieces)
    per_piece = isinstance(w, (list, tuple))
    shapes = [jax.ShapeDtypeStruct((r1 - r0, c1 - c0), F32) for r0, r1, c0, c1 in pieces]
    args = (g, *w, *m, *v) if per_piece else (g, w, m, v)

    def body(g_ref, *refs):
        ins, outs = refs[:len(args) - 1], refs[len(args) - 1:]
        gv = g_ref[...]
        if not per_piece:
            results = (gv,) + _adamw_math(gv, ins[0][...], ins[1][...], ins[2][...])
        for p, (r0, r1, c0, c1) in enumerate(pieces):
            if per_piece:
                gp = gv[r0:r1, c0:c1]
                vals = (gp,) + _adamw_math(gp, ins[p][...], ins[n + p][...], ins[2 * n + p][...])
            else:
                vals = [full[r0:r1, c0:c1] for full in results]
            for kind, val in enumerate(vals):
                outs[kind * n + p][...] = val

    flat = pl.pallas_call(
        body, name=name, out_shape=tuple(shapes) * 4,
        in_specs=[IN_VMEM] * len(args), out_specs=tuple([IN_VMEM] * (4 * n)),
        compiler_params=_params(None),
    )(*args)
    return [[flat[kind * n + p] for kind in range(4)] for p in range(n)]


def _core_sum(g, recv, core, rows, tr, name, ride=None):
    cols = g.shape[2]
    nblk = rows // tr
    n_in = len(ride.args) if ride else 0
    n_out = len(ride.out_shapes) if ride else 0

    def body(c_ref, g_ref, r_ref, *rest):
        sf_ref, sb_ref = rest[n_in], rest[n_in + 1]
        if ride:
            j, i = pl.program_id(0), pl.program_id(1)
            ride.run(j * nblk + i, 4 * nblk, rest[:n_in], rest[n_in + 2:n_in + 2 + n_out],
                     rest[n_in + 2 + n_out:])
        tot = g_ref[...] + r_ref[...]
        sf_ref[...] = tot
        sb_ref[...] = tot.astype(BF16)

    half = pl.BlockSpec((None, tr, cols), lambda j, i, c_ref: (j, i, 0))
    shapes = (jax.ShapeDtypeStruct((4, rows, cols), F32), jax.ShapeDtypeStruct((4, rows, cols), BF16))
    return pl.pallas_call(
        body, name=name,
        grid_spec=pltpu.PrefetchScalarGridSpec(
            num_scalar_prefetch=1, grid=(4, nblk),
            in_specs=[pl.BlockSpec((None, tr, cols), lambda j, i, c_ref: (j, c_ref[0] * nblk + i, 0)), half]
            + (ride.in_specs if ride else []),
            out_specs=(half, half) + (ANY,) * n_out,
            scratch_shapes=ride.scratch() if ride else []),
        out_shape=shapes + tuple(ride.out_shapes if ride else ()),
        compiler_params=_params(("arbitrary", "arbitrary") if ride else ("parallel", "parallel")),
    )(core, g, recv, *(ride.args if ride else ()))


def _half_to_sibling(g4):
    def plan(in_refs, out_refs, send_sems, recv_sems):
        x, y, c = _position()
        copies = [pltpu.make_async_remote_copy(
            src_ref=in_refs[0].at[j, 1 - c], dst_ref=out_refs[0].at[j], send_sem=send_sems.at[j],
            recv_sem=recv_sems.at[j], device_id=(x, y, 1 - c), device_id_type=MESH) for j in range(4)]

        def start():
            for cp in copies:
                cp.start()

        def finish():
            for cp in copies:
                cp.wait_recv()
            for cp in copies:
                cp.wait_send()

        return start, finish

    return _Ride([g4], [jax.ShapeDtypeStruct((4, g4.shape[2], 1024), F32)], (4, 4), plan)


def _gather_plan(src_ref, dst_ref, send_sems, recv_sems, local_sems):
    x, y, c = _position()
    me = 2 * x + y
    rows = src_ref.shape[1]
    cut = -(-rows // 32) * 16
    pieces = (pl.ds(0, cut), pl.ds(cut, rows - cut))
    local = pltpu.make_async_copy(src_ref, dst_ref.at[me], local_sems.at[0])

    def over_ici(sem, k, chip, t, src=None):
        where = dst_ref.at[chip, c, pieces[t]]
        return pltpu.make_async_remote_copy(
            src_ref=where if src is None else src, dst_ref=where, send_sem=send_sems.at[sem],
            recv_sem=recv_sems.at[sem], device_id=(x ^ (k >> 1), y ^ (k & 1), c), device_id_type=MESH)

    def mine_to(k, t):
        return over_ici(2 * (k - 1) + t, k, me, t, src=src_ref.at[c, pieces[t]])

    def from_neighbour(k, t):
        return over_ici(2 * (k - 1) + t, k, me ^ k, t)

    def to_sibling(k, half):
        piece = dst_ref.at[me ^ k, half]
        return pltpu.make_async_remote_copy(
            src_ref=piece, dst_ref=piece, send_sem=send_sems.at[5 + k], recv_sem=recv_sems.at[5 + k],
            device_id=(x, y, 1 - c), device_id_type=MESH)

    sends = [mine_to(2, 0), mine_to(1, 1), mine_to(2, 1), mine_to(1, 0)]
    onward = [over_ici(4, 1, me ^ 2, 0), over_ici(5, 2, me ^ 1, 1)]

    def start():
        local.start()
        for cp in sends:
            cp.start()

    def pass_on():
        from_neighbour(2, 0).wait_recv()
        onward[0].start()
        from_neighbour(1, 1).wait_recv()
        onward[1].start()

    def to_other_core():
        from_neighbour(2, 1).wait_recv()
        to_sibling(2, c).start()
        from_neighbour(1, 0).wait_recv()
        to_sibling(1, c).start()
        over_ici(4, 1, me ^ 3, 0).wait_recv()
        over_ici(5, 2, me ^ 3, 1).wait_recv()
        to_sibling(3, c).start()

    def finish():
        for k in (1, 2, 3):
            to_sibling(k, 1 - c).wait_recv()
        for cp in sends + onward + [to_sibling(k, c) for k in (1, 2, 3)]:
            cp.wait_send()
        local.wait()

    return start, pass_on, to_other_core, finish


def _gather_ride(shard, spread):
    def plan(in_refs, out_refs, send_sems, recv_sems, local_sems):
        return _gather_plan(in_refs[0], out_refs[0], send_sems, recv_sems, local_sems)

    return _Ride([shard], [jax.ShapeDtypeStruct((4,) + shard.shape, shard.dtype)], (9, 9, 1), plan,
                 in_specs=[IN_VMEM], spread=spread)


def _chip_sum(sf, recv, chip, rows, tr, name):
    cols = sf.shape[2]
    n_recv = recv.shape[0]

    def body(me_ref, sf_ref, r_ref, out_ref):
        acc = sf_ref[...]
        for k in range(n_recv):
            acc = acc + r_ref[k].astype(F32)
        out_ref[...] = acc

    return pl.pallas_call(
        body, name=name,
        grid_spec=pltpu.PrefetchScalarGridSpec(
            num_scalar_prefetch=1, grid=(rows // tr,),
            in_specs=[pl.BlockSpec((None, tr, cols), lambda i, me_ref: (me_ref[0], i, 0)),
                      pl.BlockSpec((n_recv, tr, cols), lambda i, me_ref: (0, i, 0))],
            out_specs=pl.BlockSpec((tr, cols), lambda i, me_ref: (i, 0))),
        out_shape=jax.ShapeDtypeStruct((rows, cols), F32),
        compiler_params=_params(("parallel",)),
    )(chip, sf, recv)


def _position():
    return lax.axis_index("x"), lax.axis_index("y"), lax.axis_index("c")


def _dh_scatter(dproj, w_in_arr_t, x, dz, g, sb_in, sb_rest, tm=512, tk=3072):
    t, d = x.shape
    nk = dproj.shape[1] // tk
    ni = t // tm
    total = ni * nk
    halves = (HALF_IN, HALF_REST)
    cuts = tuple(-(-rows // 32) * 16 for rows in halves)

    def rows_of(a, p):
        return cuts[a] if p == 0 else halves[a] - cuts[a]

    def piece(a, p):
        return pl.ds(0, cuts[a]) if p == 0 else pl.ds(cuts[a], halves[a] - cuts[a])

    def body(dp_ref, w_ref, x_ref, dz_ref, g_ref, sbin_ref, sbrest_ref, dx_ref, dg_ref, db_ref, rin_ref, rrest_ref,
             acc_ref, pay_in0, pay_in1, pay_rest0, pay_rest1, own_in0, own_in1, own_rest0, own_rest1,
             send_sems, recv_sems, local_sems):
        step = pl.program_id(0) * nk + pl.program_id(1)
        kk = pl.program_id(1)
        px, py, pc = _position()
        me = 2 * px + py
        srcs = (sbin_ref, sbrest_ref)
        dsts = (rin_ref, rrest_ref)
        pays = ((pay_in0, pay_in1), (pay_rest0, pay_rest1))
        owns = ((own_in0, own_in1), (own_rest0, own_rest1))
        via = (2, 1)
        onto = (1, 2)

        def peer(k):
            return (px ^ (k >> 1), py ^ (k & 1), pc)

        def payload(a, p):
            return pltpu.make_async_remote_copy(
                src_ref=srcs[a].at[me ^ 3, piece(a, p)], dst_ref=pays[a][p], send_sem=send_sems.at[2 * a + p],
                recv_sem=recv_sems.at[2 * a + p], device_id=peer(via[p]), device_id_type=MESH)

        def direct(a, k, p, src):
            sem = 4 + 4 * a + 2 * (k - 1) + p
            return pltpu.make_async_remote_copy(
                src_ref=src, dst_ref=dsts[a].at[k - 1, piece(a, p)], send_sem=send_sems.at[sem],
                recv_sem=recv_sems.at[sem], device_id=peer(k), device_id_type=MESH)

        def plain(a, k, p):
            return direct(a, k, p, srcs[a].at[me ^ k, piece(a, p)])

        def stage(a, p):
            return pltpu.make_async_copy(srcs[a].at[me ^ onto[p], piece(a, p)], owns[a][p], local_sems.at[2 * a + p])

        @pl.when(step == 0)
        def _():
            dg_ref[...] = jnp.zeros_like(dg_ref)
            db_ref[...] = jnp.zeros_like(db_ref)
            for a in range(2):
                for p in range(2):
                    payload(a, p).start()
                    stage(a, p).start()
                plain(a, 1, 1).start()
                plain(a, 2, 0).start()

        @pl.when(step == (5 * total) // 8)
        def _():
            for a in range(2):
                for p in range(2):
                    payload(a, p).wait_recv()
                    stage(a, p).wait()
                    owns[a][p][...] = (owns[a][p][...].astype(F32) + pays[a][p][...].astype(F32)).astype(BF16)
                    direct(a, onto[p], p, owns[a][p]).start()

        part = _dot(dp_ref[...], w_ref[...])

        @pl.when(kk == 0)
        def _():
            acc_ref[...] = part

        @pl.when(kk > 0)
        def _():
            acc_ref[...] += part

        @pl.when(kk == nk - 1)
        def _():
            xh, rstd = _ln_hat(x_ref[...])
            dht = acc_ref[...] + DEEPNORM_ALPHA * dz_ref[...]
            dg_ref[...] += _colsum(dht * xh)
            db_ref[...] += _colsum(dht)
            dx_ref[...] = _ln_bwd_rows(dht * g_ref[...], xh, rstd)

        @pl.when(step == total - 1)
        def _():
            for a in range(2):
                for k in (1, 2):
                    for p in range(2):
                        plain(a, k, p).wait_recv()
            for a in range(2):
                for p in range(2):
                    payload(a, p).wait_send()
                    direct(a, onto[p], p, owns[a][p]).wait_send()
                plain(a, 1, 1).wait_send()
                plain(a, 2, 0).wait_send()

    tile = pl.BlockSpec((tm, d), lambda i, kk: (i, 0))
    row = pl.BlockSpec((1, d), lambda i, kk: (0, 0))
    pieces = [pltpu.VMEM((rows_of(a, p), 1024), BF16) for a in range(2) for p in range(2)]
    return pl.pallas_call(
        body, name="dh_scatter", grid=(ni, nk),
        out_shape=(jax.ShapeDtypeStruct((t, d), F32), jax.ShapeDtypeStruct((1, d), F32),
                   jax.ShapeDtypeStruct((1, d), F32),
                   jax.ShapeDtypeStruct((2, HALF_IN, 1024), BF16),
                   jax.ShapeDtypeStruct((2, HALF_REST, 1024), BF16)),
        in_specs=[pl.BlockSpec((tm, tk), lambda i, kk: (i, kk)), pl.BlockSpec((tk, d), lambda i, kk: (kk, 0)),
                  tile, tile, row, ANY, ANY],
        out_specs=(tile, row, row, ANY, ANY),
        scratch_shapes=[pltpu.VMEM((tm, d), F32)] + pieces + pieces
        + [pltpu.SemaphoreType.DMA((12,)), pltpu.SemaphoreType.DMA((12,)), pltpu.SemaphoreType.DMA((4,))],
        compiler_params=_params(("arbitrary", "arbitrary")),
    )(dproj, w_in_arr_t, x, dz, g, sb_in, sb_rest)


def _join_and_allreduce(gh_in, gh_rest, vec):
    def body(hin_ref, hrest_ref, vec_ref, oin_ref, orest_ref, sum_ref, all_ref, send_sems, recv_sems, local_sems):
        x, y, c = _position()
        srcs = (hin_ref, hrest_ref)
        dsts = (oin_ref, orest_ref)
        me = 4 * x + 2 * y + c
        all_ref[me] = vec_ref[...]

        def small(k, slot):
            return pltpu.make_async_remote_copy(
                src_ref=vec_ref, dst_ref=all_ref.at[slot], send_sem=send_sems.at[k + 1], recv_sem=recv_sems.at[k + 1],
                device_id=(x ^ (k >> 2), y ^ ((k >> 1) & 1), c ^ (k & 1)), device_id_type=MESH)

        def half(a, slot):
            return pltpu.make_async_remote_copy(
                src_ref=srcs[a], dst_ref=dsts[a].at[slot], send_sem=send_sems.at[a], recv_sem=recv_sems.at[a],
                device_id=(x, y, 1 - c), device_id_type=MESH)

        local = [pltpu.make_async_copy(srcs[a], dsts[a].at[c], local_sems.at[a]) for a in range(2)]
        remote = [half(a, c) for a in range(2)] + [small(k, me) for k in range(1, 8)]
        for cp in local + remote:
            cp.start()
        for k in range(1, 8):
            small(k, me ^ k).wait_recv()
        for a in range(2):
            half(a, 1 - c).wait_recv()
        for cp in remote:
            cp.wait_send()
        for cp in local:
            cp.wait()
        total = all_ref[0]
        for d in range(1, 8):
            total = total + all_ref[d]
        sum_ref[...] = total

    return pl.pallas_call(
        body, name="join_halves",
        out_shape=(jax.ShapeDtypeStruct((2, HALF_IN, 1024), F32),
                   jax.ShapeDtypeStruct((2, HALF_REST, 1024), F32),
                   jax.ShapeDtypeStruct(vec.shape, vec.dtype)),
        in_specs=[IN_VMEM, IN_VMEM, IN_VMEM], out_specs=(ANY, ANY, IN_VMEM),
        scratch_shapes=[pltpu.VMEM((8,) + vec.shape, vec.dtype), pltpu.SemaphoreType.DMA((9,)),
                        pltpu.SemaphoreType.DMA((9,)), pltpu.SemaphoreType.DMA((2,))],
    )(gh_in, gh_rest, vec)


def _pack_rest(w_uq, w_ukv, w_mem, w_out):
    rows = jnp.concatenate([w_uq[0].T.reshape(-1, 1024), w_ukv.reshape(-1, 1024), w_mem.reshape(-1, 1024),
                            w_out.reshape(-1, 1024)], axis=0)
    return jnp.pad(rows, ((0, ROWS_REST - ROWS_USED), (0, 0)))


def _arranged_w_in(g_in):
    z = functools.partial(jnp.zeros, dtype=g_in.dtype)
    cut = 4480 - 2 * SHARD_ROWS
    return jnp.concatenate(
        [g_in[0, :SHARD_ROWS], g_in[1, :SHARD_ROWS], g_in[2, :cut], z((64, 1024)), g_in[2, cut:cut + 32],
         z((32, 1024)), g_in[2, cut + 32:SHARD_ROWS], g_in[3, :SHARD_ROWS]], axis=0)


def _rest_weights(g_rest):
    w_uq_t = g_rest[:, 0:ROWS_UQ].reshape(768, 256)
    w_uq_pad_t = jnp.pad(w_uq_t.reshape(MLA_HEADS, MLA_QK_DIM, 256), ((0, 0), (0, 32), (0, 0))).reshape(1024, 256)
    w_ukv = jnp.concatenate([g_rest[j, ROWS_UQ:ROWS_UQ + ROWS_UKV].reshape(128, 256) for j in range(4)], axis=1)
    lo = ROWS_UQ + ROWS_UKV
    w_mem = g_rest[:, lo:lo + ROWS_MEM].reshape(4 * ROWS_MEM, 1024)
    w_out = g_rest[:, lo + ROWS_MEM:lo + ROWS_MEM + ROWS_OUT].reshape(4 * ROWS_OUT, 1024)
    return w_uq_pad_t, w_ukv, w_mem, w_out


def _dw_in_split(dproj, h, tm=1024):
    t = dproj.shape[0]
    steps = PROJ_W // tm
    gap = ROWS_IN - SHARD_ROWS
    nat = 4608 - 96
    last = 4608 + 3 * SHARD_ROWS - nat
    segments = ((0, SHARD_ROWS, 0, 0), (SHARD_ROWS, 2 * SHARD_ROWS, 1, 0), (2 * SHARD_ROWS, 4480, 2, 0),
                (4544, 4576, 2, 4480 - 2 * SHARD_ROWS), (4608, last, 2, 4512 - 2 * SHARD_ROWS), (last, PROJ_W, 3, 0))

    def pieces(j):
        out = []
        for lo, hi, chip, dst in segments:
            a, b = max(lo, j * tm), min(hi, (j + 1) * tm)
            if a < b:
                out.append((a - j * tm, chip, dst + a - lo, b - a))
        return out

    n_sem = max(len(pieces(j)) for j in range(steps))

    def body(a_ref, b_ref, o_ref, tile_ref, zero_ref, sems, pad_sems):
        i = pl.program_id(0)

        def copies(j):
            return [pltpu.make_async_copy(tile_ref.at[j % 2, pl.ds(off, n)], o_ref.at[chip, pl.ds(dst, n)],
                                          sems.at[j % 2, q])
                    for q, (off, chip, dst, n) in enumerate(pieces(j))]

        def pad_copies():
            return [pltpu.make_async_copy(zero_ref, o_ref.at[chip, pl.ds(SHARD_ROWS, gap)], pad_sems.at[chip])
                    for chip in range(4)]

        @pl.when(i == 0)
        def _():
            zero_ref[...] = jnp.zeros_like(zero_ref)
            for c in pad_copies():
                c.start()

        for j in range(2, steps):
            @pl.when(i == j)
            def _(j=j):
                for c in copies(j - 2):
                    c.wait()

        tile_ref[i % 2] = _dot_tn(a_ref[...], b_ref[...])

        for j in range(steps):
            @pl.when(i == j)
            def _(j=j):
                for c in copies(j):
                    c.start()
                if j == steps - 1:
                    for c in copies(j - 1) + copies(j) + pad_copies():
                        c.wait()

    return pl.pallas_call(
        body, name="dw_in", grid=(steps,),
        out_shape=jax.ShapeDtypeStruct((4, ROWS_IN, 1024), F32),
        in_specs=[pl.BlockSpec((t, tm), lambda i: (0, i)), pl.BlockSpec((t, 1024), lambda i: (0, 0))],
        out_specs=ANY,
        scratch_shapes=[pltpu.VMEM((2, tm, 1024), F32), pltpu.VMEM((gap, 1024), F32),
                        pltpu.SemaphoreType.DMA((2, n_sem)), pltpu.SemaphoreType.DMA((4,))],
        compiler_params=_params(("arbitrary",)),
    )(dproj, h)


def _split_rest(dw_uq_pad_t, dw_ukv, dw_mem, dw_out):
    dw_uq_t = dw_uq_pad_t.reshape(MLA_HEADS, LANES, 256)[:, :MLA_QK_DIM].reshape(4, ROWS_UQ, 1024)
    parts = [dw_uq_t, dw_ukv.reshape(128, 4, 256).transpose(1, 0, 2).reshape(4, ROWS_UKV, 1024),
             dw_mem.reshape(4, ROWS_MEM, 1024), dw_out.reshape(4, ROWS_OUT, 1024)]
    return jnp.pad(jnp.concatenate(parts, axis=1), ((0, 0), (0, ROWS_REST - ROWS_USED), (0, 0)))


def _rope_consts(rot, first, period):
    half = rot // 2
    inv_freq = np.float32(ROPE_THETA) ** (-(np.arange(0, rot, 2, dtype=np.float32) / np.float32(rot)))
    lane = np.arange(LANES) % period - first
    in_rot = (lane >= 0) & (lane < rot)
    out = np.zeros((8, LANES), np.float32)
    out[0] = np.where(in_rot, inv_freq[np.clip(lane, 0, rot - 1) % half], 0.0)
    out[1] = in_rot & (lane < half)
    out[2] = in_rot & (lane >= half)
    return jnp.asarray(out)


def _band_bias(s):
    nblk = s // BAND_Q
    starts = np.array([_band_start(i, s) for i in range(nblk)])
    uq = (np.arange(nblk)[:, None] * BAND_Q + np.arange(BAND_Q)[None, :])[:, :, None]
    uk = (starts[:, None] + np.arange(BAND_WIN)[None, :])[:, None, :]
    tiles, index, seen = [], [], {}
    for _, d in DILATED:
        length = s // d
        ok = (uq // length == uk // length) & (np.abs(uq - uk) <= 64)
        row = []
        for i in range(nblk):
            key = ok[i].tobytes()
            if key not in seen:
                seen[key] = len(tiles)
                tiles.append(np.where(ok[i], 0.0, NEG_INF).astype(np.float32))
            row.append(seen[key])
        index.append(row)
    return jnp.asarray(np.stack(tiles, axis=0)), index


def _forward_backward(h, h32, proj, trig, rope_consts, x, mem, target, weights, gains):
    w_uq_pad_t, w_ukv, w_mem, w_out = weights
    g_emb, b_emb, g_cq, g_ckv, g_out_a, g_out_b, g_out_m, g_post, b_post = gains
    nb, s, d = x.shape
    t = nb * s
    x2 = x.reshape(t, d)
    mem2 = mem.reshape(nb * N_MEM, d)
    tgt2 = target.reshape(t, d)
    rope_a, rope_b = rope_consts
    bias, bias_index = _band_bias(s)
    scales = (0.125, MLA_QK_DIM ** -0.5, 128 ** -0.5)

    qa, ka, va, qb, kb, vb, qm = _prep(proj, trig, w_uq_pad_t, w_ukv, g_cq, g_ckv, rope_a, rope_b, scales)
    mkv = _mm(mem2, w_mem, BF16, nb * N_MEM, 1024, 1024, "mem_kv")

    cfg_b = dict(nb=nb, s=s, sk=s, heads=8, voff=0, bq=256)
    cfg_m = dict(nb=nb, s=s, sk=N_MEM, heads=4, hpb=2, voff=4, bq=1024)
    ya, lse_a, qkv_ordered = _dilated_fwd(qa, ka, va, bias, bias_index, nb=nb, s=s, name="attn_a_fwd")
    yb, lse_b = _attn_fwd(qb, kb, vb, name="attn_b_fwd", hpb=4, **cfg_b)
    ym, lse_m = _attn_fwd(qm, mkv, mkv, name="attn_m_fwd", **cfg_m)

    (y, dz, doa, dob, dom, dga, dgb, dgm, loss, dg_post, db_post, dg_a, dg_b, dg_m) = _post(
        h32, ya, yb, ym, proj, tgt2, w_out, g_out_a, g_out_b, g_out_m, g_post, b_post)

    dqa, dka, dva = _dilated_bwd(qa, ka, va, qkv_ordered, ya, doa, lse_a, bias, bias_index, nb=nb, s=s, scale=scales[0],
                                 name="attn_a_bwd")
    dqb, dkb, dvb = _attn_bwd(qb, kb, vb, yb, dob, lse_b, name="attn_b_bwd", scale=scales[1], hpb=4, **cfg_b)
    dqm, dmk, dmv = _attn_bwd(qm, mkv, mkv, ym, dom, lse_m, name="attn_m_bwd", scale=scales[2], **cfg_m)
    dmkv = jnp.concatenate([dmk, dmv], axis=1)

    dproj, dw_uq_pad_t, dw_ukv, dg_cq, dg_ckv = _prep_bwd(
        dqa, dka, dva, dqb, dkb, dvb, dqm, dga, dgb, dgm, proj, trig, w_uq_pad_t, w_ukv, g_cq, g_ckv, rope_a, rope_b)

    small_rows = (dg_cq, dg_ckv, loss, dg_a, dg_b, dg_m, dg_post, db_post)
    return (dproj, h, y, dz, dw_uq_pad_t, dw_ukv, mem2, dmkv), x2, small_rows


def _weight_grads(operands, core):
    dproj, h, y, dz, dw_uq_pad_t, dw_ukv, mem2, dmkv = operands
    g_in = _dw_in_split(dproj, h)
    dw_out, r_in = _mm(y, dz, F32, 1024, 1024, 2048, "dw_out", mode="tn",
                       ride=_half_to_sibling(g_in.reshape(4, 2, HALF_IN, 1024)))
    dw_mem = _mm(mem2, dmkv, F32, 1024, 1024, mem2.shape[0], "dw_mem", mode="tn")
    g_rest = _split_rest(dw_uq_pad_t, dw_ukv, dw_mem, dw_out)
    sf_in, sb_in, r_rest = _core_sum(g_in, r_in, core, HALF_IN, HALF_IN // 2, "core_sum_in",
                                     ride=_half_to_sibling(g_rest.reshape(4, 2, HALF_REST, 1024)))
    sf_rest, sb_rest = _core_sum(g_rest, r_rest, core, HALF_REST, HALF_REST, "core_sum_rest")
    return sf_in, sb_in, sf_rest, sb_rest


def _small_block(dg_emb, db_emb, small_rows):
    dg_cq, dg_ckv, loss, dg_a, dg_b, dg_m, dg_post, db_post = small_rows
    row2 = jnp.concatenate([dg_cq, dg_ckv, loss, jnp.zeros((1, 512), F32)], axis=1)
    return jnp.concatenate([dg_emb, db_emb, row2, dg_a, jnp.concatenate([dg_b, dg_m], axis=1), dg_post, db_post,
                            jnp.zeros((1, 1024), F32)], axis=0)


def _pack_small(g_emb, b_emb, g_cq, g_ckv, g_out_a, g_out_b, g_out_m, g_post, b_post):
    row2 = jnp.concatenate([g_cq.reshape(1, -1), g_ckv.reshape(1, -1), jnp.zeros((1, 640), F32)], axis=1)
    return jnp.concatenate([g_emb.reshape(1, -1), b_emb.reshape(1, -1), row2, g_out_a.reshape(1, -1),
                            jnp.concatenate([g_out_b.reshape(1, -1), g_out_m.reshape(1, -1)], axis=1),
                            g_post.reshape(1, -1), b_post.reshape(1, -1), jnp.zeros((1, 1024), F32)], axis=0)


def kernel(x, mem, positions, g_emb, b_emb, w_in, g_cq, g_ckv, w_uq, w_ukv, w_mem_kv, g_out_a, g_out_b, g_out_m, w_out, g_post, b_post, loss_target, m_g_emb, m_b_emb, m_w_in, m_g_cq, m_g_ckv, m_w_uq, m_w_ukv, m_w_mem_kv, m_g_out_a, m_g_out_b, m_g_out_m, m_w_out, m_g_post, m_b_post, v_g_emb, v_b_emb, v_w_in, v_g_cq, v_g_ckv, v_w_uq, v_w_ukv, v_w_mem_kv, v_g_out_a, v_g_out_b, v_g_out_m, v_w_out, v_g_post, v_b_post):
    w_rest = _pack_rest(w_uq, w_ukv, w_mem_kv, w_out)
    w_in_t = w_in[0].T
    w_in_b = jnp.pad(w_in_t.astype(BF16), ((0, ROWS_IN - SHARD_ROWS), (0, 0)))
    gains = (g_emb.reshape(1, -1), b_emb.reshape(1, -1), g_cq, g_ckv, g_out_a, g_out_b, g_out_m, g_post, b_post)
    rope_consts = (_rope_consts(16, 0, 64), _rope_consts(32, 64, 128))
    h, h32, trig, gathered_in = _ln_fwd(x.reshape(-1, D_MODEL), gains[0], gains[1],
                                        positions.reshape(-1, 1).astype(F32), *rope_consts,
                                        ride=_gather_ride(w_in_b.reshape(2, HALF_IN, 1024), spread=False))
    w_in_arr_t = _arranged_w_in(gathered_in.reshape(4, ROWS_IN, 1024))
    proj, gathered_rest = _mm(h, w_in_arr_t, F32, 1024, 2048, 1024, "in_proj", mode="nt",
                              ride=_gather_ride(w_rest.astype(BF16).reshape(2, HALF_REST, 1024), spread=True))
    weights = _rest_weights(gathered_rest.reshape(4, ROWS_REST, 1024))
    operands, x2, small_rows = _forward_backward(h, h32, proj, trig, rope_consts, x, mem, loss_target, weights,
                                                 gains)

    core = lax.axis_index("c").astype(jnp.int32).reshape(1)
    chip = (2 * lax.axis_index("x") + lax.axis_index("y")).astype(jnp.int32).reshape(1)
    sf_in, sb_in, sf_rest, sb_rest = _weight_grads(operands, core)
    grad_x, dg_emb, db_emb, rb_in, rb_rest = _dh_scatter(operands[0], w_in_arr_t, x2, operands[3], gains[0],
                                                         sb_in, sb_rest)
    gh_in = _chip_sum(sf_in, rb_in, chip, HALF_IN, HALF_IN // 2, "chip_sum_in")
    gh_rest = _chip_sum(sf_rest, rb_rest, chip, HALF_REST, HALF_REST, "chip_sum_rest")
    grad_in, grad_rest, small_sum = _join_and_allreduce(gh_in, gh_rest, _small_block(dg_emb, db_emb, small_rows))
    grad_in = grad_in.reshape(ROWS_IN, 1024)
    grad_rest = grad_rest.reshape(ROWS_REST, 1024)

    big_in = _adamw(grad_in, w_in_t, m_w_in[0].T, v_w_in[0].T, SHARD_ROWS // 3, "adamw_in")
    def rest_parts(a_uq, a_ukv, a_mem, a_out):
        return [a_uq[0].T.reshape(ROWS_UQ, 1024), a_ukv.reshape(ROWS_UKV, 1024), a_mem[0], a_out[0]]

    uq, ukv, wmem, wout = _adamw_pieces(
        grad_rest, rest_parts(w_uq, w_ukv, w_mem_kv, w_out), rest_parts(m_w_uq, m_w_ukv, m_w_mem_kv, m_w_out),
        rest_parts(v_w_uq, v_w_ukv, v_w_mem_kv, v_w_out), REST_PIECES, "adamw_rest")
    sm = _adamw_pieces(
        small_sum,
        _pack_small(g_emb, b_emb, g_cq, g_ckv, g_out_a, g_out_b, g_out_m, g_post, b_post),
        _pack_small(m_g_emb, m_b_emb, m_g_cq, m_g_ckv, m_g_out_a, m_g_out_b, m_g_out_m, m_g_post, m_b_post),
        _pack_small(v_g_emb, v_b_emb, v_g_cq, v_g_ckv, v_g_out_a, v_g_out_b, v_g_out_m, v_g_post, v_b_post),
        SMALL_PIECES, "adamw_small")
    loss = small_sum[2, 384]

    def ordered(kind):
        s_gemb, s_bemb, s_gcq, s_gckv, s_ga, s_gb, s_gm, s_gpost, s_bpost = [piece[kind] for piece in sm]
        return [s_gemb.reshape(-1), s_bemb.reshape(-1), big_in[kind].T[None], s_gcq, s_gckv,
                uq[kind].reshape(192, 256).T[None], ukv[kind].reshape(1, 128, 256), wmem[kind][None], s_ga, s_gb,
                s_gm, wout[kind][None], s_gpost, s_bpost]

    return (loss, grad_x.reshape(x.shape), *ordered(0), *ordered(1), *ordered(2), *ordered(3))
```

```python
import functools
import math

import jax
import jax.numpy as jnp
import numpy as np
from jax import lax
from jax.experimental import pallas as pl
from jax.experimental.pallas import tpu as pltpu

F32 = jnp.float32
BF16 = jnp.bfloat16
MESH = pl.DeviceIdType.MESH
ANY = pl.BlockSpec(memory_space=pl.ANY)
IN_VMEM = pl.BlockSpec(memory_space=pltpu.VMEM)

D_MODEL = 1024
A_WIDTH = 1024
MLA_HEADS = 8
MLA_Q_RANK = 256
MLA_KV_RANK = 128
MLA_QK_DIM = 96
MEM_WIDTH = 512
N_MEM = 256
ROPE_THETA = 500000.0
NORM_EPS = 1e-5
NEG_INF = -1e30
DEEPNORM_ALPHA = 2.0 ** 0.25
DILATED = ((64, 1), (256, 4), (1024, 16))

ADAM_LR = 0.001
ADAM_B1 = 0.9
ADAM_B2 = 0.999
ADAM_EPS = 1e-08
ADAM_WD = 0.01
ADAM_STEP = 10

LANES = 128
VMEM_LIMIT = 56 * 1024 * 1024
LOG2E = math.log2(math.e)
LN2 = math.log(2.0)

PROJ_W = 6144
COL_CQ = 4096
COL_BG = 4608
COL_MQ = 5120
COL_MG = 5632

SHARD_ROWS = 1512
ROWS_IN = 1536
ROWS_UQ, ROWS_UKV, ROWS_MEM, ROWS_OUT = 48, 32, 256, 512
ROWS_USED = ROWS_UQ + ROWS_UKV + ROWS_MEM + ROWS_OUT
ROWS_REST = 864
HALF_IN = ROWS_IN // 2
HALF_REST = ROWS_REST // 2
REST_PIECES = ((0, 48, 0, 1024), (48, 80, 0, 1024), (80, 336, 0, 1024), (336, 848, 0, 1024))
SMALL_PIECES = ((0, 1, 0, 1024), (1, 2, 0, 1024), (2, 3, 0, 256), (2, 3, 256, 384), (3, 4, 0, 1024), (4, 5, 0, 512),
                (4, 5, 512, 1024), (5, 6, 0, 1024), (6, 7, 0, 1024))


def _params(sem=None, vmem=VMEM_LIMIT):
    return pltpu.CompilerParams(dimension_semantics=sem, vmem_limit_bytes=vmem)


def _dot(a, b):
    return jnp.dot(a, b, preferred_element_type=F32)


def _dot_nt(a, b):
    return lax.dot_general(a, b, (((1,), (1,)), ((), ())), preferred_element_type=F32)


def _dot_tn(a, b):
    return lax.dot_general(a, b, (((0,), (0,)), ((), ())), preferred_element_type=F32)


def _ln_hat(x):
    mu = jnp.mean(x, axis=-1, keepdims=True)
    xc = x - mu
    var = jnp.mean(xc * xc, axis=-1, keepdims=True)
    rstd = lax.rsqrt(var + NORM_EPS)
    return xc * rstd, rstd


def _ln_bwd_rows(dxh, xh, rstd):
    return rstd * (dxh - jnp.mean(dxh, axis=-1, keepdims=True) - xh * jnp.mean(dxh * xh, axis=-1, keepdims=True))


def _rms_hat(x, width):
    ms = jnp.sum(x * x, axis=-1, keepdims=True) * (1.0 / width)
    r = lax.rsqrt(ms + NORM_EPS)
    return x * r, r


def _rms_bwd(u, xh, r, width):
    return r * (u - xh * (jnp.sum(u * xh, axis=-1, keepdims=True) * (1.0 / width)))


def _colsum(v):
    return jnp.sum(v, axis=0, keepdims=True)


def _rope_tables(cos, sin, consts):
    return cos, sin * consts[2:3, :], -sin * consts[1:2, :]


def _rope(x, tables, half, inverse=False):
    c, s_up, s_dn = tables
    if inverse:
        s_up, s_dn = -s_up, -s_dn
    return x * c + pltpu.roll(x, half, 1) * s_up + pltpu.roll(x, LANES - half, 1) * s_dn


def _ln_fwd(x, g, b, pos, rope_a, rope_b, tm=512, ride=None):
    t, d = x.shape
    n_in = len(ride.args) if ride else 0
    n_out = len(ride.out_shapes) if ride else 0
    steps = t // tm

    def body(x_ref, g_ref, b_ref, pos_ref, ra_ref, rb_ref, *rest):
        h_ref, h32_ref, trig_ref = rest[n_in:n_in + 3]
        if ride:
            i = pl.program_id(0)
            ride.run(i, steps, rest[:n_in], rest[n_in + 3:n_in + 3 + n_out], rest[n_in + 3 + n_out:])
        xh, _ = _ln_hat(x_ref[...])
        h = xh * g_ref[...] + b_ref[...]
        h32_ref[...] = h
        h_ref[...] = h.astype(BF16)
        for j, consts in enumerate((ra_ref, rb_ref)):
            ang = pos_ref[...] * consts[0:1, :]
            trig_ref[:, 2 * j * LANES:(2 * j + 1) * LANES] = jnp.cos(ang)
            trig_ref[:, (2 * j + 1) * LANES:(2 * j + 2) * LANES] = jnp.sin(ang)

    row = pl.BlockSpec((1, d), lambda i: (0, 0))
    tile = pl.BlockSpec((tm, d), lambda i: (i, 0))
    consts = pl.BlockSpec((8, LANES), lambda i: (0, 0))
    trig_tile = pl.BlockSpec((tm, 4 * LANES), lambda i: (i, 0))
    in_specs = [tile, row, row, pl.BlockSpec((tm, 1), lambda i: (i, 0)), consts, consts]
    shapes = (jax.ShapeDtypeStruct((t, d), BF16), jax.ShapeDtypeStruct((t, d), F32),
              jax.ShapeDtypeStruct((t, 4 * LANES), F32))
    if not ride:
        return pl.pallas_call(
            body, name="ln_fwd", grid=(steps,), out_shape=shapes, in_specs=in_specs,
            out_specs=(tile, tile, trig_tile), compiler_params=_params(("parallel",)),
        )(x, g, b, pos, rope_a, rope_b)
    return pl.pallas_call(
        body, name="ln_fwd", grid=(steps,),
        out_shape=(*shapes, *ride.out_shapes),
        in_specs=in_specs + ride.in_specs, out_specs=(tile, tile, trig_tile) + (ANY,) * n_out,
        scratch_shapes=ride.scratch(),
        compiler_params=_params(("arbitrary",)),
    )(x, g, b, pos, rope_a, rope_b, *ride.args)


class _Ride:
    def __init__(self, args, out_shapes, sem_counts, plan, in_specs=None, spread=True):
        self.args, self.out_shapes, self.plan = list(args), list(out_shapes), plan
        self.sem_counts = sem_counts
        self.in_specs = in_specs or [ANY] * len(self.args)
        self.spread = spread

    def scratch(self):
        return [pltpu.SemaphoreType.DMA((n,)) for n in self.sem_counts]

    def run(self, step, total, in_refs, out_refs, sems):
        count = len(self.plan(in_refs, out_refs, *sems))
        at = [(k * (total - 1)) // (count - 1) if self.spread or k == 0 else total - 1 for k in range(count)]
        for when in sorted(set(at)):
            @pl.when(step == when)
            def _(when=when):
                stages = self.plan(in_refs, out_refs, *sems)
                for k in range(count):
                    if at[k] == when:
                        stages[k]()


def _mm(a, b, out_dtype, tm, tn, tk, name, mode="nn", ride=None):
    if mode == "tn":
        k, m = a.shape
    else:
        m, k = a.shape
    n = b.shape[0] if mode == "nt" else b.shape[1]
    nk = k // tk
    nj, ni = n // tn, m // tm
    n_in = len(ride.args) if ride else 0
    n_out = len(ride.out_shapes) if ride else 0

    def body(a_ref, b_ref, *rest):
        o_ref = rest[n_in]
        acc_ref = rest[n_in + 1 + n_out]
        if ride:
            j, i, kk = pl.program_id(0), pl.program_id(1), pl.program_id(2)
            ride.run((j * ni + i) * nk + kk, nj * ni * nk, rest[:n_in], rest[n_in + 1:n_in + 1 + n_out],
                     rest[n_in + 2 + n_out:])
        av = a_ref[...].astype(BF16)
        bv = b_ref[...].astype(BF16)
        part = _dot_tn(av, bv) if mode == "tn" else _dot_nt(av, bv) if mode == "nt" else _dot(av, bv)
        if nk == 1:
            o_ref[...] = part.astype(out_dtype)
        else:
            kk = pl.program_id(2)

            @pl.when(kk == 0)
            def _():
                acc_ref[...] = part

            @pl.when(kk > 0)
            def _():
                acc_ref[...] += part

            @pl.when(kk == nk - 1)
            def _():
                o_ref[...] = acc_ref[...].astype(out_dtype)

    a_spec = (pl.BlockSpec((tk, tm), lambda j, i, kk: (kk, i)) if mode == "tn"
              else pl.BlockSpec((tm, tk), lambda j, i, kk: (i, kk)))
    b_spec = (pl.BlockSpec((tn, tk), lambda j, i, kk: (j, kk)) if mode == "nt"
              else pl.BlockSpec((tk, tn), lambda j, i, kk: (kk, j)))
    o_spec = pl.BlockSpec((tm, tn), lambda j, i, kk: (i, j))
    o_shape = jax.ShapeDtypeStruct((m, n), out_dtype)
    if not ride:
        return pl.pallas_call(
            body, name=name, grid=(nj, ni, nk), out_shape=o_shape, in_specs=[a_spec, b_spec], out_specs=o_spec,
            scratch_shapes=[pltpu.VMEM((tm, tn), F32)],
            compiler_params=_params(("parallel", "parallel", "arbitrary")),
        )(a, b)
    return pl.pallas_call(
        body, name=name, grid=(nj, ni, nk),
        out_shape=(o_shape, *ride.out_shapes),
        in_specs=[a_spec, b_spec] + ride.in_specs,
        out_specs=(o_spec,) + (ANY,) * n_out,
        scratch_shapes=[pltpu.VMEM((tm, tn), F32)] + ride.scratch(),
        compiler_params=_params(("arbitrary", "arbitrary", "arbitrary")),
    )(a, b, *ride.args)


def _prep(proj, trig, w_uq, w_ukv, g_cq, g_ckv, rope_a, rope_b, scales, tm=512):
    t = proj.shape[0]
    sc_a, sc_b, sc_m = (s * LOG2E for s in scales)

    def body(aq_ref, ak_ref, av_ref, bs_ref, mq_ref, trig_ref, wuq_ref, wukv_ref, gcq_ref, gckv_ref,
             ra_ref, rb_ref, qa_ref, ka_ref, va_ref, qb_ref, kb_ref, vb_ref, qm_ref):
        ta = _rope_tables(trig_ref[:, 0:LANES], trig_ref[:, LANES:2 * LANES], ra_ref[...])
        tb = _rope_tables(trig_ref[:, 2 * LANES:3 * LANES], trig_ref[:, 3 * LANES:4 * LANES], rb_ref[...])
        for j in range(A_WIDTH // LANES):
            sl = slice(j * LANES, (j + 1) * LANES)
            qa_ref[:, sl] = (_rope(aq_ref[:, sl], ta, 8) * sc_a).astype(BF16)
            ka_ref[:, sl] = _rope(ak_ref[:, sl], ta, 8).astype(BF16)
        va_ref[...] = av_ref[...].astype(BF16)
        qm_ref[...] = (mq_ref[...] * sc_m).astype(BF16)

        cq_hat, _ = _rms_hat(bs_ref[:, 0:MLA_Q_RANK], MLA_Q_RANK)
        cqn = (cq_hat * gcq_ref[...]).astype(BF16)
        ckv_hat, _ = _rms_hat(bs_ref[:, MLA_Q_RANK:MLA_Q_RANK + MLA_KV_RANK], MLA_KV_RANK)
        ckvn = (ckv_hat * gckv_ref[...]).astype(BF16)
        qfull = _dot_nt(cqn, wuq_ref[...])
        kv = _dot(ckvn, wukv_ref[...])
        kr = _rope(bs_ref[:, 384:512], tb, 16)
        lane = lax.broadcasted_iota(jnp.int32, (1, LANES), 1)
        low = lane < 64
        for h in range(MLA_HEADS):
            sl = slice(h * LANES, (h + 1) * LANES)
            qb_ref[:, sl] = (_rope(qfull[:, sl], tb, 16) * sc_b).astype(BF16)
            kb_ref[:, sl] = jnp.where(low, kv[:, sl], kr).astype(BF16)
            vb_ref[:, sl] = jnp.where(low, 0.0, kv[:, sl]).astype(BF16)

    def col(width, idx):
        return pl.BlockSpec((tm, width), lambda i: (i, idx))

    def full(shape):
        return pl.BlockSpec(shape, lambda i: (0, 0))

    wide = jax.ShapeDtypeStruct((t, 1024), BF16)
    return pl.pallas_call(
        body, name="prep", grid=(t // tm,),
        out_shape=(wide, wide, wide, wide, wide, wide,
                   jax.ShapeDtypeStruct((t, MEM_WIDTH), BF16)),
        in_specs=[col(1024, 0), col(1024, 1), col(1024, 2), col(512, COL_CQ // 512), col(512, COL_MQ // 512),
                  pl.BlockSpec((tm, 4 * LANES), lambda i: (i, 0)),
                  full((1024, MLA_Q_RANK)), full((MLA_KV_RANK, 1024)),
                  full((1, MLA_Q_RANK)), full((1, MLA_KV_RANK)), full((8, LANES)), full((8, LANES))],
        out_specs=(col(1024, 0),) * 6 + (col(MEM_WIDTH, 0),),
        compiler_params=_params(("parallel",)),
    )(proj, proj, proj, proj, proj, trig, w_uq, w_ukv, g_cq, g_ckv, rope_a, rope_b)


def _attn_fwd(q, k, v, *, nb, s, sk, heads, hpb, voff, bq, name):
    nq = s // bq
    width = hpb * LANES
    vblk = voff // hpb

    def body(q_ref, k_ref, v_ref, o_ref, lse_ref):
        for h in range(hpb):
            sl = slice(h * LANES, (h + 1) * LANES)
            sc = _dot_nt(q_ref[:, sl], k_ref[:, sl])
            m = jnp.max(sc, axis=1, keepdims=True)
            p = jnp.exp2(sc - m)
            l = jnp.sum(p, axis=1, keepdims=True)
            o_ref[:, sl] = _dot(p.astype(BF16), v_ref[:, sl]) / l
            lse_ref[:, sl] = jnp.broadcast_to(m + jnp.log(l) * LOG2E, (bq, LANES))

    out = jax.ShapeDtypeStruct((nb * s, heads * LANES), F32)
    ospec = pl.BlockSpec((bq, width), lambda b, i, g: (b * nq + i, g))
    return pl.pallas_call(
        body, name=name, grid=(nb, nq, heads // hpb),
        out_shape=(out, out),
        in_specs=[ospec, pl.BlockSpec((sk, width), lambda b, i, g: (b, g)),
                  pl.BlockSpec((sk, width), lambda b, i, g: (b, vblk + g))],
        out_specs=(ospec, ospec),
        compiler_params=_params(("parallel", "parallel", "parallel")),
    )(q, k, v)


def _attn_bwd(q, k, v, o, do, lse, *, nb, s, sk, heads, hpb, voff, scale, bq, name):
    nq = s // bq
    width = hpb * LANES
    vblk = voff // hpb

    def body(q_ref, k_ref, v_ref, o_ref, do_ref, lse_ref, dq_ref, dk_ref, dv_ref, dk_acc, dv_acc):
        i = pl.program_id(2)

        @pl.when(i == 0)
        def _():
            dk_acc[...] = jnp.zeros_like(dk_acc)
            dv_acc[...] = jnp.zeros_like(dv_acc)

        for h in range(hpb):
            sl = slice(h * LANES, (h + 1) * LANES)
            qh = q_ref[:, sl]
            kk = k_ref[:, sl]
            doh = do_ref[:, sl]
            delta = jnp.sum(doh.astype(F32) * o_ref[:, sl], axis=1, keepdims=True)
            p = jnp.exp2(_dot_nt(qh, kk) - lse_ref[:, h * LANES:h * LANES + 1])
            ds = (p * (_dot_nt(doh, v_ref[:, sl]) - delta)).astype(BF16)
            dq_ref[:, sl] = (_dot(ds, kk) * scale).astype(BF16)
            dk_acc[:, sl] += _dot_tn(ds, qh)
            dv_acc[:, sl] += _dot_tn(p.astype(BF16), doh)

        @pl.when(i == nq - 1)
        def _():
            dk_ref[...] = (dk_acc[...] * LN2).astype(BF16)
            dv_ref[...] = dv_acc[...].astype(BF16)

    qspec = pl.BlockSpec((bq, width), lambda b, g, i: (b * nq + i, g))
    kv_spec = pl.BlockSpec((sk, width), lambda b, g, i: (b, g))
    dq_shape = jax.ShapeDtypeStruct((nb * s, heads * LANES), BF16)
    dkv_shape = jax.ShapeDtypeStruct((nb * sk, heads * LANES), BF16)
    return pl.pallas_call(
        body, name=name, grid=(nb, heads // hpb, nq),
        out_shape=(dq_shape, dkv_shape, dkv_shape),
        in_specs=[qspec, kv_spec, pl.BlockSpec((sk, width), lambda b, g, i: (b, vblk + g)), qspec, qspec, qspec],
        out_specs=(qspec, kv_spec, kv_spec),
        scratch_shapes=[pltpu.VMEM((sk, width), F32), pltpu.VMEM((sk, width), F32)],
        compiler_params=_params(("parallel", "parallel", "arbitrary")),
    )(q, k, v, o, do, lse)


BAND_Q = 128
BAND_WIN = 256


def _band_start(i, s):
    return min(max(i * BAND_Q - 64, 0), s - BAND_WIN)


def _to_pattern_order(src_ref, dst_ref, stage_ref, s, d):
    length = s // d
    stage_ref[...] = src_ref[...].astype(F32)
    for r in range(d):
        dst_ref[r * length:(r + 1) * length, :] = stage_ref[pl.ds(r, length, stride=d), :].astype(dst_ref.dtype)


def _dilated_fwd(q, k, v, bias, bias_index, *, nb, s, name):
    nblk = s // BAND_Q
    npat = len(DILATED)

    def body(q_ref, k_ref, v_ref, bias_ref, o_ref, lse_ref, *rest):
        ordered = rest[:3 * (npat - 1)]
        stage_ref, op_ref, lp_ref, on_ref, ln_ref = rest[3 * (npat - 1):]
        lane = lax.broadcasted_iota(jnp.int32, (1, LANES), 1)
        first = lane < 64
        for p, (_, d) in enumerate(DILATED):
            if d == 1:
                qs, ks, vs = q_ref, k_ref, v_ref
            else:
                qs, ks, vs = ordered[3 * (p - 1):3 * p]
                for src, dst in ((q_ref, qs), (k_ref, ks), (v_ref, vs)):
                    _to_pattern_order(src, dst, stage_ref, s, d)
            for i in range(nblk):
                u0 = i * BAND_Q
                st = _band_start(i, s)
                qi = qs[u0:u0 + BAND_Q, :]
                kw = ks[st:st + BAND_WIN, :]
                vw = vs[st:st + BAND_WIN, :]
                zero = jnp.zeros_like(qi)
                q2 = jnp.concatenate([jnp.where(first, qi, zero), jnp.where(first, zero, qi)], axis=0)
                sc = _dot_nt(q2, kw)
                b = bias_ref[bias_index[p][i]]
                halves = []
                for h in range(2):
                    sh = sc[h * BAND_Q:(h + 1) * BAND_Q] + b
                    m = jnp.max(sh, axis=1, keepdims=True)
                    pr = jnp.exp2(sh - m)
                    l = jnp.sum(pr, axis=1, keepdims=True)
                    halves.append((pr.astype(BF16), l, m + jnp.log(l) * LOG2E))
                o2 = _dot(jnp.concatenate([halves[0][0], halves[1][0]], axis=0), vw)
                o_blk = jnp.where(first, o2[:BAND_Q] / halves[0][1], o2[BAND_Q:] / halves[1][1])
                lse_blk = jnp.where(first, jnp.broadcast_to(halves[0][2], (BAND_Q, LANES)),
                                    jnp.broadcast_to(halves[1][2], (BAND_Q, LANES)))
                op_ref[p, u0:u0 + BAND_Q, :] = o_blk
                lp_ref[p, u0:u0 + BAND_Q, :] = lse_blk
            if d > 1:
                length = s // d
                for r in range(d):
                    on_ref.at[p - 1][pl.ds(r, length, stride=d), :] = op_ref[p, r * length:(r + 1) * length, :]
                    ln_ref.at[p - 1][pl.ds(r, length, stride=d), :] = lp_ref[p, r * length:(r + 1) * length, :]
        lses = [lp_ref[0]] + [ln_ref[p] for p in range(npat - 1)]
        outs = [op_ref[0]] + [on_ref[p] for p in range(npat - 1)]
        m = functools.reduce(jnp.maximum, lses)
        ws = [jnp.exp2(l - m) for l in lses]
        den = functools.reduce(lambda a, c: a + c, ws)
        o_ref[...] = functools.reduce(lambda a, c: a + c, [w * o for w, o in zip(ws, outs)]) / den
        lse_ref[...] = m + jnp.log(den) * LOG2E

    blk = pl.BlockSpec((s, LANES), lambda b, g: (b, g))
    out = jax.ShapeDtypeStruct((nb * s, A_WIDTH), F32)
    copy = jax.ShapeDtypeStruct((nb * s, A_WIDTH), BF16)
    n_copies = 3 * (npat - 1)
    res = pl.pallas_call(
        body, name=name, grid=(nb, A_WIDTH // LANES),
        out_shape=(out, out) + (copy,) * n_copies,
        in_specs=[blk, blk, blk, pl.BlockSpec(bias.shape, lambda b, g: (0, 0, 0))],
        out_specs=(blk, blk) + (blk,) * n_copies,
        scratch_shapes=[pltpu.VMEM((s, LANES), F32), pltpu.VMEM((npat, s, LANES), F32),
                        pltpu.VMEM((npat, s, LANES), F32), pltpu.VMEM((npat - 1, s, LANES), F32),
                        pltpu.VMEM((npat - 1, s, LANES), F32)],
        compiler_params=_params(("parallel", "parallel")),
    )(q, k, v, bias)
    return res[0], res[1], res[2:]


def _dilated_bwd(q, k, v, ordered, o, do, lse, bias, bias_index, *, nb, s, scale, name):
    nblk = s // BAND_Q
    npat = len(DILATED)
    n_copies = 3 * (npat - 1)

    def body(q_ref, k_ref, v_ref, *rest):
        ordered_refs = rest[:n_copies]
        (o_ref, do_ref, lse_ref, bias_ref, dq_out, dk_out, dv_out, stage_ref, rs_ref, dop_ref, rsp_ref,
         dqp_ref, dkp_ref, dvp_ref, dq_ref, dk_ref, dv_ref, nat_ref) = rest[n_copies:]
        lane = lax.broadcasted_iota(jnp.int32, (1, LANES), 1)
        first = lane < 64
        prod = do_ref[...].astype(F32) * o_ref[...]
        d0 = jnp.sum(jnp.where(first, prod, 0.0), axis=1, keepdims=True)
        d1 = jnp.sum(jnp.where(first, 0.0, prod), axis=1, keepdims=True)
        delta = jnp.where(first, jnp.broadcast_to(d0, (s, LANES)), jnp.broadcast_to(d1, (s, LANES)))
        rs_ref[...] = jnp.where((lane & 32) == 0, lse_ref[...], delta)
        for p, (_, d) in enumerate(DILATED):
            length = s // d
            if d == 1:
                qs, ks, vs, dos, rss = q_ref, k_ref, v_ref, do_ref, rs_ref
                dqs, dks, dvs = dq_ref, dk_ref, dv_ref
            else:
                for src, dst in ((do_ref, dop_ref), (rs_ref, rsp_ref)):
                    _to_pattern_order(src, dst, stage_ref, s, d)
                qs, ks, vs = ordered_refs[3 * (p - 1):3 * p]
                dos, rss = dop_ref, rsp_ref
                dqs, dks, dvs = dqp_ref, dkp_ref, dvp_ref
            dks[...] = jnp.zeros((s, LANES), F32)
            dvs[...] = jnp.zeros((s, LANES), F32)
            for i in range(nblk):
                u0 = i * BAND_Q
                st = _band_start(i, s)
                qi = qs[u0:u0 + BAND_Q, :]
                doi = dos[u0:u0 + BAND_Q, :]
                kw = ks[st:st + BAND_WIN, :]
                vw = vs[st:st + BAND_WIN, :]
                zero = jnp.zeros_like(qi)
                q2 = jnp.concatenate([jnp.where(first, qi, zero), jnp.where(first, zero, qi)], axis=0)
                do2 = jnp.concatenate([jnp.where(first, doi, zero), jnp.where(first, zero, doi)], axis=0)
                sc = _dot_nt(q2, kw)
                dp = _dot_nt(do2, vw)
                b = bias_ref[bias_index[p][i]]
                rs_i = rss[u0:u0 + BAND_Q, :]
                ps, dss = [], []
                for h in range(2):
                    rows = slice(h * BAND_Q, (h + 1) * BAND_Q)
                    pr = jnp.exp2(sc[rows] + b - rs_i[:, 64 * h:64 * h + 1])
                    ps.append(pr.astype(BF16))
                    dss.append((pr * (dp[rows] - rs_i[:, 64 * h + 32:64 * h + 33])).astype(BF16))
                p2 = jnp.concatenate(ps, axis=0)
                ds2 = jnp.concatenate(dss, axis=0)
                dq2 = _dot(ds2, kw)
                dqs[u0:u0 + BAND_Q, :] = jnp.where(first, dq2[:BAND_Q], dq2[BAND_Q:]) * scale
                dks[st:st + BAND_WIN, :] += _dot_tn(ds2, q2)
                dvs[st:st + BAND_WIN, :] += _dot_tn(p2, do2)
            if d > 1:
                for j, src in enumerate((dqp_ref, dkp_ref, dvp_ref)):
                    for r in range(d):
                        nat_ref.at[p - 1, j][pl.ds(r, length, stride=d), :] = src[r * length:(r + 1) * length, :]

        def total(j, first_ref):
            return functools.reduce(lambda a, c: a + c, [first_ref[...]] + [nat_ref[p, j] for p in range(npat - 1)])

        dq_out[...] = total(0, dq_ref).astype(BF16)
        dk_out[...] = (total(1, dk_ref) * LN2).astype(BF16)
        dv_out[...] = total(2, dv_ref).astype(BF16)

    blk = pl.BlockSpec((s, LANES), lambda b, g: (b, g))
    out = jax.ShapeDtypeStruct((nb * s, A_WIDTH), BF16)
    f32_buf = pltpu.VMEM((s, LANES), F32)
    bf_buf = pltpu.VMEM((s, LANES), BF16)
    return pl.pallas_call(
        body, name=name, grid=(nb, A_WIDTH // LANES),
        out_shape=(out, out, out),
        in_specs=[blk] * (6 + n_copies) + [pl.BlockSpec(bias.shape, lambda b, g: (0, 0, 0))],
        out_specs=(blk, blk, blk),
        scratch_shapes=[f32_buf, f32_buf, bf_buf] + [f32_buf] * 7 + [pltpu.VMEM((npat - 1, 3, s, LANES), F32)],
        compiler_params=_params(("parallel", "parallel")),
    )(q, k, v, *ordered, o, do, lse, bias)


def _post(h32, ya, ybp, ym, proj, target, w_out, g_a, g_b, g_m, g_post, b_post, tm=256):
    t = h32.shape[0]

    def body(h_ref, ya_ref, yb_ref, ym_ref, ga_ref, gb_ref, gm_ref, tg_ref, wo_ref,
             goa_ref, gob_ref, gom_ref, gp_ref, bp_ref,
             y_ref, dz_ref, doa_ref, dob_ref, dom_ref, dga_ref, dgb_ref, dgm_ref,
             loss_ref, dgp_ref, dbp_ref, dgoa_ref, dgob_ref, dgom_ref):
        i = pl.program_id(0)

        @pl.when(i == 0)
        def _():
            for r in (loss_ref, dgp_ref, dbp_ref, dgoa_ref, dgob_ref, dgom_ref):
                r[...] = jnp.zeros_like(r)

        lane = lax.broadcasted_iota(jnp.int32, (1, LANES), 1)
        low = lane < 64
        h = h_ref[...]

        ybp_v = yb_ref[...]
        yb = jnp.concatenate(
            [jnp.where(low, pltpu.roll(ybp_v[:, 2 * j * LANES:(2 * j + 1) * LANES], 64, 1),
                       ybp_v[:, (2 * j + 1) * LANES:(2 * j + 2) * LANES]) for j in range(4)], axis=1)

        def gated(raw, gate, gain, width):
            xh, r = _rms_hat(raw, width)
            n = xh * gain
            sg = 1.0 / (1.0 + jnp.exp(-gate))
            return xh, r, n, sg, n * (gate * sg)

        gate_a, gate_b, gate_m = ga_ref[...], gb_ref[...], gm_ref[...]
        xh_a, r_a, n_a, sg_a, y_a = gated(ya_ref[...], gate_a, goa_ref[...], A_WIDTH)
        xh_b, r_b, n_b, sg_b, y_b = gated(yb, gate_b, gob_ref[...], 512)
        xh_m, r_m, n_m, sg_m, y_m = gated(ym_ref[...], gate_m, gom_ref[...], 512)
        y = jnp.concatenate([y_a, y_b, y_m], axis=1).astype(BF16)
        y_ref[...] = y
        z = DEEPNORM_ALPHA * h + _dot(y, wo_ref[...])
        zh, rstd = _ln_hat(z)
        err = zh * gp_ref[...] + bp_ref[...] - tg_ref[...]
        rows = jnp.sum(err * err, axis=1, keepdims=True)
        loss_ref[...] += jnp.broadcast_to(jnp.sum(rows, axis=0, keepdims=True) * (0.5 / D_MODEL), (1, LANES))
        dout = err * (1.0 / D_MODEL)
        dgp_ref[...] += _colsum(dout * zh)
        dbp_ref[...] += _colsum(dout)
        dz = _ln_bwd_rows(dout * gp_ref[...], zh, rstd)
        dz_ref[...] = dz
        dy = _dot_nt(dz.astype(BF16), wo_ref[...])

        def gated_bwd(dyg, xh, r, n, sg, gate, gain, width, dgain_ref):
            dn = dyg * (gate * sg)
            dgate = dyg * n * (sg * (1.0 + gate * (1.0 - sg)))
            dgain_ref[...] += _colsum(dn * xh)
            return _rms_bwd(dn * gain, xh, r, width), dgate

        dya, dgate_a = gated_bwd(dy[:, 0:1024], xh_a, r_a, n_a, sg_a, gate_a, goa_ref[...], A_WIDTH, dgoa_ref)
        dyb, dgate_b = gated_bwd(dy[:, 1024:1536], xh_b, r_b, n_b, sg_b, gate_b, gob_ref[...], 512, dgob_ref)
        dym, dgate_m = gated_bwd(dy[:, 1536:2048], xh_m, r_m, n_m, sg_m, gate_m, gom_ref[...], 512, dgom_ref)
        doa_ref[...] = dya.astype(BF16)
        dom_ref[...] = dym.astype(BF16)
        dga_ref[...] = dgate_a.astype(BF16)
        dgb_ref[...] = dgate_b.astype(BF16)
        dgm_ref[...] = dgate_m.astype(BF16)
        for j in range(4):
            blk = dyb[:, j * LANES:(j + 1) * LANES]
            dob_ref[:, 2 * j * LANES:(2 * j + 1) * LANES] = jnp.where(low, 0.0, pltpu.roll(blk, 64, 1)).astype(BF16)
            dob_ref[:, (2 * j + 1) * LANES:(2 * j + 2) * LANES] = jnp.where(low, 0.0, blk).astype(BF16)

    def col(width, idx):
        return pl.BlockSpec((tm, width), lambda i: (i, idx))

    def full(shape):
        return pl.BlockSpec(shape, lambda i: (0, 0))

    def acc(width):
        return jax.ShapeDtypeStruct((1, width), F32)

    return pl.pallas_call(
        body, name="post", grid=(t // tm,),
        out_shape=(jax.ShapeDtypeStruct((t, 2048), BF16), jax.ShapeDtypeStruct((t, 1024), F32),
                   jax.ShapeDtypeStruct((t, 1024), BF16), jax.ShapeDtypeStruct((t, 1024), BF16),
                   jax.ShapeDtypeStruct((t, 512), BF16),
                   jax.ShapeDtypeStruct((t, 1024), BF16), jax.ShapeDtypeStruct((t, 512), BF16),
                   jax.ShapeDtypeStruct((t, 512), BF16),
                   acc(LANES), acc(1024), acc(1024), acc(1024), acc(512), acc(512)),
        in_specs=[col(1024, 0), col(1024, 0), col(1024, 0), col(512, 0),
                  col(1024, 3), col(512, COL_BG // 512), col(512, COL_MG // 512), col(1024, 0),
                  full((2048, 1024)),
                  full((1, 1024)), full((1, 512)), full((1, 512)), full((1, 1024)), full((1, 1024))],
        out_specs=(col(2048, 0), col(1024, 0), col(1024, 0), col(1024, 0), col(512, 0),
                   col(1024, 0), col(512, 0), col(512, 0),
                   full((1, LANES)), full((1, 1024)), full((1, 1024)), full((1, 1024)), full((1, 512)),
                   full((1, 512))),
        compiler_params=_params(("arbitrary",)),
    )(h32, ya, ybp, ym, proj, proj, proj, target, w_out, g_a, g_b, g_m, g_post, b_post)


def _prep_bwd(dqa, dka, dva, dqb, dkb, dvb, dqm, dga, dgb, dgm, proj, trig, w_uq, w_ukv, g_cq, g_ckv,
              rope_a, rope_b, tm=512):
    t = proj.shape[0]

    def body(dqa_ref, dka_ref, dva_ref, dqb_ref, dkb_ref, dvb_ref, dqm_ref, dga_ref, dgb_ref, dgm_ref,
             bs_ref, trig_ref, wuq_ref, wukv_ref, gcq_ref, gckv_ref, ra_ref, rb_ref,
             dproj_ref, dwuq_ref, dwukv_ref, dgcq_ref, dgckv_ref, dqf_ref, dkv_ref):
        i = pl.program_id(0)

        @pl.when(i == 0)
        def _():
            dwuq_ref[...] = jnp.zeros_like(dwuq_ref)
            dwukv_ref[...] = jnp.zeros_like(dwukv_ref)
            dgcq_ref[...] = jnp.zeros_like(dgcq_ref)
            dgckv_ref[...] = jnp.zeros_like(dgckv_ref)

        ta = _rope_tables(trig_ref[:, 0:LANES], trig_ref[:, LANES:2 * LANES], ra_ref[...])
        tb = _rope_tables(trig_ref[:, 2 * LANES:3 * LANES], trig_ref[:, 3 * LANES:4 * LANES], rb_ref[...])
        for j in range(A_WIDTH // LANES):
            sl = slice(j * LANES, (j + 1) * LANES)
            dproj_ref[:, j * LANES:(j + 1) * LANES] = (
                _rope(dqa_ref[:, sl].astype(F32), ta, 8, inverse=True).astype(BF16))
            dproj_ref[:, 1024 + j * LANES:1024 + (j + 1) * LANES] = (
                _rope(dka_ref[:, sl].astype(F32), ta, 8, inverse=True).astype(BF16))
        dproj_ref[:, 2048:3072] = dva_ref[...]
        dproj_ref[:, 3072:4096] = dga_ref[...]

        lane = lax.broadcasted_iota(jnp.int32, (1, LANES), 1)
        low = lane < 64
        rope_lanes = (lane >= 64) & (lane < 96)
        dkr = jnp.zeros((tm, LANES), F32)
        for h in range(MLA_HEADS):
            sl = slice(h * LANES, (h + 1) * LANES)
            dqf_ref[:, sl] = _rope(dqb_ref[:, sl].astype(F32), tb, 16, inverse=True).astype(BF16)
            dk_h = dkb_ref[:, sl]
            dkv_ref[:, sl] = jnp.where(low, dk_h, dvb_ref[:, sl])
            dkr = dkr + jnp.where(rope_lanes, dk_h.astype(F32), 0.0)
        dkr = _rope(dkr, tb, 16, inverse=True)

        cq_hat, r_q = _rms_hat(bs_ref[:, 0:MLA_Q_RANK], MLA_Q_RANK)
        dwuq_ref[...] += _dot_tn(dqf_ref[...], (cq_hat * gcq_ref[...]).astype(BF16))
        dcqn = _dot(dqf_ref[...], wuq_ref[...])
        dgcq_ref[...] += _colsum(dcqn * cq_hat)
        dproj_ref[:, COL_CQ:COL_CQ + 256] = _rms_bwd(dcqn * gcq_ref[...], cq_hat, r_q, MLA_Q_RANK).astype(BF16)
        ckv_hat, r_kv = _rms_hat(bs_ref[:, MLA_Q_RANK:MLA_Q_RANK + MLA_KV_RANK], MLA_KV_RANK)
        dwukv_ref[...] += _dot_tn((ckv_hat * gckv_ref[...]).astype(BF16), dkv_ref[...])
        dckvn = _dot_nt(dkv_ref[...], wukv_ref[...])
        dgckv_ref[...] += _colsum(dckvn * ckv_hat)
        dproj_ref[:, COL_CQ + 256:COL_CQ + 384] = (
            _rms_bwd(dckvn * gckv_ref[...], ckv_hat, r_kv, MLA_KV_RANK).astype(BF16))
        dproj_ref[:, COL_CQ + 384:COL_CQ + 512] = dkr.astype(BF16)
        dproj_ref[:, COL_BG:COL_BG + 512] = dgb_ref[...]
        dproj_ref[:, COL_MQ:COL_MQ + 512] = dqm_ref[...]
        dproj_ref[:, COL_MG:COL_MG + 512] = dgm_ref[...]

    def col(width, idx):
        return pl.BlockSpec((tm, width), lambda i: (i, idx))

    def full(shape):
        return pl.BlockSpec(shape, lambda i: (0, 0))

    return pl.pallas_call(
        body, name="prep_bwd", grid=(t // tm,),
        out_shape=(jax.ShapeDtypeStruct((t, PROJ_W), BF16), jax.ShapeDtypeStruct((1024, MLA_Q_RANK), F32),
                   jax.ShapeDtypeStruct((MLA_KV_RANK, 1024), F32),
                   jax.ShapeDtypeStruct((1, MLA_Q_RANK), F32), jax.ShapeDtypeStruct((1, MLA_KV_RANK), F32)),
        in_specs=[col(1024, 0)] * 6 + [col(512, 0), col(1024, 0), col(512, 0), col(512, 0),
                  col(512, COL_CQ // 512), pl.BlockSpec((tm, 4 * LANES), lambda i: (i, 0)),
                  full((1024, MLA_Q_RANK)), full((MLA_KV_RANK, 1024)),
                  full((1, MLA_Q_RANK)), full((1, MLA_KV_RANK)), full((8, LANES)), full((8, LANES))],
        out_specs=(col(PROJ_W, 0), full((1024, MLA_Q_RANK)), full((MLA_KV_RANK, 1024)),
                   full((1, MLA_Q_RANK)), full((1, MLA_KV_RANK))),
        scratch_shapes=[pltpu.VMEM((tm, 1024), BF16), pltpu.VMEM((tm, 1024), BF16)],
        compiler_params=_params(("arbitrary",)),
    )(dqa, dka, dva, dqb, dkb, dvb, dqm, dga, dgb, dgm, proj, trig, w_uq, w_ukv, g_cq, g_ckv, rope_a, rope_b)


def _adamw_math(gv, w, m, v):
    m_new = ADAM_B1 * m + (1.0 - ADAM_B1) * gv
    v_new = ADAM_B2 * v + (1.0 - ADAM_B2) * (gv * gv)
    m_hat = m_new / (1.0 - ADAM_B1 ** ADAM_STEP)
    v_hat = v_new / (1.0 - ADAM_B2 ** ADAM_STEP)
    return -ADAM_LR * (m_hat / (jnp.sqrt(v_hat) + ADAM_EPS) + ADAM_WD * w), m_new, v_new


def _adamw(g, w, m, v, tr, name):
    r, cols = w.shape

    def body(g_ref, w_ref, m_ref, v_ref, go_ref, d_ref, nm_ref, nv_ref):
        gv = g_ref[...]
        go_ref[...] = gv
        d_ref[...], nm_ref[...], nv_ref[...] = _adamw_math(gv, w_ref[...], m_ref[...], v_ref[...])

    tile = pl.BlockSpec((tr, cols), lambda i: (i, 0))
    shape = jax.ShapeDtypeStruct((r, cols), F32)
    return pl.pallas_call(
        body, name=name, grid=(r // tr,),
        out_shape=(shape,) * 4, in_specs=[tile] * 4, out_specs=(tile,) * 4,
        compiler_params=_params(("parallel",)),
    )(g, w, m, v)


def _adamw_pieces(g, w, m, v, pieces, name):
    n = len(pieces)
    per_piece = isinstance(w, (list, tuple))
    shapes = [jax.ShapeDtypeStruct((r1 - r0, c1 - c0), F32) for r0, r1, c0, c1 in pieces]
    args = (g, *w, *m, *v) if per_piece else (g, w, m, v)

    def body(g_ref, *refs):
        ins, outs = refs[:len(args) - 1], refs[len(args) - 1:]
        gv = g_ref[...]
        if not per_piece:
            results = (gv,) + _adamw_math(gv, ins[0][...], ins[1][...], ins[2][...])
        for p, (r0, r1, c0, c1) in enumerate(pieces):
            if per_piece:
                gp = gv[r0:r1, c0:c1]
                vals = (gp,) + _adamw_math(gp, ins[p][...], ins[n + p][...], ins[2 * n + p][...])
            else:
                vals = [full[r0:r1, c0:c1] for full in results]
            for kind, val in enumerate(vals):
                outs[kind * n + p][...] = val

    flat = pl.pallas_call(
        body, name=name, out_shape=tuple(shapes) * 4,
        in_specs=[IN_VMEM] * len(args), out_specs=tuple([IN_VMEM] * (4 * n)),
        compiler_params=_params(None),
    )(*args)
    return [[flat[kind * n + p] for kind in range(4)] for p in range(n)]


def _core_sum(g, recv, core, rows, tr, name, ride=None):
    cols = g.shape[2]
    nblk = rows // tr
    n_in = len(ride.args) if ride else 0
    n_out = len(ride.out_shapes) if ride else 0

    def body(c_ref, g_ref, r_ref, *rest):
        sf_ref, sb_ref = rest[n_in], rest[n_in + 1]
        if ride:
            j, i = pl.program_id(0), pl.program_id(1)
            ride.run(j * nblk + i, 4 * nblk, rest[:n_in], rest[n_in + 2:n_in + 2 + n_out],
                     rest[n_in + 2 + n_out:])
        tot = g_ref[...] + r_ref[...]
        sf_ref[...] = tot
        sb_ref[...] = tot.astype(BF16)

    half = pl.BlockSpec((None, tr, cols), lambda j, i, c_ref: (j, i, 0))
    shapes = (jax.ShapeDtypeStruct((4, rows, cols), F32), jax.ShapeDtypeStruct((4, rows, cols), BF16))
    return pl.pallas_call(
        body, name=name,
        grid_spec=pltpu.PrefetchScalarGridSpec(
            num_scalar_prefetch=1, grid=(4, nblk),
            in_specs=[pl.BlockSpec((None, tr, cols), lambda j, i, c_ref: (j, c_ref[0] * nblk + i, 0)), half]
            + (ride.in_specs if ride else []),
            out_specs=(half, half) + (ANY,) * n_out,
            scratch_shapes=ride.scratch() if ride else []),
        out_shape=shapes + tuple(ride.out_shapes if ride else ()),
        compiler_params=_params(("arbitrary", "arbitrary") if ride else ("parallel", "parallel")),
    )(core, g, recv, *(ride.args if ride else ()))


def _half_to_sibling(g4):
    def plan(in_refs, out_refs, send_sems, recv_sems):
        x, y, c = _position()
        cp = pltpu.make_async_remote_copy(
            src_ref=in_refs[0].at[:, 1 - c], dst_ref=out_refs[0], send_sem=send_sems.at[0],
            recv_sem=recv_sems.at[0], device_id=(x, y, 1 - c), device_id_type=MESH)

        def finish():
            cp.wait_recv()
            cp.wait_send()

        return cp.start, finish

    return _Ride([g4], [jax.ShapeDtypeStruct((4, g4.shape[2], 1024), F32)], (1, 1), plan)


def _gather_plan(src_ref, dst_ref, send_sems, recv_sems, local_sems):
    x, y, c = _position()
    me = 2 * x + y
    rows = src_ref.shape[1]
    cut = -(-rows // 32) * 16
    pieces = (pl.ds(0, cut), pl.ds(cut, rows - cut))
    local = pltpu.make_async_copy(src_ref, dst_ref.at[me], local_sems.at[0])

    def over_ici(sem, k, chip, t, src=None):
        where = dst_ref.at[chip, c, pieces[t]]
        return pltpu.make_async_remote_copy(
            src_ref=where if src is None else src, dst_ref=where, send_sem=send_sems.at[sem],
            recv_sem=recv_sems.at[sem], device_id=(x ^ (k >> 1), y ^ (k & 1), c), device_id_type=MESH)

    def mine_to(k, t):
        return over_ici(2 * (k - 1) + t, k, me, t, src=src_ref.at[c, pieces[t]])

    def from_neighbour(k, t):
        return over_ici(2 * (k - 1) + t, k, me ^ k, t)

    def to_sibling(k, half):
        piece = dst_ref.at[me ^ k, half]
        return pltpu.make_async_remote_copy(
            src_ref=piece, dst_ref=piece, send_sem=send_sems.at[5 + k], recv_sem=recv_sems.at[5 + k],
            device_id=(x, y, 1 - c), device_id_type=MESH)

    sends = [mine_to(2, 0), mine_to(1, 1), mine_to(2, 1), mine_to(1, 0)]
    onward = [over_ici(4, 1, me ^ 2, 0), over_ici(5, 2, me ^ 1, 1)]

    def start():
        local.start()
        for cp in sends:
            cp.start()

    def pass_on():
        from_neighbour(2, 0).wait_recv()
        onward[0].start()
        from_neighbour(1, 1).wait_recv()
        onward[1].start()

    def to_other_core():
        from_neighbour(2, 1).wait_recv()
        to_sibling(2, c).start()
        from_neighbour(1, 0).wait_recv()
        to_sibling(1, c).start()
        over_ici(4, 1, me ^ 3, 0).wait_recv()
        over_ici(5, 2, me ^ 3, 1).wait_recv()
        to_sibling(3, c).start()

    def finish():
        for k in (1, 2, 3):
            to_sibling(k, 1 - c).wait_recv()
        for cp in sends + onward + [to_sibling(k, c) for k in (1, 2, 3)]:
            cp.wait_send()
        local.wait()

    return start, pass_on, to_other_core, finish


def _gather_ride(shard, spread):
    def plan(in_refs, out_refs, send_sems, recv_sems, local_sems):
        return _gather_plan(in_refs[0], out_refs[0], send_sems, recv_sems, local_sems)

    return _Ride([shard], [jax.ShapeDtypeStruct((4,) + shard.shape, shard.dtype)], (9, 9, 1), plan,
                 in_specs=[IN_VMEM], spread=spread)


def _chip_sum(sf, recv, chip, rows, tr, name):
    cols = sf.shape[2]
    n_recv = recv.shape[0]

    def body(me_ref, sf_ref, r_ref, out_ref):
        acc = sf_ref[...]
        for k in range(n_recv):
            acc = acc + r_ref[k].astype(F32)
        out_ref[...] = acc

    return pl.pallas_call(
        body, name=name,
        grid_spec=pltpu.PrefetchScalarGridSpec(
            num_scalar_prefetch=1, grid=(rows // tr,),
            in_specs=[pl.BlockSpec((None, tr, cols), lambda i, me_ref: (me_ref[0], i, 0)),
                      pl.BlockSpec((n_recv, tr, cols), lambda i, me_ref: (0, i, 0))],
            out_specs=pl.BlockSpec((tr, cols), lambda i, me_ref: (i, 0))),
        out_shape=jax.ShapeDtypeStruct((rows, cols), F32),
        compiler_params=_params(("parallel",)),
    )(chip, sf, recv)


def _position():
    return lax.axis_index("x"), lax.axis_index("y"), lax.axis_index("c")


def _dh_scatter(dproj, w_in_arr_t, x, dz, g, sb_in, sb_rest, tm=512, tk=3072):
    t, d = x.shape
    nk = dproj.shape[1] // tk
    ni = t // tm
    total = ni * nk
    halves = (HALF_IN, HALF_REST)
    cuts = tuple(-(-rows // 32) * 16 for rows in halves)

    def rows_of(a, p):
        return cuts[a] if p == 0 else halves[a] - cuts[a]

    def piece(a, p):
        return pl.ds(0, cuts[a]) if p == 0 else pl.ds(cuts[a], halves[a] - cuts[a])

    def body(dp_ref, w_ref, x_ref, dz_ref, g_ref, sbin_ref, sbrest_ref, dx_ref, dg_ref, db_ref, rin_ref, rrest_ref,
             acc_ref, pay_in0, pay_in1, pay_rest0, pay_rest1, own_in0, own_in1, own_rest0, own_rest1,
             send_sems, recv_sems, local_sems):
        step = pl.program_id(0) * nk + pl.program_id(1)
        kk = pl.program_id(1)
        px, py, pc = _position()
        me = 2 * px + py
        srcs = (sbin_ref, sbrest_ref)
        dsts = (rin_ref, rrest_ref)
        pays = ((pay_in0, pay_in1), (pay_rest0, pay_rest1))
        owns = ((own_in0, own_in1), (own_rest0, own_rest1))
        via = (2, 1)
        onto = (1, 2)

        def peer(k):
            return (px ^ (k >> 1), py ^ (k & 1), pc)

        def payload(a, p):
            return pltpu.make_async_remote_copy(
                src_ref=srcs[a].at[me ^ 3, piece(a, p)], dst_ref=pays[a][p], send_sem=send_sems.at[2 * a + p],
                recv_sem=recv_sems.at[2 * a + p], device_id=peer(via[p]), device_id_type=MESH)

        def direct(a, k, p, src):
            sem = 4 + 4 * a + 2 * (k - 1) + p
            return pltpu.make_async_remote_copy(
                src_ref=src, dst_ref=dsts[a].at[k - 1, piece(a, p)], send_sem=send_sems.at[sem],
                recv_sem=recv_sems.at[sem], device_id=peer(k), device_id_type=MESH)

        def plain(a, k, p):
            return direct(a, k, p, srcs[a].at[me ^ k, piece(a, p)])

        def stage(a, p):
            return pltpu.make_async_copy(srcs[a].at[me ^ onto[p], piece(a, p)], owns[a][p], local_sems.at[2 * a + p])

        @pl.when(step == 0)
        def _():
            dg_ref[...] = jnp.zeros_like(dg_ref)
            db_ref[...] = jnp.zeros_like(db_ref)
            for a in range(2):
                for p in range(2):
                    payload(a, p).start()
                    stage(a, p).start()
                plain(a, 1, 1).start()
                plain(a, 2, 0).start()

        @pl.when(step == (5 * total) // 8)
        def _():
            for a in range(2):
                for p in range(2):
                    payload(a, p).wait_recv()
                    stage(a, p).wait()
                    owns[a][p][...] = (owns[a][p][...].astype(F32) + pays[a][p][...].astype(F32)).astype(BF16)
                    direct(a, onto[p], p, owns[a][p]).start()

        part = _dot(dp_ref[...], w_ref[...])

        @pl.when(kk == 0)
        def _():
            acc_ref[...] = part

        @pl.when(kk > 0)
        def _():
            acc_ref[...] += part

        @pl.when(kk == nk - 1)
        def _():
            xh, rstd = _ln_hat(x_ref[...])
            dht = acc_ref[...] + DEEPNORM_ALPHA * dz_ref[...]
            dg_ref[...] += _colsum(dht * xh)
            db_ref[...] += _colsum(dht)
            dx_ref[...] = _ln_bwd_rows(dht * g_ref[...], xh, rstd)

        @pl.when(step == total - 1)
        def _():
            for a in range(2):
                for k in (1, 2):
                    for p in range(2):
                        plain(a, k, p).wait_recv()
            for a in range(2):
                for p in range(2):
                    payload(a, p).wait_send()
                    direct(a, onto[p], p, owns[a][p]).wait_send()
                plain(a, 1, 1).wait_send()
                plain(a, 2, 0).wait_send()

    tile = pl.BlockSpec((tm, d), lambda i, kk: (i, 0))
    row = pl.BlockSpec((1, d), lambda i, kk: (0, 0))
    pieces = [pltpu.VMEM((rows_of(a, p), 1024), BF16) for a in range(2) for p in range(2)]
    return pl.pallas_call(
        body, name="dh_scatter", grid=(ni, nk),
        out_shape=(jax.ShapeDtypeStruct((t, d), F32), jax.ShapeDtypeStruct((1, d), F32),
                   jax.ShapeDtypeStruct((1, d), F32),
                   jax.ShapeDtypeStruct((2, HALF_IN, 1024), BF16),
                   jax.ShapeDtypeStruct((2, HALF_REST, 1024), BF16)),
        in_specs=[pl.BlockSpec((tm, tk), lambda i, kk: (i, kk)), pl.BlockSpec((tk, d), lambda i, kk: (kk, 0)),
                  tile, tile, row, ANY, ANY],
        out_specs=(tile, row, row, ANY, ANY),
        scratch_shapes=[pltpu.VMEM((tm, d), F32)] + pieces + pieces
        + [pltpu.SemaphoreType.DMA((12,)), pltpu.SemaphoreType.DMA((12,)), pltpu.SemaphoreType.DMA((4,))],
        compiler_params=_params(("arbitrary", "arbitrary")),
    )(dproj, w_in_arr_t, x, dz, g, sb_in, sb_rest)


def _join_and_allreduce(gh_in, gh_rest, vec):
    def body(hin_ref, hrest_ref, vec_ref, oin_ref, orest_ref, sum_ref, all_ref, send_sems, recv_sems, local_sems):
        x, y, c = _position()
        srcs = (hin_ref, hrest_ref)
        dsts = (oin_ref, orest_ref)
        me = 4 * x + 2 * y + c
        all_ref[me] = vec_ref[...]

        def small(k, slot):
            return pltpu.make_async_remote_copy(
                src_ref=vec_ref, dst_ref=all_ref.at[slot], send_sem=send_sems.at[k + 1], recv_sem=recv_sems.at[k + 1],
                device_id=(x ^ (k >> 2), y ^ ((k >> 1) & 1), c ^ (k & 1)), device_id_type=MESH)

        def half(a, slot):
            return pltpu.make_async_remote_copy(
                src_ref=srcs[a], dst_ref=dsts[a].at[slot], send_sem=send_sems.at[a], recv_sem=recv_sems.at[a],
                device_id=(x, y, 1 - c), device_id_type=MESH)

        local = [pltpu.make_async_copy(srcs[a], dsts[a].at[c], local_sems.at[a]) for a in range(2)]
        remote = [half(a, c) for a in range(2)] + [small(k, me) for k in range(1, 8)]
        for cp in local + remote:
            cp.start()
        for k in range(1, 8):
            small(k, me ^ k).wait_recv()
        for a in range(2):
            half(a, 1 - c).wait_recv()
        for cp in remote:
            cp.wait_send()
        for cp in local:
            cp.wait()
        total = all_ref[0]
        for d in range(1, 8):
            total = total + all_ref[d]
        sum_ref[...] = total

    return pl.pallas_call(
        body, name="join_halves",
        out_shape=(jax.ShapeDtypeStruct((2, HALF_IN, 1024), F32),
                   jax.ShapeDtypeStruct((2, HALF_REST, 1024), F32),
                   jax.ShapeDtypeStruct(vec.shape, vec.dtype)),
        in_specs=[IN_VMEM, IN_VMEM, IN_VMEM], out_specs=(ANY, ANY, IN_VMEM),
        scratch_shapes=[pltpu.VMEM((8,) + vec.shape, vec.dtype), pltpu.SemaphoreType.DMA((9,)),
                        pltpu.SemaphoreType.DMA((9,)), pltpu.SemaphoreType.DMA((2,))],
    )(gh_in, gh_rest, vec)


def _pack_rest(w_uq, w_ukv, w_mem, w_out):
    rows = jnp.concatenate([w_uq[0].T.reshape(-1, 1024), w_ukv.reshape(-1, 1024), w_mem.reshape(-1, 1024),
                            w_out.reshape(-1, 1024)], axis=0)
    return jnp.pad(rows, ((0, ROWS_REST - ROWS_USED), (0, 0)))


def _arranged_w_in(g_in):
    z = functools.partial(jnp.zeros, dtype=g_in.dtype)
    cut = 4480 - 2 * SHARD_ROWS
    return jnp.concatenate(
        [g_in[0, :SHARD_ROWS], g_in[1, :SHARD_ROWS], g_in[2, :cut], z((64, 1024)), g_in[2, cut:cut + 32],
         z((32, 1024)), g_in[2, cut + 32:SHARD_ROWS], g_in[3, :SHARD_ROWS]], axis=0)


def _rest_weights(g_rest):
    w_uq_t = g_rest[:, 0:ROWS_UQ].reshape(768, 256)
    w_uq_pad_t = jnp.pad(w_uq_t.reshape(MLA_HEADS, MLA_QK_DIM, 256), ((0, 0), (0, 32), (0, 0))).reshape(1024, 256)
    w_ukv = jnp.concatenate([g_rest[j, ROWS_UQ:ROWS_UQ + ROWS_UKV].reshape(128, 256) for j in range(4)], axis=1)
    lo = ROWS_UQ + ROWS_UKV
    w_mem = g_rest[:, lo:lo + ROWS_MEM].reshape(4 * ROWS_MEM, 1024)
    w_out = g_rest[:, lo + ROWS_MEM:lo + ROWS_MEM + ROWS_OUT].reshape(4 * ROWS_OUT, 1024)
    return w_uq_pad_t, w_ukv, w_mem, w_out


def _dw_in_split(dproj, h, tm=1024):
    t = dproj.shape[0]
    steps = PROJ_W // tm
    gap = ROWS_IN - SHARD_ROWS
    nat = 4608 - 96
    last = 4608 + 3 * SHARD_ROWS - nat
    segments = ((0, SHARD_ROWS, 0, 0), (SHARD_ROWS, 2 * SHARD_ROWS, 1, 0), (2 * SHARD_ROWS, 4480, 2, 0),
                (4544, 4576, 2, 4480 - 2 * SHARD_ROWS), (4608, last, 2, 4512 - 2 * SHARD_ROWS), (last, PROJ_W, 3, 0))

    def pieces(j):
        out = []
        for lo, hi, chip, dst in segments:
            a, b = max(lo, j * tm), min(hi, (j + 1) * tm)
            if a < b:
                out.append((a - j * tm, chip, dst + a - lo, b - a))
        return out

    n_sem = max(len(pieces(j)) for j in range(steps))

    def body(a_ref, b_ref, o_ref, tile_ref, zero_ref, sems, pad_sems):
        i = pl.program_id(0)

        def copies(j):
            return [pltpu.make_async_copy(tile_ref.at[j % 2, pl.ds(off, n)], o_ref.at[chip, pl.ds(dst, n)],
                                          sems.at[j % 2, q])
                    for q, (off, chip, dst, n) in enumerate(pieces(j))]

        def pad_copies():
            return [pltpu.make_async_copy(zero_ref, o_ref.at[chip, pl.ds(SHARD_ROWS, gap)], pad_sems.at[chip])
                    for chip in range(4)]

        @pl.when(i == 0)
        def _():
            zero_ref[...] = jnp.zeros_like(zero_ref)
            for c in pad_copies():
                c.start()

        for j in range(2, steps):
            @pl.when(i == j)
            def _(j=j):
                for c in copies(j - 2):
                    c.wait()

        tile_ref[i % 2] = _dot_tn(a_ref[...], b_ref[...])

        for j in range(steps):
            @pl.when(i == j)
            def _(j=j):
                for c in copies(j):
                    c.start()
                if j == steps - 1:
                    for c in copies(j - 1) + copies(j) + pad_copies():
                        c.wait()

    return pl.pallas_call(
        body, name="dw_in", grid=(steps,),
        out_shape=jax.ShapeDtypeStruct((4, ROWS_IN, 1024), F32),
        in_specs=[pl.BlockSpec((t, tm), lambda i: (0, i)), pl.BlockSpec((t, 1024), lambda i: (0, 0))],
        out_specs=ANY,
        scratch_shapes=[pltpu.VMEM((2, tm, 1024), F32), pltpu.VMEM((gap, 1024), F32),
                        pltpu.SemaphoreType.DMA((2, n_sem)), pltpu.SemaphoreType.DMA((4,))],
        compiler_params=_params(("arbitrary",)),
    )(dproj, h)


def _split_rest(dw_uq_pad_t, dw_ukv, dw_mem, dw_out):
    dw_uq_t = dw_uq_pad_t.reshape(MLA_HEADS, LANES, 256)[:, :MLA_QK_DIM].reshape(4, ROWS_UQ, 1024)
    parts = [dw_uq_t, dw_ukv.reshape(128, 4, 256).transpose(1, 0, 2).reshape(4, ROWS_UKV, 1024),
             dw_mem.reshape(4, ROWS_MEM, 1024), dw_out.reshape(4, ROWS_OUT, 1024)]
    return jnp.pad(jnp.concatenate(parts, axis=1), ((0, 0), (0, ROWS_REST - ROWS_USED), (0, 0)))


def _rope_consts(rot, first, period):
    half = rot // 2
    inv_freq = np.float32(ROPE_THETA) ** (-(np.arange(0, rot, 2, dtype=np.float32) / np.float32(rot)))
    lane = np.arange(LANES) % period - first
    in_rot = (lane >= 0) & (lane < rot)
    out = np.zeros((8, LANES), np.float32)
    out[0] = np.where(in_rot, inv_freq[np.clip(lane, 0, rot - 1) % half], 0.0)
    out[1] = in_rot & (lane < half)
    out[2] = in_rot & (lane >= half)
    return jnp.asarray(out)


def _band_bias(s):
    nblk = s // BAND_Q
    starts = np.array([_band_start(i, s) for i in range(nblk)])
    uq = (np.arange(nblk)[:, None] * BAND_Q + np.arange(BAND_Q)[None, :])[:, :, None]
    uk = (starts[:, None] + np.arange(BAND_WIN)[None, :])[:, None, :]
    tiles, index, seen = [], [], {}
    for _, d in DILATED:
        length = s // d
        ok = (uq // length == uk // length) & (np.abs(uq - uk) <= 64)
        row = []
        for i in range(nblk):
            key = ok[i].tobytes()
            if key not in seen:
                seen[key] = len(tiles)
                tiles.append(np.where(ok[i], 0.0, NEG_INF).astype(np.float32))
            row.append(seen[key])
        index.append(row)
    return jnp.asarray(np.stack(tiles, axis=0)), index


def _forward_backward(h, h32, proj, trig, rope_consts, x, mem, target, weights, gains):
    w_uq_pad_t, w_ukv, w_mem, w_out = weights
    g_emb, b_emb, g_cq, g_ckv, g_out_a, g_out_b, g_out_m, g_post, b_post = gains
    nb, s, d = x.shape
    t = nb * s
    x2 = x.reshape(t, d)
    mem2 = mem.reshape(nb * N_MEM, d)
    tgt2 = target.reshape(t, d)
    rope_a, rope_b = rope_consts
    bias, bias_index = _band_bias(s)
    scales = (0.125, MLA_QK_DIM ** -0.5, 128 ** -0.5)

    qa, ka, va, qb, kb, vb, qm = _prep(proj, trig, w_uq_pad_t, w_ukv, g_cq, g_ckv, rope_a, rope_b, scales)
    mkv = _mm(mem2, w_mem, BF16, nb * N_MEM, 1024, 1024, "mem_kv")

    cfg_b = dict(nb=nb, s=s, sk=s, heads=8, voff=0, bq=256)
    cfg_m = dict(nb=nb, s=s, sk=N_MEM, heads=4, hpb=2, voff=4, bq=1024)
    ya, lse_a, qkv_ordered = _dilated_fwd(qa, ka, va, bias, bias_index, nb=nb, s=s, name="attn_a_fwd")
    yb, lse_b = _attn_fwd(qb, kb, vb, name="attn_b_fwd", hpb=4, **cfg_b)
    ym, lse_m = _attn_fwd(qm, mkv, mkv, name="attn_m_fwd", **cfg_m)

    (y, dz, doa, dob, dom, dga, dgb, dgm, loss, dg_post, db_post, dg_a, dg_b, dg_m) = _post(
        h32, ya, yb, ym, proj, tgt2, w_out, g_out_a, g_out_b, g_out_m, g_post, b_post)

    dqa, dka, dva = _dilated_bwd(qa, ka, va, qkv_ordered, ya, doa, lse_a, bias, bias_index, nb=nb, s=s, scale=scales[0],
                                 name="attn_a_bwd")
    dqb, dkb, dvb = _attn_bwd(qb, kb, vb, yb, dob, lse_b, name="attn_b_bwd", scale=scales[1], hpb=4, **cfg_b)
    dqm, dmk, dmv = _attn_bwd(qm, mkv, mkv, ym, dom, lse_m, name="attn_m_bwd", scale=scales[2], **cfg_m)
    dmkv = jnp.concatenate([dmk, dmv], axis=1)

    dproj, dw_uq_pad_t, dw_ukv, dg_cq, dg_ckv = _prep_bwd(
        dqa, dka, dva, dqb, dkb, dvb, dqm, dga, dgb, dgm, proj, trig, w_uq_pad_t, w_ukv, g_cq, g_ckv, rope_a, rope_b)

    small_rows = (dg_cq, dg_ckv, loss, dg_a, dg_b, dg_m, dg_post, db_post)
    return (dproj, h, y, dz, dw_uq_pad_t, dw_ukv, mem2, dmkv), x2, small_rows


def _weight_grads(operands, core):
    dproj, h, y, dz, dw_uq_pad_t, dw_ukv, mem2, dmkv = operands
    g_in = _dw_in_split(dproj, h)
    dw_out, r_in = _mm(y, dz, F32, 2048, 1024, 1024, "dw_out", mode="tn",
                       ride=_half_to_sibling(g_in.reshape(4, 2, HALF_IN, 1024)))
    dw_mem = _mm(mem2, dmkv, F32, 1024, 1024, mem2.shape[0], "dw_mem", mode="tn")
    g_rest = _split_rest(dw_uq_pad_t, dw_ukv, dw_mem, dw_out)
    sf_in, sb_in, r_rest = _core_sum(g_in, r_in, core, HALF_IN, HALF_IN // 2, "core_sum_in",
                                     ride=_half_to_sibling(g_rest.reshape(4, 2, HALF_REST, 1024)))
    sf_rest, sb_rest = _core_sum(g_rest, r_rest, core, HALF_REST, HALF_REST, "core_sum_rest")
    return sf_in, sb_in, sf_rest, sb_rest


def _small_block(dg_emb, db_emb, small_rows):
    dg_cq, dg_ckv, loss, dg_a, dg_b, dg_m, dg_post, db_post = small_rows
    row2 = jnp.concatenate([dg_cq, dg_ckv, loss, jnp.zeros((1, 512), F32)], axis=1)
    return jnp.concatenate([dg_emb, db_emb, row2, dg_a, jnp.concatenate([dg_b, dg_m], axis=1), dg_post, db_post,
                            jnp.zeros((1, 1024), F32)], axis=0)


def _pack_small(g_emb, b_emb, g_cq, g_ckv, g_out_a, g_out_b, g_out_m, g_post, b_post):
    row2 = jnp.concatenate([g_cq.reshape(1, -1), g_ckv.reshape(1, -1), jnp.zeros((1, 640), F32)], axis=1)
    return jnp.concatenate([g_emb.reshape(1, -1), b_emb.reshape(1, -1), row2, g_out_a.reshape(1, -1),
                            jnp.concatenate([g_out_b.reshape(1, -1), g_out_m.reshape(1, -1)], axis=1),
                            g_post.reshape(1, -1), b_post.reshape(1, -1), jnp.zeros((1, 1024), F32)], axis=0)


def kernel(x, mem, positions, g_emb, b_emb, w_in, g_cq, g_ckv, w_uq, w_ukv, w_mem_kv, g_out_a, g_out_b, g_out_m, w_out, g_post, b_post, loss_target, m_g_emb, m_b_emb, m_w_in, m_g_cq, m_g_ckv, m_w_uq, m_w_ukv, m_w_mem_kv, m_g_out_a, m_g_out_b, m_g_out_m, m_w_out, m_g_post, m_b_post, v_g_emb, v_b_emb, v_w_in, v_g_cq, v_g_ckv, v_w_uq, v_w_ukv, v_w_mem_kv, v_g_out_a, v_g_out_b, v_g_out_m, v_w_out, v_g_post, v_b_post):
    w_rest = _pack_rest(w_uq, w_ukv, w_mem_kv, w_out)
    w_in_t = w_in[0].T
    w_in_b = jnp.pad(w_in_t.astype(BF16), ((0, ROWS_IN - SHARD_ROWS), (0, 0)))
    gains = (g_emb.reshape(1, -1), b_emb.reshape(1, -1), g_cq, g_ckv, g_out_a, g_out_b, g_out_m, g_post, b_post)
    rope_consts = (_rope_consts(16, 0, 64), _rope_consts(32, 64, 128))
    h, h32, trig, gathered_in = _ln_fwd(x.reshape(-1, D_MODEL), gains[0], gains[1],
                                        positions.reshape(-1, 1).astype(F32), *rope_consts,
                                        ride=_gather_ride(w_in_b.reshape(2, HALF_IN, 1024), spread=False))
    w_in_arr_t = _arranged_w_in(gathered_in.reshape(4, ROWS_IN, 1024))
    proj, gathered_rest = _mm(h, w_in_arr_t, F32, 1024, 2048, 1024, "in_proj", mode="nt",
                              ride=_gather_ride(w_rest.astype(BF16).reshape(2, HALF_REST, 1024), spread=True))
    weights = _rest_weights(gathered_rest.reshape(4, ROWS_REST, 1024))
    operands, x2, small_rows = _forward_backward(h, h32, proj, trig, rope_consts, x, mem, loss_target, weights,
                                                 gains)

    core = lax.axis_index("c").astype(jnp.int32).reshape(1)
    chip = (2 * lax.axis_index("x") + lax.axis_index("y")).astype(jnp.int32).reshape(1)
    sf_in, sb_in, sf_rest, sb_rest = _weight_grads(operands, core)
    grad_x, dg_emb, db_emb, rb_in, rb_rest = _dh_scatter(operands[0], w_in_arr_t, x2, operands[3], gains[0],
                                                         sb_in, sb_rest)
    gh_in = _chip_sum(sf_in, rb_in, chip, HALF_IN, HALF_IN // 2, "chip_sum_in")
    gh_rest = _chip_sum(sf_rest, rb_rest, chip, HALF_REST, HALF_REST, "chip_sum_rest")
    grad_in, grad_rest, small_sum = _join_and_allreduce(gh_in, gh_rest, _small_block(dg_emb, db_emb, small_rows))
    grad_in = grad_in.reshape(ROWS_IN, 1024)
    grad_rest = grad_rest.reshape(ROWS_REST, 1024)

    big_in = _adamw(grad_in, w_in_t, m_w_in[0].T, v_w_in[0].T, SHARD_ROWS // 3, "adamw_in")
    def rest_parts(a_uq, a_ukv, a_mem, a_out):
        return [a_uq[0].T.reshape(ROWS_UQ, 1024), a_ukv.reshape(ROWS_UKV, 1024), a_mem[0], a_out[0]]

    uq, ukv, wmem, wout = _adamw_pieces(
        grad_rest, rest_parts(w_uq, w_ukv, w_mem_kv, w_out), rest_parts(m_w_uq, m_w_ukv, m_w_mem_kv, m_w_out),
        rest_parts(v_w_uq, v_w_ukv, v_w_mem_kv, v_w_out), REST_PIECES, "adamw_rest")
    sm = _adamw_pieces(
        small_sum,
        _pack_small(g_emb, b_emb, g_cq, g_ckv, g_out_a, g_out_b, g_out_m, g_post, b_post),
        _pack_small(m_g_emb, m_b_emb, m_g_cq, m_g_ckv, m_g_out_a, m_g_out_b, m_g_out_m, m_g_post, m_b_post),
        _pack_small(v_g_emb, v_b_emb, v_g_cq, v_g_ckv, v_g_out_a, v_g_out_b, v_g_out_m, v_g_post, v_b_post),
        SMALL_PIECES, "adamw_small")
    loss = small_sum[2, 384]

    def ordered(kind):
        s_gemb, s_bemb, s_gcq, s_gckv, s_ga, s_gb, s_gm, s_gpost, s_bpost = [piece[kind] for piece in sm]
        return [s_gemb.reshape(-1), s_bemb.reshape(-1), big_in[kind].T[None], s_gcq, s_gckv,
                uq[kind].reshape(192, 256).T[None], ukv[kind].reshape(1, 128, 256), wmem[kind][None], s_ga, s_gb,
                s_gm, wout[kind][None], s_gpost, s_bpost]

    return (loss, grad_x.reshape(x.shape), *ordered(0), *ordered(1), *ordered(2), *ordered(3))
```

```python
import functools
import math

import jax
import jax.numpy as jnp
import numpy as np
from jax import lax
from jax.experimental import pallas as pl
from jax.experimental.pallas import tpu as pltpu

F32 = jnp.float32
BF16 = jnp.bfloat16
MESH = pl.DeviceIdType.MESH
ANY = pl.BlockSpec(memory_space=pl.ANY)
IN_VMEM = pl.BlockSpec(memory_space=pltpu.VMEM)

D_MODEL = 1024
A_WIDTH = 1024
MLA_HEADS = 8
MLA_Q_RANK = 256
MLA_KV_RANK = 128
MLA_QK_DIM = 96
MEM_WIDTH = 512
N_MEM = 256
ROPE_THETA = 500000.0
NORM_EPS = 1e-5
NEG_INF = -1e30
DEEPNORM_ALPHA = 2.0 ** 0.25
DILATED = ((64, 1), (256, 4), (1024, 16))

ADAM_LR = 0.001
ADAM_B1 = 0.9
ADAM_B2 = 0.999
ADAM_EPS = 1e-08
ADAM_WD = 0.01
ADAM_STEP = 10

LANES = 128
VMEM_LIMIT = 56 * 1024 * 1024
LOG2E = math.log2(math.e)
LN2 = math.log(2.0)

PROJ_W = 6144
COL_CQ = 4096
COL_BG = 4608
COL_MQ = 5120
COL_MG = 5632

SHARD_ROWS = 1512
ROWS_IN = 1536
ROWS_UQ, ROWS_UKV, ROWS_MEM, ROWS_OUT = 48, 32, 256, 512
ROWS_USED = ROWS_UQ + ROWS_UKV + ROWS_MEM + ROWS_OUT
ROWS_REST = 864
HALF_IN = ROWS_IN // 2
HALF_REST = ROWS_REST // 2
REST_PIECES = ((0, 48, 0, 1024), (48, 80, 0, 1024), (80, 336, 0, 1024), (336, 848, 0, 1024))
SMALL_PIECES = ((0, 1, 0, 1024), (1, 2, 0, 1024), (2, 3, 0, 256), (2, 3, 256, 384), (3, 4, 0, 1024), (4, 5, 0, 512),
                (4, 5, 512, 1024), (5, 6, 0, 1024), (6, 7, 0, 1024))


def _params(sem=None, vmem=VMEM_LIMIT):
    return pltpu.CompilerParams(dimension_semantics=sem, vmem_limit_bytes=vmem)


def _dot(a, b):
    return jnp.dot(a, b, preferred_element_type=F32)


def _dot_nt(a, b):
    return lax.dot_general(a, b, (((1,), (1,)), ((), ())), preferred_element_type=F32)


def _dot_tn(a, b):
    return lax.dot_general(a, b, (((0,), (0,)), ((), ())), preferred_element_type=F32)


def _ln_hat(x):
    mu = jnp.mean(x, axis=-1, keepdims=True)
    xc = x - mu
    var = jnp.mean(xc * xc, axis=-1, keepdims=True)
    rstd = lax.rsqrt(var + NORM_EPS)
    return xc * rstd, rstd


def _ln_bwd_rows(dxh, xh, rstd):
    return rstd * (dxh - jnp.mean(dxh, axis=-1, keepdims=True) - xh * jnp.mean(dxh * xh, axis=-1, keepdims=True))


def _rms_hat(x, width):
    ms = jnp.sum(x * x, axis=-1, keepdims=True) * (1.0 / width)
    r = lax.rsqrt(ms + NORM_EPS)
    return x * r, r


def _rms_bwd(u, xh, r, width):
    return r * (u - xh * (jnp.sum(u * xh, axis=-1, keepdims=True) * (1.0 / width)))


def _colsum(v):
    return jnp.sum(v, axis=0, keepdims=True)


def _rope_tables(cos, sin, consts):
    return cos, sin * consts[2:3, :], -sin * consts[1:2, :]


def _rope(x, tables, half, inverse=False):
    c, s_up, s_dn = tables
    if inverse:
        s_up, s_dn = -s_up, -s_dn
    return x * c + pltpu.roll(x, half, 1) * s_up + pltpu.roll(x, LANES - half, 1) * s_dn


def _ln_fwd(x, g, b, pos, rope_a, rope_b, tm=512, ride=None):
    t, d = x.shape
    n_in = len(ride.args) if ride else 0
    n_out = len(ride.out_shapes) if ride else 0
    steps = t // tm

    def body(x_ref, g_ref, b_ref, pos_ref, ra_ref, rb_ref, *rest):
        h_ref, h32_ref, trig_ref = rest[n_in:n_in + 3]
        if ride:
            i = pl.program_id(0)
            ride.run(i, steps, rest[:n_in], rest[n_in + 3:n_in + 3 + n_out], rest[n_in + 3 + n_out:])
        xh, _ = _ln_hat(x_ref[...])
        h = xh * g_ref[...] + b_ref[...]
        h32_ref[...] = h
        h_ref[...] = h.astype(BF16)
        for j, consts in enumerate((ra_ref, rb_ref)):
            ang = pos_ref[...] * consts[0:1, :]
            trig_ref[:, 2 * j * LANES:(2 * j + 1) * LANES] = jnp.cos(ang)
            trig_ref[:, (2 * j + 1) * LANES:(2 * j + 2) * LANES] = jnp.sin(ang)

    row = pl.BlockSpec((1, d), lambda i: (0, 0))
    tile = pl.BlockSpec((tm, d), lambda i: (i, 0))
    consts = pl.BlockSpec((8, LANES), lambda i: (0, 0))
    trig_tile = pl.BlockSpec((tm, 4 * LANES), lambda i: (i, 0))
    in_specs = [tile, row, row, pl.BlockSpec((tm, 1), lambda i: (i, 0)), consts, consts]
    shapes = (jax.ShapeDtypeStruct((t, d), BF16), jax.ShapeDtypeStruct((t, d), F32),
              jax.ShapeDtypeStruct((t, 4 * LANES), F32))
    if not ride:
        return pl.pallas_call(
            body, name="ln_fwd", grid=(steps,), out_shape=shapes, in_specs=in_specs,
            out_specs=(tile, tile, trig_tile), compiler_params=_params(("parallel",)),
        )(x, g, b, pos, rope_a, rope_b)
    return pl.pallas_call(
        body, name="ln_fwd", grid=(steps,),
        out_shape=(*shapes, *ride.out_shapes),
        in_specs=in_specs + ride.in_specs, out_specs=(tile, tile, trig_tile) + (ANY,) * n_out,
        scratch_shapes=ride.scratch(),
        compiler_params=_params(("arbitrary",)),
    )(x, g, b, pos, rope_a, rope_b, *ride.args)


class _Ride:
    def __init__(self, args, out_shapes, sem_counts, plan, in_specs=None, spread=True):
        self.args, self.out_shapes, self.plan = list(args), list(out_shapes), plan
        self.sem_counts = sem_counts
        self.in_specs = in_specs or [ANY] * len(self.args)
        self.spread = spread

    def scratch(self):
        return [pltpu.SemaphoreType.DMA((n,)) for n in self.sem_counts]

    def run(self, step, total, in_refs, out_refs, sems):
        count = len(self.plan(in_refs, out_refs, *sems))
        at = [(k * (total - 1)) // (count - 1) if self.spread or k == 0 else total - 1 for k in range(count)]
        for when in sorted(set(at)):
            @pl.when(step == when)
            def _(when=when):
                stages = self.plan(in_refs, out_refs, *sems)
                for k in range(count):
                    if at[k] == when:
                        stages[k]()


def _mm(a, b, out_dtype, tm, tn, tk, name, mode="nn", ride=None):
    if mode == "tn":
        k, m = a.shape
    else:
        m, k = a.shape
    n = b.shape[0] if mode == "nt" else b.shape[1]
    nk = k // tk
    nj, ni = n // tn, m // tm
    n_in = len(ride.args) if ride else 0
    n_out = len(ride.out_shapes) if ride else 0

    def body(a_ref, b_ref, *rest):
        o_ref = rest[n_in]
        acc_ref = rest[n_in + 1 + n_out]
        if ride:
            j, i, kk = pl.program_id(0), pl.program_id(1), pl.program_id(2)
            ride.run((j * ni + i) * nk + kk, nj * ni * nk, rest[:n_in], rest[n_in + 1:n_in + 1 + n_out],
                     rest[n_in + 2 + n_out:])
        av = a_ref[...].astype(BF16)
        bv = b_ref[...].astype(BF16)
        part = _dot_tn(av, bv) if mode == "tn" else _dot_nt(av, bv) if mode == "nt" else _dot(av, bv)
        if nk == 1:
            o_ref[...] = part.astype(out_dtype)
        else:
            kk = pl.program_id(2)

            @pl.when(kk == 0)
            def _():
                acc_ref[...] = part

            @pl.when(kk > 0)
            def _():
                acc_ref[...] += part

            @pl.when(kk == nk - 1)
            def _():
                o_ref[...] = acc_ref[...].astype(out_dtype)

    a_spec = (pl.BlockSpec((tk, tm), lambda j, i, kk: (kk, i)) if mode == "tn"
              else pl.BlockSpec((tm, tk), lambda j, i, kk: (i, kk)))
    b_spec = (pl.BlockSpec((tn, tk), lambda j, i, kk: (j, kk)) if mode == "nt"
              else pl.BlockSpec((tk, tn), lambda j, i, kk: (kk, j)))
    o_spec = pl.BlockSpec((tm, tn), lambda j, i, kk: (i, j))
    o_shape = jax.ShapeDtypeStruct((m, n), out_dtype)
    if not ride:
        return pl.pallas_call(
            body, name=name, grid=(nj, ni, nk), out_shape=o_shape, in_specs=[a_spec, b_spec], out_specs=o_spec,
            scratch_shapes=[pltpu.VMEM((tm, tn), F32)],
            compiler_params=_params(("parallel", "parallel", "arbitrary")),
        )(a, b)
    return pl.pallas_call(
        body, name=name, grid=(nj, ni, nk),
        out_shape=(o_shape, *ride.out_shapes),
        in_specs=[a_spec, b_spec] + ride.in_specs,
        out_specs=(o_spec,) + (ANY,) * n_out,
        scratch_shapes=[pltpu.VMEM((tm, tn), F32)] + ride.scratch(),
        compiler_params=_params(("arbitrary", "arbitrary", "arbitrary")),
    )(a, b, *ride.args)


def _prep(proj, trig, w_uq, w_ukv, g_cq, g_ckv, rope_a, rope_b, scales, tm=512):
    t = proj.shape[0]
    sc_a, sc_b, sc_m = (s * LOG2E for s in scales)

    def body(aq_ref, ak_ref, av_ref, bs_ref, mq_ref, trig_ref, wuq_ref, wukv_ref, gcq_ref, gckv_ref,
             ra_ref, rb_ref, qa_ref, ka_ref, va_ref, qb_ref, kb_ref, vb_ref, qm_ref):
        ta = _rope_tables(trig_ref[:, 0:LANES], trig_ref[:, LANES:2 * LANES], ra_ref[...])
        tb = _rope_tables(trig_ref[:, 2 * LANES:3 * LANES], trig_ref[:, 3 * LANES:4 * LANES], rb_ref[...])
        for j in range(A_WIDTH // LANES):
            sl = slice(j * LANES, (j + 1) * LANES)
            qa_ref[:, sl] = (_rope(aq_ref[:, sl], ta, 8) * sc_a).astype(BF16)
            ka_ref[:, sl] = _rope(ak_ref[:, sl], ta, 8).astype(BF16)
        va_ref[...] = av_ref[...].astype(BF16)
        qm_ref[...] = (mq_ref[...] * sc_m).astype(BF16)

        cq_hat, _ = _rms_hat(bs_ref[:, 0:MLA_Q_RANK], MLA_Q_RANK)
        cqn = (cq_hat * gcq_ref[...]).astype(BF16)
        ckv_hat, _ = _rms_hat(bs_ref[:, MLA_Q_RANK:MLA_Q_RANK + MLA_KV_RANK], MLA_KV_RANK)
        ckvn = (ckv_hat * gckv_ref[...]).astype(BF16)
        qfull = _dot_nt(cqn, wuq_ref[...])
        kv = _dot(ckvn, wukv_ref[...])
        kr = _rope(bs_ref[:, 384:512], tb, 16)
        lane = lax.broadcasted_iota(jnp.int32, (1, LANES), 1)
        low = lane < 64
        for h in range(MLA_HEADS):
            sl = slice(h * LANES, (h + 1) * LANES)
            qb_ref[:, sl] = (_rope(qfull[:, sl], tb, 16) * sc_b).astype(BF16)
            kb_ref[:, sl] = jnp.where(low, kv[:, sl], kr).astype(BF16)
            vb_ref[:, sl] = jnp.where(low, 0.0, kv[:, sl]).astype(BF16)

    def col(width, idx):
        return pl.BlockSpec((tm, width), lambda i: (i, idx))

    def full(shape):
        return pl.BlockSpec(shape, lambda i: (0, 0))

    wide = jax.ShapeDtypeStruct((t, 1024), BF16)
    return pl.pallas_call(
        body, name="prep", grid=(t // tm,),
        out_shape=(wide, wide, wide, wide, wide, wide,
                   jax.ShapeDtypeStruct((t, MEM_WIDTH), BF16)),
        in_specs=[col(1024, 0), col(1024, 1), col(1024, 2), col(512, COL_CQ // 512), col(512, COL_MQ // 512),
                  pl.BlockSpec((tm, 4 * LANES), lambda i: (i, 0)),
                  full((1024, MLA_Q_RANK)), full((MLA_KV_RANK, 1024)),
                  full((1, MLA_Q_RANK)), full((1, MLA_KV_RANK)), full((8, LANES)), full((8, LANES))],
        out_specs=(col(1024, 0),) * 6 + (col(MEM_WIDTH, 0),),
        compiler_params=_params(("parallel",)),
    )(proj, proj, proj, proj, proj, trig, w_uq, w_ukv, g_cq, g_ckv, rope_a, rope_b)


def _attn_fwd(q, k, v, *, nb, s, sk, heads, hpb, voff, bq, name):
    nq = s // bq
    width = hpb * LANES
    vblk = voff // hpb

    def body(q_ref, k_ref, v_ref, o_ref, lse_ref):
        for h in range(hpb):
            sl = slice(h * LANES, (h + 1) * LANES)
            sc = _dot_nt(q_ref[:, sl], k_ref[:, sl])
            m = jnp.max(sc, axis=1, keepdims=True)
            p = jnp.exp2(sc - m)
            l = jnp.sum(p, axis=1, keepdims=True)
            o_ref[:, sl] = _dot(p.astype(BF16), v_ref[:, sl]) / l
            lse_ref[:, sl] = jnp.broadcast_to(m + jnp.log(l) * LOG2E, (bq, LANES))

    out = jax.ShapeDtypeStruct((nb * s, heads * LANES), F32)
    ospec = pl.BlockSpec((bq, width), lambda b, i, g: (b * nq + i, g))
    return pl.pallas_call(
        body, name=name, grid=(nb, nq, heads // hpb),
        out_shape=(out, out),
        in_specs=[ospec, pl.BlockSpec((sk, width), lambda b, i, g: (b, g)),
                  pl.BlockSpec((sk, width), lambda b, i, g: (b, vblk + g))],
        out_specs=(ospec, ospec),
        compiler_params=_params(("parallel", "parallel", "parallel")),
    )(q, k, v)


def _attn_bwd(q, k, v, o, do, lse, *, nb, s, sk, heads, hpb, voff, scale, bq, name):
    nq = s // bq
    width = hpb * LANES
    vblk = voff // hpb

    def body(q_ref, k_ref, v_ref, o_ref, do_ref, lse_ref, dq_ref, dk_ref, dv_ref, dk_acc, dv_acc):
        i = pl.program_id(2)

        @pl.when(i == 0)
        def _():
            dk_acc[...] = jnp.zeros_like(dk_acc)
            dv_acc[...] = jnp.zeros_like(dv_acc)

        for h in range(hpb):
            sl = slice(h * LANES, (h + 1) * LANES)
            qh = q_ref[:, sl]
            kk = k_ref[:, sl]
            doh = do_ref[:, sl]
            delta = jnp.sum(doh.astype(F32) * o_ref[:, sl], axis=1, keepdims=True)
            p = jnp.exp2(_dot_nt(qh, kk) - lse_ref[:, h * LANES:h * LANES + 1])
            ds = (p * (_dot_nt(doh, v_ref[:, sl]) - delta)).astype(BF16)
            dq_ref[:, sl] = (_dot(ds, kk) * scale).astype(BF16)
            dk_acc[:, sl] += _dot_tn(ds, qh)
            dv_acc[:, sl] += _dot_tn(p.astype(BF16), doh)

        @pl.when(i == nq - 1)
        def _():
            dk_ref[...] = (dk_acc[...] * LN2).astype(BF16)
            dv_ref[...] = dv_acc[...].astype(BF16)

    qspec = pl.BlockSpec((bq, width), lambda b, g, i: (b * nq + i, g))
    kv_spec = pl.BlockSpec((sk, width), lambda b, g, i: (b, g))
    dq_shape = jax.ShapeDtypeStruct((nb * s, heads * LANES), BF16)
    dkv_shape = jax.ShapeDtypeStruct((nb * sk, heads * LANES), BF16)
    return pl.pallas_call(
        body, name=name, grid=(nb, heads // hpb, nq),
        out_shape=(dq_shape, dkv_shape, dkv_shape),
        in_specs=[qspec, kv_spec, pl.BlockSpec((sk, width), lambda b, g, i: (b, vblk + g)), qspec, qspec, qspec],
        out_specs=(qspec, kv_spec, kv_spec),
        scratch_shapes=[pltpu.VMEM((sk, width), F32), pltpu.VMEM((sk, width), F32)],
        compiler_params=_params(("parallel", "parallel", "arbitrary")),
    )(q, k, v, o, do, lse)


BAND_Q = 128
BAND_WIN = 256


def _band_start(i, s):
    return min(max(i * BAND_Q - 64, 0), s - BAND_WIN)


def _to_pattern_order(src_ref, dst_ref, stage_ref, s, d):
    length = s // d
    stage_ref[...] = src_ref[...].astype(F32)
    for r in range(d):
        dst_ref[r * length:(r + 1) * length, :] = stage_ref[pl.ds(r, length, stride=d), :].astype(dst_ref.dtype)


def _dilated_fwd(q, k, v, bias, bias_index, *, nb, s, name):
    nblk = s // BAND_Q
    npat = len(DILATED)

    def body(q_ref, k_ref, v_ref, bias_ref, o_ref, lse_ref, *rest):
        ordered = rest[:3 * (npat - 1)]
        stage_ref, op_ref, lp_ref, on_ref, ln_ref = rest[3 * (npat - 1):]
        lane = lax.broadcasted_iota(jnp.int32, (1, LANES), 1)
        first = lane < 64
        for p, (_, d) in enumerate(DILATED):
            if d == 1:
                qs, ks, vs = q_ref, k_ref, v_ref
            else:
                qs, ks, vs = ordered[3 * (p - 1):3 * p]
                for src, dst in ((q_ref, qs), (k_ref, ks), (v_ref, vs)):
                    _to_pattern_order(src, dst, stage_ref, s, d)
            for i in range(nblk):
                u0 = i * BAND_Q
                st = _band_start(i, s)
                qi = qs[u0:u0 + BAND_Q, :]
                kw = ks[st:st + BAND_WIN, :]
                vw = vs[st:st + BAND_WIN, :]
                zero = jnp.zeros_like(qi)
                q2 = jnp.concatenate([jnp.where(first, qi, zero), jnp.where(first, zero, qi)], axis=0)
                sc = _dot_nt(q2, kw)
                b = bias_ref[bias_index[p][i]]
                halves = []
                for h in range(2):
                    sh = sc[h * BAND_Q:(h + 1) * BAND_Q] + b
                    m = jnp.max(sh, axis=1, keepdims=True)
                    pr = jnp.exp2(sh - m)
                    l = jnp.sum(pr, axis=1, keepdims=True)
                    halves.append((pr.astype(BF16), l, m + jnp.log(l) * LOG2E))
                o2 = _dot(jnp.concatenate([halves[0][0], halves[1][0]], axis=0), vw)
                o_blk = jnp.where(first, o2[:BAND_Q] / halves[0][1], o2[BAND_Q:] / halves[1][1])
                lse_blk = jnp.where(first, jnp.broadcast_to(halves[0][2], (BAND_Q, LANES)),
                                    jnp.broadcast_to(halves[1][2], (BAND_Q, LANES)))
                op_ref[p, u0:u0 + BAND_Q, :] = o_blk
                lp_ref[p, u0:u0 + BAND_Q, :] = lse_blk
            if d > 1:
                length = s // d
                for r in range(d):
                    on_ref.at[p - 1][pl.ds(r, length, stride=d), :] = op_ref[p, r * length:(r + 1) * length, :]
                    ln_ref.at[p - 1][pl.ds(r, length, stride=d), :] = lp_ref[p, r * length:(r + 1) * length, :]
        lses = [lp_ref[0]] + [ln_ref[p] for p in range(npat - 1)]
        outs = [op_ref[0]] + [on_ref[p] for p in range(npat - 1)]
        m = functools.reduce(jnp.maximum, lses)
        ws = [jnp.exp2(l - m) for l in lses]
        den = functools.reduce(lambda a, c: a + c, ws)
        o_ref[...] = functools.reduce(lambda a, c: a + c, [w * o for w, o in zip(ws, outs)]) / den
        lse_ref[...] = m + jnp.log(den) * LOG2E

    blk = pl.BlockSpec((s, LANES), lambda b, g: (b, g))
    out = jax.ShapeDtypeStruct((nb * s, A_WIDTH), F32)
    copy = jax.ShapeDtypeStruct((nb * s, A_WIDTH), BF16)
    n_copies = 3 * (npat - 1)
    res = pl.pallas_call(
        body, name=name, grid=(nb, A_WIDTH // LANES),
        out_shape=(out, out) + (copy,) * n_copies,
        in_specs=[blk, blk, blk, pl.BlockSpec(bias.shape, lambda b, g: (0, 0, 0))],
        out_specs=(blk, blk) + (blk,) * n_copies,
        scratch_shapes=[pltpu.VMEM((s, LANES), F32), pltpu.VMEM((npat, s, LANES), F32),
                        pltpu.VMEM((npat, s, LANES), F32), pltpu.VMEM((npat - 1, s, LANES), F32),
                        pltpu.VMEM((npat - 1, s, LANES), F32)],
        compiler_params=_params(("parallel", "parallel")),
    )(q, k, v, bias)
    return res[0], res[1], res[2:]


def _dilated_bwd(q, k, v, ordered, o, do, lse, bias, bias_index, *, nb, s, scale, name):
    nblk = s // BAND_Q
    npat = len(DILATED)
    n_copies = 3 * (npat - 1)

    def body(q_ref, k_ref, v_ref, *rest):
        ordered_refs = rest[:n_copies]
        (o_ref, do_ref, lse_ref, bias_ref, dq_out, dk_out, dv_out, stage_ref, rs_ref, dop_ref, rsp_ref,
         dqp_ref, dkp_ref, dvp_ref, dq_ref, dk_ref, dv_ref, nat_ref) = rest[n_copies:]
        lane = lax.broadcasted_iota(jnp.int32, (1, LANES), 1)
        first = lane < 64
        prod = do_ref[...].astype(F32) * o_ref[...]
        d0 = jnp.sum(jnp.where(first, prod, 0.0), axis=1, keepdims=True)
        d1 = jnp.sum(jnp.where(first, 0.0, prod), axis=1, keepdims=True)
        delta = jnp.where(first, jnp.broadcast_to(d0, (s, LANES)), jnp.broadcast_to(d1, (s, LANES)))
        rs_ref[...] = jnp.where((lane & 32) == 0, lse_ref[...], delta)
        for p, (_, d) in enumerate(DILATED):
            length = s // d
            if d == 1:
                qs, ks, vs, dos, rss = q_ref, k_ref, v_ref, do_ref, rs_ref
                dqs, dks, dvs = dq_ref, dk_ref, dv_ref
            else:
                for src, dst in ((do_ref, dop_ref), (rs_ref, rsp_ref)):
                    _to_pattern_order(src, dst, stage_ref, s, d)
                qs, ks, vs = ordered_refs[3 * (p - 1):3 * p]
                dos, rss = dop_ref, rsp_ref
                dqs, dks, dvs = dqp_ref, dkp_ref, dvp_ref
            dks[...] = jnp.zeros((s, LANES), F32)
            dvs[...] = jnp.zeros((s, LANES), F32)
            for i in range(nblk):
                u0 = i * BAND_Q
                st = _band_start(i, s)
                qi = qs[u0:u0 + BAND_Q, :]
                doi = dos[u0:u0 + BAND_Q, :]
                kw = ks[st:st + BAND_WIN, :]
                vw = vs[st:st + BAND_WIN, :]
                zero = jnp.zeros_like(qi)
                q2 = jnp.concatenate([jnp.where(first, qi, zero), jnp.where(first, zero, qi)], axis=0)
                do2 = jnp.concatenate([jnp.where(first, doi, zero), jnp.where(first, zero, doi)], axis=0)
                sc = _dot_nt(q2, kw)
                dp = _dot_nt(do2, vw)
                b = bias_ref[bias_index[p][i]]
                rs_i = rss[u0:u0 + BAND_Q, :]
                ps, dss = [], []
                for h in range(2):
                    rows = slice(h * BAND_Q, (h + 1) * BAND_Q)
                    pr = jnp.exp2(sc[rows] + b - rs_i[:, 64 * h:64 * h + 1])
                    ps.append(pr.astype(BF16))
                    dss.append((pr * (dp[rows] - rs_i[:, 64 * h + 32:64 * h + 33])).astype(BF16))
                p2 = jnp.concatenate(ps, axis=0)
                ds2 = jnp.concatenate(dss, axis=0)
                dq2 = _dot(ds2, kw)
                dqs[u0:u0 + BAND_Q, :] = jnp.where(first, dq2[:BAND_Q], dq2[BAND_Q:]) * scale
                dks[st:st + BAND_WIN, :] += _dot_tn(ds2, q2)
                dvs[st:st + BAND_WIN, :] += _dot_tn(p2, do2)
            if d > 1:
                for j, src in enumerate((dqp_ref, dkp_ref, dvp_ref)):
                    for r in range(d):
                        nat_ref.at[p - 1, j][pl.ds(r, length, stride=d), :] = src[r * length:(r + 1) * length, :]

        def total(j, first_ref):
            return functools.reduce(lambda a, c: a + c, [first_ref[...]] + [nat_ref[p, j] for p in range(npat - 1)])

        dq_out[...] = total(0, dq_ref).astype(BF16)
        dk_out[...] = (total(1, dk_ref) * LN2).astype(BF16)
        dv_out[...] = total(2, dv_ref).astype(BF16)

    blk = pl.BlockSpec((s, LANES), lambda b, g: (b, g))
    out = jax.ShapeDtypeStruct((nb * s, A_WIDTH), BF16)
    f32_buf = pltpu.VMEM((s, LANES), F32)
    bf_buf = pltpu.VMEM((s, LANES), BF16)
    return pl.pallas_call(
        body, name=name, grid=(nb, A_WIDTH // LANES),
        out_shape=(out, out, out),
        in_specs=[blk] * (6 + n_copies) + [pl.BlockSpec(bias.shape, lambda b, g: (0, 0, 0))],
        out_specs=(blk, blk, blk),
        scratch_shapes=[f32_buf, f32_buf, bf_buf] + [f32_buf] * 7 + [pltpu.VMEM((npat - 1, 3, s, LANES), F32)],
        compiler_params=_params(("parallel", "parallel")),
    )(q, k, v, *ordered, o, do, lse, bias)


def _post(h32, ya, ybp, ym, proj, target, w_out, g_a, g_b, g_m, g_post, b_post, tm=256):
    t = h32.shape[0]

    def body(h_ref, ya_ref, yb_ref, ym_ref, ga_ref, gb_ref, gm_ref, tg_ref, wo_ref,
             goa_ref, gob_ref, gom_ref, gp_ref, bp_ref,
             y_ref, dz_ref, doa_ref, dob_ref, dom_ref, dga_ref, dgb_ref, dgm_ref,
             loss_ref, dgp_ref, dbp_ref, dgoa_ref, dgob_ref, dgom_ref):
        i = pl.program_id(0)

        @pl.when(i == 0)
        def _():
            for r in (loss_ref, dgp_ref, dbp_ref, dgoa_ref, dgob_ref, dgom_ref):
                r[...] = jnp.zeros_like(r)

        lane = lax.broadcasted_iota(jnp.int32, (1, LANES), 1)
        low = lane < 64
        h = h_ref[...]

        ybp_v = yb_ref[...]
        yb = jnp.concatenate(
            [jnp.where(low, pltpu.roll(ybp_v[:, 2 * j * LANES:(2 * j + 1) * LANES], 64, 1),
                       ybp_v[:, (2 * j + 1) * LANES:(2 * j + 2) * LANES]) for j in range(4)], axis=1)

        def gated(raw, gate, gain, width):
            xh, r = _rms_hat(raw, width)
            n = xh * gain
            sg = 1.0 / (1.0 + jnp.exp(-gate))
            return xh, r, n, sg, n * (gate * sg)

        gate_a, gate_b, gate_m = ga_ref[...], gb_ref[...], gm_ref[...]
        xh_a, r_a, n_a, sg_a, y_a = gated(ya_ref[...], gate_a, goa_ref[...], A_WIDTH)
        xh_b, r_b, n_b, sg_b, y_b = gated(yb, gate_b, gob_ref[...], 512)
        xh_m, r_m, n_m, sg_m, y_m = gated(ym_ref[...], gate_m, gom_ref[...], 512)
        y = jnp.concatenate([y_a, y_b, y_m], axis=1).astype(BF16)
        y_ref[...] = y
        z = DEEPNORM_ALPHA * h + _dot(y, wo_ref[...])
        zh, rstd = _ln_hat(z)
        err = zh * gp_ref[...] + bp_ref[...] - tg_ref[...]
        rows = jnp.sum(err * err, axis=1, keepdims=True)
        loss_ref[...] += jnp.broadcast_to(jnp.sum(rows, axis=0, keepdims=True) * (0.5 / D_MODEL), (1, LANES))
        dout = err * (1.0 / D_MODEL)
        dgp_ref[...] += _colsum(dout * zh)
        dbp_ref[...] += _colsum(dout)
        dz = _ln_bwd_rows(dout * gp_ref[...], zh, rstd)
        dz_ref[...] = dz
        dy = _dot_nt(dz.astype(BF16), wo_ref[...])

        def gated_bwd(dyg, xh, r, n, sg, gate, gain, width, dgain_ref):
            dn = dyg * (gate * sg)
            dgate = dyg * n * (sg * (1.0 + gate * (1.0 - sg)))
            dgain_ref[...] += _colsum(dn * xh)
            return _rms_bwd(dn * gain, xh, r, width), dgate

        dya, dgate_a = gated_bwd(dy[:, 0:1024], xh_a, r_a, n_a, sg_a, gate_a, goa_ref[...], A_WIDTH, dgoa_ref)
        dyb, dgate_b = gated_bwd(dy[:, 1024:1536], xh_b, r_b, n_b, sg_b, gate_b, gob_ref[...], 512, dgob_ref)
        dym, dgate_m = gated_bwd(dy[:, 1536:2048], xh_m, r_m, n_m, sg_m, gate_m, gom_ref[...], 512, dgom_ref)
        doa_ref[...] = dya.astype(BF16)
        dom_ref[...] = dym.astype(BF16)
        dga_ref[...] = dgate_a.astype(BF16)
        dgb_ref[...] = dgate_b.astype(BF16)
        dgm_ref[...] = dgate_m.astype(BF16)
        for j in range(4):
            blk = dyb[:, j * LANES:(j + 1) * LANES]
            dob_ref[:, 2 * j * LANES:(2 * j + 1) * LANES] = jnp.where(low, 0.0, pltpu.roll(blk, 64, 1)).astype(BF16)
            dob_ref[:, (2 * j + 1) * LANES:(2 * j + 2) * LANES] = jnp.where(low, 0.0, blk).astype(BF16)

    def col(width, idx):
        return pl.BlockSpec((tm, width), lambda i: (i, idx))

    def full(shape):
        return pl.BlockSpec(shape, lambda i: (0, 0))

    def acc(width):
        return jax.ShapeDtypeStruct((1, width), F32)

    return pl.pallas_call(
        body, name="post", grid=(t // tm,),
        out_shape=(jax.ShapeDtypeStruct((t, 2048), BF16), jax.ShapeDtypeStruct((t, 1024), F32),
                   jax.ShapeDtypeStruct((t, 1024), BF16), jax.ShapeDtypeStruct((t, 1024), BF16),
                   jax.ShapeDtypeStruct((t, 512), BF16),
                   jax.ShapeDtypeStruct((t, 1024), BF16), jax.ShapeDtypeStruct((t, 512), BF16),
                   jax.ShapeDtypeStruct((t, 512), BF16),
                   acc(LANES), acc(1024), acc(1024), acc(1024), acc(512), acc(512)),
        in_specs=[col(1024, 0), col(1024, 0), col(1024, 0), col(512, 0),
                  col(1024, 3), col(512, COL_BG // 512), col(512, COL_MG // 512), col(1024, 0),
                  full((2048, 1024)),
                  full((1, 1024)), full((1, 512)), full((1, 512)), full((1, 1024)), full((1, 1024))],
        out_specs=(col(2048, 0), col(1024, 0), col(1024, 0), col(1024, 0), col(512, 0),
                   col(1024, 0), col(512, 0), col(512, 0),
                   full((1, LANES)), full((1, 1024)), full((1, 1024)), full((1, 1024)), full((1, 512)),
                   full((1, 512))),
        compiler_params=_params(("arbitrary",)),
    )(h32, ya, ybp, ym, proj, proj, proj, target, w_out, g_a, g_b, g_m, g_post, b_post)


def _prep_bwd(dqa, dka, dva, dqb, dkb, dvb, dqm, dga, dgb, dgm, proj, trig, w_uq, w_ukv, g_cq, g_ckv,
              rope_a, rope_b, tm=512):
    t = proj.shape[0]

    def body(dqa_ref, dka_ref, dva_ref, dqb_ref, dkb_ref, dvb_ref, dqm_ref, dga_ref, dgb_ref, dgm_ref,
             bs_ref, trig_ref, wuq_ref, wukv_ref, gcq_ref, gckv_ref, ra_ref, rb_ref,
             dproj_ref, dwuq_ref, dwukv_ref, dgcq_ref, dgckv_ref, dqf_ref, dkv_ref):
        i = pl.program_id(0)

        @pl.when(i == 0)
        def _():
            dwuq_ref[...] = jnp.zeros_like(dwuq_ref)
            dwukv_ref[...] = jnp.zeros_like(dwukv_ref)
            dgcq_ref[...] = jnp.zeros_like(dgcq_ref)
            dgckv_ref[...] = jnp.zeros_like(dgckv_ref)

        ta = _rope_tables(trig_ref[:, 0:LANES], trig_ref[:, LANES:2 * LANES], ra_ref[...])
        tb = _rope_tables(trig_ref[:, 2 * LANES:3 * LANES], trig_ref[:, 3 * LANES:4 * LANES], rb_ref[...])
        for j in range(A_WIDTH // LANES):
            sl = slice(j * LANES, (j + 1) * LANES)
            dproj_ref[:, j * LANES:(j + 1) * LANES] = (
                _rope(dqa_ref[:, sl].astype(F32), ta, 8, inverse=True).astype(BF16))
            dproj_ref[:, 1024 + j * LANES:1024 + (j + 1) * LANES] = (
                _rope(dka_ref[:, sl].astype(F32), ta, 8, inverse=True).astype(BF16))
        dproj_ref[:, 2048:3072] = dva_ref[...]
        dproj_ref[:, 3072:4096] = dga_ref[...]

        lane = lax.broadcasted_iota(jnp.int32, (1, LANES), 1)
        low = lane < 64
        rope_lanes = (lane >= 64) & (lane < 96)
        dkr = jnp.zeros((tm, LANES), F32)
        for h in range(MLA_HEADS):
            sl = slice(h * LANES, (h + 1) * LANES)
            dqf_ref[:, sl] = _rope(dqb_ref[:, sl].astype(F32), tb, 16, inverse=True).astype(BF16)
            dk_h = dkb_ref[:, sl]
            dkv_ref[:, sl] = jnp.where(low, dk_h, dvb_ref[:, sl])
            dkr = dkr + jnp.where(rope_lanes, dk_h.astype(F32), 0.0)
        dkr = _rope(dkr, tb, 16, inverse=True)

        cq_hat, r_q = _rms_hat(bs_ref[:, 0:MLA_Q_RANK], MLA_Q_RANK)
        dwuq_ref[...] += _dot_tn(dqf_ref[...], (cq_hat * gcq_ref[...]).astype(BF16))
        dcqn = _dot(dqf_ref[...], wuq_ref[...])
        dgcq_ref[...] += _colsum(dcqn * cq_hat)
        dproj_ref[:, COL_CQ:COL_CQ + 256] = _rms_bwd(dcqn * gcq_ref[...], cq_hat, r_q, MLA_Q_RANK).astype(BF16)
        ckv_hat, r_kv = _rms_hat(bs_ref[:, MLA_Q_RANK:MLA_Q_RANK + MLA_KV_RANK], MLA_KV_RANK)
        dwukv_ref[...] += _dot_tn((ckv_hat * gckv_ref[...]).astype(BF16), dkv_ref[...])
        dckvn = _dot_nt(dkv_ref[...], wukv_ref[...])
        dgckv_ref[...] += _colsum(dckvn * ckv_hat)
        dproj_ref[:, COL_CQ + 256:COL_CQ + 384] = (
            _rms_bwd(dckvn * gckv_ref[...], ckv_hat, r_kv, MLA_KV_RANK).astype(BF16))
        dproj_ref[:, COL_CQ + 384:COL_CQ + 512] = dkr.astype(BF16)
        dproj_ref[:, COL_BG:COL_BG + 512] = dgb_ref[...]
        dproj_ref[:, COL_MQ:COL_MQ + 512] = dqm_ref[...]
        dproj_ref[:, COL_MG:COL_MG + 512] = dgm_ref[...]

    def col(width, idx):
        return pl.BlockSpec((tm, width), lambda i: (i, idx))

    def full(shape):
        return pl.BlockSpec(shape, lambda i: (0, 0))

    return pl.pallas_call(
        body, name="prep_bwd", grid=(t // tm,),
        out_shape=(jax.ShapeDtypeStruct((t, PROJ_W), BF16), jax.ShapeDtypeStruct((1024, MLA_Q_RANK), F32),
                   jax.ShapeDtypeStruct((MLA_KV_RANK, 1024), F32),
                   jax.ShapeDtypeStruct((1, MLA_Q_RANK), F32), jax.ShapeDtypeStruct((1, MLA_KV_RANK), F32)),
        in_specs=[col(1024, 0)] * 6 + [col(512, 0), col(1024, 0), col(512, 0), col(512, 0),
                  col(512, COL_CQ // 512), pl.BlockSpec((tm, 4 * LANES), lambda i: (i, 0)),
                  full((1024, MLA_Q_RANK)), full((MLA_KV_RANK, 1024)),
                  full((1, MLA_Q_RANK)), full((1, MLA_KV_RANK)), full((8, LANES)), full((8, LANES))],
        out_specs=(col(PROJ_W, 0), full((1024, MLA_Q_RANK)), full((MLA_KV_RANK, 1024)),
                   full((1, MLA_Q_RANK)), full((1, MLA_KV_RANK))),
        scratch_shapes=[pltpu.VMEM((tm, 1024), BF16), pltpu.VMEM((tm, 1024), BF16)],
        compiler_params=_params(("arbitrary",)),
    )(dqa, dka, dva, dqb, dkb, dvb, dqm, dga, dgb, dgm, proj, trig, w_uq, w_ukv, g_cq, g_ckv, rope_a, rope_b)


def _adamw_math(gv, w, m, v):
    m_new = ADAM_B1 * m + (1.0 - ADAM_B1) * gv
    v_new = ADAM_B2 * v + (1.0 - ADAM_B2) * (gv * gv)
    m_hat = m_new / (1.0 - ADAM_B1 ** ADAM_STEP)
    v_hat = v_new / (1.0 - ADAM_B2 ** ADAM_STEP)
    return -ADAM_LR * (m_hat / (jnp.sqrt(v_hat) + ADAM_EPS) + ADAM_WD * w), m_new, v_new


def _adamw(g, w, m, v, tr, name, passed):
    r, cols = w.shape

    def body(g_ref, w_ref, m_ref, v_ref, passed_ref, go_ref, d_ref, nm_ref, nv_ref, passed_out_ref):
        gv = g_ref[...]
        go_ref[...] = gv
        d_ref[...], nm_ref[...], nv_ref[...] = _adamw_math(gv, w_ref[...], m_ref[...], v_ref[...])

    tile = pl.BlockSpec((tr, cols), lambda i: (i, 0))
    shape = jax.ShapeDtypeStruct((r, cols), F32)
    return pl.pallas_call(
        body, name=name, grid=(r // tr,),
        out_shape=(shape,) * 4 + (jax.ShapeDtypeStruct(passed.shape, passed.dtype),),
        in_specs=[tile] * 4 + [ANY], out_specs=(tile,) * 4 + (ANY,),
        input_output_aliases={4: 4},
        compiler_params=_params(("parallel",)),
    )(g, w, m, v, passed)


def _adamw_pieces(g, w, m, v, pieces, name):
    n = len(pieces)
    per_piece = isinstance(w, (list, tuple))
    shapes = [jax.ShapeDtypeStruct((r1 - r0, c1 - c0), F32) for r0, r1, c0, c1 in pieces]
    args = (g, *w, *m, *v) if per_piece else (g, w, m, v)

    def body(g_ref, *refs):
        ins, outs = refs[:len(args) - 1], refs[len(args) - 1:]
        gv = g_ref[...]
        if not per_piece:
            results = (gv,) + _adamw_math(gv, ins[0][...], ins[1][...], ins[2][...])
        for p, (r0, r1, c0, c1) in enumerate(pieces):
            if per_piece:
                gp = gv[r0:r1, c0:c1]
                vals = (gp,) + _adamw_math(gp, ins[p][...], ins[n + p][...], ins[2 * n + p][...])
            else:
                vals = [full[r0:r1, c0:c1] for full in results]
            for kind, val in enumerate(vals):
                outs[kind * n + p][...] = val

    flat = pl.pallas_call(
        body, name=name, out_shape=tuple(shapes) * 4,
        in_specs=[IN_VMEM] * len(args), out_specs=tuple([IN_VMEM] * (4 * n)),
        compiler_params=_params(None),
    )(*args)
    return [[flat[kind * n + p] for kind in range(4)] for p in range(n)]


def _core_sum(g, recv, core, rows, tr, name, ride=None):
    cols = g.shape[2]
    nblk = rows // tr
    n_in = len(ride.args) if ride else 0
    n_out = len(ride.out_shapes) if ride else 0

    def body(c_ref, g_ref, r_ref, *rest):
        sf_ref, sb_ref = rest[n_in], rest[n_in + 1]
        if ride:
            j, i = pl.program_id(0), pl.program_id(1)
            ride.run(j * nblk + i, 4 * nblk, rest[:n_in], rest[n_in + 2:n_in + 2 + n_out],
                     rest[n_in + 2 + n_out:])
        tot = g_ref[...] + r_ref[...]
        sf_ref[...] = tot
        sb_ref[...] = tot.astype(BF16)

    half = pl.BlockSpec((None, tr, cols), lambda j, i, c_ref: (j, i, 0))
    shapes = (jax.ShapeDtypeStruct((4, rows, cols), F32), jax.ShapeDtypeStruct((4, rows, cols), BF16))
    return pl.pallas_call(
        body, name=name,
        grid_spec=pltpu.PrefetchScalarGridSpec(
            num_scalar_prefetch=1, grid=(4, nblk),
            in_specs=[pl.BlockSpec((None, tr, cols), lambda j, i, c_ref: (j, c_ref[0] * nblk + i, 0)), half]
            + (ride.in_specs if ride else []),
            out_specs=(half, half) + (ANY,) * n_out,
            scratch_shapes=ride.scratch() if ride else []),
        out_shape=shapes + tuple(ride.out_shapes if ride else ()),
        compiler_params=_params(("arbitrary", "arbitrary") if ride else ("parallel", "parallel")),
    )(core, g, recv, *(ride.args if ride else ()))


def _half_to_sibling(g4):
    def plan(in_refs, out_refs, send_sems, recv_sems):
        x, y, c = _position()
        cp = pltpu.make_async_remote_copy(
            src_ref=in_refs[0].at[:, 1 - c], dst_ref=out_refs[0], send_sem=send_sems.at[0],
            recv_sem=recv_sems.at[0], device_id=(x, y, 1 - c), device_id_type=MESH)

        def finish():
            cp.wait_recv()
            cp.wait_send()

        return cp.start, finish

    return _Ride([g4], [jax.ShapeDtypeStruct((4, g4.shape[2], 1024), F32)], (1, 1), plan)


def _gather_plan(src_ref, dst_ref, send_sems, recv_sems, local_sems):
    x, y, c = _position()
    me = 2 * x + y
    rows = src_ref.shape[1]
    cut = -(-rows // 32) * 16
    pieces = (pl.ds(0, cut), pl.ds(cut, rows - cut))
    local = pltpu.make_async_copy(src_ref, dst_ref.at[me], local_sems.at[0])

    def over_ici(sem, k, chip, t, src=None):
        where = dst_ref.at[chip, c, pieces[t]]
        return pltpu.make_async_remote_copy(
            src_ref=where if src is None else src, dst_ref=where, send_sem=send_sems.at[sem],
            recv_sem=recv_sems.at[sem], device_id=(x ^ (k >> 1), y ^ (k & 1), c), device_id_type=MESH)

    def mine_to(k, t):
        return over_ici(2 * (k - 1) + t, k, me, t, src=src_ref.at[c, pieces[t]])

    def from_neighbour(k, t):
        return over_ici(2 * (k - 1) + t, k, me ^ k, t)

    def to_sibling(k, half):
        piece = dst_ref.at[me ^ k, half]
        return pltpu.make_async_remote_copy(
            src_ref=piece, dst_ref=piece, send_sem=send_sems.at[5 + k], recv_sem=recv_sems.at[5 + k],
            device_id=(x, y, 1 - c), device_id_type=MESH)

    sends = [mine_to(2, 0), mine_to(1, 1), mine_to(2, 1), mine_to(1, 0)]
    onward = [over_ici(4, 1, me ^ 2, 0), over_ici(5, 2, me ^ 1, 1)]

    def start():
        local.start()
        for cp in sends:
            cp.start()

    def pass_on():
        from_neighbour(2, 0).wait_recv()
        onward[0].start()
        from_neighbour(1, 1).wait_recv()
        onward[1].start()

    def to_other_core():
        from_neighbour(2, 1).wait_recv()
        to_sibling(2, c).start()
        from_neighbour(1, 0).wait_recv()
        to_sibling(1, c).start()
        over_ici(4, 1, me ^ 3, 0).wait_recv()
        over_ici(5, 2, me ^ 3, 1).wait_recv()
        to_sibling(3, c).start()

    def finish():
        for k in (1, 2, 3):
            to_sibling(k, 1 - c).wait_recv()
        for cp in sends + onward + [to_sibling(k, c) for k in (1, 2, 3)]:
            cp.wait_send()
        local.wait()

    return start, pass_on, to_other_core, finish


def _gather_ride(shard, spread):
    def plan(in_refs, out_refs, send_sems, recv_sems, local_sems):
        return _gather_plan(in_refs[0], out_refs[0], send_sems, recv_sems, local_sems)

    return _Ride([shard], [jax.ShapeDtypeStruct((4,) + shard.shape, shard.dtype)], (9, 9, 1), plan,
                 in_specs=[IN_VMEM], spread=spread)


def _chip_sum(sf, recv, chip, rows, tr, name):
    cols = sf.shape[2]
    n_recv = recv.shape[0]

    def body(me_ref, sf_ref, r_ref, out_ref):
        acc = sf_ref[...]
        for k in range(n_recv):
            acc = acc + r_ref[k].astype(F32)
        out_ref[...] = acc

    return pl.pallas_call(
        body, name=name,
        grid_spec=pltpu.PrefetchScalarGridSpec(
            num_scalar_prefetch=1, grid=(rows // tr,),
            in_specs=[pl.BlockSpec((None, tr, cols), lambda i, me_ref: (me_ref[0], i, 0)),
                      pl.BlockSpec((n_recv, tr, cols), lambda i, me_ref: (0, i, 0))],
            out_specs=pl.BlockSpec((tr, cols), lambda i, me_ref: (i, 0))),
        out_shape=jax.ShapeDtypeStruct((rows, cols), F32),
        compiler_params=_params(("parallel",)),
    )(chip, sf, recv)


def _position():
    return lax.axis_index("x"), lax.axis_index("y"), lax.axis_index("c")


def _dh_scatter(dproj, w_in_arr_t, x, dz, g, sb_in, sb_rest, tm=512, tk=3072):
    t, d = x.shape
    nk = dproj.shape[1] // tk
    ni = t // tm
    total = ni * nk
    halves = (HALF_IN, HALF_REST)
    cuts = tuple(-(-rows // 32) * 16 for rows in halves)

    def rows_of(a, p):
        return cuts[a] if p == 0 else halves[a] - cuts[a]

    def piece(a, p):
        return pl.ds(0, cuts[a]) if p == 0 else pl.ds(cuts[a], halves[a] - cuts[a])

    def body(dp_ref, w_ref, x_ref, dz_ref, g_ref, sbin_ref, sbrest_ref, dx_ref, dg_ref, db_ref, rin_ref, rrest_ref,
             acc_ref, pay_in0, pay_in1, pay_rest0, pay_rest1, own_in0, own_in1, own_rest0, own_rest1,
             send_sems, recv_sems, local_sems):
        step = pl.program_id(0) * nk + pl.program_id(1)
        kk = pl.program_id(1)
        px, py, pc = _position()
        me = 2 * px + py
        srcs = (sbin_ref, sbrest_ref)
        dsts = (rin_ref, rrest_ref)
        pays = ((pay_in0, pay_in1), (pay_rest0, pay_rest1))
        owns = ((own_in0, own_in1), (own_rest0, own_rest1))
        via = (2, 1)
        onto = (1, 2)

        def peer(k):
            return (px ^ (k >> 1), py ^ (k & 1), pc)

        def payload(a, p):
            return pltpu.make_async_remote_copy(
                src_ref=srcs[a].at[me ^ 3, piece(a, p)], dst_ref=pays[a][p], send_sem=send_sems.at[2 * a + p],
                recv_sem=recv_sems.at[2 * a + p], device_id=peer(via[p]), device_id_type=MESH)

        def direct(a, k, p, src):
            sem = 4 + 4 * a + 2 * (k - 1) + p
            return pltpu.make_async_remote_copy(
                src_ref=src, dst_ref=dsts[a].at[k - 1, piece(a, p)], send_sem=send_sems.at[sem],
                recv_sem=recv_sems.at[sem], device_id=peer(k), device_id_type=MESH)

        def plain(a, k, p):
            return direct(a, k, p, srcs[a].at[me ^ k, piece(a, p)])

        def stage(a, p):
            return pltpu.make_async_copy(srcs[a].at[me ^ onto[p], piece(a, p)], owns[a][p], local_sems.at[2 * a + p])

        @pl.when(step == 0)
        def _():
            dg_ref[...] = jnp.zeros_like(dg_ref)
            db_ref[...] = jnp.zeros_like(db_ref)
            for a in range(2):
                for p in range(2):
                    payload(a, p).start()
                    stage(a, p).start()
                plain(a, 1, 1).start()
                plain(a, 2, 0).start()

        @pl.when(step == (5 * total) // 8)
        def _():
            for a in range(2):
                for p in range(2):
                    payload(a, p).wait_recv()
                    stage(a, p).wait()
                    owns[a][p][...] = (owns[a][p][...].astype(F32) + pays[a][p][...].astype(F32)).astype(BF16)
                    direct(a, onto[p], p, owns[a][p]).start()

        part = _dot(dp_ref[...], w_ref[...])

        @pl.when(kk == 0)
        def _():
            acc_ref[...] = part

        @pl.when(kk > 0)
        def _():
            acc_ref[...] += part

        @pl.when(kk == nk - 1)
        def _():
            xh, rstd = _ln_hat(x_ref[...])
            dht = acc_ref[...] + DEEPNORM_ALPHA * dz_ref[...]
            dg_ref[...] += _colsum(dht * xh)
            db_ref[...] += _colsum(dht)
            dx_ref[...] = _ln_bwd_rows(dht * g_ref[...], xh, rstd)

        @pl.when(step == total - 1)
        def _():
            for a in range(2):
                for k in (1, 2):
                    for p in range(2):
                        plain(a, k, p).wait_recv()
            for a in range(2):
                for p in range(2):
                    payload(a, p).wait_send()
                    direct(a, onto[p], p, owns[a][p]).wait_send()
                plain(a, 1, 1).wait_send()
                plain(a, 2, 0).wait_send()

    tile = pl.BlockSpec((tm, d), lambda i, kk: (i, 0))
    row = pl.BlockSpec((1, d), lambda i, kk: (0, 0))
    pieces = [pltpu.VMEM((rows_of(a, p), 1024), BF16) for a in range(2) for p in range(2)]
    return pl.pallas_call(
        body, name="dh_scatter", grid=(ni, nk),
        out_shape=(jax.ShapeDtypeStruct((t, d), F32), jax.ShapeDtypeStruct((1, d), F32),
                   jax.ShapeDtypeStruct((1, d), F32),
                   jax.ShapeDtypeStruct((2, HALF_IN, 1024), BF16),
                   jax.ShapeDtypeStruct((2, HALF_REST, 1024), BF16)),
        in_specs=[pl.BlockSpec((tm, tk), lambda i, kk: (i, kk)), pl.BlockSpec((tk, d), lambda i, kk: (kk, 0)),
                  tile, tile, row, ANY, ANY],
        out_specs=(tile, row, row, ANY, ANY),
        scratch_shapes=[pltpu.VMEM((tm, d), F32)] + pieces + pieces
        + [pltpu.SemaphoreType.DMA((12,)), pltpu.SemaphoreType.DMA((12,)), pltpu.SemaphoreType.DMA((4,))],
        compiler_params=_params(("arbitrary", "arbitrary")),
    )(dproj, w_in_arr_t, x, dz, g, sb_in, sb_rest)


def _join_and_allreduce(gh_in, gh_rest, vec):
    def body(hin_ref, hrest_ref, vec_ref, oin_ref, orest_ref, sum_ref, all_ref, send_sems, recv_sems, local_sems):
        x, y, c = _position()
        srcs = (hin_ref, hrest_ref)
        dsts = (oin_ref, orest_ref)
        me = 4 * x + 2 * y + c
        all_ref[me] = vec_ref[...]

        def small(k, slot):
            return pltpu.make_async_remote_copy(
                src_ref=vec_ref, dst_ref=all_ref.at[slot], send_sem=send_sems.at[k + 1], recv_sem=recv_sems.at[k + 1],
                device_id=(x ^ (k >> 2), y ^ ((k >> 1) & 1), c ^ (k & 1)), device_id_type=MESH)

        def half(a, slot):
            return pltpu.make_async_remote_copy(
                src_ref=srcs[a], dst_ref=dsts[a].at[slot], send_sem=send_sems.at[a], recv_sem=recv_sems.at[a],
                device_id=(x, y, 1 - c), device_id_type=MESH)

        local = [pltpu.make_async_copy(srcs[a], dsts[a].at[c], local_sems.at[a]) for a in range(2)]
        remote = [half(a, c) for a in range(2)] + [small(k, me) for k in range(1, 8)]
        for cp in local + remote:
            cp.start()
        for k in range(1, 8):
            small(k, me ^ k).wait_recv()
        for a in range(2):
            half(a, 1 - c).wait_recv()
        for cp in remote:
            cp.wait_send()
        for cp in local:
            cp.wait()
        total = all_ref[0]
        for d in range(1, 8):
            total = total + all_ref[d]
        sum_ref[...] = total

    return pl.pallas_call(
        body, name="join_halves",
        out_shape=(jax.ShapeDtypeStruct((2, HALF_IN, 1024), F32),
                   jax.ShapeDtypeStruct((2, HALF_REST, 1024), F32),
                   jax.ShapeDtypeStruct(vec.shape, vec.dtype)),
        in_specs=[IN_VMEM, IN_VMEM, IN_VMEM], out_specs=(ANY, ANY, IN_VMEM),
        scratch_shapes=[pltpu.VMEM((8,) + vec.shape, vec.dtype), pltpu.SemaphoreType.DMA((9,)),
                        pltpu.SemaphoreType.DMA((9,)), pltpu.SemaphoreType.DMA((2,))],
    )(gh_in, gh_rest, vec)


def _pack_rest(w_uq, w_ukv, w_mem, w_out):
    rows = jnp.concatenate([w_uq[0].T.reshape(-1, 1024), w_ukv.reshape(-1, 1024), w_mem.reshape(-1, 1024),
                            w_out.reshape(-1, 1024)], axis=0)
    return jnp.pad(rows, ((0, ROWS_REST - ROWS_USED), (0, 0)))


def _arranged_w_in(g_in):
    z = functools.partial(jnp.zeros, dtype=g_in.dtype)
    cut = 4480 - 2 * SHARD_ROWS
    return jnp.concatenate(
        [g_in[0, :SHARD_ROWS], g_in[1, :SHARD_ROWS], g_in[2, :cut], z((64, 1024)), g_in[2, cut:cut + 32],
         z((32, 1024)), g_in[2, cut + 32:SHARD_ROWS], g_in[3, :SHARD_ROWS]], axis=0)


def _rest_weights(g_rest):
    w_uq_t = g_rest[:, 0:ROWS_UQ].reshape(768, 256)
    w_uq_pad_t = jnp.pad(w_uq_t.reshape(MLA_HEADS, MLA_QK_DIM, 256), ((0, 0), (0, 32), (0, 0))).reshape(1024, 256)
    w_ukv = jnp.concatenate([g_rest[j, ROWS_UQ:ROWS_UQ + ROWS_UKV].reshape(128, 256) for j in range(4)], axis=1)
    lo = ROWS_UQ + ROWS_UKV
    w_mem = g_rest[:, lo:lo + ROWS_MEM].reshape(4 * ROWS_MEM, 1024)
    w_out = g_rest[:, lo + ROWS_MEM:lo + ROWS_MEM + ROWS_OUT].reshape(4 * ROWS_OUT, 1024)
    return w_uq_pad_t, w_ukv, w_mem, w_out


def _dw_in_split(dproj, h, tm=1024):
    t = dproj.shape[0]
    steps = PROJ_W // tm
    gap = ROWS_IN - SHARD_ROWS
    nat = 4608 - 96
    last = 4608 + 3 * SHARD_ROWS - nat
    segments = ((0, SHARD_ROWS, 0, 0), (SHARD_ROWS, 2 * SHARD_ROWS, 1, 0), (2 * SHARD_ROWS, 4480, 2, 0),
                (4544, 4576, 2, 4480 - 2 * SHARD_ROWS), (4608, last, 2, 4512 - 2 * SHARD_ROWS), (last, PROJ_W, 3, 0))

    def pieces(j):
        out = []
        for lo, hi, chip, dst in segments:
            a, b = max(lo, j * tm), min(hi, (j + 1) * tm)
            if a < b:
                out.append((a - j * tm, chip, dst + a - lo, b - a))
        return out

    n_sem = max(len(pieces(j)) for j in range(steps))

    def body(a_ref, b_ref, o_ref, tile_ref, zero_ref, sems, pad_sems):
        i = pl.program_id(0)

        def copies(j):
            return [pltpu.make_async_copy(tile_ref.at[j % 2, pl.ds(off, n)], o_ref.at[chip, pl.ds(dst, n)],
                                          sems.at[j % 2, q])
                    for q, (off, chip, dst, n) in enumerate(pieces(j))]

        def pad_copies():
            return [pltpu.make_async_copy(zero_ref, o_ref.at[chip, pl.ds(SHARD_ROWS, gap)], pad_sems.at[chip])
                    for chip in range(4)]

        @pl.when(i == 0)
        def _():
            zero_ref[...] = jnp.zeros_like(zero_ref)
            for c in pad_copies():
                c.start()

        for j in range(2, steps):
            @pl.when(i == j)
            def _(j=j):
                for c in copies(j - 2):
                    c.wait()

        tile_ref[i % 2] = _dot_tn(a_ref[...], b_ref[...])

        for j in range(steps):
            @pl.when(i == j)
            def _(j=j):
                for c in copies(j):
                    c.start()
                if j == steps - 1:
                    for c in copies(j - 1) + copies(j) + pad_copies():
                        c.wait()

    return pl.pallas_call(
        body, name="dw_in", grid=(steps,),
        out_shape=jax.ShapeDtypeStruct((4, ROWS_IN, 1024), F32),
        in_specs=[pl.BlockSpec((t, tm), lambda i: (0, i)), pl.BlockSpec((t, 1024), lambda i: (0, 0))],
        out_specs=ANY,
        scratch_shapes=[pltpu.VMEM((2, tm, 1024), F32), pltpu.VMEM((gap, 1024), F32),
                        pltpu.SemaphoreType.DMA((2, n_sem)), pltpu.SemaphoreType.DMA((4,))],
        compiler_params=_params(("arbitrary",)),
    )(dproj, h)


def _split_rest(dw_uq_pad_t, dw_ukv, dw_mem, dw_out):
    dw_uq_t = dw_uq_pad_t.reshape(MLA_HEADS, LANES, 256)[:, :MLA_QK_DIM].reshape(4, ROWS_UQ, 1024)
    parts = [dw_uq_t, dw_ukv.reshape(128, 4, 256).transpose(1, 0, 2).reshape(4, ROWS_UKV, 1024),
             dw_mem.reshape(4, ROWS_MEM, 1024), dw_out.reshape(4, ROWS_OUT, 1024)]
    return jnp.pad(jnp.concatenate(parts, axis=1), ((0, 0), (0, ROWS_REST - ROWS_USED), (0, 0)))


def _rope_consts(rot, first, period):
    half = rot // 2
    inv_freq = np.float32(ROPE_THETA) ** (-(np.arange(0, rot, 2, dtype=np.float32) / np.float32(rot)))
    lane = np.arange(LANES) % period - first
    in_rot = (lane >= 0) & (lane < rot)
    out = np.zeros((8, LANES), np.float32)
    out[0] = np.where(in_rot, inv_freq[np.clip(lane, 0, rot - 1) % half], 0.0)
    out[1] = in_rot & (lane < half)
    out[2] = in_rot & (lane >= half)
    return jnp.asarray(out)


def _band_bias(s):
    nblk = s // BAND_Q
    starts = np.array([_band_start(i, s) for i in range(nblk)])
    uq = (np.arange(nblk)[:, None] * BAND_Q + np.arange(BAND_Q)[None, :])[:, :, None]
    uk = (starts[:, None] + np.arange(BAND_WIN)[None, :])[:, None, :]
    tiles, index, seen = [], [], {}
    for _, d in DILATED:
        length = s // d
        ok = (uq // length == uk // length) & (np.abs(uq - uk) <= 64)
        row = []
        for i in range(nblk):
            key = ok[i].tobytes()
            if key not in seen:
                seen[key] = len(tiles)
                tiles.append(np.where(ok[i], 0.0, NEG_INF).astype(np.float32))
            row.append(seen[key])
        index.append(row)
    return jnp.asarray(np.stack(tiles, axis=0)), index


def _forward_backward(h, h32, proj, trig, rope_consts, x, mem, target, weights, gains):
    w_uq_pad_t, w_ukv, w_mem, w_out = weights
    g_emb, b_emb, g_cq, g_ckv, g_out_a, g_out_b, g_out_m, g_post, b_post = gains
    nb, s, d = x.shape
    t = nb * s
    x2 = x.reshape(t, d)
    mem2 = mem.reshape(nb * N_MEM, d)
    tgt2 = target.reshape(t, d)
    rope_a, rope_b = rope_consts
    bias, bias_index = _band_bias(s)
    scales = (0.125, MLA_QK_DIM ** -0.5, 128 ** -0.5)

    qa, ka, va, qb, kb, vb, qm = _prep(proj, trig, w_uq_pad_t, w_ukv, g_cq, g_ckv, rope_a, rope_b, scales)
    mkv = _mm(mem2, w_mem, BF16, nb * N_MEM, 1024, 1024, "mem_kv")

    cfg_b = dict(nb=nb, s=s, sk=s, heads=8, voff=0, bq=256)
    cfg_m = dict(nb=nb, s=s, sk=N_MEM, heads=4, hpb=2, voff=4, bq=1024)
    ya, lse_a, qkv_ordered = _dilated_fwd(qa, ka, va, bias, bias_index, nb=nb, s=s, name="attn_a_fwd")
    yb, lse_b = _attn_fwd(qb, kb, vb, name="attn_b_fwd", hpb=4, **cfg_b)
    ym, lse_m = _attn_fwd(qm, mkv, mkv, name="attn_m_fwd", **cfg_m)

    (y, dz, doa, dob, dom, dga, dgb, dgm, loss, dg_post, db_post, dg_a, dg_b, dg_m) = _post(
        h32, ya, yb, ym, proj, tgt2, w_out, g_out_a, g_out_b, g_out_m, g_post, b_post)

    dqa, dka, dva = _dilated_bwd(qa, ka, va, qkv_ordered, ya, doa, lse_a, bias, bias_index, nb=nb, s=s, scale=scales[0],
                                 name="attn_a_bwd")
    dqb, dkb, dvb = _attn_bwd(qb, kb, vb, yb, dob, lse_b, name="attn_b_bwd", scale=scales[1], hpb=4, **cfg_b)
    dqm, dmk, dmv = _attn_bwd(qm, mkv, mkv, ym, dom, lse_m, name="attn_m_bwd", scale=scales[2], **cfg_m)
    dmkv = jnp.concatenate([dmk, dmv], axis=1)

    dproj, dw_uq_pad_t, dw_ukv, dg_cq, dg_ckv = _prep_bwd(
        dqa, dka, dva, dqb, dkb, dvb, dqm, dga, dgb, dgm, proj, trig, w_uq_pad_t, w_ukv, g_cq, g_ckv, rope_a, rope_b)

    small_rows = (dg_cq, dg_ckv, loss, dg_a, dg_b, dg_m, dg_post, db_post)
    return (dproj, h, y, dz, dw_uq_pad_t, dw_ukv, mem2, dmkv), x2, small_rows


def _weight_grads(operands, core):
    dproj, h, y, dz, dw_uq_pad_t, dw_ukv, mem2, dmkv = operands
    g_in = _dw_in_split(dproj, h)
    dw_out, r_in = _mm(y, dz, F32, 1024, 1024, 2048, "dw_out", mode="tn",
                       ride=_half_to_sibling(g_in.reshape(4, 2, HALF_IN, 1024)))
    dw_mem = _mm(mem2, dmkv, F32, 1024, 1024, mem2.shape[0], "dw_mem", mode="tn")
    g_rest = _split_rest(dw_uq_pad_t, dw_ukv, dw_mem, dw_out)
    sf_in, sb_in, r_rest = _core_sum(g_in, r_in, core, HALF_IN, HALF_IN // 2, "core_sum_in",
                                     ride=_half_to_sibling(g_rest.reshape(4, 2, HALF_REST, 1024)))
    sf_rest, sb_rest = _core_sum(g_rest, r_rest, core, HALF_REST, HALF_REST, "core_sum_rest")
    return sf_in, sb_in, sf_rest, sb_rest


def _small_block(dg_emb, db_emb, small_rows):
    dg_cq, dg_ckv, loss, dg_a, dg_b, dg_m, dg_post, db_post = small_rows
    row2 = jnp.concatenate([dg_cq, dg_ckv, loss, jnp.zeros((1, 512), F32)], axis=1)
    return jnp.concatenate([dg_emb, db_emb, row2, dg_a, jnp.concatenate([dg_b, dg_m], axis=1), dg_post, db_post,
                            jnp.zeros((1, 1024), F32)], axis=0)


def _pack_small(g_emb, b_emb, g_cq, g_ckv, g_out_a, g_out_b, g_out_m, g_post, b_post):
    row2 = jnp.concatenate([g_cq.reshape(1, -1), g_ckv.reshape(1, -1), jnp.zeros((1, 640), F32)], axis=1)
    return jnp.concatenate([g_emb.reshape(1, -1), b_emb.reshape(1, -1), row2, g_out_a.reshape(1, -1),
                            jnp.concatenate([g_out_b.reshape(1, -1), g_out_m.reshape(1, -1)], axis=1),
                            g_post.reshape(1, -1), b_post.reshape(1, -1), jnp.zeros((1, 1024), F32)], axis=0)


def kernel(x, mem, positions, g_emb, b_emb, w_in, g_cq, g_ckv, w_uq, w_ukv, w_mem_kv, g_out_a, g_out_b, g_out_m, w_out, g_post, b_post, loss_target, m_g_emb, m_b_emb, m_w_in, m_g_cq, m_g_ckv, m_w_uq, m_w_ukv, m_w_mem_kv, m_g_out_a, m_g_out_b, m_g_out_m, m_w_out, m_g_post, m_b_post, v_g_emb, v_b_emb, v_w_in, v_g_cq, v_g_ckv, v_w_uq, v_w_ukv, v_w_mem_kv, v_g_out_a, v_g_out_b, v_g_out_m, v_w_out, v_g_post, v_b_post):
    w_rest = _pack_rest(w_uq, w_ukv, w_mem_kv, w_out)
    w_in_t = w_in[0].T
    w_in_b = jnp.pad(w_in_t.astype(BF16), ((0, ROWS_IN - SHARD_ROWS), (0, 0)))
    gains = (g_emb.reshape(1, -1), b_emb.reshape(1, -1), g_cq, g_ckv, g_out_a, g_out_b, g_out_m, g_post, b_post)
    rope_consts = (_rope_consts(16, 0, 64), _rope_consts(32, 64, 128))
    h, h32, trig, gathered_in = _ln_fwd(x.reshape(-1, D_MODEL), gains[0], gains[1],
                                        positions.reshape(-1, 1).astype(F32), *rope_consts,
                                        ride=_gather_ride(w_in_b.reshape(2, HALF_IN, 1024), spread=False))
    w_in_arr_t = _arranged_w_in(gathered_in.reshape(4, ROWS_IN, 1024))
    proj, gathered_rest = _mm(h, w_in_arr_t, F32, 1024, 2048, 1024, "in_proj", mode="nt",
                              ride=_gather_ride(w_rest.astype(BF16).reshape(2, HALF_REST, 1024), spread=True))
    weights = _rest_weights(gathered_rest.reshape(4, ROWS_REST, 1024))
    operands, x2, small_rows = _forward_backward(h, h32, proj, trig, rope_consts, x, mem, loss_target, weights,
                                                 gains)

    core = lax.axis_index("c").astype(jnp.int32).reshape(1)
    chip = (2 * lax.axis_index("x") + lax.axis_index("y")).astype(jnp.int32).reshape(1)
    sf_in, sb_in, sf_rest, sb_rest = _weight_grads(operands, core)
    grad_x, dg_emb, db_emb, rb_in, rb_rest = _dh_scatter(operands[0], w_in_arr_t, x2, operands[3], gains[0],
                                                         sb_in, sb_rest)
    gh_in = _chip_sum(sf_in, rb_in, chip, HALF_IN, HALF_IN // 2, "chip_sum_in")
    gh_rest = _chip_sum(sf_rest, rb_rest, chip, HALF_REST, HALF_REST, "chip_sum_rest")
    grad_in, grad_rest, small_sum = _join_and_allreduce(gh_in, gh_rest, _small_block(dg_emb, db_emb, small_rows))
    grad_in = grad_in.reshape(ROWS_IN, 1024)
    grad_rest = grad_rest.reshape(ROWS_REST, 1024)

    *big_in, grad_x = _adamw(grad_in, w_in_t, m_w_in[0].T, v_w_in[0].T, SHARD_ROWS // 3, "adamw_in", grad_x)
    def rest_parts(a_uq, a_ukv, a_mem, a_out):
        return [a_uq[0].T.reshape(ROWS_UQ, 1024), a_ukv.reshape(ROWS_UKV, 1024), a_mem[0], a_out[0]]

    uq, ukv, wmem, wout = _adamw_pieces(
        grad_rest, rest_parts(w_uq, w_ukv, w_mem_kv, w_out), rest_parts(m_w_uq, m_w_ukv, m_w_mem_kv, m_w_out),
        rest_parts(v_w_uq, v_w_ukv, v_w_mem_kv, v_w_out), REST_PIECES, "adamw_rest")
    sm = _adamw_pieces(
        small_sum,
        _pack_small(g_emb, b_emb, g_cq, g_ckv, g_out_a, g_out_b, g_out_m, g_post, b_post),
        _pack_small(m_g_emb, m_b_emb, m_g_cq, m_g_ckv, m_g_out_a, m_g_out_b, m_g_out_m, m_g_post, m_b_post),
        _pack_small(v_g_emb, v_b_emb, v_g_cq, v_g_ckv, v_g_out_a, v_g_out_b, v_g_out_m, v_g_post, v_b_post),
        SMALL_PIECES, "adamw_small")
    loss = small_sum[2, 384]

    def ordered(kind):
        s_gemb, s_bemb, s_gcq, s_gckv, s_ga, s_gb, s_gm, s_gpost, s_bpost = [piece[kind] for piece in sm]
        return [s_gemb.reshape(-1), s_bemb.reshape(-1), big_in[kind].T[None], s_gcq, s_gckv,
                uq[kind].reshape(192, 256).T[None], ukv[kind].reshape(1, 128, 256), wmem[kind][None], s_ga, s_gb,
                s_gm, wout[kind][None], s_gpost, s_bpost]

    return (loss, grad_x.reshape(x.shape), *ordered(0), *ordered(1), *ordered(2), *ordered(3))
```

```python
import functools
import math

import jax
import jax.numpy as jnp
import numpy as np
from jax import lax
from jax.experimental import pallas as pl
from jax.experimental.pallas import tpu as pltpu

F32 = jnp.float32
BF16 = jnp.bfloat16
MESH = pl.DeviceIdType.MESH
ANY = pl.BlockSpec(memory_space=pl.ANY)
IN_VMEM = pl.BlockSpec(memory_space=pltpu.VMEM)

D_MODEL = 1024
A_WIDTH = 1024
MLA_HEADS = 8
MLA_Q_RANK = 256
MLA_KV_RANK = 128
MLA_QK_DIM = 96
MEM_WIDTH = 512
N_MEM = 256
ROPE_THETA = 500000.0
NORM_EPS = 1e-5
NEG_INF = -1e30
DEEPNORM_ALPHA = 2.0 ** 0.25
DILATED = ((64, 1), (256, 4), (1024, 16))

ADAM_LR = 0.001
ADAM_B1 = 0.9
ADAM_B2 = 0.999
ADAM_EPS = 1e-08
ADAM_WD = 0.01
ADAM_STEP = 10

LANES = 128
VMEM_LIMIT = 56 * 1024 * 1024
LOG2E = math.log2(math.e)
LN2 = math.log(2.0)

PROJ_W = 6144
COL_CQ = 4096
COL_BG = 4608
COL_MQ = 5120
COL_MG = 5632

SHARD_ROWS = 1512
ROWS_IN = 1536
ROWS_UQ, ROWS_UKV, ROWS_MEM, ROWS_OUT = 48, 32, 256, 512
ROWS_USED = ROWS_UQ + ROWS_UKV + ROWS_MEM + ROWS_OUT
ROWS_REST = 864
HALF_IN = ROWS_IN // 2
HALF_REST = ROWS_REST // 2
REST_PIECES = ((0, 48, 0, 1024), (48, 80, 0, 1024), (80, 336, 0, 1024), (336, 848, 0, 1024))
SMALL_PIECES = ((0, 1, 0, 1024), (1, 2, 0, 1024), (2, 3, 0, 256), (2, 3, 256, 384), (3, 4, 0, 1024), (4, 5, 0, 512),
                (4, 5, 512, 1024), (5, 6, 0, 1024), (6, 7, 0, 1024))


def _params(sem=None, vmem=VMEM_LIMIT):
    return pltpu.CompilerParams(dimension_semantics=sem, vmem_limit_bytes=vmem)


def _dot(a, b):
    return jnp.dot(a, b, preferred_element_type=F32)


def _dot_nt(a, b):
    return lax.dot_general(a, b, (((1,), (1,)), ((), ())), preferred_element_type=F32)


def _dot_tn(a, b):
    return lax.dot_general(a, b, (((0,), (0,)), ((), ())), preferred_element_type=F32)


def _ln_hat(x):
    mu = jnp.mean(x, axis=-1, keepdims=True)
    xc = x - mu
    var = jnp.mean(xc * xc, axis=-1, keepdims=True)
    rstd = lax.rsqrt(var + NORM_EPS)
    return xc * rstd, rstd


def _ln_bwd_rows(dxh, xh, rstd):
    return rstd * (dxh - jnp.mean(dxh, axis=-1, keepdims=True) - xh * jnp.mean(dxh * xh, axis=-1, keepdims=True))


def _rms_hat(x, width):
    ms = jnp.sum(x * x, axis=-1, keepdims=True) * (1.0 / width)
    r = lax.rsqrt(ms + NORM_EPS)
    return x * r, r


def _rms_bwd(u, xh, r, width):
    return r * (u - xh * (jnp.sum(u * xh, axis=-1, keepdims=True) * (1.0 / width)))


def _colsum(v):
    return jnp.sum(v, axis=0, keepdims=True)


def _rope_tables(cos, sin, consts):
    return cos, sin * consts[2:3, :], -sin * consts[1:2, :]


def _rope(x, tables, half, inverse=False):
    c, s_up, s_dn = tables
    if inverse:
        s_up, s_dn = -s_up, -s_dn
    return x * c + pltpu.roll(x, half, 1) * s_up + pltpu.roll(x, LANES - half, 1) * s_dn


def _ln_fwd(x, g, b, pos, rope_a, rope_b, tm=512, ride=None):
    t, d = x.shape
    n_in = len(ride.args) if ride else 0
    n_out = len(ride.out_shapes) if ride else 0
    steps = t // tm

    def body(x_ref, g_ref, b_ref, pos_ref, ra_ref, rb_ref, *rest):
        h_ref, h32_ref, trig_ref = rest[n_in:n_in + 3]
        if ride:
            i = pl.program_id(0)
            ride.run(i, steps, rest[:n_in], rest[n_in + 3:n_in + 3 + n_out], rest[n_in + 3 + n_out:])
        xh, _ = _ln_hat(x_ref[...])
        h = xh * g_ref[...] + b_ref[...]
        h32_ref[...] = h
        h_ref[...] = h.astype(BF16)
        for j, consts in enumerate((ra_ref, rb_ref)):
            ang = pos_ref[...] * consts[0:1, :]
            trig_ref[:, 2 * j * LANES:(2 * j + 1) * LANES] = jnp.cos(ang)
            trig_ref[:, (2 * j + 1) * LANES:(2 * j + 2) * LANES] = jnp.sin(ang)

    row = pl.BlockSpec((1, d), lambda i: (0, 0))
    tile = pl.BlockSpec((tm, d), lambda i: (i, 0))
    consts = pl.BlockSpec((8, LANES), lambda i: (0, 0))
    trig_tile = pl.BlockSpec((tm, 4 * LANES), lambda i: (i, 0))
    in_specs = [tile, row, row, pl.BlockSpec((tm, 1), lambda i: (i, 0)), consts, consts]
    shapes = (jax.ShapeDtypeStruct((t, d), BF16), jax.ShapeDtypeStruct((t, d), F32),
              jax.ShapeDtypeStruct((t, 4 * LANES), F32))
    if not ride:
        return pl.pallas_call(
            body, name="ln_fwd", grid=(steps,), out_shape=shapes, in_specs=in_specs,
            out_specs=(tile, tile, trig_tile), compiler_params=_params(("parallel",)),
        )(x, g, b, pos, rope_a, rope_b)
    return pl.pallas_call(
        body, name="ln_fwd", grid=(steps,),
        out_shape=(*shapes, *ride.out_shapes),
        in_specs=in_specs + ride.in_specs, out_specs=(tile, tile, trig_tile) + (ANY,) * n_out,
        scratch_shapes=ride.scratch(),
        compiler_params=_params(("arbitrary",)),
    )(x, g, b, pos, rope_a, rope_b, *ride.args)


class _Ride:
    def __init__(self, args, out_shapes, sem_counts, plan, in_specs=None, spread=True):
        self.args, self.out_shapes, self.plan = list(args), list(out_shapes), plan
        self.sem_counts = sem_counts
        self.in_specs = in_specs or [ANY] * len(self.args)
        self.spread = spread

    def scratch(self):
        return [pltpu.SemaphoreType.DMA((n,)) for n in self.sem_counts]

    def run(self, step, total, in_refs, out_refs, sems):
        count = len(self.plan(in_refs, out_refs, *sems))
        at = [(k * (total - 1)) // (count - 1) if self.spread or k == 0 else total - 1 for k in range(count)]
        for when in sorted(set(at)):
            @pl.when(step == when)
            def _(when=when):
                stages = self.plan(in_refs, out_refs, *sems)
                for k in range(count):
                    if at[k] == when:
                        stages[k]()


def _mm(a, b, out_dtype, tm, tn, tk, name, mode="nn", ride=None):
    if mode == "tn":
        k, m = a.shape
    else:
        m, k = a.shape
    n = b.shape[0] if mode == "nt" else b.shape[1]
    nk = k // tk
    nj, ni = n // tn, m // tm
    n_in = len(ride.args) if ride else 0
    n_out = len(ride.out_shapes) if ride else 0

    def body(a_ref, b_ref, *rest):
        o_ref = rest[n_in]
        acc_ref = rest[n_in + 1 + n_out]
        if ride:
            j, i, kk = pl.program_id(0), pl.program_id(1), pl.program_id(2)
            ride.run((j * ni + i) * nk + kk, nj * ni * nk, rest[:n_in], rest[n_in + 1:n_in + 1 + n_out],
                     rest[n_in + 2 + n_out:])
        av = a_ref[...].astype(BF16)
        bv = b_ref[...].astype(BF16)
        part = _dot_tn(av, bv) if mode == "tn" else _dot_nt(av, bv) if mode == "nt" else _dot(av, bv)
        if nk == 1:
            o_ref[...] = part.astype(out_dtype)
        else:
            kk = pl.program_id(2)

            @pl.when(kk == 0)
            def _():
                acc_ref[...] = part

            @pl.when(kk > 0)
            def _():
                acc_ref[...] += part

            @pl.when(kk == nk - 1)
            def _():
                o_ref[...] = acc_ref[...].astype(out_dtype)

    a_spec = (pl.BlockSpec((tk, tm), lambda j, i, kk: (kk, i)) if mode == "tn"
              else pl.BlockSpec((tm, tk), lambda j, i, kk: (i, kk)))
    b_spec = (pl.BlockSpec((tn, tk), lambda j, i, kk: (j, kk)) if mode == "nt"
              else pl.BlockSpec((tk, tn), lambda j, i, kk: (kk, j)))
    o_spec = pl.BlockSpec((tm, tn), lambda j, i, kk: (i, j))
    o_shape = jax.ShapeDtypeStruct((m, n), out_dtype)
    if not ride:
        return pl.pallas_call(
            body, name=name, grid=(nj, ni, nk), out_shape=o_shape, in_specs=[a_spec, b_spec], out_specs=o_spec,
            scratch_shapes=[pltpu.VMEM((tm, tn), F32)],
            compiler_params=_params(("parallel", "parallel", "arbitrary")),
        )(a, b)
    return pl.pallas_call(
        body, name=name, grid=(nj, ni, nk),
        out_shape=(o_shape, *ride.out_shapes),
        in_specs=[a_spec, b_spec] + ride.in_specs,
        out_specs=(o_spec,) + (ANY,) * n_out,
        scratch_shapes=[pltpu.VMEM((tm, tn), F32)] + ride.scratch(),
        compiler_params=_params(("arbitrary", "arbitrary", "arbitrary")),
    )(a, b, *ride.args)


def _prep(proj, trig, w_uq, w_ukv, g_cq, g_ckv, rope_a, rope_b, scales, tm=512):
    t = proj.shape[0]
    sc_a, sc_b, sc_m = (s * LOG2E for s in scales)

    steps = t // tm
    streams = ((0, 0, 1024), (0, 1024, 1024), (0, 2048, 1024), (0, COL_CQ, 512), (0, COL_MQ, 512), (1, 0, 4 * LANES))

    def body(proj_ref, trig_hbm, wuq_ref, wukv_ref, gcq_ref, gckv_ref, ra_ref, rb_ref,
             qa_ref, ka_ref, va_ref, qb_ref, kb_ref, vb_ref, qm_ref, *scratch):
        rings, sems = scratch[:len(streams)], scratch[len(streams)]
        s = pl.program_id(0)

        def fetch(step):
            return [pltpu.make_async_copy(
                (proj_ref, trig_hbm)[src].at[pl.ds(step * tm, tm), pl.ds(c0, width)], ring.at[step % 3],
                sems.at[step % 3, q]) for q, ((src, c0, width), ring) in enumerate(zip(streams, rings))]

        @pl.when(s == 0)
        def _():
            for cp in fetch(0) + fetch(1):
                cp.start()

        @pl.when(s + 2 < steps)
        def _():
            for cp in fetch(s + 2):
                cp.start()

        for cp in fetch(s):
            cp.wait()
        aq_ref, ak_ref, av_ref, bs_ref, mq_ref, trig_ref = [ring.at[s % 3] for ring in rings]
        ta = _rope_tables(trig_ref[:, 0:LANES], trig_ref[:, LANES:2 * LANES], ra_ref[...])
        tb = _rope_tables(trig_ref[:, 2 * LANES:3 * LANES], trig_ref[:, 3 * LANES:4 * LANES], rb_ref[...])
        for j in range(A_WIDTH // LANES):
            sl = slice(j * LANES, (j + 1) * LANES)
            qa_ref[:, sl] = (_rope(aq_ref[:, sl], ta, 8) * sc_a).astype(BF16)
            ka_ref[:, sl] = _rope(ak_ref[:, sl], ta, 8).astype(BF16)
        va_ref[...] = av_ref[...].astype(BF16)
        qm_ref[...] = (mq_ref[...] * sc_m).astype(BF16)

        cq_hat, _ = _rms_hat(bs_ref[:, 0:MLA_Q_RANK], MLA_Q_RANK)
        cqn = (cq_hat * gcq_ref[...]).astype(BF16)
        ckv_hat, _ = _rms_hat(bs_ref[:, MLA_Q_RANK:MLA_Q_RANK + MLA_KV_RANK], MLA_KV_RANK)
        ckvn = (ckv_hat * gckv_ref[...]).astype(BF16)
        qfull = _dot_nt(cqn, wuq_ref[...])
        kv = _dot(ckvn, wukv_ref[...])
        kr = _rope(bs_ref[:, 384:512], tb, 16)
        lane = lax.broadcasted_iota(jnp.int32, (1, LANES), 1)
        low = lane < 64
        for h in range(MLA_HEADS):
            sl = slice(h * LANES, (h + 1) * LANES)
            qb_ref[:, sl] = (_rope(qfull[:, sl], tb, 16) * sc_b).astype(BF16)
            kb_ref[:, sl] = jnp.where(low, kv[:, sl], kr).astype(BF16)
            vb_ref[:, sl] = jnp.where(low, 0.0, kv[:, sl]).astype(BF16)

    def col(width, idx):
        return pl.BlockSpec((tm, width), lambda i: (i, idx))

    def full(shape):
        return pl.BlockSpec(shape, lambda i: (0, 0))

    wide = jax.ShapeDtypeStruct((t, 1024), BF16)
    return pl.pallas_call(
        body, name="prep", grid=(steps,),
        out_shape=(wide, wide, wide, wide, wide, wide,
                   jax.ShapeDtypeStruct((t, MEM_WIDTH), BF16)),
        in_specs=[ANY, ANY, full((1024, MLA_Q_RANK)), full((MLA_KV_RANK, 1024)),
                  full((1, MLA_Q_RANK)), full((1, MLA_KV_RANK)), full((8, LANES)), full((8, LANES))],
        out_specs=(col(1024, 0),) * 6 + (col(MEM_WIDTH, 0),),
        scratch_shapes=[pltpu.VMEM((3, tm, width), F32) for _, _, width in streams]
        + [pltpu.SemaphoreType.DMA((3, len(streams)))],
        compiler_params=_params(("arbitrary",)),
    )(proj, trig, w_uq, w_ukv, g_cq, g_ckv, rope_a, rope_b)


def _attn_fwd(q, k, v, *, nb, s, sk, heads, hpb, voff, bq, name):
    nq = s // bq
    width = hpb * LANES
    vblk = voff // hpb

    def body(q_ref, k_ref, v_ref, o_ref, lse_ref):
        for h in range(hpb):
            sl = slice(h * LANES, (h + 1) * LANES)
            sc = _dot_nt(q_ref[:, sl], k_ref[:, sl])
            m = jnp.max(sc, axis=1, keepdims=True)
            p = jnp.exp2(sc - m)
            l = jnp.sum(p, axis=1, keepdims=True)
            o_ref[:, sl] = _dot(p.astype(BF16), v_ref[:, sl]) / l
            lse_ref[:, sl] = jnp.broadcast_to(m + jnp.log(l) * LOG2E, (bq, LANES))

    out = jax.ShapeDtypeStruct((nb * s, heads * LANES), F32)
    ospec = pl.BlockSpec((bq, width), lambda b, i, g: (b * nq + i, g))
    return pl.pallas_call(
        body, name=name, grid=(nb, nq, heads // hpb),
        out_shape=(out, out),
        in_specs=[ospec, pl.BlockSpec((sk, width), lambda b, i, g: (b, g)),
                  pl.BlockSpec((sk, width), lambda b, i, g: (b, vblk + g))],
        out_specs=(ospec, ospec),
        compiler_params=_params(("parallel", "parallel", "parallel")),
    )(q, k, v)


def _attn_bwd(q, k, v, o, do, lse, *, nb, s, sk, heads, hpb, voff, scale, bq, name):
    nq = s // bq
    width = hpb * LANES
    vblk = voff // hpb

    def body(q_ref, k_ref, v_ref, o_ref, do_ref, lse_ref, dq_ref, dk_ref, dv_ref, dk_acc, dv_acc):
        i = pl.program_id(2)

        @pl.when(i == 0)
        def _():
            dk_acc[...] = jnp.zeros_like(dk_acc)
            dv_acc[...] = jnp.zeros_like(dv_acc)

        for h in range(hpb):
            sl = slice(h * LANES, (h + 1) * LANES)
            qh = q_ref[:, sl]
            kk = k_ref[:, sl]
            doh = do_ref[:, sl]
            delta = jnp.sum(doh.astype(F32) * o_ref[:, sl], axis=1, keepdims=True)
            p = jnp.exp2(_dot_nt(qh, kk) - lse_ref[:, h * LANES:h * LANES + 1])
            ds = (p * (_dot_nt(doh, v_ref[:, sl]) - delta)).astype(BF16)
            dq_ref[:, sl] = (_dot(ds, kk) * scale).astype(BF16)
            dk_acc[:, sl] += _dot_tn(ds, qh)
            dv_acc[:, sl] += _dot_tn(p.astype(BF16), doh)

        @pl.when(i == nq - 1)
        def _():
            dk_ref[...] = (dk_acc[...] * LN2).astype(BF16)
            dv_ref[...] = dv_acc[...].astype(BF16)

    qspec = pl.BlockSpec((bq, width), lambda b, g, i: (b * nq + i, g))
    kv_spec = pl.BlockSpec((sk, width), lambda b, g, i: (b, g))
    dq_shape = jax.ShapeDtypeStruct((nb * s, heads * LANES), BF16)
    dkv_shape = jax.ShapeDtypeStruct((nb * sk, heads * LANES), BF16)
    return pl.pallas_call(
        body, name=name, grid=(nb, heads // hpb, nq),
        out_shape=(dq_shape, dkv_shape, dkv_shape),
        in_specs=[qspec, kv_spec, pl.BlockSpec((sk, width), lambda b, g, i: (b, vblk + g)), qspec, qspec, qspec],
        out_specs=(qspec, kv_spec, kv_spec),
        scratch_shapes=[pltpu.VMEM((sk, width), F32), pltpu.VMEM((sk, width), F32)],
        compiler_params=_params(("parallel", "parallel", "arbitrary")),
    )(q, k, v, o, do, lse)


BAND_Q = 128
BAND_WIN = 256


def _band_start(i, s):
    return min(max(i * BAND_Q - 64, 0), s - BAND_WIN)


def _to_pattern_order(src_ref, dst_ref, stage_ref, s, d):
    length = s // d
    stage_ref[...] = src_ref[...].astype(F32)
    for r in range(d):
        dst_ref[r * length:(r + 1) * length, :] = stage_ref[pl.ds(r, length, stride=d), :].astype(dst_ref.dtype)


def _dilated_fwd(q, k, v, bias, bias_index, *, nb, s, name):
    nblk = s // BAND_Q
    npat = len(DILATED)

    def body(q_ref, k_ref, v_ref, bias_ref, o_ref, lse_ref, *rest):
        ordered = rest[:3 * (npat - 1)]
        stage_ref, op_ref, lp_ref, on_ref, ln_ref = rest[3 * (npat - 1):]
        lane = lax.broadcasted_iota(jnp.int32, (1, LANES), 1)
        first = lane < 64
        for p, (_, d) in enumerate(DILATED):
            if d == 1:
                qs, ks, vs = q_ref, k_ref, v_ref
            else:
                qs, ks, vs = ordered[3 * (p - 1):3 * p]
                for src, dst in ((q_ref, qs), (k_ref, ks), (v_ref, vs)):
                    _to_pattern_order(src, dst, stage_ref, s, d)
            for i in range(nblk):
                u0 = i * BAND_Q
                st = _band_start(i, s)
                qi = qs[u0:u0 + BAND_Q, :]
                kw = ks[st:st + BAND_WIN, :]
                vw = vs[st:st + BAND_WIN, :]
                zero = jnp.zeros_like(qi)
                q2 = jnp.concatenate([jnp.where(first, qi, zero), jnp.where(first, zero, qi)], axis=0)
                sc = _dot_nt(q2, kw)
                b = bias_ref[bias_index[p][i]]
                halves = []
                for h in range(2):
                    sh = sc[h * BAND_Q:(h + 1) * BAND_Q] + b
                    m = jnp.max(sh, axis=1, keepdims=True)
                    pr = jnp.exp2(sh - m)
                    l = jnp.sum(pr, axis=1, keepdims=True)
                    halves.append((pr.astype(BF16), l, m + jnp.log(l) * LOG2E))
                o2 = _dot(jnp.concatenate([halves[0][0], halves[1][0]], axis=0), vw)
                o_blk = jnp.where(first, o2[:BAND_Q] / halves[0][1], o2[BAND_Q:] / halves[1][1])
                lse_blk = jnp.where(first, jnp.broadcast_to(halves[0][2], (BAND_Q, LANES)),
                                    jnp.broadcast_to(halves[1][2], (BAND_Q, LANES)))
                op_ref[p, u0:u0 + BAND_Q, :] = o_blk
                lp_ref[p, u0:u0 + BAND_Q, :] = lse_blk
            if d > 1:
                length = s // d
                for r in range(d):
                    on_ref.at[p - 1][pl.ds(r, length, stride=d), :] = op_ref[p, r * length:(r + 1) * length, :]
                    ln_ref.at[p - 1][pl.ds(r, length, stride=d), :] = lp_ref[p, r * length:(r + 1) * length, :]
        lses = [lp_ref[0]] + [ln_ref[p] for p in range(npat - 1)]
        outs = [op_ref[0]] + [on_ref[p] for p in range(npat - 1)]
        m = functools.reduce(jnp.maximum, lses)
        ws = [jnp.exp2(l - m) for l in lses]
        den = functools.reduce(lambda a, c: a + c, ws)
        o_ref[...] = functools.reduce(lambda a, c: a + c, [w * o for w, o in zip(ws, outs)]) / den
        lse_ref[...] = m + jnp.log(den) * LOG2E

    blk = pl.BlockSpec((s, LANES), lambda b, g: (b, g))
    out = jax.ShapeDtypeStruct((nb * s, A_WIDTH), F32)
    copy = jax.ShapeDtypeStruct((nb * s, A_WIDTH), BF16)
    n_copies = 3 * (npat - 1)
    res = pl.pallas_call(
        body, name=name, grid=(nb, A_WIDTH // LANES),
        out_shape=(out, out) + (copy,) * n_copies,
        in_specs=[blk, blk, blk, pl.BlockSpec(bias.shape, lambda b, g: (0, 0, 0))],
        out_specs=(blk, blk) + (blk,) * n_copies,
        scratch_shapes=[pltpu.VMEM((s, LANES), F32), pltpu.VMEM((npat, s, LANES), F32),
                        pltpu.VMEM((npat, s, LANES), F32), pltpu.VMEM((npat - 1, s, LANES), F32),
                        pltpu.VMEM((npat - 1, s, LANES), F32)],
        compiler_params=_params(("parallel", "parallel")),
    )(q, k, v, bias)
    return res[0], res[1], res[2:]


def _dilated_bwd(q, k, v, ordered, o, do, lse, bias, bias_index, *, nb, s, scale, name):
    nblk = s // BAND_Q
    npat = len(DILATED)
    n_copies = 3 * (npat - 1)

    def body(q_ref, k_ref, v_ref, *rest):
        ordered_refs = rest[:n_copies]
        (o_ref, do_ref, lse_ref, bias_ref, dq_out, dk_out, dv_out, stage_ref, rs_ref, dop_ref, rsp_ref,
         dqp_ref, dkp_ref, dvp_ref, dq_ref, dk_ref, dv_ref, nat_ref) = rest[n_copies:]
        lane = lax.broadcasted_iota(jnp.int32, (1, LANES), 1)
        first = lane < 64
        prod = do_ref[...].astype(F32) * o_ref[...]
        d0 = jnp.sum(jnp.where(first, prod, 0.0), axis=1, keepdims=True)
        d1 = jnp.sum(jnp.where(first, 0.0, prod), axis=1, keepdims=True)
        delta = jnp.where(first, jnp.broadcast_to(d0, (s, LANES)), jnp.broadcast_to(d1, (s, LANES)))
        rs_ref[...] = jnp.where((lane & 32) == 0, lse_ref[...], delta)
        for p, (_, d) in enumerate(DILATED):
            length = s // d
            if d == 1:
                qs, ks, vs, dos, rss = q_ref, k_ref, v_ref, do_ref, rs_ref
                dqs, dks, dvs = dq_ref, dk_ref, dv_ref
            else:
                for src, dst in ((do_ref, dop_ref), (rs_ref, rsp_ref)):
                    _to_pattern_order(src, dst, stage_ref, s, d)
                qs, ks, vs = ordered_refs[3 * (p - 1):3 * p]
                dos, rss = dop_ref, rsp_ref
                dqs, dks, dvs = dqp_ref, dkp_ref, dvp_ref
            dks[...] = jnp.zeros((s, LANES), F32)
            dvs[...] = jnp.zeros((s, LANES), F32)
            for i in range(nblk):
                u0 = i * BAND_Q
                st = _band_start(i, s)
                qi = qs[u0:u0 + BAND_Q, :]
                doi = dos[u0:u0 + BAND_Q, :]
                kw = ks[st:st + BAND_WIN, :]
                vw = vs[st:st + BAND_WIN, :]
                zero = jnp.zeros_like(qi)
                q2 = jnp.concatenate([jnp.where(first, qi, zero), jnp.where(first, zero, qi)], axis=0)
                do2 = jnp.concatenate([jnp.where(first, doi, zero), jnp.where(first, zero, doi)], axis=0)
                sc = _dot_nt(q2, kw)
                dp = _dot_nt(do2, vw)
                b = bias_ref[bias_index[p][i]]
                rs_i = rss[u0:u0 + BAND_Q, :]
                ps, dss = [], []
                for h in range(2):
                    rows = slice(h * BAND_Q, (h + 1) * BAND_Q)
                    pr = jnp.exp2(sc[rows] + b - rs_i[:, 64 * h:64 * h + 1])
                    ps.append(pr.astype(BF16))
                    dss.append((pr * (dp[rows] - rs_i[:, 64 * h + 32:64 * h + 33])).astype(BF16))
                p2 = jnp.concatenate(ps, axis=0)
                ds2 = jnp.concatenate(dss, axis=0)
                dq2 = _dot(ds2, kw)
                dqs[u0:u0 + BAND_Q, :] = jnp.where(first, dq2[:BAND_Q], dq2[BAND_Q:]) * scale
                dks[st:st + BAND_WIN, :] += _dot_tn(ds2, q2)
                dvs[st:st + BAND_WIN, :] += _dot_tn(p2, do2)
            if d > 1:
                for j, src in enumerate((dqp_ref, dkp_ref, dvp_ref)):
                    for r in range(d):
                        nat_ref.at[p - 1, j][pl.ds(r, length, stride=d), :] = src[r * length:(r + 1) * length, :]

        def total(j, first_ref):
            return functools.reduce(lambda a, c: a + c, [first_ref[...]] + [nat_ref[p, j] for p in range(npat - 1)])

        dq_out[...] = total(0, dq_ref).astype(BF16)
        dk_out[...] = (total(1, dk_ref) * LN2).astype(BF16)
        dv_out[...] = total(2, dv_ref).astype(BF16)

    blk = pl.BlockSpec((s, LANES), lambda b, g: (b, g))
    out = jax.ShapeDtypeStruct((nb * s, A_WIDTH), BF16)
    f32_buf = pltpu.VMEM((s, LANES), F32)
    bf_buf = pltpu.VMEM((s, LANES), BF16)
    return pl.pallas_call(
        body, name=name, grid=(nb, A_WIDTH // LANES),
        out_shape=(out, out, out),
        in_specs=[blk] * (6 + n_copies) + [pl.BlockSpec(bias.shape, lambda b, g: (0, 0, 0))],
        out_specs=(blk, blk, blk),
        scratch_shapes=[f32_buf, f32_buf, bf_buf] + [f32_buf] * 7 + [pltpu.VMEM((npat - 1, 3, s, LANES), F32)],
        compiler_params=_params(("parallel", "parallel")),
    )(q, k, v, *ordered, o, do, lse, bias)


def _post(h32, ya, ybp, ym, proj, target, w_out, g_a, g_b, g_m, g_post, b_post, tm=256):
    t = h32.shape[0]

    def body(h_ref, ya_ref, yb_ref, ym_ref, ga_ref, gb_ref, gm_ref, tg_ref, wo_ref,
             goa_ref, gob_ref, gom_ref, gp_ref, bp_ref,
             y_ref, dz_ref, doa_ref, dob_ref, dom_ref, dga_ref, dgb_ref, dgm_ref,
             loss_ref, dgp_ref, dbp_ref, dgoa_ref, dgob_ref, dgom_ref):
        i = pl.program_id(0)

        @pl.when(i == 0)
        def _():
            for r in (loss_ref, dgp_ref, dbp_ref, dgoa_ref, dgob_ref, dgom_ref):
                r[...] = jnp.zeros_like(r)

        lane = lax.broadcasted_iota(jnp.int32, (1, LANES), 1)
        low = lane < 64
        h = h_ref[...]

        ybp_v = yb_ref[...]
        yb = jnp.concatenate(
            [jnp.where(low, pltpu.roll(ybp_v[:, 2 * j * LANES:(2 * j + 1) * LANES], 64, 1),
                       ybp_v[:, (2 * j + 1) * LANES:(2 * j + 2) * LANES]) for j in range(4)], axis=1)

        def gated(raw, gate, gain, width):
            xh, r = _rms_hat(raw, width)
            n = xh * gain
            sg = 1.0 / (1.0 + jnp.exp(-gate))
            return xh, r, n, sg, n * (gate * sg)

        gate_a, gate_b, gate_m = ga_ref[...], gb_ref[...], gm_ref[...]
        xh_a, r_a, n_a, sg_a, y_a = gated(ya_ref[...], gate_a, goa_ref[...], A_WIDTH)
        xh_b, r_b, n_b, sg_b, y_b = gated(yb, gate_b, gob_ref[...], 512)
        xh_m, r_m, n_m, sg_m, y_m = gated(ym_ref[...], gate_m, gom_ref[...], 512)
        y = jnp.concatenate([y_a, y_b, y_m], axis=1).astype(BF16)
        y_ref[...] = y
        z = DEEPNORM_ALPHA * h + _dot(y, wo_ref[...])
        zh, rstd = _ln_hat(z)
        err = zh * gp_ref[...] + bp_ref[...] - tg_ref[...]
        rows = jnp.sum(err * err, axis=1, keepdims=True)
        loss_ref[...] += jnp.broadcast_to(jnp.sum(rows, axis=0, keepdims=True) * (0.5 / D_MODEL), (1, LANES))
        dout = err * (1.0 / D_MODEL)
        dgp_ref[...] += _colsum(dout * zh)
        dbp_ref[...] += _colsum(dout)
        dz = _ln_bwd_rows(dout * gp_ref[...], zh, rstd)
        dz_ref[...] = dz
        dy = _dot_nt(dz.astype(BF16), wo_ref[...])

        def gated_bwd(dyg, xh, r, n, sg, gate, gain, width, dgain_ref):
            dn = dyg * (gate * sg)
            dgate = dyg * n * (sg * (1.0 + gate * (1.0 - sg)))
            dgain_ref[...] += _colsum(dn * xh)
            return _rms_bwd(dn * gain, xh, r, width), dgate

        dya, dgate_a = gated_bwd(dy[:, 0:1024], xh_a, r_a, n_a, sg_a, gate_a, goa_ref[...], A_WIDTH, dgoa_ref)
        dyb, dgate_b = gated_bwd(dy[:, 1024:1536], xh_b, r_b, n_b, sg_b, gate_b, gob_ref[...], 512, dgob_ref)
        dym, dgate_m = gated_bwd(dy[:, 1536:2048], xh_m, r_m, n_m, sg_m, gate_m, gom_ref[...], 512, dgom_ref)
        doa_ref[...] = dya.astype(BF16)
        dom_ref[...] = dym.astype(BF16)
        dga_ref[...] = dgate_a.astype(BF16)
        dgb_ref[...] = dgate_b.astype(BF16)
        dgm_ref[...] = dgate_m.astype(BF16)
        for j in range(4):
            blk = dyb[:, j * LANES:(j + 1) * LANES]
            dob_ref[:, 2 * j * LANES:(2 * j + 1) * LANES] = jnp.where(low, 0.0, pltpu.roll(blk, 64, 1)).astype(BF16)
            dob_ref[:, (2 * j + 1) * LANES:(2 * j + 2) * LANES] = jnp.where(low, 0.0, blk).astype(BF16)

    def col(width, idx):
        return pl.BlockSpec((tm, width), lambda i: (i, idx))

    def full(shape):
        return pl.BlockSpec(shape, lambda i: (0, 0))

    def acc(width):
        return jax.ShapeDtypeStruct((1, width), F32)

    return pl.pallas_call(
        body, name="post", grid=(t // tm,),
        out_shape=(jax.ShapeDtypeStruct((t, 2048), BF16), jax.ShapeDtypeStruct((t, 1024), F32),
                   jax.ShapeDtypeStruct((t, 1024), BF16), jax.ShapeDtypeStruct((t, 1024), BF16),
                   jax.ShapeDtypeStruct((t, 512), BF16),
                   jax.ShapeDtypeStruct((t, 1024), BF16), jax.ShapeDtypeStruct((t, 512), BF16),
                   jax.ShapeDtypeStruct((t, 512), BF16),
                   acc(LANES), acc(1024), acc(1024), acc(1024), acc(512), acc(512)),
        in_specs=[col(1024, 0), col(1024, 0), col(1024, 0), col(512, 0),
                  col(1024, 3), col(512, COL_BG // 512), col(512, COL_MG // 512), col(1024, 0),
                  full((2048, 1024)),
                  full((1, 1024)), full((1, 512)), full((1, 512)), full((1, 1024)), full((1, 1024))],
        out_specs=(col(2048, 0), col(1024, 0), col(1024, 0), col(1024, 0), col(512, 0),
                   col(1024, 0), col(512, 0), col(512, 0),
                   full((1, LANES)), full((1, 1024)), full((1, 1024)), full((1, 1024)), full((1, 512)),
                   full((1, 512))),
        compiler_params=_params(("arbitrary",)),
    )(h32, ya, ybp, ym, proj, proj, proj, target, w_out, g_a, g_b, g_m, g_post, b_post)


def _prep_bwd(dqa, dka, dva, dqb, dkb, dvb, dqm, dga, dgb, dgm, proj, trig, w_uq, w_ukv, g_cq, g_ckv,
              rope_a, rope_b, tm=512):
    t = proj.shape[0]

    def body(dqa_ref, dka_ref, dva_ref, dqb_ref, dkb_ref, dvb_ref, dqm_ref, dga_ref, dgb_ref, dgm_ref,
             bs_ref, trig_ref, wuq_ref, wukv_ref, gcq_ref, gckv_ref, ra_ref, rb_ref,
             dproj_ref, dwuq_ref, dwukv_ref, dgcq_ref, dgckv_ref, dqf_ref, dkv_ref):
        i = pl.program_id(0)

        @pl.when(i == 0)
        def _():
            dwuq_ref[...] = jnp.zeros_like(dwuq_ref)
            dwukv_ref[...] = jnp.zeros_like(dwukv_ref)
            dgcq_ref[...] = jnp.zeros_like(dgcq_ref)
            dgckv_ref[...] = jnp.zeros_like(dgckv_ref)

        ta = _rope_tables(trig_ref[:, 0:LANES], trig_ref[:, LANES:2 * LANES], ra_ref[...])
        tb = _rope_tables(trig_ref[:, 2 * LANES:3 * LANES], trig_ref[:, 3 * LANES:4 * LANES], rb_ref[...])
        for j in range(A_WIDTH // LANES):
            sl = slice(j * LANES, (j + 1) * LANES)
            dproj_ref[:, j * LANES:(j + 1) * LANES] = (
                _rope(dqa_ref[:, sl].astype(F32), ta, 8, inverse=True).astype(BF16))
            dproj_ref[:, 1024 + j * LANES:1024 + (j + 1) * LANES] = (
                _rope(dka_ref[:, sl].astype(F32), ta, 8, inverse=True).astype(BF16))
        dproj_ref[:, 2048:3072] = dva_ref[...]
        dproj_ref[:, 3072:4096] = dga_ref[...]

        lane = lax.broadcasted_iota(jnp.int32, (1, LANES), 1)
        low = lane < 64
        rope_lanes = (lane >= 64) & (lane < 96)
        dkr = jnp.zeros((tm, LANES), F32)
        for h in range(MLA_HEADS):
            sl = slice(h * LANES, (h + 1) * LANES)
            dqf_ref[:, sl] = _rope(dqb_ref[:, sl].astype(F32), tb, 16, inverse=True).astype(BF16)
            dk_h = dkb_ref[:, sl]
            dkv_ref[:, sl] = jnp.where(low, dk_h, dvb_ref[:, sl])
            dkr = dkr + jnp.where(rope_lanes, dk_h.astype(F32), 0.0)
        dkr = _rope(dkr, tb, 16, inverse=True)

        cq_hat, r_q = _rms_hat(bs_ref[:, 0:MLA_Q_RANK], MLA_Q_RANK)
        dwuq_ref[...] += _dot_tn(dqf_ref[...], (cq_hat * gcq_ref[...]).astype(BF16))
        dcqn = _dot(dqf_ref[...], wuq_ref[...])
        dgcq_ref[...] += _colsum(dcqn * cq_hat)
        dproj_ref[:, COL_CQ:COL_CQ + 256] = _rms_bwd(dcqn * gcq_ref[...], cq_hat, r_q, MLA_Q_RANK).astype(BF16)
        ckv_hat, r_kv = _rms_hat(bs_ref[:, MLA_Q_RANK:MLA_Q_RANK + MLA_KV_RANK], MLA_KV_RANK)
        dwukv_ref[...] += _dot_tn((ckv_hat * gckv_ref[...]).astype(BF16), dkv_ref[...])
        dckvn = _dot_nt(dkv_ref[...], wukv_ref[...])
        dgckv_ref[...] += _colsum(dckvn * ckv_hat)
        dproj_ref[:, COL_CQ + 256:COL_CQ + 384] = (
            _rms_bwd(dckvn * gckv_ref[...], ckv_hat, r_kv, MLA_KV_RANK).astype(BF16))
        dproj_ref[:, COL_CQ + 384:COL_CQ + 512] = dkr.astype(BF16)
        dproj_ref[:, COL_BG:COL_BG + 512] = dgb_ref[...]
        dproj_ref[:, COL_MQ:COL_MQ + 512] = dqm_ref[...]
        dproj_ref[:, COL_MG:COL_MG + 512] = dgm_ref[...]

    def col(width, idx):
        return pl.BlockSpec((tm, width), lambda i: (i, idx))

    def full(shape):
        return pl.BlockSpec(shape, lambda i: (0, 0))

    return pl.pallas_call(
        body, name="prep_bwd", grid=(t // tm,),
        out_shape=(jax.ShapeDtypeStruct((t, PROJ_W), BF16), jax.ShapeDtypeStruct((1024, MLA_Q_RANK), F32),
                   jax.ShapeDtypeStruct((MLA_KV_RANK, 1024), F32),
                   jax.ShapeDtypeStruct((1, MLA_Q_RANK), F32), jax.ShapeDtypeStruct((1, MLA_KV_RANK), F32)),
        in_specs=[col(1024, 0)] * 6 + [col(512, 0), col(1024, 0), col(512, 0), col(512, 0),
                  col(512, COL_CQ // 512), pl.BlockSpec((tm, 4 * LANES), lambda i: (i, 0)),
                  full((1024, MLA_Q_RANK)), full((MLA_KV_RANK, 1024)),
                  full((1, MLA_Q_RANK)), full((1, MLA_KV_RANK)), full((8, LANES)), full((8, LANES))],
        out_specs=(col(PROJ_W, 0), full((1024, MLA_Q_RANK)), full((MLA_KV_RANK, 1024)),
                   full((1, MLA_Q_RANK)), full((1, MLA_KV_RANK))),
        scratch_shapes=[pltpu.VMEM((tm, 1024), BF16), pltpu.VMEM((tm, 1024), BF16)],
        compiler_params=_params(("arbitrary",)),
    )(dqa, dka, dva, dqb, dkb, dvb, dqm, dga, dgb, dgm, proj, trig, w_uq, w_ukv, g_cq, g_ckv, rope_a, rope_b)


def _adamw_math(gv, w, m, v):
    m_new = ADAM_B1 * m + (1.0 - ADAM_B1) * gv
    v_new = ADAM_B2 * v + (1.0 - ADAM_B2) * (gv * gv)
    m_hat = m_new / (1.0 - ADAM_B1 ** ADAM_STEP)
    v_hat = v_new / (1.0 - ADAM_B2 ** ADAM_STEP)
    return -ADAM_LR * (m_hat / (jnp.sqrt(v_hat) + ADAM_EPS) + ADAM_WD * w), m_new, v_new


def _adamw(g, w, m, v, tr, name):
    r, cols = w.shape

    def body(g_ref, w_ref, m_ref, v_ref, go_ref, d_ref, nm_ref, nv_ref):
        gv = g_ref[...]
        go_ref[...] = gv
        d_ref[...], nm_ref[...], nv_ref[...] = _adamw_math(gv, w_ref[...], m_ref[...], v_ref[...])

    tile = pl.BlockSpec((tr, cols), lambda i: (i, 0))
    shape = jax.ShapeDtypeStruct((r, cols), F32)
    return pl.pallas_call(
        body, name=name, grid=(r // tr,),
        out_shape=(shape,) * 4, in_specs=[tile] * 4, out_specs=(tile,) * 4,
        compiler_params=_params(("parallel",)),
    )(g, w, m, v)


def _adamw_pieces(g, w, m, v, pieces, name):
    n = len(pieces)
    per_piece = isinstance(w, (list, tuple))
    shapes = [jax.ShapeDtypeStruct((r1 - r0, c1 - c0), F32) for r0, r1, c0, c1 in pieces]
    args = (g, *w, *m, *v) if per_piece else (g, w, m, v)

    def body(g_ref, *refs):
        ins, outs = refs[:len(args) - 1], refs[len(args) - 1:]
        gv = g_ref[...]
        if not per_piece:
            results = (gv,) + _adamw_math(gv, ins[0][...], ins[1][...], ins[2][...])
        for p, (r0, r1, c0, c1) in enumerate(pieces):
            if per_piece:
                gp = gv[r0:r1, c0:c1]
                vals = (gp,) + _adamw_math(gp, ins[p][...], ins[n + p][...], ins[2 * n + p][...])
            else:
                vals = [full[r0:r1, c0:c1] for full in results]
            for kind, val in enumerate(vals):
                outs[kind * n + p][...] = val

    flat = pl.pallas_call(
        body, name=name, out_shape=tuple(shapes) * 4,
        in_specs=[IN_VMEM] * len(args), out_specs=tuple([IN_VMEM] * (4 * n)),
        compiler_params=_params(None),
    )(*args)
    return [[flat[kind * n + p] for kind in range(4)] for p in range(n)]


def _core_sum(g, recv, core, rows, tr, name, ride=None):
    cols = g.shape[2]
    nblk = rows // tr
    n_in = len(ride.args) if ride else 0
    n_out = len(ride.out_shapes) if ride else 0

    def body(c_ref, g_ref, r_ref, *rest):
        sf_ref, sb_ref = rest[n_in], rest[n_in + 1]
        if ride:
            j, i = pl.program_id(0), pl.program_id(1)
            ride.run(j * nblk + i, 4 * nblk, rest[:n_in], rest[n_in + 2:n_in + 2 + n_out],
                     rest[n_in + 2 + n_out:])
        tot = g_ref[...] + r_ref[...]
        sf_ref[...] = tot
        sb_ref[...] = tot.astype(BF16)

    half = pl.BlockSpec((None, tr, cols), lambda j, i, c_ref: (j, i, 0))
    shapes = (jax.ShapeDtypeStruct((4, rows, cols), F32), jax.ShapeDtypeStruct((4, rows, cols), BF16))
    return pl.pallas_call(
        body, name=name,
        grid_spec=pltpu.PrefetchScalarGridSpec(
            num_scalar_prefetch=1, grid=(4, nblk),
            in_specs=[pl.BlockSpec((None, tr, cols), lambda j, i, c_ref: (j, c_ref[0] * nblk + i, 0)), half]
            + (ride.in_specs if ride else []),
            out_specs=(half, half) + (ANY,) * n_out,
            scratch_shapes=ride.scratch() if ride else []),
        out_shape=shapes + tuple(ride.out_shapes if ride else ()),
        compiler_params=_params(("arbitrary", "arbitrary") if ride else ("parallel", "parallel")),
    )(core, g, recv, *(ride.args if ride else ()))


def _half_to_sibling(g4):
    def plan(in_refs, out_refs, send_sems, recv_sems):
        x, y, c = _position()
        cp = pltpu.make_async_remote_copy(
            src_ref=in_refs[0].at[:, 1 - c], dst_ref=out_refs[0], send_sem=send_sems.at[0],
            recv_sem=recv_sems.at[0], device_id=(x, y, 1 - c), device_id_type=MESH)

        def finish():
            cp.wait_recv()
            cp.wait_send()

        return cp.start, finish

    return _Ride([g4], [jax.ShapeDtypeStruct((4, g4.shape[2], 1024), F32)], (1, 1), plan)


def _gather_plan(src_ref, dst_ref, send_sems, recv_sems, local_sems):
    x, y, c = _position()
    me = 2 * x + y
    rows = src_ref.shape[1]
    cut = -(-rows // 32) * 16
    pieces = (pl.ds(0, cut), pl.ds(cut, rows - cut))
    local = pltpu.make_async_copy(src_ref, dst_ref.at[me], local_sems.at[0])

    def over_ici(sem, k, chip, t, src=None):
        where = dst_ref.at[chip, c, pieces[t]]
        return pltpu.make_async_remote_copy(
            src_ref=where if src is None else src, dst_ref=where, send_sem=send_sems.at[sem],
            recv_sem=recv_sems.at[sem], device_id=(x ^ (k >> 1), y ^ (k & 1), c), device_id_type=MESH)

    def mine_to(k, t):
        return over_ici(2 * (k - 1) + t, k, me, t, src=src_ref.at[c, pieces[t]])

    def from_neighbour(k, t):
        return over_ici(2 * (k - 1) + t, k, me ^ k, t)

    def to_sibling(k, half):
        piece = dst_ref.at[me ^ k, half]
        return pltpu.make_async_remote_copy(
            src_ref=piece, dst_ref=piece, send_sem=send_sems.at[5 + k], recv_sem=recv_sems.at[5 + k],
            device_id=(x, y, 1 - c), device_id_type=MESH)

    sends = [mine_to(2, 0), mine_to(1, 1), mine_to(2, 1), mine_to(1, 0)]
    onward = [over_ici(4, 1, me ^ 2, 0), over_ici(5, 2, me ^ 1, 1)]

    def start():
        local.start()
        for cp in sends:
            cp.start()

    def pass_on():
        from_neighbour(2, 0).wait_recv()
        onward[0].start()
        from_neighbour(1, 1).wait_recv()
        onward[1].start()

    def to_other_core():
        from_neighbour(2, 1).wait_recv()
        to_sibling(2, c).start()
        from_neighbour(1, 0).wait_recv()
        to_sibling(1, c).start()
        over_ici(4, 1, me ^ 3, 0).wait_recv()
        over_ici(5, 2, me ^ 3, 1).wait_recv()
        to_sibling(3, c).start()

    def finish():
        for k in (1, 2, 3):
            to_sibling(k, 1 - c).wait_recv()
        for cp in sends + onward + [to_sibling(k, c) for k in (1, 2, 3)]:
            cp.wait_send()
        local.wait()

    return start, pass_on, to_other_core, finish


def _gather_ride(shard, spread):
    def plan(in_refs, out_refs, send_sems, recv_sems, local_sems):
        return _gather_plan(in_refs[0], out_refs[0], send_sems, recv_sems, local_sems)

    return _Ride([shard], [jax.ShapeDtypeStruct((4,) + shard.shape, shard.dtype)], (9, 9, 1), plan,
                 in_specs=[IN_VMEM], spread=spread)


def _chip_sum(sf, recv, chip, rows, tr, name):
    cols = sf.shape[2]
    n_recv = recv.shape[0]

    def body(me_ref, sf_ref, r_ref, out_ref):
        acc = sf_ref[...]
        for k in range(n_recv):
            acc = acc + r_ref[k].astype(F32)
        out_ref[...] = acc

    return pl.pallas_call(
        body, name=name,
        grid_spec=pltpu.PrefetchScalarGridSpec(
            num_scalar_prefetch=1, grid=(rows // tr,),
            in_specs=[pl.BlockSpec((None, tr, cols), lambda i, me_ref: (me_ref[0], i, 0)),
                      pl.BlockSpec((n_recv, tr, cols), lambda i, me_ref: (0, i, 0))],
            out_specs=pl.BlockSpec((tr, cols), lambda i, me_ref: (i, 0))),
        out_shape=jax.ShapeDtypeStruct((rows, cols), F32),
        compiler_params=_params(("parallel",)),
    )(chip, sf, recv)


def _position():
    return lax.axis_index("x"), lax.axis_index("y"), lax.axis_index("c")


def _dh_scatter(dproj, w_in_arr_t, x, dz, g, sb_in, sb_rest, tm=512, tk=3072):
    t, d = x.shape
    nk = dproj.shape[1] // tk
    ni = t // tm
    total = ni * nk
    halves = (HALF_IN, HALF_REST)
    cuts = tuple(-(-rows // 32) * 16 for rows in halves)

    def rows_of(a, p):
        return cuts[a] if p == 0 else halves[a] - cuts[a]

    def piece(a, p):
        return pl.ds(0, cuts[a]) if p == 0 else pl.ds(cuts[a], halves[a] - cuts[a])

    def body(dp_ref, w_ref, x_ref, dz_ref, g_ref, sbin_ref, sbrest_ref, dx_ref, dg_ref, db_ref, rin_ref, rrest_ref,
             acc_ref, pay_in0, pay_in1, pay_rest0, pay_rest1, own_in0, own_in1, own_rest0, own_rest1,
             send_sems, recv_sems, local_sems):
        step = pl.program_id(0) * nk + pl.program_id(1)
        kk = pl.program_id(1)
        px, py, pc = _position()
        me = 2 * px + py
        srcs = (sbin_ref, sbrest_ref)
        dsts = (rin_ref, rrest_ref)
        pays = ((pay_in0, pay_in1), (pay_rest0, pay_rest1))
        owns = ((own_in0, own_in1), (own_rest0, own_rest1))
        via = (2, 1)
        onto = (1, 2)

        def peer(k):
            return (px ^ (k >> 1), py ^ (k & 1), pc)

        def payload(a, p):
            return pltpu.make_async_remote_copy(
                src_ref=srcs[a].at[me ^ 3, piece(a, p)], dst_ref=pays[a][p], send_sem=send_sems.at[2 * a + p],
                recv_sem=recv_sems.at[2 * a + p], device_id=peer(via[p]), device_id_type=MESH)

        def direct(a, k, p, src):
            sem = 4 + 4 * a + 2 * (k - 1) + p
            return pltpu.make_async_remote_copy(
                src_ref=src, dst_ref=dsts[a].at[k - 1, piece(a, p)], send_sem=send_sems.at[sem],
                recv_sem=recv_sems.at[sem], device_id=peer(k), device_id_type=MESH)

        def plain(a, k, p):
            return direct(a, k, p, srcs[a].at[me ^ k, piece(a, p)])

        def stage(a, p):
            return pltpu.make_async_copy(srcs[a].at[me ^ onto[p], piece(a, p)], owns[a][p], local_sems.at[2 * a + p])

        @pl.when(step == 0)
        def _():
            dg_ref[...] = jnp.zeros_like(dg_ref)
            db_ref[...] = jnp.zeros_like(db_ref)
            for a in range(2):
                for p in range(2):
                    payload(a, p).start()
                    stage(a, p).start()
                plain(a, 1, 1).start()
                plain(a, 2, 0).start()

        @pl.when(step == (5 * total) // 8)
        def _():
            for a in range(2):
                for p in range(2):
                    payload(a, p).wait_recv()
                    stage(a, p).wait()
                    owns[a][p][...] = (owns[a][p][...].astype(F32) + pays[a][p][...].astype(F32)).astype(BF16)
                    direct(a, onto[p], p, owns[a][p]).start()

        part = _dot(dp_ref[...], w_ref[...])

        @pl.when(kk == 0)
        def _():
            acc_ref[...] = part

        @pl.when(kk > 0)
        def _():
            acc_ref[...] += part

        @pl.when(kk == nk - 1)
        def _():
            xh, rstd = _ln_hat(x_ref[...])
            dht = acc_ref[...] + DEEPNORM_ALPHA * dz_ref[...]
            dg_ref[...] += _colsum(dht * xh)
            db_ref[...] += _colsum(dht)
            dx_ref[...] = _ln_bwd_rows(dht * g_ref[...], xh, rstd)

        @pl.when(step == total - 1)
        def _():
            for a in range(2):
                for k in (1, 2):
                    for p in range(2):
                        plain(a, k, p).wait_recv()
            for a in range(2):
                for p in range(2):
                    payload(a, p).wait_send()
                    direct(a, onto[p], p, owns[a][p]).wait_send()
                plain(a, 1, 1).wait_send()
                plain(a, 2, 0).wait_send()

    tile = pl.BlockSpec((tm, d), lambda i, kk: (i, 0))
    row = pl.BlockSpec((1, d), lambda i, kk: (0, 0))
    pieces = [pltpu.VMEM((rows_of(a, p), 1024), BF16) for a in range(2) for p in range(2)]
    return pl.pallas_call(
        body, name="dh_scatter", grid=(ni, nk),
        out_shape=(jax.ShapeDtypeStruct((t, d), F32), jax.ShapeDtypeStruct((1, d), F32),
                   jax.ShapeDtypeStruct((1, d), F32),
                   jax.ShapeDtypeStruct((2, HALF_IN, 1024), BF16),
                   jax.ShapeDtypeStruct((2, HALF_REST, 1024), BF16)),
        in_specs=[pl.BlockSpec((tm, tk), lambda i, kk: (i, kk)), pl.BlockSpec((tk, d), lambda i, kk: (kk, 0)),
                  tile, tile, row, ANY, ANY],
        out_specs=(tile, row, row, ANY, ANY),
        scratch_shapes=[pltpu.VMEM((tm, d), F32)] + pieces + pieces
        + [pltpu.SemaphoreType.DMA((12,)), pltpu.SemaphoreType.DMA((12,)), pltpu.SemaphoreType.DMA((4,))],
        compiler_params=_params(("arbitrary", "arbitrary")),
    )(dproj, w_in_arr_t, x, dz, g, sb_in, sb_rest)


def _join_and_allreduce(gh_in, gh_rest, vec):
    def body(hin_ref, hrest_ref, vec_ref, oin_ref, orest_ref, sum_ref, all_ref, send_sems, recv_sems, local_sems):
        x, y, c = _position()
        srcs = (hin_ref, hrest_ref)
        dsts = (oin_ref, orest_ref)
        me = 4 * x + 2 * y + c
        all_ref[me] = vec_ref[...]

        def small(k, slot):
            return pltpu.make_async_remote_copy(
                src_ref=vec_ref, dst_ref=all_ref.at[slot], send_sem=send_sems.at[k + 1], recv_sem=recv_sems.at[k + 1],
                device_id=(x ^ (k >> 2), y ^ ((k >> 1) & 1), c ^ (k & 1)), device_id_type=MESH)

        def half(a, slot):
            return pltpu.make_async_remote_copy(
                src_ref=srcs[a], dst_ref=dsts[a].at[slot], send_sem=send_sems.at[a], recv_sem=recv_sems.at[a],
                device_id=(x, y, 1 - c), device_id_type=MESH)

        local = [pltpu.make_async_copy(srcs[a], dsts[a].at[c], local_sems.at[a]) for a in range(2)]
        remote = [half(a, c) for a in range(2)] + [small(k, me) for k in range(1, 8)]
        for cp in local + remote:
            cp.start()
        for k in range(1, 8):
            small(k, me ^ k).wait_recv()
        for a in range(2):
            half(a, 1 - c).wait_recv()
        for cp in remote:
            cp.wait_send()
        for cp in local:
            cp.wait()
        total = all_ref[0]
        for d in range(1, 8):
            total = total + all_ref[d]
        sum_ref[...] = total

    return pl.pallas_call(
        body, name="join_halves",
        out_shape=(jax.ShapeDtypeStruct((2, HALF_IN, 1024), F32),
                   jax.ShapeDtypeStruct((2, HALF_REST, 1024), F32),
                   jax.ShapeDtypeStruct(vec.shape, vec.dtype)),
        in_specs=[IN_VMEM, IN_VMEM, IN_VMEM], out_specs=(ANY, ANY, IN_VMEM),
        scratch_shapes=[pltpu.VMEM((8,) + vec.shape, vec.dtype), pltpu.SemaphoreType.DMA((9,)),
                        pltpu.SemaphoreType.DMA((9,)), pltpu.SemaphoreType.DMA((2,))],
    )(gh_in, gh_rest, vec)


def _pack_rest(w_uq, w_ukv, w_mem, w_out):
    rows = jnp.concatenate([w_uq[0].T.reshape(-1, 1024), w_ukv.reshape(-1, 1024), w_mem.reshape(-1, 1024),
                            w_out.reshape(-1, 1024)], axis=0)
    return jnp.pad(rows, ((0, ROWS_REST - ROWS_USED), (0, 0)))


def _arranged_w_in(g_in):
    z = functools.partial(jnp.zeros, dtype=g_in.dtype)
    cut = 4480 - 2 * SHARD_ROWS
    return jnp.concatenate(
        [g_in[0, :SHARD_ROWS], g_in[1, :SHARD_ROWS], g_in[2, :cut], z((64, 1024)), g_in[2, cut:cut + 32],
         z((32, 1024)), g_in[2, cut + 32:SHARD_ROWS], g_in[3, :SHARD_ROWS]], axis=0)


def _rest_weights(g_rest):
    w_uq_t = g_rest[:, 0:ROWS_UQ].reshape(768, 256)
    w_uq_pad_t = jnp.pad(w_uq_t.reshape(MLA_HEADS, MLA_QK_DIM, 256), ((0, 0), (0, 32), (0, 0))).reshape(1024, 256)
    w_ukv = jnp.concatenate([g_rest[j, ROWS_UQ:ROWS_UQ + ROWS_UKV].reshape(128, 256) for j in range(4)], axis=1)
    lo = ROWS_UQ + ROWS_UKV
    w_mem = g_rest[:, lo:lo + ROWS_MEM].reshape(4 * ROWS_MEM, 1024)
    w_out = g_rest[:, lo + ROWS_MEM:lo + ROWS_MEM + ROWS_OUT].reshape(4 * ROWS_OUT, 1024)
    return w_uq_pad_t, w_ukv, w_mem, w_out


def _dw_in_split(dproj, h, tm=1024):
    t = dproj.shape[0]
    steps = PROJ_W // tm
    gap = ROWS_IN - SHARD_ROWS
    nat = 4608 - 96
    last = 4608 + 3 * SHARD_ROWS - nat
    segments = ((0, SHARD_ROWS, 0, 0), (SHARD_ROWS, 2 * SHARD_ROWS, 1, 0), (2 * SHARD_ROWS, 4480, 2, 0),
                (4544, 4576, 2, 4480 - 2 * SHARD_ROWS), (4608, last, 2, 4512 - 2 * SHARD_ROWS), (last, PROJ_W, 3, 0))

    def pieces(j):
        out = []
        for lo, hi, chip, dst in segments:
            a, b = max(lo, j * tm), min(hi, (j + 1) * tm)
            if a < b:
                out.append((a - j * tm, chip, dst + a - lo, b - a))
        return out

    n_sem = max(len(pieces(j)) for j in range(steps))

    def body(a_ref, b_ref, o_ref, tile_ref, zero_ref, sems, pad_sems):
        i = pl.program_id(0)

        def copies(j):
            return [pltpu.make_async_copy(tile_ref.at[j % 2, pl.ds(off, n)], o_ref.at[chip, pl.ds(dst, n)],
                                          sems.at[j % 2, q])
                    for q, (off, chip, dst, n) in enumerate(pieces(j))]

        def pad_copies():
            return [pltpu.make_async_copy(zero_ref, o_ref.at[chip, pl.ds(SHARD_ROWS, gap)], pad_sems.at[chip])
                    for chip in range(4)]

        @pl.when(i == 0)
        def _():
            zero_ref[...] = jnp.zeros_like(zero_ref)
            for c in pad_copies():
                c.start()

        for j in range(2, steps):
            @pl.when(i == j)
            def _(j=j):
                for c in copies(j - 2):
                    c.wait()

        tile_ref[i % 2] = _dot_tn(a_ref[...], b_ref[...])

        for j in range(steps):
            @pl.when(i == j)
            def _(j=j):
                for c in copies(j):
                    c.start()
                if j == steps - 1:
                    for c in copies(j - 1) + copies(j) + pad_copies():
                        c.wait()

    return pl.pallas_call(
        body, name="dw_in", grid=(steps,),
        out_shape=jax.ShapeDtypeStruct((4, ROWS_IN, 1024), F32),
        in_specs=[pl.BlockSpec((t, tm), lambda i: (0, i)), pl.BlockSpec((t, 1024), lambda i: (0, 0))],
        out_specs=ANY,
        scratch_shapes=[pltpu.VMEM((2, tm, 1024), F32), pltpu.VMEM((gap, 1024), F32),
                        pltpu.SemaphoreType.DMA((2, n_sem)), pltpu.SemaphoreType.DMA((4,))],
        compiler_params=_params(("arbitrary",)),
    )(dproj, h)


def _split_rest(dw_uq_pad_t, dw_ukv, dw_mem, dw_out):
    dw_uq_t = dw_uq_pad_t.reshape(MLA_HEADS, LANES, 256)[:, :MLA_QK_DIM].reshape(4, ROWS_UQ, 1024)
    parts = [dw_uq_t, dw_ukv.reshape(128, 4, 256).transpose(1, 0, 2).reshape(4, ROWS_UKV, 1024),
             dw_mem.reshape(4, ROWS_MEM, 1024), dw_out.reshape(4, ROWS_OUT, 1024)]
    return jnp.pad(jnp.concatenate(parts, axis=1), ((0, 0), (0, ROWS_REST - ROWS_USED), (0, 0)))


def _rope_consts(rot, first, period):
    half = rot // 2
    inv_freq = np.float32(ROPE_THETA) ** (-(np.arange(0, rot, 2, dtype=np.float32) / np.float32(rot)))
    lane = np.arange(LANES) % period - first
    in_rot = (lane >= 0) & (lane < rot)
    out = np.zeros((8, LANES), np.float32)
    out[0] = np.where(in_rot, inv_freq[np.clip(lane, 0, rot - 1) % half], 0.0)
    out[1] = in_rot & (lane < half)
    out[2] = in_rot & (lane >= half)
    return jnp.asarray(out)


def _band_bias(s):
    nblk = s // BAND_Q
    starts = np.array([_band_start(i, s) for i in range(nblk)])
    uq = (np.arange(nblk)[:, None] * BAND_Q + np.arange(BAND_Q)[None, :])[:, :, None]
    uk = (starts[:, None] + np.arange(BAND_WIN)[None, :])[:, None, :]
    tiles, index, seen = [], [], {}
    for _, d in DILATED:
        length = s // d
        ok = (uq // length == uk // length) & (np.abs(uq - uk) <= 64)
        row = []
        for i in range(nblk):
            key = ok[i].tobytes()
            if key not in seen:
                seen[key] = len(tiles)
                tiles.append(np.where(ok[i], 0.0, NEG_INF).astype(np.float32))
            row.append(seen[key])
        index.append(row)
    return jnp.asarray(np.stack(tiles, axis=0)), index


def _forward_backward(h, h32, proj, trig, rope_consts, x, mem, target, weights, gains):
    w_uq_pad_t, w_ukv, w_mem, w_out = weights
    g_emb, b_emb, g_cq, g_ckv, g_out_a, g_out_b, g_out_m, g_post, b_post = gains
    nb, s, d = x.shape
    t = nb * s
    x2 = x.reshape(t, d)
    mem2 = mem.reshape(nb * N_MEM, d)
    tgt2 = target.reshape(t, d)
    rope_a, rope_b = rope_consts
    bias, bias_index = _band_bias(s)
    scales = (0.125, MLA_QK_DIM ** -0.5, 128 ** -0.5)

    qa, ka, va, qb, kb, vb, qm = _prep(proj, trig, w_uq_pad_t, w_ukv, g_cq, g_ckv, rope_a, rope_b, scales)
    mkv = _mm(mem2, w_mem, BF16, nb * N_MEM, 1024, 1024, "mem_kv")

    cfg_b = dict(nb=nb, s=s, sk=s, heads=8, voff=0, bq=256)
    cfg_m = dict(nb=nb, s=s, sk=N_MEM, heads=4, hpb=2, voff=4, bq=1024)
    ya, lse_a, qkv_ordered = _dilated_fwd(qa, ka, va, bias, bias_index, nb=nb, s=s, name="attn_a_fwd")
    yb, lse_b = _attn_fwd(qb, kb, vb, name="attn_b_fwd", hpb=4, **cfg_b)
    ym, lse_m = _attn_fwd(qm, mkv, mkv, name="attn_m_fwd", **cfg_m)

    (y, dz, doa, dob, dom, dga, dgb, dgm, loss, dg_post, db_post, dg_a, dg_b, dg_m) = _post(
        h32, ya, yb, ym, proj, tgt2, w_out, g_out_a, g_out_b, g_out_m, g_post, b_post)

    dqa, dka, dva = _dilated_bwd(qa, ka, va, qkv_ordered, ya, doa, lse_a, bias, bias_index, nb=nb, s=s, scale=scales[0],
                                 name="attn_a_bwd")
    dqb, dkb, dvb = _attn_bwd(qb, kb, vb, yb, dob, lse_b, name="attn_b_bwd", scale=scales[1], hpb=4, **cfg_b)
    dqm, dmk, dmv = _attn_bwd(qm, mkv, mkv, ym, dom, lse_m, name="attn_m_bwd", scale=scales[2], **cfg_m)
    dmkv = jnp.concatenate([dmk, dmv], axis=1)

    dproj, dw_uq_pad_t, dw_ukv, dg_cq, dg_ckv = _prep_bwd(
        dqa, dka, dva, dqb, dkb, dvb, dqm, dga, dgb, dgm, proj, trig, w_uq_pad_t, w_ukv, g_cq, g_ckv, rope_a, rope_b)

    small_rows = (dg_cq, dg_ckv, loss, dg_a, dg_b, dg_m, dg_post, db_post)
    return (dproj, h, y, dz, dw_uq_pad_t, dw_ukv, mem2, dmkv), x2, small_rows


def _weight_grads(operands, core):
    dproj, h, y, dz, dw_uq_pad_t, dw_ukv, mem2, dmkv = operands
    g_in = _dw_in_split(dproj, h)
    dw_out, r_in = _mm(y, dz, F32, 1024, 1024, 2048, "dw_out", mode="tn",
                       ride=_half_to_sibling(g_in.reshape(4, 2, HALF_IN, 1024)))
    dw_mem = _mm(mem2, dmkv, F32, 1024, 1024, mem2.shape[0], "dw_mem", mode="tn")
    g_rest = _split_rest(dw_uq_pad_t, dw_ukv, dw_mem, dw_out)
    sf_in, sb_in, r_rest = _core_sum(g_in, r_in, core, HALF_IN, HALF_IN // 2, "core_sum_in",
                                     ride=_half_to_sibling(g_rest.reshape(4, 2, HALF_REST, 1024)))
    sf_rest, sb_rest = _core_sum(g_rest, r_rest, core, HALF_REST, HALF_REST, "core_sum_rest")
    return sf_in, sb_in, sf_rest, sb_rest


def _small_block(dg_emb, db_emb, small_rows):
    dg_cq, dg_ckv, loss, dg_a, dg_b, dg_m, dg_post, db_post = small_rows
    row2 = jnp.concatenate([dg_cq, dg_ckv, loss, jnp.zeros((1, 512), F32)], axis=1)
    return jnp.concatenate([dg_emb, db_emb, row2, dg_a, jnp.concatenate([dg_b, dg_m], axis=1), dg_post, db_post,
                            jnp.zeros((1, 1024), F32)], axis=0)


def _pack_small(g_emb, b_emb, g_cq, g_ckv, g_out_a, g_out_b, g_out_m, g_post, b_post):
    row2 = jnp.concatenate([g_cq.reshape(1, -1), g_ckv.reshape(1, -1), jnp.zeros((1, 640), F32)], axis=1)
    return jnp.concatenate([g_emb.reshape(1, -1), b_emb.reshape(1, -1), row2, g_out_a.reshape(1, -1),
                            jnp.concatenate([g_out_b.reshape(1, -1), g_out_m.reshape(1, -1)], axis=1),
                            g_post.reshape(1, -1), b_post.reshape(1, -1), jnp.zeros((1, 1024), F32)], axis=0)


def kernel(x, mem, positions, g_emb, b_emb, w_in, g_cq, g_ckv, w_uq, w_ukv, w_mem_kv, g_out_a, g_out_b, g_out_m, w_out, g_post, b_post, loss_target, m_g_emb, m_b_emb, m_w_in, m_g_cq, m_g_ckv, m_w_uq, m_w_ukv, m_w_mem_kv, m_g_out_a, m_g_out_b, m_g_out_m, m_w_out, m_g_post, m_b_post, v_g_emb, v_b_emb, v_w_in, v_g_cq, v_g_ckv, v_w_uq, v_w_ukv, v_w_mem_kv, v_g_out_a, v_g_out_b, v_g_out_m, v_w_out, v_g_post, v_b_post):
    w_rest = _pack_rest(w_uq, w_ukv, w_mem_kv, w_out)
    w_in_t = w_in[0].T
    w_in_b = jnp.pad(w_in_t.astype(BF16), ((0, ROWS_IN - SHARD_ROWS), (0, 0)))
    gains = (g_emb.reshape(1, -1), b_emb.reshape(1, -1), g_cq, g_ckv, g_out_a, g_out_b, g_out_m, g_post, b_post)
    rope_consts = (_rope_consts(16, 0, 64), _rope_consts(32, 64, 128))
    h, h32, trig, gathered_in = _ln_fwd(x.reshape(-1, D_MODEL), gains[0], gains[1],
                                        positions.reshape(-1, 1).astype(F32), *rope_consts,
                                        ride=_gather_ride(w_in_b.reshape(2, HALF_IN, 1024), spread=False))
    w_in_arr_t = _arranged_w_in(gathered_in.reshape(4, ROWS_IN, 1024))
    proj, gathered_rest = _mm(h, w_in_arr_t, F32, 1024, 2048, 1024, "in_proj", mode="nt",
                              ride=_gather_ride(w_rest.astype(BF16).reshape(2, HALF_REST, 1024), spread=True))
    weights = _rest_weights(gathered_rest.reshape(4, ROWS_REST, 1024))
    operands, x2, small_rows = _forward_backward(h, h32, proj, trig, rope_consts, x, mem, loss_target, weights,
                                                 gains)

    core = lax.axis_index("c").astype(jnp.int32).reshape(1)
    chip = (2 * lax.axis_index("x") + lax.axis_index("y")).astype(jnp.int32).reshape(1)
    sf_in, sb_in, sf_rest, sb_rest = _weight_grads(operands, core)
    grad_x, dg_emb, db_emb, rb_in, rb_rest = _dh_scatter(operands[0], w_in_arr_t, x2, operands[3], gains[0],
                                                         sb_in, sb_rest)
    gh_in = _chip_sum(sf_in, rb_in, chip, HALF_IN, HALF_IN // 2, "chip_sum_in")
    gh_rest = _chip_sum(sf_rest, rb_rest, chip, HALF_REST, HALF_REST, "chip_sum_rest")
    grad_in, grad_rest, small_sum = _join_and_allreduce(gh_in, gh_rest, _small_block(dg_emb, db_emb, small_rows))
    grad_in = grad_in.reshape(ROWS_IN, 1024)
    grad_rest = grad_rest.reshape(ROWS_REST, 1024)

    big_in = _adamw(grad_in, w_in_t, m_w_in[0].T, v_w_in[0].T, SHARD_ROWS // 3, "adamw_in")
    def rest_parts(a_uq, a_ukv, a_mem, a_out):
        return [a_uq[0].T.reshape(ROWS_UQ, 1024), a_ukv.reshape(ROWS_UKV, 1024), a_mem[0], a_out[0]]

    uq, ukv, wmem, wout = _adamw_pieces(
        grad_rest, rest_parts(w_uq, w_ukv, w_mem_kv, w_out), rest_parts(m_w_uq, m_w_ukv, m_w_mem_kv, m_w_out),
        rest_parts(v_w_uq, v_w_ukv, v_w_mem_kv, v_w_out), REST_PIECES, "adamw_rest")
    sm = _adamw_pieces(
        small_sum,
        _pack_small(g_emb, b_emb, g_cq, g_ckv, g_out_a, g_out_b, g_out_m, g_post, b_post),
        _pack_small(m_g_emb, m_b_emb, m_g_cq, m_g_ckv, m_g_out_a, m_g_out_b, m_g_out_m, m_g_post, m_b_post),
        _pack_small(v_g_emb, v_b_emb, v_g_cq, v_g_ckv, v_g_out_a, v_g_out_b, v_g_out_m, v_g_post, v_b_post),
        SMALL_PIECES, "adamw_small")
    loss = small_sum[2, 384]

    def ordered(kind):
        s_gemb, s_bemb, s_gcq, s_gckv, s_ga, s_gb, s_gm, s_gpost, s_bpost = [piece[kind] for piece in sm]
        return [s_gemb.reshape(-1), s_bemb.reshape(-1), big_in[kind].T[None], s_gcq, s_gckv,
                uq[kind].reshape(192, 256).T[None], ukv[kind].reshape(1, 128, 256), wmem[kind][None], s_ga, s_gb,
                s_gm, wout[kind][None], s_gpost, s_bpost]

    return (loss, grad_x.reshape(x.shape), *ordered(0), *ordered(1), *ordered(2), *ordered(3))
```

```python
import functools
import math

import jax
import jax.numpy as jnp
import numpy as np
from jax import lax
from jax.experimental import pallas as pl
from jax.experimental.pallas import tpu as pltpu

F32 = jnp.float32
BF16 = jnp.bfloat16
MESH = pl.DeviceIdType.MESH
ANY = pl.BlockSpec(memory_space=pl.ANY)
IN_VMEM = pl.BlockSpec(memory_space=pltpu.VMEM)

D_MODEL = 1024
A_WIDTH = 1024
MLA_HEADS = 8
MLA_Q_RANK = 256
MLA_KV_RANK = 128
MLA_QK_DIM = 96
MEM_WIDTH = 512
N_MEM = 256
ROPE_THETA = 500000.0
NORM_EPS = 1e-5
NEG_INF = -1e30
DEEPNORM_ALPHA = 2.0 ** 0.25
DILATED = ((64, 1), (256, 4), (1024, 16))

ADAM_LR = 0.001
ADAM_B1 = 0.9
ADAM_B2 = 0.999
ADAM_EPS = 1e-08
ADAM_WD = 0.01
ADAM_STEP = 10

LANES = 128
VMEM_LIMIT = 56 * 1024 * 1024
LOG2E = math.log2(math.e)
LN2 = math.log(2.0)

PROJ_W = 6144
COL_CQ = 4096
COL_BG = 4608
COL_MQ = 5120
COL_MG = 5632

SHARD_ROWS = 1512
ROWS_IN = 1536
ROWS_UQ, ROWS_UKV, ROWS_MEM, ROWS_OUT = 48, 32, 256, 512
ROWS_USED = ROWS_UQ + ROWS_UKV + ROWS_MEM + ROWS_OUT
ROWS_REST = 864
HALF_IN = ROWS_IN // 2
HALF_REST = ROWS_REST // 2
REST_PIECES = ((0, 48, 0, 1024), (48, 80, 0, 1024), (80, 336, 0, 1024), (336, 848, 0, 1024))
SMALL_PIECES = ((0, 1, 0, 1024), (1, 2, 0, 1024), (2, 3, 0, 256), (2, 3, 256, 384), (3, 4, 0, 1024), (4, 5, 0, 512),
                (4, 5, 512, 1024), (5, 6, 0, 1024), (6, 7, 0, 1024))


def _params(sem=None, vmem=VMEM_LIMIT):
    return pltpu.CompilerParams(dimension_semantics=sem, vmem_limit_bytes=vmem)


def _dot(a, b):
    return jnp.dot(a, b, preferred_element_type=F32)


def _dot_nt(a, b):
    return lax.dot_general(a, b, (((1,), (1,)), ((), ())), preferred_element_type=F32)


def _dot_tn(a, b):
    return lax.dot_general(a, b, (((0,), (0,)), ((), ())), preferred_element_type=F32)


def _ln_hat(x):
    mu = jnp.mean(x, axis=-1, keepdims=True)
    xc = x - mu
    var = jnp.mean(xc * xc, axis=-1, keepdims=True)
    rstd = lax.rsqrt(var + NORM_EPS)
    return xc * rstd, rstd


def _ln_bwd_rows(dxh, xh, rstd):
    return rstd * (dxh - jnp.mean(dxh, axis=-1, keepdims=True) - xh * jnp.mean(dxh * xh, axis=-1, keepdims=True))


def _rms_hat(x, width):
    ms = jnp.sum(x * x, axis=-1, keepdims=True) * (1.0 / width)
    r = lax.rsqrt(ms + NORM_EPS)
    return x * r, r


def _rms_bwd(u, xh, r, width):
    return r * (u - xh * (jnp.sum(u * xh, axis=-1, keepdims=True) * (1.0 / width)))


def _colsum(v):
    return jnp.sum(v, axis=0, keepdims=True)


def _rope_tables(cos, sin, consts):
    return cos, sin * consts[2:3, :], -sin * consts[1:2, :]


def _rope(x, tables, half, inverse=False):
    c, s_up, s_dn = tables
    if inverse:
        s_up, s_dn = -s_up, -s_dn
    return x * c + pltpu.roll(x, half, 1) * s_up + pltpu.roll(x, LANES - half, 1) * s_dn


def _ln_fwd(x, g, b, pos, rope_a, rope_b, tm=512, ride=None):
    t, d = x.shape
    n_in = len(ride.args) if ride else 0
    n_out = len(ride.out_shapes) if ride else 0
    steps = t // tm

    def body(x_ref, g_ref, b_ref, pos_ref, ra_ref, rb_ref, *rest):
        h_ref, h32_ref, trig_ref = rest[n_in:n_in + 3]
        if ride:
            i = pl.program_id(0)
            ride.run(i, steps, rest[:n_in], rest[n_in + 3:n_in + 3 + n_out], rest[n_in + 3 + n_out:])
        xh, _ = _ln_hat(x_ref[...])
        h = xh * g_ref[...] + b_ref[...]
        h32_ref[...] = h
        h_ref[...] = h.astype(BF16)
        for j, consts in enumerate((ra_ref, rb_ref)):
            ang = pos_ref[...] * consts[0:1, :]
            trig_ref[:, 2 * j * LANES:(2 * j + 1) * LANES] = jnp.cos(ang)
            trig_ref[:, (2 * j + 1) * LANES:(2 * j + 2) * LANES] = jnp.sin(ang)

    row = pl.BlockSpec((1, d), lambda i: (0, 0))
    tile = pl.BlockSpec((tm, d), lambda i: (i, 0))
    consts = pl.BlockSpec((8, LANES), lambda i: (0, 0))
    trig_tile = pl.BlockSpec((tm, 4 * LANES), lambda i: (i, 0))
    in_specs = [tile, row, row, pl.BlockSpec((tm, 1), lambda i: (i, 0)), consts, consts]
    shapes = (jax.ShapeDtypeStruct((t, d), BF16), jax.ShapeDtypeStruct((t, d), F32),
              jax.ShapeDtypeStruct((t, 4 * LANES), F32))
    if not ride:
        return pl.pallas_call(
            body, name="ln_fwd", grid=(steps,), out_shape=shapes, in_specs=in_specs,
            out_specs=(tile, tile, trig_tile), compiler_params=_params(("parallel",)),
        )(x, g, b, pos, rope_a, rope_b)
    return pl.pallas_call(
        body, name="ln_fwd", grid=(steps,),
        out_shape=(*shapes, *ride.out_shapes),
        in_specs=in_specs + ride.in_specs, out_specs=(tile, tile, trig_tile) + (ANY,) * n_out,
        scratch_shapes=ride.scratch(),
        compiler_params=_params(("arbitrary",)),
    )(x, g, b, pos, rope_a, rope_b, *ride.args)


class _Ride:
    def __init__(self, args, out_shapes, sem_counts, plan, in_specs=None, spread=True):
        self.args, self.out_shapes, self.plan = list(args), list(out_shapes), plan
        self.sem_counts = sem_counts
        self.in_specs = in_specs or [ANY] * len(self.args)
        self.spread = spread

    def scratch(self):
        return [pltpu.SemaphoreType.DMA((n,)) for n in self.sem_counts]

    def run(self, step, total, in_refs, out_refs, sems):
        count = len(self.plan(in_refs, out_refs, *sems))
        at = [(k * (total - 1)) // (count - 1) if self.spread or k == 0 else total - 1 for k in range(count)]
        for when in sorted(set(at)):
            @pl.when(step == when)
            def _(when=when):
                stages = self.plan(in_refs, out_refs, *sems)
                for k in range(count):
                    if at[k] == when:
                        stages[k]()


def _mm(a, b, out_dtype, tm, tn, tk, name, mode="nn", ride=None):
    if mode == "tn":
        k, m = a.shape
    else:
        m, k = a.shape
    n = b.shape[0] if mode == "nt" else b.shape[1]
    nk = k // tk
    nj, ni = n // tn, m // tm
    n_in = len(ride.args) if ride else 0
    n_out = len(ride.out_shapes) if ride else 0

    def body(a_ref, b_ref, *rest):
        o_ref = rest[n_in]
        acc_ref = rest[n_in + 1 + n_out]
        if ride:
            j, i, kk = pl.program_id(0), pl.program_id(1), pl.program_id(2)
            ride.run((j * ni + i) * nk + kk, nj * ni * nk, rest[:n_in], rest[n_in + 1:n_in + 1 + n_out],
                     rest[n_in + 2 + n_out:])
        av = a_ref[...].astype(BF16)
        bv = b_ref[...].astype(BF16)
        part = _dot_tn(av, bv) if mode == "tn" else _dot_nt(av, bv) if mode == "nt" else _dot(av, bv)
        if nk == 1:
            o_ref[...] = part.astype(out_dtype)
        else:
            kk = pl.program_id(2)

            @pl.when(kk == 0)
            def _():
                acc_ref[...] = part

            @pl.when(kk > 0)
            def _():
                acc_ref[...] += part

            @pl.when(kk == nk - 1)
            def _():
                o_ref[...] = acc_ref[...].astype(out_dtype)

    a_spec = (pl.BlockSpec((tk, tm), lambda j, i, kk: (kk, i)) if mode == "tn"
              else pl.BlockSpec((tm, tk), lambda j, i, kk: (i, kk)))
    b_spec = (pl.BlockSpec((tn, tk), lambda j, i, kk: (j, kk)) if mode == "nt"
              else pl.BlockSpec((tk, tn), lambda j, i, kk: (kk, j)))
    o_spec = pl.BlockSpec((tm, tn), lambda j, i, kk: (i, j))
    o_shape = jax.ShapeDtypeStruct((m, n), out_dtype)
    if not ride:
        return pl.pallas_call(
            body, name=name, grid=(nj, ni, nk), out_shape=o_shape, in_specs=[a_spec, b_spec], out_specs=o_spec,
            scratch_shapes=[pltpu.VMEM((tm, tn), F32)],
            compiler_params=_params(("parallel", "parallel", "arbitrary")),
        )(a, b)
    return pl.pallas_call(
        body, name=name, grid=(nj, ni, nk),
        out_shape=(o_shape, *ride.out_shapes),
        in_specs=[a_spec, b_spec] + ride.in_specs,
        out_specs=(o_spec,) + (ANY,) * n_out,
        scratch_shapes=[pltpu.VMEM((tm, tn), F32)] + ride.scratch(),
        compiler_params=_params(("arbitrary", "arbitrary", "arbitrary")),
    )(a, b, *ride.args)


def _prep(proj, trig, w_uq, w_ukv, g_cq, g_ckv, rope_a, rope_b, scales, tm=512):
    t = proj.shape[0]
    sc_a, sc_b, sc_m = (s * LOG2E for s in scales)

    steps = t // tm
    streams = ((0, 0, 1024), (0, 1024, 1024), (0, 2048, 1024), (0, COL_CQ, 512), (0, COL_MQ, 512), (1, 0, 4 * LANES))

    def body(proj_ref, trig_hbm, wuq_ref, wukv_ref, gcq_ref, gckv_ref, ra_ref, rb_ref,
             qa_ref, ka_ref, va_ref, qb_ref, kb_ref, vb_ref, qm_ref, *scratch):
        rings, sems = scratch[:len(streams)], scratch[len(streams)]
        s = pl.program_id(0)

        def fetch(step):
            return [pltpu.make_async_copy(
                (proj_ref, trig_hbm)[src].at[pl.ds(step * tm, tm), pl.ds(c0, width)], ring.at[step % 3],
                sems.at[step % 3, q]) for q, ((src, c0, width), ring) in enumerate(zip(streams, rings))]

        @pl.when(s == 0)
        def _():
            for cp in fetch(0) + fetch(1):
                cp.start()

        @pl.when(s + 2 < steps)
        def _():
            for cp in fetch(s + 2):
                cp.start()

        for cp in fetch(s):
            cp.wait()
        aq_ref, ak_ref, av_ref, bs_ref, mq_ref, trig_ref = [ring.at[s % 3] for ring in rings]
        ta = _rope_tables(trig_ref[:, 0:LANES], trig_ref[:, LANES:2 * LANES], ra_ref[...])
        tb = _rope_tables(trig_ref[:, 2 * LANES:3 * LANES], trig_ref[:, 3 * LANES:4 * LANES], rb_ref[...])
        for j in range(A_WIDTH // LANES):
            sl = slice(j * LANES, (j + 1) * LANES)
            qa_ref[:, sl] = (_rope(aq_ref[:, sl], ta, 8) * sc_a).astype(BF16)
            ka_ref[:, sl] = _rope(ak_ref[:, sl], ta, 8).astype(BF16)
        va_ref[...] = av_ref[...].astype(BF16)
        qm_ref[...] = (mq_ref[...] * sc_m).astype(BF16)

        cq_hat, _ = _rms_hat(bs_ref[:, 0:MLA_Q_RANK], MLA_Q_RANK)
        cqn = (cq_hat * gcq_ref[...]).astype(BF16)
        ckv_hat, _ = _rms_hat(bs_ref[:, MLA_Q_RANK:MLA_Q_RANK + MLA_KV_RANK], MLA_KV_RANK)
        ckvn = (ckv_hat * gckv_ref[...]).astype(BF16)
        qfull = _dot_nt(cqn, wuq_ref[...])
        kv = _dot(ckvn, wukv_ref[...])
        kr = _rope(bs_ref[:, 384:512], tb, 16)
        lane = lax.broadcasted_iota(jnp.int32, (1, LANES), 1)
        low = lane < 64
        for h in range(MLA_HEADS):
            sl = slice(h * LANES, (h + 1) * LANES)
            qb_ref[:, sl] = (_rope(qfull[:, sl], tb, 16) * sc_b).astype(BF16)
            kb_ref[:, sl] = jnp.where(low, kv[:, sl], kr).astype(BF16)
            vb_ref[:, sl] = jnp.where(low, 0.0, kv[:, sl]).astype(BF16)

    def col(width, idx):
        return pl.BlockSpec((tm, width), lambda i: (i, idx))

    def full(shape):
        return pl.BlockSpec(shape, lambda i: (0, 0))

    wide = jax.ShapeDtypeStruct((t, 1024), BF16)
    return pl.pallas_call(
        body, name="prep", grid=(steps,),
        out_shape=(wide, wide, wide, wide, wide, wide,
                   jax.ShapeDtypeStruct((t, MEM_WIDTH), BF16)),
        in_specs=[ANY, ANY, full((1024, MLA_Q_RANK)), full((MLA_KV_RANK, 1024)),
                  full((1, MLA_Q_RANK)), full((1, MLA_KV_RANK)), full((8, LANES)), full((8, LANES))],
        out_specs=(col(1024, 0),) * 6 + (col(MEM_WIDTH, 0),),
        scratch_shapes=[pltpu.VMEM((3, tm, width), F32) for _, _, width in streams]
        + [pltpu.SemaphoreType.DMA((3, len(streams)))],
        compiler_params=_params(("arbitrary",)),
    )(proj, trig, w_uq, w_ukv, g_cq, g_ckv, rope_a, rope_b)


def _attn_fwd(q, k, v, *, nb, s, sk, heads, hpb, voff, bq, name):
    nq = s // bq
    width = hpb * LANES
    vblk = voff // hpb

    def body(q_ref, k_ref, v_ref, o_ref, lse_ref):
        for h in range(hpb):
            sl = slice(h * LANES, (h + 1) * LANES)
            sc = _dot_nt(q_ref[:, sl], k_ref[:, sl])
            m = jnp.max(sc, axis=1, keepdims=True)
            p = jnp.exp2(sc - m)
            l = jnp.sum(p, axis=1, keepdims=True)
            o_ref[:, sl] = _dot(p.astype(BF16), v_ref[:, sl]) / l
            lse_ref[:, sl] = jnp.broadcast_to(m + jnp.log(l) * LOG2E, (bq, LANES))

    out = jax.ShapeDtypeStruct((nb * s, heads * LANES), F32)
    ospec = pl.BlockSpec((bq, width), lambda b, i, g: (b * nq + i, g))
    return pl.pallas_call(
        body, name=name, grid=(nb, nq, heads // hpb),
        out_shape=(out, out),
        in_specs=[ospec, pl.BlockSpec((sk, width), lambda b, i, g: (b, g)),
                  pl.BlockSpec((sk, width), lambda b, i, g: (b, vblk + g))],
        out_specs=(ospec, ospec),
        compiler_params=_params(("parallel", "parallel", "parallel")),
    )(q, k, v)


def _attn_bwd(q, k, v, o, do, lse, *, nb, s, sk, heads, hpb, voff, scale, bq, name):
    nq = s // bq
    width = hpb * LANES
    vblk = voff // hpb

    def body(q_ref, k_ref, v_ref, o_ref, do_ref, lse_ref, dq_ref, dk_ref, dv_ref, dk_acc, dv_acc):
        i = pl.program_id(2)

        @pl.when(i == 0)
        def _():
            dk_acc[...] = jnp.zeros_like(dk_acc)
            dv_acc[...] = jnp.zeros_like(dv_acc)

        for h in range(hpb):
            sl = slice(h * LANES, (h + 1) * LANES)
            qh = q_ref[:, sl]
            kk = k_ref[:, sl]
            doh = do_ref[:, sl]
            delta = jnp.sum(doh.astype(F32) * o_ref[:, sl], axis=1, keepdims=True)
            p = jnp.exp2(_dot_nt(qh, kk) - lse_ref[:, h * LANES:h * LANES + 1])
            ds = (p * (_dot_nt(doh, v_ref[:, sl]) - delta)).astype(BF16)
            dq_ref[:, sl] = (_dot(ds, kk) * scale).astype(BF16)
            dk_acc[:, sl] += _dot_tn(ds, qh)
            dv_acc[:, sl] += _dot_tn(p.astype(BF16), doh)

        @pl.when(i == nq - 1)
        def _():
            dk_ref[...] = (dk_acc[...] * LN2).astype(BF16)
            dv_ref[...] = dv_acc[...].astype(BF16)

    qspec = pl.BlockSpec((bq, width), lambda b, g, i: (b * nq + i, g))
    kv_spec = pl.BlockSpec((sk, width), lambda b, g, i: (b, g))
    dq_shape = jax.ShapeDtypeStruct((nb * s, heads * LANES), BF16)
    dkv_shape = jax.ShapeDtypeStruct((nb * sk, heads * LANES), BF16)
    return pl.pallas_call(
        body, name=name, grid=(nb, heads // hpb, nq),
        out_shape=(dq_shape, dkv_shape, dkv_shape),
        in_specs=[qspec, kv_spec, pl.BlockSpec((sk, width), lambda b, g, i: (b, vblk + g)), qspec, qspec, qspec],
        out_specs=(qspec, kv_spec, kv_spec),
        scratch_shapes=[pltpu.VMEM((sk, width), F32), pltpu.VMEM((sk, width), F32)],
        compiler_params=_params(("parallel", "parallel", "arbitrary")),
    )(q, k, v, o, do, lse)


BAND_Q = 128
BAND_WIN = 256


def _band_start(i, s):
    return min(max(i * BAND_Q - 64, 0), s - BAND_WIN)


def _to_pattern_order(src_ref, dst_ref, stage_ref, s, d):
    length = s // d
    stage_ref[...] = src_ref[...].astype(F32)
    for r in range(d):
        dst_ref[r * length:(r + 1) * length, :] = stage_ref[pl.ds(r, length, stride=d), :].astype(dst_ref.dtype)


def _dilated_fwd(q, k, v, bias, bias_index, *, nb, s, name):
    nblk = s // BAND_Q
    npat = len(DILATED)

    def body(q_ref, k_ref, v_ref, bias_ref, o_ref, lse_ref, *rest):
        ordered = rest[:3 * (npat - 1)]
        stage_ref, op_ref, lp_ref, on_ref, ln_ref = rest[3 * (npat - 1):]
        lane = lax.broadcasted_iota(jnp.int32, (1, LANES), 1)
        first = lane < 64
        for p, (_, d) in enumerate(DILATED):
            if d == 1:
                qs, ks, vs = q_ref, k_ref, v_ref
            else:
                qs, ks, vs = ordered[3 * (p - 1):3 * p]
                for src, dst in ((q_ref, qs), (k_ref, ks), (v_ref, vs)):
                    _to_pattern_order(src, dst, stage_ref, s, d)
            for i in range(nblk):
                u0 = i * BAND_Q
                st = _band_start(i, s)
                qi = qs[u0:u0 + BAND_Q, :]
                kw = ks[st:st + BAND_WIN, :]
                vw = vs[st:st + BAND_WIN, :]
                zero = jnp.zeros_like(qi)
                q2 = jnp.concatenate([jnp.where(first, qi, zero), jnp.where(first, zero, qi)], axis=0)
                sc = _dot_nt(q2, kw)
                b = bias_ref[bias_index[p][i]]
                halves = []
                for h in range(2):
                    sh = sc[h * BAND_Q:(h + 1) * BAND_Q] + b
                    m = jnp.max(sh, axis=1, keepdims=True)
                    pr = jnp.exp2(sh - m)
                    l = jnp.sum(pr, axis=1, keepdims=True)
                    halves.append((pr.astype(BF16), l, m + jnp.log(l) * LOG2E))
                o2 = _dot(jnp.concatenate([halves[0][0], halves[1][0]], axis=0), vw)
                o_blk = jnp.where(first, o2[:BAND_Q] / halves[0][1], o2[BAND_Q:] / halves[1][1])
                lse_blk = jnp.where(first, jnp.broadcast_to(halves[0][2], (BAND_Q, LANES)),
                                    jnp.broadcast_to(halves[1][2], (BAND_Q, LANES)))
                op_ref[p, u0:u0 + BAND_Q, :] = o_blk
                lp_ref[p, u0:u0 + BAND_Q, :] = lse_blk
            if d > 1:
                length = s // d
                for r in range(d):
                    on_ref.at[p - 1][pl.ds(r, length, stride=d), :] = op_ref[p, r * length:(r + 1) * length, :]
                    ln_ref.at[p - 1][pl.ds(r, length, stride=d), :] = lp_ref[p, r * length:(r + 1) * length, :]
        lses = [lp_ref[0]] + [ln_ref[p] for p in range(npat - 1)]
        outs = [op_ref[0]] + [on_ref[p] for p in range(npat - 1)]
        m = functools.reduce(jnp.maximum, lses)
        ws = [jnp.exp2(l - m) for l in lses]
        den = functools.reduce(lambda a, c: a + c, ws)
        o_ref[...] = functools.reduce(lambda a, c: a + c, [w * o for w, o in zip(ws, outs)]) / den
        lse_ref[...] = m + jnp.log(den) * LOG2E

    blk = pl.BlockSpec((s, LANES), lambda b, g: (b, g))
    out = jax.ShapeDtypeStruct((nb * s, A_WIDTH), F32)
    copy = jax.ShapeDtypeStruct((nb * s, A_WIDTH), BF16)
    n_copies = 3 * (npat - 1)
    res = pl.pallas_call(
        body, name=name, grid=(nb, A_WIDTH // LANES),
        out_shape=(out, out) + (copy,) * n_copies,
        in_specs=[blk, blk, blk, pl.BlockSpec(bias.shape, lambda b, g: (0, 0, 0))],
        out_specs=(blk, blk) + (blk,) * n_copies,
        scratch_shapes=[pltpu.VMEM((s, LANES), F32), pltpu.VMEM((npat, s, LANES), F32),
                        pltpu.VMEM((npat, s, LANES), F32), pltpu.VMEM((npat - 1, s, LANES), F32),
                        pltpu.VMEM((npat - 1, s, LANES), F32)],
        compiler_params=_params(("parallel", "parallel")),
    )(q, k, v, bias)
    return res[0], res[1], res[2:]


def _dilated_bwd(q, k, v, ordered, o, do, lse, bias, bias_index, *, nb, s, scale, name):
    nblk = s // BAND_Q
    npat = len(DILATED)
    n_copies = 3 * (npat - 1)

    def body(q_ref, k_ref, v_ref, *rest):
        ordered_refs = rest[:n_copies]
        (o_ref, do_ref, lse_ref, bias_ref, dq_out, dk_out, dv_out, stage_ref, rs_ref, dop_ref, rsp_ref,
         dqp_ref, dkp_ref, dvp_ref, dq_ref, dk_ref, dv_ref, nat_ref) = rest[n_copies:]
        lane = lax.broadcasted_iota(jnp.int32, (1, LANES), 1)
        first = lane < 64
        prod = do_ref[...].astype(F32) * o_ref[...]
        d0 = jnp.sum(jnp.where(first, prod, 0.0), axis=1, keepdims=True)
        d1 = jnp.sum(jnp.where(first, 0.0, prod), axis=1, keepdims=True)
        delta = jnp.where(first, jnp.broadcast_to(d0, (s, LANES)), jnp.broadcast_to(d1, (s, LANES)))
        rs_ref[...] = jnp.where((lane & 32) == 0, lse_ref[...], delta)
        for p, (_, d) in enumerate(DILATED):
            length = s // d
            if d == 1:
                qs, ks, vs, dos, rss = q_ref, k_ref, v_ref, do_ref, rs_ref
                dqs, dks, dvs = dq_ref, dk_ref, dv_ref
            else:
                for src, dst in ((do_ref, dop_ref), (rs_ref, rsp_ref)):
                    _to_pattern_order(src, dst, stage_ref, s, d)
                qs, ks, vs = ordered_refs[3 * (p - 1):3 * p]
                dos, rss = dop_ref, rsp_ref
                dqs, dks, dvs = dqp_ref, dkp_ref, dvp_ref
            dks[...] = jnp.zeros((s, LANES), F32)
            dvs[...] = jnp.zeros((s, LANES), F32)
            for i in range(nblk):
                u0 = i * BAND_Q
                st = _band_start(i, s)
                qi = qs[u0:u0 + BAND_Q, :]
                doi = dos[u0:u0 + BAND_Q, :]
                kw = ks[st:st + BAND_WIN, :]
                vw = vs[st:st + BAND_WIN, :]
                zero = jnp.zeros_like(qi)
                q2 = jnp.concatenate([jnp.where(first, qi, zero), jnp.where(first, zero, qi)], axis=0)
                do2 = jnp.concatenate([jnp.where(first, doi, zero), jnp.where(first, zero, doi)], axis=0)
                sc = _dot_nt(q2, kw)
                dp = _dot_nt(do2, vw)
                b = bias_ref[bias_index[p][i]]
                rs_i = rss[u0:u0 + BAND_Q, :]
                ps, dss = [], []
                for h in range(2):
                    rows = slice(h * BAND_Q, (h + 1) * BAND_Q)
                    pr = jnp.exp2(sc[rows] + b - rs_i[:, 64 * h:64 * h + 1])
                    ps.append(pr.astype(BF16))
                    dss.append((pr * (dp[rows] - rs_i[:, 64 * h + 32:64 * h + 33])).astype(BF16))
                p2 = jnp.concatenate(ps, axis=0)
                ds2 = jnp.concatenate(dss, axis=0)
                dq2 = _dot(ds2, kw)
                dqs[u0:u0 + BAND_Q, :] = jnp.where(first, dq2[:BAND_Q], dq2[BAND_Q:]) * scale
                dks[st:st + BAND_WIN, :] += _dot_tn(ds2, q2)
                dvs[st:st + BAND_WIN, :] += _dot_tn(p2, do2)
            if d > 1:
                for j, src in enumerate((dqp_ref, dkp_ref, dvp_ref)):
                    for r in range(d):
                        nat_ref.at[p - 1, j][pl.ds(r, length, stride=d), :] = src[r * length:(r + 1) * length, :]

        def total(j, first_ref):
            return functools.reduce(lambda a, c: a + c, [first_ref[...]] + [nat_ref[p, j] for p in range(npat - 1)])

        dq_out[...] = total(0, dq_ref).astype(BF16)
        dk_out[...] = (total(1, dk_ref) * LN2).astype(BF16)
        dv_out[...] = total(2, dv_ref).astype(BF16)

    blk = pl.BlockSpec((s, LANES), lambda b, g: (b, g))
    out = jax.ShapeDtypeStruct((nb * s, A_WIDTH), BF16)
    f32_buf = pltpu.VMEM((s, LANES), F32)
    bf_buf = pltpu.VMEM((s, LANES), BF16)
    return pl.pallas_call(
        body, name=name, grid=(nb, A_WIDTH // LANES),
        out_shape=(out, out, out),
        in_specs=[blk] * (6 + n_copies) + [pl.BlockSpec(bias.shape, lambda b, g: (0, 0, 0))],
        out_specs=(blk, blk, blk),
        scratch_shapes=[f32_buf, f32_buf, bf_buf] + [f32_buf] * 7 + [pltpu.VMEM((npat - 1, 3, s, LANES), F32)],
        compiler_params=_params(("parallel", "parallel")),
    )(q, k, v, *ordered, o, do, lse, bias)


def _post(h32, ya, ybp, ym, proj, target, w_out, g_a, g_b, g_m, g_post, b_post, tm=256):
    t = h32.shape[0]

    def body(h_ref, ya_ref, yb_ref, ym_ref, ga_ref, gb_ref, gm_ref, tg_ref, wo_ref,
             goa_ref, gob_ref, gom_ref, gp_ref, bp_ref,
             y_ref, dz_ref, doa_ref, dob_ref, dom_ref, dga_ref, dgb_ref, dgm_ref,
             loss_ref, dgp_ref, dbp_ref, dgoa_ref, dgob_ref, dgom_ref):
        i = pl.program_id(0)

        @pl.when(i == 0)
        def _():
            for r in (loss_ref, dgp_ref, dbp_ref, dgoa_ref, dgob_ref, dgom_ref):
                r[...] = jnp.zeros_like(r)

        lane = lax.broadcasted_iota(jnp.int32, (1, LANES), 1)
        low = lane < 64
        h = h_ref[...]

        ybp_v = yb_ref[...]
        yb = jnp.concatenate(
            [jnp.where(low, pltpu.roll(ybp_v[:, 2 * j * LANES:(2 * j + 1) * LANES], 64, 1),
                       ybp_v[:, (2 * j + 1) * LANES:(2 * j + 2) * LANES]) for j in range(4)], axis=1)

        def gated(raw, gate, gain, width):
            xh, r = _rms_hat(raw, width)
            n = xh * gain
            sg = 1.0 / (1.0 + jnp.exp(-gate))
            return xh, r, n, sg, n * (gate * sg)

        gate_a, gate_b, gate_m = ga_ref[...], gb_ref[...], gm_ref[...]
        xh_a, r_a, n_a, sg_a, y_a = gated(ya_ref[...], gate_a, goa_ref[...], A_WIDTH)
        xh_b, r_b, n_b, sg_b, y_b = gated(yb, gate_b, gob_ref[...], 512)
        xh_m, r_m, n_m, sg_m, y_m = gated(ym_ref[...], gate_m, gom_ref[...], 512)
        y = jnp.concatenate([y_a, y_b, y_m], axis=1).astype(BF16)
        y_ref[...] = y
        z = DEEPNORM_ALPHA * h + _dot(y, wo_ref[...])
        zh, rstd = _ln_hat(z)
        err = zh * gp_ref[...] + bp_ref[...] - tg_ref[...]
        rows = jnp.sum(err * err, axis=1, keepdims=True)
        loss_ref[...] += jnp.broadcast_to(jnp.sum(rows, axis=0, keepdims=True) * (0.5 / D_MODEL), (1, LANES))
        dout = err * (1.0 / D_MODEL)
        dgp_ref[...] += _colsum(dout * zh)
        dbp_ref[...] += _colsum(dout)
        dz = _ln_bwd_rows(dout * gp_ref[...], zh, rstd)
        dz_ref[...] = dz
        dy = _dot_nt(dz.astype(BF16), wo_ref[...])

        def gated_bwd(dyg, xh, r, n, sg, gate, gain, width, dgain_ref):
            dn = dyg * (gate * sg)
            dgate = dyg * n * (sg * (1.0 + gate * (1.0 - sg)))
            dgain_ref[...] += _colsum(dn * xh)
            return _rms_bwd(dn * gain, xh, r, width), dgate

        dya, dgate_a = gated_bwd(dy[:, 0:1024], xh_a, r_a, n_a, sg_a, gate_a, goa_ref[...], A_WIDTH, dgoa_ref)
        dyb, dgate_b = gated_bwd(dy[:, 1024:1536], xh_b, r_b, n_b, sg_b, gate_b, gob_ref[...], 512, dgob_ref)
        dym, dgate_m = gated_bwd(dy[:, 1536:2048], xh_m, r_m, n_m, sg_m, gate_m, gom_ref[...], 512, dgom_ref)
        doa_ref[...] = dya.astype(BF16)
        dom_ref[...] = dym.astype(BF16)
        dga_ref[...] = dgate_a.astype(BF16)
        dgb_ref[...] = dgate_b.astype(BF16)
        dgm_ref[...] = dgate_m.astype(BF16)
        for j in range(4):
            blk = dyb[:, j * LANES:(j + 1) * LANES]
            dob_ref[:, 2 * j * LANES:(2 * j + 1) * LANES] = jnp.where(low, 0.0, pltpu.roll(blk, 64, 1)).astype(BF16)
            dob_ref[:, (2 * j + 1) * LANES:(2 * j + 2) * LANES] = jnp.where(low, 0.0, blk).astype(BF16)

    def col(width, idx):
        return pl.BlockSpec((tm, width), lambda i: (i, idx))

    def full(shape):
        return pl.BlockSpec(shape, lambda i: (0, 0))

    def acc(width):
        return jax.ShapeDtypeStruct((1, width), F32)

    return pl.pallas_call(
        body, name="post", grid=(t // tm,),
        out_shape=(jax.ShapeDtypeStruct((t, 2048), BF16), jax.ShapeDtypeStruct((t, 1024), F32),
                   jax.ShapeDtypeStruct((t, 1024), BF16), jax.ShapeDtypeStruct((t, 1024), BF16),
                   jax.ShapeDtypeStruct((t, 512), BF16),
                   jax.ShapeDtypeStruct((t, 1024), BF16), jax.ShapeDtypeStruct((t, 512), BF16),
                   jax.ShapeDtypeStruct((t, 512), BF16),
                   acc(LANES), acc(1024), acc(1024), acc(1024), acc(512), acc(512)),
        in_specs=[col(1024, 0), col(1024, 0), col(1024, 0), col(512, 0),
                  col(1024, 3), col(512, COL_BG // 512), col(512, COL_MG // 512), col(1024, 0),
                  full((2048, 1024)),
                  full((1, 1024)), full((1, 512)), full((1, 512)), full((1, 1024)), full((1, 1024))],
        out_specs=(col(2048, 0), col(1024, 0), col(1024, 0), col(1024, 0), col(512, 0),
                   col(1024, 0), col(512, 0), col(512, 0),
                   full((1, LANES)), full((1, 1024)), full((1, 1024)), full((1, 1024)), full((1, 512)),
                   full((1, 512))),
        compiler_params=_params(("arbitrary",)),
    )(h32, ya, ybp, ym, proj, proj, proj, target, w_out, g_a, g_b, g_m, g_post, b_post)


def _prep_bwd(dqa, dka, dva, dqb, dkb, dvb, dqm, dga, dgb, dgm, proj, trig, w_uq, w_ukv, g_cq, g_ckv,
              rope_a, rope_b, tm=512):
    t = proj.shape[0]

    steps = t // tm
    streams = ((0, 1024, BF16),) * 6 + ((0, 512, BF16), (0, 1024, BF16), (0, 512, BF16), (0, 512, BF16),
                                        (COL_CQ, 512, F32), (0, 4 * LANES, F32))
    n_str = len(streams)

    def body(*refs):
        hbm = refs[:n_str]
        wuq_ref, wukv_ref, gcq_ref, gckv_ref, ra_ref, rb_ref = refs[n_str:n_str + 6]
        dproj_ref, dwuq_ref, dwukv_ref, dgcq_ref, dgckv_ref, dqf_ref, dkv_ref = refs[n_str + 6:n_str + 13]
        rings, sems = refs[n_str + 13:2 * n_str + 13], refs[2 * n_str + 13]
        i = pl.program_id(0)

        def fetch(step):
            return [pltpu.make_async_copy(src.at[pl.ds(step * tm, tm), pl.ds(c0, width)], ring.at[step % 3],
                                          sems.at[step % 3, q])
                    for q, (src, (c0, width, _), ring) in enumerate(zip(hbm, streams, rings))]

        @pl.when(i == 0)
        def _():
            for cp in fetch(0) + fetch(1):
                cp.start()

        @pl.when(i + 2 < steps)
        def _():
            for cp in fetch(i + 2):
                cp.start()

        for cp in fetch(i):
            cp.wait()
        (dqa_ref, dka_ref, dva_ref, dqb_ref, dkb_ref, dvb_ref, dqm_ref, dga_ref, dgb_ref, dgm_ref,
         bs_ref, trig_ref) = [ring.at[i % 3] for ring in rings]

        @pl.when(i == 0)
        def _():
            dwuq_ref[...] = jnp.zeros_like(dwuq_ref)
            dwukv_ref[...] = jnp.zeros_like(dwukv_ref)
            dgcq_ref[...] = jnp.zeros_like(dgcq_ref)
            dgckv_ref[...] = jnp.zeros_like(dgckv_ref)

        ta = _rope_tables(trig_ref[:, 0:LANES], trig_ref[:, LANES:2 * LANES], ra_ref[...])
        tb = _rope_tables(trig_ref[:, 2 * LANES:3 * LANES], trig_ref[:, 3 * LANES:4 * LANES], rb_ref[...])
        for j in range(A_WIDTH // LANES):
            sl = slice(j * LANES, (j + 1) * LANES)
            dproj_ref[:, j * LANES:(j + 1) * LANES] = (
                _rope(dqa_ref[:, sl].astype(F32), ta, 8, inverse=True).astype(BF16))
            dproj_ref[:, 1024 + j * LANES:1024 + (j + 1) * LANES] = (
                _rope(dka_ref[:, sl].astype(F32), ta, 8, inverse=True).astype(BF16))
        dproj_ref[:, 2048:3072] = dva_ref[...]
        dproj_ref[:, 3072:4096] = dga_ref[...]

        lane = lax.broadcasted_iota(jnp.int32, (1, LANES), 1)
        low = lane < 64
        rope_lanes = (lane >= 64) & (lane < 96)
        dkr = jnp.zeros((tm, LANES), F32)
        for h in range(MLA_HEADS):
            sl = slice(h * LANES, (h + 1) * LANES)
            dqf_ref[:, sl] = _rope(dqb_ref[:, sl].astype(F32), tb, 16, inverse=True).astype(BF16)
            dk_h = dkb_ref[:, sl]
            dkv_ref[:, sl] = jnp.where(low, dk_h, dvb_ref[:, sl])
            dkr = dkr + jnp.where(rope_lanes, dk_h.astype(F32), 0.0)
        dkr = _rope(dkr, tb, 16, inverse=True)

        cq_hat, r_q = _rms_hat(bs_ref[:, 0:MLA_Q_RANK], MLA_Q_RANK)
        dwuq_ref[...] += _dot_tn(dqf_ref[...], (cq_hat * gcq_ref[...]).astype(BF16))
        dcqn = _dot(dqf_ref[...], wuq_ref[...])
        dgcq_ref[...] += _colsum(dcqn * cq_hat)
        dproj_ref[:, COL_CQ:COL_CQ + 256] = _rms_bwd(dcqn * gcq_ref[...], cq_hat, r_q, MLA_Q_RANK).astype(BF16)
        ckv_hat, r_kv = _rms_hat(bs_ref[:, MLA_Q_RANK:MLA_Q_RANK + MLA_KV_RANK], MLA_KV_RANK)
        dwukv_ref[...] += _dot_tn((ckv_hat * gckv_ref[...]).astype(BF16), dkv_ref[...])
        dckvn = _dot_nt(dkv_ref[...], wukv_ref[...])
        dgckv_ref[...] += _colsum(dckvn * ckv_hat)
        dproj_ref[:, COL_CQ + 256:COL_CQ + 384] = (
            _rms_bwd(dckvn * gckv_ref[...], ckv_hat, r_kv, MLA_KV_RANK).astype(BF16))
        dproj_ref[:, COL_CQ + 384:COL_CQ + 512] = dkr.astype(BF16)
        dproj_ref[:, COL_BG:COL_BG + 512] = dgb_ref[...]
        dproj_ref[:, COL_MQ:COL_MQ + 512] = dqm_ref[...]
        dproj_ref[:, COL_MG:COL_MG + 512] = dgm_ref[...]

    def col(width, idx):
        return pl.BlockSpec((tm, width), lambda i: (i, idx))

    def full(shape):
        return pl.BlockSpec(shape, lambda i: (0, 0))

    return pl.pallas_call(
        body, name="prep_bwd", grid=(steps,),
        out_shape=(jax.ShapeDtypeStruct((t, PROJ_W), BF16), jax.ShapeDtypeStruct((1024, MLA_Q_RANK), F32),
                   jax.ShapeDtypeStruct((MLA_KV_RANK, 1024), F32),
                   jax.ShapeDtypeStruct((1, MLA_Q_RANK), F32), jax.ShapeDtypeStruct((1, MLA_KV_RANK), F32)),
        in_specs=[ANY] * n_str + [full((1024, MLA_Q_RANK)), full((MLA_KV_RANK, 1024)),
                  full((1, MLA_Q_RANK)), full((1, MLA_KV_RANK)), full((8, LANES)), full((8, LANES))],
        out_specs=(col(PROJ_W, 0), full((1024, MLA_Q_RANK)), full((MLA_KV_RANK, 1024)),
                   full((1, MLA_Q_RANK)), full((1, MLA_KV_RANK))),
        scratch_shapes=[pltpu.VMEM((tm, 1024), BF16), pltpu.VMEM((tm, 1024), BF16)]
        + [pltpu.VMEM((3, tm, width), dtype) for _, width, dtype in streams]
        + [pltpu.SemaphoreType.DMA((3, n_str))],
        compiler_params=_params(("arbitrary",)),
    )(dqa, dka, dva, dqb, dkb, dvb, dqm, dga, dgb, dgm, proj, trig, w_uq, w_ukv, g_cq, g_ckv, rope_a, rope_b)


def _adamw_math(gv, w, m, v):
    m_new = ADAM_B1 * m + (1.0 - ADAM_B1) * gv
    v_new = ADAM_B2 * v + (1.0 - ADAM_B2) * (gv * gv)
    m_hat = m_new / (1.0 - ADAM_B1 ** ADAM_STEP)
    v_hat = v_new / (1.0 - ADAM_B2 ** ADAM_STEP)
    return -ADAM_LR * (m_hat / (jnp.sqrt(v_hat) + ADAM_EPS) + ADAM_WD * w), m_new, v_new


def _adamw(g, w, m, v, tr, name):
    r, cols = w.shape

    def body(g_ref, w_ref, m_ref, v_ref, go_ref, d_ref, nm_ref, nv_ref):
        gv = g_ref[...]
        go_ref[...] = gv
        d_ref[...], nm_ref[...], nv_ref[...] = _adamw_math(gv, w_ref[...], m_ref[...], v_ref[...])

    tile = pl.BlockSpec((tr, cols), lambda i: (i, 0))
    shape = jax.ShapeDtypeStruct((r, cols), F32)
    return pl.pallas_call(
        body, name=name, grid=(r // tr,),
        out_shape=(shape,) * 4, in_specs=[tile] * 4, out_specs=(tile,) * 4,
        compiler_params=_params(("parallel",)),
    )(g, w, m, v)


def _adamw_pieces(g, w, m, v, pieces, name):
    n = len(pieces)
    per_piece = isinstance(w, (list, tuple))
    shapes = [jax.ShapeDtypeStruct((r1 - r0, c1 - c0), F32) for r0, r1, c0, c1 in pieces]
    args = (g, *w, *m, *v) if per_piece else (g, w, m, v)

    def body(g_ref, *refs):
        ins, outs = refs[:len(args) - 1], refs[len(args) - 1:]
        gv = g_ref[...]
        if not per_piece:
            results = (gv,) + _adamw_math(gv, ins[0][...], ins[1][...], ins[2][...])
        for p, (r0, r1, c0, c1) in enumerate(pieces):
            if per_piece:
                gp = gv[r0:r1, c0:c1]
                vals = (gp,) + _adamw_math(gp, ins[p][...], ins[n + p][...], ins[2 * n + p][...])
            else:
                vals = [full[r0:r1, c0:c1] for full in results]
            for kind, val in enumerate(vals):
                outs[kind * n + p][...] = val

    flat = pl.pallas_call(
        body, name=name, out_shape=tuple(shapes) * 4,
        in_specs=[IN_VMEM] * len(args), out_specs=tuple([IN_VMEM] * (4 * n)),
        compiler_params=_params(None),
    )(*args)
    return [[flat[kind * n + p] for kind in range(4)] for p in range(n)]


def _core_sum(g, recv, core, rows, tr, name, ride=None):
    cols = g.shape[2]
    nblk = rows // tr
    n_in = len(ride.args) if ride else 0
    n_out = len(ride.out_shapes) if ride else 0

    def body(c_ref, g_ref, r_ref, *rest):
        sf_ref, sb_ref = rest[n_in], rest[n_in + 1]
        if ride:
            j, i = pl.program_id(0), pl.program_id(1)
            ride.run(j * nblk + i, 4 * nblk, rest[:n_in], rest[n_in + 2:n_in + 2 + n_out],
                     rest[n_in + 2 + n_out:])
        tot = g_ref[...] + r_ref[...]
        sf_ref[...] = tot
        sb_ref[...] = tot.astype(BF16)

    half = pl.BlockSpec((None, tr, cols), lambda j, i, c_ref: (j, i, 0))
    shapes = (jax.ShapeDtypeStruct((4, rows, cols), F32), jax.ShapeDtypeStruct((4, rows, cols), BF16))
    return pl.pallas_call(
        body, name=name,
        grid_spec=pltpu.PrefetchScalarGridSpec(
            num_scalar_prefetch=1, grid=(4, nblk),
            in_specs=[pl.BlockSpec((None, tr, cols), lambda j, i, c_ref: (j, c_ref[0] * nblk + i, 0)), half]
            + (ride.in_specs if ride else []),
            out_specs=(half, half) + (ANY,) * n_out,
            scratch_shapes=ride.scratch() if ride else []),
        out_shape=shapes + tuple(ride.out_shapes if ride else ()),
        compiler_params=_params(("arbitrary", "arbitrary") if ride else ("parallel", "parallel")),
    )(core, g, recv, *(ride.args if ride else ()))


def _half_to_sibling(g4):
    def plan(in_refs, out_refs, send_sems, recv_sems):
        x, y, c = _position()
        cp = pltpu.make_async_remote_copy(
            src_ref=in_refs[0].at[:, 1 - c], dst_ref=out_refs[0], send_sem=send_sems.at[0],
            recv_sem=recv_sems.at[0], device_id=(x, y, 1 - c), device_id_type=MESH)

        def finish():
            cp.wait_recv()
            cp.wait_send()

        return cp.start, finish

    return _Ride([g4], [jax.ShapeDtypeStruct((4, g4.shape[2], 1024), F32)], (1, 1), plan)


def _gather_plan(src_ref, dst_ref, send_sems, recv_sems, local_sems):
    x, y, c = _position()
    me = 2 * x + y
    rows = src_ref.shape[1]
    cut = -(-rows // 32) * 16
    pieces = (pl.ds(0, cut), pl.ds(cut, rows - cut))
    local = pltpu.make_async_copy(src_ref, dst_ref.at[me], local_sems.at[0])

    def over_ici(sem, k, chip, t, src=None):
        where = dst_ref.at[chip, c, pieces[t]]
        return pltpu.make_async_remote_copy(
            src_ref=where if src is None else src, dst_ref=where, send_sem=send_sems.at[sem],
            recv_sem=recv_sems.at[sem], device_id=(x ^ (k >> 1), y ^ (k & 1), c), device_id_type=MESH)

    def mine_to(k, t):
        return over_ici(2 * (k - 1) + t, k, me, t, src=src_ref.at[c, pieces[t]])

    def from_neighbour(k, t):
        return over_ici(2 * (k - 1) + t, k, me ^ k, t)

    def to_sibling(k, half):
        piece = dst_ref.at[me ^ k, half]
        return pltpu.make_async_remote_copy(
            src_ref=piece, dst_ref=piece, send_sem=send_sems.at[5 + k], recv_sem=recv_sems.at[5 + k],
            device_id=(x, y, 1 - c), device_id_type=MESH)

    sends = [mine_to(2, 0), mine_to(1, 1), mine_to(2, 1), mine_to(1, 0)]
    onward = [over_ici(4, 1, me ^ 2, 0), over_ici(5, 2, me ^ 1, 1)]

    def start():
        local.start()
        for cp in sends:
            cp.start()

    def pass_on():
        from_neighbour(2, 0).wait_recv()
        onward[0].start()
        from_neighbour(1, 1).wait_recv()
        onward[1].start()

    def to_other_core():
        from_neighbour(2, 1).wait_recv()
        to_sibling(2, c).start()
        from_neighbour(1, 0).wait_recv()
        to_sibling(1, c).start()
        over_ici(4, 1, me ^ 3, 0).wait_recv()
        over_ici(5, 2, me ^ 3, 1).wait_recv()
        to_sibling(3, c).start()

    def finish():
        for k in (1, 2, 3):
            to_sibling(k, 1 - c).wait_recv()
        for cp in sends + onward + [to_sibling(k, c) for k in (1, 2, 3)]:
            cp.wait_send()
        local.wait()

    return start, pass_on, to_other_core, finish


def _gather_ride(shard, spread):
    def plan(in_refs, out_refs, send_sems, recv_sems, local_sems):
        return _gather_plan(in_refs[0], out_refs[0], send_sems, recv_sems, local_sems)

    return _Ride([shard], [jax.ShapeDtypeStruct((4,) + shard.shape, shard.dtype)], (9, 9, 1), plan,
                 in_specs=[IN_VMEM], spread=spread)


def _chip_sum(sf, recv, chip, rows, tr, name):
    cols = sf.shape[2]
    n_recv = recv.shape[0]

    def body(me_ref, sf_ref, r_ref, out_ref):
        acc = sf_ref[...]
        for k in range(n_recv):
            acc = acc + r_ref[k].astype(F32)
        out_ref[...] = acc

    return pl.pallas_call(
        body, name=name,
        grid_spec=pltpu.PrefetchScalarGridSpec(
            num_scalar_prefetch=1, grid=(rows // tr,),
            in_specs=[pl.BlockSpec((None, tr, cols), lambda i, me_ref: (me_ref[0], i, 0)),
                      pl.BlockSpec((n_recv, tr, cols), lambda i, me_ref: (0, i, 0))],
            out_specs=pl.BlockSpec((tr, cols), lambda i, me_ref: (i, 0))),
        out_shape=jax.ShapeDtypeStruct((rows, cols), F32),
        compiler_params=_params(("parallel",)),
    )(chip, sf, recv)


def _position():
    return lax.axis_index("x"), lax.axis_index("y"), lax.axis_index("c")


def _dh_scatter(dproj, w_in_arr_t, x, dz, g, sb_in, sb_rest, tm=512, tk=3072):
    t, d = x.shape
    nk = dproj.shape[1] // tk
    ni = t // tm
    total = ni * nk
    halves = (HALF_IN, HALF_REST)
    cuts = tuple(-(-rows // 32) * 16 for rows in halves)

    def rows_of(a, p):
        return cuts[a] if p == 0 else halves[a] - cuts[a]

    def piece(a, p):
        return pl.ds(0, cuts[a]) if p == 0 else pl.ds(cuts[a], halves[a] - cuts[a])

    def body(dp_ref, w_ref, x_ref, dz_ref, g_ref, sbin_ref, sbrest_ref, dx_ref, dg_ref, db_ref, rin_ref, rrest_ref,
             acc_ref, pay_in0, pay_in1, pay_rest0, pay_rest1, own_in0, own_in1, own_rest0, own_rest1,
             send_sems, recv_sems, local_sems):
        step = pl.program_id(0) * nk + pl.program_id(1)
        kk = pl.program_id(1)
        px, py, pc = _position()
        me = 2 * px + py
        srcs = (sbin_ref, sbrest_ref)
        dsts = (rin_ref, rrest_ref)
        pays = ((pay_in0, pay_in1), (pay_rest0, pay_rest1))
        owns = ((own_in0, own_in1), (own_rest0, own_rest1))
        via = (2, 1)
        onto = (1, 2)

        def peer(k):
            return (px ^ (k >> 1), py ^ (k & 1), pc)

        def payload(a, p):
            return pltpu.make_async_remote_copy(
                src_ref=srcs[a].at[me ^ 3, piece(a, p)], dst_ref=pays[a][p], send_sem=send_sems.at[2 * a + p],
                recv_sem=recv_sems.at[2 * a + p], device_id=peer(via[p]), device_id_type=MESH)

        def direct(a, k, p, src):
            sem = 4 + 4 * a + 2 * (k - 1) + p
            return pltpu.make_async_remote_copy(
                src_ref=src, dst_ref=dsts[a].at[k - 1, piece(a, p)], send_sem=send_sems.at[sem],
                recv_sem=recv_sems.at[sem], device_id=peer(k), device_id_type=MESH)

        def plain(a, k, p):
            return direct(a, k, p, srcs[a].at[me ^ k, piece(a, p)])

        def stage(a, p):
            return pltpu.make_async_copy(srcs[a].at[me ^ onto[p], piece(a, p)], owns[a][p], local_sems.at[2 * a + p])

        @pl.when(step == 0)
        def _():
            dg_ref[...] = jnp.zeros_like(dg_ref)
            db_ref[...] = jnp.zeros_like(db_ref)
            for a in range(2):
                for p in range(2):
                    payload(a, p).start()
                    stage(a, p).start()
                plain(a, 1, 1).start()
                plain(a, 2, 0).start()

        @pl.when(step == (5 * total) // 8)
        def _():
            for a in range(2):
                for p in range(2):
                    payload(a, p).wait_recv()
                    stage(a, p).wait()
                    owns[a][p][...] = (owns[a][p][...].astype(F32) + pays[a][p][...].astype(F32)).astype(BF16)
                    direct(a, onto[p], p, owns[a][p]).start()

        part = _dot(dp_ref[...], w_ref[...])

        @pl.when(kk == 0)
        def _():
            acc_ref[...] = part

        @pl.when(kk > 0)
        def _():
            acc_ref[...] += part

        @pl.when(kk == nk - 1)
        def _():
            xh, rstd = _ln_hat(x_ref[...])
            dht = acc_ref[...] + DEEPNORM_ALPHA * dz_ref[...]
            dg_ref[...] += _colsum(dht * xh)
            db_ref[...] += _colsum(dht)
            dx_ref[...] = _ln_bwd_rows(dht * g_ref[...], xh, rstd)

        @pl.when(step == total - 1)
        def _():
            for a in range(2):
                for k in (1, 2):
                    for p in range(2):
                        plain(a, k, p).wait_recv()
            for a in range(2):
                for p in range(2):
                    payload(a, p).wait_send()
                    direct(a, onto[p], p, owns[a][p]).wait_send()
                plain(a, 1, 1).wait_send()
                plain(a, 2, 0).wait_send()

    tile = pl.BlockSpec((tm, d), lambda i, kk: (i, 0))
    row = pl.BlockSpec((1, d), lambda i, kk: (0, 0))
    pieces = [pltpu.VMEM((rows_of(a, p), 1024), BF16) for a in range(2) for p in range(2)]
    return pl.pallas_call(
        body, name="dh_scatter", grid=(ni, nk),
        out_shape=(jax.ShapeDtypeStruct((t, d), F32), jax.ShapeDtypeStruct((1, d), F32),
                   jax.ShapeDtypeStruct((1, d), F32),
                   jax.ShapeDtypeStruct((2, HALF_IN, 1024), BF16),
                   jax.ShapeDtypeStruct((2, HALF_REST, 1024), BF16)),
        in_specs=[pl.BlockSpec((tm, tk), lambda i, kk: (i, kk)), pl.BlockSpec((tk, d), lambda i, kk: (kk, 0)),
                  tile, tile, row, ANY, ANY],
        out_specs=(tile, row, row, ANY, ANY),
        scratch_shapes=[pltpu.VMEM((tm, d), F32)] + pieces + pieces
        + [pltpu.SemaphoreType.DMA((12,)), pltpu.SemaphoreType.DMA((12,)), pltpu.SemaphoreType.DMA((4,))],
        compiler_params=_params(("arbitrary", "arbitrary")),
    )(dproj, w_in_arr_t, x, dz, g, sb_in, sb_rest)


def _join_and_allreduce(gh_in, gh_rest, vec):
    def body(hin_ref, hrest_ref, vec_ref, oin_ref, orest_ref, sum_ref, all_ref, send_sems, recv_sems, local_sems):
        x, y, c = _position()
        srcs = (hin_ref, hrest_ref)
        dsts = (oin_ref, orest_ref)
        me = 4 * x + 2 * y + c
        all_ref[me] = vec_ref[...]

        def small(k, slot):
            return pltpu.make_async_remote_copy(
                src_ref=vec_ref, dst_ref=all_ref.at[slot], send_sem=send_sems.at[k + 1], recv_sem=recv_sems.at[k + 1],
                device_id=(x ^ (k >> 2), y ^ ((k >> 1) & 1), c ^ (k & 1)), device_id_type=MESH)

        def half(a, slot):
            return pltpu.make_async_remote_copy(
                src_ref=srcs[a], dst_ref=dsts[a].at[slot], send_sem=send_sems.at[a], recv_sem=recv_sems.at[a],
                device_id=(x, y, 1 - c), device_id_type=MESH)

        local = [pltpu.make_async_copy(srcs[a], dsts[a].at[c], local_sems.at[a]) for a in range(2)]
        remote = [half(a, c) for a in range(2)] + [small(k, me) for k in range(1, 8)]
        for cp in local + remote:
            cp.start()
        for k in range(1, 8):
            small(k, me ^ k).wait_recv()
        for a in range(2):
            half(a, 1 - c).wait_recv()
        for cp in remote:
            cp.wait_send()
        for cp in local:
            cp.wait()
        total = all_ref[0]
        for d in range(1, 8):
            total = total + all_ref[d]
        sum_ref[...] = total

    return pl.pallas_call(
        body, name="join_halves",
        out_shape=(jax.ShapeDtypeStruct((2, HALF_IN, 1024), F32),
                   jax.ShapeDtypeStruct((2, HALF_REST, 1024), F32),
                   jax.ShapeDtypeStruct(vec.shape, vec.dtype)),
        in_specs=[IN_VMEM, IN_VMEM, IN_VMEM], out_specs=(ANY, ANY, IN_VMEM),
        scratch_shapes=[pltpu.VMEM((8,) + vec.shape, vec.dtype), pltpu.SemaphoreType.DMA((9,)),
                        pltpu.SemaphoreType.DMA((9,)), pltpu.SemaphoreType.DMA((2,))],
    )(gh_in, gh_rest, vec)


def _pack_rest(w_uq, w_ukv, w_mem, w_out):
    rows = jnp.concatenate([w_uq[0].T.reshape(-1, 1024), w_ukv.reshape(-1, 1024), w_mem.reshape(-1, 1024),
                            w_out.reshape(-1, 1024)], axis=0)
    return jnp.pad(rows, ((0, ROWS_REST - ROWS_USED), (0, 0)))


def _arranged_w_in(g_in):
    z = functools.partial(jnp.zeros, dtype=g_in.dtype)
    cut = 4480 - 2 * SHARD_ROWS
    return jnp.concatenate(
        [g_in[0, :SHARD_ROWS], g_in[1, :SHARD_ROWS], g_in[2, :cut], z((64, 1024)), g_in[2, cut:cut + 32],
         z((32, 1024)), g_in[2, cut + 32:SHARD_ROWS], g_in[3, :SHARD_ROWS]], axis=0)


def _rest_weights(g_rest):
    w_uq_t = g_rest[:, 0:ROWS_UQ].reshape(768, 256)
    w_uq_pad_t = jnp.pad(w_uq_t.reshape(MLA_HEADS, MLA_QK_DIM, 256), ((0, 0), (0, 32), (0, 0))).reshape(1024, 256)
    w_ukv = jnp.concatenate([g_rest[j, ROWS_UQ:ROWS_UQ + ROWS_UKV].reshape(128, 256) for j in range(4)], axis=1)
    lo = ROWS_UQ + ROWS_UKV
    w_mem = g_rest[:, lo:lo + ROWS_MEM].reshape(4 * ROWS_MEM, 1024)
    w_out = g_rest[:, lo + ROWS_MEM:lo + ROWS_MEM + ROWS_OUT].reshape(4 * ROWS_OUT, 1024)
    return w_uq_pad_t, w_ukv, w_mem, w_out


def _dw_in_split(dproj, h, tm=1024):
    t = dproj.shape[0]
    steps = PROJ_W // tm
    gap = ROWS_IN - SHARD_ROWS
    nat = 4608 - 96
    last = 4608 + 3 * SHARD_ROWS - nat
    segments = ((0, SHARD_ROWS, 0, 0), (SHARD_ROWS, 2 * SHARD_ROWS, 1, 0), (2 * SHARD_ROWS, 4480, 2, 0),
                (4544, 4576, 2, 4480 - 2 * SHARD_ROWS), (4608, last, 2, 4512 - 2 * SHARD_ROWS), (last, PROJ_W, 3, 0))

    def pieces(j):
        out = []
        for lo, hi, chip, dst in segments:
            a, b = max(lo, j * tm), min(hi, (j + 1) * tm)
            if a < b:
                out.append((a - j * tm, chip, dst + a - lo, b - a))
        return out

    n_sem = max(len(pieces(j)) for j in range(steps))

    def body(a_ref, b_ref, o_ref, tile_ref, zero_ref, sems, pad_sems):
        i = pl.program_id(0)

        def copies(j):
            return [pltpu.make_async_copy(tile_ref.at[j % 2, pl.ds(off, n)], o_ref.at[chip, pl.ds(dst, n)],
                                          sems.at[j % 2, q])
                    for q, (off, chip, dst, n) in enumerate(pieces(j))]

        def pad_copies():
            return [pltpu.make_async_copy(zero_ref, o_ref.at[chip, pl.ds(SHARD_ROWS, gap)], pad_sems.at[chip])
                    for chip in range(4)]

        @pl.when(i == 0)
        def _():
            zero_ref[...] = jnp.zeros_like(zero_ref)
            for c in pad_copies():
                c.start()

        for j in range(2, steps):
            @pl.when(i == j)
            def _(j=j):
                for c in copies(j - 2):
                    c.wait()

        tile_ref[i % 2] = _dot_tn(a_ref[...], b_ref[...])

        for j in range(steps):
            @pl.when(i == j)
            def _(j=j):
                for c in copies(j):
                    c.start()
                if j == steps - 1:
                    for c in copies(j - 1) + copies(j) + pad_copies():
                        c.wait()

    return pl.pallas_call(
        body, name="dw_in", grid=(steps,),
        out_shape=jax.ShapeDtypeStruct((4, ROWS_IN, 1024), F32),
        in_specs=[pl.BlockSpec((t, tm), lambda i: (0, i)), pl.BlockSpec((t, 1024), lambda i: (0, 0))],
        out_specs=ANY,
        scratch_shapes=[pltpu.VMEM((2, tm, 1024), F32), pltpu.VMEM((gap, 1024), F32),
                        pltpu.SemaphoreType.DMA((2, n_sem)), pltpu.SemaphoreType.DMA((4,))],
        compiler_params=_params(("arbitrary",)),
    )(dproj, h)


def _split_rest(dw_uq_pad_t, dw_ukv, dw_mem, dw_out):
    dw_uq_t = dw_uq_pad_t.reshape(MLA_HEADS, LANES, 256)[:, :MLA_QK_DIM].reshape(4, ROWS_UQ, 1024)
    parts = [dw_uq_t, dw_ukv.reshape(128, 4, 256).transpose(1, 0, 2).reshape(4, ROWS_UKV, 1024),
             dw_mem.reshape(4, ROWS_MEM, 1024), dw_out.reshape(4, ROWS_OUT, 1024)]
    return jnp.pad(jnp.concatenate(parts, axis=1), ((0, 0), (0, ROWS_REST - ROWS_USED), (0, 0)))


def _rope_consts(rot, first, period):
    half = rot // 2
    inv_freq = np.float32(ROPE_THETA) ** (-(np.arange(0, rot, 2, dtype=np.float32) / np.float32(rot)))
    lane = np.arange(LANES) % period - first
    in_rot = (lane >= 0) & (lane < rot)
    out = np.zeros((8, LANES), np.float32)
    out[0] = np.where(in_rot, inv_freq[np.clip(lane, 0, rot - 1) % half], 0.0)
    out[1] = in_rot & (lane < half)
    out[2] = in_rot & (lane >= half)
    return jnp.asarray(out)


def _band_bias(s):
    nblk = s // BAND_Q
    starts = np.array([_band_start(i, s) for i in range(nblk)])
    uq = (np.arange(nblk)[:, None] * BAND_Q + np.arange(BAND_Q)[None, :])[:, :, None]
    uk = (starts[:, None] + np.arange(BAND_WIN)[None, :])[:, None, :]
    tiles, index, seen = [], [], {}
    for _, d in DILATED:
        length = s // d
        ok = (uq // length == uk // length) & (np.abs(uq - uk) <= 64)
        row = []
        for i in range(nblk):
            key = ok[i].tobytes()
            if key not in seen:
                seen[key] = len(tiles)
                tiles.append(np.where(ok[i], 0.0, NEG_INF).astype(np.float32))
            row.append(seen[key])
        index.append(row)
    return jnp.asarray(np.stack(tiles, axis=0)), index


def _forward_backward(h, h32, proj, trig, rope_consts, x, mem, target, weights, gains):
    w_uq_pad_t, w_ukv, w_mem, w_out = weights
    g_emb, b_emb, g_cq, g_ckv, g_out_a, g_out_b, g_out_m, g_post, b_post = gains
    nb, s, d = x.shape
    t = nb * s
    x2 = x.reshape(t, d)
    mem2 = mem.reshape(nb * N_MEM, d)
    tgt2 = target.reshape(t, d)
    rope_a, rope_b = rope_consts
    bias, bias_index = _band_bias(s)
    scales = (0.125, MLA_QK_DIM ** -0.5, 128 ** -0.5)

    qa, ka, va, qb, kb, vb, qm = _prep(proj, trig, w_uq_pad_t, w_ukv, g_cq, g_ckv, rope_a, rope_b, scales)
    mkv = _mm(mem2, w_mem, BF16, nb * N_MEM, 1024, 1024, "mem_kv")

    cfg_b = dict(nb=nb, s=s, sk=s, heads=8, voff=0, bq=256)
    cfg_m = dict(nb=nb, s=s, sk=N_MEM, heads=4, hpb=2, voff=4, bq=1024)
    ya, lse_a, qkv_ordered = _dilated_fwd(qa, ka, va, bias, bias_index, nb=nb, s=s, name="attn_a_fwd")
    yb, lse_b = _attn_fwd(qb, kb, vb, name="attn_b_fwd", hpb=4, **cfg_b)
    ym, lse_m = _attn_fwd(qm, mkv, mkv, name="attn_m_fwd", **cfg_m)

    (y, dz, doa, dob, dom, dga, dgb, dgm, loss, dg_post, db_post, dg_a, dg_b, dg_m) = _post(
        h32, ya, yb, ym, proj, tgt2, w_out, g_out_a, g_out_b, g_out_m, g_post, b_post)

    dqa, dka, dva = _dilated_bwd(qa, ka, va, qkv_ordered, ya, doa, lse_a, bias, bias_index, nb=nb, s=s, scale=scales[0],
                                 name="attn_a_bwd")
    dqb, dkb, dvb = _attn_bwd(qb, kb, vb, yb, dob, lse_b, name="attn_b_bwd", scale=scales[1], hpb=4, **cfg_b)
    dqm, dmk, dmv = _attn_bwd(qm, mkv, mkv, ym, dom, lse_m, name="attn_m_bwd", scale=scales[2], **cfg_m)
    dmkv = jnp.concatenate([dmk, dmv], axis=1)

    dproj, dw_uq_pad_t, dw_ukv, dg_cq, dg_ckv = _prep_bwd(
        dqa, dka, dva, dqb, dkb, dvb, dqm, dga, dgb, dgm, proj, trig, w_uq_pad_t, w_ukv, g_cq, g_ckv, rope_a, rope_b)

    small_rows = (dg_cq, dg_ckv, loss, dg_a, dg_b, dg_m, dg_post, db_post)
    return (dproj, h, y, dz, dw_uq_pad_t, dw_ukv, mem2, dmkv), x2, small_rows


def _weight_grads(operands, core):
    dproj, h, y, dz, dw_uq_pad_t, dw_ukv, mem2, dmkv = operands
    g_in = _dw_in_split(dproj, h)
    dw_out, r_in = _mm(y, dz, F32, 1024, 1024, 2048, "dw_out", mode="tn",
                       ride=_half_to_sibling(g_in.reshape(4, 2, HALF_IN, 1024)))
    dw_mem = _mm(mem2, dmkv, F32, 1024, 1024, mem2.shape[0], "dw_mem", mode="tn")
    g_rest = _split_rest(dw_uq_pad_t, dw_ukv, dw_mem, dw_out)
    sf_in, sb_in, r_rest = _core_sum(g_in, r_in, core, HALF_IN, HALF_IN // 2, "core_sum_in",
                                     ride=_half_to_sibling(g_rest.reshape(4, 2, HALF_REST, 1024)))
    sf_rest, sb_rest = _core_sum(g_rest, r_rest, core, HALF_REST, HALF_REST, "core_sum_rest")
    return sf_in, sb_in, sf_rest, sb_rest


def _small_block(dg_emb, db_emb, small_rows):
    dg_cq, dg_ckv, loss, dg_a, dg_b, dg_m, dg_post, db_post = small_rows
    row2 = jnp.concatenate([dg_cq, dg_ckv, loss, jnp.zeros((1, 512), F32)], axis=1)
    return jnp.concatenate([dg_emb, db_emb, row2, dg_a, jnp.concatenate([dg_b, dg_m], axis=1), dg_post, db_post,
                            jnp.zeros((1, 1024), F32)], axis=0)


def _pack_small(g_emb, b_emb, g_cq, g_ckv, g_out_a, g_out_b, g_out_m, g_post, b_post):
    row2 = jnp.concatenate([g_cq.reshape(1, -1), g_ckv.reshape(1, -1), jnp.zeros((1, 640), F32)], axis=1)
    return jnp.concatenate([g_emb.reshape(1, -1), b_emb.reshape(1, -1), row2, g_out_a.reshape(1, -1),
                            jnp.concatenate([g_out_b.reshape(1, -1), g_out_m.reshape(1, -1)], axis=1),
                            g_post.reshape(1, -1), b_post.reshape(1, -1), jnp.zeros((1, 1024), F32)], axis=0)


def kernel(x, mem, positions, g_emb, b_emb, w_in, g_cq, g_ckv, w_uq, w_ukv, w_mem_kv, g_out_a, g_out_b, g_out_m, w_out, g_post, b_post, loss_target, m_g_emb, m_b_emb, m_w_in, m_g_cq, m_g_ckv, m_w_uq, m_w_ukv, m_w_mem_kv, m_g_out_a, m_g_out_b, m_g_out_m, m_w_out, m_g_post, m_b_post, v_g_emb, v_b_emb, v_w_in, v_g_cq, v_g_ckv, v_w_uq, v_w_ukv, v_w_mem_kv, v_g_out_a, v_g_out_b, v_g_out_m, v_w_out, v_g_post, v_b_post):
    w_rest = _pack_rest(w_uq, w_ukv, w_mem_kv, w_out)
    w_in_t = w_in[0].T
    w_in_b = jnp.pad(w_in_t.astype(BF16), ((0, ROWS_IN - SHARD_ROWS), (0, 0)))
    gains = (g_emb.reshape(1, -1), b_emb.reshape(1, -1), g_cq, g_ckv, g_out_a, g_out_b, g_out_m, g_post, b_post)
    rope_consts = (_rope_consts(16, 0, 64), _rope_consts(32, 64, 128))
    h, h32, trig, gathered_in = _ln_fwd(x.reshape(-1, D_MODEL), gains[0], gains[1],
                                        positions.reshape(-1, 1).astype(F32), *rope_consts,
                                        ride=_gather_ride(w_in_b.reshape(2, HALF_IN, 1024), spread=False))
    w_in_arr_t = _arranged_w_in(gathered_in.reshape(4, ROWS_IN, 1024))
    proj, gathered_rest = _mm(h, w_in_arr_t, F32, 1024, 2048, 1024, "in_proj", mode="nt",
                              ride=_gather_ride(w_rest.astype(BF16).reshape(2, HALF_REST, 1024), spread=True))
    weights = _rest_weights(gathered_rest.reshape(4, ROWS_REST, 1024))
    operands, x2, small_rows = _forward_backward(h, h32, proj, trig, rope_consts, x, mem, loss_target, weights,
                                                 gains)

    core = lax.axis_index("c").astype(jnp.int32).reshape(1)
    chip = (2 * lax.axis_index("x") + lax.axis_index("y")).astype(jnp.int32).reshape(1)
    sf_in, sb_in, sf_rest, sb_rest = _weight_grads(operands, core)
    grad_x, dg_emb, db_emb, rb_in, rb_rest = _dh_scatter(operands[0], w_in_arr_t, x2, operands[3], gains[0],
                                                         sb_in, sb_rest)
    gh_in = _chip_sum(sf_in, rb_in, chip, HALF_IN, HALF_IN // 2, "chip_sum_in")
    gh_rest = _chip_sum(sf_rest, rb_rest, chip, HALF_REST, HALF_REST, "chip_sum_rest")
    grad_in, grad_rest, small_sum = _join_and_allreduce(gh_in, gh_rest, _small_block(dg_emb, db_emb, small_rows))
    grad_in = grad_in.reshape(ROWS_IN, 1024)
    grad_rest = grad_rest.reshape(ROWS_REST, 1024)

    big_in = _adamw(grad_in, w_in_t, m_w_in[0].T, v_w_in[0].T, SHARD_ROWS // 3, "adamw_in")
    def rest_parts(a_uq, a_ukv, a_mem, a_out):
        return [a_uq[0].T.reshape(ROWS_UQ, 1024), a_ukv.reshape(ROWS_UKV, 1024), a_mem[0], a_out[0]]

    uq, ukv, wmem, wout = _adamw_pieces(
        grad_rest, rest_parts(w_uq, w_ukv, w_mem_kv, w_out), rest_parts(m_w_uq, m_w_ukv, m_w_mem_kv, m_w_out),
        rest_parts(v_w_uq, v_w_ukv, v_w_mem_kv, v_w_out), REST_PIECES, "adamw_rest")
    sm = _adamw_pieces(
        small_sum,
        _pack_small(g_emb, b_emb, g_cq, g_ckv, g_out_a, g_out_b, g_out_m, g_post, b_post),
        _pack_small(m_g_emb, m_b_emb, m_g_cq, m_g_ckv, m_g_out_a, m_g_out_b, m_g_out_m, m_g_post, m_b_post),
        _pack_small(v_g_emb, v_b_emb, v_g_cq, v_g_ckv, v_g_out_a, v_g_out_b, v_g_out_m, v_g_post, v_b_post),
        SMALL_PIECES, "adamw_small")
    loss = small_sum[2, 384]

    def ordered(kind):
        s_gemb, s_bemb, s_gcq, s_gckv, s_ga, s_gb, s_gm, s_gpost, s_bpost = [piece[kind] for piece in sm]
        return [s_gemb.reshape(-1), s_bemb.reshape(-1), big_in[kind].T[None], s_gcq, s_gckv,
                uq[kind].reshape(192, 256).T[None], ukv[kind].reshape(1, 128, 256), wmem[kind][None], s_ga, s_gb,
                s_gm, wout[kind][None], s_gpost, s_bpost]

    return (loss, grad_x.reshape(x.shape), *ordered(0), *ordered(1), *ordered(2), *ordered(3))
```
